```python
import jax, jax.numpy as jnp
from jax import lax
import numpy as np

D_MODEL = 1024
BATCH = 16
SEQ = 2048
DEPTH = 2

CHUNK = 64
Q_BLOCK = 128
EPS = 1e-6
NEG = -1e30

A_HEADS = 6
A_NOPE = 64
A_ROPE = 32
A_V = 64
A_Q_RANK = 384
A_KV_RANK = 256
A_WIDTH = A_HEADS * A_V
ROPE_THETA = 10000.0

B_HEADS = 5
B_HD = 64
B_WIDTH = B_HEADS * B_HD
B_LEFT_CHUNKS = 8
B_BAND = (B_LEFT_CHUNKS + 1) * CHUNK
REL_CLIP = 128

C_HEADS = 5
C_HD = 64
C_WIDTH = C_HEADS * C_HD
FORGET_BIAS_INIT = 2.0

D_MIX = A_WIDTH + B_WIDTH + C_WIDTH

IN_SIZES = (A_Q_RANK, A_KV_RANK, A_ROPE, A_WIDTH,
            B_WIDTH, B_WIDTH, B_WIDTH, B_WIDTH,
            C_WIDTH, C_WIDTH, C_WIDTH, C_HEADS, C_WIDTH)
N_IN = (A_Q_RANK + A_KV_RANK + A_ROPE + A_WIDTH + 4 * B_WIDTH + 4 * C_WIDTH + C_HEADS)

kernel_name = "hybrid_mla_chunkband_fox_encoder"


def rmsnorm(x, g):
    xf = x.astype(jnp.float32)
    y = xf * lax.rsqrt(jnp.mean(xf * xf, axis=-1, keepdims=True) + EPS)
    return (y * g.astype(jnp.float32)).astype(x.dtype)


def split_cols(z, sizes):
    idx, o = [], 0
    for s in sizes[:-1]:
        o += s
        idx.append(o)
    return jnp.split(z, idx, axis=-1)


def rope_tables(positions):
    inv = ROPE_THETA ** (-jnp.arange(0, A_ROPE, 2, dtype=jnp.float32) / A_ROPE)
    ang = positions.astype(jnp.float32)[..., None] * inv
    return jnp.cos(ang), jnp.sin(ang)


def apply_rope(x, cos, sin):
    x1, x2 = jnp.split(x.astype(jnp.float32), 2, axis=-1)
    out = jnp.concatenate([x1 * cos - x2 * sin, x1 * sin + x2 * cos], axis=-1)
    return out.astype(x.dtype)


def attend(s, mask, v):
    p = jax.nn.softmax(jnp.where(mask, s, NEG), axis=-1)
    return jnp.einsum('bhqk,bhkd->bhqd', p.astype(v.dtype), v)


def mla_mixer(c_q, c_kv, k_pe, q_norm_g, w_uq, kv_norm_g, w_ukv, cos, sin):
    Bn, S, _ = c_q.shape
    q = (rmsnorm(c_q, q_norm_g) @ w_uq).reshape(Bn, S, A_HEADS, A_NOPE + A_ROPE)
    q_nope, q_pe = q[..., :A_NOPE], q[..., A_NOPE:]
    q_pe = apply_rope(q_pe, cos[:, :, None, :], sin[:, :, None, :])
    kv = (rmsnorm(c_kv, kv_norm_g) @ w_ukv).reshape(Bn, S, A_HEADS, A_NOPE + A_V)
    k_nope, v = kv[..., :A_NOPE], kv[..., A_NOPE:]
    k_pe = apply_rope(k_pe, cos, sin)
    k_pe = jnp.broadcast_to(k_pe[:, :, None, :], (Bn, S, A_HEADS, A_ROPE))
    q = jnp.concatenate([q_nope, q_pe], axis=-1).transpose(0, 2, 1, 3)
    k = jnp.concatenate([k_nope, k_pe], axis=-1).transpose(0, 2, 1, 3)
    v = v.transpose(0, 2, 1, 3)
    scale = (A_NOPE + A_ROPE) ** -0.5
    outs = []
    for i in range(S // Q_BLOCK):
        q0 = i * Q_BLOCK
        kend = q0 + Q_BLOCK
        s = jnp.einsum('bhqd,bhkd->bhqk', q[:, :, q0:kend], k[:, :, :kend],
                       preferred_element_type=jnp.float32) * scale
        q_chunk = (q0 + jnp.arange(Q_BLOCK)) // CHUNK
        k_chunk = jnp.arange(kend) // CHUNK
        mask = k_chunk[None, :] <= q_chunk[:, None]
        outs.append(attend(s, mask, v[:, :, :kend]))
    o = jnp.concatenate(outs, axis=2)
    return o.transpose(0, 2, 1, 3).reshape(Bn, S, A_WIDTH)


def chunk_band_mixer(q, k, v, rel_bias):
    Bn, S, _ = q.shape
    NC = S // CHUNK
    qc = q.reshape(Bn, NC, CHUNK, B_HEADS, B_HD)

    def band(t):
        t = t.reshape(Bn, S, B_HEADS, B_HD)
        t = jnp.pad(t, ((0, 0), (B_LEFT_CHUNKS * CHUNK, 0), (0, 0), (0, 0)))
        t = t.reshape(Bn, NC + B_LEFT_CHUNKS, CHUNK, B_HEADS, B_HD)
        t = jnp.stack([t[:, j:j + NC] for j in range(B_LEFT_CHUNKS + 1)], axis=2)
        return t.reshape(Bn, NC, B_BAND, B_HEADS, B_HD)

    kb, vb = band(k), band(v)
    s = jnp.einsum('bnqhd,bnkhd->bnhqk', qc, kb,
                   preferred_element_type=jnp.float32) * (B_HD ** -0.5)
    rel = (B_LEFT_CHUNKS * CHUNK + jnp.arange(CHUNK))[:, None] - jnp.arange(B_BAND)[None, :]
    idx = jnp.clip(rel, -REL_CLIP, REL_CLIP) + REL_CLIP
    s = s + rel_bias[:, idx].astype(jnp.float32)
    k_chunk = jnp.arange(NC)[:, None] - B_LEFT_CHUNKS + (jnp.arange(B_BAND) // CHUNK)[None, :]
    mask = (k_chunk >= 0)[None, :, None, None, :]
    p = jax.nn.softmax(jnp.where(mask, s, NEG), axis=-1)
    o = jnp.einsum('bnhqk,bnkhd->bnqhd', p.astype(vb.dtype), vb)
    return o.reshape(Bn, S, B_WIDTH)


def forgetting_mixer(q, k, v, f_logit, f_bias):
    Bn, S, _ = q.shape
    q = q.reshape(Bn, S, C_HEADS, C_HD).transpose(0, 2, 1, 3)
    k = k.reshape(Bn, S, C_HEADS, C_HD).transpose(0, 2, 1, 3)
    v = v.reshape(Bn, S, C_HEADS, C_HD).transpose(0, 2, 1, 3)
    log_f = jax.nn.log_sigmoid(f_logit.astype(jnp.float32) + f_bias.astype(jnp.float32))
    F = jnp.cumsum(log_f, axis=1).transpose(0, 2, 1)
    scale = C_HD ** -0.5
    outs = []
    for i in range(S // Q_BLOCK):
        q0 = i * Q_BLOCK
        kend = q0 + Q_BLOCK
        s = jnp.einsum('bhqd,bhkd->bhqk', q[:, :, q0:kend], k[:, :, :kend],
                       preferred_element_type=jnp.float32) * scale
        s = s + F[:, :, q0:kend, None] - F[:, :, None, :kend]
        mask = jnp.arange(kend)[None, :] <= (q0 + jnp.arange(Q_BLOCK))[:, None]
        outs.append(attend(s, mask, v[:, :, :kend]))
    o = jnp.concatenate(outs, axis=2)
    return o.transpose(0, 2, 1, 3).reshape(Bn, S, C_WIDTH)


def hybrid_layer(x, c_act, cos, sin, w_ada, b_ada, norm_g, w_in, a_q_norm_g, a_w_uq,
                 a_kv_norm_g, a_w_ukv, b_rel_bias, c_forget_b, w_out):
    mod = c_act @ w_ada + b_ada
    shift, scale, gate = jnp.split(mod, 3, axis=-1)
    h = rmsnorm(x, norm_g) * (1.0 + scale[:, None, :]) + shift[:, None, :]
    z = h @ w_in
    (a_cq, a_ckv, a_kpe, a_gate,
     b_q, b_k, b_v, b_gate,
     c_q, c_k, c_v, c_f, c_gate) = split_cols(z, IN_SIZES)
    a_out = mla_mixer(a_cq, a_ckv, a_kpe, a_q_norm_g, a_w_uq, a_kv_norm_g, a_w_ukv, cos, sin)
    b_out = chunk_band_mixer(b_q, b_k, b_v, b_rel_bias)
    c_out = forgetting_mixer(c_q, c_k, c_v, c_f, c_forget_b)
    y = jnp.concatenate([a_out * jax.nn.silu(a_gate),
                         b_out * jax.nn.silu(b_gate),
                         c_out * jax.nn.silu(c_gate)], axis=-1) @ w_out
    return x + gate[:, None, :] * y


def _fwd_setup_inputs(seed: int = 0) -> dict:
    key = jax.random.key(seed)
    ks = jax.random.split(key, 16)
    f32 = jnp.float32
    x = jax.random.normal(ks[0], (BATCH, SEQ, D_MODEL), f32)
    c = jax.random.normal(ks[1], (BATCH, D_MODEL), f32)
    start = jax.random.randint(ks[2], (BATCH, 1), 0, 64, dtype=jnp.int32) * CHUNK
    positions = (start + jnp.arange(SEQ, dtype=jnp.int32)[None, :]).astype(jnp.int32)
    w_ada = jax.random.normal(ks[3], (DEPTH, D_MODEL, 3 * D_MODEL), f32) * D_MODEL ** -0.5
    b_ada = jax.random.normal(ks[4], (DEPTH, 3 * D_MODEL), f32) * 0.02
    norm_g = 1.0 + 0.02 * jax.random.normal(ks[5], (DEPTH, D_MODEL), f32)
    w_in = jax.random.normal(ks[6], (DEPTH, D_MODEL, N_IN), f32) * D_MODEL ** -0.5
    a_q_norm_g = 1.0 + 0.02 * jax.random.normal(ks[7], (DEPTH, A_Q_RANK), f32)
    a_w_uq = jax.random.normal(ks[8], (DEPTH, A_Q_RANK, A_HEADS * (A_NOPE + A_ROPE)), f32) * A_Q_RANK ** -0.5
    a_kv_norm_g = 1.0 + 0.02 * jax.random.normal(ks[9], (DEPTH, A_KV_RANK), f32)
    a_w_ukv = jax.random.normal(ks[10], (DEPTH, A_KV_RANK, A_HEADS * (A_NOPE + A_V)), f32) * A_KV_RANK ** -0.5
    b_rel_bias = 0.2 * jax.random.normal(ks[11], (DEPTH, B_HEADS, 2 * REL_CLIP + 1), f32)
    c_forget_b = FORGET_BIAS_INIT + 0.1 * jax.random.normal(ks[12], (DEPTH, C_HEADS), f32)
    w_out = jax.random.normal(ks[13], (DEPTH, D_MIX, D_MODEL), f32) * D_MIX ** -0.5
    final_g = 1.0 + 0.02 * jax.random.normal(ks[14], (D_MODEL,), f32)
    return {"x": x, "c": c, "positions": positions, "w_ada": w_ada, "b_ada": b_ada,
            "norm_g": norm_g, "w_in": w_in, "a_q_norm_g": a_q_norm_g, "a_w_uq": a_w_uq,
            "a_kv_norm_g": a_kv_norm_g, "a_w_ukv": a_w_ukv, "b_rel_bias": b_rel_bias,
            "c_forget_b": c_forget_b, "w_out": w_out, "final_g": final_g}


def _fwd_reference(x, c, positions, w_ada, b_ada, norm_g, w_in, a_q_norm_g, a_w_uq,
              a_kv_norm_g, a_w_ukv, b_rel_bias, c_forget_b, w_out, final_g):
    cos, sin = rope_tables(positions)
    c_act = jax.nn.silu(c)
    for l in range(DEPTH):
        x = hybrid_layer(x, c_act, cos, sin, w_ada[l], b_ada[l], norm_g[l], w_in[l],
                         a_q_norm_g[l], a_w_uq[l], a_kv_norm_g[l], a_w_ukv[l],
                         b_rel_bias[l], c_forget_b[l], w_out[l])
    return rmsnorm(x, final_g)


import jax as _jax
import jax.numpy as _jnp

TWIN_FORMAT = 'train_step'
FWD_PARAMS = ['x', 'c', 'positions', 'w_ada', 'b_ada', 'norm_g', 'w_in', 'a_q_norm_g', 'a_w_uq', 'a_kv_norm_g', 'a_w_ukv', 'b_rel_bias', 'c_forget_b', 'w_out', 'final_g']
TWIN_WEIGHTS = ['w_ada', 'b_ada', 'norm_g', 'w_in', 'a_q_norm_g', 'a_w_uq', 'a_kv_norm_g', 'a_w_ukv', 'b_rel_bias', 'c_forget_b', 'w_out', 'final_g']
TWIN_DIFF_INPUT = 'x'
TWIN_INPUTS = ['x', 'c', 'positions', 'w_ada', 'b_ada', 'norm_g', 'w_in', 'a_q_norm_g', 'a_w_uq', 'a_kv_norm_g', 'a_w_ukv', 'b_rel_bias', 'c_forget_b', 'w_out', 'final_g', 'loss_target', 'm_w_ada', 'm_b_ada', 'm_norm_g', 'm_w_in', 'm_a_q_norm_g', 'm_a_w_uq', 'm_a_kv_norm_g', 'm_a_w_ukv', 'm_b_rel_bias', 'm_c_forget_b', 'm_w_out', 'm_final_g', 'v_w_ada', 'v_b_ada', 'v_norm_g', 'v_w_in', 'v_a_q_norm_g', 'v_a_w_uq', 'v_a_kv_norm_g', 'v_a_w_ukv', 'v_b_rel_bias', 'v_c_forget_b', 'v_w_out', 'v_final_g']
TWIN_OUTPUTS = ['loss', 'grad_x', 'grad_w_ada', 'grad_b_ada', 'grad_norm_g', 'grad_w_in', 'grad_a_q_norm_g', 'grad_a_w_uq', 'grad_a_kv_norm_g', 'grad_a_w_ukv', 'grad_b_rel_bias', 'grad_c_forget_b', 'grad_w_out', 'grad_final_g', 'delta_w_ada', 'delta_b_ada', 'delta_norm_g', 'delta_w_in', 'delta_a_q_norm_g', 'delta_a_w_uq', 'delta_a_kv_norm_g', 'delta_a_w_ukv', 'delta_b_rel_bias', 'delta_c_forget_b', 'delta_w_out', 'delta_final_g', 'new_m_w_ada', 'new_m_b_ada', 'new_m_norm_g', 'new_m_w_in', 'new_m_a_q_norm_g', 'new_m_a_w_uq', 'new_m_a_kv_norm_g', 'new_m_a_w_ukv', 'new_m_b_rel_bias', 'new_m_c_forget_b', 'new_m_w_out', 'new_m_final_g', 'new_v_w_ada', 'new_v_b_ada', 'new_v_norm_g', 'new_v_w_in', 'new_v_a_q_norm_g', 'new_v_a_w_uq', 'new_v_a_kv_norm_g', 'new_v_a_w_ukv', 'new_v_b_rel_bias', 'new_v_c_forget_b', 'new_v_w_out', 'new_v_final_g']
TWIN_LEAF_KINDS = {'loss': 'loss', 'grad_x': 'grad_x', 'grad_w_ada': 'grad_w', 'grad_b_ada': 'grad_w', 'grad_norm_g': 'grad_w', 'grad_w_in': 'grad_w', 'grad_a_q_norm_g': 'grad_w', 'grad_a_w_uq': 'grad_w', 'grad_a_kv_norm_g': 'grad_w', 'grad_a_w_ukv': 'grad_w', 'grad_b_rel_bias': 'grad_w', 'grad_c_forget_b': 'grad_w', 'grad_w_out': 'grad_w', 'grad_final_g': 'grad_w', 'delta_w_ada': 'delta_w', 'delta_b_ada': 'delta_w', 'delta_norm_g': 'delta_w', 'delta_w_in': 'delta_w', 'delta_a_q_norm_g': 'delta_w', 'delta_a_w_uq': 'delta_w', 'delta_a_kv_norm_g': 'delta_w', 'delta_a_w_ukv': 'delta_w', 'delta_b_rel_bias': 'delta_w', 'delta_c_forget_b': 'delta_w', 'delta_w_out': 'delta_w', 'delta_final_g': 'delta_w', 'new_m_w_ada': 'new_m', 'new_m_b_ada': 'new_m', 'new_m_norm_g': 'new_m', 'new_m_w_in': 'new_m', 'new_m_a_q_norm_g': 'new_m', 'new_m_a_w_uq': 'new_m', 'new_m_a_kv_norm_g': 'new_m', 'new_m_a_w_ukv': 'new_m', 'new_m_b_rel_bias': 'new_m', 'new_m_c_forget_b': 'new_m', 'new_m_w_out': 'new_m', 'new_m_final_g': 'new_m', 'new_v_w_ada': 'new_v', 'new_v_b_ada': 'new_v', 'new_v_norm_g': 'new_v', 'new_v_w_in': 'new_v', 'new_v_a_q_norm_g': 'new_v', 'new_v_a_w_uq': 'new_v', 'new_v_a_kv_norm_g': 'new_v', 'new_v_a_w_ukv': 'new_v', 'new_v_b_rel_bias': 'new_v', 'new_v_c_forget_b': 'new_v', 'new_v_w_out': 'new_v', 'new_v_final_g': 'new_v'}


def _forward(args):
    return _fwd_reference(*[args[k] for k in FWD_PARAMS])


def _output_shape():
    out = _jax.eval_shape(lambda: _forward(_fwd_setup_inputs(0)))
    return out.shape, out.dtype

N_MICROBATCH = 1
ADAM_LR = 0.001
ADAM_B1 = 0.9
ADAM_B2 = 0.999
ADAM_EPS = 1e-08
ADAM_WD = 0.01
ADAM_STEP = 10
PER_EXAMPLE_BATCH_AXIS = {'x': 0, 'c': 0, 'positions': 0, 'loss_target': 0}
SHARED_INPUTS = []
_WEIGHT_DTYPES = {'w_ada': _jnp.float32, 'b_ada': _jnp.float32, 'norm_g': _jnp.float32, 'w_in': _jnp.float32, 'a_q_norm_g': _jnp.float32, 'a_w_uq': _jnp.float32, 'a_kv_norm_g': _jnp.float32, 'a_w_ukv': _jnp.float32, 'b_rel_bias': _jnp.float32, 'c_forget_b': _jnp.float32, 'w_out': _jnp.float32, 'final_g': _jnp.float32}
MOMENT_SCALE = {'w_ada': 5.626381e-02, 'b_ada': 9.189709e-02, 'norm_g': 8.140965e-02, 'w_in': 5.399230e-02, 'a_q_norm_g': 1.186039e-02, 'a_w_uq': 9.363961e-03, 'a_kv_norm_g': 5.532033e-02, 'a_w_ukv': 3.182733e-02, 'b_rel_bias': 1.026957e-02, 'c_forget_b': 2.559589e-01, 'w_out': 6.117286e-02, 'final_g': 3.213809e+01}


def _to_microbatches(a, axis):
    t = _jnp.moveaxis(a, axis, 0)
    t = t.reshape((N_MICROBATCH, t.shape[0] // N_MICROBATCH) + t.shape[1:])
    return _jnp.moveaxis(t, 1, axis + 1)


def setup_inputs(seed: int = 0) -> dict:
    inp = _fwd_setup_inputs(seed)
    key = _jax.random.fold_in(_jax.random.key(seed), 7919)
    shape, _ = _output_shape()
    out = dict(inp)
    out["loss_target"] = _jax.random.normal(_jax.random.fold_in(key, 0), shape, _jnp.float32)
    for i, name in enumerate(TWIN_WEIGHTS):
        w = inp[name].astype(_jnp.float32)
        if MOMENT_SCALE is None:
            s = _jnp.sqrt(_jnp.mean(_jnp.square(w)) + 1e-30)
        else:
            s = MOMENT_SCALE[name]
        km, kv = _jax.random.split(_jax.random.fold_in(key, i + 1))
        out[name] = w
        out["m_" + name] = s * _jax.random.normal(km, w.shape, _jnp.float32)
        out["v_" + name] = (s * s) * _jax.random.uniform(kv, w.shape, _jnp.float32, 0.5, 1.5)
    if N_MICROBATCH > 1:
        for name, axis in PER_EXAMPLE_BATCH_AXIS.items():
            out[name] = _to_microbatches(out[name], axis)
    return {'x': out['x'], 'c': out['c'], 'positions': out['positions'], 'w_ada': out['w_ada'], 'b_ada': out['b_ada'], 'norm_g': out['norm_g'], 'w_in': out['w_in'], 'a_q_norm_g': out['a_q_norm_g'], 'a_w_uq': out['a_w_uq'], 'a_kv_norm_g': out['a_kv_norm_g'], 'a_w_ukv': out['a_w_ukv'], 'b_rel_bias': out['b_rel_bias'], 'c_forget_b': out['c_forget_b'], 'w_out': out['w_out'], 'final_g': out['final_g'], 'loss_target': out['loss_target'], 'm_w_ada': out['m_w_ada'], 'm_b_ada': out['m_b_ada'], 'm_norm_g': out['m_norm_g'], 'm_w_in': out['m_w_in'], 'm_a_q_norm_g': out['m_a_q_norm_g'], 'm_a_w_uq': out['m_a_w_uq'], 'm_a_kv_norm_g': out['m_a_kv_norm_g'], 'm_a_w_ukv': out['m_a_w_ukv'], 'm_b_rel_bias': out['m_b_rel_bias'], 'm_c_forget_b': out['m_c_forget_b'], 'm_w_out': out['m_w_out'], 'm_final_g': out['m_final_g'], 'v_w_ada': out['v_w_ada'], 'v_b_ada': out['v_b_ada'], 'v_norm_g': out['v_norm_g'], 'v_w_in': out['v_w_in'], 'v_a_q_norm_g': out['v_a_q_norm_g'], 'v_a_w_uq': out['v_a_w_uq'], 'v_a_kv_norm_g': out['v_a_kv_norm_g'], 'v_a_w_ukv': out['v_a_w_ukv'], 'v_b_rel_bias': out['v_b_rel_bias'], 'v_c_forget_b': out['v_c_forget_b'], 'v_w_out': out['v_w_out'], 'v_final_g': out['v_final_g']}


def _loss(weights, diff, rest, loss_target):
    with _jax.named_scope("forward"):
        args = {**rest, TWIN_DIFF_INPUT: diff, **{k: w.astype(_WEIGHT_DTYPES[k]) for k, w in weights.items()}}
        y = _forward(args)
    with _jax.named_scope("loss_head"):
        err = _jnp.square(y.astype(_jnp.float32) - loss_target)
        return 0.5 * _jnp.sum(_jnp.mean(err, axis=-1)) if err.ndim else 0.5 * err


def _adamw(w, g, m, v):
    m = ADAM_B1 * m + (1.0 - ADAM_B1) * g
    v = ADAM_B2 * v + (1.0 - ADAM_B2) * _jnp.square(g)
    m_hat = m / (1.0 - ADAM_B1 ** ADAM_STEP)
    v_hat = v / (1.0 - ADAM_B2 ** ADAM_STEP)
    delta = -ADAM_LR * (m_hat / (_jnp.sqrt(v_hat) + ADAM_EPS) + ADAM_WD * w)
    return delta, m, v


def reference(x, c, positions, w_ada, b_ada, norm_g, w_in, a_q_norm_g, a_w_uq, a_kv_norm_g, a_w_ukv, b_rel_bias, c_forget_b, w_out, final_g, loss_target, m_w_ada, m_b_ada, m_norm_g, m_w_in, m_a_q_norm_g, m_a_w_uq, m_a_kv_norm_g, m_a_w_ukv, m_b_rel_bias, m_c_forget_b, m_w_out, m_final_g, v_w_ada, v_b_ada, v_norm_g, v_w_in, v_a_q_norm_g, v_a_w_uq, v_a_kv_norm_g, v_a_w_ukv, v_b_rel_bias, v_c_forget_b, v_w_out, v_final_g):
    given = dict(x=x, c=c, positions=positions, w_ada=w_ada, b_ada=b_ada, norm_g=norm_g, w_in=w_in, a_q_norm_g=a_q_norm_g, a_w_uq=a_w_uq, a_kv_norm_g=a_kv_norm_g, a_w_ukv=a_w_ukv, b_rel_bias=b_rel_bias, c_forget_b=c_forget_b, w_out=w_out, final_g=final_g, loss_target=loss_target, m_w_ada=m_w_ada, m_b_ada=m_b_ada, m_norm_g=m_norm_g, m_w_in=m_w_in, m_a_q_norm_g=m_a_q_norm_g, m_a_w_uq=m_a_w_uq, m_a_kv_norm_g=m_a_kv_norm_g, m_a_w_ukv=m_a_w_ukv, m_b_rel_bias=m_b_rel_bias, m_c_forget_b=m_c_forget_b, m_w_out=m_w_out, m_final_g=m_final_g, v_w_ada=v_w_ada, v_b_ada=v_b_ada, v_norm_g=v_norm_g, v_w_in=v_w_in, v_a_q_norm_g=v_a_q_norm_g, v_a_w_uq=v_a_w_uq, v_a_kv_norm_g=v_a_kv_norm_g, v_a_w_ukv=v_a_w_ukv, v_b_rel_bias=v_b_rel_bias, v_c_forget_b=v_c_forget_b, v_w_out=v_w_out, v_final_g=v_final_g)
    weights = {n: given[n] for n in TWIN_WEIGHTS}
    shared = {n: given[n] for n in SHARED_INPUTS}
    per_example = {n: given[n] for n in ['x', 'c', 'positions']}
    grad_fn = _jax.value_and_grad(_loss, argnums=(0, 1))

    def one_microbatch(ex, loss_target):
        ex = dict(ex)
        diff = ex.pop(TWIN_DIFF_INPUT)
        return grad_fn(weights, diff, {**shared, **ex}, loss_target)

    if N_MICROBATCH == 1:
        loss, (grad_w, grad_x) = one_microbatch(per_example, given["loss_target"])
    else:
        def body(carry, xs):
            loss_sum, grad_sum = carry
            l_k, (gw_k, gx_k) = one_microbatch(xs[0], xs[1])
            with _jax.named_scope("update"):
                return (loss_sum + l_k, _jax.tree.map(_jnp.add, grad_sum, gw_k)), gx_k

        init = (_jnp.zeros((), _jnp.float32), _jax.tree.map(_jnp.zeros_like, weights))
        (loss, grad_w), grad_x = _jax.lax.scan(body, init, (per_example, given["loss_target"]))
    with _jax.named_scope("update"):
        delta_w, new_m, new_v = {}, {}, {}
        for n in TWIN_WEIGHTS:
            delta_w[n], new_m[n], new_v[n] = _adamw(weights[n], grad_w[n], given["m_" + n], given["v_" + n])
    return (loss, grad_x, *[grad_w[n] for n in TWIN_WEIGHTS], *[delta_w[n] for n in TWIN_WEIGHTS],
            *[new_m[n] for n in TWIN_WEIGHTS], *[new_v[n] for n in TWIN_WEIGHTS])
```

```python
import functools

import jax
import jax.numpy as jnp
from jax import lax
from jax.experimental import pallas as pl
from jax.experimental.pallas import tpu as pltpu

F32 = jnp.float32
BF16 = jnp.bfloat16

D_MODEL = 1024
DEPTH = 2
EPS = 1e-6
NEG = -1e30
ROPE_THETA = 10000.0
A_ROPE = 32
A_Q_RANK = 384
A_KV_RANK = 256
REL_CLIP = 128
N_REL = 2 * REL_CLIP + 1
N_IN = 3621

ADAM_LR = 0.001
ADAM_B1 = 0.9
ADAM_B2 = 0.999
ADAM_EPS = 1e-08
ADAM_WD = 0.01
ADAM_STEP = 10

LANE = 128
VMEM_LIMIT = 56 * 1024 * 1024

NP_IN = 4352
Z_A = (0, 768)
Z_G = (768, 1920)
Z_QKV = tuple((1920 + 384 * i, 1920 + 384 * (i + 1)) for i in range(6))
Z_F = (4224, 4352)
IN_PIECES = ((0, 672, 96), (672, 1056, 0), (2016, 2336, 64), (3301, 3621, 64), (1056, 1376, 64), (1376, 1696, 64),
             (1696, 2016, 64), (2336, 2656, 64), (2656, 2976, 64), (2976, 3296, 64), (3296, 3301, 123))
D_CAT = 1152

TM = 256
T_CAUSAL = 256
T_BAND = 128
BAND_TILES = 5
N_DEV = 8

PAY_DMOD = (0, 96)
PAY_NORM = (96, 112)
PAY_GQ = (112, 120)
PAY_GKV = (120, 128)
PAY_RB = (128, 176)
PAY_FB = (176, 184)
PAY_FINAL = (184, 192)
PAY_LOSS = (192, 200)
PAY_ROWS = 200

PACK_COLS = 1024
PACK_ROWS = 2560
HALF_ROWS = PACK_ROWS // 2


def _params(sem=None):
    return pltpu.CompilerParams(dimension_semantics=sem, vmem_limit_bytes=VMEM_LIMIT)


def _lane_iota(shape):
    return lax.broadcasted_iota(jnp.int32, shape, len(shape) - 1)


def _silu(u):
    return u * jax.nn.sigmoid(u)


def _dsilu(u):
    s = jax.nn.sigmoid(u)
    return s * (1.0 + u * (1.0 - s))


def _rms(x, g):
    r = lax.rsqrt(jnp.mean(x * x, axis=-1, keepdims=True) + EPS)
    xh = x * r
    return xh * g, xh, r


def _rms_bwd(dy, xh, r, g):
    dxh = dy * g
    return r * (dxh - xh * jnp.mean(dxh * xh, axis=-1, keepdims=True))


def _rope(x, cos, sina, sinb):
    return x * cos + pltpu.roll(x, 16, 1) * sinb + pltpu.roll(x, LANE - 16, 1) * sina


def _rope_t(dy, cos, sina, sinb):
    return dy * cos + pltpu.roll(dy * sinb, LANE - 16, 1) + pltpu.roll(dy * sina, 16, 1)


def _split3(x):
    hi = x.astype(BF16)
    r1 = x - hi.astype(F32)
    mid = r1.astype(BF16)
    lo = (r1 - mid.astype(F32)).astype(BF16)
    return hi, mid, lo


def _dot(a, b):
    return jnp.dot(a, b, preferred_element_type=F32)


def _dot_nt(a, b):
    return lax.dot_general(a, b, (((1,), (1,)), ((), ())), preferred_element_type=F32)


def _dot_tn(a, b):
    return lax.dot_general(a, b, (((0,), (0,)), ((), ())), preferred_element_type=F32)


def _row_spec(cols):
    return pl.BlockSpec((TM, cols), lambda i: (i, 0))


def _full_spec(shape):
    return pl.BlockSpec(shape, lambda i: (0,) * len(shape))


def _ex_spec(tiles_per_ex):
    return pl.BlockSpec((1, 1, D_MODEL), lambda i: (i // tiles_per_ex, 0, 0))


def _ln_inproj(x, shift, scale, g, w_in_p, seq):
    t = x.shape[0]

    def body(x_ref, sh_ref, sc_ref, g_ref, w_ref, h_ref, za_ref, zg_ref, q0, q1, q2, q3, q4, q5, zf_ref):
        n, _, _ = _rms(x_ref[...], g_ref[...])
        h = (n * (1.0 + sc_ref[0]) + sh_ref[0]).astype(BF16)
        h_ref[...] = h
        za_ref[...] = _dot(h, w_ref[:, Z_A[0]:Z_A[1]])
        zg_ref[...] = _dot(h, w_ref[:, Z_G[0]:Z_G[1]])
        for ref, (c0, c1) in zip((q0, q1, q2, q3, q4, q5), Z_QKV):
            ref[...] = _dot(h, w_ref[:, c0:c1]).astype(BF16)
        zf_ref[...] = _dot(h, w_ref[:, Z_F[0]:Z_F[1]])

    tpe = seq // TM
    shapes = [jax.ShapeDtypeStruct((t, D_MODEL), BF16), jax.ShapeDtypeStruct((t, 768), F32),
              jax.ShapeDtypeStruct((t, D_CAT), F32)]
    shapes += [jax.ShapeDtypeStruct((t, 384), BF16)] * 6 + [jax.ShapeDtypeStruct((t, LANE), F32)]
    return pl.pallas_call(
        body, name="ln_inproj", grid=(t // TM,),
        in_specs=[_row_spec(D_MODEL), _ex_spec(tpe), _ex_spec(tpe), _full_spec((1, D_MODEL)),
                  _full_spec((D_MODEL, NP_IN))],
        out_specs=[_row_spec(D_MODEL), _row_spec(768), _row_spec(D_CAT)] + [_row_spec(384)] * 6 + [_row_spec(LANE)],
        out_shape=shapes, compiler_params=_params(("parallel",)),
    )(x, shift, scale, g, w_in_p)


def _a_up(za, gq, gkv, w_uq_p, w_ukv_p, cos, sina, sinb):
    t = za.shape[0]

    def body(za_ref, gq_ref, gkv_ref, wq_ref, wkv_ref, cos_ref, sa_ref, sb_ref, q_ref, k_ref, v_ref):
        cos_t, sa, sb = cos_ref[...], sa_ref[...], sb_ref[...]
        cqn, _, _ = _rms(za_ref[:, 0:384], gq_ref[...])
        q = _dot(cqn.astype(BF16), wq_ref[...])
        ckvn, _, _ = _rms(za_ref[:, 384:640], gkv_ref[...])
        kv = _dot(ckvn.astype(BF16), wkv_ref[...])
        kpe = za_ref[:, 640:768]
        kpe = _rope(kpe + pltpu.roll(kpe, 32, 1), cos_t, sa, sb).astype(BF16)
        for p in range(3):
            q_ref[:, 256 * p:256 * p + 128] = q[:, 256 * p:256 * p + 128].astype(BF16)
            q_ref[:, 256 * p + 128:256 * p + 256] = _rope(q[:, 256 * p + 128:256 * p + 256], cos_t, sa, sb).astype(BF16)
            k_ref[:, 256 * p:256 * p + 128] = kv[:, 128 * p:128 * p + 128].astype(BF16)
            k_ref[:, 256 * p + 128:256 * p + 256] = kpe
        v_ref[...] = kv[:, 384:768].astype(BF16)

    return pl.pallas_call(
        body, name="a_up", grid=(t // TM,),
        in_specs=[_row_spec(768), _full_spec((1, 384)), _full_spec((1, 256)), _full_spec((384, 768)),
                  _full_spec((256, 768)), _row_spec(LANE), _row_spec(LANE), _row_spec(LANE)],
        out_specs=[_row_spec(768), _row_spec(768), _row_spec(384)],
        out_shape=[jax.ShapeDtypeStruct((t, 768), BF16), jax.ShapeDtypeStruct((t, 768), BF16),
                   jax.ShapeDtypeStruct((t, 384), BF16)],
        compiler_params=_params(("parallel",)),
    )(za, gq, gkv, w_uq_p, w_ukv_p, cos, sina, sinb)


def _tri(n, upper):
    r = lax.broadcasted_iota(jnp.int32, (n, n), 0)
    c = lax.broadcasted_iota(jnp.int32, (n, n), 1)
    return jnp.where((c >= r) if upper else (c <= r), 1.0, 0.0).astype(BF16)


def _forget_fwd(zf, fb, nb, seq):
    blk = 256

    def body(zf_ref, fb_ref, f_ref):
        tri = _tri(blk, False)
        live = _lane_iota((blk, LANE)) < 5
        carry = jnp.zeros((1, LANE), F32)
        for i in range(seq // blk):
            u = zf_ref[i * blk:(i + 1) * blk, :] + fb_ref[...]
            lf = jnp.where(live, jnp.minimum(u, 0.0) - jnp.log(1.0 + jnp.exp(-jnp.abs(u))), 0.0)
            hi, mid, lo = _split3(lf)
            f_ref[i * blk:(i + 1) * blk, :] = _dot(tri, hi) + _dot(tri, mid) + _dot(tri, lo) + carry
            carry = carry + jnp.sum(lf, axis=0, keepdims=True)

    return pl.pallas_call(
        body, name="forget_fwd", grid=(nb,),
        in_specs=[pl.BlockSpec((seq, LANE), lambda b: (b, 0)), pl.BlockSpec((1, LANE), lambda b: (0, 0))],
        out_specs=pl.BlockSpec((seq, LANE), lambda b: (b, 0)),
        out_shape=jax.ShapeDtypeStruct((nb * seq, LANE), F32), compiler_params=_params(("parallel",)),
    )(zf, fb)


def _forget_bwd(df, zf, fb, nb, seq):
    blk = 256

    def body(df_ref, zf_ref, fb_ref, dz_ref, gb_ref):
        @pl.when(pl.program_id(0) == 0)
        def _():
            gb_ref[...] = jnp.zeros_like(gb_ref)

        tri = _tri(blk, True)
        live = _lane_iota((blk, LANE)) < 5
        carry = jnp.zeros((1, LANE), F32)
        gsum = jnp.zeros((1, LANE), F32)
        for i in reversed(range(seq // blk)):
            d = df_ref[i * blk:(i + 1) * blk, :]
            hi, mid, lo = _split3(d)
            dlf = _dot(tri, hi) + _dot(tri, mid) + _dot(tri, lo) + carry
            carry = carry + jnp.sum(d, axis=0, keepdims=True)
            u = zf_ref[i * blk:(i + 1) * blk, :] + fb_ref[...]
            du = jnp.where(live, dlf * jax.nn.sigmoid(-u), 0.0)
            dz_ref[i * blk:(i + 1) * blk, :] = du.astype(BF16)
            gsum = gsum + jnp.sum(du, axis=0, keepdims=True)
        gb_ref[...] += jnp.broadcast_to(gsum, gb_ref.shape)

    return pl.pallas_call(
        body, name="forget_bwd", grid=(nb,),
        in_specs=[pl.BlockSpec((seq, LANE), lambda b: (b, 0)), pl.BlockSpec((seq, LANE), lambda b: (b, 0)),
                  pl.BlockSpec((1, LANE), lambda b: (0, 0))],
        out_specs=[pl.BlockSpec((seq, LANE), lambda b: (b, 0)), pl.BlockSpec((8, LANE), lambda b: (0, 0))],
        out_shape=[jax.ShapeDtypeStruct((nb * seq, LANE), BF16), jax.ShapeDtypeStruct((8, LANE), F32)],
        compiler_params=_params(("arbitrary",)),
    )(df, zf, fb)


def _gate_outproj(x, gate, oa, ob, oc, zg, w_out_p, seq):
    t = x.shape[0]

    def body(x_ref, gate_ref, oa_ref, ob_ref, oc_ref, zg_ref, w_ref, y_ref, xn_ref):
        y = jnp.zeros((TM, D_MODEL), F32)
        for i, o_ref in enumerate((oa_ref, ob_ref, oc_ref)):
            cat = (o_ref[...] * _silu(zg_ref[:, 384 * i:384 * (i + 1)])).astype(BF16)
            y = y + _dot(cat, w_ref[384 * i:384 * (i + 1), :])
        y_ref[...] = y
        xn_ref[...] = x_ref[...] + gate_ref[0] * y

    return pl.pallas_call(
        body, name="gate_outproj", grid=(t // TM,),
        in_specs=[_row_spec(D_MODEL), _ex_spec(seq // TM), _row_spec(384), _row_spec(384), _row_spec(384),
                  _row_spec(D_CAT), _full_spec((D_CAT, D_MODEL))],
        out_specs=[_row_spec(D_MODEL), _row_spec(D_MODEL)],
        out_shape=[jax.ShapeDtypeStruct((t, D_MODEL), F32)] * 2, compiler_params=_params(("parallel",)),
    )(x, gate, oa, ob, oc, zg, w_out_p)


def _final_loss(x, target, g):
    t = x.shape[0]

    def body(x_ref, t_ref, g_ref, dx_ref, loss_ref, gg_ref):
        @pl.when(pl.program_id(0) == 0)
        def _():
            loss_ref[...] = jnp.zeros_like(loss_ref)
            gg_ref[...] = jnp.zeros_like(gg_ref)

        gv = g_ref[...]
        out, xh, r = _rms(x_ref[...], gv)
        err = out - t_ref[...]
        loss_ref[...] += 0.5 * jnp.sum(jnp.mean(err * err, axis=-1, keepdims=True), axis=0, keepdims=True)
        dout = err / D_MODEL
        gg_ref[...] += jnp.broadcast_to(jnp.sum(dout * xh, axis=0, keepdims=True), gg_ref.shape)
        dx_ref[...] = _rms_bwd(dout, xh, r, gv)

    return pl.pallas_call(
        body, name="final_loss", grid=(t // TM,),
        in_specs=[_row_spec(D_MODEL), _row_spec(D_MODEL), _full_spec((1, D_MODEL))],
        out_specs=[_row_spec(D_MODEL), _full_spec((8, LANE)), _full_spec((8, D_MODEL))],
        out_shape=[jax.ShapeDtypeStruct((t, D_MODEL), F32), jax.ShapeDtypeStruct((8, LANE), F32),
                   jax.ShapeDtypeStruct((8, D_MODEL), F32)],
        compiler_params=_params(("arbitrary",)),
    )(x, target, g)


def _head_masks(kind, rows, dq, h):
    lq = _lane_iota((rows, dq))
    lv = _lane_iota((rows, LANE))
    mq = (lq >= 64 * h) & (lq < 64 * h + 64)
    if kind == "A":
        mq = mq | ((lq >= 128 + 32 * h) & (lq < 160 + 32 * h))
    return mq, (lv >= 64 * h) & (lv < 64 * h + 64)


def _tile_mask(kind, tile, m=None):
    row = lax.broadcasted_iota(jnp.int32, (tile, tile), 0)
    col = lax.broadcasted_iota(jnp.int32, (tile, tile), 1)
    if kind == "A":
        return (col >> 6) <= (row >> 6)
    if kind == "C":
        return col <= row
    first = (m == 0) & (row >= 64) & (col < 64)
    last = (m == BAND_TILES - 1) & (row < 64) & (col >= 64)
    return jnp.logical_not(first | last)


def _attn_scale(kind):
    return 96.0 ** -0.5 if kind == "A" else 0.125


def _attn_fwd(kind, q, k, v, aux, nb, seq):
    dq = q.shape[1] // 3
    tile = T_BAND if kind == "B" else T_CAUSAL
    nq = seq // tile
    scale = _attn_scale(kind)

    def body(*refs):
        if kind == "A":
            q_ref, k_ref, v_ref, o_ref, lse_ref = refs
            aux_ref = None
        else:
            q_ref, k_ref, v_ref, aux_ref, o_ref, lse_ref = refs
        qi = pl.program_id(2)
        q2 = q_ref[...]
        res = []
        for h in range(2):
            mq, _ = _head_masks(kind, tile, dq, h)
            qh = jnp.where(mq, q2, jnp.zeros_like(q2))

            def step(kj, carry, m=None, diag=False):
                mx, l, acc = carry
                ks = pl.ds(pl.multiple_of(kj * tile, tile), tile)
                kt = k_ref[ks, :]
                vt = v_ref[ks, :]
                s = _dot_nt(qh, kt) * scale
                if kind == "B":
                    s = jnp.where(_tile_mask("B", tile, m), s + aux_ref[h, m], NEG)
                if kind == "C":
                    s = s - aux_ref[0, h, pl.ds(kj, 1), :]
                if diag:
                    s = jnp.where(_tile_mask(kind, tile), s, NEG)
                mn = jnp.maximum(mx, jnp.max(s, axis=-1, keepdims=True))
                alpha = jnp.exp(mx - mn)
                p = jnp.exp(s - mn)
                l = alpha * l + jnp.sum(p, axis=-1, keepdims=True)
                acc = alpha * acc + _dot(p.astype(BF16), vt)
                return mn, l, acc

            init = (jnp.full((tile, 1), NEG, F32), jnp.zeros((tile, 1), F32), jnp.zeros((tile, LANE), F32))
            if kind == "B":
                m0 = jnp.maximum(BAND_TILES - 1 - qi, 0)
                mx, l, acc = lax.fori_loop(m0, BAND_TILES, lambda m, c: step(qi - (BAND_TILES - 1) + m, c, m=m), init)
            else:
                carry = lax.fori_loop(0, qi, lambda kj, c: step(kj, c), init)
                mx, l, acc = step(qi, carry, diag=True)
            res.append((acc / l, mx + jnp.log(l)))
        first = _lane_iota((tile, LANE)) < 64
        o_ref[...] = jnp.where(first, res[0][0], res[1][0])
        lse_ref[...] = jnp.where(first, res[0][1], res[1][1])

    in_specs = [pl.BlockSpec((tile, dq), lambda b, p, i: (b * nq + i, p)),
                pl.BlockSpec((seq, dq), lambda b, p, i: (b, p)),
                pl.BlockSpec((seq, LANE), lambda b, p, i: (b, p))]
    args = [q, k, v]
    if kind == "B":
        in_specs.append(pl.BlockSpec((2, BAND_TILES, tile, tile), lambda b, p, i: (p, 0, 0, 0)))
        args.append(aux)
    if kind == "C":
        in_specs.append(pl.BlockSpec((1, 2, nq, tile), lambda b, p, i: (b, p, 0, 0)))
        args.append(aux)
    out_spec = pl.BlockSpec((tile, LANE), lambda b, p, i: (b * nq + i, p))
    return pl.pallas_call(
        body, name="attn_fwd_" + kind, grid=(nb, 3, nq), in_specs=in_specs, out_specs=[out_spec, out_spec],
        out_shape=[jax.ShapeDtypeStruct((nb * seq, 384), F32)] * 2,
        compiler_params=_params(("parallel", "parallel", "parallel")),
    )(*args)


def _attn_bwd(kind, q, k, v, o, do, lse, aux, nb, seq):
    dq = q.shape[1] // 3
    tile = T_BAND if kind == "B" else T_CAUSAL
    nq = seq // tile
    scale = _attn_scale(kind)
    dqk_dtype = F32 if kind == "A" else BF16

    def body(*refs):
        dfr_ref = dfq_ref = dbt_ref = aux_ref = None
        if kind == "A":
            q_ref, k_ref, v_ref, o_ref, do_ref, lse_ref, dq_ref, dk_ref, dv_ref, dk_acc, dv_acc = refs
        elif kind == "B":
            q_ref, k_ref, v_ref, o_ref, do_ref, lse_ref, aux_ref, dq_ref, dk_ref, dv_ref, dbt_ref, dk_acc, dv_acc = refs
        else:
            (q_ref, k_ref, v_ref, o_ref, do_ref, lse_ref, aux_ref, dq_ref, dk_ref, dv_ref, dfr_ref, dfq_ref,
             dk_acc, dv_acc) = refs
        dk_acc[...] = jnp.zeros_like(dk_acc)
        dv_acc[...] = jnp.zeros_like(dv_acc)
        if kind == "C":
            dfr_ref[...] = jnp.zeros_like(dfr_ref)
        if kind == "B":
            @pl.when(pl.program_id(1) == 0)
            def _():
                dbt_ref[...] = jnp.zeros_like(dbt_ref)

        def q_step(qi, _):
            qs = pl.ds(pl.multiple_of(qi * tile, tile), tile)
            q2 = q_ref[qs, :]
            do2 = do_ref[qs, :]
            o2 = o_ref[qs, :]
            lse2 = lse_ref[qs, :]
            dq_tot = jnp.zeros((tile, dq), F32)
            row_sums = []
            for h in range(2):
                mq, mv = _head_masks(kind, tile, dq, h)
                qh = jnp.where(mq, q2, jnp.zeros_like(q2))
                doh = jnp.where(mv, do2, 0.0)
                dob = doh.astype(BF16)
                delta = jnp.sum(doh * o2, axis=-1, keepdims=True)
                lseh = jnp.max(jnp.where(mv, lse2, NEG), axis=-1, keepdims=True)

                def step(kj, carry, m=None, diag=False):
                    dq_t, rs = carry
                    ks = pl.ds(pl.multiple_of(kj * tile, tile), tile)
                    kt = k_ref[ks, :]
                    vt = v_ref[ks, :]
                    s = _dot_nt(qh, kt) * scale
                    if kind == "B":
                        s = jnp.where(_tile_mask("B", tile, m), s + aux_ref[h, m], NEG)
                    if kind == "C":
                        s = s - aux_ref[0, h, pl.ds(kj, 1), :]
                    if diag:
                        s = jnp.where(_tile_mask(kind, tile), s, NEG)
                    p = jnp.exp(s - lseh)
                    ds = p * (_dot_nt(dob, vt) - delta)
                    if kind == "B":
                        dbt_ref[h, m] += ds
                    if kind == "C":
                        dfr_ref[0, h, pl.ds(kj, 1), :] -= jnp.sum(ds, axis=0, keepdims=True)
                        rs = rs + jnp.sum(ds, axis=-1, keepdims=True)
                    dss = (ds * scale).astype(BF16)
                    dv_acc[ks, :] += _dot_tn(p.astype(BF16), dob)
                    dk_acc[ks, :] += _dot_tn(dss, qh)
                    return dq_t + jnp.where(mq, _dot(dss, kt), 0.0), rs

                zero = (jnp.zeros((tile, dq), F32), jnp.zeros((tile, 1), F32))
                if kind == "B":
                    m0 = jnp.maximum(BAND_TILES - 1 - qi, 0)
                    dq_h, rs_h = lax.fori_loop(m0, BAND_TILES, lambda m, c: step(qi - (BAND_TILES - 1) + m, c, m=m), zero)
                else:
                    carry = lax.fori_loop(0, qi, lambda kj, c: step(kj, c), zero)
                    dq_h, rs_h = step(qi, carry, diag=True)
                dq_tot = dq_tot + dq_h
                row_sums.append(rs_h)
            dq_ref[qs, :] = dq_tot.astype(dqk_dtype)
            if kind == "C":
                dfq_ref[qs, :] = jnp.where(_lane_iota((tile, LANE)) < 64, row_sums[0], row_sums[1])
            return 0

        lax.fori_loop(0, nq, q_step, 0)
        dk_ref[...] = dk_acc[...].astype(dqk_dtype)
        dv_ref[...] = dv_acc[...].astype(BF16)

    def seq_spec(cols):
        return pl.BlockSpec((seq, cols), lambda p, b: (b, p))

    in_specs = [seq_spec(dq), seq_spec(dq), seq_spec(LANE), seq_spec(LANE), seq_spec(LANE), seq_spec(LANE)]
    args = [q, k, v, o, do, lse]
    out_specs = [seq_spec(dq), seq_spec(dq), seq_spec(LANE)]
    out_shape = [jax.ShapeDtypeStruct((nb * seq, 3 * dq), dqk_dtype)] * 2 + [jax.ShapeDtypeStruct((nb * seq, 384), BF16)]
    if kind == "B":
        spec = pl.BlockSpec((2, BAND_TILES, tile, tile), lambda p, b: (p, 0, 0, 0))
        in_specs.append(spec)
        args.append(aux)
        out_specs.append(spec)
        out_shape.append(jax.ShapeDtypeStruct((6, BAND_TILES, tile, tile), F32))
    if kind == "C":
        spec = pl.BlockSpec((1, 2, nq, tile), lambda p, b: (b, p, 0, 0))
        in_specs.append(spec)
        args.append(aux)
        out_specs += [spec, seq_spec(LANE)]
        out_shape += [jax.ShapeDtypeStruct((nb, 6, nq, tile), F32), jax.ShapeDtypeStruct((nb * seq, 384), F32)]
    return pl.pallas_call(
        body, name="attn_bwd_" + kind, grid=(3, nb), in_specs=in_specs, out_specs=out_specs, out_shape=out_shape,
        scratch_shapes=[pltpu.VMEM((seq, dq), F32), pltpu.VMEM((seq, LANE), F32)],
        compiler_params=_params(("arbitrary", "arbitrary")),
    )(*args)


def _rel_onehot(chunk):
    f = pl.program_id(0) * chunk + lax.broadcasted_iota(jnp.int32, (384, chunk), 1)
    r = lax.broadcasted_iota(jnp.int32, (384, chunk), 0)
    m = f >> 14
    i = (f >> 7) & 127
    j = f & 127
    idx = jnp.clip((BAND_TILES - 1 - m) * T_BAND + i - j, -REL_CLIP, REL_CLIP) + REL_CLIP
    return jnp.where(idx == r, 1.0, 0.0).astype(BF16)


BIAS_FLAT = BAND_TILES * T_BAND * T_BAND
BIAS_CHUNK = 2048


def _bias_expand(rb8):
    def body(rb_ref, out_ref):
        oh = _rel_onehot(BIAS_CHUNK)
        hi, mid, lo = _split3(rb_ref[...])
        out_ref[...] = _dot(hi, oh) + _dot(mid, oh) + _dot(lo, oh)

    return pl.pallas_call(
        body, name="bias_expand", grid=(BIAS_FLAT // BIAS_CHUNK,),
        in_specs=[pl.BlockSpec((8, 384), lambda i: (0, 0))], out_specs=pl.BlockSpec((8, BIAS_CHUNK), lambda i: (0, i)),
        out_shape=jax.ShapeDtypeStruct((8, BIAS_FLAT), F32), compiler_params=_params(("parallel",)),
    )(rb8)


def _bias_reduce(dbt8):
    def body(d_ref, out_ref):
        @pl.when(pl.program_id(0) == 0)
        def _():
            out_ref[...] = jnp.zeros_like(out_ref)

        oh = _rel_onehot(BIAS_CHUNK)
        hi, mid, lo = _split3(d_ref[...])
        out_ref[...] += _dot_nt(hi, oh) + _dot_nt(mid, oh) + _dot_nt(lo, oh)

    return pl.pallas_call(
        body, name="bias_reduce", grid=(BIAS_FLAT // BIAS_CHUNK,),
        in_specs=[pl.BlockSpec((8, BIAS_CHUNK), lambda i: (0, i))], out_specs=pl.BlockSpec((8, 384), lambda i: (0, 0)),
        out_shape=jax.ShapeDtypeStruct((8, 384), F32), compiler_params=_params(("arbitrary",)),
    )(dbt8)


def _outproj_bwd(dxn, y, gate, oa, ob, oc, zg, w_out_p, w_out_pt, nb, seq):
    t = dxn.shape[0]
    tpe = seq // TM

    def body(dxn_ref, y_ref, gate_ref, oa_ref, ob_ref, oc_ref, zg_ref, w_ref, wt_ref,
             doa_ref, dob_ref, doc_ref, dzg_ref, gw_ref, dgate_ref):
        i = pl.program_id(0)

        @pl.when(i == 0)
        def _():
            gw_ref[...] = jnp.zeros_like(gw_ref)

        @pl.when(i % tpe == 0)
        def _():
            dgate_ref[...] = jnp.zeros_like(dgate_ref)

        dxn_t = dxn_ref[...]
        dgate_ref[0] += jnp.sum(dxn_t * y_ref[...], axis=0, keepdims=True)
        dy = (dxn_t * gate_ref[0]).astype(BF16)
        for gi, (o_ref, do_ref) in enumerate(((oa_ref, doa_ref), (ob_ref, dob_ref), (oc_ref, doc_ref))):
            cols = slice(384 * gi, 384 * (gi + 1))
            u = zg_ref[:, cols]
            o_t = o_ref[...]
            su = _silu(u)
            dcat = _dot(dy, wt_ref[:, cols])
            do_ref[...] = dcat * su
            dzg_ref[:, cols] = (dcat * o_t * _dsilu(u)).astype(BF16)
            gw_ref[cols, :] += _dot_tn((o_t * su).astype(BF16), dy)

    return pl.pallas_call(
        body, name="outproj_bwd", grid=(t // TM,),
        in_specs=[_row_spec(D_MODEL), _row_spec(D_MODEL), _ex_spec(tpe), _row_spec(384), _row_spec(384), _row_spec(384),
                  _row_spec(D_CAT), _full_spec((D_CAT, D_MODEL)), _full_spec((D_MODEL, D_CAT))],
        out_specs=[_row_spec(384), _row_spec(384), _row_spec(384), _row_spec(D_CAT), _full_spec((D_CAT, D_MODEL)),
                   _ex_spec(tpe)],
        out_shape=[jax.ShapeDtypeStruct((t, 384), F32)] * 3 + [jax.ShapeDtypeStruct((t, D_CAT), BF16),
                                                                jax.ShapeDtypeStruct((D_CAT, D_MODEL), F32),
                                                                jax.ShapeDtypeStruct((nb, 1, D_MODEL), F32)],
        compiler_params=_params(("arbitrary",)),
    )(dxn, y, gate, oa, ob, oc, zg, w_out_p, w_out_pt)


def _a_up_bwd(dqa, dka, dva, za, gq, gkv, w_uq_pt, w_ukv_pt, cos, sina, sinb):
    t = za.shape[0]

    def body(dq_ref, dk_ref, dv_ref, za_ref, gq_ref, gkv_ref, wqt_ref, wkvt_ref, cos_ref, sa_ref, sb_ref,
             dza_ref, gwq_ref, gwkv_ref, ggq_ref, ggkv_ref, dqb, dkvb):
        @pl.when(pl.program_id(0) == 0)
        def _():
            gwq_ref[...] = jnp.zeros_like(gwq_ref)
            gwkv_ref[...] = jnp.zeros_like(gwkv_ref)
            ggq_ref[...] = jnp.zeros_like(ggq_ref)
            ggkv_ref[...] = jnp.zeros_like(ggkv_ref)

        cos_t, sa, sb = cos_ref[...], sa_ref[...], sb_ref[...]
        dkpe = jnp.zeros((TM, LANE), F32)
        for p in range(3):
            dqb[:, 256 * p:256 * p + 128] = dq_ref[:, 256 * p:256 * p + 128].astype(BF16)
            dqb[:, 256 * p + 128:256 * p + 256] = _rope_t(dq_ref[:, 256 * p + 128:256 * p + 256], cos_t, sa, sb).astype(BF16)
            dkvb[:, 128 * p:128 * p + 128] = dk_ref[:, 256 * p:256 * p + 128].astype(BF16)
            dkpe = dkpe + dk_ref[:, 256 * p + 128:256 * p + 256]
        dkvb[:, 384:768] = dv_ref[...]
        dkpe = _rope_t(dkpe, cos_t, sa, sb)
        dkpe = jnp.where(_lane_iota((TM, LANE)) < A_ROPE, dkpe + pltpu.roll(dkpe, LANE - 32, 1), 0.0)

        gqv = gq_ref[...]
        cqn, cqh, rq = _rms(za_ref[:, 0:384], gqv)
        dq_t = dqb[...]
        gwq_ref[...] += _dot_tn(cqn.astype(BF16), dq_t)
        dcqn = _dot(dq_t, wqt_ref[...])
        ggq_ref[...] += jnp.broadcast_to(jnp.sum(dcqn * cqh, axis=0, keepdims=True), ggq_ref.shape)
        dza_ref[:, 0:384] = _rms_bwd(dcqn, cqh, rq, gqv).astype(BF16)

        gkvv = gkv_ref[...]
        ckvn, ckvh, rkv = _rms(za_ref[:, 384:640], gkvv)
        dkv_t = dkvb[...]
        gwkv_ref[...] += _dot_tn(ckvn.astype(BF16), dkv_t)
        dckvn = _dot(dkv_t, wkvt_ref[...])
        ggkv_ref[...] += jnp.broadcast_to(jnp.sum(dckvn * ckvh, axis=0, keepdims=True), ggkv_ref.shape)
        dza_ref[:, 384:640] = _rms_bwd(dckvn, ckvh, rkv, gkvv).astype(BF16)
        dza_ref[:, 640:768] = dkpe.astype(BF16)

    return pl.pallas_call(
        body, name="a_up_bwd", grid=(t // TM,),
        in_specs=[_row_spec(768), _row_spec(768), _row_spec(384), _row_spec(768), _full_spec((1, 384)),
                  _full_spec((1, 256)), _full_spec((768, 384)), _full_spec((768, 256)), _row_spec(LANE), _row_spec(LANE),
                  _row_spec(LANE)],
        out_specs=[_row_spec(768), _full_spec((384, 768)), _full_spec((256, 768)), _full_spec((8, 384)),
                   _full_spec((8, 256))],
        out_shape=[jax.ShapeDtypeStruct((t, 768), BF16), jax.ShapeDtypeStruct((384, 768), F32),
                   jax.ShapeDtypeStruct((256, 768), F32), jax.ShapeDtypeStruct((8, 384), F32),
                   jax.ShapeDtypeStruct((8, 256), F32)],
        scratch_shapes=[pltpu.VMEM((TM, 768), BF16), pltpu.VMEM((TM, 768), BF16)],
        compiler_params=_params(("arbitrary",)),
    )(dqa, dka, dva, za, gq, gkv, w_uq_pt, w_ukv_pt, cos, sina, sinb)


def _dz_cols():
    return (Z_A, Z_G) + Z_QKV + (Z_F,)


def _inproj_bwd_dx(dz, dxn, x, shift, scale, g, w_in_pt, nb, seq):
    t = x.shape[0]
    tpe = seq // TM
    cols = _dz_cols()

    def body(*refs):
        dz_refs = refs[:len(cols)]
        dxn_ref, x_ref, sh_ref, sc_ref, g_ref, wt_ref, dx_ref, dsh_ref, dsc_ref, dg_ref = refs[len(cols):]
        i = pl.program_id(0)

        @pl.when(i == 0)
        def _():
            dg_ref[...] = jnp.zeros_like(dg_ref)

        @pl.when(i % tpe == 0)
        def _():
            dsh_ref[...] = jnp.zeros_like(dsh_ref)
            dsc_ref[...] = jnp.zeros_like(dsc_ref)

        dh = jnp.zeros((TM, D_MODEL), F32)
        for ref, (c0, c1) in zip(dz_refs, cols):
            dh = dh + _dot(ref[...], wt_ref[c0:c1, :])
        gv = g_ref[...]
        n, xh, r = _rms(x_ref[...], gv)
        dsh_ref[0] += jnp.sum(dh, axis=0, keepdims=True)
        dsc_ref[0] += jnp.sum(dh * n, axis=0, keepdims=True)
        dn = dh * (1.0 + sc_ref[0])
        dg_ref[...] += jnp.broadcast_to(jnp.sum(dn * xh, axis=0, keepdims=True), dg_ref.shape)
        dx_ref[...] = dxn_ref[...] + _rms_bwd(dn, xh, r, gv)

    in_specs = [_row_spec(c1 - c0) for c0, c1 in cols]
    in_specs += [_row_spec(D_MODEL), _row_spec(D_MODEL), _ex_spec(tpe), _ex_spec(tpe), _full_spec((1, D_MODEL)),
                 _full_spec((NP_IN, D_MODEL))]
    return pl.pallas_call(
        body, name="inproj_bwd_dx", grid=(t // TM,), in_specs=in_specs,
        out_specs=[_row_spec(D_MODEL), _ex_spec(tpe), _ex_spec(tpe), _full_spec((8, D_MODEL))],
        out_shape=[jax.ShapeDtypeStruct((t, D_MODEL), F32), jax.ShapeDtypeStruct((nb, 1, D_MODEL), F32),
                   jax.ShapeDtypeStruct((nb, 1, D_MODEL), F32), jax.ShapeDtypeStruct((8, D_MODEL), F32)],
        compiler_params=_params(("arbitrary",)),
    )(*dz, dxn, x, shift, scale, g, w_in_pt)


def _inproj_bwd_dw(h, dz, name):
    t = h.shape[0]
    widths = [d.shape[1] for d in dz]
    total = sum(widths)

    def body(*refs):
        h_ref = refs[0]
        dz_refs = refs[1:1 + len(dz)]
        gw_ref = refs[1 + len(dz)]

        @pl.when(pl.program_id(0) == 0)
        def _():
            gw_ref[...] = jnp.zeros_like(gw_ref)

        h_t = h_ref[...]
        c0 = 0
        for ref, w in zip(dz_refs, widths):
            gw_ref[:, c0:c0 + w] += _dot_tn(h_t, ref[...])
            c0 += w

    return pl.pallas_call(
        body, name=name, grid=(t // TM,), in_specs=[_row_spec(D_MODEL)] + [_row_spec(w) for w in widths],
        out_specs=_full_spec((D_MODEL, total)), out_shape=jax.ShapeDtypeStruct((D_MODEL, total), F32),
        compiler_params=_params(("arbitrary",)),
    )(h, *dz)


def _ada_fwd(c_all, w_ada, b_cols):
    n = c_all.shape[0]
    cols = w_ada.shape[2]

    def body(c_ref, w_ref, b_ref, out_ref):
        act = _silu(c_ref[...]).astype(BF16)
        out_ref[0] = _dot(act, w_ref[0].astype(BF16)) + b_ref[0]

    return pl.pallas_call(
        body, name="ada_fwd", grid=(DEPTH,),
        in_specs=[pl.BlockSpec((n, D_MODEL), lambda l: (0, 0)), pl.BlockSpec((1, D_MODEL, cols), lambda l: (l, 0, 0)),
                  pl.BlockSpec((1, 1, cols), lambda l: (l, 0, 0))],
        out_specs=pl.BlockSpec((1, n, cols), lambda l: (l, 0, 0)),
        out_shape=jax.ShapeDtypeStruct((DEPTH, n, cols), F32), compiler_params=_params(("parallel",)),
    )(c_all, w_ada, b_cols)


def _ada_bwd(c_all, dmod_cols, dmod_all):
    n = c_all.shape[0]
    cols = dmod_cols.shape[2]

    def body(c_ref, dc_ref, da_ref, gw_ref, gb_ref):
        act = _silu(c_ref[...]).astype(BF16)
        gw_ref[0] = _dot_tn(act, dc_ref[0].astype(BF16))
        gb_ref[0] = jnp.sum(da_ref[0], axis=0, keepdims=True)

    return pl.pallas_call(
        body, name="ada_bwd", grid=(DEPTH,),
        in_specs=[pl.BlockSpec((n, D_MODEL), lambda l: (0, 0)), pl.BlockSpec((1, n, cols), lambda l: (l, 0, 0)),
                  pl.BlockSpec((1, n, 3 * D_MODEL), lambda l: (l, 0, 0))],
        out_specs=[pl.BlockSpec((1, D_MODEL, cols), lambda l: (l, 0, 0)),
                   pl.BlockSpec((1, 1, 3 * D_MODEL), lambda l: (l, 0, 0))],
        out_shape=[jax.ShapeDtypeStruct((DEPTH, D_MODEL, cols), F32), jax.ShapeDtypeStruct((DEPTH, 1, 3 * D_MODEL), F32)],
        compiler_params=_params(("parallel",)),
    )(c_all, dmod_cols, dmod_all)


def _sum_blocks(parts, name):
    n, rows, cols = parts.shape
    tr = rows if rows <= 256 else 8 * next(d for d in range(32, 0, -1) if (rows // 8) % d == 0)

    def body(p_ref, out_ref):
        acc = p_ref[0].astype(F32)
        for k in range(1, n):
            acc = acc + p_ref[k].astype(F32)
        out_ref[...] = acc

    return pl.pallas_call(
        body, name=name, grid=(rows // tr,), in_specs=[pl.BlockSpec((n, tr, cols), lambda i: (0, i, 0))],
        out_specs=pl.BlockSpec((tr, cols), lambda i: (i, 0)), out_shape=jax.ShapeDtypeStruct((rows, cols), F32),
        compiler_params=_params(("parallel",)),
    )(parts)


def _adamw(w, g, m, v, name):
    rows, cols = w.shape
    tr = rows if rows <= 256 else 8 * next(d for d in range(32, 0, -1) if (rows // 8) % d == 0)

    def body(w_ref, g_ref, m_ref, v_ref, d_ref, mo_ref, vo_ref):
        gv = g_ref[...]
        mn = ADAM_B1 * m_ref[...] + (1.0 - ADAM_B1) * gv
        vn = ADAM_B2 * v_ref[...] + (1.0 - ADAM_B2) * jnp.square(gv)
        m_hat = mn / (1.0 - ADAM_B1 ** ADAM_STEP)
        v_hat = vn / (1.0 - ADAM_B2 ** ADAM_STEP)
        d_ref[...] = -ADAM_LR * (m_hat / (jnp.sqrt(v_hat) + ADAM_EPS) + ADAM_WD * w_ref[...])
        mo_ref[...] = mn
        vo_ref[...] = vn

    spec = pl.BlockSpec((tr, cols), lambda i: (i, 0))
    return pl.pallas_call(
        body, name=name, grid=(rows // tr,), in_specs=[spec] * 4, out_specs=[spec] * 3,
        out_shape=[jax.ShapeDtypeStruct((rows, cols), F32)] * 3, compiler_params=_params(("parallel",)),
    )(w, g, m, v)


ALL_FLIPS = tuple(range(1, N_DEV))


def _exchange(src, flips, mode, name):
    _, rows, cols = src.shape
    nslot = 2 if mode == "pair" else N_DEV
    nf = len(flips)

    def body(src_ref, dst_ref, send_sems, recv_sems, local_sem):
        x, y, c = lax.axis_index("x"), lax.axis_index("y"), lax.axis_index("c")
        me = 4 * x + 2 * y + c

        def slot(j):
            return (j & 1) if mode == "pair" else j

        own = pltpu.make_async_copy(src_ref.at[me if mode == "scatter" else 0], dst_ref.at[slot(me)], local_sem)
        own.start()
        copies = []
        for i, f in enumerate(flips):
            peer = me ^ f
            to = (1 - x if f & 4 else x, 1 - y if f & 2 else y, 1 - c if f & 1 else c)
            cp = pltpu.make_async_remote_copy(
                src_ref=src_ref.at[peer if mode == "scatter" else 0], dst_ref=dst_ref.at[slot(me)],
                send_sem=send_sems.at[i], recv_sem=recv_sems.at[i], device_id=to, device_id_type=pl.DeviceIdType.MESH)
            cp.start()
            copies.append(cp)
        for i, f in enumerate(flips):
            peer = me ^ f
            to = (1 - x if f & 4 else x, 1 - y if f & 2 else y, 1 - c if f & 1 else c)
            pltpu.make_async_remote_copy(
                src_ref=src_ref.at[0], dst_ref=dst_ref.at[slot(peer)], send_sem=send_sems.at[i],
                recv_sem=recv_sems.at[i], device_id=to, device_id_type=pl.DeviceIdType.MESH).wait_recv()
        for cp in copies:
            cp.wait_send()
        own.wait()

    return pl.pallas_call(
        body, name=name, out_shape=jax.ShapeDtypeStruct((nslot, rows, cols), src.dtype),
        in_specs=[pl.BlockSpec(memory_space=pl.ANY)], out_specs=pl.BlockSpec(memory_space=pl.ANY),
        scratch_shapes=[pltpu.SemaphoreType.DMA((nf,)), pltpu.SemaphoreType.DMA((nf,)), pltpu.SemaphoreType.DMA],
    )(src)


def _pad_cols(a, n):
    return a if n == 0 else jnp.pad(a, ((0, 0), (0, n)))


def _in_to_padded(w):
    return jnp.concatenate([_pad_cols(w[:, a:b], z) for a, b, z in IN_PIECES], axis=1)


def _in_from_padded(gp):
    pos, out = 0, {}
    for a, b, z in IN_PIECES:
        out[a] = gp[:, pos:pos + (b - a)]
        pos += (b - a) + z
    return jnp.concatenate([out[a] for a in sorted(out)], axis=1)


def _out_to_padded(w):
    z = jnp.zeros((64, w.shape[1]), w.dtype)
    return jnp.concatenate([w[0:384], w[384:704], z, w[704:1024], z], axis=0)


def _out_from_padded(gp):
    return jnp.concatenate([gp[0:384], gp[384:704], gp[768:1088]], axis=0)


def _uq_to_padded(w):
    parts = []
    for p in range(3):
        h0, h1 = 2 * p, 2 * p + 1
        parts += [w[:, 96 * h0:96 * h0 + 64], w[:, 96 * h1:96 * h1 + 64], w[:, 96 * h0 + 64:96 * h0 + 96],
                  w[:, 96 * h1 + 64:96 * h1 + 96], jnp.zeros((w.shape[0], 64), w.dtype)]
    return jnp.concatenate(parts, axis=1)


def _uq_from_padded(gp):
    parts = []
    for h in range(6):
        p, s = h // 2, h % 2
        parts += [gp[:, 256 * p + 64 * s:256 * p + 64 * s + 64], gp[:, 256 * p + 128 + 32 * s:256 * p + 160 + 32 * s]]
    return jnp.concatenate(parts, axis=1)


def _ukv_to_padded(w):
    return jnp.concatenate([w[:, 128 * h:128 * h + 64] for h in range(6)]
                           + [w[:, 128 * h + 64:128 * h + 128] for h in range(6)], axis=1)


def _ukv_from_padded(gp):
    parts = []
    for h in range(6):
        parts += [gp[:, 64 * h:64 * h + 64], gp[:, 384 + 64 * h:384 + 64 * h + 64]]
    return jnp.concatenate(parts, axis=1)


PACK_SIZES = (DEPTH * 256 * N_IN, DEPTH * 256 * D_MODEL, DEPTH * A_Q_RANK * 144, DEPTH * A_KV_RANK * 192)


def _pack(w_in, w_out, w_uq, w_ukv):
    flat = jnp.concatenate([w_in.reshape(-1), w_out.reshape(-1), w_uq.reshape(-1), w_ukv.reshape(-1)])
    flat = jnp.pad(flat, (0, PACK_ROWS * PACK_COLS - flat.shape[0]))
    return flat.reshape(PACK_ROWS, PACK_COLS)


def _unpack(packed):
    flat = packed.reshape(-1)
    o0, o1, o2, o3 = PACK_SIZES
    w_in = flat[0:o0].reshape(DEPTH, 256, N_IN)
    w_out = flat[o0:o0 + o1].reshape(DEPTH, 256, D_MODEL)
    w_uq = flat[o0 + o1:o0 + o1 + o2].reshape(DEPTH, A_Q_RANK, 144)
    w_ukv = flat[o0 + o1 + o2:o0 + o1 + o2 + o3].reshape(DEPTH, A_KV_RANK, 192)
    return w_in, w_out, w_uq, w_ukv


def _rope_tables(positions):
    inv = ROPE_THETA ** (-jnp.arange(0, A_ROPE, 2, dtype=F32) / A_ROPE)
    ang = positions.astype(F32)[..., None] * inv
    cos = jnp.tile(jnp.cos(ang), (1, 1, 4)).reshape(-1, 64)
    sin = jnp.tile(jnp.sin(ang), (1, 1, 4)).reshape(-1, 64)
    second = (jnp.arange(64) % 32) >= 16
    pad = ((0, 0), (0, 64))
    return (jnp.pad(cos, pad), jnp.pad(jnp.where(second, 0.0, -sin), pad), jnp.pad(jnp.where(second, sin, 0.0), pad))


def _rows(a, n):
    flat = a.reshape(-1)
    return jnp.pad(flat, (0, n * LANE - flat.shape[0])).reshape(n, LANE)


def _forward_backward(x, mod, tables, target, weights, small, nb, seq):
    cos, sina, sinb = tables
    saved = []
    for l in range(DEPTH):
        w, s = weights[l], small[l]
        shift = mod[l][:, None, 0:D_MODEL]
        scale = mod[l][:, None, D_MODEL:2 * D_MODEL]
        gate = mod[l][:, None, 2 * D_MODEL:]
        h, za, zg, qb, kb, vb, qc, kc, vc, zf = _ln_inproj(x, shift, scale, s["norm_g"], w["in"], seq)
        qa, ka, va = _a_up(za, s["gq"], s["gkv"], w["uq"], w["ukv"], cos, sina, sinb)
        bias = _bias_expand(s["rb8"])[0:6].reshape(6, BAND_TILES, T_BAND, T_BAND)
        f = _forget_fwd(zf, s["fb"], nb, seq)
        frow = jnp.pad(f[:, 0:5].reshape(nb, seq, 5).transpose(0, 2, 1), ((0, 0), (0, 1), (0, 0)))
        frow = frow.reshape(nb, 6, seq // T_CAUSAL, T_CAUSAL)
        oa, lse_a = _attn_fwd("A", qa, ka, va, None, nb, seq)
        ob, lse_b = _attn_fwd("B", qb, kb, vb, bias, nb, seq)
        oc, lse_c = _attn_fwd("C", qc, kc, vc, frow, nb, seq)
        y, xn = _gate_outproj(x, gate, oa, ob, oc, zg, w["out"], seq)
        saved.append(dict(x=x, h=h, za=za, zg=zg, zf=zf, y=y, shift=shift, scale=scale, gate=gate, bias=bias, frow=frow,
                          a=(qa, ka, va, oa, lse_a), b=(qb, kb, vb, ob, lse_b), c=(qc, kc, vc, oc, lse_c)))
        x = xn
    dx, loss8, gfinal8 = _final_loss(x, target, small[0]["final_g"])
    grads = []
    for l in reversed(range(DEPTH)):
        w, s, sv = weights[l], small[l], saved[l]
        qa, ka, va, oa, lse_a = sv["a"]
        qb, kb, vb, ob, lse_b = sv["b"]
        qc, kc, vc, oc, lse_c = sv["c"]
        doa, dob, doc, dzg, gw_out, dgate = _outproj_bwd(dx, sv["y"], sv["gate"], oa, ob, oc, sv["zg"], w["out"],
                                                          w["out_t"], nb, seq)
        dqa, dka, dva = _attn_bwd("A", qa, ka, va, oa, doa, lse_a, None, nb, seq)
        dqb, dkb, dvb, dbt = _attn_bwd("B", qb, kb, vb, ob, dob, lse_b, sv["bias"], nb, seq)
        dqc, dkc, dvc, dfr, dfq = _attn_bwd("C", qc, kc, vc, oc, doc, lse_c, sv["frow"], nb, seq)
        grb = _bias_reduce(jnp.pad(dbt.reshape(6, BIAS_FLAT), ((0, 2), (0, 0))))
        df = dfr.reshape(nb, 6, seq).transpose(0, 2, 1).reshape(nb * seq, 6) + dfq.reshape(nb * seq, 6, 64)[:, :, 0]
        dzf, gfb = _forget_bwd(jnp.pad(df, ((0, 0), (0, LANE - 6))), sv["zf"], s["fb"], nb, seq)
        dza, gw_uq, gw_ukv, ggq, ggkv = _a_up_bwd(dqa, dka, dva, sv["za"], s["gq"], s["gkv"], w["uq_t"], w["ukv_t"],
                                                  cos, sina, sinb)
        dz = (dza, dzg, dqb, dkb, dvb, dqc, dkc, dvc, dzf)
        dx, dshift, dscale, gnorm = _inproj_bwd_dx(dz, dx, sv["x"], sv["shift"], sv["scale"], s["norm_g"], w["in_t"],
                                                   nb, seq)
        gw_in = jnp.concatenate([_inproj_bwd_dw(sv["h"], dz[0:2], "inproj_bwd_dw0"),
                                 _inproj_bwd_dw(sv["h"], dz[2:], "inproj_bwd_dw1")], axis=1)
        dmod = jnp.concatenate([dshift[:, 0], dscale[:, 0], dgate[:, 0]], axis=1)
        grads.append(dict(w_in=gw_in, w_out=gw_out, w_uq=gw_uq, w_ukv=gw_ukv, dmod=dmod, norm_g=gnorm[0], gq=ggq[0],
                          gkv=ggkv[0], rb8=grb, fb=gfb[0]))
    grads.reverse()
    return loss8[0, 0], dx, grads, gfinal8[0]


def _layer_weights(w_in, w_out, w_uq, w_ukv):
    wi, wo, wq, wkv = _in_to_padded(w_in), _out_to_padded(w_out), _uq_to_padded(w_uq), _ukv_to_padded(w_ukv)
    return {"in": wi, "in_t": wi.T, "out": wo, "out_t": wo.T, "uq": wq, "uq_t": wq.T, "ukv": wkv, "ukv_t": wkv.T}


def _layer_small(norm_g, gq, gkv, rel_bias, forget_b, final_g):
    rb8 = jnp.pad(rel_bias, ((0, 3), (0, 384 - N_REL)))
    fb = jnp.pad(forget_b, (0, LANE - 5)).reshape(1, LANE)
    return dict(norm_g=norm_g.reshape(1, -1), gq=gq.reshape(1, -1), gkv=gkv.reshape(1, -1), rb8=rb8, fb=fb,
                final_g=final_g.reshape(1, -1))


def _small_payload(per_layer, final_g, loss):
    def stack(key):
        return jnp.stack([p[key] for p in per_layer])

    def rows(a, rng):
        return _rows(a, rng[1] - rng[0])

    dmod = stack("dmod") if "dmod" in per_layer[0] else jnp.zeros((LANE,), F32)
    parts = [rows(dmod, PAY_DMOD), rows(stack("norm_g"), PAY_NORM), rows(stack("gq"), PAY_GQ),
             rows(stack("gkv"), PAY_GKV), rows(stack("rb8"), PAY_RB), rows(stack("fb"), PAY_FB),
             rows(final_g, PAY_FINAL), rows(loss, PAY_LOSS)]
    return jnp.concatenate(parts, axis=0)


def _payload_split(pay):
    def take(rng, shape):
        n = 1
        for d in shape:
            n *= d
        return pay[rng[0]:rng[1]].reshape(-1)[0:n].reshape(shape)

    norm_g = take(PAY_NORM, (DEPTH, D_MODEL))
    gq = take(PAY_GQ, (DEPTH, A_Q_RANK))
    gkv = take(PAY_GKV, (DEPTH, A_KV_RANK))
    rb = take(PAY_RB, (DEPTH, 8, 384))[:, 0:5, 0:N_REL]
    fb = take(PAY_FB, (DEPTH, LANE))[:, 0:5]
    final_g = take(PAY_FINAL, (D_MODEL,))
    return norm_g, gq, gkv, rb, fb, final_g


def kernel(x, c, positions, w_ada, b_ada, norm_g, w_in, a_q_norm_g, a_w_uq, a_kv_norm_g, a_w_ukv, b_rel_bias, c_forget_b, w_out, final_g, loss_target, m_w_ada, m_b_ada, m_norm_g, m_w_in, m_a_q_norm_g, m_a_w_uq, m_a_kv_norm_g, m_a_w_ukv, m_b_rel_bias, m_c_forget_b, m_w_out, m_final_g, v_w_ada, v_b_ada, v_norm_g, v_w_in, v_a_q_norm_g, v_a_w_uq, v_a_kv_norm_g, v_a_w_ukv, v_b_rel_bias, v_c_forget_b, v_w_out, v_final_g):
    nb, seq, _ = x.shape
    ix, iy, ic = lax.axis_index("x"), lax.axis_index("y"), lax.axis_index("c")
    chip = 2 * ix + iy
    me = 2 * chip + ic

    packed_w = _pack(w_in, w_out, a_w_uq, a_w_ukv)
    half = lax.dynamic_slice_in_dim(packed_w, ic * HALF_ROWS, HALF_ROWS, axis=0).astype(BF16)
    gathered = _exchange(half[None], ALL_FLIPS, "gather", "gather_weights").reshape(4, PACK_ROWS, PACK_COLS)
    shards = [_unpack(gathered[j]) for j in range(4)]
    full_in = jnp.concatenate([s[0] for s in shards], axis=1)
    full_out = jnp.concatenate([s[1] for s in shards], axis=1)
    full_uq = jnp.concatenate([s[2] for s in shards], axis=2)
    full_ukv = jnp.concatenate([s[3] for s in shards], axis=2)
    weights = [_layer_weights(full_in[l], full_out[l], full_uq[l], full_ukv[l]) for l in range(DEPTH)]
    small = [_layer_small(norm_g[l], a_q_norm_g[l], a_kv_norm_g[l], b_rel_bias[l], c_forget_b[l], final_g)
             for l in range(DEPTH)]

    c_all = _exchange(c[None], ALL_FLIPS, "gather", "gather_c").reshape(N_DEV * nb, D_MODEL)
    cols = w_ada.shape[2]
    b_cols = lax.dynamic_slice_in_dim(b_ada, chip * cols, cols, axis=1)[:, None, :]
    mod_cols = _ada_fwd(c_all, w_ada, b_cols)
    mod_g = _exchange(mod_cols.reshape(1, DEPTH * N_DEV * nb, cols), ALL_FLIPS, "gather", "gather_mod")
    mod_all = jnp.concatenate([mod_g[2 * j].reshape(DEPTH, N_DEV * nb, cols) for j in range(4)], axis=2)
    mod = lax.dynamic_slice_in_dim(mod_all, me * nb, nb, axis=1)

    tables = _rope_tables(positions)
    loss_part, dx, grads, gfinal = _forward_backward(
        x.reshape(nb * seq, D_MODEL), mod, tables, loss_target.reshape(nb * seq, D_MODEL), weights, small, nb, seq)

    pay = _small_payload(grads, gfinal, loss_part)
    pay_all = _exchange(pay[None], ALL_FLIPS, "gather", "gather_small")
    tot = _sum_blocks(pay_all, "sum_small")
    loss = tot[PAY_LOSS[0], 0]
    dmod_all = pay_all[:, PAY_DMOD[0]:PAY_DMOD[1]].reshape(N_DEV, -1)[:, 0:DEPTH * nb * 3 * D_MODEL]
    dmod_all = dmod_all.reshape(N_DEV, DEPTH, nb, 3 * D_MODEL).transpose(1, 0, 2, 3)
    dmod_all = dmod_all.reshape(DEPTH, N_DEV * nb, 3 * D_MODEL)
    my_cols = lax.dynamic_slice_in_dim(dmod_all, chip * cols, cols, axis=2)
    g_w_ada, g_b_ada = _ada_bwd(c_all, my_cols, dmod_all)
    g_b_ada = g_b_ada[:, 0]

    g_in = jnp.stack([_in_from_padded(g["w_in"]) for g in grads])
    g_out = jnp.stack([_out_from_padded(g["w_out"]) for g in grads])
    g_uq = jnp.stack([_uq_from_padded(g["w_uq"]) for g in grads])
    g_ukv = jnp.stack([_ukv_from_padded(g["w_ukv"]) for g in grads])
    per_chip = jnp.stack([_pack(g_in[:, 256 * j:256 * (j + 1)], g_out[:, 256 * j:256 * (j + 1)],
                                g_uq[:, :, 144 * j:144 * (j + 1)], g_ukv[:, :, 192 * j:192 * (j + 1)])
                          for j in range(4)])
    parts = _exchange(per_chip.reshape(N_DEV, HALF_ROWS, PACK_COLS), ALL_FLIPS, "scatter", "scatter_grads")
    reduced = _sum_blocks(parts, "sum_grads")
    g_packed = _exchange(reduced[None], (1,), "pair", "pair_grads").reshape(PACK_ROWS, PACK_COLS)

    d_p, m_p, v_p = _adamw(packed_w, g_packed, _pack(m_w_in, m_w_out, m_a_w_uq, m_a_w_ukv),
                           _pack(v_w_in, v_w_out, v_a_w_uq, v_a_w_ukv), "adamw_packed")
    gw = _unpack(g_packed)
    dw = _unpack(d_p)
    mw = _unpack(m_p)
    vw = _unpack(v_p)
    ada_shape = w_ada.shape
    flat_ada = (DEPTH * D_MODEL, cols)
    d_ada, m_ada, v_ada = (a.reshape(ada_shape) for a in _adamw(
        w_ada.reshape(flat_ada), g_w_ada.reshape(flat_ada), m_w_ada.reshape(flat_ada), v_w_ada.reshape(flat_ada),
        "adamw_ada"))
    d_b, m_b, v_b = (a.reshape(DEPTH, 3 * D_MODEL) for a in _adamw(
        _rows(b_ada, 48), _rows(g_b_ada, 48), _rows(m_b_ada, 48), _rows(v_b_ada, 48), "adamw_b_ada"))

    def small_rows(ng, gq, gkv, rb, fb, fg):
        per_layer = [dict(norm_g=ng[l], gq=gq[l], gkv=gkv[l], rb8=jnp.pad(rb[l], ((0, 3), (0, 384 - N_REL))),
                          fb=jnp.pad(fb[l], (0, LANE - 5))) for l in range(DEPTH)]
        return _small_payload(per_layer, fg, jnp.zeros((), F32))

    w_s = small_rows(norm_g, a_q_norm_g, a_kv_norm_g, b_rel_bias, c_forget_b, final_g)
    m_s = small_rows(m_norm_g, m_a_q_norm_g, m_a_kv_norm_g, m_b_rel_bias, m_c_forget_b, m_final_g)
    v_s = small_rows(v_norm_g, v_a_q_norm_g, v_a_kv_norm_g, v_b_rel_bias, v_c_forget_b, v_final_g)
    d_s, mo_s, vo_s = _adamw(w_s, tot, m_s, v_s, "adamw_small")
    gs = _payload_split(tot)
    ds = _payload_split(d_s)
    ms = _payload_split(mo_s)
    vs = _payload_split(vo_s)

    def ordered(ada, b, sm, big):
        ng, gq, gkv, rb, fb, fg = sm
        b_in, b_out, b_uq, b_ukv = big
        return (ada, b, ng, b_in, gq, b_uq, gkv, b_ukv, rb, fb, b_out, fg)

    return (loss, dx.reshape(nb, seq, D_MODEL), *ordered(g_w_ada, g_b_ada, gs, gw), *ordered(d_ada, d_b, ds, dw),
            *ordered(m_ada, m_b, ms, mw), *ordered(v_ada, v_b, vs, vw))
```

```python
import functools

import jax
import jax.numpy as jnp
from jax import lax
from jax.experimental import pallas as pl
from jax.experimental.pallas import tpu as pltpu

F32 = jnp.float32
BF16 = jnp.bfloat16

D_MODEL = 1024
DEPTH = 2
EPS = 1e-6
NEG = -1e30
ROPE_THETA = 10000.0
A_ROPE = 32
A_Q_RANK = 384
A_KV_RANK = 256
REL_CLIP = 128
N_REL = 2 * REL_CLIP + 1
N_IN = 3621

ADAM_LR = 0.001
ADAM_B1 = 0.9
ADAM_B2 = 0.999
ADAM_EPS = 1e-08
ADAM_WD = 0.01
ADAM_STEP = 10

LANE = 128
VMEM_LIMIT = 56 * 1024 * 1024

NP_IN = 4352
Z_A = (0, 768)
Z_G = (768, 1920)
Z_QKV = tuple((1920 + 384 * i, 1920 + 384 * (i + 1)) for i in range(6))
Z_F = (4224, 4352)
IN_PIECES = ((0, 672, 96), (672, 1056, 0), (2016, 2336, 64), (3301, 3621, 64), (1056, 1376, 64), (1376, 1696, 64),
             (1696, 2016, 64), (2336, 2656, 64), (2656, 2976, 64), (2976, 3296, 64), (3296, 3301, 123))
D_CAT = 1152

TM = 256
T_CAUSAL = 256
T_BAND = 128
BAND_TILES = 5
N_DEV = 8

PAY_DMOD = (0, 96)
PAY_NORM = (96, 112)
PAY_GQ = (112, 120)
PAY_GKV = (120, 128)
PAY_RB = (128, 176)
PAY_FB = (176, 184)
PAY_FINAL = (184, 192)
PAY_LOSS = (192, 200)
PAY_ROWS = 200

PACK_COLS = 1024
PACK_ROWS = 2560
HALF_ROWS = PACK_ROWS // 2


def _params(sem=None):
    return pltpu.CompilerParams(dimension_semantics=sem, vmem_limit_bytes=VMEM_LIMIT)


def _lane_iota(shape):
    return lax.broadcasted_iota(jnp.int32, shape, len(shape) - 1)


def _silu(u):
    return u * jax.nn.sigmoid(u)


def _dsilu(u):
    s = jax.nn.sigmoid(u)
    return s * (1.0 + u * (1.0 - s))


def _rms(x, g):
    r = lax.rsqrt(jnp.mean(x * x, axis=-1, keepdims=True) + EPS)
    xh = x * r
    return xh * g, xh, r


def _rms_bwd(dy, xh, r, g):
    dxh = dy * g
    return r * (dxh - xh * jnp.mean(dxh * xh, axis=-1, keepdims=True))


def _rope(x, cos, sina, sinb):
    return x * cos + pltpu.roll(x, 16, 1) * sinb + pltpu.roll(x, LANE - 16, 1) * sina


def _rope_t(dy, cos, sina, sinb):
    return dy * cos + pltpu.roll(dy * sinb, LANE - 16, 1) + pltpu.roll(dy * sina, 16, 1)


def _split3(x):
    hi = x.astype(BF16)
    r1 = x - hi.astype(F32)
    mid = r1.astype(BF16)
    lo = (r1 - mid.astype(F32)).astype(BF16)
    return hi, mid, lo


def _dot(a, b):
    return jnp.dot(a, b, preferred_element_type=F32)


def _dot_nt(a, b):
    return lax.dot_general(a, b, (((1,), (1,)), ((), ())), preferred_element_type=F32)


def _dot_tn(a, b):
    return lax.dot_general(a, b, (((0,), (0,)), ((), ())), preferred_element_type=F32)


def _row_spec(cols):
    return pl.BlockSpec((TM, cols), lambda i: (i, 0))


def _full_spec(shape):
    return pl.BlockSpec(shape, lambda i: (0,) * len(shape))


def _ex_spec(tiles_per_ex):
    return pl.BlockSpec((1, 1, D_MODEL), lambda i: (i // tiles_per_ex, 0, 0))


def _ln_inproj(x, shift, scale, g, w_in_p, seq):
    t = x.shape[0]

    def body(x_ref, sh_ref, sc_ref, g_ref, w_ref, h_ref, za_ref, zg_ref, q0, q1, q2, q3, q4, q5, zf_ref):
        n, _, _ = _rms(x_ref[...], g_ref[...])
        h = (n * (1.0 + sc_ref[0]) + sh_ref[0]).astype(BF16)
        h_ref[...] = h
        za_ref[...] = _dot(h, w_ref[:, Z_A[0]:Z_A[1]])
        zg_ref[...] = _dot(h, w_ref[:, Z_G[0]:Z_G[1]])
        for ref, (c0, c1) in zip((q0, q1, q2, q3, q4, q5), Z_QKV):
            ref[...] = _dot(h, w_ref[:, c0:c1]).astype(BF16)
        zf_ref[...] = _dot(h, w_ref[:, Z_F[0]:Z_F[1]])

    tpe = seq // TM
    shapes = [jax.ShapeDtypeStruct((t, D_MODEL), BF16), jax.ShapeDtypeStruct((t, 768), F32),
              jax.ShapeDtypeStruct((t, D_CAT), F32)]
    shapes += [jax.ShapeDtypeStruct((t, 384), BF16)] * 6 + [jax.ShapeDtypeStruct((t, LANE), F32)]
    return pl.pallas_call(
        body, name="ln_inproj", grid=(t // TM,),
        in_specs=[_row_spec(D_MODEL), _ex_spec(tpe), _ex_spec(tpe), _full_spec((1, D_MODEL)),
                  _full_spec((D_MODEL, NP_IN))],
        out_specs=[_row_spec(D_MODEL), _row_spec(768), _row_spec(D_CAT)] + [_row_spec(384)] * 6 + [_row_spec(LANE)],
        out_shape=shapes, compiler_params=_params(("parallel",)),
    )(x, shift, scale, g, w_in_p)


def _a_up(za, gq, gkv, w_uq_p, w_ukv_p, cos, sina, sinb):
    t = za.shape[0]

    def body(za_ref, gq_ref, gkv_ref, wq_ref, wkv_ref, cos_ref, sa_ref, sb_ref, q_ref, k_ref, v_ref):
        cos_t, sa, sb = cos_ref[...], sa_ref[...], sb_ref[...]
        cqn, _, _ = _rms(za_ref[:, 0:384], gq_ref[...])
        q = _dot(cqn.astype(BF16), wq_ref[...])
        ckvn, _, _ = _rms(za_ref[:, 384:640], gkv_ref[...])
        kv = _dot(ckvn.astype(BF16), wkv_ref[...])
        kpe = za_ref[:, 640:768]
        kpe = _rope(kpe + pltpu.roll(kpe, 32, 1), cos_t, sa, sb).astype(BF16)
        for p in range(3):
            q_ref[:, 256 * p:256 * p + 128] = q[:, 256 * p:256 * p + 128].astype(BF16)
            q_ref[:, 256 * p + 128:256 * p + 256] = _rope(q[:, 256 * p + 128:256 * p + 256], cos_t, sa, sb).astype(BF16)
            k_ref[:, 256 * p:256 * p + 128] = kv[:, 128 * p:128 * p + 128].astype(BF16)
            k_ref[:, 256 * p + 128:256 * p + 256] = kpe
        v_ref[...] = kv[:, 384:768].astype(BF16)

    return pl.pallas_call(
        body, name="a_up", grid=(t // TM,),
        in_specs=[_row_spec(768), _full_spec((1, 384)), _full_spec((1, 256)), _full_spec((384, 768)),
                  _full_spec((256, 768)), _row_spec(LANE), _row_spec(LANE), _row_spec(LANE)],
        out_specs=[_row_spec(768), _row_spec(768), _row_spec(384)],
        out_shape=[jax.ShapeDtypeStruct((t, 768), BF16), jax.ShapeDtypeStruct((t, 768), BF16),
                   jax.ShapeDtypeStruct((t, 384), BF16)],
        compiler_params=_params(("parallel",)),
    )(za, gq, gkv, w_uq_p, w_ukv_p, cos, sina, sinb)


def _tri(n, upper):
    r = lax.broadcasted_iota(jnp.int32, (n, n), 0)
    c = lax.broadcasted_iota(jnp.int32, (n, n), 1)
    return jnp.where((c >= r) if upper else (c <= r), 1.0, 0.0).astype(BF16)


def _forget_fwd(zf, fb, nb, seq):
    blk = 256

    def body(zf_ref, fb_ref, f_ref):
        tri = _tri(blk, False)
        live = _lane_iota((blk, LANE)) < 5
        carry = jnp.zeros((1, LANE), F32)
        for i in range(seq // blk):
            u = zf_ref[i * blk:(i + 1) * blk, :] + fb_ref[...]
            lf = jnp.where(live, jnp.minimum(u, 0.0) - jnp.log(1.0 + jnp.exp(-jnp.abs(u))), 0.0)
            hi, mid, lo = _split3(lf)
            f_ref[i * blk:(i + 1) * blk, :] = _dot(tri, hi) + _dot(tri, mid) + _dot(tri, lo) + carry
            carry = carry + jnp.sum(lf, axis=0, keepdims=True)

    return pl.pallas_call(
        body, name="forget_fwd", grid=(nb,),
        in_specs=[pl.BlockSpec((seq, LANE), lambda b: (b, 0)), pl.BlockSpec((1, LANE), lambda b: (0, 0))],
        out_specs=pl.BlockSpec((seq, LANE), lambda b: (b, 0)),
        out_shape=jax.ShapeDtypeStruct((nb * seq, LANE), F32), compiler_params=_params(("parallel",)),
    )(zf, fb)


def _forget_bwd(df, zf, fb, nb, seq):
    blk = 256

    def body(df_ref, zf_ref, fb_ref, dz_ref, gb_ref):
        @pl.when(pl.program_id(0) == 0)
        def _():
            gb_ref[...] = jnp.zeros_like(gb_ref)

        tri = _tri(blk, True)
        live = _lane_iota((blk, LANE)) < 5
        carry = jnp.zeros((1, LANE), F32)
        gsum = jnp.zeros((1, LANE), F32)
        for i in reversed(range(seq // blk)):
            d = df_ref[i * blk:(i + 1) * blk, :]
            hi, mid, lo = _split3(d)
            dlf = _dot(tri, hi) + _dot(tri, mid) + _dot(tri, lo) + carry
            carry = carry + jnp.sum(d, axis=0, keepdims=True)
            u = zf_ref[i * blk:(i + 1) * blk, :] + fb_ref[...]
            du = jnp.where(live, dlf * jax.nn.sigmoid(-u), 0.0)
            dz_ref[i * blk:(i + 1) * blk, :] = du.astype(BF16)
            gsum = gsum + jnp.sum(du, axis=0, keepdims=True)
        gb_ref[...] += jnp.broadcast_to(gsum, gb_ref.shape)

    return pl.pallas_call(
        body, name="forget_bwd", grid=(nb,),
        in_specs=[pl.BlockSpec((seq, LANE), lambda b: (b, 0)), pl.BlockSpec((seq, LANE), lambda b: (b, 0)),
                  pl.BlockSpec((1, LANE), lambda b: (0, 0))],
        out_specs=[pl.BlockSpec((seq, LANE), lambda b: (b, 0)), pl.BlockSpec((8, LANE), lambda b: (0, 0))],
        out_shape=[jax.ShapeDtypeStruct((nb * seq, LANE), BF16), jax.ShapeDtypeStruct((8, LANE), F32)],
        compiler_params=_params(("arbitrary",)),
    )(df, zf, fb)


def _gate_outproj(x, gate, oa, ob, oc, zg, w_out_p, seq):
    t = x.shape[0]

    def body(x_ref, gate_ref, oa_ref, ob_ref, oc_ref, zg_ref, w_ref, y_ref, xn_ref):
        y = jnp.zeros((TM, D_MODEL), F32)
        for i, o_ref in enumerate((oa_ref, ob_ref, oc_ref)):
            cat = (o_ref[...] * _silu(zg_ref[:, 384 * i:384 * (i + 1)])).astype(BF16)
            y = y + _dot(cat, w_ref[384 * i:384 * (i + 1), :])
        y_ref[...] = y
        xn_ref[...] = x_ref[...] + gate_ref[0] * y

    return pl.pallas_call(
        body, name="gate_outproj", grid=(t // TM,),
        in_specs=[_row_spec(D_MODEL), _ex_spec(seq // TM), _row_spec(384), _row_spec(384), _row_spec(384),
                  _row_spec(D_CAT), _full_spec((D_CAT, D_MODEL))],
        out_specs=[_row_spec(D_MODEL), _row_spec(D_MODEL)],
        out_shape=[jax.ShapeDtypeStruct((t, D_MODEL), F32)] * 2, compiler_params=_params(("parallel",)),
    )(x, gate, oa, ob, oc, zg, w_out_p)


def _final_loss(x, target, g):
    t = x.shape[0]

    def body(x_ref, t_ref, g_ref, dx_ref, loss_ref, gg_ref):
        @pl.when(pl.program_id(0) == 0)
        def _():
            loss_ref[...] = jnp.zeros_like(loss_ref)
            gg_ref[...] = jnp.zeros_like(gg_ref)

        gv = g_ref[...]
        out, xh, r = _rms(x_ref[...], gv)
        err = out - t_ref[...]
        loss_ref[...] += 0.5 * jnp.sum(jnp.mean(err * err, axis=-1, keepdims=True), axis=0, keepdims=True)
        dout = err / D_MODEL
        gg_ref[...] += jnp.broadcast_to(jnp.sum(dout * xh, axis=0, keepdims=True), gg_ref.shape)
        dx_ref[...] = _rms_bwd(dout, xh, r, gv)

    return pl.pallas_call(
        body, name="final_loss", grid=(t // TM,),
        in_specs=[_row_spec(D_MODEL), _row_spec(D_MODEL), _full_spec((1, D_MODEL))],
        out_specs=[_row_spec(D_MODEL), _full_spec((8, LANE)), _full_spec((8, D_MODEL))],
        out_shape=[jax.ShapeDtypeStruct((t, D_MODEL), F32), jax.ShapeDtypeStruct((8, LANE), F32),
                   jax.ShapeDtypeStruct((8, D_MODEL), F32)],
        compiler_params=_params(("arbitrary",)),
    )(x, target, g)


def _head_masks(kind, rows, dq, h):
    lq = _lane_iota((rows, dq))
    lv = _lane_iota((rows, LANE))
    mq = (lq >= 64 * h) & (lq < 64 * h + 64)
    if kind == "A":
        mq = mq | ((lq >= 128 + 32 * h) & (lq < 160 + 32 * h))
    return mq, (lv >= 64 * h) & (lv < 64 * h + 64)


def _tile_mask(kind, tile, m=None):
    row = lax.broadcasted_iota(jnp.int32, (tile, tile), 0)
    col = lax.broadcasted_iota(jnp.int32, (tile, tile), 1)
    if kind == "A":
        return (col >> 6) <= (row >> 6)
    if kind == "C":
        return col <= row
    first = (m == 0) & (row >= 64) & (col < 64)
    last = (m == BAND_TILES - 1) & (row < 64) & (col >= 64)
    return jnp.logical_not(first | last)


def _attn_scale(kind):
    return 96.0 ** -0.5 if kind == "A" else 0.125


def _attn_fwd(kind, q, k, v, aux, nb, seq):
    dq = q.shape[1] // 3
    tile = T_BAND if kind == "B" else T_CAUSAL
    nq = seq // tile
    scale = _attn_scale(kind)

    def body(*refs):
        if kind == "A":
            q_ref, k_ref, v_ref, o_ref, lse_ref = refs
            aux_ref = None
        else:
            q_ref, k_ref, v_ref, aux_ref, o_ref, lse_ref = refs
        qi = pl.program_id(2)
        q2 = q_ref[...]
        res = []
        for h in range(2):
            mq, _ = _head_masks(kind, tile, dq, h)
            qh = jnp.where(mq, q2, jnp.zeros_like(q2))

            def step(kj, carry, m=None, diag=False):
                mx, l, acc = carry
                ks = pl.ds(pl.multiple_of(kj * tile, tile), tile)
                kt = k_ref[ks, :]
                vt = v_ref[ks, :]
                s = _dot_nt(qh, kt) * scale
                if kind == "B":
                    s = jnp.where(_tile_mask("B", tile, m), s + aux_ref[h, m], NEG)
                if kind == "C":
                    s = s - aux_ref[0, h, pl.ds(kj, 1), :]
                if diag:
                    s = jnp.where(_tile_mask(kind, tile), s, NEG)
                mn = jnp.maximum(mx, jnp.max(s, axis=-1, keepdims=True))
                alpha = jnp.exp(mx - mn)
                p = jnp.exp(s - mn)
                l = alpha * l + jnp.sum(p, axis=-1, keepdims=True)
                acc = alpha * acc + _dot(p.astype(BF16), vt)
                return mn, l, acc

            init = (jnp.full((tile, 1), NEG, F32), jnp.zeros((tile, 1), F32), jnp.zeros((tile, LANE), F32))
            if kind == "B":
                m0 = jnp.maximum(BAND_TILES - 1 - qi, 0)
                mx, l, acc = lax.fori_loop(m0, BAND_TILES, lambda m, c: step(qi - (BAND_TILES - 1) + m, c, m=m), init)
            else:
                carry = lax.fori_loop(0, qi, lambda kj, c: step(kj, c), init)
                mx, l, acc = step(qi, carry, diag=True)
            res.append((acc / l, mx + jnp.log(l)))
        first = _lane_iota((tile, LANE)) < 64
        o_ref[...] = jnp.where(first, res[0][0], res[1][0])
        lse_ref[...] = jnp.where(first, res[0][1], res[1][1])

    in_specs = [pl.BlockSpec((tile, dq), lambda b, p, i: (b * nq + i, p)),
                pl.BlockSpec((seq, dq), lambda b, p, i: (b, p)),
                pl.BlockSpec((seq, LANE), lambda b, p, i: (b, p))]
    args = [q, k, v]
    if kind == "B":
        in_specs.append(pl.BlockSpec((2, BAND_TILES, tile, tile), lambda b, p, i: (p, 0, 0, 0)))
        args.append(aux)
    if kind == "C":
        in_specs.append(pl.BlockSpec((1, 2, nq, tile), lambda b, p, i: (b, p, 0, 0)))
        args.append(aux)
    out_spec = pl.BlockSpec((tile, LANE), lambda b, p, i: (b * nq + i, p))
    return pl.pallas_call(
        body, name="attn_fwd_" + kind, grid=(nb, 3, nq), in_specs=in_specs, out_specs=[out_spec, out_spec],
        out_shape=[jax.ShapeDtypeStruct((nb * seq, 384), F32)] * 2,
        compiler_params=_params(("parallel", "parallel", "parallel")),
    )(*args)


def _attn_bwd(kind, q, k, v, o, do, lse, aux, nb, seq):
    dq = q.shape[1] // 3
    tile = T_BAND if kind == "B" else T_CAUSAL
    nq = seq // tile
    scale = _attn_scale(kind)
    dqk_dtype = F32 if kind == "A" else BF16

    def body(*refs):
        dfr_ref = dfq_ref = dbt_ref = aux_ref = None
        if kind == "A":
            q_ref, k_ref, v_ref, o_ref, do_ref, lse_ref, dq_ref, dk_ref, dv_ref, dk_acc, dv_acc = refs
        elif kind == "B":
            q_ref, k_ref, v_ref, o_ref, do_ref, lse_ref, aux_ref, dq_ref, dk_ref, dv_ref, dbt_ref, dk_acc, dv_acc = refs
        else:
            (q_ref, k_ref, v_ref, o_ref, do_ref, lse_ref, aux_ref, dq_ref, dk_ref, dv_ref, dfr_ref, dfq_ref,
             dk_acc, dv_acc) = refs
        dk_acc[...] = jnp.zeros_like(dk_acc)
        dv_acc[...] = jnp.zeros_like(dv_acc)
        if kind == "C":
            dfr_ref[...] = jnp.zeros_like(dfr_ref)
        if kind == "B":
            @pl.when(pl.program_id(1) == 0)
            def _():
                dbt_ref[...] = jnp.zeros_like(dbt_ref)

        def q_step(qi, _):
            qs = pl.ds(pl.multiple_of(qi * tile, tile), tile)
            q2 = q_ref[qs, :]
            do2 = do_ref[qs, :]
            o2 = o_ref[qs, :]
            lse2 = lse_ref[qs, :]
            dq_tot = jnp.zeros((tile, dq), F32)
            row_sums = []
            for h in range(2):
                mq, mv = _head_masks(kind, tile, dq, h)
                qh = jnp.where(mq, q2, jnp.zeros_like(q2))
                doh = jnp.where(mv, do2, 0.0)
                dob = doh.astype(BF16)
                delta = jnp.sum(doh * o2, axis=-1, keepdims=True)
                lseh = jnp.max(jnp.where(mv, lse2, NEG), axis=-1, keepdims=True)

                def step(kj, carry, m=None, diag=False):
                    dq_t, rs = carry
                    ks = pl.ds(pl.multiple_of(kj * tile, tile), tile)
                    kt = k_ref[ks, :]
                    vt = v_ref[ks, :]
                    s = _dot_nt(qh, kt) * scale
                    if kind == "B":
                        s = jnp.where(_tile_mask("B", tile, m), s + aux_ref[h, m], NEG)
                    if kind == "C":
                        s = s - aux_ref[0, h, pl.ds(kj, 1), :]
                    if diag:
                        s = jnp.where(_tile_mask(kind, tile), s, NEG)
                    p = jnp.exp(s - lseh)
                    ds = p * (_dot_nt(dob, vt) - delta)
                    if kind == "B":
                        dbt_ref[h, m] += ds
                    if kind == "C":
                        dfr_ref[0, h, pl.ds(kj, 1), :] -= jnp.sum(ds, axis=0, keepdims=True)
                        rs = rs + jnp.sum(ds, axis=-1, keepdims=True)
                    dss = (ds * scale).astype(BF16)
                    dv_acc[ks, :] += _dot_tn(p.astype(BF16), dob)
                    dk_acc[ks, :] += _dot_tn(dss, qh)
                    return dq_t + jnp.where(mq, _dot(dss, kt), 0.0), rs

                zero = (jnp.zeros((tile, dq), F32), jnp.zeros((tile, 1), F32))
                if kind == "B":
                    m0 = jnp.maximum(BAND_TILES - 1 - qi, 0)
                    dq_h, rs_h = lax.fori_loop(m0, BAND_TILES, lambda m, c: step(qi - (BAND_TILES - 1) + m, c, m=m), zero)
                else:
                    carry = lax.fori_loop(0, qi, lambda kj, c: step(kj, c), zero)
                    dq_h, rs_h = step(qi, carry, diag=True)
                dq_tot = dq_tot + dq_h
                row_sums.append(rs_h)
            dq_ref[qs, :] = dq_tot.astype(dqk_dtype)
            if kind == "C":
                dfq_ref[qs, :] = jnp.where(_lane_iota((tile, LANE)) < 64, row_sums[0], row_sums[1])
            return 0

        lax.fori_loop(0, nq, q_step, 0)
        dk_ref[...] = dk_acc[...].astype(dqk_dtype)
        dv_ref[...] = dv_acc[...].astype(BF16)

    def seq_spec(cols):
        return pl.BlockSpec((seq, cols), lambda p, b: (b, p))

    in_specs = [seq_spec(dq), seq_spec(dq), seq_spec(LANE), seq_spec(LANE), seq_spec(LANE), seq_spec(LANE)]
    args = [q, k, v, o, do, lse]
    out_specs = [seq_spec(dq), seq_spec(dq), seq_spec(LANE)]
    out_shape = [jax.ShapeDtypeStruct((nb * seq, 3 * dq), dqk_dtype)] * 2 + [jax.ShapeDtypeStruct((nb * seq, 384), BF16)]
    if kind == "B":
        spec = pl.BlockSpec((2, BAND_TILES, tile, tile), lambda p, b: (p, 0, 0, 0))
        in_specs.append(spec)
        args.append(aux)
        out_specs.append(spec)
        out_shape.append(jax.ShapeDtypeStruct((6, BAND_TILES, tile, tile), F32))
    if kind == "C":
        spec = pl.BlockSpec((1, 2, nq, tile), lambda p, b: (b, p, 0, 0))
        in_specs.append(spec)
        args.append(aux)
        out_specs += [spec, seq_spec(LANE)]
        out_shape += [jax.ShapeDtypeStruct((nb, 6, nq, tile), F32), jax.ShapeDtypeStruct((nb * seq, 384), F32)]
    return pl.pallas_call(
        body, name="attn_bwd_" + kind, grid=(3, nb), in_specs=in_specs, out_specs=out_specs, out_shape=out_shape,
        scratch_shapes=[pltpu.VMEM((seq, dq), F32), pltpu.VMEM((seq, LANE), F32)],
        compiler_params=_params(("arbitrary", "arbitrary")),
    )(*args)


BAND_W = BAND_TILES * T_BAND


def _segments(kind, qi, tile):
    r0 = qi * tile
    if kind == "B":
        lo = max(qi - (BAND_TILES - 1), 0) * tile
        return [(lo, r0 + tile, False, lo - (qi - (BAND_TILES - 1)) * tile)]
    return ([(0, r0, False, 0)] if qi else []) + [(r0, r0 + tile, True, 0)]


def _scores(kind, qh, k_ref, aux_ref, h, seg, tile, scale):
    a, b, diag, c0 = seg
    s = _dot_nt(qh, k_ref[a:b, :]) * scale
    if kind == "B":
        row = lax.broadcasted_iota(jnp.int32, (tile, b - a), 0)
        col = lax.broadcasted_iota(jnp.int32, (tile, b - a), 1) + c0
        hidden = ((row >= 64) & (col < 64)) | ((row < 64) & (col >= BAND_W - 64))
        return jnp.where(hidden, NEG, s + aux_ref[h, :, c0:BAND_W])
    if kind == "C":
        s = s - aux_ref[0, h, :, a:b]
    if diag:
        s = jnp.where(_tile_mask(kind, tile), s, NEG)
    return s


def _attn_fwd(kind, q, k, v, aux, nb, seq):
    dq = q.shape[1] // 3
    tile = T_BAND if kind == "B" else T_CAUSAL
    nq = seq // tile
    scale = _attn_scale(kind)

    def body(*refs):
        if kind == "A":
            q_ref, k_ref, v_ref, o_ref, lse_ref = refs
            aux_ref = None
        else:
            q_ref, k_ref, v_ref, aux_ref, o_ref, lse_ref = refs
        first = _lane_iota((tile, LANE)) < 64
        for qi in range(nq):
            rows = slice(qi * tile, (qi + 1) * tile)
            q2 = q_ref[rows, :]
            res = []
            for h in range(2):
                mq, _ = _head_masks(kind, tile, dq, h)
                qh = jnp.where(mq, q2, jnp.zeros_like(q2))
                segs = _segments(kind, qi, tile)
                ss = [_scores(kind, qh, k_ref, aux_ref, h, seg, tile, scale) for seg in segs]
                mx = functools.reduce(jnp.maximum, [jnp.max(s, axis=-1, keepdims=True) for s in ss])
                ps = [jnp.exp(s - mx) for s in ss]
                l = functools.reduce(jnp.add, [jnp.sum(p, axis=-1, keepdims=True) for p in ps])
                acc = functools.reduce(jnp.add, [_dot(p.astype(BF16), v_ref[seg[0]:seg[1], :]) for p, seg in zip(ps, segs)])
                res.append((acc / l, mx + jnp.log(l)))
            o_ref[rows, :] = jnp.where(first, res[0][0], res[1][0])
            lse_ref[rows, :] = jnp.where(first, res[0][1], res[1][1])

    def seq_spec(cols):
        return pl.BlockSpec((seq, cols), lambda b, p: (b, p))

    in_specs = [seq_spec(dq), seq_spec(dq), seq_spec(LANE)]
    args = [q, k, v]
    if kind == "B":
        in_specs.append(pl.BlockSpec((2, tile, BAND_W), lambda b, p: (p, 0, 0)))
        args.append(aux)
    if kind == "C":
        in_specs.append(pl.BlockSpec((1, 2, 1, seq), lambda b, p: (b, p, 0, 0)))
        args.append(aux)
    return pl.pallas_call(
        body, name="attn_fwd_" + kind, grid=(nb, 3), in_specs=in_specs, out_specs=[seq_spec(LANE), seq_spec(LANE)],
        out_shape=[jax.ShapeDtypeStruct((nb * seq, 384), F32)] * 2, compiler_params=_params(("parallel", "parallel")),
    )(*args)


def _attn_bwd(kind, q, k, v, o, do, lse, aux, nb, seq):
    dq = q.shape[1] // 3
    tile = T_BAND if kind == "B" else T_CAUSAL
    nq = seq // tile
    scale = _attn_scale(kind)
    dqk_dtype = F32 if kind == "A" else BF16

    def body(*refs):
        dfr_ref = dfq_ref = dbt_ref = aux_ref = None
        if kind == "A":
            q_ref, k_ref, v_ref, o_ref, do_ref, lse_ref, dq_ref, dk_ref, dv_ref, dkt_acc, dvt_acc = refs
        elif kind == "B":
            q_ref, k_ref, v_ref, o_ref, do_ref, lse_ref, aux_ref, dq_ref, dk_ref, dv_ref, dbt_ref, dkt_acc, dvt_acc = refs
        else:
            (q_ref, k_ref, v_ref, o_ref, do_ref, lse_ref, aux_ref, dq_ref, dk_ref, dv_ref, dfr_ref, dfq_ref,
             dkt_acc, dvt_acc) = refs
        dkt_acc[...] = jnp.zeros_like(dkt_acc)
        dvt_acc[...] = jnp.zeros_like(dvt_acc)
        if kind == "C":
            dfr_ref[...] = jnp.zeros_like(dfr_ref)
        if kind == "B":
            @pl.when(pl.program_id(1) == 0)
            def _():
                dbt_ref[...] = jnp.zeros_like(dbt_ref)

        first = _lane_iota((tile, LANE)) < 64
        for qi in range(nq):
            rows = slice(qi * tile, (qi + 1) * tile)
            q2 = q_ref[rows, :]
            do2 = do_ref[rows, :]
            o2 = o_ref[rows, :]
            lse2 = lse_ref[rows, :]
            dq_tot = jnp.zeros((tile, dq), F32)
            row_sums = []
            for h in range(2):
                mq, mv = _head_masks(kind, tile, dq, h)
                qh = jnp.where(mq, q2, jnp.zeros_like(q2))
                doh = jnp.where(mv, do2, 0.0)
                dob = doh.astype(BF16)
                qht = qh.astype(F32).T.astype(BF16)
                dobt = doh.T.astype(BF16)
                delta = jnp.sum(doh * o2, axis=-1, keepdims=True)
                lseh = jnp.max(jnp.where(mv, lse2, NEG), axis=-1, keepdims=True)
                rs = jnp.zeros((tile, 1), F32)
                for seg in _segments(kind, qi, tile):
                    a, b, _, c0 = seg
                    p = jnp.exp(_scores(kind, qh, k_ref, aux_ref, h, seg, tile, scale) - lseh)
                    ds = p * (_dot_nt(dob, v_ref[a:b, :]) - delta)
                    if kind == "B":
                        dbt_ref[h, :, c0:BAND_W] += ds
                    if kind == "C":
                        dfr_ref[0, h, :, a:b] -= jnp.sum(ds, axis=0, keepdims=True)
                        rs = rs + jnp.sum(ds, axis=-1, keepdims=True)
                    dss = (ds * scale).astype(BF16)
                    dvt_acc[:, a:b] += _dot(dobt, p.astype(BF16))
                    dkt_acc[:, a:b] += _dot(qht, dss)
                    dq_tot = dq_tot + jnp.where(mq, _dot(dss, k_ref[a:b, :]), 0.0)
                row_sums.append(rs)
            dq_ref[rows, :] = dq_tot.astype(dqk_dtype)
            if kind == "C":
                dfq_ref[rows, :] = jnp.where(first, row_sums[0], row_sums[1])
        for j in range(seq // 256):
            cols = slice(256 * j, 256 * (j + 1))
            dk_ref[cols, :] = dkt_acc[:, cols].T.astype(dqk_dtype)
            dv_ref[cols, :] = dvt_acc[:, cols].T.astype(BF16)

    def seq_spec(cols):
        return pl.BlockSpec((seq, cols), lambda p, b: (b, p))

    in_specs = [seq_spec(dq), seq_spec(dq), seq_spec(LANE), seq_spec(LANE), seq_spec(LANE), seq_spec(LANE)]
    args = [q, k, v, o, do, lse]
    out_specs = [seq_spec(dq), seq_spec(dq), seq_spec(LANE)]
    out_shape = [jax.ShapeDtypeStruct((nb * seq, 3 * dq), dqk_dtype)] * 2 + [jax.ShapeDtypeStruct((nb * seq, 384), BF16)]
    if kind == "B":
        spec = pl.BlockSpec((2, tile, BAND_W), lambda p, b: (p, 0, 0))
        in_specs.append(spec)
        args.append(aux)
        out_specs.append(spec)
        out_shape.append(jax.ShapeDtypeStruct((6, tile, BAND_W), F32))
    if kind == "C":
        spec = pl.BlockSpec((1, 2, 1, seq), lambda p, b: (b, p, 0, 0))
        in_specs.append(spec)
        args.append(aux)
        out_specs += [spec, seq_spec(LANE)]
        out_shape += [jax.ShapeDtypeStruct((nb, 6, 1, seq), F32), jax.ShapeDtypeStruct((nb * seq, 384), F32)]
    return pl.pallas_call(
        body, name="attn_bwd_" + kind, grid=(3, nb), in_specs=in_specs, out_specs=out_specs, out_shape=out_shape,
        scratch_shapes=[pltpu.VMEM((dq, seq), F32), pltpu.VMEM((LANE, seq), F32)],
        compiler_params=_params(("arbitrary", "arbitrary")),
    )(*args)


BIAS_FLAT = T_BAND * BAND_W
BIAS_CHUNK = 4 * BAND_W


def _rel_onehot(chunk):
    lane = lax.broadcasted_iota(jnp.int32, (384, chunk), 1)
    r = lax.broadcasted_iota(jnp.int32, (384, chunk), 0)
    sub = jnp.where(lane >= BAND_W, 1, 0) + jnp.where(lane >= 2 * BAND_W, 1, 0) + jnp.where(lane >= 3 * BAND_W, 1, 0)
    i = pl.program_id(0) * 4 + sub
    col = lane - sub * BAND_W
    idx = jnp.clip(BAND_W - T_BAND + i - col, -REL_CLIP, REL_CLIP) + REL_CLIP
    return jnp.where(idx == r, 1.0, 0.0).astype(BF16)


def _bias_expand(rb8):
    def body(rb_ref, out_ref):
        oh = _rel_onehot(BIAS_CHUNK)
        hi, mid, lo = _split3(rb_ref[...])
        out_ref[...] = _dot(hi, oh) + _dot(mid, oh) + _dot(lo, oh)

    return pl.pallas_call(
        body, name="bias_expand", grid=(BIAS_FLAT // BIAS_CHUNK,),
        in_specs=[pl.BlockSpec((8, 384), lambda i: (0, 0))], out_specs=pl.BlockSpec((8, BIAS_CHUNK), lambda i: (0, i)),
        out_shape=jax.ShapeDtypeStruct((8, BIAS_FLAT), F32), compiler_params=_params(("parallel",)),
    )(rb8)


def _bias_reduce(dbt8):
    def body(d_ref, out_ref):
        @pl.when(pl.program_id(0) == 0)
        def _():
            out_ref[...] = jnp.zeros_like(out_ref)

        oh = _rel_onehot(BIAS_CHUNK)
        hi, mid, lo = _split3(d_ref[...])
        out_ref[...] += _dot_nt(hi, oh) + _dot_nt(mid, oh) + _dot_nt(lo, oh)

    return pl.pallas_call(
        body, name="bias_reduce", grid=(BIAS_FLAT // BIAS_CHUNK,),
        in_specs=[pl.BlockSpec((8, BIAS_CHUNK), lambda i: (0, i))], out_specs=pl.BlockSpec((8, 384), lambda i: (0, 0)),
        out_shape=jax.ShapeDtypeStruct((8, 384), F32), compiler_params=_params(("arbitrary",)),
    )(dbt8)


def _outproj_bwd(dxn, y, gate, oa, ob, oc, zg, w_out_p, w_out_pt, nb, seq):
    t = dxn.shape[0]
    tpe = seq // TM

    def body(dxn_ref, y_ref, gate_ref, oa_ref, ob_ref, oc_ref, zg_ref, w_ref, wt_ref,
             doa_ref, dob_ref, doc_ref, dzg_ref, gw_ref, dgate_ref):
        i = pl.program_id(0)

        @pl.when(i == 0)
        def _():
            gw_ref[...] = jnp.zeros_like(gw_ref)

        @pl.when(i % tpe == 0)
        def _():
            dgate_ref[...] = jnp.zeros_like(dgate_ref)

        dxn_t = dxn_ref[...]
        dgate_ref[0] += jnp.sum(dxn_t * y_ref[...], axis=0, keepdims=True)
        dy = (dxn_t * gate_ref[0]).astype(BF16)
        for gi, (o_ref, do_ref) in enumerate(((oa_ref, doa_ref), (ob_ref, dob_ref), (oc_ref, doc_ref))):
            cols = slice(384 * gi, 384 * (gi + 1))
            u = zg_ref[:, cols]
            o_t = o_ref[...]
            su = _silu(u)
            dcat = _dot(dy, wt_ref[:, cols])
            do_ref[...] = dcat * su
            dzg_ref[:, cols] = (dcat * o_t * _dsilu(u)).astype(BF16)
            gw_ref[cols, :] += _dot_tn((o_t * su).astype(BF16), dy)

    return pl.pallas_call(
        body, name="outproj_bwd", grid=(t // TM,),
        in_specs=[_row_spec(D_MODEL), _row_spec(D_MODEL), _ex_spec(tpe), _row_spec(384), _row_spec(384), _row_spec(384),
                  _row_spec(D_CAT), _full_spec((D_CAT, D_MODEL)), _full_spec((D_MODEL, D_CAT))],
        out_specs=[_row_spec(384), _row_spec(384), _row_spec(384), _row_spec(D_CAT), _full_spec((D_CAT, D_MODEL)),
                   _ex_spec(tpe)],
        out_shape=[jax.ShapeDtypeStruct((t, 384), F32)] * 3 + [jax.ShapeDtypeStruct((t, D_CAT), BF16),
                                                                jax.ShapeDtypeStruct((D_CAT, D_MODEL), F32),
                                                                jax.ShapeDtypeStruct((nb, 1, D_MODEL), F32)],
        compiler_params=_params(("arbitrary",)),
    )(dxn, y, gate, oa, ob, oc, zg, w_out_p, w_out_pt)


def _a_up_bwd(dqa, dka, dva, za, gq, gkv, w_uq_pt, w_ukv_pt, cos, sina, sinb):
    t = za.shape[0]

    def body(dq_ref, dk_ref, dv_ref, za_ref, gq_ref, gkv_ref, wqt_ref, wkvt_ref, cos_ref, sa_ref, sb_ref,
             dza_ref, gwq_ref, gwkv_ref, ggq_ref, ggkv_ref, dqb, dkvb):
        @pl.when(pl.program_id(0) == 0)
        def _():
            gwq_ref[...] = jnp.zeros_like(gwq_ref)
            gwkv_ref[...] = jnp.zeros_like(gwkv_ref)
            ggq_ref[...] = jnp.zeros_like(ggq_ref)
            ggkv_ref[...] = jnp.zeros_like(ggkv_ref)

        cos_t, sa, sb = cos_ref[...], sa_ref[...], sb_ref[...]
        dkpe = jnp.zeros((TM, LANE), F32)
        for p in range(3):
            dqb[:, 256 * p:256 * p + 128] = dq_ref[:, 256 * p:256 * p + 128].astype(BF16)
            dqb[:, 256 * p + 128:256 * p + 256] = _rope_t(dq_ref[:, 256 * p + 128:256 * p + 256], cos_t, sa, sb).astype(BF16)
            dkvb[:, 128 * p:128 * p + 128] = dk_ref[:, 256 * p:256 * p + 128].astype(BF16)
            dkpe = dkpe + dk_ref[:, 256 * p + 128:256 * p + 256]
        dkvb[:, 384:768] = dv_ref[...]
        dkpe = _rope_t(dkpe, cos_t, sa, sb)
        dkpe = jnp.where(_lane_iota((TM, LANE)) < A_ROPE, dkpe + pltpu.roll(dkpe, LANE - 32, 1), 0.0)

        gqv = gq_ref[...]
        cqn, cqh, rq = _rms(za_ref[:, 0:384], gqv)
        dq_t = dqb[...]
        gwq_ref[...] += _dot_tn(cqn.astype(BF16), dq_t)
        dcqn = _dot(dq_t, wqt_ref[...])
        ggq_ref[...] += jnp.broadcast_to(jnp.sum(dcqn * cqh, axis=0, keepdims=True), ggq_ref.shape)
        dza_ref[:, 0:384] = _rms_bwd(dcqn, cqh, rq, gqv).astype(BF16)

        gkvv = gkv_ref[...]
        ckvn, ckvh, rkv = _rms(za_ref[:, 384:640], gkvv)
        dkv_t = dkvb[...]
        gwkv_ref[...] += _dot_tn(ckvn.astype(BF16), dkv_t)
        dckvn = _dot(dkv_t, wkvt_ref[...])
        ggkv_ref[...] += jnp.broadcast_to(jnp.sum(dckvn * ckvh, axis=0, keepdims=True), ggkv_ref.shape)
        dza_ref[:, 384:640] = _rms_bwd(dckvn, ckvh, rkv, gkvv).astype(BF16)
        dza_ref[:, 640:768] = dkpe.astype(BF16)

    return pl.pallas_call(
        body, name="a_up_bwd", grid=(t // TM,),
        in_specs=[_row_spec(768), _row_spec(768), _row_spec(384), _row_spec(768), _full_spec((1, 384)),
                  _full_spec((1, 256)), _full_spec((768, 384)), _full_spec((768, 256)), _row_spec(LANE), _row_spec(LANE),
                  _row_spec(LANE)],
        out_specs=[_row_spec(768), _full_spec((384, 768)), _full_spec((256, 768)), _full_spec((8, 384)),
                   _full_spec((8, 256))],
        out_shape=[jax.ShapeDtypeStruct((t, 768), BF16), jax.ShapeDtypeStruct((384, 768), F32),
                   jax.ShapeDtypeStruct((256, 768), F32), jax.ShapeDtypeStruct((8, 384), F32),
                   jax.ShapeDtypeStruct((8, 256), F32)],
        scratch_shapes=[pltpu.VMEM((TM, 768), BF16), pltpu.VMEM((TM, 768), BF16)],
        compiler_params=_params(("arbitrary",)),
    )(dqa, dka, dva, za, gq, gkv, w_uq_pt, w_ukv_pt, cos, sina, sinb)


def _dz_cols():
    return (Z_A, Z_G) + Z_QKV + (Z_F,)


def _inproj_bwd_dx(dz, dxn, x, shift, scale, g, w_in_pt, nb, seq):
    t = x.shape[0]
    tpe = seq // TM
    cols = _dz_cols()

    def body(*refs):
        dz_refs = refs[:len(cols)]
        dxn_ref, x_ref, sh_ref, sc_ref, g_ref, wt_ref, dx_ref, dsh_ref, dsc_ref, dg_ref = refs[len(cols):]
        i = pl.program_id(0)

        @pl.when(i == 0)
        def _():
            dg_ref[...] = jnp.zeros_like(dg_ref)

        @pl.when(i % tpe == 0)
        def _():
            dsh_ref[...] = jnp.zeros_like(dsh_ref)
            dsc_ref[...] = jnp.zeros_like(dsc_ref)

        dh = jnp.zeros((TM, D_MODEL), F32)
        for ref, (c0, c1) in zip(dz_refs, cols):
            dh = dh + _dot(ref[...], wt_ref[c0:c1, :])
        gv = g_ref[...]
        n, xh, r = _rms(x_ref[...], gv)
        dsh_ref[0] += jnp.sum(dh, axis=0, keepdims=True)
        dsc_ref[0] += jnp.sum(dh * n, axis=0, keepdims=True)
        dn = dh * (1.0 + sc_ref[0])
        dg_ref[...] += jnp.broadcast_to(jnp.sum(dn * xh, axis=0, keepdims=True), dg_ref.shape)
        dx_ref[...] = dxn_ref[...] + _rms_bwd(dn, xh, r, gv)

    in_specs = [_row_spec(c1 - c0) for c0, c1 in cols]
    in_specs += [_row_spec(D_MODEL), _row_spec(D_MODEL), _ex_spec(tpe), _ex_spec(tpe), _full_spec((1, D_MODEL)),
                 _full_spec((NP_IN, D_MODEL))]
    return pl.pallas_call(
        body, name="inproj_bwd_dx", grid=(t // TM,), in_specs=in_specs,
        out_specs=[_row_spec(D_MODEL), _ex_spec(tpe), _ex_spec(tpe), _full_spec((8, D_MODEL))],
        out_shape=[jax.ShapeDtypeStruct((t, D_MODEL), F32), jax.ShapeDtypeStruct((nb, 1, D_MODEL), F32),
                   jax.ShapeDtypeStruct((nb, 1, D_MODEL), F32), jax.ShapeDtypeStruct((8, D_MODEL), F32)],
        compiler_params=_params(("arbitrary",)),
    )(*dz, dxn, x, shift, scale, g, w_in_pt)


def _inproj_bwd_dw(h, dz, name):
    t = h.shape[0]
    widths = [d.shape[1] for d in dz]
    total = sum(widths)

    def body(*refs):
        h_ref = refs[0]
        dz_refs = refs[1:1 + len(dz)]
        gw_ref = refs[1 + len(dz)]

        @pl.when(pl.program_id(0) == 0)
        def _():
            gw_ref[...] = jnp.zeros_like(gw_ref)

        h_t = h_ref[...]
        c0 = 0
        for ref, w in zip(dz_refs, widths):
            gw_ref[:, c0:c0 + w] += _dot_tn(h_t, ref[...])
            c0 += w

    return pl.pallas_call(
        body, name=name, grid=(t // TM,), in_specs=[_row_spec(D_MODEL)] + [_row_spec(w) for w in widths],
        out_specs=_full_spec((D_MODEL, total)), out_shape=jax.ShapeDtypeStruct((D_MODEL, total), F32),
        compiler_params=_params(("arbitrary",)),
    )(h, *dz)


def _ada_fwd(c_all, w_ada, b_cols):
    n = c_all.shape[0]
    cols = w_ada.shape[2]

    def body(c_ref, w_ref, b_ref, out_ref):
        act = _silu(c_ref[...]).astype(BF16)
        out_ref[0] = _dot(act, w_ref[0].astype(BF16)) + b_ref[0]

    return pl.pallas_call(
        body, name="ada_fwd", grid=(DEPTH,),
        in_specs=[pl.BlockSpec((n, D_MODEL), lambda l: (0, 0)), pl.BlockSpec((1, D_MODEL, cols), lambda l: (l, 0, 0)),
                  pl.BlockSpec((1, 1, cols), lambda l: (l, 0, 0))],
        out_specs=pl.BlockSpec((1, n, cols), lambda l: (l, 0, 0)),
        out_shape=jax.ShapeDtypeStruct((DEPTH, n, cols), F32), compiler_params=_params(("parallel",)),
    )(c_all, w_ada, b_cols)


def _ada_bwd(c_all, dmod_cols, dmod_all):
    n = c_all.shape[0]
    cols = dmod_cols.shape[2]

    def body(c_ref, dc_ref, da_ref, gw_ref, gb_ref):
        act = _silu(c_ref[...]).astype(BF16)
        gw_ref[0] = _dot_tn(act, dc_ref[0].astype(BF16))
        gb_ref[0] = jnp.sum(da_ref[0], axis=0, keepdims=True)

    return pl.pallas_call(
        body, name="ada_bwd", grid=(DEPTH,),
        in_specs=[pl.BlockSpec((n, D_MODEL), lambda l: (0, 0)), pl.BlockSpec((1, n, cols), lambda l: (l, 0, 0)),
                  pl.BlockSpec((1, n, 3 * D_MODEL), lambda l: (l, 0, 0))],
        out_specs=[pl.BlockSpec((1, D_MODEL, cols), lambda l: (l, 0, 0)),
                   pl.BlockSpec((1, 1, 3 * D_MODEL), lambda l: (l, 0, 0))],
        out_shape=[jax.ShapeDtypeStruct((DEPTH, D_MODEL, cols), F32), jax.ShapeDtypeStruct((DEPTH, 1, 3 * D_MODEL), F32)],
        compiler_params=_params(("parallel",)),
    )(c_all, dmod_cols, dmod_all)


def _sum_blocks(parts, name):
    n, rows, cols = parts.shape
    tr = rows if rows <= 256 else 8 * next(d for d in range(32, 0, -1) if (rows // 8) % d == 0)

    def body(p_ref, out_ref):
        acc = p_ref[0].astype(F32)
        for k in range(1, n):
            acc = acc + p_ref[k].astype(F32)
        out_ref[...] = acc

    return pl.pallas_call(
        body, name=name, grid=(rows // tr,), in_specs=[pl.BlockSpec((n, tr, cols), lambda i: (0, i, 0))],
        out_specs=pl.BlockSpec((tr, cols), lambda i: (i, 0)), out_shape=jax.ShapeDtypeStruct((rows, cols), F32),
        compiler_params=_params(("parallel",)),
    )(parts)


def _adamw(w, g, m, v, name):
    rows, cols = w.shape
    tr = rows if rows <= 256 else 8 * next(d for d in range(32, 0, -1) if (rows // 8) % d == 0)

    def body(w_ref, g_ref, m_ref, v_ref, d_ref, mo_ref, vo_ref):
        gv = g_ref[...]
        mn = ADAM_B1 * m_ref[...] + (1.0 - ADAM_B1) * gv
        vn = ADAM_B2 * v_ref[...] + (1.0 - ADAM_B2) * jnp.square(gv)
        m_hat = mn / (1.0 - ADAM_B1 ** ADAM_STEP)
        v_hat = vn / (1.0 - ADAM_B2 ** ADAM_STEP)
        d_ref[...] = -ADAM_LR * (m_hat / (jnp.sqrt(v_hat) + ADAM_EPS) + ADAM_WD * w_ref[...])
        mo_ref[...] = mn
        vo_ref[...] = vn

    spec = pl.BlockSpec((tr, cols), lambda i: (i, 0))
    return pl.pallas_call(
        body, name=name, grid=(rows // tr,), in_specs=[spec] * 4, out_specs=[spec] * 3,
        out_shape=[jax.ShapeDtypeStruct((rows, cols), F32)] * 3, compiler_params=_params(("parallel",)),
    )(w, g, m, v)


ALL_FLIPS = tuple(range(1, N_DEV))


def _exchange(src, flips, mode, name):
    _, rows, cols = src.shape
    nslot = 2 if mode == "pair" else N_DEV
    nf = len(flips)

    def body(src_ref, dst_ref, send_sems, recv_sems, local_sem):
        x, y, c = lax.axis_index("x"), lax.axis_index("y"), lax.axis_index("c")
        me = 4 * x + 2 * y + c

        def slot(j):
            return (j & 1) if mode == "pair" else j

        own = pltpu.make_async_copy(src_ref.at[me if mode == "scatter" else 0], dst_ref.at[slot(me)], local_sem)
        own.start()
        copies = []
        for i, f in enumerate(flips):
            peer = me ^ f
            to = (1 - x if f & 4 else x, 1 - y if f & 2 else y, 1 - c if f & 1 else c)
            cp = pltpu.make_async_remote_copy(
                src_ref=src_ref.at[peer if mode == "scatter" else 0], dst_ref=dst_ref.at[slot(me)],
                send_sem=send_sems.at[i], recv_sem=recv_sems.at[i], device_id=to, device_id_type=pl.DeviceIdType.MESH)
            cp.start()
            copies.append(cp)
        for i, f in enumerate(flips):
            peer = me ^ f
            to = (1 - x if f & 4 else x, 1 - y if f & 2 else y, 1 - c if f & 1 else c)
            pltpu.make_async_remote_copy(
                src_ref=src_ref.at[0], dst_ref=dst_ref.at[slot(peer)], send_sem=send_sems.at[i],
                recv_sem=recv_sems.at[i], device_id=to, device_id_type=pl.DeviceIdType.MESH).wait_recv()
        for cp in copies:
            cp.wait_send()
        own.wait()

    return pl.pallas_call(
        body, name=name, out_shape=jax.ShapeDtypeStruct((nslot, rows, cols), src.dtype),
        in_specs=[pl.BlockSpec(memory_space=pl.ANY)], out_specs=pl.BlockSpec(memory_space=pl.ANY),
        scratch_shapes=[pltpu.SemaphoreType.DMA((nf,)), pltpu.SemaphoreType.DMA((nf,)), pltpu.SemaphoreType.DMA],
    )(src)


def _pad_cols(a, n):
    return a if n == 0 else jnp.pad(a, ((0, 0), (0, n)))


def _in_to_padded(w):
    return jnp.concatenate([_pad_cols(w[:, a:b], z) for a, b, z in IN_PIECES], axis=1)


def _in_from_padded(gp):
    pos, out = 0, {}
    for a, b, z in IN_PIECES:
        out[a] = gp[:, pos:pos + (b - a)]
        pos += (b - a) + z
    return jnp.concatenate([out[a] for a in sorted(out)], axis=1)


def _out_to_padded(w):
    z = jnp.zeros((64, w.shape[1]), w.dtype)
    return jnp.concatenate([w[0:384], w[384:704], z, w[704:1024], z], axis=0)


def _out_from_padded(gp):
    return jnp.concatenate([gp[0:384], gp[384:704], gp[768:1088]], axis=0)


def _uq_to_padded(w):
    parts = []
    for p in range(3):
        h0, h1 = 2 * p, 2 * p + 1
        parts += [w[:, 96 * h0:96 * h0 + 64], w[:, 96 * h1:96 * h1 + 64], w[:, 96 * h0 + 64:96 * h0 + 96],
                  w[:, 96 * h1 + 64:96 * h1 + 96], jnp.zeros((w.shape[0], 64), w.dtype)]
    return jnp.concatenate(parts, axis=1)


def _uq_from_padded(gp):
    parts = []
    for h in range(6):
        p, s = h // 2, h % 2
        parts += [gp[:, 256 * p + 64 * s:256 * p + 64 * s + 64], gp[:, 256 * p + 128 + 32 * s:256 * p + 160 + 32 * s]]
    return jnp.concatenate(parts, axis=1)


def _ukv_to_padded(w):
    return jnp.concatenate([w[:, 128 * h:128 * h + 64] for h in range(6)]
                           + [w[:, 128 * h + 64:128 * h + 128] for h in range(6)], axis=1)


def _ukv_from_padded(gp):
    parts = []
    for h in range(6):
        parts += [gp[:, 64 * h:64 * h + 64], gp[:, 384 + 64 * h:384 + 64 * h + 64]]
    return jnp.concatenate(parts, axis=1)


PACK_SIZES = (DEPTH * 256 * N_IN, DEPTH * 256 * D_MODEL, DEPTH * A_Q_RANK * 144, DEPTH * A_KV_RANK * 192)


def _pack(w_in, w_out, w_uq, w_ukv):
    flat = jnp.concatenate([w_in.reshape(-1), w_out.reshape(-1), w_uq.reshape(-1), w_ukv.reshape(-1)])
    flat = jnp.pad(flat, (0, PACK_ROWS * PACK_COLS - flat.shape[0]))
    return flat.reshape(PACK_ROWS, PACK_COLS)


def _unpack(packed):
    flat = packed.reshape(-1)
    o0, o1, o2, o3 = PACK_SIZES
    w_in = flat[0:o0].reshape(DEPTH, 256, N_IN)
    w_out = flat[o0:o0 + o1].reshape(DEPTH, 256, D_MODEL)
    w_uq = flat[o0 + o1:o0 + o1 + o2].reshape(DEPTH, A_Q_RANK, 144)
    w_ukv = flat[o0 + o1 + o2:o0 + o1 + o2 + o3].reshape(DEPTH, A_KV_RANK, 192)
    return w_in, w_out, w_uq, w_ukv


def _rope_tables(positions):
    inv = ROPE_THETA ** (-jnp.arange(0, A_ROPE, 2, dtype=F32) / A_ROPE)
    ang = positions.astype(F32)[..., None] * inv
    cos = jnp.tile(jnp.cos(ang), (1, 1, 4)).reshape(-1, 64)
    sin = jnp.tile(jnp.sin(ang), (1, 1, 4)).reshape(-1, 64)
    second = (jnp.arange(64) % 32) >= 16
    pad = ((0, 0), (0, 64))
    return (jnp.pad(cos, pad), jnp.pad(jnp.where(second, 0.0, -sin), pad), jnp.pad(jnp.where(second, sin, 0.0), pad))


def _rows(a, n):
    flat = a.reshape(-1)
    return jnp.pad(flat, (0, n * LANE - flat.shape[0])).reshape(n, LANE)


def _forward_backward(x, mod, tables, target, weights, small, nb, seq):
    cos, sina, sinb = tables
    saved = []
    for l in range(DEPTH):
        w, s = weights[l], small[l]
        shift = mod[l][:, None, 0:D_MODEL]
        scale = mod[l][:, None, D_MODEL:2 * D_MODEL]
        gate = mod[l][:, None, 2 * D_MODEL:]
        h, za, zg, qb, kb, vb, qc, kc, vc, zf = _ln_inproj(x, shift, scale, s["norm_g"], w["in"], seq)
        qa, ka, va = _a_up(za, s["gq"], s["gkv"], w["uq"], w["ukv"], cos, sina, sinb)
        bias = _bias_expand(s["rb8"])[0:6].reshape(6, T_BAND, BAND_W)
        f = _forget_fwd(zf, s["fb"], nb, seq)
        frow = jnp.pad(f[:, 0:5].reshape(nb, seq, 5).transpose(0, 2, 1), ((0, 0), (0, 1), (0, 0)))
        frow = frow.reshape(nb, 6, 1, seq)
        oa, lse_a = _attn_fwd("A", qa, ka, va, None, nb, seq)
        ob, lse_b = _attn_fwd("B", qb, kb, vb, bias, nb, seq)
        oc, lse_c = _attn_fwd("C", qc, kc, vc, frow, nb, seq)
        y, xn = _gate_outproj(x, gate, oa, ob, oc, zg, w["out"], seq)
        saved.append(dict(x=x, h=h, za=za, zg=zg, zf=zf, y=y, shift=shift, scale=scale, gate=gate, bias=bias, frow=frow,
                          a=(qa, ka, va, oa, lse_a), b=(qb, kb, vb, ob, lse_b), c=(qc, kc, vc, oc, lse_c)))
        x = xn
    dx, loss8, gfinal8 = _final_loss(x, target, small[0]["final_g"])
    grads = []
    for l in reversed(range(DEPTH)):
        w, s, sv = weights[l], small[l], saved[l]
        qa, ka, va, oa, lse_a = sv["a"]
        qb, kb, vb, ob, lse_b = sv["b"]
        qc, kc, vc, oc, lse_c = sv["c"]
        doa, dob, doc, dzg, gw_out, dgate = _outproj_bwd(dx, sv["y"], sv["gate"], oa, ob, oc, sv["zg"], w["out"],
                                                          w["out_t"], nb, seq)
        dqa, dka, dva = _attn_bwd("A", qa, ka, va, oa, doa, lse_a, None, nb, seq)
        dqb, dkb, dvb, dbt = _attn_bwd("B", qb, kb, vb, ob, dob, lse_b, sv["bias"], nb, seq)
        dqc, dkc, dvc, dfr, dfq = _attn_bwd("C", qc, kc, vc, oc, doc, lse_c, sv["frow"], nb, seq)
        grb = _bias_reduce(jnp.pad(dbt.reshape(6, BIAS_FLAT), ((0, 2), (0, 0))))
        df = dfr.reshape(nb, 6, seq).transpose(0, 2, 1).reshape(nb * seq, 6) + dfq.reshape(nb * seq, 6, 64)[:, :, 0]
        dzf, gfb = _forget_bwd(jnp.pad(df, ((0, 0), (0, LANE - 6))), sv["zf"], s["fb"], nb, seq)
        dza, gw_uq, gw_ukv, ggq, ggkv = _a_up_bwd(dqa, dka, dva, sv["za"], s["gq"], s["gkv"], w["uq_t"], w["ukv_t"],
                                                  cos, sina, sinb)
        dz = (dza, dzg, dqb, dkb, dvb, dqc, dkc, dvc, dzf)
        dx, dshift, dscale, gnorm = _inproj_bwd_dx(dz, dx, sv["x"], sv["shift"], sv["scale"], s["norm_g"], w["in_t"],
                                                   nb, seq)
        gw_in = jnp.concatenate([_inproj_bwd_dw(sv["h"], dz[0:2], "inproj_bwd_dw0"),
                                 _inproj_bwd_dw(sv["h"], dz[2:], "inproj_bwd_dw1")], axis=1)
        dmod = jnp.concatenate([dshift[:, 0], dscale[:, 0], dgate[:, 0]], axis=1)
        grads.append(dict(w_in=gw_in, w_out=gw_out, w_uq=gw_uq, w_ukv=gw_ukv, dmod=dmod, norm_g=gnorm[0], gq=ggq[0],
                          gkv=ggkv[0], rb8=grb, fb=gfb[0]))
    grads.reverse()
    return loss8[0, 0], dx, grads, gfinal8[0]


def _layer_weights(w_in, w_out, w_uq, w_ukv):
    wi, wo, wq, wkv = _in_to_padded(w_in), _out_to_padded(w_out), _uq_to_padded(w_uq), _ukv_to_padded(w_ukv)
    return {"in": wi, "in_t": wi.T, "out": wo, "out_t": wo.T, "uq": wq, "uq_t": wq.T, "ukv": wkv, "ukv_t": wkv.T}


def _layer_small(norm_g, gq, gkv, rel_bias, forget_b, final_g):
    rb8 = jnp.pad(rel_bias, ((0, 3), (0, 384 - N_REL)))
    fb = jnp.pad(forget_b, (0, LANE - 5)).reshape(1, LANE)
    return dict(norm_g=norm_g.reshape(1, -1), gq=gq.reshape(1, -1), gkv=gkv.reshape(1, -1), rb8=rb8, fb=fb,
                final_g=final_g.reshape(1, -1))


def _small_payload(per_layer, final_g, loss):
    def stack(key):
        return jnp.stack([p[key] for p in per_layer])

    def rows(a, rng):
        return _rows(a, rng[1] - rng[0])

    dmod = stack("dmod") if "dmod" in per_layer[0] else jnp.zeros((LANE,), F32)
    parts = [rows(dmod, PAY_DMOD), rows(stack("norm_g"), PAY_NORM), rows(stack("gq"), PAY_GQ),
             rows(stack("gkv"), PAY_GKV), rows(stack("rb8"), PAY_RB), rows(stack("fb"), PAY_FB),
             rows(final_g, PAY_FINAL), rows(loss, PAY_LOSS)]
    return jnp.concatenate(parts, axis=0)


def _payload_split(pay):
    def take(rng, shape):
        n = 1
        for d in shape:
            n *= d
        return pay[rng[0]:rng[1]].reshape(-1)[0:n].reshape(shape)

    norm_g = take(PAY_NORM, (DEPTH, D_MODEL))
    gq = take(PAY_GQ, (DEPTH, A_Q_RANK))
    gkv = take(PAY_GKV, (DEPTH, A_KV_RANK))
    rb = take(PAY_RB, (DEPTH, 8, 384))[:, 0:5, 0:N_REL]
    fb = take(PAY_FB, (DEPTH, LANE))[:, 0:5]
    final_g = take(PAY_FINAL, (D_MODEL,))
    return norm_g, gq, gkv, rb, fb, final_g


def kernel(x, c, positions, w_ada, b_ada, norm_g, w_in, a_q_norm_g, a_w_uq, a_kv_norm_g, a_w_ukv, b_rel_bias, c_forget_b, w_out, final_g, loss_target, m_w_ada, m_b_ada, m_norm_g, m_w_in, m_a_q_norm_g, m_a_w_uq, m_a_kv_norm_g, m_a_w_ukv, m_b_rel_bias, m_c_forget_b, m_w_out, m_final_g, v_w_ada, v_b_ada, v_norm_g, v_w_in, v_a_q_norm_g, v_a_w_uq, v_a_kv_norm_g, v_a_w_ukv, v_b_rel_bias, v_c_forget_b, v_w_out, v_final_g):
    nb, seq, _ = x.shape
    ix, iy, ic = lax.axis_index("x"), lax.axis_index("y"), lax.axis_index("c")
    chip = 2 * ix + iy
    me = 2 * chip + ic

    packed_w = _pack(w_in, w_out, a_w_uq, a_w_ukv)
    half = lax.dynamic_slice_in_dim(packed_w, ic * HALF_ROWS, HALF_ROWS, axis=0).astype(BF16)
    gathered = _exchange(half[None], ALL_FLIPS, "gather", "gather_weights").reshape(4, PACK_ROWS, PACK_COLS)
    shards = [_unpack(gathered[j]) for j in range(4)]
    full_in = jnp.concatenate([s[0] for s in shards], axis=1)
    full_out = jnp.concatenate([s[1] for s in shards], axis=1)
    full_uq = jnp.concatenate([s[2] for s in shards], axis=2)
    full_ukv = jnp.concatenate([s[3] for s in shards], axis=2)
    weights = [_layer_weights(full_in[l], full_out[l], full_uq[l], full_ukv[l]) for l in range(DEPTH)]
    small = [_layer_small(norm_g[l], a_q_norm_g[l], a_kv_norm_g[l], b_rel_bias[l], c_forget_b[l], final_g)
             for l in range(DEPTH)]

    c_all = _exchange(c[None], ALL_FLIPS, "gather", "gather_c").reshape(N_DEV * nb, D_MODEL)
    cols = w_ada.shape[2]
    b_cols = lax.dynamic_slice_in_dim(b_ada, chip * cols, cols, axis=1)[:, None, :]
    mod_cols = _ada_fwd(c_all, w_ada, b_cols)
    mod_g = _exchange(mod_cols.reshape(1, DEPTH * N_DEV * nb, cols), ALL_FLIPS, "gather", "gather_mod")
    mod_all = jnp.concatenate([mod_g[2 * j].reshape(DEPTH, N_DEV * nb, cols) for j in range(4)], axis=2)
    mod = lax.dynamic_slice_in_dim(mod_all, me * nb, nb, axis=1)

    tables = _rope_tables(positions)
    loss_part, dx, grads, gfinal = _forward_backward(
        x.reshape(nb * seq, D_MODEL), mod, tables, loss_target.reshape(nb * seq, D_MODEL), weights, small, nb, seq)

    pay = _small_payload(grads, gfinal, loss_part)
    pay_all = _exchange(pay[None], ALL_FLIPS, "gather", "gather_small")
    tot = _sum_blocks(pay_all, "sum_small")
    loss = tot[PAY_LOSS[0], 0]
    dmod_all = pay_all[:, PAY_DMOD[0]:PAY_DMOD[1]].reshape(N_DEV, -1)[:, 0:DEPTH * nb * 3 * D_MODEL]
    dmod_all = dmod_all.reshape(N_DEV, DEPTH, nb, 3 * D_MODEL).transpose(1, 0, 2, 3)
    dmod_all = dmod_all.reshape(DEPTH, N_DEV * nb, 3 * D_MODEL)
    my_cols = lax.dynamic_slice_in_dim(dmod_all, chip * cols, cols, axis=2)
    g_w_ada, g_b_ada = _ada_bwd(c_all, my_cols, dmod_all)
    g_b_ada = g_b_ada[:, 0]

    g_in = jnp.stack([_in_from_padded(g["w_in"]) for g in grads])
    g_out = jnp.stack([_out_from_padded(g["w_out"]) for g in grads])
    g_uq = jnp.stack([_uq_from_padded(g["w_uq"]) for g in grads])
    g_ukv = jnp.stack([_ukv_from_padded(g["w_ukv"]) for g in grads])
    per_chip = jnp.stack([_pack(g_in[:, 256 * j:256 * (j + 1)], g_out[:, 256 * j:256 * (j + 1)],
                                g_uq[:, :, 144 * j:144 * (j + 1)], g_ukv[:, :, 192 * j:192 * (j + 1)])
                          for j in range(4)])
    parts = _exchange(per_chip.reshape(N_DEV, HALF_ROWS, PACK_COLS), ALL_FLIPS, "scatter", "scatter_grads")
    reduced = _sum_blocks(parts, "sum_grads")
    g_packed = _exchange(reduced[None], (1,), "pair", "pair_grads").reshape(PACK_ROWS, PACK_COLS)

    d_p, m_p, v_p = _adamw(packed_w, g_packed, _pack(m_w_in, m_w_out, m_a_w_uq, m_a_w_ukv),
                           _pack(v_w_in, v_w_out, v_a_w_uq, v_a_w_ukv), "adamw_packed")
    gw = _unpack(g_packed)
    dw = _unpack(d_p)
    mw = _unpack(m_p)
    vw = _unpack(v_p)
    ada_shape = w_ada.shape
    flat_ada = (DEPTH * D_MODEL, cols)
    d_ada, m_ada, v_ada = (a.reshape(ada_shape) for a in _adamw(
        w_ada.reshape(flat_ada), g_w_ada.reshape(flat_ada), m_w_ada.reshape(flat_ada), v_w_ada.reshape(flat_ada),
        "adamw_ada"))
    d_b, m_b, v_b = (a.reshape(DEPTH, 3 * D_MODEL) for a in _adamw(
        _rows(b_ada, 48), _rows(g_b_ada, 48), _rows(m_b_ada, 48), _rows(v_b_ada, 48), "adamw_b_ada"))

    def small_rows(ng, gq, gkv, rb, fb, fg):
        per_layer = [dict(norm_g=ng[l], gq=gq[l], gkv=gkv[l], rb8=jnp.pad(rb[l], ((0, 3), (0, 384 - N_REL))),
                          fb=jnp.pad(fb[l], (0, LANE - 5))) for l in range(DEPTH)]
        return _small_payload(per_layer, fg, jnp.zeros((), F32))

    w_s = small_rows(norm_g, a_q_norm_g, a_kv_norm_g, b_rel_bias, c_forget_b, final_g)
    m_s = small_rows(m_norm_g, m_a_q_norm_g, m_a_kv_norm_g, m_b_rel_bias, m_c_forget_b, m_final_g)
    v_s = small_rows(v_norm_g, v_a_q_norm_g, v_a_kv_norm_g, v_b_rel_bias, v_c_forget_b, v_final_g)
    d_s, mo_s, vo_s = _adamw(w_s, tot, m_s, v_s, "adamw_small")
    gs = _payload_split(tot)
    ds = _payload_split(d_s)
    ms = _payload_split(mo_s)
    vs = _payload_split(vo_s)

    def ordered(ada, b, sm, big):
        ng, gq, gkv, rb, fb, fg = sm
        b_in, b_out, b_uq, b_ukv = big
        return (ada, b, ng, b_in, gq, b_uq, gkv, b_ukv, rb, fb, b_out, fg)

    return (loss, dx.reshape(nb, seq, D_MODEL), *ordered(g_w_ada, g_b_ada, gs, gw), *ordered(d_ada, d_b, ds, dw),
            *ordered(m_ada, m_b, ms, mw), *ordered(v_ada, v_b, vs, vw))
```

```python
import functools

import jax
import jax.numpy as jnp
from jax import lax
from jax.experimental import pallas as pl
from jax.experimental.pallas import tpu as pltpu

F32 = jnp.float32
BF16 = jnp.bfloat16

D_MODEL = 1024
DEPTH = 2
EPS = 1e-6
NEG = -1e30
ROPE_THETA = 10000.0
A_ROPE = 32
A_Q_RANK = 384
A_KV_RANK = 256
REL_CLIP = 128
N_REL = 2 * REL_CLIP + 1
N_IN = 3621

ADAM_LR = 0.001
ADAM_B1 = 0.9
ADAM_B2 = 0.999
ADAM_EPS = 1e-08
ADAM_WD = 0.01
ADAM_STEP = 10

LANE = 128
VMEM_LIMIT = 56 * 1024 * 1024

NP_IN = 4352
Z_A = (0, 768)
Z_G = (768, 1920)
Z_QKV = tuple((1920 + 384 * i, 1920 + 384 * (i + 1)) for i in range(6))
Z_F = (4224, 4352)
IN_PIECES = ((0, 672, 96), (672, 1056, 0), (2016, 2336, 64), (3301, 3621, 64), (1056, 1376, 64), (1376, 1696, 64),
             (1696, 2016, 64), (2336, 2656, 64), (2656, 2976, 64), (2976, 3296, 64), (3296, 3301, 123))
D_CAT = 1152

TM = 256
T_CAUSAL = 256
T_BAND = 128
BAND_TILES = 5
N_DEV = 8

PAY_DMOD = (0, 96)
PAY_NORM = (96, 112)
PAY_GQ = (112, 120)
PAY_GKV = (120, 128)
PAY_RB = (128, 176)
PAY_FB = (176, 184)
PAY_FINAL = (184, 192)
PAY_LOSS = (192, 200)
PAY_ROWS = 200

PACK_COLS = 1024
PACK_ROWS = 2560
HALF_ROWS = PACK_ROWS // 2


def _params(sem=None):
    return pltpu.CompilerParams(dimension_semantics=sem, vmem_limit_bytes=VMEM_LIMIT)


def _lane_iota(shape):
    return lax.broadcasted_iota(jnp.int32, shape, len(shape) - 1)


def _silu(u):
    return u * jax.nn.sigmoid(u)


def _dsilu(u):
    s = jax.nn.sigmoid(u)
    return s * (1.0 + u * (1.0 - s))


def _rms(x, g):
    r = lax.rsqrt(jnp.mean(x * x, axis=-1, keepdims=True) + EPS)
    xh = x * r
    return xh * g, xh, r


def _rms_bwd(dy, xh, r, g):
    dxh = dy * g
    return r * (dxh - xh * jnp.mean(dxh * xh, axis=-1, keepdims=True))


def _rope(x, cos, sina, sinb):
    return x * cos + pltpu.roll(x, 16, 1) * sinb + pltpu.roll(x, LANE - 16, 1) * sina


def _rope_t(dy, cos, sina, sinb):
    return dy * cos + pltpu.roll(dy * sinb, LANE - 16, 1) + pltpu.roll(dy * sina, 16, 1)


def _split3(x):
    hi = x.astype(BF16)
    r1 = x - hi.astype(F32)
    mid = r1.astype(BF16)
    lo = (r1 - mid.astype(F32)).astype(BF16)
    return hi, mid, lo


def _dot(a, b):
    return jnp.dot(a, b, preferred_element_type=F32)


def _dot_nt(a, b):
    return lax.dot_general(a, b, (((1,), (1,)), ((), ())), preferred_element_type=F32)


def _dot_tn(a, b):
    return lax.dot_general(a, b, (((0,), (0,)), ((), ())), preferred_element_type=F32)


def _row_spec(cols):
    return pl.BlockSpec((TM, cols), lambda i: (i, 0))


def _full_spec(shape):
    return pl.BlockSpec(shape, lambda i: (0,) * len(shape))


def _ex_spec(tiles_per_ex):
    return pl.BlockSpec((1, 1, D_MODEL), lambda i: (i // tiles_per_ex, 0, 0))


def _ln_inproj(x, shift, scale, g, w_in_p, seq):
    t = x.shape[0]

    def body(x_ref, sh_ref, sc_ref, g_ref, w_ref, h_ref, za_ref, zg_ref, q0, q1, q2, q3, q4, q5, zf_ref):
        n, _, _ = _rms(x_ref[...], g_ref[...])
        h = (n * (1.0 + sc_ref[0]) + sh_ref[0]).astype(BF16)
        h_ref[...] = h
        za_ref[...] = _dot(h, w_ref[:, Z_A[0]:Z_A[1]])
        zg_ref[...] = _dot(h, w_ref[:, Z_G[0]:Z_G[1]])
        for ref, (c0, c1) in zip((q0, q1, q2, q3, q4, q5), Z_QKV):
            ref[...] = _dot(h, w_ref[:, c0:c1]).astype(BF16)
        zf_ref[...] = _dot(h, w_ref[:, Z_F[0]:Z_F[1]])

    tpe = seq // TM
    shapes = [jax.ShapeDtypeStruct((t, D_MODEL), BF16), jax.ShapeDtypeStruct((t, 768), F32),
              jax.ShapeDtypeStruct((t, D_CAT), F32)]
    shapes += [jax.ShapeDtypeStruct((t, 384), BF16)] * 6 + [jax.ShapeDtypeStruct((t, LANE), F32)]
    return pl.pallas_call(
        body, name="ln_inproj", grid=(t // TM,),
        in_specs=[_row_spec(D_MODEL), _ex_spec(tpe), _ex_spec(tpe), _full_spec((1, D_MODEL)),
                  _full_spec((D_MODEL, NP_IN))],
        out_specs=[_row_spec(D_MODEL), _row_spec(768), _row_spec(D_CAT)] + [_row_spec(384)] * 6 + [_row_spec(LANE)],
        out_shape=shapes, compiler_params=_params(("parallel",)),
    )(x, shift, scale, g, w_in_p)


def _a_up(za, gq, gkv, w_uq_p, w_ukv_p, cos, sina, sinb):
    t = za.shape[0]

    def body(za_ref, gq_ref, gkv_ref, wq_ref, wkv_ref, cos_ref, sa_ref, sb_ref, q_ref, k_ref, v_ref):
        cos_t, sa, sb = cos_ref[...], sa_ref[...], sb_ref[...]
        cqn, _, _ = _rms(za_ref[:, 0:384], gq_ref[...])
        q = _dot(cqn.astype(BF16), wq_ref[...])
        ckvn, _, _ = _rms(za_ref[:, 384:640], gkv_ref[...])
        kv = _dot(ckvn.astype(BF16), wkv_ref[...])
        kpe = za_ref[:, 640:768]
        kpe = _rope(kpe + pltpu.roll(kpe, 32, 1), cos_t, sa, sb).astype(BF16)
        for p in range(3):
            q_ref[:, 256 * p:256 * p + 128] = q[:, 256 * p:256 * p + 128].astype(BF16)
            q_ref[:, 256 * p + 128:256 * p + 256] = _rope(q[:, 256 * p + 128:256 * p + 256], cos_t, sa, sb).astype(BF16)
            k_ref[:, 256 * p:256 * p + 128] = kv[:, 128 * p:128 * p + 128].astype(BF16)
            k_ref[:, 256 * p + 128:256 * p + 256] = kpe
        v_ref[...] = kv[:, 384:768].astype(BF16)

    return pl.pallas_call(
        body, name="a_up", grid=(t // TM,),
        in_specs=[_row_spec(768), _full_spec((1, 384)), _full_spec((1, 256)), _full_spec((384, 768)),
                  _full_spec((256, 768)), _row_spec(LANE), _row_spec(LANE), _row_spec(LANE)],
        out_specs=[_row_spec(768), _row_spec(768), _row_spec(384)],
        out_shape=[jax.ShapeDtypeStruct((t, 768), BF16), jax.ShapeDtypeStruct((t, 768), BF16),
                   jax.ShapeDtypeStruct((t, 384), BF16)],
        compiler_params=_params(("parallel",)),
    )(za, gq, gkv, w_uq_p, w_ukv_p, cos, sina, sinb)


def _tri(n, upper):
    r = lax.broadcasted_iota(jnp.int32, (n, n), 0)
    c = lax.broadcasted_iota(jnp.int32, (n, n), 1)
    return jnp.where((c >= r) if upper else (c <= r), 1.0, 0.0).astype(BF16)


def _forget_fwd(zf, fb, nb, seq):
    blk = 256

    def body(zf_ref, fb_ref, f_ref):
        tri = _tri(blk, False)
        live = _lane_iota((blk, LANE)) < 5
        carry = jnp.zeros((1, LANE), F32)
        for i in range(seq // blk):
            u = zf_ref[i * blk:(i + 1) * blk, :] + fb_ref[...]
            lf = jnp.where(live, jnp.minimum(u, 0.0) - jnp.log(1.0 + jnp.exp(-jnp.abs(u))), 0.0)
            hi, mid, lo = _split3(lf)
            f_ref[i * blk:(i + 1) * blk, :] = _dot(tri, hi) + _dot(tri, mid) + _dot(tri, lo) + carry
            carry = carry + jnp.sum(lf, axis=0, keepdims=True)

    return pl.pallas_call(
        body, name="forget_fwd", grid=(nb,),
        in_specs=[pl.BlockSpec((seq, LANE), lambda b: (b, 0)), pl.BlockSpec((1, LANE), lambda b: (0, 0))],
        out_specs=pl.BlockSpec((seq, LANE), lambda b: (b, 0)),
        out_shape=jax.ShapeDtypeStruct((nb * seq, LANE), F32), compiler_params=_params(("parallel",)),
    )(zf, fb)


def _forget_bwd(df, zf, fb, nb, seq):
    blk = 256

    def body(df_ref, zf_ref, fb_ref, dz_ref, gb_ref):
        @pl.when(pl.program_id(0) == 0)
        def _():
            gb_ref[...] = jnp.zeros_like(gb_ref)

        tri = _tri(blk, True)
        live = _lane_iota((blk, LANE)) < 5
        carry = jnp.zeros((1, LANE), F32)
        gsum = jnp.zeros((1, LANE), F32)
        for i in reversed(range(seq // blk)):
            d = df_ref[i * blk:(i + 1) * blk, :]
            hi, mid, lo = _split3(d)
            dlf = _dot(tri, hi) + _dot(tri, mid) + _dot(tri, lo) + carry
            carry = carry + jnp.sum(d, axis=0, keepdims=True)
            u = zf_ref[i * blk:(i + 1) * blk, :] + fb_ref[...]
            du = jnp.where(live, dlf * jax.nn.sigmoid(-u), 0.0)
            dz_ref[i * blk:(i + 1) * blk, :] = du.astype(BF16)
            gsum = gsum + jnp.sum(du, axis=0, keepdims=True)
        gb_ref[...] += jnp.broadcast_to(gsum, gb_ref.shape)

    return pl.pallas_call(
        body, name="forget_bwd", grid=(nb,),
        in_specs=[pl.BlockSpec((seq, LANE), lambda b: (b, 0)), pl.BlockSpec((seq, LANE), lambda b: (b, 0)),
                  pl.BlockSpec((1, LANE), lambda b: (0, 0))],
        out_specs=[pl.BlockSpec((seq, LANE), lambda b: (b, 0)), pl.BlockSpec((8, LANE), lambda b: (0, 0))],
        out_shape=[jax.ShapeDtypeStruct((nb * seq, LANE), BF16), jax.ShapeDtypeStruct((8, LANE), F32)],
        compiler_params=_params(("arbitrary",)),
    )(df, zf, fb)


def _gate_outproj(x, gate, oa, ob, oc, zg, w_out_p, seq):
    t = x.shape[0]

    def body(x_ref, gate_ref, oa_ref, ob_ref, oc_ref, zg_ref, w_ref, y_ref, xn_ref):
        y = jnp.zeros((TM, D_MODEL), F32)
        for i, o_ref in enumerate((oa_ref, ob_ref, oc_ref)):
            cat = (o_ref[...] * _silu(zg_ref[:, 384 * i:384 * (i + 1)])).astype(BF16)
            y = y + _dot(cat, w_ref[384 * i:384 * (i + 1), :])
        y_ref[...] = y
        xn_ref[...] = x_ref[...] + gate_ref[0] * y

    return pl.pallas_call(
        body, name="gate_outproj", grid=(t // TM,),
        in_specs=[_row_spec(D_MODEL), _ex_spec(seq // TM), _row_spec(384), _row_spec(384), _row_spec(384),
                  _row_spec(D_CAT), _full_spec((D_CAT, D_MODEL))],
        out_specs=[_row_spec(D_MODEL), _row_spec(D_MODEL)],
        out_shape=[jax.ShapeDtypeStruct((t, D_MODEL), F32)] * 2, compiler_params=_params(("parallel",)),
    )(x, gate, oa, ob, oc, zg, w_out_p)


def _final_loss(x, target, g):
    t = x.shape[0]

    def body(x_ref, t_ref, g_ref, dx_ref, loss_ref, gg_ref):
        @pl.when(pl.program_id(0) == 0)
        def _():
            loss_ref[...] = jnp.zeros_like(loss_ref)
            gg_ref[...] = jnp.zeros_like(gg_ref)

        gv = g_ref[...]
        out, xh, r = _rms(x_ref[...], gv)
        err = out - t_ref[...]
        loss_ref[...] += 0.5 * jnp.sum(jnp.mean(err * err, axis=-1, keepdims=True), axis=0, keepdims=True)
        dout = err / D_MODEL
        gg_ref[...] += jnp.broadcast_to(jnp.sum(dout * xh, axis=0, keepdims=True), gg_ref.shape)
        dx_ref[...] = _rms_bwd(dout, xh, r, gv)

    return pl.pallas_call(
        body, name="final_loss", grid=(t // TM,),
        in_specs=[_row_spec(D_MODEL), _row_spec(D_MODEL), _full_spec((1, D_MODEL))],
        out_specs=[_row_spec(D_MODEL), _full_spec((8, LANE)), _full_spec((8, D_MODEL))],
        out_shape=[jax.ShapeDtypeStruct((t, D_MODEL), F32), jax.ShapeDtypeStruct((8, LANE), F32),
                   jax.ShapeDtypeStruct((8, D_MODEL), F32)],
        compiler_params=_params(("arbitrary",)),
    )(x, target, g)


def _head_masks(kind, rows, dq, h):
    lq = _lane_iota((rows, dq))
    lv = _lane_iota((rows, LANE))
    mq = (lq >= 64 * h) & (lq < 64 * h + 64)
    if kind == "A":
        mq = mq | ((lq >= 128 + 32 * h) & (lq < 160 + 32 * h))
    return mq, (lv >= 64 * h) & (lv < 64 * h + 64)


def _tile_mask(kind, tile, m=None):
    row = lax.broadcasted_iota(jnp.int32, (tile, tile), 0)
    col = lax.broadcasted_iota(jnp.int32, (tile, tile), 1)
    if kind == "A":
        return (col >> 6) <= (row >> 6)
    if kind == "C":
        return col <= row
    first = (m == 0) & (row >= 64) & (col < 64)
    last = (m == BAND_TILES - 1) & (row < 64) & (col >= 64)
    return jnp.logical_not(first | last)


def _attn_scale(kind):
    return 96.0 ** -0.5 if kind == "A" else 0.125


def _attn_fwd(kind, q, k, v, aux, nb, seq):
    dq = q.shape[1] // 3
    tile = T_BAND if kind == "B" else T_CAUSAL
    nq = seq // tile
    scale = _attn_scale(kind)

    def body(*refs):
        if kind == "A":
            q_ref, k_ref, v_ref, o_ref, lse_ref = refs
            aux_ref = None
        else:
            q_ref, k_ref, v_ref, aux_ref, o_ref, lse_ref = refs
        qi = pl.program_id(2)
        q2 = q_ref[...]
        res = []
        for h in range(2):
            mq, _ = _head_masks(kind, tile, dq, h)
            qh = jnp.where(mq, q2, jnp.zeros_like(q2))

            def step(kj, carry, m=None, diag=False):
                mx, l, acc = carry
                ks = pl.ds(pl.multiple_of(kj * tile, tile), tile)
                kt = k_ref[ks, :]
                vt = v_ref[ks, :]
                s = _dot_nt(qh, kt) * scale
                if kind == "B":
                    s = jnp.where(_tile_mask("B", tile, m), s + aux_ref[h, m], NEG)
                if kind == "C":
                    s = s - aux_ref[0, h, pl.ds(kj, 1), :]
                if diag:
                    s = jnp.where(_tile_mask(kind, tile), s, NEG)
                mn = jnp.maximum(mx, jnp.max(s, axis=-1, keepdims=True))
                alpha = jnp.exp(mx - mn)
                p = jnp.exp(s - mn)
                l = alpha * l + jnp.sum(p, axis=-1, keepdims=True)
                acc = alpha * acc + _dot(p.astype(BF16), vt)
                return mn, l, acc

            init = (jnp.full((tile, 1), NEG, F32), jnp.zeros((tile, 1), F32), jnp.zeros((tile, LANE), F32))
            if kind == "B":
                m0 = jnp.maximum(BAND_TILES - 1 - qi, 0)
                mx, l, acc = lax.fori_loop(m0, BAND_TILES, lambda m, c: step(qi - (BAND_TILES - 1) + m, c, m=m), init)
            else:
                carry = lax.fori_loop(0, qi, lambda kj, c: step(kj, c), init)
                mx, l, acc = step(qi, carry, diag=True)
            res.append((acc / l, mx + jnp.log(l)))
        first = _lane_iota((tile, LANE)) < 64
        o_ref[...] = jnp.where(first, res[0][0], res[1][0])
        lse_ref[...] = jnp.where(first, res[0][1], res[1][1])

    in_specs = [pl.BlockSpec((tile, dq), lambda b, p, i: (b * nq + i, p)),
                pl.BlockSpec((seq, dq), lambda b, p, i: (b, p)),
                pl.BlockSpec((seq, LANE), lambda b, p, i: (b, p))]
    args = [q, k, v]
    if kind == "B":
        in_specs.append(pl.BlockSpec((2, BAND_TILES, tile, tile), lambda b, p, i: (p, 0, 0, 0)))
        args.append(aux)
    if kind == "C":
        in_specs.append(pl.BlockSpec((1, 2, nq, tile), lambda b, p, i: (b, p, 0, 0)))
        args.append(aux)
    out_spec = pl.BlockSpec((tile, LANE), lambda b, p, i: (b * nq + i, p))
    return pl.pallas_call(
        body, name="attn_fwd_" + kind, grid=(nb, 3, nq), in_specs=in_specs, out_specs=[out_spec, out_spec],
        out_shape=[jax.ShapeDtypeStruct((nb * seq, 384), F32)] * 2,
        compiler_params=_params(("parallel", "parallel", "parallel")),
    )(*args)


def _attn_bwd(kind, q, k, v, o, do, lse, aux, nb, seq):
    dq = q.shape[1] // 3
    tile = T_BAND if kind == "B" else T_CAUSAL
    nq = seq // tile
    scale = _attn_scale(kind)
    dqk_dtype = F32 if kind == "A" else BF16

    def body(*refs):
        dfr_ref = dfq_ref = dbt_ref = aux_ref = None
        if kind == "A":
            q_ref, k_ref, v_ref, o_ref, do_ref, lse_ref, dq_ref, dk_ref, dv_ref, dk_acc, dv_acc = refs
        elif kind == "B":
            q_ref, k_ref, v_ref, o_ref, do_ref, lse_ref, aux_ref, dq_ref, dk_ref, dv_ref, dbt_ref, dk_acc, dv_acc = refs
        else:
            (q_ref, k_ref, v_ref, o_ref, do_ref, lse_ref, aux_ref, dq_ref, dk_ref, dv_ref, dfr_ref, dfq_ref,
             dk_acc, dv_acc) = refs
        dk_acc[...] = jnp.zeros_like(dk_acc)
        dv_acc[...] = jnp.zeros_like(dv_acc)
        if kind == "C":
            dfr_ref[...] = jnp.zeros_like(dfr_ref)
        if kind == "B":
            @pl.when(pl.program_id(1) == 0)
            def _():
                dbt_ref[...] = jnp.zeros_like(dbt_ref)

        def q_step(qi, _):
            qs = pl.ds(pl.multiple_of(qi * tile, tile), tile)
            q2 = q_ref[qs, :]
            do2 = do_ref[qs, :]
            o2 = o_ref[qs, :]
            lse2 = lse_ref[qs, :]
            dq_tot = jnp.zeros((tile, dq), F32)
            row_sums = []
            for h in range(2):
                mq, mv = _head_masks(kind, tile, dq, h)
                qh = jnp.where(mq, q2, jnp.zeros_like(q2))
                doh = jnp.where(mv, do2, 0.0)
                dob = doh.astype(BF16)
                delta = jnp.sum(doh * o2, axis=-1, keepdims=True)
                lseh = jnp.max(jnp.where(mv, lse2, NEG), axis=-1, keepdims=True)

                def step(kj, carry, m=None, diag=False):
                    dq_t, rs = carry
                    ks = pl.ds(pl.multiple_of(kj * tile, tile), tile)
                    kt = k_ref[ks, :]
                    vt = v_ref[ks, :]
                    s = _dot_nt(qh, kt) * scale
                    if kind == "B":
                        s = jnp.where(_tile_mask("B", tile, m), s + aux_ref[h, m], NEG)
                    if kind == "C":
                        s = s - aux_ref[0, h, pl.ds(kj, 1), :]
                    if diag:
                        s = jnp.where(_tile_mask(kind, tile), s, NEG)
                    p = jnp.exp(s - lseh)
                    ds = p * (_dot_nt(dob, vt) - delta)
                    if kind == "B":
                        dbt_ref[h, m] += ds
                    if kind == "C":
                        dfr_ref[0, h, pl.ds(kj, 1), :] -= jnp.sum(ds, axis=0, keepdims=True)
                        rs = rs + jnp.sum(ds, axis=-1, keepdims=True)
                    dss = (ds * scale).astype(BF16)
                    dv_acc[ks, :] += _dot_tn(p.astype(BF16), dob)
                    dk_acc[ks, :] += _dot_tn(dss, qh)
                    return dq_t + jnp.where(mq, _dot(dss, kt), 0.0), rs

                zero = (jnp.zeros((tile, dq), F32), jnp.zeros((tile, 1), F32))
                if kind == "B":
                    m0 = jnp.maximum(BAND_TILES - 1 - qi, 0)
                    dq_h, rs_h = lax.fori_loop(m0, BAND_TILES, lambda m, c: step(qi - (BAND_TILES - 1) + m, c, m=m), zero)
                else:
                    carry = lax.fori_loop(0, qi, lambda kj, c: step(kj, c), zero)
                    dq_h, rs_h = step(qi, carry, diag=True)
                dq_tot = dq_tot + dq_h
                row_sums.append(rs_h)
            dq_ref[qs, :] = dq_tot.astype(dqk_dtype)
            if kind == "C":
                dfq_ref[qs, :] = jnp.where(_lane_iota((tile, LANE)) < 64, row_sums[0], row_sums[1])
            return 0

        lax.fori_loop(0, nq, q_step, 0)
        dk_ref[...] = dk_acc[...].astype(dqk_dtype)
        dv_ref[...] = dv_acc[...].astype(BF16)

    def seq_spec(cols):
        return pl.BlockSpec((seq, cols), lambda p, b: (b, p))

    in_specs = [seq_spec(dq), seq_spec(dq), seq_spec(LANE), seq_spec(LANE), seq_spec(LANE), seq_spec(LANE)]
    args = [q, k, v, o, do, lse]
    out_specs = [seq_spec(dq), seq_spec(dq), seq_spec(LANE)]
    out_shape = [jax.ShapeDtypeStruct((nb * seq, 3 * dq), dqk_dtype)] * 2 + [jax.ShapeDtypeStruct((nb * seq, 384), BF16)]
    if kind == "B":
        spec = pl.BlockSpec((2, BAND_TILES, tile, tile), lambda p, b: (p, 0, 0, 0))
        in_specs.append(spec)
        args.append(aux)
        out_specs.append(spec)
        out_shape.append(jax.ShapeDtypeStruct((6, BAND_TILES, tile, tile), F32))
    if kind == "C":
        spec = pl.BlockSpec((1, 2, nq, tile), lambda p, b: (b, p, 0, 0))
        in_specs.append(spec)
        args.append(aux)
        out_specs += [spec, seq_spec(LANE)]
        out_shape += [jax.ShapeDtypeStruct((nb, 6, nq, tile), F32), jax.ShapeDtypeStruct((nb * seq, 384), F32)]
    return pl.pallas_call(
        body, name="attn_bwd_" + kind, grid=(3, nb), in_specs=in_specs, out_specs=out_specs, out_shape=out_shape,
        scratch_shapes=[pltpu.VMEM((seq, dq), F32), pltpu.VMEM((seq, LANE), F32)],
        compiler_params=_params(("arbitrary", "arbitrary")),
    )(*args)


BAND_W = BAND_TILES * T_BAND


def _segments(kind, qi, tile):
    r0 = qi * tile
    if kind == "B":
        lo = max(qi - (BAND_TILES - 1), 0) * tile
        return [(lo, r0 + tile, False, lo - (qi - (BAND_TILES - 1)) * tile)]
    return ([(0, r0, False, 0)] if qi else []) + [(r0, r0 + tile, True, 0)]


def _scores(kind, qh, k_ref, aux_ref, h, seg, tile, scale):
    a, b, diag, c0 = seg
    s = _dot_nt(qh, k_ref[a:b, :]) * scale
    if kind == "B":
        row = lax.broadcasted_iota(jnp.int32, (tile, b - a), 0)
        col = lax.broadcasted_iota(jnp.int32, (tile, b - a), 1) + c0
        hidden = ((row >= 64) & (col < 64)) | ((row < 64) & (col >= BAND_W - 64))
        return jnp.where(hidden, NEG, s + aux_ref[h, :, c0:BAND_W])
    if kind == "C":
        s = s - aux_ref[0, h, :, a:b]
    if diag:
        s = jnp.where(_tile_mask(kind, tile), s, NEG)
    return s


def _attn_fwd(kind, q, k, v, aux, nb, seq):
    dq = q.shape[1] // 3
    tile = T_BAND if kind == "B" else T_CAUSAL
    nq = seq // tile
    scale = _attn_scale(kind)

    def body(*refs):
        if kind == "A":
            q_ref, k_ref, v_ref, o_ref, lse_ref = refs
            aux_ref = None
        else:
            q_ref, k_ref, v_ref, aux_ref, o_ref, lse_ref = refs
        first = _lane_iota((tile, LANE)) < 64
        for qi in range(nq):
            rows = slice(qi * tile, (qi + 1) * tile)
            q2 = q_ref[rows, :]
            res = []
            for h in range(2):
                mq, _ = _head_masks(kind, tile, dq, h)
                qh = jnp.where(mq, q2, jnp.zeros_like(q2))
                segs = _segments(kind, qi, tile)
                ss = [_scores(kind, qh, k_ref, aux_ref, h, seg, tile, scale) for seg in segs]
                mx = functools.reduce(jnp.maximum, [jnp.max(s, axis=-1, keepdims=True) for s in ss])
                ps = [jnp.exp(s - mx) for s in ss]
                l = functools.reduce(jnp.add, [jnp.sum(p, axis=-1, keepdims=True) for p in ps])
                acc = functools.reduce(jnp.add, [_dot(p.astype(BF16), v_ref[seg[0]:seg[1], :]) for p, seg in zip(ps, segs)])
                res.append((acc / l, mx + jnp.log(l)))
            o_ref[rows, :] = jnp.where(first, res[0][0], res[1][0])
            lse_ref[rows, :] = jnp.where(first, res[0][1], res[1][1])

    def seq_spec(cols):
        return pl.BlockSpec((seq, cols), lambda b, p: (b, p))

    in_specs = [seq_spec(dq), seq_spec(dq), seq_spec(LANE)]
    args = [q, k, v]
    if kind == "B":
        in_specs.append(pl.BlockSpec((2, tile, BAND_W), lambda b, p: (p, 0, 0)))
        args.append(aux)
    if kind == "C":
        in_specs.append(pl.BlockSpec((1, 2, 1, seq), lambda b, p: (b, p, 0, 0)))
        args.append(aux)
    return pl.pallas_call(
        body, name="attn_fwd_" + kind, grid=(nb, 3), in_specs=in_specs, out_specs=[seq_spec(LANE), seq_spec(LANE)],
        out_shape=[jax.ShapeDtypeStruct((nb * seq, 384), F32)] * 2, compiler_params=_params(("parallel", "parallel")),
    )(*args)


def _attn_bwd(kind, q, k, v, o, do, lse, aux, nb, seq):
    dq = q.shape[1] // 3
    tile = T_BAND if kind == "B" else T_CAUSAL
    nq = seq // tile
    scale = _attn_scale(kind)
    dqk_dtype = F32 if kind == "A" else BF16

    def body(*refs):
        dfr_ref = dfq_ref = dbt_ref = aux_ref = None
        if kind == "A":
            q_ref, k_ref, v_ref, o_ref, do_ref, lse_ref, dq_ref, dk_ref, dv_ref, dkt_acc, dvt_acc = refs
        elif kind == "B":
            q_ref, k_ref, v_ref, o_ref, do_ref, lse_ref, aux_ref, dq_ref, dk_ref, dv_ref, dbt_ref, dkt_acc, dvt_acc = refs
        else:
            (q_ref, k_ref, v_ref, o_ref, do_ref, lse_ref, aux_ref, dq_ref, dk_ref, dv_ref, dfr_ref, dfq_ref,
             dkt_acc, dvt_acc) = refs
        dkt_acc[...] = jnp.zeros_like(dkt_acc)
        dvt_acc[...] = jnp.zeros_like(dvt_acc)
        if kind == "C":
            dfr_ref[...] = jnp.zeros_like(dfr_ref)
        if kind == "B":
            @pl.when(pl.program_id(1) == 0)
            def _():
                dbt_ref[...] = jnp.zeros_like(dbt_ref)

        first = _lane_iota((tile, LANE)) < 64
        for qi in range(nq):
            rows = slice(qi * tile, (qi + 1) * tile)
            q2 = q_ref[rows, :]
            do2 = do_ref[rows, :]
            o2 = o_ref[rows, :]
            lse2 = lse_ref[rows, :]
            dq_tot = jnp.zeros((tile, dq), F32)
            row_sums = []
            for h in range(2):
                mq, mv = _head_masks(kind, tile, dq, h)
                qh = jnp.where(mq, q2, jnp.zeros_like(q2))
                doh = jnp.where(mv, do2, 0.0)
                dob = doh.astype(BF16)
                qht = qh.astype(F32).T.astype(BF16)
                dobt = doh.T.astype(BF16)
                delta = jnp.sum(doh * o2, axis=-1, keepdims=True)
                lseh = jnp.max(jnp.where(mv, lse2, NEG), axis=-1, keepdims=True)
                rs = jnp.zeros((tile, 1), F32)
                for seg in _segments(kind, qi, tile):
                    a, b, _, c0 = seg
                    p = jnp.exp(_scores(kind, qh, k_ref, aux_ref, h, seg, tile, scale) - lseh)
                    ds = p * (_dot_nt(dob, v_ref[a:b, :]) - delta)
                    if kind == "B":
                        dbt_ref[h, :, c0:BAND_W] += ds
                    if kind == "C":
                        dfr_ref[0, h, :, a:b] -= jnp.sum(ds, axis=0, keepdims=True)
                        rs = rs + jnp.sum(ds, axis=-1, keepdims=True)
                    dss = (ds * scale).astype(BF16)
                    dvt_acc[:, a:b] += _dot(dobt, p.astype(BF16))
                    dkt_acc[:, a:b] += _dot(qht, dss)
                    dq_tot = dq_tot + jnp.where(mq, _dot(dss, k_ref[a:b, :]), 0.0)
                row_sums.append(rs)
            dq_ref[rows, :] = dq_tot.astype(dqk_dtype)
            if kind == "C":
                dfq_ref[rows, :] = jnp.where(first, row_sums[0], row_sums[1])
        for j in range(seq // 256):
            cols = slice(256 * j, 256 * (j + 1))
            dk_ref[cols, :] = dkt_acc[:, cols].T.astype(dqk_dtype)
            dv_ref[cols, :] = dvt_acc[:, cols].T.astype(BF16)

    def seq_spec(cols):
        return pl.BlockSpec((seq, cols), lambda p, b: (b, p))

    in_specs = [seq_spec(dq), seq_spec(dq), seq_spec(LANE), seq_spec(LANE), seq_spec(LANE), seq_spec(LANE)]
    args = [q, k, v, o, do, lse]
    out_specs = [seq_spec(dq), seq_spec(dq), seq_spec(LANE)]
    out_shape = [jax.ShapeDtypeStruct((nb * seq, 3 * dq), dqk_dtype)] * 2 + [jax.ShapeDtypeStruct((nb * seq, 384), BF16)]
    if kind == "B":
        spec = pl.BlockSpec((2, tile, BAND_W), lambda p, b: (p, 0, 0))
        in_specs.append(spec)
        args.append(aux)
        out_specs.append(spec)
        out_shape.append(jax.ShapeDtypeStruct((6, tile, BAND_W), F32))
    if kind == "C":
        spec = pl.BlockSpec((1, 2, 1, seq), lambda p, b: (b, p, 0, 0))
        in_specs.append(spec)
        args.append(aux)
        out_specs += [spec, seq_spec(LANE)]
        out_shape += [jax.ShapeDtypeStruct((nb, 6, 1, seq), F32), jax.ShapeDtypeStruct((nb * seq, 384), F32)]
    return pl.pallas_call(
        body, name="attn_bwd_" + kind, grid=(3, nb), in_specs=in_specs, out_specs=out_specs, out_shape=out_shape,
        scratch_shapes=[pltpu.VMEM((dq, seq), F32), pltpu.VMEM((LANE, seq), F32)],
        compiler_params=_params(("arbitrary", "arbitrary")),
    )(*args)


BIAS_FLAT = T_BAND * BAND_W
BIAS_CHUNK = 4 * BAND_W


def _rel_onehot(chunk):
    lane = lax.broadcasted_iota(jnp.int32, (384, chunk), 1)
    r = lax.broadcasted_iota(jnp.int32, (384, chunk), 0)
    sub = jnp.where(lane >= BAND_W, 1, 0) + jnp.where(lane >= 2 * BAND_W, 1, 0) + jnp.where(lane >= 3 * BAND_W, 1, 0)
    i = pl.program_id(0) * 4 + sub
    col = lane - sub * BAND_W
    idx = jnp.clip(BAND_W - T_BAND + i - col, -REL_CLIP, REL_CLIP) + REL_CLIP
    return jnp.where(idx == r, 1.0, 0.0).astype(BF16)


def _bias_expand(rb8):
    def body(rb_ref, out_ref):
        oh = _rel_onehot(BIAS_CHUNK)
        hi, mid, lo = _split3(rb_ref[...])
        out_ref[...] = _dot(hi, oh) + _dot(mid, oh) + _dot(lo, oh)

    return pl.pallas_call(
        body, name="bias_expand", grid=(BIAS_FLAT // BIAS_CHUNK,),
        in_specs=[pl.BlockSpec((8, 384), lambda i: (0, 0))], out_specs=pl.BlockSpec((8, BIAS_CHUNK), lambda i: (0, i)),
        out_shape=jax.ShapeDtypeStruct((8, BIAS_FLAT), F32), compiler_params=_params(("parallel",)),
    )(rb8)


def _bias_reduce(dbt8):
    def body(d_ref, out_ref):
        @pl.when(pl.program_id(0) == 0)
        def _():
            out_ref[...] = jnp.zeros_like(out_ref)

        oh = _rel_onehot(BIAS_CHUNK)
        hi, mid, lo = _split3(d_ref[...])
        out_ref[...] += _dot_nt(hi, oh) + _dot_nt(mid, oh) + _dot_nt(lo, oh)

    return pl.pallas_call(
        body, name="bias_reduce", grid=(BIAS_FLAT // BIAS_CHUNK,),
        in_specs=[pl.BlockSpec((8, BIAS_CHUNK), lambda i: (0, i))], out_specs=pl.BlockSpec((8, 384), lambda i: (0, 0)),
        out_shape=jax.ShapeDtypeStruct((8, 384), F32), compiler_params=_params(("arbitrary",)),
    )(dbt8)


def _outproj_bwd(dxn, y, gate, oa, ob, oc, zg, w_out_p, w_out_pt, nb, seq):
    t = dxn.shape[0]
    tpe = seq // TM

    def body(dxn_ref, y_ref, gate_ref, oa_ref, ob_ref, oc_ref, zg_ref, w_ref, wt_ref,
             doa_ref, dob_ref, doc_ref, dzg_ref, gw_ref, dgate_ref):
        i = pl.program_id(0)

        @pl.when(i == 0)
        def _():
            gw_ref[...] = jnp.zeros_like(gw_ref)

        @pl.when(i % tpe == 0)
        def _():
            dgate_ref[...] = jnp.zeros_like(dgate_ref)

        dxn_t = dxn_ref[...]
        dgate_ref[0] += jnp.sum(dxn_t * y_ref[...], axis=0, keepdims=True)
        dy = (dxn_t * gate_ref[0]).astype(BF16)
        for gi, (o_ref, do_ref) in enumerate(((oa_ref, doa_ref), (ob_ref, dob_ref), (oc_ref, doc_ref))):
            cols = slice(384 * gi, 384 * (gi + 1))
            u = zg_ref[:, cols]
            o_t = o_ref[...]
            su = _silu(u)
            dcat = _dot(dy, wt_ref[:, cols])
            do_ref[...] = dcat * su
            dzg_ref[:, cols] = (dcat * o_t * _dsilu(u)).astype(BF16)
            gw_ref[cols, :] += _dot_tn((o_t * su).astype(BF16), dy)

    return pl.pallas_call(
        body, name="outproj_bwd", grid=(t // TM,),
        in_specs=[_row_spec(D_MODEL), _row_spec(D_MODEL), _ex_spec(tpe), _row_spec(384), _row_spec(384), _row_spec(384),
                  _row_spec(D_CAT), _full_spec((D_CAT, D_MODEL)), _full_spec((D_MODEL, D_CAT))],
        out_specs=[_row_spec(384), _row_spec(384), _row_spec(384), _row_spec(D_CAT), _full_spec((D_CAT, D_MODEL)),
                   _ex_spec(tpe)],
        out_shape=[jax.ShapeDtypeStruct((t, 384), F32)] * 3 + [jax.ShapeDtypeStruct((t, D_CAT), BF16),
                                                                jax.ShapeDtypeStruct((D_CAT, D_MODEL), F32),
                                                                jax.ShapeDtypeStruct((nb, 1, D_MODEL), F32)],
        compiler_params=_params(("arbitrary",)),
    )(dxn, y, gate, oa, ob, oc, zg, w_out_p, w_out_pt)


def _a_up_bwd(dqa, dka, dva, za, gq, gkv, w_uq_pt, w_ukv_pt, cos, sina, sinb):
    t = za.shape[0]

    def body(dq_ref, dk_ref, dv_ref, za_ref, gq_ref, gkv_ref, wqt_ref, wkvt_ref, cos_ref, sa_ref, sb_ref,
             dza_ref, gwq_ref, gwkv_ref, ggq_ref, ggkv_ref, dqb, dkvb):
        @pl.when(pl.program_id(0) == 0)
        def _():
            gwq_ref[...] = jnp.zeros_like(gwq_ref)
            gwkv_ref[...] = jnp.zeros_like(gwkv_ref)
            ggq_ref[...] = jnp.zeros_like(ggq_ref)
            ggkv_ref[...] = jnp.zeros_like(ggkv_ref)

        cos_t, sa, sb = cos_ref[...], sa_ref[...], sb_ref[...]
        dkpe = jnp.zeros((TM, LANE), F32)
        for p in range(3):
            dqb[:, 256 * p:256 * p + 128] = dq_ref[:, 256 * p:256 * p + 128].astype(BF16)
            dqb[:, 256 * p + 128:256 * p + 256] = _rope_t(dq_ref[:, 256 * p + 128:256 * p + 256], cos_t, sa, sb).astype(BF16)
            dkvb[:, 128 * p:128 * p + 128] = dk_ref[:, 256 * p:256 * p + 128].astype(BF16)
            dkpe = dkpe + dk_ref[:, 256 * p + 128:256 * p + 256]
        dkvb[:, 384:768] = dv_ref[...]
        dkpe = _rope_t(dkpe, cos_t, sa, sb)
        dkpe = jnp.where(_lane_iota((TM, LANE)) < A_ROPE, dkpe + pltpu.roll(dkpe, LANE - 32, 1), 0.0)

        gqv = gq_ref[...]
        cqn, cqh, rq = _rms(za_ref[:, 0:384], gqv)
        dq_t = dqb[...]
        gwq_ref[...] += _dot_tn(cqn.astype(BF16), dq_t)
        dcqn = _dot(dq_t, wqt_ref[...])
        ggq_ref[...] += jnp.broadcast_to(jnp.sum(dcqn * cqh, axis=0, keepdims=True), ggq_ref.shape)
        dza_ref[:, 0:384] = _rms_bwd(dcqn, cqh, rq, gqv).astype(BF16)

        gkvv = gkv_ref[...]
        ckvn, ckvh, rkv = _rms(za_ref[:, 384:640], gkvv)
        dkv_t = dkvb[...]
        gwkv_ref[...] += _dot_tn(ckvn.astype(BF16), dkv_t)
        dckvn = _dot(dkv_t, wkvt_ref[...])
        ggkv_ref[...] += jnp.broadcast_to(jnp.sum(dckvn * ckvh, axis=0, keepdims=True), ggkv_ref.shape)
        dza_ref[:, 384:640] = _rms_bwd(dckvn, ckvh, rkv, gkvv).astype(BF16)
        dza_ref[:, 640:768] = dkpe.astype(BF16)

    return pl.pallas_call(
        body, name="a_up_bwd", grid=(t // TM,),
        in_specs=[_row_spec(768), _row_spec(768), _row_spec(384), _row_spec(768), _full_spec((1, 384)),
                  _full_spec((1, 256)), _full_spec((768, 384)), _full_spec((768, 256)), _row_spec(LANE), _row_spec(LANE),
                  _row_spec(LANE)],
        out_specs=[_row_spec(768), _full_spec((384, 768)), _full_spec((256, 768)), _full_spec((8, 384)),
                   _full_spec((8, 256))],
        out_shape=[jax.ShapeDtypeStruct((t, 768), BF16), jax.ShapeDtypeStruct((384, 768), F32),
                   jax.ShapeDtypeStruct((256, 768), F32), jax.ShapeDtypeStruct((8, 384), F32),
                   jax.ShapeDtypeStruct((8, 256), F32)],
        scratch_shapes=[pltpu.VMEM((TM, 768), BF16), pltpu.VMEM((TM, 768), BF16)],
        compiler_params=_params(("arbitrary",)),
    )(dqa, dka, dva, za, gq, gkv, w_uq_pt, w_ukv_pt, cos, sina, sinb)


def _dz_cols():
    return (Z_A, Z_G) + Z_QKV + (Z_F,)


def _inproj_bwd_dx(dz, dxn, x, shift, scale, g, w_in_pt, nb, seq):
    t = x.shape[0]
    tpe = seq // TM
    cols = _dz_cols()

    def body(*refs):
        dz_refs = refs[:len(cols)]
        dxn_ref, x_ref, sh_ref, sc_ref, g_ref, wt_ref, dx_ref, dsh_ref, dsc_ref, dg_ref = refs[len(cols):]
        i = pl.program_id(0)

        @pl.when(i == 0)
        def _():
            dg_ref[...] = jnp.zeros_like(dg_ref)

        @pl.when(i % tpe == 0)
        def _():
            dsh_ref[...] = jnp.zeros_like(dsh_ref)
            dsc_ref[...] = jnp.zeros_like(dsc_ref)

        dh = jnp.zeros((TM, D_MODEL), F32)
        for ref, (c0, c1) in zip(dz_refs, cols):
            dh = dh + _dot(ref[...], wt_ref[c0:c1, :])
        gv = g_ref[...]
        n, xh, r = _rms(x_ref[...], gv)
        dsh_ref[0] += jnp.sum(dh, axis=0, keepdims=True)
        dsc_ref[0] += jnp.sum(dh * n, axis=0, keepdims=True)
        dn = dh * (1.0 + sc_ref[0])
        dg_ref[...] += jnp.broadcast_to(jnp.sum(dn * xh, axis=0, keepdims=True), dg_ref.shape)
        dx_ref[...] = dxn_ref[...] + _rms_bwd(dn, xh, r, gv)

    in_specs = [_row_spec(c1 - c0) for c0, c1 in cols]
    in_specs += [_row_spec(D_MODEL), _row_spec(D_MODEL), _ex_spec(tpe), _ex_spec(tpe), _full_spec((1, D_MODEL)),
                 _full_spec((NP_IN, D_MODEL))]
    return pl.pallas_call(
        body, name="inproj_bwd_dx", grid=(t // TM,), in_specs=in_specs,
        out_specs=[_row_spec(D_MODEL), _ex_spec(tpe), _ex_spec(tpe), _full_spec((8, D_MODEL))],
        out_shape=[jax.ShapeDtypeStruct((t, D_MODEL), F32), jax.ShapeDtypeStruct((nb, 1, D_MODEL), F32),
                   jax.ShapeDtypeStruct((nb, 1, D_MODEL), F32), jax.ShapeDtypeStruct((8, D_MODEL), F32)],
        compiler_params=_params(("arbitrary",)),
    )(*dz, dxn, x, shift, scale, g, w_in_pt)


def _inproj_bwd_dw(h, dz, name):
    t = h.shape[0]
    widths = [d.shape[1] for d in dz]
    total = sum(widths)

    def body(*refs):
        h_ref = refs[0]
        dz_refs = refs[1:1 + len(dz)]
        gw_ref = refs[1 + len(dz)]

        @pl.when(pl.program_id(0) == 0)
        def _():
            gw_ref[...] = jnp.zeros_like(gw_ref)

        h_t = h_ref[...]
        c0 = 0
        for ref, w in zip(dz_refs, widths):
            gw_ref[:, c0:c0 + w] += _dot_tn(h_t, ref[...])
            c0 += w

    return pl.pallas_call(
        body, name=name, grid=(t // TM,), in_specs=[_row_spec(D_MODEL)] + [_row_spec(w) for w in widths],
        out_specs=_full_spec((D_MODEL, total)), out_shape=jax.ShapeDtypeStruct((D_MODEL, total), F32),
        compiler_params=_params(("arbitrary",)),
    )(h, *dz)


def _ada_fwd(c_all, w_ada, b_cols):
    n = c_all.shape[0]
    cols = w_ada.shape[2]

    def body(c_ref, w_ref, b_ref, out_ref):
        act = _silu(c_ref[...]).astype(BF16)
        out_ref[0] = _dot(act, w_ref[0].astype(BF16)) + b_ref[0]

    return pl.pallas_call(
        body, name="ada_fwd", grid=(DEPTH,),
        in_specs=[pl.BlockSpec((n, D_MODEL), lambda l: (0, 0)), pl.BlockSpec((1, D_MODEL, cols), lambda l: (l, 0, 0)),
                  pl.BlockSpec((1, 1, cols), lambda l: (l, 0, 0))],
        out_specs=pl.BlockSpec((1, n, cols), lambda l: (l, 0, 0)),
        out_shape=jax.ShapeDtypeStruct((DEPTH, n, cols), F32), compiler_params=_params(("parallel",)),
    )(c_all, w_ada, b_cols)


def _ada_bwd(c_all, dmod_cols, dmod_all):
    n = c_all.shape[0]
    cols = dmod_cols.shape[2]

    def body(c_ref, dc_ref, da_ref, gw_ref, gb_ref):
        act = _silu(c_ref[...]).astype(BF16)
        gw_ref[0] = _dot_tn(act, dc_ref[0].astype(BF16))
        gb_ref[0] = jnp.sum(da_ref[0], axis=0, keepdims=True)

    return pl.pallas_call(
        body, name="ada_bwd", grid=(DEPTH,),
        in_specs=[pl.BlockSpec((n, D_MODEL), lambda l: (0, 0)), pl.BlockSpec((1, n, cols), lambda l: (l, 0, 0)),
                  pl.BlockSpec((1, n, 3 * D_MODEL), lambda l: (l, 0, 0))],
        out_specs=[pl.BlockSpec((1, D_MODEL, cols), lambda l: (l, 0, 0)),
                   pl.BlockSpec((1, 1, 3 * D_MODEL), lambda l: (l, 0, 0))],
        out_shape=[jax.ShapeDtypeStruct((DEPTH, D_MODEL, cols), F32), jax.ShapeDtypeStruct((DEPTH, 1, 3 * D_MODEL), F32)],
        compiler_params=_params(("parallel",)),
    )(c_all, dmod_cols, dmod_all)


def _sum_blocks(parts, name):
    n, rows, cols = parts.shape
    tr = rows if rows <= 256 else 8 * next(d for d in range(32, 0, -1) if (rows // 8) % d == 0)

    def body(p_ref, out_ref):
        acc = p_ref[0].astype(F32)
        for k in range(1, n):
            acc = acc + p_ref[k].astype(F32)
        out_ref[...] = acc

    return pl.pallas_call(
        body, name=name, grid=(rows // tr,), in_specs=[pl.BlockSpec((n, tr, cols), lambda i: (0, i, 0))],
        out_specs=pl.BlockSpec((tr, cols), lambda i: (i, 0)), out_shape=jax.ShapeDtypeStruct((rows, cols), F32),
        compiler_params=_params(("parallel",)),
    )(parts)


def _adamw(w, g, m, v, name):
    rows, cols = w.shape
    tr = next(t for t in (rows, 256, 128, 64, 32, 16, 8) if rows % t == 0 and t * cols <= 256 * 1024)

    def body(w_ref, g_ref, m_ref, v_ref, d_ref, mo_ref, vo_ref):
        gv = g_ref[...]
        mn = ADAM_B1 * m_ref[...] + (1.0 - ADAM_B1) * gv
        vn = ADAM_B2 * v_ref[...] + (1.0 - ADAM_B2) * jnp.square(gv)
        m_hat = mn / (1.0 - ADAM_B1 ** ADAM_STEP)
        v_hat = vn / (1.0 - ADAM_B2 ** ADAM_STEP)
        d_ref[...] = -ADAM_LR * (m_hat / (jnp.sqrt(v_hat) + ADAM_EPS) + ADAM_WD * w_ref[...])
        mo_ref[...] = mn
        vo_ref[...] = vn

    spec = pl.BlockSpec((tr, cols), lambda i: (i, 0))
    return pl.pallas_call(
        body, name=name, grid=(rows // tr,), in_specs=[spec] * 4, out_specs=[spec] * 3,
        out_shape=[jax.ShapeDtypeStruct((rows, cols), F32)] * 3, compiler_params=_params(("parallel",)),
    )(w, g, m, v)


ALL_FLIPS = tuple(range(1, N_DEV))


def _exchange(src, flips, mode, name):
    _, rows, cols = src.shape
    nslot = 2 if mode == "pair" else N_DEV
    nf = len(flips)

    def body(src_ref, dst_ref, send_sems, recv_sems, local_sem):
        x, y, c = lax.axis_index("x"), lax.axis_index("y"), lax.axis_index("c")
        me = 4 * x + 2 * y + c

        def slot(j):
            return (j & 1) if mode == "pair" else j

        own = pltpu.make_async_copy(src_ref.at[me if mode == "scatter" else 0], dst_ref.at[slot(me)], local_sem)
        own.start()
        copies = []
        for i, f in enumerate(flips):
            peer = me ^ f
            to = (1 - x if f & 4 else x, 1 - y if f & 2 else y, 1 - c if f & 1 else c)
            cp = pltpu.make_async_remote_copy(
                src_ref=src_ref.at[peer if mode == "scatter" else 0], dst_ref=dst_ref.at[slot(me)],
                send_sem=send_sems.at[i], recv_sem=recv_sems.at[i], device_id=to, device_id_type=pl.DeviceIdType.MESH)
            cp.start()
            copies.append(cp)
        for i, f in enumerate(flips):
            peer = me ^ f
            to = (1 - x if f & 4 else x, 1 - y if f & 2 else y, 1 - c if f & 1 else c)
            pltpu.make_async_remote_copy(
                src_ref=src_ref.at[0], dst_ref=dst_ref.at[slot(peer)], send_sem=send_sems.at[i],
                recv_sem=recv_sems.at[i], device_id=to, device_id_type=pl.DeviceIdType.MESH).wait_recv()
        for cp in copies:
            cp.wait_send()
        own.wait()

    return pl.pallas_call(
        body, name=name, out_shape=jax.ShapeDtypeStruct((nslot, rows, cols), src.dtype),
        in_specs=[pl.BlockSpec(memory_space=pl.ANY)], out_specs=pl.BlockSpec(memory_space=pl.ANY),
        scratch_shapes=[pltpu.SemaphoreType.DMA((nf,)), pltpu.SemaphoreType.DMA((nf,)), pltpu.SemaphoreType.DMA],
    )(src)


def _transfer(name, srcs, dst_shapes, plan):
    n_arr = len(srcs)
    probe = plan(0, 0, 0)
    n_steps = len(probe)

    def body(*refs):
        src_refs, dst_refs = refs[:n_arr], refs[n_arr:2 * n_arr]
        send_sems, recv_sems, local_sems = refs[2 * n_arr:]
        x, y, c = lax.axis_index("x"), lax.axis_index("y"), lax.axis_index("c")
        steps = plan(x, y, c)

        def rows(ref, r0, n):
            return ref.at[:, pl.ds(r0, n), :]

        def arrival(t):
            a, _, _, n, _, f, _ = steps[t]
            return pltpu.make_async_remote_copy(
                src_ref=rows(dst_refs[a], 0, n), dst_ref=rows(dst_refs[a], 0, n), send_sem=send_sems.at[t],
                recv_sem=recv_sems.at[t], device_id=(x, y, c), device_id_type=pl.DeviceIdType.MESH)

        arrived, started = set(), []
        for t, (a, from_dst, sr, n, dr, f, after) in enumerate(steps):
            for u in after:
                if u not in arrived:
                    arrival(u).wait_recv()
                    arrived.add(u)
            src = rows(dst_refs[a] if from_dst else src_refs[a], sr, n)
            dst = rows(dst_refs[a], dr, n)
            if f == 0:
                cp = pltpu.make_async_copy(src, dst, local_sems.at[t])
            else:
                to = (1 - x if f & 4 else x, 1 - y if f & 2 else y, 1 - c if f & 1 else c)
                cp = pltpu.make_async_remote_copy(src_ref=src, dst_ref=dst, send_sem=send_sems.at[t],
                                                  recv_sem=recv_sems.at[t], device_id=to,
                                                  device_id_type=pl.DeviceIdType.MESH)
            cp.start()
            started.append(cp)
        for t, step in enumerate(steps):
            if step[5] != 0 and t not in arrived:
                arrival(t).wait_recv()
        for cp, step in zip(started, steps):
            if step[5] == 0:
                cp.wait()
            else:
                cp.wait_send()

    any_spec = pl.BlockSpec(memory_space=pl.ANY)
    return pl.pallas_call(
        body, name=name, out_shape=[jax.ShapeDtypeStruct(s, d) for s, d in dst_shapes],
        in_specs=[any_spec] * n_arr, out_specs=[any_spec] * n_arr,
        scratch_shapes=[pltpu.SemaphoreType.DMA((n_steps,)), pltpu.SemaphoreType.DMA((n_steps,)),
                        pltpu.SemaphoreType.DMA((n_steps,))],
    )(*srcs)


CHIP_FLIPS = (2, 4, 6)


def _gather_plan(chip_rows):
    def plan(x, y, c):
        steps = []
        for a, rc in enumerate(chip_rows):
            h = rc // 2
            mine = rc * (2 * x + y) + h * c
            steps.append((a, False, h * c, h, mine, 0, ()))
            ici = {}
            for f in CHIP_FLIPS:
                ici[f] = len(steps)
                steps.append((a, False, h * c, h, mine, f, ()))
            steps.append((a, False, h * c, h, mine, 1, ()))
            for f in CHIP_FLIPS:
                theirs = rc * ((2 * x + y) ^ (f >> 1)) + h * c
                steps.append((a, True, theirs, h, theirs, 1, (ici[f],)))
        return steps
    return plan


def _pair_reduce_plan(chip_rows):
    def plan(x, y, c):
        steps = []
        for a, rc in enumerate(chip_rows):
            h = rc // 2
            for j in range(4):
                steps.append((a, False, rc * j + h * c, h, (4 * c + j) * h, 0, ()))
                steps.append((a, False, rc * j + h * (1 - c), h, (4 * c + j) * h, 1, ()))
        return steps
    return plan


def _chip_scatter_plan(chip_rows):
    def plan(x, y, c):
        steps = []
        for a, rc in enumerate(chip_rows):
            h = rc // 2
            me = 2 * x + y
            steps.append((a, False, h * me, h, h * me, 0, ()))
            for f in CHIP_FLIPS:
                steps.append((a, False, h * (me ^ (f >> 1)), h, h * me, f, ()))
        return steps
    return plan


def _pair_share_plan(chip_rows):
    def plan(x, y, c):
        steps = []
        for a, rc in enumerate(chip_rows):
            h = rc // 2
            steps.append((a, False, 0, h, h * c, 0, ()))
            steps.append((a, False, 0, h, h * c, 1, ()))
        return steps
    return plan


def _sum_slots(parts, out_dtype, name):
    nl, n, rows, cols = parts.shape
    tr = next(t for t in (128, 64, 32, 16) if rows % t == 0 and n * t * cols * 4 <= (4 << 20))

    def body(p_ref, out_ref):
        acc = p_ref[0, 0].astype(F32)
        for k in range(1, n):
            acc = acc + p_ref[0, k].astype(F32)
        out_ref[0] = acc.astype(out_dtype)

    return pl.pallas_call(
        body, name=name, grid=(nl, rows // tr),
        in_specs=[pl.BlockSpec((1, n, tr, cols), lambda l, i: (l, 0, i, 0))],
        out_specs=pl.BlockSpec((1, tr, cols), lambda l, i: (l, i, 0)),
        out_shape=jax.ShapeDtypeStruct((nl, rows, cols), out_dtype), compiler_params=_params(("parallel", "parallel")),
    )(parts)


def _pad_cols(a, n):
    return a if n == 0 else jnp.pad(a, ((0, 0), (0, n)))


def _in_to_padded(w):
    return jnp.concatenate([_pad_cols(w[:, a:b], z) for a, b, z in IN_PIECES], axis=1)


def _in_from_padded(gp):
    pos, out = 0, {}
    for a, b, z in IN_PIECES:
        out[a] = gp[:, pos:pos + (b - a)]
        pos += (b - a) + z
    return jnp.concatenate([out[a] for a in sorted(out)], axis=1)


def _out_to_padded(w):
    z = jnp.zeros((64, w.shape[1]), w.dtype)
    return jnp.concatenate([w[0:384], w[384:704], z, w[704:1024], z], axis=0)


def _out_from_padded(gp):
    return jnp.concatenate([gp[0:384], gp[384:704], gp[768:1088]], axis=0)


def _uq_to_padded(w):
    parts = []
    for p in range(3):
        h0, h1 = 2 * p, 2 * p + 1
        parts += [w[:, 96 * h0:96 * h0 + 64], w[:, 96 * h1:96 * h1 + 64], w[:, 96 * h0 + 64:96 * h0 + 96],
                  w[:, 96 * h1 + 64:96 * h1 + 96], jnp.zeros((w.shape[0], 64), w.dtype)]
    return jnp.concatenate(parts, axis=1)


def _uq_from_padded(gp):
    parts = []
    for h in range(6):
        p, s = h // 2, h % 2
        parts += [gp[:, 256 * p + 64 * s:256 * p + 64 * s + 64], gp[:, 256 * p + 128 + 32 * s:256 * p + 160 + 32 * s]]
    return jnp.concatenate(parts, axis=1)


def _ukv_to_padded(w):
    return jnp.concatenate([w[:, 128 * h:128 * h + 64] for h in range(6)]
                           + [w[:, 128 * h + 64:128 * h + 128] for h in range(6)], axis=1)


def _ukv_from_padded(gp):
    parts = []
    for h in range(6):
        parts += [gp[:, 64 * h:64 * h + 64], gp[:, 384 + 64 * h:384 + 64 * h + 64]]
    return jnp.concatenate(parts, axis=1)


LR_ROWS = 224
SHARD_ROWS = (256, 256, LR_ROWS)


def _pack_lowrank(w_uq, w_ukv):
    flat = jnp.concatenate([w_uq.reshape(-1), w_ukv.reshape(-1)])
    return jnp.pad(flat, (0, LR_ROWS * PACK_COLS - flat.shape[0])).reshape(1, LR_ROWS, PACK_COLS)


def _unpack_lowrank(packed):
    flat = packed.reshape(-1)
    n_uq = DEPTH * A_Q_RANK * 144
    n_ukv = DEPTH * A_KV_RANK * 192
    return flat[0:n_uq].reshape(DEPTH, A_Q_RANK, 144), flat[n_uq:n_uq + n_ukv].reshape(DEPTH, A_KV_RANK, 192)


PACK_SIZES = (DEPTH * 256 * N_IN, DEPTH * 256 * D_MODEL, DEPTH * A_Q_RANK * 144, DEPTH * A_KV_RANK * 192)


def _pack(w_in, w_out, w_uq, w_ukv):
    flat = jnp.concatenate([w_in.reshape(-1), w_out.reshape(-1), w_uq.reshape(-1), w_ukv.reshape(-1)])
    flat = jnp.pad(flat, (0, PACK_ROWS * PACK_COLS - flat.shape[0]))
    return flat.reshape(PACK_ROWS, PACK_COLS)


def _unpack(packed):
    flat = packed.reshape(-1)
    o0, o1, o2, o3 = PACK_SIZES
    w_in = flat[0:o0].reshape(DEPTH, 256, N_IN)
    w_out = flat[o0:o0 + o1].reshape(DEPTH, 256, D_MODEL)
    w_uq = flat[o0 + o1:o0 + o1 + o2].reshape(DEPTH, A_Q_RANK, 144)
    w_ukv = flat[o0 + o1 + o2:o0 + o1 + o2 + o3].reshape(DEPTH, A_KV_RANK, 192)
    return w_in, w_out, w_uq, w_ukv


def _rope_tables(positions):
    inv = ROPE_THETA ** (-jnp.arange(0, A_ROPE, 2, dtype=F32) / A_ROPE)
    ang = positions.astype(F32)[..., None] * inv
    cos = jnp.tile(jnp.cos(ang), (1, 1, 4)).reshape(-1, 64)
    sin = jnp.tile(jnp.sin(ang), (1, 1, 4)).reshape(-1, 64)
    second = (jnp.arange(64) % 32) >= 16
    pad = ((0, 0), (0, 64))
    return (jnp.pad(cos, pad), jnp.pad(jnp.where(second, 0.0, -sin), pad), jnp.pad(jnp.where(second, sin, 0.0), pad))


def _rows(a, n):
    flat = a.reshape(-1)
    return jnp.pad(flat, (0, n * LANE - flat.shape[0])).reshape(n, LANE)


def _forward_backward(x, mod, tables, target, weights, small, nb, seq):
    cos, sina, sinb = tables
    saved = []
    for l in range(DEPTH):
        w, s = weights[l], small[l]
        shift = mod[l][:, None, 0:D_MODEL]
        scale = mod[l][:, None, D_MODEL:2 * D_MODEL]
        gate = mod[l][:, None, 2 * D_MODEL:]
        h, za, zg, qb, kb, vb, qc, kc, vc, zf = _ln_inproj(x, shift, scale, s["norm_g"], w["in"], seq)
        qa, ka, va = _a_up(za, s["gq"], s["gkv"], w["uq"], w["ukv"], cos, sina, sinb)
        bias = _bias_expand(s["rb8"])[0:6].reshape(6, T_BAND, BAND_W)
        f = _forget_fwd(zf, s["fb"], nb, seq)
        frow = jnp.pad(f[:, 0:5].reshape(nb, seq, 5).transpose(0, 2, 1), ((0, 0), (0, 1), (0, 0)))
        frow = frow.reshape(nb, 6, 1, seq)
        oa, lse_a = _attn_fwd("A", qa, ka, va, None, nb, seq)
        ob, lse_b = _attn_fwd("B", qb, kb, vb, bias, nb, seq)
        oc, lse_c = _attn_fwd("C", qc, kc, vc, frow, nb, seq)
        y, xn = _gate_outproj(x, gate, oa, ob, oc, zg, w["out"], seq)
        saved.append(dict(x=x, h=h, za=za, zg=zg, zf=zf, y=y, shift=shift, scale=scale, gate=gate, bias=bias, frow=frow,
                          a=(qa, ka, va, oa, lse_a), b=(qb, kb, vb, ob, lse_b), c=(qc, kc, vc, oc, lse_c)))
        x = xn
    dx, loss8, gfinal8 = _final_loss(x, target, small[0]["final_g"])
    grads = []
    for l in reversed(range(DEPTH)):
        w, s, sv = weights[l], small[l], saved[l]
        qa, ka, va, oa, lse_a = sv["a"]
        qb, kb, vb, ob, lse_b = sv["b"]
        qc, kc, vc, oc, lse_c = sv["c"]
        doa, dob, doc, dzg, gw_out, dgate = _outproj_bwd(dx, sv["y"], sv["gate"], oa, ob, oc, sv["zg"], w["out"],
                                                          w["out_t"], nb, seq)
        dqa, dka, dva = _attn_bwd("A", qa, ka, va, oa, doa, lse_a, None, nb, seq)
        dqb, dkb, dvb, dbt = _attn_bwd("B", qb, kb, vb, ob, dob, lse_b, sv["bias"], nb, seq)
        dqc, dkc, dvc, dfr, dfq = _attn_bwd("C", qc, kc, vc, oc, doc, lse_c, sv["frow"], nb, seq)
        grb = _bias_reduce(jnp.pad(dbt.reshape(6, BIAS_FLAT), ((0, 2), (0, 0))))
        df = dfr.reshape(nb, 6, seq).transpose(0, 2, 1).reshape(nb * seq, 6) + dfq.reshape(nb * seq, 6, 64)[:, :, 0]
        dzf, gfb = _forget_bwd(jnp.pad(df, ((0, 0), (0, LANE - 6))), sv["zf"], s["fb"], nb, seq)
        dza, gw_uq, gw_ukv, ggq, ggkv = _a_up_bwd(dqa, dka, dva, sv["za"], s["gq"], s["gkv"], w["uq_t"], w["ukv_t"],
                                                  cos, sina, sinb)
        dz = (dza, dzg, dqb, dkb, dvb, dqc, dkc, dvc, dzf)
        dx, dshift, dscale, gnorm = _inproj_bwd_dx(dz, dx, sv["x"], sv["shift"], sv["scale"], s["norm_g"], w["in_t"],
                                                   nb, seq)
        gw_in = jnp.concatenate([_inproj_bwd_dw(sv["h"], dz[0:2], "inproj_bwd_dw0"),
                                 _inproj_bwd_dw(sv["h"], dz[2:], "inproj_bwd_dw1")], axis=1)
        dmod = jnp.concatenate([dshift[:, 0], dscale[:, 0], dgate[:, 0]], axis=1)
        grads.append(dict(w_in=gw_in, w_out=gw_out, w_uq=gw_uq, w_ukv=gw_ukv, dmod=dmod, norm_g=gnorm[0], gq=ggq[0],
                          gkv=ggkv[0], rb8=grb, fb=gfb[0]))
    grads.reverse()
    return loss8[0, 0], dx, grads, gfinal8[0]


def _layer_weights(w_in, w_out, w_uq, w_ukv):
    wi, wo, wq, wkv = _in_to_padded(w_in), _out_to_padded(w_out), _uq_to_padded(w_uq), _ukv_to_padded(w_ukv)
    return {"in": wi, "in_t": wi.T, "out": wo, "out_t": wo.T, "uq": wq, "uq_t": wq.T, "ukv": wkv, "ukv_t": wkv.T}


def _layer_small(norm_g, gq, gkv, rel_bias, forget_b, final_g):
    rb8 = jnp.pad(rel_bias, ((0, 3), (0, 384 - N_REL)))
    fb = jnp.pad(forget_b, (0, LANE - 5)).reshape(1, LANE)
    return dict(norm_g=norm_g.reshape(1, -1), gq=gq.reshape(1, -1), gkv=gkv.reshape(1, -1), rb8=rb8, fb=fb,
                final_g=final_g.reshape(1, -1))


def _small_payload(per_layer, final_g, loss):
    def stack(key):
        return jnp.stack([p[key] for p in per_layer])

    def rows(a, rng):
        return _rows(a, rng[1] - rng[0])

    dmod = stack("dmod") if "dmod" in per_layer[0] else jnp.zeros((LANE,), F32)
    parts = [rows(dmod, PAY_DMOD), rows(stack("norm_g"), PAY_NORM), rows(stack("gq"), PAY_GQ),
             rows(stack("gkv"), PAY_GKV), rows(stack("rb8"), PAY_RB), rows(stack("fb"), PAY_FB),
             rows(final_g, PAY_FINAL), rows(loss, PAY_LOSS)]
    return jnp.concatenate(parts, axis=0)


def _payload_split(pay):
    def take(rng, shape):
        n = 1
        for d in shape:
            n *= d
        return pay[rng[0]:rng[1]].reshape(-1)[0:n].reshape(shape)

    norm_g = take(PAY_NORM, (DEPTH, D_MODEL))
    gq = take(PAY_GQ, (DEPTH, A_Q_RANK))
    gkv = take(PAY_GKV, (DEPTH, A_KV_RANK))
    rb = take(PAY_RB, (DEPTH, 8, 384))[:, 0:5, 0:N_REL]
    fb = take(PAY_FB, (DEPTH, LANE))[:, 0:5]
    final_g = take(PAY_FINAL, (D_MODEL,))
    return norm_g, gq, gkv, rb, fb, final_g


def kernel(x, c, positions, w_ada, b_ada, norm_g, w_in, a_q_norm_g, a_w_uq, a_kv_norm_g, a_w_ukv, b_rel_bias, c_forget_b, w_out, final_g, loss_target, m_w_ada, m_b_ada, m_norm_g, m_w_in, m_a_q_norm_g, m_a_w_uq, m_a_kv_norm_g, m_a_w_ukv, m_b_rel_bias, m_c_forget_b, m_w_out, m_final_g, v_w_ada, v_b_ada, v_norm_g, v_w_in, v_a_q_norm_g, v_a_w_uq, v_a_kv_norm_g, v_a_w_ukv, v_b_rel_bias, v_c_forget_b, v_w_out, v_final_g):
    nb, seq, _ = x.shape
    ix, iy, ic = lax.axis_index("x"), lax.axis_index("y"), lax.axis_index("c")
    chip = 2 * ix + iy
    me = 2 * chip + ic

    full_in, full_out, full_lr = _transfer(
        "gather_weights", [w_in.astype(BF16), w_out.astype(BF16), _pack_lowrank(a_w_uq, a_w_ukv).astype(BF16)],
        [((DEPTH, D_MODEL, N_IN), BF16), ((DEPTH, D_MODEL, D_MODEL), BF16), ((1, 4 * LR_ROWS, PACK_COLS), BF16)],
        _gather_plan(SHARD_ROWS))
    lowrank = [_unpack_lowrank(full_lr[0, LR_ROWS * j:LR_ROWS * (j + 1)]) for j in range(4)]
    full_uq = jnp.concatenate([s[0] for s in lowrank], axis=2)
    full_ukv = jnp.concatenate([s[1] for s in lowrank], axis=2)
    weights = [_layer_weights(full_in[l], full_out[l], full_uq[l], full_ukv[l]) for l in range(DEPTH)]
    small = [_layer_small(norm_g[l], a_q_norm_g[l], a_kv_norm_g[l], b_rel_bias[l], c_forget_b[l], final_g)
             for l in range(DEPTH)]

    c_all = _exchange(c[None], ALL_FLIPS, "gather", "gather_c").reshape(N_DEV * nb, D_MODEL)
    cols = w_ada.shape[2]
    b_cols = lax.dynamic_slice_in_dim(b_ada, chip * cols, cols, axis=1)[:, None, :]
    mod_cols = _ada_fwd(c_all, w_ada, b_cols)
    mod_g = _exchange(mod_cols.reshape(1, DEPTH * N_DEV * nb, cols), ALL_FLIPS, "gather", "gather_mod")
    mod_all = jnp.concatenate([mod_g[2 * j].reshape(DEPTH, N_DEV * nb, cols) for j in range(4)], axis=2)
    mod = lax.dynamic_slice_in_dim(mod_all, me * nb, nb, axis=1)

    tables = _rope_tables(positions)
    loss_part, dx, grads, gfinal = _forward_backward(
        x.reshape(nb * seq, D_MODEL), mod, tables, loss_target.reshape(nb * seq, D_MODEL), weights, small, nb, seq)

    pay = _small_payload(grads, gfinal, loss_part)
    pay_all = _exchange(pay[None], ALL_FLIPS, "gather", "gather_small")
    tot = _sum_blocks(pay_all, "sum_small")
    loss = tot[PAY_LOSS[0], 0]
    dmod_all = pay_all[:, PAY_DMOD[0]:PAY_DMOD[1]].reshape(N_DEV, -1)[:, 0:DEPTH * nb * 3 * D_MODEL]
    dmod_all = dmod_all.reshape(N_DEV, DEPTH, nb, 3 * D_MODEL).transpose(1, 0, 2, 3)
    dmod_all = dmod_all.reshape(DEPTH, N_DEV * nb, 3 * D_MODEL)
    my_cols = lax.dynamic_slice_in_dim(dmod_all, chip * cols, cols, axis=2)
    g_w_ada, g_b_ada = _ada_bwd(c_all, my_cols, dmod_all)
    g_b_ada = g_b_ada[:, 0]

    g_uq = jnp.stack([_uq_from_padded(g["w_uq"]) for g in grads])
    g_ukv = jnp.stack([_ukv_from_padded(g["w_ukv"]) for g in grads])
    g_lr = jnp.concatenate([_pack_lowrank(g_uq[:, :, 144 * j:144 * (j + 1)], g_ukv[:, :, 192 * j:192 * (j + 1)])
                            for j in range(4)], axis=1)
    partials = [jnp.stack([g["w_in"] for g in grads]), jnp.stack([_out_from_padded(g["w_out"]) for g in grads]), g_lr]
    shapes = [(p.shape[0], p.shape[2]) for p in partials]
    halves = [r // 2 for r in SHARD_ROWS]
    both = _transfer("pair_reduce", partials, [((nl, 8 * h, nc), F32) for (nl, nc), h in zip(shapes, halves)],
                     _pair_reduce_plan(SHARD_ROWS))
    chip_sums = [_sum_slots(b.reshape(nl, 2, 4 * h, nc), BF16, "sum_pair%d" % i)
                 for i, (b, (nl, nc), h) in enumerate(zip(both, shapes, halves))]
    landed = _transfer("chip_scatter", chip_sums, [((nl, 4 * h, nc), BF16) for (nl, nc), h in zip(shapes, halves)],
                       _chip_scatter_plan(SHARD_ROWS))
    reduced = [_sum_slots(b.reshape(nl, 4, h, nc), F32, "sum_chips%d" % i)
               for i, (b, (nl, nc), h) in enumerate(zip(landed, shapes, halves))]
    g_in_p, g_out_sh, g_lr_sh = _transfer("pair_share", reduced,
                                          [((nl, 2 * h, nc), F32) for (nl, nc), h in zip(shapes, halves)],
                                          _pair_share_plan(SHARD_ROWS))
    g_uq_sh, g_ukv_sh = _unpack_lowrank(g_lr_sh[0])
    gw = (jnp.stack([_in_from_padded(g_in_p[l]) for l in range(DEPTH)]), g_out_sh, g_uq_sh, g_ukv_sh)

    def adam_nd(w, g, m, v, name):
        flat = (-1, w.shape[-1])
        return tuple(a.reshape(w.shape) for a in _adamw(w.reshape(flat), g.reshape(flat), m.reshape(flat),
                                                        v.reshape(flat), name))

    upd = [adam_nd(w_in, gw[0], m_w_in, v_w_in, "adamw_in"), adam_nd(w_out, gw[1], m_w_out, v_w_out, "adamw_out"),
           adam_nd(a_w_uq, gw[2], m_a_w_uq, v_a_w_uq, "adamw_uq"), adam_nd(a_w_ukv, gw[3], m_a_w_ukv, v_a_w_ukv, "adamw_ukv")]
    dw, mw, vw = (tuple(u[i] for u in upd) for i in range(3))
    ada_shape = w_ada.shape
    flat_ada = (DEPTH * D_MODEL, cols)
    d_ada, m_ada, v_ada = (a.reshape(ada_shape) for a in _adamw(
        w_ada.reshape(flat_ada), g_w_ada.reshape(flat_ada), m_w_ada.reshape(flat_ada), v_w_ada.reshape(flat_ada),
        "adamw_ada"))
    d_b, m_b, v_b = (a.reshape(DEPTH, 3 * D_MODEL) for a in _adamw(
        _rows(b_ada, 48), _rows(g_b_ada, 48), _rows(m_b_ada, 48), _rows(v_b_ada, 48), "adamw_b_ada"))

    def small_rows(ng, gq, gkv, rb, fb, fg):
        per_layer = [dict(norm_g=ng[l], gq=gq[l], gkv=gkv[l], rb8=jnp.pad(rb[l], ((0, 3), (0, 384 - N_REL))),
                          fb=jnp.pad(fb[l], (0, LANE - 5))) for l in range(DEPTH)]
        return _small_payload(per_layer, fg, jnp.zeros((), F32))

    w_s = small_rows(norm_g, a_q_norm_g, a_kv_norm_g, b_rel_bias, c_forget_b, final_g)
    m_s = small_rows(m_norm_g, m_a_q_norm_g, m_a_kv_norm_g, m_b_rel_bias, m_c_forget_b, m_final_g)
    v_s = small_rows(v_norm_g, v_a_q_norm_g, v_a_kv_norm_g, v_b_rel_bias, v_c_forget_b, v_final_g)
    d_s, mo_s, vo_s = _adamw(w_s, tot, m_s, v_s, "adamw_small")
    gs = _payload_split(tot)
    ds = _payload_split(d_s)
    ms = _payload_split(mo_s)
    vs = _payload_split(vo_s)

    def ordered(ada, b, sm, big):
        ng, gq, gkv, rb, fb, fg = sm
        b_in, b_out, b_uq, b_ukv = big
        return (ada, b, ng, b_in, gq, b_uq, gkv, b_ukv, rb, fb, b_out, fg)

    return (loss, dx.reshape(nb, seq, D_MODEL), *ordered(g_w_ada, g_b_ada, gs, gw), *ordered(d_ada, d_b, ds, dw),
            *ordered(m_ada, m_b, ms, mw), *ordered(v_ada, v_b, vs, vw))
```

```python
import functools

import jax
import jax.numpy as jnp
from jax import lax
from jax.experimental import pallas as pl
from jax.experimental.pallas import tpu as pltpu

F32 = jnp.float32
BF16 = jnp.bfloat16

D_MODEL = 1024
DEPTH = 2
EPS = 1e-6
NEG = -1e30
ROPE_THETA = 10000.0
A_ROPE = 32
A_Q_RANK = 384
A_KV_RANK = 256
REL_CLIP = 128
N_REL = 2 * REL_CLIP + 1
N_IN = 3621

ADAM_LR = 0.001
ADAM_B1 = 0.9
ADAM_B2 = 0.999
ADAM_EPS = 1e-08
ADAM_WD = 0.01
ADAM_STEP = 10

LANE = 128
VMEM_LIMIT = 56 * 1024 * 1024

NP_IN = 4352
Z_A = (0, 768)
Z_G = (768, 1920)
Z_QKV = tuple((1920 + 384 * i, 1920 + 384 * (i + 1)) for i in range(6))
Z_F = (4224, 4352)
IN_PIECES = ((0, 672, 96), (672, 1056, 0), (2016, 2336, 64), (3301, 3621, 64), (1056, 1376, 64), (1376, 1696, 64),
             (1696, 2016, 64), (2336, 2656, 64), (2656, 2976, 64), (2976, 3296, 64), (3296, 3301, 123))
D_CAT = 1152

TM = 256
T_CAUSAL = 256
T_BAND = 128
BAND_TILES = 5
N_DEV = 8

PAY_DMOD = (0, 96)
PAY_NORM = (96, 112)
PAY_GQ = (112, 120)
PAY_GKV = (120, 128)
PAY_RB = (128, 176)
PAY_FB = (176, 184)
PAY_FINAL = (184, 192)
PAY_LOSS = (192, 200)
PAY_ROWS = 200

PACK_COLS = 1024
PACK_ROWS = 2560
HALF_ROWS = PACK_ROWS // 2


def _params(sem=None):
    return pltpu.CompilerParams(dimension_semantics=sem, vmem_limit_bytes=VMEM_LIMIT)


def _lane_iota(shape):
    return lax.broadcasted_iota(jnp.int32, shape, len(shape) - 1)


def _silu(u):
    return u * jax.nn.sigmoid(u)


def _dsilu(u):
    s = jax.nn.sigmoid(u)
    return s * (1.0 + u * (1.0 - s))


def _rms(x, g):
    r = lax.rsqrt(jnp.mean(x * x, axis=-1, keepdims=True) + EPS)
    xh = x * r
    return xh * g, xh, r


def _rms_bwd(dy, xh, r, g):
    dxh = dy * g
    return r * (dxh - xh * jnp.mean(dxh * xh, axis=-1, keepdims=True))


def _rope(x, cos, sina, sinb):
    return x * cos + pltpu.roll(x, 16, 1) * sinb + pltpu.roll(x, LANE - 16, 1) * sina


def _rope_t(dy, cos, sina, sinb):
    return dy * cos + pltpu.roll(dy * sinb, LANE - 16, 1) + pltpu.roll(dy * sina, 16, 1)


def _split3(x):
    hi = x.astype(BF16)
    r1 = x - hi.astype(F32)
    mid = r1.astype(BF16)
    lo = (r1 - mid.astype(F32)).astype(BF16)
    return hi, mid, lo


def _dot(a, b):
    return jnp.dot(a, b, preferred_element_type=F32)


def _dot_nt(a, b):
    return lax.dot_general(a, b, (((1,), (1,)), ((), ())), preferred_element_type=F32)


def _dot_tn(a, b):
    return lax.dot_general(a, b, (((0,), (0,)), ((), ())), preferred_element_type=F32)


def _row_spec(cols):
    return pl.BlockSpec((TM, cols), lambda i: (i, 0))


def _full_spec(shape):
    return pl.BlockSpec(shape, lambda i: (0,) * len(shape))


def _ex_spec(tiles_per_ex):
    return pl.BlockSpec((1, 1, D_MODEL), lambda i: (i // tiles_per_ex, 0, 0))


def _ln_inproj(x, shift, scale, g, w_in_p, seq):
    t = x.shape[0]

    def body(x_ref, sh_ref, sc_ref, g_ref, w_ref, h_ref, za_ref, zg_ref, q0, q1, q2, q3, q4, q5, zf_ref):
        n, _, _ = _rms(x_ref[...], g_ref[...])
        h = (n * (1.0 + sc_ref[0]) + sh_ref[0]).astype(BF16)
        h_ref[...] = h
        za_ref[...] = _dot(h, w_ref[:, Z_A[0]:Z_A[1]])
        zg_ref[...] = _dot(h, w_ref[:, Z_G[0]:Z_G[1]])
        for ref, (c0, c1) in zip((q0, q1, q2, q3, q4, q5), Z_QKV):
            ref[...] = _dot(h, w_ref[:, c0:c1]).astype(BF16)
        zf_ref[...] = _dot(h, w_ref[:, Z_F[0]:Z_F[1]])

    tpe = seq // TM
    shapes = [jax.ShapeDtypeStruct((t, D_MODEL), BF16), jax.ShapeDtypeStruct((t, 768), F32),
              jax.ShapeDtypeStruct((t, D_CAT), F32)]
    shapes += [jax.ShapeDtypeStruct((t, 384), BF16)] * 6 + [jax.ShapeDtypeStruct((t, LANE), F32)]
    return pl.pallas_call(
        body, name="ln_inproj", grid=(t // TM,),
        in_specs=[_row_spec(D_MODEL), _ex_spec(tpe), _ex_spec(tpe), _full_spec((1, D_MODEL)),
                  _full_spec((D_MODEL, NP_IN))],
        out_specs=[_row_spec(D_MODEL), _row_spec(768), _row_spec(D_CAT)] + [_row_spec(384)] * 6 + [_row_spec(LANE)],
        out_shape=shapes, compiler_params=_params(("parallel",)),
    )(x, shift, scale, g, w_in_p)


def _a_up(za, gq, gkv, w_uq_p, w_ukv_p, cos, sina, sinb):
    t = za.shape[0]

    def body(za_ref, gq_ref, gkv_ref, wq_ref, wkv_ref, cos_ref, sa_ref, sb_ref, q_ref, k_ref, v_ref):
        cos_t, sa, sb = cos_ref[...], sa_ref[...], sb_ref[...]
        cqn, _, _ = _rms(za_ref[:, 0:384], gq_ref[...])
        q = _dot(cqn.astype(BF16), wq_ref[...])
        ckvn, _, _ = _rms(za_ref[:, 384:640], gkv_ref[...])
        kv = _dot(ckvn.astype(BF16), wkv_ref[...])
        kpe = za_ref[:, 640:768]
        kpe = _rope(kpe + pltpu.roll(kpe, 32, 1), cos_t, sa, sb).astype(BF16)
        for p in range(3):
            q_ref[:, 256 * p:256 * p + 128] = q[:, 256 * p:256 * p + 128].astype(BF16)
            q_ref[:, 256 * p + 128:256 * p + 256] = _rope(q[:, 256 * p + 128:256 * p + 256], cos_t, sa, sb).astype(BF16)
            k_ref[:, 256 * p:256 * p + 128] = kv[:, 128 * p:128 * p + 128].astype(BF16)
            k_ref[:, 256 * p + 128:256 * p + 256] = kpe
        v_ref[...] = kv[:, 384:768].astype(BF16)

    return pl.pallas_call(
        body, name="a_up", grid=(t // TM,),
        in_specs=[_row_spec(768), _full_spec((1, 384)), _full_spec((1, 256)), _full_spec((384, 768)),
                  _full_spec((256, 768)), _row_spec(LANE), _row_spec(LANE), _row_spec(LANE)],
        out_specs=[_row_spec(768), _row_spec(768), _row_spec(384)],
        out_shape=[jax.ShapeDtypeStruct((t, 768), BF16), jax.ShapeDtypeStruct((t, 768), BF16),
                   jax.ShapeDtypeStruct((t, 384), BF16)],
        compiler_params=_params(("parallel",)),
    )(za, gq, gkv, w_uq_p, w_ukv_p, cos, sina, sinb)


def _tri(n, upper):
    r = lax.broadcasted_iota(jnp.int32, (n, n), 0)
    c = lax.broadcasted_iota(jnp.int32, (n, n), 1)
    return jnp.where((c >= r) if upper else (c <= r), 1.0, 0.0).astype(BF16)


def _forget_fwd(zf, fb, nb, seq):
    blk = 256

    def body(zf_ref, fb_ref, f_ref):
        tri = _tri(blk, False)
        live = _lane_iota((blk, LANE)) < 5
        carry = jnp.zeros((1, LANE), F32)
        for i in range(seq // blk):
            u = zf_ref[i * blk:(i + 1) * blk, :] + fb_ref[...]
            lf = jnp.where(live, jnp.minimum(u, 0.0) - jnp.log(1.0 + jnp.exp(-jnp.abs(u))), 0.0)
            hi, mid, lo = _split3(lf)
            f_ref[i * blk:(i + 1) * blk, :] = _dot(tri, hi) + _dot(tri, mid) + _dot(tri, lo) + carry
            carry = carry + jnp.sum(lf, axis=0, keepdims=True)

    return pl.pallas_call(
        body, name="forget_fwd", grid=(nb,),
        in_specs=[pl.BlockSpec((seq, LANE), lambda b: (b, 0)), pl.BlockSpec((1, LANE), lambda b: (0, 0))],
        out_specs=pl.BlockSpec((seq, LANE), lambda b: (b, 0)),
        out_shape=jax.ShapeDtypeStruct((nb * seq, LANE), F32), compiler_params=_params(("parallel",)),
    )(zf, fb)


def _forget_bwd(df, zf, fb, nb, seq):
    blk = 256

    def body(df_ref, zf_ref, fb_ref, dz_ref, gb_ref):
        @pl.when(pl.program_id(0) == 0)
        def _():
            gb_ref[...] = jnp.zeros_like(gb_ref)

        tri = _tri(blk, True)
        live = _lane_iota((blk, LANE)) < 5
        carry = jnp.zeros((1, LANE), F32)
        gsum = jnp.zeros((1, LANE), F32)
        for i in reversed(range(seq // blk)):
            d = df_ref[i * blk:(i + 1) * blk, :]
            hi, mid, lo = _split3(d)
            dlf = _dot(tri, hi) + _dot(tri, mid) + _dot(tri, lo) + carry
            carry = carry + jnp.sum(d, axis=0, keepdims=True)
            u = zf_ref[i * blk:(i + 1) * blk, :] + fb_ref[...]
            du = jnp.where(live, dlf * jax.nn.sigmoid(-u), 0.0)
            dz_ref[i * blk:(i + 1) * blk, :] = du.astype(BF16)
            gsum = gsum + jnp.sum(du, axis=0, keepdims=True)
        gb_ref[...] += jnp.broadcast_to(gsum, gb_ref.shape)

    return pl.pallas_call(
        body, name="forget_bwd", grid=(nb,),
        in_specs=[pl.BlockSpec((seq, LANE), lambda b: (b, 0)), pl.BlockSpec((seq, LANE), lambda b: (b, 0)),
                  pl.BlockSpec((1, LANE), lambda b: (0, 0))],
        out_specs=[pl.BlockSpec((seq, LANE), lambda b: (b, 0)), pl.BlockSpec((8, LANE), lambda b: (0, 0))],
        out_shape=[jax.ShapeDtypeStruct((nb * seq, LANE), BF16), jax.ShapeDtypeStruct((8, LANE), F32)],
        compiler_params=_params(("arbitrary",)),
    )(df, zf, fb)


def _gate_outproj(x, gate, oa, ob, oc, zg, w_out_p, seq):
    t = x.shape[0]

    def body(x_ref, gate_ref, oa_ref, ob_ref, oc_ref, zg_ref, w_ref, y_ref, xn_ref):
        y = jnp.zeros((TM, D_MODEL), F32)
        for i, o_ref in enumerate((oa_ref, ob_ref, oc_ref)):
            cat = (o_ref[...] * _silu(zg_ref[:, 384 * i:384 * (i + 1)])).astype(BF16)
            y = y + _dot(cat, w_ref[384 * i:384 * (i + 1), :])
        y_ref[...] = y
        xn_ref[...] = x_ref[...] + gate_ref[0] * y

    return pl.pallas_call(
        body, name="gate_outproj", grid=(t // TM,),
        in_specs=[_row_spec(D_MODEL), _ex_spec(seq // TM), _row_spec(384), _row_spec(384), _row_spec(384),
                  _row_spec(D_CAT), _full_spec((D_CAT, D_MODEL))],
        out_specs=[_row_spec(D_MODEL), _row_spec(D_MODEL)],
        out_shape=[jax.ShapeDtypeStruct((t, D_MODEL), F32)] * 2, compiler_params=_params(("parallel",)),
    )(x, gate, oa, ob, oc, zg, w_out_p)


def _final_loss(x, target, g):
    t = x.shape[0]

    def body(x_ref, t_ref, g_ref, dx_ref, loss_ref, gg_ref):
        @pl.when(pl.program_id(0) == 0)
        def _():
            loss_ref[...] = jnp.zeros_like(loss_ref)
            gg_ref[...] = jnp.zeros_like(gg_ref)

        gv = g_ref[...]
        out, xh, r = _rms(x_ref[...], gv)
        err = out - t_ref[...]
        loss_ref[...] += 0.5 * jnp.sum(jnp.mean(err * err, axis=-1, keepdims=True), axis=0, keepdims=True)
        dout = err / D_MODEL
        gg_ref[...] += jnp.broadcast_to(jnp.sum(dout * xh, axis=0, keepdims=True), gg_ref.shape)
        dx_ref[...] = _rms_bwd(dout, xh, r, gv)

    return pl.pallas_call(
        body, name="final_loss", grid=(t // TM,),
        in_specs=[_row_spec(D_MODEL), _row_spec(D_MODEL), _full_spec((1, D_MODEL))],
        out_specs=[_row_spec(D_MODEL), _full_spec((8, LANE)), _full_spec((8, D_MODEL))],
        out_shape=[jax.ShapeDtypeStruct((t, D_MODEL), F32), jax.ShapeDtypeStruct((8, LANE), F32),
                   jax.ShapeDtypeStruct((8, D_MODEL), F32)],
        compiler_params=_params(("arbitrary",)),
    )(x, target, g)


def _head_masks(kind, rows, dq, h):
    lq = _lane_iota((rows, dq))
    lv = _lane_iota((rows, LANE))
    mq = (lq >= 64 * h) & (lq < 64 * h + 64)
    if kind == "A":
        mq = mq | ((lq >= 128 + 32 * h) & (lq < 160 + 32 * h))
    return mq, (lv >= 64 * h) & (lv < 64 * h + 64)


def _tile_mask(kind, tile, m=None):
    row = lax.broadcasted_iota(jnp.int32, (tile, tile), 0)
    col = lax.broadcasted_iota(jnp.int32, (tile, tile), 1)
    if kind == "A":
        return (col >> 6) <= (row >> 6)
    if kind == "C":
        return col <= row
    first = (m == 0) & (row >= 64) & (col < 64)
    last = (m == BAND_TILES - 1) & (row < 64) & (col >= 64)
    return jnp.logical_not(first | last)


def _attn_scale(kind):
    return 96.0 ** -0.5 if kind == "A" else 0.125


def _attn_fwd(kind, q, k, v, aux, nb, seq):
    dq = q.shape[1] // 3
    tile = T_BAND if kind == "B" else T_CAUSAL
    nq = seq // tile
    scale = _attn_scale(kind)

    def body(*refs):
        if kind == "A":
            q_ref, k_ref, v_ref, o_ref, lse_ref = refs
            aux_ref = None
        else:
            q_ref, k_ref, v_ref, aux_ref, o_ref, lse_ref = refs
        qi = pl.program_id(2)
        q2 = q_ref[...]
        res = []
        for h in range(2):
            mq, _ = _head_masks(kind, tile, dq, h)
            qh = jnp.where(mq, q2, jnp.zeros_like(q2))

            def step(kj, carry, m=None, diag=False):
                mx, l, acc = carry
                ks = pl.ds(pl.multiple_of(kj * tile, tile), tile)
                kt = k_ref[ks, :]
                vt = v_ref[ks, :]
                s = _dot_nt(qh, kt) * scale
                if kind == "B":
                    s = jnp.where(_tile_mask("B", tile, m), s + aux_ref[h, m], NEG)
                if kind == "C":
                    s = s - aux_ref[0, h, pl.ds(kj, 1), :]
                if diag:
                    s = jnp.where(_tile_mask(kind, tile), s, NEG)
                mn = jnp.maximum(mx, jnp.max(s, axis=-1, keepdims=True))
                alpha = jnp.exp(mx - mn)
                p = jnp.exp(s - mn)
                l = alpha * l + jnp.sum(p, axis=-1, keepdims=True)
                acc = alpha * acc + _dot(p.astype(BF16), vt)
                return mn, l, acc

            init = (jnp.full((tile, 1), NEG, F32), jnp.zeros((tile, 1), F32), jnp.zeros((tile, LANE), F32))
            if kind == "B":
                m0 = jnp.maximum(BAND_TILES - 1 - qi, 0)
                mx, l, acc = lax.fori_loop(m0, BAND_TILES, lambda m, c: step(qi - (BAND_TILES - 1) + m, c, m=m), init)
            else:
                carry = lax.fori_loop(0, qi, lambda kj, c: step(kj, c), init)
                mx, l, acc = step(qi, carry, diag=True)
            res.append((acc / l, mx + jnp.log(l)))
        first = _lane_iota((tile, LANE)) < 64
        o_ref[...] = jnp.where(first, res[0][0], res[1][0])
        lse_ref[...] = jnp.where(first, res[0][1], res[1][1])

    in_specs = [pl.BlockSpec((tile, dq), lambda b, p, i: (b * nq + i, p)),
                pl.BlockSpec((seq, dq), lambda b, p, i: (b, p)),
                pl.BlockSpec((seq, LANE), lambda b, p, i: (b, p))]
    args = [q, k, v]
    if kind == "B":
        in_specs.append(pl.BlockSpec((2, BAND_TILES, tile, tile), lambda b, p, i: (p, 0, 0, 0)))
        args.append(aux)
    if kind == "C":
        in_specs.append(pl.BlockSpec((1, 2, nq, tile), lambda b, p, i: (b, p, 0, 0)))
        args.append(aux)
    out_spec = pl.BlockSpec((tile, LANE), lambda b, p, i: (b * nq + i, p))
    return pl.pallas_call(
        body, name="attn_fwd_" + kind, grid=(nb, 3, nq), in_specs=in_specs, out_specs=[out_spec, out_spec],
        out_shape=[jax.ShapeDtypeStruct((nb * seq, 384), F32)] * 2,
        compiler_params=_params(("parallel", "parallel", "parallel")),
    )(*args)


def _attn_bwd(kind, q, k, v, o, do, lse, aux, nb, seq):
    dq = q.shape[1] // 3
    tile = T_BAND if kind == "B" else T_CAUSAL
    nq = seq // tile
    scale = _attn_scale(kind)
    dqk_dtype = F32 if kind == "A" else BF16

    def body(*refs):
        dfr_ref = dfq_ref = dbt_ref = aux_ref = None
        if kind == "A":
            q_ref, k_ref, v_ref, o_ref, do_ref, lse_ref, dq_ref, dk_ref, dv_ref, dk_acc, dv_acc = refs
        elif kind == "B":
            q_ref, k_ref, v_ref, o_ref, do_ref, lse_ref, aux_ref, dq_ref, dk_ref, dv_ref, dbt_ref, dk_acc, dv_acc = refs
        else:
            (q_ref, k_ref, v_ref, o_ref, do_ref, lse_ref, aux_ref, dq_ref, dk_ref, dv_ref, dfr_ref, dfq_ref,
             dk_acc, dv_acc) = refs
        dk_acc[...] = jnp.zeros_like(dk_acc)
        dv_acc[...] = jnp.zeros_like(dv_acc)
        if kind == "C":
            dfr_ref[...] = jnp.zeros_like(dfr_ref)
        if kind == "B":
            @pl.when(pl.program_id(1) == 0)
            def _():
                dbt_ref[...] = jnp.zeros_like(dbt_ref)

        def q_step(qi, _):
            qs = pl.ds(pl.multiple_of(qi * tile, tile), tile)
            q2 = q_ref[qs, :]
            do2 = do_ref[qs, :]
            o2 = o_ref[qs, :]
            lse2 = lse_ref[qs, :]
            dq_tot = jnp.zeros((tile, dq), F32)
            row_sums = []
            for h in range(2):
                mq, mv = _head_masks(kind, tile, dq, h)
                qh = jnp.where(mq, q2, jnp.zeros_like(q2))
                doh = jnp.where(mv, do2, 0.0)
                dob = doh.astype(BF16)
                delta = jnp.sum(doh * o2, axis=-1, keepdims=True)
                lseh = jnp.max(jnp.where(mv, lse2, NEG), axis=-1, keepdims=True)

                def step(kj, carry, m=None, diag=False):
                    dq_t, rs = carry
                    ks = pl.ds(pl.multiple_of(kj * tile, tile), tile)
                    kt = k_ref[ks, :]
                    vt = v_ref[ks, :]
                    s = _dot_nt(qh, kt) * scale
                    if kind == "B":
                        s = jnp.where(_tile_mask("B", tile, m), s + aux_ref[h, m], NEG)
                    if kind == "C":
                        s = s - aux_ref[0, h, pl.ds(kj, 1), :]
                    if diag:
                        s = jnp.where(_tile_mask(kind, tile), s, NEG)
                    p = jnp.exp(s - lseh)
                    ds = p * (_dot_nt(dob, vt) - delta)
                    if kind == "B":
                        dbt_ref[h, m] += ds
                    if kind == "C":
                        dfr_ref[0, h, pl.ds(kj, 1), :] -= jnp.sum(ds, axis=0, keepdims=True)
                        rs = rs + jnp.sum(ds, axis=-1, keepdims=True)
                    dss = (ds * scale).astype(BF16)
                    dv_acc[ks, :] += _dot_tn(p.astype(BF16), dob)
                    dk_acc[ks, :] += _dot_tn(dss, qh)
                    return dq_t + jnp.where(mq, _dot(dss, kt), 0.0), rs

                zero = (jnp.zeros((tile, dq), F32), jnp.zeros((tile, 1), F32))
                if kind == "B":
                    m0 = jnp.maximum(BAND_TILES - 1 - qi, 0)
                    dq_h, rs_h = lax.fori_loop(m0, BAND_TILES, lambda m, c: step(qi - (BAND_TILES - 1) + m, c, m=m), zero)
                else:
                    carry = lax.fori_loop(0, qi, lambda kj, c: step(kj, c), zero)
                    dq_h, rs_h = step(qi, carry, diag=True)
                dq_tot = dq_tot + dq_h
                row_sums.append(rs_h)
            dq_ref[qs, :] = dq_tot.astype(dqk_dtype)
            if kind == "C":
                dfq_ref[qs, :] = jnp.where(_lane_iota((tile, LANE)) < 64, row_sums[0], row_sums[1])
            return 0

        lax.fori_loop(0, nq, q_step, 0)
        dk_ref[...] = dk_acc[...].astype(dqk_dtype)
        dv_ref[...] = dv_acc[...].astype(BF16)

    def seq_spec(cols):
        return pl.BlockSpec((seq, cols), lambda p, b: (b, p))

    in_specs = [seq_spec(dq), seq_spec(dq), seq_spec(LANE), seq_spec(LANE), seq_spec(LANE), seq_spec(LANE)]
    args = [q, k, v, o, do, lse]
    out_specs = [seq_spec(dq), seq_spec(dq), seq_spec(LANE)]
    out_shape = [jax.ShapeDtypeStruct((nb * seq, 3 * dq), dqk_dtype)] * 2 + [jax.ShapeDtypeStruct((nb * seq, 384), BF16)]
    if kind == "B":
        spec = pl.BlockSpec((2, BAND_TILES, tile, tile), lambda p, b: (p, 0, 0, 0))
        in_specs.append(spec)
        args.append(aux)
        out_specs.append(spec)
        out_shape.append(jax.ShapeDtypeStruct((6, BAND_TILES, tile, tile), F32))
    if kind == "C":
        spec = pl.BlockSpec((1, 2, nq, tile), lambda p, b: (b, p, 0, 0))
        in_specs.append(spec)
        args.append(aux)
        out_specs += [spec, seq_spec(LANE)]
        out_shape += [jax.ShapeDtypeStruct((nb, 6, nq, tile), F32), jax.ShapeDtypeStruct((nb * seq, 384), F32)]
    return pl.pallas_call(
        body, name="attn_bwd_" + kind, grid=(3, nb), in_specs=in_specs, out_specs=out_specs, out_shape=out_shape,
        scratch_shapes=[pltpu.VMEM((seq, dq), F32), pltpu.VMEM((seq, LANE), F32)],
        compiler_params=_params(("arbitrary", "arbitrary")),
    )(*args)


BAND_W = BAND_TILES * T_BAND


def _segments(kind, qi, tile):
    r0 = qi * tile
    if kind == "B":
        lo = max(qi - (BAND_TILES - 1), 0) * tile
        return [(lo, r0 + tile, False, lo - (qi - (BAND_TILES - 1)) * tile)]
    return ([(0, r0, False, 0)] if qi else []) + [(r0, r0 + tile, True, 0)]


def _scores(kind, qh, k_ref, aux_ref, h, seg, tile, scale):
    a, b, diag, c0 = seg
    s = _dot_nt(qh, k_ref[a:b, :]) * scale
    if kind == "B":
        row = lax.broadcasted_iota(jnp.int32, (tile, b - a), 0)
        col = lax.broadcasted_iota(jnp.int32, (tile, b - a), 1) + c0
        hidden = ((row >= 64) & (col < 64)) | ((row < 64) & (col >= BAND_W - 64))
        return jnp.where(hidden, NEG, s + aux_ref[h, :, c0:BAND_W])
    if kind == "C":
        s = s - aux_ref[0, h, :, a:b]
    if diag:
        s = jnp.where(_tile_mask(kind, tile), s, NEG)
    return s


def _attn_fwd(kind, q, k, v, aux, nb, seq):
    dq = q.shape[1] // 3
    tile = T_BAND if kind == "B" else T_CAUSAL
    nq = seq // tile
    scale = _attn_scale(kind)

    def body(*refs):
        if kind == "A":
            q_ref, k_ref, v_ref, o_ref, lse_ref = refs
            aux_ref = None
        else:
            q_ref, k_ref, v_ref, aux_ref, o_ref, lse_ref = refs
        first = _lane_iota((tile, LANE)) < 64
        for qi in range(nq):
            rows = slice(qi * tile, (qi + 1) * tile)
            q2 = q_ref[rows, :]
            res = []
            for h in range(2):
                mq, _ = _head_masks(kind, tile, dq, h)
                qh = jnp.where(mq, q2, jnp.zeros_like(q2))
                segs = _segments(kind, qi, tile)
                ss = [_scores(kind, qh, k_ref, aux_ref, h, seg, tile, scale) for seg in segs]
                mx = functools.reduce(jnp.maximum, [jnp.max(s, axis=-1, keepdims=True) for s in ss])
                ps = [jnp.exp(s - mx) for s in ss]
                l = functools.reduce(jnp.add, [jnp.sum(p, axis=-1, keepdims=True) for p in ps])
                acc = functools.reduce(jnp.add, [_dot(p.astype(BF16), v_ref[seg[0]:seg[1], :]) for p, seg in zip(ps, segs)])
                res.append((acc / l, mx + jnp.log(l)))
            o_ref[rows, :] = jnp.where(first, res[0][0], res[1][0])
            lse_ref[rows, :] = jnp.where(first, res[0][1], res[1][1])

    def seq_spec(cols):
        return pl.BlockSpec((seq, cols), lambda b, p: (b, p))

    in_specs = [seq_spec(dq), seq_spec(dq), seq_spec(LANE)]
    args = [q, k, v]
    if kind == "B":
        in_specs.append(pl.BlockSpec((2, tile, BAND_W), lambda b, p: (p, 0, 0)))
        args.append(aux)
    if kind == "C":
        in_specs.append(pl.BlockSpec((1, 2, 1, seq), lambda b, p: (b, p, 0, 0)))
        args.append(aux)
    return pl.pallas_call(
        body, name="attn_fwd_" + kind, grid=(nb, 3), in_specs=in_specs, out_specs=[seq_spec(LANE), seq_spec(LANE)],
        out_shape=[jax.ShapeDtypeStruct((nb * seq, 384), F32)] * 2, compiler_params=_params(("parallel", "parallel")),
    )(*args)


def _attn_bwd(kind, q, k, v, o, do, lse, aux, nb, seq):
    dq = q.shape[1] // 3
    tile = T_BAND if kind == "B" else T_CAUSAL
    nq = seq // tile
    scale = _attn_scale(kind)
    dqk_dtype = F32 if kind == "A" else BF16

    def body(*refs):
        dfr_ref = dfq_ref = dbt_ref = aux_ref = None
        if kind == "A":
            q_ref, k_ref, v_ref, o_ref, do_ref, lse_ref, dq_ref, dk_ref, dv_ref, dkt_acc, dvt_acc = refs
        elif kind == "B":
            q_ref, k_ref, v_ref, o_ref, do_ref, lse_ref, aux_ref, dq_ref, dk_ref, dv_ref, dbt_ref, dkt_acc, dvt_acc = refs
        else:
            (q_ref, k_ref, v_ref, o_ref, do_ref, lse_ref, aux_ref, dq_ref, dk_ref, dv_ref, dfr_ref, dfq_ref,
             dkt_acc, dvt_acc) = refs
        dkt_acc[...] = jnp.zeros_like(dkt_acc)
        dvt_acc[...] = jnp.zeros_like(dvt_acc)
        if kind == "C":
            dfr_ref[...] = jnp.zeros_like(dfr_ref)
        if kind == "B":
            @pl.when(pl.program_id(1) == 0)
            def _():
                dbt_ref[...] = jnp.zeros_like(dbt_ref)

        first = _lane_iota((tile, LANE)) < 64
        for qi in range(nq):
            rows = slice(qi * tile, (qi + 1) * tile)
            q2 = q_ref[rows, :]
            do2 = do_ref[rows, :]
            o2 = o_ref[rows, :]
            lse2 = lse_ref[rows, :]
            dq_tot = jnp.zeros((tile, dq), F32)
            row_sums = []
            for h in range(2):
                mq, mv = _head_masks(kind, tile, dq, h)
                qh = jnp.where(mq, q2, jnp.zeros_like(q2))
                doh = jnp.where(mv, do2, 0.0)
                dob = doh.astype(BF16)
                qht = qh.astype(F32).T.astype(BF16)
                dobt = doh.T.astype(BF16)
                delta = jnp.sum(doh * o2, axis=-1, keepdims=True)
                lseh = jnp.max(jnp.where(mv, lse2, NEG), axis=-1, keepdims=True)
                rs = jnp.zeros((tile, 1), F32)
                for seg in _segments(kind, qi, tile):
                    a, b, _, c0 = seg
                    p = jnp.exp(_scores(kind, qh, k_ref, aux_ref, h, seg, tile, scale) - lseh)
                    ds = p * (_dot_nt(dob, v_ref[a:b, :]) - delta)
                    if kind == "B":
                        dbt_ref[h, :, c0:BAND_W] += ds
                    if kind == "C":
                        dfr_ref[0, h, :, a:b] -= jnp.sum(ds, axis=0, keepdims=True)
                        rs = rs + jnp.sum(ds, axis=-1, keepdims=True)
                    dss = (ds * scale).astype(BF16)
                    dvt_acc[:, a:b] += _dot(dobt, p.astype(BF16))
                    dkt_acc[:, a:b] += _dot(qht, dss)
                    dq_tot = dq_tot + jnp.where(mq, _dot(dss, k_ref[a:b, :]), 0.0)
                row_sums.append(rs)
            dq_ref[rows, :] = dq_tot.astype(dqk_dtype)
            if kind == "C":
                dfq_ref[rows, :] = jnp.where(first, row_sums[0], row_sums[1])
        for j in range(seq // 256):
            cols = slice(256 * j, 256 * (j + 1))
            dk_ref[cols, :] = dkt_acc[:, cols].T.astype(dqk_dtype)
            dv_ref[cols, :] = dvt_acc[:, cols].T.astype(BF16)

    def seq_spec(cols):
        return pl.BlockSpec((seq, cols), lambda p, b: (b, p))

    in_specs = [seq_spec(dq), seq_spec(dq), seq_spec(LANE), seq_spec(LANE), seq_spec(LANE), seq_spec(LANE)]
    args = [q, k, v, o, do, lse]
    out_specs = [seq_spec(dq), seq_spec(dq), seq_spec(LANE)]
    out_shape = [jax.ShapeDtypeStruct((nb * seq, 3 * dq), dqk_dtype)] * 2 + [jax.ShapeDtypeStruct((nb * seq, 384), BF16)]
    if kind == "B":
        spec = pl.BlockSpec((2, tile, BAND_W), lambda p, b: (p, 0, 0))
        in_specs.append(spec)
        args.append(aux)
        out_specs.append(spec)
        out_shape.append(jax.ShapeDtypeStruct((6, tile, BAND_W), F32))
    if kind == "C":
        spec = pl.BlockSpec((1, 2, 1, seq), lambda p, b: (b, p, 0, 0))
        in_specs.append(spec)
        args.append(aux)
        out_specs += [spec, seq_spec(LANE)]
        out_shape += [jax.ShapeDtypeStruct((nb, 6, 1, seq), F32), jax.ShapeDtypeStruct((nb * seq, 384), F32)]
    return pl.pallas_call(
        body, name="attn_bwd_" + kind, grid=(3, nb), in_specs=in_specs, out_specs=out_specs, out_shape=out_shape,
        scratch_shapes=[pltpu.VMEM((dq, seq), F32), pltpu.VMEM((LANE, seq), F32)],
        compiler_params=_params(("arbitrary", "arbitrary")),
    )(*args)


BIAS_FLAT = T_BAND * BAND_W
BIAS_CHUNK = 4 * BAND_W


def _rel_onehot(chunk):
    lane = lax.broadcasted_iota(jnp.int32, (384, chunk), 1)
    r = lax.broadcasted_iota(jnp.int32, (384, chunk), 0)
    sub = jnp.where(lane >= BAND_W, 1, 0) + jnp.where(lane >= 2 * BAND_W, 1, 0) + jnp.where(lane >= 3 * BAND_W, 1, 0)
    i = pl.program_id(0) * 4 + sub
    col = lane - sub * BAND_W
    idx = jnp.clip(BAND_W - T_BAND + i - col, -REL_CLIP, REL_CLIP) + REL_CLIP
    return jnp.where(idx == r, 1.0, 0.0).astype(BF16)


BIAS_G = 768
BIAS_EDGE = BIAS_G - N_REL


def _bias_line(rel_bias):
    g = jnp.concatenate([jnp.broadcast_to(rel_bias[:, N_REL - 1:], (rel_bias.shape[0], BIAS_EDGE)),
                         jnp.flip(rel_bias, axis=1)], axis=1)
    return jnp.pad(g, ((0, 8 - g.shape[0]), (0, 0)))


def _bias_unline(dg):
    return jnp.flip(dg[:, BIAS_EDGE:], axis=1)


def _bias_expand(g8):
    def body(g_ref, out_ref):
        line = jnp.broadcast_to(g_ref[0], (T_BAND, BIAS_G))
        out_ref[0] = pltpu.roll(line, 1, 1, stride=1, stride_axis=0)[:, LANE:BIAS_G]

    return pl.pallas_call(
        body, name="bias_expand", grid=(8,), in_specs=[pl.BlockSpec((1, 1, BIAS_G), lambda h: (h, 0, 0))],
        out_specs=pl.BlockSpec((1, T_BAND, BAND_W), lambda h: (h, 0, 0)),
        out_shape=jax.ShapeDtypeStruct((8, T_BAND, BAND_W), F32), compiler_params=_params(("parallel",)),
    )(g8.reshape(8, 1, BIAS_G))


def _bias_reduce(d_rev):
    def body(d_ref, out_ref):
        wide = jnp.concatenate([jnp.zeros((T_BAND, LANE), F32), d_ref[0], jnp.zeros((T_BAND, 2 * LANE), F32)], axis=1)
        skew = pltpu.roll(wide, 0, 1, stride=1, stride_axis=0)
        dg = jnp.sum(skew, axis=0, keepdims=True)[:, LANE:LANE + BIAS_G]
        lane = _lane_iota((1, BIAS_G))
        clipped = jnp.sum(jnp.where(lane <= BIAS_EDGE, dg, 0.0), axis=1, keepdims=True)
        out_ref[0] = jnp.where(lane == BIAS_EDGE, clipped, dg)

    return pl.pallas_call(
        body, name="bias_reduce", grid=(8,), in_specs=[pl.BlockSpec((1, T_BAND, BAND_W), lambda h: (h, 0, 0))],
        out_specs=pl.BlockSpec((1, 1, BIAS_G), lambda h: (h, 0, 0)),
        out_shape=jax.ShapeDtypeStruct((8, 1, BIAS_G), F32), compiler_params=_params(("parallel",)),
    )(d_rev).reshape(8, BIAS_G)


def _outproj_bwd(dxn, y, gate, oa, ob, oc, zg, w_out_p, w_out_pt, nb, seq):
    t = dxn.shape[0]
    tpe = seq // TM

    def body(dxn_ref, y_ref, gate_ref, oa_ref, ob_ref, oc_ref, zg_ref, w_ref, wt_ref,
             doa_ref, dob_ref, doc_ref, dzg_ref, gw_ref, dgate_ref):
        i = pl.program_id(0)

        @pl.when(i == 0)
        def _():
            gw_ref[...] = jnp.zeros_like(gw_ref)

        @pl.when(i % tpe == 0)
        def _():
            dgate_ref[...] = jnp.zeros_like(dgate_ref)

        dxn_t = dxn_ref[...]
        dgate_ref[0] += jnp.sum(dxn_t * y_ref[...], axis=0, keepdims=True)
        dy = (dxn_t * gate_ref[0]).astype(BF16)
        for gi, (o_ref, do_ref) in enumerate(((oa_ref, doa_ref), (ob_ref, dob_ref), (oc_ref, doc_ref))):
            cols = slice(384 * gi, 384 * (gi + 1))
            u = zg_ref[:, cols]
            o_t = o_ref[...]
            su = _silu(u)
            dcat = _dot(dy, wt_ref[:, cols])
            do_ref[...] = dcat * su
            dzg_ref[:, cols] = (dcat * o_t * _dsilu(u)).astype(BF16)
            gw_ref[cols, :] += _dot_tn((o_t * su).astype(BF16), dy)

    return pl.pallas_call(
        body, name="outproj_bwd", grid=(t // TM,),
        in_specs=[_row_spec(D_MODEL), _row_spec(D_MODEL), _ex_spec(tpe), _row_spec(384), _row_spec(384), _row_spec(384),
                  _row_spec(D_CAT), _full_spec((D_CAT, D_MODEL)), _full_spec((D_MODEL, D_CAT))],
        out_specs=[_row_spec(384), _row_spec(384), _row_spec(384), _row_spec(D_CAT), _full_spec((D_CAT, D_MODEL)),
                   _ex_spec(tpe)],
        out_shape=[jax.ShapeDtypeStruct((t, 384), F32)] * 3 + [jax.ShapeDtypeStruct((t, D_CAT), BF16),
                                                                jax.ShapeDtypeStruct((D_CAT, D_MODEL), F32),
                                                                jax.ShapeDtypeStruct((nb, 1, D_MODEL), F32)],
        compiler_params=_params(("arbitrary",)),
    )(dxn, y, gate, oa, ob, oc, zg, w_out_p, w_out_pt)


def _a_up_bwd(dqa, dka, dva, za, gq, gkv, w_uq_pt, w_ukv_pt, cos, sina, sinb):
    t = za.shape[0]

    def body(dq_ref, dk_ref, dv_ref, za_ref, gq_ref, gkv_ref, wqt_ref, wkvt_ref, cos_ref, sa_ref, sb_ref,
             dza_ref, gwq_ref, gwkv_ref, ggq_ref, ggkv_ref, dqb, dkvb):
        @pl.when(pl.program_id(0) == 0)
        def _():
            gwq_ref[...] = jnp.zeros_like(gwq_ref)
            gwkv_ref[...] = jnp.zeros_like(gwkv_ref)
            ggq_ref[...] = jnp.zeros_like(ggq_ref)
            ggkv_ref[...] = jnp.zeros_like(ggkv_ref)

        cos_t, sa, sb = cos_ref[...], sa_ref[...], sb_ref[...]
        dkpe = jnp.zeros((TM, LANE), F32)
        for p in range(3):
            dqb[:, 256 * p:256 * p + 128] = dq_ref[:, 256 * p:256 * p + 128].astype(BF16)
            dqb[:, 256 * p + 128:256 * p + 256] = _rope_t(dq_ref[:, 256 * p + 128:256 * p + 256], cos_t, sa, sb).astype(BF16)
            dkvb[:, 128 * p:128 * p + 128] = dk_ref[:, 256 * p:256 * p + 128].astype(BF16)
            dkpe = dkpe + dk_ref[:, 256 * p + 128:256 * p + 256]
        dkvb[:, 384:768] = dv_ref[...]
        dkpe = _rope_t(dkpe, cos_t, sa, sb)
        dkpe = jnp.where(_lane_iota((TM, LANE)) < A_ROPE, dkpe + pltpu.roll(dkpe, LANE - 32, 1), 0.0)

        gqv = gq_ref[...]
        cqn, cqh, rq = _rms(za_ref[:, 0:384], gqv)
        dq_t = dqb[...]
        gwq_ref[...] += _dot_tn(cqn.astype(BF16), dq_t)
        dcqn = _dot(dq_t, wqt_ref[...])
        ggq_ref[...] += jnp.broadcast_to(jnp.sum(dcqn * cqh, axis=0, keepdims=True), ggq_ref.shape)
        dza_ref[:, 0:384] = _rms_bwd(dcqn, cqh, rq, gqv).astype(BF16)

        gkvv = gkv_ref[...]
        ckvn, ckvh, rkv = _rms(za_ref[:, 384:640], gkvv)
        dkv_t = dkvb[...]
        gwkv_ref[...] += _dot_tn(ckvn.astype(BF16), dkv_t)
        dckvn = _dot(dkv_t, wkvt_ref[...])
        ggkv_ref[...] += jnp.broadcast_to(jnp.sum(dckvn * ckvh, axis=0, keepdims=True), ggkv_ref.shape)
        dza_ref[:, 384:640] = _rms_bwd(dckvn, ckvh, rkv, gkvv).astype(BF16)
        dza_ref[:, 640:768] = dkpe.astype(BF16)

    return pl.pallas_call(
        body, name="a_up_bwd", grid=(t // TM,),
        in_specs=[_row_spec(768), _row_spec(768), _row_spec(384), _row_spec(768), _full_spec((1, 384)),
                  _full_spec((1, 256)), _full_spec((768, 384)), _full_spec((768, 256)), _row_spec(LANE), _row_spec(LANE),
                  _row_spec(LANE)],
        out_specs=[_row_spec(768), _full_spec((384, 768)), _full_spec((256, 768)), _full_spec((8, 384)),
                   _full_spec((8, 256))],
        out_shape=[jax.ShapeDtypeStruct((t, 768), BF16), jax.ShapeDtypeStruct((384, 768), F32),
                   jax.ShapeDtypeStruct((256, 768), F32), jax.ShapeDtypeStruct((8, 384), F32),
                   jax.ShapeDtypeStruct((8, 256), F32)],
        scratch_shapes=[pltpu.VMEM((TM, 768), BF16), pltpu.VMEM((TM, 768), BF16)],
        compiler_params=_params(("arbitrary",)),
    )(dqa, dka, dva, za, gq, gkv, w_uq_pt, w_ukv_pt, cos, sina, sinb)


def _dz_cols():
    return (Z_A, Z_G) + Z_QKV + (Z_F,)


def _inproj_bwd_dx(dz, dxn, x, shift, scale, g, w_in_pt, nb, seq):
    t = x.shape[0]
    tpe = seq // TM
    cols = _dz_cols()

    def body(*refs):
        dz_refs = refs[:len(cols)]
        dxn_ref, x_ref, sh_ref, sc_ref, g_ref, wt_ref, dx_ref, dsh_ref, dsc_ref, dg_ref = refs[len(cols):]
        i = pl.program_id(0)

        @pl.when(i == 0)
        def _():
            dg_ref[...] = jnp.zeros_like(dg_ref)

        @pl.when(i % tpe == 0)
        def _():
            dsh_ref[...] = jnp.zeros_like(dsh_ref)
            dsc_ref[...] = jnp.zeros_like(dsc_ref)

        dh = jnp.zeros((TM, D_MODEL), F32)
        for ref, (c0, c1) in zip(dz_refs, cols):
            dh = dh + _dot(ref[...], wt_ref[c0:c1, :])
        gv = g_ref[...]
        n, xh, r = _rms(x_ref[...], gv)
        dsh_ref[0] += jnp.sum(dh, axis=0, keepdims=True)
        dsc_ref[0] += jnp.sum(dh * n, axis=0, keepdims=True)
        dn = dh * (1.0 + sc_ref[0])
        dg_ref[...] += jnp.broadcast_to(jnp.sum(dn * xh, axis=0, keepdims=True), dg_ref.shape)
        dx_ref[...] = dxn_ref[...] + _rms_bwd(dn, xh, r, gv)

    in_specs = [_row_spec(c1 - c0) for c0, c1 in cols]
    in_specs += [_row_spec(D_MODEL), _row_spec(D_MODEL), _ex_spec(tpe), _ex_spec(tpe), _full_spec((1, D_MODEL)),
                 _full_spec((NP_IN, D_MODEL))]
    return pl.pallas_call(
        body, name="inproj_bwd_dx", grid=(t // TM,), in_specs=in_specs,
        out_specs=[_row_spec(D_MODEL), _ex_spec(tpe), _ex_spec(tpe), _full_spec((8, D_MODEL))],
        out_shape=[jax.ShapeDtypeStruct((t, D_MODEL), F32), jax.ShapeDtypeStruct((nb, 1, D_MODEL), F32),
                   jax.ShapeDtypeStruct((nb, 1, D_MODEL), F32), jax.ShapeDtypeStruct((8, D_MODEL), F32)],
        compiler_params=_params(("arbitrary",)),
    )(*dz, dxn, x, shift, scale, g, w_in_pt)


def _inproj_bwd_dw(h, dz, name):
    t = h.shape[0]
    widths = [d.shape[1] for d in dz]
    total = sum(widths)

    def body(*refs):
        h_ref = refs[0]
        dz_refs = refs[1:1 + len(dz)]
        gw_ref = refs[1 + len(dz)]

        @pl.when(pl.program_id(0) == 0)
        def _():
            gw_ref[...] = jnp.zeros_like(gw_ref)

        h_t = h_ref[...]
        c0 = 0
        for ref, w in zip(dz_refs, widths):
            gw_ref[:, c0:c0 + w] += _dot_tn(h_t, ref[...])
            c0 += w

    return pl.pallas_call(
        body, name=name, grid=(t // TM,), in_specs=[_row_spec(D_MODEL)] + [_row_spec(w) for w in widths],
        out_specs=_full_spec((D_MODEL, total)), out_shape=jax.ShapeDtypeStruct((D_MODEL, total), F32),
        compiler_params=_params(("arbitrary",)),
    )(h, *dz)


def _ada_fwd(c_all, w_ada, b_cols):
    n = c_all.shape[0]
    cols = w_ada.shape[2]

    def body(c_ref, w_ref, b_ref, out_ref):
        act = _silu(c_ref[...]).astype(BF16)
        out_ref[0] = _dot(act, w_ref[0].astype(BF16)) + b_ref[0]

    return pl.pallas_call(
        body, name="ada_fwd", grid=(DEPTH,),
        in_specs=[pl.BlockSpec((n, D_MODEL), lambda l: (0, 0)), pl.BlockSpec((1, D_MODEL, cols), lambda l: (l, 0, 0)),
                  pl.BlockSpec((1, 1, cols), lambda l: (l, 0, 0))],
        out_specs=pl.BlockSpec((1, n, cols), lambda l: (l, 0, 0)),
        out_shape=jax.ShapeDtypeStruct((DEPTH, n, cols), F32), compiler_params=_params(("parallel",)),
    )(c_all, w_ada, b_cols)


def _ada_bwd(c_all, dmod_cols, dmod_all):
    n = c_all.shape[0]
    cols = dmod_cols.shape[2]

    def body(c_ref, dc_ref, da_ref, gw_ref, gb_ref):
        act = _silu(c_ref[...]).astype(BF16)
        gw_ref[0] = _dot_tn(act, dc_ref[0].astype(BF16))
        gb_ref[0] = jnp.sum(da_ref[0], axis=0, keepdims=True)

    return pl.pallas_call(
        body, name="ada_bwd", grid=(DEPTH,),
        in_specs=[pl.BlockSpec((n, D_MODEL), lambda l: (0, 0)), pl.BlockSpec((1, n, cols), lambda l: (l, 0, 0)),
                  pl.BlockSpec((1, n, 3 * D_MODEL), lambda l: (l, 0, 0))],
        out_specs=[pl.BlockSpec((1, D_MODEL, cols), lambda l: (l, 0, 0)),
                   pl.BlockSpec((1, 1, 3 * D_MODEL), lambda l: (l, 0, 0))],
        out_shape=[jax.ShapeDtypeStruct((DEPTH, D_MODEL, cols), F32), jax.ShapeDtypeStruct((DEPTH, 1, 3 * D_MODEL), F32)],
        compiler_params=_params(("parallel",)),
    )(c_all, dmod_cols, dmod_all)


def _sum_blocks(parts, name):
    n, rows, cols = parts.shape
    tr = rows if rows <= 256 else 8 * next(d for d in range(32, 0, -1) if (rows // 8) % d == 0)

    def body(p_ref, out_ref):
        acc = p_ref[0].astype(F32)
        for k in range(1, n):
            acc = acc + p_ref[k].astype(F32)
        out_ref[...] = acc

    return pl.pallas_call(
        body, name=name, grid=(rows // tr,), in_specs=[pl.BlockSpec((n, tr, cols), lambda i: (0, i, 0))],
        out_specs=pl.BlockSpec((tr, cols), lambda i: (i, 0)), out_shape=jax.ShapeDtypeStruct((rows, cols), F32),
        compiler_params=_params(("parallel",)),
    )(parts)


def _adamw(w, g, m, v, name):
    rows, cols = w.shape
    tr = next(t for t in (rows, 256, 128, 64, 32, 16, 8) if rows % t == 0 and t * cols <= 256 * 1024)

    def body(w_ref, g_ref, m_ref, v_ref, d_ref, mo_ref, vo_ref):
        gv = g_ref[...]
        mn = ADAM_B1 * m_ref[...] + (1.0 - ADAM_B1) * gv
        vn = ADAM_B2 * v_ref[...] + (1.0 - ADAM_B2) * jnp.square(gv)
        m_hat = mn / (1.0 - ADAM_B1 ** ADAM_STEP)
        v_hat = vn / (1.0 - ADAM_B2 ** ADAM_STEP)
        d_ref[...] = -ADAM_LR * (m_hat / (jnp.sqrt(v_hat) + ADAM_EPS) + ADAM_WD * w_ref[...])
        mo_ref[...] = mn
        vo_ref[...] = vn

    spec = pl.BlockSpec((tr, cols), lambda i: (i, 0))
    return pl.pallas_call(
        body, name=name, grid=(rows // tr,), in_specs=[spec] * 4, out_specs=[spec] * 3,
        out_shape=[jax.ShapeDtypeStruct((rows, cols), F32)] * 3, compiler_params=_params(("parallel",)),
    )(w, g, m, v)


ALL_FLIPS = tuple(range(1, N_DEV))


def _exchange(src, flips, mode, name):
    _, rows, cols = src.shape
    nslot = 2 if mode == "pair" else N_DEV
    nf = len(flips)

    def body(src_ref, dst_ref, send_sems, recv_sems, local_sem):
        x, y, c = lax.axis_index("x"), lax.axis_index("y"), lax.axis_index("c")
        me = 4 * x + 2 * y + c

        def slot(j):
            return (j & 1) if mode == "pair" else j

        own = pltpu.make_async_copy(src_ref.at[me if mode == "scatter" else 0], dst_ref.at[slot(me)], local_sem)
        own.start()
        copies = []
        for i, f in enumerate(flips):
            peer = me ^ f
            to = (1 - x if f & 4 else x, 1 - y if f & 2 else y, 1 - c if f & 1 else c)
            cp = pltpu.make_async_remote_copy(
                src_ref=src_ref.at[peer if mode == "scatter" else 0], dst_ref=dst_ref.at[slot(me)],
                send_sem=send_sems.at[i], recv_sem=recv_sems.at[i], device_id=to, device_id_type=pl.DeviceIdType.MESH)
            cp.start()
            copies.append(cp)
        for i, f in enumerate(flips):
            peer = me ^ f
            to = (1 - x if f & 4 else x, 1 - y if f & 2 else y, 1 - c if f & 1 else c)
            pltpu.make_async_remote_copy(
                src_ref=src_ref.at[0], dst_ref=dst_ref.at[slot(peer)], send_sem=send_sems.at[i],
                recv_sem=recv_sems.at[i], device_id=to, device_id_type=pl.DeviceIdType.MESH).wait_recv()
        for cp in copies:
            cp.wait_send()
        own.wait()

    return pl.pallas_call(
        body, name=name, out_shape=jax.ShapeDtypeStruct((nslot, rows, cols), src.dtype),
        in_specs=[pl.BlockSpec(memory_space=pl.ANY)], out_specs=pl.BlockSpec(memory_space=pl.ANY),
        scratch_shapes=[pltpu.SemaphoreType.DMA((nf,)), pltpu.SemaphoreType.DMA((nf,)), pltpu.SemaphoreType.DMA],
    )(src)


def _transfer(name, srcs, dst_shapes, plan):
    n_arr = len(srcs)
    probe = plan(0, 0, 0)
    n_steps = len(probe)

    def body(*refs):
        src_refs, dst_refs = refs[:n_arr], refs[n_arr:2 * n_arr]
        send_sems, recv_sems, local_sems = refs[2 * n_arr:]
        x, y, c = lax.axis_index("x"), lax.axis_index("y"), lax.axis_index("c")
        steps = plan(x, y, c)

        def rows(ref, r0, n):
            return ref.at[:, pl.ds(r0, n), :]

        def arrival(t):
            a, _, _, n, _, f, _ = steps[t]
            return pltpu.make_async_remote_copy(
                src_ref=rows(dst_refs[a], 0, n), dst_ref=rows(dst_refs[a], 0, n), send_sem=send_sems.at[t],
                recv_sem=recv_sems.at[t], device_id=(x, y, c), device_id_type=pl.DeviceIdType.MESH)

        arrived, started = set(), []
        for t, (a, from_dst, sr, n, dr, f, after) in enumerate(steps):
            for u in after:
                if u not in arrived:
                    arrival(u).wait_recv()
                    arrived.add(u)
            src = rows(dst_refs[a] if from_dst else src_refs[a], sr, n)
            dst = rows(dst_refs[a], dr, n)
            if f == 0:
                cp = pltpu.make_async_copy(src, dst, local_sems.at[t])
            else:
                to = (1 - x if f & 4 else x, 1 - y if f & 2 else y, 1 - c if f & 1 else c)
                cp = pltpu.make_async_remote_copy(src_ref=src, dst_ref=dst, send_sem=send_sems.at[t],
                                                  recv_sem=recv_sems.at[t], device_id=to,
                                                  device_id_type=pl.DeviceIdType.MESH)
            cp.start()
            started.append(cp)
        for t, step in enumerate(steps):
            if step[5] != 0 and t not in arrived:
                arrival(t).wait_recv()
        for cp, step in zip(started, steps):
            if step[5] == 0:
                cp.wait()
            else:
                cp.wait_send()

    any_spec = pl.BlockSpec(memory_space=pl.ANY)
    return pl.pallas_call(
        body, name=name, out_shape=[jax.ShapeDtypeStruct(s, d) for s, d in dst_shapes],
        in_specs=[any_spec] * n_arr, out_specs=[any_spec] * n_arr,
        scratch_shapes=[pltpu.SemaphoreType.DMA((n_steps,)), pltpu.SemaphoreType.DMA((n_steps,)),
                        pltpu.SemaphoreType.DMA((n_steps,))],
    )(*srcs)


CHIP_FLIPS = (2, 4, 6)


def _gather_plan(chip_rows):
    def plan(x, y, c):
        steps = []
        for a, rc in enumerate(chip_rows):
            h = rc // 2
            mine = rc * (2 * x + y) + h * c
            steps.append((a, False, h * c, h, mine, 0, ()))
            ici = {}
            for f in CHIP_FLIPS:
                ici[f] = len(steps)
                steps.append((a, False, h * c, h, mine, f, ()))
            steps.append((a, False, h * c, h, mine, 1, ()))
            for f in CHIP_FLIPS:
                theirs = rc * ((2 * x + y) ^ (f >> 1)) + h * c
                steps.append((a, True, theirs, h, theirs, 1, (ici[f],)))
        return steps
    return plan


def _pair_reduce_plan(chip_rows):
    def plan(x, y, c):
        steps = []
        for a, rc in enumerate(chip_rows):
            h = rc // 2
            for j in range(4):
                steps.append((a, False, rc * j + h * (1 - c), h, h * j, 1, ()))
        return steps
    return plan


def _chip_scatter_plan(chip_rows):
    def plan(x, y, c):
        steps = []
        for a, rc in enumerate(chip_rows):
            h = rc // 2
            for k, f in enumerate(CHIP_FLIPS):
                steps.append((a, False, h * ((2 * x + y) ^ (f >> 1)), h, h * k, f, ()))
        return steps
    return plan


def _pair_share_plan(chip_rows):
    def plan(x, y, c):
        return [(a, False, 0, rc // 2, 0, 1, ()) for a, rc in enumerate(chip_rows)]
    return plan


def _tile_rows(h):
    return next(t for t in (64, 32, 16) if h % t == 0)


def _sum_pair(partial, recv, core, rc, name):
    nl, _, cols = partial.shape
    h = rc // 2
    tr = _tile_rows(h)

    def body(c_ref, p_ref, r_ref, out_ref):
        out_ref[...] = (p_ref[...] + r_ref[...]).astype(BF16)

    spec = pl.BlockSpec((1, tr, cols), lambda l, j, i, c_ref: (l, (h // tr) * j + i, 0))
    return pl.pallas_call(
        body, name=name, out_shape=jax.ShapeDtypeStruct((nl, 4 * h, cols), BF16),
        grid_spec=pltpu.PrefetchScalarGridSpec(
            num_scalar_prefetch=1, grid=(nl, 4, h // tr),
            in_specs=[pl.BlockSpec((1, tr, cols), lambda l, j, i, c_ref: (l, (rc // tr) * j + (h // tr) * c_ref[0] + i, 0)),
                      spec],
            out_specs=spec),
        compiler_params=_params(("parallel", "parallel", "parallel")),
    )(core, partial, recv)


def _sum_chips(chip_sum, recv, chip, rc, name):
    nl, _, cols = chip_sum.shape
    h = rc // 2
    tr = _tile_rows(h)

    def body(j_ref, own_ref, r_ref, out_ref):
        acc = own_ref[0].astype(F32)
        for k in range(3):
            acc = acc + r_ref[0, k].astype(F32)
        out_ref[0] = acc

    return pl.pallas_call(
        body, name=name, out_shape=jax.ShapeDtypeStruct((nl, h, cols), F32),
        grid_spec=pltpu.PrefetchScalarGridSpec(
            num_scalar_prefetch=1, grid=(nl, h // tr),
            in_specs=[pl.BlockSpec((1, tr, cols), lambda l, i, j_ref: (l, (h // tr) * j_ref[0] + i, 0)),
                      pl.BlockSpec((1, 3, tr, cols), lambda l, i, j_ref: (l, 0, i, 0))],
            out_specs=pl.BlockSpec((1, tr, cols), lambda l, i, j_ref: (l, i, 0))),
        compiler_params=_params(("parallel", "parallel")),
    )(chip, chip_sum, recv.reshape(nl, 3, h, cols))


def _sum_slots(parts, out_dtype, name):
    nl, n, rows, cols = parts.shape
    tr = next(t for t in (128, 64, 32, 16) if rows % t == 0 and n * t * cols * 4 <= (4 << 20))

    def body(p_ref, out_ref):
        acc = p_ref[0, 0].astype(F32)
        for k in range(1, n):
            acc = acc + p_ref[0, k].astype(F32)
        out_ref[0] = acc.astype(out_dtype)

    return pl.pallas_call(
        body, name=name, grid=(nl, rows // tr),
        in_specs=[pl.BlockSpec((1, n, tr, cols), lambda l, i: (l, 0, i, 0))],
        out_specs=pl.BlockSpec((1, tr, cols), lambda l, i: (l, i, 0)),
        out_shape=jax.ShapeDtypeStruct((nl, rows, cols), out_dtype), compiler_params=_params(("parallel", "parallel")),
    )(parts)


def _pad_cols(a, n):
    return a if n == 0 else jnp.pad(a, ((0, 0), (0, n)))


def _in_to_padded(w):
    return jnp.concatenate([_pad_cols(w[:, a:b], z) for a, b, z in IN_PIECES], axis=1)


def _in_from_padded(gp):
    pos, out = 0, {}
    for a, b, z in IN_PIECES:
        out[a] = gp[:, pos:pos + (b - a)]
        pos += (b - a) + z
    return jnp.concatenate([out[a] for a in sorted(out)], axis=1)


def _out_to_padded(w):
    z = jnp.zeros((64, w.shape[1]), w.dtype)
    return jnp.concatenate([w[0:384], w[384:704], z, w[704:1024], z], axis=0)


def _out_from_padded(gp):
    return jnp.concatenate([gp[0:384], gp[384:704], gp[768:1088]], axis=0)


def _uq_to_padded(w):
    parts = []
    for p in range(3):
        h0, h1 = 2 * p, 2 * p + 1
        parts += [w[:, 96 * h0:96 * h0 + 64], w[:, 96 * h1:96 * h1 + 64], w[:, 96 * h0 + 64:96 * h0 + 96],
                  w[:, 96 * h1 + 64:96 * h1 + 96], jnp.zeros((w.shape[0], 64), w.dtype)]
    return jnp.concatenate(parts, axis=1)


def _uq_from_padded(gp):
    parts = []
    for h in range(6):
        p, s = h // 2, h % 2
        parts += [gp[:, 256 * p + 64 * s:256 * p + 64 * s + 64], gp[:, 256 * p + 128 + 32 * s:256 * p + 160 + 32 * s]]
    return jnp.concatenate(parts, axis=1)


def _ukv_to_padded(w):
    return jnp.concatenate([w[:, 128 * h:128 * h + 64] for h in range(6)]
                           + [w[:, 128 * h + 64:128 * h + 128] for h in range(6)], axis=1)


def _ukv_from_padded(gp):
    parts = []
    for h in range(6):
        parts += [gp[:, 64 * h:64 * h + 64], gp[:, 384 + 64 * h:384 + 64 * h + 64]]
    return jnp.concatenate(parts, axis=1)


LR_ROWS = 224
SHARD_ROWS = (256, 256, LR_ROWS)


def _pack_lowrank(w_uq, w_ukv):
    flat = jnp.concatenate([w_uq.reshape(-1), w_ukv.reshape(-1)])
    return jnp.pad(flat, (0, LR_ROWS * PACK_COLS - flat.shape[0])).reshape(1, LR_ROWS, PACK_COLS)


def _unpack_lowrank(packed):
    flat = packed.reshape(-1)
    n_uq = DEPTH * A_Q_RANK * 144
    n_ukv = DEPTH * A_KV_RANK * 192
    return flat[0:n_uq].reshape(DEPTH, A_Q_RANK, 144), flat[n_uq:n_uq + n_ukv].reshape(DEPTH, A_KV_RANK, 192)


PACK_SIZES = (DEPTH * 256 * N_IN, DEPTH * 256 * D_MODEL, DEPTH * A_Q_RANK * 144, DEPTH * A_KV_RANK * 192)


def _pack(w_in, w_out, w_uq, w_ukv):
    flat = jnp.concatenate([w_in.reshape(-1), w_out.reshape(-1), w_uq.reshape(-1), w_ukv.reshape(-1)])
    flat = jnp.pad(flat, (0, PACK_ROWS * PACK_COLS - flat.shape[0]))
    return flat.reshape(PACK_ROWS, PACK_COLS)


def _unpack(packed):
    flat = packed.reshape(-1)
    o0, o1, o2, o3 = PACK_SIZES
    w_in = flat[0:o0].reshape(DEPTH, 256, N_IN)
    w_out = flat[o0:o0 + o1].reshape(DEPTH, 256, D_MODEL)
    w_uq = flat[o0 + o1:o0 + o1 + o2].reshape(DEPTH, A_Q_RANK, 144)
    w_ukv = flat[o0 + o1 + o2:o0 + o1 + o2 + o3].reshape(DEPTH, A_KV_RANK, 192)
    return w_in, w_out, w_uq, w_ukv


def _rope_tables(positions):
    inv = ROPE_THETA ** (-jnp.arange(0, A_ROPE, 2, dtype=F32) / A_ROPE)
    ang = positions.astype(F32)[..., None] * inv
    cos = jnp.tile(jnp.cos(ang), (1, 1, 4)).reshape(-1, 64)
    sin = jnp.tile(jnp.sin(ang), (1, 1, 4)).reshape(-1, 64)
    second = (jnp.arange(64) % 32) >= 16
    pad = ((0, 0), (0, 64))
    return (jnp.pad(cos, pad), jnp.pad(jnp.where(second, 0.0, -sin), pad), jnp.pad(jnp.where(second, sin, 0.0), pad))


def _rows(a, n):
    flat = a.reshape(-1)
    return jnp.pad(flat, (0, n * LANE - flat.shape[0])).reshape(n, LANE)


def _forward_backward(x, mod, tables, target, weights, small, nb, seq):
    cos, sina, sinb = tables
    saved = []
    for l in range(DEPTH):
        w, s = weights[l], small[l]
        shift = mod[l][:, None, 0:D_MODEL]
        scale = mod[l][:, None, D_MODEL:2 * D_MODEL]
        gate = mod[l][:, None, 2 * D_MODEL:]
        h, za, zg, qb, kb, vb, qc, kc, vc, zf = _ln_inproj(x, shift, scale, s["norm_g"], w["in"], seq)
        qa, ka, va = _a_up(za, s["gq"], s["gkv"], w["uq"], w["ukv"], cos, sina, sinb)
        bias = _bias_expand(s["g8"])[0:6]
        f = _forget_fwd(zf, s["fb"], nb, seq)
        frow = jnp.pad(f[:, 0:5].reshape(nb, seq, 5).transpose(0, 2, 1), ((0, 0), (0, 1), (0, 0)))
        frow = frow.reshape(nb, 6, 1, seq)
        oa, lse_a = _attn_fwd("A", qa, ka, va, None, nb, seq)
        ob, lse_b = _attn_fwd("B", qb, kb, vb, bias, nb, seq)
        oc, lse_c = _attn_fwd("C", qc, kc, vc, frow, nb, seq)
        y, xn = _gate_outproj(x, gate, oa, ob, oc, zg, w["out"], seq)
        saved.append(dict(x=x, h=h, za=za, zg=zg, zf=zf, y=y, shift=shift, scale=scale, gate=gate, bias=bias, frow=frow,
                          a=(qa, ka, va, oa, lse_a), b=(qb, kb, vb, ob, lse_b), c=(qc, kc, vc, oc, lse_c)))
        x = xn
    dx, loss8, gfinal8 = _final_loss(x, target, small[0]["final_g"])
    grads = []
    for l in reversed(range(DEPTH)):
        w, s, sv = weights[l], small[l], saved[l]
        qa, ka, va, oa, lse_a = sv["a"]
        qb, kb, vb, ob, lse_b = sv["b"]
        qc, kc, vc, oc, lse_c = sv["c"]
        doa, dob, doc, dzg, gw_out, dgate = _outproj_bwd(dx, sv["y"], sv["gate"], oa, ob, oc, sv["zg"], w["out"],
                                                          w["out_t"], nb, seq)
        dqa, dka, dva = _attn_bwd("A", qa, ka, va, oa, doa, lse_a, None, nb, seq)
        dqb, dkb, dvb, dbt = _attn_bwd("B", qb, kb, vb, ob, dob, lse_b, sv["bias"], nb, seq)
        dqc, dkc, dvc, dfr, dfq = _attn_bwd("C", qc, kc, vc, oc, doc, lse_c, sv["frow"], nb, seq)
        dg = _bias_reduce(jnp.pad(jnp.flip(dbt, axis=1), ((0, 2), (0, 0), (0, 0))))
        grb = jnp.pad(_bias_unline(dg), ((0, 0), (0, 384 - N_REL)))
        df = dfr.reshape(nb, 6, seq).transpose(0, 2, 1).reshape(nb * seq, 6) + dfq.reshape(nb * seq, 6, 64)[:, :, 0]
        dzf, gfb = _forget_bwd(jnp.pad(df, ((0, 0), (0, LANE - 6))), sv["zf"], s["fb"], nb, seq)
        dza, gw_uq, gw_ukv, ggq, ggkv = _a_up_bwd(dqa, dka, dva, sv["za"], s["gq"], s["gkv"], w["uq_t"], w["ukv_t"],
                                                  cos, sina, sinb)
        dz = (dza, dzg, dqb, dkb, dvb, dqc, dkc, dvc, dzf)
        dx, dshift, dscale, gnorm = _inproj_bwd_dx(dz, dx, sv["x"], sv["shift"], sv["scale"], s["norm_g"], w["in_t"],
                                                   nb, seq)
        gw_in = jnp.concatenate([_inproj_bwd_dw(sv["h"], dz[0:2], "inproj_bwd_dw0"),
                                 _inproj_bwd_dw(sv["h"], dz[2:], "inproj_bwd_dw1")], axis=1)
        dmod = jnp.concatenate([dshift[:, 0], dscale[:, 0], dgate[:, 0]], axis=1)
        grads.append(dict(w_in=gw_in, w_out=gw_out, w_uq=gw_uq, w_ukv=gw_ukv, dmod=dmod, norm_g=gnorm[0], gq=ggq[0],
                          gkv=ggkv[0], rb8=grb, fb=gfb[0]))
    grads.reverse()
    return loss8[0, 0], dx, grads, gfinal8[0]


def _layer_weights(w_in, w_out, w_uq, w_ukv):
    wi, wo, wq, wkv = _in_to_padded(w_in), _out_to_padded(w_out), _uq_to_padded(w_uq), _ukv_to_padded(w_ukv)
    return {"in": wi, "in_t": wi.T, "out": wo, "out_t": wo.T, "uq": wq, "uq_t": wq.T, "ukv": wkv, "ukv_t": wkv.T}


def _layer_small(norm_g, gq, gkv, rel_bias, forget_b, final_g):
    fb = jnp.pad(forget_b, (0, LANE - 5)).reshape(1, LANE)
    return dict(norm_g=norm_g.reshape(1, -1), gq=gq.reshape(1, -1), gkv=gkv.reshape(1, -1), g8=_bias_line(rel_bias), fb=fb,
                final_g=final_g.reshape(1, -1))


def _small_payload(per_layer, final_g, loss):
    def stack(key):
        return jnp.stack([p[key] for p in per_layer])

    def rows(a, rng):
        return _rows(a, rng[1] - rng[0])

    dmod = stack("dmod") if "dmod" in per_layer[0] else jnp.zeros((LANE,), F32)
    parts = [rows(dmod, PAY_DMOD), rows(stack("norm_g"), PAY_NORM), rows(stack("gq"), PAY_GQ),
             rows(stack("gkv"), PAY_GKV), rows(stack("rb8"), PAY_RB), rows(stack("fb"), PAY_FB),
             rows(final_g, PAY_FINAL), rows(loss, PAY_LOSS)]
    return jnp.concatenate(parts, axis=0)


def _payload_split(pay):
    def take(rng, shape):
        n = 1
        for d in shape:
            n *= d
        return pay[rng[0]:rng[1]].reshape(-1)[0:n].reshape(shape)

    norm_g = take(PAY_NORM, (DEPTH, D_MODEL))
    gq = take(PAY_GQ, (DEPTH, A_Q_RANK))
    gkv = take(PAY_GKV, (DEPTH, A_KV_RANK))
    rb = take(PAY_RB, (DEPTH, 8, 384))[:, 0:5, 0:N_REL]
    fb = take(PAY_FB, (DEPTH, LANE))[:, 0:5]
    final_g = take(PAY_FINAL, (D_MODEL,))
    return norm_g, gq, gkv, rb, fb, final_g


def kernel(x, c, positions, w_ada, b_ada, norm_g, w_in, a_q_norm_g, a_w_uq, a_kv_norm_g, a_w_ukv, b_rel_bias, c_forget_b, w_out, final_g, loss_target, m_w_ada, m_b_ada, m_norm_g, m_w_in, m_a_q_norm_g, m_a_w_uq, m_a_kv_norm_g, m_a_w_ukv, m_b_rel_bias, m_c_forget_b, m_w_out, m_final_g, v_w_ada, v_b_ada, v_norm_g, v_w_in, v_a_q_norm_g, v_a_w_uq, v_a_kv_norm_g, v_a_w_ukv, v_b_rel_bias, v_c_forget_b, v_w_out, v_final_g):
    nb, seq, _ = x.shape
    ix, iy, ic = lax.axis_index("x"), lax.axis_index("y"), lax.axis_index("c")
    chip = 2 * ix + iy
    me = 2 * chip + ic

    full_in, full_out, full_lr = _transfer(
        "gather_weights", [w_in.astype(BF16), w_out.astype(BF16), _pack_lowrank(a_w_uq, a_w_ukv).astype(BF16)],
        [((DEPTH, D_MODEL, N_IN), BF16), ((DEPTH, D_MODEL, D_MODEL), BF16), ((1, 4 * LR_ROWS, PACK_COLS), BF16)],
        _gather_plan(SHARD_ROWS))
    lowrank = [_unpack_lowrank(full_lr[0, LR_ROWS * j:LR_ROWS * (j + 1)]) for j in range(4)]
    full_uq = jnp.concatenate([s[0] for s in lowrank], axis=2)
    full_ukv = jnp.concatenate([s[1] for s in lowrank], axis=2)
    weights = [_layer_weights(full_in[l], full_out[l], full_uq[l], full_ukv[l]) for l in range(DEPTH)]
    small = [_layer_small(norm_g[l], a_q_norm_g[l], a_kv_norm_g[l], b_rel_bias[l], c_forget_b[l], final_g)
             for l in range(DEPTH)]

    c_all = _exchange(c[None], ALL_FLIPS, "gather", "gather_c").reshape(N_DEV * nb, D_MODEL)
    cols = w_ada.shape[2]
    b_cols = lax.dynamic_slice_in_dim(b_ada, chip * cols, cols, axis=1)[:, None, :]
    mod_cols = _ada_fwd(c_all, w_ada, b_cols)
    mod_g = _exchange(mod_cols.reshape(1, DEPTH * N_DEV * nb, cols), ALL_FLIPS, "gather", "gather_mod")
    mod_all = jnp.concatenate([mod_g[2 * j].reshape(DEPTH, N_DEV * nb, cols) for j in range(4)], axis=2)
    mod = lax.dynamic_slice_in_dim(mod_all, me * nb, nb, axis=1)

    tables = _rope_tables(positions)
    loss_part, dx, grads, gfinal = _forward_backward(
        x.reshape(nb * seq, D_MODEL), mod, tables, loss_target.reshape(nb * seq, D_MODEL), weights, small, nb, seq)

    pay = _small_payload(grads, gfinal, loss_part)
    pay_all = _exchange(pay[None], ALL_FLIPS, "gather", "gather_small")
    tot = _sum_blocks(pay_all, "sum_small")
    loss = tot[PAY_LOSS[0], 0]
    dmod_all = pay_all[:, PAY_DMOD[0]:PAY_DMOD[1]].reshape(N_DEV, -1)[:, 0:DEPTH * nb * 3 * D_MODEL]
    dmod_all = dmod_all.reshape(N_DEV, DEPTH, nb, 3 * D_MODEL).transpose(1, 0, 2, 3)
    dmod_all = dmod_all.reshape(DEPTH, N_DEV * nb, 3 * D_MODEL)
    my_cols = lax.dynamic_slice_in_dim(dmod_all, chip * cols, cols, axis=2)
    g_w_ada, g_b_ada = _ada_bwd(c_all, my_cols, dmod_all)
    g_b_ada = g_b_ada[:, 0]

    g_uq = jnp.stack([_uq_from_padded(g["w_uq"]) for g in grads])
    g_ukv = jnp.stack([_ukv_from_padded(g["w_ukv"]) for g in grads])
    g_lr = jnp.concatenate([_pack_lowrank(g_uq[:, :, 144 * j:144 * (j + 1)], g_ukv[:, :, 192 * j:192 * (j + 1)])
                            for j in range(4)], axis=1)
    partials = [jnp.stack([g["w_in"] for g in grads]), jnp.stack([_out_from_padded(g["w_out"]) for g in grads]), g_lr]
    shapes = [(p.shape[0], p.shape[2]) for p in partials]
    halves = [r // 2 for r in SHARD_ROWS]
    core_s = jnp.reshape(ic, (1,)).astype(jnp.int32)
    chip_s = jnp.reshape(chip, (1,)).astype(jnp.int32)
    from_pair = _transfer("pair_reduce", partials, [((nl, 4 * h, nc), F32) for (nl, nc), h in zip(shapes, halves)],
                          _pair_reduce_plan(SHARD_ROWS))
    chip_sums = [_sum_pair(p, r, core_s, rc, "sum_pair%d" % i)
                 for i, (p, r, rc) in enumerate(zip(partials, from_pair, SHARD_ROWS))]
    from_chips = _transfer("chip_scatter", chip_sums, [((nl, 3 * h, nc), BF16) for (nl, nc), h in zip(shapes, halves)],
                           _chip_scatter_plan(SHARD_ROWS))
    reduced = [_sum_chips(s, r, chip_s, rc, "sum_chips%d" % i)
               for i, (s, r, rc) in enumerate(zip(chip_sums, from_chips, SHARD_ROWS))]
    from_sibling = _transfer("pair_share", reduced, [((nl, h, nc), F32) for (nl, nc), h in zip(shapes, halves)],
                             _pair_share_plan(SHARD_ROWS))

    def whole(mine, other, h):
        out = jnp.zeros((mine.shape[0], 2 * h, mine.shape[2]), F32)
        out = lax.dynamic_update_slice_in_dim(out, mine, h * ic, axis=1)
        return lax.dynamic_update_slice_in_dim(out, other, h * (1 - ic), axis=1)

    g_in_p, g_out_sh, g_lr_sh = (whole(m, o, h) for m, o, h in zip(reduced, from_sibling, halves))
    g_uq_sh, g_ukv_sh = _unpack_lowrank(g_lr_sh[0])
    gw = (jnp.stack([_in_from_padded(g_in_p[l]) for l in range(DEPTH)]), g_out_sh, g_uq_sh, g_ukv_sh)

    def adam_nd(w, g, m, v, name):
        flat = (-1, w.shape[-1])
        return tuple(a.reshape(w.shape) for a in _adamw(w.reshape(flat), g.reshape(flat), m.reshape(flat),
                                                        v.reshape(flat), name))

    upd = [adam_nd(w_in, gw[0], m_w_in, v_w_in, "adamw_in"), adam_nd(w_out, gw[1], m_w_out, v_w_out, "adamw_out"),
           adam_nd(a_w_uq, gw[2], m_a_w_uq, v_a_w_uq, "adamw_uq"), adam_nd(a_w_ukv, gw[3], m_a_w_ukv, v_a_w_ukv, "adamw_ukv")]
    dw, mw, vw = (tuple(u[i] for u in upd) for i in range(3))
    ada_shape = w_ada.shape
    flat_ada = (DEPTH * D_MODEL, cols)
    d_ada, m_ada, v_ada = (a.reshape(ada_shape) for a in _adamw(
        w_ada.reshape(flat_ada), g_w_ada.reshape(flat_ada), m_w_ada.reshape(flat_ada), v_w_ada.reshape(flat_ada),
        "adamw_ada"))
    d_b, m_b, v_b = (a.reshape(DEPTH, 3 * D_MODEL) for a in _adamw(
        _rows(b_ada, 48), _rows(g_b_ada, 48), _rows(m_b_ada, 48), _rows(v_b_ada, 48), "adamw_b_ada"))

    def small_rows(ng, gq, gkv, rb, fb, fg):
        per_layer = [dict(norm_g=ng[l], gq=gq[l], gkv=gkv[l], rb8=jnp.pad(rb[l], ((0, 3), (0, 384 - N_REL))),
                          fb=jnp.pad(fb[l], (0, LANE - 5))) for l in range(DEPTH)]
        return _small_payload(per_layer, fg, jnp.zeros((), F32))

    w_s = small_rows(norm_g, a_q_norm_g, a_kv_norm_g, b_rel_bias, c_forget_b, final_g)
    m_s = small_rows(m_norm_g, m_a_q_norm_g, m_a_kv_norm_g, m_b_rel_bias, m_c_forget_b, m_final_g)
    v_s = small_rows(v_norm_g, v_a_q_norm_g, v_a_kv_norm_g, v_b_rel_bias, v_c_forget_b, v_final_g)
    d_s, mo_s, vo_s = _adamw(w_s, tot, m_s, v_s, "adamw_small")
    gs = _payload_split(tot)
    ds = _payload_split(d_s)
    ms = _payload_split(mo_s)
    vs = _payload_split(vo_s)

    def ordered(ada, b, sm, big):
        ng, gq, gkv, rb, fb, fg = sm
        b_in, b_out, b_uq, b_ukv = big
        return (ada, b, ng, b_in, gq, b_uq, gkv, b_ukv, rb, fb, b_out, fg)

    return (loss, dx.reshape(nb, seq, D_MODEL), *ordered(g_w_ada, g_b_ada, gs, gw), *ordered(d_ada, d_b, ds, dw),
            *ordered(m_ada, m_b, ms, mw), *ordered(v_ada, v_b, vs, vw))
```

```python
import functools

import jax
import jax.numpy as jnp
from jax import lax
from jax.experimental import pallas as pl
from jax.experimental.pallas import tpu as pltpu

F32 = jnp.float32
BF16 = jnp.bfloat16

D_MODEL = 1024
DEPTH = 2
EPS = 1e-6
NEG = -1e30
LOG2E = 1.4426950408889634
ROPE_THETA = 10000.0
A_ROPE = 32
A_Q_RANK = 384
A_KV_RANK = 256
REL_CLIP = 128
N_REL = 2 * REL_CLIP + 1
N_IN = 3621

ADAM_LR = 0.001
ADAM_B1 = 0.9
ADAM_B2 = 0.999
ADAM_EPS = 1e-08
ADAM_WD = 0.01
ADAM_STEP = 10

LANE = 128
VMEM_LIMIT = 56 * 1024 * 1024

NP_IN = 4352
Z_A = (0, 768)
Z_G = (768, 1920)
Z_QKV = tuple((1920 + 384 * i, 1920 + 384 * (i + 1)) for i in range(6))
Z_F = (4224, 4352)
IN_PIECES = ((0, 672, 96), (672, 1056, 0), (2016, 2336, 64), (3301, 3621, 64), (1056, 1376, 64), (1376, 1696, 64),
             (1696, 2016, 64), (2336, 2656, 64), (2656, 2976, 64), (2976, 3296, 64), (3296, 3301, 123))
D_CAT = 1152

TM = 256
T_CAUSAL = 256
T_BAND = 128
BAND_TILES = 5
N_DEV = 8

PAY_DMOD = (0, 96)
PAY_NORM = (96, 112)
PAY_GQ = (112, 120)
PAY_GKV = (120, 128)
PAY_RB = (128, 176)
PAY_FB = (176, 184)
PAY_FINAL = (184, 192)
PAY_LOSS = (192, 200)
PAY_ROWS = 200

PACK_COLS = 1024
PACK_ROWS = 2560
HALF_ROWS = PACK_ROWS // 2


def _params(sem=None):
    return pltpu.CompilerParams(dimension_semantics=sem, vmem_limit_bytes=VMEM_LIMIT)


def _lane_iota(shape):
    return lax.broadcasted_iota(jnp.int32, shape, len(shape) - 1)


def _silu(u):
    return u * jax.nn.sigmoid(u)


def _dsilu(u):
    s = jax.nn.sigmoid(u)
    return s * (1.0 + u * (1.0 - s))


def _rms(x, g):
    r = lax.rsqrt(jnp.mean(x * x, axis=-1, keepdims=True) + EPS)
    xh = x * r
    return xh * g, xh, r


def _rms_bwd(dy, xh, r, g):
    dxh = dy * g
    return r * (dxh - xh * jnp.mean(dxh * xh, axis=-1, keepdims=True))


def _rope(x, cos, sina, sinb):
    return x * cos + pltpu.roll(x, 16, 1) * sinb + pltpu.roll(x, LANE - 16, 1) * sina


def _rope_t(dy, cos, sina, sinb):
    return dy * cos + pltpu.roll(dy * sinb, LANE - 16, 1) + pltpu.roll(dy * sina, 16, 1)


def _split3(x):
    hi = x.astype(BF16)
    r1 = x - hi.astype(F32)
    mid = r1.astype(BF16)
    lo = (r1 - mid.astype(F32)).astype(BF16)
    return hi, mid, lo


def _dot(a, b):
    return jnp.dot(a, b, preferred_element_type=F32)


def _dot_nt(a, b):
    return lax.dot_general(a, b, (((1,), (1,)), ((), ())), preferred_element_type=F32)


def _dot_tn(a, b):
    return lax.dot_general(a, b, (((0,), (0,)), ((), ())), preferred_element_type=F32)


def _row_spec(cols):
    return pl.BlockSpec((TM, cols), lambda i: (i, 0))


def _full_spec(shape):
    return pl.BlockSpec(shape, lambda i: (0,) * len(shape))


def _ex_spec(tiles_per_ex):
    return pl.BlockSpec((1, 1, D_MODEL), lambda i: (i // tiles_per_ex, 0, 0))


def _ln_inproj(x, shift, scale, g, w_in_p, seq):
    t = x.shape[0]

    def body(x_ref, sh_ref, sc_ref, g_ref, w_ref, h_ref, za_ref, zg_ref, q0, q1, q2, q3, q4, q5, zf_ref):
        n, _, _ = _rms(x_ref[...], g_ref[...])
        h = (n * (1.0 + sc_ref[0]) + sh_ref[0]).astype(BF16)
        h_ref[...] = h
        za_ref[...] = _dot(h, w_ref[:, Z_A[0]:Z_A[1]])
        zg_ref[...] = _dot(h, w_ref[:, Z_G[0]:Z_G[1]])
        for ref, (c0, c1) in zip((q0, q1, q2, q3, q4, q5), Z_QKV):
            ref[...] = _dot(h, w_ref[:, c0:c1]).astype(BF16)
        zf_ref[...] = _dot(h, w_ref[:, Z_F[0]:Z_F[1]])

    tpe = seq // TM
    shapes = [jax.ShapeDtypeStruct((t, D_MODEL), BF16), jax.ShapeDtypeStruct((t, 768), F32),
              jax.ShapeDtypeStruct((t, D_CAT), F32)]
    shapes += [jax.ShapeDtypeStruct((t, 384), BF16)] * 6 + [jax.ShapeDtypeStruct((t, LANE), F32)]
    return pl.pallas_call(
        body, name="ln_inproj", grid=(t // TM,),
        in_specs=[_row_spec(D_MODEL), _ex_spec(tpe), _ex_spec(tpe), _full_spec((1, D_MODEL)),
                  _full_spec((D_MODEL, NP_IN))],
        out_specs=[_row_spec(D_MODEL), _row_spec(768), _row_spec(D_CAT)] + [_row_spec(384)] * 6 + [_row_spec(LANE)],
        out_shape=shapes, compiler_params=_params(("parallel",)),
    )(x, shift, scale, g, w_in_p)


def _a_up(za, gq, gkv, w_uq_p, w_ukv_p, cos, sina, sinb):
    t = za.shape[0]

    def body(za_ref, gq_ref, gkv_ref, wq_ref, wkv_ref, cos_ref, sa_ref, sb_ref, q_ref, k_ref, v_ref):
        cos_t, sa, sb = cos_ref[...], sa_ref[...], sb_ref[...]
        cqn, _, _ = _rms(za_ref[:, 0:384], gq_ref[...])
        q = _dot(cqn.astype(BF16), wq_ref[...])
        ckvn, _, _ = _rms(za_ref[:, 384:640], gkv_ref[...])
        kv = _dot(ckvn.astype(BF16), wkv_ref[...])
        kpe = za_ref[:, 640:768]
        kpe = _rope(kpe + pltpu.roll(kpe, 32, 1), cos_t, sa, sb).astype(BF16)
        for p in range(3):
            q_ref[:, 256 * p:256 * p + 128] = q[:, 256 * p:256 * p + 128].astype(BF16)
            q_ref[:, 256 * p + 128:256 * p + 256] = _rope(q[:, 256 * p + 128:256 * p + 256], cos_t, sa, sb).astype(BF16)
            k_ref[:, 256 * p:256 * p + 128] = kv[:, 128 * p:128 * p + 128].astype(BF16)
            k_ref[:, 256 * p + 128:256 * p + 256] = kpe
        v_ref[...] = kv[:, 384:768].astype(BF16)

    return pl.pallas_call(
        body, name="a_up", grid=(t // TM,),
        in_specs=[_row_spec(768), _full_spec((1, 384)), _full_spec((1, 256)), _full_spec((384, 768)),
                  _full_spec((256, 768)), _row_spec(LANE), _row_spec(LANE), _row_spec(LANE)],
        out_specs=[_row_spec(768), _row_spec(768), _row_spec(384)],
        out_shape=[jax.ShapeDtypeStruct((t, 768), BF16), jax.ShapeDtypeStruct((t, 768), BF16),
                   jax.ShapeDtypeStruct((t, 384), BF16)],
        compiler_params=_params(("parallel",)),
    )(za, gq, gkv, w_uq_p, w_ukv_p, cos, sina, sinb)


def _tri(n, upper):
    r = lax.broadcasted_iota(jnp.int32, (n, n), 0)
    c = lax.broadcasted_iota(jnp.int32, (n, n), 1)
    return jnp.where((c >= r) if upper else (c <= r), 1.0, 0.0).astype(BF16)


def _forget_fwd(zf, fb, nb, seq):
    blk = 256

    def body(zf_ref, fb_ref, f_ref):
        tri = _tri(blk, False)
        live = _lane_iota((blk, LANE)) < 5
        carry = jnp.zeros((1, LANE), F32)
        for i in range(seq // blk):
            u = zf_ref[i * blk:(i + 1) * blk, :] + fb_ref[...]
            lf = jnp.where(live, jnp.minimum(u, 0.0) - jnp.log(1.0 + jnp.exp(-jnp.abs(u))), 0.0)
            hi, mid, lo = _split3(lf)
            f_ref[i * blk:(i + 1) * blk, :] = (_dot(tri, hi) + _dot(tri, mid) + _dot(tri, lo) + carry) * LOG2E
            carry = carry + jnp.sum(lf, axis=0, keepdims=True)

    return pl.pallas_call(
        body, name="forget_fwd", grid=(nb,),
        in_specs=[pl.BlockSpec((seq, LANE), lambda b: (b, 0)), pl.BlockSpec((1, LANE), lambda b: (0, 0))],
        out_specs=pl.BlockSpec((seq, LANE), lambda b: (b, 0)),
        out_shape=jax.ShapeDtypeStruct((nb * seq, LANE), F32), compiler_params=_params(("parallel",)),
    )(zf, fb)


def _forget_bwd(df, zf, fb, nb, seq):
    blk = 256

    def body(df_ref, zf_ref, fb_ref, dz_ref, gb_ref):
        @pl.when(pl.program_id(0) == 0)
        def _():
            gb_ref[...] = jnp.zeros_like(gb_ref)

        tri = _tri(blk, True)
        live = _lane_iota((blk, LANE)) < 5
        carry = jnp.zeros((1, LANE), F32)
        gsum = jnp.zeros((1, LANE), F32)
        for i in reversed(range(seq // blk)):
            d = df_ref[i * blk:(i + 1) * blk, :]
            hi, mid, lo = _split3(d)
            dlf = _dot(tri, hi) + _dot(tri, mid) + _dot(tri, lo) + carry
            carry = carry + jnp.sum(d, axis=0, keepdims=True)
            u = zf_ref[i * blk:(i + 1) * blk, :] + fb_ref[...]
            du = jnp.where(live, dlf * jax.nn.sigmoid(-u), 0.0)
            dz_ref[i * blk:(i + 1) * blk, :] = du.astype(BF16)
            gsum = gsum + jnp.sum(du, axis=0, keepdims=True)
        gb_ref[...] += jnp.broadcast_to(gsum, gb_ref.shape)

    return pl.pallas_call(
        body, name="forget_bwd", grid=(nb,),
        in_specs=[pl.BlockSpec((seq, LANE), lambda b: (b, 0)), pl.BlockSpec((seq, LANE), lambda b: (b, 0)),
                  pl.BlockSpec((1, LANE), lambda b: (0, 0))],
        out_specs=[pl.BlockSpec((seq, LANE), lambda b: (b, 0)), pl.BlockSpec((8, LANE), lambda b: (0, 0))],
        out_shape=[jax.ShapeDtypeStruct((nb * seq, LANE), BF16), jax.ShapeDtypeStruct((8, LANE), F32)],
        compiler_params=_params(("arbitrary",)),
    )(df, zf, fb)


def _gate_outproj(x, gate, oa, ob, oc, zg, w_out_p, seq):
    t = x.shape[0]

    def body(x_ref, gate_ref, oa_ref, ob_ref, oc_ref, zg_ref, w_ref, y_ref, xn_ref):
        y = jnp.zeros((TM, D_MODEL), F32)
        for i, o_ref in enumerate((oa_ref, ob_ref, oc_ref)):
            cat = (o_ref[...] * _silu(zg_ref[:, 384 * i:384 * (i + 1)])).astype(BF16)
            y = y + _dot(cat, w_ref[384 * i:384 * (i + 1), :])
        y_ref[...] = y
        xn_ref[...] = x_ref[...] + gate_ref[0] * y

    return pl.pallas_call(
        body, name="gate_outproj", grid=(t // TM,),
        in_specs=[_row_spec(D_MODEL), _ex_spec(seq // TM), _row_spec(384), _row_spec(384), _row_spec(384),
                  _row_spec(D_CAT), _full_spec((D_CAT, D_MODEL))],
        out_specs=[_row_spec(D_MODEL), _row_spec(D_MODEL)],
        out_shape=[jax.ShapeDtypeStruct((t, D_MODEL), F32)] * 2, compiler_params=_params(("parallel",)),
    )(x, gate, oa, ob, oc, zg, w_out_p)


def _final_loss(x, target, g):
    t = x.shape[0]

    def body(x_ref, t_ref, g_ref, dx_ref, loss_ref, gg_ref):
        @pl.when(pl.program_id(0) == 0)
        def _():
            loss_ref[...] = jnp.zeros_like(loss_ref)
            gg_ref[...] = jnp.zeros_like(gg_ref)

        gv = g_ref[...]
        out, xh, r = _rms(x_ref[...], gv)
        err = out - t_ref[...]
        loss_ref[...] += 0.5 * jnp.sum(jnp.mean(err * err, axis=-1, keepdims=True), axis=0, keepdims=True)
        dout = err / D_MODEL
        gg_ref[...] += jnp.broadcast_to(jnp.sum(dout * xh, axis=0, keepdims=True), gg_ref.shape)
        dx_ref[...] = _rms_bwd(dout, xh, r, gv)

    return pl.pallas_call(
        body, name="final_loss", grid=(t // TM,),
        in_specs=[_row_spec(D_MODEL), _row_spec(D_MODEL), _full_spec((1, D_MODEL))],
        out_specs=[_row_spec(D_MODEL), _full_spec((8, LANE)), _full_spec((8, D_MODEL))],
        out_shape=[jax.ShapeDtypeStruct((t, D_MODEL), F32), jax.ShapeDtypeStruct((8, LANE), F32),
                   jax.ShapeDtypeStruct((8, D_MODEL), F32)],
        compiler_params=_params(("arbitrary",)),
    )(x, target, g)


def _head_masks(kind, rows, dq, h):
    lq = _lane_iota((rows, dq))
    lv = _lane_iota((rows, LANE))
    mq = (lq >= 64 * h) & (lq < 64 * h + 64)
    if kind == "A":
        mq = mq | ((lq >= 128 + 32 * h) & (lq < 160 + 32 * h))
    return mq, (lv >= 64 * h) & (lv < 64 * h + 64)


def _tile_mask(kind, tile, m=None):
    row = lax.broadcasted_iota(jnp.int32, (tile, tile), 0)
    col = lax.broadcasted_iota(jnp.int32, (tile, tile), 1)
    if kind == "A":
        return (col >> 6) <= (row >> 6)
    if kind == "C":
        return col <= row
    first = (m == 0) & (row >= 64) & (col < 64)
    last = (m == BAND_TILES - 1) & (row < 64) & (col >= 64)
    return jnp.logical_not(first | last)


def _attn_scale(kind):
    return 96.0 ** -0.5 if kind == "A" else 0.125


def _attn_fwd(kind, q, k, v, aux, nb, seq):
    dq = q.shape[1] // 3
    tile = T_BAND if kind == "B" else T_CAUSAL
    nq = seq // tile
    scale = _attn_scale(kind)

    def body(*refs):
        if kind == "A":
            q_ref, k_ref, v_ref, o_ref, lse_ref = refs
            aux_ref = None
        else:
            q_ref, k_ref, v_ref, aux_ref, o_ref, lse_ref = refs
        qi = pl.program_id(2)
        q2 = q_ref[...]
        res = []
        for h in range(2):
            mq, _ = _head_masks(kind, tile, dq, h)
            qh = jnp.where(mq, q2, jnp.zeros_like(q2))

            def step(kj, carry, m=None, diag=False):
                mx, l, acc = carry
                ks = pl.ds(pl.multiple_of(kj * tile, tile), tile)
                kt = k_ref[ks, :]
                vt = v_ref[ks, :]
                s = _dot_nt(qh, kt) * scale
                if kind == "B":
                    s = jnp.where(_tile_mask("B", tile, m), s + aux_ref[h, m], NEG)
                if kind == "C":
                    s = s - aux_ref[0, h, pl.ds(kj, 1), :]
                if diag:
                    s = jnp.where(_tile_mask(kind, tile), s, NEG)
                mn = jnp.maximum(mx, jnp.max(s, axis=-1, keepdims=True))
                alpha = jnp.exp(mx - mn)
                p = jnp.exp(s - mn)
                l = alpha * l + jnp.sum(p, axis=-1, keepdims=True)
                acc = alpha * acc + _dot(p.astype(BF16), vt)
                return mn, l, acc

            init = (jnp.full((tile, 1), NEG, F32), jnp.zeros((tile, 1), F32), jnp.zeros((tile, LANE), F32))
            if kind == "B":
                m0 = jnp.maximum(BAND_TILES - 1 - qi, 0)
                mx, l, acc = lax.fori_loop(m0, BAND_TILES, lambda m, c: step(qi - (BAND_TILES - 1) + m, c, m=m), init)
            else:
                carry = lax.fori_loop(0, qi, lambda kj, c: step(kj, c), init)
                mx, l, acc = step(qi, carry, diag=True)
            res.append((acc / l, mx + jnp.log(l)))
        first = _lane_iota((tile, LANE)) < 64
        o_ref[...] = jnp.where(first, res[0][0], res[1][0])
        lse_ref[...] = jnp.where(first, res[0][1], res[1][1])

    in_specs = [pl.BlockSpec((tile, dq), lambda b, p, i: (b * nq + i, p)),
                pl.BlockSpec((seq, dq), lambda b, p, i: (b, p)),
                pl.BlockSpec((seq, LANE), lambda b, p, i: (b, p))]
    args = [q, k, v]
    if kind == "B":
        in_specs.append(pl.BlockSpec((2, BAND_TILES, tile, tile), lambda b, p, i: (p, 0, 0, 0)))
        args.append(aux)
    if kind == "C":
        in_specs.append(pl.BlockSpec((1, 2, nq, tile), lambda b, p, i: (b, p, 0, 0)))
        args.append(aux)
    out_spec = pl.BlockSpec((tile, LANE), lambda b, p, i: (b * nq + i, p))
    return pl.pallas_call(
        body, name="attn_fwd_" + kind, grid=(nb, 3, nq), in_specs=in_specs, out_specs=[out_spec, out_spec],
        out_shape=[jax.ShapeDtypeStruct((nb * seq, 384), F32)] * 2,
        compiler_params=_params(("parallel", "parallel", "parallel")),
    )(*args)


def _attn_bwd(kind, q, k, v, o, do, lse, aux, nb, seq):
    dq = q.shape[1] // 3
    tile = T_BAND if kind == "B" else T_CAUSAL
    nq = seq // tile
    scale = _attn_scale(kind)
    dqk_dtype = F32 if kind == "A" else BF16

    def body(*refs):
        dfr_ref = dfq_ref = dbt_ref = aux_ref = None
        if kind == "A":
            q_ref, k_ref, v_ref, o_ref, do_ref, lse_ref, dq_ref, dk_ref, dv_ref, dk_acc, dv_acc = refs
        elif kind == "B":
            q_ref, k_ref, v_ref, o_ref, do_ref, lse_ref, aux_ref, dq_ref, dk_ref, dv_ref, dbt_ref, dk_acc, dv_acc = refs
        else:
            (q_ref, k_ref, v_ref, o_ref, do_ref, lse_ref, aux_ref, dq_ref, dk_ref, dv_ref, dfr_ref, dfq_ref,
             dk_acc, dv_acc) = refs
        dk_acc[...] = jnp.zeros_like(dk_acc)
        dv_acc[...] = jnp.zeros_like(dv_acc)
        if kind == "C":
            dfr_ref[...] = jnp.zeros_like(dfr_ref)
        if kind == "B":
            @pl.when(pl.program_id(1) == 0)
            def _():
                dbt_ref[...] = jnp.zeros_like(dbt_ref)

        def q_step(qi, _):
            qs = pl.ds(pl.multiple_of(qi * tile, tile), tile)
            q2 = q_ref[qs, :]
            do2 = do_ref[qs, :]
            o2 = o_ref[qs, :]
            lse2 = lse_ref[qs, :]
            dq_tot = jnp.zeros((tile, dq), F32)
            row_sums = []
            for h in range(2):
                mq, mv = _head_masks(kind, tile, dq, h)
                qh = jnp.where(mq, q2, jnp.zeros_like(q2))
                doh = jnp.where(mv, do2, 0.0)
                dob = doh.astype(BF16)
                delta = jnp.sum(doh * o2, axis=-1, keepdims=True)
                lseh = jnp.max(jnp.where(mv, lse2, NEG), axis=-1, keepdims=True)

                def step(kj, carry, m=None, diag=False):
                    dq_t, rs = carry
                    ks = pl.ds(pl.multiple_of(kj * tile, tile), tile)
                    kt = k_ref[ks, :]
                    vt = v_ref[ks, :]
                    s = _dot_nt(qh, kt) * scale
                    if kind == "B":
                        s = jnp.where(_tile_mask("B", tile, m), s + aux_ref[h, m], NEG)
                    if kind == "C":
                        s = s - aux_ref[0, h, pl.ds(kj, 1), :]
                    if diag:
                        s = jnp.where(_tile_mask(kind, tile), s, NEG)
                    p = jnp.exp(s - lseh)
                    ds = p * (_dot_nt(dob, vt) - delta)
                    if kind == "B":
                        dbt_ref[h, m] += ds
                    if kind == "C":
                        dfr_ref[0, h, pl.ds(kj, 1), :] -= jnp.sum(ds, axis=0, keepdims=True)
                        rs = rs + jnp.sum(ds, axis=-1, keepdims=True)
                    dss = (ds * scale).astype(BF16)
                    dv_acc[ks, :] += _dot_tn(p.astype(BF16), dob)
                    dk_acc[ks, :] += _dot_tn(dss, qh)
                    return dq_t + jnp.where(mq, _dot(dss, kt), 0.0), rs

                zero = (jnp.zeros((tile, dq), F32), jnp.zeros((tile, 1), F32))
                if kind == "B":
                    m0 = jnp.maximum(BAND_TILES - 1 - qi, 0)
                    dq_h, rs_h = lax.fori_loop(m0, BAND_TILES, lambda m, c: step(qi - (BAND_TILES - 1) + m, c, m=m), zero)
                else:
                    carry = lax.fori_loop(0, qi, lambda kj, c: step(kj, c), zero)
                    dq_h, rs_h = step(qi, carry, diag=True)
                dq_tot = dq_tot + dq_h
                row_sums.append(rs_h)
            dq_ref[qs, :] = dq_tot.astype(dqk_dtype)
            if kind == "C":
                dfq_ref[qs, :] = jnp.where(_lane_iota((tile, LANE)) < 64, row_sums[0], row_sums[1])
            return 0

        lax.fori_loop(0, nq, q_step, 0)
        dk_ref[...] = dk_acc[...].astype(dqk_dtype)
        dv_ref[...] = dv_acc[...].astype(BF16)

    def seq_spec(cols):
        return pl.BlockSpec((seq, cols), lambda p, b: (b, p))

    in_specs = [seq_spec(dq), seq_spec(dq), seq_spec(LANE), seq_spec(LANE), seq_spec(LANE), seq_spec(LANE)]
    args = [q, k, v, o, do, lse]
    out_specs = [seq_spec(dq), seq_spec(dq), seq_spec(LANE)]
    out_shape = [jax.ShapeDtypeStruct((nb * seq, 3 * dq), dqk_dtype)] * 2 + [jax.ShapeDtypeStruct((nb * seq, 384), BF16)]
    if kind == "B":
        spec = pl.BlockSpec((2, BAND_TILES, tile, tile), lambda p, b: (p, 0, 0, 0))
        in_specs.append(spec)
        args.append(aux)
        out_specs.append(spec)
        out_shape.append(jax.ShapeDtypeStruct((6, BAND_TILES, tile, tile), F32))
    if kind == "C":
        spec = pl.BlockSpec((1, 2, nq, tile), lambda p, b: (b, p, 0, 0))
        in_specs.append(spec)
        args.append(aux)
        out_specs += [spec, seq_spec(LANE)]
        out_shape += [jax.ShapeDtypeStruct((nb, 6, nq, tile), F32), jax.ShapeDtypeStruct((nb * seq, 384), F32)]
    return pl.pallas_call(
        body, name="attn_bwd_" + kind, grid=(3, nb), in_specs=in_specs, out_specs=out_specs, out_shape=out_shape,
        scratch_shapes=[pltpu.VMEM((seq, dq), F32), pltpu.VMEM((seq, LANE), F32)],
        compiler_params=_params(("arbitrary", "arbitrary")),
    )(*args)


BAND_W = BAND_TILES * T_BAND


def _segments(kind, qi, tile):
    r0 = qi * tile
    if kind == "B":
        lo = max(qi - (BAND_TILES - 1), 0) * tile
        return [(lo, r0 + tile, False, lo - (qi - (BAND_TILES - 1)) * tile)]
    return ([(0, r0, False, 0)] if qi else []) + [(r0, r0 + tile, True, 0)]


def _scores(kind, qh, k_ref, aux_ref, h, seg, tile, scale):
    a, b, diag, c0 = seg
    s = _dot_nt(qh, k_ref[a:b, :]) * (scale * LOG2E)
    if kind == "B":
        row = lax.broadcasted_iota(jnp.int32, (tile, b - a), 0)
        col = lax.broadcasted_iota(jnp.int32, (tile, b - a), 1) + c0
        hidden = ((row >= 64) & (col < 64)) | ((row < 64) & (col >= BAND_W - 64))
        return jnp.where(hidden, NEG, s + aux_ref[h, :, c0:BAND_W])
    if kind == "C":
        s = s - aux_ref[0, h, :, a:b]
    if diag:
        s = jnp.where(_tile_mask(kind, tile), s, NEG)
    return s


def _second_head(kind, pair, fn):
    if kind == "A":
        fn()
    else:
        pl.when(pair < 2)(fn)


def _attn_fwd(kind, q, k, v, aux, nb, seq):
    dq = q.shape[1] // 3
    tile = T_BAND if kind == "B" else T_CAUSAL
    nq = seq // tile
    scale = _attn_scale(kind)

    def body(*refs):
        if kind == "A":
            q_ref, k_ref, v_ref, o_ref, lse_ref = refs
            aux_ref = None
        else:
            q_ref, k_ref, v_ref, aux_ref, o_ref, lse_ref = refs
        pair = pl.program_id(1)
        for qi in range(nq):
            rows = slice(qi * tile, (qi + 1) * tile)
            segs = _segments(kind, qi, tile)

            def head(h, rows=rows, segs=segs):
                q2 = q_ref[rows, :]
                mq, mv = _head_masks(kind, tile, dq, h)
                qh = jnp.where(mq, q2, jnp.zeros_like(q2))
                ss = [_scores(kind, qh, k_ref, aux_ref, h, seg, tile, scale) for seg in segs]
                mx = functools.reduce(jnp.maximum, [jnp.max(s, axis=-1, keepdims=True) for s in ss])
                ps = [jnp.exp2(s - mx) for s in ss]
                l = functools.reduce(jnp.add, [jnp.sum(p, axis=-1, keepdims=True) for p in ps])
                acc = functools.reduce(jnp.add, [_dot(p.astype(BF16), v_ref[seg[0]:seg[1], :]) for p, seg in zip(ps, segs)])
                o_h = jnp.where(mv, acc / l, 0.0)
                lse_h = jnp.where(mv, mx + jnp.log(l) * LOG2E, 0.0)
                if h == 0:
                    o_ref[rows, :] = o_h
                    lse_ref[rows, :] = lse_h
                else:
                    o_ref[rows, :] += o_h
                    lse_ref[rows, :] += lse_h

            head(0)
            _second_head(kind, pair, functools.partial(head, 1))

    def seq_spec(cols):
        return pl.BlockSpec((seq, cols), lambda b, p: (b, p))

    in_specs = [seq_spec(dq), seq_spec(dq), seq_spec(LANE)]
    args = [q, k, v]
    if kind == "B":
        in_specs.append(pl.BlockSpec((2, tile, BAND_W), lambda b, p: (p, 0, 0)))
        args.append(aux)
    if kind == "C":
        in_specs.append(pl.BlockSpec((1, 2, 1, seq), lambda b, p: (b, p, 0, 0)))
        args.append(aux)
    return pl.pallas_call(
        body, name="attn_fwd_" + kind, grid=(nb, 3), in_specs=in_specs, out_specs=[seq_spec(LANE), seq_spec(LANE)],
        out_shape=[jax.ShapeDtypeStruct((nb * seq, 384), F32)] * 2, compiler_params=_params(("parallel", "parallel")),
    )(*args)


def _attn_bwd(kind, q, k, v, o, do, lse, aux, nb, seq):
    dq = q.shape[1] // 3
    tile = T_BAND if kind == "B" else T_CAUSAL
    nq = seq // tile
    scale = _attn_scale(kind)
    dqk_dtype = F32 if kind == "A" else BF16

    def body(*refs):
        dfr_ref = dfq_ref = dbt_ref = aux_ref = None
        if kind == "A":
            q_ref, k_ref, v_ref, o_ref, do_ref, lse_ref, dq_ref, dk_ref, dv_ref, dkt_acc, dvt_acc = refs
        elif kind == "B":
            q_ref, k_ref, v_ref, o_ref, do_ref, lse_ref, aux_ref, dq_ref, dk_ref, dv_ref, dbt_ref, dkt_acc, dvt_acc = refs
        else:
            (q_ref, k_ref, v_ref, o_ref, do_ref, lse_ref, aux_ref, dq_ref, dk_ref, dv_ref, dfr_ref, dfq_ref,
             dkt_acc, dvt_acc) = refs
        dkt_acc[...] = jnp.zeros_like(dkt_acc)
        dvt_acc[...] = jnp.zeros_like(dvt_acc)
        if kind == "C":
            dfr_ref[...] = jnp.zeros_like(dfr_ref)
        if kind == "B":
            @pl.when(pl.program_id(1) == 0)
            def _():
                dbt_ref[...] = jnp.zeros_like(dbt_ref)

        pair = pl.program_id(0)
        for qi in range(nq):
            rows = slice(qi * tile, (qi + 1) * tile)
            segs = _segments(kind, qi, tile)

            def head(h, rows=rows, segs=segs):
                q2 = q_ref[rows, :]
                do2 = do_ref[rows, :]
                mq, mv = _head_masks(kind, tile, dq, h)
                qh = jnp.where(mq, q2, jnp.zeros_like(q2))
                doh = jnp.where(mv, do2, 0.0)
                dob = doh.astype(BF16)
                qht = qh.astype(F32).T.astype(BF16)
                dobt = doh.T.astype(BF16)
                delta = jnp.sum(doh * o_ref[rows, :], axis=-1, keepdims=True)
                lseh = jnp.max(jnp.where(mv, lse_ref[rows, :], NEG), axis=-1, keepdims=True)
                rs = jnp.zeros((tile, 1), F32)
                dq_h = jnp.zeros((tile, dq), F32)
                for seg in segs:
                    a, b, _, c0 = seg
                    p = jnp.exp2(_scores(kind, qh, k_ref, aux_ref, h, seg, tile, scale) - lseh)
                    ds = p * (_dot_nt(dob, v_ref[a:b, :]) - delta)
                    if kind == "B":
                        dbt_ref[h, :, c0:BAND_W] += ds
                    if kind == "C":
                        dfr_ref[0, h, :, a:b] -= jnp.sum(ds, axis=0, keepdims=True)
                        rs = rs + jnp.sum(ds, axis=-1, keepdims=True)
                    dss = (ds * scale).astype(BF16)
                    dvt_acc[:, a:b] += _dot(dobt, p.astype(BF16))
                    dkt_acc[:, a:b] += _dot(qht, dss)
                    dq_h = dq_h + _dot(dss, k_ref[a:b, :])
                dq_h = jnp.where(mq, dq_h, 0.0).astype(dqk_dtype)
                if h == 0:
                    dq_ref[rows, :] = dq_h
                else:
                    dq_ref[rows, :] += dq_h
                if kind == "C":
                    if h == 0:
                        dfq_ref[rows, :] = jnp.where(mv, rs, 0.0)
                    else:
                        dfq_ref[rows, :] += jnp.where(mv, rs, 0.0)

            head(0)
            _second_head(kind, pair, functools.partial(head, 1))
        for j in range(seq // 256):
            cols = slice(256 * j, 256 * (j + 1))
            dk_ref[cols, :] = dkt_acc[:, cols].T.astype(dqk_dtype)
            dv_ref[cols, :] = dvt_acc[:, cols].T.astype(BF16)

    def seq_spec(cols):
        return pl.BlockSpec((seq, cols), lambda p, b: (b, p))

    in_specs = [seq_spec(dq), seq_spec(dq), seq_spec(LANE), seq_spec(LANE), seq_spec(LANE), seq_spec(LANE)]
    args = [q, k, v, o, do, lse]
    out_specs = [seq_spec(dq), seq_spec(dq), seq_spec(LANE)]
    out_shape = [jax.ShapeDtypeStruct((nb * seq, 3 * dq), dqk_dtype)] * 2 + [jax.ShapeDtypeStruct((nb * seq, 384), BF16)]
    if kind == "B":
        spec = pl.BlockSpec((2, tile, BAND_W), lambda p, b: (p, 0, 0))
        in_specs.append(spec)
        args.append(aux)
        out_specs.append(spec)
        out_shape.append(jax.ShapeDtypeStruct((6, tile, BAND_W), F32))
    if kind == "C":
        spec = pl.BlockSpec((1, 2, 1, seq), lambda p, b: (b, p, 0, 0))
        in_specs.append(spec)
        args.append(aux)
        out_specs += [spec, seq_spec(LANE)]
        out_shape += [jax.ShapeDtypeStruct((nb, 6, 1, seq), F32), jax.ShapeDtypeStruct((nb * seq, 384), F32)]
    return pl.pallas_call(
        body, name="attn_bwd_" + kind, grid=(3, nb), in_specs=in_specs, out_specs=out_specs, out_shape=out_shape,
        scratch_shapes=[pltpu.VMEM((dq, seq), F32), pltpu.VMEM((LANE, seq), F32)],
        compiler_params=_params(("arbitrary", "arbitrary")),
    )(*args)


BIAS_FLAT = T_BAND * BAND_W
BIAS_CHUNK = 4 * BAND_W


def _rel_onehot(chunk):
    lane = lax.broadcasted_iota(jnp.int32, (384, chunk), 1)
    r = lax.broadcasted_iota(jnp.int32, (384, chunk), 0)
    sub = jnp.where(lane >= BAND_W, 1, 0) + jnp.where(lane >= 2 * BAND_W, 1, 0) + jnp.where(lane >= 3 * BAND_W, 1, 0)
    i = pl.program_id(0) * 4 + sub
    col = lane - sub * BAND_W
    idx = jnp.clip(BAND_W - T_BAND + i - col, -REL_CLIP, REL_CLIP) + REL_CLIP
    return jnp.where(idx == r, 1.0, 0.0).astype(BF16)


BIAS_G = 768
BIAS_EDGE = BIAS_G - N_REL


def _bias_line(rel_bias):
    g = jnp.concatenate([jnp.broadcast_to(rel_bias[:, N_REL - 1:], (rel_bias.shape[0], BIAS_EDGE)),
                         jnp.flip(rel_bias, axis=1)], axis=1)
    return jnp.pad(g, ((0, 8 - g.shape[0]), (0, 0)))


def _bias_unline(dg):
    return jnp.flip(dg[:, BIAS_EDGE:], axis=1)


def _bias_expand(g8):
    def body(g_ref, out_ref):
        line = jnp.broadcast_to(g_ref[0] * LOG2E, (T_BAND, BIAS_G))
        out_ref[0] = pltpu.roll(line, 1, 1, stride=1, stride_axis=0)[:, LANE:BIAS_G]

    return pl.pallas_call(
        body, name="bias_expand", grid=(8,), in_specs=[pl.BlockSpec((1, 1, BIAS_G), lambda h: (h, 0, 0))],
        out_specs=pl.BlockSpec((1, T_BAND, BAND_W), lambda h: (h, 0, 0)),
        out_shape=jax.ShapeDtypeStruct((8, T_BAND, BAND_W), F32), compiler_params=_params(("parallel",)),
    )(g8.reshape(8, 1, BIAS_G))


def _bias_reduce(d_slab):
    def body(d_ref, out_ref):
        r = lax.broadcasted_iota(jnp.int32, (T_BAND, T_BAND), 0)
        k = lax.broadcasted_iota(jnp.int32, (T_BAND, T_BAND), 1)
        flip = jnp.where(r + k == T_BAND - 1, 1.0, 0.0).astype(BF16)
        hi, mid, lo = _split3(d_ref[0])
        d_rev = _dot(flip, hi) + _dot(flip, mid) + _dot(flip, lo)
        wide = jnp.concatenate([jnp.zeros((T_BAND, LANE), F32), d_rev, jnp.zeros((T_BAND, 2 * LANE), F32)], axis=1)
        skew = pltpu.roll(wide, 0, 1, stride=1, stride_axis=0)
        dg = jnp.sum(skew, axis=0, keepdims=True)[:, LANE:LANE + BIAS_G]
        lane = _lane_iota((1, BIAS_G))
        clipped = jnp.sum(jnp.where(lane <= BIAS_EDGE, dg, 0.0), axis=1, keepdims=True)
        out_ref[0] = jnp.where(lane == BIAS_EDGE, clipped, dg)

    return pl.pallas_call(
        body, name="bias_reduce", grid=(8,), in_specs=[pl.BlockSpec((1, T_BAND, BAND_W), lambda h: (h, 0, 0))],
        out_specs=pl.BlockSpec((1, 1, BIAS_G), lambda h: (h, 0, 0)),
        out_shape=jax.ShapeDtypeStruct((8, 1, BIAS_G), F32), compiler_params=_params(("parallel",)),
    )(d_slab).reshape(8, BIAS_G)


def _outproj_bwd(dxn, y, gate, oa, ob, oc, zg, w_out_p, w_out_pt, nb, seq):
    t = dxn.shape[0]
    tpe = seq // TM

    def body(dxn_ref, y_ref, gate_ref, oa_ref, ob_ref, oc_ref, zg_ref, w_ref, wt_ref,
             doa_ref, dob_ref, doc_ref, dzg_ref, gw_ref, dgate_ref):
        i = pl.program_id(0)

        @pl.when(i == 0)
        def _():
            gw_ref[...] = jnp.zeros_like(gw_ref)

        @pl.when(i % tpe == 0)
        def _():
            dgate_ref[...] = jnp.zeros_like(dgate_ref)

        dxn_t = dxn_ref[...]
        dgate_ref[0] += jnp.sum(dxn_t * y_ref[...], axis=0, keepdims=True)
        dy = (dxn_t * gate_ref[0]).astype(BF16)
        for gi, (o_ref, do_ref) in enumerate(((oa_ref, doa_ref), (ob_ref, dob_ref), (oc_ref, doc_ref))):
            cols = slice(384 * gi, 384 * (gi + 1))
            u = zg_ref[:, cols]
            o_t = o_ref[...]
            su = _silu(u)
            dcat = _dot(dy, wt_ref[:, cols])
            do_ref[...] = dcat * su
            dzg_ref[:, cols] = (dcat * o_t * _dsilu(u)).astype(BF16)
            gw_ref[cols, :] += _dot_tn((o_t * su).astype(BF16), dy)

    return pl.pallas_call(
        body, name="outproj_bwd", grid=(t // TM,),
        in_specs=[_row_spec(D_MODEL), _row_spec(D_MODEL), _ex_spec(tpe), _row_spec(384), _row_spec(384), _row_spec(384),
                  _row_spec(D_CAT), _full_spec((D_CAT, D_MODEL)), _full_spec((D_MODEL, D_CAT))],
        out_specs=[_row_spec(384), _row_spec(384), _row_spec(384), _row_spec(D_CAT), _full_spec((D_CAT, D_MODEL)),
                   _ex_spec(tpe)],
        out_shape=[jax.ShapeDtypeStruct((t, 384), F32)] * 3 + [jax.ShapeDtypeStruct((t, D_CAT), BF16),
                                                                jax.ShapeDtypeStruct((D_CAT, D_MODEL), F32),
                                                                jax.ShapeDtypeStruct((nb, 1, D_MODEL), F32)],
        compiler_params=_params(("arbitrary",)),
    )(dxn, y, gate, oa, ob, oc, zg, w_out_p, w_out_pt)


def _a_up_bwd(dqa, dka, dva, za, gq, gkv, w_uq_pt, w_ukv_pt, cos, sina, sinb):
    t = za.shape[0]

    def body(dq_ref, dk_ref, dv_ref, za_ref, gq_ref, gkv_ref, wqt_ref, wkvt_ref, cos_ref, sa_ref, sb_ref,
             dza_ref, gwq_ref, gwkv_ref, ggq_ref, ggkv_ref, dqb, dkvb):
        @pl.when(pl.program_id(0) == 0)
        def _():
            gwq_ref[...] = jnp.zeros_like(gwq_ref)
            gwkv_ref[...] = jnp.zeros_like(gwkv_ref)
            ggq_ref[...] = jnp.zeros_like(ggq_ref)
            ggkv_ref[...] = jnp.zeros_like(ggkv_ref)

        cos_t, sa, sb = cos_ref[...], sa_ref[...], sb_ref[...]
        dkpe = jnp.zeros((TM, LANE), F32)
        for p in range(3):
            dqb[:, 256 * p:256 * p + 128] = dq_ref[:, 256 * p:256 * p + 128].astype(BF16)
            dqb[:, 256 * p + 128:256 * p + 256] = _rope_t(dq_ref[:, 256 * p + 128:256 * p + 256], cos_t, sa, sb).astype(BF16)
            dkvb[:, 128 * p:128 * p + 128] = dk_ref[:, 256 * p:256 * p + 128].astype(BF16)
            dkpe = dkpe + dk_ref[:, 256 * p + 128:256 * p + 256]
        dkvb[:, 384:768] = dv_ref[...]
        dkpe = _rope_t(dkpe, cos_t, sa, sb)
        dkpe = jnp.where(_lane_iota((TM, LANE)) < A_ROPE, dkpe + pltpu.roll(dkpe, LANE - 32, 1), 0.0)

        gqv = gq_ref[...]
        cqn, cqh, rq = _rms(za_ref[:, 0:384], gqv)
        dq_t = dqb[...]
        gwq_ref[...] += _dot_tn(cqn.astype(BF16), dq_t)
        dcqn = _dot(dq_t, wqt_ref[...])
        ggq_ref[...] += jnp.broadcast_to(jnp.sum(dcqn * cqh, axis=0, keepdims=True), ggq_ref.shape)
        dza_ref[:, 0:384] = _rms_bwd(dcqn, cqh, rq, gqv).astype(BF16)

        gkvv = gkv_ref[...]
        ckvn, ckvh, rkv = _rms(za_ref[:, 384:640], gkvv)
        dkv_t = dkvb[...]
        gwkv_ref[...] += _dot_tn(ckvn.astype(BF16), dkv_t)
        dckvn = _dot(dkv_t, wkvt_ref[...])
        ggkv_ref[...] += jnp.broadcast_to(jnp.sum(dckvn * ckvh, axis=0, keepdims=True), ggkv_ref.shape)
        dza_ref[:, 384:640] = _rms_bwd(dckvn, ckvh, rkv, gkvv).astype(BF16)
        dza_ref[:, 640:768] = dkpe.astype(BF16)

    return pl.pallas_call(
        body, name="a_up_bwd", grid=(t // TM,),
        in_specs=[_row_spec(768), _row_spec(768), _row_spec(384), _row_spec(768), _full_spec((1, 384)),
                  _full_spec((1, 256)), _full_spec((768, 384)), _full_spec((768, 256)), _row_spec(LANE), _row_spec(LANE),
                  _row_spec(LANE)],
        out_specs=[_row_spec(768), _full_spec((384, 768)), _full_spec((256, 768)), _full_spec((8, 384)),
                   _full_spec((8, 256))],
        out_shape=[jax.ShapeDtypeStruct((t, 768), BF16), jax.ShapeDtypeStruct((384, 768), F32),
                   jax.ShapeDtypeStruct((256, 768), F32), jax.ShapeDtypeStruct((8, 384), F32),
                   jax.ShapeDtypeStruct((8, 256), F32)],
        scratch_shapes=[pltpu.VMEM((TM, 768), BF16), pltpu.VMEM((TM, 768), BF16)],
        compiler_params=_params(("arbitrary",)),
    )(dqa, dka, dva, za, gq, gkv, w_uq_pt, w_ukv_pt, cos, sina, sinb)


def _dz_cols():
    return (Z_A, Z_G) + Z_QKV + (Z_F,)


def _inproj_bwd_dx(dz, dxn, x, shift, scale, g, w_in_pt, nb, seq):
    t = x.shape[0]
    tpe = seq // TM
    cols = _dz_cols()

    def body(*refs):
        dz_refs = refs[:len(cols)]
        dxn_ref, x_ref, sh_ref, sc_ref, g_ref, wt_ref, dx_ref, dsh_ref, dsc_ref, dg_ref = refs[len(cols):]
        i = pl.program_id(0)

        @pl.when(i == 0)
        def _():
            dg_ref[...] = jnp.zeros_like(dg_ref)

        @pl.when(i % tpe == 0)
        def _():
            dsh_ref[...] = jnp.zeros_like(dsh_ref)
            dsc_ref[...] = jnp.zeros_like(dsc_ref)

        dh = jnp.zeros((TM, D_MODEL), F32)
        for ref, (c0, c1) in zip(dz_refs, cols):
            dh = dh + _dot(ref[...], wt_ref[c0:c1, :])
        gv = g_ref[...]
        n, xh, r = _rms(x_ref[...], gv)
        dsh_ref[0] += jnp.sum(dh, axis=0, keepdims=True)
        dsc_ref[0] += jnp.sum(dh * n, axis=0, keepdims=True)
        dn = dh * (1.0 + sc_ref[0])
        dg_ref[...] += jnp.broadcast_to(jnp.sum(dn * xh, axis=0, keepdims=True), dg_ref.shape)
        dx_ref[...] = dxn_ref[...] + _rms_bwd(dn, xh, r, gv)

    in_specs = [_row_spec(c1 - c0) for c0, c1 in cols]
    in_specs += [_row_spec(D_MODEL), _row_spec(D_MODEL), _ex_spec(tpe), _ex_spec(tpe), _full_spec((1, D_MODEL)),
                 _full_spec((NP_IN, D_MODEL))]
    return pl.pallas_call(
        body, name="inproj_bwd_dx", grid=(t // TM,), in_specs=in_specs,
        out_specs=[_row_spec(D_MODEL), _ex_spec(tpe), _ex_spec(tpe), _full_spec((8, D_MODEL))],
        out_shape=[jax.ShapeDtypeStruct((t, D_MODEL), F32), jax.ShapeDtypeStruct((nb, 1, D_MODEL), F32),
                   jax.ShapeDtypeStruct((nb, 1, D_MODEL), F32), jax.ShapeDtypeStruct((8, D_MODEL), F32)],
        compiler_params=_params(("arbitrary",)),
    )(*dz, dxn, x, shift, scale, g, w_in_pt)


def _inproj_bwd_dw(h, dz, name):
    t = h.shape[0]
    widths = [d.shape[1] for d in dz]
    total = sum(widths)

    def body(*refs):
        h_ref = refs[0]
        dz_refs = refs[1:1 + len(dz)]
        gw_ref = refs[1 + len(dz)]

        @pl.when(pl.program_id(0) == 0)
        def _():
            gw_ref[...] = jnp.zeros_like(gw_ref)

        h_t = h_ref[...]
        c0 = 0
        for ref, w in zip(dz_refs, widths):
            gw_ref[:, c0:c0 + w] += _dot_tn(h_t, ref[...])
            c0 += w

    return pl.pallas_call(
        body, name=name, grid=(t // TM,), in_specs=[_row_spec(D_MODEL)] + [_row_spec(w) for w in widths],
        out_specs=_full_spec((D_MODEL, total)), out_shape=jax.ShapeDtypeStruct((D_MODEL, total), F32),
        compiler_params=_params(("arbitrary",)),
    )(h, *dz)


def _ada_fwd(c_all, w_ada, b_cols):
    n = c_all.shape[0]
    cols = w_ada.shape[2]

    def body(c_ref, w_ref, b_ref, out_ref):
        act = _silu(c_ref[...]).astype(BF16)
        out_ref[0] = _dot(act, w_ref[0].astype(BF16)) + b_ref[0]

    return pl.pallas_call(
        body, name="ada_fwd", grid=(DEPTH,),
        in_specs=[pl.BlockSpec((n, D_MODEL), lambda l: (0, 0)), pl.BlockSpec((1, D_MODEL, cols), lambda l: (l, 0, 0)),
                  pl.BlockSpec((1, 1, cols), lambda l: (l, 0, 0))],
        out_specs=pl.BlockSpec((1, n, cols), lambda l: (l, 0, 0)),
        out_shape=jax.ShapeDtypeStruct((DEPTH, n, cols), F32), compiler_params=_params(("parallel",)),
    )(c_all, w_ada, b_cols)


def _ada_bwd(c_all, dmod_cols, dmod_all):
    n = c_all.shape[0]
    cols = dmod_cols.shape[2]

    def body(c_ref, dc_ref, da_ref, gw_ref, gb_ref):
        act = _silu(c_ref[...]).astype(BF16)
        gw_ref[0] = _dot_tn(act, dc_ref[0].astype(BF16))
        gb_ref[0] = jnp.sum(da_ref[0], axis=0, keepdims=True)

    return pl.pallas_call(
        body, name="ada_bwd", grid=(DEPTH,),
        in_specs=[pl.BlockSpec((n, D_MODEL), lambda l: (0, 0)), pl.BlockSpec((1, n, cols), lambda l: (l, 0, 0)),
                  pl.BlockSpec((1, n, 3 * D_MODEL), lambda l: (l, 0, 0))],
        out_specs=[pl.BlockSpec((1, D_MODEL, cols), lambda l: (l, 0, 0)),
                   pl.BlockSpec((1, 1, 3 * D_MODEL), lambda l: (l, 0, 0))],
        out_shape=[jax.ShapeDtypeStruct((DEPTH, D_MODEL, cols), F32), jax.ShapeDtypeStruct((DEPTH, 1, 3 * D_MODEL), F32)],
        compiler_params=_params(("parallel",)),
    )(c_all, dmod_cols, dmod_all)


def _sum_blocks(parts, name):
    n, rows, cols = parts.shape
    tr = rows if rows <= 256 else 8 * next(d for d in range(32, 0, -1) if (rows // 8) % d == 0)

    def body(p_ref, out_ref):
        acc = p_ref[0].astype(F32)
        for k in range(1, n):
            acc = acc + p_ref[k].astype(F32)
        out_ref[...] = acc

    return pl.pallas_call(
        body, name=name, grid=(rows // tr,), in_specs=[pl.BlockSpec((n, tr, cols), lambda i: (0, i, 0))],
        out_specs=pl.BlockSpec((tr, cols), lambda i: (i, 0)), out_shape=jax.ShapeDtypeStruct((rows, cols), F32),
        compiler_params=_params(("parallel",)),
    )(parts)


def _adamw(w, g, m, v, name):
    rows, cols = w.shape
    tr = next(t for t in (rows, 256, 128, 64, 32, 16, 8) if rows % t == 0 and t * cols <= 256 * 1024)

    def body(w_ref, g_ref, m_ref, v_ref, d_ref, mo_ref, vo_ref):
        gv = g_ref[...]
        mn = ADAM_B1 * m_ref[...] + (1.0 - ADAM_B1) * gv
        vn = ADAM_B2 * v_ref[...] + (1.0 - ADAM_B2) * jnp.square(gv)
        m_hat = mn / (1.0 - ADAM_B1 ** ADAM_STEP)
        v_hat = vn / (1.0 - ADAM_B2 ** ADAM_STEP)
        d_ref[...] = -ADAM_LR * (m_hat / (jnp.sqrt(v_hat) + ADAM_EPS) + ADAM_WD * w_ref[...])
        mo_ref[...] = mn
        vo_ref[...] = vn

    spec = pl.BlockSpec((tr, cols), lambda i: (i, 0))
    return pl.pallas_call(
        body, name=name, grid=(rows // tr,), in_specs=[spec] * 4, out_specs=[spec] * 3,
        out_shape=[jax.ShapeDtypeStruct((rows, cols), F32)] * 3, compiler_params=_params(("parallel",)),
    )(w, g, m, v)


ALL_FLIPS = tuple(range(1, N_DEV))


def _exchange(src, flips, mode, name):
    _, rows, cols = src.shape
    nslot = 2 if mode == "pair" else N_DEV
    nf = len(flips)

    def body(src_ref, dst_ref, send_sems, recv_sems, local_sem):
        x, y, c = lax.axis_index("x"), lax.axis_index("y"), lax.axis_index("c")
        me = 4 * x + 2 * y + c

        def slot(j):
            return (j & 1) if mode == "pair" else j

        own = pltpu.make_async_copy(src_ref.at[me if mode == "scatter" else 0], dst_ref.at[slot(me)], local_sem)
        own.start()
        copies = []
        for i, f in enumerate(flips):
            peer = me ^ f
            to = (1 - x if f & 4 else x, 1 - y if f & 2 else y, 1 - c if f & 1 else c)
            cp = pltpu.make_async_remote_copy(
                src_ref=src_ref.at[peer if mode == "scatter" else 0], dst_ref=dst_ref.at[slot(me)],
                send_sem=send_sems.at[i], recv_sem=recv_sems.at[i], device_id=to, device_id_type=pl.DeviceIdType.MESH)
            cp.start()
            copies.append(cp)
        for i, f in enumerate(flips):
            peer = me ^ f
            to = (1 - x if f & 4 else x, 1 - y if f & 2 else y, 1 - c if f & 1 else c)
            pltpu.make_async_remote_copy(
                src_ref=src_ref.at[0], dst_ref=dst_ref.at[slot(peer)], send_sem=send_sems.at[i],
                recv_sem=recv_sems.at[i], device_id=to, device_id_type=pl.DeviceIdType.MESH).wait_recv()
        for cp in copies:
            cp.wait_send()
        own.wait()

    return pl.pallas_call(
        body, name=name, out_shape=jax.ShapeDtypeStruct((nslot, rows, cols), src.dtype),
        in_specs=[pl.BlockSpec(memory_space=pl.ANY)], out_specs=pl.BlockSpec(memory_space=pl.ANY),
        scratch_shapes=[pltpu.SemaphoreType.DMA((nf,)), pltpu.SemaphoreType.DMA((nf,)), pltpu.SemaphoreType.DMA],
    )(src)


def _transfer(name, srcs, dst_shapes, plan, in_place=False):
    n_arr = len(srcs)
    probe = plan(0, 0, 0)
    n_steps = len(probe)

    def body(*refs):
        src_refs, dst_refs = refs[:n_arr], refs[n_arr:2 * n_arr]
        send_sems, recv_sems, local_sems = refs[2 * n_arr:]
        x, y, c = lax.axis_index("x"), lax.axis_index("y"), lax.axis_index("c")
        steps = plan(x, y, c)

        def rows(ref, r0, n):
            return ref.at[:, pl.ds(r0, n), :]

        def arrival(t):
            a, _, _, n, _, f, _ = steps[t]
            return pltpu.make_async_remote_copy(
                src_ref=rows(dst_refs[a], 0, n), dst_ref=rows(dst_refs[a], 0, n), send_sem=send_sems.at[t],
                recv_sem=recv_sems.at[t], device_id=(x, y, c), device_id_type=pl.DeviceIdType.MESH)

        arrived, started = set(), []
        for t, (a, from_dst, sr, n, dr, f, after) in enumerate(steps):
            for u in after:
                if u not in arrived:
                    arrival(u).wait_recv()
                    arrived.add(u)
            src = rows(dst_refs[a] if from_dst else src_refs[a], sr, n)
            dst = rows(dst_refs[a], dr, n)
            if f == 0:
                cp = pltpu.make_async_copy(src, dst, local_sems.at[t])
            else:
                to = (1 - x if f & 4 else x, 1 - y if f & 2 else y, 1 - c if f & 1 else c)
                cp = pltpu.make_async_remote_copy(src_ref=src, dst_ref=dst, send_sem=send_sems.at[t],
                                                  recv_sem=recv_sems.at[t], device_id=to,
                                                  device_id_type=pl.DeviceIdType.MESH)
            cp.start()
            started.append(cp)
        for t, step in enumerate(steps):
            if step[5] != 0 and t not in arrived:
                arrival(t).wait_recv()
        for cp, step in zip(started, steps):
            if step[5] == 0:
                cp.wait()
            else:
                cp.wait_send()

    any_spec = pl.BlockSpec(memory_space=pl.ANY)
    return pl.pallas_call(
        body, name=name, out_shape=[jax.ShapeDtypeStruct(s, d) for s, d in dst_shapes],
        in_specs=[any_spec] * n_arr, out_specs=[any_spec] * n_arr,
        input_output_aliases={a: a for a in range(n_arr)} if in_place else {},
        scratch_shapes=[pltpu.SemaphoreType.DMA((n_steps,)), pltpu.SemaphoreType.DMA((n_steps,)),
                        pltpu.SemaphoreType.DMA((n_steps,))],
    )(*srcs)


CHIP_FLIPS = (2, 4, 6)


def _gather_plan(chip_rows):
    def plan(x, y, c):
        steps = []
        for a, rc in enumerate(chip_rows):
            h = rc // 2
            mine = rc * (2 * x + y) + h * c
            steps.append((a, False, h * c, h, mine, 0, ()))
            ici = {}
            for f in CHIP_FLIPS:
                ici[f] = len(steps)
                steps.append((a, False, h * c, h, mine, f, ()))
            steps.append((a, False, h * c, h, mine, 1, ()))
            for f in CHIP_FLIPS:
                theirs = rc * ((2 * x + y) ^ (f >> 1)) + h * c
                steps.append((a, True, theirs, h, theirs, 1, (ici[f],)))
        return steps
    return plan


def _pair_reduce_plan(chip_rows):
    def plan(x, y, c):
        steps = []
        for a, rc in enumerate(chip_rows):
            h = rc // 2
            for j in range(4):
                steps.append((a, False, rc * j + h * (1 - c), h, h * j, 1, ()))
        return steps
    return plan


def _chip_scatter_plan(chip_rows):
    def plan(x, y, c):
        steps = []
        for a, rc in enumerate(chip_rows):
            h = rc // 2
            for k, f in enumerate(CHIP_FLIPS):
                steps.append((a, False, h * ((2 * x + y) ^ (f >> 1)), h, h * k, f, ()))
        return steps
    return plan


def _pair_share_plan(chip_rows):
    def plan(x, y, c):
        return [(a, True, (rc // 2) * c, rc // 2, (rc // 2) * c, 1, ()) for a, rc in enumerate(chip_rows)]
    return plan


def _tile_rows(h):
    return next(t for t in (64, 32, 16) if h % t == 0)


def _sum_pair(partial, recv, core, rc, name):
    nl, _, cols = partial.shape
    h = rc // 2
    tr = _tile_rows(h)

    def body(c_ref, p_ref, r_ref, out_ref):
        out_ref[...] = (p_ref[...] + r_ref[...]).astype(BF16)

    spec = pl.BlockSpec((1, tr, cols), lambda l, j, i, c_ref: (l, (h // tr) * j + i, 0))
    return pl.pallas_call(
        body, name=name, out_shape=jax.ShapeDtypeStruct((nl, 4 * h, cols), BF16),
        grid_spec=pltpu.PrefetchScalarGridSpec(
            num_scalar_prefetch=1, grid=(nl, 4, h // tr),
            in_specs=[pl.BlockSpec((1, tr, cols), lambda l, j, i, c_ref: (l, (rc // tr) * j + (h // tr) * c_ref[0] + i, 0)),
                      spec],
            out_specs=spec),
        compiler_params=_params(("parallel", "parallel", "parallel")),
    )(core, partial, recv)


def _sum_chips(chip_sum, recv, place, rc, name):
    nl, _, cols = chip_sum.shape
    h = rc // 2
    tr = _tile_rows(h)

    def body(s_ref, own_ref, r_ref, out_ref):
        acc = own_ref[0].astype(F32)
        for k in range(3):
            acc = acc + r_ref[0, k].astype(F32)
        out_ref[0] = acc

    return pl.pallas_call(
        body, name=name, out_shape=jax.ShapeDtypeStruct((nl, rc, cols), F32),
        grid_spec=pltpu.PrefetchScalarGridSpec(
            num_scalar_prefetch=1, grid=(nl, h // tr),
            in_specs=[pl.BlockSpec((1, tr, cols), lambda l, i, s_ref: (l, (h // tr) * s_ref[0] + i, 0)),
                      pl.BlockSpec((1, 3, tr, cols), lambda l, i, s_ref: (l, 0, i, 0))],
            out_specs=pl.BlockSpec((1, tr, cols), lambda l, i, s_ref: (l, (h // tr) * s_ref[1] + i, 0))),
        compiler_params=_params(("parallel", "parallel")),
    )(place, chip_sum, recv.reshape(nl, 3, h, cols))


def _sum_slots(parts, out_dtype, name):
    nl, n, rows, cols = parts.shape
    tr = next(t for t in (128, 64, 32, 16) if rows % t == 0 and n * t * cols * 4 <= (4 << 20))

    def body(p_ref, out_ref):
        acc = p_ref[0, 0].astype(F32)
        for k in range(1, n):
            acc = acc + p_ref[0, k].astype(F32)
        out_ref[0] = acc.astype(out_dtype)

    return pl.pallas_call(
        body, name=name, grid=(nl, rows // tr),
        in_specs=[pl.BlockSpec((1, n, tr, cols), lambda l, i: (l, 0, i, 0))],
        out_specs=pl.BlockSpec((1, tr, cols), lambda l, i: (l, i, 0)),
        out_shape=jax.ShapeDtypeStruct((nl, rows, cols), out_dtype), compiler_params=_params(("parallel", "parallel")),
    )(parts)


def _pad_cols(a, n):
    return a if n == 0 else jnp.pad(a, ((0, 0), (0, n)))


def _in_to_padded(w):
    return jnp.concatenate([_pad_cols(w[:, a:b], z) for a, b, z in IN_PIECES], axis=1)


def _in_from_padded(gp):
    pos, out = 0, {}
    for a, b, z in IN_PIECES:
        out[a] = gp[:, pos:pos + (b - a)]
        pos += (b - a) + z
    return jnp.concatenate([out[a] for a in sorted(out)], axis=1)


def _out_to_padded(w):
    z = jnp.zeros((64, w.shape[1]), w.dtype)
    return jnp.concatenate([w[0:384], w[384:704], z, w[704:1024], z], axis=0)


def _out_from_padded(gp):
    return jnp.concatenate([gp[0:384], gp[384:704], gp[768:1088]], axis=0)


def _uq_to_padded(w):
    parts = []
    for p in range(3):
        h0, h1 = 2 * p, 2 * p + 1
        parts += [w[:, 96 * h0:96 * h0 + 64], w[:, 96 * h1:96 * h1 + 64], w[:, 96 * h0 + 64:96 * h0 + 96],
                  w[:, 96 * h1 + 64:96 * h1 + 96], jnp.zeros((w.shape[0], 64), w.dtype)]
    return jnp.concatenate(parts, axis=1)


def _uq_from_padded(gp):
    parts = []
    for h in range(6):
        p, s = h // 2, h % 2
        parts += [gp[:, 256 * p + 64 * s:256 * p + 64 * s + 64], gp[:, 256 * p + 128 + 32 * s:256 * p + 160 + 32 * s]]
    return jnp.concatenate(parts, axis=1)


def _ukv_to_padded(w):
    return jnp.concatenate([w[:, 128 * h:128 * h + 64] for h in range(6)]
                           + [w[:, 128 * h + 64:128 * h + 128] for h in range(6)], axis=1)


def _ukv_from_padded(gp):
    parts = []
    for h in range(6):
        parts += [gp[:, 64 * h:64 * h + 64], gp[:, 384 + 64 * h:384 + 64 * h + 64]]
    return jnp.concatenate(parts, axis=1)


LR_ROWS = 224
SHARD_ROWS = (256, 256, LR_ROWS)


def _pack_lowrank(w_uq, w_ukv):
    flat = jnp.concatenate([w_uq.reshape(-1), w_ukv.reshape(-1)])
    return jnp.pad(flat, (0, LR_ROWS * PACK_COLS - flat.shape[0])).reshape(1, LR_ROWS, PACK_COLS)


def _unpack_lowrank(packed):
    flat = packed.reshape(-1)
    n_uq = DEPTH * A_Q_RANK * 144
    n_ukv = DEPTH * A_KV_RANK * 192
    return flat[0:n_uq].reshape(DEPTH, A_Q_RANK, 144), flat[n_uq:n_uq + n_ukv].reshape(DEPTH, A_KV_RANK, 192)


PACK_SIZES = (DEPTH * 256 * N_IN, DEPTH * 256 * D_MODEL, DEPTH * A_Q_RANK * 144, DEPTH * A_KV_RANK * 192)


def _pack(w_in, w_out, w_uq, w_ukv):
    flat = jnp.concatenate([w_in.reshape(-1), w_out.reshape(-1), w_uq.reshape(-1), w_ukv.reshape(-1)])
    flat = jnp.pad(flat, (0, PACK_ROWS * PACK_COLS - flat.shape[0]))
    return flat.reshape(PACK_ROWS, PACK_COLS)


def _unpack(packed):
    flat = packed.reshape(-1)
    o0, o1, o2, o3 = PACK_SIZES
    w_in = flat[0:o0].reshape(DEPTH, 256, N_IN)
    w_out = flat[o0:o0 + o1].reshape(DEPTH, 256, D_MODEL)
    w_uq = flat[o0 + o1:o0 + o1 + o2].reshape(DEPTH, A_Q_RANK, 144)
    w_ukv = flat[o0 + o1 + o2:o0 + o1 + o2 + o3].reshape(DEPTH, A_KV_RANK, 192)
    return w_in, w_out, w_uq, w_ukv


def _rope_tables(positions):
    inv = ROPE_THETA ** (-jnp.arange(0, A_ROPE, 2, dtype=F32) / A_ROPE)
    ang = positions.astype(F32)[..., None] * inv
    cos = jnp.tile(jnp.cos(ang), (1, 1, 4)).reshape(-1, 64)
    sin = jnp.tile(jnp.sin(ang), (1, 1, 4)).reshape(-1, 64)
    second = (jnp.arange(64) % 32) >= 16
    pad = ((0, 0), (0, 64))
    return (jnp.pad(cos, pad), jnp.pad(jnp.where(second, 0.0, -sin), pad), jnp.pad(jnp.where(second, sin, 0.0), pad))


def _rows(a, n):
    flat = a.reshape(-1)
    return jnp.pad(flat, (0, n * LANE - flat.shape[0])).reshape(n, LANE)


def _forward_backward(x, mod, tables, target, weights, small, nb, seq):
    cos, sina, sinb = tables
    saved = []
    for l in range(DEPTH):
        w, s = weights[l], small[l]
        shift = mod[l][:, None, 0:D_MODEL]
        scale = mod[l][:, None, D_MODEL:2 * D_MODEL]
        gate = mod[l][:, None, 2 * D_MODEL:]
        h, za, zg, qb, kb, vb, qc, kc, vc, zf = _ln_inproj(x, shift, scale, s["norm_g"], w["in"], seq)
        qa, ka, va = _a_up(za, s["gq"], s["gkv"], w["uq"], w["ukv"], cos, sina, sinb)
        bias = _bias_expand(s["g8"])[0:6]
        f = _forget_fwd(zf, s["fb"], nb, seq)
        frow = jnp.pad(f[:, 0:5].reshape(nb, seq, 5).transpose(0, 2, 1), ((0, 0), (0, 1), (0, 0)))
        frow = frow.reshape(nb, 6, 1, seq)
        oa, lse_a = _attn_fwd("A", qa, ka, va, None, nb, seq)
        ob, lse_b = _attn_fwd("B", qb, kb, vb, bias, nb, seq)
        oc, lse_c = _attn_fwd("C", qc, kc, vc, frow, nb, seq)
        y, xn = _gate_outproj(x, gate, oa, ob, oc, zg, w["out"], seq)
        saved.append(dict(x=x, h=h, za=za, zg=zg, zf=zf, y=y, shift=shift, scale=scale, gate=gate, bias=bias, frow=frow,
                          a=(qa, ka, va, oa, lse_a), b=(qb, kb, vb, ob, lse_b), c=(qc, kc, vc, oc, lse_c)))
        x = xn
    dx, loss8, gfinal8 = _final_loss(x, target, small[0]["final_g"])
    grads = []
    for l in reversed(range(DEPTH)):
        w, s, sv = weights[l], small[l], saved[l]
        qa, ka, va, oa, lse_a = sv["a"]
        qb, kb, vb, ob, lse_b = sv["b"]
        qc, kc, vc, oc, lse_c = sv["c"]
        doa, dob, doc, dzg, gw_out, dgate = _outproj_bwd(dx, sv["y"], sv["gate"], oa, ob, oc, sv["zg"], w["out"],
                                                          w["out_t"], nb, seq)
        dqa, dka, dva = _attn_bwd("A", qa, ka, va, oa, doa, lse_a, None, nb, seq)
        dqb, dkb, dvb, dbt = _attn_bwd("B", qb, kb, vb, ob, dob, lse_b, sv["bias"], nb, seq)
        dqc, dkc, dvc, dfr, dfq = _attn_bwd("C", qc, kc, vc, oc, doc, lse_c, sv["frow"], nb, seq)
        dg = _bias_reduce(jnp.pad(dbt, ((0, 2), (0, 0), (0, 0))))
        grb = jnp.pad(_bias_unline(dg), ((0, 0), (0, 384 - N_REL)))
        df = dfr.reshape(nb, 6, seq).transpose(0, 2, 1).reshape(nb * seq, 6) + dfq.reshape(nb * seq, 6, 64)[:, :, 0]
        dzf, gfb = _forget_bwd(jnp.pad(df, ((0, 0), (0, LANE - 6))), sv["zf"], s["fb"], nb, seq)
        dza, gw_uq, gw_ukv, ggq, ggkv = _a_up_bwd(dqa, dka, dva, sv["za"], s["gq"], s["gkv"], w["uq_t"], w["ukv_t"],
                                                  cos, sina, sinb)
        dz = (dza, dzg, dqb, dkb, dvb, dqc, dkc, dvc, dzf)
        dx, dshift, dscale, gnorm = _inproj_bwd_dx(dz, dx, sv["x"], sv["shift"], sv["scale"], s["norm_g"], w["in_t"],
                                                   nb, seq)
        gw_in = jnp.concatenate([_inproj_bwd_dw(sv["h"], dz[0:2], "inproj_bwd_dw0"),
                                 _inproj_bwd_dw(sv["h"], dz[2:], "inproj_bwd_dw1")], axis=1)
        dmod = jnp.concatenate([dshift[:, 0], dscale[:, 0], dgate[:, 0]], axis=1)
        grads.append(dict(w_in=gw_in, w_out=gw_out, w_uq=gw_uq, w_ukv=gw_ukv, dmod=dmod, norm_g=gnorm[0], gq=ggq[0],
                          gkv=ggkv[0], rb8=grb, fb=gfb[0]))
    grads.reverse()
    return loss8[0, 0], dx, grads, gfinal8[0]


def _layer_weights(w_in, w_out, w_uq, w_ukv):
    wi, wo, wq, wkv = _in_to_padded(w_in), _out_to_padded(w_out), _uq_to_padded(w_uq), _ukv_to_padded(w_ukv)
    return {"in": wi, "in_t": wi.T, "out": wo, "out_t": wo.T, "uq": wq, "uq_t": wq.T, "ukv": wkv, "ukv_t": wkv.T}


def _layer_small(norm_g, gq, gkv, rel_bias, forget_b, final_g):
    fb = jnp.pad(forget_b, (0, LANE - 5)).reshape(1, LANE)
    return dict(norm_g=norm_g.reshape(1, -1), gq=gq.reshape(1, -1), gkv=gkv.reshape(1, -1), g8=_bias_line(rel_bias), fb=fb,
                final_g=final_g.reshape(1, -1))


def _small_payload(per_layer, final_g, loss):
    def stack(key):
        return jnp.stack([p[key] for p in per_layer])

    def rows(a, rng):
        return _rows(a, rng[1] - rng[0])

    dmod = stack("dmod") if "dmod" in per_layer[0] else jnp.zeros((LANE,), F32)
    parts = [rows(dmod, PAY_DMOD), rows(stack("norm_g"), PAY_NORM), rows(stack("gq"), PAY_GQ),
             rows(stack("gkv"), PAY_GKV), rows(stack("rb8"), PAY_RB), rows(stack("fb"), PAY_FB),
             rows(final_g, PAY_FINAL), rows(loss, PAY_LOSS)]
    return jnp.concatenate(parts, axis=0)


def _payload_split(pay):
    def take(rng, shape):
        n = 1
        for d in shape:
            n *= d
        return pay[rng[0]:rng[1]].reshape(-1)[0:n].reshape(shape)

    norm_g = take(PAY_NORM, (DEPTH, D_MODEL))
    gq = take(PAY_GQ, (DEPTH, A_Q_RANK))
    gkv = take(PAY_GKV, (DEPTH, A_KV_RANK))
    rb = take(PAY_RB, (DEPTH, 8, 384))[:, 0:5, 0:N_REL]
    fb = take(PAY_FB, (DEPTH, LANE))[:, 0:5]
    final_g = take(PAY_FINAL, (D_MODEL,))
    return norm_g, gq, gkv, rb, fb, final_g


def kernel(x, c, positions, w_ada, b_ada, norm_g, w_in, a_q_norm_g, a_w_uq, a_kv_norm_g, a_w_ukv, b_rel_bias, c_forget_b, w_out, final_g, loss_target, m_w_ada, m_b_ada, m_norm_g, m_w_in, m_a_q_norm_g, m_a_w_uq, m_a_kv_norm_g, m_a_w_ukv, m_b_rel_bias, m_c_forget_b, m_w_out, m_final_g, v_w_ada, v_b_ada, v_norm_g, v_w_in, v_a_q_norm_g, v_a_w_uq, v_a_kv_norm_g, v_a_w_ukv, v_b_rel_bias, v_c_forget_b, v_w_out, v_final_g):
    nb, seq, _ = x.shape
    ix, iy, ic = lax.axis_index("x"), lax.axis_index("y"), lax.axis_index("c")
    chip = 2 * ix + iy
    me = 2 * chip + ic

    full_in, full_out, full_lr = _transfer(
        "gather_weights", [w_in.astype(BF16), w_out.astype(BF16), _pack_lowrank(a_w_uq, a_w_ukv).astype(BF16)],
        [((DEPTH, D_MODEL, N_IN), BF16), ((DEPTH, D_MODEL, D_MODEL), BF16), ((1, 4 * LR_ROWS, PACK_COLS), BF16)],
        _gather_plan(SHARD_ROWS))
    lowrank = [_unpack_lowrank(full_lr[0, LR_ROWS * j:LR_ROWS * (j + 1)]) for j in range(4)]
    full_uq = jnp.concatenate([s[0] for s in lowrank], axis=2)
    full_ukv = jnp.concatenate([s[1] for s in lowrank], axis=2)
    weights = [_layer_weights(full_in[l], full_out[l], full_uq[l], full_ukv[l]) for l in range(DEPTH)]
    small = [_layer_small(norm_g[l], a_q_norm_g[l], a_kv_norm_g[l], b_rel_bias[l], c_forget_b[l], final_g)
             for l in range(DEPTH)]

    c_all = _exchange(c[None], ALL_FLIPS, "gather", "gather_c").reshape(N_DEV * nb, D_MODEL)
    cols = w_ada.shape[2]
    b_cols = lax.dynamic_slice_in_dim(b_ada, chip * cols, cols, axis=1)[:, None, :]
    mod_cols = _ada_fwd(c_all, w_ada, b_cols)
    mod_g = _exchange(mod_cols.reshape(1, DEPTH * N_DEV * nb, cols), ALL_FLIPS, "gather", "gather_mod")
    mod_all = jnp.concatenate([mod_g[2 * j].reshape(DEPTH, N_DEV * nb, cols) for j in range(4)], axis=2)
    mod = lax.dynamic_slice_in_dim(mod_all, me * nb, nb, axis=1)

    tables = _rope_tables(positions)
    loss_part, dx, grads, gfinal = _forward_backward(
        x.reshape(nb * seq, D_MODEL), mod, tables, loss_target.reshape(nb * seq, D_MODEL), weights, small, nb, seq)

    pay = _small_payload(grads, gfinal, loss_part)
    pay_all = _exchange(pay[None], ALL_FLIPS, "gather", "gather_small")
    tot = _sum_blocks(pay_all, "sum_small")
    loss = tot[PAY_LOSS[0], 0]
    dmod_all = pay_all[:, PAY_DMOD[0]:PAY_DMOD[1]].reshape(N_DEV, -1)[:, 0:DEPTH * nb * 3 * D_MODEL]
    dmod_all = dmod_all.reshape(N_DEV, DEPTH, nb, 3 * D_MODEL).transpose(1, 0, 2, 3)
    dmod_all = dmod_all.reshape(DEPTH, N_DEV * nb, 3 * D_MODEL)
    my_cols = lax.dynamic_slice_in_dim(dmod_all, chip * cols, cols, axis=2)
    g_w_ada, g_b_ada = _ada_bwd(c_all, my_cols, dmod_all)
    g_b_ada = g_b_ada[:, 0]

    g_uq = jnp.stack([_uq_from_padded(g["w_uq"]) for g in grads])
    g_ukv = jnp.stack([_ukv_from_padded(g["w_ukv"]) for g in grads])
    g_lr = jnp.concatenate([_pack_lowrank(g_uq[:, :, 144 * j:144 * (j + 1)], g_ukv[:, :, 192 * j:192 * (j + 1)])
                            for j in range(4)], axis=1)
    partials = [jnp.stack([g["w_in"] for g in grads]), jnp.stack([_out_from_padded(g["w_out"]) for g in grads]), g_lr]
    shapes = [(p.shape[0], p.shape[2]) for p in partials]
    halves = [r // 2 for r in SHARD_ROWS]
    core_s = jnp.reshape(ic, (1,)).astype(jnp.int32)
    place_s = jnp.stack([chip, ic]).astype(jnp.int32)
    from_pair = _transfer("pair_reduce", partials, [((nl, 4 * h, nc), F32) for (nl, nc), h in zip(shapes, halves)],
                          _pair_reduce_plan(SHARD_ROWS))
    chip_sums = [_sum_pair(p, r, core_s, rc, "sum_pair%d" % i)
                 for i, (p, r, rc) in enumerate(zip(partials, from_pair, SHARD_ROWS))]
    from_chips = _transfer("chip_scatter", chip_sums, [((nl, 3 * h, nc), BF16) for (nl, nc), h in zip(shapes, halves)],
                           _chip_scatter_plan(SHARD_ROWS))
    reduced = [_sum_chips(s, r, place_s, rc, "sum_chips%d" % i)
               for i, (s, r, rc) in enumerate(zip(chip_sums, from_chips, SHARD_ROWS))]
    g_in_p, g_out_sh, g_lr_sh = _transfer("pair_share", reduced, [(r.shape, F32) for r in reduced],
                                          _pair_share_plan(SHARD_ROWS), in_place=True)
    g_uq_sh, g_ukv_sh = _unpack_lowrank(g_lr_sh[0])
    gw = (jnp.stack([_in_from_padded(g_in_p[l]) for l in range(DEPTH)]), g_out_sh, g_uq_sh, g_ukv_sh)

    def adam_nd(w, g, m, v, name):
        flat = (-1, w.shape[-1])
        return tuple(a.reshape(w.shape) for a in _adamw(w.reshape(flat), g.reshape(flat), m.reshape(flat),
                                                        v.reshape(flat), name))

    upd = [adam_nd(w_in, gw[0], m_w_in, v_w_in, "adamw_in"), adam_nd(w_out, gw[1], m_w_out, v_w_out, "adamw_out"),
           adam_nd(a_w_uq, gw[2], m_a_w_uq, v_a_w_uq, "adamw_uq"), adam_nd(a_w_ukv, gw[3], m_a_w_ukv, v_a_w_ukv, "adamw_ukv")]
    dw, mw, vw = (tuple(u[i] for u in upd) for i in range(3))
    ada_shape = w_ada.shape
    flat_ada = (DEPTH * D_MODEL, cols)
    d_ada, m_ada, v_ada = (a.reshape(ada_shape) for a in _adamw(
        w_ada.reshape(flat_ada), g_w_ada.reshape(flat_ada), m_w_ada.reshape(flat_ada), v_w_ada.reshape(flat_ada),
        "adamw_ada"))
    d_b, m_b, v_b = (a.reshape(DEPTH, 3 * D_MODEL) for a in _adamw(
        _rows(b_ada, 48), _rows(g_b_ada, 48), _rows(m_b_ada, 48), _rows(v_b_ada, 48), "adamw_b_ada"))

    def small_rows(ng, gq, gkv, rb, fb, fg):
        per_layer = [dict(norm_g=ng[l], gq=gq[l], gkv=gkv[l], rb8=jnp.pad(rb[l], ((0, 3), (0, 384 - N_REL))),
                          fb=jnp.pad(fb[l], (0, LANE - 5))) for l in range(DEPTH)]
        return _small_payload(per_layer, fg, jnp.zeros((), F32))

    w_s = small_rows(norm_g, a_q_norm_g, a_kv_norm_g, b_rel_bias, c_forget_b, final_g)
    m_s = small_rows(m_norm_g, m_a_q_norm_g, m_a_kv_norm_g, m_b_rel_bias, m_c_forget_b, m_final_g)
    v_s = small_rows(v_norm_g, v_a_q_norm_g, v_a_kv_norm_g, v_b_rel_bias, v_c_forget_b, v_final_g)
    d_s, mo_s, vo_s = _adamw(w_s, tot, m_s, v_s, "adamw_small")
    gs = _payload_split(tot)
    ds = _payload_split(d_s)
    ms = _payload_split(mo_s)
    vs = _payload_split(vo_s)

    def ordered(ada, b, sm, big):
        ng, gq, gkv, rb, fb, fg = sm
        b_in, b_out, b_uq, b_ukv = big
        return (ada, b, ng, b_in, gq, b_uq, gkv, b_ukv, rb, fb, b_out, fg)

    return (loss, dx.reshape(nb, seq, D_MODEL), *ordered(g_w_ada, g_b_ada, gs, gw), *ordered(d_ada, d_b, ds, dw),
            *ordered(m_ada, m_b, ms, mw), *ordered(v_ada, v_b, vs, vw))
```

```python
import functools

import jax
import jax.numpy as jnp
from jax import lax
from jax.experimental import pallas as pl
from jax.experimental.pallas import tpu as pltpu

F32 = jnp.float32
BF16 = jnp.bfloat16

D_MODEL = 1024
DEPTH = 2
EPS = 1e-6
NEG = -1e30
LOG2E = 1.4426950408889634
ROPE_THETA = 10000.0
A_ROPE = 32
A_Q_RANK = 384
A_KV_RANK = 256
REL_CLIP = 128
N_REL = 2 * REL_CLIP + 1
N_IN = 3621

ADAM_LR = 0.001
ADAM_B1 = 0.9
ADAM_B2 = 0.999
ADAM_EPS = 1e-08
ADAM_WD = 0.01
ADAM_STEP = 10

LANE = 128
VMEM_LIMIT = 56 * 1024 * 1024

NP_IN = 4352
Z_A = (0, 768)
Z_G = (768, 1920)
Z_QKV = tuple((1920 + 384 * i, 1920 + 384 * (i + 1)) for i in range(6))
Z_F = (4224, 4352)
IN_PIECES = ((0, 672, 96), (672, 1056, 0), (2016, 2336, 64), (3301, 3621, 64), (1056, 1376, 64), (1376, 1696, 64),
             (1696, 2016, 64), (2336, 2656, 64), (2656, 2976, 64), (2976, 3296, 64), (3296, 3301, 123))
D_CAT = 1152

TM = 512
T_CAUSAL = 256
T_BAND = 128
BAND_TILES = 5
N_DEV = 8

PAY_DMOD = (0, 96)
PAY_NORM = (96, 112)
PAY_GQ = (112, 120)
PAY_GKV = (120, 128)
PAY_RB = (128, 176)
PAY_FB = (176, 184)
PAY_FINAL = (184, 192)
PAY_LOSS = (192, 200)
PAY_ROWS = 200

PACK_COLS = 1024
PACK_ROWS = 2560
HALF_ROWS = PACK_ROWS // 2


def _params(sem=None):
    return pltpu.CompilerParams(dimension_semantics=sem, vmem_limit_bytes=VMEM_LIMIT)


def _lane_iota(shape):
    return lax.broadcasted_iota(jnp.int32, shape, len(shape) - 1)


def _silu(u):
    return u * jax.nn.sigmoid(u)


def _dsilu(u):
    s = jax.nn.sigmoid(u)
    return s * (1.0 + u * (1.0 - s))


def _rms(x, g):
    r = lax.rsqrt(jnp.mean(x * x, axis=-1, keepdims=True) + EPS)
    xh = x * r
    return xh * g, xh, r


def _rms_bwd(dy, xh, r, g):
    dxh = dy * g
    return r * (dxh - xh * jnp.mean(dxh * xh, axis=-1, keepdims=True))


def _rope(x, cos, sina, sinb):
    return x * cos + pltpu.roll(x, 16, 1) * sinb + pltpu.roll(x, LANE - 16, 1) * sina


def _rope_t(dy, cos, sina, sinb):
    return dy * cos + pltpu.roll(dy * sinb, LANE - 16, 1) + pltpu.roll(dy * sina, 16, 1)


def _split3(x):
    hi = x.astype(BF16)
    r1 = x - hi.astype(F32)
    mid = r1.astype(BF16)
    lo = (r1 - mid.astype(F32)).astype(BF16)
    return hi, mid, lo


def _dot(a, b):
    return jnp.dot(a, b, preferred_element_type=F32)


def _dot_nt(a, b):
    return lax.dot_general(a, b, (((1,), (1,)), ((), ())), preferred_element_type=F32)


def _dot_tn(a, b):
    return lax.dot_general(a, b, (((0,), (0,)), ((), ())), preferred_element_type=F32)


def _row_spec(cols):
    return pl.BlockSpec((TM, cols), lambda i: (i, 0))


def _full_spec(shape):
    return pl.BlockSpec(shape, lambda i: (0,) * len(shape))


def _ex_spec(tiles_per_ex):
    return pl.BlockSpec((1, 1, D_MODEL), lambda i: (i // tiles_per_ex, 0, 0))


def _ln_inproj(x, shift, scale, g, w_in_p, seq):
    t = x.shape[0]

    def body(x_ref, sh_ref, sc_ref, g_ref, w_ref, h_ref, za_ref, zg_ref, q0, q1, q2, q3, q4, q5, zf_ref):
        n, _, _ = _rms(x_ref[...], g_ref[...])
        h = (n * (1.0 + sc_ref[0]) + sh_ref[0]).astype(BF16)
        h_ref[...] = h
        za_ref[...] = _dot(h, w_ref[:, Z_A[0]:Z_A[1]])
        zg_ref[...] = _dot(h, w_ref[:, Z_G[0]:Z_G[1]])
        for ref, (c0, c1) in zip((q0, q1, q2, q3, q4, q5), Z_QKV):
            ref[...] = _dot(h, w_ref[:, c0:c1]).astype(BF16)
        zf_ref[...] = _dot(h, w_ref[:, Z_F[0]:Z_F[1]])

    tpe = seq // TM
    shapes = [jax.ShapeDtypeStruct((t, D_MODEL), BF16), jax.ShapeDtypeStruct((t, 768), F32),
              jax.ShapeDtypeStruct((t, D_CAT), F32)]
    shapes += [jax.ShapeDtypeStruct((t, 384), BF16)] * 6 + [jax.ShapeDtypeStruct((t, LANE), F32)]
    return pl.pallas_call(
        body, name="ln_inproj", grid=(t // TM,),
        in_specs=[_row_spec(D_MODEL), _ex_spec(tpe), _ex_spec(tpe), _full_spec((1, D_MODEL)),
                  _full_spec((D_MODEL, NP_IN))],
        out_specs=[_row_spec(D_MODEL), _row_spec(768), _row_spec(D_CAT)] + [_row_spec(384)] * 6 + [_row_spec(LANE)],
        out_shape=shapes, compiler_params=_params(("parallel",)),
    )(x, shift, scale, g, w_in_p)


def _a_up(za, gq, gkv, w_uq_p, w_ukv_p, cos, sina, sinb):
    t = za.shape[0]

    def body(za_ref, gq_ref, gkv_ref, wq_ref, wkv_ref, cos_ref, sa_ref, sb_ref, q_ref, k_ref, v_ref):
        cos_t, sa, sb = cos_ref[...], sa_ref[...], sb_ref[...]
        cqn, _, _ = _rms(za_ref[:, 0:384], gq_ref[...])
        q = _dot(cqn.astype(BF16), wq_ref[...])
        ckvn, _, _ = _rms(za_ref[:, 384:640], gkv_ref[...])
        kv = _dot(ckvn.astype(BF16), wkv_ref[...])
        kpe = za_ref[:, 640:768]
        kpe = _rope(kpe + pltpu.roll(kpe, 32, 1), cos_t, sa, sb).astype(BF16)
        for p in range(3):
            q_ref[:, 256 * p:256 * p + 128] = q[:, 256 * p:256 * p + 128].astype(BF16)
            q_ref[:, 256 * p + 128:256 * p + 256] = _rope(q[:, 256 * p + 128:256 * p + 256], cos_t, sa, sb).astype(BF16)
            k_ref[:, 256 * p:256 * p + 128] = kv[:, 128 * p:128 * p + 128].astype(BF16)
            k_ref[:, 256 * p + 128:256 * p + 256] = kpe
        v_ref[...] = kv[:, 384:768].astype(BF16)

    return pl.pallas_call(
        body, name="a_up", grid=(t // TM,),
        in_specs=[_row_spec(768), _full_spec((1, 384)), _full_spec((1, 256)), _full_spec((384, 768)),
                  _full_spec((256, 768)), _row_spec(LANE), _row_spec(LANE), _row_spec(LANE)],
        out_specs=[_row_spec(768), _row_spec(768), _row_spec(384)],
        out_shape=[jax.ShapeDtypeStruct((t, 768), BF16), jax.ShapeDtypeStruct((t, 768), BF16),
                   jax.ShapeDtypeStruct((t, 384), BF16)],
        compiler_params=_params(("parallel",)),
    )(za, gq, gkv, w_uq_p, w_ukv_p, cos, sina, sinb)


def _tri(n, upper):
    r = lax.broadcasted_iota(jnp.int32, (n, n), 0)
    c = lax.broadcasted_iota(jnp.int32, (n, n), 1)
    return jnp.where((c >= r) if upper else (c <= r), 1.0, 0.0).astype(BF16)


def _forget_fwd(zf, fb, nb, seq):
    blk = 256

    def body(zf_ref, fb_ref, f_ref):
        tri = _tri(blk, False)
        live = _lane_iota((blk, LANE)) < 5
        carry = jnp.zeros((1, LANE), F32)
        for i in range(seq // blk):
            u = zf_ref[i * blk:(i + 1) * blk, :] + fb_ref[...]
            lf = jnp.where(live, jnp.minimum(u, 0.0) - jnp.log(1.0 + jnp.exp(-jnp.abs(u))), 0.0)
            hi, mid, lo = _split3(lf)
            f_ref[i * blk:(i + 1) * blk, :] = (_dot(tri, hi) + _dot(tri, mid) + _dot(tri, lo) + carry) * LOG2E
            carry = carry + jnp.sum(lf, axis=0, keepdims=True)

    return pl.pallas_call(
        body, name="forget_fwd", grid=(nb,),
        in_specs=[pl.BlockSpec((seq, LANE), lambda b: (b, 0)), pl.BlockSpec((1, LANE), lambda b: (0, 0))],
        out_specs=pl.BlockSpec((seq, LANE), lambda b: (b, 0)),
        out_shape=jax.ShapeDtypeStruct((nb * seq, LANE), F32), compiler_params=_params(("parallel",)),
    )(zf, fb)


def _forget_bwd(df, zf, fb, nb, seq):
    blk = 256

    def body(df_ref, zf_ref, fb_ref, dz_ref, gb_ref):
        @pl.when(pl.program_id(0) == 0)
        def _():
            gb_ref[...] = jnp.zeros_like(gb_ref)

        tri = _tri(blk, True)
        live = _lane_iota((blk, LANE)) < 5
        carry = jnp.zeros((1, LANE), F32)
        gsum = jnp.zeros((1, LANE), F32)
        for i in reversed(range(seq // blk)):
            d = df_ref[i * blk:(i + 1) * blk, :]
            hi, mid, lo = _split3(d)
            dlf = _dot(tri, hi) + _dot(tri, mid) + _dot(tri, lo) + carry
            carry = carry + jnp.sum(d, axis=0, keepdims=True)
            u = zf_ref[i * blk:(i + 1) * blk, :] + fb_ref[...]
            du = jnp.where(live, dlf * jax.nn.sigmoid(-u), 0.0)
            dz_ref[i * blk:(i + 1) * blk, :] = du.astype(BF16)
            gsum = gsum + jnp.sum(du, axis=0, keepdims=True)
        gb_ref[...] += jnp.broadcast_to(gsum, gb_ref.shape)

    return pl.pallas_call(
        body, name="forget_bwd", grid=(nb,),
        in_specs=[pl.BlockSpec((seq, LANE), lambda b: (b, 0)), pl.BlockSpec((seq, LANE), lambda b: (b, 0)),
                  pl.BlockSpec((1, LANE), lambda b: (0, 0))],
        out_specs=[pl.BlockSpec((seq, LANE), lambda b: (b, 0)), pl.BlockSpec((8, LANE), lambda b: (0, 0))],
        out_shape=[jax.ShapeDtypeStruct((nb * seq, LANE), BF16), jax.ShapeDtypeStruct((8, LANE), F32)],
        compiler_params=_params(("arbitrary",)),
    )(df, zf, fb)


def _gate_outproj(x, gate, oa, ob, oc, zg, w_out_p, seq):
    t = x.shape[0]

    def body(x_ref, gate_ref, oa_ref, ob_ref, oc_ref, zg_ref, w_ref, y_ref, xn_ref):
        y = jnp.zeros((TM, D_MODEL), F32)
        for i, o_ref in enumerate((oa_ref, ob_ref, oc_ref)):
            cat = (o_ref[...] * _silu(zg_ref[:, 384 * i:384 * (i + 1)])).astype(BF16)
            y = y + _dot(cat, w_ref[384 * i:384 * (i + 1), :])
        y_ref[...] = y
        xn_ref[...] = x_ref[...] + gate_ref[0] * y

    return pl.pallas_call(
        body, name="gate_outproj", grid=(t // TM,),
        in_specs=[_row_spec(D_MODEL), _ex_spec(seq // TM), _row_spec(384), _row_spec(384), _row_spec(384),
                  _row_spec(D_CAT), _full_spec((D_CAT, D_MODEL))],
        out_specs=[_row_spec(D_MODEL), _row_spec(D_MODEL)],
        out_shape=[jax.ShapeDtypeStruct((t, D_MODEL), F32)] * 2, compiler_params=_params(("parallel",)),
    )(x, gate, oa, ob, oc, zg, w_out_p)


def _final_loss(x, target, g):
    t = x.shape[0]

    def body(x_ref, t_ref, g_ref, dx_ref, loss_ref, gg_ref):
        @pl.when(pl.program_id(0) == 0)
        def _():
            loss_ref[...] = jnp.zeros_like(loss_ref)
            gg_ref[...] = jnp.zeros_like(gg_ref)

        gv = g_ref[...]
        out, xh, r = _rms(x_ref[...], gv)
        err = out - t_ref[...]
        loss_ref[...] += 0.5 * jnp.sum(jnp.mean(err * err, axis=-1, keepdims=True), axis=0, keepdims=True)
        dout = err / D_MODEL
        gg_ref[...] += jnp.broadcast_to(jnp.sum(dout * xh, axis=0, keepdims=True), gg_ref.shape)
        dx_ref[...] = _rms_bwd(dout, xh, r, gv)

    return pl.pallas_call(
        body, name="final_loss", grid=(t // TM,),
        in_specs=[_row_spec(D_MODEL), _row_spec(D_MODEL), _full_spec((1, D_MODEL))],
        out_specs=[_row_spec(D_MODEL), _full_spec((8, LANE)), _full_spec((8, D_MODEL))],
        out_shape=[jax.ShapeDtypeStruct((t, D_MODEL), F32), jax.ShapeDtypeStruct((8, LANE), F32),
                   jax.ShapeDtypeStruct((8, D_MODEL), F32)],
        compiler_params=_params(("arbitrary",)),
    )(x, target, g)


def _head_masks(kind, rows, dq, h):
    lq = _lane_iota((rows, dq))
    lv = _lane_iota((rows, LANE))
    mq = (lq >= 64 * h) & (lq < 64 * h + 64)
    if kind == "A":
        mq = mq | ((lq >= 128 + 32 * h) & (lq < 160 + 32 * h))
    return mq, (lv >= 64 * h) & (lv < 64 * h + 64)


def _tile_mask(kind, tile, m=None):
    row = lax.broadcasted_iota(jnp.int32, (tile, tile), 0)
    col = lax.broadcasted_iota(jnp.int32, (tile, tile), 1)
    if kind == "A":
        return (col >> 6) <= (row >> 6)
    if kind == "C":
        return col <= row
    first = (m == 0) & (row >= 64) & (col < 64)
    last = (m == BAND_TILES - 1) & (row < 64) & (col >= 64)
    return jnp.logical_not(first | last)


def _attn_scale(kind):
    return 96.0 ** -0.5 if kind == "A" else 0.125


def _attn_fwd(kind, q, k, v, aux, nb, seq):
    dq = q.shape[1] // 3
    tile = T_BAND if kind == "B" else T_CAUSAL
    nq = seq // tile
    scale = _attn_scale(kind)

    def body(*refs):
        if kind == "A":
            q_ref, k_ref, v_ref, o_ref, lse_ref = refs
            aux_ref = None
        else:
            q_ref, k_ref, v_ref, aux_ref, o_ref, lse_ref = refs
        qi = pl.program_id(2)
        q2 = q_ref[...]
        res = []
        for h in range(2):
            mq, _ = _head_masks(kind, tile, dq, h)
            qh = jnp.where(mq, q2, jnp.zeros_like(q2))

            def step(kj, carry, m=None, diag=False):
                mx, l, acc = carry
                ks = pl.ds(pl.multiple_of(kj * tile, tile), tile)
                kt = k_ref[ks, :]
                vt = v_ref[ks, :]
                s = _dot_nt(qh, kt) * scale
                if kind == "B":
                    s = jnp.where(_tile_mask("B", tile, m), s + aux_ref[h, m], NEG)
                if kind == "C":
                    s = s - aux_ref[0, h, pl.ds(kj, 1), :]
                if diag:
                    s = jnp.where(_tile_mask(kind, tile), s, NEG)
                mn = jnp.maximum(mx, jnp.max(s, axis=-1, keepdims=True))
                alpha = jnp.exp(mx - mn)
                p = jnp.exp(s - mn)
                l = alpha * l + jnp.sum(p, axis=-1, keepdims=True)
                acc = alpha * acc + _dot(p.astype(BF16), vt)
                return mn, l, acc

            init = (jnp.full((tile, 1), NEG, F32), jnp.zeros((tile, 1), F32), jnp.zeros((tile, LANE), F32))
            if kind == "B":
                m0 = jnp.maximum(BAND_TILES - 1 - qi, 0)
                mx, l, acc = lax.fori_loop(m0, BAND_TILES, lambda m, c: step(qi - (BAND_TILES - 1) + m, c, m=m), init)
            else:
                carry = lax.fori_loop(0, qi, lambda kj, c: step(kj, c), init)
                mx, l, acc = step(qi, carry, diag=True)
            res.append((acc / l, mx + jnp.log(l)))
        first = _lane_iota((tile, LANE)) < 64
        o_ref[...] = jnp.where(first, res[0][0], res[1][0])
        lse_ref[...] = jnp.where(first, res[0][1], res[1][1])

    in_specs = [pl.BlockSpec((tile, dq), lambda b, p, i: (b * nq + i, p)),
                pl.BlockSpec((seq, dq), lambda b, p, i: (b, p)),
                pl.BlockSpec((seq, LANE), lambda b, p, i: (b, p))]
    args = [q, k, v]
    if kind == "B":
        in_specs.append(pl.BlockSpec((2, BAND_TILES, tile, tile), lambda b, p, i: (p, 0, 0, 0)))
        args.append(aux)
    if kind == "C":
        in_specs.append(pl.BlockSpec((1, 2, nq, tile), lambda b, p, i: (b, p, 0, 0)))
        args.append(aux)
    out_spec = pl.BlockSpec((tile, LANE), lambda b, p, i: (b * nq + i, p))
    return pl.pallas_call(
        body, name="attn_fwd_" + kind, grid=(nb, 3, nq), in_specs=in_specs, out_specs=[out_spec, out_spec],
        out_shape=[jax.ShapeDtypeStruct((nb * seq, 384), F32)] * 2,
        compiler_params=_params(("parallel", "parallel", "parallel")),
    )(*args)


def _attn_bwd(kind, q, k, v, o, do, lse, aux, nb, seq):
    dq = q.shape[1] // 3
    tile = T_BAND if kind == "B" else T_CAUSAL
    nq = seq // tile
    scale = _attn_scale(kind)
    dqk_dtype = F32 if kind == "A" else BF16

    def body(*refs):
        dfr_ref = dfq_ref = dbt_ref = aux_ref = None
        if kind == "A":
            q_ref, k_ref, v_ref, o_ref, do_ref, lse_ref, dq_ref, dk_ref, dv_ref, dk_acc, dv_acc = refs
        elif kind == "B":
            q_ref, k_ref, v_ref, o_ref, do_ref, lse_ref, aux_ref, dq_ref, dk_ref, dv_ref, dbt_ref, dk_acc, dv_acc = refs
        else:
            (q_ref, k_ref, v_ref, o_ref, do_ref, lse_ref, aux_ref, dq_ref, dk_ref, dv_ref, dfr_ref, dfq_ref,
             dk_acc, dv_acc) = refs
        dk_acc[...] = jnp.zeros_like(dk_acc)
        dv_acc[...] = jnp.zeros_like(dv_acc)
        if kind == "C":
            dfr_ref[...] = jnp.zeros_like(dfr_ref)
        if kind == "B":
            @pl.when(pl.program_id(1) == 0)
            def _():
                dbt_ref[...] = jnp.zeros_like(dbt_ref)

        def q_step(qi, _):
            qs = pl.ds(pl.multiple_of(qi * tile, tile), tile)
            q2 = q_ref[qs, :]
            do2 = do_ref[qs, :]
            o2 = o_ref[qs, :]
            lse2 = lse_ref[qs, :]
            dq_tot = jnp.zeros((tile, dq), F32)
            row_sums = []
            for h in range(2):
                mq, mv = _head_masks(kind, tile, dq, h)
                qh = jnp.where(mq, q2, jnp.zeros_like(q2))
                doh = jnp.where(mv, do2, 0.0)
                dob = doh.astype(BF16)
                delta = jnp.sum(doh * o2, axis=-1, keepdims=True)
                lseh = jnp.max(jnp.where(mv, lse2, NEG), axis=-1, keepdims=True)

                def step(kj, carry, m=None, diag=False):
                    dq_t, rs = carry
                    ks = pl.ds(pl.multiple_of(kj * tile, tile), tile)
                    kt = k_ref[ks, :]
                    vt = v_ref[ks, :]
                    s = _dot_nt(qh, kt) * scale
                    if kind == "B":
                        s = jnp.where(_tile_mask("B", tile, m), s + aux_ref[h, m], NEG)
                    if kind == "C":
                        s = s - aux_ref[0, h, pl.ds(kj, 1), :]
                    if diag:
                        s = jnp.where(_tile_mask(kind, tile), s, NEG)
                    p = jnp.exp(s - lseh)
                    ds = p * (_dot_nt(dob, vt) - delta)
                    if kind == "B":
                        dbt_ref[h, m] += ds
                    if kind == "C":
                        dfr_ref[0, h, pl.ds(kj, 1), :] -= jnp.sum(ds, axis=0, keepdims=True)
                        rs = rs + jnp.sum(ds, axis=-1, keepdims=True)
                    dss = (ds * scale).astype(BF16)
                    dv_acc[ks, :] += _dot_tn(p.astype(BF16), dob)
                    dk_acc[ks, :] += _dot_tn(dss, qh)
                    return dq_t + jnp.where(mq, _dot(dss, kt), 0.0), rs

                zero = (jnp.zeros((tile, dq), F32), jnp.zeros((tile, 1), F32))
                if kind == "B":
                    m0 = jnp.maximum(BAND_TILES - 1 - qi, 0)
                    dq_h, rs_h = lax.fori_loop(m0, BAND_TILES, lambda m, c: step(qi - (BAND_TILES - 1) + m, c, m=m), zero)
                else:
                    carry = lax.fori_loop(0, qi, lambda kj, c: step(kj, c), zero)
                    dq_h, rs_h = step(qi, carry, diag=True)
                dq_tot = dq_tot + dq_h
                row_sums.append(rs_h)
            dq_ref[qs, :] = dq_tot.astype(dqk_dtype)
            if kind == "C":
                dfq_ref[qs, :] = jnp.where(_lane_iota((tile, LANE)) < 64, row_sums[0], row_sums[1])
            return 0

        lax.fori_loop(0, nq, q_step, 0)
        dk_ref[...] = dk_acc[...].astype(dqk_dtype)
        dv_ref[...] = dv_acc[...].astype(BF16)

    def seq_spec(cols):
        return pl.BlockSpec((seq, cols), lambda p, b: (b, p))

    in_specs = [seq_spec(dq), seq_spec(dq), seq_spec(LANE), seq_spec(LANE), seq_spec(LANE), seq_spec(LANE)]
    args = [q, k, v, o, do, lse]
    out_specs = [seq_spec(dq), seq_spec(dq), seq_spec(LANE)]
    out_shape = [jax.ShapeDtypeStruct((nb * seq, 3 * dq), dqk_dtype)] * 2 + [jax.ShapeDtypeStruct((nb * seq, 384), BF16)]
    if kind == "B":
        spec = pl.BlockSpec((2, BAND_TILES, tile, tile), lambda p, b: (p, 0, 0, 0))
        in_specs.append(spec)
        args.append(aux)
        out_specs.append(spec)
        out_shape.append(jax.ShapeDtypeStruct((6, BAND_TILES, tile, tile), F32))
    if kind == "C":
        spec = pl.BlockSpec((1, 2, nq, tile), lambda p, b: (b, p, 0, 0))
        in_specs.append(spec)
        args.append(aux)
        out_specs += [spec, seq_spec(LANE)]
        out_shape += [jax.ShapeDtypeStruct((nb, 6, nq, tile), F32), jax.ShapeDtypeStruct((nb * seq, 384), F32)]
    return pl.pallas_call(
        body, name="attn_bwd_" + kind, grid=(3, nb), in_specs=in_specs, out_specs=out_specs, out_shape=out_shape,
        scratch_shapes=[pltpu.VMEM((seq, dq), F32), pltpu.VMEM((seq, LANE), F32)],
        compiler_params=_params(("arbitrary", "arbitrary")),
    )(*args)


BAND_W = BAND_TILES * T_BAND


def _segments(kind, qi, tile):
    r0 = qi * tile
    if kind == "B":
        lo = max(qi - (BAND_TILES - 1), 0) * tile
        return [(lo, r0 + tile, False, lo - (qi - (BAND_TILES - 1)) * tile)]
    return ([(0, r0, False, 0)] if qi else []) + [(r0, r0 + tile, True, 0)]


def _scores(kind, qh, k_ref, aux_ref, h, seg, tile, scale):
    a, b, diag, c0 = seg
    s = _dot_nt(qh, k_ref[a:b, :]) * (scale * LOG2E)
    if kind == "B":
        return s + aux_ref[h, :, c0:BAND_W]
    if kind == "C":
        s = s - aux_ref[0, h, :, a:b]
    if diag:
        s = jnp.where(_tile_mask(kind, tile), s, NEG)
    return s


def _second_head(kind, pair, fn):
    del kind, pair
    fn()


def _attn_fwd(kind, q, k, v, aux, nb, seq):
    dq = q.shape[1] // 3
    tile = T_BAND if kind == "B" else T_CAUSAL
    nq = seq // tile
    scale = _attn_scale(kind)

    def body(*refs):
        if kind == "A":
            q_ref, k_ref, v_ref, o_ref, lse_ref = refs
            aux_ref = None
        else:
            q_ref, k_ref, v_ref, aux_ref, o_ref, lse_ref = refs
        pair = pl.program_id(1)
        for qi in range(nq):
            rows = slice(qi * tile, (qi + 1) * tile)
            segs = _segments(kind, qi, tile)

            def head(h, rows=rows, segs=segs):
                q2 = q_ref[rows, :]
                mq, mv = _head_masks(kind, tile, dq, h)
                qh = jnp.where(mq, q2, jnp.zeros_like(q2))
                ss = [_scores(kind, qh, k_ref, aux_ref, h, seg, tile, scale) for seg in segs]
                mx = functools.reduce(jnp.maximum, [jnp.max(s, axis=-1, keepdims=True) for s in ss])
                ps = [jnp.exp2(s - mx) for s in ss]
                l = functools.reduce(jnp.add, [jnp.sum(p, axis=-1, keepdims=True) for p in ps])
                acc = functools.reduce(jnp.add, [_dot(p.astype(BF16), v_ref[seg[0]:seg[1], :]) for p, seg in zip(ps, segs)])
                o_h = jnp.where(mv, acc / l, 0.0)
                lse_h = jnp.where(mv, mx + jnp.log(l) * LOG2E, 0.0)
                if h == 0:
                    o_ref[rows, :] = o_h
                    lse_ref[rows, :] = lse_h
                else:
                    o_ref[rows, :] += o_h
                    lse_ref[rows, :] += lse_h

            head(0)
            _second_head(kind, pair, functools.partial(head, 1))

    def seq_spec(cols):
        return pl.BlockSpec((seq, cols), lambda b, p: (b, p))

    in_specs = [seq_spec(dq), seq_spec(dq), seq_spec(LANE)]
    args = [q, k, v]
    if kind == "B":
        in_specs.append(pl.BlockSpec((2, tile, BAND_W), lambda b, p: (p, 0, 0)))
        args.append(aux)
    if kind == "C":
        in_specs.append(pl.BlockSpec((1, 2, 1, seq), lambda b, p: (b, p, 0, 0)))
        args.append(aux)
    return pl.pallas_call(
        body, name="attn_fwd_" + kind, grid=(nb, 3), in_specs=in_specs, out_specs=[seq_spec(LANE), seq_spec(LANE)],
        out_shape=[jax.ShapeDtypeStruct((nb * seq, 384), F32)] * 2, compiler_params=_params(("parallel", "parallel")),
    )(*args)


def _attn_bwd(kind, q, k, v, o, do, lse, aux, nb, seq):
    dq = q.shape[1] // 3
    tile = T_BAND if kind == "B" else T_CAUSAL
    nq = seq // tile
    scale = _attn_scale(kind)
    dqk_dtype = F32 if kind == "A" else BF16

    def body(*refs):
        dfr_ref = dfq_ref = dbt_ref = aux_ref = None
        if kind == "A":
            q_ref, k_ref, v_ref, o_ref, do_ref, lse_ref, dq_ref, dk_ref, dv_ref, dkt_acc, dvt_acc = refs
        elif kind == "B":
            q_ref, k_ref, v_ref, o_ref, do_ref, lse_ref, aux_ref, dq_ref, dk_ref, dv_ref, dbt_ref, dkt_acc, dvt_acc = refs
        else:
            (q_ref, k_ref, v_ref, o_ref, do_ref, lse_ref, aux_ref, dq_ref, dk_ref, dv_ref, dfr_ref, dfq_ref,
             dkt_acc, dvt_acc) = refs
        dkt_acc[...] = jnp.zeros_like(dkt_acc)
        dvt_acc[...] = jnp.zeros_like(dvt_acc)
        if kind == "C":
            dfr_ref[...] = jnp.zeros_like(dfr_ref)
        if kind == "B":
            @pl.when(pl.program_id(1) == 0)
            def _():
                dbt_ref[...] = jnp.zeros_like(dbt_ref)

        pair = pl.program_id(0)
        for qi in range(nq):
            rows = slice(qi * tile, (qi + 1) * tile)
            segs = _segments(kind, qi, tile)

            def head(h, rows=rows, segs=segs):
                q2 = q_ref[rows, :]
                do2 = do_ref[rows, :]
                mq, mv = _head_masks(kind, tile, dq, h)
                qh = jnp.where(mq, q2, jnp.zeros_like(q2))
                doh = jnp.where(mv, do2, 0.0)
                dob = doh.astype(BF16)
                qht = qh.astype(F32).T.astype(BF16)
                dobt = doh.T.astype(BF16)
                delta = jnp.sum(doh * o_ref[rows, :], axis=-1, keepdims=True)
                lseh = jnp.max(jnp.where(mv, lse_ref[rows, :], NEG), axis=-1, keepdims=True)
                rs = jnp.zeros((tile, 1), F32)
                dq_h = jnp.zeros((tile, dq), F32)
                for seg in segs:
                    a, b, _, c0 = seg
                    p = jnp.exp2(_scores(kind, qh, k_ref, aux_ref, h, seg, tile, scale) - lseh)
                    ds = p * (_dot_nt(dob, v_ref[a:b, :]) - delta)
                    if kind == "B":
                        dbt_ref[h, :, c0:BAND_W] += ds
                    if kind == "C":
                        dfr_ref[0, h, :, a:b] -= jnp.sum(ds, axis=0, keepdims=True)
                        rs = rs + jnp.sum(ds, axis=-1, keepdims=True)
                    dss = (ds * scale).astype(BF16)
                    dvt_acc[:, a:b] += _dot(dobt, p.astype(BF16))
                    dkt_acc[:, a:b] += _dot(qht, dss)
                    dq_h = dq_h + _dot(dss, k_ref[a:b, :])
                dq_h = jnp.where(mq, dq_h, 0.0).astype(dqk_dtype)
                if h == 0:
                    dq_ref[rows, :] = dq_h
                else:
                    dq_ref[rows, :] += dq_h
                if kind == "C":
                    if h == 0:
                        dfq_ref[rows, :] = jnp.where(mv, rs, 0.0)
                    else:
                        dfq_ref[rows, :] += jnp.where(mv, rs, 0.0)

            head(0)
            _second_head(kind, pair, functools.partial(head, 1))
        for j in range(seq // 256):
            cols = slice(256 * j, 256 * (j + 1))
            dk_ref[cols, :] = dkt_acc[:, cols].T.astype(dqk_dtype)
            dv_ref[cols, :] = dvt_acc[:, cols].T.astype(BF16)

    def seq_spec(cols):
        return pl.BlockSpec((seq, cols), lambda p, b: (b, p))

    in_specs = [seq_spec(dq), seq_spec(dq), seq_spec(LANE), seq_spec(LANE), seq_spec(LANE), seq_spec(LANE)]
    args = [q, k, v, o, do, lse]
    out_specs = [seq_spec(dq), seq_spec(dq), seq_spec(LANE)]
    out_shape = [jax.ShapeDtypeStruct((nb * seq, 3 * dq), dqk_dtype)] * 2 + [jax.ShapeDtypeStruct((nb * seq, 384), BF16)]
    if kind == "B":
        spec = pl.BlockSpec((2, tile, BAND_W), lambda p, b: (p, 0, 0))
        in_specs.append(spec)
        args.append(aux)
        out_specs.append(spec)
        out_shape.append(jax.ShapeDtypeStruct((6, tile, BAND_W), F32))
    if kind == "C":
        spec = pl.BlockSpec((1, 2, 1, seq), lambda p, b: (b, p, 0, 0))
        in_specs.append(spec)
        args.append(aux)
        out_specs += [spec, seq_spec(LANE)]
        out_shape += [jax.ShapeDtypeStruct((nb, 6, 1, seq), F32), jax.ShapeDtypeStruct((nb * seq, 384), F32)]
    return pl.pallas_call(
        body, name="attn_bwd_" + kind, grid=(3, nb), in_specs=in_specs, out_specs=out_specs, out_shape=out_shape,
        scratch_shapes=[pltpu.VMEM((dq, seq), F32), pltpu.VMEM((LANE, seq), F32)],
        compiler_params=_params(("arbitrary", "arbitrary")),
    )(*args)


BIAS_FLAT = T_BAND * BAND_W
BIAS_CHUNK = 4 * BAND_W


def _rel_onehot(chunk):
    lane = lax.broadcasted_iota(jnp.int32, (384, chunk), 1)
    r = lax.broadcasted_iota(jnp.int32, (384, chunk), 0)
    sub = jnp.where(lane >= BAND_W, 1, 0) + jnp.where(lane >= 2 * BAND_W, 1, 0) + jnp.where(lane >= 3 * BAND_W, 1, 0)
    i = pl.program_id(0) * 4 + sub
    col = lane - sub * BAND_W
    idx = jnp.clip(BAND_W - T_BAND + i - col, -REL_CLIP, REL_CLIP) + REL_CLIP
    return jnp.where(idx == r, 1.0, 0.0).astype(BF16)


BIAS_G = 768
BIAS_EDGE = BIAS_G - N_REL


def _bias_line(rel_bias):
    g = jnp.concatenate([jnp.broadcast_to(rel_bias[:, N_REL - 1:], (rel_bias.shape[0], BIAS_EDGE)),
                         jnp.flip(rel_bias, axis=1)], axis=1)
    return jnp.pad(g, ((0, 8 - g.shape[0]), (0, 0)))


def _bias_unline(dg):
    return jnp.flip(dg[:, BIAS_EDGE:], axis=1)


def _bias_expand(g8):
    def body(g_ref, out_ref):
        line = jnp.broadcast_to(g_ref[0] * LOG2E, (T_BAND, BIAS_G))
        slab = pltpu.roll(line, 1, 1, stride=1, stride_axis=0)[:, LANE:BIAS_G]
        row = lax.broadcasted_iota(jnp.int32, (T_BAND, BAND_W), 0)
        col = lax.broadcasted_iota(jnp.int32, (T_BAND, BAND_W), 1)
        hidden = ((row >= 64) & (col < 64)) | ((row < 64) & (col >= BAND_W - 64))
        out_ref[0] = jnp.where(hidden, NEG, slab)

    return pl.pallas_call(
        body, name="bias_expand", grid=(8,), in_specs=[pl.BlockSpec((1, 1, BIAS_G), lambda h: (h, 0, 0))],
        out_specs=pl.BlockSpec((1, T_BAND, BAND_W), lambda h: (h, 0, 0)),
        out_shape=jax.ShapeDtypeStruct((8, T_BAND, BAND_W), F32), compiler_params=_params(("parallel",)),
    )(g8.reshape(8, 1, BIAS_G))


def _bias_reduce(d_slab):
    def body(d_ref, out_ref):
        r = lax.broadcasted_iota(jnp.int32, (T_BAND, T_BAND), 0)
        k = lax.broadcasted_iota(jnp.int32, (T_BAND, T_BAND), 1)
        flip = jnp.where(r + k == T_BAND - 1, 1.0, 0.0).astype(BF16)
        hi, mid, lo = _split3(d_ref[0])
        d_rev = _dot(flip, hi) + _dot(flip, mid) + _dot(flip, lo)
        wide = jnp.concatenate([jnp.zeros((T_BAND, LANE), F32), d_rev, jnp.zeros((T_BAND, 2 * LANE), F32)], axis=1)
        skew = pltpu.roll(wide, 0, 1, stride=1, stride_axis=0)
        dg = jnp.sum(skew, axis=0, keepdims=True)[:, LANE:LANE + BIAS_G]
        lane = _lane_iota((1, BIAS_G))
        clipped = jnp.sum(jnp.where(lane <= BIAS_EDGE, dg, 0.0), axis=1, keepdims=True)
        out_ref[0] = jnp.where(lane == BIAS_EDGE, clipped, dg)

    return pl.pallas_call(
        body, name="bias_reduce", grid=(8,), in_specs=[pl.BlockSpec((1, T_BAND, BAND_W), lambda h: (h, 0, 0))],
        out_specs=pl.BlockSpec((1, 1, BIAS_G), lambda h: (h, 0, 0)),
        out_shape=jax.ShapeDtypeStruct((8, 1, BIAS_G), F32), compiler_params=_params(("parallel",)),
    )(d_slab).reshape(8, BIAS_G)


def _outproj_bwd(dxn, y, gate, oa, ob, oc, zg, w_out_p, w_out_pt, nb, seq):
    t = dxn.shape[0]
    tpe = seq // TM

    def body(dxn_ref, y_ref, gate_ref, oa_ref, ob_ref, oc_ref, zg_ref, w_ref, wt_ref,
             doa_ref, dob_ref, doc_ref, dzg_ref, gw_ref, dgate_ref):
        i = pl.program_id(0)

        @pl.when(i == 0)
        def _():
            gw_ref[...] = jnp.zeros_like(gw_ref)

        @pl.when(i % tpe == 0)
        def _():
            dgate_ref[...] = jnp.zeros_like(dgate_ref)

        dxn_t = dxn_ref[...]
        dgate_ref[0] += jnp.sum(dxn_t * y_ref[...], axis=0, keepdims=True)
        dy = (dxn_t * gate_ref[0]).astype(BF16)
        for gi, (o_ref, do_ref) in enumerate(((oa_ref, doa_ref), (ob_ref, dob_ref), (oc_ref, doc_ref))):
            cols = slice(384 * gi, 384 * (gi + 1))
            u = zg_ref[:, cols]
            o_t = o_ref[...]
            su = _silu(u)
            dcat = _dot(dy, wt_ref[:, cols])
            do_ref[...] = dcat * su
            dzg_ref[:, cols] = (dcat * o_t * _dsilu(u)).astype(BF16)
            gw_ref[cols, :] += _dot_tn((o_t * su).astype(BF16), dy)

    return pl.pallas_call(
        body, name="outproj_bwd", grid=(t // TM,),
        in_specs=[_row_spec(D_MODEL), _row_spec(D_MODEL), _ex_spec(tpe), _row_spec(384), _row_spec(384), _row_spec(384),
                  _row_spec(D_CAT), _full_spec((D_CAT, D_MODEL)), _full_spec((D_MODEL, D_CAT))],
        out_specs=[_row_spec(384), _row_spec(384), _row_spec(384), _row_spec(D_CAT), _full_spec((D_CAT, D_MODEL)),
                   _ex_spec(tpe)],
        out_shape=[jax.ShapeDtypeStruct((t, 384), F32)] * 3 + [jax.ShapeDtypeStruct((t, D_CAT), BF16),
                                                                jax.ShapeDtypeStruct((D_CAT, D_MODEL), F32),
                                                                jax.ShapeDtypeStruct((nb, 1, D_MODEL), F32)],
        compiler_params=_params(("arbitrary",)),
    )(dxn, y, gate, oa, ob, oc, zg, w_out_p, w_out_pt)


def _a_up_bwd(dqa, dka, dva, za, gq, gkv, w_uq_pt, w_ukv_pt, cos, sina, sinb):
    t = za.shape[0]

    def body(dq_ref, dk_ref, dv_ref, za_ref, gq_ref, gkv_ref, wqt_ref, wkvt_ref, cos_ref, sa_ref, sb_ref,
             dza_ref, gwq_ref, gwkv_ref, ggq_ref, ggkv_ref, dqb, dkvb):
        @pl.when(pl.program_id(0) == 0)
        def _():
            gwq_ref[...] = jnp.zeros_like(gwq_ref)
            gwkv_ref[...] = jnp.zeros_like(gwkv_ref)
            ggq_ref[...] = jnp.zeros_like(ggq_ref)
            ggkv_ref[...] = jnp.zeros_like(ggkv_ref)

        cos_t, sa, sb = cos_ref[...], sa_ref[...], sb_ref[...]
        dkpe = jnp.zeros((TM, LANE), F32)
        for p in range(3):
            dqb[:, 256 * p:256 * p + 128] = dq_ref[:, 256 * p:256 * p + 128].astype(BF16)
            dqb[:, 256 * p + 128:256 * p + 256] = _rope_t(dq_ref[:, 256 * p + 128:256 * p + 256], cos_t, sa, sb).astype(BF16)
            dkvb[:, 128 * p:128 * p + 128] = dk_ref[:, 256 * p:256 * p + 128].astype(BF16)
            dkpe = dkpe + dk_ref[:, 256 * p + 128:256 * p + 256]
        dkvb[:, 384:768] = dv_ref[...]
        dkpe = _rope_t(dkpe, cos_t, sa, sb)
        dkpe = jnp.where(_lane_iota((TM, LANE)) < A_ROPE, dkpe + pltpu.roll(dkpe, LANE - 32, 1), 0.0)

        gqv = gq_ref[...]
        cqn, cqh, rq = _rms(za_ref[:, 0:384], gqv)
        dq_t = dqb[...]
        gwq_ref[...] += _dot_tn(cqn.astype(BF16), dq_t)
        dcqn = _dot(dq_t, wqt_ref[...])
        ggq_ref[...] += jnp.broadcast_to(jnp.sum(dcqn * cqh, axis=0, keepdims=True), ggq_ref.shape)
        dza_ref[:, 0:384] = _rms_bwd(dcqn, cqh, rq, gqv).astype(BF16)

        gkvv = gkv_ref[...]
        ckvn, ckvh, rkv = _rms(za_ref[:, 384:640], gkvv)
        dkv_t = dkvb[...]
        gwkv_ref[...] += _dot_tn(ckvn.astype(BF16), dkv_t)
        dckvn = _dot(dkv_t, wkvt_ref[...])
        ggkv_ref[...] += jnp.broadcast_to(jnp.sum(dckvn * ckvh, axis=0, keepdims=True), ggkv_ref.shape)
        dza_ref[:, 384:640] = _rms_bwd(dckvn, ckvh, rkv, gkvv).astype(BF16)
        dza_ref[:, 640:768] = dkpe.astype(BF16)

    return pl.pallas_call(
        body, name="a_up_bwd", grid=(t // TM,),
        in_specs=[_row_spec(768), _row_spec(768), _row_spec(384), _row_spec(768), _full_spec((1, 384)),
                  _full_spec((1, 256)), _full_spec((768, 384)), _full_spec((768, 256)), _row_spec(LANE), _row_spec(LANE),
                  _row_spec(LANE)],
        out_specs=[_row_spec(768), _full_spec((384, 768)), _full_spec((256, 768)), _full_spec((8, 384)),
                   _full_spec((8, 256))],
        out_shape=[jax.ShapeDtypeStruct((t, 768), BF16), jax.ShapeDtypeStruct((384, 768), F32),
                   jax.ShapeDtypeStruct((256, 768), F32), jax.ShapeDtypeStruct((8, 384), F32),
                   jax.ShapeDtypeStruct((8, 256), F32)],
        scratch_shapes=[pltpu.VMEM((TM, 768), BF16), pltpu.VMEM((TM, 768), BF16)],
        compiler_params=_params(("arbitrary",)),
    )(dqa, dka, dva, za, gq, gkv, w_uq_pt, w_ukv_pt, cos, sina, sinb)


def _dz_cols():
    return (Z_A, Z_G) + Z_QKV + (Z_F,)


def _inproj_bwd_dx(dz, dxn, x, shift, scale, g, w_in_pt, nb, seq):
    t = x.shape[0]
    tpe = seq // TM
    cols = _dz_cols()

    def body(*refs):
        dz_refs = refs[:len(cols)]
        dxn_ref, x_ref, sh_ref, sc_ref, g_ref, wt_ref, dx_ref, dsh_ref, dsc_ref, dg_ref = refs[len(cols):]
        i = pl.program_id(0)

        @pl.when(i == 0)
        def _():
            dg_ref[...] = jnp.zeros_like(dg_ref)

        @pl.when(i % tpe == 0)
        def _():
            dsh_ref[...] = jnp.zeros_like(dsh_ref)
            dsc_ref[...] = jnp.zeros_like(dsc_ref)

        dh = jnp.zeros((TM, D_MODEL), F32)
        for ref, (c0, c1) in zip(dz_refs, cols):
            dh = dh + _dot(ref[...], wt_ref[c0:c1, :])
        gv = g_ref[...]
        n, xh, r = _rms(x_ref[...], gv)
        dsh_ref[0] += jnp.sum(dh, axis=0, keepdims=True)
        dsc_ref[0] += jnp.sum(dh * n, axis=0, keepdims=True)
        dn = dh * (1.0 + sc_ref[0])
        dg_ref[...] += jnp.broadcast_to(jnp.sum(dn * xh, axis=0, keepdims=True), dg_ref.shape)
        dx_ref[...] = dxn_ref[...] + _rms_bwd(dn, xh, r, gv)

    in_specs = [_row_spec(c1 - c0) for c0, c1 in cols]
    in_specs += [_row_spec(D_MODEL), _row_spec(D_MODEL), _ex_spec(tpe), _ex_spec(tpe), _full_spec((1, D_MODEL)),
                 _full_spec((NP_IN, D_MODEL))]
    return pl.pallas_call(
        body, name="inproj_bwd_dx", grid=(t // TM,), in_specs=in_specs,
        out_specs=[_row_spec(D_MODEL), _ex_spec(tpe), _ex_spec(tpe), _full_spec((8, D_MODEL))],
        out_shape=[jax.ShapeDtypeStruct((t, D_MODEL), F32), jax.ShapeDtypeStruct((nb, 1, D_MODEL), F32),
                   jax.ShapeDtypeStruct((nb, 1, D_MODEL), F32), jax.ShapeDtypeStruct((8, D_MODEL), F32)],
        compiler_params=_params(("arbitrary",)),
    )(*dz, dxn, x, shift, scale, g, w_in_pt)


def _inproj_bwd_dw(h, dz, name):
    t = h.shape[0]
    widths = [d.shape[1] for d in dz]
    total = sum(widths)

    def body(*refs):
        h_ref = refs[0]
        dz_refs = refs[1:1 + len(dz)]
        gw_ref = refs[1 + len(dz)]

        @pl.when(pl.program_id(0) == 0)
        def _():
            gw_ref[...] = jnp.zeros_like(gw_ref)

        h_t = h_ref[...]
        c0 = 0
        for ref, w in zip(dz_refs, widths):
            gw_ref[:, c0:c0 + w] += _dot_tn(h_t, ref[...])
            c0 += w

    return pl.pallas_call(
        body, name=name, grid=(t // TM,), in_specs=[_row_spec(D_MODEL)] + [_row_spec(w) for w in widths],
        out_specs=_full_spec((D_MODEL, total)), out_shape=jax.ShapeDtypeStruct((D_MODEL, total), F32),
        compiler_params=_params(("arbitrary",)),
    )(h, *dz)


def _ada_fwd(c_all, w_ada, b_cols):
    n = c_all.shape[0]
    cols = w_ada.shape[2]

    def body(c_ref, w_ref, b_ref, out_ref):
        act = _silu(c_ref[...]).astype(BF16)
        out_ref[0] = _dot(act, w_ref[0].astype(BF16)) + b_ref[0]

    return pl.pallas_call(
        body, name="ada_fwd", grid=(DEPTH,),
        in_specs=[pl.BlockSpec((n, D_MODEL), lambda l: (0, 0)), pl.BlockSpec((1, D_MODEL, cols), lambda l: (l, 0, 0)),
                  pl.BlockSpec((1, 1, cols), lambda l: (l, 0, 0))],
        out_specs=pl.BlockSpec((1, n, cols), lambda l: (l, 0, 0)),
        out_shape=jax.ShapeDtypeStruct((DEPTH, n, cols), F32), compiler_params=_params(("parallel",)),
    )(c_all, w_ada, b_cols)


def _ada_bwd(c_all, dmod_cols, dmod_all):
    n = c_all.shape[0]
    cols = dmod_cols.shape[2]

    def body(c_ref, dc_ref, da_ref, gw_ref, gb_ref):
        act = _silu(c_ref[...]).astype(BF16)
        gw_ref[0] = _dot_tn(act, dc_ref[0].astype(BF16))
        gb_ref[0] = jnp.sum(da_ref[0], axis=0, keepdims=True)

    return pl.pallas_call(
        body, name="ada_bwd", grid=(DEPTH,),
        in_specs=[pl.BlockSpec((n, D_MODEL), lambda l: (0, 0)), pl.BlockSpec((1, n, cols), lambda l: (l, 0, 0)),
                  pl.BlockSpec((1, n, 3 * D_MODEL), lambda l: (l, 0, 0))],
        out_specs=[pl.BlockSpec((1, D_MODEL, cols), lambda l: (l, 0, 0)),
                   pl.BlockSpec((1, 1, 3 * D_MODEL), lambda l: (l, 0, 0))],
        out_shape=[jax.ShapeDtypeStruct((DEPTH, D_MODEL, cols), F32), jax.ShapeDtypeStruct((DEPTH, 1, 3 * D_MODEL), F32)],
        compiler_params=_params(("parallel",)),
    )(c_all, dmod_cols, dmod_all)


def _sum_blocks(parts, name):
    n, rows, cols = parts.shape
    tr = rows if rows <= 256 else 8 * next(d for d in range(32, 0, -1) if (rows // 8) % d == 0)

    def body(p_ref, out_ref):
        acc = p_ref[0].astype(F32)
        for k in range(1, n):
            acc = acc + p_ref[k].astype(F32)
        out_ref[...] = acc

    return pl.pallas_call(
        body, name=name, grid=(rows // tr,), in_specs=[pl.BlockSpec((n, tr, cols), lambda i: (0, i, 0))],
        out_specs=pl.BlockSpec((tr, cols), lambda i: (i, 0)), out_shape=jax.ShapeDtypeStruct((rows, cols), F32),
        compiler_params=_params(("parallel",)),
    )(parts)


def _adamw(w, g, m, v, name):
    rows, cols = w.shape
    tr = next(t for t in (rows, 256, 128, 64, 32, 16, 8) if rows % t == 0 and t * cols <= 256 * 1024)

    def body(w_ref, g_ref, m_ref, v_ref, d_ref, mo_ref, vo_ref):
        gv = g_ref[...]
        mn = ADAM_B1 * m_ref[...] + (1.0 - ADAM_B1) * gv
        vn = ADAM_B2 * v_ref[...] + (1.0 - ADAM_B2) * jnp.square(gv)
        m_hat = mn / (1.0 - ADAM_B1 ** ADAM_STEP)
        v_hat = vn / (1.0 - ADAM_B2 ** ADAM_STEP)
        d_ref[...] = -ADAM_LR * (m_hat / (jnp.sqrt(v_hat) + ADAM_EPS) + ADAM_WD * w_ref[...])
        mo_ref[...] = mn
        vo_ref[...] = vn

    spec = pl.BlockSpec((tr, cols), lambda i: (i, 0))
    return pl.pallas_call(
        body, name=name, grid=(rows // tr,), in_specs=[spec] * 4, out_specs=[spec] * 3,
        out_shape=[jax.ShapeDtypeStruct((rows, cols), F32)] * 3, compiler_params=_params(("parallel",)),
    )(w, g, m, v)


ALL_FLIPS = tuple(range(1, N_DEV))


def _exchange(src, flips, mode, name):
    _, rows, cols = src.shape
    nslot = 2 if mode == "pair" else N_DEV
    nf = len(flips)

    def body(src_ref, dst_ref, send_sems, recv_sems, local_sem):
        x, y, c = lax.axis_index("x"), lax.axis_index("y"), lax.axis_index("c")
        me = 4 * x + 2 * y + c

        def slot(j):
            return (j & 1) if mode == "pair" else j

        own = pltpu.make_async_copy(src_ref.at[me if mode == "scatter" else 0], dst_ref.at[slot(me)], local_sem)
        own.start()
        copies = []
        for i, f in enumerate(flips):
            peer = me ^ f
            to = (1 - x if f & 4 else x, 1 - y if f & 2 else y, 1 - c if f & 1 else c)
            cp = pltpu.make_async_remote_copy(
                src_ref=src_ref.at[peer if mode == "scatter" else 0], dst_ref=dst_ref.at[slot(me)],
                send_sem=send_sems.at[i], recv_sem=recv_sems.at[i], device_id=to, device_id_type=pl.DeviceIdType.MESH)
            cp.start()
            copies.append(cp)
        for i, f in enumerate(flips):
            peer = me ^ f
            to = (1 - x if f & 4 else x, 1 - y if f & 2 else y, 1 - c if f & 1 else c)
            pltpu.make_async_remote_copy(
                src_ref=src_ref.at[0], dst_ref=dst_ref.at[slot(peer)], send_sem=send_sems.at[i],
                recv_sem=recv_sems.at[i], device_id=to, device_id_type=pl.DeviceIdType.MESH).wait_recv()
        for cp in copies:
            cp.wait_send()
        own.wait()

    return pl.pallas_call(
        body, name=name, out_shape=jax.ShapeDtypeStruct((nslot, rows, cols), src.dtype),
        in_specs=[pl.BlockSpec(memory_space=pl.ANY)], out_specs=pl.BlockSpec(memory_space=pl.ANY),
        scratch_shapes=[pltpu.SemaphoreType.DMA((nf,)), pltpu.SemaphoreType.DMA((nf,)), pltpu.SemaphoreType.DMA],
    )(src)


def _transfer(name, srcs, dst_shapes, plan, in_place=False):
    n_arr = len(srcs)
    probe = plan(0, 0, 0)
    n_steps = len(probe)

    def body(*refs):
        src_refs, dst_refs = refs[:n_arr], refs[n_arr:2 * n_arr]
        send_sems, recv_sems, local_sems = refs[2 * n_arr:]
        x, y, c = lax.axis_index("x"), lax.axis_index("y"), lax.axis_index("c")
        steps = plan(x, y, c)

        def rows(ref, r0, n):
            return ref.at[:, pl.ds(r0, n), :]

        def arrival(t):
            a, _, _, n, _, f, _ = steps[t]
            return pltpu.make_async_remote_copy(
                src_ref=rows(dst_refs[a], 0, n), dst_ref=rows(dst_refs[a], 0, n), send_sem=send_sems.at[t],
                recv_sem=recv_sems.at[t], device_id=(x, y, c), device_id_type=pl.DeviceIdType.MESH)

        arrived, started = set(), []
        for t, (a, from_dst, sr, n, dr, f, after) in enumerate(steps):
            for u in after:
                if u not in arrived:
                    arrival(u).wait_recv()
                    arrived.add(u)
            src = rows(dst_refs[a] if from_dst else src_refs[a], sr, n)
            dst = rows(dst_refs[a], dr, n)
            if f == 0:
                cp = pltpu.make_async_copy(src, dst, local_sems.at[t])
            else:
                to = (1 - x if f & 4 else x, 1 - y if f & 2 else y, 1 - c if f & 1 else c)
                cp = pltpu.make_async_remote_copy(src_ref=src, dst_ref=dst, send_sem=send_sems.at[t],
                                                  recv_sem=recv_sems.at[t], device_id=to,
                                                  device_id_type=pl.DeviceIdType.MESH)
            cp.start()
            started.append(cp)
        for t, step in enumerate(steps):
            if step[5] != 0 and t not in arrived:
                arrival(t).wait_recv()
        for cp, step in zip(started, steps):
            if step[5] == 0:
                cp.wait()
            else:
                cp.wait_send()

    any_spec = pl.BlockSpec(memory_space=pl.ANY)
    return pl.pallas_call(
        body, name=name, out_shape=[jax.ShapeDtypeStruct(s, d) for s, d in dst_shapes],
        in_specs=[any_spec] * n_arr, out_specs=[any_spec] * n_arr,
        input_output_aliases={a: a for a in range(n_arr)} if in_place else {},
        scratch_shapes=[pltpu.SemaphoreType.DMA((n_steps,)), pltpu.SemaphoreType.DMA((n_steps,)),
                        pltpu.SemaphoreType.DMA((n_steps,))],
    )(*srcs)


CHIP_FLIPS = (2, 4, 6)


def _gather_plan(chip_rows):
    def plan(x, y, c):
        steps = []
        for a, rc in enumerate(chip_rows):
            h = rc // 2
            mine = rc * (2 * x + y) + h * c
            steps.append((a, False, h * c, h, mine, 0, ()))
            ici = {}
            for f in CHIP_FLIPS:
                ici[f] = len(steps)
                steps.append((a, False, h * c, h, mine, f, ()))
            steps.append((a, False, h * c, h, mine, 1, ()))
            for f in CHIP_FLIPS:
                theirs = rc * ((2 * x + y) ^ (f >> 1)) + h * c
                steps.append((a, True, theirs, h, theirs, 1, (ici[f],)))
        return steps
    return plan


def _pair_reduce_plan(chip_rows):
    def plan(x, y, c):
        steps = []
        for a, rc in enumerate(chip_rows):
            h = rc // 2
            for j in range(4):
                steps.append((a, False, rc * j + h * (1 - c), h, h * j, 1, ()))
        return steps
    return plan


def _chip_scatter_plan(chip_rows):
    def plan(x, y, c):
        steps = []
        for a, rc in enumerate(chip_rows):
            h = rc // 2
            for k, f in enumerate(CHIP_FLIPS):
                steps.append((a, False, h * ((2 * x + y) ^ (f >> 1)), h, h * k, f, ()))
        return steps
    return plan


def _pair_share_plan(chip_rows):
    def plan(x, y, c):
        return [(a, True, (rc // 2) * c, rc // 2, (rc // 2) * c, 1, ()) for a, rc in enumerate(chip_rows)]
    return plan


def _tile_rows(h):
    return next(t for t in (64, 32, 16) if h % t == 0)


def _sum_pair(partial, recv, core, rc, name):
    nl, _, cols = partial.shape
    h = rc // 2
    tr = _tile_rows(h)

    def body(c_ref, p_ref, r_ref, out_ref):
        out_ref[...] = (p_ref[...] + r_ref[...]).astype(BF16)

    spec = pl.BlockSpec((1, tr, cols), lambda l, j, i, c_ref: (l, (h // tr) * j + i, 0))
    return pl.pallas_call(
        body, name=name, out_shape=jax.ShapeDtypeStruct((nl, 4 * h, cols), BF16),
        grid_spec=pltpu.PrefetchScalarGridSpec(
            num_scalar_prefetch=1, grid=(nl, 4, h // tr),
            in_specs=[pl.BlockSpec((1, tr, cols), lambda l, j, i, c_ref: (l, (rc // tr) * j + (h // tr) * c_ref[0] + i, 0)),
                      spec],
            out_specs=spec),
        compiler_params=_params(("parallel", "parallel", "parallel")),
    )(core, partial, recv)


def _sum_chips(chip_sum, recv, place, rc, name):
    nl, _, cols = chip_sum.shape
    h = rc // 2
    tr = _tile_rows(h)

    def body(s_ref, own_ref, r_ref, out_ref):
        acc = own_ref[0].astype(F32)
        for k in range(3):
            acc = acc + r_ref[0, k].astype(F32)
        out_ref[0] = acc

    return pl.pallas_call(
        body, name=name, out_shape=jax.ShapeDtypeStruct((nl, rc, cols), F32),
        grid_spec=pltpu.PrefetchScalarGridSpec(
            num_scalar_prefetch=1, grid=(nl, h // tr),
            in_specs=[pl.BlockSpec((1, tr, cols), lambda l, i, s_ref: (l, (h // tr) * s_ref[0] + i, 0)),
                      pl.BlockSpec((1, 3, tr, cols), lambda l, i, s_ref: (l, 0, i, 0))],
            out_specs=pl.BlockSpec((1, tr, cols), lambda l, i, s_ref: (l, (h // tr) * s_ref[1] + i, 0))),
        compiler_params=_params(("parallel", "parallel")),
    )(place, chip_sum, recv.reshape(nl, 3, h, cols))


def _sum_slots(parts, out_dtype, name):
    nl, n, rows, cols = parts.shape
    tr = next(t for t in (128, 64, 32, 16) if rows % t == 0 and n * t * cols * 4 <= (4 << 20))

    def body(p_ref, out_ref):
        acc = p_ref[0, 0].astype(F32)
        for k in range(1, n):
            acc = acc + p_ref[0, k].astype(F32)
        out_ref[0] = acc.astype(out_dtype)

    return pl.pallas_call(
        body, name=name, grid=(nl, rows // tr),
        in_specs=[pl.BlockSpec((1, n, tr, cols), lambda l, i: (l, 0, i, 0))],
        out_specs=pl.BlockSpec((1, tr, cols), lambda l, i: (l, i, 0)),
        out_shape=jax.ShapeDtypeStruct((nl, rows, cols), out_dtype), compiler_params=_params(("parallel", "parallel")),
    )(parts)


def _pad_cols(a, n):
    return a if n == 0 else jnp.pad(a, ((0, 0), (0, n)))


def _in_to_padded(w):
    return jnp.concatenate([_pad_cols(w[:, a:b], z) for a, b, z in IN_PIECES], axis=1)


def _in_from_padded(gp):
    pos, out = 0, {}
    for a, b, z in IN_PIECES:
        out[a] = gp[:, pos:pos + (b - a)]
        pos += (b - a) + z
    return jnp.concatenate([out[a] for a in sorted(out)], axis=1)


def _out_to_padded(w):
    z = jnp.zeros((64, w.shape[1]), w.dtype)
    return jnp.concatenate([w[0:384], w[384:704], z, w[704:1024], z], axis=0)


def _out_from_padded(gp):
    return jnp.concatenate([gp[0:384], gp[384:704], gp[768:1088]], axis=0)


def _uq_to_padded(w):
    parts = []
    for p in range(3):
        h0, h1 = 2 * p, 2 * p + 1
        parts += [w[:, 96 * h0:96 * h0 + 64], w[:, 96 * h1:96 * h1 + 64], w[:, 96 * h0 + 64:96 * h0 + 96],
                  w[:, 96 * h1 + 64:96 * h1 + 96], jnp.zeros((w.shape[0], 64), w.dtype)]
    return jnp.concatenate(parts, axis=1)


def _uq_from_padded(gp):
    parts = []
    for h in range(6):
        p, s = h // 2, h % 2
        parts += [gp[:, 256 * p + 64 * s:256 * p + 64 * s + 64], gp[:, 256 * p + 128 + 32 * s:256 * p + 160 + 32 * s]]
    return jnp.concatenate(parts, axis=1)


def _ukv_to_padded(w):
    return jnp.concatenate([w[:, 128 * h:128 * h + 64] for h in range(6)]
                           + [w[:, 128 * h + 64:128 * h + 128] for h in range(6)], axis=1)


def _ukv_from_padded(gp):
    parts = []
    for h in range(6):
        parts += [gp[:, 64 * h:64 * h + 64], gp[:, 384 + 64 * h:384 + 64 * h + 64]]
    return jnp.concatenate(parts, axis=1)


LR_ROWS = 224
SHARD_ROWS = (256, 256, LR_ROWS)


def _pack_lowrank(w_uq, w_ukv):
    flat = jnp.concatenate([w_uq.reshape(-1), w_ukv.reshape(-1)])
    return jnp.pad(flat, (0, LR_ROWS * PACK_COLS - flat.shape[0])).reshape(1, LR_ROWS, PACK_COLS)


def _unpack_lowrank(packed):
    flat = packed.reshape(-1)
    n_uq = DEPTH * A_Q_RANK * 144
    n_ukv = DEPTH * A_KV_RANK * 192
    return flat[0:n_uq].reshape(DEPTH, A_Q_RANK, 144), flat[n_uq:n_uq + n_ukv].reshape(DEPTH, A_KV_RANK, 192)


PACK_SIZES = (DEPTH * 256 * N_IN, DEPTH * 256 * D_MODEL, DEPTH * A_Q_RANK * 144, DEPTH * A_KV_RANK * 192)


def _pack(w_in, w_out, w_uq, w_ukv):
    flat = jnp.concatenate([w_in.reshape(-1), w_out.reshape(-1), w_uq.reshape(-1), w_ukv.reshape(-1)])
    flat = jnp.pad(flat, (0, PACK_ROWS * PACK_COLS - flat.shape[0]))
    return flat.reshape(PACK_ROWS, PACK_COLS)


def _unpack(packed):
    flat = packed.reshape(-1)
    o0, o1, o2, o3 = PACK_SIZES
    w_in = flat[0:o0].reshape(DEPTH, 256, N_IN)
    w_out = flat[o0:o0 + o1].reshape(DEPTH, 256, D_MODEL)
    w_uq = flat[o0 + o1:o0 + o1 + o2].reshape(DEPTH, A_Q_RANK, 144)
    w_ukv = flat[o0 + o1 + o2:o0 + o1 + o2 + o3].reshape(DEPTH, A_KV_RANK, 192)
    return w_in, w_out, w_uq, w_ukv


def _rope_tables(positions):
    inv = ROPE_THETA ** (-jnp.arange(0, A_ROPE, 2, dtype=F32) / A_ROPE)
    ang = positions.astype(F32)[..., None] * inv
    cos = jnp.tile(jnp.cos(ang), (1, 1, 4)).reshape(-1, 64)
    sin = jnp.tile(jnp.sin(ang), (1, 1, 4)).reshape(-1, 64)
    second = (jnp.arange(64) % 32) >= 16
    pad = ((0, 0), (0, 64))
    return (jnp.pad(cos, pad), jnp.pad(jnp.where(second, 0.0, -sin), pad), jnp.pad(jnp.where(second, sin, 0.0), pad))


def _rows(a, n):
    flat = a.reshape(-1)
    return jnp.pad(flat, (0, n * LANE - flat.shape[0])).reshape(n, LANE)


def _forward_backward(x, mod, tables, target, weights, small, nb, seq):
    cos, sina, sinb = tables
    saved = []
    for l in range(DEPTH):
        w, s = weights[l], small[l]
        shift = mod[l][:, None, 0:D_MODEL]
        scale = mod[l][:, None, D_MODEL:2 * D_MODEL]
        gate = mod[l][:, None, 2 * D_MODEL:]
        h, za, zg, qb, kb, vb, qc, kc, vc, zf = _ln_inproj(x, shift, scale, s["norm_g"], w["in"], seq)
        qa, ka, va = _a_up(za, s["gq"], s["gkv"], w["uq"], w["ukv"], cos, sina, sinb)
        bias = _bias_expand(s["g8"])[0:6]
        f = _forget_fwd(zf, s["fb"], nb, seq)
        frow = jnp.pad(f[:, 0:5].reshape(nb, seq, 5).transpose(0, 2, 1), ((0, 0), (0, 1), (0, 0)))
        frow = frow.reshape(nb, 6, 1, seq)
        oa, lse_a = _attn_fwd("A", qa, ka, va, None, nb, seq)
        ob, lse_b = _attn_fwd("B", qb, kb, vb, bias, nb, seq)
        oc, lse_c = _attn_fwd("C", qc, kc, vc, frow, nb, seq)
        y, xn = _gate_outproj(x, gate, oa, ob, oc, zg, w["out"], seq)
        saved.append(dict(x=x, h=h, za=za, zg=zg, zf=zf, y=y, shift=shift, scale=scale, gate=gate, bias=bias, frow=frow,
                          a=(qa, ka, va, oa, lse_a), b=(qb, kb, vb, ob, lse_b), c=(qc, kc, vc, oc, lse_c)))
        x = xn
    dx, loss8, gfinal8 = _final_loss(x, target, small[0]["final_g"])
    grads = []
    for l in reversed(range(DEPTH)):
        w, s, sv = weights[l], small[l], saved[l]
        qa, ka, va, oa, lse_a = sv["a"]
        qb, kb, vb, ob, lse_b = sv["b"]
        qc, kc, vc, oc, lse_c = sv["c"]
        doa, dob, doc, dzg, gw_out, dgate = _outproj_bwd(dx, sv["y"], sv["gate"], oa, ob, oc, sv["zg"], w["out"],
                                                          w["out_t"], nb, seq)
        dqa, dka, dva = _attn_bwd("A", qa, ka, va, oa, doa, lse_a, None, nb, seq)
        dqb, dkb, dvb, dbt = _attn_bwd("B", qb, kb, vb, ob, dob, lse_b, sv["bias"], nb, seq)
        dqc, dkc, dvc, dfr, dfq = _attn_bwd("C", qc, kc, vc, oc, doc, lse_c, sv["frow"], nb, seq)
        dg = _bias_reduce(jnp.pad(dbt, ((0, 2), (0, 0), (0, 0))))
        grb = jnp.pad(_bias_unline(dg), ((0, 0), (0, 384 - N_REL)))
        df = dfr.reshape(nb, 6, seq).transpose(0, 2, 1).reshape(nb * seq, 6) + dfq.reshape(nb * seq, 6, 64)[:, :, 0]
        dzf, gfb = _forget_bwd(jnp.pad(df, ((0, 0), (0, LANE - 6))), sv["zf"], s["fb"], nb, seq)
        dza, gw_uq, gw_ukv, ggq, ggkv = _a_up_bwd(dqa, dka, dva, sv["za"], s["gq"], s["gkv"], w["uq_t"], w["ukv_t"],
                                                  cos, sina, sinb)
        dz = (dza, dzg, dqb, dkb, dvb, dqc, dkc, dvc, dzf)
        dx, dshift, dscale, gnorm = _inproj_bwd_dx(dz, dx, sv["x"], sv["shift"], sv["scale"], s["norm_g"], w["in_t"],
                                                   nb, seq)
        gw_in = jnp.concatenate([_inproj_bwd_dw(sv["h"], dz[0:2], "inproj_bwd_dw0"),
                                 _inproj_bwd_dw(sv["h"], dz[2:], "inproj_bwd_dw1")], axis=1)
        dmod = jnp.concatenate([dshift[:, 0], dscale[:, 0], dgate[:, 0]], axis=1)
        grads.append(dict(w_in=gw_in, w_out=gw_out, w_uq=gw_uq, w_ukv=gw_ukv, dmod=dmod, norm_g=gnorm[0], gq=ggq[0],
                          gkv=ggkv[0], rb8=grb, fb=gfb[0]))
    grads.reverse()
    return loss8[0, 0], dx, grads, gfinal8[0]


def _layer_weights(w_in, w_out, w_uq, w_ukv):
    wi, wo, wq, wkv = _in_to_padded(w_in), _out_to_padded(w_out), _uq_to_padded(w_uq), _ukv_to_padded(w_ukv)
    return {"in": wi, "in_t": wi.T, "out": wo, "out_t": wo.T, "uq": wq, "uq_t": wq.T, "ukv": wkv, "ukv_t": wkv.T}


def _layer_small(norm_g, gq, gkv, rel_bias, forget_b, final_g):
    fb = jnp.pad(forget_b, (0, LANE - 5)).reshape(1, LANE)
    return dict(norm_g=norm_g.reshape(1, -1), gq=gq.reshape(1, -1), gkv=gkv.reshape(1, -1), g8=_bias_line(rel_bias), fb=fb,
                final_g=final_g.reshape(1, -1))


def _small_payload(per_layer, final_g, loss):
    def stack(key):
        return jnp.stack([p[key] for p in per_layer])

    def rows(a, rng):
        return _rows(a, rng[1] - rng[0])

    dmod = stack("dmod") if "dmod" in per_layer[0] else jnp.zeros((LANE,), F32)
    parts = [rows(dmod, PAY_DMOD), rows(stack("norm_g"), PAY_NORM), rows(stack("gq"), PAY_GQ),
             rows(stack("gkv"), PAY_GKV), rows(stack("rb8"), PAY_RB), rows(stack("fb"), PAY_FB),
             rows(final_g, PAY_FINAL), rows(loss, PAY_LOSS)]
    return jnp.concatenate(parts, axis=0)


def _payload_split(pay):
    def take(rng, shape):
        n = 1
        for d in shape:
            n *= d
        return pay[rng[0]:rng[1]].reshape(-1)[0:n].reshape(shape)

    norm_g = take(PAY_NORM, (DEPTH, D_MODEL))
    gq = take(PAY_GQ, (DEPTH, A_Q_RANK))
    gkv = take(PAY_GKV, (DEPTH, A_KV_RANK))
    rb = take(PAY_RB, (DEPTH, 8, 384))[:, 0:5, 0:N_REL]
    fb = take(PAY_FB, (DEPTH, LANE))[:, 0:5]
    final_g = take(PAY_FINAL, (D_MODEL,))
    return norm_g, gq, gkv, rb, fb, final_g


def kernel(x, c, positions, w_ada, b_ada, norm_g, w_in, a_q_norm_g, a_w_uq, a_kv_norm_g, a_w_ukv, b_rel_bias, c_forget_b, w_out, final_g, loss_target, m_w_ada, m_b_ada, m_norm_g, m_w_in, m_a_q_norm_g, m_a_w_uq, m_a_kv_norm_g, m_a_w_ukv, m_b_rel_bias, m_c_forget_b, m_w_out, m_final_g, v_w_ada, v_b_ada, v_norm_g, v_w_in, v_a_q_norm_g, v_a_w_uq, v_a_kv_norm_g, v_a_w_ukv, v_b_rel_bias, v_c_forget_b, v_w_out, v_final_g):
    nb, seq, _ = x.shape
    ix, iy, ic = lax.axis_index("x"), lax.axis_index("y"), lax.axis_index("c")
    chip = 2 * ix + iy
    me = 2 * chip + ic

    full_in, full_out, full_lr = _transfer(
        "gather_weights", [w_in.astype(BF16), w_out.astype(BF16), _pack_lowrank(a_w_uq, a_w_ukv).astype(BF16)],
        [((DEPTH, D_MODEL, N_IN), BF16), ((DEPTH, D_MODEL, D_MODEL), BF16), ((1, 4 * LR_ROWS, PACK_COLS), BF16)],
        _gather_plan(SHARD_ROWS))
    lowrank = [_unpack_lowrank(full_lr[0, LR_ROWS * j:LR_ROWS * (j + 1)]) for j in range(4)]
    full_uq = jnp.concatenate([s[0] for s in lowrank], axis=2)
    full_ukv = jnp.concatenate([s[1] for s in lowrank], axis=2)
    weights = [_layer_weights(full_in[l], full_out[l], full_uq[l], full_ukv[l]) for l in range(DEPTH)]
    small = [_layer_small(norm_g[l], a_q_norm_g[l], a_kv_norm_g[l], b_rel_bias[l], c_forget_b[l], final_g)
             for l in range(DEPTH)]

    c_all = _exchange(c[None], ALL_FLIPS, "gather", "gather_c").reshape(N_DEV * nb, D_MODEL)
    cols = w_ada.shape[2]
    b_cols = lax.dynamic_slice_in_dim(b_ada, chip * cols, cols, axis=1)[:, None, :]
    mod_cols = _ada_fwd(c_all, w_ada, b_cols)
    mod_g = _exchange(mod_cols.reshape(1, DEPTH * N_DEV * nb, cols), ALL_FLIPS, "gather", "gather_mod")
    mod_all = jnp.concatenate([mod_g[2 * j].reshape(DEPTH, N_DEV * nb, cols) for j in range(4)], axis=2)
    mod = lax.dynamic_slice_in_dim(mod_all, me * nb, nb, axis=1)

    tables = _rope_tables(positions)
    loss_part, dx, grads, gfinal = _forward_backward(
        x.reshape(nb * seq, D_MODEL), mod, tables, loss_target.reshape(nb * seq, D_MODEL), weights, small, nb, seq)

    pay = _small_payload(grads, gfinal, loss_part)
    pay_all = _exchange(pay[None], ALL_FLIPS, "gather", "gather_small")
    tot = _sum_blocks(pay_all, "sum_small")
    loss = tot[PAY_LOSS[0], 0]
    dmod_all = pay_all[:, PAY_DMOD[0]:PAY_DMOD[1]].reshape(N_DEV, -1)[:, 0:DEPTH * nb * 3 * D_MODEL]
    dmod_all = dmod_all.reshape(N_DEV, DEPTH, nb, 3 * D_MODEL).transpose(1, 0, 2, 3)
    dmod_all = dmod_all.reshape(DEPTH, N_DEV * nb, 3 * D_MODEL)
    my_cols = lax.dynamic_slice_in_dim(dmod_all, chip * cols, cols, axis=2)
    g_w_ada, g_b_ada = _ada_bwd(c_all, my_cols, dmod_all)
    g_b_ada = g_b_ada[:, 0]

    g_uq = jnp.stack([_uq_from_padded(g["w_uq"]) for g in grads])
    g_ukv = jnp.stack([_ukv_from_padded(g["w_ukv"]) for g in grads])
    g_lr = jnp.concatenate([_pack_lowrank(g_uq[:, :, 144 * j:144 * (j + 1)], g_ukv[:, :, 192 * j:192 * (j + 1)])
                            for j in range(4)], axis=1)
    partials = [jnp.stack([g["w_in"] for g in grads]), jnp.stack([_out_from_padded(g["w_out"]) for g in grads]), g_lr]
    shapes = [(p.shape[0], p.shape[2]) for p in partials]
    halves = [r // 2 for r in SHARD_ROWS]
    core_s = jnp.reshape(ic, (1,)).astype(jnp.int32)
    place_s = jnp.stack([chip, ic]).astype(jnp.int32)
    from_pair = _transfer("pair_reduce", partials, [((nl, 4 * h, nc), F32) for (nl, nc), h in zip(shapes, halves)],
                          _pair_reduce_plan(SHARD_ROWS))
    chip_sums = [_sum_pair(p, r, core_s, rc, "sum_pair%d" % i)
                 for i, (p, r, rc) in enumerate(zip(partials, from_pair, SHARD_ROWS))]
    from_chips = _transfer("chip_scatter", chip_sums, [((nl, 3 * h, nc), BF16) for (nl, nc), h in zip(shapes, halves)],
                           _chip_scatter_plan(SHARD_ROWS))
    reduced = [_sum_chips(s, r, place_s, rc, "sum_chips%d" % i)
               for i, (s, r, rc) in enumerate(zip(chip_sums, from_chips, SHARD_ROWS))]
    g_in_p, g_out_sh, g_lr_sh = _transfer("pair_share", reduced, [(r.shape, F32) for r in reduced],
                                          _pair_share_plan(SHARD_ROWS), in_place=True)
    g_uq_sh, g_ukv_sh = _unpack_lowrank(g_lr_sh[0])
    gw = (jnp.stack([_in_from_padded(g_in_p[l]) for l in range(DEPTH)]), g_out_sh, g_uq_sh, g_ukv_sh)

    def adam_nd(w, g, m, v, name):
        flat = (-1, w.shape[-1])
        return tuple(a.reshape(w.shape) for a in _adamw(w.reshape(flat), g.reshape(flat), m.reshape(flat),
                                                        v.reshape(flat), name))

    upd = [adam_nd(w_in, gw[0], m_w_in, v_w_in, "adamw_in"), adam_nd(w_out, gw[1], m_w_out, v_w_out, "adamw_out"),
           adam_nd(a_w_uq, gw[2], m_a_w_uq, v_a_w_uq, "adamw_uq"), adam_nd(a_w_ukv, gw[3], m_a_w_ukv, v_a_w_ukv, "adamw_ukv")]
    dw, mw, vw = (tuple(u[i] for u in upd) for i in range(3))
    ada_shape = w_ada.shape
    flat_ada = (DEPTH * D_MODEL, cols)
    d_ada, m_ada, v_ada = (a.reshape(ada_shape) for a in _adamw(
        w_ada.reshape(flat_ada), g_w_ada.reshape(flat_ada), m_w_ada.reshape(flat_ada), v_w_ada.reshape(flat_ada),
        "adamw_ada"))
    d_b, m_b, v_b = (a.reshape(DEPTH, 3 * D_MODEL) for a in _adamw(
        _rows(b_ada, 48), _rows(g_b_ada, 48), _rows(m_b_ada, 48), _rows(v_b_ada, 48), "adamw_b_ada"))

    def small_rows(ng, gq, gkv, rb, fb, fg):
        per_layer = [dict(norm_g=ng[l], gq=gq[l], gkv=gkv[l], rb8=jnp.pad(rb[l], ((0, 3), (0, 384 - N_REL))),
                          fb=jnp.pad(fb[l], (0, LANE - 5))) for l in range(DEPTH)]
        return _small_payload(per_layer, fg, jnp.zeros((), F32))

    w_s = small_rows(norm_g, a_q_norm_g, a_kv_norm_g, b_rel_bias, c_forget_b, final_g)
    m_s = small_rows(m_norm_g, m_a_q_norm_g, m_a_kv_norm_g, m_b_rel_bias, m_c_forget_b, m_final_g)
    v_s = small_rows(v_norm_g, v_a_q_norm_g, v_a_kv_norm_g, v_b_rel_bias, v_c_forget_b, v_final_g)
    d_s, mo_s, vo_s = _adamw(w_s, tot, m_s, v_s, "adamw_small")
    gs = _payload_split(tot)
    ds = _payload_split(d_s)
    ms = _payload_split(mo_s)
    vs = _payload_split(vo_s)

    def ordered(ada, b, sm, big):
        ng, gq, gkv, rb, fb, fg = sm
        b_in, b_out, b_uq, b_ukv = big
        return (ada, b, ng, b_in, gq, b_uq, gkv, b_ukv, rb, fb, b_out, fg)

    return (loss, dx.reshape(nb, seq, D_MODEL), *ordered(g_w_ada, g_b_ada, gs, gw), *ordered(d_ada, d_b, ds, dw),
            *ordered(m_ada, m_b, ms, mw), *ordered(v_ada, v_b, vs, vw))
```

```python
import functools

import jax
import jax.numpy as jnp
from jax import lax
from jax.experimental import pallas as pl
from jax.experimental.pallas import tpu as pltpu

F32 = jnp.float32
BF16 = jnp.bfloat16

D_MODEL = 1024
DEPTH = 2
EPS = 1e-6
NEG = -1e30
LOG2E = 1.4426950408889634
ROPE_THETA = 10000.0
A_ROPE = 32
A_Q_RANK = 384
A_KV_RANK = 256
REL_CLIP = 128
N_REL = 2 * REL_CLIP + 1
N_IN = 3621

ADAM_LR = 0.001
ADAM_B1 = 0.9
ADAM_B2 = 0.999
ADAM_EPS = 1e-08
ADAM_WD = 0.01
ADAM_STEP = 10

LANE = 128
VMEM_LIMIT = 56 * 1024 * 1024

NP_IN = 4352
Z_A = (0, 768)
Z_G = (768, 1920)
Z_QKV = tuple((1920 + 384 * i, 1920 + 384 * (i + 1)) for i in range(6))
Z_F = (4224, 4352)
IN_PIECES = ((0, 672, 96), (672, 1056, 0), (2016, 2336, 64), (3301, 3621, 64), (1056, 1376, 64), (1376, 1696, 64),
             (1696, 2016, 64), (2336, 2656, 64), (2656, 2976, 64), (2976, 3296, 64), (3296, 3301, 123))
D_CAT = 1152

TM = 512
T_CAUSAL = 256
T_BAND = 128
BAND_TILES = 5
N_DEV = 8

PAY_DMOD = (0, 96)
PAY_NORM = (96, 112)
PAY_GQ = (112, 120)
PAY_GKV = (120, 128)
PAY_RB = (128, 176)
PAY_FB = (176, 184)
PAY_FINAL = (184, 192)
PAY_LOSS = (192, 200)
PAY_ROWS = 200

PACK_COLS = 1024
PACK_ROWS = 2560
HALF_ROWS = PACK_ROWS // 2


def _params(sem=None):
    return pltpu.CompilerParams(dimension_semantics=sem, vmem_limit_bytes=VMEM_LIMIT)


def _lane_iota(shape):
    return lax.broadcasted_iota(jnp.int32, shape, len(shape) - 1)


def _silu(u):
    return u * jax.nn.sigmoid(u)


def _dsilu(u):
    s = jax.nn.sigmoid(u)
    return s * (1.0 + u * (1.0 - s))


def _rms(x, g):
    r = lax.rsqrt(jnp.mean(x * x, axis=-1, keepdims=True) + EPS)
    xh = x * r
    return xh * g, xh, r


def _rms_bwd(dy, xh, r, g):
    dxh = dy * g
    return r * (dxh - xh * jnp.mean(dxh * xh, axis=-1, keepdims=True))


def _rope(x, cos, sina, sinb):
    return x * cos + pltpu.roll(x, 16, 1) * sinb + pltpu.roll(x, LANE - 16, 1) * sina


def _rope_t(dy, cos, sina, sinb):
    return dy * cos + pltpu.roll(dy * sinb, LANE - 16, 1) + pltpu.roll(dy * sina, 16, 1)


def _split3(x):
    hi = x.astype(BF16)
    r1 = x - hi.astype(F32)
    mid = r1.astype(BF16)
    lo = (r1 - mid.astype(F32)).astype(BF16)
    return hi, mid, lo


def _dot(a, b):
    return jnp.dot(a, b, preferred_element_type=F32)


def _dot_nt(a, b):
    return lax.dot_general(a, b, (((1,), (1,)), ((), ())), preferred_element_type=F32)


def _dot_tn(a, b):
    return lax.dot_general(a, b, (((0,), (0,)), ((), ())), preferred_element_type=F32)


def _row_spec(cols):
    return pl.BlockSpec((TM, cols), lambda i: (i, 0))


def _full_spec(shape):
    return pl.BlockSpec(shape, lambda i: (0,) * len(shape))


def _ex_spec(tiles_per_ex):
    return pl.BlockSpec((1, 1, D_MODEL), lambda i: (i // tiles_per_ex, 0, 0))


def _ln_inproj(x, shift, scale, g, w_in_p, seq):
    t = x.shape[0]

    def body(x_ref, sh_ref, sc_ref, g_ref, w_ref, h_ref, za_ref, zg_ref, q0, q1, q2, q3, q4, q5, zf_ref):
        n, _, _ = _rms(x_ref[...], g_ref[...])
        h = (n * (1.0 + sc_ref[0]) + sh_ref[0]).astype(BF16)
        h_ref[...] = h
        za_ref[...] = _dot(h, w_ref[:, Z_A[0]:Z_A[1]])
        zg_ref[...] = _dot(h, w_ref[:, Z_G[0]:Z_G[1]])
        for ref, (c0, c1) in zip((q0, q1, q2, q3, q4, q5), Z_QKV):
            ref[...] = _dot(h, w_ref[:, c0:c1]).astype(BF16)
        zf_ref[...] = _dot(h, w_ref[:, Z_F[0]:Z_F[1]])

    tpe = seq // TM
    shapes = [jax.ShapeDtypeStruct((t, D_MODEL), BF16), jax.ShapeDtypeStruct((t, 768), F32),
              jax.ShapeDtypeStruct((t, D_CAT), F32)]
    shapes += [jax.ShapeDtypeStruct((t, 384), BF16)] * 6 + [jax.ShapeDtypeStruct((t, LANE), F32)]
    return pl.pallas_call(
        body, name="ln_inproj", grid=(t // TM,),
        in_specs=[_row_spec(D_MODEL), _ex_spec(tpe), _ex_spec(tpe), _full_spec((1, D_MODEL)),
                  _full_spec((D_MODEL, NP_IN))],
        out_specs=[_row_spec(D_MODEL), _row_spec(768), _row_spec(D_CAT)] + [_row_spec(384)] * 6 + [_row_spec(LANE)],
        out_shape=shapes, compiler_params=_params(("parallel",)),
    )(x, shift, scale, g, w_in_p)


def _a_up(za, gq, gkv, w_uq_p, w_ukv_p, cos, sina, sinb):
    t = za.shape[0]

    def body(za_ref, gq_ref, gkv_ref, wq_ref, wkv_ref, cos_ref, sa_ref, sb_ref, q_ref, k_ref, v_ref):
        cos_t, sa, sb = cos_ref[...], sa_ref[...], sb_ref[...]
        cqn, _, _ = _rms(za_ref[:, 0:384], gq_ref[...])
        q = _dot(cqn.astype(BF16), wq_ref[...])
        ckvn, _, _ = _rms(za_ref[:, 384:640], gkv_ref[...])
        kv = _dot(ckvn.astype(BF16), wkv_ref[...])
        kpe = za_ref[:, 640:768]
        kpe = _rope(kpe + pltpu.roll(kpe, 32, 1), cos_t, sa, sb).astype(BF16)
        for p in range(3):
            q_ref[:, 256 * p:256 * p + 128] = q[:, 256 * p:256 * p + 128].astype(BF16)
            q_ref[:, 256 * p + 128:256 * p + 256] = _rope(q[:, 256 * p + 128:256 * p + 256], cos_t, sa, sb).astype(BF16)
            k_ref[:, 256 * p:256 * p + 128] = kv[:, 128 * p:128 * p + 128].astype(BF16)
            k_ref[:, 256 * p + 128:256 * p + 256] = kpe
        v_ref[...] = kv[:, 384:768].astype(BF16)

    return pl.pallas_call(
        body, name="a_up", grid=(t // TM,),
        in_specs=[_row_spec(768), _full_spec((1, 384)), _full_spec((1, 256)), _full_spec((384, 768)),
                  _full_spec((256, 768)), _row_spec(LANE), _row_spec(LANE), _row_spec(LANE)],
        out_specs=[_row_spec(768), _row_spec(768), _row_spec(384)],
        out_shape=[jax.ShapeDtypeStruct((t, 768), BF16), jax.ShapeDtypeStruct((t, 768), BF16),
                   jax.ShapeDtypeStruct((t, 384), BF16)],
        compiler_params=_params(("parallel",)),
    )(za, gq, gkv, w_uq_p, w_ukv_p, cos, sina, sinb)


def _tri(n, upper):
    r = lax.broadcasted_iota(jnp.int32, (n, n), 0)
    c = lax.broadcasted_iota(jnp.int32, (n, n), 1)
    return jnp.where((c >= r) if upper else (c <= r), 1.0, 0.0).astype(BF16)


def _forget_fwd(zf, fb, nb, seq):
    blk = 256

    def body(zf_ref, fb_ref, f_ref):
        tri = _tri(blk, False)
        live = _lane_iota((blk, LANE)) < 5
        carry = jnp.zeros((1, LANE), F32)
        for i in range(seq // blk):
            u = zf_ref[i * blk:(i + 1) * blk, :] + fb_ref[...]
            lf = jnp.where(live, jnp.minimum(u, 0.0) - jnp.log(1.0 + jnp.exp(-jnp.abs(u))), 0.0)
            hi, mid, lo = _split3(lf)
            f_ref[i * blk:(i + 1) * blk, :] = (_dot(tri, hi) + _dot(tri, mid) + _dot(tri, lo) + carry) * LOG2E
            carry = carry + jnp.sum(lf, axis=0, keepdims=True)

    return pl.pallas_call(
        body, name="forget_fwd", grid=(nb,),
        in_specs=[pl.BlockSpec((seq, LANE), lambda b: (b, 0)), pl.BlockSpec((1, LANE), lambda b: (0, 0))],
        out_specs=pl.BlockSpec((seq, LANE), lambda b: (b, 0)),
        out_shape=jax.ShapeDtypeStruct((nb * seq, LANE), F32), compiler_params=_params(("parallel",)),
    )(zf, fb)


def _forget_bwd(dfq, dfk, zf, fb, nb, seq):
    blk = 256

    def body(dfq_ref, dfk_ref, zf_ref, fb_ref, dz_ref, gb_ref):
        @pl.when(pl.program_id(0) == 0)
        def _():
            gb_ref[...] = jnp.zeros_like(gb_ref)

        tri = _tri(blk, True)
        lane = _lane_iota((blk, LANE))
        wide = _lane_iota((blk, 384))
        live = lane < 5
        carry = jnp.zeros((1, LANE), F32)
        gsum = jnp.zeros((1, LANE), F32)
        for i in reversed(range(seq // blk)):
            d = dfk_ref[i * blk:(i + 1) * blk, :]
            dq = dfq_ref[i * blk:(i + 1) * blk, :]
            for hd in range(5):
                col = jnp.sum(jnp.where(wide == 64 * hd, dq, 0.0), axis=-1, keepdims=True)
                d = d + jnp.where(lane == hd, col, 0.0)
            hi, mid, lo = _split3(d)
            dlf = _dot(tri, hi) + _dot(tri, mid) + _dot(tri, lo) + carry
            carry = carry + jnp.sum(d, axis=0, keepdims=True)
            u = zf_ref[i * blk:(i + 1) * blk, :] + fb_ref[...]
            du = jnp.where(live, dlf * jax.nn.sigmoid(-u), 0.0)
            dz_ref[i * blk:(i + 1) * blk, :] = du.astype(BF16)
            gsum = gsum + jnp.sum(du, axis=0, keepdims=True)
        gb_ref[...] += jnp.broadcast_to(gsum, gb_ref.shape)

    return pl.pallas_call(
        body, name="forget_bwd", grid=(nb,),
        in_specs=[pl.BlockSpec((seq, 384), lambda b: (b, 0)), pl.BlockSpec((seq, LANE), lambda b: (b, 0)),
                  pl.BlockSpec((seq, LANE), lambda b: (b, 0)), pl.BlockSpec((1, LANE), lambda b: (0, 0))],
        out_specs=[pl.BlockSpec((seq, LANE), lambda b: (b, 0)), pl.BlockSpec((8, LANE), lambda b: (0, 0))],
        out_shape=[jax.ShapeDtypeStruct((nb * seq, LANE), BF16), jax.ShapeDtypeStruct((8, LANE), F32)],
        compiler_params=_params(("arbitrary",)),
    )(dfq, dfk, zf, fb)


def _gate_outproj(x, gate, oa, ob, oc, zg, w_out_p, seq):
    t = x.shape[0]

    def body(x_ref, gate_ref, oa_ref, ob_ref, oc_ref, zg_ref, w_ref, y_ref, xn_ref):
        y = jnp.zeros((TM, D_MODEL), F32)
        for i, o_ref in enumerate((oa_ref, ob_ref, oc_ref)):
            cat = (o_ref[...] * _silu(zg_ref[:, 384 * i:384 * (i + 1)])).astype(BF16)
            y = y + _dot(cat, w_ref[384 * i:384 * (i + 1), :])
        y_ref[...] = y
        xn_ref[...] = x_ref[...] + gate_ref[0] * y

    return pl.pallas_call(
        body, name="gate_outproj", grid=(t // TM,),
        in_specs=[_row_spec(D_MODEL), _ex_spec(seq // TM), _row_spec(384), _row_spec(384), _row_spec(384),
                  _row_spec(D_CAT), _full_spec((D_CAT, D_MODEL))],
        out_specs=[_row_spec(D_MODEL), _row_spec(D_MODEL)],
        out_shape=[jax.ShapeDtypeStruct((t, D_MODEL), F32)] * 2, compiler_params=_params(("parallel",)),
    )(x, gate, oa, ob, oc, zg, w_out_p)


def _final_loss(x, target, g):
    t = x.shape[0]

    def body(x_ref, t_ref, g_ref, dx_ref, loss_ref, gg_ref):
        @pl.when(pl.program_id(0) == 0)
        def _():
            loss_ref[...] = jnp.zeros_like(loss_ref)
            gg_ref[...] = jnp.zeros_like(gg_ref)

        gv = g_ref[...]
        out, xh, r = _rms(x_ref[...], gv)
        err = out - t_ref[...]
        loss_ref[...] += 0.5 * jnp.sum(jnp.mean(err * err, axis=-1, keepdims=True), axis=0, keepdims=True)
        dout = err / D_MODEL
        gg_ref[...] += jnp.broadcast_to(jnp.sum(dout * xh, axis=0, keepdims=True), gg_ref.shape)
        dx_ref[...] = _rms_bwd(dout, xh, r, gv)

    return pl.pallas_call(
        body, name="final_loss", grid=(t // TM,),
        in_specs=[_row_spec(D_MODEL), _row_spec(D_MODEL), _full_spec((1, D_MODEL))],
        out_specs=[_row_spec(D_MODEL), _full_spec((8, LANE)), _full_spec((8, D_MODEL))],
        out_shape=[jax.ShapeDtypeStruct((t, D_MODEL), F32), jax.ShapeDtypeStruct((8, LANE), F32),
                   jax.ShapeDtypeStruct((8, D_MODEL), F32)],
        compiler_params=_params(("arbitrary",)),
    )(x, target, g)


def _head_masks(kind, rows, dq, h):
    lq = _lane_iota((rows, dq))
    lv = _lane_iota((rows, LANE))
    mq = (lq >= 64 * h) & (lq < 64 * h + 64)
    if kind == "A":
        mq = mq | ((lq >= 128 + 32 * h) & (lq < 160 + 32 * h))
    return mq, (lv >= 64 * h) & (lv < 64 * h + 64)


def _tile_mask(kind, tile, m=None):
    row = lax.broadcasted_iota(jnp.int32, (tile, tile), 0)
    col = lax.broadcasted_iota(jnp.int32, (tile, tile), 1)
    if kind == "A":
        return (col >> 6) <= (row >> 6)
    if kind == "C":
        return col <= row
    first = (m == 0) & (row >= 64) & (col < 64)
    last = (m == BAND_TILES - 1) & (row < 64) & (col >= 64)
    return jnp.logical_not(first | last)


def _attn_scale(kind):
    return 96.0 ** -0.5 if kind == "A" else 0.125


def _attn_fwd(kind, q, k, v, aux, nb, seq):
    dq = q.shape[1] // 3
    tile = T_BAND if kind == "B" else T_CAUSAL
    nq = seq // tile
    scale = _attn_scale(kind)

    def body(*refs):
        if kind == "A":
            q_ref, k_ref, v_ref, o_ref, lse_ref = refs
            aux_ref = None
        else:
            q_ref, k_ref, v_ref, aux_ref, o_ref, lse_ref = refs
        qi = pl.program_id(2)
        q2 = q_ref[...]
        res = []
        for h in range(2):
            mq, _ = _head_masks(kind, tile, dq, h)
            qh = jnp.where(mq, q2, jnp.zeros_like(q2))

            def step(kj, carry, m=None, diag=False):
                mx, l, acc = carry
                ks = pl.ds(pl.multiple_of(kj * tile, tile), tile)
                kt = k_ref[ks, :]
                vt = v_ref[ks, :]
                s = _dot_nt(qh, kt) * scale
                if kind == "B":
                    s = jnp.where(_tile_mask("B", tile, m), s + aux_ref[h, m], NEG)
                if kind == "C":
                    s = s - aux_ref[0, h, pl.ds(kj, 1), :]
                if diag:
                    s = jnp.where(_tile_mask(kind, tile), s, NEG)
                mn = jnp.maximum(mx, jnp.max(s, axis=-1, keepdims=True))
                alpha = jnp.exp(mx - mn)
                p = jnp.exp(s - mn)
                l = alpha * l + jnp.sum(p, axis=-1, keepdims=True)
                acc = alpha * acc + _dot(p.astype(BF16), vt)
                return mn, l, acc

            init = (jnp.full((tile, 1), NEG, F32), jnp.zeros((tile, 1), F32), jnp.zeros((tile, LANE), F32))
            if kind == "B":
                m0 = jnp.maximum(BAND_TILES - 1 - qi, 0)
                mx, l, acc = lax.fori_loop(m0, BAND_TILES, lambda m, c: step(qi - (BAND_TILES - 1) + m, c, m=m), init)
            else:
                carry = lax.fori_loop(0, qi, lambda kj, c: step(kj, c), init)
                mx, l, acc = step(qi, carry, diag=True)
            res.append((acc / l, mx + jnp.log(l)))
        first = _lane_iota((tile, LANE)) < 64
        o_ref[...] = jnp.where(first, res[0][0], res[1][0])
        lse_ref[...] = jnp.where(first, res[0][1], res[1][1])

    in_specs = [pl.BlockSpec((tile, dq), lambda b, p, i: (b * nq + i, p)),
                pl.BlockSpec((seq, dq), lambda b, p, i: (b, p)),
                pl.BlockSpec((seq, LANE), lambda b, p, i: (b, p))]
    args = [q, k, v]
    if kind == "B":
        in_specs.append(pl.BlockSpec((2, BAND_TILES, tile, tile), lambda b, p, i: (p, 0, 0, 0)))
        args.append(aux)
    if kind == "C":
        in_specs.append(pl.BlockSpec((1, 2, nq, tile), lambda b, p, i: (b, p, 0, 0)))
        args.append(aux)
    out_spec = pl.BlockSpec((tile, LANE), lambda b, p, i: (b * nq + i, p))
    return pl.pallas_call(
        body, name="attn_fwd_" + kind, grid=(nb, 3, nq), in_specs=in_specs, out_specs=[out_spec, out_spec],
        out_shape=[jax.ShapeDtypeStruct((nb * seq, 384), F32)] * 2,
        compiler_params=_params(("parallel", "parallel", "parallel")),
    )(*args)


def _attn_bwd(kind, q, k, v, o, do, lse, aux, nb, seq):
    dq = q.shape[1] // 3
    tile = T_BAND if kind == "B" else T_CAUSAL
    nq = seq // tile
    scale = _attn_scale(kind)
    dqk_dtype = F32 if kind == "A" else BF16

    def body(*refs):
        dfr_ref = dfq_ref = dbt_ref = aux_ref = None
        if kind == "A":
            q_ref, k_ref, v_ref, o_ref, do_ref, lse_ref, dq_ref, dk_ref, dv_ref, dk_acc, dv_acc = refs
        elif kind == "B":
            q_ref, k_ref, v_ref, o_ref, do_ref, lse_ref, aux_ref, dq_ref, dk_ref, dv_ref, dbt_ref, dk_acc, dv_acc = refs
        else:
            (q_ref, k_ref, v_ref, o_ref, do_ref, lse_ref, aux_ref, dq_ref, dk_ref, dv_ref, dfr_ref, dfq_ref,
             dk_acc, dv_acc) = refs
        dk_acc[...] = jnp.zeros_like(dk_acc)
        dv_acc[...] = jnp.zeros_like(dv_acc)
        if kind == "C":
            dfr_ref[...] = jnp.zeros_like(dfr_ref)
        if kind == "B":
            @pl.when(pl.program_id(1) == 0)
            def _():
                dbt_ref[...] = jnp.zeros_like(dbt_ref)

        def q_step(qi, _):
            qs = pl.ds(pl.multiple_of(qi * tile, tile), tile)
            q2 = q_ref[qs, :]
            do2 = do_ref[qs, :]
            o2 = o_ref[qs, :]
            lse2 = lse_ref[qs, :]
            dq_tot = jnp.zeros((tile, dq), F32)
            row_sums = []
            for h in range(2):
                mq, mv = _head_masks(kind, tile, dq, h)
                qh = jnp.where(mq, q2, jnp.zeros_like(q2))
                doh = jnp.where(mv, do2, 0.0)
                dob = doh.astype(BF16)
                delta = jnp.sum(doh * o2, axis=-1, keepdims=True)
                lseh = jnp.max(jnp.where(mv, lse2, NEG), axis=-1, keepdims=True)

                def step(kj, carry, m=None, diag=False):
                    dq_t, rs = carry
                    ks = pl.ds(pl.multiple_of(kj * tile, tile), tile)
                    kt = k_ref[ks, :]
                    vt = v_ref[ks, :]
                    s = _dot_nt(qh, kt) * scale
                    if kind == "B":
                        s = jnp.where(_tile_mask("B", tile, m), s + aux_ref[h, m], NEG)
                    if kind == "C":
                        s = s - aux_ref[0, h, pl.ds(kj, 1), :]
                    if diag:
                        s = jnp.where(_tile_mask(kind, tile), s, NEG)
                    p = jnp.exp(s - lseh)
                    ds = p * (_dot_nt(dob, vt) - delta)
                    if kind == "B":
                        dbt_ref[h, m] += ds
                    if kind == "C":
                        dfr_ref[0, h, pl.ds(kj, 1), :] -= jnp.sum(ds, axis=0, keepdims=True)
                        rs = rs + jnp.sum(ds, axis=-1, keepdims=True)
                    dss = (ds * scale).astype(BF16)
                    dv_acc[ks, :] += _dot_tn(p.astype(BF16), dob)
                    dk_acc[ks, :] += _dot_tn(dss, qh)
                    return dq_t + jnp.where(mq, _dot(dss, kt), 0.0), rs

                zero = (jnp.zeros((tile, dq), F32), jnp.zeros((tile, 1), F32))
                if kind == "B":
                    m0 = jnp.maximum(BAND_TILES - 1 - qi, 0)
                    dq_h, rs_h = lax.fori_loop(m0, BAND_TILES, lambda m, c: step(qi - (BAND_TILES - 1) + m, c, m=m), zero)
                else:
                    carry = lax.fori_loop(0, qi, lambda kj, c: step(kj, c), zero)
                    dq_h, rs_h = step(qi, carry, diag=True)
                dq_tot = dq_tot + dq_h
                row_sums.append(rs_h)
            dq_ref[qs, :] = dq_tot.astype(dqk_dtype)
            if kind == "C":
                dfq_ref[qs, :] = jnp.where(_lane_iota((tile, LANE)) < 64, row_sums[0], row_sums[1])
            return 0

        lax.fori_loop(0, nq, q_step, 0)
        dk_ref[...] = dk_acc[...].astype(dqk_dtype)
        dv_ref[...] = dv_acc[...].astype(BF16)

    def seq_spec(cols):
        return pl.BlockSpec((seq, cols), lambda p, b: (b, p))

    in_specs = [seq_spec(dq), seq_spec(dq), seq_spec(LANE), seq_spec(LANE), seq_spec(LANE), seq_spec(LANE)]
    args = [q, k, v, o, do, lse]
    out_specs = [seq_spec(dq), seq_spec(dq), seq_spec(LANE)]
    out_shape = [jax.ShapeDtypeStruct((nb * seq, 3 * dq), dqk_dtype)] * 2 + [jax.ShapeDtypeStruct((nb * seq, 384), BF16)]
    if kind == "B":
        spec = pl.BlockSpec((2, BAND_TILES, tile, tile), lambda p, b: (p, 0, 0, 0))
        in_specs.append(spec)
        args.append(aux)
        out_specs.append(spec)
        out_shape.append(jax.ShapeDtypeStruct((6, BAND_TILES, tile, tile), F32))
    if kind == "C":
        spec = pl.BlockSpec((1, 2, nq, tile), lambda p, b: (b, p, 0, 0))
        in_specs.append(spec)
        args.append(aux)
        out_specs += [spec, seq_spec(LANE)]
        out_shape += [jax.ShapeDtypeStruct((nb, 6, nq, tile), F32), jax.ShapeDtypeStruct((nb * seq, 384), F32)]
    return pl.pallas_call(
        body, name="attn_bwd_" + kind, grid=(3, nb), in_specs=in_specs, out_specs=out_specs, out_shape=out_shape,
        scratch_shapes=[pltpu.VMEM((seq, dq), F32), pltpu.VMEM((seq, LANE), F32)],
        compiler_params=_params(("arbitrary", "arbitrary")),
    )(*args)


BAND_W = BAND_TILES * T_BAND


def _segments(kind, qi, tile):
    r0 = qi * tile
    if kind == "B":
        lo = max(qi - (BAND_TILES - 1), 0) * tile
        return [(lo, r0 + tile, False, lo - (qi - (BAND_TILES - 1)) * tile)]
    return ([(0, r0, False, 0)] if qi else []) + [(r0, r0 + tile, True, 0)]


def _scores(kind, qh, k_ref, aux_ref, h, seg, tile, scale):
    a, b, diag, c0 = seg
    s = _dot_nt(qh, k_ref[a:b, :]) * (scale * LOG2E)
    if kind == "B":
        return s + aux_ref[h, :, c0:BAND_W]
    if kind == "C":
        s = s - aux_ref[0, h, :, a:b]
    if diag:
        s = jnp.where(_tile_mask(kind, tile), s, NEG)
    return s


def _second_head(kind, pair, fn):
    del kind, pair
    fn()


def _attn_fwd(kind, q, k, v, aux, nb, seq):
    dq = q.shape[1] // 3
    tile = T_BAND if kind == "B" else T_CAUSAL
    nq = seq // tile
    scale = _attn_scale(kind)

    def body(*refs):
        if kind == "A":
            q_ref, k_ref, v_ref, o_ref, lse_ref = refs
            aux_ref = None
        else:
            q_ref, k_ref, v_ref, aux_ref, o_ref, lse_ref = refs
        pair = pl.program_id(1)
        for qi in range(nq):
            rows = slice(qi * tile, (qi + 1) * tile)
            segs = _segments(kind, qi, tile)

            def head(h, rows=rows, segs=segs):
                q2 = q_ref[rows, :]
                mq, mv = _head_masks(kind, tile, dq, h)
                qh = jnp.where(mq, q2, jnp.zeros_like(q2))
                ss = [_scores(kind, qh, k_ref, aux_ref, h, seg, tile, scale) for seg in segs]
                mx = functools.reduce(jnp.maximum, [jnp.max(s, axis=-1, keepdims=True) for s in ss])
                ps = [jnp.exp2(s - mx) for s in ss]
                l = functools.reduce(jnp.add, [jnp.sum(p, axis=-1, keepdims=True) for p in ps])
                acc = functools.reduce(jnp.add, [_dot(p.astype(BF16), v_ref[seg[0]:seg[1], :]) for p, seg in zip(ps, segs)])
                o_h = jnp.where(mv, acc / l, 0.0)
                lse_h = jnp.where(mv, mx + jnp.log(l) * LOG2E, 0.0)
                if h == 0:
                    o_ref[rows, :] = o_h
                    lse_ref[rows, :] = lse_h
                else:
                    o_ref[rows, :] += o_h
                    lse_ref[rows, :] += lse_h

            head(0)
            _second_head(kind, pair, functools.partial(head, 1))

    def seq_spec(cols):
        return pl.BlockSpec((seq, cols), lambda b, p: (b, p))

    in_specs = [seq_spec(dq), seq_spec(dq), seq_spec(LANE)]
    args = [q, k, v]
    if kind == "B":
        in_specs.append(pl.BlockSpec((2, tile, BAND_W), lambda b, p: (p, 0, 0)))
        args.append(aux)
    if kind == "C":
        in_specs.append(pl.BlockSpec((1, 2, 1, seq), lambda b, p: (b, p, 0, 0)))
        args.append(aux)
    return pl.pallas_call(
        body, name="attn_fwd_" + kind, grid=(nb, 3), in_specs=in_specs, out_specs=[seq_spec(LANE), seq_spec(LANE)],
        out_shape=[jax.ShapeDtypeStruct((nb * seq, 384), F32)] * 2, compiler_params=_params(("parallel", "parallel")),
    )(*args)


def _attn_bwd(kind, q, k, v, o, do, lse, aux, nb, seq):
    dq = q.shape[1] // 3
    tile = T_BAND if kind == "B" else T_CAUSAL
    nq = seq // tile
    scale = _attn_scale(kind)
    dqk_dtype = F32 if kind == "A" else BF16

    def body(*refs):
        dfr_ref = dfq_ref = dbt_ref = aux_ref = None
        if kind == "A":
            q_ref, k_ref, v_ref, o_ref, do_ref, lse_ref, dq_ref, dk_ref, dv_ref, dkt_acc, dvt_acc = refs
        elif kind == "B":
            q_ref, k_ref, v_ref, o_ref, do_ref, lse_ref, aux_ref, dq_ref, dk_ref, dv_ref, dbt_ref, dkt_acc, dvt_acc = refs
        else:
            (q_ref, k_ref, v_ref, o_ref, do_ref, lse_ref, aux_ref, dq_ref, dk_ref, dv_ref, dfr_ref, dfq_ref,
             dkt_acc, dvt_acc) = refs
        dkt_acc[...] = jnp.zeros_like(dkt_acc)
        dvt_acc[...] = jnp.zeros_like(dvt_acc)
        if kind == "C":
            dfr_ref[...] = jnp.zeros_like(dfr_ref)
        if kind == "B":
            @pl.when(pl.program_id(1) == 0)
            def _():
                dbt_ref[...] = jnp.zeros_like(dbt_ref)

        pair = pl.program_id(0)
        for qi in range(nq):
            rows = slice(qi * tile, (qi + 1) * tile)
            segs = _segments(kind, qi, tile)

            def head(h, rows=rows, segs=segs):
                q2 = q_ref[rows, :]
                do2 = do_ref[rows, :]
                mq, mv = _head_masks(kind, tile, dq, h)
                qh = jnp.where(mq, q2, jnp.zeros_like(q2))
                doh = jnp.where(mv, do2, 0.0)
                dob = doh.astype(BF16)
                qht = qh.astype(F32).T.astype(BF16)
                dobt = doh.T.astype(BF16)
                delta = jnp.sum(doh * o_ref[rows, :], axis=-1, keepdims=True)
                lseh = jnp.max(jnp.where(mv, lse_ref[rows, :], NEG), axis=-1, keepdims=True)
                rs = jnp.zeros((tile, 1), F32)
                dq_h = jnp.zeros((tile, dq), F32)
                for seg in segs:
                    a, b, _, c0 = seg
                    p = jnp.exp2(_scores(kind, qh, k_ref, aux_ref, h, seg, tile, scale) - lseh)
                    ds = p * (_dot_nt(dob, v_ref[a:b, :]) - delta)
                    if kind == "B":
                        dbt_ref[h, :, c0:BAND_W] += ds
                    if kind == "C":
                        dfr_ref[0, h, :, a:b] -= jnp.sum(ds, axis=0, keepdims=True)
                        rs = rs + jnp.sum(ds, axis=-1, keepdims=True)
                    dss = (ds * scale).astype(BF16)
                    dvt_acc[:, a:b] += _dot(dobt, p.astype(BF16))
                    dkt_acc[:, a:b] += _dot(qht, dss)
                    dq_h = dq_h + _dot(dss, k_ref[a:b, :])
                dq_h = jnp.where(mq, dq_h, 0.0).astype(dqk_dtype)
                if h == 0:
                    dq_ref[rows, :] = dq_h
                else:
                    dq_ref[rows, :] += dq_h
                if kind == "C":
                    if h == 0:
                        dfq_ref[rows, :] = jnp.where(mv, rs, 0.0)
                    else:
                        dfq_ref[rows, :] += jnp.where(mv, rs, 0.0)

            head(0)
            _second_head(kind, pair, functools.partial(head, 1))
        for j in range(seq // 256):
            cols = slice(256 * j, 256 * (j + 1))
            dk_ref[cols, :] = dkt_acc[:, cols].T.astype(dqk_dtype)
            dv_ref[cols, :] = dvt_acc[:, cols].T.astype(BF16)

    def seq_spec(cols):
        return pl.BlockSpec((seq, cols), lambda p, b: (b, p))

    in_specs = [seq_spec(dq), seq_spec(dq), seq_spec(LANE), seq_spec(LANE), seq_spec(LANE), seq_spec(LANE)]
    args = [q, k, v, o, do, lse]
    out_specs = [seq_spec(dq), seq_spec(dq), seq_spec(LANE)]
    out_shape = [jax.ShapeDtypeStruct((nb * seq, 3 * dq), dqk_dtype)] * 2 + [jax.ShapeDtypeStruct((nb * seq, 384), BF16)]
    if kind == "B":
        spec = pl.BlockSpec((2, tile, BAND_W), lambda p, b: (p, 0, 0))
        in_specs.append(spec)
        args.append(aux)
        out_specs.append(spec)
        out_shape.append(jax.ShapeDtypeStruct((6, tile, BAND_W), F32))
    if kind == "C":
        spec = pl.BlockSpec((1, 2, 1, seq), lambda p, b: (b, p, 0, 0))
        in_specs.append(spec)
        args.append(aux)
        out_specs += [spec, seq_spec(LANE)]
        out_shape += [jax.ShapeDtypeStruct((nb, 6, 1, seq), F32), jax.ShapeDtypeStruct((nb * seq, 384), F32)]
    return pl.pallas_call(
        body, name="attn_bwd_" + kind, grid=(3, nb), in_specs=in_specs, out_specs=out_specs, out_shape=out_shape,
        scratch_shapes=[pltpu.VMEM((dq, seq), F32), pltpu.VMEM((LANE, seq), F32)],
        compiler_params=_params(("arbitrary", "arbitrary")),
    )(*args)


BIAS_FLAT = T_BAND * BAND_W
BIAS_CHUNK = 4 * BAND_W


def _rel_onehot(chunk):
    lane = lax.broadcasted_iota(jnp.int32, (384, chunk), 1)
    r = lax.broadcasted_iota(jnp.int32, (384, chunk), 0)
    sub = jnp.where(lane >= BAND_W, 1, 0) + jnp.where(lane >= 2 * BAND_W, 1, 0) + jnp.where(lane >= 3 * BAND_W, 1, 0)
    i = pl.program_id(0) * 4 + sub
    col = lane - sub * BAND_W
    idx = jnp.clip(BAND_W - T_BAND + i - col, -REL_CLIP, REL_CLIP) + REL_CLIP
    return jnp.where(idx == r, 1.0, 0.0).astype(BF16)


BIAS_G = 768
BIAS_EDGE = BIAS_G - N_REL


def _bias_line(rel_bias):
    g = jnp.concatenate([jnp.broadcast_to(rel_bias[:, N_REL - 1:], (rel_bias.shape[0], BIAS_EDGE)),
                         jnp.flip(rel_bias, axis=1)], axis=1)
    return jnp.pad(g, ((0, 8 - g.shape[0]), (0, 0)))


def _bias_unline(dg):
    return jnp.flip(dg[:, BIAS_EDGE:], axis=1)


def _bias_expand(g8):
    def body(g_ref, out_ref):
        line = jnp.broadcast_to(g_ref[0] * LOG2E, (T_BAND, BIAS_G))
        slab = pltpu.roll(line, 1, 1, stride=1, stride_axis=0)[:, LANE:BIAS_G]
        row = lax.broadcasted_iota(jnp.int32, (T_BAND, BAND_W), 0)
        col = lax.broadcasted_iota(jnp.int32, (T_BAND, BAND_W), 1)
        hidden = ((row >= 64) & (col < 64)) | ((row < 64) & (col >= BAND_W - 64))
        out_ref[0] = jnp.where(hidden, NEG, slab)

    return pl.pallas_call(
        body, name="bias_expand", grid=(8,), in_specs=[pl.BlockSpec((1, 1, BIAS_G), lambda h: (h, 0, 0))],
        out_specs=pl.BlockSpec((1, T_BAND, BAND_W), lambda h: (h, 0, 0)),
        out_shape=jax.ShapeDtypeStruct((8, T_BAND, BAND_W), F32), compiler_params=_params(("parallel",)),
    )(g8.reshape(8, 1, BIAS_G))


def _bias_reduce(d_slab):
    def body(d_ref, out_ref):
        r = lax.broadcasted_iota(jnp.int32, (T_BAND, T_BAND), 0)
        k = lax.broadcasted_iota(jnp.int32, (T_BAND, T_BAND), 1)
        flip = jnp.where(r + k == T_BAND - 1, 1.0, 0.0).astype(BF16)
        hi, mid, lo = _split3(d_ref[0])
        d_rev = _dot(flip, hi) + _dot(flip, mid) + _dot(flip, lo)
        wide = jnp.concatenate([jnp.zeros((T_BAND, LANE), F32), d_rev, jnp.zeros((T_BAND, 2 * LANE), F32)], axis=1)
        skew = pltpu.roll(wide, 0, 1, stride=1, stride_axis=0)
        dg = jnp.sum(skew, axis=0, keepdims=True)[:, LANE:LANE + BIAS_G]
        lane = _lane_iota((1, BIAS_G))
        clipped = jnp.sum(jnp.where(lane <= BIAS_EDGE, dg, 0.0), axis=1, keepdims=True)
        out_ref[0] = jnp.where(lane == BIAS_EDGE, clipped, dg)

    return pl.pallas_call(
        body, name="bias_reduce", grid=(8,), in_specs=[pl.BlockSpec((1, T_BAND, BAND_W), lambda h: (h, 0, 0))],
        out_specs=pl.BlockSpec((1, 1, BIAS_G), lambda h: (h, 0, 0)),
        out_shape=jax.ShapeDtypeStruct((8, 1, BIAS_G), F32), compiler_params=_params(("parallel",)),
    )(d_slab).reshape(8, BIAS_G)


def _outproj_bwd(dxn, y, gate, oa, ob, oc, zg, w_out_p, w_out_pt, nb, seq):
    t = dxn.shape[0]
    tpe = seq // TM

    def body(dxn_ref, y_ref, gate_ref, oa_ref, ob_ref, oc_ref, zg_ref, w_ref, wt_ref,
             doa_ref, dob_ref, doc_ref, dzg_ref, gw_ref, dgate_ref):
        i = pl.program_id(0)

        @pl.when(i == 0)
        def _():
            gw_ref[...] = jnp.zeros_like(gw_ref)

        @pl.when(i % tpe == 0)
        def _():
            dgate_ref[...] = jnp.zeros_like(dgate_ref)

        dxn_t = dxn_ref[...]
        dgate_ref[0] += jnp.sum(dxn_t * y_ref[...], axis=0, keepdims=True)
        dy = (dxn_t * gate_ref[0]).astype(BF16)
        for gi, (o_ref, do_ref) in enumerate(((oa_ref, doa_ref), (ob_ref, dob_ref), (oc_ref, doc_ref))):
            cols = slice(384 * gi, 384 * (gi + 1))
            u = zg_ref[:, cols]
            o_t = o_ref[...]
            su = _silu(u)
            dcat = _dot(dy, wt_ref[:, cols])
            do_ref[...] = dcat * su
            dzg_ref[:, cols] = (dcat * o_t * _dsilu(u)).astype(BF16)
            gw_ref[cols, :] += _dot_tn((o_t * su).astype(BF16), dy)

    return pl.pallas_call(
        body, name="outproj_bwd", grid=(t // TM,),
        in_specs=[_row_spec(D_MODEL), _row_spec(D_MODEL), _ex_spec(tpe), _row_spec(384), _row_spec(384), _row_spec(384),
                  _row_spec(D_CAT), _full_spec((D_CAT, D_MODEL)), _full_spec((D_MODEL, D_CAT))],
        out_specs=[_row_spec(384), _row_spec(384), _row_spec(384), _row_spec(D_CAT), _full_spec((D_CAT, D_MODEL)),
                   _ex_spec(tpe)],
        out_shape=[jax.ShapeDtypeStruct((t, 384), F32)] * 3 + [jax.ShapeDtypeStruct((t, D_CAT), BF16),
                                                                jax.ShapeDtypeStruct((D_CAT, D_MODEL), F32),
                                                                jax.ShapeDtypeStruct((nb, 1, D_MODEL), F32)],
        compiler_params=_params(("arbitrary",)),
    )(dxn, y, gate, oa, ob, oc, zg, w_out_p, w_out_pt)


def _a_up_bwd(dqa, dka, dva, za, gq, gkv, w_uq_pt, w_ukv_pt, cos, sina, sinb):
    t = za.shape[0]

    def body(dq_ref, dk_ref, dv_ref, za_ref, gq_ref, gkv_ref, wqt_ref, wkvt_ref, cos_ref, sa_ref, sb_ref,
             dza_ref, gwq_ref, gwkv_ref, ggq_ref, ggkv_ref, dqb, dkvb):
        @pl.when(pl.program_id(0) == 0)
        def _():
            gwq_ref[...] = jnp.zeros_like(gwq_ref)
            gwkv_ref[...] = jnp.zeros_like(gwkv_ref)
            ggq_ref[...] = jnp.zeros_like(ggq_ref)
            ggkv_ref[...] = jnp.zeros_like(ggkv_ref)

        cos_t, sa, sb = cos_ref[...], sa_ref[...], sb_ref[...]
        dkpe = jnp.zeros((TM, LANE), F32)
        for p in range(3):
            dqb[:, 256 * p:256 * p + 128] = dq_ref[:, 256 * p:256 * p + 128].astype(BF16)
            dqb[:, 256 * p + 128:256 * p + 256] = _rope_t(dq_ref[:, 256 * p + 128:256 * p + 256], cos_t, sa, sb).astype(BF16)
            dkvb[:, 128 * p:128 * p + 128] = dk_ref[:, 256 * p:256 * p + 128].astype(BF16)
            dkpe = dkpe + dk_ref[:, 256 * p + 128:256 * p + 256]
        dkvb[:, 384:768] = dv_ref[...]
        dkpe = _rope_t(dkpe, cos_t, sa, sb)
        dkpe = jnp.where(_lane_iota((TM, LANE)) < A_ROPE, dkpe + pltpu.roll(dkpe, LANE - 32, 1), 0.0)

        gqv = gq_ref[...]
        cqn, cqh, rq = _rms(za_ref[:, 0:384], gqv)
        dq_t = dqb[...]
        gwq_ref[...] += _dot_tn(cqn.astype(BF16), dq_t)
        dcqn = _dot(dq_t, wqt_ref[...])
        ggq_ref[...] += jnp.broadcast_to(jnp.sum(dcqn * cqh, axis=0, keepdims=True), ggq_ref.shape)
        dza_ref[:, 0:384] = _rms_bwd(dcqn, cqh, rq, gqv).astype(BF16)

        gkvv = gkv_ref[...]
        ckvn, ckvh, rkv = _rms(za_ref[:, 384:640], gkvv)
        dkv_t = dkvb[...]
        gwkv_ref[...] += _dot_tn(ckvn.astype(BF16), dkv_t)
        dckvn = _dot(dkv_t, wkvt_ref[...])
        ggkv_ref[...] += jnp.broadcast_to(jnp.sum(dckvn * ckvh, axis=0, keepdims=True), ggkv_ref.shape)
        dza_ref[:, 384:640] = _rms_bwd(dckvn, ckvh, rkv, gkvv).astype(BF16)
        dza_ref[:, 640:768] = dkpe.astype(BF16)

    return pl.pallas_call(
        body, name="a_up_bwd", grid=(t // TM,),
        in_specs=[_row_spec(768), _row_spec(768), _row_spec(384), _row_spec(768), _full_spec((1, 384)),
                  _full_spec((1, 256)), _full_spec((768, 384)), _full_spec((768, 256)), _row_spec(LANE), _row_spec(LANE),
                  _row_spec(LANE)],
        out_specs=[_row_spec(768), _full_spec((384, 768)), _full_spec((256, 768)), _full_spec((8, 384)),
                   _full_spec((8, 256))],
        out_shape=[jax.ShapeDtypeStruct((t, 768), BF16), jax.ShapeDtypeStruct((384, 768), F32),
                   jax.ShapeDtypeStruct((256, 768), F32), jax.ShapeDtypeStruct((8, 384), F32),
                   jax.ShapeDtypeStruct((8, 256), F32)],
        scratch_shapes=[pltpu.VMEM((TM, 768), BF16), pltpu.VMEM((TM, 768), BF16)],
        compiler_params=_params(("arbitrary",)),
    )(dqa, dka, dva, za, gq, gkv, w_uq_pt, w_ukv_pt, cos, sina, sinb)


def _dz_cols():
    return (Z_A, Z_G) + Z_QKV + (Z_F,)


def _inproj_bwd_dx(dz, dxn, x, shift, scale, g, w_in_pt, nb, seq):
    t = x.shape[0]
    tpe = seq // TM
    cols = _dz_cols()

    def body(*refs):
        dz_refs = refs[:len(cols)]
        dxn_ref, x_ref, sh_ref, sc_ref, g_ref, wt_ref, dx_ref, dsh_ref, dsc_ref, dg_ref = refs[len(cols):]
        i = pl.program_id(0)

        @pl.when(i == 0)
        def _():
            dg_ref[...] = jnp.zeros_like(dg_ref)

        @pl.when(i % tpe == 0)
        def _():
            dsh_ref[...] = jnp.zeros_like(dsh_ref)
            dsc_ref[...] = jnp.zeros_like(dsc_ref)

        dh = jnp.zeros((TM, D_MODEL), F32)
        for ref, (c0, c1) in zip(dz_refs, cols):
            dh = dh + _dot_nt(ref[...], wt_ref[:, c0:c1])
        gv = g_ref[...]
        n, xh, r = _rms(x_ref[...], gv)
        dsh_ref[0] += jnp.sum(dh, axis=0, keepdims=True)
        dsc_ref[0] += jnp.sum(dh * n, axis=0, keepdims=True)
        dn = dh * (1.0 + sc_ref[0])
        dg_ref[...] += jnp.broadcast_to(jnp.sum(dn * xh, axis=0, keepdims=True), dg_ref.shape)
        dx_ref[...] = dxn_ref[...] + _rms_bwd(dn, xh, r, gv)

    in_specs = [_row_spec(c1 - c0) for c0, c1 in cols]
    in_specs += [_row_spec(D_MODEL), _row_spec(D_MODEL), _ex_spec(tpe), _ex_spec(tpe), _full_spec((1, D_MODEL)),
                 _full_spec((D_MODEL, NP_IN))]
    return pl.pallas_call(
        body, name="inproj_bwd_dx", grid=(t // TM,), in_specs=in_specs,
        out_specs=[_row_spec(D_MODEL), _ex_spec(tpe), _ex_spec(tpe), _full_spec((8, D_MODEL))],
        out_shape=[jax.ShapeDtypeStruct((t, D_MODEL), F32), jax.ShapeDtypeStruct((nb, 1, D_MODEL), F32),
                   jax.ShapeDtypeStruct((nb, 1, D_MODEL), F32), jax.ShapeDtypeStruct((8, D_MODEL), F32)],
        compiler_params=_params(("arbitrary",)),
    )(*dz, dxn, x, shift, scale, g, w_in_pt)


def _inproj_bwd_dw(h, dz, name, layer, both=None):
    t = h.shape[0]
    widths = [d.shape[1] for d in dz]
    total = sum(widths)

    def body(*refs):
        h_ref = refs[0]
        dz_refs = refs[1:1 + len(dz)]
        gw_ref = refs[-1]

        @pl.when(pl.program_id(0) == 0)
        def _():
            gw_ref[...] = jnp.zeros_like(gw_ref)

        h_t = h_ref[...]
        c0 = 0
        for ref, w in zip(dz_refs, widths):
            gw_ref[0, :, c0:c0 + w] += _dot_tn(h_t, ref[...])
            c0 += w

    in_specs = [_row_spec(D_MODEL)] + [_row_spec(w) for w in widths]
    args = [h, *dz]
    aliases = {}
    if both is not None:
        in_specs.append(pl.BlockSpec(memory_space=pl.ANY))
        aliases = {len(args): 0}
        args.append(both)
    return pl.pallas_call(
        body, name=name, grid=(t // TM,), in_specs=in_specs,
        out_specs=pl.BlockSpec((1, D_MODEL, total), lambda i: (layer, 0, 0)),
        out_shape=jax.ShapeDtypeStruct((DEPTH, D_MODEL, total), F32), input_output_aliases=aliases,
        compiler_params=_params(("arbitrary",)),
    )(*args)


def _ada_fwd(c_all, w_ada, b_cols):
    n = c_all.shape[0]
    cols = w_ada.shape[2]

    def body(c_ref, w_ref, b_ref, out_ref):
        act = _silu(c_ref[...]).astype(BF16)
        out_ref[0] = _dot(act, w_ref[0].astype(BF16)) + b_ref[0]

    return pl.pallas_call(
        body, name="ada_fwd", grid=(DEPTH,),
        in_specs=[pl.BlockSpec((n, D_MODEL), lambda l: (0, 0)), pl.BlockSpec((1, D_MODEL, cols), lambda l: (l, 0, 0)),
                  pl.BlockSpec((1, 1, cols), lambda l: (l, 0, 0))],
        out_specs=pl.BlockSpec((1, n, cols), lambda l: (l, 0, 0)),
        out_shape=jax.ShapeDtypeStruct((DEPTH, n, cols), F32), compiler_params=_params(("parallel",)),
    )(c_all, w_ada, b_cols)


def _ada_bwd(c_all, dmod_cols, dmod_all):
    n = c_all.shape[0]
    cols = dmod_cols.shape[2]

    def body(c_ref, dc_ref, da_ref, gw_ref, gb_ref):
        act = _silu(c_ref[...]).astype(BF16)
        gw_ref[0] = _dot_tn(act, dc_ref[0].astype(BF16))
        gb_ref[0] = jnp.sum(da_ref[0], axis=0, keepdims=True)

    return pl.pallas_call(
        body, name="ada_bwd", grid=(DEPTH,),
        in_specs=[pl.BlockSpec((n, D_MODEL), lambda l: (0, 0)), pl.BlockSpec((1, n, cols), lambda l: (l, 0, 0)),
                  pl.BlockSpec((1, n, 3 * D_MODEL), lambda l: (l, 0, 0))],
        out_specs=[pl.BlockSpec((1, D_MODEL, cols), lambda l: (l, 0, 0)),
                   pl.BlockSpec((1, 1, 3 * D_MODEL), lambda l: (l, 0, 0))],
        out_shape=[jax.ShapeDtypeStruct((DEPTH, D_MODEL, cols), F32), jax.ShapeDtypeStruct((DEPTH, 1, 3 * D_MODEL), F32)],
        compiler_params=_params(("parallel",)),
    )(c_all, dmod_cols, dmod_all)


def _sum_blocks(parts, name):
    n, rows, cols = parts.shape
    tr = rows if rows <= 256 else 8 * next(d for d in range(32, 0, -1) if (rows // 8) % d == 0)

    def body(p_ref, out_ref):
        acc = p_ref[0].astype(F32)
        for k in range(1, n):
            acc = acc + p_ref[k].astype(F32)
        out_ref[...] = acc

    return pl.pallas_call(
        body, name=name, grid=(rows // tr,), in_specs=[pl.BlockSpec((n, tr, cols), lambda i: (0, i, 0))],
        out_specs=pl.BlockSpec((tr, cols), lambda i: (i, 0)), out_shape=jax.ShapeDtypeStruct((rows, cols), F32),
        compiler_params=_params(("parallel",)),
    )(parts)


def _adamw(w, g, m, v, name):
    nl, rows, cols = w.shape
    tr = next(t for t in (rows, 256, 128, 64, 32, 16, 8) if rows % t == 0 and t * cols <= 256 * 1024)

    def body(w_ref, g_ref, m_ref, v_ref, d_ref, mo_ref, vo_ref):
        gv = g_ref[...]
        mn = ADAM_B1 * m_ref[...] + (1.0 - ADAM_B1) * gv
        vn = ADAM_B2 * v_ref[...] + (1.0 - ADAM_B2) * jnp.square(gv)
        m_hat = mn / (1.0 - ADAM_B1 ** ADAM_STEP)
        v_hat = vn / (1.0 - ADAM_B2 ** ADAM_STEP)
        d_ref[...] = -ADAM_LR * (m_hat / (jnp.sqrt(v_hat) + ADAM_EPS) + ADAM_WD * w_ref[...])
        mo_ref[...] = mn
        vo_ref[...] = vn

    spec = pl.BlockSpec((1, tr, cols), lambda l, i: (l, i, 0))
    return pl.pallas_call(
        body, name=name, grid=(nl, rows // tr), in_specs=[spec] * 4, out_specs=[spec] * 3,
        out_shape=[jax.ShapeDtypeStruct((nl, rows, cols), F32)] * 3, compiler_params=_params(("parallel", "parallel")),
    )(w, g, m, v)


ALL_FLIPS = tuple(range(1, N_DEV))


def _exchange(src, flips, mode, name):
    _, rows, cols = src.shape
    nslot = 2 if mode == "pair" else N_DEV
    nf = len(flips)

    def body(src_ref, dst_ref, send_sems, recv_sems, local_sem):
        x, y, c = lax.axis_index("x"), lax.axis_index("y"), lax.axis_index("c")
        me = 4 * x + 2 * y + c

        def slot(j):
            return (j & 1) if mode == "pair" else j

        own = pltpu.make_async_copy(src_ref.at[me if mode == "scatter" else 0], dst_ref.at[slot(me)], local_sem)
        own.start()
        copies = []
        for i, f in enumerate(flips):
            peer = me ^ f
            to = (1 - x if f & 4 else x, 1 - y if f & 2 else y, 1 - c if f & 1 else c)
            cp = pltpu.make_async_remote_copy(
                src_ref=src_ref.at[peer if mode == "scatter" else 0], dst_ref=dst_ref.at[slot(me)],
                send_sem=send_sems.at[i], recv_sem=recv_sems.at[i], device_id=to, device_id_type=pl.DeviceIdType.MESH)
            cp.start()
            copies.append(cp)
        for i, f in enumerate(flips):
            peer = me ^ f
            to = (1 - x if f & 4 else x, 1 - y if f & 2 else y, 1 - c if f & 1 else c)
            pltpu.make_async_remote_copy(
                src_ref=src_ref.at[0], dst_ref=dst_ref.at[slot(peer)], send_sem=send_sems.at[i],
                recv_sem=recv_sems.at[i], device_id=to, device_id_type=pl.DeviceIdType.MESH).wait_recv()
        for cp in copies:
            cp.wait_send()
        own.wait()

    return pl.pallas_call(
        body, name=name, out_shape=jax.ShapeDtypeStruct((nslot, rows, cols), src.dtype),
        in_specs=[pl.BlockSpec(memory_space=pl.ANY)], out_specs=pl.BlockSpec(memory_space=pl.ANY),
        scratch_shapes=[pltpu.SemaphoreType.DMA((nf,)), pltpu.SemaphoreType.DMA((nf,)), pltpu.SemaphoreType.DMA],
    )(src)


def _transfer(name, srcs, dst_shapes, plan, in_place=False):
    n_arr = len(srcs)
    probe = plan(0, 0, 0)
    n_steps = len(probe)

    def body(*refs):
        src_refs, dst_refs = refs[:n_arr], refs[n_arr:2 * n_arr]
        send_sems, recv_sems, local_sems = refs[2 * n_arr:]
        x, y, c = lax.axis_index("x"), lax.axis_index("y"), lax.axis_index("c")
        steps = plan(x, y, c)

        def rows(ref, r0, n):
            return ref.at[:, pl.ds(r0, n), :]

        def arrival(t):
            a, _, _, n, _, f, _ = steps[t]
            return pltpu.make_async_remote_copy(
                src_ref=rows(dst_refs[a], 0, n), dst_ref=rows(dst_refs[a], 0, n), send_sem=send_sems.at[t],
                recv_sem=recv_sems.at[t], device_id=(x, y, c), device_id_type=pl.DeviceIdType.MESH)

        arrived, started = set(), []
        for t, (a, from_dst, sr, n, dr, f, after) in enumerate(steps):
            for u in after:
                if u not in arrived:
                    arrival(u).wait_recv()
                    arrived.add(u)
            src = rows(dst_refs[a] if from_dst else src_refs[a], sr, n)
            dst = rows(dst_refs[a], dr, n)
            if f == 0:
                cp = pltpu.make_async_copy(src, dst, local_sems.at[t])
            else:
                to = (1 - x if f & 4 else x, 1 - y if f & 2 else y, 1 - c if f & 1 else c)
                cp = pltpu.make_async_remote_copy(src_ref=src, dst_ref=dst, send_sem=send_sems.at[t],
                                                  recv_sem=recv_sems.at[t], device_id=to,
                                                  device_id_type=pl.DeviceIdType.MESH)
            cp.start()
            started.append(cp)
        for t, step in enumerate(steps):
            if step[5] != 0 and t not in arrived:
                arrival(t).wait_recv()
        for cp, step in zip(started, steps):
            if step[5] == 0:
                cp.wait()
            else:
                cp.wait_send()

    any_spec = pl.BlockSpec(memory_space=pl.ANY)
    return pl.pallas_call(
        body, name=name, out_shape=[jax.ShapeDtypeStruct(s, d) for s, d in dst_shapes],
        in_specs=[any_spec] * n_arr, out_specs=[any_spec] * n_arr,
        input_output_aliases={a: a for a in range(n_arr)} if in_place else {},
        scratch_shapes=[pltpu.SemaphoreType.DMA((n_steps,)), pltpu.SemaphoreType.DMA((n_steps,)),
                        pltpu.SemaphoreType.DMA((n_steps,))],
    )(*srcs)


CHIP_FLIPS = (2, 4, 6)


def _gather_plan(chip_rows):
    def plan(x, y, c):
        steps = []
        for a, rc in enumerate(chip_rows):
            h = rc // 2
            mine = rc * (2 * x + y) + h * c
            steps.append((a, False, h * c, h, mine, 0, ()))
            ici = {}
            for f in CHIP_FLIPS:
                ici[f] = len(steps)
                steps.append((a, False, h * c, h, mine, f, ()))
            steps.append((a, False, h * c, h, mine, 1, ()))
            for f in CHIP_FLIPS:
                theirs = rc * ((2 * x + y) ^ (f >> 1)) + h * c
                steps.append((a, True, theirs, h, theirs, 1, (ici[f],)))
        return steps
    return plan


def _pair_reduce_plan(chip_rows):
    def plan(x, y, c):
        steps = []
        for a, rc in enumerate(chip_rows):
            h = rc // 2
            for j in range(4):
                steps.append((a, False, rc * j + h * (1 - c), h, h * j, 1, ()))
        return steps
    return plan


def _chip_scatter_plan(chip_rows):
    def plan(x, y, c):
        steps = []
        for a, rc in enumerate(chip_rows):
            h = rc // 2
            for k, f in enumerate(CHIP_FLIPS):
                steps.append((a, False, h * ((2 * x + y) ^ (f >> 1)), h, h * k, f, ()))
        return steps
    return plan


def _pair_share_plan(chip_rows):
    def plan(x, y, c):
        return [(a, True, (rc // 2) * c, rc // 2, (rc // 2) * c, 1, ()) for a, rc in enumerate(chip_rows)]
    return plan


def _tile_rows(h):
    return next(t for t in (64, 32, 16) if h % t == 0)


def _sum_pair(partial, recv, core, rc, name):
    nl, _, cols = partial.shape
    h = rc // 2
    tr = _tile_rows(h)

    def body(c_ref, p_ref, r_ref, out_ref):
        out_ref[...] = (p_ref[...] + r_ref[...]).astype(BF16)

    spec = pl.BlockSpec((1, tr, cols), lambda l, j, i, c_ref: (l, (h // tr) * j + i, 0))
    return pl.pallas_call(
        body, name=name, out_shape=jax.ShapeDtypeStruct((nl, 4 * h, cols), BF16),
        grid_spec=pltpu.PrefetchScalarGridSpec(
            num_scalar_prefetch=1, grid=(nl, 4, h // tr),
            in_specs=[pl.BlockSpec((1, tr, cols), lambda l, j, i, c_ref: (l, (rc // tr) * j + (h // tr) * c_ref[0] + i, 0)),
                      spec],
            out_specs=spec),
        compiler_params=_params(("parallel", "parallel", "parallel")),
    )(core, partial, recv)


def _sum_chips(chip_sum, recv, place, rc, name):
    nl, _, cols = chip_sum.shape
    h = rc // 2
    tr = _tile_rows(h)

    def body(s_ref, own_ref, r_ref, out_ref):
        acc = own_ref[0].astype(F32)
        for k in range(3):
            acc = acc + r_ref[0, k].astype(F32)
        out_ref[0] = acc

    return pl.pallas_call(
        body, name=name, out_shape=jax.ShapeDtypeStruct((nl, rc, cols), F32),
        grid_spec=pltpu.PrefetchScalarGridSpec(
            num_scalar_prefetch=1, grid=(nl, h // tr),
            in_specs=[pl.BlockSpec((1, tr, cols), lambda l, i, s_ref: (l, (h // tr) * s_ref[0] + i, 0)),
                      pl.BlockSpec((1, 3, tr, cols), lambda l, i, s_ref: (l, 0, i, 0))],
            out_specs=pl.BlockSpec((1, tr, cols), lambda l, i, s_ref: (l, (h // tr) * s_ref[1] + i, 0))),
        compiler_params=_params(("parallel", "parallel")),
    )(place, chip_sum, recv.reshape(nl, 3, h, cols))


def _sum_slots(parts, out_dtype, name):
    nl, n, rows, cols = parts.shape
    tr = next(t for t in (128, 64, 32, 16) if rows % t == 0 and n * t * cols * 4 <= (4 << 20))

    def body(p_ref, out_ref):
        acc = p_ref[0, 0].astype(F32)
        for k in range(1, n):
            acc = acc + p_ref[0, k].astype(F32)
        out_ref[0] = acc.astype(out_dtype)

    return pl.pallas_call(
        body, name=name, grid=(nl, rows // tr),
        in_specs=[pl.BlockSpec((1, n, tr, cols), lambda l, i: (l, 0, i, 0))],
        out_specs=pl.BlockSpec((1, tr, cols), lambda l, i: (l, i, 0)),
        out_shape=jax.ShapeDtypeStruct((nl, rows, cols), out_dtype), compiler_params=_params(("parallel", "parallel")),
    )(parts)


def _pad_cols(a, n):
    return a if n == 0 else jnp.pad(a, ((0, 0), (0, n)))


def _in_to_padded(w):
    return jnp.concatenate([_pad_cols(w[:, a:b], z) for a, b, z in IN_PIECES], axis=1)


def _in_from_padded(gp):
    pos, out = 0, {}
    for a, b, z in IN_PIECES:
        out[a] = gp[:, pos:pos + (b - a)]
        pos += (b - a) + z
    return jnp.concatenate([out[a] for a in sorted(out)], axis=1)


def _out_to_padded(w):
    z = jnp.zeros((64, w.shape[1]), w.dtype)
    return jnp.concatenate([w[0:384], w[384:704], z, w[704:1024], z], axis=0)


def _out_from_padded(gp):
    return jnp.concatenate([gp[0:384], gp[384:704], gp[768:1088]], axis=0)


def _uq_to_padded(w):
    parts = []
    for p in range(3):
        h0, h1 = 2 * p, 2 * p + 1
        parts += [w[:, 96 * h0:96 * h0 + 64], w[:, 96 * h1:96 * h1 + 64], w[:, 96 * h0 + 64:96 * h0 + 96],
                  w[:, 96 * h1 + 64:96 * h1 + 96], jnp.zeros((w.shape[0], 64), w.dtype)]
    return jnp.concatenate(parts, axis=1)


def _uq_from_padded(gp):
    parts = []
    for h in range(6):
        p, s = h // 2, h % 2
        parts += [gp[:, 256 * p + 64 * s:256 * p + 64 * s + 64], gp[:, 256 * p + 128 + 32 * s:256 * p + 160 + 32 * s]]
    return jnp.concatenate(parts, axis=1)


def _ukv_to_padded(w):
    return jnp.concatenate([w[:, 128 * h:128 * h + 64] for h in range(6)]
                           + [w[:, 128 * h + 64:128 * h + 128] for h in range(6)], axis=1)


def _ukv_from_padded(gp):
    parts = []
    for h in range(6):
        parts += [gp[:, 64 * h:64 * h + 64], gp[:, 384 + 64 * h:384 + 64 * h + 64]]
    return jnp.concatenate(parts, axis=1)


LR_ROWS = 224
SHARD_ROWS = (256, 256, 256, LR_ROWS)


def _pack_lowrank(w_uq, w_ukv):
    flat = jnp.concatenate([w_uq.reshape(-1), w_ukv.reshape(-1)])
    return jnp.pad(flat, (0, LR_ROWS * PACK_COLS - flat.shape[0])).reshape(1, LR_ROWS, PACK_COLS)


def _unpack_lowrank(packed):
    flat = packed.reshape(-1)
    n_uq = DEPTH * A_Q_RANK * 144
    n_ukv = DEPTH * A_KV_RANK * 192
    return flat[0:n_uq].reshape(DEPTH, A_Q_RANK, 144), flat[n_uq:n_uq + n_ukv].reshape(DEPTH, A_KV_RANK, 192)


PACK_SIZES = (DEPTH * 256 * N_IN, DEPTH * 256 * D_MODEL, DEPTH * A_Q_RANK * 144, DEPTH * A_KV_RANK * 192)


def _pack(w_in, w_out, w_uq, w_ukv):
    flat = jnp.concatenate([w_in.reshape(-1), w_out.reshape(-1), w_uq.reshape(-1), w_ukv.reshape(-1)])
    flat = jnp.pad(flat, (0, PACK_ROWS * PACK_COLS - flat.shape[0]))
    return flat.reshape(PACK_ROWS, PACK_COLS)


def _unpack(packed):
    flat = packed.reshape(-1)
    o0, o1, o2, o3 = PACK_SIZES
    w_in = flat[0:o0].reshape(DEPTH, 256, N_IN)
    w_out = flat[o0:o0 + o1].reshape(DEPTH, 256, D_MODEL)
    w_uq = flat[o0 + o1:o0 + o1 + o2].reshape(DEPTH, A_Q_RANK, 144)
    w_ukv = flat[o0 + o1 + o2:o0 + o1 + o2 + o3].reshape(DEPTH, A_KV_RANK, 192)
    return w_in, w_out, w_uq, w_ukv


def _rope_tables(positions):
    inv = ROPE_THETA ** (-jnp.arange(0, A_ROPE, 2, dtype=F32) / A_ROPE)
    ang = positions.astype(F32)[..., None] * inv
    cos = jnp.tile(jnp.cos(ang), (1, 1, 4)).reshape(-1, 64)
    sin = jnp.tile(jnp.sin(ang), (1, 1, 4)).reshape(-1, 64)
    second = (jnp.arange(64) % 32) >= 16
    pad = ((0, 0), (0, 64))
    return (jnp.pad(cos, pad), jnp.pad(jnp.where(second, 0.0, -sin), pad), jnp.pad(jnp.where(second, sin, 0.0), pad))


def _rows(a, n):
    flat = a.reshape(-1)
    return jnp.pad(flat, (0, n * LANE - flat.shape[0])).reshape(n, LANE)


def _forward_backward(x, mod, tables, target, weights, small, nb, seq):
    cos, sina, sinb = tables
    saved = []
    for l in range(DEPTH):
        w, s = weights[l], small[l]
        shift = mod[l][:, None, 0:D_MODEL]
        scale = mod[l][:, None, D_MODEL:2 * D_MODEL]
        gate = mod[l][:, None, 2 * D_MODEL:]
        h, za, zg, qb, kb, vb, qc, kc, vc, zf = _ln_inproj(x, shift, scale, s["norm_g"], w["in"], seq)
        qa, ka, va = _a_up(za, s["gq"], s["gkv"], w["uq"], w["ukv"], cos, sina, sinb)
        bias = _bias_expand(s["g8"])[0:6]
        f = _forget_fwd(zf, s["fb"], nb, seq)
        frow = jnp.pad(f[:, 0:5].reshape(nb, seq, 5).transpose(0, 2, 1), ((0, 0), (0, 1), (0, 0)))
        frow = frow.reshape(nb, 6, 1, seq)
        oa, lse_a = _attn_fwd("A", qa, ka, va, None, nb, seq)
        ob, lse_b = _attn_fwd("B", qb, kb, vb, bias, nb, seq)
        oc, lse_c = _attn_fwd("C", qc, kc, vc, frow, nb, seq)
        y, xn = _gate_outproj(x, gate, oa, ob, oc, zg, w["out"], seq)
        saved.append(dict(x=x, h=h, za=za, zg=zg, zf=zf, y=y, shift=shift, scale=scale, gate=gate, bias=bias, frow=frow,
                          a=(qa, ka, va, oa, lse_a), b=(qb, kb, vb, ob, lse_b), c=(qc, kc, vc, oc, lse_c)))
        x = xn
    dx, loss8, gfinal8 = _final_loss(x, target, small[0]["final_g"])
    grads = []
    gw_in = (None, None)
    for l in reversed(range(DEPTH)):
        w, s, sv = weights[l], small[l], saved[l]
        qa, ka, va, oa, lse_a = sv["a"]
        qb, kb, vb, ob, lse_b = sv["b"]
        qc, kc, vc, oc, lse_c = sv["c"]
        doa, dob, doc, dzg, gw_out, dgate = _outproj_bwd(dx, sv["y"], sv["gate"], oa, ob, oc, sv["zg"], w["out"],
                                                          w["out_t"], nb, seq)
        dqa, dka, dva = _attn_bwd("A", qa, ka, va, oa, doa, lse_a, None, nb, seq)
        dqb, dkb, dvb, dbt = _attn_bwd("B", qb, kb, vb, ob, dob, lse_b, sv["bias"], nb, seq)
        dqc, dkc, dvc, dfr, dfq = _attn_bwd("C", qc, kc, vc, oc, doc, lse_c, sv["frow"], nb, seq)
        dg = _bias_reduce(jnp.pad(dbt, ((0, 2), (0, 0), (0, 0))))
        grb = jnp.pad(_bias_unline(dg), ((0, 0), (0, 384 - N_REL)))
        dfk = dfr.reshape(nb, 6, seq).transpose(0, 2, 1).reshape(nb * seq, 6)
        dzf, gfb = _forget_bwd(dfq, jnp.pad(dfk, ((0, 0), (0, LANE - 6))), sv["zf"], s["fb"], nb, seq)
        dza, gw_uq, gw_ukv, ggq, ggkv = _a_up_bwd(dqa, dka, dva, sv["za"], s["gq"], s["gkv"], w["uq_t"], w["ukv_t"],
                                                  cos, sina, sinb)
        dz = (dza, dzg, dqb, dkb, dvb, dqc, dkc, dvc, dzf)
        dx, dshift, dscale, gnorm = _inproj_bwd_dx(dz, dx, sv["x"], sv["shift"], sv["scale"], s["norm_g"], w["in"],
                                                   nb, seq)
        gw_in = (_inproj_bwd_dw(sv["h"], dz[0:2], "inproj_bwd_dw0", l, gw_in[0]),
                 _inproj_bwd_dw(sv["h"], dz[2:], "inproj_bwd_dw1", l, gw_in[1]))
        dmod = jnp.concatenate([dshift[:, 0], dscale[:, 0], dgate[:, 0]], axis=1)
        grads.append(dict(w_out=gw_out, w_uq=gw_uq, w_ukv=gw_ukv, dmod=dmod, norm_g=gnorm[0], gq=ggq[0],
                          gkv=ggkv[0], rb8=grb, fb=gfb[0]))
    grads.reverse()
    return loss8[0, 0], dx, grads, gfinal8[0], gw_in


def _layer_weights(w_in, w_out, w_uq, w_ukv):
    wi, wo, wq, wkv = _in_to_padded(w_in), _out_to_padded(w_out), _uq_to_padded(w_uq), _ukv_to_padded(w_ukv)
    return {"in": wi, "out": wo, "out_t": wo.T, "uq": wq, "uq_t": wq.T, "ukv": wkv, "ukv_t": wkv.T}


def _layer_small(norm_g, gq, gkv, rel_bias, forget_b, final_g):
    fb = jnp.pad(forget_b, (0, LANE - 5)).reshape(1, LANE)
    return dict(norm_g=norm_g.reshape(1, -1), gq=gq.reshape(1, -1), gkv=gkv.reshape(1, -1), g8=_bias_line(rel_bias), fb=fb,
                final_g=final_g.reshape(1, -1))


def _small_payload(per_layer, final_g, loss):
    def stack(key):
        return jnp.stack([p[key] for p in per_layer])

    def rows(a, rng):
        return _rows(a, rng[1] - rng[0])

    dmod = stack("dmod") if "dmod" in per_layer[0] else jnp.zeros((LANE,), F32)
    parts = [rows(dmod, PAY_DMOD), rows(stack("norm_g"), PAY_NORM), rows(stack("gq"), PAY_GQ),
             rows(stack("gkv"), PAY_GKV), rows(stack("rb8"), PAY_RB), rows(stack("fb"), PAY_FB),
             rows(final_g, PAY_FINAL), rows(loss, PAY_LOSS)]
    return jnp.concatenate(parts, axis=0)


def _payload_split(pay):
    def take(rng, shape):
        n = 1
        for d in shape:
            n *= d
        return pay[rng[0]:rng[1]].reshape(-1)[0:n].reshape(shape)

    norm_g = take(PAY_NORM, (DEPTH, D_MODEL))
    gq = take(PAY_GQ, (DEPTH, A_Q_RANK))
    gkv = take(PAY_GKV, (DEPTH, A_KV_RANK))
    rb = take(PAY_RB, (DEPTH, 8, 384))[:, 0:5, 0:N_REL]
    fb = take(PAY_FB, (DEPTH, LANE))[:, 0:5]
    final_g = take(PAY_FINAL, (D_MODEL,))
    return norm_g, gq, gkv, rb, fb, final_g


def kernel(x, c, positions, w_ada, b_ada, norm_g, w_in, a_q_norm_g, a_w_uq, a_kv_norm_g, a_w_ukv, b_rel_bias, c_forget_b, w_out, final_g, loss_target, m_w_ada, m_b_ada, m_norm_g, m_w_in, m_a_q_norm_g, m_a_w_uq, m_a_kv_norm_g, m_a_w_ukv, m_b_rel_bias, m_c_forget_b, m_w_out, m_final_g, v_w_ada, v_b_ada, v_norm_g, v_w_in, v_a_q_norm_g, v_a_w_uq, v_a_kv_norm_g, v_a_w_ukv, v_b_rel_bias, v_c_forget_b, v_w_out, v_final_g):
    nb, seq, _ = x.shape
    ix, iy, ic = lax.axis_index("x"), lax.axis_index("y"), lax.axis_index("c")
    chip = 2 * ix + iy
    me = 2 * chip + ic

    full_in, full_out, full_lr = _transfer(
        "gather_weights", [w_in.astype(BF16), w_out.astype(BF16), _pack_lowrank(a_w_uq, a_w_ukv).astype(BF16)],
        [((DEPTH, D_MODEL, N_IN), BF16), ((DEPTH, D_MODEL, D_MODEL), BF16), ((1, 4 * LR_ROWS, PACK_COLS), BF16)],
        _gather_plan((256, 256, LR_ROWS)))
    lowrank = [_unpack_lowrank(full_lr[0, LR_ROWS * j:LR_ROWS * (j + 1)]) for j in range(4)]
    full_uq = jnp.concatenate([s[0] for s in lowrank], axis=2)
    full_ukv = jnp.concatenate([s[1] for s in lowrank], axis=2)
    weights = [_layer_weights(full_in[l], full_out[l], full_uq[l], full_ukv[l]) for l in range(DEPTH)]
    small = [_layer_small(norm_g[l], a_q_norm_g[l], a_kv_norm_g[l], b_rel_bias[l], c_forget_b[l], final_g)
             for l in range(DEPTH)]

    c_all = _exchange(c[None], ALL_FLIPS, "gather", "gather_c").reshape(N_DEV * nb, D_MODEL)
    cols = w_ada.shape[2]
    b_cols = lax.dynamic_slice_in_dim(b_ada, chip * cols, cols, axis=1)[:, None, :]
    mod_cols = _ada_fwd(c_all, w_ada, b_cols)
    mod_g = _exchange(mod_cols.reshape(1, DEPTH * N_DEV * nb, cols), ALL_FLIPS, "gather", "gather_mod")
    mod_all = jnp.concatenate([mod_g[2 * j].reshape(DEPTH, N_DEV * nb, cols) for j in range(4)], axis=2)
    mod = lax.dynamic_slice_in_dim(mod_all, me * nb, nb, axis=1)

    tables = _rope_tables(positions)
    loss_part, dx, grads, gfinal, gw_in = _forward_backward(
        x.reshape(nb * seq, D_MODEL), mod, tables, loss_target.reshape(nb * seq, D_MODEL), weights, small, nb, seq)

    pay = _small_payload(grads, gfinal, loss_part)
    pay_all = _exchange(pay[None], ALL_FLIPS, "gather", "gather_small")
    tot = _sum_blocks(pay_all, "sum_small")
    loss = tot[PAY_LOSS[0], 0]
    dmod_all = pay_all[:, PAY_DMOD[0]:PAY_DMOD[1]].reshape(N_DEV, -1)[:, 0:DEPTH * nb * 3 * D_MODEL]
    dmod_all = dmod_all.reshape(N_DEV, DEPTH, nb, 3 * D_MODEL).transpose(1, 0, 2, 3)
    dmod_all = dmod_all.reshape(DEPTH, N_DEV * nb, 3 * D_MODEL)
    my_cols = lax.dynamic_slice_in_dim(dmod_all, chip * cols, cols, axis=2)
    g_w_ada, g_b_ada = _ada_bwd(c_all, my_cols, dmod_all)
    g_b_ada = g_b_ada[:, 0]

    g_uq = jnp.stack([_uq_from_padded(g["w_uq"]) for g in grads])
    g_ukv = jnp.stack([_ukv_from_padded(g["w_ukv"]) for g in grads])
    g_lr = jnp.concatenate([_pack_lowrank(g_uq[:, :, 144 * j:144 * (j + 1)], g_ukv[:, :, 192 * j:192 * (j + 1)])
                            for j in range(4)], axis=1)
    partials = [gw_in[0], gw_in[1], jnp.stack([_out_from_padded(g["w_out"]) for g in grads]), g_lr]
    shapes = [(p.shape[0], p.shape[2]) for p in partials]
    halves = [r // 2 for r in SHARD_ROWS]
    core_s = jnp.reshape(ic, (1,)).astype(jnp.int32)
    place_s = jnp.stack([chip, ic]).astype(jnp.int32)
    from_pair = _transfer("pair_reduce", partials, [((nl, 4 * h, nc), F32) for (nl, nc), h in zip(shapes, halves)],
                          _pair_reduce_plan(SHARD_ROWS))
    chip_sums = [_sum_pair(p, r, core_s, rc, "sum_pair%d" % i)
                 for i, (p, r, rc) in enumerate(zip(partials, from_pair, SHARD_ROWS))]
    from_chips = _transfer("chip_scatter", chip_sums, [((nl, 3 * h, nc), BF16) for (nl, nc), h in zip(shapes, halves)],
                           _chip_scatter_plan(SHARD_ROWS))
    reduced = [_sum_chips(s, r, place_s, rc, "sum_chips%d" % i)
               for i, (s, r, rc) in enumerate(zip(chip_sums, from_chips, SHARD_ROWS))]
    g_in_a, g_in_b, g_out_sh, g_lr_sh = _transfer("pair_share", reduced, [(r.shape, F32) for r in reduced],
                                                  _pair_share_plan(SHARD_ROWS), in_place=True)
    g_uq_sh, g_ukv_sh = _unpack_lowrank(g_lr_sh[0])
    g_in_sh = jnp.stack([_in_from_padded(jnp.concatenate([g_in_a[l], g_in_b[l]], axis=1)) for l in range(DEPTH)])
    gw = (g_in_sh, g_out_sh, g_uq_sh, g_ukv_sh)

    upd = [_adamw(w_in, gw[0], m_w_in, v_w_in, "adamw_in"), _adamw(w_out, gw[1], m_w_out, v_w_out, "adamw_out"),
           _adamw(a_w_uq, gw[2], m_a_w_uq, v_a_w_uq, "adamw_uq"), _adamw(a_w_ukv, gw[3], m_a_w_ukv, v_a_w_ukv, "adamw_ukv")]
    dw, mw, vw = (tuple(u[i] for u in upd) for i in range(3))
    d_ada, m_ada, v_ada = _adamw(w_ada, g_w_ada, m_w_ada, v_w_ada, "adamw_ada")
    d_b, m_b, v_b = (a.reshape(DEPTH, 3 * D_MODEL) for a in _adamw(
        _rows(b_ada, 48)[None], _rows(g_b_ada, 48)[None], _rows(m_b_ada, 48)[None], _rows(v_b_ada, 48)[None],
        "adamw_b_ada"))

    def small_rows(ng, gq, gkv, rb, fb, fg):
        per_layer = [dict(norm_g=ng[l], gq=gq[l], gkv=gkv[l], rb8=jnp.pad(rb[l], ((0, 3), (0, 384 - N_REL))),
                          fb=jnp.pad(fb[l], (0, LANE - 5))) for l in range(DEPTH)]
        return _small_payload(per_layer, fg, jnp.zeros((), F32))

    w_s = small_rows(norm_g, a_q_norm_g, a_kv_norm_g, b_rel_bias, c_forget_b, final_g)
    m_s = small_rows(m_norm_g, m_a_q_norm_g, m_a_kv_norm_g, m_b_rel_bias, m_c_forget_b, m_final_g)
    v_s = small_rows(v_norm_g, v_a_q_norm_g, v_a_kv_norm_g, v_b_rel_bias, v_c_forget_b, v_final_g)
    d_s, mo_s, vo_s = (a[0] for a in _adamw(w_s[None], tot[None], m_s[None], v_s[None], "adamw_small"))
    gs = _payload_split(tot)
    ds = _payload_split(d_s)
    ms = _payload_split(mo_s)
    vs = _payload_split(vo_s)

    def ordered(ada, b, sm, big):
        ng, gq, gkv, rb, fb, fg = sm
        b_in, b_out, b_uq, b_ukv = big
        return (ada, b, ng, b_in, gq, b_uq, gkv, b_ukv, rb, fb, b_out, fg)

    return (loss, dx.reshape(nb, seq, D_MODEL), *ordered(g_w_ada, g_b_ada, gs, gw), *ordered(d_ada, d_b, ds, dw),
            *ordered(m_ada, m_b, ms, mw), *ordered(v_ada, v_b, vs, vw))
```

```python
import functools

import jax
import jax.numpy as jnp
from jax import lax
from jax.experimental import pallas as pl
from jax.experimental.pallas import tpu as pltpu

F32 = jnp.float32
BF16 = jnp.bfloat16

D_MODEL = 1024
DEPTH = 2
EPS = 1e-6
NEG = -1e30
LOG2E = 1.4426950408889634
ROPE_THETA = 10000.0
A_ROPE = 32
A_Q_RANK = 384
A_KV_RANK = 256
REL_CLIP = 128
N_REL = 2 * REL_CLIP + 1
N_IN = 3621

ADAM_LR = 0.001
ADAM_B1 = 0.9
ADAM_B2 = 0.999
ADAM_EPS = 1e-08
ADAM_WD = 0.01
ADAM_STEP = 10

LANE = 128
VMEM_LIMIT = 56 * 1024 * 1024

NP_IN = 4352
Z_A = (0, 768)
Z_G = (768, 1920)
Z_QKV = tuple((1920 + 384 * i, 1920 + 384 * (i + 1)) for i in range(6))
Z_F = (4224, 4352)
IN_PIECES = ((0, 672, 96), (672, 1056, 0), (2016, 2336, 64), (3301, 3621, 64), (1056, 1376, 64), (1376, 1696, 64),
             (1696, 2016, 64), (2336, 2656, 64), (2656, 2976, 64), (2976, 3296, 64), (3296, 3301, 123))
D_CAT = 1152

TM = 512
T_CAUSAL = 256
T_BAND = 128
BAND_TILES = 5
N_DEV = 8

PAY_DMOD = (0, 96)
PAY_NORM = (96, 112)
PAY_GQ = (112, 120)
PAY_GKV = (120, 128)
PAY_RB = (128, 176)
PAY_FB = (176, 184)
PAY_FINAL = (184, 192)
PAY_LOSS = (192, 200)
PAY_ROWS = 200

PACK_COLS = 1024
PACK_ROWS = 2560
HALF_ROWS = PACK_ROWS // 2


def _params(sem=None):
    return pltpu.CompilerParams(dimension_semantics=sem, vmem_limit_bytes=VMEM_LIMIT)


def _lane_iota(shape):
    return lax.broadcasted_iota(jnp.int32, shape, len(shape) - 1)


def _silu(u):
    return u * jax.nn.sigmoid(u)


def _dsilu(u):
    s = jax.nn.sigmoid(u)
    return s * (1.0 + u * (1.0 - s))


def _rms(x, g):
    r = lax.rsqrt(jnp.mean(x * x, axis=-1, keepdims=True) + EPS)
    xh = x * r
    return xh * g, xh, r


def _rms_bwd(dy, xh, r, g):
    dxh = dy * g
    return r * (dxh - xh * jnp.mean(dxh * xh, axis=-1, keepdims=True))


def _rope(x, cos, sina, sinb):
    return x * cos + pltpu.roll(x, 16, 1) * sinb + pltpu.roll(x, LANE - 16, 1) * sina


def _rope_t(dy, cos, sina, sinb):
    return dy * cos + pltpu.roll(dy * sinb, LANE - 16, 1) + pltpu.roll(dy * sina, 16, 1)


def _split3(x):
    hi = x.astype(BF16)
    r1 = x - hi.astype(F32)
    mid = r1.astype(BF16)
    lo = (r1 - mid.astype(F32)).astype(BF16)
    return hi, mid, lo


def _dot(a, b):
    return jnp.dot(a, b, preferred_element_type=F32)


def _dot_nt(a, b):
    return lax.dot_general(a, b, (((1,), (1,)), ((), ())), preferred_element_type=F32)


def _dot_tn(a, b):
    return lax.dot_general(a, b, (((0,), (0,)), ((), ())), preferred_element_type=F32)


def _row_spec(cols):
    return pl.BlockSpec((TM, cols), lambda i: (i, 0))


def _full_spec(shape):
    return pl.BlockSpec(shape, lambda i: (0,) * len(shape))


def _ex_spec(tiles_per_ex):
    return pl.BlockSpec((1, 1, D_MODEL), lambda i: (i // tiles_per_ex, 0, 0))


def _ln_inproj(x, shift, scale, g, w_in_p, seq):
    t = x.shape[0]

    def body(x_ref, sh_ref, sc_ref, g_ref, w_ref, h_ref, za_ref, zg_ref, q0, q1, q2, q3, q4, q5, zf_ref):
        n, _, _ = _rms(x_ref[...], g_ref[...])
        h = (n * (1.0 + sc_ref[0]) + sh_ref[0]).astype(BF16)
        h_ref[...] = h
        za_ref[...] = _dot(h, w_ref[:, Z_A[0]:Z_A[1]])
        zg_ref[...] = _dot(h, w_ref[:, Z_G[0]:Z_G[1]])
        for ref, (c0, c1) in zip((q0, q1, q2, q3, q4, q5), Z_QKV):
            ref[...] = _dot(h, w_ref[:, c0:c1]).astype(BF16)
        zf_ref[...] = _dot(h, w_ref[:, Z_F[0]:Z_F[1]])

    tpe = seq // TM
    shapes = [jax.ShapeDtypeStruct((t, D_MODEL), BF16), jax.ShapeDtypeStruct((t, 768), F32),
              jax.ShapeDtypeStruct((t, D_CAT), F32)]
    shapes += [jax.ShapeDtypeStruct((t, 384), BF16)] * 6 + [jax.ShapeDtypeStruct((t, LANE), F32)]
    return pl.pallas_call(
        body, name="ln_inproj", grid=(t // TM,),
        in_specs=[_row_spec(D_MODEL), _ex_spec(tpe), _ex_spec(tpe), _full_spec((1, D_MODEL)),
                  _full_spec((D_MODEL, NP_IN))],
        out_specs=[_row_spec(D_MODEL), _row_spec(768), _row_spec(D_CAT)] + [_row_spec(384)] * 6 + [_row_spec(LANE)],
        out_shape=shapes, compiler_params=_params(("parallel",)),
    )(x, shift, scale, g, w_in_p)


def _a_up(za, gq, gkv, w_uq_p, w_ukv_p, cos, sina, sinb):
    t = za.shape[0]

    def body(za_ref, gq_ref, gkv_ref, wq_ref, wkv_ref, cos_ref, sa_ref, sb_ref, q_ref, k_ref, v_ref):
        cos_t, sa, sb = cos_ref[...], sa_ref[...], sb_ref[...]
        cqn, _, _ = _rms(za_ref[:, 0:384], gq_ref[...])
        q = _dot(cqn.astype(BF16), wq_ref[...])
        ckvn, _, _ = _rms(za_ref[:, 384:640], gkv_ref[...])
        kv = _dot(ckvn.astype(BF16), wkv_ref[...])
        kpe = za_ref[:, 640:768]
        kpe = _rope(kpe + pltpu.roll(kpe, 32, 1), cos_t, sa, sb).astype(BF16)
        for p in range(3):
            q_ref[:, 256 * p:256 * p + 128] = q[:, 256 * p:256 * p + 128].astype(BF16)
            q_ref[:, 256 * p + 128:256 * p + 256] = _rope(q[:, 256 * p + 128:256 * p + 256], cos_t, sa, sb).astype(BF16)
            k_ref[:, 256 * p:256 * p + 128] = kv[:, 128 * p:128 * p + 128].astype(BF16)
            k_ref[:, 256 * p + 128:256 * p + 256] = kpe
        v_ref[...] = kv[:, 384:768].astype(BF16)

    return pl.pallas_call(
        body, name="a_up", grid=(t // TM,),
        in_specs=[_row_spec(768), _full_spec((1, 384)), _full_spec((1, 256)), _full_spec((384, 768)),
                  _full_spec((256, 768)), _row_spec(LANE), _row_spec(LANE), _row_spec(LANE)],
        out_specs=[_row_spec(768), _row_spec(768), _row_spec(384)],
        out_shape=[jax.ShapeDtypeStruct((t, 768), BF16), jax.ShapeDtypeStruct((t, 768), BF16),
                   jax.ShapeDtypeStruct((t, 384), BF16)],
        compiler_params=_params(("parallel",)),
    )(za, gq, gkv, w_uq_p, w_ukv_p, cos, sina, sinb)


def _tri(n, upper):
    r = lax.broadcasted_iota(jnp.int32, (n, n), 0)
    c = lax.broadcasted_iota(jnp.int32, (n, n), 1)
    return jnp.where((c >= r) if upper else (c <= r), 1.0, 0.0).astype(BF16)


def _forget_fwd(zf, fb, nb, seq):
    blk = 256

    def body(zf_ref, fb_ref, f_ref):
        tri = _tri(blk, False)
        live = _lane_iota((blk, LANE)) < 5
        carry = jnp.zeros((1, LANE), F32)
        for i in range(seq // blk):
            u = zf_ref[i * blk:(i + 1) * blk, :] + fb_ref[...]
            lf = jnp.where(live, jnp.minimum(u, 0.0) - jnp.log(1.0 + jnp.exp(-jnp.abs(u))), 0.0)
            hi, mid, lo = _split3(lf)
            f_ref[i * blk:(i + 1) * blk, :] = (_dot(tri, hi) + _dot(tri, mid) + _dot(tri, lo) + carry) * LOG2E
            carry = carry + jnp.sum(lf, axis=0, keepdims=True)

    return pl.pallas_call(
        body, name="forget_fwd", grid=(nb,),
        in_specs=[pl.BlockSpec((seq, LANE), lambda b: (b, 0)), pl.BlockSpec((1, LANE), lambda b: (0, 0))],
        out_specs=pl.BlockSpec((seq, LANE), lambda b: (b, 0)),
        out_shape=jax.ShapeDtypeStruct((nb * seq, LANE), F32), compiler_params=_params(("parallel",)),
    )(zf, fb)


def _forget_bwd(dfq, dfk, zf, fb, nb, seq):
    blk = 256

    def body(dfq_ref, dfk_ref, zf_ref, fb_ref, dz_ref, gb_ref):
        @pl.when(pl.program_id(0) == 0)
        def _():
            gb_ref[...] = jnp.zeros_like(gb_ref)

        tri = _tri(blk, True)
        lane = _lane_iota((blk, LANE))
        wide = _lane_iota((blk, 384))
        live = lane < 5
        carry = jnp.zeros((1, LANE), F32)
        gsum = jnp.zeros((1, LANE), F32)
        for i in reversed(range(seq // blk)):
            d = dfk_ref[i * blk:(i + 1) * blk, :]
            dq = dfq_ref[i * blk:(i + 1) * blk, :]
            for hd in range(5):
                col = jnp.sum(jnp.where(wide == 64 * hd, dq, 0.0), axis=-1, keepdims=True)
                d = d + jnp.where(lane == hd, col, 0.0)
            hi, mid, lo = _split3(d)
            dlf = _dot(tri, hi) + _dot(tri, mid) + _dot(tri, lo) + carry
            carry = carry + jnp.sum(d, axis=0, keepdims=True)
            u = zf_ref[i * blk:(i + 1) * blk, :] + fb_ref[...]
            du = jnp.where(live, dlf * jax.nn.sigmoid(-u), 0.0)
            dz_ref[i * blk:(i + 1) * blk, :] = du.astype(BF16)
            gsum = gsum + jnp.sum(du, axis=0, keepdims=True)
        gb_ref[...] += jnp.broadcast_to(gsum, gb_ref.shape)

    return pl.pallas_call(
        body, name="forget_bwd", grid=(nb,),
        in_specs=[pl.BlockSpec((seq, 384), lambda b: (b, 0)), pl.BlockSpec((seq, LANE), lambda b: (b, 0)),
                  pl.BlockSpec((seq, LANE), lambda b: (b, 0)), pl.BlockSpec((1, LANE), lambda b: (0, 0))],
        out_specs=[pl.BlockSpec((seq, LANE), lambda b: (b, 0)), pl.BlockSpec((8, LANE), lambda b: (0, 0))],
        out_shape=[jax.ShapeDtypeStruct((nb * seq, LANE), BF16), jax.ShapeDtypeStruct((8, LANE), F32)],
        compiler_params=_params(("arbitrary",)),
    )(dfq, dfk, zf, fb)


def _gate_outproj(x, gate, oa, ob, oc, zg, w_out_p, seq):
    t = x.shape[0]

    def body(x_ref, gate_ref, oa_ref, ob_ref, oc_ref, zg_ref, w_ref, y_ref, xn_ref):
        y = jnp.zeros((TM, D_MODEL), F32)
        for i, o_ref in enumerate((oa_ref, ob_ref, oc_ref)):
            cat = (o_ref[...] * _silu(zg_ref[:, 384 * i:384 * (i + 1)])).astype(BF16)
            y = y + _dot(cat, w_ref[384 * i:384 * (i + 1), :])
        y_ref[...] = y
        xn_ref[...] = x_ref[...] + gate_ref[0] * y

    return pl.pallas_call(
        body, name="gate_outproj", grid=(t // TM,),
        in_specs=[_row_spec(D_MODEL), _ex_spec(seq // TM), _row_spec(384), _row_spec(384), _row_spec(384),
                  _row_spec(D_CAT), _full_spec((D_CAT, D_MODEL))],
        out_specs=[_row_spec(D_MODEL), _row_spec(D_MODEL)],
        out_shape=[jax.ShapeDtypeStruct((t, D_MODEL), F32)] * 2, compiler_params=_params(("parallel",)),
    )(x, gate, oa, ob, oc, zg, w_out_p)


def _final_loss(x, target, g):
    t = x.shape[0]

    def body(x_ref, t_ref, g_ref, dx_ref, loss_ref, gg_ref):
        @pl.when(pl.program_id(0) == 0)
        def _():
            loss_ref[...] = jnp.zeros_like(loss_ref)
            gg_ref[...] = jnp.zeros_like(gg_ref)

        gv = g_ref[...]
        out, xh, r = _rms(x_ref[...], gv)
        err = out - t_ref[...]
        loss_ref[...] += 0.5 * jnp.sum(jnp.mean(err * err, axis=-1, keepdims=True), axis=0, keepdims=True)
        dout = err / D_MODEL
        gg_ref[...] += jnp.broadcast_to(jnp.sum(dout * xh, axis=0, keepdims=True), gg_ref.shape)
        dx_ref[...] = _rms_bwd(dout, xh, r, gv)

    return pl.pallas_call(
        body, name="final_loss", grid=(t // TM,),
        in_specs=[_row_spec(D_MODEL), _row_spec(D_MODEL), _full_spec((1, D_MODEL))],
        out_specs=[_row_spec(D_MODEL), _full_spec((8, LANE)), _full_spec((8, D_MODEL))],
        out_shape=[jax.ShapeDtypeStruct((t, D_MODEL), F32), jax.ShapeDtypeStruct((8, LANE), F32),
                   jax.ShapeDtypeStruct((8, D_MODEL), F32)],
        compiler_params=_params(("arbitrary",)),
    )(x, target, g)


def _head_masks(kind, rows, dq, h):
    lq = _lane_iota((rows, dq))
    lv = _lane_iota((rows, LANE))
    mq = (lq >= 64 * h) & (lq < 64 * h + 64)
    if kind == "A":
        mq = mq | ((lq >= 128 + 32 * h) & (lq < 160 + 32 * h))
    return mq, (lv >= 64 * h) & (lv < 64 * h + 64)


def _tile_mask(kind, tile, m=None):
    row = lax.broadcasted_iota(jnp.int32, (tile, tile), 0)
    col = lax.broadcasted_iota(jnp.int32, (tile, tile), 1)
    if kind == "A":
        return (col >> 6) <= (row >> 6)
    if kind == "C":
        return col <= row
    first = (m == 0) & (row >= 64) & (col < 64)
    last = (m == BAND_TILES - 1) & (row < 64) & (col >= 64)
    return jnp.logical_not(first | last)


def _attn_scale(kind):
    return 96.0 ** -0.5 if kind == "A" else 0.125


def _attn_fwd(kind, q, k, v, aux, nb, seq):
    dq = q.shape[1] // 3
    tile = T_BAND if kind == "B" else T_CAUSAL
    nq = seq // tile
    scale = _attn_scale(kind)

    def body(*refs):
        if kind == "A":
            q_ref, k_ref, v_ref, o_ref, lse_ref = refs
            aux_ref = None
        else:
            q_ref, k_ref, v_ref, aux_ref, o_ref, lse_ref = refs
        qi = pl.program_id(2)
        q2 = q_ref[...]
        res = []
        for h in range(2):
            mq, _ = _head_masks(kind, tile, dq, h)
            qh = jnp.where(mq, q2, jnp.zeros_like(q2))

            def step(kj, carry, m=None, diag=False):
                mx, l, acc = carry
                ks = pl.ds(pl.multiple_of(kj * tile, tile), tile)
                kt = k_ref[ks, :]
                vt = v_ref[ks, :]
                s = _dot_nt(qh, kt) * scale
                if kind == "B":
                    s = jnp.where(_tile_mask("B", tile, m), s + aux_ref[h, m], NEG)
                if kind == "C":
                    s = s - aux_ref[0, h, pl.ds(kj, 1), :]
                if diag:
                    s = jnp.where(_tile_mask(kind, tile), s, NEG)
                mn = jnp.maximum(mx, jnp.max(s, axis=-1, keepdims=True))
                alpha = jnp.exp(mx - mn)
                p = jnp.exp(s - mn)
                l = alpha * l + jnp.sum(p, axis=-1, keepdims=True)
                acc = alpha * acc + _dot(p.astype(BF16), vt)
                return mn, l, acc

            init = (jnp.full((tile, 1), NEG, F32), jnp.zeros((tile, 1), F32), jnp.zeros((tile, LANE), F32))
            if kind == "B":
                m0 = jnp.maximum(BAND_TILES - 1 - qi, 0)
                mx, l, acc = lax.fori_loop(m0, BAND_TILES, lambda m, c: step(qi - (BAND_TILES - 1) + m, c, m=m), init)
            else:
                carry = lax.fori_loop(0, qi, lambda kj, c: step(kj, c), init)
                mx, l, acc = step(qi, carry, diag=True)
            res.append((acc / l, mx + jnp.log(l)))
        first = _lane_iota((tile, LANE)) < 64
        o_ref[...] = jnp.where(first, res[0][0], res[1][0])
        lse_ref[...] = jnp.where(first, res[0][1], res[1][1])

    in_specs = [pl.BlockSpec((tile, dq), lambda b, p, i: (b * nq + i, p)),
                pl.BlockSpec((seq, dq), lambda b, p, i: (b, p)),
                pl.BlockSpec((seq, LANE), lambda b, p, i: (b, p))]
    args = [q, k, v]
    if kind == "B":
        in_specs.append(pl.BlockSpec((2, BAND_TILES, tile, tile), lambda b, p, i: (p, 0, 0, 0)))
        args.append(aux)
    if kind == "C":
        in_specs.append(pl.BlockSpec((1, 2, nq, tile), lambda b, p, i: (b, p, 0, 0)))
        args.append(aux)
    out_spec = pl.BlockSpec((tile, LANE), lambda b, p, i: (b * nq + i, p))
    return pl.pallas_call(
        body, name="attn_fwd_" + kind, grid=(nb, 3, nq), in_specs=in_specs, out_specs=[out_spec, out_spec],
        out_shape=[jax.ShapeDtypeStruct((nb * seq, 384), F32)] * 2,
        compiler_params=_params(("parallel", "parallel", "parallel")),
    )(*args)


def _attn_bwd(kind, q, k, v, o, do, lse, aux, nb, seq):
    dq = q.shape[1] // 3
    tile = T_BAND if kind == "B" else T_CAUSAL
    nq = seq // tile
    scale = _attn_scale(kind)
    dqk_dtype = F32 if kind == "A" else BF16

    def body(*refs):
        dfr_ref = dfq_ref = dbt_ref = aux_ref = None
        if kind == "A":
            q_ref, k_ref, v_ref, o_ref, do_ref, lse_ref, dq_ref, dk_ref, dv_ref, dk_acc, dv_acc = refs
        elif kind == "B":
            q_ref, k_ref, v_ref, o_ref, do_ref, lse_ref, aux_ref, dq_ref, dk_ref, dv_ref, dbt_ref, dk_acc, dv_acc = refs
        else:
            (q_ref, k_ref, v_ref, o_ref, do_ref, lse_ref, aux_ref, dq_ref, dk_ref, dv_ref, dfr_ref, dfq_ref,
             dk_acc, dv_acc) = refs
        dk_acc[...] = jnp.zeros_like(dk_acc)
        dv_acc[...] = jnp.zeros_like(dv_acc)
        if kind == "C":
            dfr_ref[...] = jnp.zeros_like(dfr_ref)
        if kind == "B":
            @pl.when(pl.program_id(1) == 0)
            def _():
                dbt_ref[...] = jnp.zeros_like(dbt_ref)

        def q_step(qi, _):
            qs = pl.ds(pl.multiple_of(qi * tile, tile), tile)
            q2 = q_ref[qs, :]
            do2 = do_ref[qs, :]
            o2 = o_ref[qs, :]
            lse2 = lse_ref[qs, :]
            dq_tot = jnp.zeros((tile, dq), F32)
            row_sums = []
            for h in range(2):
                mq, mv = _head_masks(kind, tile, dq, h)
                qh = jnp.where(mq, q2, jnp.zeros_like(q2))
                doh = jnp.where(mv, do2, 0.0)
                dob = doh.astype(BF16)
                delta = jnp.sum(doh * o2, axis=-1, keepdims=True)
                lseh = jnp.max(jnp.where(mv, lse2, NEG), axis=-1, keepdims=True)

                def step(kj, carry, m=None, diag=False):
                    dq_t, rs = carry
                    ks = pl.ds(pl.multiple_of(kj * tile, tile), tile)
                    kt = k_ref[ks, :]
                    vt = v_ref[ks, :]
                    s = _dot_nt(qh, kt) * scale
                    if kind == "B":
                        s = jnp.where(_tile_mask("B", tile, m), s + aux_ref[h, m], NEG)
                    if kind == "C":
                        s = s - aux_ref[0, h, pl.ds(kj, 1), :]
                    if diag:
                        s = jnp.where(_tile_mask(kind, tile), s, NEG)
                    p = jnp.exp(s - lseh)
                    ds = p * (_dot_nt(dob, vt) - delta)
                    if kind == "B":
                        dbt_ref[h, m] += ds
                    if kind == "C":
                        dfr_ref[0, h, pl.ds(kj, 1), :] -= jnp.sum(ds, axis=0, keepdims=True)
                        rs = rs + jnp.sum(ds, axis=-1, keepdims=True)
                    dss = (ds * scale).astype(BF16)
                    dv_acc[ks, :] += _dot_tn(p.astype(BF16), dob)
                    dk_acc[ks, :] += _dot_tn(dss, qh)
                    return dq_t + jnp.where(mq, _dot(dss, kt), 0.0), rs

                zero = (jnp.zeros((tile, dq), F32), jnp.zeros((tile, 1), F32))
                if kind == "B":
                    m0 = jnp.maximum(BAND_TILES - 1 - qi, 0)
                    dq_h, rs_h = lax.fori_loop(m0, BAND_TILES, lambda m, c: step(qi - (BAND_TILES - 1) + m, c, m=m), zero)
                else:
                    carry = lax.fori_loop(0, qi, lambda kj, c: step(kj, c), zero)
                    dq_h, rs_h = step(qi, carry, diag=True)
                dq_tot = dq_tot + dq_h
                row_sums.append(rs_h)
            dq_ref[qs, :] = dq_tot.astype(dqk_dtype)
            if kind == "C":
                dfq_ref[qs, :] = jnp.where(_lane_iota((tile, LANE)) < 64, row_sums[0], row_sums[1])
            return 0

        lax.fori_loop(0, nq, q_step, 0)
        dk_ref[...] = dk_acc[...].astype(dqk_dtype)
        dv_ref[...] = dv_acc[...].astype(BF16)

    def seq_spec(cols):
        return pl.BlockSpec((seq, cols), lambda p, b: (b, p))

    in_specs = [seq_spec(dq), seq_spec(dq), seq_spec(LANE), seq_spec(LANE), seq_spec(LANE), seq_spec(LANE)]
    args = [q, k, v, o, do, lse]
    out_specs = [seq_spec(dq), seq_spec(dq), seq_spec(LANE)]
    out_shape = [jax.ShapeDtypeStruct((nb * seq, 3 * dq), dqk_dtype)] * 2 + [jax.ShapeDtypeStruct((nb * seq, 384), BF16)]
    if kind == "B":
        spec = pl.BlockSpec((2, BAND_TILES, tile, tile), lambda p, b: (p, 0, 0, 0))
        in_specs.append(spec)
        args.append(aux)
        out_specs.append(spec)
        out_shape.append(jax.ShapeDtypeStruct((6, BAND_TILES, tile, tile), F32))
    if kind == "C":
        spec = pl.BlockSpec((1, 2, nq, tile), lambda p, b: (b, p, 0, 0))
        in_specs.append(spec)
        args.append(aux)
        out_specs += [spec, seq_spec(LANE)]
        out_shape += [jax.ShapeDtypeStruct((nb, 6, nq, tile), F32), jax.ShapeDtypeStruct((nb * seq, 384), F32)]
    return pl.pallas_call(
        body, name="attn_bwd_" + kind, grid=(3, nb), in_specs=in_specs, out_specs=out_specs, out_shape=out_shape,
        scratch_shapes=[pltpu.VMEM((seq, dq), F32), pltpu.VMEM((seq, LANE), F32)],
        compiler_params=_params(("arbitrary", "arbitrary")),
    )(*args)


BAND_W = BAND_TILES * T_BAND


def _segments(kind, qi, tile):
    r0 = qi * tile
    if kind == "B":
        lo = max(qi - (BAND_TILES - 1), 0) * tile
        return [(lo, r0 + tile, False, lo - (qi - (BAND_TILES - 1)) * tile)]
    return ([(0, r0, False, 0)] if qi else []) + [(r0, r0 + tile, True, 0)]


def _scores(kind, qh, k_ref, aux_ref, h, seg, tile, scale):
    a, b, diag, c0 = seg
    s = _dot_nt(qh, k_ref[a:b, :]) * (scale * LOG2E)
    if kind == "B":
        return s + aux_ref[h, :, c0:BAND_W]
    if kind == "C":
        s = s - aux_ref[0, h, :, a:b]
    if diag:
        s = jnp.where(_tile_mask(kind, tile), s, NEG)
    return s


def _second_head(kind, pair, fn):
    del kind, pair
    fn()


def _attn_fwd(kind, q, k, v, aux, nb, seq):
    dq = q.shape[1] // 3
    tile = T_BAND if kind == "B" else T_CAUSAL
    nq = seq // tile
    scale = _attn_scale(kind)

    def body(*refs):
        if kind == "A":
            q_ref, k_ref, v_ref, o_ref, lse_ref = refs
            aux_ref = None
        else:
            q_ref, k_ref, v_ref, aux_ref, o_ref, lse_ref = refs
        pair = pl.program_id(1)
        for qi in range(nq):
            rows = slice(qi * tile, (qi + 1) * tile)
            segs = _segments(kind, qi, tile)

            def head(h, rows=rows, segs=segs):
                q2 = q_ref[rows, :]
                mq, mv = _head_masks(kind, tile, dq, h)
                qh = jnp.where(mq, q2, jnp.zeros_like(q2))
                ss = [_scores(kind, qh, k_ref, aux_ref, h, seg, tile, scale) for seg in segs]
                mx = functools.reduce(jnp.maximum, [jnp.max(s, axis=-1, keepdims=True) for s in ss])
                ps = [jnp.exp2(s - mx) for s in ss]
                l = functools.reduce(jnp.add, [jnp.sum(p, axis=-1, keepdims=True) for p in ps])
                acc = functools.reduce(jnp.add, [_dot(p.astype(BF16), v_ref[seg[0]:seg[1], :]) for p, seg in zip(ps, segs)])
                o_h = jnp.where(mv, acc / l, 0.0)
                lse_h = jnp.where(mv, mx + jnp.log(l) * LOG2E, 0.0)
                if h == 0:
                    o_ref[rows, :] = o_h
                    lse_ref[rows, :] = lse_h
                else:
                    o_ref[rows, :] += o_h
                    lse_ref[rows, :] += lse_h

            head(0)
            _second_head(kind, pair, functools.partial(head, 1))

    def seq_spec(cols):
        return pl.BlockSpec((seq, cols), lambda b, p: (b, p))

    in_specs = [seq_spec(dq), seq_spec(dq), seq_spec(LANE)]
    args = [q, k, v]
    if kind == "B":
        in_specs.append(pl.BlockSpec((2, tile, BAND_W), lambda b, p: (p, 0, 0)))
        args.append(aux)
    if kind == "C":
        in_specs.append(pl.BlockSpec((1, 2, 1, seq), lambda b, p: (b, p, 0, 0)))
        args.append(aux)
    return pl.pallas_call(
        body, name="attn_fwd_" + kind, grid=(nb, 3), in_specs=in_specs, out_specs=[seq_spec(LANE), seq_spec(LANE)],
        out_shape=[jax.ShapeDtypeStruct((nb * seq, 384), F32)] * 2, compiler_params=_params(("parallel", "parallel")),
    )(*args)


def _attn_bwd(kind, q, k, v, o, do, lse, aux, nb, seq):
    dq = q.shape[1] // 3
    tile = T_BAND if kind == "B" else T_CAUSAL
    nq = seq // tile
    scale = _attn_scale(kind)
    dqk_dtype = F32 if kind == "A" else BF16

    def body(*refs):
        dfr_ref = dfq_ref = dbt_ref = aux_ref = None
        if kind == "A":
            q_ref, k_ref, v_ref, o_ref, do_ref, lse_ref, dq_ref, dk_ref, dv_ref, dkt_acc, dvt_acc = refs
        elif kind == "B":
            q_ref, k_ref, v_ref, o_ref, do_ref, lse_ref, aux_ref, dq_ref, dk_ref, dv_ref, dbt_ref, dkt_acc, dvt_acc = refs
        else:
            (q_ref, k_ref, v_ref, o_ref, do_ref, lse_ref, aux_ref, dq_ref, dk_ref, dv_ref, dfr_ref, dfq_ref,
             dkt_acc, dvt_acc) = refs
        dkt_acc[...] = jnp.zeros_like(dkt_acc)
        dvt_acc[...] = jnp.zeros_like(dvt_acc)
        if kind == "C":
            dfr_ref[...] = jnp.zeros_like(dfr_ref)
        if kind == "B":
            @pl.when(pl.program_id(1) == 0)
            def _():
                dbt_ref[...] = jnp.zeros_like(dbt_ref)

        pair = pl.program_id(0)
        for qi in range(nq):
            rows = slice(qi * tile, (qi + 1) * tile)
            segs = _segments(kind, qi, tile)

            def head(h, rows=rows, segs=segs):
                q2 = q_ref[rows, :]
                do2 = do_ref[rows, :]
                mq, mv = _head_masks(kind, tile, dq, h)
                qh = jnp.where(mq, q2, jnp.zeros_like(q2))
                doh = jnp.where(mv, do2, 0.0)
                dob = doh.astype(BF16)
                qht = qh.astype(F32).T.astype(BF16)
                dobt = doh.T.astype(BF16)
                delta = jnp.sum(doh * o_ref[rows, :], axis=-1, keepdims=True)
                lseh = jnp.max(jnp.where(mv, lse_ref[rows, :], NEG), axis=-1, keepdims=True)
                rs = jnp.zeros((tile, 1), F32)
                dq_h = jnp.zeros((tile, dq), F32)
                for seg in segs:
                    a, b, _, c0 = seg
                    p = jnp.exp2(_scores(kind, qh, k_ref, aux_ref, h, seg, tile, scale) - lseh)
                    ds = p * (_dot_nt(dob, v_ref[a:b, :]) - delta)
                    if kind == "B":
                        dbt_ref[h, :, c0:BAND_W] += ds
                    if kind == "C":
                        dfr_ref[0, h, :, a:b] -= jnp.sum(ds, axis=0, keepdims=True)
                        rs = rs + jnp.sum(ds, axis=-1, keepdims=True)
                    dss = (ds * scale).astype(BF16)
                    dvt_acc[:, a:b] += _dot(dobt, p.astype(BF16))
                    dkt_acc[:, a:b] += _dot(qht, dss)
                    dq_h = dq_h + _dot(dss, k_ref[a:b, :])
                dq_h = jnp.where(mq, dq_h, 0.0).astype(dqk_dtype)
                if h == 0:
                    dq_ref[rows, :] = dq_h
                else:
                    dq_ref[rows, :] += dq_h
                if kind == "C":
                    if h == 0:
                        dfq_ref[rows, :] = jnp.where(mv, rs, 0.0)
                    else:
                        dfq_ref[rows, :] += jnp.where(mv, rs, 0.0)

            head(0)
            _second_head(kind, pair, functools.partial(head, 1))
        for j in range(seq // 256):
            cols = slice(256 * j, 256 * (j + 1))
            dk_ref[cols, :] = dkt_acc[:, cols].T.astype(dqk_dtype)
            dv_ref[cols, :] = dvt_acc[:, cols].T.astype(BF16)

    def seq_spec(cols):
        return pl.BlockSpec((seq, cols), lambda p, b: (b, p))

    in_specs = [seq_spec(dq), seq_spec(dq), seq_spec(LANE), seq_spec(LANE), seq_spec(LANE), seq_spec(LANE)]
    args = [q, k, v, o, do, lse]
    out_specs = [seq_spec(dq), seq_spec(dq), seq_spec(LANE)]
    out_shape = [jax.ShapeDtypeStruct((nb * seq, 3 * dq), dqk_dtype)] * 2 + [jax.ShapeDtypeStruct((nb * seq, 384), BF16)]
    if kind == "B":
        spec = pl.BlockSpec((2, tile, BAND_W), lambda p, b: (p, 0, 0))
        in_specs.append(spec)
        args.append(aux)
        out_specs.append(spec)
        out_shape.append(jax.ShapeDtypeStruct((6, tile, BAND_W), F32))
    if kind == "C":
        spec = pl.BlockSpec((1, 2, 1, seq), lambda p, b: (b, p, 0, 0))
        in_specs.append(spec)
        args.append(aux)
        out_specs += [spec, seq_spec(LANE)]
        out_shape += [jax.ShapeDtypeStruct((nb, 6, 1, seq), F32), jax.ShapeDtypeStruct((nb * seq, 384), F32)]
    return pl.pallas_call(
        body, name="attn_bwd_" + kind, grid=(3, nb), in_specs=in_specs, out_specs=out_specs, out_shape=out_shape,
        scratch_shapes=[pltpu.VMEM((dq, seq), F32), pltpu.VMEM((LANE, seq), F32)],
        compiler_params=_params(("arbitrary", "arbitrary")),
    )(*args)


BIAS_FLAT = T_BAND * BAND_W
BIAS_CHUNK = 4 * BAND_W


def _rel_onehot(chunk):
    lane = lax.broadcasted_iota(jnp.int32, (384, chunk), 1)
    r = lax.broadcasted_iota(jnp.int32, (384, chunk), 0)
    sub = jnp.where(lane >= BAND_W, 1, 0) + jnp.where(lane >= 2 * BAND_W, 1, 0) + jnp.where(lane >= 3 * BAND_W, 1, 0)
    i = pl.program_id(0) * 4 + sub
    col = lane - sub * BAND_W
    idx = jnp.clip(BAND_W - T_BAND + i - col, -REL_CLIP, REL_CLIP) + REL_CLIP
    return jnp.where(idx == r, 1.0, 0.0).astype(BF16)


BIAS_G = 768
BIAS_EDGE = BIAS_G - N_REL


def _bias_line(rel_bias):
    g = jnp.concatenate([jnp.broadcast_to(rel_bias[:, N_REL - 1:], (rel_bias.shape[0], BIAS_EDGE)),
                         jnp.flip(rel_bias, axis=1)], axis=1)
    return jnp.pad(g, ((0, 8 - g.shape[0]), (0, 0)))


def _bias_unline(dg):
    return jnp.flip(dg[:, BIAS_EDGE:], axis=1)


def _bias_expand(g8):
    def body(g_ref, out_ref):
        line = jnp.broadcast_to(g_ref[0] * LOG2E, (T_BAND, BIAS_G))
        slab = pltpu.roll(line, 1, 1, stride=1, stride_axis=0)[:, LANE:BIAS_G]
        row = lax.broadcasted_iota(jnp.int32, (T_BAND, BAND_W), 0)
        col = lax.broadcasted_iota(jnp.int32, (T_BAND, BAND_W), 1)
        hidden = ((row >= 64) & (col < 64)) | ((row < 64) & (col >= BAND_W - 64))
        out_ref[0] = jnp.where(hidden, NEG, slab)

    return pl.pallas_call(
        body, name="bias_expand", grid=(8,), in_specs=[pl.BlockSpec((1, 1, BIAS_G), lambda h: (h, 0, 0))],
        out_specs=pl.BlockSpec((1, T_BAND, BAND_W), lambda h: (h, 0, 0)),
        out_shape=jax.ShapeDtypeStruct((8, T_BAND, BAND_W), F32), compiler_params=_params(("parallel",)),
    )(g8.reshape(8, 1, BIAS_G))


def _bias_reduce(d_slab):
    def body(d_ref, out_ref):
        r = lax.broadcasted_iota(jnp.int32, (T_BAND, T_BAND), 0)
        k = lax.broadcasted_iota(jnp.int32, (T_BAND, T_BAND), 1)
        flip = jnp.where(r + k == T_BAND - 1, 1.0, 0.0).astype(BF16)
        hi, mid, lo = _split3(d_ref[0])
        d_rev = _dot(flip, hi) + _dot(flip, mid) + _dot(flip, lo)
        wide = jnp.concatenate([jnp.zeros((T_BAND, LANE), F32), d_rev, jnp.zeros((T_BAND, 2 * LANE), F32)], axis=1)
        skew = pltpu.roll(wide, 0, 1, stride=1, stride_axis=0)
        dg = jnp.sum(skew, axis=0, keepdims=True)[:, LANE:LANE + BIAS_G]
        lane = _lane_iota((1, BIAS_G))
        clipped = jnp.sum(jnp.where(lane <= BIAS_EDGE, dg, 0.0), axis=1, keepdims=True)
        out_ref[0] = jnp.where(lane == BIAS_EDGE, clipped, dg)

    return pl.pallas_call(
        body, name="bias_reduce", grid=(8,), in_specs=[pl.BlockSpec((1, T_BAND, BAND_W), lambda h: (h, 0, 0))],
        out_specs=pl.BlockSpec((1, 1, BIAS_G), lambda h: (h, 0, 0)),
        out_shape=jax.ShapeDtypeStruct((8, 1, BIAS_G), F32), compiler_params=_params(("parallel",)),
    )(d_slab).reshape(8, BIAS_G)


def _outproj_bwd(dxn, y, gate, oa, ob, oc, zg, w_out_p, w_out_pt, nb, seq):
    t = dxn.shape[0]
    tpe = seq // TM

    def body(dxn_ref, y_ref, gate_ref, oa_ref, ob_ref, oc_ref, zg_ref, w_ref, wt_ref,
             doa_ref, dob_ref, doc_ref, dzg_ref, gw_ref, dgate_ref):
        i = pl.program_id(0)

        @pl.when(i == 0)
        def _():
            gw_ref[...] = jnp.zeros_like(gw_ref)

        @pl.when(i % tpe == 0)
        def _():
            dgate_ref[...] = jnp.zeros_like(dgate_ref)

        dxn_t = dxn_ref[...]
        dgate_ref[0] += jnp.sum(dxn_t * y_ref[...], axis=0, keepdims=True)
        dy = (dxn_t * gate_ref[0]).astype(BF16)
        for gi, (o_ref, do_ref) in enumerate(((oa_ref, doa_ref), (ob_ref, dob_ref), (oc_ref, doc_ref))):
            cols = slice(384 * gi, 384 * (gi + 1))
            u = zg_ref[:, cols]
            o_t = o_ref[...]
            su = _silu(u)
            dcat = _dot(dy, wt_ref[:, cols])
            do_ref[...] = dcat * su
            dzg_ref[:, cols] = (dcat * o_t * _dsilu(u)).astype(BF16)
            gw_ref[cols, :] += _dot_tn((o_t * su).astype(BF16), dy)

    return pl.pallas_call(
        body, name="outproj_bwd", grid=(t // TM,),
        in_specs=[_row_spec(D_MODEL), _row_spec(D_MODEL), _ex_spec(tpe), _row_spec(384), _row_spec(384), _row_spec(384),
                  _row_spec(D_CAT), _full_spec((D_CAT, D_MODEL)), _full_spec((D_MODEL, D_CAT))],
        out_specs=[_row_spec(384), _row_spec(384), _row_spec(384), _row_spec(D_CAT), _full_spec((D_CAT, D_MODEL)),
                   _ex_spec(tpe)],
        out_shape=[jax.ShapeDtypeStruct((t, 384), F32)] * 3 + [jax.ShapeDtypeStruct((t, D_CAT), BF16),
                                                                jax.ShapeDtypeStruct((D_CAT, D_MODEL), F32),
                                                                jax.ShapeDtypeStruct((nb, 1, D_MODEL), F32)],
        compiler_params=_params(("arbitrary",)),
    )(dxn, y, gate, oa, ob, oc, zg, w_out_p, w_out_pt)


def _a_up_bwd(dqa, dka, dva, za, gq, gkv, w_uq_pt, w_ukv_pt, cos, sina, sinb):
    t = za.shape[0]

    def body(dq_ref, dk_ref, dv_ref, za_ref, gq_ref, gkv_ref, wqt_ref, wkvt_ref, cos_ref, sa_ref, sb_ref,
             dza_ref, gwq_ref, gwkv_ref, ggq_ref, ggkv_ref, dqb, dkvb):
        @pl.when(pl.program_id(0) == 0)
        def _():
            gwq_ref[...] = jnp.zeros_like(gwq_ref)
            gwkv_ref[...] = jnp.zeros_like(gwkv_ref)
            ggq_ref[...] = jnp.zeros_like(ggq_ref)
            ggkv_ref[...] = jnp.zeros_like(ggkv_ref)

        cos_t, sa, sb = cos_ref[...], sa_ref[...], sb_ref[...]
        dkpe = jnp.zeros((TM, LANE), F32)
        for p in range(3):
            dqb[:, 256 * p:256 * p + 128] = dq_ref[:, 256 * p:256 * p + 128].astype(BF16)
            dqb[:, 256 * p + 128:256 * p + 256] = _rope_t(dq_ref[:, 256 * p + 128:256 * p + 256], cos_t, sa, sb).astype(BF16)
            dkvb[:, 128 * p:128 * p + 128] = dk_ref[:, 256 * p:256 * p + 128].astype(BF16)
            dkpe = dkpe + dk_ref[:, 256 * p + 128:256 * p + 256]
        dkvb[:, 384:768] = dv_ref[...]
        dkpe = _rope_t(dkpe, cos_t, sa, sb)
        dkpe = jnp.where(_lane_iota((TM, LANE)) < A_ROPE, dkpe + pltpu.roll(dkpe, LANE - 32, 1), 0.0)

        gqv = gq_ref[...]
        cqn, cqh, rq = _rms(za_ref[:, 0:384], gqv)
        dq_t = dqb[...]
        gwq_ref[...] += _dot_tn(cqn.astype(BF16), dq_t)
        dcqn = _dot(dq_t, wqt_ref[...])
        ggq_ref[...] += jnp.broadcast_to(jnp.sum(dcqn * cqh, axis=0, keepdims=True), ggq_ref.shape)
        dza_ref[:, 0:384] = _rms_bwd(dcqn, cqh, rq, gqv).astype(BF16)

        gkvv = gkv_ref[...]
        ckvn, ckvh, rkv = _rms(za_ref[:, 384:640], gkvv)
        dkv_t = dkvb[...]
        gwkv_ref[...] += _dot_tn(ckvn.astype(BF16), dkv_t)
        dckvn = _dot(dkv_t, wkvt_ref[...])
        ggkv_ref[...] += jnp.broadcast_to(jnp.sum(dckvn * ckvh, axis=0, keepdims=True), ggkv_ref.shape)
        dza_ref[:, 384:640] = _rms_bwd(dckvn, ckvh, rkv, gkvv).astype(BF16)
        dza_ref[:, 640:768] = dkpe.astype(BF16)

    return pl.pallas_call(
        body, name="a_up_bwd", grid=(t // TM,),
        in_specs=[_row_spec(768), _row_spec(768), _row_spec(384), _row_spec(768), _full_spec((1, 384)),
                  _full_spec((1, 256)), _full_spec((768, 384)), _full_spec((768, 256)), _row_spec(LANE), _row_spec(LANE),
                  _row_spec(LANE)],
        out_specs=[_row_spec(768), _full_spec((384, 768)), _full_spec((256, 768)), _full_spec((8, 384)),
                   _full_spec((8, 256))],
        out_shape=[jax.ShapeDtypeStruct((t, 768), BF16), jax.ShapeDtypeStruct((384, 768), F32),
                   jax.ShapeDtypeStruct((256, 768), F32), jax.ShapeDtypeStruct((8, 384), F32),
                   jax.ShapeDtypeStruct((8, 256), F32)],
        scratch_shapes=[pltpu.VMEM((TM, 768), BF16), pltpu.VMEM((TM, 768), BF16)],
        compiler_params=_params(("arbitrary",)),
    )(dqa, dka, dva, za, gq, gkv, w_uq_pt, w_ukv_pt, cos, sina, sinb)


def _dz_cols():
    return (Z_A, Z_G) + Z_QKV + (Z_F,)


def _inproj_bwd_dx(dz, dxn, x, shift, scale, g, w_in_pt, nb, seq):
    t = x.shape[0]
    tpe = seq // TM
    cols = _dz_cols()

    def body(*refs):
        dz_refs = refs[:len(cols)]
        dxn_ref, x_ref, sh_ref, sc_ref, g_ref, wt_ref, dx_ref, dsh_ref, dsc_ref, dg_ref = refs[len(cols):]
        i = pl.program_id(0)

        @pl.when(i == 0)
        def _():
            dg_ref[...] = jnp.zeros_like(dg_ref)

        @pl.when(i % tpe == 0)
        def _():
            dsh_ref[...] = jnp.zeros_like(dsh_ref)
            dsc_ref[...] = jnp.zeros_like(dsc_ref)

        dh = jnp.zeros((TM, D_MODEL), F32)
        for ref, (c0, c1) in zip(dz_refs, cols):
            dh = dh + _dot_nt(ref[...], wt_ref[:, c0:c1])
        gv = g_ref[...]
        n, xh, r = _rms(x_ref[...], gv)
        dsh_ref[0] += jnp.sum(dh, axis=0, keepdims=True)
        dsc_ref[0] += jnp.sum(dh * n, axis=0, keepdims=True)
        dn = dh * (1.0 + sc_ref[0])
        dg_ref[...] += jnp.broadcast_to(jnp.sum(dn * xh, axis=0, keepdims=True), dg_ref.shape)
        dx_ref[...] = dxn_ref[...] + _rms_bwd(dn, xh, r, gv)

    in_specs = [_row_spec(c1 - c0) for c0, c1 in cols]
    in_specs += [_row_spec(D_MODEL), _row_spec(D_MODEL), _ex_spec(tpe), _ex_spec(tpe), _full_spec((1, D_MODEL)),
                 _full_spec((D_MODEL, NP_IN))]
    return pl.pallas_call(
        body, name="inproj_bwd_dx", grid=(t // TM,), in_specs=in_specs,
        out_specs=[_row_spec(D_MODEL), _ex_spec(tpe), _ex_spec(tpe), _full_spec((8, D_MODEL))],
        out_shape=[jax.ShapeDtypeStruct((t, D_MODEL), F32), jax.ShapeDtypeStruct((nb, 1, D_MODEL), F32),
                   jax.ShapeDtypeStruct((nb, 1, D_MODEL), F32), jax.ShapeDtypeStruct((8, D_MODEL), F32)],
        compiler_params=_params(("arbitrary",)),
    )(*dz, dxn, x, shift, scale, g, w_in_pt)


def _inproj_bwd_dw(h, dz, name, layer, both=None):
    t = h.shape[0]
    widths = [d.shape[1] for d in dz]
    total = sum(widths)

    def body(*refs):
        h_ref = refs[0]
        dz_refs = refs[1:1 + len(dz)]
        gw_ref = refs[-1]

        @pl.when(pl.program_id(0) == 0)
        def _():
            gw_ref[...] = jnp.zeros_like(gw_ref)

        h_t = h_ref[...]
        c0 = 0
        for ref, w in zip(dz_refs, widths):
            gw_ref[0, :, c0:c0 + w] += _dot_tn(h_t, ref[...])
            c0 += w

    in_specs = [_row_spec(D_MODEL)] + [_row_spec(w) for w in widths]
    args = [h, *dz]
    aliases = {}
    if both is not None:
        in_specs.append(pl.BlockSpec(memory_space=pl.ANY))
        aliases = {len(args): 0}
        args.append(both)
    return pl.pallas_call(
        body, name=name, grid=(t // TM,), in_specs=in_specs,
        out_specs=pl.BlockSpec((1, D_MODEL, total), lambda i: (layer, 0, 0)),
        out_shape=jax.ShapeDtypeStruct((DEPTH, D_MODEL, total), F32), input_output_aliases=aliases,
        compiler_params=_params(("arbitrary",)),
    )(*args)


def _ada_fwd(c_all, w_ada, b_cols):
    n = c_all.shape[0]
    cols = w_ada.shape[2]

    def body(c_ref, w_ref, b_ref, out_ref):
        act = _silu(c_ref[...]).astype(BF16)
        out_ref[0] = _dot(act, w_ref[0].astype(BF16)) + b_ref[0]

    return pl.pallas_call(
        body, name="ada_fwd", grid=(DEPTH,),
        in_specs=[pl.BlockSpec((n, D_MODEL), lambda l: (0, 0)), pl.BlockSpec((1, D_MODEL, cols), lambda l: (l, 0, 0)),
                  pl.BlockSpec((1, 1, cols), lambda l: (l, 0, 0))],
        out_specs=pl.BlockSpec((1, n, cols), lambda l: (l, 0, 0)),
        out_shape=jax.ShapeDtypeStruct((DEPTH, n, cols), F32), compiler_params=_params(("parallel",)),
    )(c_all, w_ada, b_cols)


def _ada_bwd(c_all, dmod_cols, dmod_all):
    n = c_all.shape[0]
    cols = dmod_cols.shape[2]

    def body(c_ref, dc_ref, da_ref, gw_ref, gb_ref):
        act = _silu(c_ref[...]).astype(BF16)
        gw_ref[0] = _dot_tn(act, dc_ref[0].astype(BF16))
        gb_ref[0] = jnp.sum(da_ref[0], axis=0, keepdims=True)

    return pl.pallas_call(
        body, name="ada_bwd", grid=(DEPTH,),
        in_specs=[pl.BlockSpec((n, D_MODEL), lambda l: (0, 0)), pl.BlockSpec((1, n, cols), lambda l: (l, 0, 0)),
                  pl.BlockSpec((1, n, 3 * D_MODEL), lambda l: (l, 0, 0))],
        out_specs=[pl.BlockSpec((1, D_MODEL, cols), lambda l: (l, 0, 0)),
                   pl.BlockSpec((1, 1, 3 * D_MODEL), lambda l: (l, 0, 0))],
        out_shape=[jax.ShapeDtypeStruct((DEPTH, D_MODEL, cols), F32), jax.ShapeDtypeStruct((DEPTH, 1, 3 * D_MODEL), F32)],
        compiler_params=_params(("parallel",)),
    )(c_all, dmod_cols, dmod_all)


def _sum_blocks(parts, name):
    n, rows, cols = parts.shape
    tr = rows if rows <= 256 else 8 * next(d for d in range(32, 0, -1) if (rows // 8) % d == 0)

    def body(p_ref, out_ref):
        acc = p_ref[0].astype(F32)
        for k in range(1, n):
            acc = acc + p_ref[k].astype(F32)
        out_ref[...] = acc

    return pl.pallas_call(
        body, name=name, grid=(rows // tr,), in_specs=[pl.BlockSpec((n, tr, cols), lambda i: (0, i, 0))],
        out_specs=pl.BlockSpec((tr, cols), lambda i: (i, 0)), out_shape=jax.ShapeDtypeStruct((rows, cols), F32),
        compiler_params=_params(("parallel",)),
    )(parts)


def _adamw(w, g, m, v, name):
    nl, rows, cols = w.shape
    if rows * cols <= 64 * 1024:
        tr = rows
        tl = next(t for t in range(nl, 0, -1) if nl % t == 0 and t * max(rows, 8) * cols <= 512 * 1024)
    else:
        tl = 1
        tr = next(t for t in (rows, 256, 128, 64, 32, 16, 8) if rows % t == 0 and t * cols <= 256 * 1024)

    def body(w_ref, g_ref, m_ref, v_ref, d_ref, mo_ref, vo_ref):
        gv = g_ref[...]
        mn = ADAM_B1 * m_ref[...] + (1.0 - ADAM_B1) * gv
        vn = ADAM_B2 * v_ref[...] + (1.0 - ADAM_B2) * jnp.square(gv)
        m_hat = mn / (1.0 - ADAM_B1 ** ADAM_STEP)
        v_hat = vn / (1.0 - ADAM_B2 ** ADAM_STEP)
        d_ref[...] = -ADAM_LR * (m_hat / (jnp.sqrt(v_hat) + ADAM_EPS) + ADAM_WD * w_ref[...])
        mo_ref[...] = mn
        vo_ref[...] = vn

    spec = pl.BlockSpec((tl, tr, cols), lambda l, i: (l, i, 0))
    return pl.pallas_call(
        body, name=name, grid=(nl // tl, rows // tr), in_specs=[spec] * 4, out_specs=[spec] * 3,
        out_shape=[jax.ShapeDtypeStruct((nl, rows, cols), F32)] * 3, compiler_params=_params(("parallel", "parallel")),
    )(w, g, m, v)


ALL_FLIPS = tuple(range(1, N_DEV))


def _exchange(src, flips, mode, name):
    _, rows, cols = src.shape
    nslot = 2 if mode == "pair" else N_DEV
    nf = len(flips)

    def body(src_ref, dst_ref, send_sems, recv_sems, local_sem):
        x, y, c = lax.axis_index("x"), lax.axis_index("y"), lax.axis_index("c")
        me = 4 * x + 2 * y + c

        def slot(j):
            return (j & 1) if mode == "pair" else j

        own = pltpu.make_async_copy(src_ref.at[me if mode == "scatter" else 0], dst_ref.at[slot(me)], local_sem)
        own.start()
        copies = []
        for i, f in enumerate(flips):
            peer = me ^ f
            to = (1 - x if f & 4 else x, 1 - y if f & 2 else y, 1 - c if f & 1 else c)
            cp = pltpu.make_async_remote_copy(
                src_ref=src_ref.at[peer if mode == "scatter" else 0], dst_ref=dst_ref.at[slot(me)],
                send_sem=send_sems.at[i], recv_sem=recv_sems.at[i], device_id=to, device_id_type=pl.DeviceIdType.MESH)
            cp.start()
            copies.append(cp)
        for i, f in enumerate(flips):
            peer = me ^ f
            to = (1 - x if f & 4 else x, 1 - y if f & 2 else y, 1 - c if f & 1 else c)
            pltpu.make_async_remote_copy(
                src_ref=src_ref.at[0], dst_ref=dst_ref.at[slot(peer)], send_sem=send_sems.at[i],
                recv_sem=recv_sems.at[i], device_id=to, device_id_type=pl.DeviceIdType.MESH).wait_recv()
        for cp in copies:
            cp.wait_send()
        own.wait()

    return pl.pallas_call(
        body, name=name, out_shape=jax.ShapeDtypeStruct((nslot, rows, cols), src.dtype),
        in_specs=[pl.BlockSpec(memory_space=pl.ANY)], out_specs=pl.BlockSpec(memory_space=pl.ANY),
        scratch_shapes=[pltpu.SemaphoreType.DMA((nf,)), pltpu.SemaphoreType.DMA((nf,)), pltpu.SemaphoreType.DMA],
    )(src)


def _transfer(name, srcs, dst_shapes, plan, in_place=False):
    n_arr = len(srcs)
    probe = plan(0, 0, 0)
    n_steps = len(probe)

    def body(*refs):
        src_refs, dst_refs = refs[:n_arr], refs[n_arr:2 * n_arr]
        send_sems, recv_sems, local_sems = refs[2 * n_arr:]
        x, y, c = lax.axis_index("x"), lax.axis_index("y"), lax.axis_index("c")
        steps = plan(x, y, c)

        def rows(ref, r0, n):
            return ref.at[:, pl.ds(r0, n), :]

        def arrival(t):
            a, _, _, n, _, f, _ = steps[t]
            return pltpu.make_async_remote_copy(
                src_ref=rows(dst_refs[a], 0, n), dst_ref=rows(dst_refs[a], 0, n), send_sem=send_sems.at[t],
                recv_sem=recv_sems.at[t], device_id=(x, y, c), device_id_type=pl.DeviceIdType.MESH)

        arrived, started = set(), []
        for t, (a, from_dst, sr, n, dr, f, after) in enumerate(steps):
            for u in after:
                if u not in arrived:
                    arrival(u).wait_recv()
                    arrived.add(u)
            src = rows(dst_refs[a] if from_dst else src_refs[a], sr, n)
            dst = rows(dst_refs[a], dr, n)
            if f == 0:
                cp = pltpu.make_async_copy(src, dst, local_sems.at[t])
            else:
                to = (1 - x if f & 4 else x, 1 - y if f & 2 else y, 1 - c if f & 1 else c)
                cp = pltpu.make_async_remote_copy(src_ref=src, dst_ref=dst, send_sem=send_sems.at[t],
                                                  recv_sem=recv_sems.at[t], device_id=to,
                                                  device_id_type=pl.DeviceIdType.MESH)
            cp.start()
            started.append(cp)
        for t, step in enumerate(steps):
            if step[5] != 0 and t not in arrived:
                arrival(t).wait_recv()
        for cp, step in zip(started, steps):
            if step[5] == 0:
                cp.wait()
            else:
                cp.wait_send()

    any_spec = pl.BlockSpec(memory_space=pl.ANY)
    return pl.pallas_call(
        body, name=name, out_shape=[jax.ShapeDtypeStruct(s, d) for s, d in dst_shapes],
        in_specs=[any_spec] * n_arr, out_specs=[any_spec] * n_arr,
        input_output_aliases={a: a for a in range(n_arr)} if in_place else {},
        scratch_shapes=[pltpu.SemaphoreType.DMA((n_steps,)), pltpu.SemaphoreType.DMA((n_steps,)),
                        pltpu.SemaphoreType.DMA((n_steps,))],
    )(*srcs)


CHIP_FLIPS = (2, 4, 6)


def _gather_plan(chip_rows):
    def plan(x, y, c):
        steps = []
        for a, rc in enumerate(chip_rows):
            h = rc // 2
            mine = rc * (2 * x + y) + h * c
            steps.append((a, False, h * c, h, mine, 0, ()))
            ici = {}
            for f in CHIP_FLIPS:
                ici[f] = len(steps)
                steps.append((a, False, h * c, h, mine, f, ()))
            steps.append((a, False, h * c, h, mine, 1, ()))
            for f in CHIP_FLIPS:
                theirs = rc * ((2 * x + y) ^ (f >> 1)) + h * c
                steps.append((a, True, theirs, h, theirs, 1, (ici[f],)))
        return steps
    return plan


def _pair_reduce_plan(chip_rows):
    def plan(x, y, c):
        steps = []
        for a, rc in enumerate(chip_rows):
            h = rc // 2
            for j in range(4):
                steps.append((a, False, rc * j + h * (1 - c), h, h * j, 1, ()))
        return steps
    return plan


def _chip_scatter_plan(chip_rows):
    def plan(x, y, c):
        steps = []
        for a, rc in enumerate(chip_rows):
            h = rc // 2
            for k, f in enumerate(CHIP_FLIPS):
                steps.append((a, False, h * ((2 * x + y) ^ (f >> 1)), h, h * k, f, ()))
        return steps
    return plan


def _pair_share_plan(chip_rows):
    def plan(x, y, c):
        return [(a, True, (rc // 2) * c, rc // 2, (rc // 2) * c, 1, ()) for a, rc in enumerate(chip_rows)]
    return plan


def _tile_rows(h):
    return next(t for t in (64, 32, 16) if h % t == 0)


def _sum_pair(partial, recv, core, rc, name):
    nl, _, cols = partial.shape
    h = rc // 2
    tr = _tile_rows(h)

    def body(c_ref, p_ref, r_ref, out_ref):
        out_ref[...] = (p_ref[...] + r_ref[...]).astype(BF16)

    spec = pl.BlockSpec((1, tr, cols), lambda l, j, i, c_ref: (l, (h // tr) * j + i, 0))
    return pl.pallas_call(
        body, name=name, out_shape=jax.ShapeDtypeStruct((nl, 4 * h, cols), BF16),
        grid_spec=pltpu.PrefetchScalarGridSpec(
            num_scalar_prefetch=1, grid=(nl, 4, h // tr),
            in_specs=[pl.BlockSpec((1, tr, cols), lambda l, j, i, c_ref: (l, (rc // tr) * j + (h // tr) * c_ref[0] + i, 0)),
                      spec],
            out_specs=spec),
        compiler_params=_params(("parallel", "parallel", "parallel")),
    )(core, partial, recv)


def _sum_chips(chip_sum, recv, place, rc, name):
    nl, _, cols = chip_sum.shape
    h = rc // 2
    tr = _tile_rows(h)

    def body(s_ref, own_ref, r_ref, out_ref):
        acc = own_ref[0].astype(F32)
        for k in range(3):
            acc = acc + r_ref[0, k].astype(F32)
        out_ref[0] = acc

    return pl.pallas_call(
        body, name=name, out_shape=jax.ShapeDtypeStruct((nl, rc, cols), F32),
        grid_spec=pltpu.PrefetchScalarGridSpec(
            num_scalar_prefetch=1, grid=(nl, h // tr),
            in_specs=[pl.BlockSpec((1, tr, cols), lambda l, i, s_ref: (l, (h // tr) * s_ref[0] + i, 0)),
                      pl.BlockSpec((1, 3, tr, cols), lambda l, i, s_ref: (l, 0, i, 0))],
            out_specs=pl.BlockSpec((1, tr, cols), lambda l, i, s_ref: (l, (h // tr) * s_ref[1] + i, 0))),
        compiler_params=_params(("parallel", "parallel")),
    )(place, chip_sum, recv.reshape(nl, 3, h, cols))


def _sum_slots(parts, out_dtype, name):
    nl, n, rows, cols = parts.shape
    tr = next(t for t in (128, 64, 32, 16) if rows % t == 0 and n * t * cols * 4 <= (4 << 20))

    def body(p_ref, out_ref):
        acc = p_ref[0, 0].astype(F32)
        for k in range(1, n):
            acc = acc + p_ref[0, k].astype(F32)
        out_ref[0] = acc.astype(out_dtype)

    return pl.pallas_call(
        body, name=name, grid=(nl, rows // tr),
        in_specs=[pl.BlockSpec((1, n, tr, cols), lambda l, i: (l, 0, i, 0))],
        out_specs=pl.BlockSpec((1, tr, cols), lambda l, i: (l, i, 0)),
        out_shape=jax.ShapeDtypeStruct((nl, rows, cols), out_dtype), compiler_params=_params(("parallel", "parallel")),
    )(parts)


def _pad_cols(a, n):
    return a if n == 0 else jnp.pad(a, ((0, 0), (0, n)))


def _in_to_padded(w):
    return jnp.concatenate([_pad_cols(w[:, a:b], z) for a, b, z in IN_PIECES], axis=1)


def _in_from_padded(gp):
    pos, out = 0, {}
    for a, b, z in IN_PIECES:
        out[a] = gp[:, pos:pos + (b - a)]
        pos += (b - a) + z
    return jnp.concatenate([out[a] for a in sorted(out)], axis=1)


def _out_to_padded(w):
    z = jnp.zeros((64, w.shape[1]), w.dtype)
    return jnp.concatenate([w[0:384], w[384:704], z, w[704:1024], z], axis=0)


def _out_from_padded(gp):
    return jnp.concatenate([gp[0:384], gp[384:704], gp[768:1088]], axis=0)


def _uq_to_padded(w):
    parts = []
    for p in range(3):
        h0, h1 = 2 * p, 2 * p + 1
        parts += [w[:, 96 * h0:96 * h0 + 64], w[:, 96 * h1:96 * h1 + 64], w[:, 96 * h0 + 64:96 * h0 + 96],
                  w[:, 96 * h1 + 64:96 * h1 + 96], jnp.zeros((w.shape[0], 64), w.dtype)]
    return jnp.concatenate(parts, axis=1)


def _uq_from_padded(gp):
    parts = []
    for h in range(6):
        p, s = h // 2, h % 2
        parts += [gp[:, 256 * p + 64 * s:256 * p + 64 * s + 64], gp[:, 256 * p + 128 + 32 * s:256 * p + 160 + 32 * s]]
    return jnp.concatenate(parts, axis=1)


def _ukv_to_padded(w):
    return jnp.concatenate([w[:, 128 * h:128 * h + 64] for h in range(6)]
                           + [w[:, 128 * h + 64:128 * h + 128] for h in range(6)], axis=1)


def _ukv_from_padded(gp):
    parts = []
    for h in range(6):
        parts += [gp[:, 64 * h:64 * h + 64], gp[:, 384 + 64 * h:384 + 64 * h + 64]]
    return jnp.concatenate(parts, axis=1)


LR_ROWS = 224
SHARD_ROWS = (256, 256, 256, LR_ROWS)


def _pack_lowrank(w_uq, w_ukv):
    flat = jnp.concatenate([w_uq.reshape(-1), w_ukv.reshape(-1)])
    return jnp.pad(flat, (0, LR_ROWS * PACK_COLS - flat.shape[0])).reshape(1, LR_ROWS, PACK_COLS)


def _unpack_lowrank(packed):
    flat = packed.reshape(-1)
    n_uq = DEPTH * A_Q_RANK * 144
    n_ukv = DEPTH * A_KV_RANK * 192
    return flat[0:n_uq].reshape(DEPTH, A_Q_RANK, 144), flat[n_uq:n_uq + n_ukv].reshape(DEPTH, A_KV_RANK, 192)


PACK_SIZES = (DEPTH * 256 * N_IN, DEPTH * 256 * D_MODEL, DEPTH * A_Q_RANK * 144, DEPTH * A_KV_RANK * 192)


def _pack(w_in, w_out, w_uq, w_ukv):
    flat = jnp.concatenate([w_in.reshape(-1), w_out.reshape(-1), w_uq.reshape(-1), w_ukv.reshape(-1)])
    flat = jnp.pad(flat, (0, PACK_ROWS * PACK_COLS - flat.shape[0]))
    return flat.reshape(PACK_ROWS, PACK_COLS)


def _unpack(packed):
    flat = packed.reshape(-1)
    o0, o1, o2, o3 = PACK_SIZES
    w_in = flat[0:o0].reshape(DEPTH, 256, N_IN)
    w_out = flat[o0:o0 + o1].reshape(DEPTH, 256, D_MODEL)
    w_uq = flat[o0 + o1:o0 + o1 + o2].reshape(DEPTH, A_Q_RANK, 144)
    w_ukv = flat[o0 + o1 + o2:o0 + o1 + o2 + o3].reshape(DEPTH, A_KV_RANK, 192)
    return w_in, w_out, w_uq, w_ukv


def _rope_tables(positions):
    t = positions.size
    inv = ROPE_THETA ** (-jnp.arange(0, A_ROPE, 2, dtype=F32) / A_ROPE)
    inv_row = jnp.pad(jnp.tile(inv, 4), (0, 64)).reshape(1, LANE)

    def body(p_ref, i_ref, c_ref, sa_ref, sb_ref):
        ang = p_ref[...].astype(F32) * i_ref[...]
        lane = _lane_iota((TM, LANE))
        live = lane < 64
        second = (lane & 31) >= 16
        s = jnp.sin(ang)
        c_ref[...] = jnp.where(live, jnp.cos(ang), 0.0)
        sa_ref[...] = jnp.where(live & jnp.logical_not(second), -s, 0.0)
        sb_ref[...] = jnp.where(live & second, s, 0.0)

    return pl.pallas_call(
        body, name="rope_tables", grid=(t // TM,), in_specs=[_row_spec(1), _full_spec((1, LANE))],
        out_specs=[_row_spec(LANE)] * 3, out_shape=[jax.ShapeDtypeStruct((t, LANE), F32)] * 3,
        compiler_params=_params(("parallel",)),
    )(positions.reshape(t, 1), inv_row)


def _rows(a, n):
    flat = a.reshape(-1)
    return jnp.pad(flat, (0, n * LANE - flat.shape[0])).reshape(n, LANE)


def _forward_backward(x, mod, tables, target, weights, small, nb, seq):
    cos, sina, sinb = tables
    saved = []
    for l in range(DEPTH):
        w, s = weights[l], small[l]
        shift = mod[l][:, None, 0:D_MODEL]
        scale = mod[l][:, None, D_MODEL:2 * D_MODEL]
        gate = mod[l][:, None, 2 * D_MODEL:]
        h, za, zg, qb, kb, vb, qc, kc, vc, zf = _ln_inproj(x, shift, scale, s["norm_g"], w["in"], seq)
        qa, ka, va = _a_up(za, s["gq"], s["gkv"], w["uq"], w["ukv"], cos, sina, sinb)
        bias = _bias_expand(s["g8"])[0:6]
        f = _forget_fwd(zf, s["fb"], nb, seq)
        frow = jnp.pad(f[:, 0:5].reshape(nb, seq, 5).transpose(0, 2, 1), ((0, 0), (0, 1), (0, 0)))
        frow = frow.reshape(nb, 6, 1, seq)
        oa, lse_a = _attn_fwd("A", qa, ka, va, None, nb, seq)
        ob, lse_b = _attn_fwd("B", qb, kb, vb, bias, nb, seq)
        oc, lse_c = _attn_fwd("C", qc, kc, vc, frow, nb, seq)
        y, xn = _gate_outproj(x, gate, oa, ob, oc, zg, w["out"], seq)
        saved.append(dict(x=x, h=h, za=za, zg=zg, zf=zf, y=y, shift=shift, scale=scale, gate=gate, bias=bias, frow=frow,
                          a=(qa, ka, va, oa, lse_a), b=(qb, kb, vb, ob, lse_b), c=(qc, kc, vc, oc, lse_c)))
        x = xn
    dx, loss8, gfinal8 = _final_loss(x, target, small[0]["final_g"])
    grads = []
    gw_in = (None, None)
    for l in reversed(range(DEPTH)):
        w, s, sv = weights[l], small[l], saved[l]
        qa, ka, va, oa, lse_a = sv["a"]
        qb, kb, vb, ob, lse_b = sv["b"]
        qc, kc, vc, oc, lse_c = sv["c"]
        doa, dob, doc, dzg, gw_out, dgate = _outproj_bwd(dx, sv["y"], sv["gate"], oa, ob, oc, sv["zg"], w["out"],
                                                          w["out_t"], nb, seq)
        dqa, dka, dva = _attn_bwd("A", qa, ka, va, oa, doa, lse_a, None, nb, seq)
        dqb, dkb, dvb, dbt = _attn_bwd("B", qb, kb, vb, ob, dob, lse_b, sv["bias"], nb, seq)
        dqc, dkc, dvc, dfr, dfq = _attn_bwd("C", qc, kc, vc, oc, doc, lse_c, sv["frow"], nb, seq)
        dg = _bias_reduce(jnp.pad(dbt, ((0, 2), (0, 0), (0, 0))))
        grb = jnp.pad(_bias_unline(dg), ((0, 0), (0, 384 - N_REL)))
        dfk = dfr.reshape(nb, 6, seq).transpose(0, 2, 1).reshape(nb * seq, 6)
        dzf, gfb = _forget_bwd(dfq, jnp.pad(dfk, ((0, 0), (0, LANE - 6))), sv["zf"], s["fb"], nb, seq)
        dza, gw_uq, gw_ukv, ggq, ggkv = _a_up_bwd(dqa, dka, dva, sv["za"], s["gq"], s["gkv"], w["uq_t"], w["ukv_t"],
                                                  cos, sina, sinb)
        dz = (dza, dzg, dqb, dkb, dvb, dqc, dkc, dvc, dzf)
        dx, dshift, dscale, gnorm = _inproj_bwd_dx(dz, dx, sv["x"], sv["shift"], sv["scale"], s["norm_g"], w["in"],
                                                   nb, seq)
        gw_in = (_inproj_bwd_dw(sv["h"], dz[0:2], "inproj_bwd_dw0", l, gw_in[0]),
                 _inproj_bwd_dw(sv["h"], dz[2:], "inproj_bwd_dw1", l, gw_in[1]))
        dmod = jnp.concatenate([dshift[:, 0], dscale[:, 0], dgate[:, 0]], axis=1)
        grads.append(dict(w_out=gw_out, w_uq=gw_uq, w_ukv=gw_ukv, dmod=dmod, norm_g=gnorm[0], gq=ggq[0],
                          gkv=ggkv[0], rb8=grb, fb=gfb[0]))
    grads.reverse()
    return loss8[0, 0], dx, grads, gfinal8[0], gw_in


def _layer_weights(w_in, w_out, w_uq, w_ukv):
    wi, wo, wq, wkv = _in_to_padded(w_in), _out_to_padded(w_out), _uq_to_padded(w_uq), _ukv_to_padded(w_ukv)
    return {"in": wi, "out": wo, "out_t": wo.T, "uq": wq, "uq_t": wq.T, "ukv": wkv, "ukv_t": wkv.T}


def _layer_small(norm_g, gq, gkv, rel_bias, forget_b, final_g):
    fb = jnp.pad(forget_b, (0, LANE - 5)).reshape(1, LANE)
    return dict(norm_g=norm_g.reshape(1, -1), gq=gq.reshape(1, -1), gkv=gkv.reshape(1, -1), g8=_bias_line(rel_bias), fb=fb,
                final_g=final_g.reshape(1, -1))


def _small_payload(per_layer, final_g, loss):
    def stack(key):
        return jnp.stack([p[key] for p in per_layer])

    def rows(a, rng):
        return _rows(a, rng[1] - rng[0])

    dmod = stack("dmod") if "dmod" in per_layer[0] else jnp.zeros((LANE,), F32)
    parts = [rows(dmod, PAY_DMOD), rows(stack("norm_g"), PAY_NORM), rows(stack("gq"), PAY_GQ),
             rows(stack("gkv"), PAY_GKV), rows(stack("rb8"), PAY_RB), rows(stack("fb"), PAY_FB),
             rows(final_g, PAY_FINAL), rows(loss, PAY_LOSS)]
    return jnp.concatenate(parts, axis=0)


def _payload_split(pay):
    def take(rng, shape):
        n = 1
        for d in shape:
            n *= d
        return pay[rng[0]:rng[1]].reshape(-1)[0:n].reshape(shape)

    norm_g = take(PAY_NORM, (DEPTH, D_MODEL))
    gq = take(PAY_GQ, (DEPTH, A_Q_RANK))
    gkv = take(PAY_GKV, (DEPTH, A_KV_RANK))
    rb = take(PAY_RB, (DEPTH, 8, 384))[:, 0:5, 0:N_REL]
    fb = take(PAY_FB, (DEPTH, LANE))[:, 0:5]
    final_g = take(PAY_FINAL, (D_MODEL,))
    return norm_g, gq, gkv, rb, fb, final_g


def kernel(x, c, positions, w_ada, b_ada, norm_g, w_in, a_q_norm_g, a_w_uq, a_kv_norm_g, a_w_ukv, b_rel_bias, c_forget_b, w_out, final_g, loss_target, m_w_ada, m_b_ada, m_norm_g, m_w_in, m_a_q_norm_g, m_a_w_uq, m_a_kv_norm_g, m_a_w_ukv, m_b_rel_bias, m_c_forget_b, m_w_out, m_final_g, v_w_ada, v_b_ada, v_norm_g, v_w_in, v_a_q_norm_g, v_a_w_uq, v_a_kv_norm_g, v_a_w_ukv, v_b_rel_bias, v_c_forget_b, v_w_out, v_final_g):
    nb, seq, _ = x.shape
    ix, iy, ic = lax.axis_index("x"), lax.axis_index("y"), lax.axis_index("c")
    chip = 2 * ix + iy
    me = 2 * chip + ic

    full_in, full_out, full_lr = _transfer(
        "gather_weights", [w_in.astype(BF16), w_out.astype(BF16), _pack_lowrank(a_w_uq, a_w_ukv).astype(BF16)],
        [((DEPTH, D_MODEL, N_IN), BF16), ((DEPTH, D_MODEL, D_MODEL), BF16), ((1, 4 * LR_ROWS, PACK_COLS), BF16)],
        _gather_plan((256, 256, LR_ROWS)))
    lowrank = [_unpack_lowrank(full_lr[0, LR_ROWS * j:LR_ROWS * (j + 1)]) for j in range(4)]
    full_uq = jnp.concatenate([s[0] for s in lowrank], axis=2)
    full_ukv = jnp.concatenate([s[1] for s in lowrank], axis=2)
    weights = [_layer_weights(full_in[l], full_out[l], full_uq[l], full_ukv[l]) for l in range(DEPTH)]
    small = [_layer_small(norm_g[l], a_q_norm_g[l], a_kv_norm_g[l], b_rel_bias[l], c_forget_b[l], final_g)
             for l in range(DEPTH)]

    c_all = _exchange(c[None], ALL_FLIPS, "gather", "gather_c").reshape(N_DEV * nb, D_MODEL)
    cols = w_ada.shape[2]
    b_cols = lax.dynamic_slice_in_dim(b_ada, chip * cols, cols, axis=1)[:, None, :]
    mod_cols = _ada_fwd(c_all, w_ada, b_cols)
    mod_g = _exchange(mod_cols.reshape(1, DEPTH * N_DEV * nb, cols), ALL_FLIPS, "gather", "gather_mod")
    mod_all = jnp.concatenate([mod_g[2 * j].reshape(DEPTH, N_DEV * nb, cols) for j in range(4)], axis=2)
    mod = lax.dynamic_slice_in_dim(mod_all, me * nb, nb, axis=1)

    tables = _rope_tables(positions)
    loss_part, dx, grads, gfinal, gw_in = _forward_backward(
        x.reshape(nb * seq, D_MODEL), mod, tables, loss_target.reshape(nb * seq, D_MODEL), weights, small, nb, seq)

    pay = _small_payload(grads, gfinal, loss_part)
    pay_all = _exchange(pay[None], ALL_FLIPS, "gather", "gather_small")
    tot = _sum_blocks(pay_all, "sum_small")
    loss = tot[PAY_LOSS[0], 0]
    dmod_all = pay_all[:, PAY_DMOD[0]:PAY_DMOD[1]].reshape(N_DEV, -1)[:, 0:DEPTH * nb * 3 * D_MODEL]
    dmod_all = dmod_all.reshape(N_DEV, DEPTH, nb, 3 * D_MODEL).transpose(1, 0, 2, 3)
    dmod_all = dmod_all.reshape(DEPTH, N_DEV * nb, 3 * D_MODEL)
    my_cols = lax.dynamic_slice_in_dim(dmod_all, chip * cols, cols, axis=2)
    g_w_ada, g_b_ada = _ada_bwd(c_all, my_cols, dmod_all)
    g_b_ada = g_b_ada[:, 0]

    g_uq = jnp.stack([_uq_from_padded(g["w_uq"]) for g in grads])
    g_ukv = jnp.stack([_ukv_from_padded(g["w_ukv"]) for g in grads])
    g_lr = jnp.concatenate([_pack_lowrank(g_uq[:, :, 144 * j:144 * (j + 1)], g_ukv[:, :, 192 * j:192 * (j + 1)])
                            for j in range(4)], axis=1)
    partials = [gw_in[0], gw_in[1], jnp.stack([_out_from_padded(g["w_out"]) for g in grads]), g_lr]
    shapes = [(p.shape[0], p.shape[2]) for p in partials]
    halves = [r // 2 for r in SHARD_ROWS]
    core_s = jnp.reshape(ic, (1,)).astype(jnp.int32)
    place_s = jnp.stack([chip, ic]).astype(jnp.int32)
    from_pair = _transfer("pair_reduce", partials, [((nl, 4 * h, nc), F32) for (nl, nc), h in zip(shapes, halves)],
                          _pair_reduce_plan(SHARD_ROWS))
    chip_sums = [_sum_pair(p, r, core_s, rc, "sum_pair%d" % i)
                 for i, (p, r, rc) in enumerate(zip(partials, from_pair, SHARD_ROWS))]
    from_chips = _transfer("chip_scatter", chip_sums, [((nl, 3 * h, nc), BF16) for (nl, nc), h in zip(shapes, halves)],
                           _chip_scatter_plan(SHARD_ROWS))
    reduced = [_sum_chips(s, r, place_s, rc, "sum_chips%d" % i)
               for i, (s, r, rc) in enumerate(zip(chip_sums, from_chips, SHARD_ROWS))]
    g_in_a, g_in_b, g_out_sh, g_lr_sh = _transfer("pair_share", reduced, [(r.shape, F32) for r in reduced],
                                                  _pair_share_plan(SHARD_ROWS), in_place=True)
    g_uq_sh, g_ukv_sh = _unpack_lowrank(g_lr_sh[0])
    g_in_sh = jnp.stack([_in_from_padded(jnp.concatenate([g_in_a[l], g_in_b[l]], axis=1)) for l in range(DEPTH)])

    def cols_first(a):
        return jnp.transpose(a, (2, 0, 1))

    g_in_t = cols_first(g_in_sh)
    upd_in = tuple(jnp.transpose(a, (1, 2, 0)) for a in _adamw(cols_first(w_in), g_in_t, cols_first(m_w_in),
                                                               cols_first(v_w_in), "adamw_in"))
    gw = (jnp.transpose(g_in_t, (1, 2, 0)), g_out_sh, g_uq_sh, g_ukv_sh)
    upd = [upd_in, _adamw(w_out, gw[1], m_w_out, v_w_out, "adamw_out"),
           _adamw(a_w_uq, gw[2], m_a_w_uq, v_a_w_uq, "adamw_uq"), _adamw(a_w_ukv, gw[3], m_a_w_ukv, v_a_w_ukv, "adamw_ukv")]
    dw, mw, vw = (tuple(u[i] for u in upd) for i in range(3))
    d_ada, m_ada, v_ada = _adamw(w_ada, g_w_ada, m_w_ada, v_w_ada, "adamw_ada")
    d_b, m_b, v_b = (a.reshape(DEPTH, 3 * D_MODEL) for a in _adamw(
        _rows(b_ada, 48)[None], _rows(g_b_ada, 48)[None], _rows(m_b_ada, 48)[None], _rows(v_b_ada, 48)[None],
        "adamw_b_ada"))

    def small_rows(ng, gq, gkv, rb, fb, fg):
        per_layer = [dict(norm_g=ng[l], gq=gq[l], gkv=gkv[l], rb8=jnp.pad(rb[l], ((0, 3), (0, 384 - N_REL))),
                          fb=jnp.pad(fb[l], (0, LANE - 5))) for l in range(DEPTH)]
        return _small_payload(per_layer, fg, jnp.zeros((), F32))

    w_s = small_rows(norm_g, a_q_norm_g, a_kv_norm_g, b_rel_bias, c_forget_b, final_g)
    m_s = small_rows(m_norm_g, m_a_q_norm_g, m_a_kv_norm_g, m_b_rel_bias, m_c_forget_b, m_final_g)
    v_s = small_rows(v_norm_g, v_a_q_norm_g, v_a_kv_norm_g, v_b_rel_bias, v_c_forget_b, v_final_g)
    d_s, mo_s, vo_s = (a[0] for a in _adamw(w_s[None], tot[None], m_s[None], v_s[None], "adamw_small"))
    gs = _payload_split(tot)
    ds = _payload_split(d_s)
    ms = _payload_split(mo_s)
    vs = _payload_split(vo_s)

    def ordered(ada, b, sm, big):
        ng, gq, gkv, rb, fb, fg = sm
        b_in, b_out, b_uq, b_ukv = big
        return (ada, b, ng, b_in, gq, b_uq, gkv, b_ukv, rb, fb, b_out, fg)

    return (loss, dx.reshape(nb, seq, D_MODEL), *ordered(g_w_ada, g_b_ada, gs, gw), *ordered(d_ada, d_b, ds, dw),
            *ordered(m_ada, m_b, ms, mw), *ordered(v_ada, v_b, vs, vw))
```

```python
import functools

import jax
import jax.numpy as jnp
from jax import lax
from jax.experimental import pallas as pl
from jax.experimental.pallas import tpu as pltpu

F32 = jnp.float32
BF16 = jnp.bfloat16

D_MODEL = 1024
DEPTH = 2
EPS = 1e-6
NEG = -1e30
LOG2E = 1.4426950408889634
ROPE_THETA = 10000.0
A_ROPE = 32
A_Q_RANK = 384
A_KV_RANK = 256
REL_CLIP = 128
N_REL = 2 * REL_CLIP + 1
N_IN = 3621

ADAM_LR = 0.001
ADAM_B1 = 0.9
ADAM_B2 = 0.999
ADAM_EPS = 1e-08
ADAM_WD = 0.01
ADAM_STEP = 10

LANE = 128
VMEM_LIMIT = 56 * 1024 * 1024

NP_IN = 4352
Z_A = (0, 768)
Z_G = (768, 1920)
Z_QKV = tuple((1920 + 384 * i, 1920 + 384 * (i + 1)) for i in range(6))
Z_F = (4224, 4352)
IN_PIECES = ((0, 672, 96), (672, 1056, 0), (2016, 2336, 64), (3301, 3621, 64), (1056, 1376, 64), (1376, 1696, 64),
             (1696, 2016, 64), (2336, 2656, 64), (2656, 2976, 64), (2976, 3296, 64), (3296, 3301, 123))
D_CAT = 1152

TM = 512
T_CAUSAL = 256
T_BAND = 128
BAND_TILES = 5
N_DEV = 8

PAY_DMOD = (0, 96)
PAY_NORM = (96, 112)
PAY_GQ = (112, 120)
PAY_GKV = (120, 128)
PAY_RB = (128, 176)
PAY_FB = (176, 184)
PAY_FINAL = (184, 192)
PAY_LOSS = (192, 200)
PAY_ROWS = 200

PACK_COLS = 1024
PACK_ROWS = 2560
HALF_ROWS = PACK_ROWS // 2


def _params(sem=None):
    return pltpu.CompilerParams(dimension_semantics=sem, vmem_limit_bytes=VMEM_LIMIT)


def _lane_iota(shape):
    return lax.broadcasted_iota(jnp.int32, shape, len(shape) - 1)


def _silu(u):
    return u * jax.nn.sigmoid(u)


def _dsilu(u):
    s = jax.nn.sigmoid(u)
    return s * (1.0 + u * (1.0 - s))


def _rms(x, g):
    r = lax.rsqrt(jnp.mean(x * x, axis=-1, keepdims=True) + EPS)
    xh = x * r
    return xh * g, xh, r


def _rms_bwd(dy, xh, r, g):
    dxh = dy * g
    return r * (dxh - xh * jnp.mean(dxh * xh, axis=-1, keepdims=True))


def _rope(x, cos, sina, sinb):
    return x * cos + pltpu.roll(x, 16, 1) * sinb + pltpu.roll(x, LANE - 16, 1) * sina


def _rope_t(dy, cos, sina, sinb):
    return dy * cos + pltpu.roll(dy * sinb, LANE - 16, 1) + pltpu.roll(dy * sina, 16, 1)


def _split3(x):
    hi = x.astype(BF16)
    r1 = x - hi.astype(F32)
    mid = r1.astype(BF16)
    lo = (r1 - mid.astype(F32)).astype(BF16)
    return hi, mid, lo


def _dot(a, b):
    return jnp.dot(a, b, preferred_element_type=F32)


def _dot_nt(a, b):
    return lax.dot_general(a, b, (((1,), (1,)), ((), ())), preferred_element_type=F32)


def _dot_tn(a, b):
    return lax.dot_general(a, b, (((0,), (0,)), ((), ())), preferred_element_type=F32)


def _row_spec(cols):
    return pl.BlockSpec((TM, cols), lambda i: (i, 0))


def _full_spec(shape):
    return pl.BlockSpec(shape, lambda i: (0,) * len(shape))


def _ex_spec(tiles_per_ex):
    return pl.BlockSpec((1, 1, D_MODEL), lambda i: (i // tiles_per_ex, 0, 0))


def _ln_inproj(x, shift, scale, g, w_in_p, seq):
    t = x.shape[0]

    def body(x_ref, sh_ref, sc_ref, g_ref, w_ref, h_ref, za_ref, zg_ref, q0, q1, q2, q3, q4, q5, zf_ref):
        n, _, _ = _rms(x_ref[...], g_ref[...])
        h = (n * (1.0 + sc_ref[0]) + sh_ref[0]).astype(BF16)
        h_ref[...] = h
        za_ref[...] = _dot(h, w_ref[:, Z_A[0]:Z_A[1]])
        zg_ref[...] = _dot(h, w_ref[:, Z_G[0]:Z_G[1]])
        for ref, (c0, c1) in zip((q0, q1, q2, q3, q4, q5), Z_QKV):
            ref[...] = _dot(h, w_ref[:, c0:c1]).astype(BF16)
        zf_ref[...] = _dot(h, w_ref[:, Z_F[0]:Z_F[1]])

    tpe = seq // TM
    shapes = [jax.ShapeDtypeStruct((t, D_MODEL), BF16), jax.ShapeDtypeStruct((t, 768), F32),
              jax.ShapeDtypeStruct((t, D_CAT), F32)]
    shapes += [jax.ShapeDtypeStruct((t, 384), BF16)] * 6 + [jax.ShapeDtypeStruct((t, LANE), F32)]
    return pl.pallas_call(
        body, name="ln_inproj", grid=(t // TM,),
        in_specs=[_row_spec(D_MODEL), _ex_spec(tpe), _ex_spec(tpe), _full_spec((1, D_MODEL)),
                  _full_spec((D_MODEL, NP_IN))],
        out_specs=[_row_spec(D_MODEL), _row_spec(768), _row_spec(D_CAT)] + [_row_spec(384)] * 6 + [_row_spec(LANE)],
        out_shape=shapes, compiler_params=_params(("parallel",)),
    )(x, shift, scale, g, w_in_p)


def _a_up(za, gq, gkv, w_uq_p, w_ukv_p, cos, sina, sinb):
    t = za.shape[0]

    def body(za_ref, gq_ref, gkv_ref, wq_ref, wkv_ref, cos_ref, sa_ref, sb_ref, q_ref, k_ref, v_ref):
        cos_t, sa, sb = cos_ref[...], sa_ref[...], sb_ref[...]
        cqn, _, _ = _rms(za_ref[:, 0:384], gq_ref[...])
        q = _dot(cqn.astype(BF16), wq_ref[...])
        ckvn, _, _ = _rms(za_ref[:, 384:640], gkv_ref[...])
        kv = _dot(ckvn.astype(BF16), wkv_ref[...])
        kpe = za_ref[:, 640:768]
        kpe = _rope(kpe + pltpu.roll(kpe, 32, 1), cos_t, sa, sb).astype(BF16)
        for p in range(3):
            q_ref[:, 256 * p:256 * p + 128] = q[:, 256 * p:256 * p + 128].astype(BF16)
            q_ref[:, 256 * p + 128:256 * p + 256] = _rope(q[:, 256 * p + 128:256 * p + 256], cos_t, sa, sb).astype(BF16)
            k_ref[:, 256 * p:256 * p + 128] = kv[:, 128 * p:128 * p + 128].astype(BF16)
            k_ref[:, 256 * p + 128:256 * p + 256] = kpe
        v_ref[...] = kv[:, 384:768].astype(BF16)

    return pl.pallas_call(
        body, name="a_up", grid=(t // TM,),
        in_specs=[_row_spec(768), _full_spec((1, 384)), _full_spec((1, 256)), _full_spec((384, 768)),
                  _full_spec((256, 768)), _row_spec(LANE), _row_spec(LANE), _row_spec(LANE)],
        out_specs=[_row_spec(768), _row_spec(768), _row_spec(384)],
        out_shape=[jax.ShapeDtypeStruct((t, 768), BF16), jax.ShapeDtypeStruct((t, 768), BF16),
                   jax.ShapeDtypeStruct((t, 384), BF16)],
        compiler_params=_params(("parallel",)),
    )(za, gq, gkv, w_uq_p, w_ukv_p, cos, sina, sinb)


def _tri(n, upper):
    r = lax.broadcasted_iota(jnp.int32, (n, n), 0)
    c = lax.broadcasted_iota(jnp.int32, (n, n), 1)
    return jnp.where((c >= r) if upper else (c <= r), 1.0, 0.0).astype(BF16)


def _forget_fwd(zf, fb, nb, seq):
    blk = 256

    def body(zf_ref, fb_ref, f_ref):
        tri = _tri(blk, False)
        live = _lane_iota((blk, LANE)) < 5
        carry = jnp.zeros((1, LANE), F32)
        for i in range(seq // blk):
            u = zf_ref[i * blk:(i + 1) * blk, :] + fb_ref[...]
            lf = jnp.where(live, jnp.minimum(u, 0.0) - jnp.log(1.0 + jnp.exp(-jnp.abs(u))), 0.0)
            hi, mid, lo = _split3(lf)
            f_ref[i * blk:(i + 1) * blk, :] = (_dot(tri, hi) + _dot(tri, mid) + _dot(tri, lo) + carry) * LOG2E
            carry = carry + jnp.sum(lf, axis=0, keepdims=True)

    return pl.pallas_call(
        body, name="forget_fwd", grid=(nb,),
        in_specs=[pl.BlockSpec((seq, LANE), lambda b: (b, 0)), pl.BlockSpec((1, LANE), lambda b: (0, 0))],
        out_specs=pl.BlockSpec((seq, LANE), lambda b: (b, 0)),
        out_shape=jax.ShapeDtypeStruct((nb * seq, LANE), F32), compiler_params=_params(("parallel",)),
    )(zf, fb)


def _forget_bwd(dfq, dfk, zf, fb, nb, seq):
    blk = 256

    def body(dfq_ref, dfk_ref, zf_ref, fb_ref, dz_ref, gb_ref):
        @pl.when(pl.program_id(0) == 0)
        def _():
            gb_ref[...] = jnp.zeros_like(gb_ref)

        tri = _tri(blk, True)
        lane = _lane_iota((blk, LANE))
        wide = _lane_iota((blk, 384))
        live = lane < 5
        carry = jnp.zeros((1, LANE), F32)
        gsum = jnp.zeros((1, LANE), F32)
        for i in reversed(range(seq // blk)):
            d = dfk_ref[i * blk:(i + 1) * blk, :]
            dq = dfq_ref[i * blk:(i + 1) * blk, :]
            for hd in range(5):
                col = jnp.sum(jnp.where(wide == 64 * hd, dq, 0.0), axis=-1, keepdims=True)
                d = d + jnp.where(lane == hd, col, 0.0)
            hi, mid, lo = _split3(d)
            dlf = _dot(tri, hi) + _dot(tri, mid) + _dot(tri, lo) + carry
            carry = carry + jnp.sum(d, axis=0, keepdims=True)
            u = zf_ref[i * blk:(i + 1) * blk, :] + fb_ref[...]
            du = jnp.where(live, dlf * jax.nn.sigmoid(-u), 0.0)
            dz_ref[i * blk:(i + 1) * blk, :] = du.astype(BF16)
            gsum = gsum + jnp.sum(du, axis=0, keepdims=True)
        gb_ref[...] += jnp.broadcast_to(gsum, gb_ref.shape)

    return pl.pallas_call(
        body, name="forget_bwd", grid=(nb,),
        in_specs=[pl.BlockSpec((seq, 384), lambda b: (b, 0)), pl.BlockSpec((seq, LANE), lambda b: (b, 0)),
                  pl.BlockSpec((seq, LANE), lambda b: (b, 0)), pl.BlockSpec((1, LANE), lambda b: (0, 0))],
        out_specs=[pl.BlockSpec((seq, LANE), lambda b: (b, 0)), pl.BlockSpec((8, LANE), lambda b: (0, 0))],
        out_shape=[jax.ShapeDtypeStruct((nb * seq, LANE), BF16), jax.ShapeDtypeStruct((8, LANE), F32)],
        compiler_params=_params(("arbitrary",)),
    )(dfq, dfk, zf, fb)


def _gate_outproj(x, gate, oa, ob, oc, zg, w_out_p, seq):
    t = x.shape[0]

    def body(x_ref, gate_ref, oa_ref, ob_ref, oc_ref, zg_ref, w_ref, y_ref, xn_ref):
        y = jnp.zeros((TM, D_MODEL), F32)
        for i, o_ref in enumerate((oa_ref, ob_ref, oc_ref)):
            cat = (o_ref[...] * _silu(zg_ref[:, 384 * i:384 * (i + 1)])).astype(BF16)
            y = y + _dot(cat, w_ref[384 * i:384 * (i + 1), :])
        y_ref[...] = y
        xn_ref[...] = x_ref[...] + gate_ref[0] * y

    return pl.pallas_call(
        body, name="gate_outproj", grid=(t // TM,),
        in_specs=[_row_spec(D_MODEL), _ex_spec(seq // TM), _row_spec(384), _row_spec(384), _row_spec(384),
                  _row_spec(D_CAT), _full_spec((D_CAT, D_MODEL))],
        out_specs=[_row_spec(D_MODEL), _row_spec(D_MODEL)],
        out_shape=[jax.ShapeDtypeStruct((t, D_MODEL), F32)] * 2, compiler_params=_params(("parallel",)),
    )(x, gate, oa, ob, oc, zg, w_out_p)


def _final_loss(x, target, g):
    t = x.shape[0]

    def body(x_ref, t_ref, g_ref, dx_ref, loss_ref, gg_ref):
        @pl.when(pl.program_id(0) == 0)
        def _():
            loss_ref[...] = jnp.zeros_like(loss_ref)
            gg_ref[...] = jnp.zeros_like(gg_ref)

        gv = g_ref[...]
        out, xh, r = _rms(x_ref[...], gv)
        err = out - t_ref[...]
        loss_ref[...] += 0.5 * jnp.sum(jnp.mean(err * err, axis=-1, keepdims=True), axis=0, keepdims=True)
        dout = err / D_MODEL
        gg_ref[...] += jnp.broadcast_to(jnp.sum(dout * xh, axis=0, keepdims=True), gg_ref.shape)
        dx_ref[...] = _rms_bwd(dout, xh, r, gv)

    return pl.pallas_call(
        body, name="final_loss", grid=(t // TM,),
        in_specs=[_row_spec(D_MODEL), _row_spec(D_MODEL), _full_spec((1, D_MODEL))],
        out_specs=[_row_spec(D_MODEL), _full_spec((8, LANE)), _full_spec((8, D_MODEL))],
        out_shape=[jax.ShapeDtypeStruct((t, D_MODEL), F32), jax.ShapeDtypeStruct((8, LANE), F32),
                   jax.ShapeDtypeStruct((8, D_MODEL), F32)],
        compiler_params=_params(("arbitrary",)),
    )(x, target, g)


def _head_masks(kind, rows, dq, h):
    lq = _lane_iota((rows, dq))
    lv = _lane_iota((rows, LANE))
    mq = (lq >= 64 * h) & (lq < 64 * h + 64)
    if kind == "A":
        mq = mq | ((lq >= 128 + 32 * h) & (lq < 160 + 32 * h))
    return mq, (lv >= 64 * h) & (lv < 64 * h + 64)


def _tile_mask(kind, tile, m=None):
    row = lax.broadcasted_iota(jnp.int32, (tile, tile), 0)
    col = lax.broadcasted_iota(jnp.int32, (tile, tile), 1)
    if kind == "A":
        return (col >> 6) <= (row >> 6)
    if kind == "C":
        return col <= row
    first = (m == 0) & (row >= 64) & (col < 64)
    last = (m == BAND_TILES - 1) & (row < 64) & (col >= 64)
    return jnp.logical_not(first | last)


def _attn_scale(kind):
    return 96.0 ** -0.5 if kind == "A" else 0.125


def _attn_fwd(kind, q, k, v, aux, nb, seq):
    dq = q.shape[1] // 3
    tile = T_BAND if kind == "B" else T_CAUSAL
    nq = seq // tile
    scale = _attn_scale(kind)

    def body(*refs):
        if kind == "A":
            q_ref, k_ref, v_ref, o_ref, lse_ref = refs
            aux_ref = None
        else:
            q_ref, k_ref, v_ref, aux_ref, o_ref, lse_ref = refs
        qi = pl.program_id(2)
        q2 = q_ref[...]
        res = []
        for h in range(2):
            mq, _ = _head_masks(kind, tile, dq, h)
            qh = jnp.where(mq, q2, jnp.zeros_like(q2))

            def step(kj, carry, m=None, diag=False):
                mx, l, acc = carry
                ks = pl.ds(pl.multiple_of(kj * tile, tile), tile)
                kt = k_ref[ks, :]
                vt = v_ref[ks, :]
                s = _dot_nt(qh, kt) * scale
                if kind == "B":
                    s = jnp.where(_tile_mask("B", tile, m), s + aux_ref[h, m], NEG)
                if kind == "C":
                    s = s - aux_ref[0, h, pl.ds(kj, 1), :]
                if diag:
                    s = jnp.where(_tile_mask(kind, tile), s, NEG)
                mn = jnp.maximum(mx, jnp.max(s, axis=-1, keepdims=True))
                alpha = jnp.exp(mx - mn)
                p = jnp.exp(s - mn)
                l = alpha * l + jnp.sum(p, axis=-1, keepdims=True)
                acc = alpha * acc + _dot(p.astype(BF16), vt)
                return mn, l, acc

            init = (jnp.full((tile, 1), NEG, F32), jnp.zeros((tile, 1), F32), jnp.zeros((tile, LANE), F32))
            if kind == "B":
                m0 = jnp.maximum(BAND_TILES - 1 - qi, 0)
                mx, l, acc = lax.fori_loop(m0, BAND_TILES, lambda m, c: step(qi - (BAND_TILES - 1) + m, c, m=m), init)
            else:
                carry = lax.fori_loop(0, qi, lambda kj, c: step(kj, c), init)
                mx, l, acc = step(qi, carry, diag=True)
            res.append((acc / l, mx + jnp.log(l)))
        first = _lane_iota((tile, LANE)) < 64
        o_ref[...] = jnp.where(first, res[0][0], res[1][0])
        lse_ref[...] = jnp.where(first, res[0][1], res[1][1])

    in_specs = [pl.BlockSpec((tile, dq), lambda b, p, i: (b * nq + i, p)),
                pl.BlockSpec((seq, dq), lambda b, p, i: (b, p)),
                pl.BlockSpec((seq, LANE), lambda b, p, i: (b, p))]
    args = [q, k, v]
    if kind == "B":
        in_specs.append(pl.BlockSpec((2, BAND_TILES, tile, tile), lambda b, p, i: (p, 0, 0, 0)))
        args.append(aux)
    if kind == "C":
        in_specs.append(pl.BlockSpec((1, 2, nq, tile), lambda b, p, i: (b, p, 0, 0)))
        args.append(aux)
    out_spec = pl.BlockSpec((tile, LANE), lambda b, p, i: (b * nq + i, p))
    return pl.pallas_call(
        body, name="attn_fwd_" + kind, grid=(nb, 3, nq), in_specs=in_specs, out_specs=[out_spec, out_spec],
        out_shape=[jax.ShapeDtypeStruct((nb * seq, 384), F32)] * 2,
        compiler_params=_params(("parallel", "parallel", "parallel")),
    )(*args)


def _attn_bwd(kind, q, k, v, o, do, lse, aux, nb, seq):
    dq = q.shape[1] // 3
    tile = T_BAND if kind == "B" else T_CAUSAL
    nq = seq // tile
    scale = _attn_scale(kind)
    dqk_dtype = F32 if kind == "A" else BF16

    def body(*refs):
        dfr_ref = dfq_ref = dbt_ref = aux_ref = None
        if kind == "A":
            q_ref, k_ref, v_ref, o_ref, do_ref, lse_ref, dq_ref, dk_ref, dv_ref, dk_acc, dv_acc = refs
        elif kind == "B":
            q_ref, k_ref, v_ref, o_ref, do_ref, lse_ref, aux_ref, dq_ref, dk_ref, dv_ref, dbt_ref, dk_acc, dv_acc = refs
        else:
            (q_ref, k_ref, v_ref, o_ref, do_ref, lse_ref, aux_ref, dq_ref, dk_ref, dv_ref, dfr_ref, dfq_ref,
             dk_acc, dv_acc) = refs
        dk_acc[...] = jnp.zeros_like(dk_acc)
        dv_acc[...] = jnp.zeros_like(dv_acc)
        if kind == "C":
            dfr_ref[...] = jnp.zeros_like(dfr_ref)
        if kind == "B":
            @pl.when(pl.program_id(1) == 0)
            def _():
                dbt_ref[...] = jnp.zeros_like(dbt_ref)

        def q_step(qi, _):
            qs = pl.ds(pl.multiple_of(qi * tile, tile), tile)
            q2 = q_ref[qs, :]
            do2 = do_ref[qs, :]
            o2 = o_ref[qs, :]
            lse2 = lse_ref[qs, :]
            dq_tot = jnp.zeros((tile, dq), F32)
            row_sums = []
            for h in range(2):
                mq, mv = _head_masks(kind, tile, dq, h)
                qh = jnp.where(mq, q2, jnp.zeros_like(q2))
                doh = jnp.where(mv, do2, 0.0)
                dob = doh.astype(BF16)
                delta = jnp.sum(doh * o2, axis=-1, keepdims=True)
                lseh = jnp.max(jnp.where(mv, lse2, NEG), axis=-1, keepdims=True)

                def step(kj, carry, m=None, diag=False):
                    dq_t, rs = carry
                    ks = pl.ds(pl.multiple_of(kj * tile, tile), tile)
                    kt = k_ref[ks, :]
                    vt = v_ref[ks, :]
                    s = _dot_nt(qh, kt) * scale
                    if kind == "B":
                        s = jnp.where(_tile_mask("B", tile, m), s + aux_ref[h, m], NEG)
                    if kind == "C":
                        s = s - aux_ref[0, h, pl.ds(kj, 1), :]
                    if diag:
                        s = jnp.where(_tile_mask(kind, tile), s, NEG)
                    p = jnp.exp(s - lseh)
                    ds = p * (_dot_nt(dob, vt) - delta)
                    if kind == "B":
                        dbt_ref[h, m] += ds
                    if kind == "C":
                        dfr_ref[0, h, pl.ds(kj, 1), :] -= jnp.sum(ds, axis=0, keepdims=True)
                        rs = rs + jnp.sum(ds, axis=-1, keepdims=True)
                    dss = (ds * scale).astype(BF16)
                    dv_acc[ks, :] += _dot_tn(p.astype(BF16), dob)
                    dk_acc[ks, :] += _dot_tn(dss, qh)
                    return dq_t + jnp.where(mq, _dot(dss, kt), 0.0), rs

                zero = (jnp.zeros((tile, dq), F32), jnp.zeros((tile, 1), F32))
                if kind == "B":
                    m0 = jnp.maximum(BAND_TILES - 1 - qi, 0)
                    dq_h, rs_h = lax.fori_loop(m0, BAND_TILES, lambda m, c: step(qi - (BAND_TILES - 1) + m, c, m=m), zero)
                else:
                    carry = lax.fori_loop(0, qi, lambda kj, c: step(kj, c), zero)
                    dq_h, rs_h = step(qi, carry, diag=True)
                dq_tot = dq_tot + dq_h
                row_sums.append(rs_h)
            dq_ref[qs, :] = dq_tot.astype(dqk_dtype)
            if kind == "C":
                dfq_ref[qs, :] = jnp.where(_lane_iota((tile, LANE)) < 64, row_sums[0], row_sums[1])
            return 0

        lax.fori_loop(0, nq, q_step, 0)
        dk_ref[...] = dk_acc[...].astype(dqk_dtype)
        dv_ref[...] = dv_acc[...].astype(BF16)

    def seq_spec(cols):
        return pl.BlockSpec((seq, cols), lambda p, b: (b, p))

    in_specs = [seq_spec(dq), seq_spec(dq), seq_spec(LANE), seq_spec(LANE), seq_spec(LANE), seq_spec(LANE)]
    args = [q, k, v, o, do, lse]
    out_specs = [seq_spec(dq), seq_spec(dq), seq_spec(LANE)]
    out_shape = [jax.ShapeDtypeStruct((nb * seq, 3 * dq), dqk_dtype)] * 2 + [jax.ShapeDtypeStruct((nb * seq, 384), BF16)]
    if kind == "B":
        spec = pl.BlockSpec((2, BAND_TILES, tile, tile), lambda p, b: (p, 0, 0, 0))
        in_specs.append(spec)
        args.append(aux)
        out_specs.append(spec)
        out_shape.append(jax.ShapeDtypeStruct((6, BAND_TILES, tile, tile), F32))
    if kind == "C":
        spec = pl.BlockSpec((1, 2, nq, tile), lambda p, b: (b, p, 0, 0))
        in_specs.append(spec)
        args.append(aux)
        out_specs += [spec, seq_spec(LANE)]
        out_shape += [jax.ShapeDtypeStruct((nb, 6, nq, tile), F32), jax.ShapeDtypeStruct((nb * seq, 384), F32)]
    return pl.pallas_call(
        body, name="attn_bwd_" + kind, grid=(3, nb), in_specs=in_specs, out_specs=out_specs, out_shape=out_shape,
        scratch_shapes=[pltpu.VMEM((seq, dq), F32), pltpu.VMEM((seq, LANE), F32)],
        compiler_params=_params(("arbitrary", "arbitrary")),
    )(*args)


BAND_W = BAND_TILES * T_BAND


def _segments(kind, qi, tile):
    r0 = qi * tile
    if kind == "B":
        lo = max(qi - (BAND_TILES - 1), 0) * tile
        return [(lo, r0 + tile, False, lo - (qi - (BAND_TILES - 1)) * tile)]
    return ([(0, r0, False, 0)] if qi else []) + [(r0, r0 + tile, True, 0)]


def _scores(kind, qh, k_ref, aux_ref, h, seg, tile, scale):
    a, b, diag, c0 = seg
    s = _dot_nt(qh, k_ref[a:b, :]) * (scale * LOG2E)
    if kind == "B":
        return s + aux_ref[h, :, c0:BAND_W]
    if kind == "C":
        s = s - aux_ref[0, h, :, a:b]
    if diag:
        s = jnp.where(_tile_mask(kind, tile), s, NEG)
    return s


def _second_head(kind, pair, fn):
    del kind, pair
    fn()


def _attn_fwd(kind, q, k, v, aux, nb, seq):
    dq = q.shape[1] // 3
    tile = T_BAND if kind == "B" else T_CAUSAL
    nq = seq // tile
    scale = _attn_scale(kind)

    def body(*refs):
        if kind == "A":
            q_ref, k_ref, v_ref, o_ref, lse_ref = refs
            aux_ref = None
        else:
            q_ref, k_ref, v_ref, aux_ref, o_ref, lse_ref = refs
        pair = pl.program_id(1)
        for qi in range(nq):
            rows = slice(qi * tile, (qi + 1) * tile)
            segs = _segments(kind, qi, tile)

            def head(h, rows=rows, segs=segs):
                q2 = q_ref[rows, :]
                mq, mv = _head_masks(kind, tile, dq, h)
                qh = jnp.where(mq, q2, jnp.zeros_like(q2))
                ss = [_scores(kind, qh, k_ref, aux_ref, h, seg, tile, scale) for seg in segs]
                mx = functools.reduce(jnp.maximum, [jnp.max(s, axis=-1, keepdims=True) for s in ss])
                ps = [jnp.exp2(s - mx) for s in ss]
                l = functools.reduce(jnp.add, [jnp.sum(p, axis=-1, keepdims=True) for p in ps])
                acc = functools.reduce(jnp.add, [_dot(p.astype(BF16), v_ref[seg[0]:seg[1], :]) for p, seg in zip(ps, segs)])
                o_h = jnp.where(mv, acc / l, 0.0)
                lse_h = jnp.where(mv, mx + jnp.log(l) * LOG2E, 0.0)
                if h == 0:
                    o_ref[rows, :] = o_h
                    lse_ref[rows, :] = lse_h
                else:
                    o_ref[rows, :] += o_h
                    lse_ref[rows, :] += lse_h

            head(0)
            _second_head(kind, pair, functools.partial(head, 1))

    def seq_spec(cols):
        return pl.BlockSpec((seq, cols), lambda b, p: (b, p))

    in_specs = [seq_spec(dq), seq_spec(dq), seq_spec(LANE)]
    args = [q, k, v]
    if kind == "B":
        in_specs.append(pl.BlockSpec((2, tile, BAND_W), lambda b, p: (p, 0, 0)))
        args.append(aux)
    if kind == "C":
        in_specs.append(pl.BlockSpec((1, 2, 1, seq), lambda b, p: (b, p, 0, 0)))
        args.append(aux)
    return pl.pallas_call(
        body, name="attn_fwd_" + kind, grid=(nb, 3), in_specs=in_specs, out_specs=[seq_spec(LANE), seq_spec(LANE)],
        out_shape=[jax.ShapeDtypeStruct((nb * seq, 384), F32)] * 2, compiler_params=_params(("parallel", "parallel")),
    )(*args)


def _attn_bwd(kind, q, k, v, o, do, lse, aux, nb, seq):
    dq = q.shape[1] // 3
    tile = T_BAND if kind == "B" else T_CAUSAL
    nq = seq // tile
    scale = _attn_scale(kind)
    dqk_dtype = F32 if kind == "A" else BF16

    def body(*refs):
        dfr_ref = dfq_ref = dbt_ref = aux_ref = None
        if kind == "A":
            q_ref, k_ref, v_ref, o_ref, do_ref, lse_ref, dq_ref, dk_ref, dv_ref, dkt_acc, dvt_acc = refs
        elif kind == "B":
            q_ref, k_ref, v_ref, o_ref, do_ref, lse_ref, aux_ref, dq_ref, dk_ref, dv_ref, dbt_ref, dkt_acc, dvt_acc = refs
        else:
            (q_ref, k_ref, v_ref, o_ref, do_ref, lse_ref, aux_ref, dq_ref, dk_ref, dv_ref, dfr_ref, dfq_ref,
             dkt_acc, dvt_acc) = refs
        dkt_acc[...] = jnp.zeros_like(dkt_acc)
        dvt_acc[...] = jnp.zeros_like(dvt_acc)
        if kind == "C":
            dfr_ref[...] = jnp.zeros_like(dfr_ref)
        if kind == "B":
            @pl.when(pl.program_id(1) == 0)
            def _():
                dbt_ref[...] = jnp.zeros_like(dbt_ref)

        pair = pl.program_id(0)
        for qi in range(nq):
            rows = slice(qi * tile, (qi + 1) * tile)
            segs = _segments(kind, qi, tile)

            def head(h, rows=rows, segs=segs):
                q2 = q_ref[rows, :]
                do2 = do_ref[rows, :]
                mq, mv = _head_masks(kind, tile, dq, h)
                qh = jnp.where(mq, q2, jnp.zeros_like(q2))
                doh = jnp.where(mv, do2, 0.0)
                dob = doh.astype(BF16)
                qht = qh.astype(F32).T.astype(BF16)
                dobt = doh.T.astype(BF16)
                delta = jnp.sum(doh * o_ref[rows, :], axis=-1, keepdims=True)
                lseh = jnp.max(jnp.where(mv, lse_ref[rows, :], NEG), axis=-1, keepdims=True)
                rs = jnp.zeros((tile, 1), F32)
                dq_h = jnp.zeros((tile, dq), F32)
                for seg in segs:
                    a, b, _, c0 = seg
                    p = jnp.exp2(_scores(kind, qh, k_ref, aux_ref, h, seg, tile, scale) - lseh)
                    ds = p * (_dot_nt(dob, v_ref[a:b, :]) - delta)
                    if kind == "B":
                        dbt_ref[h, :, c0:BAND_W] += ds
                    if kind == "C":
                        dfr_ref[0, h, :, a:b] -= jnp.sum(ds, axis=0, keepdims=True)
                        rs = rs + jnp.sum(ds, axis=-1, keepdims=True)
                    dss = (ds * scale).astype(BF16)
                    dvt_acc[:, a:b] += _dot(dobt, p.astype(BF16))
                    dkt_acc[:, a:b] += _dot(qht, dss)
                    dq_h = dq_h + _dot(dss, k_ref[a:b, :])
                dq_h = jnp.where(mq, dq_h, 0.0).astype(dqk_dtype)
                if h == 0:
                    dq_ref[rows, :] = dq_h
                else:
                    dq_ref[rows, :] += dq_h
                if kind == "C":
                    if h == 0:
                        dfq_ref[rows, :] = jnp.where(mv, rs, 0.0)
                    else:
                        dfq_ref[rows, :] += jnp.where(mv, rs, 0.0)

            head(0)
            _second_head(kind, pair, functools.partial(head, 1))
        for j in range(seq // 256):
            cols = slice(256 * j, 256 * (j + 1))
            dk_ref[cols, :] = dkt_acc[:, cols].T.astype(dqk_dtype)
            dv_ref[cols, :] = dvt_acc[:, cols].T.astype(BF16)

    def seq_spec(cols):
        return pl.BlockSpec((seq, cols), lambda p, b: (b, p))

    in_specs = [seq_spec(dq), seq_spec(dq), seq_spec(LANE), seq_spec(LANE), seq_spec(LANE), seq_spec(LANE)]
    args = [q, k, v, o, do, lse]
    out_specs = [seq_spec(dq), seq_spec(dq), seq_spec(LANE)]
    out_shape = [jax.ShapeDtypeStruct((nb * seq, 3 * dq), dqk_dtype)] * 2 + [jax.ShapeDtypeStruct((nb * seq, 384), BF16)]
    if kind == "B":
        spec = pl.BlockSpec((2, tile, BAND_W), lambda p, b: (p, 0, 0))
        in_specs.append(spec)
        args.append(aux)
        out_specs.append(spec)
        out_shape.append(jax.ShapeDtypeStruct((6, tile, BAND_W), F32))
    if kind == "C":
        spec = pl.BlockSpec((1, 2, 1, seq), lambda p, b: (b, p, 0, 0))
        in_specs.append(spec)
        args.append(aux)
        out_specs += [spec, seq_spec(LANE)]
        out_shape += [jax.ShapeDtypeStruct((nb, 6, 1, seq), F32), jax.ShapeDtypeStruct((nb * seq, 384), F32)]
    return pl.pallas_call(
        body, name="attn_bwd_" + kind, grid=(3, nb), in_specs=in_specs, out_specs=out_specs, out_shape=out_shape,
        scratch_shapes=[pltpu.VMEM((dq, seq), F32), pltpu.VMEM((LANE, seq), F32)],
        compiler_params=_params(("arbitrary", "arbitrary")),
    )(*args)


BIAS_FLAT = T_BAND * BAND_W
BIAS_CHUNK = 4 * BAND_W


def _rel_onehot(chunk):
    lane = lax.broadcasted_iota(jnp.int32, (384, chunk), 1)
    r = lax.broadcasted_iota(jnp.int32, (384, chunk), 0)
    sub = jnp.where(lane >= BAND_W, 1, 0) + jnp.where(lane >= 2 * BAND_W, 1, 0) + jnp.where(lane >= 3 * BAND_W, 1, 0)
    i = pl.program_id(0) * 4 + sub
    col = lane - sub * BAND_W
    idx = jnp.clip(BAND_W - T_BAND + i - col, -REL_CLIP, REL_CLIP) + REL_CLIP
    return jnp.where(idx == r, 1.0, 0.0).astype(BF16)


BIAS_G = 768
BIAS_EDGE = BIAS_G - N_REL


def _bias_line(rel_bias):
    g = jnp.concatenate([jnp.broadcast_to(rel_bias[:, N_REL - 1:], (rel_bias.shape[0], BIAS_EDGE)),
                         jnp.flip(rel_bias, axis=1)], axis=1)
    return jnp.pad(g, ((0, 8 - g.shape[0]), (0, 0)))


def _bias_unline(dg):
    return jnp.flip(dg[:, BIAS_EDGE:], axis=1)


def _bias_expand(g8):
    def body(g_ref, out_ref):
        line = jnp.broadcast_to(g_ref[0] * LOG2E, (T_BAND, BIAS_G))
        slab = pltpu.roll(line, 1, 1, stride=1, stride_axis=0)[:, LANE:BIAS_G]
        row = lax.broadcasted_iota(jnp.int32, (T_BAND, BAND_W), 0)
        col = lax.broadcasted_iota(jnp.int32, (T_BAND, BAND_W), 1)
        hidden = ((row >= 64) & (col < 64)) | ((row < 64) & (col >= BAND_W - 64))
        out_ref[0] = jnp.where(hidden, NEG, slab)

    return pl.pallas_call(
        body, name="bias_expand", grid=(8,), in_specs=[pl.BlockSpec((1, 1, BIAS_G), lambda h: (h, 0, 0))],
        out_specs=pl.BlockSpec((1, T_BAND, BAND_W), lambda h: (h, 0, 0)),
        out_shape=jax.ShapeDtypeStruct((8, T_BAND, BAND_W), F32), compiler_params=_params(("parallel",)),
    )(g8.reshape(8, 1, BIAS_G))


def _bias_reduce(d_slab):
    def body(d_ref, out_ref):
        r = lax.broadcasted_iota(jnp.int32, (T_BAND, T_BAND), 0)
        k = lax.broadcasted_iota(jnp.int32, (T_BAND, T_BAND), 1)
        flip = jnp.where(r + k == T_BAND - 1, 1.0, 0.0).astype(BF16)
        hi, mid, lo = _split3(d_ref[0])
        d_rev = _dot(flip, hi) + _dot(flip, mid) + _dot(flip, lo)
        wide = jnp.concatenate([jnp.zeros((T_BAND, LANE), F32), d_rev, jnp.zeros((T_BAND, 2 * LANE), F32)], axis=1)
        skew = pltpu.roll(wide, 0, 1, stride=1, stride_axis=0)
        dg = jnp.sum(skew, axis=0, keepdims=True)[:, LANE:LANE + BIAS_G]
        lane = _lane_iota((1, BIAS_G))
        clipped = jnp.sum(jnp.where(lane <= BIAS_EDGE, dg, 0.0), axis=1, keepdims=True)
        out_ref[0] = jnp.where(lane == BIAS_EDGE, clipped, dg)

    return pl.pallas_call(
        body, name="bias_reduce", grid=(8,), in_specs=[pl.BlockSpec((1, T_BAND, BAND_W), lambda h: (h, 0, 0))],
        out_specs=pl.BlockSpec((1, 1, BIAS_G), lambda h: (h, 0, 0)),
        out_shape=jax.ShapeDtypeStruct((8, 1, BIAS_G), F32), compiler_params=_params(("parallel",)),
    )(d_slab).reshape(8, BIAS_G)


def _outproj_bwd(dxn, y, gate, oa, ob, oc, zg, w_out_p, w_out_pt, nb, seq):
    t = dxn.shape[0]
    tpe = seq // TM

    def body(dxn_ref, y_ref, gate_ref, oa_ref, ob_ref, oc_ref, zg_ref, w_ref, wt_ref,
             doa_ref, dob_ref, doc_ref, dzg_ref, gw_ref, dgate_ref):
        i = pl.program_id(0)

        @pl.when(i == 0)
        def _():
            gw_ref[...] = jnp.zeros_like(gw_ref)

        @pl.when(i % tpe == 0)
        def _():
            dgate_ref[...] = jnp.zeros_like(dgate_ref)

        dxn_t = dxn_ref[...]
        dgate_ref[0] += jnp.sum(dxn_t * y_ref[...], axis=0, keepdims=True)
        dy = (dxn_t * gate_ref[0]).astype(BF16)
        for gi, (o_ref, do_ref) in enumerate(((oa_ref, doa_ref), (ob_ref, dob_ref), (oc_ref, doc_ref))):
            cols = slice(384 * gi, 384 * (gi + 1))
            u = zg_ref[:, cols]
            o_t = o_ref[...]
            su = _silu(u)
            dcat = _dot(dy, wt_ref[:, cols])
            do_ref[...] = dcat * su
            dzg_ref[:, cols] = (dcat * o_t * _dsilu(u)).astype(BF16)
            gw_ref[cols, :] += _dot_tn((o_t * su).astype(BF16), dy)

    return pl.pallas_call(
        body, name="outproj_bwd", grid=(t // TM,),
        in_specs=[_row_spec(D_MODEL), _row_spec(D_MODEL), _ex_spec(tpe), _row_spec(384), _row_spec(384), _row_spec(384),
                  _row_spec(D_CAT), _full_spec((D_CAT, D_MODEL)), _full_spec((D_MODEL, D_CAT))],
        out_specs=[_row_spec(384), _row_spec(384), _row_spec(384), _row_spec(D_CAT), _full_spec((D_CAT, D_MODEL)),
                   _ex_spec(tpe)],
        out_shape=[jax.ShapeDtypeStruct((t, 384), F32)] * 3 + [jax.ShapeDtypeStruct((t, D_CAT), BF16),
                                                                jax.ShapeDtypeStruct((D_CAT, D_MODEL), F32),
                                                                jax.ShapeDtypeStruct((nb, 1, D_MODEL), F32)],
        compiler_params=_params(("arbitrary",)),
    )(dxn, y, gate, oa, ob, oc, zg, w_out_p, w_out_pt)


def _a_up_bwd(dqa, dka, dva, za, gq, gkv, w_uq_pt, w_ukv_pt, cos, sina, sinb):
    t = za.shape[0]

    def body(dq_ref, dk_ref, dv_ref, za_ref, gq_ref, gkv_ref, wqt_ref, wkvt_ref, cos_ref, sa_ref, sb_ref,
             dza_ref, gwq_ref, gwkv_ref, ggq_ref, ggkv_ref, dqb, dkvb):
        @pl.when(pl.program_id(0) == 0)
        def _():
            gwq_ref[...] = jnp.zeros_like(gwq_ref)
            gwkv_ref[...] = jnp.zeros_like(gwkv_ref)
            ggq_ref[...] = jnp.zeros_like(ggq_ref)
            ggkv_ref[...] = jnp.zeros_like(ggkv_ref)

        cos_t, sa, sb = cos_ref[...], sa_ref[...], sb_ref[...]
        dkpe = jnp.zeros((TM, LANE), F32)
        for p in range(3):
            dqb[:, 256 * p:256 * p + 128] = dq_ref[:, 256 * p:256 * p + 128].astype(BF16)
            dqb[:, 256 * p + 128:256 * p + 256] = _rope_t(dq_ref[:, 256 * p + 128:256 * p + 256], cos_t, sa, sb).astype(BF16)
            dkvb[:, 128 * p:128 * p + 128] = dk_ref[:, 256 * p:256 * p + 128].astype(BF16)
            dkpe = dkpe + dk_ref[:, 256 * p + 128:256 * p + 256]
        dkvb[:, 384:768] = dv_ref[...]
        dkpe = _rope_t(dkpe, cos_t, sa, sb)
        dkpe = jnp.where(_lane_iota((TM, LANE)) < A_ROPE, dkpe + pltpu.roll(dkpe, LANE - 32, 1), 0.0)

        gqv = gq_ref[...]
        cqn, cqh, rq = _rms(za_ref[:, 0:384], gqv)
        dq_t = dqb[...]
        gwq_ref[...] += _dot_tn(cqn.astype(BF16), dq_t)
        dcqn = _dot(dq_t, wqt_ref[...])
        ggq_ref[...] += jnp.broadcast_to(jnp.sum(dcqn * cqh, axis=0, keepdims=True), ggq_ref.shape)
        dza_ref[:, 0:384] = _rms_bwd(dcqn, cqh, rq, gqv).astype(BF16)

        gkvv = gkv_ref[...]
        ckvn, ckvh, rkv = _rms(za_ref[:, 384:640], gkvv)
        dkv_t = dkvb[...]
        gwkv_ref[...] += _dot_tn(ckvn.astype(BF16), dkv_t)
        dckvn = _dot(dkv_t, wkvt_ref[...])
        ggkv_ref[...] += jnp.broadcast_to(jnp.sum(dckvn * ckvh, axis=0, keepdims=True), ggkv_ref.shape)
        dza_ref[:, 384:640] = _rms_bwd(dckvn, ckvh, rkv, gkvv).astype(BF16)
        dza_ref[:, 640:768] = dkpe.astype(BF16)

    return pl.pallas_call(
        body, name="a_up_bwd", grid=(t // TM,),
        in_specs=[_row_spec(768), _row_spec(768), _row_spec(384), _row_spec(768), _full_spec((1, 384)),
                  _full_spec((1, 256)), _full_spec((768, 384)), _full_spec((768, 256)), _row_spec(LANE), _row_spec(LANE),
                  _row_spec(LANE)],
        out_specs=[_row_spec(768), _full_spec((384, 768)), _full_spec((256, 768)), _full_spec((8, 384)),
                   _full_spec((8, 256))],
        out_shape=[jax.ShapeDtypeStruct((t, 768), BF16), jax.ShapeDtypeStruct((384, 768), F32),
                   jax.ShapeDtypeStruct((256, 768), F32), jax.ShapeDtypeStruct((8, 384), F32),
                   jax.ShapeDtypeStruct((8, 256), F32)],
        scratch_shapes=[pltpu.VMEM((TM, 768), BF16), pltpu.VMEM((TM, 768), BF16)],
        compiler_params=_params(("arbitrary",)),
    )(dqa, dka, dva, za, gq, gkv, w_uq_pt, w_ukv_pt, cos, sina, sinb)


def _dz_cols():
    return (Z_A, Z_G) + Z_QKV + (Z_F,)


def _inproj_bwd_dx(dz, dxn, x, shift, scale, g, w_in_pt, nb, seq):
    t = x.shape[0]
    tpe = seq // TM
    cols = _dz_cols()

    def body(*refs):
        dz_refs = refs[:len(cols)]
        dxn_ref, x_ref, sh_ref, sc_ref, g_ref, wt_ref, dx_ref, dsh_ref, dsc_ref, dg_ref = refs[len(cols):]
        i = pl.program_id(0)

        @pl.when(i == 0)
        def _():
            dg_ref[...] = jnp.zeros_like(dg_ref)

        @pl.when(i % tpe == 0)
        def _():
            dsh_ref[...] = jnp.zeros_like(dsh_ref)
            dsc_ref[...] = jnp.zeros_like(dsc_ref)

        dh = jnp.zeros((TM, D_MODEL), F32)
        for ref, (c0, c1) in zip(dz_refs, cols):
            dh = dh + _dot_nt(ref[...], wt_ref[:, c0:c1])
        gv = g_ref[...]
        n, xh, r = _rms(x_ref[...], gv)
        dsh_ref[0] += jnp.sum(dh, axis=0, keepdims=True)
        dsc_ref[0] += jnp.sum(dh * n, axis=0, keepdims=True)
        dn = dh * (1.0 + sc_ref[0])
        dg_ref[...] += jnp.broadcast_to(jnp.sum(dn * xh, axis=0, keepdims=True), dg_ref.shape)
        dx_ref[...] = dxn_ref[...] + _rms_bwd(dn, xh, r, gv)

    in_specs = [_row_spec(c1 - c0) for c0, c1 in cols]
    in_specs += [_row_spec(D_MODEL), _row_spec(D_MODEL), _ex_spec(tpe), _ex_spec(tpe), _full_spec((1, D_MODEL)),
                 _full_spec((D_MODEL, NP_IN))]
    return pl.pallas_call(
        body, name="inproj_bwd_dx", grid=(t // TM,), in_specs=in_specs,
        out_specs=[_row_spec(D_MODEL), _ex_spec(tpe), _ex_spec(tpe), _full_spec((8, D_MODEL))],
        out_shape=[jax.ShapeDtypeStruct((t, D_MODEL), F32), jax.ShapeDtypeStruct((nb, 1, D_MODEL), F32),
                   jax.ShapeDtypeStruct((nb, 1, D_MODEL), F32), jax.ShapeDtypeStruct((8, D_MODEL), F32)],
        compiler_params=_params(("arbitrary",)),
    )(*dz, dxn, x, shift, scale, g, w_in_pt)


def _inproj_bwd_dw(h, dz, name, layer, both=None):
    t = h.shape[0]
    widths = [d.shape[1] for d in dz]
    total = sum(widths)

    def body(*refs):
        h_ref = refs[0]
        dz_refs = refs[1:1 + len(dz)]
        gw_ref = refs[-1]

        @pl.when(pl.program_id(0) == 0)
        def _():
            gw_ref[...] = jnp.zeros_like(gw_ref)

        h_t = h_ref[...]
        c0 = 0
        for ref, w in zip(dz_refs, widths):
            gw_ref[0, :, c0:c0 + w] += _dot_tn(h_t, ref[...])
            c0 += w

    in_specs = [_row_spec(D_MODEL)] + [_row_spec(w) for w in widths]
    args = [h, *dz]
    aliases = {}
    if both is not None:
        in_specs.append(pl.BlockSpec(memory_space=pl.ANY))
        aliases = {len(args): 0}
        args.append(both)
    return pl.pallas_call(
        body, name=name, grid=(t // TM,), in_specs=in_specs,
        out_specs=pl.BlockSpec((1, D_MODEL, total), lambda i: (layer, 0, 0)),
        out_shape=jax.ShapeDtypeStruct((DEPTH, D_MODEL, total), F32), input_output_aliases=aliases,
        compiler_params=_params(("arbitrary",)),
    )(*args)


def _ada_fwd(c_all, w_ada, b_cols):
    n = c_all.shape[0]
    cols = w_ada.shape[2]

    def body(c_ref, w_ref, b_ref, out_ref):
        act = _silu(c_ref[...]).astype(BF16)
        out_ref[0] = _dot(act, w_ref[0].astype(BF16)) + b_ref[0]

    return pl.pallas_call(
        body, name="ada_fwd", grid=(DEPTH,),
        in_specs=[pl.BlockSpec((n, D_MODEL), lambda l: (0, 0)), pl.BlockSpec((1, D_MODEL, cols), lambda l: (l, 0, 0)),
                  pl.BlockSpec((1, 1, cols), lambda l: (l, 0, 0))],
        out_specs=pl.BlockSpec((1, n, cols), lambda l: (l, 0, 0)),
        out_shape=jax.ShapeDtypeStruct((DEPTH, n, cols), F32), compiler_params=_params(("parallel",)),
    )(c_all, w_ada, b_cols)


def _ada_bwd(c_all, dmod_cols, dmod_all):
    n = c_all.shape[0]
    cols = dmod_cols.shape[2]

    def body(c_ref, dc_ref, da_ref, gw_ref, gb_ref):
        act = _silu(c_ref[...]).astype(BF16)
        gw_ref[0] = _dot_tn(act, dc_ref[0].astype(BF16))
        gb_ref[0] = jnp.sum(da_ref[0], axis=0, keepdims=True)

    return pl.pallas_call(
        body, name="ada_bwd", grid=(DEPTH,),
        in_specs=[pl.BlockSpec((n, D_MODEL), lambda l: (0, 0)), pl.BlockSpec((1, n, cols), lambda l: (l, 0, 0)),
                  pl.BlockSpec((1, n, 3 * D_MODEL), lambda l: (l, 0, 0))],
        out_specs=[pl.BlockSpec((1, D_MODEL, cols), lambda l: (l, 0, 0)),
                   pl.BlockSpec((1, 1, 3 * D_MODEL), lambda l: (l, 0, 0))],
        out_shape=[jax.ShapeDtypeStruct((DEPTH, D_MODEL, cols), F32), jax.ShapeDtypeStruct((DEPTH, 1, 3 * D_MODEL), F32)],
        compiler_params=_params(("parallel",)),
    )(c_all, dmod_cols, dmod_all)


def _sum_blocks(parts, name):
    n, rows, cols = parts.shape
    tr = rows if rows <= 256 else 8 * next(d for d in range(32, 0, -1) if (rows // 8) % d == 0)

    def body(p_ref, out_ref):
        acc = p_ref[0].astype(F32)
        for k in range(1, n):
            acc = acc + p_ref[k].astype(F32)
        out_ref[...] = acc

    return pl.pallas_call(
        body, name=name, grid=(rows // tr,), in_specs=[pl.BlockSpec((n, tr, cols), lambda i: (0, i, 0))],
        out_specs=pl.BlockSpec((tr, cols), lambda i: (i, 0)), out_shape=jax.ShapeDtypeStruct((rows, cols), F32),
        compiler_params=_params(("parallel",)),
    )(parts)


def _adamw(w, g, m, v, name):
    nl, rows, cols = w.shape
    if rows * cols <= 64 * 1024:
        tr = rows
        tl = next(t for t in range(nl, 0, -1) if nl % t == 0 and t * max(rows, 8) * cols <= 512 * 1024)
    else:
        tl = 1
        tr = next(t for t in (rows, 256, 128, 64, 32, 16, 8) if rows % t == 0 and t * cols <= 256 * 1024)

    def body(w_ref, g_ref, m_ref, v_ref, d_ref, mo_ref, vo_ref):
        gv = g_ref[...]
        mn = ADAM_B1 * m_ref[...] + (1.0 - ADAM_B1) * gv
        vn = ADAM_B2 * v_ref[...] + (1.0 - ADAM_B2) * jnp.square(gv)
        m_hat = mn / (1.0 - ADAM_B1 ** ADAM_STEP)
        v_hat = vn / (1.0 - ADAM_B2 ** ADAM_STEP)
        d_ref[...] = -ADAM_LR * (m_hat / (jnp.sqrt(v_hat) + ADAM_EPS) + ADAM_WD * w_ref[...])
        mo_ref[...] = mn
        vo_ref[...] = vn

    spec = pl.BlockSpec((tl, tr, cols), lambda l, i: (l, i, 0))
    return pl.pallas_call(
        body, name=name, grid=(nl // tl, rows // tr), in_specs=[spec] * 4, out_specs=[spec] * 3,
        out_shape=[jax.ShapeDtypeStruct((nl, rows, cols), F32)] * 3, compiler_params=_params(("parallel", "parallel")),
    )(w, g, m, v)


ALL_FLIPS = tuple(range(1, N_DEV))


def _exchange(src, flips, mode, name):
    _, rows, cols = src.shape
    nslot = 2 if mode == "pair" else N_DEV
    nf = len(flips)

    def body(src_ref, dst_ref, send_sems, recv_sems, local_sem):
        x, y, c = lax.axis_index("x"), lax.axis_index("y"), lax.axis_index("c")
        me = 4 * x + 2 * y + c

        def slot(j):
            return (j & 1) if mode == "pair" else j

        own = pltpu.make_async_copy(src_ref.at[me if mode == "scatter" else 0], dst_ref.at[slot(me)], local_sem)
        own.start()
        copies = []
        for i, f in enumerate(flips):
            peer = me ^ f
            to = (1 - x if f & 4 else x, 1 - y if f & 2 else y, 1 - c if f & 1 else c)
            cp = pltpu.make_async_remote_copy(
                src_ref=src_ref.at[peer if mode == "scatter" else 0], dst_ref=dst_ref.at[slot(me)],
                send_sem=send_sems.at[i], recv_sem=recv_sems.at[i], device_id=to, device_id_type=pl.DeviceIdType.MESH)
            cp.start()
            copies.append(cp)
        for i, f in enumerate(flips):
            peer = me ^ f
            to = (1 - x if f & 4 else x, 1 - y if f & 2 else y, 1 - c if f & 1 else c)
            pltpu.make_async_remote_copy(
                src_ref=src_ref.at[0], dst_ref=dst_ref.at[slot(peer)], send_sem=send_sems.at[i],
                recv_sem=recv_sems.at[i], device_id=to, device_id_type=pl.DeviceIdType.MESH).wait_recv()
        for cp in copies:
            cp.wait_send()
        own.wait()

    return pl.pallas_call(
        body, name=name, out_shape=jax.ShapeDtypeStruct((nslot, rows, cols), src.dtype),
        in_specs=[pl.BlockSpec(memory_space=pl.ANY)], out_specs=pl.BlockSpec(memory_space=pl.ANY),
        scratch_shapes=[pltpu.SemaphoreType.DMA((nf,)), pltpu.SemaphoreType.DMA((nf,)), pltpu.SemaphoreType.DMA],
    )(src)


def _transfer(name, srcs, dst_shapes, plan, in_place=False):
    n_arr = len(srcs)
    probe = plan(0, 0, 0)
    n_steps = len(probe)

    def body(*refs):
        src_refs, dst_refs = refs[:n_arr], refs[n_arr:2 * n_arr]
        send_sems, recv_sems, local_sems = refs[2 * n_arr:]
        x, y, c = lax.axis_index("x"), lax.axis_index("y"), lax.axis_index("c")
        steps = plan(x, y, c)

        def rows(ref, r0, n):
            return ref.at[:, pl.ds(r0, n), :]

        def arrival(t):
            a, _, _, n, _, f, _ = steps[t]
            return pltpu.make_async_remote_copy(
                src_ref=rows(dst_refs[a], 0, n), dst_ref=rows(dst_refs[a], 0, n), send_sem=send_sems.at[t],
                recv_sem=recv_sems.at[t], device_id=(x, y, c), device_id_type=pl.DeviceIdType.MESH)

        arrived, started = set(), []
        for t, (a, from_dst, sr, n, dr, f, after) in enumerate(steps):
            for u in after:
                if u not in arrived:
                    arrival(u).wait_recv()
                    arrived.add(u)
            src = rows(dst_refs[a] if from_dst else src_refs[a], sr, n)
            dst = rows(dst_refs[a], dr, n)
            if f == 0:
                cp = pltpu.make_async_copy(src, dst, local_sems.at[t])
            else:
                to = (1 - x if f & 4 else x, 1 - y if f & 2 else y, 1 - c if f & 1 else c)
                cp = pltpu.make_async_remote_copy(src_ref=src, dst_ref=dst, send_sem=send_sems.at[t],
                                                  recv_sem=recv_sems.at[t], device_id=to,
                                                  device_id_type=pl.DeviceIdType.MESH)
            cp.start()
            started.append(cp)
        for t, step in enumerate(steps):
            if step[5] != 0 and t not in arrived:
                arrival(t).wait_recv()
        for cp, step in zip(started, steps):
            if step[5] == 0:
                cp.wait()
            else:
                cp.wait_send()

    any_spec = pl.BlockSpec(memory_space=pl.ANY)
    return pl.pallas_call(
        body, name=name, out_shape=[jax.ShapeDtypeStruct(s, d) for s, d in dst_shapes],
        in_specs=[any_spec] * n_arr, out_specs=[any_spec] * n_arr,
        input_output_aliases={a: a for a in range(n_arr)} if in_place else {},
        scratch_shapes=[pltpu.SemaphoreType.DMA((n_steps,)), pltpu.SemaphoreType.DMA((n_steps,)),
                        pltpu.SemaphoreType.DMA((n_steps,))],
    )(*srcs)


CHIP_FLIPS = (2, 4, 6)


def _gather_plan(chip_rows):
    def plan(x, y, c):
        steps = []
        for a, rc in enumerate(chip_rows):
            h = rc // 2
            mine = rc * (2 * x + y) + h * c
            steps.append((a, False, h * c, h, mine, 0, ()))
            ici = {}
            for f in CHIP_FLIPS:
                ici[f] = len(steps)
                steps.append((a, False, h * c, h, mine, f, ()))
            steps.append((a, False, h * c, h, mine, 1, ()))
            for f in CHIP_FLIPS:
                theirs = rc * ((2 * x + y) ^ (f >> 1)) + h * c
                steps.append((a, True, theirs, h, theirs, 1, (ici[f],)))
        return steps
    return plan


def _pair_reduce_plan(chip_rows):
    def plan(x, y, c):
        steps = []
        for a, rc in enumerate(chip_rows):
            h = rc // 2
            for j in range(4):
                steps.append((a, False, rc * j + h * (1 - c), h, h * j, 1, ()))
        return steps
    return plan


def _chip_scatter_plan(chip_rows):
    def plan(x, y, c):
        steps = []
        for a, rc in enumerate(chip_rows):
            h = rc // 2
            for k, f in enumerate(CHIP_FLIPS):
                steps.append((a, False, h * ((2 * x + y) ^ (f >> 1)), h, h * k, f, ()))
        return steps
    return plan


def _pair_share_plan(chip_rows):
    def plan(x, y, c):
        return [(a, True, (rc // 2) * c, rc // 2, (rc // 2) * c, 1, ()) for a, rc in enumerate(chip_rows)]
    return plan


def _tile_rows(h):
    return h


def _sum_pair(partial, recv, core, rc, name):
    nl, _, cols = partial.shape
    h = rc // 2
    tr = _tile_rows(h)

    def body(c_ref, p_ref, r_ref, out_ref):
        out_ref[...] = (p_ref[...] + r_ref[...]).astype(BF16)

    spec = pl.BlockSpec((1, tr, cols), lambda l, j, i, c_ref: (l, (h // tr) * j + i, 0))
    return pl.pallas_call(
        body, name=name, out_shape=jax.ShapeDtypeStruct((nl, 4 * h, cols), BF16),
        grid_spec=pltpu.PrefetchScalarGridSpec(
            num_scalar_prefetch=1, grid=(nl, 4, h // tr),
            in_specs=[pl.BlockSpec((1, tr, cols), lambda l, j, i, c_ref: (l, (rc // tr) * j + (h // tr) * c_ref[0] + i, 0)),
                      spec],
            out_specs=spec),
        compiler_params=_params(("parallel", "parallel", "parallel")),
    )(core, partial, recv)


def _sum_chips(chip_sum, recv, place, rc, name):
    nl, _, cols = chip_sum.shape
    h = rc // 2
    tr = _tile_rows(h)

    def body(s_ref, own_ref, r_ref, out_ref):
        acc = own_ref[0].astype(F32)
        for k in range(3):
            acc = acc + r_ref[0, k].astype(F32)
        out_ref[0] = acc

    return pl.pallas_call(
        body, name=name, out_shape=jax.ShapeDtypeStruct((nl, rc, cols), F32),
        grid_spec=pltpu.PrefetchScalarGridSpec(
            num_scalar_prefetch=1, grid=(nl, h // tr),
            in_specs=[pl.BlockSpec((1, tr, cols), lambda l, i, s_ref: (l, (h // tr) * s_ref[0] + i, 0)),
                      pl.BlockSpec((1, 3, tr, cols), lambda l, i, s_ref: (l, 0, i, 0))],
            out_specs=pl.BlockSpec((1, tr, cols), lambda l, i, s_ref: (l, (h // tr) * s_ref[1] + i, 0))),
        compiler_params=_params(("parallel", "parallel")),
    )(place, chip_sum, recv.reshape(nl, 3, h, cols))


def _sum_slots(parts, out_dtype, name):
    nl, n, rows, cols = parts.shape
    tr = next(t for t in (128, 64, 32, 16) if rows % t == 0 and n * t * cols * 4 <= (4 << 20))

    def body(p_ref, out_ref):
        acc = p_ref[0, 0].astype(F32)
        for k in range(1, n):
            acc = acc + p_ref[0, k].astype(F32)
        out_ref[0] = acc.astype(out_dtype)

    return pl.pallas_call(
        body, name=name, grid=(nl, rows // tr),
        in_specs=[pl.BlockSpec((1, n, tr, cols), lambda l, i: (l, 0, i, 0))],
        out_specs=pl.BlockSpec((1, tr, cols), lambda l, i: (l, i, 0)),
        out_shape=jax.ShapeDtypeStruct((nl, rows, cols), out_dtype), compiler_params=_params(("parallel", "parallel")),
    )(parts)


def _pad_cols(a, n):
    return a if n == 0 else jnp.pad(a, ((0, 0), (0, n)))


def _in_to_padded(w):
    return jnp.concatenate([_pad_cols(w[:, a:b], z) for a, b, z in IN_PIECES], axis=1)


def _in_from_padded(gp):
    pos, out = 0, {}
    for a, b, z in IN_PIECES:
        out[a] = gp[:, pos:pos + (b - a)]
        pos += (b - a) + z
    return jnp.concatenate([out[a] for a in sorted(out)], axis=1)


def _in_cols_first(ga, gb):
    split = Z_G[1]
    pos, out = 0, {}
    for a, b, z in IN_PIECES:
        src, off = (ga, pos) if pos < split else (gb, pos - split)
        out[a] = jnp.transpose(src[:, :, off:off + (b - a)], (2, 0, 1))
        pos += (b - a) + z
    return jnp.concatenate([out[a] for a in sorted(out)], axis=0)


def _out_to_padded(w):
    z = jnp.zeros((64, w.shape[1]), w.dtype)
    return jnp.concatenate([w[0:384], w[384:704], z, w[704:1024], z], axis=0)


def _out_from_padded(gp):
    return jnp.concatenate([gp[0:384], gp[384:704], gp[768:1088]], axis=0)


def _uq_to_padded(w):
    parts = []
    for p in range(3):
        h0, h1 = 2 * p, 2 * p + 1
        parts += [w[:, 96 * h0:96 * h0 + 64], w[:, 96 * h1:96 * h1 + 64], w[:, 96 * h0 + 64:96 * h0 + 96],
                  w[:, 96 * h1 + 64:96 * h1 + 96], jnp.zeros((w.shape[0], 64), w.dtype)]
    return jnp.concatenate(parts, axis=1)


def _uq_from_padded(gp):
    parts = []
    for h in range(6):
        p, s = h // 2, h % 2
        parts += [gp[:, 256 * p + 64 * s:256 * p + 64 * s + 64], gp[:, 256 * p + 128 + 32 * s:256 * p + 160 + 32 * s]]
    return jnp.concatenate(parts, axis=1)


def _ukv_to_padded(w):
    return jnp.concatenate([w[:, 128 * h:128 * h + 64] for h in range(6)]
                           + [w[:, 128 * h + 64:128 * h + 128] for h in range(6)], axis=1)


def _ukv_from_padded(gp):
    parts = []
    for h in range(6):
        parts += [gp[:, 64 * h:64 * h + 64], gp[:, 384 + 64 * h:384 + 64 * h + 64]]
    return jnp.concatenate(parts, axis=1)


LR_ROWS = 224
SHARD_ROWS = (256, 256, 256, LR_ROWS)


def _pack_lowrank(w_uq, w_ukv):
    flat = jnp.concatenate([w_uq.reshape(-1), w_ukv.reshape(-1)])
    return jnp.pad(flat, (0, LR_ROWS * PACK_COLS - flat.shape[0])).reshape(1, LR_ROWS, PACK_COLS)


def _unpack_lowrank(packed):
    flat = packed.reshape(-1)
    n_uq = DEPTH * A_Q_RANK * 144
    n_ukv = DEPTH * A_KV_RANK * 192
    return flat[0:n_uq].reshape(DEPTH, A_Q_RANK, 144), flat[n_uq:n_uq + n_ukv].reshape(DEPTH, A_KV_RANK, 192)


PACK_SIZES = (DEPTH * 256 * N_IN, DEPTH * 256 * D_MODEL, DEPTH * A_Q_RANK * 144, DEPTH * A_KV_RANK * 192)


def _pack(w_in, w_out, w_uq, w_ukv):
    flat = jnp.concatenate([w_in.reshape(-1), w_out.reshape(-1), w_uq.reshape(-1), w_ukv.reshape(-1)])
    flat = jnp.pad(flat, (0, PACK_ROWS * PACK_COLS - flat.shape[0]))
    return flat.reshape(PACK_ROWS, PACK_COLS)


def _unpack(packed):
    flat = packed.reshape(-1)
    o0, o1, o2, o3 = PACK_SIZES
    w_in = flat[0:o0].reshape(DEPTH, 256, N_IN)
    w_out = flat[o0:o0 + o1].reshape(DEPTH, 256, D_MODEL)
    w_uq = flat[o0 + o1:o0 + o1 + o2].reshape(DEPTH, A_Q_RANK, 144)
    w_ukv = flat[o0 + o1 + o2:o0 + o1 + o2 + o3].reshape(DEPTH, A_KV_RANK, 192)
    return w_in, w_out, w_uq, w_ukv


def _rope_tables(positions):
    t = positions.size
    inv = ROPE_THETA ** (-jnp.arange(0, A_ROPE, 2, dtype=F32) / A_ROPE)
    inv_row = jnp.pad(jnp.tile(inv, 4), (0, 64)).reshape(1, LANE)

    def body(p_ref, i_ref, c_ref, sa_ref, sb_ref):
        ang = p_ref[...].astype(F32) * i_ref[...]
        lane = _lane_iota((TM, LANE))
        live = lane < 64
        second = (lane & 31) >= 16
        s = jnp.sin(ang)
        c_ref[...] = jnp.where(live, jnp.cos(ang), 0.0)
        sa_ref[...] = jnp.where(live & jnp.logical_not(second), -s, 0.0)
        sb_ref[...] = jnp.where(live & second, s, 0.0)

    return pl.pallas_call(
        body, name="rope_tables", grid=(t // TM,), in_specs=[_row_spec(1), _full_spec((1, LANE))],
        out_specs=[_row_spec(LANE)] * 3, out_shape=[jax.ShapeDtypeStruct((t, LANE), F32)] * 3,
        compiler_params=_params(("parallel",)),
    )(positions.reshape(t, 1), inv_row)


def _rows(a, n):
    flat = a.reshape(-1)
    return jnp.pad(flat, (0, n * LANE - flat.shape[0])).reshape(n, LANE)


def _forward_backward(x, mod, tables, target, weights, small, nb, seq):
    cos, sina, sinb = tables
    saved = []
    for l in range(DEPTH):
        w, s = weights[l], small[l]
        shift = mod[l][:, None, 0:D_MODEL]
        scale = mod[l][:, None, D_MODEL:2 * D_MODEL]
        gate = mod[l][:, None, 2 * D_MODEL:]
        h, za, zg, qb, kb, vb, qc, kc, vc, zf = _ln_inproj(x, shift, scale, s["norm_g"], w["in"], seq)
        qa, ka, va = _a_up(za, s["gq"], s["gkv"], w["uq"], w["ukv"], cos, sina, sinb)
        bias = _bias_expand(s["g8"])[0:6]
        f = _forget_fwd(zf, s["fb"], nb, seq)
        frow = jnp.pad(f[:, 0:5].reshape(nb, seq, 5).transpose(0, 2, 1), ((0, 0), (0, 1), (0, 0)))
        frow = frow.reshape(nb, 6, 1, seq)
        oa, lse_a = _attn_fwd("A", qa, ka, va, None, nb, seq)
        ob, lse_b = _attn_fwd("B", qb, kb, vb, bias, nb, seq)
        oc, lse_c = _attn_fwd("C", qc, kc, vc, frow, nb, seq)
        y, xn = _gate_outproj(x, gate, oa, ob, oc, zg, w["out"], seq)
        saved.append(dict(x=x, h=h, za=za, zg=zg, zf=zf, y=y, shift=shift, scale=scale, gate=gate, bias=bias, frow=frow,
                          a=(qa, ka, va, oa, lse_a), b=(qb, kb, vb, ob, lse_b), c=(qc, kc, vc, oc, lse_c)))
        x = xn
    dx, loss8, gfinal8 = _final_loss(x, target, small[0]["final_g"])
    grads = []
    gw_in = (None, None)
    for l in reversed(range(DEPTH)):
        w, s, sv = weights[l], small[l], saved[l]
        qa, ka, va, oa, lse_a = sv["a"]
        qb, kb, vb, ob, lse_b = sv["b"]
        qc, kc, vc, oc, lse_c = sv["c"]
        doa, dob, doc, dzg, gw_out, dgate = _outproj_bwd(dx, sv["y"], sv["gate"], oa, ob, oc, sv["zg"], w["out"],
                                                          w["out_t"], nb, seq)
        dqa, dka, dva = _attn_bwd("A", qa, ka, va, oa, doa, lse_a, None, nb, seq)
        dqb, dkb, dvb, dbt = _attn_bwd("B", qb, kb, vb, ob, dob, lse_b, sv["bias"], nb, seq)
        dqc, dkc, dvc, dfr, dfq = _attn_bwd("C", qc, kc, vc, oc, doc, lse_c, sv["frow"], nb, seq)
        dg = _bias_reduce(jnp.pad(dbt, ((0, 2), (0, 0), (0, 0))))
        grb = jnp.pad(_bias_unline(dg), ((0, 0), (0, 384 - N_REL)))
        dfk = dfr.reshape(nb, 6, seq).transpose(0, 2, 1).reshape(nb * seq, 6)
        dzf, gfb = _forget_bwd(dfq, jnp.pad(dfk, ((0, 0), (0, LANE - 6))), sv["zf"], s["fb"], nb, seq)
        dza, gw_uq, gw_ukv, ggq, ggkv = _a_up_bwd(dqa, dka, dva, sv["za"], s["gq"], s["gkv"], w["uq_t"], w["ukv_t"],
                                                  cos, sina, sinb)
        dz = (dza, dzg, dqb, dkb, dvb, dqc, dkc, dvc, dzf)
        dx, dshift, dscale, gnorm = _inproj_bwd_dx(dz, dx, sv["x"], sv["shift"], sv["scale"], s["norm_g"], w["in"],
                                                   nb, seq)
        gw_in = (_inproj_bwd_dw(sv["h"], dz[0:2], "inproj_bwd_dw0", l, gw_in[0]),
                 _inproj_bwd_dw(sv["h"], dz[2:], "inproj_bwd_dw1", l, gw_in[1]))
        dmod = jnp.concatenate([dshift[:, 0], dscale[:, 0], dgate[:, 0]], axis=1)
        grads.append(dict(w_out=gw_out, w_uq=gw_uq, w_ukv=gw_ukv, dmod=dmod, norm_g=gnorm[0], gq=ggq[0],
                          gkv=ggkv[0], rb8=grb, fb=gfb[0]))
    grads.reverse()
    return loss8[0, 0], dx, grads, gfinal8[0], gw_in


def _layer_weights(w_in, w_out, w_uq, w_ukv):
    wi, wo, wq, wkv = _in_to_padded(w_in), _out_to_padded(w_out), _uq_to_padded(w_uq), _ukv_to_padded(w_ukv)
    return {"in": wi, "out": wo, "out_t": wo.T, "uq": wq, "uq_t": wq.T, "ukv": wkv, "ukv_t": wkv.T}


def _layer_small(norm_g, gq, gkv, rel_bias, forget_b, final_g):
    fb = jnp.pad(forget_b, (0, LANE - 5)).reshape(1, LANE)
    return dict(norm_g=norm_g.reshape(1, -1), gq=gq.reshape(1, -1), gkv=gkv.reshape(1, -1), g8=_bias_line(rel_bias), fb=fb,
                final_g=final_g.reshape(1, -1))


def _small_payload(per_layer, final_g, loss):
    def stack(key):
        return jnp.stack([p[key] for p in per_layer])

    def rows(a, rng):
        return _rows(a, rng[1] - rng[0])

    dmod = stack("dmod") if "dmod" in per_layer[0] else jnp.zeros((LANE,), F32)
    parts = [rows(dmod, PAY_DMOD), rows(stack("norm_g"), PAY_NORM), rows(stack("gq"), PAY_GQ),
             rows(stack("gkv"), PAY_GKV), rows(stack("rb8"), PAY_RB), rows(stack("fb"), PAY_FB),
             rows(final_g, PAY_FINAL), rows(loss, PAY_LOSS)]
    return jnp.concatenate(parts, axis=0)


def _payload_split(pay):
    def take(rng, shape):
        n = 1
        for d in shape:
            n *= d
        return pay[rng[0]:rng[1]].reshape(-1)[0:n].reshape(shape)

    norm_g = take(PAY_NORM, (DEPTH, D_MODEL))
    gq = take(PAY_GQ, (DEPTH, A_Q_RANK))
    gkv = take(PAY_GKV, (DEPTH, A_KV_RANK))
    rb = take(PAY_RB, (DEPTH, 8, 384))[:, 0:5, 0:N_REL]
    fb = take(PAY_FB, (DEPTH, LANE))[:, 0:5]
    final_g = take(PAY_FINAL, (D_MODEL,))
    return norm_g, gq, gkv, rb, fb, final_g


def kernel(x, c, positions, w_ada, b_ada, norm_g, w_in, a_q_norm_g, a_w_uq, a_kv_norm_g, a_w_ukv, b_rel_bias, c_forget_b, w_out, final_g, loss_target, m_w_ada, m_b_ada, m_norm_g, m_w_in, m_a_q_norm_g, m_a_w_uq, m_a_kv_norm_g, m_a_w_ukv, m_b_rel_bias, m_c_forget_b, m_w_out, m_final_g, v_w_ada, v_b_ada, v_norm_g, v_w_in, v_a_q_norm_g, v_a_w_uq, v_a_kv_norm_g, v_a_w_ukv, v_b_rel_bias, v_c_forget_b, v_w_out, v_final_g):
    nb, seq, _ = x.shape
    ix, iy, ic = lax.axis_index("x"), lax.axis_index("y"), lax.axis_index("c")
    chip = 2 * ix + iy
    me = 2 * chip + ic

    weight_plan = _gather_plan((256, 256, LR_ROWS))

    def gather_plan(x, y, c_):
        me_ = 4 * x + 2 * y + c_
        return weight_plan(x, y, c_) + [(3, False, 0, 8, 8 * me_, f, ()) for f in range(N_DEV)]

    full_in, full_out, full_lr, c_rows = _transfer(
        "gather_weights", [w_in.astype(BF16), w_out.astype(BF16), _pack_lowrank(a_w_uq, a_w_ukv).astype(BF16),
                           jnp.pad(c, ((0, 8 - nb), (0, 0)))[None]],
        [((DEPTH, D_MODEL, N_IN), BF16), ((DEPTH, D_MODEL, D_MODEL), BF16), ((1, 4 * LR_ROWS, PACK_COLS), BF16),
         ((1, 8 * N_DEV, D_MODEL), F32)], gather_plan)
    lowrank = [_unpack_lowrank(full_lr[0, LR_ROWS * j:LR_ROWS * (j + 1)]) for j in range(4)]
    full_uq = jnp.concatenate([s[0] for s in lowrank], axis=2)
    full_ukv = jnp.concatenate([s[1] for s in lowrank], axis=2)
    weights = [_layer_weights(full_in[l], full_out[l], full_uq[l], full_ukv[l]) for l in range(DEPTH)]
    small = [_layer_small(norm_g[l], a_q_norm_g[l], a_kv_norm_g[l], b_rel_bias[l], c_forget_b[l], final_g)
             for l in range(DEPTH)]

    c_all = c_rows.reshape(N_DEV, 8, D_MODEL)[:, 0:nb].reshape(N_DEV * nb, D_MODEL)
    cols = w_ada.shape[2]
    b_cols = lax.dynamic_slice_in_dim(b_ada, chip * cols, cols, axis=1)[:, None, :]
    mod_cols = _ada_fwd(c_all, w_ada, b_cols)
    mod_g = _exchange(mod_cols.reshape(1, DEPTH * N_DEV * nb, cols), ALL_FLIPS, "gather", "gather_mod")
    mod_all = jnp.concatenate([mod_g[2 * j].reshape(DEPTH, N_DEV * nb, cols) for j in range(4)], axis=2)
    mod = lax.dynamic_slice_in_dim(mod_all, me * nb, nb, axis=1)

    tables = _rope_tables(positions)
    loss_part, dx, grads, gfinal, gw_in = _forward_backward(
        x.reshape(nb * seq, D_MODEL), mod, tables, loss_target.reshape(nb * seq, D_MODEL), weights, small, nb, seq)

    pay = _small_payload(grads, gfinal, loss_part)
    pay_all = _exchange(pay[None], ALL_FLIPS, "gather", "gather_small")
    tot = _sum_blocks(pay_all, "sum_small")
    loss = tot[PAY_LOSS[0], 0]
    dmod_all = pay_all[:, PAY_DMOD[0]:PAY_DMOD[1]].reshape(N_DEV, -1)[:, 0:DEPTH * nb * 3 * D_MODEL]
    dmod_all = dmod_all.reshape(N_DEV, DEPTH, nb, 3 * D_MODEL).transpose(1, 0, 2, 3)
    dmod_all = dmod_all.reshape(DEPTH, N_DEV * nb, 3 * D_MODEL)
    my_cols = lax.dynamic_slice_in_dim(dmod_all, chip * cols, cols, axis=2)
    g_w_ada, g_b_ada = _ada_bwd(c_all, my_cols, dmod_all)
    g_b_ada = g_b_ada[:, 0]

    g_uq = jnp.stack([_uq_from_padded(g["w_uq"]) for g in grads])
    g_ukv = jnp.stack([_ukv_from_padded(g["w_ukv"]) for g in grads])
    g_lr = jnp.concatenate([_pack_lowrank(g_uq[:, :, 144 * j:144 * (j + 1)], g_ukv[:, :, 192 * j:192 * (j + 1)])
                            for j in range(4)], axis=1)
    partials = [gw_in[0], gw_in[1], jnp.stack([_out_from_padded(g["w_out"]) for g in grads]), g_lr]
    shapes = [(p.shape[0], p.shape[2]) for p in partials]
    halves = [r // 2 for r in SHARD_ROWS]
    core_s = jnp.reshape(ic, (1,)).astype(jnp.int32)
    place_s = jnp.stack([chip, ic]).astype(jnp.int32)
    from_pair = _transfer("pair_reduce", partials, [((nl, 4 * h, nc), F32) for (nl, nc), h in zip(shapes, halves)],
                          _pair_reduce_plan(SHARD_ROWS))
    chip_sums = [_sum_pair(p, r, core_s, rc, "sum_pair%d" % i)
                 for i, (p, r, rc) in enumerate(zip(partials, from_pair, SHARD_ROWS))]
    from_chips = _transfer("chip_scatter", chip_sums, [((nl, 3 * h, nc), BF16) for (nl, nc), h in zip(shapes, halves)],
                           _chip_scatter_plan(SHARD_ROWS))
    reduced = [_sum_chips(s, r, place_s, rc, "sum_chips%d" % i)
               for i, (s, r, rc) in enumerate(zip(chip_sums, from_chips, SHARD_ROWS))]
    g_in_a, g_in_b, g_out_sh, g_lr_sh = _transfer("pair_share", reduced, [(r.shape, F32) for r in reduced],
                                                  _pair_share_plan(SHARD_ROWS), in_place=True)
    g_uq_sh, g_ukv_sh = _unpack_lowrank(g_lr_sh[0])

    def cols_first(a):
        return jnp.transpose(a, (2, 0, 1))

    g_in_t = _in_cols_first(g_in_a, g_in_b)
    upd_in = tuple(jnp.transpose(a, (1, 2, 0)) for a in _adamw(cols_first(w_in), g_in_t, cols_first(m_w_in),
                                                               cols_first(v_w_in), "adamw_in"))
    gw = (jnp.transpose(g_in_t, (1, 2, 0)), g_out_sh, g_uq_sh, g_ukv_sh)
    upd = [upd_in, _adamw(w_out, gw[1], m_w_out, v_w_out, "adamw_out"),
           _adamw(a_w_uq, gw[2], m_a_w_uq, v_a_w_uq, "adamw_uq"), _adamw(a_w_ukv, gw[3], m_a_w_ukv, v_a_w_ukv, "adamw_ukv")]
    dw, mw, vw = (tuple(u[i] for u in upd) for i in range(3))
    d_ada, m_ada, v_ada = _adamw(w_ada, g_w_ada, m_w_ada, v_w_ada, "adamw_ada")
    d_b, m_b, v_b = (a.reshape(DEPTH, 3 * D_MODEL) for a in _adamw(
        _rows(b_ada, 48)[None], _rows(g_b_ada, 48)[None], _rows(m_b_ada, 48)[None], _rows(v_b_ada, 48)[None],
        "adamw_b_ada"))

    def small_rows(ng, gq, gkv, rb, fb, fg):
        per_layer = [dict(norm_g=ng[l], gq=gq[l], gkv=gkv[l], rb8=jnp.pad(rb[l], ((0, 3), (0, 384 - N_REL))),
                          fb=jnp.pad(fb[l], (0, LANE - 5))) for l in range(DEPTH)]
        return _small_payload(per_layer, fg, jnp.zeros((), F32))

    w_s = small_rows(norm_g, a_q_norm_g, a_kv_norm_g, b_rel_bias, c_forget_b, final_g)
    m_s = small_rows(m_norm_g, m_a_q_norm_g, m_a_kv_norm_g, m_b_rel_bias, m_c_forget_b, m_final_g)
    v_s = small_rows(v_norm_g, v_a_q_norm_g, v_a_kv_norm_g, v_b_rel_bias, v_c_forget_b, v_final_g)
    d_s, mo_s, vo_s = (a[0] for a in _adamw(w_s[None], tot[None], m_s[None], v_s[None], "adamw_small"))
    gs = _payload_split(tot)
    ds = _payload_split(d_s)
    ms = _payload_split(mo_s)
    vs = _payload_split(vo_s)

    def ordered(ada, b, sm, big):
        ng, gq, gkv, rb, fb, fg = sm
        b_in, b_out, b_uq, b_ukv = big
        return (ada, b, ng, b_in, gq, b_uq, gkv, b_ukv, rb, fb, b_out, fg)

    return (loss, dx.reshape(nb, seq, D_MODEL), *ordered(g_w_ada, g_b_ada, gs, gw), *ordered(d_ada, d_b, ds, dw),
            *ordered(m_ada, m_b, ms, mw), *ordered(v_ada, v_b, vs, vw))
```

```python
import functools

import jax
import jax.numpy as jnp
from jax import lax
from jax.experimental import pallas as pl
from jax.experimental.pallas import tpu as pltpu

F32 = jnp.float32
BF16 = jnp.bfloat16

D_MODEL = 1024
DEPTH = 2
EPS = 1e-6
NEG = -1e30
LOG2E = 1.4426950408889634
ROPE_THETA = 10000.0
A_ROPE = 32
A_Q_RANK = 384
A_KV_RANK = 256
REL_CLIP = 128
N_REL = 2 * REL_CLIP + 1
N_IN = 3621

ADAM_LR = 0.001
ADAM_B1 = 0.9
ADAM_B2 = 0.999
ADAM_EPS = 1e-08
ADAM_WD = 0.01
ADAM_STEP = 10

LANE = 128
VMEM_LIMIT = 56 * 1024 * 1024

NP_IN = 4352
Z_A = (0, 768)
Z_G = (768, 1920)
Z_QKV = tuple((1920 + 384 * i, 1920 + 384 * (i + 1)) for i in range(6))
Z_F = (4224, 4352)
IN_PIECES = ((0, 672, 96), (672, 1056, 0), (2016, 2336, 64), (3301, 3621, 64), (1056, 1376, 64), (1376, 1696, 64),
             (1696, 2016, 64), (2336, 2656, 64), (2656, 2976, 64), (2976, 3296, 64), (3296, 3301, 123))
D_CAT = 1152

TM = 512
T_CAUSAL = 256
T_BAND = 128
BAND_TILES = 5
N_DEV = 8

PAY_DMOD = (0, 96)
PAY_NORM = (96, 112)
PAY_GQ = (112, 120)
PAY_GKV = (120, 128)
PAY_RB = (128, 176)
PAY_FB = (176, 184)
PAY_FINAL = (184, 192)
PAY_LOSS = (192, 200)
PAY_ROWS = 200

PACK_COLS = 1024
PACK_ROWS = 2560
HALF_ROWS = PACK_ROWS // 2


def _params(sem=None):
    return pltpu.CompilerParams(dimension_semantics=sem, vmem_limit_bytes=VMEM_LIMIT)


def _lane_iota(shape):
    return lax.broadcasted_iota(jnp.int32, shape, len(shape) - 1)


def _silu(u):
    return u * jax.nn.sigmoid(u)


def _dsilu(u):
    s = jax.nn.sigmoid(u)
    return s * (1.0 + u * (1.0 - s))


def _rms(x, g):
    r = lax.rsqrt(jnp.mean(x * x, axis=-1, keepdims=True) + EPS)
    xh = x * r
    return xh * g, xh, r


def _rms_bwd(dy, xh, r, g):
    dxh = dy * g
    return r * (dxh - xh * jnp.mean(dxh * xh, axis=-1, keepdims=True))


def _rope(x, cos, sina, sinb):
    return x * cos + pltpu.roll(x, 16, 1) * sinb + pltpu.roll(x, LANE - 16, 1) * sina


def _rope_t(dy, cos, sina, sinb):
    return dy * cos + pltpu.roll(dy * sinb, LANE - 16, 1) + pltpu.roll(dy * sina, 16, 1)


def _split3(x):
    hi = x.astype(BF16)
    r1 = x - hi.astype(F32)
    mid = r1.astype(BF16)
    lo = (r1 - mid.astype(F32)).astype(BF16)
    return hi, mid, lo


def _dot(a, b):
    return jnp.dot(a, b, preferred_element_type=F32)


def _dot_nt(a, b):
    return lax.dot_general(a, b, (((1,), (1,)), ((), ())), preferred_element_type=F32)


def _dot_tn(a, b):
    return lax.dot_general(a, b, (((0,), (0,)), ((), ())), preferred_element_type=F32)


def _row_spec(cols):
    return pl.BlockSpec((TM, cols), lambda i: (i, 0))


def _full_spec(shape):
    return pl.BlockSpec(shape, lambda i: (0,) * len(shape))


def _ex_spec(tiles_per_ex):
    return pl.BlockSpec((1, 1, D_MODEL), lambda i: (i // tiles_per_ex, 0, 0))


def _ln_inproj(x, shift, scale, g, w_in_p, seq):
    t = x.shape[0]

    def body(x_ref, sh_ref, sc_ref, g_ref, w_ref, h_ref, za_ref, zg_ref, q0, q1, q2, q3, q4, q5, zf_ref):
        n, _, _ = _rms(x_ref[...], g_ref[...])
        h = (n * (1.0 + sc_ref[0]) + sh_ref[0]).astype(BF16)
        h_ref[...] = h
        za_ref[...] = _dot(h, w_ref[:, Z_A[0]:Z_A[1]])
        zg_ref[...] = _dot(h, w_ref[:, Z_G[0]:Z_G[1]])
        for ref, (c0, c1) in zip((q0, q1, q2, q3, q4, q5), Z_QKV):
            ref[...] = _dot(h, w_ref[:, c0:c1]).astype(BF16)
        zf_ref[...] = _dot(h, w_ref[:, Z_F[0]:Z_F[1]])

    tpe = seq // TM
    shapes = [jax.ShapeDtypeStruct((t, D_MODEL), BF16), jax.ShapeDtypeStruct((t, 768), F32),
              jax.ShapeDtypeStruct((t, D_CAT), F32)]
    shapes += [jax.ShapeDtypeStruct((t, 384), BF16)] * 6 + [jax.ShapeDtypeStruct((t, LANE), F32)]
    return pl.pallas_call(
        body, name="ln_inproj", grid=(t // TM,),
        in_specs=[_row_spec(D_MODEL), _ex_spec(tpe), _ex_spec(tpe), _full_spec((1, D_MODEL)),
                  _full_spec((D_MODEL, NP_IN))],
        out_specs=[_row_spec(D_MODEL), _row_spec(768), _row_spec(D_CAT)] + [_row_spec(384)] * 6 + [_row_spec(LANE)],
        out_shape=shapes, compiler_params=_params(("parallel",)),
    )(x, shift, scale, g, w_in_p)


def _a_up(za, gq, gkv, w_uq_p, w_ukv_p, cos, sina, sinb):
    t = za.shape[0]

    def body(za_ref, gq_ref, gkv_ref, wq_ref, wkv_ref, cos_ref, sa_ref, sb_ref, q_ref, k_ref, v_ref):
        cos_t, sa, sb = cos_ref[...], sa_ref[...], sb_ref[...]
        cqn, _, _ = _rms(za_ref[:, 0:384], gq_ref[...])
        q = _dot(cqn.astype(BF16), wq_ref[...])
        ckvn, _, _ = _rms(za_ref[:, 384:640], gkv_ref[...])
        kv = _dot(ckvn.astype(BF16), wkv_ref[...])
        kpe = za_ref[:, 640:768]
        kpe = _rope(kpe + pltpu.roll(kpe, 32, 1), cos_t, sa, sb).astype(BF16)
        for p in range(3):
            q_ref[:, 256 * p:256 * p + 128] = q[:, 256 * p:256 * p + 128].astype(BF16)
            q_ref[:, 256 * p + 128:256 * p + 256] = _rope(q[:, 256 * p + 128:256 * p + 256], cos_t, sa, sb).astype(BF16)
            k_ref[:, 256 * p:256 * p + 128] = kv[:, 128 * p:128 * p + 128].astype(BF16)
            k_ref[:, 256 * p + 128:256 * p + 256] = kpe
        v_ref[...] = kv[:, 384:768].astype(BF16)

    return pl.pallas_call(
        body, name="a_up", grid=(t // TM,),
        in_specs=[_row_spec(768), _full_spec((1, 384)), _full_spec((1, 256)), _full_spec((384, 768)),
                  _full_spec((256, 768)), _row_spec(LANE), _row_spec(LANE), _row_spec(LANE)],
        out_specs=[_row_spec(768), _row_spec(768), _row_spec(384)],
        out_shape=[jax.ShapeDtypeStruct((t, 768), BF16), jax.ShapeDtypeStruct((t, 768), BF16),
                   jax.ShapeDtypeStruct((t, 384), BF16)],
        compiler_params=_params(("parallel",)),
    )(za, gq, gkv, w_uq_p, w_ukv_p, cos, sina, sinb)


def _tri(n, upper):
    r = lax.broadcasted_iota(jnp.int32, (n, n), 0)
    c = lax.broadcasted_iota(jnp.int32, (n, n), 1)
    return jnp.where((c >= r) if upper else (c <= r), 1.0, 0.0).astype(BF16)


def _forget_fwd(zf, fb, nb, seq):
    blk = 256

    def body(zf_ref, fb_ref, f_ref):
        tri = _tri(blk, False)
        live = _lane_iota((blk, LANE)) < 5
        carry = jnp.zeros((1, LANE), F32)
        for i in range(seq // blk):
            u = zf_ref[i * blk:(i + 1) * blk, :] + fb_ref[...]
            lf = jnp.where(live, jnp.minimum(u, 0.0) - jnp.log(1.0 + jnp.exp(-jnp.abs(u))), 0.0)
            hi, mid, lo = _split3(lf)
            f_ref[i * blk:(i + 1) * blk, :] = (_dot(tri, hi) + _dot(tri, mid) + _dot(tri, lo) + carry) * LOG2E
            carry = carry + jnp.sum(lf, axis=0, keepdims=True)

    return pl.pallas_call(
        body, name="forget_fwd", grid=(nb,),
        in_specs=[pl.BlockSpec((seq, LANE), lambda b: (b, 0)), pl.BlockSpec((1, LANE), lambda b: (0, 0))],
        out_specs=pl.BlockSpec((seq, LANE), lambda b: (b, 0)),
        out_shape=jax.ShapeDtypeStruct((nb * seq, LANE), F32), compiler_params=_params(("parallel",)),
    )(zf, fb)


def _forget_bwd(dfq, dfk, zf, fb, nb, seq):
    blk = 256

    def body(dfq_ref, dfk_ref, zf_ref, fb_ref, dz_ref, gb_ref):
        @pl.when(pl.program_id(0) == 0)
        def _():
            gb_ref[...] = jnp.zeros_like(gb_ref)

        tri = _tri(blk, True)
        lane = _lane_iota((blk, LANE))
        wide = _lane_iota((blk, 384))
        live = lane < 5
        carry = jnp.zeros((1, LANE), F32)
        gsum = jnp.zeros((1, LANE), F32)
        for i in reversed(range(seq // blk)):
            d = dfk_ref[i * blk:(i + 1) * blk, :]
            dq = dfq_ref[i * blk:(i + 1) * blk, :]
            for hd in range(5):
                col = jnp.sum(jnp.where(wide == 64 * hd, dq, 0.0), axis=-1, keepdims=True)
                d = d + jnp.where(lane == hd, col, 0.0)
            hi, mid, lo = _split3(d)
            dlf = _dot(tri, hi) + _dot(tri, mid) + _dot(tri, lo) + carry
            carry = carry + jnp.sum(d, axis=0, keepdims=True)
            u = zf_ref[i * blk:(i + 1) * blk, :] + fb_ref[...]
            du = jnp.where(live, dlf * jax.nn.sigmoid(-u), 0.0)
            dz_ref[i * blk:(i + 1) * blk, :] = du.astype(BF16)
            gsum = gsum + jnp.sum(du, axis=0, keepdims=True)
        gb_ref[...] += jnp.broadcast_to(gsum, gb_ref.shape)

    return pl.pallas_call(
        body, name="forget_bwd", grid=(nb,),
        in_specs=[pl.BlockSpec((seq, 384), lambda b: (b, 0)), pl.BlockSpec((seq, LANE), lambda b: (b, 0)),
                  pl.BlockSpec((seq, LANE), lambda b: (b, 0)), pl.BlockSpec((1, LANE), lambda b: (0, 0))],
        out_specs=[pl.BlockSpec((seq, LANE), lambda b: (b, 0)), pl.BlockSpec((8, LANE), lambda b: (0, 0))],
        out_shape=[jax.ShapeDtypeStruct((nb * seq, LANE), BF16), jax.ShapeDtypeStruct((8, LANE), F32)],
        compiler_params=_params(("arbitrary",)),
    )(dfq, dfk, zf, fb)


def _gate_outproj(x, gate, oa, ob, oc, zg, w_out_p, seq):
    t = x.shape[0]

    def body(x_ref, gate_ref, oa_ref, ob_ref, oc_ref, zg_ref, w_ref, y_ref, xn_ref):
        y = jnp.zeros((TM, D_MODEL), F32)
        for i, o_ref in enumerate((oa_ref, ob_ref, oc_ref)):
            cat = (o_ref[...] * _silu(zg_ref[:, 384 * i:384 * (i + 1)])).astype(BF16)
            y = y + _dot(cat, w_ref[384 * i:384 * (i + 1), :])
        y_ref[...] = y
        xn_ref[...] = x_ref[...] + gate_ref[0] * y

    return pl.pallas_call(
        body, name="gate_outproj", grid=(t // TM,),
        in_specs=[_row_spec(D_MODEL), _ex_spec(seq // TM), _row_spec(384), _row_spec(384), _row_spec(384),
                  _row_spec(D_CAT), _full_spec((D_CAT, D_MODEL))],
        out_specs=[_row_spec(D_MODEL), _row_spec(D_MODEL)],
        out_shape=[jax.ShapeDtypeStruct((t, D_MODEL), F32)] * 2, compiler_params=_params(("parallel",)),
    )(x, gate, oa, ob, oc, zg, w_out_p)


def _final_loss(x, target, g):
    t = x.shape[0]

    def body(x_ref, t_ref, g_ref, dx_ref, loss_ref, gg_ref):
        @pl.when(pl.program_id(0) == 0)
        def _():
            loss_ref[...] = jnp.zeros_like(loss_ref)
            gg_ref[...] = jnp.zeros_like(gg_ref)

        gv = g_ref[...]
        out, xh, r = _rms(x_ref[...], gv)
        err = out - t_ref[...]
        loss_ref[...] += 0.5 * jnp.sum(jnp.mean(err * err, axis=-1, keepdims=True), axis=0, keepdims=True)
        dout = err / D_MODEL
        gg_ref[...] += jnp.broadcast_to(jnp.sum(dout * xh, axis=0, keepdims=True), gg_ref.shape)
        dx_ref[...] = _rms_bwd(dout, xh, r, gv)

    return pl.pallas_call(
        body, name="final_loss", grid=(t // TM,),
        in_specs=[_row_spec(D_MODEL), _row_spec(D_MODEL), _full_spec((1, D_MODEL))],
        out_specs=[_row_spec(D_MODEL), _full_spec((8, LANE)), _full_spec((8, D_MODEL))],
        out_shape=[jax.ShapeDtypeStruct((t, D_MODEL), F32), jax.ShapeDtypeStruct((8, LANE), F32),
                   jax.ShapeDtypeStruct((8, D_MODEL), F32)],
        compiler_params=_params(("arbitrary",)),
    )(x, target, g)


def _head_masks(kind, rows, dq, h):
    lq = _lane_iota((rows, dq))
    lv = _lane_iota((rows, LANE))
    mq = (lq >= 64 * h) & (lq < 64 * h + 64)
    if kind == "A":
        mq = mq | ((lq >= 128 + 32 * h) & (lq < 160 + 32 * h))
    return mq, (lv >= 64 * h) & (lv < 64 * h + 64)


def _tile_mask(kind, tile, m=None):
    row = lax.broadcasted_iota(jnp.int32, (tile, tile), 0)
    col = lax.broadcasted_iota(jnp.int32, (tile, tile), 1)
    if kind == "A":
        return (col >> 6) <= (row >> 6)
    if kind == "C":
        return col <= row
    first = (m == 0) & (row >= 64) & (col < 64)
    last = (m == BAND_TILES - 1) & (row < 64) & (col >= 64)
    return jnp.logical_not(first | last)


def _attn_scale(kind):
    return 96.0 ** -0.5 if kind == "A" else 0.125


def _attn_fwd(kind, q, k, v, aux, nb, seq):
    dq = q.shape[1] // 3
    tile = T_BAND if kind == "B" else T_CAUSAL
    nq = seq // tile
    scale = _attn_scale(kind)

    def body(*refs):
        if kind == "A":
            q_ref, k_ref, v_ref, o_ref, lse_ref = refs
            aux_ref = None
        else:
            q_ref, k_ref, v_ref, aux_ref, o_ref, lse_ref = refs
        qi = pl.program_id(2)
        q2 = q_ref[...]
        res = []
        for h in range(2):
            mq, _ = _head_masks(kind, tile, dq, h)
            qh = jnp.where(mq, q2, jnp.zeros_like(q2))

            def step(kj, carry, m=None, diag=False):
                mx, l, acc = carry
                ks = pl.ds(pl.multiple_of(kj * tile, tile), tile)
                kt = k_ref[ks, :]
                vt = v_ref[ks, :]
                s = _dot_nt(qh, kt) * scale
                if kind == "B":
                    s = jnp.where(_tile_mask("B", tile, m), s + aux_ref[h, m], NEG)
                if kind == "C":
                    s = s - aux_ref[0, h, pl.ds(kj, 1), :]
                if diag:
                    s = jnp.where(_tile_mask(kind, tile), s, NEG)
                mn = jnp.maximum(mx, jnp.max(s, axis=-1, keepdims=True))
                alpha = jnp.exp(mx - mn)
                p = jnp.exp(s - mn)
                l = alpha * l + jnp.sum(p, axis=-1, keepdims=True)
                acc = alpha * acc + _dot(p.astype(BF16), vt)
                return mn, l, acc

            init = (jnp.full((tile, 1), NEG, F32), jnp.zeros((tile, 1), F32), jnp.zeros((tile, LANE), F32))
            if kind == "B":
                m0 = jnp.maximum(BAND_TILES - 1 - qi, 0)
                mx, l, acc = lax.fori_loop(m0, BAND_TILES, lambda m, c: step(qi - (BAND_TILES - 1) + m, c, m=m), init)
            else:
                carry = lax.fori_loop(0, qi, lambda kj, c: step(kj, c), init)
                mx, l, acc = step(qi, carry, diag=True)
            res.append((acc / l, mx + jnp.log(l)))
        first = _lane_iota((tile, LANE)) < 64
        o_ref[...] = jnp.where(first, res[0][0], res[1][0])
        lse_ref[...] = jnp.where(first, res[0][1], res[1][1])

    in_specs = [pl.BlockSpec((tile, dq), lambda b, p, i: (b * nq + i, p)),
                pl.BlockSpec((seq, dq), lambda b, p, i: (b, p)),
                pl.BlockSpec((seq, LANE), lambda b, p, i: (b, p))]
    args = [q, k, v]
    if kind == "B":
        in_specs.append(pl.BlockSpec((2, BAND_TILES, tile, tile), lambda b, p, i: (p, 0, 0, 0)))
        args.append(aux)
    if kind == "C":
        in_specs.append(pl.BlockSpec((1, 2, nq, tile), lambda b, p, i: (b, p, 0, 0)))
        args.append(aux)
    out_spec = pl.BlockSpec((tile, LANE), lambda b, p, i: (b * nq + i, p))
    return pl.pallas_call(
        body, name="attn_fwd_" + kind, grid=(nb, 3, nq), in_specs=in_specs, out_specs=[out_spec, out_spec],
        out_shape=[jax.ShapeDtypeStruct((nb * seq, 384), F32)] * 2,
        compiler_params=_params(("parallel", "parallel", "parallel")),
    )(*args)


def _attn_bwd(kind, q, k, v, o, do, lse, aux, nb, seq):
    dq = q.shape[1] // 3
    tile = T_BAND if kind == "B" else T_CAUSAL
    nq = seq // tile
    scale = _attn_scale(kind)
    dqk_dtype = F32 if kind == "A" else BF16

    def body(*refs):
        dfr_ref = dfq_ref = dbt_ref = aux_ref = None
        if kind == "A":
            q_ref, k_ref, v_ref, o_ref, do_ref, lse_ref, dq_ref, dk_ref, dv_ref, dk_acc, dv_acc = refs
        elif kind == "B":
            q_ref, k_ref, v_ref, o_ref, do_ref, lse_ref, aux_ref, dq_ref, dk_ref, dv_ref, dbt_ref, dk_acc, dv_acc = refs
        else:
            (q_ref, k_ref, v_ref, o_ref, do_ref, lse_ref, aux_ref, dq_ref, dk_ref, dv_ref, dfr_ref, dfq_ref,
             dk_acc, dv_acc) = refs
        dk_acc[...] = jnp.zeros_like(dk_acc)
        dv_acc[...] = jnp.zeros_like(dv_acc)
        if kind == "C":
            dfr_ref[...] = jnp.zeros_like(dfr_ref)
        if kind == "B":
            @pl.when(pl.program_id(1) == 0)
            def _():
                dbt_ref[...] = jnp.zeros_like(dbt_ref)

        def q_step(qi, _):
            qs = pl.ds(pl.multiple_of(qi * tile, tile), tile)
            q2 = q_ref[qs, :]
            do2 = do_ref[qs, :]
            o2 = o_ref[qs, :]
            lse2 = lse_ref[qs, :]
            dq_tot = jnp.zeros((tile, dq), F32)
            row_sums = []
            for h in range(2):
                mq, mv = _head_masks(kind, tile, dq, h)
                qh = jnp.where(mq, q2, jnp.zeros_like(q2))
                doh = jnp.where(mv, do2, 0.0)
                dob = doh.astype(BF16)
                delta = jnp.sum(doh * o2, axis=-1, keepdims=True)
                lseh = jnp.max(jnp.where(mv, lse2, NEG), axis=-1, keepdims=True)

                def step(kj, carry, m=None, diag=False):
                    dq_t, rs = carry
                    ks = pl.ds(pl.multiple_of(kj * tile, tile), tile)
                    kt = k_ref[ks, :]
                    vt = v_ref[ks, :]
                    s = _dot_nt(qh, kt) * scale
                    if kind == "B":
                        s = jnp.where(_tile_mask("B", tile, m), s + aux_ref[h, m], NEG)
                    if kind == "C":
                        s = s - aux_ref[0, h, pl.ds(kj, 1), :]
                    if diag:
                        s = jnp.where(_tile_mask(kind, tile), s, NEG)
                    p = jnp.exp(s - lseh)
                    ds = p * (_dot_nt(dob, vt) - delta)
                    if kind == "B":
                        dbt_ref[h, m] += ds
                    if kind == "C":
                        dfr_ref[0, h, pl.ds(kj, 1), :] -= jnp.sum(ds, axis=0, keepdims=True)
                        rs = rs + jnp.sum(ds, axis=-1, keepdims=True)
                    dss = (ds * scale).astype(BF16)
                    dv_acc[ks, :] += _dot_tn(p.astype(BF16), dob)
                    dk_acc[ks, :] += _dot_tn(dss, qh)
                    return dq_t + jnp.where(mq, _dot(dss, kt), 0.0), rs

                zero = (jnp.zeros((tile, dq), F32), jnp.zeros((tile, 1), F32))
                if kind == "B":
                    m0 = jnp.maximum(BAND_TILES - 1 - qi, 0)
                    dq_h, rs_h = lax.fori_loop(m0, BAND_TILES, lambda m, c: step(qi - (BAND_TILES - 1) + m, c, m=m), zero)
                else:
                    carry = lax.fori_loop(0, qi, lambda kj, c: step(kj, c), zero)
                    dq_h, rs_h = step(qi, carry, diag=True)
                dq_tot = dq_tot + dq_h
                row_sums.append(rs_h)
            dq_ref[qs, :] = dq_tot.astype(dqk_dtype)
            if kind == "C":
                dfq_ref[qs, :] = jnp.where(_lane_iota((tile, LANE)) < 64, row_sums[0], row_sums[1])
            return 0

        lax.fori_loop(0, nq, q_step, 0)
        dk_ref[...] = dk_acc[...].astype(dqk_dtype)
        dv_ref[...] = dv_acc[...].astype(BF16)

    def seq_spec(cols):
        return pl.BlockSpec((seq, cols), lambda p, b: (b, p))

    in_specs = [seq_spec(dq), seq_spec(dq), seq_spec(LANE), seq_spec(LANE), seq_spec(LANE), seq_spec(LANE)]
    args = [q, k, v, o, do, lse]
    out_specs = [seq_spec(dq), seq_spec(dq), seq_spec(LANE)]
    out_shape = [jax.ShapeDtypeStruct((nb * seq, 3 * dq), dqk_dtype)] * 2 + [jax.ShapeDtypeStruct((nb * seq, 384), BF16)]
    if kind == "B":
        spec = pl.BlockSpec((2, BAND_TILES, tile, tile), lambda p, b: (p, 0, 0, 0))
        in_specs.append(spec)
        args.append(aux)
        out_specs.append(spec)
        out_shape.append(jax.ShapeDtypeStruct((6, BAND_TILES, tile, tile), F32))
    if kind == "C":
        spec = pl.BlockSpec((1, 2, nq, tile), lambda p, b: (b, p, 0, 0))
        in_specs.append(spec)
        args.append(aux)
        out_specs += [spec, seq_spec(LANE)]
        out_shape += [jax.ShapeDtypeStruct((nb, 6, nq, tile), F32), jax.ShapeDtypeStruct((nb * seq, 384), F32)]
    return pl.pallas_call(
        body, name="attn_bwd_" + kind, grid=(3, nb), in_specs=in_specs, out_specs=out_specs, out_shape=out_shape,
        scratch_shapes=[pltpu.VMEM((seq, dq), F32), pltpu.VMEM((seq, LANE), F32)],
        compiler_params=_params(("arbitrary", "arbitrary")),
    )(*args)


BAND_W = BAND_TILES * T_BAND


def _segments(kind, qi, tile):
    r0 = qi * tile
    if kind == "B":
        lo = max(qi - (BAND_TILES - 1), 0) * tile
        return [(lo, r0 + tile, False, lo - (qi - (BAND_TILES - 1)) * tile)]
    return ([(0, r0, False, 0)] if qi else []) + [(r0, r0 + tile, True, 0)]


def _scores(kind, qh, k_ref, aux_ref, h, seg, tile, scale):
    a, b, diag, c0 = seg
    s = _dot_nt(qh, k_ref[a:b, :]) * (scale * LOG2E)
    if kind == "B":
        return s + aux_ref[h, :, c0:BAND_W]
    if kind == "C":
        s = s - aux_ref[0, h, :, a:b]
    if diag:
        s = jnp.where(_tile_mask(kind, tile), s, NEG)
    return s


def _second_head(kind, pair, fn):
    del kind, pair
    fn()


def _attn_fwd(kind, q, k, v, aux, nb, seq):
    dq = q.shape[1] // 3
    tile = T_BAND if kind == "B" else T_CAUSAL
    nq = seq // tile
    scale = _attn_scale(kind)

    def body(*refs):
        if kind == "A":
            q_ref, k_ref, v_ref, o_ref, lse_ref = refs
            aux_ref = None
        else:
            q_ref, k_ref, v_ref, aux_ref, o_ref, lse_ref = refs
        def logits(qi, h):
            rows = slice(qi * tile, (qi + 1) * tile)
            q2 = q_ref[rows, :]
            mq, _ = _head_masks(kind, tile, dq, h)
            qh = jnp.where(mq, q2, jnp.zeros_like(q2))
            segs = _segments(kind, qi, tile)
            return rows, segs, [_scores(kind, qh, k_ref, aux_ref, h, seg, tile, scale) for seg in segs]

        def finish(h, rows, segs, ss):
            _, mv = _head_masks(kind, tile, dq, h)
            mx = functools.reduce(jnp.maximum, [jnp.max(s, axis=-1, keepdims=True) for s in ss])
            ps = [jnp.exp2(s - mx) for s in ss]
            l = functools.reduce(jnp.add, [jnp.sum(p, axis=-1, keepdims=True) for p in ps])
            acc = functools.reduce(jnp.add, [_dot(p.astype(BF16), v_ref[seg[0]:seg[1], :]) for p, seg in zip(ps, segs)])
            o_h = jnp.where(mv, acc / l, 0.0)
            lse_h = jnp.where(mv, mx + jnp.log(l) * LOG2E, 0.0)
            if h == 0:
                o_ref[rows, :] = o_h
                lse_ref[rows, :] = lse_h
            else:
                o_ref[rows, :] += o_h
                lse_ref[rows, :] += lse_h

        units = [(qi, h) for qi in range(nq) for h in range(2)]
        ahead = logits(*units[0])
        for i, (_, h) in enumerate(units):
            current = ahead
            ahead = logits(*units[i + 1]) if i + 1 < len(units) else None
            finish(h, *current)

    def seq_spec(cols):
        return pl.BlockSpec((seq, cols), lambda b, p: (b, p))

    in_specs = [seq_spec(dq), seq_spec(dq), seq_spec(LANE)]
    args = [q, k, v]
    if kind == "B":
        in_specs.append(pl.BlockSpec((2, tile, BAND_W), lambda b, p: (p, 0, 0)))
        args.append(aux)
    if kind == "C":
        in_specs.append(pl.BlockSpec((1, 2, 1, seq), lambda b, p: (b, p, 0, 0)))
        args.append(aux)
    return pl.pallas_call(
        body, name="attn_fwd_" + kind, grid=(nb, 3), in_specs=in_specs, out_specs=[seq_spec(LANE), seq_spec(LANE)],
        out_shape=[jax.ShapeDtypeStruct((nb * seq, 384), F32)] * 2, compiler_params=_params(("parallel", "parallel")),
    )(*args)


def _attn_bwd(kind, q, k, v, o, do, lse, aux, nb, seq):
    dq = q.shape[1] // 3
    tile = T_BAND if kind == "B" else T_CAUSAL
    nq = seq // tile
    scale = _attn_scale(kind)
    dqk_dtype = F32 if kind == "A" else BF16

    def body(*refs):
        dfr_ref = dfq_ref = dbt_ref = aux_ref = None
        if kind == "A":
            q_ref, k_ref, v_ref, o_ref, do_ref, lse_ref, dq_ref, dk_ref, dv_ref, dkt_acc, dvt_acc = refs
        elif kind == "B":
            q_ref, k_ref, v_ref, o_ref, do_ref, lse_ref, aux_ref, dq_ref, dk_ref, dv_ref, dbt_ref, dkt_acc, dvt_acc = refs
        else:
            (q_ref, k_ref, v_ref, o_ref, do_ref, lse_ref, aux_ref, dq_ref, dk_ref, dv_ref, dfr_ref, dfq_ref,
             dkt_acc, dvt_acc) = refs
        dkt_acc[...] = jnp.zeros_like(dkt_acc)
        dvt_acc[...] = jnp.zeros_like(dvt_acc)
        if kind == "C":
            dfr_ref[...] = jnp.zeros_like(dfr_ref)
        if kind == "B":
            @pl.when(pl.program_id(1) == 0)
            def _():
                dbt_ref[...] = jnp.zeros_like(dbt_ref)

        def products(qi, h):
            rows = slice(qi * tile, (qi + 1) * tile)
            q2 = q_ref[rows, :]
            mq, mv = _head_masks(kind, tile, dq, h)
            qh = jnp.where(mq, q2, jnp.zeros_like(q2))
            doh = jnp.where(mv, do_ref[rows, :], 0.0)
            dob = doh.astype(BF16)
            segs = _segments(kind, qi, tile)
            ts = [_scores(kind, qh, k_ref, aux_ref, h, seg, tile, scale) for seg in segs]
            dps = [_dot_nt(dob, v_ref[seg[0]:seg[1], :]) for seg in segs]
            return rows, segs, qh, doh, ts, dps

        def finish(h, rows, segs, qh, doh, ts, dps):
            mq, mv = _head_masks(kind, tile, dq, h)
            qht = qh.astype(F32).T.astype(BF16)
            dobt = doh.T.astype(BF16)
            delta = jnp.sum(doh * o_ref[rows, :], axis=-1, keepdims=True)
            lseh = jnp.max(jnp.where(mv, lse_ref[rows, :], NEG), axis=-1, keepdims=True)
            rs = jnp.zeros((tile, 1), F32)
            dq_h = jnp.zeros((tile, dq), F32)
            for (a, b, _, c0), t, dp in zip(segs, ts, dps):
                p = jnp.exp2(t - lseh)
                ds = p * (dp - delta)
                if kind == "B":
                    dbt_ref[h, :, c0:BAND_W] += ds
                if kind == "C":
                    dfr_ref[0, h, :, a:b] -= jnp.sum(ds, axis=0, keepdims=True)
                    rs = rs + jnp.sum(ds, axis=-1, keepdims=True)
                dss = (ds * scale).astype(BF16)
                dvt_acc[:, a:b] += _dot(dobt, p.astype(BF16))
                dkt_acc[:, a:b] += _dot(qht, dss)
                dq_h = dq_h + _dot(dss, k_ref[a:b, :])
            dq_h = jnp.where(mq, dq_h, 0.0).astype(dqk_dtype)
            if h == 0:
                dq_ref[rows, :] = dq_h
            else:
                dq_ref[rows, :] += dq_h
            if kind == "C":
                if h == 0:
                    dfq_ref[rows, :] = jnp.where(mv, rs, 0.0)
                else:
                    dfq_ref[rows, :] += jnp.where(mv, rs, 0.0)

        units = [(qi, h) for qi in range(nq) for h in range(2)]
        ahead = products(*units[0])
        for i, (_, h) in enumerate(units):
            current = ahead
            ahead = products(*units[i + 1]) if i + 1 < len(units) else None
            finish(h, *current)
        for j in range(seq // 256):
            cols = slice(256 * j, 256 * (j + 1))
            dk_ref[cols, :] = dkt_acc[:, cols].T.astype(dqk_dtype)
            dv_ref[cols, :] = dvt_acc[:, cols].T.astype(BF16)

    def seq_spec(cols):
        return pl.BlockSpec((seq, cols), lambda p, b: (b, p))

    in_specs = [seq_spec(dq), seq_spec(dq), seq_spec(LANE), seq_spec(LANE), seq_spec(LANE), seq_spec(LANE)]
    args = [q, k, v, o, do, lse]
    out_specs = [seq_spec(dq), seq_spec(dq), seq_spec(LANE)]
    out_shape = [jax.ShapeDtypeStruct((nb * seq, 3 * dq), dqk_dtype)] * 2 + [jax.ShapeDtypeStruct((nb * seq, 384), BF16)]
    if kind == "B":
        spec = pl.BlockSpec((2, tile, BAND_W), lambda p, b: (p, 0, 0))
        in_specs.append(spec)
        args.append(aux)
        out_specs.append(spec)
        out_shape.append(jax.ShapeDtypeStruct((6, tile, BAND_W), F32))
    if kind == "C":
        spec = pl.BlockSpec((1, 2, 1, seq), lambda p, b: (b, p, 0, 0))
        in_specs.append(spec)
        args.append(aux)
        out_specs += [spec, seq_spec(LANE)]
        out_shape += [jax.ShapeDtypeStruct((nb, 6, 1, seq), F32), jax.ShapeDtypeStruct((nb * seq, 384), F32)]
    return pl.pallas_call(
        body, name="attn_bwd_" + kind, grid=(3, nb), in_specs=in_specs, out_specs=out_specs, out_shape=out_shape,
        scratch_shapes=[pltpu.VMEM((dq, seq), F32), pltpu.VMEM((LANE, seq), F32)],
        compiler_params=_params(("arbitrary", "arbitrary")),
    )(*args)


BIAS_FLAT = T_BAND * BAND_W
BIAS_CHUNK = 4 * BAND_W


def _rel_onehot(chunk):
    lane = lax.broadcasted_iota(jnp.int32, (384, chunk), 1)
    r = lax.broadcasted_iota(jnp.int32, (384, chunk), 0)
    sub = jnp.where(lane >= BAND_W, 1, 0) + jnp.where(lane >= 2 * BAND_W, 1, 0) + jnp.where(lane >= 3 * BAND_W, 1, 0)
    i = pl.program_id(0) * 4 + sub
    col = lane - sub * BAND_W
    idx = jnp.clip(BAND_W - T_BAND + i - col, -REL_CLIP, REL_CLIP) + REL_CLIP
    return jnp.where(idx == r, 1.0, 0.0).astype(BF16)


BIAS_G = 768
BIAS_EDGE = BIAS_G - N_REL


def _bias_line(rel_bias):
    g = jnp.concatenate([jnp.broadcast_to(rel_bias[:, N_REL - 1:], (rel_bias.shape[0], BIAS_EDGE)),
                         jnp.flip(rel_bias, axis=1)], axis=1)
    return jnp.pad(g, ((0, 8 - g.shape[0]), (0, 0)))


def _bias_unline(dg):
    return jnp.flip(dg[:, BIAS_EDGE:], axis=1)


def _bias_expand(g8):
    def body(g_ref, out_ref):
        line = jnp.broadcast_to(g_ref[0] * LOG2E, (T_BAND, BIAS_G))
        slab = pltpu.roll(line, 1, 1, stride=1, stride_axis=0)[:, LANE:BIAS_G]
        row = lax.broadcasted_iota(jnp.int32, (T_BAND, BAND_W), 0)
        col = lax.broadcasted_iota(jnp.int32, (T_BAND, BAND_W), 1)
        hidden = ((row >= 64) & (col < 64)) | ((row < 64) & (col >= BAND_W - 64))
        out_ref[0] = jnp.where(hidden, NEG, slab)

    return pl.pallas_call(
        body, name="bias_expand", grid=(8,), in_specs=[pl.BlockSpec((1, 1, BIAS_G), lambda h: (h, 0, 0))],
        out_specs=pl.BlockSpec((1, T_BAND, BAND_W), lambda h: (h, 0, 0)),
        out_shape=jax.ShapeDtypeStruct((8, T_BAND, BAND_W), F32), compiler_params=_params(("parallel",)),
    )(g8.reshape(8, 1, BIAS_G))


def _bias_reduce(d_slab):
    def body(d_ref, out_ref):
        r = lax.broadcasted_iota(jnp.int32, (T_BAND, T_BAND), 0)
        k = lax.broadcasted_iota(jnp.int32, (T_BAND, T_BAND), 1)
        flip = jnp.where(r + k == T_BAND - 1, 1.0, 0.0).astype(BF16)
        hi, mid, lo = _split3(d_ref[0])
        d_rev = _dot(flip, hi) + _dot(flip, mid) + _dot(flip, lo)
        wide = jnp.concatenate([jnp.zeros((T_BAND, LANE), F32), d_rev, jnp.zeros((T_BAND, 2 * LANE), F32)], axis=1)
        skew = pltpu.roll(wide, 0, 1, stride=1, stride_axis=0)
        dg = jnp.sum(skew, axis=0, keepdims=True)[:, LANE:LANE + BIAS_G]
        lane = _lane_iota((1, BIAS_G))
        clipped = jnp.sum(jnp.where(lane <= BIAS_EDGE, dg, 0.0), axis=1, keepdims=True)
        out_ref[0] = jnp.where(lane == BIAS_EDGE, clipped, dg)

    return pl.pallas_call(
        body, name="bias_reduce", grid=(8,), in_specs=[pl.BlockSpec((1, T_BAND, BAND_W), lambda h: (h, 0, 0))],
        out_specs=pl.BlockSpec((1, 1, BIAS_G), lambda h: (h, 0, 0)),
        out_shape=jax.ShapeDtypeStruct((8, 1, BIAS_G), F32), compiler_params=_params(("parallel",)),
    )(d_slab).reshape(8, BIAS_G)


def _outproj_bwd(dxn, y, gate, oa, ob, oc, zg, w_out_p, w_out_pt, nb, seq):
    t = dxn.shape[0]
    tpe = seq // TM

    def body(dxn_ref, y_ref, gate_ref, oa_ref, ob_ref, oc_ref, zg_ref, w_ref, wt_ref,
             doa_ref, dob_ref, doc_ref, dzg_ref, gw_ref, dgate_ref):
        i = pl.program_id(0)

        @pl.when(i == 0)
        def _():
            gw_ref[...] = jnp.zeros_like(gw_ref)

        @pl.when(i % tpe == 0)
        def _():
            dgate_ref[...] = jnp.zeros_like(dgate_ref)

        dxn_t = dxn_ref[...]
        dgate_ref[0] += jnp.sum(dxn_t * y_ref[...], axis=0, keepdims=True)
        dy = (dxn_t * gate_ref[0]).astype(BF16)
        for gi, (o_ref, do_ref) in enumerate(((oa_ref, doa_ref), (ob_ref, dob_ref), (oc_ref, doc_ref))):
            cols = slice(384 * gi, 384 * (gi + 1))
            u = zg_ref[:, cols]
            o_t = o_ref[...]
            su = _silu(u)
            dcat = _dot(dy, wt_ref[:, cols])
            do_ref[...] = dcat * su
            dzg_ref[:, cols] = (dcat * o_t * _dsilu(u)).astype(BF16)
            gw_ref[cols, :] += _dot_tn((o_t * su).astype(BF16), dy)

    return pl.pallas_call(
        body, name="outproj_bwd", grid=(t // TM,),
        in_specs=[_row_spec(D_MODEL), _row_spec(D_MODEL), _ex_spec(tpe), _row_spec(384), _row_spec(384), _row_spec(384),
                  _row_spec(D_CAT), _full_spec((D_CAT, D_MODEL)), _full_spec((D_MODEL, D_CAT))],
        out_specs=[_row_spec(384), _row_spec(384), _row_spec(384), _row_spec(D_CAT), _full_spec((D_CAT, D_MODEL)),
                   _ex_spec(tpe)],
        out_shape=[jax.ShapeDtypeStruct((t, 384), F32)] * 3 + [jax.ShapeDtypeStruct((t, D_CAT), BF16),
                                                                jax.ShapeDtypeStruct((D_CAT, D_MODEL), F32),
                                                                jax.ShapeDtypeStruct((nb, 1, D_MODEL), F32)],
        compiler_params=_params(("arbitrary",)),
    )(dxn, y, gate, oa, ob, oc, zg, w_out_p, w_out_pt)


def _a_up_bwd(dqa, dka, dva, za, gq, gkv, w_uq_pt, w_ukv_pt, cos, sina, sinb):
    t = za.shape[0]

    def body(dq_ref, dk_ref, dv_ref, za_ref, gq_ref, gkv_ref, wqt_ref, wkvt_ref, cos_ref, sa_ref, sb_ref,
             dza_ref, gwq_ref, gwkv_ref, ggq_ref, ggkv_ref, dqb, dkvb):
        @pl.when(pl.program_id(0) == 0)
        def _():
            gwq_ref[...] = jnp.zeros_like(gwq_ref)
            gwkv_ref[...] = jnp.zeros_like(gwkv_ref)
            ggq_ref[...] = jnp.zeros_like(ggq_ref)
            ggkv_ref[...] = jnp.zeros_like(ggkv_ref)

        cos_t, sa, sb = cos_ref[...], sa_ref[...], sb_ref[...]
        dkpe = jnp.zeros((TM, LANE), F32)
        for p in range(3):
            dqb[:, 256 * p:256 * p + 128] = dq_ref[:, 256 * p:256 * p + 128].astype(BF16)
            dqb[:, 256 * p + 128:256 * p + 256] = _rope_t(dq_ref[:, 256 * p + 128:256 * p + 256], cos_t, sa, sb).astype(BF16)
            dkvb[:, 128 * p:128 * p + 128] = dk_ref[:, 256 * p:256 * p + 128].astype(BF16)
            dkpe = dkpe + dk_ref[:, 256 * p + 128:256 * p + 256]
        dkvb[:, 384:768] = dv_ref[...]
        dkpe = _rope_t(dkpe, cos_t, sa, sb)
        dkpe = jnp.where(_lane_iota((TM, LANE)) < A_ROPE, dkpe + pltpu.roll(dkpe, LANE - 32, 1), 0.0)

        gqv = gq_ref[...]
        cqn, cqh, rq = _rms(za_ref[:, 0:384], gqv)
        dq_t = dqb[...]
        gwq_ref[...] += _dot_tn(cqn.astype(BF16), dq_t)
        dcqn = _dot(dq_t, wqt_ref[...])
        ggq_ref[...] += jnp.broadcast_to(jnp.sum(dcqn * cqh, axis=0, keepdims=True), ggq_ref.shape)
        dza_ref[:, 0:384] = _rms_bwd(dcqn, cqh, rq, gqv).astype(BF16)

        gkvv = gkv_ref[...]
        ckvn, ckvh, rkv = _rms(za_ref[:, 384:640], gkvv)
        dkv_t = dkvb[...]
        gwkv_ref[...] += _dot_tn(ckvn.astype(BF16), dkv_t)
        dckvn = _dot(dkv_t, wkvt_ref[...])
        ggkv_ref[...] += jnp.broadcast_to(jnp.sum(dckvn * ckvh, axis=0, keepdims=True), ggkv_ref.shape)
        dza_ref[:, 384:640] = _rms_bwd(dckvn, ckvh, rkv, gkvv).astype(BF16)
        dza_ref[:, 640:768] = dkpe.astype(BF16)

    return pl.pallas_call(
        body, name="a_up_bwd", grid=(t // TM,),
        in_specs=[_row_spec(768), _row_spec(768), _row_spec(384), _row_spec(768), _full_spec((1, 384)),
                  _full_spec((1, 256)), _full_spec((768, 384)), _full_spec((768, 256)), _row_spec(LANE), _row_spec(LANE),
                  _row_spec(LANE)],
        out_specs=[_row_spec(768), _full_spec((384, 768)), _full_spec((256, 768)), _full_spec((8, 384)),
                   _full_spec((8, 256))],
        out_shape=[jax.ShapeDtypeStruct((t, 768), BF16), jax.ShapeDtypeStruct((384, 768), F32),
                   jax.ShapeDtypeStruct((256, 768), F32), jax.ShapeDtypeStruct((8, 384), F32),
                   jax.ShapeDtypeStruct((8, 256), F32)],
        scratch_shapes=[pltpu.VMEM((TM, 768), BF16), pltpu.VMEM((TM, 768), BF16)],
        compiler_params=_params(("arbitrary",)),
    )(dqa, dka, dva, za, gq, gkv, w_uq_pt, w_ukv_pt, cos, sina, sinb)


def _dz_cols():
    return (Z_A, Z_G) + Z_QKV + (Z_F,)


def _inproj_bwd_dx(dz, dxn, x, shift, scale, g, w_in_pt, nb, seq):
    t = x.shape[0]
    tpe = seq // TM
    cols = _dz_cols()

    def body(*refs):
        dz_refs = refs[:len(cols)]
        dxn_ref, x_ref, sh_ref, sc_ref, g_ref, wt_ref, dx_ref, dsh_ref, dsc_ref, dg_ref = refs[len(cols):]
        i = pl.program_id(0)

        @pl.when(i == 0)
        def _():
            dg_ref[...] = jnp.zeros_like(dg_ref)

        @pl.when(i % tpe == 0)
        def _():
            dsh_ref[...] = jnp.zeros_like(dsh_ref)
            dsc_ref[...] = jnp.zeros_like(dsc_ref)

        dh = jnp.zeros((TM, D_MODEL), F32)
        for ref, (c0, c1) in zip(dz_refs, cols):
            dh = dh + _dot_nt(ref[...], wt_ref[:, c0:c1])
        gv = g_ref[...]
        n, xh, r = _rms(x_ref[...], gv)
        dsh_ref[0] += jnp.sum(dh, axis=0, keepdims=True)
        dsc_ref[0] += jnp.sum(dh * n, axis=0, keepdims=True)
        dn = dh * (1.0 + sc_ref[0])
        dg_ref[...] += jnp.broadcast_to(jnp.sum(dn * xh, axis=0, keepdims=True), dg_ref.shape)
        dx_ref[...] = dxn_ref[...] + _rms_bwd(dn, xh, r, gv)

    in_specs = [_row_spec(c1 - c0) for c0, c1 in cols]
    in_specs += [_row_spec(D_MODEL), _row_spec(D_MODEL), _ex_spec(tpe), _ex_spec(tpe), _full_spec((1, D_MODEL)),
                 _full_spec((D_MODEL, NP_IN))]
    return pl.pallas_call(
        body, name="inproj_bwd_dx", grid=(t // TM,), in_specs=in_specs,
        out_specs=[_row_spec(D_MODEL), _ex_spec(tpe), _ex_spec(tpe), _full_spec((8, D_MODEL))],
        out_shape=[jax.ShapeDtypeStruct((t, D_MODEL), F32), jax.ShapeDtypeStruct((nb, 1, D_MODEL), F32),
                   jax.ShapeDtypeStruct((nb, 1, D_MODEL), F32), jax.ShapeDtypeStruct((8, D_MODEL), F32)],
        compiler_params=_params(("arbitrary",)),
    )(*dz, dxn, x, shift, scale, g, w_in_pt)


def _inproj_bwd_dw(h, dz, name, layer, both=None):
    t = h.shape[0]
    widths = [d.shape[1] for d in dz]
    total = sum(widths)

    def body(*refs):
        h_ref = refs[0]
        dz_refs = refs[1:1 + len(dz)]
        gw_ref = refs[-1]

        @pl.when(pl.program_id(0) == 0)
        def _():
            gw_ref[...] = jnp.zeros_like(gw_ref)

        h_t = h_ref[...]
        c0 = 0
        for ref, w in zip(dz_refs, widths):
            gw_ref[0, :, c0:c0 + w] += _dot_tn(h_t, ref[...])
            c0 += w

    in_specs = [_row_spec(D_MODEL)] + [_row_spec(w) for w in widths]
    args = [h, *dz]
    aliases = {}
    if both is not None:
        in_specs.append(pl.BlockSpec(memory_space=pl.ANY))
        aliases = {len(args): 0}
        args.append(both)
    return pl.pallas_call(
        body, name=name, grid=(t // TM,), in_specs=in_specs,
        out_specs=pl.BlockSpec((1, D_MODEL, total), lambda i: (layer, 0, 0)),
        out_shape=jax.ShapeDtypeStruct((DEPTH, D_MODEL, total), F32), input_output_aliases=aliases,
        compiler_params=_params(("arbitrary",)),
    )(*args)


def _ada_fwd(c_all, w_ada, b_cols):
    n = c_all.shape[0]
    cols = w_ada.shape[2]

    def body(c_ref, w_ref, b_ref, out_ref):
        act = _silu(c_ref[...]).astype(BF16)
        out_ref[0] = _dot(act, w_ref[0].astype(BF16)) + b_ref[0]

    return pl.pallas_call(
        body, name="ada_fwd", grid=(DEPTH,),
        in_specs=[pl.BlockSpec((n, D_MODEL), lambda l: (0, 0)), pl.BlockSpec((1, D_MODEL, cols), lambda l: (l, 0, 0)),
                  pl.BlockSpec((1, 1, cols), lambda l: (l, 0, 0))],
        out_specs=pl.BlockSpec((1, n, cols), lambda l: (l, 0, 0)),
        out_shape=jax.ShapeDtypeStruct((DEPTH, n, cols), F32), compiler_params=_params(("parallel",)),
    )(c_all, w_ada, b_cols)


def _ada_bwd(c_all, dmod_cols, dmod_all):
    n = c_all.shape[0]
    cols = dmod_cols.shape[2]

    def body(c_ref, dc_ref, da_ref, gw_ref, gb_ref):
        act = _silu(c_ref[...]).astype(BF16)
        gw_ref[0] = _dot_tn(act, dc_ref[0].astype(BF16))
        gb_ref[0] = jnp.sum(da_ref[0], axis=0, keepdims=True)

    return pl.pallas_call(
        body, name="ada_bwd", grid=(DEPTH,),
        in_specs=[pl.BlockSpec((n, D_MODEL), lambda l: (0, 0)), pl.BlockSpec((1, n, cols), lambda l: (l, 0, 0)),
                  pl.BlockSpec((1, n, 3 * D_MODEL), lambda l: (l, 0, 0))],
        out_specs=[pl.BlockSpec((1, D_MODEL, cols), lambda l: (l, 0, 0)),
                   pl.BlockSpec((1, 1, 3 * D_MODEL), lambda l: (l, 0, 0))],
        out_shape=[jax.ShapeDtypeStruct((DEPTH, D_MODEL, cols), F32), jax.ShapeDtypeStruct((DEPTH, 1, 3 * D_MODEL), F32)],
        compiler_params=_params(("parallel",)),
    )(c_all, dmod_cols, dmod_all)


def _sum_blocks(parts, name):
    n, rows, cols = parts.shape
    tr = rows if rows <= 256 else 8 * next(d for d in range(32, 0, -1) if (rows // 8) % d == 0)

    def body(p_ref, out_ref):
        acc = p_ref[0].astype(F32)
        for k in range(1, n):
            acc = acc + p_ref[k].astype(F32)
        out_ref[...] = acc

    return pl.pallas_call(
        body, name=name, grid=(rows // tr,), in_specs=[pl.BlockSpec((n, tr, cols), lambda i: (0, i, 0))],
        out_specs=pl.BlockSpec((tr, cols), lambda i: (i, 0)), out_shape=jax.ShapeDtypeStruct((rows, cols), F32),
        compiler_params=_params(("parallel",)),
    )(parts)


def _adamw(w, g, m, v, name):
    nl, rows, cols = w.shape
    if rows * cols <= 64 * 1024:
        tr = rows
        tl = next(t for t in range(nl, 0, -1) if nl % t == 0 and t * max(rows, 8) * cols <= 512 * 1024)
    else:
        tl = 1
        tr = next(t for t in (rows, 256, 128, 64, 32, 16, 8) if rows % t == 0 and t * cols <= 256 * 1024)

    def body(w_ref, g_ref, m_ref, v_ref, d_ref, mo_ref, vo_ref):
        gv = g_ref[...]
        mn = ADAM_B1 * m_ref[...] + (1.0 - ADAM_B1) * gv
        vn = ADAM_B2 * v_ref[...] + (1.0 - ADAM_B2) * jnp.square(gv)
        m_hat = mn / (1.0 - ADAM_B1 ** ADAM_STEP)
        v_hat = vn / (1.0 - ADAM_B2 ** ADAM_STEP)
        d_ref[...] = -ADAM_LR * (m_hat / (jnp.sqrt(v_hat) + ADAM_EPS) + ADAM_WD * w_ref[...])
        mo_ref[...] = mn
        vo_ref[...] = vn

    spec = pl.BlockSpec((tl, tr, cols), lambda l, i: (l, i, 0))
    return pl.pallas_call(
        body, name=name, grid=(nl // tl, rows // tr), in_specs=[spec] * 4, out_specs=[spec] * 3,
        out_shape=[jax.ShapeDtypeStruct((nl, rows, cols), F32)] * 3, compiler_params=_params(("parallel", "parallel")),
    )(w, g, m, v)


ALL_FLIPS = tuple(range(1, N_DEV))


def _exchange(src, flips, mode, name):
    _, rows, cols = src.shape
    nslot = 2 if mode == "pair" else N_DEV
    nf = len(flips)

    def body(src_ref, dst_ref, send_sems, recv_sems, local_sem):
        x, y, c = lax.axis_index("x"), lax.axis_index("y"), lax.axis_index("c")
        me = 4 * x + 2 * y + c

        def slot(j):
            return (j & 1) if mode == "pair" else j

        own = pltpu.make_async_copy(src_ref.at[me if mode == "scatter" else 0], dst_ref.at[slot(me)], local_sem)
        own.start()
        copies = []
        for i, f in enumerate(flips):
            peer = me ^ f
            to = (1 - x if f & 4 else x, 1 - y if f & 2 else y, 1 - c if f & 1 else c)
            cp = pltpu.make_async_remote_copy(
                src_ref=src_ref.at[peer if mode == "scatter" else 0], dst_ref=dst_ref.at[slot(me)],
                send_sem=send_sems.at[i], recv_sem=recv_sems.at[i], device_id=to, device_id_type=pl.DeviceIdType.MESH)
            cp.start()
            copies.append(cp)
        for i, f in enumerate(flips):
            peer = me ^ f
            to = (1 - x if f & 4 else x, 1 - y if f & 2 else y, 1 - c if f & 1 else c)
            pltpu.make_async_remote_copy(
                src_ref=src_ref.at[0], dst_ref=dst_ref.at[slot(peer)], send_sem=send_sems.at[i],
                recv_sem=recv_sems.at[i], device_id=to, device_id_type=pl.DeviceIdType.MESH).wait_recv()
        for cp in copies:
            cp.wait_send()
        own.wait()

    return pl.pallas_call(
        body, name=name, out_shape=jax.ShapeDtypeStruct((nslot, rows, cols), src.dtype),
        in_specs=[pl.BlockSpec(memory_space=pl.ANY)], out_specs=pl.BlockSpec(memory_space=pl.ANY),
        scratch_shapes=[pltpu.SemaphoreType.DMA((nf,)), pltpu.SemaphoreType.DMA((nf,)), pltpu.SemaphoreType.DMA],
    )(src)


def _transfer(name, srcs, dst_shapes, plan, in_place=False):
    n_arr = len(srcs)
    probe = plan(0, 0, 0)
    n_steps = len(probe)

    def body(*refs):
        src_refs, dst_refs = refs[:n_arr], refs[n_arr:2 * n_arr]
        send_sems, recv_sems, local_sems = refs[2 * n_arr:]
        x, y, c = lax.axis_index("x"), lax.axis_index("y"), lax.axis_index("c")
        steps = plan(x, y, c)

        def rows(ref, r0, n):
            return ref.at[:, pl.ds(r0, n), :]

        def arrival(t):
            a, _, _, n, _, f, _ = steps[t]
            return pltpu.make_async_remote_copy(
                src_ref=rows(dst_refs[a], 0, n), dst_ref=rows(dst_refs[a], 0, n), send_sem=send_sems.at[t],
                recv_sem=recv_sems.at[t], device_id=(x, y, c), device_id_type=pl.DeviceIdType.MESH)

        arrived, started = set(), []
        for t, (a, from_dst, sr, n, dr, f, after) in enumerate(steps):
            for u in after:
                if u not in arrived:
                    arrival(u).wait_recv()
                    arrived.add(u)
            src = rows(dst_refs[a] if from_dst else src_refs[a], sr, n)
            dst = rows(dst_refs[a], dr, n)
            if f == 0:
                cp = pltpu.make_async_copy(src, dst, local_sems.at[t])
            else:
                to = (1 - x if f & 4 else x, 1 - y if f & 2 else y, 1 - c if f & 1 else c)
                cp = pltpu.make_async_remote_copy(src_ref=src, dst_ref=dst, send_sem=send_sems.at[t],
                                                  recv_sem=recv_sems.at[t], device_id=to,
                                                  device_id_type=pl.DeviceIdType.MESH)
            cp.start()
            started.append(cp)
        for t, step in enumerate(steps):
            if step[5] != 0 and t not in arrived:
                arrival(t).wait_recv()
        for cp, step in zip(started, steps):
            if step[5] == 0:
                cp.wait()
            else:
                cp.wait_send()

    any_spec = pl.BlockSpec(memory_space=pl.ANY)
    return pl.pallas_call(
        body, name=name, out_shape=[jax.ShapeDtypeStruct(s, d) for s, d in dst_shapes],
        in_specs=[any_spec] * n_arr, out_specs=[any_spec] * n_arr,
        input_output_aliases={a: a for a in range(n_arr)} if in_place else {},
        scratch_shapes=[pltpu.SemaphoreType.DMA((n_steps,)), pltpu.SemaphoreType.DMA((n_steps,)),
                        pltpu.SemaphoreType.DMA((n_steps,))],
    )(*srcs)


CHIP_FLIPS = (2, 4, 6)


def _gather_plan(chip_rows):
    def plan(x, y, c):
        steps = []
        for a, rc in enumerate(chip_rows):
            h = rc // 2
            mine = rc * (2 * x + y) + h * c
            steps.append((a, False, h * c, h, mine, 0, ()))
            ici = {}
            for f in CHIP_FLIPS:
                ici[f] = len(steps)
                steps.append((a, False, h * c, h, mine, f, ()))
            steps.append((a, False, h * c, h, mine, 1, ()))
            for f in CHIP_FLIPS:
                theirs = rc * ((2 * x + y) ^ (f >> 1)) + h * c
                steps.append((a, True, theirs, h, theirs, 1, (ici[f],)))
        return steps
    return plan


def _pair_reduce_plan(chip_rows):
    def plan(x, y, c):
        steps = []
        for a, rc in enumerate(chip_rows):
            h = rc // 2
            for j in range(4):
                steps.append((a, False, rc * j + h * (1 - c), h, h * j, 1, ()))
        return steps
    return plan


def _chip_scatter_plan(chip_rows):
    def plan(x, y, c):
        steps = []
        for a, rc in enumerate(chip_rows):
            h = rc // 2
            for k, f in enumerate(CHIP_FLIPS):
                steps.append((a, False, h * ((2 * x + y) ^ (f >> 1)), h, h * k, f, ()))
        return steps
    return plan


def _pair_share_plan(chip_rows):
    def plan(x, y, c):
        return [(a, True, (rc // 2) * c, rc // 2, (rc // 2) * c, 1, ()) for a, rc in enumerate(chip_rows)]
    return plan


def _tile_rows(h):
    return h


def _sum_pair(partial, recv, core, rc, name):
    nl, _, cols = partial.shape
    h = rc // 2
    tr = _tile_rows(h)

    def body(c_ref, p_ref, r_ref, out_ref):
        out_ref[...] = (p_ref[...] + r_ref[...]).astype(BF16)

    spec = pl.BlockSpec((1, tr, cols), lambda l, j, i, c_ref: (l, (h // tr) * j + i, 0))
    return pl.pallas_call(
        body, name=name, out_shape=jax.ShapeDtypeStruct((nl, 4 * h, cols), BF16),
        grid_spec=pltpu.PrefetchScalarGridSpec(
            num_scalar_prefetch=1, grid=(nl, 4, h // tr),
            in_specs=[pl.BlockSpec((1, tr, cols), lambda l, j, i, c_ref: (l, (rc // tr) * j + (h // tr) * c_ref[0] + i, 0)),
                      spec],
            out_specs=spec),
        compiler_params=_params(("parallel", "parallel", "parallel")),
    )(core, partial, recv)


def _sum_chips(chip_sum, recv, place, rc, name):
    nl, _, cols = chip_sum.shape
    h = rc // 2
    tr = _tile_rows(h)

    def body(s_ref, own_ref, r_ref, out_ref):
        acc = own_ref[0].astype(F32)
        for k in range(3):
            acc = acc + r_ref[0, k].astype(F32)
        out_ref[0] = acc

    return pl.pallas_call(
        body, name=name, out_shape=jax.ShapeDtypeStruct((nl, rc, cols), F32),
        grid_spec=pltpu.PrefetchScalarGridSpec(
            num_scalar_prefetch=1, grid=(nl, h // tr),
            in_specs=[pl.BlockSpec((1, tr, cols), lambda l, i, s_ref: (l, (h // tr) * s_ref[0] + i, 0)),
                      pl.BlockSpec((1, 3, tr, cols), lambda l, i, s_ref: (l, 0, i, 0))],
            out_specs=pl.BlockSpec((1, tr, cols), lambda l, i, s_ref: (l, (h // tr) * s_ref[1] + i, 0))),
        compiler_params=_params(("parallel", "parallel")),
    )(place, chip_sum, recv.reshape(nl, 3, h, cols))


def _sum_slots(parts, out_dtype, name):
    nl, n, rows, cols = parts.shape
    tr = next(t for t in (128, 64, 32, 16) if rows % t == 0 and n * t * cols * 4 <= (4 << 20))

    def body(p_ref, out_ref):
        acc = p_ref[0, 0].astype(F32)
        for k in range(1, n):
            acc = acc + p_ref[0, k].astype(F32)
        out_ref[0] = acc.astype(out_dtype)

    return pl.pallas_call(
        body, name=name, grid=(nl, rows // tr),
        in_specs=[pl.BlockSpec((1, n, tr, cols), lambda l, i: (l, 0, i, 0))],
        out_specs=pl.BlockSpec((1, tr, cols), lambda l, i: (l, i, 0)),
        out_shape=jax.ShapeDtypeStruct((nl, rows, cols), out_dtype), compiler_params=_params(("parallel", "parallel")),
    )(parts)


def _pad_cols(a, n):
    return a if n == 0 else jnp.pad(a, ((0, 0), (0, n)))


def _in_to_padded(w):
    return jnp.concatenate([_pad_cols(w[:, a:b], z) for a, b, z in IN_PIECES], axis=1)


def _in_from_padded(gp):
    pos, out = 0, {}
    for a, b, z in IN_PIECES:
        out[a] = gp[:, pos:pos + (b - a)]
        pos += (b - a) + z
    return jnp.concatenate([out[a] for a in sorted(out)], axis=1)


def _in_cols_first(ga, gb):
    split = Z_G[1]
    pos, out = 0, {}
    for a, b, z in IN_PIECES:
        src, off = (ga, pos) if pos < split else (gb, pos - split)
        out[a] = jnp.transpose(src[:, :, off:off + (b - a)], (2, 0, 1))
        pos += (b - a) + z
    return jnp.concatenate([out[a] for a in sorted(out)], axis=0)


def _out_to_padded(w):
    z = jnp.zeros((64, w.shape[1]), w.dtype)
    return jnp.concatenate([w[0:384], w[384:704], z, w[704:1024], z], axis=0)


def _out_from_padded(gp):
    return jnp.concatenate([gp[0:384], gp[384:704], gp[768:1088]], axis=0)


def _uq_to_padded(w):
    parts = []
    for p in range(3):
        h0, h1 = 2 * p, 2 * p + 1
        parts += [w[:, 96 * h0:96 * h0 + 64], w[:, 96 * h1:96 * h1 + 64], w[:, 96 * h0 + 64:96 * h0 + 96],
                  w[:, 96 * h1 + 64:96 * h1 + 96], jnp.zeros((w.shape[0], 64), w.dtype)]
    return jnp.concatenate(parts, axis=1)


def _uq_from_padded(gp):
    parts = []
    for h in range(6):
        p, s = h // 2, h % 2
        parts += [gp[:, 256 * p + 64 * s:256 * p + 64 * s + 64], gp[:, 256 * p + 128 + 32 * s:256 * p + 160 + 32 * s]]
    return jnp.concatenate(parts, axis=1)


def _ukv_to_padded(w):
    return jnp.concatenate([w[:, 128 * h:128 * h + 64] for h in range(6)]
                           + [w[:, 128 * h + 64:128 * h + 128] for h in range(6)], axis=1)


def _ukv_from_padded(gp):
    parts = []
    for h in range(6):
        parts += [gp[:, 64 * h:64 * h + 64], gp[:, 384 + 64 * h:384 + 64 * h + 64]]
    return jnp.concatenate(parts, axis=1)


LR_ROWS = 224
SHARD_ROWS = (256, 256, 256, LR_ROWS)


def _pack_lowrank(w_uq, w_ukv):
    flat = jnp.concatenate([w_uq.reshape(-1), w_ukv.reshape(-1)])
    return jnp.pad(flat, (0, LR_ROWS * PACK_COLS - flat.shape[0])).reshape(1, LR_ROWS, PACK_COLS)


def _unpack_lowrank(packed):
    flat = packed.reshape(-1)
    n_uq = DEPTH * A_Q_RANK * 144
    n_ukv = DEPTH * A_KV_RANK * 192
    return flat[0:n_uq].reshape(DEPTH, A_Q_RANK, 144), flat[n_uq:n_uq + n_ukv].reshape(DEPTH, A_KV_RANK, 192)


PACK_SIZES = (DEPTH * 256 * N_IN, DEPTH * 256 * D_MODEL, DEPTH * A_Q_RANK * 144, DEPTH * A_KV_RANK * 192)


def _pack(w_in, w_out, w_uq, w_ukv):
    flat = jnp.concatenate([w_in.reshape(-1), w_out.reshape(-1), w_uq.reshape(-1), w_ukv.reshape(-1)])
    flat = jnp.pad(flat, (0, PACK_ROWS * PACK_COLS - flat.shape[0]))
    return flat.reshape(PACK_ROWS, PACK_COLS)


def _unpack(packed):
    flat = packed.reshape(-1)
    o0, o1, o2, o3 = PACK_SIZES
    w_in = flat[0:o0].reshape(DEPTH, 256, N_IN)
    w_out = flat[o0:o0 + o1].reshape(DEPTH, 256, D_MODEL)
    w_uq = flat[o0 + o1:o0 + o1 + o2].reshape(DEPTH, A_Q_RANK, 144)
    w_ukv = flat[o0 + o1 + o2:o0 + o1 + o2 + o3].reshape(DEPTH, A_KV_RANK, 192)
    return w_in, w_out, w_uq, w_ukv


def _rope_tables(positions):
    t = positions.size
    inv = ROPE_THETA ** (-jnp.arange(0, A_ROPE, 2, dtype=F32) / A_ROPE)
    inv_row = jnp.pad(jnp.tile(inv, 4), (0, 64)).reshape(1, LANE)

    def body(p_ref, i_ref, c_ref, sa_ref, sb_ref):
        ang = p_ref[...].astype(F32) * i_ref[...]
        lane = _lane_iota((TM, LANE))
        live = lane < 64
        second = (lane & 31) >= 16
        s = jnp.sin(ang)
        c_ref[...] = jnp.where(live, jnp.cos(ang), 0.0)
        sa_ref[...] = jnp.where(live & jnp.logical_not(second), -s, 0.0)
        sb_ref[...] = jnp.where(live & second, s, 0.0)

    return pl.pallas_call(
        body, name="rope_tables", grid=(t // TM,), in_specs=[_row_spec(1), _full_spec((1, LANE))],
        out_specs=[_row_spec(LANE)] * 3, out_shape=[jax.ShapeDtypeStruct((t, LANE), F32)] * 3,
        compiler_params=_params(("parallel",)),
    )(positions.reshape(t, 1), inv_row)


def _rows(a, n):
    flat = a.reshape(-1)
    return jnp.pad(flat, (0, n * LANE - flat.shape[0])).reshape(n, LANE)


def _forward_backward(x, mod, tables, target, weights, small, nb, seq):
    cos, sina, sinb = tables
    saved = []
    for l in range(DEPTH):
        w, s = weights[l], small[l]
        shift = mod[l][:, None, 0:D_MODEL]
        scale = mod[l][:, None, D_MODEL:2 * D_MODEL]
        gate = mod[l][:, None, 2 * D_MODEL:]
        h, za, zg, qb, kb, vb, qc, kc, vc, zf = _ln_inproj(x, shift, scale, s["norm_g"], w["in"], seq)
        qa, ka, va = _a_up(za, s["gq"], s["gkv"], w["uq"], w["ukv"], cos, sina, sinb)
        bias = _bias_expand(s["g8"])[0:6]
        f = _forget_fwd(zf, s["fb"], nb, seq)
        frow = jnp.pad(f[:, 0:5].reshape(nb, seq, 5).transpose(0, 2, 1), ((0, 0), (0, 1), (0, 0)))
        frow = frow.reshape(nb, 6, 1, seq)
        oa, lse_a = _attn_fwd("A", qa, ka, va, None, nb, seq)
        ob, lse_b = _attn_fwd("B", qb, kb, vb, bias, nb, seq)
        oc, lse_c = _attn_fwd("C", qc, kc, vc, frow, nb, seq)
        y, xn = _gate_outproj(x, gate, oa, ob, oc, zg, w["out"], seq)
        saved.append(dict(x=x, h=h, za=za, zg=zg, zf=zf, y=y, shift=shift, scale=scale, gate=gate, bias=bias, frow=frow,
                          a=(qa, ka, va, oa, lse_a), b=(qb, kb, vb, ob, lse_b), c=(qc, kc, vc, oc, lse_c)))
        x = xn
    dx, loss8, gfinal8 = _final_loss(x, target, small[0]["final_g"])
    grads = []
    gw_in = (None, None)
    for l in reversed(range(DEPTH)):
        w, s, sv = weights[l], small[l], saved[l]
        qa, ka, va, oa, lse_a = sv["a"]
        qb, kb, vb, ob, lse_b = sv["b"]
        qc, kc, vc, oc, lse_c = sv["c"]
        doa, dob, doc, dzg, gw_out, dgate = _outproj_bwd(dx, sv["y"], sv["gate"], oa, ob, oc, sv["zg"], w["out"],
                                                          w["out_t"], nb, seq)
        dqa, dka, dva = _attn_bwd("A", qa, ka, va, oa, doa, lse_a, None, nb, seq)
        dqb, dkb, dvb, dbt = _attn_bwd("B", qb, kb, vb, ob, dob, lse_b, sv["bias"], nb, seq)
        dqc, dkc, dvc, dfr, dfq = _attn_bwd("C", qc, kc, vc, oc, doc, lse_c, sv["frow"], nb, seq)
        dg = _bias_reduce(jnp.pad(dbt, ((0, 2), (0, 0), (0, 0))))
        grb = jnp.pad(_bias_unline(dg), ((0, 0), (0, 384 - N_REL)))
        dfk = dfr.reshape(nb, 6, seq).transpose(0, 2, 1).reshape(nb * seq, 6)
        dzf, gfb = _forget_bwd(dfq, jnp.pad(dfk, ((0, 0), (0, LANE - 6))), sv["zf"], s["fb"], nb, seq)
        dza, gw_uq, gw_ukv, ggq, ggkv = _a_up_bwd(dqa, dka, dva, sv["za"], s["gq"], s["gkv"], w["uq_t"], w["ukv_t"],
                                                  cos, sina, sinb)
        dz = (dza, dzg, dqb, dkb, dvb, dqc, dkc, dvc, dzf)
        dx, dshift, dscale, gnorm = _inproj_bwd_dx(dz, dx, sv["x"], sv["shift"], sv["scale"], s["norm_g"], w["in"],
                                                   nb, seq)
        gw_in = (_inproj_bwd_dw(sv["h"], dz[0:2], "inproj_bwd_dw0", l, gw_in[0]),
                 _inproj_bwd_dw(sv["h"], dz[2:], "inproj_bwd_dw1", l, gw_in[1]))
        dmod = jnp.concatenate([dshift[:, 0], dscale[:, 0], dgate[:, 0]], axis=1)
        grads.append(dict(w_out=gw_out, w_uq=gw_uq, w_ukv=gw_ukv, dmod=dmod, norm_g=gnorm[0], gq=ggq[0],
                          gkv=ggkv[0], rb8=grb, fb=gfb[0]))
    grads.reverse()
    return loss8[0, 0], dx, grads, gfinal8[0], gw_in


def _layer_weights(w_in, w_out, w_uq, w_ukv):
    wi, wo, wq, wkv = _in_to_padded(w_in), _out_to_padded(w_out), _uq_to_padded(w_uq), _ukv_to_padded(w_ukv)
    return {"in": wi, "out": wo, "out_t": wo.T, "uq": wq, "uq_t": wq.T, "ukv": wkv, "ukv_t": wkv.T}


def _layer_small(norm_g, gq, gkv, rel_bias, forget_b, final_g):
    fb = jnp.pad(forget_b, (0, LANE - 5)).reshape(1, LANE)
    return dict(norm_g=norm_g.reshape(1, -1), gq=gq.reshape(1, -1), gkv=gkv.reshape(1, -1), g8=_bias_line(rel_bias), fb=fb,
                final_g=final_g.reshape(1, -1))


def _small_payload(per_layer, final_g, loss):
    def stack(key):
        return jnp.stack([p[key] for p in per_layer])

    def rows(a, rng):
        return _rows(a, rng[1] - rng[0])

    dmod = stack("dmod") if "dmod" in per_layer[0] else jnp.zeros((LANE,), F32)
    parts = [rows(dmod, PAY_DMOD), rows(stack("norm_g"), PAY_NORM), rows(stack("gq"), PAY_GQ),
             rows(stack("gkv"), PAY_GKV), rows(stack("rb8"), PAY_RB), rows(stack("fb"), PAY_FB),
             rows(final_g, PAY_FINAL), rows(loss, PAY_LOSS)]
    return jnp.concatenate(parts, axis=0)


def _payload_split(pay):
    def take(rng, shape):
        n = 1
        for d in shape:
            n *= d
        return pay[rng[0]:rng[1]].reshape(-1)[0:n].reshape(shape)

    norm_g = take(PAY_NORM, (DEPTH, D_MODEL))
    gq = take(PAY_GQ, (DEPTH, A_Q_RANK))
    gkv = take(PAY_GKV, (DEPTH, A_KV_RANK))
    rb = take(PAY_RB, (DEPTH, 8, 384))[:, 0:5, 0:N_REL]
    fb = take(PAY_FB, (DEPTH, LANE))[:, 0:5]
    final_g = take(PAY_FINAL, (D_MODEL,))
    return norm_g, gq, gkv, rb, fb, final_g


def kernel(x, c, positions, w_ada, b_ada, norm_g, w_in, a_q_norm_g, a_w_uq, a_kv_norm_g, a_w_ukv, b_rel_bias, c_forget_b, w_out, final_g, loss_target, m_w_ada, m_b_ada, m_norm_g, m_w_in, m_a_q_norm_g, m_a_w_uq, m_a_kv_norm_g, m_a_w_ukv, m_b_rel_bias, m_c_forget_b, m_w_out, m_final_g, v_w_ada, v_b_ada, v_norm_g, v_w_in, v_a_q_norm_g, v_a_w_uq, v_a_kv_norm_g, v_a_w_ukv, v_b_rel_bias, v_c_forget_b, v_w_out, v_final_g):
    nb, seq, _ = x.shape
    ix, iy, ic = lax.axis_index("x"), lax.axis_index("y"), lax.axis_index("c")
    chip = 2 * ix + iy
    me = 2 * chip + ic

    weight_plan = _gather_plan((256, 256, LR_ROWS))

    def gather_plan(x, y, c_):
        me_ = 4 * x + 2 * y + c_
        return weight_plan(x, y, c_) + [(3, False, 0, 8, 8 * me_, f, ()) for f in range(N_DEV)]

    full_in, full_out, full_lr, c_rows = _transfer(
        "gather_weights", [w_in.astype(BF16), w_out.astype(BF16), _pack_lowrank(a_w_uq, a_w_ukv).astype(BF16),
                           jnp.pad(c, ((0, 8 - nb), (0, 0)))[None]],
        [((DEPTH, D_MODEL, N_IN), BF16), ((DEPTH, D_MODEL, D_MODEL), BF16), ((1, 4 * LR_ROWS, PACK_COLS), BF16),
         ((1, 8 * N_DEV, D_MODEL), F32)], gather_plan)
    lowrank = [_unpack_lowrank(full_lr[0, LR_ROWS * j:LR_ROWS * (j + 1)]) for j in range(4)]
    full_uq = jnp.concatenate([s[0] for s in lowrank], axis=2)
    full_ukv = jnp.concatenate([s[1] for s in lowrank], axis=2)
    weights = [_layer_weights(full_in[l], full_out[l], full_uq[l], full_ukv[l]) for l in range(DEPTH)]
    small = [_layer_small(norm_g[l], a_q_norm_g[l], a_kv_norm_g[l], b_rel_bias[l], c_forget_b[l], final_g)
             for l in range(DEPTH)]

    c_all = c_rows.reshape(N_DEV, 8, D_MODEL)[:, 0:nb].reshape(N_DEV * nb, D_MODEL)
    cols = w_ada.shape[2]
    b_cols = lax.dynamic_slice_in_dim(b_ada, chip * cols, cols, axis=1)[:, None, :]
    mod_cols = _ada_fwd(c_all, w_ada, b_cols)
    mod_g = _exchange(mod_cols.reshape(1, DEPTH * N_DEV * nb, cols), ALL_FLIPS, "gather", "gather_mod")
    mod_all = jnp.concatenate([mod_g[2 * j].reshape(DEPTH, N_DEV * nb, cols) for j in range(4)], axis=2)
    mod = lax.dynamic_slice_in_dim(mod_all, me * nb, nb, axis=1)

    tables = _rope_tables(positions)
    loss_part, dx, grads, gfinal, gw_in = _forward_backward(
        x.reshape(nb * seq, D_MODEL), mod, tables, loss_target.reshape(nb * seq, D_MODEL), weights, small, nb, seq)

    pay = _small_payload(grads, gfinal, loss_part)
    pay_all = _exchange(pay[None], ALL_FLIPS, "gather", "gather_small")
    tot = _sum_blocks(pay_all, "sum_small")
    loss = tot[PAY_LOSS[0], 0]
    dmod_all = pay_all[:, PAY_DMOD[0]:PAY_DMOD[1]].reshape(N_DEV, -1)[:, 0:DEPTH * nb * 3 * D_MODEL]
    dmod_all = dmod_all.reshape(N_DEV, DEPTH, nb, 3 * D_MODEL).transpose(1, 0, 2, 3)
    dmod_all = dmod_all.reshape(DEPTH, N_DEV * nb, 3 * D_MODEL)
    my_cols = lax.dynamic_slice_in_dim(dmod_all, chip * cols, cols, axis=2)
    g_w_ada, g_b_ada = _ada_bwd(c_all, my_cols, dmod_all)
    g_b_ada = g_b_ada[:, 0]

    g_uq = jnp.stack([_uq_from_padded(g["w_uq"]) for g in grads])
    g_ukv = jnp.stack([_ukv_from_padded(g["w_ukv"]) for g in grads])
    g_lr = jnp.concatenate([_pack_lowrank(g_uq[:, :, 144 * j:144 * (j + 1)], g_ukv[:, :, 192 * j:192 * (j + 1)])
                            for j in range(4)], axis=1)
    partials = [gw_in[0], gw_in[1], jnp.stack([_out_from_padded(g["w_out"]) for g in grads]), g_lr]
    shapes = [(p.shape[0], p.shape[2]) for p in partials]
    halves = [r // 2 for r in SHARD_ROWS]
    core_s = jnp.reshape(ic, (1,)).astype(jnp.int32)
    place_s = jnp.stack([chip, ic]).astype(jnp.int32)
    from_pair = _transfer("pair_reduce", partials, [((nl, 4 * h, nc), F32) for (nl, nc), h in zip(shapes, halves)],
                          _pair_reduce_plan(SHARD_ROWS))
    chip_sums = [_sum_pair(p, r, core_s, rc, "sum_pair%d" % i)
                 for i, (p, r, rc) in enumerate(zip(partials, from_pair, SHARD_ROWS))]
    from_chips = _transfer("chip_scatter", chip_sums, [((nl, 3 * h, nc), BF16) for (nl, nc), h in zip(shapes, halves)],
                           _chip_scatter_plan(SHARD_ROWS))
    reduced = [_sum_chips(s, r, place_s, rc, "sum_chips%d" % i)
               for i, (s, r, rc) in enumerate(zip(chip_sums, from_chips, SHARD_ROWS))]
    g_in_a, g_in_b, g_out_sh, g_lr_sh = _transfer("pair_share", reduced, [(r.shape, F32) for r in reduced],
                                                  _pair_share_plan(SHARD_ROWS), in_place=True)
    g_uq_sh, g_ukv_sh = _unpack_lowrank(g_lr_sh[0])

    def cols_first(a):
        return jnp.transpose(a, (2, 0, 1))

    g_in_t = _in_cols_first(g_in_a, g_in_b)
    upd_in = tuple(jnp.transpose(a, (1, 2, 0)) for a in _adamw(cols_first(w_in), g_in_t, cols_first(m_w_in),
                                                               cols_first(v_w_in), "adamw_in"))
    gw = (jnp.transpose(g_in_t, (1, 2, 0)), g_out_sh, g_uq_sh, g_ukv_sh)
    upd = [upd_in, _adamw(w_out, gw[1], m_w_out, v_w_out, "adamw_out"),
           _adamw(a_w_uq, gw[2], m_a_w_uq, v_a_w_uq, "adamw_uq"), _adamw(a_w_ukv, gw[3], m_a_w_ukv, v_a_w_ukv, "adamw_ukv")]
    dw, mw, vw = (tuple(u[i] for u in upd) for i in range(3))
    d_ada, m_ada, v_ada = _adamw(w_ada, g_w_ada, m_w_ada, v_w_ada, "adamw_ada")
    d_b, m_b, v_b = (a.reshape(DEPTH, 3 * D_MODEL) for a in _adamw(
        _rows(b_ada, 48)[None], _rows(g_b_ada, 48)[None], _rows(m_b_ada, 48)[None], _rows(v_b_ada, 48)[None],
        "adamw_b_ada"))

    def small_rows(ng, gq, gkv, rb, fb, fg):
        per_layer = [dict(norm_g=ng[l], gq=gq[l], gkv=gkv[l], rb8=jnp.pad(rb[l], ((0, 3), (0, 384 - N_REL))),
                          fb=jnp.pad(fb[l], (0, LANE - 5))) for l in range(DEPTH)]
        return _small_payload(per_layer, fg, jnp.zeros((), F32))

    w_s = small_rows(norm_g, a_q_norm_g, a_kv_norm_g, b_rel_bias, c_forget_b, final_g)
    m_s = small_rows(m_norm_g, m_a_q_norm_g, m_a_kv_norm_g, m_b_rel_bias, m_c_forget_b, m_final_g)
    v_s = small_rows(v_norm_g, v_a_q_norm_g, v_a_kv_norm_g, v_b_rel_bias, v_c_forget_b, v_final_g)
    d_s, mo_s, vo_s = (a[0] for a in _adamw(w_s[None], tot[None], m_s[None], v_s[None], "adamw_small"))
    gs = _payload_split(tot)
    ds = _payload_split(d_s)
    ms = _payload_split(mo_s)
    vs = _payload_split(vo_s)

    def ordered(ada, b, sm, big):
        ng, gq, gkv, rb, fb, fg = sm
        b_in, b_out, b_uq, b_ukv = big
        return (ada, b, ng, b_in, gq, b_uq, gkv, b_ukv, rb, fb, b_out, fg)

    return (loss, dx.reshape(nb, seq, D_MODEL), *ordered(g_w_ada, g_b_ada, gs, gw), *ordered(d_ada, d_b, ds, dw),
            *ordered(m_ada, m_b, ms, mw), *ordered(v_ada, v_b, vs, vw))
```

```python
import functools

import jax
import jax.numpy as jnp
from jax import lax
from jax.experimental import pallas as pl
from jax.experimental.pallas import tpu as pltpu

F32 = jnp.float32
BF16 = jnp.bfloat16

D_MODEL = 1024
DEPTH = 2
EPS = 1e-6
NEG = -1e30
LOG2E = 1.4426950408889634
ROPE_THETA = 10000.0
A_ROPE = 32
A_Q_RANK = 384
A_KV_RANK = 256
REL_CLIP = 128
N_REL = 2 * REL_CLIP + 1
N_IN = 3621

ADAM_LR = 0.001
ADAM_B1 = 0.9
ADAM_B2 = 0.999
ADAM_EPS = 1e-08
ADAM_WD = 0.01
ADAM_STEP = 10

LANE = 128
VMEM_LIMIT = 56 * 1024 * 1024

NP_IN = 4352
Z_A = (0, 768)
Z_G = (768, 1920)
Z_QKV = tuple((1920 + 384 * i, 1920 + 384 * (i + 1)) for i in range(6))
Z_F = (4224, 4352)
IN_PIECES = ((0, 672, 96), (672, 1056, 0), (2016, 2336, 64), (3301, 3621, 64), (1056, 1376, 64), (1376, 1696, 64),
             (1696, 2016, 64), (2336, 2656, 64), (2656, 2976, 64), (2976, 3296, 64), (3296, 3301, 123))
D_CAT = 1152

TM = 512
T_CAUSAL = 256
T_BAND = 128
BAND_TILES = 5
N_DEV = 8

PAY_DMOD = (0, 96)
PAY_NORM = (96, 112)
PAY_GQ = (112, 120)
PAY_GKV = (120, 128)
PAY_RB = (128, 176)
PAY_FB = (176, 184)
PAY_FINAL = (184, 192)
PAY_LOSS = (192, 200)
PAY_ROWS = 200

PACK_COLS = 1024
PACK_ROWS = 2560
HALF_ROWS = PACK_ROWS // 2


def _params(sem=None):
    return pltpu.CompilerParams(dimension_semantics=sem, vmem_limit_bytes=VMEM_LIMIT)


def _lane_iota(shape):
    return lax.broadcasted_iota(jnp.int32, shape, len(shape) - 1)


def _silu(u):
    return u * jax.nn.sigmoid(u)


def _dsilu(u):
    s = jax.nn.sigmoid(u)
    return s * (1.0 + u * (1.0 - s))


def _rms(x, g):
    r = lax.rsqrt(jnp.mean(x * x, axis=-1, keepdims=True) + EPS)
    xh = x * r
    return xh * g, xh, r


def _rms_bwd(dy, xh, r, g):
    dxh = dy * g
    return r * (dxh - xh * jnp.mean(dxh * xh, axis=-1, keepdims=True))


def _rope(x, cos, sina, sinb):
    return x * cos + pltpu.roll(x, 16, 1) * sinb + pltpu.roll(x, LANE - 16, 1) * sina


def _rope_t(dy, cos, sina, sinb):
    return dy * cos + pltpu.roll(dy * sinb, LANE - 16, 1) + pltpu.roll(dy * sina, 16, 1)


def _split3(x):
    hi = x.astype(BF16)
    r1 = x - hi.astype(F32)
    mid = r1.astype(BF16)
    lo = (r1 - mid.astype(F32)).astype(BF16)
    return hi, mid, lo


def _dot(a, b):
    return jnp.dot(a, b, preferred_element_type=F32)


def _dot_nt(a, b):
    return lax.dot_general(a, b, (((1,), (1,)), ((), ())), preferred_element_type=F32)


def _dot_tn(a, b):
    return lax.dot_general(a, b, (((0,), (0,)), ((), ())), preferred_element_type=F32)


def _row_spec(cols):
    return pl.BlockSpec((TM, cols), lambda i: (i, 0))


def _full_spec(shape):
    return pl.BlockSpec(shape, lambda i: (0,) * len(shape))


def _ex_spec(tiles_per_ex):
    return pl.BlockSpec((1, 1, D_MODEL), lambda i: (i // tiles_per_ex, 0, 0))


def _ln_inproj(x, shift, scale, g, w_in_p, seq):
    t = x.shape[0]

    def body(x_ref, sh_ref, sc_ref, g_ref, w_ref, h_ref, za_ref, zg_ref, q0, q1, q2, q3, q4, q5, zf_ref):
        n, _, _ = _rms(x_ref[...], g_ref[...])
        h = (n * (1.0 + sc_ref[0]) + sh_ref[0]).astype(BF16)
        h_ref[...] = h
        za_ref[...] = _dot(h, w_ref[:, Z_A[0]:Z_A[1]])
        zg_ref[...] = _dot(h, w_ref[:, Z_G[0]:Z_G[1]])
        for ref, (c0, c1) in zip((q0, q1, q2, q3, q4, q5), Z_QKV):
            ref[...] = _dot(h, w_ref[:, c0:c1]).astype(BF16)
        zf_ref[...] = _dot(h, w_ref[:, Z_F[0]:Z_F[1]])

    tpe = seq // TM
    shapes = [jax.ShapeDtypeStruct((t, D_MODEL), BF16), jax.ShapeDtypeStruct((t, 768), F32),
              jax.ShapeDtypeStruct((t, D_CAT), F32)]
    shapes += [jax.ShapeDtypeStruct((t, 384), BF16)] * 6 + [jax.ShapeDtypeStruct((t, LANE), F32)]
    return pl.pallas_call(
        body, name="ln_inproj", grid=(t // TM,),
        in_specs=[_row_spec(D_MODEL), _ex_spec(tpe), _ex_spec(tpe), _full_spec((1, D_MODEL)),
                  _full_spec((D_MODEL, NP_IN))],
        out_specs=[_row_spec(D_MODEL), _row_spec(768), _row_spec(D_CAT)] + [_row_spec(384)] * 6 + [_row_spec(LANE)],
        out_shape=shapes, compiler_params=_params(("parallel",)),
    )(x, shift, scale, g, w_in_p)


def _a_up(za, gq, gkv, w_uq_p, w_ukv_p, cos, sina, sinb):
    t = za.shape[0]

    def body(za_ref, gq_ref, gkv_ref, wq_ref, wkv_ref, cos_ref, sa_ref, sb_ref, q_ref, k_ref, v_ref):
        cos_t, sa, sb = cos_ref[...], sa_ref[...], sb_ref[...]
        cqn, _, _ = _rms(za_ref[:, 0:384], gq_ref[...])
        q = _dot(cqn.astype(BF16), wq_ref[...])
        ckvn, _, _ = _rms(za_ref[:, 384:640], gkv_ref[...])
        kv = _dot(ckvn.astype(BF16), wkv_ref[...])
        kpe = za_ref[:, 640:768]
        kpe = _rope(kpe + pltpu.roll(kpe, 32, 1), cos_t, sa, sb).astype(BF16)
        for p in range(3):
            q_ref[:, 256 * p:256 * p + 128] = q[:, 256 * p:256 * p + 128].astype(BF16)
            q_ref[:, 256 * p + 128:256 * p + 256] = _rope(q[:, 256 * p + 128:256 * p + 256], cos_t, sa, sb).astype(BF16)
            k_ref[:, 256 * p:256 * p + 128] = kv[:, 128 * p:128 * p + 128].astype(BF16)
            k_ref[:, 256 * p + 128:256 * p + 256] = kpe
        v_ref[...] = kv[:, 384:768].astype(BF16)

    return pl.pallas_call(
        body, name="a_up", grid=(t // TM,),
        in_specs=[_row_spec(768), _full_spec((1, 384)), _full_spec((1, 256)), _full_spec((384, 768)),
                  _full_spec((256, 768)), _row_spec(LANE), _row_spec(LANE), _row_spec(LANE)],
        out_specs=[_row_spec(768), _row_spec(768), _row_spec(384)],
        out_shape=[jax.ShapeDtypeStruct((t, 768), BF16), jax.ShapeDtypeStruct((t, 768), BF16),
                   jax.ShapeDtypeStruct((t, 384), BF16)],
        compiler_params=_params(("parallel",)),
    )(za, gq, gkv, w_uq_p, w_ukv_p, cos, sina, sinb)


def _tri(n, upper):
    r = lax.broadcasted_iota(jnp.int32, (n, n), 0)
    c = lax.broadcasted_iota(jnp.int32, (n, n), 1)
    return jnp.where((c >= r) if upper else (c <= r), 1.0, 0.0).astype(BF16)


def _forget_fwd(zf, fb, nb, seq):
    blk = 256

    def body(zf_ref, fb_ref, f_ref):
        tri = _tri(blk, False)
        live = _lane_iota((blk, LANE)) < 5
        carry = jnp.zeros((1, LANE), F32)
        for i in range(seq // blk):
            u = zf_ref[i * blk:(i + 1) * blk, :] + fb_ref[...]
            lf = jnp.where(live, jnp.minimum(u, 0.0) - jnp.log(1.0 + jnp.exp(-jnp.abs(u))), 0.0)
            hi, mid, lo = _split3(lf)
            f_ref[i * blk:(i + 1) * blk, :] = (_dot(tri, hi) + _dot(tri, mid) + _dot(tri, lo) + carry) * LOG2E
            carry = carry + jnp.sum(lf, axis=0, keepdims=True)

    return pl.pallas_call(
        body, name="forget_fwd", grid=(nb,),
        in_specs=[pl.BlockSpec((seq, LANE), lambda b: (b, 0)), pl.BlockSpec((1, LANE), lambda b: (0, 0))],
        out_specs=pl.BlockSpec((seq, LANE), lambda b: (b, 0)),
        out_shape=jax.ShapeDtypeStruct((nb * seq, LANE), F32), compiler_params=_params(("parallel",)),
    )(zf, fb)


def _forget_bwd(dfq, dfk, zf, fb, nb, seq):
    blk = 256

    def body(dfq_ref, dfk_ref, zf_ref, fb_ref, dz_ref, gb_ref):
        @pl.when(pl.program_id(0) == 0)
        def _():
            gb_ref[...] = jnp.zeros_like(gb_ref)

        tri = _tri(blk, True)
        lane = _lane_iota((blk, LANE))
        wide = _lane_iota((blk, 384))
        live = lane < 5
        carry = jnp.zeros((1, LANE), F32)
        gsum = jnp.zeros((1, LANE), F32)
        for i in reversed(range(seq // blk)):
            d = dfk_ref[i * blk:(i + 1) * blk, :]
            dq = dfq_ref[i * blk:(i + 1) * blk, :]
            for hd in range(5):
                col = jnp.sum(jnp.where(wide == 64 * hd, dq, 0.0), axis=-1, keepdims=True)
                d = d + jnp.where(lane == hd, col, 0.0)
            hi, mid, lo = _split3(d)
            dlf = _dot(tri, hi) + _dot(tri, mid) + _dot(tri, lo) + carry
            carry = carry + jnp.sum(d, axis=0, keepdims=True)
            u = zf_ref[i * blk:(i + 1) * blk, :] + fb_ref[...]
            du = jnp.where(live, dlf * jax.nn.sigmoid(-u), 0.0)
            dz_ref[i * blk:(i + 1) * blk, :] = du.astype(BF16)
            gsum = gsum + jnp.sum(du, axis=0, keepdims=True)
        gb_ref[...] += jnp.broadcast_to(gsum, gb_ref.shape)

    return pl.pallas_call(
        body, name="forget_bwd", grid=(nb,),
        in_specs=[pl.BlockSpec((seq, 384), lambda b: (b, 0)), pl.BlockSpec((seq, LANE), lambda b: (b, 0)),
                  pl.BlockSpec((seq, LANE), lambda b: (b, 0)), pl.BlockSpec((1, LANE), lambda b: (0, 0))],
        out_specs=[pl.BlockSpec((seq, LANE), lambda b: (b, 0)), pl.BlockSpec((8, LANE), lambda b: (0, 0))],
        out_shape=[jax.ShapeDtypeStruct((nb * seq, LANE), BF16), jax.ShapeDtypeStruct((8, LANE), F32)],
        compiler_params=_params(("arbitrary",)),
    )(dfq, dfk, zf, fb)


def _gate_outproj(x, gate, oa, ob, oc, zg, w_out_p, seq):
    t = x.shape[0]

    def body(x_ref, gate_ref, oa_ref, ob_ref, oc_ref, zg_ref, w_ref, y_ref, xn_ref):
        y = jnp.zeros((TM, D_MODEL), F32)
        for i, o_ref in enumerate((oa_ref, ob_ref, oc_ref)):
            cat = (o_ref[...] * _silu(zg_ref[:, 384 * i:384 * (i + 1)])).astype(BF16)
            y = y + _dot(cat, w_ref[384 * i:384 * (i + 1), :])
        y_ref[...] = y
        xn_ref[...] = x_ref[...] + gate_ref[0] * y

    return pl.pallas_call(
        body, name="gate_outproj", grid=(t // TM,),
        in_specs=[_row_spec(D_MODEL), _ex_spec(seq // TM), _row_spec(384), _row_spec(384), _row_spec(384),
                  _row_spec(D_CAT), _full_spec((D_CAT, D_MODEL))],
        out_specs=[_row_spec(D_MODEL), _row_spec(D_MODEL)],
        out_shape=[jax.ShapeDtypeStruct((t, D_MODEL), F32)] * 2, compiler_params=_params(("parallel",)),
    )(x, gate, oa, ob, oc, zg, w_out_p)


def _final_loss(x, target, g):
    t = x.shape[0]

    def body(x_ref, t_ref, g_ref, dx_ref, loss_ref, gg_ref):
        @pl.when(pl.program_id(0) == 0)
        def _():
            loss_ref[...] = jnp.zeros_like(loss_ref)
            gg_ref[...] = jnp.zeros_like(gg_ref)

        gv = g_ref[...]
        out, xh, r = _rms(x_ref[...], gv)
        err = out - t_ref[...]
        loss_ref[...] += 0.5 * jnp.sum(jnp.mean(err * err, axis=-1, keepdims=True), axis=0, keepdims=True)
        dout = err / D_MODEL
        gg_ref[...] += jnp.broadcast_to(jnp.sum(dout * xh, axis=0, keepdims=True), gg_ref.shape)
        dx_ref[...] = _rms_bwd(dout, xh, r, gv)

    return pl.pallas_call(
        body, name="final_loss", grid=(t // TM,),
        in_specs=[_row_spec(D_MODEL), _row_spec(D_MODEL), _full_spec((1, D_MODEL))],
        out_specs=[_row_spec(D_MODEL), _full_spec((8, LANE)), _full_spec((8, D_MODEL))],
        out_shape=[jax.ShapeDtypeStruct((t, D_MODEL), F32), jax.ShapeDtypeStruct((8, LANE), F32),
                   jax.ShapeDtypeStruct((8, D_MODEL), F32)],
        compiler_params=_params(("arbitrary",)),
    )(x, target, g)


def _head_masks(kind, rows, dq, h):
    lq = _lane_iota((rows, dq))
    lv = _lane_iota((rows, LANE))
    mq = (lq >= 64 * h) & (lq < 64 * h + 64)
    if kind == "A":
        mq = mq | ((lq >= 128 + 32 * h) & (lq < 160 + 32 * h))
    return mq, (lv >= 64 * h) & (lv < 64 * h + 64)


def _tile_mask(kind, tile, m=None):
    row = lax.broadcasted_iota(jnp.int32, (tile, tile), 0)
    col = lax.broadcasted_iota(jnp.int32, (tile, tile), 1)
    if kind == "A":
        return (col >> 6) <= (row >> 6)
    if kind == "C":
        return col <= row
    first = (m == 0) & (row >= 64) & (col < 64)
    last = (m == BAND_TILES - 1) & (row < 64) & (col >= 64)
    return jnp.logical_not(first | last)


def _attn_scale(kind):
    return 96.0 ** -0.5 if kind == "A" else 0.125


def _attn_fwd(kind, q, k, v, aux, nb, seq):
    dq = q.shape[1] // 3
    tile = T_BAND if kind == "B" else T_CAUSAL
    nq = seq // tile
    scale = _attn_scale(kind)

    def body(*refs):
        if kind == "A":
            q_ref, k_ref, v_ref, o_ref, lse_ref = refs
            aux_ref = None
        else:
            q_ref, k_ref, v_ref, aux_ref, o_ref, lse_ref = refs
        qi = pl.program_id(2)
        q2 = q_ref[...]
        res = []
        for h in range(2):
            mq, _ = _head_masks(kind, tile, dq, h)
            qh = jnp.where(mq, q2, jnp.zeros_like(q2))

            def step(kj, carry, m=None, diag=False):
                mx, l, acc = carry
                ks = pl.ds(pl.multiple_of(kj * tile, tile), tile)
                kt = k_ref[ks, :]
                vt = v_ref[ks, :]
                s = _dot_nt(qh, kt) * scale
                if kind == "B":
                    s = jnp.where(_tile_mask("B", tile, m), s + aux_ref[h, m], NEG)
                if kind == "C":
                    s = s - aux_ref[0, h, pl.ds(kj, 1), :]
                if diag:
                    s = jnp.where(_tile_mask(kind, tile), s, NEG)
                mn = jnp.maximum(mx, jnp.max(s, axis=-1, keepdims=True))
                alpha = jnp.exp(mx - mn)
                p = jnp.exp(s - mn)
                l = alpha * l + jnp.sum(p, axis=-1, keepdims=True)
                acc = alpha * acc + _dot(p.astype(BF16), vt)
                return mn, l, acc

            init = (jnp.full((tile, 1), NEG, F32), jnp.zeros((tile, 1), F32), jnp.zeros((tile, LANE), F32))
            if kind == "B":
                m0 = jnp.maximum(BAND_TILES - 1 - qi, 0)
                mx, l, acc = lax.fori_loop(m0, BAND_TILES, lambda m, c: step(qi - (BAND_TILES - 1) + m, c, m=m), init)
            else:
                carry = lax.fori_loop(0, qi, lambda kj, c: step(kj, c), init)
                mx, l, acc = step(qi, carry, diag=True)
            res.append((acc / l, mx + jnp.log(l)))
        first = _lane_iota((tile, LANE)) < 64
        o_ref[...] = jnp.where(first, res[0][0], res[1][0])
        lse_ref[...] = jnp.where(first, res[0][1], res[1][1])

    in_specs = [pl.BlockSpec((tile, dq), lambda b, p, i: (b * nq + i, p)),
                pl.BlockSpec((seq, dq), lambda b, p, i: (b, p)),
                pl.BlockSpec((seq, LANE), lambda b, p, i: (b, p))]
    args = [q, k, v]
    if kind == "B":
        in_specs.append(pl.BlockSpec((2, BAND_TILES, tile, tile), lambda b, p, i: (p, 0, 0, 0)))
        args.append(aux)
    if kind == "C":
        in_specs.append(pl.BlockSpec((1, 2, nq, tile), lambda b, p, i: (b, p, 0, 0)))
        args.append(aux)
    out_spec = pl.BlockSpec((tile, LANE), lambda b, p, i: (b * nq + i, p))
    return pl.pallas_call(
        body, name="attn_fwd_" + kind, grid=(nb, 3, nq), in_specs=in_specs, out_specs=[out_spec, out_spec],
        out_shape=[jax.ShapeDtypeStruct((nb * seq, 384), F32)] * 2,
        compiler_params=_params(("parallel", "parallel", "parallel")),
    )(*args)


def _attn_bwd(kind, q, k, v, o, do, lse, aux, nb, seq):
    dq = q.shape[1] // 3
    tile = T_BAND if kind == "B" else T_CAUSAL
    nq = seq // tile
    scale = _attn_scale(kind)
    dqk_dtype = F32 if kind == "A" else BF16

    def body(*refs):
        dfr_ref = dfq_ref = dbt_ref = aux_ref = None
        if kind == "A":
            q_ref, k_ref, v_ref, o_ref, do_ref, lse_ref, dq_ref, dk_ref, dv_ref, dk_acc, dv_acc = refs
        elif kind == "B":
            q_ref, k_ref, v_ref, o_ref, do_ref, lse_ref, aux_ref, dq_ref, dk_ref, dv_ref, dbt_ref, dk_acc, dv_acc = refs
        else:
            (q_ref, k_ref, v_ref, o_ref, do_ref, lse_ref, aux_ref, dq_ref, dk_ref, dv_ref, dfr_ref, dfq_ref,
             dk_acc, dv_acc) = refs
        dk_acc[...] = jnp.zeros_like(dk_acc)
        dv_acc[...] = jnp.zeros_like(dv_acc)
        if kind == "C":
            dfr_ref[...] = jnp.zeros_like(dfr_ref)
        if kind == "B":
            @pl.when(pl.program_id(1) == 0)
            def _():
                dbt_ref[...] = jnp.zeros_like(dbt_ref)

        def q_step(qi, _):
            qs = pl.ds(pl.multiple_of(qi * tile, tile), tile)
            q2 = q_ref[qs, :]
            do2 = do_ref[qs, :]
            o2 = o_ref[qs, :]
            lse2 = lse_ref[qs, :]
            dq_tot = jnp.zeros((tile, dq), F32)
            row_sums = []
            for h in range(2):
                mq, mv = _head_masks(kind, tile, dq, h)
                qh = jnp.where(mq, q2, jnp.zeros_like(q2))
                doh = jnp.where(mv, do2, 0.0)
                dob = doh.astype(BF16)
                delta = jnp.sum(doh * o2, axis=-1, keepdims=True)
                lseh = jnp.max(jnp.where(mv, lse2, NEG), axis=-1, keepdims=True)

                def step(kj, carry, m=None, diag=False):
                    dq_t, rs = carry
                    ks = pl.ds(pl.multiple_of(kj * tile, tile), tile)
                    kt = k_ref[ks, :]
                    vt = v_ref[ks, :]
                    s = _dot_nt(qh, kt) * scale
                    if kind == "B":
                        s = jnp.where(_tile_mask("B", tile, m), s + aux_ref[h, m], NEG)
                    if kind == "C":
                        s = s - aux_ref[0, h, pl.ds(kj, 1), :]
                    if diag:
                        s = jnp.where(_tile_mask(kind, tile), s, NEG)
                    p = jnp.exp(s - lseh)
                    ds = p * (_dot_nt(dob, vt) - delta)
                    if kind == "B":
                        dbt_ref[h, m] += ds
                    if kind == "C":
                        dfr_ref[0, h, pl.ds(kj, 1), :] -= jnp.sum(ds, axis=0, keepdims=True)
                        rs = rs + jnp.sum(ds, axis=-1, keepdims=True)
                    dss = (ds * scale).astype(BF16)
                    dv_acc[ks, :] += _dot_tn(p.astype(BF16), dob)
                    dk_acc[ks, :] += _dot_tn(dss, qh)
                    return dq_t + jnp.where(mq, _dot(dss, kt), 0.0), rs

                zero = (jnp.zeros((tile, dq), F32), jnp.zeros((tile, 1), F32))
                if kind == "B":
                    m0 = jnp.maximum(BAND_TILES - 1 - qi, 0)
                    dq_h, rs_h = lax.fori_loop(m0, BAND_TILES, lambda m, c: step(qi - (BAND_TILES - 1) + m, c, m=m), zero)
                else:
                    carry = lax.fori_loop(0, qi, lambda kj, c: step(kj, c), zero)
                    dq_h, rs_h = step(qi, carry, diag=True)
                dq_tot = dq_tot + dq_h
                row_sums.append(rs_h)
            dq_ref[qs, :] = dq_tot.astype(dqk_dtype)
            if kind == "C":
                dfq_ref[qs, :] = jnp.where(_lane_iota((tile, LANE)) < 64, row_sums[0], row_sums[1])
            return 0

        lax.fori_loop(0, nq, q_step, 0)
        dk_ref[...] = dk_acc[...].astype(dqk_dtype)
        dv_ref[...] = dv_acc[...].astype(BF16)

    def seq_spec(cols):
        return pl.BlockSpec((seq, cols), lambda p, b: (b, p))

    in_specs = [seq_spec(dq), seq_spec(dq), seq_spec(LANE), seq_spec(LANE), seq_spec(LANE), seq_spec(LANE)]
    args = [q, k, v, o, do, lse]
    out_specs = [seq_spec(dq), seq_spec(dq), seq_spec(LANE)]
    out_shape = [jax.ShapeDtypeStruct((nb * seq, 3 * dq), dqk_dtype)] * 2 + [jax.ShapeDtypeStruct((nb * seq, 384), BF16)]
    if kind == "B":
        spec = pl.BlockSpec((2, BAND_TILES, tile, tile), lambda p, b: (p, 0, 0, 0))
        in_specs.append(spec)
        args.append(aux)
        out_specs.append(spec)
        out_shape.append(jax.ShapeDtypeStruct((6, BAND_TILES, tile, tile), F32))
    if kind == "C":
        spec = pl.BlockSpec((1, 2, nq, tile), lambda p, b: (b, p, 0, 0))
        in_specs.append(spec)
        args.append(aux)
        out_specs += [spec, seq_spec(LANE)]
        out_shape += [jax.ShapeDtypeStruct((nb, 6, nq, tile), F32), jax.ShapeDtypeStruct((nb * seq, 384), F32)]
    return pl.pallas_call(
        body, name="attn_bwd_" + kind, grid=(3, nb), in_specs=in_specs, out_specs=out_specs, out_shape=out_shape,
        scratch_shapes=[pltpu.VMEM((seq, dq), F32), pltpu.VMEM((seq, LANE), F32)],
        compiler_params=_params(("arbitrary", "arbitrary")),
    )(*args)


BAND_W = BAND_TILES * T_BAND


def _segments(kind, qi, tile):
    r0 = qi * tile
    if kind == "B":
        lo = max(qi - (BAND_TILES - 1), 0) * tile
        return [(lo, r0 + tile, False, lo - (qi - (BAND_TILES - 1)) * tile)]
    return ([(0, r0, False, 0)] if qi else []) + [(r0, r0 + tile, True, 0)]


def _scores(kind, qh, k_ref, aux_ref, h, seg, tile, scale):
    a, b, diag, c0 = seg
    s = _dot_nt(qh, k_ref[a:b, :]) * (scale * LOG2E)
    if kind == "B":
        return s + aux_ref[h, :, c0:BAND_W]
    if kind == "C":
        s = s - aux_ref[0, h, :, a:b]
    if diag:
        s = jnp.where(_tile_mask(kind, tile), s, NEG)
    return s


def _second_head(kind, pair, fn):
    del kind, pair
    fn()


FWD_AHEAD = 2
BWD_AHEAD = 2


def _run_ahead(units, first, second, depth):
    queue = [first(*u) for u in units[:depth]]
    for i, (_, h) in enumerate(units):
        if i + depth < len(units):
            queue.append(first(*units[i + depth]))
        second(h, *queue.pop(0))


def _attn_fwd(kind, q, k, v, aux, nb, seq):
    dq = q.shape[1] // 3
    tile = T_BAND if kind == "B" else T_CAUSAL
    nq = seq // tile
    scale = _attn_scale(kind)

    def body(*refs):
        if kind == "A":
            q_ref, k_ref, v_ref, o_ref, lse_ref = refs
            aux_ref = None
        else:
            q_ref, k_ref, v_ref, aux_ref, o_ref, lse_ref = refs
        def logits(qi, h):
            rows = slice(qi * tile, (qi + 1) * tile)
            q2 = q_ref[rows, :]
            mq, _ = _head_masks(kind, tile, dq, h)
            qh = jnp.where(mq, q2, jnp.zeros_like(q2))
            segs = _segments(kind, qi, tile)
            return rows, segs, [_scores(kind, qh, k_ref, aux_ref, h, seg, tile, scale) for seg in segs]

        def finish(h, rows, segs, ss):
            _, mv = _head_masks(kind, tile, dq, h)
            mx = functools.reduce(jnp.maximum, [jnp.max(s, axis=-1, keepdims=True) for s in ss])
            ps = [jnp.exp2(s - mx) for s in ss]
            l = functools.reduce(jnp.add, [jnp.sum(p, axis=-1, keepdims=True) for p in ps])
            acc = functools.reduce(jnp.add, [_dot(p.astype(BF16), v_ref[seg[0]:seg[1], :]) for p, seg in zip(ps, segs)])
            o_h = jnp.where(mv, acc / l, 0.0)
            lse_h = jnp.where(mv, mx + jnp.log(l) * LOG2E, 0.0)
            if h == 0:
                o_ref[rows, :] = o_h
                lse_ref[rows, :] = lse_h
            else:
                o_ref[rows, :] += o_h
                lse_ref[rows, :] += lse_h

        _run_ahead([(qi, h) for qi in range(nq) for h in range(2)], logits, finish, FWD_AHEAD)

    def seq_spec(cols):
        return pl.BlockSpec((seq, cols), lambda b, p: (b, p))

    in_specs = [seq_spec(dq), seq_spec(dq), seq_spec(LANE)]
    args = [q, k, v]
    if kind == "B":
        in_specs.append(pl.BlockSpec((2, tile, BAND_W), lambda b, p: (p, 0, 0)))
        args.append(aux)
    if kind == "C":
        in_specs.append(pl.BlockSpec((1, 2, 1, seq), lambda b, p: (b, p, 0, 0)))
        args.append(aux)
    return pl.pallas_call(
        body, name="attn_fwd_" + kind, grid=(nb, 3), in_specs=in_specs, out_specs=[seq_spec(LANE), seq_spec(LANE)],
        out_shape=[jax.ShapeDtypeStruct((nb * seq, 384), F32)] * 2, compiler_params=_params(("parallel", "parallel")),
    )(*args)


def _attn_bwd(kind, q, k, v, o, do, lse, aux, nb, seq):
    dq = q.shape[1] // 3
    tile = T_BAND if kind == "B" else T_CAUSAL
    nq = seq // tile
    scale = _attn_scale(kind)
    dqk_dtype = F32 if kind == "A" else BF16

    def body(*refs):
        dfr_ref = dfq_ref = dbt_ref = aux_ref = None
        if kind == "A":
            q_ref, k_ref, v_ref, o_ref, do_ref, lse_ref, dq_ref, dk_ref, dv_ref, dkt_acc, dvt_acc = refs
        elif kind == "B":
            q_ref, k_ref, v_ref, o_ref, do_ref, lse_ref, aux_ref, dq_ref, dk_ref, dv_ref, dbt_ref, dkt_acc, dvt_acc = refs
        else:
            (q_ref, k_ref, v_ref, o_ref, do_ref, lse_ref, aux_ref, dq_ref, dk_ref, dv_ref, dfr_ref, dfq_ref,
             dkt_acc, dvt_acc) = refs
        dkt_acc[...] = jnp.zeros_like(dkt_acc)
        dvt_acc[...] = jnp.zeros_like(dvt_acc)
        if kind == "C":
            dfr_ref[...] = jnp.zeros_like(dfr_ref)
        if kind == "B":
            @pl.when(pl.program_id(1) == 0)
            def _():
                dbt_ref[...] = jnp.zeros_like(dbt_ref)

        def products(qi, h):
            rows = slice(qi * tile, (qi + 1) * tile)
            q2 = q_ref[rows, :]
            mq, mv = _head_masks(kind, tile, dq, h)
            qh = jnp.where(mq, q2, jnp.zeros_like(q2))
            doh = jnp.where(mv, do_ref[rows, :], 0.0)
            dob = doh.astype(BF16)
            segs = _segments(kind, qi, tile)
            ts = [_scores(kind, qh, k_ref, aux_ref, h, seg, tile, scale) for seg in segs]
            dps = [_dot_nt(dob, v_ref[seg[0]:seg[1], :]) for seg in segs]
            return rows, segs, qh, doh, ts, dps

        def finish(h, rows, segs, qh, doh, ts, dps):
            mq, mv = _head_masks(kind, tile, dq, h)
            qht = qh.astype(F32).T.astype(BF16)
            dobt = doh.T.astype(BF16)
            head_rows = [(64 * h, 64)] + ([(128 + 32 * h, 32)] if kind == "A" else [])
            delta = jnp.sum(doh * o_ref[rows, :], axis=-1, keepdims=True)
            lseh = jnp.max(jnp.where(mv, lse_ref[rows, :], NEG), axis=-1, keepdims=True)
            rs = jnp.zeros((tile, 1), F32)
            dq_h = jnp.zeros((tile, dq), F32)
            for (a, b, _, c0), t, dp in zip(segs, ts, dps):
                p = jnp.exp2(t - lseh)
                ds = p * (dp - delta)
                if kind == "B":
                    dbt_ref[h, :, c0:BAND_W] += ds
                if kind == "C":
                    dfr_ref[0, h, :, a:b] -= jnp.sum(ds, axis=0, keepdims=True)
                    rs = rs + jnp.sum(ds, axis=-1, keepdims=True)
                dss = (ds * scale).astype(BF16)
                dvt_acc[64 * h:64 * h + 64, a:b] += _dot(dobt[64 * h:64 * h + 64, :], p.astype(BF16))
                for r0, n in head_rows:
                    dkt_acc[r0:r0 + n, a:b] += _dot(qht[r0:r0 + n, :], dss)
                dq_h = dq_h + _dot(dss, k_ref[a:b, :])
            dq_h = jnp.where(mq, dq_h, 0.0).astype(dqk_dtype)
            if h == 0:
                dq_ref[rows, :] = dq_h
            else:
                dq_ref[rows, :] += dq_h
            if kind == "C":
                if h == 0:
                    dfq_ref[rows, :] = jnp.where(mv, rs, 0.0)
                else:
                    dfq_ref[rows, :] += jnp.where(mv, rs, 0.0)

        _run_ahead([(qi, h) for qi in range(nq) for h in range(2)], products, finish, BWD_AHEAD)
        for j in range(seq // 256):
            cols = slice(256 * j, 256 * (j + 1))
            dk_ref[cols, :] = dkt_acc[:, cols].T.astype(dqk_dtype)
            dv_ref[cols, :] = dvt_acc[:, cols].T.astype(BF16)

    def seq_spec(cols):
        return pl.BlockSpec((seq, cols), lambda p, b: (b, p))

    in_specs = [seq_spec(dq), seq_spec(dq), seq_spec(LANE), seq_spec(LANE), seq_spec(LANE), seq_spec(LANE)]
    args = [q, k, v, o, do, lse]
    out_specs = [seq_spec(dq), seq_spec(dq), seq_spec(LANE)]
    out_shape = [jax.ShapeDtypeStruct((nb * seq, 3 * dq), dqk_dtype)] * 2 + [jax.ShapeDtypeStruct((nb * seq, 384), BF16)]
    if kind == "B":
        spec = pl.BlockSpec((2, tile, BAND_W), lambda p, b: (p, 0, 0))
        in_specs.append(spec)
        args.append(aux)
        out_specs.append(spec)
        out_shape.append(jax.ShapeDtypeStruct((6, tile, BAND_W), F32))
    if kind == "C":
        spec = pl.BlockSpec((1, 2, 1, seq), lambda p, b: (b, p, 0, 0))
        in_specs.append(spec)
        args.append(aux)
        out_specs += [spec, seq_spec(LANE)]
        out_shape += [jax.ShapeDtypeStruct((nb, 6, 1, seq), F32), jax.ShapeDtypeStruct((nb * seq, 384), F32)]
    return pl.pallas_call(
        body, name="attn_bwd_" + kind, grid=(3, nb), in_specs=in_specs, out_specs=out_specs, out_shape=out_shape,
        scratch_shapes=[pltpu.VMEM((dq, seq), F32), pltpu.VMEM((LANE, seq), F32)],
        compiler_params=_params(("arbitrary", "arbitrary")),
    )(*args)


BIAS_FLAT = T_BAND * BAND_W
BIAS_CHUNK = 4 * BAND_W


def _rel_onehot(chunk):
    lane = lax.broadcasted_iota(jnp.int32, (384, chunk), 1)
    r = lax.broadcasted_iota(jnp.int32, (384, chunk), 0)
    sub = jnp.where(lane >= BAND_W, 1, 0) + jnp.where(lane >= 2 * BAND_W, 1, 0) + jnp.where(lane >= 3 * BAND_W, 1, 0)
    i = pl.program_id(0) * 4 + sub
    col = lane - sub * BAND_W
    idx = jnp.clip(BAND_W - T_BAND + i - col, -REL_CLIP, REL_CLIP) + REL_CLIP
    return jnp.where(idx == r, 1.0, 0.0).astype(BF16)


BIAS_G = 768
BIAS_EDGE = BIAS_G - N_REL


def _bias_line(rel_bias):
    g = jnp.concatenate([jnp.broadcast_to(rel_bias[:, N_REL - 1:], (rel_bias.shape[0], BIAS_EDGE)),
                         jnp.flip(rel_bias, axis=1)], axis=1)
    return jnp.pad(g, ((0, 8 - g.shape[0]), (0, 0)))


def _bias_unline(dg):
    return jnp.flip(dg[:, BIAS_EDGE:], axis=1)


def _bias_expand(g8):
    def body(g_ref, out_ref):
        line = jnp.broadcast_to(g_ref[0] * LOG2E, (T_BAND, BIAS_G))
        slab = pltpu.roll(line, 1, 1, stride=1, stride_axis=0)[:, LANE:BIAS_G]
        row = lax.broadcasted_iota(jnp.int32, (T_BAND, BAND_W), 0)
        col = lax.broadcasted_iota(jnp.int32, (T_BAND, BAND_W), 1)
        hidden = ((row >= 64) & (col < 64)) | ((row < 64) & (col >= BAND_W - 64))
        out_ref[0] = jnp.where(hidden, NEG, slab)

    return pl.pallas_call(
        body, name="bias_expand", grid=(8,), in_specs=[pl.BlockSpec((1, 1, BIAS_G), lambda h: (h, 0, 0))],
        out_specs=pl.BlockSpec((1, T_BAND, BAND_W), lambda h: (h, 0, 0)),
        out_shape=jax.ShapeDtypeStruct((8, T_BAND, BAND_W), F32), compiler_params=_params(("parallel",)),
    )(g8.reshape(8, 1, BIAS_G))


def _bias_reduce(d_slab):
    def body(d_ref, out_ref):
        r = lax.broadcasted_iota(jnp.int32, (T_BAND, T_BAND), 0)
        k = lax.broadcasted_iota(jnp.int32, (T_BAND, T_BAND), 1)
        flip = jnp.where(r + k == T_BAND - 1, 1.0, 0.0).astype(BF16)
        hi, mid, lo = _split3(d_ref[0])
        d_rev = _dot(flip, hi) + _dot(flip, mid) + _dot(flip, lo)
        wide = jnp.concatenate([jnp.zeros((T_BAND, LANE), F32), d_rev, jnp.zeros((T_BAND, 2 * LANE), F32)], axis=1)
        skew = pltpu.roll(wide, 0, 1, stride=1, stride_axis=0)
        dg = jnp.sum(skew, axis=0, keepdims=True)[:, LANE:LANE + BIAS_G]
        lane = _lane_iota((1, BIAS_G))
        clipped = jnp.sum(jnp.where(lane <= BIAS_EDGE, dg, 0.0), axis=1, keepdims=True)
        out_ref[0] = jnp.where(lane == BIAS_EDGE, clipped, dg)

    return pl.pallas_call(
        body, name="bias_reduce", grid=(8,), in_specs=[pl.BlockSpec((1, T_BAND, BAND_W), lambda h: (h, 0, 0))],
        out_specs=pl.BlockSpec((1, 1, BIAS_G), lambda h: (h, 0, 0)),
        out_shape=jax.ShapeDtypeStruct((8, 1, BIAS_G), F32), compiler_params=_params(("parallel",)),
    )(d_slab).reshape(8, BIAS_G)


def _outproj_bwd(dxn, y, gate, oa, ob, oc, zg, w_out_p, w_out_pt, nb, seq):
    t = dxn.shape[0]
    tpe = seq // TM

    def body(dxn_ref, y_ref, gate_ref, oa_ref, ob_ref, oc_ref, zg_ref, w_ref, wt_ref,
             doa_ref, dob_ref, doc_ref, dzg_ref, gw_ref, dgate_ref):
        i = pl.program_id(0)

        @pl.when(i == 0)
        def _():
            gw_ref[...] = jnp.zeros_like(gw_ref)

        @pl.when(i % tpe == 0)
        def _():
            dgate_ref[...] = jnp.zeros_like(dgate_ref)

        dxn_t = dxn_ref[...]
        dgate_ref[0] += jnp.sum(dxn_t * y_ref[...], axis=0, keepdims=True)
        dy = (dxn_t * gate_ref[0]).astype(BF16)
        for gi, (o_ref, do_ref) in enumerate(((oa_ref, doa_ref), (ob_ref, dob_ref), (oc_ref, doc_ref))):
            cols = slice(384 * gi, 384 * (gi + 1))
            u = zg_ref[:, cols]
            o_t = o_ref[...]
            su = _silu(u)
            dcat = _dot(dy, wt_ref[:, cols])
            do_ref[...] = dcat * su
            dzg_ref[:, cols] = (dcat * o_t * _dsilu(u)).astype(BF16)
            gw_ref[cols, :] += _dot_tn((o_t * su).astype(BF16), dy)

    return pl.pallas_call(
        body, name="outproj_bwd", grid=(t // TM,),
        in_specs=[_row_spec(D_MODEL), _row_spec(D_MODEL), _ex_spec(tpe), _row_spec(384), _row_spec(384), _row_spec(384),
                  _row_spec(D_CAT), _full_spec((D_CAT, D_MODEL)), _full_spec((D_MODEL, D_CAT))],
        out_specs=[_row_spec(384), _row_spec(384), _row_spec(384), _row_spec(D_CAT), _full_spec((D_CAT, D_MODEL)),
                   _ex_spec(tpe)],
        out_shape=[jax.ShapeDtypeStruct((t, 384), F32)] * 3 + [jax.ShapeDtypeStruct((t, D_CAT), BF16),
                                                                jax.ShapeDtypeStruct((D_CAT, D_MODEL), F32),
                                                                jax.ShapeDtypeStruct((nb, 1, D_MODEL), F32)],
        compiler_params=_params(("arbitrary",)),
    )(dxn, y, gate, oa, ob, oc, zg, w_out_p, w_out_pt)


def _a_up_bwd(dqa, dka, dva, za, gq, gkv, w_uq_pt, w_ukv_pt, cos, sina, sinb):
    t = za.shape[0]

    def body(dq_ref, dk_ref, dv_ref, za_ref, gq_ref, gkv_ref, wqt_ref, wkvt_ref, cos_ref, sa_ref, sb_ref,
             dza_ref, gwq_ref, gwkv_ref, ggq_ref, ggkv_ref, dqb, dkvb):
        @pl.when(pl.program_id(0) == 0)
        def _():
            gwq_ref[...] = jnp.zeros_like(gwq_ref)
            gwkv_ref[...] = jnp.zeros_like(gwkv_ref)
            ggq_ref[...] = jnp.zeros_like(ggq_ref)
            ggkv_ref[...] = jnp.zeros_like(ggkv_ref)

        cos_t, sa, sb = cos_ref[...], sa_ref[...], sb_ref[...]
        dkpe = jnp.zeros((TM, LANE), F32)
        for p in range(3):
            dqb[:, 256 * p:256 * p + 128] = dq_ref[:, 256 * p:256 * p + 128].astype(BF16)
            dqb[:, 256 * p + 128:256 * p + 256] = _rope_t(dq_ref[:, 256 * p + 128:256 * p + 256], cos_t, sa, sb).astype(BF16)
            dkvb[:, 128 * p:128 * p + 128] = dk_ref[:, 256 * p:256 * p + 128].astype(BF16)
            dkpe = dkpe + dk_ref[:, 256 * p + 128:256 * p + 256]
        dkvb[:, 384:768] = dv_ref[...]
        dkpe = _rope_t(dkpe, cos_t, sa, sb)
        dkpe = jnp.where(_lane_iota((TM, LANE)) < A_ROPE, dkpe + pltpu.roll(dkpe, LANE - 32, 1), 0.0)

        gqv = gq_ref[...]
        cqn, cqh, rq = _rms(za_ref[:, 0:384], gqv)
        dq_t = dqb[...]
        gwq_ref[...] += _dot_tn(cqn.astype(BF16), dq_t)
        dcqn = _dot(dq_t, wqt_ref[...])
        ggq_ref[...] += jnp.broadcast_to(jnp.sum(dcqn * cqh, axis=0, keepdims=True), ggq_ref.shape)
        dza_ref[:, 0:384] = _rms_bwd(dcqn, cqh, rq, gqv).astype(BF16)

        gkvv = gkv_ref[...]
        ckvn, ckvh, rkv = _rms(za_ref[:, 384:640], gkvv)
        dkv_t = dkvb[...]
        gwkv_ref[...] += _dot_tn(ckvn.astype(BF16), dkv_t)
        dckvn = _dot(dkv_t, wkvt_ref[...])
        ggkv_ref[...] += jnp.broadcast_to(jnp.sum(dckvn * ckvh, axis=0, keepdims=True), ggkv_ref.shape)
        dza_ref[:, 384:640] = _rms_bwd(dckvn, ckvh, rkv, gkvv).astype(BF16)
        dza_ref[:, 640:768] = dkpe.astype(BF16)

    return pl.pallas_call(
        body, name="a_up_bwd", grid=(t // TM,),
        in_specs=[_row_spec(768), _row_spec(768), _row_spec(384), _row_spec(768), _full_spec((1, 384)),
                  _full_spec((1, 256)), _full_spec((768, 384)), _full_spec((768, 256)), _row_spec(LANE), _row_spec(LANE),
                  _row_spec(LANE)],
        out_specs=[_row_spec(768), _full_spec((384, 768)), _full_spec((256, 768)), _full_spec((8, 384)),
                   _full_spec((8, 256))],
        out_shape=[jax.ShapeDtypeStruct((t, 768), BF16), jax.ShapeDtypeStruct((384, 768), F32),
                   jax.ShapeDtypeStruct((256, 768), F32), jax.ShapeDtypeStruct((8, 384), F32),
                   jax.ShapeDtypeStruct((8, 256), F32)],
        scratch_shapes=[pltpu.VMEM((TM, 768), BF16), pltpu.VMEM((TM, 768), BF16)],
        compiler_params=_params(("arbitrary",)),
    )(dqa, dka, dva, za, gq, gkv, w_uq_pt, w_ukv_pt, cos, sina, sinb)


def _dz_cols():
    return (Z_A, Z_G) + Z_QKV + (Z_F,)


def _inproj_bwd_dx(dz, dxn, x, shift, scale, g, w_in_pt, nb, seq):
    t = x.shape[0]
    tpe = seq // TM
    cols = _dz_cols()

    def body(*refs):
        dz_refs = refs[:len(cols)]
        dxn_ref, x_ref, sh_ref, sc_ref, g_ref, wt_ref, dx_ref, dsh_ref, dsc_ref, dg_ref = refs[len(cols):]
        i = pl.program_id(0)

        @pl.when(i == 0)
        def _():
            dg_ref[...] = jnp.zeros_like(dg_ref)

        @pl.when(i % tpe == 0)
        def _():
            dsh_ref[...] = jnp.zeros_like(dsh_ref)
            dsc_ref[...] = jnp.zeros_like(dsc_ref)

        dh = jnp.zeros((TM, D_MODEL), F32)
        for ref, (c0, c1) in zip(dz_refs, cols):
            dh = dh + _dot_nt(ref[...], wt_ref[:, c0:c1])
        gv = g_ref[...]
        n, xh, r = _rms(x_ref[...], gv)
        dsh_ref[0] += jnp.sum(dh, axis=0, keepdims=True)
        dsc_ref[0] += jnp.sum(dh * n, axis=0, keepdims=True)
        dn = dh * (1.0 + sc_ref[0])
        dg_ref[...] += jnp.broadcast_to(jnp.sum(dn * xh, axis=0, keepdims=True), dg_ref.shape)
        dx_ref[...] = dxn_ref[...] + _rms_bwd(dn, xh, r, gv)

    in_specs = [_row_spec(c1 - c0) for c0, c1 in cols]
    in_specs += [_row_spec(D_MODEL), _row_spec(D_MODEL), _ex_spec(tpe), _ex_spec(tpe), _full_spec((1, D_MODEL)),
                 _full_spec((D_MODEL, NP_IN))]
    return pl.pallas_call(
        body, name="inproj_bwd_dx", grid=(t // TM,), in_specs=in_specs,
        out_specs=[_row_spec(D_MODEL), _ex_spec(tpe), _ex_spec(tpe), _full_spec((8, D_MODEL))],
        out_shape=[jax.ShapeDtypeStruct((t, D_MODEL), F32), jax.ShapeDtypeStruct((nb, 1, D_MODEL), F32),
                   jax.ShapeDtypeStruct((nb, 1, D_MODEL), F32), jax.ShapeDtypeStruct((8, D_MODEL), F32)],
        compiler_params=_params(("arbitrary",)),
    )(*dz, dxn, x, shift, scale, g, w_in_pt)


def _inproj_bwd_dw(h, dz, name, layer, both=None):
    t = h.shape[0]
    widths = [d.shape[1] for d in dz]
    total = sum(widths)

    def body(*refs):
        h_ref = refs[0]
        dz_refs = refs[1:1 + len(dz)]
        gw_ref = refs[-1]

        @pl.when(pl.program_id(0) == 0)
        def _():
            gw_ref[...] = jnp.zeros_like(gw_ref)

        h_t = h_ref[...]
        c0 = 0
        for ref, w in zip(dz_refs, widths):
            gw_ref[0, :, c0:c0 + w] += _dot_tn(h_t, ref[...])
            c0 += w

    in_specs = [_row_spec(D_MODEL)] + [_row_spec(w) for w in widths]
    args = [h, *dz]
    aliases = {}
    if both is not None:
        in_specs.append(pl.BlockSpec(memory_space=pl.ANY))
        aliases = {len(args): 0}
        args.append(both)
    return pl.pallas_call(
        body, name=name, grid=(t // TM,), in_specs=in_specs,
        out_specs=pl.BlockSpec((1, D_MODEL, total), lambda i: (layer, 0, 0)),
        out_shape=jax.ShapeDtypeStruct((DEPTH, D_MODEL, total), F32), input_output_aliases=aliases,
        compiler_params=_params(("arbitrary",)),
    )(*args)


def _ada_fwd(c_all, w_ada, b_cols):
    n = c_all.shape[0]
    cols = w_ada.shape[2]

    def body(c_ref, w_ref, b_ref, out_ref):
        act = _silu(c_ref[...]).astype(BF16)
        out_ref[0] = _dot(act, w_ref[0].astype(BF16)) + b_ref[0]

    return pl.pallas_call(
        body, name="ada_fwd", grid=(DEPTH,),
        in_specs=[pl.BlockSpec((n, D_MODEL), lambda l: (0, 0)), pl.BlockSpec((1, D_MODEL, cols), lambda l: (l, 0, 0)),
                  pl.BlockSpec((1, 1, cols), lambda l: (l, 0, 0))],
        out_specs=pl.BlockSpec((1, n, cols), lambda l: (l, 0, 0)),
        out_shape=jax.ShapeDtypeStruct((DEPTH, n, cols), F32), compiler_params=_params(("parallel",)),
    )(c_all, w_ada, b_cols)


def _ada_bwd(c_all, dmod_cols, dmod_all):
    n = c_all.shape[0]
    cols = dmod_cols.shape[2]

    def body(c_ref, dc_ref, da_ref, gw_ref, gb_ref):
        act = _silu(c_ref[...]).astype(BF16)
        gw_ref[0] = _dot_tn(act, dc_ref[0].astype(BF16))
        gb_ref[0] = jnp.sum(da_ref[0], axis=0, keepdims=True)

    return pl.pallas_call(
        body, name="ada_bwd", grid=(DEPTH,),
        in_specs=[pl.BlockSpec((n, D_MODEL), lambda l: (0, 0)), pl.BlockSpec((1, n, cols), lambda l: (l, 0, 0)),
                  pl.BlockSpec((1, n, 3 * D_MODEL), lambda l: (l, 0, 0))],
        out_specs=[pl.BlockSpec((1, D_MODEL, cols), lambda l: (l, 0, 0)),
                   pl.BlockSpec((1, 1, 3 * D_MODEL), lambda l: (l, 0, 0))],
        out_shape=[jax.ShapeDtypeStruct((DEPTH, D_MODEL, cols), F32), jax.ShapeDtypeStruct((DEPTH, 1, 3 * D_MODEL), F32)],
        compiler_params=_params(("parallel",)),
    )(c_all, dmod_cols, dmod_all)


def _sum_blocks(parts, name):
    n, rows, cols = parts.shape
    tr = rows if rows <= 256 else 8 * next(d for d in range(32, 0, -1) if (rows // 8) % d == 0)

    def body(p_ref, out_ref):
        acc = p_ref[0].astype(F32)
        for k in range(1, n):
            acc = acc + p_ref[k].astype(F32)
        out_ref[...] = acc

    return pl.pallas_call(
        body, name=name, grid=(rows // tr,), in_specs=[pl.BlockSpec((n, tr, cols), lambda i: (0, i, 0))],
        out_specs=pl.BlockSpec((tr, cols), lambda i: (i, 0)), out_shape=jax.ShapeDtypeStruct((rows, cols), F32),
        compiler_params=_params(("parallel",)),
    )(parts)


def _adamw(w, g, m, v, name):
    nl, rows, cols = w.shape
    if rows * cols <= 64 * 1024:
        tr = rows
        tl = next(t for t in range(nl, 0, -1) if nl % t == 0 and t * max(rows, 8) * cols <= 512 * 1024)
    else:
        tl = 1
        tr = next(t for t in (rows, 256, 128, 64, 32, 16, 8) if rows % t == 0 and t * cols <= 256 * 1024)

    def body(w_ref, g_ref, m_ref, v_ref, d_ref, mo_ref, vo_ref):
        gv = g_ref[...]
        mn = ADAM_B1 * m_ref[...] + (1.0 - ADAM_B1) * gv
        vn = ADAM_B2 * v_ref[...] + (1.0 - ADAM_B2) * jnp.square(gv)
        m_hat = mn / (1.0 - ADAM_B1 ** ADAM_STEP)
        v_hat = vn / (1.0 - ADAM_B2 ** ADAM_STEP)
        d_ref[...] = -ADAM_LR * (m_hat / (jnp.sqrt(v_hat) + ADAM_EPS) + ADAM_WD * w_ref[...])
        mo_ref[...] = mn
        vo_ref[...] = vn

    spec = pl.BlockSpec((tl, tr, cols), lambda l, i: (l, i, 0))
    return pl.pallas_call(
        body, name=name, grid=(nl // tl, rows // tr), in_specs=[spec] * 4, out_specs=[spec] * 3,
        out_shape=[jax.ShapeDtypeStruct((nl, rows, cols), F32)] * 3, compiler_params=_params(("parallel", "parallel")),
    )(w, g, m, v)


ALL_FLIPS = tuple(range(1, N_DEV))


def _exchange(src, flips, mode, name):
    _, rows, cols = src.shape
    nslot = 2 if mode == "pair" else N_DEV
    nf = len(flips)

    def body(src_ref, dst_ref, send_sems, recv_sems, local_sem):
        x, y, c = lax.axis_index("x"), lax.axis_index("y"), lax.axis_index("c")
        me = 4 * x + 2 * y + c

        def slot(j):
            return (j & 1) if mode == "pair" else j

        own = pltpu.make_async_copy(src_ref.at[me if mode == "scatter" else 0], dst_ref.at[slot(me)], local_sem)
        own.start()
        copies = []
        for i, f in enumerate(flips):
            peer = me ^ f
            to = (1 - x if f & 4 else x, 1 - y if f & 2 else y, 1 - c if f & 1 else c)
            cp = pltpu.make_async_remote_copy(
                src_ref=src_ref.at[peer if mode == "scatter" else 0], dst_ref=dst_ref.at[slot(me)],
                send_sem=send_sems.at[i], recv_sem=recv_sems.at[i], device_id=to, device_id_type=pl.DeviceIdType.MESH)
            cp.start()
            copies.append(cp)
        for i, f in enumerate(flips):
            peer = me ^ f
            to = (1 - x if f & 4 else x, 1 - y if f & 2 else y, 1 - c if f & 1 else c)
            pltpu.make_async_remote_copy(
                src_ref=src_ref.at[0], dst_ref=dst_ref.at[slot(peer)], send_sem=send_sems.at[i],
                recv_sem=recv_sems.at[i], device_id=to, device_id_type=pl.DeviceIdType.MESH).wait_recv()
        for cp in copies:
            cp.wait_send()
        own.wait()

    return pl.pallas_call(
        body, name=name, out_shape=jax.ShapeDtypeStruct((nslot, rows, cols), src.dtype),
        in_specs=[pl.BlockSpec(memory_space=pl.ANY)], out_specs=pl.BlockSpec(memory_space=pl.ANY),
        scratch_shapes=[pltpu.SemaphoreType.DMA((nf,)), pltpu.SemaphoreType.DMA((nf,)), pltpu.SemaphoreType.DMA],
    )(src)


def _transfer(name, srcs, dst_shapes, plan, in_place=False):
    n_arr = len(srcs)
    probe = plan(0, 0, 0)
    n_steps = len(probe)

    def body(*refs):
        src_refs, dst_refs = refs[:n_arr], refs[n_arr:2 * n_arr]
        send_sems, recv_sems, local_sems = refs[2 * n_arr:]
        x, y, c = lax.axis_index("x"), lax.axis_index("y"), lax.axis_index("c")
        steps = plan(x, y, c)

        def rows(ref, r0, n):
            return ref.at[:, pl.ds(r0, n), :]

        def arrival(t):
            a, _, _, n, _, f, _ = steps[t]
            return pltpu.make_async_remote_copy(
                src_ref=rows(dst_refs[a], 0, n), dst_ref=rows(dst_refs[a], 0, n), send_sem=send_sems.at[t],
                recv_sem=recv_sems.at[t], device_id=(x, y, c), device_id_type=pl.DeviceIdType.MESH)

        arrived, started = set(), []
        for t, (a, from_dst, sr, n, dr, f, after) in enumerate(steps):
            for u in after:
                if u not in arrived:
                    arrival(u).wait_recv()
                    arrived.add(u)
            src = rows(dst_refs[a] if from_dst else src_refs[a], sr, n)
            dst = rows(dst_refs[a], dr, n)
            if f == 0:
                cp = pltpu.make_async_copy(src, dst, local_sems.at[t])
            else:
                to = (1 - x if f & 4 else x, 1 - y if f & 2 else y, 1 - c if f & 1 else c)
                cp = pltpu.make_async_remote_copy(src_ref=src, dst_ref=dst, send_sem=send_sems.at[t],
                                                  recv_sem=recv_sems.at[t], device_id=to,
                                                  device_id_type=pl.DeviceIdType.MESH)
            cp.start()
            started.append(cp)
        for t, step in enumerate(steps):
            if step[5] != 0 and t not in arrived:
                arrival(t).wait_recv()
        for cp, step in zip(started, steps):
            if step[5] == 0:
                cp.wait()
            else:
                cp.wait_send()

    any_spec = pl.BlockSpec(memory_space=pl.ANY)
    return pl.pallas_call(
        body, name=name, out_shape=[jax.ShapeDtypeStruct(s, d) for s, d in dst_shapes],
        in_specs=[any_spec] * n_arr, out_specs=[any_spec] * n_arr,
        input_output_aliases={a: a for a in range(n_arr)} if in_place else {},
        scratch_shapes=[pltpu.SemaphoreType.DMA((n_steps,)), pltpu.SemaphoreType.DMA((n_steps,)),
                        pltpu.SemaphoreType.DMA((n_steps,))],
    )(*srcs)


CHIP_FLIPS = (2, 4, 6)


def _gather_plan(chip_rows):
    def plan(x, y, c):
        steps = []
        for a, rc in enumerate(chip_rows):
            h = rc // 2
            mine = rc * (2 * x + y) + h * c
            steps.append((a, False, h * c, h, mine, 0, ()))
            ici = {}
            for f in CHIP_FLIPS:
                ici[f] = len(steps)
                steps.append((a, False, h * c, h, mine, f, ()))
            steps.append((a, False, h * c, h, mine, 1, ()))
            for f in CHIP_FLIPS:
                theirs = rc * ((2 * x + y) ^ (f >> 1)) + h * c
                steps.append((a, True, theirs, h, theirs, 1, (ici[f],)))
        return steps
    return plan


def _pair_reduce_plan(chip_rows):
    def plan(x, y, c):
        steps = []
        for a, rc in enumerate(chip_rows):
            h = rc // 2
            for j in range(4):
                steps.append((a, False, rc * j + h * (1 - c), h, h * j, 1, ()))
        return steps
    return plan


def _chip_scatter_plan(chip_rows):
    def plan(x, y, c):
        steps = []
        for a, rc in enumerate(chip_rows):
            h = rc // 2
            for k, f in enumerate(CHIP_FLIPS):
                steps.append((a, False, h * ((2 * x + y) ^ (f >> 1)), h, h * k, f, ()))
        return steps
    return plan


def _pair_share_plan(chip_rows):
    def plan(x, y, c):
        return [(a, True, (rc // 2) * c, rc // 2, (rc // 2) * c, 1, ()) for a, rc in enumerate(chip_rows)]
    return plan


def _tile_rows(h):
    return h


def _sum_pair(partial, recv, core, rc, name):
    nl, _, cols = partial.shape
    h = rc // 2
    tr = _tile_rows(h)

    def body(c_ref, p_ref, r_ref, out_ref):
        out_ref[...] = (p_ref[...] + r_ref[...]).astype(BF16)

    spec = pl.BlockSpec((1, tr, cols), lambda l, j, i, c_ref: (l, (h // tr) * j + i, 0))
    return pl.pallas_call(
        body, name=name, out_shape=jax.ShapeDtypeStruct((nl, 4 * h, cols), BF16),
        grid_spec=pltpu.PrefetchScalarGridSpec(
            num_scalar_prefetch=1, grid=(nl, 4, h // tr),
            in_specs=[pl.BlockSpec((1, tr, cols), lambda l, j, i, c_ref: (l, (rc // tr) * j + (h // tr) * c_ref[0] + i, 0)),
                      spec],
            out_specs=spec),
        compiler_params=_params(("parallel", "parallel", "parallel")),
    )(core, partial, recv)


def _sum_chips(chip_sum, recv, place, rc, name):
    nl, _, cols = chip_sum.shape
    h = rc // 2
    tr = _tile_rows(h)

    def body(s_ref, own_ref, r_ref, out_ref):
        acc = own_ref[0].astype(F32)
        for k in range(3):
            acc = acc + r_ref[0, k].astype(F32)
        out_ref[0] = acc

    return pl.pallas_call(
        body, name=name, out_shape=jax.ShapeDtypeStruct((nl, rc, cols), F32),
        grid_spec=pltpu.PrefetchScalarGridSpec(
            num_scalar_prefetch=1, grid=(nl, h // tr),
            in_specs=[pl.BlockSpec((1, tr, cols), lambda l, i, s_ref: (l, (h // tr) * s_ref[0] + i, 0)),
                      pl.BlockSpec((1, 3, tr, cols), lambda l, i, s_ref: (l, 0, i, 0))],
            out_specs=pl.BlockSpec((1, tr, cols), lambda l, i, s_ref: (l, (h // tr) * s_ref[1] + i, 0))),
        compiler_params=_params(("parallel", "parallel")),
    )(place, chip_sum, recv.reshape(nl, 3, h, cols))


def _sum_slots(parts, out_dtype, name):
    nl, n, rows, cols = parts.shape
    tr = next(t for t in (128, 64, 32, 16) if rows % t == 0 and n * t * cols * 4 <= (4 << 20))

    def body(p_ref, out_ref):
        acc = p_ref[0, 0].astype(F32)
        for k in range(1, n):
            acc = acc + p_ref[0, k].astype(F32)
        out_ref[0] = acc.astype(out_dtype)

    return pl.pallas_call(
        body, name=name, grid=(nl, rows // tr),
        in_specs=[pl.BlockSpec((1, n, tr, cols), lambda l, i: (l, 0, i, 0))],
        out_specs=pl.BlockSpec((1, tr, cols), lambda l, i: (l, i, 0)),
        out_shape=jax.ShapeDtypeStruct((nl, rows, cols), out_dtype), compiler_params=_params(("parallel", "parallel")),
    )(parts)


def _pad_cols(a, n):
    return a if n == 0 else jnp.pad(a, ((0, 0), (0, n)))


def _in_to_padded(w):
    return jnp.concatenate([_pad_cols(w[:, a:b], z) for a, b, z in IN_PIECES], axis=1)


def _in_from_padded(gp):
    pos, out = 0, {}
    for a, b, z in IN_PIECES:
        out[a] = gp[:, pos:pos + (b - a)]
        pos += (b - a) + z
    return jnp.concatenate([out[a] for a in sorted(out)], axis=1)


def _in_cols_first(ga, gb):
    split = Z_G[1]
    pos, out = 0, {}
    for a, b, z in IN_PIECES:
        src, off = (ga, pos) if pos < split else (gb, pos - split)
        out[a] = jnp.transpose(src[:, :, off:off + (b - a)], (2, 0, 1))
        pos += (b - a) + z
    return jnp.concatenate([out[a] for a in sorted(out)], axis=0)


def _out_to_padded(w):
    z = jnp.zeros((64, w.shape[1]), w.dtype)
    return jnp.concatenate([w[0:384], w[384:704], z, w[704:1024], z], axis=0)


def _out_from_padded(gp):
    return jnp.concatenate([gp[0:384], gp[384:704], gp[768:1088]], axis=0)


def _uq_to_padded(w):
    parts = []
    for p in range(3):
        h0, h1 = 2 * p, 2 * p + 1
        parts += [w[:, 96 * h0:96 * h0 + 64], w[:, 96 * h1:96 * h1 + 64], w[:, 96 * h0 + 64:96 * h0 + 96],
                  w[:, 96 * h1 + 64:96 * h1 + 96], jnp.zeros((w.shape[0], 64), w.dtype)]
    return jnp.concatenate(parts, axis=1)


def _uq_from_padded(gp):
    parts = []
    for h in range(6):
        p, s = h // 2, h % 2
        parts += [gp[:, 256 * p + 64 * s:256 * p + 64 * s + 64], gp[:, 256 * p + 128 + 32 * s:256 * p + 160 + 32 * s]]
    return jnp.concatenate(parts, axis=1)


def _ukv_to_padded(w):
    return jnp.concatenate([w[:, 128 * h:128 * h + 64] for h in range(6)]
                           + [w[:, 128 * h + 64:128 * h + 128] for h in range(6)], axis=1)


def _ukv_from_padded(gp):
    parts = []
    for h in range(6):
        parts += [gp[:, 64 * h:64 * h + 64], gp[:, 384 + 64 * h:384 + 64 * h + 64]]
    return jnp.concatenate(parts, axis=1)


LR_ROWS = 224
SHARD_ROWS = (256, 256, 256, LR_ROWS)


def _pack_lowrank(w_uq, w_ukv):
    flat = jnp.concatenate([w_uq.reshape(-1), w_ukv.reshape(-1)])
    return jnp.pad(flat, (0, LR_ROWS * PACK_COLS - flat.shape[0])).reshape(1, LR_ROWS, PACK_COLS)


def _unpack_lowrank(packed):
    flat = packed.reshape(-1)
    n_uq = DEPTH * A_Q_RANK * 144
    n_ukv = DEPTH * A_KV_RANK * 192
    return flat[0:n_uq].reshape(DEPTH, A_Q_RANK, 144), flat[n_uq:n_uq + n_ukv].reshape(DEPTH, A_KV_RANK, 192)


PACK_SIZES = (DEPTH * 256 * N_IN, DEPTH * 256 * D_MODEL, DEPTH * A_Q_RANK * 144, DEPTH * A_KV_RANK * 192)


def _pack(w_in, w_out, w_uq, w_ukv):
    flat = jnp.concatenate([w_in.reshape(-1), w_out.reshape(-1), w_uq.reshape(-1), w_ukv.reshape(-1)])
    flat = jnp.pad(flat, (0, PACK_ROWS * PACK_COLS - flat.shape[0]))
    return flat.reshape(PACK_ROWS, PACK_COLS)


def _unpack(packed):
    flat = packed.reshape(-1)
    o0, o1, o2, o3 = PACK_SIZES
    w_in = flat[0:o0].reshape(DEPTH, 256, N_IN)
    w_out = flat[o0:o0 + o1].reshape(DEPTH, 256, D_MODEL)
    w_uq = flat[o0 + o1:o0 + o1 + o2].reshape(DEPTH, A_Q_RANK, 144)
    w_ukv = flat[o0 + o1 + o2:o0 + o1 + o2 + o3].reshape(DEPTH, A_KV_RANK, 192)
    return w_in, w_out, w_uq, w_ukv


def _rope_tables(positions):
    t = positions.size
    inv = ROPE_THETA ** (-jnp.arange(0, A_ROPE, 2, dtype=F32) / A_ROPE)
    inv_row = jnp.pad(jnp.tile(inv, 4), (0, 64)).reshape(1, LANE)

    def body(p_ref, i_ref, c_ref, sa_ref, sb_ref):
        ang = p_ref[...].astype(F32) * i_ref[...]
        lane = _lane_iota((TM, LANE))
        live = lane < 64
        second = (lane & 31) >= 16
        s = jnp.sin(ang)
        c_ref[...] = jnp.where(live, jnp.cos(ang), 0.0)
        sa_ref[...] = jnp.where(live & jnp.logical_not(second), -s, 0.0)
        sb_ref[...] = jnp.where(live & second, s, 0.0)

    return pl.pallas_call(
        body, name="rope_tables", grid=(t // TM,), in_specs=[_row_spec(1), _full_spec((1, LANE))],
        out_specs=[_row_spec(LANE)] * 3, out_shape=[jax.ShapeDtypeStruct((t, LANE), F32)] * 3,
        compiler_params=_params(("parallel",)),
    )(positions.reshape(t, 1), inv_row)


def _rows(a, n):
    flat = a.reshape(-1)
    return jnp.pad(flat, (0, n * LANE - flat.shape[0])).reshape(n, LANE)


def _forward_backward(x, mod, tables, target, weights, small, nb, seq):
    cos, sina, sinb = tables
    saved = []
    for l in range(DEPTH):
        w, s = weights[l], small[l]
        shift = mod[l][:, None, 0:D_MODEL]
        scale = mod[l][:, None, D_MODEL:2 * D_MODEL]
        gate = mod[l][:, None, 2 * D_MODEL:]
        h, za, zg, qb, kb, vb, qc, kc, vc, zf = _ln_inproj(x, shift, scale, s["norm_g"], w["in"], seq)
        qa, ka, va = _a_up(za, s["gq"], s["gkv"], w["uq"], w["ukv"], cos, sina, sinb)
        bias = _bias_expand(s["g8"])[0:6]
        f = _forget_fwd(zf, s["fb"], nb, seq)
        frow = jnp.pad(f[:, 0:5].reshape(nb, seq, 5).transpose(0, 2, 1), ((0, 0), (0, 1), (0, 0)))
        frow = frow.reshape(nb, 6, 1, seq)
        oa, lse_a = _attn_fwd("A", qa, ka, va, None, nb, seq)
        ob, lse_b = _attn_fwd("B", qb, kb, vb, bias, nb, seq)
        oc, lse_c = _attn_fwd("C", qc, kc, vc, frow, nb, seq)
        y, xn = _gate_outproj(x, gate, oa, ob, oc, zg, w["out"], seq)
        saved.append(dict(x=x, h=h, za=za, zg=zg, zf=zf, y=y, shift=shift, scale=scale, gate=gate, bias=bias, frow=frow,
                          a=(qa, ka, va, oa, lse_a), b=(qb, kb, vb, ob, lse_b), c=(qc, kc, vc, oc, lse_c)))
        x = xn
    dx, loss8, gfinal8 = _final_loss(x, target, small[0]["final_g"])
    grads = []
    gw_in = (None, None)
    for l in reversed(range(DEPTH)):
        w, s, sv = weights[l], small[l], saved[l]
        qa, ka, va, oa, lse_a = sv["a"]
        qb, kb, vb, ob, lse_b = sv["b"]
        qc, kc, vc, oc, lse_c = sv["c"]
        doa, dob, doc, dzg, gw_out, dgate = _outproj_bwd(dx, sv["y"], sv["gate"], oa, ob, oc, sv["zg"], w["out"],
                                                          w["out_t"], nb, seq)
        dqa, dka, dva = _attn_bwd("A", qa, ka, va, oa, doa, lse_a, None, nb, seq)
        dqb, dkb, dvb, dbt = _attn_bwd("B", qb, kb, vb, ob, dob, lse_b, sv["bias"], nb, seq)
        dqc, dkc, dvc, dfr, dfq = _attn_bwd("C", qc, kc, vc, oc, doc, lse_c, sv["frow"], nb, seq)
        dg = _bias_reduce(jnp.pad(dbt, ((0, 2), (0, 0), (0, 0))))
        grb = jnp.pad(_bias_unline(dg), ((0, 0), (0, 384 - N_REL)))
        dfk = dfr.reshape(nb, 6, seq).transpose(0, 2, 1).reshape(nb * seq, 6)
        dzf, gfb = _forget_bwd(dfq, jnp.pad(dfk, ((0, 0), (0, LANE - 6))), sv["zf"], s["fb"], nb, seq)
        dza, gw_uq, gw_ukv, ggq, ggkv = _a_up_bwd(dqa, dka, dva, sv["za"], s["gq"], s["gkv"], w["uq_t"], w["ukv_t"],
                                                  cos, sina, sinb)
        dz = (dza, dzg, dqb, dkb, dvb, dqc, dkc, dvc, dzf)
        dx, dshift, dscale, gnorm = _inproj_bwd_dx(dz, dx, sv["x"], sv["shift"], sv["scale"], s["norm_g"], w["in"],
                                                   nb, seq)
        gw_in = (_inproj_bwd_dw(sv["h"], dz[0:2], "inproj_bwd_dw0", l, gw_in[0]),
                 _inproj_bwd_dw(sv["h"], dz[2:], "inproj_bwd_dw1", l, gw_in[1]))
        dmod = jnp.concatenate([dshift[:, 0], dscale[:, 0], dgate[:, 0]], axis=1)
        grads.append(dict(w_out=gw_out, w_uq=gw_uq, w_ukv=gw_ukv, dmod=dmod, norm_g=gnorm[0], gq=ggq[0],
                          gkv=ggkv[0], rb8=grb, fb=gfb[0]))
    grads.reverse()
    return loss8[0, 0], dx, grads, gfinal8[0], gw_in


def _layer_weights(w_in, w_out, w_uq, w_ukv):
    wi, wo, wq, wkv = _in_to_padded(w_in), _out_to_padded(w_out), _uq_to_padded(w_uq), _ukv_to_padded(w_ukv)
    return {"in": wi, "out": wo, "out_t": wo.T, "uq": wq, "uq_t": wq.T, "ukv": wkv, "ukv_t": wkv.T}


def _layer_small(norm_g, gq, gkv, rel_bias, forget_b, final_g):
    fb = jnp.pad(forget_b, (0, LANE - 5)).reshape(1, LANE)
    return dict(norm_g=norm_g.reshape(1, -1), gq=gq.reshape(1, -1), gkv=gkv.reshape(1, -1), g8=_bias_line(rel_bias), fb=fb,
                final_g=final_g.reshape(1, -1))


def _small_payload(per_layer, final_g, loss):
    def stack(key):
        return jnp.stack([p[key] for p in per_layer])

    def rows(a, rng):
        return _rows(a, rng[1] - rng[0])

    dmod = stack("dmod") if "dmod" in per_layer[0] else jnp.zeros((LANE,), F32)
    parts = [rows(dmod, PAY_DMOD), rows(stack("norm_g"), PAY_NORM), rows(stack("gq"), PAY_GQ),
             rows(stack("gkv"), PAY_GKV), rows(stack("rb8"), PAY_RB), rows(stack("fb"), PAY_FB),
             rows(final_g, PAY_FINAL), rows(loss, PAY_LOSS)]
    return jnp.concatenate(parts, axis=0)


def _payload_split(pay):
    def take(rng, shape):
        n = 1
        for d in shape:
            n *= d
        return pay[rng[0]:rng[1]].reshape(-1)[0:n].reshape(shape)

    norm_g = take(PAY_NORM, (DEPTH, D_MODEL))
    gq = take(PAY_GQ, (DEPTH, A_Q_RANK))
    gkv = take(PAY_GKV, (DEPTH, A_KV_RANK))
    rb = take(PAY_RB, (DEPTH, 8, 384))[:, 0:5, 0:N_REL]
    fb = take(PAY_FB, (DEPTH, LANE))[:, 0:5]
    final_g = take(PAY_FINAL, (D_MODEL,))
    return norm_g, gq, gkv, rb, fb, final_g


def kernel(x, c, positions, w_ada, b_ada, norm_g, w_in, a_q_norm_g, a_w_uq, a_kv_norm_g, a_w_ukv, b_rel_bias, c_forget_b, w_out, final_g, loss_target, m_w_ada, m_b_ada, m_norm_g, m_w_in, m_a_q_norm_g, m_a_w_uq, m_a_kv_norm_g, m_a_w_ukv, m_b_rel_bias, m_c_forget_b, m_w_out, m_final_g, v_w_ada, v_b_ada, v_norm_g, v_w_in, v_a_q_norm_g, v_a_w_uq, v_a_kv_norm_g, v_a_w_ukv, v_b_rel_bias, v_c_forget_b, v_w_out, v_final_g):
    nb, seq, _ = x.shape
    ix, iy, ic = lax.axis_index("x"), lax.axis_index("y"), lax.axis_index("c")
    chip = 2 * ix + iy
    me = 2 * chip + ic

    weight_plan = _gather_plan((256, 256, LR_ROWS))

    def gather_plan(x, y, c_):
        me_ = 4 * x + 2 * y + c_
        return weight_plan(x, y, c_) + [(3, False, 0, 8, 8 * me_, f, ()) for f in range(N_DEV)]

    full_in, full_out, full_lr, c_rows = _transfer(
        "gather_weights", [w_in.astype(BF16), w_out.astype(BF16), _pack_lowrank(a_w_uq, a_w_ukv).astype(BF16),
                           jnp.pad(c, ((0, 8 - nb), (0, 0)))[None]],
        [((DEPTH, D_MODEL, N_IN), BF16), ((DEPTH, D_MODEL, D_MODEL), BF16), ((1, 4 * LR_ROWS, PACK_COLS), BF16),
         ((1, 8 * N_DEV, D_MODEL), F32)], gather_plan)
    lowrank = [_unpack_lowrank(full_lr[0, LR_ROWS * j:LR_ROWS * (j + 1)]) for j in range(4)]
    full_uq = jnp.concatenate([s[0] for s in lowrank], axis=2)
    full_ukv = jnp.concatenate([s[1] for s in lowrank], axis=2)
    weights = [_layer_weights(full_in[l], full_out[l], full_uq[l], full_ukv[l]) for l in range(DEPTH)]
    small = [_layer_small(norm_g[l], a_q_norm_g[l], a_kv_norm_g[l], b_rel_bias[l], c_forget_b[l], final_g)
             for l in range(DEPTH)]

    c_all = c_rows.reshape(N_DEV, 8, D_MODEL)[:, 0:nb].reshape(N_DEV * nb, D_MODEL)
    cols = w_ada.shape[2]
    b_cols = lax.dynamic_slice_in_dim(b_ada, chip * cols, cols, axis=1)[:, None, :]
    mod_cols = _ada_fwd(c_all, w_ada, b_cols)
    mod_g = _exchange(mod_cols.reshape(1, DEPTH * N_DEV * nb, cols), ALL_FLIPS, "gather", "gather_mod")
    mod_all = jnp.concatenate([mod_g[2 * j].reshape(DEPTH, N_DEV * nb, cols) for j in range(4)], axis=2)
    mod = lax.dynamic_slice_in_dim(mod_all, me * nb, nb, axis=1)

    tables = _rope_tables(positions)
    loss_part, dx, grads, gfinal, gw_in = _forward_backward(
        x.reshape(nb * seq, D_MODEL), mod, tables, loss_target.reshape(nb * seq, D_MODEL), weights, small, nb, seq)

    pay = _small_payload(grads, gfinal, loss_part)
    pay_all = _exchange(pay[None], ALL_FLIPS, "gather", "gather_small")
    tot = _sum_blocks(pay_all, "sum_small")
    loss = tot[PAY_LOSS[0], 0]
    dmod_all = pay_all[:, PAY_DMOD[0]:PAY_DMOD[1]].reshape(N_DEV, -1)[:, 0:DEPTH * nb * 3 * D_MODEL]
    dmod_all = dmod_all.reshape(N_DEV, DEPTH, nb, 3 * D_MODEL).transpose(1, 0, 2, 3)
    dmod_all = dmod_all.reshape(DEPTH, N_DEV * nb, 3 * D_MODEL)
    my_cols = lax.dynamic_slice_in_dim(dmod_all, chip * cols, cols, axis=2)
    g_w_ada, g_b_ada = _ada_bwd(c_all, my_cols, dmod_all)
    g_b_ada = g_b_ada[:, 0]

    g_uq = jnp.stack([_uq_from_padded(g["w_uq"]) for g in grads])
    g_ukv = jnp.stack([_ukv_from_padded(g["w_ukv"]) for g in grads])
    g_lr = jnp.concatenate([_pack_lowrank(g_uq[:, :, 144 * j:144 * (j + 1)], g_ukv[:, :, 192 * j:192 * (j + 1)])
                            for j in range(4)], axis=1)
    partials = [gw_in[0], gw_in[1], jnp.stack([_out_from_padded(g["w_out"]) for g in grads]), g_lr]
    shapes = [(p.shape[0], p.shape[2]) for p in partials]
    halves = [r // 2 for r in SHARD_ROWS]
    core_s = jnp.reshape(ic, (1,)).astype(jnp.int32)
    place_s = jnp.stack([chip, ic]).astype(jnp.int32)
    from_pair = _transfer("pair_reduce", partials, [((nl, 4 * h, nc), F32) for (nl, nc), h in zip(shapes, halves)],
                          _pair_reduce_plan(SHARD_ROWS))
    chip_sums = [_sum_pair(p, r, core_s, rc, "sum_pair%d" % i)
                 for i, (p, r, rc) in enumerate(zip(partials, from_pair, SHARD_ROWS))]
    from_chips = _transfer("chip_scatter", chip_sums, [((nl, 3 * h, nc), BF16) for (nl, nc), h in zip(shapes, halves)],
                           _chip_scatter_plan(SHARD_ROWS))
    reduced = [_sum_chips(s, r, place_s, rc, "sum_chips%d" % i)
               for i, (s, r, rc) in enumerate(zip(chip_sums, from_chips, SHARD_ROWS))]
    g_in_a, g_in_b, g_out_sh, g_lr_sh = _transfer("pair_share", reduced, [(r.shape, F32) for r in reduced],
                                                  _pair_share_plan(SHARD_ROWS), in_place=True)
    g_uq_sh, g_ukv_sh = _unpack_lowrank(g_lr_sh[0])

    def cols_first(a):
        return jnp.transpose(a, (2, 0, 1))

    g_in_t = _in_cols_first(g_in_a, g_in_b)
    upd_in = tuple(jnp.transpose(a, (1, 2, 0)) for a in _adamw(cols_first(w_in), g_in_t, cols_first(m_w_in),
                                                               cols_first(v_w_in), "adamw_in"))
    gw = (jnp.transpose(g_in_t, (1, 2, 0)), g_out_sh, g_uq_sh, g_ukv_sh)
    upd = [upd_in, _adamw(w_out, gw[1], m_w_out, v_w_out, "adamw_out"),
           _adamw(a_w_uq, gw[2], m_a_w_uq, v_a_w_uq, "adamw_uq"), _adamw(a_w_ukv, gw[3], m_a_w_ukv, v_a_w_ukv, "adamw_ukv")]
    dw, mw, vw = (tuple(u[i] for u in upd) for i in range(3))
    d_ada, m_ada, v_ada = _adamw(w_ada, g_w_ada, m_w_ada, v_w_ada, "adamw_ada")
    d_b, m_b, v_b = (a.reshape(DEPTH, 3 * D_MODEL) for a in _adamw(
        _rows(b_ada, 48)[None], _rows(g_b_ada, 48)[None], _rows(m_b_ada, 48)[None], _rows(v_b_ada, 48)[None],
        "adamw_b_ada"))

    def small_rows(ng, gq, gkv, rb, fb, fg):
        per_layer = [dict(norm_g=ng[l], gq=gq[l], gkv=gkv[l], rb8=jnp.pad(rb[l], ((0, 3), (0, 384 - N_REL))),
                          fb=jnp.pad(fb[l], (0, LANE - 5))) for l in range(DEPTH)]
        return _small_payload(per_layer, fg, jnp.zeros((), F32))

    w_s = small_rows(norm_g, a_q_norm_g, a_kv_norm_g, b_rel_bias, c_forget_b, final_g)
    m_s = small_rows(m_norm_g, m_a_q_norm_g, m_a_kv_norm_g, m_b_rel_bias, m_c_forget_b, m_final_g)
    v_s = small_rows(v_norm_g, v_a_q_norm_g, v_a_kv_norm_g, v_b_rel_bias, v_c_forget_b, v_final_g)
    d_s, mo_s, vo_s = (a[0] for a in _adamw(w_s[None], tot[None], m_s[None], v_s[None], "adamw_small"))
    gs = _payload_split(tot)
    ds = _payload_split(d_s)
    ms = _payload_split(mo_s)
    vs = _payload_split(vo_s)

    def ordered(ada, b, sm, big):
        ng, gq, gkv, rb, fb, fg = sm
        b_in, b_out, b_uq, b_ukv = big
        return (ada, b, ng, b_in, gq, b_uq, gkv, b_ukv, rb, fb, b_out, fg)

    return (loss, dx.reshape(nb, seq, D_MODEL), *ordered(g_w_ada, g_b_ada, gs, gw), *ordered(d_ada, d_b, ds, dw),
            *ordered(m_ada, m_b, ms, mw), *ordered(v_ada, v_b, vs, vw))
```

```python
import functools

import jax
import jax.numpy as jnp
from jax import lax
from jax.experimental import pallas as pl
from jax.experimental.pallas import tpu as pltpu

F32 = jnp.float32
BF16 = jnp.bfloat16

D_MODEL = 1024
DEPTH = 2
EPS = 1e-6
NEG = -1e30
LOG2E = 1.4426950408889634
ROPE_THETA = 10000.0
A_ROPE = 32
A_Q_RANK = 384
A_KV_RANK = 256
REL_CLIP = 128
N_REL = 2 * REL_CLIP + 1
N_IN = 3621

ADAM_LR = 0.001
ADAM_B1 = 0.9
ADAM_B2 = 0.999
ADAM_EPS = 1e-08
ADAM_WD = 0.01
ADAM_STEP = 10

LANE = 128
VMEM_LIMIT = 56 * 1024 * 1024

NP_IN = 4352
Z_A = (0, 768)
Z_G = (768, 1920)
Z_QKV = tuple((1920 + 384 * i, 1920 + 384 * (i + 1)) for i in range(6))
Z_F = (4224, 4352)
IN_PIECES = ((0, 672, 96), (672, 1056, 0), (2016, 2336, 64), (3301, 3621, 64), (1056, 1376, 64), (1376, 1696, 64),
             (1696, 2016, 64), (2336, 2656, 64), (2656, 2976, 64), (2976, 3296, 64), (3296, 3301, 123))
D_CAT = 1152

TM = 512
T_CAUSAL = 256
T_BAND = 128
BAND_TILES = 5
N_DEV = 8

PAY_DMOD = (0, 96)
PAY_NORM = (96, 112)
PAY_GQ = (112, 120)
PAY_GKV = (120, 128)
PAY_RB = (128, 176)
PAY_FB = (176, 184)
PAY_FINAL = (184, 192)
PAY_LOSS = (192, 200)
PAY_ROWS = 200

PACK_COLS = 1024
PACK_ROWS = 2560
HALF_ROWS = PACK_ROWS // 2


def _params(sem=None):
    return pltpu.CompilerParams(dimension_semantics=sem, vmem_limit_bytes=VMEM_LIMIT)


def _lane_iota(shape):
    return lax.broadcasted_iota(jnp.int32, shape, len(shape) - 1)


def _silu(u):
    return u * jax.nn.sigmoid(u)


def _dsilu(u):
    s = jax.nn.sigmoid(u)
    return s * (1.0 + u * (1.0 - s))


def _rms(x, g):
    r = lax.rsqrt(jnp.mean(x * x, axis=-1, keepdims=True) + EPS)
    xh = x * r
    return xh * g, xh, r


def _rms_bwd(dy, xh, r, g):
    dxh = dy * g
    return r * (dxh - xh * jnp.mean(dxh * xh, axis=-1, keepdims=True))


def _rope(x, cos, sina, sinb):
    return x * cos + pltpu.roll(x, 16, 1) * sinb + pltpu.roll(x, LANE - 16, 1) * sina


def _rope_t(dy, cos, sina, sinb):
    return dy * cos + pltpu.roll(dy * sinb, LANE - 16, 1) + pltpu.roll(dy * sina, 16, 1)


def _split3(x):
    hi = x.astype(BF16)
    r1 = x - hi.astype(F32)
    mid = r1.astype(BF16)
    lo = (r1 - mid.astype(F32)).astype(BF16)
    return hi, mid, lo


def _dot(a, b):
    return jnp.dot(a, b, preferred_element_type=F32)


def _dot_nt(a, b):
    return lax.dot_general(a, b, (((1,), (1,)), ((), ())), preferred_element_type=F32)


def _dot_tn(a, b):
    return lax.dot_general(a, b, (((0,), (0,)), ((), ())), preferred_element_type=F32)


def _row_spec(cols):
    return pl.BlockSpec((TM, cols), lambda i: (i, 0))


def _full_spec(shape):
    return pl.BlockSpec(shape, lambda i: (0,) * len(shape))


def _ex_spec(tiles_per_ex):
    return pl.BlockSpec((1, 1, D_MODEL), lambda i: (i // tiles_per_ex, 0, 0))


def _ln_inproj(x, shift, scale, g, w_in_p, seq):
    t = x.shape[0]

    def body(x_ref, sh_ref, sc_ref, g_ref, w_ref, h_ref, za_ref, zg_ref, q0, q1, q2, q3, q4, q5, zf_ref):
        n, _, _ = _rms(x_ref[...], g_ref[...])
        h = (n * (1.0 + sc_ref[0]) + sh_ref[0]).astype(BF16)
        h_ref[...] = h
        za_ref[...] = _dot(h, w_ref[:, Z_A[0]:Z_A[1]])
        zg_ref[...] = _dot(h, w_ref[:, Z_G[0]:Z_G[1]])
        for ref, (c0, c1) in zip((q0, q1, q2, q3, q4, q5), Z_QKV):
            ref[...] = _dot(h, w_ref[:, c0:c1]).astype(BF16)
        zf_ref[...] = _dot(h, w_ref[:, Z_F[0]:Z_F[1]])

    tpe = seq // TM
    shapes = [jax.ShapeDtypeStruct((t, D_MODEL), BF16), jax.ShapeDtypeStruct((t, 768), F32),
              jax.ShapeDtypeStruct((t, D_CAT), F32)]
    shapes += [jax.ShapeDtypeStruct((t, 384), BF16)] * 6 + [jax.ShapeDtypeStruct((t, LANE), F32)]
    return pl.pallas_call(
        body, name="ln_inproj", grid=(t // TM,),
        in_specs=[_row_spec(D_MODEL), _ex_spec(tpe), _ex_spec(tpe), _full_spec((1, D_MODEL)),
                  _full_spec((D_MODEL, NP_IN))],
        out_specs=[_row_spec(D_MODEL), _row_spec(768), _row_spec(D_CAT)] + [_row_spec(384)] * 6 + [_row_spec(LANE)],
        out_shape=shapes, compiler_params=_params(("parallel",)),
    )(x, shift, scale, g, w_in_p)


def _a_up(za, gq, gkv, w_uq_p, w_ukv_p, cos, sina, sinb):
    t = za.shape[0]

    def body(za_ref, gq_ref, gkv_ref, wq_ref, wkv_ref, cos_ref, sa_ref, sb_ref, q_ref, k_ref, v_ref):
        cos_t, sa, sb = cos_ref[...], sa_ref[...], sb_ref[...]
        cqn, _, _ = _rms(za_ref[:, 0:384], gq_ref[...])
        q = _dot(cqn.astype(BF16), wq_ref[...])
        ckvn, _, _ = _rms(za_ref[:, 384:640], gkv_ref[...])
        kv = _dot(ckvn.astype(BF16), wkv_ref[...])
        kpe = za_ref[:, 640:768]
        kpe = _rope(kpe + pltpu.roll(kpe, 32, 1), cos_t, sa, sb).astype(BF16)
        for p in range(3):
            q_ref[:, 256 * p:256 * p + 128] = q[:, 256 * p:256 * p + 128].astype(BF16)
            q_ref[:, 256 * p + 128:256 * p + 256] = _rope(q[:, 256 * p + 128:256 * p + 256], cos_t, sa, sb).astype(BF16)
            k_ref[:, 256 * p:256 * p + 128] = kv[:, 128 * p:128 * p + 128].astype(BF16)
            k_ref[:, 256 * p + 128:256 * p + 256] = kpe
        v_ref[...] = kv[:, 384:768].astype(BF16)

    return pl.pallas_call(
        body, name="a_up", grid=(t // TM,),
        in_specs=[_row_spec(768), _full_spec((1, 384)), _full_spec((1, 256)), _full_spec((384, 768)),
                  _full_spec((256, 768)), _row_spec(LANE), _row_spec(LANE), _row_spec(LANE)],
        out_specs=[_row_spec(768), _row_spec(768), _row_spec(384)],
        out_shape=[jax.ShapeDtypeStruct((t, 768), BF16), jax.ShapeDtypeStruct((t, 768), BF16),
                   jax.ShapeDtypeStruct((t, 384), BF16)],
        compiler_params=_params(("parallel",)),
    )(za, gq, gkv, w_uq_p, w_ukv_p, cos, sina, sinb)


def _tri(n, upper):
    r = lax.broadcasted_iota(jnp.int32, (n, n), 0)
    c = lax.broadcasted_iota(jnp.int32, (n, n), 1)
    return jnp.where((c >= r) if upper else (c <= r), 1.0, 0.0).astype(BF16)


def _forget_fwd(zf, fb, nb, seq):
    blk = 256

    def body(zf_ref, fb_ref, f_ref):
        tri = _tri(blk, False)
        live = _lane_iota((blk, LANE)) < 5
        carry = jnp.zeros((1, LANE), F32)
        for i in range(seq // blk):
            u = zf_ref[i * blk:(i + 1) * blk, :] + fb_ref[...]
            lf = jnp.where(live, jnp.minimum(u, 0.0) - jnp.log(1.0 + jnp.exp(-jnp.abs(u))), 0.0)
            hi, mid, lo = _split3(lf)
            f_ref[i * blk:(i + 1) * blk, :] = (_dot(tri, hi) + _dot(tri, mid) + _dot(tri, lo) + carry) * LOG2E
            carry = carry + jnp.sum(lf, axis=0, keepdims=True)

    return pl.pallas_call(
        body, name="forget_fwd", grid=(nb,),
        in_specs=[pl.BlockSpec((seq, LANE), lambda b: (b, 0)), pl.BlockSpec((1, LANE), lambda b: (0, 0))],
        out_specs=pl.BlockSpec((seq, LANE), lambda b: (b, 0)),
        out_shape=jax.ShapeDtypeStruct((nb * seq, LANE), F32), compiler_params=_params(("parallel",)),
    )(zf, fb)


def _forget_bwd(dfq, dfk, zf, fb, nb, seq):
    blk = 256

    def body(dfq_ref, dfk_ref, zf_ref, fb_ref, dz_ref, gb_ref):
        @pl.when(pl.program_id(0) == 0)
        def _():
            gb_ref[...] = jnp.zeros_like(gb_ref)

        tri = _tri(blk, True)
        lane = _lane_iota((blk, LANE))
        wide = _lane_iota((blk, 384))
        live = lane < 5
        carry = jnp.zeros((1, LANE), F32)
        gsum = jnp.zeros((1, LANE), F32)
        for i in reversed(range(seq // blk)):
            d = dfk_ref[i * blk:(i + 1) * blk, :]
            dq = dfq_ref[i * blk:(i + 1) * blk, :]
            for hd in range(5):
                col = jnp.sum(jnp.where(wide == 64 * hd, dq, 0.0), axis=-1, keepdims=True)
                d = d + jnp.where(lane == hd, col, 0.0)
            hi, mid, lo = _split3(d)
            dlf = _dot(tri, hi) + _dot(tri, mid) + _dot(tri, lo) + carry
            carry = carry + jnp.sum(d, axis=0, keepdims=True)
            u = zf_ref[i * blk:(i + 1) * blk, :] + fb_ref[...]
            du = jnp.where(live, dlf * jax.nn.sigmoid(-u), 0.0)
            dz_ref[i * blk:(i + 1) * blk, :] = du.astype(BF16)
            gsum = gsum + jnp.sum(du, axis=0, keepdims=True)
        gb_ref[...] += jnp.broadcast_to(gsum, gb_ref.shape)

    return pl.pallas_call(
        body, name="forget_bwd", grid=(nb,),
        in_specs=[pl.BlockSpec((seq, 384), lambda b: (b, 0)), pl.BlockSpec((seq, LANE), lambda b: (b, 0)),
                  pl.BlockSpec((seq, LANE), lambda b: (b, 0)), pl.BlockSpec((1, LANE), lambda b: (0, 0))],
        out_specs=[pl.BlockSpec((seq, LANE), lambda b: (b, 0)), pl.BlockSpec((8, LANE), lambda b: (0, 0))],
        out_shape=[jax.ShapeDtypeStruct((nb * seq, LANE), BF16), jax.ShapeDtypeStruct((8, LANE), F32)],
        compiler_params=_params(("arbitrary",)),
    )(dfq, dfk, zf, fb)


def _gate_outproj(x, gate, oa, ob, oc, zg, w_out_p, seq):
    t = x.shape[0]

    def body(x_ref, gate_ref, oa_ref, ob_ref, oc_ref, zg_ref, w_ref, y_ref, xn_ref):
        y = jnp.zeros((TM, D_MODEL), F32)
        for i, o_ref in enumerate((oa_ref, ob_ref, oc_ref)):
            cat = (o_ref[...] * _silu(zg_ref[:, 384 * i:384 * (i + 1)])).astype(BF16)
            y = y + _dot(cat, w_ref[384 * i:384 * (i + 1), :])
        y_ref[...] = y
        xn_ref[...] = x_ref[...] + gate_ref[0] * y

    return pl.pallas_call(
        body, name="gate_outproj", grid=(t // TM,),
        in_specs=[_row_spec(D_MODEL), _ex_spec(seq // TM), _row_spec(384), _row_spec(384), _row_spec(384),
                  _row_spec(D_CAT), _full_spec((D_CAT, D_MODEL))],
        out_specs=[_row_spec(D_MODEL), _row_spec(D_MODEL)],
        out_shape=[jax.ShapeDtypeStruct((t, D_MODEL), F32)] * 2, compiler_params=_params(("parallel",)),
    )(x, gate, oa, ob, oc, zg, w_out_p)


def _final_loss(x, target, g):
    t = x.shape[0]

    def body(x_ref, t_ref, g_ref, dx_ref, loss_ref, gg_ref):
        @pl.when(pl.program_id(0) == 0)
        def _():
            loss_ref[...] = jnp.zeros_like(loss_ref)
            gg_ref[...] = jnp.zeros_like(gg_ref)

        gv = g_ref[...]
        out, xh, r = _rms(x_ref[...], gv)
        err = out - t_ref[...]
        loss_ref[...] += 0.5 * jnp.sum(jnp.mean(err * err, axis=-1, keepdims=True), axis=0, keepdims=True)
        dout = err / D_MODEL
        gg_ref[...] += jnp.broadcast_to(jnp.sum(dout * xh, axis=0, keepdims=True), gg_ref.shape)
        dx_ref[...] = _rms_bwd(dout, xh, r, gv)

    return pl.pallas_call(
        body, name="final_loss", grid=(t // TM,),
        in_specs=[_row_spec(D_MODEL), _row_spec(D_MODEL), _full_spec((1, D_MODEL))],
        out_specs=[_row_spec(D_MODEL), _full_spec((8, LANE)), _full_spec((8, D_MODEL))],
        out_shape=[jax.ShapeDtypeStruct((t, D_MODEL), F32), jax.ShapeDtypeStruct((8, LANE), F32),
                   jax.ShapeDtypeStruct((8, D_MODEL), F32)],
        compiler_params=_params(("arbitrary",)),
    )(x, target, g)


def _head_masks(kind, rows, dq, h):
    lq = _lane_iota((rows, dq))
    lv = _lane_iota((rows, LANE))
    mq = (lq >= 64 * h) & (lq < 64 * h + 64)
    if kind == "A":
        mq = mq | ((lq >= 128 + 32 * h) & (lq < 160 + 32 * h))
    return mq, (lv >= 64 * h) & (lv < 64 * h + 64)


def _tile_mask(kind, tile, m=None):
    row = lax.broadcasted_iota(jnp.int32, (tile, tile), 0)
    col = lax.broadcasted_iota(jnp.int32, (tile, tile), 1)
    if kind == "A":
        return (col >> 6) <= (row >> 6)
    if kind == "C":
        return col <= row
    first = (m == 0) & (row >= 64) & (col < 64)
    last = (m == BAND_TILES - 1) & (row < 64) & (col >= 64)
    return jnp.logical_not(first | last)


def _attn_scale(kind):
    return 96.0 ** -0.5 if kind == "A" else 0.125


def _attn_fwd(kind, q, k, v, aux, nb, seq):
    dq = q.shape[1] // 3
    tile = T_BAND if kind == "B" else T_CAUSAL
    nq = seq // tile
    scale = _attn_scale(kind)

    def body(*refs):
        if kind == "A":
            q_ref, k_ref, v_ref, o_ref, lse_ref = refs
            aux_ref = None
        else:
            q_ref, k_ref, v_ref, aux_ref, o_ref, lse_ref = refs
        qi = pl.program_id(2)
        q2 = q_ref[...]
        res = []
        for h in range(2):
            mq, _ = _head_masks(kind, tile, dq, h)
            qh = jnp.where(mq, q2, jnp.zeros_like(q2))

            def step(kj, carry, m=None, diag=False):
                mx, l, acc = carry
                ks = pl.ds(pl.multiple_of(kj * tile, tile), tile)
                kt = k_ref[ks, :]
                vt = v_ref[ks, :]
                s = _dot_nt(qh, kt) * scale
                if kind == "B":
                    s = jnp.where(_tile_mask("B", tile, m), s + aux_ref[h, m], NEG)
                if kind == "C":
                    s = s - aux_ref[0, h, pl.ds(kj, 1), :]
                if diag:
                    s = jnp.where(_tile_mask(kind, tile), s, NEG)
                mn = jnp.maximum(mx, jnp.max(s, axis=-1, keepdims=True))
                alpha = jnp.exp(mx - mn)
                p = jnp.exp(s - mn)
                l = alpha * l + jnp.sum(p, axis=-1, keepdims=True)
                acc = alpha * acc + _dot(p.astype(BF16), vt)
                return mn, l, acc

            init = (jnp.full((tile, 1), NEG, F32), jnp.zeros((tile, 1), F32), jnp.zeros((tile, LANE), F32))
            if kind == "B":
                m0 = jnp.maximum(BAND_TILES - 1 - qi, 0)
                mx, l, acc = lax.fori_loop(m0, BAND_TILES, lambda m, c: step(qi - (BAND_TILES - 1) + m, c, m=m), init)
            else:
                carry = lax.fori_loop(0, qi, lambda kj, c: step(kj, c), init)
                mx, l, acc = step(qi, carry, diag=True)
            res.append((acc / l, mx + jnp.log(l)))
        first = _lane_iota((tile, LANE)) < 64
        o_ref[...] = jnp.where(first, res[0][0], res[1][0])
        lse_ref[...] = jnp.where(first, res[0][1], res[1][1])

    in_specs = [pl.BlockSpec((tile, dq), lambda b, p, i: (b * nq + i, p)),
                pl.BlockSpec((seq, dq), lambda b, p, i: (b, p)),
                pl.BlockSpec((seq, LANE), lambda b, p, i: (b, p))]
    args = [q, k, v]
    if kind == "B":
        in_specs.append(pl.BlockSpec((2, BAND_TILES, tile, tile), lambda b, p, i: (p, 0, 0, 0)))
        args.append(aux)
    if kind == "C":
        in_specs.append(pl.BlockSpec((1, 2, nq, tile), lambda b, p, i: (b, p, 0, 0)))
        args.append(aux)
    out_spec = pl.BlockSpec((tile, LANE), lambda b, p, i: (b * nq + i, p))
    return pl.pallas_call(
        body, name="attn_fwd_" + kind, grid=(nb, 3, nq), in_specs=in_specs, out_specs=[out_spec, out_spec],
        out_shape=[jax.ShapeDtypeStruct((nb * seq, 384), F32)] * 2,
        compiler_params=_params(("parallel", "parallel", "parallel")),
    )(*args)


def _attn_bwd(kind, q, k, v, o, do, lse, aux, nb, seq):
    dq = q.shape[1] // 3
    tile = T_BAND if kind == "B" else T_CAUSAL
    nq = seq // tile
    scale = _attn_scale(kind)
    dqk_dtype = F32 if kind == "A" else BF16

    def body(*refs):
        dfr_ref = dfq_ref = dbt_ref = aux_ref = None
        if kind == "A":
            q_ref, k_ref, v_ref, o_ref, do_ref, lse_ref, dq_ref, dk_ref, dv_ref, dk_acc, dv_acc = refs
        elif kind == "B":
            q_ref, k_ref, v_ref, o_ref, do_ref, lse_ref, aux_ref, dq_ref, dk_ref, dv_ref, dbt_ref, dk_acc, dv_acc = refs
        else:
            (q_ref, k_ref, v_ref, o_ref, do_ref, lse_ref, aux_ref, dq_ref, dk_ref, dv_ref, dfr_ref, dfq_ref,
             dk_acc, dv_acc) = refs
        dk_acc[...] = jnp.zeros_like(dk_acc)
        dv_acc[...] = jnp.zeros_like(dv_acc)
        if kind == "C":
            dfr_ref[...] = jnp.zeros_like(dfr_ref)
        if kind == "B":
            @pl.when(pl.program_id(1) == 0)
            def _():
                dbt_ref[...] = jnp.zeros_like(dbt_ref)

        def q_step(qi, _):
            qs = pl.ds(pl.multiple_of(qi * tile, tile), tile)
            q2 = q_ref[qs, :]
            do2 = do_ref[qs, :]
            o2 = o_ref[qs, :]
            lse2 = lse_ref[qs, :]
            dq_tot = jnp.zeros((tile, dq), F32)
            row_sums = []
            for h in range(2):
                mq, mv = _head_masks(kind, tile, dq, h)
                qh = jnp.where(mq, q2, jnp.zeros_like(q2))
                doh = jnp.where(mv, do2, 0.0)
                dob = doh.astype(BF16)
                delta = jnp.sum(doh * o2, axis=-1, keepdims=True)
                lseh = jnp.max(jnp.where(mv, lse2, NEG), axis=-1, keepdims=True)

                def step(kj, carry, m=None, diag=False):
                    dq_t, rs = carry
                    ks = pl.ds(pl.multiple_of(kj * tile, tile), tile)
                    kt = k_ref[ks, :]
                    vt = v_ref[ks, :]
                    s = _dot_nt(qh, kt) * scale
                    if kind == "B":
                        s = jnp.where(_tile_mask("B", tile, m), s + aux_ref[h, m], NEG)
                    if kind == "C":
                        s = s - aux_ref[0, h, pl.ds(kj, 1), :]
                    if diag:
                        s = jnp.where(_tile_mask(kind, tile), s, NEG)
                    p = jnp.exp(s - lseh)
                    ds = p * (_dot_nt(dob, vt) - delta)
                    if kind == "B":
                        dbt_ref[h, m] += ds
                    if kind == "C":
                        dfr_ref[0, h, pl.ds(kj, 1), :] -= jnp.sum(ds, axis=0, keepdims=True)
                        rs = rs + jnp.sum(ds, axis=-1, keepdims=True)
                    dss = (ds * scale).astype(BF16)
                    dv_acc[ks, :] += _dot_tn(p.astype(BF16), dob)
                    dk_acc[ks, :] += _dot_tn(dss, qh)
                    return dq_t + jnp.where(mq, _dot(dss, kt), 0.0), rs

                zero = (jnp.zeros((tile, dq), F32), jnp.zeros((tile, 1), F32))
                if kind == "B":
                    m0 = jnp.maximum(BAND_TILES - 1 - qi, 0)
                    dq_h, rs_h = lax.fori_loop(m0, BAND_TILES, lambda m, c: step(qi - (BAND_TILES - 1) + m, c, m=m), zero)
                else:
                    carry = lax.fori_loop(0, qi, lambda kj, c: step(kj, c), zero)
                    dq_h, rs_h = step(qi, carry, diag=True)
                dq_tot = dq_tot + dq_h
                row_sums.append(rs_h)
            dq_ref[qs, :] = dq_tot.astype(dqk_dtype)
            if kind == "C":
                dfq_ref[qs, :] = jnp.where(_lane_iota((tile, LANE)) < 64, row_sums[0], row_sums[1])
            return 0

        lax.fori_loop(0, nq, q_step, 0)
        dk_ref[...] = dk_acc[...].astype(dqk_dtype)
        dv_ref[...] = dv_acc[...].astype(BF16)

    def seq_spec(cols):
        return pl.BlockSpec((seq, cols), lambda p, b: (b, p))

    in_specs = [seq_spec(dq), seq_spec(dq), seq_spec(LANE), seq_spec(LANE), seq_spec(LANE), seq_spec(LANE)]
    args = [q, k, v, o, do, lse]
    out_specs = [seq_spec(dq), seq_spec(dq), seq_spec(LANE)]
    out_shape = [jax.ShapeDtypeStruct((nb * seq, 3 * dq), dqk_dtype)] * 2 + [jax.ShapeDtypeStruct((nb * seq, 384), BF16)]
    if kind == "B":
        spec = pl.BlockSpec((2, BAND_TILES, tile, tile), lambda p, b: (p, 0, 0, 0))
        in_specs.append(spec)
        args.append(aux)
        out_specs.append(spec)
        out_shape.append(jax.ShapeDtypeStruct((6, BAND_TILES, tile, tile), F32))
    if kind == "C":
        spec = pl.BlockSpec((1, 2, nq, tile), lambda p, b: (b, p, 0, 0))
        in_specs.append(spec)
        args.append(aux)
        out_specs += [spec, seq_spec(LANE)]
        out_shape += [jax.ShapeDtypeStruct((nb, 6, nq, tile), F32), jax.ShapeDtypeStruct((nb * seq, 384), F32)]
    return pl.pallas_call(
        body, name="attn_bwd_" + kind, grid=(3, nb), in_specs=in_specs, out_specs=out_specs, out_shape=out_shape,
        scratch_shapes=[pltpu.VMEM((seq, dq), F32), pltpu.VMEM((seq, LANE), F32)],
        compiler_params=_params(("arbitrary", "arbitrary")),
    )(*args)


BAND_W = BAND_TILES * T_BAND


def _segments(kind, qi, tile):
    r0 = qi * tile
    if kind == "B":
        lo = max(qi - (BAND_TILES - 1), 0) * tile
        return [(lo, r0 + tile, False, lo - (qi - (BAND_TILES - 1)) * tile)]
    return ([(0, r0, False, 0)] if qi else []) + [(r0, r0 + tile, True, 0)]


def _scores(kind, qh, k_ref, aux_ref, h, seg, tile, scale):
    a, b, diag, c0 = seg
    s = _dot_nt(qh, k_ref[a:b, :]) * (scale * LOG2E)
    if kind == "B":
        return s + aux_ref[h, :, c0:BAND_W]
    if kind == "C":
        s = s - aux_ref[0, h, :, a:b]
    if diag:
        s = jnp.where(_tile_mask(kind, tile), s, NEG)
    return s


def _second_head(kind, pair, fn):
    del kind, pair
    fn()


FWD_AHEAD = 2
BWD_AHEAD = 2


def _run_ahead(units, first, second, depth):
    queue = [first(*u) for u in units[:depth]]
    for i, (_, h) in enumerate(units):
        if i + depth < len(units):
            queue.append(first(*units[i + depth]))
        second(h, *queue.pop(0))


def _attn_fwd(kind, q, k, v, aux, nb, seq):
    dq = q.shape[1] // 3
    tile = T_BAND if kind == "B" else T_CAUSAL
    nq = seq // tile
    scale = _attn_scale(kind)

    def body(*refs):
        if kind == "A":
            q_ref, k_ref, v_ref, o_ref, lse_ref = refs
            aux_ref = None
        else:
            q_ref, k_ref, v_ref, aux_ref, o_ref, lse_ref = refs
        def logits(qi, h):
            rows = slice(qi * tile, (qi + 1) * tile)
            q2 = q_ref[rows, :]
            mq, _ = _head_masks(kind, tile, dq, h)
            qh = jnp.where(mq, q2, jnp.zeros_like(q2))
            segs = _segments(kind, qi, tile)
            return rows, segs, [_scores(kind, qh, k_ref, aux_ref, h, seg, tile, scale) for seg in segs]

        def finish(h, rows, segs, ss):
            _, mv = _head_masks(kind, tile, dq, h)
            mx = functools.reduce(jnp.maximum, [jnp.max(s, axis=-1, keepdims=True) for s in ss])
            ps = [jnp.exp2(s - mx) for s in ss]
            l = functools.reduce(jnp.add, [jnp.sum(p, axis=-1, keepdims=True) for p in ps])
            acc = functools.reduce(jnp.add, [_dot(p.astype(BF16), v_ref[seg[0]:seg[1], :]) for p, seg in zip(ps, segs)])
            o_h = jnp.where(mv, acc / l, 0.0)
            lse_h = jnp.where(mv, mx + jnp.log(l) * LOG2E, 0.0)
            if h == 0:
                o_ref[rows, :] = o_h
                lse_ref[rows, :] = lse_h
            else:
                o_ref[rows, :] += o_h
                lse_ref[rows, :] += lse_h

        _run_ahead([(qi, h) for qi in range(nq) for h in range(2)], logits, finish, FWD_AHEAD)

    def seq_spec(cols):
        return pl.BlockSpec((seq, cols), lambda b, p: (b, p))

    in_specs = [seq_spec(dq), seq_spec(dq), seq_spec(LANE)]
    args = [q, k, v]
    if kind == "B":
        in_specs.append(pl.BlockSpec((2, tile, BAND_W), lambda b, p: (p, 0, 0)))
        args.append(aux)
    if kind == "C":
        in_specs.append(pl.BlockSpec((1, 2, 1, seq), lambda b, p: (b, p, 0, 0)))
        args.append(aux)
    return pl.pallas_call(
        body, name="attn_fwd_" + kind, grid=(nb, 3), in_specs=in_specs, out_specs=[seq_spec(LANE), seq_spec(LANE)],
        out_shape=[jax.ShapeDtypeStruct((nb * seq, 384), F32)] * 2, compiler_params=_params(("parallel", "parallel")),
    )(*args)


def _attn_bwd(kind, q, k, v, o, do, lse, aux, nb, seq):
    dq = q.shape[1] // 3
    tile = T_BAND if kind == "B" else T_CAUSAL
    nq = seq // tile
    scale = _attn_scale(kind)
    dqk_dtype = F32 if kind == "A" else BF16

    def body(*refs):
        dfr_ref = dfq_ref = dbt_ref = aux_ref = None
        if kind == "A":
            q_ref, k_ref, v_ref, o_ref, do_ref, lse_ref, dq_ref, dk_ref, dv_ref, dkt_acc, dvt_acc = refs
        elif kind == "B":
            q_ref, k_ref, v_ref, o_ref, do_ref, lse_ref, aux_ref, dq_ref, dk_ref, dv_ref, dbt_ref, dkt_acc, dvt_acc = refs
        else:
            (q_ref, k_ref, v_ref, o_ref, do_ref, lse_ref, aux_ref, dq_ref, dk_ref, dv_ref, dfr_ref, dfq_ref,
             dkt_acc, dvt_acc) = refs
        dkt_acc[...] = jnp.zeros_like(dkt_acc)
        dvt_acc[...] = jnp.zeros_like(dvt_acc)
        if kind == "C":
            dfr_ref[...] = jnp.zeros_like(dfr_ref)
        if kind == "B":
            @pl.when(pl.program_id(1) == 0)
            def _():
                dbt_ref[...] = jnp.zeros_like(dbt_ref)

        def products(qi, h):
            rows = slice(qi * tile, (qi + 1) * tile)
            q2 = q_ref[rows, :]
            mq, mv = _head_masks(kind, tile, dq, h)
            qh = jnp.where(mq, q2, jnp.zeros_like(q2))
            doh = jnp.where(mv, do_ref[rows, :], 0.0)
            dob = doh.astype(BF16)
            segs = _segments(kind, qi, tile)
            ts = [_scores(kind, qh, k_ref, aux_ref, h, seg, tile, scale) for seg in segs]
            dps = [_dot_nt(dob, v_ref[seg[0]:seg[1], :]) for seg in segs]
            return rows, segs, qh, doh, ts, dps

        def finish(h, rows, segs, qh, doh, ts, dps):
            mq, mv = _head_masks(kind, tile, dq, h)
            qht = qh.astype(F32).T.astype(BF16)
            dobt = doh.T.astype(BF16)
            head_rows = [(64 * h, 64)] + ([(128 + 32 * h, 32)] if kind == "A" else [])
            delta = jnp.sum(doh * o_ref[rows, :], axis=-1, keepdims=True)
            lseh = jnp.max(jnp.where(mv, lse_ref[rows, :], NEG), axis=-1, keepdims=True)
            rs = jnp.zeros((tile, 1), F32)
            dq_h = jnp.zeros((tile, dq), F32)
            for (a, b, _, c0), t, dp in zip(segs, ts, dps):
                p = jnp.exp2(t - lseh)
                ds = p * (dp - delta)
                if kind == "B":
                    dbt_ref[h, :, c0:BAND_W] += ds
                if kind == "C":
                    dfr_ref[0, h, :, a:b] -= jnp.sum(ds, axis=0, keepdims=True)
                    rs = rs + jnp.sum(ds, axis=-1, keepdims=True)
                dss = (ds * scale).astype(BF16)
                dvt_acc[64 * h:64 * h + 64, a:b] += _dot(dobt[64 * h:64 * h + 64, :], p.astype(BF16))
                for r0, n in head_rows:
                    dkt_acc[r0:r0 + n, a:b] += _dot(qht[r0:r0 + n, :], dss)
                dq_h = dq_h + _dot(dss, k_ref[a:b, :])
            dq_h = jnp.where(mq, dq_h, 0.0).astype(dqk_dtype)
            if h == 0:
                dq_ref[rows, :] = dq_h
            else:
                dq_ref[rows, :] += dq_h
            if kind == "C":
                if h == 0:
                    dfq_ref[rows, :] = jnp.where(mv, rs, 0.0)
                else:
                    dfq_ref[rows, :] += jnp.where(mv, rs, 0.0)

        _run_ahead([(qi, h) for qi in range(nq) for h in range(2)], products, finish, BWD_AHEAD)
        for j in range(seq // 256):
            cols = slice(256 * j, 256 * (j + 1))
            dk_ref[cols, :] = dkt_acc[:, cols].T.astype(dqk_dtype)
            dv_ref[cols, :] = dvt_acc[:, cols].T.astype(BF16)

    def seq_spec(cols):
        return pl.BlockSpec((seq, cols), lambda p, b: (b, p))

    in_specs = [seq_spec(dq), seq_spec(dq), seq_spec(LANE), seq_spec(LANE), seq_spec(LANE), seq_spec(LANE)]
    args = [q, k, v, o, do, lse]
    out_specs = [seq_spec(dq), seq_spec(dq), seq_spec(LANE)]
    out_shape = [jax.ShapeDtypeStruct((nb * seq, 3 * dq), dqk_dtype)] * 2 + [jax.ShapeDtypeStruct((nb * seq, 384), BF16)]
    if kind == "B":
        spec = pl.BlockSpec((2, tile, BAND_W), lambda p, b: (p, 0, 0))
        in_specs.append(spec)
        args.append(aux)
        out_specs.append(spec)
        out_shape.append(jax.ShapeDtypeStruct((6, tile, BAND_W), F32))
    if kind == "C":
        spec = pl.BlockSpec((1, 2, 1, seq), lambda p, b: (b, p, 0, 0))
        in_specs.append(spec)
        args.append(aux)
        out_specs += [spec, seq_spec(LANE)]
        out_shape += [jax.ShapeDtypeStruct((nb, 6, 1, seq), F32), jax.ShapeDtypeStruct((nb * seq, 384), F32)]
    return pl.pallas_call(
        body, name="attn_bwd_" + kind, grid=(3, nb), in_specs=in_specs, out_specs=out_specs, out_shape=out_shape,
        scratch_shapes=[pltpu.VMEM((dq, seq), F32), pltpu.VMEM((LANE, seq), F32)],
        compiler_params=_params(("arbitrary", "arbitrary")),
    )(*args)


BIAS_FLAT = T_BAND * BAND_W
BIAS_CHUNK = 4 * BAND_W


def _rel_onehot(chunk):
    lane = lax.broadcasted_iota(jnp.int32, (384, chunk), 1)
    r = lax.broadcasted_iota(jnp.int32, (384, chunk), 0)
    sub = jnp.where(lane >= BAND_W, 1, 0) + jnp.where(lane >= 2 * BAND_W, 1, 0) + jnp.where(lane >= 3 * BAND_W, 1, 0)
    i = pl.program_id(0) * 4 + sub
    col = lane - sub * BAND_W
    idx = jnp.clip(BAND_W - T_BAND + i - col, -REL_CLIP, REL_CLIP) + REL_CLIP
    return jnp.where(idx == r, 1.0, 0.0).astype(BF16)


BIAS_G = 768
BIAS_EDGE = BIAS_G - N_REL


def _bias_line(rel_bias):
    g = jnp.concatenate([jnp.broadcast_to(rel_bias[:, N_REL - 1:], (rel_bias.shape[0], BIAS_EDGE)),
                         jnp.flip(rel_bias, axis=1)], axis=1)
    return jnp.pad(g, ((0, 8 - g.shape[0]), (0, 0)))


def _bias_unline(dg):
    return jnp.flip(dg[:, BIAS_EDGE:], axis=1)


def _bias_expand(g8):
    def body(g_ref, out_ref):
        line = jnp.broadcast_to(g_ref[0] * LOG2E, (T_BAND, BIAS_G))
        slab = pltpu.roll(line, 1, 1, stride=1, stride_axis=0)[:, LANE:BIAS_G]
        row = lax.broadcasted_iota(jnp.int32, (T_BAND, BAND_W), 0)
        col = lax.broadcasted_iota(jnp.int32, (T_BAND, BAND_W), 1)
        hidden = ((row >= 64) & (col < 64)) | ((row < 64) & (col >= BAND_W - 64))
        out_ref[0] = jnp.where(hidden, NEG, slab)

    return pl.pallas_call(
        body, name="bias_expand", grid=(8,), in_specs=[pl.BlockSpec((1, 1, BIAS_G), lambda h: (h, 0, 0))],
        out_specs=pl.BlockSpec((1, T_BAND, BAND_W), lambda h: (h, 0, 0)),
        out_shape=jax.ShapeDtypeStruct((8, T_BAND, BAND_W), F32), compiler_params=_params(("parallel",)),
    )(g8.reshape(8, 1, BIAS_G))


def _bias_reduce(d_slab):
    def body(d_ref, out_ref):
        r = lax.broadcasted_iota(jnp.int32, (T_BAND, T_BAND), 0)
        k = lax.broadcasted_iota(jnp.int32, (T_BAND, T_BAND), 1)
        flip = jnp.where(r + k == T_BAND - 1, 1.0, 0.0).astype(BF16)
        hi, mid, lo = _split3(d_ref[0])
        d_rev = _dot(flip, hi) + _dot(flip, mid) + _dot(flip, lo)
        wide = jnp.concatenate([jnp.zeros((T_BAND, LANE), F32), d_rev, jnp.zeros((T_BAND, 2 * LANE), F32)], axis=1)
        skew = pltpu.roll(wide, 0, 1, stride=1, stride_axis=0)
        dg = jnp.sum(skew, axis=0, keepdims=True)[:, LANE:LANE + BIAS_G]
        lane = _lane_iota((1, BIAS_G))
        clipped = jnp.sum(jnp.where(lane <= BIAS_EDGE, dg, 0.0), axis=1, keepdims=True)
        out_ref[0] = jnp.where(lane == BIAS_EDGE, clipped, dg)

    return pl.pallas_call(
        body, name="bias_reduce", grid=(8,), in_specs=[pl.BlockSpec((1, T_BAND, BAND_W), lambda h: (h, 0, 0))],
        out_specs=pl.BlockSpec((1, 1, BIAS_G), lambda h: (h, 0, 0)),
        out_shape=jax.ShapeDtypeStruct((8, 1, BIAS_G), F32), compiler_params=_params(("parallel",)),
    )(d_slab).reshape(8, BIAS_G)


def _outproj_bwd(dxn, y, gate, oa, ob, oc, zg, w_out_p, w_out_pt, nb, seq):
    t = dxn.shape[0]
    tpe = seq // TM

    def body(dxn_ref, y_ref, gate_ref, oa_ref, ob_ref, oc_ref, zg_ref, w_ref, wt_ref,
             doa_ref, dob_ref, doc_ref, dzg_ref, gw_ref, dgate_ref):
        i = pl.program_id(0)

        @pl.when(i == 0)
        def _():
            gw_ref[...] = jnp.zeros_like(gw_ref)

        @pl.when(i % tpe == 0)
        def _():
            dgate_ref[...] = jnp.zeros_like(dgate_ref)

        dxn_t = dxn_ref[...]
        dgate_ref[0] += jnp.sum(dxn_t * y_ref[...], axis=0, keepdims=True)
        dy = (dxn_t * gate_ref[0]).astype(BF16)
        for gi, (o_ref, do_ref) in enumerate(((oa_ref, doa_ref), (ob_ref, dob_ref), (oc_ref, doc_ref))):
            cols = slice(384 * gi, 384 * (gi + 1))
            u = zg_ref[:, cols]
            o_t = o_ref[...]
            su = _silu(u)
            dcat = _dot(dy, wt_ref[:, cols])
            do_ref[...] = dcat * su
            dzg_ref[:, cols] = (dcat * o_t * _dsilu(u)).astype(BF16)
            gw_ref[cols, :] += _dot_tn((o_t * su).astype(BF16), dy)

    return pl.pallas_call(
        body, name="outproj_bwd", grid=(t // TM,),
        in_specs=[_row_spec(D_MODEL), _row_spec(D_MODEL), _ex_spec(tpe), _row_spec(384), _row_spec(384), _row_spec(384),
                  _row_spec(D_CAT), _full_spec((D_CAT, D_MODEL)), _full_spec((D_MODEL, D_CAT))],
        out_specs=[_row_spec(384), _row_spec(384), _row_spec(384), _row_spec(D_CAT), _full_spec((D_CAT, D_MODEL)),
                   _ex_spec(tpe)],
        out_shape=[jax.ShapeDtypeStruct((t, 384), F32)] * 3 + [jax.ShapeDtypeStruct((t, D_CAT), BF16),
                                                                jax.ShapeDtypeStruct((D_CAT, D_MODEL), F32),
                                                                jax.ShapeDtypeStruct((nb, 1, D_MODEL), F32)],
        compiler_params=_params(("arbitrary",)),
    )(dxn, y, gate, oa, ob, oc, zg, w_out_p, w_out_pt)


def _a_up_bwd(dqa, dka, dva, za, gq, gkv, w_uq_pt, w_ukv_pt, cos, sina, sinb):
    t = za.shape[0]

    def body(dq_ref, dk_ref, dv_ref, za_ref, gq_ref, gkv_ref, wqt_ref, wkvt_ref, cos_ref, sa_ref, sb_ref,
             dza_ref, gwq_ref, gwkv_ref, ggq_ref, ggkv_ref, dqb, dkvb):
        @pl.when(pl.program_id(0) == 0)
        def _():
            gwq_ref[...] = jnp.zeros_like(gwq_ref)
            gwkv_ref[...] = jnp.zeros_like(gwkv_ref)
            ggq_ref[...] = jnp.zeros_like(ggq_ref)
            ggkv_ref[...] = jnp.zeros_like(ggkv_ref)

        cos_t, sa, sb = cos_ref[...], sa_ref[...], sb_ref[...]
        dkpe = jnp.zeros((TM, LANE), F32)
        for p in range(3):
            dqb[:, 256 * p:256 * p + 128] = dq_ref[:, 256 * p:256 * p + 128].astype(BF16)
            dqb[:, 256 * p + 128:256 * p + 256] = _rope_t(dq_ref[:, 256 * p + 128:256 * p + 256], cos_t, sa, sb).astype(BF16)
            dkvb[:, 128 * p:128 * p + 128] = dk_ref[:, 256 * p:256 * p + 128].astype(BF16)
            dkpe = dkpe + dk_ref[:, 256 * p + 128:256 * p + 256]
        dkvb[:, 384:768] = dv_ref[...]
        dkpe = _rope_t(dkpe, cos_t, sa, sb)
        dkpe = jnp.where(_lane_iota((TM, LANE)) < A_ROPE, dkpe + pltpu.roll(dkpe, LANE - 32, 1), 0.0)

        gqv = gq_ref[...]
        cqn, cqh, rq = _rms(za_ref[:, 0:384], gqv)
        dq_t = dqb[...]
        gwq_ref[...] += _dot_tn(cqn.astype(BF16), dq_t)
        dcqn = _dot(dq_t, wqt_ref[...])
        ggq_ref[...] += jnp.broadcast_to(jnp.sum(dcqn * cqh, axis=0, keepdims=True), ggq_ref.shape)
        dza_ref[:, 0:384] = _rms_bwd(dcqn, cqh, rq, gqv).astype(BF16)

        gkvv = gkv_ref[...]
        ckvn, ckvh, rkv = _rms(za_ref[:, 384:640], gkvv)
        dkv_t = dkvb[...]
        gwkv_ref[...] += _dot_tn(ckvn.astype(BF16), dkv_t)
        dckvn = _dot(dkv_t, wkvt_ref[...])
        ggkv_ref[...] += jnp.broadcast_to(jnp.sum(dckvn * ckvh, axis=0, keepdims=True), ggkv_ref.shape)
        dza_ref[:, 384:640] = _rms_bwd(dckvn, ckvh, rkv, gkvv).astype(BF16)
        dza_ref[:, 640:768] = dkpe.astype(BF16)

    return pl.pallas_call(
        body, name="a_up_bwd", grid=(t // TM,),
        in_specs=[_row_spec(768), _row_spec(768), _row_spec(384), _row_spec(768), _full_spec((1, 384)),
                  _full_spec((1, 256)), _full_spec((768, 384)), _full_spec((768, 256)), _row_spec(LANE), _row_spec(LANE),
                  _row_spec(LANE)],
        out_specs=[_row_spec(768), _full_spec((384, 768)), _full_spec((256, 768)), _full_spec((8, 384)),
                   _full_spec((8, 256))],
        out_shape=[jax.ShapeDtypeStruct((t, 768), BF16), jax.ShapeDtypeStruct((384, 768), F32),
                   jax.ShapeDtypeStruct((256, 768), F32), jax.ShapeDtypeStruct((8, 384), F32),
                   jax.ShapeDtypeStruct((8, 256), F32)],
        scratch_shapes=[pltpu.VMEM((TM, 768), BF16), pltpu.VMEM((TM, 768), BF16)],
        compiler_params=_params(("arbitrary",)),
    )(dqa, dka, dva, za, gq, gkv, w_uq_pt, w_ukv_pt, cos, sina, sinb)


def _dz_cols():
    return (Z_A, Z_G) + Z_QKV + (Z_F,)


def _inproj_bwd_dx(dz, dxn, x, shift, scale, g, w_in_pt, nb, seq):
    t = x.shape[0]
    tpe = seq // TM
    cols = _dz_cols()

    def body(*refs):
        dz_refs = refs[:len(cols)]
        dxn_ref, x_ref, sh_ref, sc_ref, g_ref, wt_ref, dx_ref, dsh_ref, dsc_ref, dg_ref = refs[len(cols):]
        i = pl.program_id(0)

        @pl.when(i == 0)
        def _():
            dg_ref[...] = jnp.zeros_like(dg_ref)

        @pl.when(i % tpe == 0)
        def _():
            dsh_ref[...] = jnp.zeros_like(dsh_ref)
            dsc_ref[...] = jnp.zeros_like(dsc_ref)

        dh = jnp.zeros((TM, D_MODEL), F32)
        for ref, (c0, c1) in zip(dz_refs, cols):
            dh = dh + _dot_nt(ref[...], wt_ref[:, c0:c1])
        gv = g_ref[...]
        n, xh, r = _rms(x_ref[...], gv)
        dsh_ref[0] += jnp.sum(dh, axis=0, keepdims=True)
        dsc_ref[0] += jnp.sum(dh * n, axis=0, keepdims=True)
        dn = dh * (1.0 + sc_ref[0])
        dg_ref[...] += jnp.broadcast_to(jnp.sum(dn * xh, axis=0, keepdims=True), dg_ref.shape)
        dx_ref[...] = dxn_ref[...] + _rms_bwd(dn, xh, r, gv)

    in_specs = [_row_spec(c1 - c0) for c0, c1 in cols]
    in_specs += [_row_spec(D_MODEL), _row_spec(D_MODEL), _ex_spec(tpe), _ex_spec(tpe), _full_spec((1, D_MODEL)),
                 _full_spec((D_MODEL, NP_IN))]
    return pl.pallas_call(
        body, name="inproj_bwd_dx", grid=(t // TM,), in_specs=in_specs,
        out_specs=[_row_spec(D_MODEL), _ex_spec(tpe), _ex_spec(tpe), _full_spec((8, D_MODEL))],
        out_shape=[jax.ShapeDtypeStruct((t, D_MODEL), F32), jax.ShapeDtypeStruct((nb, 1, D_MODEL), F32),
                   jax.ShapeDtypeStruct((nb, 1, D_MODEL), F32), jax.ShapeDtypeStruct((8, D_MODEL), F32)],
        compiler_params=_params(("arbitrary",)),
    )(*dz, dxn, x, shift, scale, g, w_in_pt)


def _inproj_bwd_dw(h, dz, name, layer, both=None):
    t = h.shape[0]
    widths = [d.shape[1] for d in dz]
    total = sum(widths)

    def body(*refs):
        h_ref = refs[0]
        dz_refs = refs[1:1 + len(dz)]
        gw_ref = refs[-1]

        @pl.when(pl.program_id(0) == 0)
        def _():
            gw_ref[...] = jnp.zeros_like(gw_ref)

        h_t = h_ref[...]
        c0 = 0
        for ref, w in zip(dz_refs, widths):
            gw_ref[0, :, c0:c0 + w] += _dot_tn(h_t, ref[...])
            c0 += w

    in_specs = [_row_spec(D_MODEL)] + [_row_spec(w) for w in widths]
    args = [h, *dz]
    aliases = {}
    if both is not None:
        in_specs.append(pl.BlockSpec(memory_space=pl.ANY))
        aliases = {len(args): 0}
        args.append(both)
    return pl.pallas_call(
        body, name=name, grid=(t // TM,), in_specs=in_specs,
        out_specs=pl.BlockSpec((1, D_MODEL, total), lambda i: (layer, 0, 0)),
        out_shape=jax.ShapeDtypeStruct((DEPTH, D_MODEL, total), F32), input_output_aliases=aliases,
        compiler_params=_params(("arbitrary",)),
    )(*args)


def _ada_fwd(c_all, w_ada, b_cols):
    n = c_all.shape[0]
    cols = w_ada.shape[2]

    def body(c_ref, w_ref, b_ref, out_ref):
        act = _silu(c_ref[...]).astype(BF16)
        out_ref[0] = _dot(act, w_ref[0].astype(BF16)) + b_ref[0]

    return pl.pallas_call(
        body, name="ada_fwd", grid=(DEPTH,),
        in_specs=[pl.BlockSpec((n, D_MODEL), lambda l: (0, 0)), pl.BlockSpec((1, D_MODEL, cols), lambda l: (l, 0, 0)),
                  pl.BlockSpec((1, 1, cols), lambda l: (l, 0, 0))],
        out_specs=pl.BlockSpec((1, n, cols), lambda l: (l, 0, 0)),
        out_shape=jax.ShapeDtypeStruct((DEPTH, n, cols), F32), compiler_params=_params(("parallel",)),
    )(c_all, w_ada, b_cols)


def _ada_bwd(c_all, dmod_cols, dmod_all):
    n = c_all.shape[0]
    cols = dmod_cols.shape[2]

    def body(c_ref, dc_ref, da_ref, gw_ref, gb_ref):
        act = _silu(c_ref[...]).astype(BF16)
        gw_ref[0] = _dot_tn(act, dc_ref[0].astype(BF16))
        gb_ref[0] = jnp.sum(da_ref[0], axis=0, keepdims=True)

    return pl.pallas_call(
        body, name="ada_bwd", grid=(DEPTH,),
        in_specs=[pl.BlockSpec((n, D_MODEL), lambda l: (0, 0)), pl.BlockSpec((1, n, cols), lambda l: (l, 0, 0)),
                  pl.BlockSpec((1, n, 3 * D_MODEL), lambda l: (l, 0, 0))],
        out_specs=[pl.BlockSpec((1, D_MODEL, cols), lambda l: (l, 0, 0)),
                   pl.BlockSpec((1, 1, 3 * D_MODEL), lambda l: (l, 0, 0))],
        out_shape=[jax.ShapeDtypeStruct((DEPTH, D_MODEL, cols), F32), jax.ShapeDtypeStruct((DEPTH, 1, 3 * D_MODEL), F32)],
        compiler_params=_params(("parallel",)),
    )(c_all, dmod_cols, dmod_all)


def _sum_blocks(parts, name):
    n, rows, cols = parts.shape
    tr = rows if rows <= 256 else 8 * next(d for d in range(32, 0, -1) if (rows // 8) % d == 0)

    def body(p_ref, out_ref):
        acc = p_ref[0].astype(F32)
        for k in range(1, n):
            acc = acc + p_ref[k].astype(F32)
        out_ref[...] = acc

    return pl.pallas_call(
        body, name=name, grid=(rows // tr,), in_specs=[pl.BlockSpec((n, tr, cols), lambda i: (0, i, 0))],
        out_specs=pl.BlockSpec((tr, cols), lambda i: (i, 0)), out_shape=jax.ShapeDtypeStruct((rows, cols), F32),
        compiler_params=_params(("parallel",)),
    )(parts)


def _adamw(w, g, m, v, name):
    nl, rows, cols = w.shape
    if rows * cols <= 64 * 1024:
        tr = rows
        tl = next(t for t in range(nl, 0, -1) if nl % t == 0 and t * max(rows, 8) * cols <= 512 * 1024)
    else:
        tl = 1
        tr = next(t for t in (rows, 256, 128, 64, 32, 16, 8) if rows % t == 0 and t * cols <= 256 * 1024)

    def body(w_ref, g_ref, m_ref, v_ref, d_ref, mo_ref, vo_ref):
        gv = g_ref[...]
        mn = ADAM_B1 * m_ref[...] + (1.0 - ADAM_B1) * gv
        vn = ADAM_B2 * v_ref[...] + (1.0 - ADAM_B2) * jnp.square(gv)
        m_hat = mn / (1.0 - ADAM_B1 ** ADAM_STEP)
        v_hat = vn / (1.0 - ADAM_B2 ** ADAM_STEP)
        d_ref[...] = -ADAM_LR * (m_hat / (jnp.sqrt(v_hat) + ADAM_EPS) + ADAM_WD * w_ref[...])
        mo_ref[...] = mn
        vo_ref[...] = vn

    spec = pl.BlockSpec((tl, tr, cols), lambda l, i: (l, i, 0))
    return pl.pallas_call(
        body, name=name, grid=(nl // tl, rows // tr), in_specs=[spec] * 4, out_specs=[spec] * 3,
        out_shape=[jax.ShapeDtypeStruct((nl, rows, cols), F32)] * 3, compiler_params=_params(("parallel", "parallel")),
    )(w, g, m, v)


ALL_FLIPS = tuple(range(1, N_DEV))


def _exchange(src, flips, mode, name):
    _, rows, cols = src.shape
    nslot = 2 if mode == "pair" else N_DEV
    nf = len(flips)

    def body(src_ref, dst_ref, send_sems, recv_sems, local_sem):
        x, y, c = lax.axis_index("x"), lax.axis_index("y"), lax.axis_index("c")
        me = 4 * x + 2 * y + c

        def slot(j):
            return (j & 1) if mode == "pair" else j

        own = pltpu.make_async_copy(src_ref.at[me if mode == "scatter" else 0], dst_ref.at[slot(me)], local_sem)
        own.start()
        copies = []
        for i, f in enumerate(flips):
            peer = me ^ f
            to = (1 - x if f & 4 else x, 1 - y if f & 2 else y, 1 - c if f & 1 else c)
            cp = pltpu.make_async_remote_copy(
                src_ref=src_ref.at[peer if mode == "scatter" else 0], dst_ref=dst_ref.at[slot(me)],
                send_sem=send_sems.at[i], recv_sem=recv_sems.at[i], device_id=to, device_id_type=pl.DeviceIdType.MESH)
            cp.start()
            copies.append(cp)
        for i, f in enumerate(flips):
            peer = me ^ f
            to = (1 - x if f & 4 else x, 1 - y if f & 2 else y, 1 - c if f & 1 else c)
            pltpu.make_async_remote_copy(
                src_ref=src_ref.at[0], dst_ref=dst_ref.at[slot(peer)], send_sem=send_sems.at[i],
                recv_sem=recv_sems.at[i], device_id=to, device_id_type=pl.DeviceIdType.MESH).wait_recv()
        for cp in copies:
            cp.wait_send()
        own.wait()

    return pl.pallas_call(
        body, name=name, out_shape=jax.ShapeDtypeStruct((nslot, rows, cols), src.dtype),
        in_specs=[pl.BlockSpec(memory_space=pl.ANY)], out_specs=pl.BlockSpec(memory_space=pl.ANY),
        scratch_shapes=[pltpu.SemaphoreType.DMA((nf,)), pltpu.SemaphoreType.DMA((nf,)), pltpu.SemaphoreType.DMA],
    )(src)


def _transfer(name, srcs, dst_shapes, plan, in_place=False):
    n_arr = len(srcs)
    probe = plan(0, 0, 0)
    n_steps = len(probe)

    def body(*refs):
        src_refs, dst_refs = refs[:n_arr], refs[n_arr:2 * n_arr]
        send_sems, recv_sems, local_sems = refs[2 * n_arr:]
        x, y, c = lax.axis_index("x"), lax.axis_index("y"), lax.axis_index("c")
        steps = plan(x, y, c)

        def rows(ref, r0, n):
            return ref.at[:, pl.ds(r0, n), :]

        def arrival(t):
            a, _, _, n, _, f, _ = steps[t]
            return pltpu.make_async_remote_copy(
                src_ref=rows(dst_refs[a], 0, n), dst_ref=rows(dst_refs[a], 0, n), send_sem=send_sems.at[t],
                recv_sem=recv_sems.at[t], device_id=(x, y, c), device_id_type=pl.DeviceIdType.MESH)

        arrived, started = set(), []
        for t, (a, from_dst, sr, n, dr, f, after) in enumerate(steps):
            for u in after:
                if u not in arrived:
                    arrival(u).wait_recv()
                    arrived.add(u)
            src = rows(dst_refs[a] if from_dst else src_refs[a], sr, n)
            dst = rows(dst_refs[a], dr, n)
            if f == 0:
                cp = pltpu.make_async_copy(src, dst, local_sems.at[t])
            else:
                to = (1 - x if f & 4 else x, 1 - y if f & 2 else y, 1 - c if f & 1 else c)
                cp = pltpu.make_async_remote_copy(src_ref=src, dst_ref=dst, send_sem=send_sems.at[t],
                                                  recv_sem=recv_sems.at[t], device_id=to,
                                                  device_id_type=pl.DeviceIdType.MESH)
            cp.start()
            started.append(cp)
        for t, step in enumerate(steps):
            if step[5] != 0 and t not in arrived:
                arrival(t).wait_recv()
        for cp, step in zip(started, steps):
            if step[5] == 0:
                cp.wait()
            else:
                cp.wait_send()

    any_spec = pl.BlockSpec(memory_space=pl.ANY)
    return pl.pallas_call(
        body, name=name, out_shape=[jax.ShapeDtypeStruct(s, d) for s, d in dst_shapes],
        in_specs=[any_spec] * n_arr, out_specs=[any_spec] * n_arr,
        input_output_aliases={a: a for a in range(n_arr)} if in_place else {},
        scratch_shapes=[pltpu.SemaphoreType.DMA((n_steps,)), pltpu.SemaphoreType.DMA((n_steps,)),
                        pltpu.SemaphoreType.DMA((n_steps,))],
    )(*srcs)


CHIP_FLIPS = (2, 4, 6)


def _gather_plan(chip_rows):
    def plan(x, y, c):
        steps = []
        chip = 2 * x + y
        for a, rc in enumerate(chip_rows):
            h = rc // 2
            first = (h // 32) * 16
            mine = rc * chip + h * c
            from_x, from_y, diag = rc * (chip ^ 2) + h * c, rc * (chip ^ 1) + h * c, rc * (chip ^ 3) + h * c
            steps.append((a, False, h * c, h, mine, 0, ()))
            to_x = len(steps)
            steps.append((a, False, h * c, h, mine, 4, ()))
            to_y = len(steps)
            steps.append((a, False, h * c, h, mine, 2, ()))
            fwd_y = len(steps)
            steps.append((a, True, from_x, first, from_x, 2, (to_x,)))
            fwd_x = len(steps)
            steps.append((a, True, from_y + first, h - first, from_y + first, 4, (to_y,)))
            steps.append((a, False, h * c, h, mine, 1, ()))
            steps.append((a, True, from_x, h, from_x, 1, (to_x,)))
            steps.append((a, True, from_y, h, from_y, 1, (to_y,)))
            steps.append((a, True, diag, first, diag, 1, (fwd_y,)))
            steps.append((a, True, diag + first, h - first, diag + first, 1, (fwd_x,)))
        return steps
    return plan


def _pair_reduce_plan(chip_rows):
    def plan(x, y, c):
        steps = []
        for a, rc in enumerate(chip_rows):
            h = rc // 2
            for j in range(4):
                steps.append((a, False, rc * j + h * (1 - c), h, h * j, 1, ()))
        return steps
    return plan


def _chip_scatter_plan(chip_rows):
    def plan(x, y, c):
        steps = []
        for a, rc in enumerate(chip_rows):
            h = rc // 2
            for k, f in enumerate(CHIP_FLIPS):
                steps.append((a, False, h * ((2 * x + y) ^ (f >> 1)), h, h * k, f, ()))
        return steps
    return plan


def _pair_share_plan(chip_rows):
    def plan(x, y, c):
        return [(a, True, (rc // 2) * c, rc // 2, (rc // 2) * c, 1, ()) for a, rc in enumerate(chip_rows)]
    return plan


def _tile_rows(h):
    return h


def _sum_pair(partial, recv, core, rc, name):
    nl, _, cols = partial.shape
    h = rc // 2
    tr = _tile_rows(h)

    def body(c_ref, p_ref, r_ref, out_ref):
        out_ref[...] = (p_ref[...] + r_ref[...]).astype(BF16)

    spec = pl.BlockSpec((1, tr, cols), lambda l, j, i, c_ref: (l, (h // tr) * j + i, 0))
    return pl.pallas_call(
        body, name=name, out_shape=jax.ShapeDtypeStruct((nl, 4 * h, cols), BF16),
        grid_spec=pltpu.PrefetchScalarGridSpec(
            num_scalar_prefetch=1, grid=(nl, 4, h // tr),
            in_specs=[pl.BlockSpec((1, tr, cols), lambda l, j, i, c_ref: (l, (rc // tr) * j + (h // tr) * c_ref[0] + i, 0)),
                      spec],
            out_specs=spec),
        compiler_params=_params(("parallel", "parallel", "parallel")),
    )(core, partial, recv)


def _sum_chips(chip_sum, recv, place, rc, name):
    nl, _, cols = chip_sum.shape
    h = rc // 2
    tr = _tile_rows(h)

    def body(s_ref, own_ref, r_ref, out_ref):
        acc = own_ref[0].astype(F32)
        for k in range(3):
            acc = acc + r_ref[0, k].astype(F32)
        out_ref[0] = acc

    return pl.pallas_call(
        body, name=name, out_shape=jax.ShapeDtypeStruct((nl, rc, cols), F32),
        grid_spec=pltpu.PrefetchScalarGridSpec(
            num_scalar_prefetch=1, grid=(nl, h // tr),
            in_specs=[pl.BlockSpec((1, tr, cols), lambda l, i, s_ref: (l, (h // tr) * s_ref[0] + i, 0)),
                      pl.BlockSpec((1, 3, tr, cols), lambda l, i, s_ref: (l, 0, i, 0))],
            out_specs=pl.BlockSpec((1, tr, cols), lambda l, i, s_ref: (l, (h // tr) * s_ref[1] + i, 0))),
        compiler_params=_params(("parallel", "parallel")),
    )(place, chip_sum, recv.reshape(nl, 3, h, cols))


def _sum_slots(parts, out_dtype, name):
    nl, n, rows, cols = parts.shape
    tr = next(t for t in (128, 64, 32, 16) if rows % t == 0 and n * t * cols * 4 <= (4 << 20))

    def body(p_ref, out_ref):
        acc = p_ref[0, 0].astype(F32)
        for k in range(1, n):
            acc = acc + p_ref[0, k].astype(F32)
        out_ref[0] = acc.astype(out_dtype)

    return pl.pallas_call(
        body, name=name, grid=(nl, rows // tr),
        in_specs=[pl.BlockSpec((1, n, tr, cols), lambda l, i: (l, 0, i, 0))],
        out_specs=pl.BlockSpec((1, tr, cols), lambda l, i: (l, i, 0)),
        out_shape=jax.ShapeDtypeStruct((nl, rows, cols), out_dtype), compiler_params=_params(("parallel", "parallel")),
    )(parts)


def _pad_cols(a, n):
    return a if n == 0 else jnp.pad(a, ((0, 0), (0, n)))


def _in_to_padded(w):
    return jnp.concatenate([_pad_cols(w[:, a:b], z) for a, b, z in IN_PIECES], axis=1)


def _in_from_padded(gp):
    pos, out = 0, {}
    for a, b, z in IN_PIECES:
        out[a] = gp[:, pos:pos + (b - a)]
        pos += (b - a) + z
    return jnp.concatenate([out[a] for a in sorted(out)], axis=1)


def _in_cols_first(ga, gb):
    split = Z_G[1]
    pos, out = 0, {}
    for a, b, z in IN_PIECES:
        src, off = (ga, pos) if pos < split else (gb, pos - split)
        out[a] = jnp.transpose(src[:, :, off:off + (b - a)], (2, 0, 1))
        pos += (b - a) + z
    return jnp.concatenate([out[a] for a in sorted(out)], axis=0)


def _out_to_padded(w):
    z = jnp.zeros((64, w.shape[1]), w.dtype)
    return jnp.concatenate([w[0:384], w[384:704], z, w[704:1024], z], axis=0)


def _out_from_padded(gp):
    return jnp.concatenate([gp[0:384], gp[384:704], gp[768:1088]], axis=0)


def _uq_to_padded(w):
    parts = []
    for p in range(3):
        h0, h1 = 2 * p, 2 * p + 1
        parts += [w[:, 96 * h0:96 * h0 + 64], w[:, 96 * h1:96 * h1 + 64], w[:, 96 * h0 + 64:96 * h0 + 96],
                  w[:, 96 * h1 + 64:96 * h1 + 96], jnp.zeros((w.shape[0], 64), w.dtype)]
    return jnp.concatenate(parts, axis=1)


def _uq_from_padded(gp):
    parts = []
    for h in range(6):
        p, s = h // 2, h % 2
        parts += [gp[:, 256 * p + 64 * s:256 * p + 64 * s + 64], gp[:, 256 * p + 128 + 32 * s:256 * p + 160 + 32 * s]]
    return jnp.concatenate(parts, axis=1)


def _ukv_to_padded(w):
    return jnp.concatenate([w[:, 128 * h:128 * h + 64] for h in range(6)]
                           + [w[:, 128 * h + 64:128 * h + 128] for h in range(6)], axis=1)


def _ukv_from_padded(gp):
    parts = []
    for h in range(6):
        parts += [gp[:, 64 * h:64 * h + 64], gp[:, 384 + 64 * h:384 + 64 * h + 64]]
    return jnp.concatenate(parts, axis=1)


LR_ROWS = 224
SHARD_ROWS = (256, 256, 256, LR_ROWS)


def _pack_lowrank(w_uq, w_ukv):
    flat = jnp.concatenate([w_uq.reshape(-1), w_ukv.reshape(-1)])
    return jnp.pad(flat, (0, LR_ROWS * PACK_COLS - flat.shape[0])).reshape(1, LR_ROWS, PACK_COLS)


def _unpack_lowrank(packed):
    flat = packed.reshape(-1)
    n_uq = DEPTH * A_Q_RANK * 144
    n_ukv = DEPTH * A_KV_RANK * 192
    return flat[0:n_uq].reshape(DEPTH, A_Q_RANK, 144), flat[n_uq:n_uq + n_ukv].reshape(DEPTH, A_KV_RANK, 192)


PACK_SIZES = (DEPTH * 256 * N_IN, DEPTH * 256 * D_MODEL, DEPTH * A_Q_RANK * 144, DEPTH * A_KV_RANK * 192)


def _pack(w_in, w_out, w_uq, w_ukv):
    flat = jnp.concatenate([w_in.reshape(-1), w_out.reshape(-1), w_uq.reshape(-1), w_ukv.reshape(-1)])
    flat = jnp.pad(flat, (0, PACK_ROWS * PACK_COLS - flat.shape[0]))
    return flat.reshape(PACK_ROWS, PACK_COLS)


def _unpack(packed):
    flat = packed.reshape(-1)
    o0, o1, o2, o3 = PACK_SIZES
    w_in = flat[0:o0].reshape(DEPTH, 256, N_IN)
    w_out = flat[o0:o0 + o1].reshape(DEPTH, 256, D_MODEL)
    w_uq = flat[o0 + o1:o0 + o1 + o2].reshape(DEPTH, A_Q_RANK, 144)
    w_ukv = flat[o0 + o1 + o2:o0 + o1 + o2 + o3].reshape(DEPTH, A_KV_RANK, 192)
    return w_in, w_out, w_uq, w_ukv


def _rope_tables(positions):
    t = positions.size
    inv = ROPE_THETA ** (-jnp.arange(0, A_ROPE, 2, dtype=F32) / A_ROPE)
    inv_row = jnp.pad(jnp.tile(inv, 4), (0, 64)).reshape(1, LANE)

    def body(p_ref, i_ref, c_ref, sa_ref, sb_ref):
        ang = p_ref[...].astype(F32) * i_ref[...]
        lane = _lane_iota((TM, LANE))
        live = lane < 64
        second = (lane & 31) >= 16
        s = jnp.sin(ang)
        c_ref[...] = jnp.where(live, jnp.cos(ang), 0.0)
        sa_ref[...] = jnp.where(live & jnp.logical_not(second), -s, 0.0)
        sb_ref[...] = jnp.where(live & second, s, 0.0)

    return pl.pallas_call(
        body, name="rope_tables", grid=(t // TM,), in_specs=[_row_spec(1), _full_spec((1, LANE))],
        out_specs=[_row_spec(LANE)] * 3, out_shape=[jax.ShapeDtypeStruct((t, LANE), F32)] * 3,
        compiler_params=_params(("parallel",)),
    )(positions.reshape(t, 1), inv_row)


def _rows(a, n):
    flat = a.reshape(-1)
    return jnp.pad(flat, (0, n * LANE - flat.shape[0])).reshape(n, LANE)


def _forward_backward(x, mod, tables, target, weights, small, nb, seq):
    cos, sina, sinb = tables
    saved = []
    for l in range(DEPTH):
        w, s = weights[l], small[l]
        shift = mod[l][:, None, 0:D_MODEL]
        scale = mod[l][:, None, D_MODEL:2 * D_MODEL]
        gate = mod[l][:, None, 2 * D_MODEL:]
        h, za, zg, qb, kb, vb, qc, kc, vc, zf = _ln_inproj(x, shift, scale, s["norm_g"], w["in"], seq)
        qa, ka, va = _a_up(za, s["gq"], s["gkv"], w["uq"], w["ukv"], cos, sina, sinb)
        bias = _bias_expand(s["g8"])[0:6]
        f = _forget_fwd(zf, s["fb"], nb, seq)
        frow = jnp.pad(f[:, 0:5].reshape(nb, seq, 5).transpose(0, 2, 1), ((0, 0), (0, 1), (0, 0)))
        frow = frow.reshape(nb, 6, 1, seq)
        oa, lse_a = _attn_fwd("A", qa, ka, va, None, nb, seq)
        ob, lse_b = _attn_fwd("B", qb, kb, vb, bias, nb, seq)
        oc, lse_c = _attn_fwd("C", qc, kc, vc, frow, nb, seq)
        y, xn = _gate_outproj(x, gate, oa, ob, oc, zg, w["out"], seq)
        saved.append(dict(x=x, h=h, za=za, zg=zg, zf=zf, y=y, shift=shift, scale=scale, gate=gate, bias=bias, frow=frow,
                          a=(qa, ka, va, oa, lse_a), b=(qb, kb, vb, ob, lse_b), c=(qc, kc, vc, oc, lse_c)))
        x = xn
    dx, loss8, gfinal8 = _final_loss(x, target, small[0]["final_g"])
    grads = []
    gw_in = (None, None)
    for l in reversed(range(DEPTH)):
        w, s, sv = weights[l], small[l], saved[l]
        qa, ka, va, oa, lse_a = sv["a"]
        qb, kb, vb, ob, lse_b = sv["b"]
        qc, kc, vc, oc, lse_c = sv["c"]
        doa, dob, doc, dzg, gw_out, dgate = _outproj_bwd(dx, sv["y"], sv["gate"], oa, ob, oc, sv["zg"], w["out"],
                                                          w["out_t"], nb, seq)
        dqa, dka, dva = _attn_bwd("A", qa, ka, va, oa, doa, lse_a, None, nb, seq)
        dqb, dkb, dvb, dbt = _attn_bwd("B", qb, kb, vb, ob, dob, lse_b, sv["bias"], nb, seq)
        dqc, dkc, dvc, dfr, dfq = _attn_bwd("C", qc, kc, vc, oc, doc, lse_c, sv["frow"], nb, seq)
        dg = _bias_reduce(jnp.pad(dbt, ((0, 2), (0, 0), (0, 0))))
        grb = jnp.pad(_bias_unline(dg), ((0, 0), (0, 384 - N_REL)))
        dfk = dfr.reshape(nb, 6, seq).transpose(0, 2, 1).reshape(nb * seq, 6)
        dzf, gfb = _forget_bwd(dfq, jnp.pad(dfk, ((0, 0), (0, LANE - 6))), sv["zf"], s["fb"], nb, seq)
        dza, gw_uq, gw_ukv, ggq, ggkv = _a_up_bwd(dqa, dka, dva, sv["za"], s["gq"], s["gkv"], w["uq_t"], w["ukv_t"],
                                                  cos, sina, sinb)
        dz = (dza, dzg, dqb, dkb, dvb, dqc, dkc, dvc, dzf)
        dx, dshift, dscale, gnorm = _inproj_bwd_dx(dz, dx, sv["x"], sv["shift"], sv["scale"], s["norm_g"], w["in"],
                                                   nb, seq)
        gw_in = (_inproj_bwd_dw(sv["h"], dz[0:2], "inproj_bwd_dw0", l, gw_in[0]),
                 _inproj_bwd_dw(sv["h"], dz[2:], "inproj_bwd_dw1", l, gw_in[1]))
        dmod = jnp.concatenate([dshift[:, 0], dscale[:, 0], dgate[:, 0]], axis=1)
        grads.append(dict(w_out=gw_out, w_uq=gw_uq, w_ukv=gw_ukv, dmod=dmod, norm_g=gnorm[0], gq=ggq[0],
                          gkv=ggkv[0], rb8=grb, fb=gfb[0]))
    grads.reverse()
    return loss8[0, 0], dx, grads, gfinal8[0], gw_in


def _layer_weights(w_in, w_out, w_uq, w_ukv):
    wi, wo, wq, wkv = _in_to_padded(w_in), _out_to_padded(w_out), _uq_to_padded(w_uq), _ukv_to_padded(w_ukv)
    return {"in": wi, "out": wo, "out_t": wo.T, "uq": wq, "uq_t": wq.T, "ukv": wkv, "ukv_t": wkv.T}


def _layer_small(norm_g, gq, gkv, rel_bias, forget_b, final_g):
    fb = jnp.pad(forget_b, (0, LANE - 5)).reshape(1, LANE)
    return dict(norm_g=norm_g.reshape(1, -1), gq=gq.reshape(1, -1), gkv=gkv.reshape(1, -1), g8=_bias_line(rel_bias), fb=fb,
                final_g=final_g.reshape(1, -1))


def _small_payload(per_layer, final_g, loss):
    def stack(key):
        return jnp.stack([p[key] for p in per_layer])

    def rows(a, rng):
        return _rows(a, rng[1] - rng[0])

    dmod = stack("dmod") if "dmod" in per_layer[0] else jnp.zeros((LANE,), F32)
    parts = [rows(dmod, PAY_DMOD), rows(stack("norm_g"), PAY_NORM), rows(stack("gq"), PAY_GQ),
             rows(stack("gkv"), PAY_GKV), rows(stack("rb8"), PAY_RB), rows(stack("fb"), PAY_FB),
             rows(final_g, PAY_FINAL), rows(loss, PAY_LOSS)]
    return jnp.concatenate(parts, axis=0)


def _payload_split(pay):
    def take(rng, shape):
        n = 1
        for d in shape:
            n *= d
        return pay[rng[0]:rng[1]].reshape(-1)[0:n].reshape(shape)

    norm_g = take(PAY_NORM, (DEPTH, D_MODEL))
    gq = take(PAY_GQ, (DEPTH, A_Q_RANK))
    gkv = take(PAY_GKV, (DEPTH, A_KV_RANK))
    rb = take(PAY_RB, (DEPTH, 8, 384))[:, 0:5, 0:N_REL]
    fb = take(PAY_FB, (DEPTH, LANE))[:, 0:5]
    final_g = take(PAY_FINAL, (D_MODEL,))
    return norm_g, gq, gkv, rb, fb, final_g


def kernel(x, c, positions, w_ada, b_ada, norm_g, w_in, a_q_norm_g, a_w_uq, a_kv_norm_g, a_w_ukv, b_rel_bias, c_forget_b, w_out, final_g, loss_target, m_w_ada, m_b_ada, m_norm_g, m_w_in, m_a_q_norm_g, m_a_w_uq, m_a_kv_norm_g, m_a_w_ukv, m_b_rel_bias, m_c_forget_b, m_w_out, m_final_g, v_w_ada, v_b_ada, v_norm_g, v_w_in, v_a_q_norm_g, v_a_w_uq, v_a_kv_norm_g, v_a_w_ukv, v_b_rel_bias, v_c_forget_b, v_w_out, v_final_g):
    nb, seq, _ = x.shape
    ix, iy, ic = lax.axis_index("x"), lax.axis_index("y"), lax.axis_index("c")
    chip = 2 * ix + iy
    me = 2 * chip + ic

    weight_plan = _gather_plan((256, 256, LR_ROWS))

    def gather_plan(x, y, c_):
        me_ = 4 * x + 2 * y + c_
        return weight_plan(x, y, c_) + [(3, False, 0, 8, 8 * me_, f, ()) for f in range(N_DEV)]

    full_in, full_out, full_lr, c_rows = _transfer(
        "gather_weights", [w_in.astype(BF16), w_out.astype(BF16), _pack_lowrank(a_w_uq, a_w_ukv).astype(BF16),
                           jnp.pad(c, ((0, 8 - nb), (0, 0)))[None]],
        [((DEPTH, D_MODEL, N_IN), BF16), ((DEPTH, D_MODEL, D_MODEL), BF16), ((1, 4 * LR_ROWS, PACK_COLS), BF16),
         ((1, 8 * N_DEV, D_MODEL), F32)], gather_plan)
    lowrank = [_unpack_lowrank(full_lr[0, LR_ROWS * j:LR_ROWS * (j + 1)]) for j in range(4)]
    full_uq = jnp.concatenate([s[0] for s in lowrank], axis=2)
    full_ukv = jnp.concatenate([s[1] for s in lowrank], axis=2)
    weights = [_layer_weights(full_in[l], full_out[l], full_uq[l], full_ukv[l]) for l in range(DEPTH)]
    small = [_layer_small(norm_g[l], a_q_norm_g[l], a_kv_norm_g[l], b_rel_bias[l], c_forget_b[l], final_g)
             for l in range(DEPTH)]

    c_all = c_rows.reshape(N_DEV, 8, D_MODEL)[:, 0:nb].reshape(N_DEV * nb, D_MODEL)
    cols = w_ada.shape[2]
    b_cols = lax.dynamic_slice_in_dim(b_ada, chip * cols, cols, axis=1)[:, None, :]
    mod_cols = _ada_fwd(c_all, w_ada, b_cols)
    mod_g = _exchange(mod_cols.reshape(1, DEPTH * N_DEV * nb, cols), ALL_FLIPS, "gather", "gather_mod")
    mod_all = jnp.concatenate([mod_g[2 * j].reshape(DEPTH, N_DEV * nb, cols) for j in range(4)], axis=2)
    mod = lax.dynamic_slice_in_dim(mod_all, me * nb, nb, axis=1)

    tables = _rope_tables(positions)
    loss_part, dx, grads, gfinal, gw_in = _forward_backward(
        x.reshape(nb * seq, D_MODEL), mod, tables, loss_target.reshape(nb * seq, D_MODEL), weights, small, nb, seq)

    pay = _small_payload(grads, gfinal, loss_part)
    pay_all = _exchange(pay[None], ALL_FLIPS, "gather", "gather_small")
    tot = _sum_blocks(pay_all, "sum_small")
    loss = tot[PAY_LOSS[0], 0]
    dmod_all = pay_all[:, PAY_DMOD[0]:PAY_DMOD[1]].reshape(N_DEV, -1)[:, 0:DEPTH * nb * 3 * D_MODEL]
    dmod_all = dmod_all.reshape(N_DEV, DEPTH, nb, 3 * D_MODEL).transpose(1, 0, 2, 3)
    dmod_all = dmod_all.reshape(DEPTH, N_DEV * nb, 3 * D_MODEL)
    my_cols = lax.dynamic_slice_in_dim(dmod_all, chip * cols, cols, axis=2)
    g_w_ada, g_b_ada = _ada_bwd(c_all, my_cols, dmod_all)
    g_b_ada = g_b_ada[:, 0]

    g_uq = jnp.stack([_uq_from_padded(g["w_uq"]) for g in grads])
    g_ukv = jnp.stack([_ukv_from_padded(g["w_ukv"]) for g in grads])
    g_lr = jnp.concatenate([_pack_lowrank(g_uq[:, :, 144 * j:144 * (j + 1)], g_ukv[:, :, 192 * j:192 * (j + 1)])
                            for j in range(4)], axis=1)
    partials = [gw_in[0], gw_in[1], jnp.stack([_out_from_padded(g["w_out"]) for g in grads]), g_lr]
    shapes = [(p.shape[0], p.shape[2]) for p in partials]
    halves = [r // 2 for r in SHARD_ROWS]
    core_s = jnp.reshape(ic, (1,)).astype(jnp.int32)
    place_s = jnp.stack([chip, ic]).astype(jnp.int32)
    from_pair = _transfer("pair_reduce", partials, [((nl, 4 * h, nc), F32) for (nl, nc), h in zip(shapes, halves)],
                          _pair_reduce_plan(SHARD_ROWS))
    chip_sums = [_sum_pair(p, r, core_s, rc, "sum_pair%d" % i)
                 for i, (p, r, rc) in enumerate(zip(partials, from_pair, SHARD_ROWS))]
    from_chips = _transfer("chip_scatter", chip_sums, [((nl, 3 * h, nc), BF16) for (nl, nc), h in zip(shapes, halves)],
                           _chip_scatter_plan(SHARD_ROWS))
    reduced = [_sum_chips(s, r, place_s, rc, "sum_chips%d" % i)
               for i, (s, r, rc) in enumerate(zip(chip_sums, from_chips, SHARD_ROWS))]
    g_in_a, g_in_b, g_out_sh, g_lr_sh = _transfer("pair_share", reduced, [(r.shape, F32) for r in reduced],
                                                  _pair_share_plan(SHARD_ROWS), in_place=True)
    g_uq_sh, g_ukv_sh = _unpack_lowrank(g_lr_sh[0])

    def cols_first(a):
        return jnp.transpose(a, (2, 0, 1))

    g_in_t = _in_cols_first(g_in_a, g_in_b)
    upd_in = tuple(jnp.transpose(a, (1, 2, 0)) for a in _adamw(cols_first(w_in), g_in_t, cols_first(m_w_in),
                                                               cols_first(v_w_in), "adamw_in"))
    gw = (jnp.transpose(g_in_t, (1, 2, 0)), g_out_sh, g_uq_sh, g_ukv_sh)
    upd = [upd_in, _adamw(w_out, gw[1], m_w_out, v_w_out, "adamw_out"),
           _adamw(a_w_uq, gw[2], m_a_w_uq, v_a_w_uq, "adamw_uq"), _adamw(a_w_ukv, gw[3], m_a_w_ukv, v_a_w_ukv, "adamw_ukv")]
    dw, mw, vw = (tuple(u[i] for u in upd) for i in range(3))
    d_ada, m_ada, v_ada = _adamw(w_ada, g_w_ada, m_w_ada, v_w_ada, "adamw_ada")

    def adam_small(w, g, m, v, name):
        shape3 = (1,) * (3 - w.ndim) + w.shape
        return tuple(a.reshape(w.shape) for a in _adamw(w.reshape(shape3), g.reshape(shape3), m.reshape(shape3),
                                                        v.reshape(shape3), name))

    d_b, m_b, v_b = adam_small(b_ada, g_b_ada, m_b_ada, v_b_ada, "adamw_b_ada")
    gs = _payload_split(tot)
    small_upd = [adam_small(w, g, m, v, "adamw_small%d" % i) for i, (w, g, m, v) in enumerate(zip(
        (norm_g, a_q_norm_g, a_kv_norm_g, b_rel_bias, c_forget_b, final_g), gs,
        (m_norm_g, m_a_q_norm_g, m_a_kv_norm_g, m_b_rel_bias, m_c_forget_b, m_final_g),
        (v_norm_g, v_a_q_norm_g, v_a_kv_norm_g, v_b_rel_bias, v_c_forget_b, v_final_g)))]
    ds, ms, vs = (tuple(u[i] for u in small_upd) for i in range(3))

    def ordered(ada, b, sm, big):
        ng, gq, gkv, rb, fb, fg = sm
        b_in, b_out, b_uq, b_ukv = big
        return (ada, b, ng, b_in, gq, b_uq, gkv, b_ukv, rb, fb, b_out, fg)

    return (loss, dx.reshape(nb, seq, D_MODEL), *ordered(g_w_ada, g_b_ada, gs, gw), *ordered(d_ada, d_b, ds, dw),
            *ordered(m_ada, m_b, ms, mw), *ordered(v_ada, v_b, vs, vw))
```

```python
import functools

import jax
import jax.numpy as jnp
from jax import lax
from jax.experimental import pallas as pl
from jax.experimental.pallas import tpu as pltpu

F32 = jnp.float32
BF16 = jnp.bfloat16

D_MODEL = 1024
DEPTH = 2
EPS = 1e-6
NEG = -1e30
LOG2E = 1.4426950408889634
ROPE_THETA = 10000.0
A_ROPE = 32
A_Q_RANK = 384
A_KV_RANK = 256
REL_CLIP = 128
N_REL = 2 * REL_CLIP + 1
N_IN = 3621

ADAM_LR = 0.001
ADAM_B1 = 0.9
ADAM_B2 = 0.999
ADAM_EPS = 1e-08
ADAM_WD = 0.01
ADAM_STEP = 10

LANE = 128
VMEM_LIMIT = 56 * 1024 * 1024

NP_IN = 4352
Z_A = (0, 768)
Z_G = (768, 1920)
Z_QKV = tuple((1920 + 384 * i, 1920 + 384 * (i + 1)) for i in range(6))
Z_F = (4224, 4352)
IN_PIECES = ((0, 672, 96), (672, 1056, 0), (2016, 2336, 64), (3301, 3621, 64), (1056, 1376, 64), (1376, 1696, 64),
             (1696, 2016, 64), (2336, 2656, 64), (2656, 2976, 64), (2976, 3296, 64), (3296, 3301, 123))
D_CAT = 1152

TM = 512
T_CAUSAL = 256
T_BAND = 128
BAND_TILES = 5
N_DEV = 8

PAY_DMOD = (0, 96)
PAY_NORM = (96, 112)
PAY_GQ = (112, 120)
PAY_GKV = (120, 128)
PAY_RB = (128, 176)
PAY_FB = (176, 184)
PAY_FINAL = (184, 192)
PAY_LOSS = (192, 200)

PACK_COLS = 1024


def _params(sem=None):
    return pltpu.CompilerParams(dimension_semantics=sem, vmem_limit_bytes=VMEM_LIMIT)


def _lane_iota(shape):
    return lax.broadcasted_iota(jnp.int32, shape, len(shape) - 1)


def _silu(u):
    return u * jax.nn.sigmoid(u)


def _dsilu(u):
    s = jax.nn.sigmoid(u)
    return s * (1.0 + u * (1.0 - s))


def _rms(x, g):
    r = lax.rsqrt(jnp.mean(x * x, axis=-1, keepdims=True) + EPS)
    xh = x * r
    return xh * g, xh, r


def _rms_bwd(dy, xh, r, g):
    dxh = dy * g
    return r * (dxh - xh * jnp.mean(dxh * xh, axis=-1, keepdims=True))


def _rope(x, cos, sina, sinb):
    return x * cos + pltpu.roll(x, 16, 1) * sinb + pltpu.roll(x, LANE - 16, 1) * sina


def _rope_t(dy, cos, sina, sinb):
    return dy * cos + pltpu.roll(dy * sinb, LANE - 16, 1) + pltpu.roll(dy * sina, 16, 1)


def _split3(x):
    hi = x.astype(BF16)
    r1 = x - hi.astype(F32)
    mid = r1.astype(BF16)
    lo = (r1 - mid.astype(F32)).astype(BF16)
    return hi, mid, lo


def _dot(a, b):
    return jnp.dot(a, b, preferred_element_type=F32)


def _dot_nt(a, b):
    return lax.dot_general(a, b, (((1,), (1,)), ((), ())), preferred_element_type=F32)


def _dot_tn(a, b):
    return lax.dot_general(a, b, (((0,), (0,)), ((), ())), preferred_element_type=F32)


def _row_spec(cols):
    return pl.BlockSpec((TM, cols), lambda i: (i, 0))


def _full_spec(shape):
    return pl.BlockSpec(shape, lambda i: (0,) * len(shape))


def _ex_spec(tiles_per_ex):
    return pl.BlockSpec((1, 1, D_MODEL), lambda i: (i // tiles_per_ex, 0, 0))


def _ln_inproj(x, shift, scale, g, w_in_p, seq):
    t = x.shape[0]

    def body(x_ref, sh_ref, sc_ref, g_ref, w_ref, h_ref, za_ref, zg_ref, q0, q1, q2, q3, q4, q5, zf_ref):
        n, _, _ = _rms(x_ref[...], g_ref[...])
        h = (n * (1.0 + sc_ref[0]) + sh_ref[0]).astype(BF16)
        h_ref[...] = h
        za_ref[...] = _dot(h, w_ref[:, Z_A[0]:Z_A[1]])
        zg_ref[...] = _dot(h, w_ref[:, Z_G[0]:Z_G[1]])
        for ref, (c0, c1) in zip((q0, q1, q2, q3, q4, q5), Z_QKV):
            ref[...] = _dot(h, w_ref[:, c0:c1]).astype(BF16)
        zf_ref[...] = _dot(h, w_ref[:, Z_F[0]:Z_F[1]])

    tpe = seq // TM
    shapes = [jax.ShapeDtypeStruct((t, D_MODEL), BF16), jax.ShapeDtypeStruct((t, 768), F32),
              jax.ShapeDtypeStruct((t, D_CAT), F32)]
    shapes += [jax.ShapeDtypeStruct((t, 384), BF16)] * 6 + [jax.ShapeDtypeStruct((t, LANE), F32)]
    return pl.pallas_call(
        body, name="ln_inproj", grid=(t // TM,),
        in_specs=[_row_spec(D_MODEL), _ex_spec(tpe), _ex_spec(tpe), _full_spec((1, D_MODEL)),
                  _full_spec((D_MODEL, NP_IN))],
        out_specs=[_row_spec(D_MODEL), _row_spec(768), _row_spec(D_CAT)] + [_row_spec(384)] * 6 + [_row_spec(LANE)],
        out_shape=shapes, compiler_params=_params(("parallel",)),
    )(x, shift, scale, g, w_in_p)


def _a_up(za, gq, gkv, w_uq_p, w_ukv_p, cos, sina, sinb):
    t = za.shape[0]

    def body(za_ref, gq_ref, gkv_ref, wq_ref, wkv_ref, cos_ref, sa_ref, sb_ref, q_ref, k_ref, v_ref):
        cos_t, sa, sb = cos_ref[...], sa_ref[...], sb_ref[...]
        cqn, _, _ = _rms(za_ref[:, 0:384], gq_ref[...])
        q = _dot(cqn.astype(BF16), wq_ref[...])
        ckvn, _, _ = _rms(za_ref[:, 384:640], gkv_ref[...])
        kv = _dot(ckvn.astype(BF16), wkv_ref[...])
        kpe = za_ref[:, 640:768]
        kpe = _rope(kpe + pltpu.roll(kpe, 32, 1), cos_t, sa, sb).astype(BF16)
        for p in range(3):
            q_ref[:, 256 * p:256 * p + 128] = q[:, 256 * p:256 * p + 128].astype(BF16)
            q_ref[:, 256 * p + 128:256 * p + 256] = _rope(q[:, 256 * p + 128:256 * p + 256], cos_t, sa, sb).astype(BF16)
            k_ref[:, 256 * p:256 * p + 128] = kv[:, 128 * p:128 * p + 128].astype(BF16)
            k_ref[:, 256 * p + 128:256 * p + 256] = kpe
        v_ref[...] = kv[:, 384:768].astype(BF16)

    return pl.pallas_call(
        body, name="a_up", grid=(t // TM,),
        in_specs=[_row_spec(768), _full_spec((1, 384)), _full_spec((1, 256)), _full_spec((384, 768)),
                  _full_spec((256, 768)), _row_spec(LANE), _row_spec(LANE), _row_spec(LANE)],
        out_specs=[_row_spec(768), _row_spec(768), _row_spec(384)],
        out_shape=[jax.ShapeDtypeStruct((t, 768), BF16), jax.ShapeDtypeStruct((t, 768), BF16),
                   jax.ShapeDtypeStruct((t, 384), BF16)],
        compiler_params=_params(("parallel",)),
    )(za, gq, gkv, w_uq_p, w_ukv_p, cos, sina, sinb)


def _tri(n, upper):
    r = lax.broadcasted_iota(jnp.int32, (n, n), 0)
    c = lax.broadcasted_iota(jnp.int32, (n, n), 1)
    return jnp.where((c >= r) if upper else (c <= r), 1.0, 0.0).astype(BF16)


def _forget_fwd(zf, fb, nb, seq):
    blk = 256

    def body(zf_ref, fb_ref, f_ref):
        tri = _tri(blk, False)
        live = _lane_iota((blk, LANE)) < 5
        carry = jnp.zeros((1, LANE), F32)
        for i in range(seq // blk):
            u = zf_ref[i * blk:(i + 1) * blk, :] + fb_ref[...]
            lf = jnp.where(live, jnp.minimum(u, 0.0) - jnp.log(1.0 + jnp.exp(-jnp.abs(u))), 0.0)
            hi, mid, lo = _split3(lf)
            f_ref[i * blk:(i + 1) * blk, :] = (_dot(tri, hi) + _dot(tri, mid) + _dot(tri, lo) + carry) * LOG2E
            carry = carry + jnp.sum(lf, axis=0, keepdims=True)

    return pl.pallas_call(
        body, name="forget_fwd", grid=(nb,),
        in_specs=[pl.BlockSpec((seq, LANE), lambda b: (b, 0)), pl.BlockSpec((1, LANE), lambda b: (0, 0))],
        out_specs=pl.BlockSpec((seq, LANE), lambda b: (b, 0)),
        out_shape=jax.ShapeDtypeStruct((nb * seq, LANE), F32), compiler_params=_params(("parallel",)),
    )(zf, fb)


def _forget_bwd(dfq, dfk, zf, fb, nb, seq):
    blk = 256

    def body(dfq_ref, dfk_ref, zf_ref, fb_ref, dz_ref, gb_ref):
        @pl.when(pl.program_id(0) == 0)
        def _():
            gb_ref[...] = jnp.zeros_like(gb_ref)

        tri = _tri(blk, True)
        lane = _lane_iota((blk, LANE))
        wide = _lane_iota((blk, 384))
        live = lane < 5
        carry = jnp.zeros((1, LANE), F32)
        gsum = jnp.zeros((1, LANE), F32)
        for i in reversed(range(seq // blk)):
            d = dfk_ref[i * blk:(i + 1) * blk, :]
            dq = dfq_ref[i * blk:(i + 1) * blk, :]
            for hd in range(5):
                col = jnp.sum(jnp.where(wide == 64 * hd, dq, 0.0), axis=-1, keepdims=True)
                d = d + jnp.where(lane == hd, col, 0.0)
            hi, mid, lo = _split3(d)
            dlf = _dot(tri, hi) + _dot(tri, mid) + _dot(tri, lo) + carry
            carry = carry + jnp.sum(d, axis=0, keepdims=True)
            u = zf_ref[i * blk:(i + 1) * blk, :] + fb_ref[...]
            du = jnp.where(live, dlf * jax.nn.sigmoid(-u), 0.0)
            dz_ref[i * blk:(i + 1) * blk, :] = du.astype(BF16)
            gsum = gsum + jnp.sum(du, axis=0, keepdims=True)
        gb_ref[...] += jnp.broadcast_to(gsum, gb_ref.shape)

    return pl.pallas_call(
        body, name="forget_bwd", grid=(nb,),
        in_specs=[pl.BlockSpec((seq, 384), lambda b: (b, 0)), pl.BlockSpec((seq, LANE), lambda b: (b, 0)),
                  pl.BlockSpec((seq, LANE), lambda b: (b, 0)), pl.BlockSpec((1, LANE), lambda b: (0, 0))],
        out_specs=[pl.BlockSpec((seq, LANE), lambda b: (b, 0)), pl.BlockSpec((8, LANE), lambda b: (0, 0))],
        out_shape=[jax.ShapeDtypeStruct((nb * seq, LANE), BF16), jax.ShapeDtypeStruct((8, LANE), F32)],
        compiler_params=_params(("arbitrary",)),
    )(dfq, dfk, zf, fb)


def _gate_outproj(x, gate, oa, ob, oc, zg, w_out_p, seq):
    t = x.shape[0]

    def body(x_ref, gate_ref, oa_ref, ob_ref, oc_ref, zg_ref, w_ref, y_ref, xn_ref):
        y = jnp.zeros((TM, D_MODEL), F32)
        for i, o_ref in enumerate((oa_ref, ob_ref, oc_ref)):
            cat = (o_ref[...] * _silu(zg_ref[:, 384 * i:384 * (i + 1)])).astype(BF16)
            y = y + _dot(cat, w_ref[384 * i:384 * (i + 1), :])
        y_ref[...] = y
        xn_ref[...] = x_ref[...] + gate_ref[0] * y

    return pl.pallas_call(
        body, name="gate_outproj", grid=(t // TM,),
        in_specs=[_row_spec(D_MODEL), _ex_spec(seq // TM), _row_spec(384), _row_spec(384), _row_spec(384),
                  _row_spec(D_CAT), _full_spec((D_CAT, D_MODEL))],
        out_specs=[_row_spec(D_MODEL), _row_spec(D_MODEL)],
        out_shape=[jax.ShapeDtypeStruct((t, D_MODEL), F32)] * 2, compiler_params=_params(("parallel",)),
    )(x, gate, oa, ob, oc, zg, w_out_p)


def _final_loss(x, target, g):
    t = x.shape[0]

    def body(x_ref, t_ref, g_ref, dx_ref, loss_ref, gg_ref):
        @pl.when(pl.program_id(0) == 0)
        def _():
            loss_ref[...] = jnp.zeros_like(loss_ref)
            gg_ref[...] = jnp.zeros_like(gg_ref)

        gv = g_ref[...]
        out, xh, r = _rms(x_ref[...], gv)
        err = out - t_ref[...]
        loss_ref[...] += 0.5 * jnp.sum(jnp.mean(err * err, axis=-1, keepdims=True), axis=0, keepdims=True)
        dout = err / D_MODEL
        gg_ref[...] += jnp.broadcast_to(jnp.sum(dout * xh, axis=0, keepdims=True), gg_ref.shape)
        dx_ref[...] = _rms_bwd(dout, xh, r, gv)

    return pl.pallas_call(
        body, name="final_loss", grid=(t // TM,),
        in_specs=[_row_spec(D_MODEL), _row_spec(D_MODEL), _full_spec((1, D_MODEL))],
        out_specs=[_row_spec(D_MODEL), _full_spec((8, LANE)), _full_spec((8, D_MODEL))],
        out_shape=[jax.ShapeDtypeStruct((t, D_MODEL), F32), jax.ShapeDtypeStruct((8, LANE), F32),
                   jax.ShapeDtypeStruct((8, D_MODEL), F32)],
        compiler_params=_params(("arbitrary",)),
    )(x, target, g)


def _head_masks(kind, rows, dq, h):
    lq = _lane_iota((rows, dq))
    lv = _lane_iota((rows, LANE))
    mq = (lq >= 64 * h) & (lq < 64 * h + 64)
    if kind == "A":
        mq = mq | ((lq >= 128 + 32 * h) & (lq < 160 + 32 * h))
    return mq, (lv >= 64 * h) & (lv < 64 * h + 64)


def _tile_mask(kind, tile):
    row = lax.broadcasted_iota(jnp.int32, (tile, tile), 0)
    col = lax.broadcasted_iota(jnp.int32, (tile, tile), 1)
    return (col >> 6) <= (row >> 6) if kind == "A" else col <= row


def _attn_scale(kind):
    return 96.0 ** -0.5 if kind == "A" else 0.125


BAND_W = BAND_TILES * T_BAND


def _segments(kind, qi, tile):
    r0 = qi * tile
    if kind == "B":
        lo = max(qi - (BAND_TILES - 1), 0) * tile
        return [(lo, r0 + tile, False, lo - (qi - (BAND_TILES - 1)) * tile)]
    return ([(0, r0, False, 0)] if qi else []) + [(r0, r0 + tile, True, 0)]


def _scores(kind, qh, k_ref, aux_ref, h, seg, tile, scale):
    a, b, diag, c0 = seg
    s = _dot_nt(qh, k_ref[a:b, :]) * (scale * LOG2E)
    if kind == "B":
        return s + aux_ref[h, :, c0:BAND_W]
    if kind == "C":
        s = s - aux_ref[0, h, :, a:b]
    if diag:
        s = jnp.where(_tile_mask(kind, tile), s, NEG)
    return s


FWD_AHEAD = 2
BWD_AHEAD = 2


def _run_ahead(units, first, second, depth):
    queue = [first(*u) for u in units[:depth]]
    for i, (_, h) in enumerate(units):
        if i + depth < len(units):
            queue.append(first(*units[i + depth]))
        second(h, *queue.pop(0))


def _attn_fwd(kind, q, k, v, aux, nb, seq):
    dq = q.shape[1] // 3
    tile = T_BAND if kind == "B" else T_CAUSAL
    nq = seq // tile
    scale = _attn_scale(kind)

    def body(*refs):
        if kind == "A":
            q_ref, k_ref, v_ref, o_ref, lse_ref = refs
            aux_ref = None
        else:
            q_ref, k_ref, v_ref, aux_ref, o_ref, lse_ref = refs
        def logits(qi, h):
            rows = slice(qi * tile, (qi + 1) * tile)
            q2 = q_ref[rows, :]
            mq, _ = _head_masks(kind, tile, dq, h)
            qh = jnp.where(mq, q2, jnp.zeros_like(q2))
            segs = _segments(kind, qi, tile)
            return rows, segs, [_scores(kind, qh, k_ref, aux_ref, h, seg, tile, scale) for seg in segs]

        def finish(h, rows, segs, ss):
            _, mv = _head_masks(kind, tile, dq, h)
            mx = functools.reduce(jnp.maximum, [jnp.max(s, axis=-1, keepdims=True) for s in ss])
            ps = [jnp.exp2(s - mx) for s in ss]
            l = functools.reduce(jnp.add, [jnp.sum(p, axis=-1, keepdims=True) for p in ps])
            acc = functools.reduce(jnp.add, [_dot(p.astype(BF16), v_ref[seg[0]:seg[1], :]) for p, seg in zip(ps, segs)])
            o_h = jnp.where(mv, acc / l, 0.0)
            lse_h = jnp.where(mv, mx + jnp.log(l) * LOG2E, 0.0)
            if h == 0:
                o_ref[rows, :] = o_h
                lse_ref[rows, :] = lse_h
            else:
                o_ref[rows, :] += o_h
                lse_ref[rows, :] += lse_h

        _run_ahead([(qi, h) for qi in range(nq) for h in range(2)], logits, finish, FWD_AHEAD)

    def seq_spec(cols):
        return pl.BlockSpec((seq, cols), lambda b, p: (b, p))

    in_specs = [seq_spec(dq), seq_spec(dq), seq_spec(LANE)]
    args = [q, k, v]
    if kind == "B":
        in_specs.append(pl.BlockSpec((2, tile, BAND_W), lambda b, p: (p, 0, 0)))
        args.append(aux)
    if kind == "C":
        in_specs.append(pl.BlockSpec((1, 2, 1, seq), lambda b, p: (b, p, 0, 0)))
        args.append(aux)
    return pl.pallas_call(
        body, name="attn_fwd_" + kind, grid=(nb, 3), in_specs=in_specs, out_specs=[seq_spec(LANE), seq_spec(LANE)],
        out_shape=[jax.ShapeDtypeStruct((nb * seq, 384), F32)] * 2, compiler_params=_params(("parallel", "parallel")),
    )(*args)


def _attn_bwd(kind, q, k, v, o, do, lse, aux, nb, seq):
    dq = q.shape[1] // 3
    tile = T_BAND if kind == "B" else T_CAUSAL
    nq = seq // tile
    scale = _attn_scale(kind)
    dqk_dtype = F32 if kind == "A" else BF16

    def body(*refs):
        dfr_ref = dfq_ref = dbt_ref = aux_ref = None
        if kind == "A":
            q_ref, k_ref, v_ref, o_ref, do_ref, lse_ref, dq_ref, dk_ref, dv_ref, dkt_acc, dvt_acc = refs
        elif kind == "B":
            q_ref, k_ref, v_ref, o_ref, do_ref, lse_ref, aux_ref, dq_ref, dk_ref, dv_ref, dbt_ref, dkt_acc, dvt_acc = refs
        else:
            (q_ref, k_ref, v_ref, o_ref, do_ref, lse_ref, aux_ref, dq_ref, dk_ref, dv_ref, dfr_ref, dfq_ref,
             dkt_acc, dvt_acc) = refs
        dkt_acc[...] = jnp.zeros_like(dkt_acc)
        dvt_acc[...] = jnp.zeros_like(dvt_acc)
        if kind == "C":
            dfr_ref[...] = jnp.zeros_like(dfr_ref)
        if kind == "B":
            @pl.when(pl.program_id(1) == 0)
            def _():
                dbt_ref[...] = jnp.zeros_like(dbt_ref)

        def products(qi, h):
            rows = slice(qi * tile, (qi + 1) * tile)
            q2 = q_ref[rows, :]
            mq, mv = _head_masks(kind, tile, dq, h)
            qh = jnp.where(mq, q2, jnp.zeros_like(q2))
            doh = jnp.where(mv, do_ref[rows, :], 0.0)
            dob = doh.astype(BF16)
            segs = _segments(kind, qi, tile)
            ts = [_scores(kind, qh, k_ref, aux_ref, h, seg, tile, scale) for seg in segs]
            dps = [_dot_nt(dob, v_ref[seg[0]:seg[1], :]) for seg in segs]
            return rows, segs, qh, doh, ts, dps

        def finish(h, rows, segs, qh, doh, ts, dps):
            mq, mv = _head_masks(kind, tile, dq, h)
            qht = qh.astype(F32).T.astype(BF16)
            dobt = doh.T.astype(BF16)
            head_rows = [(64 * h, 64)] + ([(128 + 32 * h, 32)] if kind == "A" else [])
            delta = jnp.sum(doh * o_ref[rows, :], axis=-1, keepdims=True)
            lseh = jnp.max(jnp.where(mv, lse_ref[rows, :], NEG), axis=-1, keepdims=True)
            rs = jnp.zeros((tile, 1), F32)
            dq_h = jnp.zeros((tile, dq), F32)
            for (a, b, _, c0), t, dp in zip(segs, ts, dps):
                p = jnp.exp2(t - lseh)
                ds = p * (dp - delta)
                if kind == "B":
                    dbt_ref[h, :, c0:BAND_W] += ds
                if kind == "C":
                    dfr_ref[0, h, :, a:b] -= jnp.sum(ds, axis=0, keepdims=True)
                    rs = rs + jnp.sum(ds, axis=-1, keepdims=True)
                dss = (ds * scale).astype(BF16)
                dvt_acc[64 * h:64 * h + 64, a:b] += _dot(dobt[64 * h:64 * h + 64, :], p.astype(BF16))
                for r0, n in head_rows:
                    dkt_acc[r0:r0 + n, a:b] += _dot(qht[r0:r0 + n, :], dss)
                dq_h = dq_h + _dot(dss, k_ref[a:b, :])
            dq_h = jnp.where(mq, dq_h, 0.0).astype(dqk_dtype)
            if h == 0:
                dq_ref[rows, :] = dq_h
            else:
                dq_ref[rows, :] += dq_h
            if kind == "C":
                if h == 0:
                    dfq_ref[rows, :] = jnp.where(mv, rs, 0.0)
                else:
                    dfq_ref[rows, :] += jnp.where(mv, rs, 0.0)

        _run_ahead([(qi, h) for qi in range(nq) for h in range(2)], products, finish, BWD_AHEAD)
        for j in range(seq // 256):
            cols = slice(256 * j, 256 * (j + 1))
            dk_ref[cols, :] = dkt_acc[:, cols].T.astype(dqk_dtype)
            dv_ref[cols, :] = dvt_acc[:, cols].T.astype(BF16)

    def seq_spec(cols):
        return pl.BlockSpec((seq, cols), lambda p, b: (b, p))

    in_specs = [seq_spec(dq), seq_spec(dq), seq_spec(LANE), seq_spec(LANE), seq_spec(LANE), seq_spec(LANE)]
    args = [q, k, v, o, do, lse]
    out_specs = [seq_spec(dq), seq_spec(dq), seq_spec(LANE)]
    out_shape = [jax.ShapeDtypeStruct((nb * seq, 3 * dq), dqk_dtype)] * 2 + [jax.ShapeDtypeStruct((nb * seq, 384), BF16)]
    if kind == "B":
        spec = pl.BlockSpec((2, tile, BAND_W), lambda p, b: (p, 0, 0))
        in_specs.append(spec)
        args.append(aux)
        out_specs.append(spec)
        out_shape.append(jax.ShapeDtypeStruct((6, tile, BAND_W), F32))
    if kind == "C":
        spec = pl.BlockSpec((1, 2, 1, seq), lambda p, b: (b, p, 0, 0))
        in_specs.append(spec)
        args.append(aux)
        out_specs += [spec, seq_spec(LANE)]
        out_shape += [jax.ShapeDtypeStruct((nb, 6, 1, seq), F32), jax.ShapeDtypeStruct((nb * seq, 384), F32)]
    return pl.pallas_call(
        body, name="attn_bwd_" + kind, grid=(3, nb), in_specs=in_specs, out_specs=out_specs, out_shape=out_shape,
        scratch_shapes=[pltpu.VMEM((dq, seq), F32), pltpu.VMEM((LANE, seq), F32)],
        compiler_params=_params(("arbitrary", "arbitrary")),
    )(*args)


BIAS_G = 768
BIAS_EDGE = BIAS_G - N_REL


def _bias_line(rel_bias):
    g = jnp.concatenate([jnp.broadcast_to(rel_bias[:, N_REL - 1:], (rel_bias.shape[0], BIAS_EDGE)),
                         jnp.flip(rel_bias, axis=1)], axis=1)
    return jnp.pad(g, ((0, 8 - g.shape[0]), (0, 0)))


def _bias_unline(dg):
    return jnp.flip(dg[:, BIAS_EDGE:], axis=1)


def _bias_expand(g8):
    def body(g_ref, out_ref):
        line = jnp.broadcast_to(g_ref[0] * LOG2E, (T_BAND, BIAS_G))
        slab = pltpu.roll(line, 1, 1, stride=1, stride_axis=0)[:, LANE:BIAS_G]
        row = lax.broadcasted_iota(jnp.int32, (T_BAND, BAND_W), 0)
        col = lax.broadcasted_iota(jnp.int32, (T_BAND, BAND_W), 1)
        hidden = ((row >= 64) & (col < 64)) | ((row < 64) & (col >= BAND_W - 64))
        out_ref[0] = jnp.where(hidden, NEG, slab)

    return pl.pallas_call(
        body, name="bias_expand", grid=(8,), in_specs=[pl.BlockSpec((1, 1, BIAS_G), lambda h: (h, 0, 0))],
        out_specs=pl.BlockSpec((1, T_BAND, BAND_W), lambda h: (h, 0, 0)),
        out_shape=jax.ShapeDtypeStruct((8, T_BAND, BAND_W), F32), compiler_params=_params(("parallel",)),
    )(g8.reshape(8, 1, BIAS_G))


def _bias_reduce(d_slab):
    def body(d_ref, out_ref):
        r = lax.broadcasted_iota(jnp.int32, (T_BAND, T_BAND), 0)
        k = lax.broadcasted_iota(jnp.int32, (T_BAND, T_BAND), 1)
        flip = jnp.where(r + k == T_BAND - 1, 1.0, 0.0).astype(BF16)
        hi, mid, lo = _split3(d_ref[0])
        d_rev = _dot(flip, hi) + _dot(flip, mid) + _dot(flip, lo)
        wide = jnp.concatenate([jnp.zeros((T_BAND, LANE), F32), d_rev, jnp.zeros((T_BAND, 2 * LANE), F32)], axis=1)
        skew = pltpu.roll(wide, 0, 1, stride=1, stride_axis=0)
        dg = jnp.sum(skew, axis=0, keepdims=True)[:, LANE:LANE + BIAS_G]
        lane = _lane_iota((1, BIAS_G))
        clipped = jnp.sum(jnp.where(lane <= BIAS_EDGE, dg, 0.0), axis=1, keepdims=True)
        out_ref[0] = jnp.where(lane == BIAS_EDGE, clipped, dg)

    return pl.pallas_call(
        body, name="bias_reduce", grid=(8,), in_specs=[pl.BlockSpec((1, T_BAND, BAND_W), lambda h: (h, 0, 0))],
        out_specs=pl.BlockSpec((1, 1, BIAS_G), lambda h: (h, 0, 0)),
        out_shape=jax.ShapeDtypeStruct((8, 1, BIAS_G), F32), compiler_params=_params(("parallel",)),
    )(d_slab).reshape(8, BIAS_G)


def _outproj_bwd(dxn, y, gate, oa, ob, oc, zg, w_out_p, w_out_pt, nb, seq):
    t = dxn.shape[0]
    tpe = seq // TM

    def body(dxn_ref, y_ref, gate_ref, oa_ref, ob_ref, oc_ref, zg_ref, w_ref, wt_ref,
             doa_ref, dob_ref, doc_ref, dzg_ref, gw_ref, dgate_ref):
        i = pl.program_id(0)

        @pl.when(i == 0)
        def _():
            gw_ref[...] = jnp.zeros_like(gw_ref)

        @pl.when(i % tpe == 0)
        def _():
            dgate_ref[...] = jnp.zeros_like(dgate_ref)

        dxn_t = dxn_ref[...]
        dgate_ref[0] += jnp.sum(dxn_t * y_ref[...], axis=0, keepdims=True)
        dy = (dxn_t * gate_ref[0]).astype(BF16)
        for gi, (o_ref, do_ref) in enumerate(((oa_ref, doa_ref), (ob_ref, dob_ref), (oc_ref, doc_ref))):
            cols = slice(384 * gi, 384 * (gi + 1))
            u = zg_ref[:, cols]
            o_t = o_ref[...]
            su = _silu(u)
            dcat = _dot(dy, wt_ref[:, cols])
            do_ref[...] = dcat * su
            dzg_ref[:, cols] = (dcat * o_t * _dsilu(u)).astype(BF16)
            gw_ref[cols, :] += _dot_tn((o_t * su).astype(BF16), dy)

    return pl.pallas_call(
        body, name="outproj_bwd", grid=(t // TM,),
        in_specs=[_row_spec(D_MODEL), _row_spec(D_MODEL), _ex_spec(tpe), _row_spec(384), _row_spec(384), _row_spec(384),
                  _row_spec(D_CAT), _full_spec((D_CAT, D_MODEL)), _full_spec((D_MODEL, D_CAT))],
        out_specs=[_row_spec(384), _row_spec(384), _row_spec(384), _row_spec(D_CAT), _full_spec((D_CAT, D_MODEL)),
                   _ex_spec(tpe)],
        out_shape=[jax.ShapeDtypeStruct((t, 384), F32)] * 3 + [jax.ShapeDtypeStruct((t, D_CAT), BF16),
                                                                jax.ShapeDtypeStruct((D_CAT, D_MODEL), F32),
                                                                jax.ShapeDtypeStruct((nb, 1, D_MODEL), F32)],
        compiler_params=_params(("arbitrary",)),
    )(dxn, y, gate, oa, ob, oc, zg, w_out_p, w_out_pt)


def _a_up_bwd(dqa, dka, dva, za, gq, gkv, w_uq_pt, w_ukv_pt, cos, sina, sinb):
    t = za.shape[0]

    def body(dq_ref, dk_ref, dv_ref, za_ref, gq_ref, gkv_ref, wqt_ref, wkvt_ref, cos_ref, sa_ref, sb_ref,
             dza_ref, gwq_ref, gwkv_ref, ggq_ref, ggkv_ref, dqb, dkvb):
        @pl.when(pl.program_id(0) == 0)
        def _():
            gwq_ref[...] = jnp.zeros_like(gwq_ref)
            gwkv_ref[...] = jnp.zeros_like(gwkv_ref)
            ggq_ref[...] = jnp.zeros_like(ggq_ref)
            ggkv_ref[...] = jnp.zeros_like(ggkv_ref)

        cos_t, sa, sb = cos_ref[...], sa_ref[...], sb_ref[...]
        dkpe = jnp.zeros((TM, LANE), F32)
        for p in range(3):
            dqb[:, 256 * p:256 * p + 128] = dq_ref[:, 256 * p:256 * p + 128].astype(BF16)
            dqb[:, 256 * p + 128:256 * p + 256] = _rope_t(dq_ref[:, 256 * p + 128:256 * p + 256], cos_t, sa, sb).astype(BF16)
            dkvb[:, 128 * p:128 * p + 128] = dk_ref[:, 256 * p:256 * p + 128].astype(BF16)
            dkpe = dkpe + dk_ref[:, 256 * p + 128:256 * p + 256]
        dkvb[:, 384:768] = dv_ref[...]
        dkpe = _rope_t(dkpe, cos_t, sa, sb)
        dkpe = jnp.where(_lane_iota((TM, LANE)) < A_ROPE, dkpe + pltpu.roll(dkpe, LANE - 32, 1), 0.0)

        gqv = gq_ref[...]
        cqn, cqh, rq = _rms(za_ref[:, 0:384], gqv)
        dq_t = dqb[...]
        gwq_ref[...] += _dot_tn(cqn.astype(BF16), dq_t)
        dcqn = _dot(dq_t, wqt_ref[...])
        ggq_ref[...] += jnp.broadcast_to(jnp.sum(dcqn * cqh, axis=0, keepdims=True), ggq_ref.shape)
        dza_ref[:, 0:384] = _rms_bwd(dcqn, cqh, rq, gqv).astype(BF16)

        gkvv = gkv_ref[...]
        ckvn, ckvh, rkv = _rms(za_ref[:, 384:640], gkvv)
        dkv_t = dkvb[...]
        gwkv_ref[...] += _dot_tn(ckvn.astype(BF16), dkv_t)
        dckvn = _dot(dkv_t, wkvt_ref[...])
        ggkv_ref[...] += jnp.broadcast_to(jnp.sum(dckvn * ckvh, axis=0, keepdims=True), ggkv_ref.shape)
        dza_ref[:, 384:640] = _rms_bwd(dckvn, ckvh, rkv, gkvv).astype(BF16)
        dza_ref[:, 640:768] = dkpe.astype(BF16)

    return pl.pallas_call(
        body, name="a_up_bwd", grid=(t // TM,),
        in_specs=[_row_spec(768), _row_spec(768), _row_spec(384), _row_spec(768), _full_spec((1, 384)),
                  _full_spec((1, 256)), _full_spec((768, 384)), _full_spec((768, 256)), _row_spec(LANE), _row_spec(LANE),
                  _row_spec(LANE)],
        out_specs=[_row_spec(768), _full_spec((384, 768)), _full_spec((256, 768)), _full_spec((8, 384)),
                   _full_spec((8, 256))],
        out_shape=[jax.ShapeDtypeStruct((t, 768), BF16), jax.ShapeDtypeStruct((384, 768), F32),
                   jax.ShapeDtypeStruct((256, 768), F32), jax.ShapeDtypeStruct((8, 384), F32),
                   jax.ShapeDtypeStruct((8, 256), F32)],
        scratch_shapes=[pltpu.VMEM((TM, 768), BF16), pltpu.VMEM((TM, 768), BF16)],
        compiler_params=_params(("arbitrary",)),
    )(dqa, dka, dva, za, gq, gkv, w_uq_pt, w_ukv_pt, cos, sina, sinb)


def _dz_cols():
    return (Z_A, Z_G) + Z_QKV + (Z_F,)


def _inproj_bwd_dx(dz, dxn, x, shift, scale, g, w_in_pt, nb, seq):
    t = x.shape[0]
    tpe = seq // TM
    cols = _dz_cols()

    def body(*refs):
        dz_refs = refs[:len(cols)]
        dxn_ref, x_ref, sh_ref, sc_ref, g_ref, wt_ref, dx_ref, dsh_ref, dsc_ref, dg_ref = refs[len(cols):]
        i = pl.program_id(0)

        @pl.when(i == 0)
        def _():
            dg_ref[...] = jnp.zeros_like(dg_ref)

        @pl.when(i % tpe == 0)
        def _():
            dsh_ref[...] = jnp.zeros_like(dsh_ref)
            dsc_ref[...] = jnp.zeros_like(dsc_ref)

        dh = jnp.zeros((TM, D_MODEL), F32)
        for ref, (c0, c1) in zip(dz_refs, cols):
            dh = dh + _dot_nt(ref[...], wt_ref[:, c0:c1])
        gv = g_ref[...]
        n, xh, r = _rms(x_ref[...], gv)
        dsh_ref[0] += jnp.sum(dh, axis=0, keepdims=True)
        dsc_ref[0] += jnp.sum(dh * n, axis=0, keepdims=True)
        dn = dh * (1.0 + sc_ref[0])
        dg_ref[...] += jnp.broadcast_to(jnp.sum(dn * xh, axis=0, keepdims=True), dg_ref.shape)
        dx_ref[...] = dxn_ref[...] + _rms_bwd(dn, xh, r, gv)

    in_specs = [_row_spec(c1 - c0) for c0, c1 in cols]
    in_specs += [_row_spec(D_MODEL), _row_spec(D_MODEL), _ex_spec(tpe), _ex_spec(tpe), _full_spec((1, D_MODEL)),
                 _full_spec((D_MODEL, NP_IN))]
    return pl.pallas_call(
        body, name="inproj_bwd_dx", grid=(t // TM,), in_specs=in_specs,
        out_specs=[_row_spec(D_MODEL), _ex_spec(tpe), _ex_spec(tpe), _full_spec((8, D_MODEL))],
        out_shape=[jax.ShapeDtypeStruct((t, D_MODEL), F32), jax.ShapeDtypeStruct((nb, 1, D_MODEL), F32),
                   jax.ShapeDtypeStruct((nb, 1, D_MODEL), F32), jax.ShapeDtypeStruct((8, D_MODEL), F32)],
        compiler_params=_params(("arbitrary",)),
    )(*dz, dxn, x, shift, scale, g, w_in_pt)


def _inproj_bwd_dw(h, dz, name, layer, both=None):
    t = h.shape[0]
    widths = [d.shape[1] for d in dz]
    total = sum(widths)

    def body(*refs):
        h_ref = refs[0]
        dz_refs = refs[1:1 + len(dz)]
        gw_ref = refs[-1]

        @pl.when(pl.program_id(0) == 0)
        def _():
            gw_ref[...] = jnp.zeros_like(gw_ref)

        h_t = h_ref[...]
        c0 = 0
        for ref, w in zip(dz_refs, widths):
            gw_ref[0, :, c0:c0 + w] += _dot_tn(h_t, ref[...])
            c0 += w

    in_specs = [_row_spec(D_MODEL)] + [_row_spec(w) for w in widths]
    args = [h, *dz]
    aliases = {}
    if both is not None:
        in_specs.append(pl.BlockSpec(memory_space=pl.ANY))
        aliases = {len(args): 0}
        args.append(both)
    return pl.pallas_call(
        body, name=name, grid=(t // TM,), in_specs=in_specs,
        out_specs=pl.BlockSpec((1, D_MODEL, total), lambda i: (layer, 0, 0)),
        out_shape=jax.ShapeDtypeStruct((DEPTH, D_MODEL, total), F32), input_output_aliases=aliases,
        compiler_params=_params(("arbitrary",)),
    )(*args)


def _ada_fwd(c_all, w_ada, b_cols):
    n = c_all.shape[0]
    cols = w_ada.shape[2]

    def body(c_ref, w_ref, b_ref, out_ref):
        act = _silu(c_ref[...]).astype(BF16)
        out_ref[0] = _dot(act, w_ref[0].astype(BF16)) + b_ref[0]

    return pl.pallas_call(
        body, name="ada_fwd", grid=(DEPTH,),
        in_specs=[pl.BlockSpec((n, D_MODEL), lambda l: (0, 0)), pl.BlockSpec((1, D_MODEL, cols), lambda l: (l, 0, 0)),
                  pl.BlockSpec((1, 1, cols), lambda l: (l, 0, 0))],
        out_specs=pl.BlockSpec((1, n, cols), lambda l: (l, 0, 0)),
        out_shape=jax.ShapeDtypeStruct((DEPTH, n, cols), F32), compiler_params=_params(("parallel",)),
    )(c_all, w_ada, b_cols)


def _ada_bwd(c_all, dmod_cols, dmod_all):
    n = c_all.shape[0]
    cols = dmod_cols.shape[2]

    def body(c_ref, dc_ref, da_ref, gw_ref, gb_ref):
        act = _silu(c_ref[...]).astype(BF16)
        gw_ref[0] = _dot_tn(act, dc_ref[0].astype(BF16))
        gb_ref[0] = jnp.sum(da_ref[0], axis=0, keepdims=True)

    return pl.pallas_call(
        body, name="ada_bwd", grid=(DEPTH,),
        in_specs=[pl.BlockSpec((n, D_MODEL), lambda l: (0, 0)), pl.BlockSpec((1, n, cols), lambda l: (l, 0, 0)),
                  pl.BlockSpec((1, n, 3 * D_MODEL), lambda l: (l, 0, 0))],
        out_specs=[pl.BlockSpec((1, D_MODEL, cols), lambda l: (l, 0, 0)),
                   pl.BlockSpec((1, 1, 3 * D_MODEL), lambda l: (l, 0, 0))],
        out_shape=[jax.ShapeDtypeStruct((DEPTH, D_MODEL, cols), F32), jax.ShapeDtypeStruct((DEPTH, 1, 3 * D_MODEL), F32)],
        compiler_params=_params(("parallel",)),
    )(c_all, dmod_cols, dmod_all)


def _sum_blocks(parts, name):
    n, rows, cols = parts.shape
    tr = rows if rows <= 256 else 8 * next(d for d in range(32, 0, -1) if (rows // 8) % d == 0)

    def body(p_ref, out_ref):
        acc = p_ref[0].astype(F32)
        for k in range(1, n):
            acc = acc + p_ref[k].astype(F32)
        out_ref[...] = acc

    return pl.pallas_call(
        body, name=name, grid=(rows // tr,), in_specs=[pl.BlockSpec((n, tr, cols), lambda i: (0, i, 0))],
        out_specs=pl.BlockSpec((tr, cols), lambda i: (i, 0)), out_shape=jax.ShapeDtypeStruct((rows, cols), F32),
        compiler_params=_params(("parallel",)),
    )(parts)


def _adamw(w, g, m, v, name):
    nl, rows, cols = w.shape
    if rows * cols <= 64 * 1024:
        tr = rows
        tl = next(t for t in range(nl, 0, -1) if nl % t == 0 and t * max(rows, 8) * cols <= 512 * 1024)
    else:
        tl = 1
        tr = next(t for t in (rows, 256, 128, 64, 32, 16, 8) if rows % t == 0 and t * cols <= 256 * 1024)

    def body(w_ref, g_ref, m_ref, v_ref, d_ref, mo_ref, vo_ref):
        gv = g_ref[...]
        mn = ADAM_B1 * m_ref[...] + (1.0 - ADAM_B1) * gv
        vn = ADAM_B2 * v_ref[...] + (1.0 - ADAM_B2) * jnp.square(gv)
        m_hat = mn / (1.0 - ADAM_B1 ** ADAM_STEP)
        v_hat = vn / (1.0 - ADAM_B2 ** ADAM_STEP)
        d_ref[...] = -ADAM_LR * (m_hat / (jnp.sqrt(v_hat) + ADAM_EPS) + ADAM_WD * w_ref[...])
        mo_ref[...] = mn
        vo_ref[...] = vn

    spec = pl.BlockSpec((tl, tr, cols), lambda l, i: (l, i, 0))
    return pl.pallas_call(
        body, name=name, grid=(nl // tl, rows // tr), in_specs=[spec] * 4, out_specs=[spec] * 3,
        out_shape=[jax.ShapeDtypeStruct((nl, rows, cols), F32)] * 3, compiler_params=_params(("parallel", "parallel")),
    )(w, g, m, v)


ALL_FLIPS = tuple(range(1, N_DEV))


def _gather_small(src, name):
    rows, cols = src.shape

    def body(src_ref, dst_ref, send_sems, recv_sems, local_sem):
        x, y, c = lax.axis_index("x"), lax.axis_index("y"), lax.axis_index("c")
        me = 4 * x + 2 * y + c

        def copy(i, f, slot):
            to = (1 - x if f & 4 else x, 1 - y if f & 2 else y, 1 - c if f & 1 else c)
            return pltpu.make_async_remote_copy(
                src_ref=src_ref, dst_ref=dst_ref.at[slot], send_sem=send_sems.at[i], recv_sem=recv_sems.at[i],
                device_id=to, device_id_type=pl.DeviceIdType.MESH)

        own = pltpu.make_async_copy(src_ref, dst_ref.at[me], local_sem)
        own.start()
        sends = [copy(i, f, me) for i, f in enumerate(ALL_FLIPS)]
        for cp in sends:
            cp.start()
        for i, f in enumerate(ALL_FLIPS):
            copy(i, f, me ^ f).wait_recv()
        for cp in sends:
            cp.wait_send()
        own.wait()

    nf = len(ALL_FLIPS)
    return pl.pallas_call(
        body, name=name, out_shape=jax.ShapeDtypeStruct((N_DEV, rows, cols), src.dtype),
        in_specs=[pl.BlockSpec(memory_space=pl.ANY)], out_specs=pl.BlockSpec(memory_space=pl.ANY),
        scratch_shapes=[pltpu.SemaphoreType.DMA((nf,)), pltpu.SemaphoreType.DMA((nf,)), pltpu.SemaphoreType.DMA],
    )(src)


def _transfer(name, srcs, dst_shapes, plan, in_place=False):
    n_arr = len(srcs)
    probe = plan(0, 0, 0)
    n_steps = len(probe)

    def body(*refs):
        src_refs, dst_refs = refs[:n_arr], refs[n_arr:2 * n_arr]
        send_sems, recv_sems, local_sems = refs[2 * n_arr:]
        x, y, c = lax.axis_index("x"), lax.axis_index("y"), lax.axis_index("c")
        steps = plan(x, y, c)

        def rows(ref, r0, n):
            return ref.at[:, pl.ds(r0, n), :]

        def arrival(t):
            a, _, _, n, _, f, _ = steps[t]
            return pltpu.make_async_remote_copy(
                src_ref=rows(dst_refs[a], 0, n), dst_ref=rows(dst_refs[a], 0, n), send_sem=send_sems.at[t],
                recv_sem=recv_sems.at[t], device_id=(x, y, c), device_id_type=pl.DeviceIdType.MESH)

        arrived, started = set(), []
        for t, (a, from_dst, sr, n, dr, f, after) in enumerate(steps):
            for u in after:
                if u not in arrived:
                    arrival(u).wait_recv()
                    arrived.add(u)
            src = rows(dst_refs[a] if from_dst else src_refs[a], sr, n)
            dst = rows(dst_refs[a], dr, n)
            if f == 0:
                cp = pltpu.make_async_copy(src, dst, local_sems.at[t])
            else:
                to = (1 - x if f & 4 else x, 1 - y if f & 2 else y, 1 - c if f & 1 else c)
                cp = pltpu.make_async_remote_copy(src_ref=src, dst_ref=dst, send_sem=send_sems.at[t],
                                                  recv_sem=recv_sems.at[t], device_id=to,
                                                  device_id_type=pl.DeviceIdType.MESH)
            cp.start()
            started.append(cp)
        for t, step in enumerate(steps):
            if step[5] != 0 and t not in arrived:
                arrival(t).wait_recv()
        for cp, step in zip(started, steps):
            if step[5] == 0:
                cp.wait()
            else:
                cp.wait_send()

    any_spec = pl.BlockSpec(memory_space=pl.ANY)
    return pl.pallas_call(
        body, name=name, out_shape=[jax.ShapeDtypeStruct(s, d) for s, d in dst_shapes],
        in_specs=[any_spec] * n_arr, out_specs=[any_spec] * n_arr,
        input_output_aliases={a: a for a in range(n_arr)} if in_place else {},
        scratch_shapes=[pltpu.SemaphoreType.DMA((n_steps,)), pltpu.SemaphoreType.DMA((n_steps,)),
                        pltpu.SemaphoreType.DMA((n_steps,))],
    )(*srcs)


CHIP_FLIPS = (2, 4, 6)


def _gather_plan(chip_rows):
    def plan(x, y, c):
        steps = []
        chip = 2 * x + y
        for a, rc in enumerate(chip_rows):
            h = rc // 2
            first = (h // 32) * 16
            mine = rc * chip + h * c
            from_x, from_y, diag = rc * (chip ^ 2) + h * c, rc * (chip ^ 1) + h * c, rc * (chip ^ 3) + h * c
            steps.append((a, False, h * c, h, mine, 0, ()))
            to_x = len(steps)
            steps.append((a, False, h * c, h, mine, 4, ()))
            to_y = len(steps)
            steps.append((a, False, h * c, h, mine, 2, ()))
            fwd_y = len(steps)
            steps.append((a, True, from_x, first, from_x, 2, (to_x,)))
            fwd_x = len(steps)
            steps.append((a, True, from_y + first, h - first, from_y + first, 4, (to_y,)))
            steps.append((a, False, h * c, h, mine, 1, ()))
            steps.append((a, True, from_x, h, from_x, 1, (to_x,)))
            steps.append((a, True, from_y, h, from_y, 1, (to_y,)))
            steps.append((a, True, diag, first, diag, 1, (fwd_y,)))
            steps.append((a, True, diag + first, h - first, diag + first, 1, (fwd_x,)))
        return steps
    return plan


def _pair_reduce_plan(chip_rows):
    def plan(x, y, c):
        steps = []
        for a, rc in enumerate(chip_rows):
            h = rc // 2
            for j in range(4):
                steps.append((a, False, rc * j + h * (1 - c), h, h * j, 1, ()))
        return steps
    return plan


def _chip_scatter_plan(chip_rows):
    def plan(x, y, c):
        steps = []
        for a, rc in enumerate(chip_rows):
            h = rc // 2
            for k, f in enumerate(CHIP_FLIPS):
                steps.append((a, False, h * ((2 * x + y) ^ (f >> 1)), h, h * k, f, ()))
        return steps
    return plan


def _pair_share_plan(chip_rows):
    def plan(x, y, c):
        return [(a, True, (rc // 2) * c, rc // 2, (rc // 2) * c, 1, ()) for a, rc in enumerate(chip_rows)]
    return plan


def _tile_rows(h):
    return h


def _sum_pair(partial, recv, core, rc, name):
    nl, _, cols = partial.shape
    h = rc // 2
    tr = _tile_rows(h)

    def body(c_ref, p_ref, r_ref, out_ref):
        out_ref[...] = (p_ref[...] + r_ref[...]).astype(BF16)

    spec = pl.BlockSpec((1, tr, cols), lambda l, j, i, c_ref: (l, (h // tr) * j + i, 0))
    return pl.pallas_call(
        body, name=name, out_shape=jax.ShapeDtypeStruct((nl, 4 * h, cols), BF16),
        grid_spec=pltpu.PrefetchScalarGridSpec(
            num_scalar_prefetch=1, grid=(nl, 4, h // tr),
            in_specs=[pl.BlockSpec((1, tr, cols), lambda l, j, i, c_ref: (l, (rc // tr) * j + (h // tr) * c_ref[0] + i, 0)),
                      spec],
            out_specs=spec),
        compiler_params=_params(("parallel", "parallel", "parallel")),
    )(core, partial, recv)


def _sum_chips(chip_sum, recv, place, rc, name):
    nl, _, cols = chip_sum.shape
    h = rc // 2
    tr = _tile_rows(h)

    def body(s_ref, own_ref, r_ref, out_ref):
        acc = own_ref[0].astype(F32)
        for k in range(3):
            acc = acc + r_ref[0, k].astype(F32)
        out_ref[0] = acc

    return pl.pallas_call(
        body, name=name, out_shape=jax.ShapeDtypeStruct((nl, rc, cols), F32),
        grid_spec=pltpu.PrefetchScalarGridSpec(
            num_scalar_prefetch=1, grid=(nl, h // tr),
            in_specs=[pl.BlockSpec((1, tr, cols), lambda l, i, s_ref: (l, (h // tr) * s_ref[0] + i, 0)),
                      pl.BlockSpec((1, 3, tr, cols), lambda l, i, s_ref: (l, 0, i, 0))],
            out_specs=pl.BlockSpec((1, tr, cols), lambda l, i, s_ref: (l, (h // tr) * s_ref[1] + i, 0))),
        compiler_params=_params(("parallel", "parallel")),
    )(place, chip_sum, recv.reshape(nl, 3, h, cols))


def _pad_cols(a, n):
    return a if n == 0 else jnp.pad(a, ((0, 0), (0, n)))


def _in_to_padded(w):
    return jnp.concatenate([_pad_cols(w[:, a:b], z) for a, b, z in IN_PIECES], axis=1)


def _in_cols_first(ga, gb):
    split = Z_G[1]
    pos, out = 0, {}
    for a, b, z in IN_PIECES:
        src, off = (ga, pos) if pos < split else (gb, pos - split)
        out[a] = jnp.transpose(src[:, :, off:off + (b - a)], (2, 0, 1))
        pos += (b - a) + z
    return jnp.concatenate([out[a] for a in sorted(out)], axis=0)


def _out_to_padded(w):
    z = jnp.zeros((64, w.shape[1]), w.dtype)
    return jnp.concatenate([w[0:384], w[384:704], z, w[704:1024], z], axis=0)


def _out_from_padded(gp):
    return jnp.concatenate([gp[0:384], gp[384:704], gp[768:1088]], axis=0)


def _uq_to_padded(w):
    parts = []
    for p in range(3):
        h0, h1 = 2 * p, 2 * p + 1
        parts += [w[:, 96 * h0:96 * h0 + 64], w[:, 96 * h1:96 * h1 + 64], w[:, 96 * h0 + 64:96 * h0 + 96],
                  w[:, 96 * h1 + 64:96 * h1 + 96], jnp.zeros((w.shape[0], 64), w.dtype)]
    return jnp.concatenate(parts, axis=1)


def _uq_from_padded(gp):
    parts = []
    for h in range(6):
        p, s = h // 2, h % 2
        parts += [gp[:, 256 * p + 64 * s:256 * p + 64 * s + 64], gp[:, 256 * p + 128 + 32 * s:256 * p + 160 + 32 * s]]
    return jnp.concatenate(parts, axis=1)


def _ukv_to_padded(w):
    return jnp.concatenate([w[:, 128 * h:128 * h + 64] for h in range(6)]
                           + [w[:, 128 * h + 64:128 * h + 128] for h in range(6)], axis=1)


def _ukv_from_padded(gp):
    parts = []
    for h in range(6):
        parts += [gp[:, 64 * h:64 * h + 64], gp[:, 384 + 64 * h:384 + 64 * h + 64]]
    return jnp.concatenate(parts, axis=1)


LR_ROWS = 224
SHARD_ROWS = (256, 256, 256, LR_ROWS)


def _pack_lowrank(w_uq, w_ukv):
    flat = jnp.concatenate([w_uq.reshape(-1), w_ukv.reshape(-1)])
    return jnp.pad(flat, (0, LR_ROWS * PACK_COLS - flat.shape[0])).reshape(1, LR_ROWS, PACK_COLS)


def _unpack_lowrank(packed):
    flat = packed.reshape(-1)
    n_uq = DEPTH * A_Q_RANK * 144
    n_ukv = DEPTH * A_KV_RANK * 192
    return flat[0:n_uq].reshape(DEPTH, A_Q_RANK, 144), flat[n_uq:n_uq + n_ukv].reshape(DEPTH, A_KV_RANK, 192)


def _rope_tables(positions):
    t = positions.size
    inv = ROPE_THETA ** (-jnp.arange(0, A_ROPE, 2, dtype=F32) / A_ROPE)
    inv_row = jnp.pad(jnp.tile(inv, 4), (0, 64)).reshape(1, LANE)

    def body(p_ref, i_ref, c_ref, sa_ref, sb_ref):
        ang = p_ref[...].astype(F32) * i_ref[...]
        lane = _lane_iota((TM, LANE))
        live = lane < 64
        second = (lane & 31) >= 16
        s = jnp.sin(ang)
        c_ref[...] = jnp.where(live, jnp.cos(ang), 0.0)
        sa_ref[...] = jnp.where(live & jnp.logical_not(second), -s, 0.0)
        sb_ref[...] = jnp.where(live & second, s, 0.0)

    return pl.pallas_call(
        body, name="rope_tables", grid=(t // TM,), in_specs=[_row_spec(1), _full_spec((1, LANE))],
        out_specs=[_row_spec(LANE)] * 3, out_shape=[jax.ShapeDtypeStruct((t, LANE), F32)] * 3,
        compiler_params=_params(("parallel",)),
    )(positions.reshape(t, 1), inv_row)


def _rows(a, n):
    flat = a.reshape(-1)
    return jnp.pad(flat, (0, n * LANE - flat.shape[0])).reshape(n, LANE)


def _forward_backward(x, mod, tables, target, weights, small, nb, seq):
    cos, sina, sinb = tables
    saved = []
    for l in range(DEPTH):
        w, s = weights[l], small[l]
        shift = mod[l][:, None, 0:D_MODEL]
        scale = mod[l][:, None, D_MODEL:2 * D_MODEL]
        gate = mod[l][:, None, 2 * D_MODEL:]
        h, za, zg, qb, kb, vb, qc, kc, vc, zf = _ln_inproj(x, shift, scale, s["norm_g"], w["in"], seq)
        qa, ka, va = _a_up(za, s["gq"], s["gkv"], w["uq"], w["ukv"], cos, sina, sinb)
        bias = _bias_expand(s["g8"])[0:6]
        f = _forget_fwd(zf, s["fb"], nb, seq)
        frow = jnp.pad(f[:, 0:5].reshape(nb, seq, 5).transpose(0, 2, 1), ((0, 0), (0, 1), (0, 0)))
        frow = frow.reshape(nb, 6, 1, seq)
        oa, lse_a = _attn_fwd("A", qa, ka, va, None, nb, seq)
        ob, lse_b = _attn_fwd("B", qb, kb, vb, bias, nb, seq)
        oc, lse_c = _attn_fwd("C", qc, kc, vc, frow, nb, seq)
        y, xn = _gate_outproj(x, gate, oa, ob, oc, zg, w["out"], seq)
        saved.append(dict(x=x, h=h, za=za, zg=zg, zf=zf, y=y, shift=shift, scale=scale, gate=gate, bias=bias, frow=frow,
                          a=(qa, ka, va, oa, lse_a), b=(qb, kb, vb, ob, lse_b), c=(qc, kc, vc, oc, lse_c)))
        x = xn
    dx, loss8, gfinal8 = _final_loss(x, target, small[0]["final_g"])
    grads = []
    gw_in = (None, None)
    for l in reversed(range(DEPTH)):
        w, s, sv = weights[l], small[l], saved[l]
        qa, ka, va, oa, lse_a = sv["a"]
        qb, kb, vb, ob, lse_b = sv["b"]
        qc, kc, vc, oc, lse_c = sv["c"]
        doa, dob, doc, dzg, gw_out, dgate = _outproj_bwd(dx, sv["y"], sv["gate"], oa, ob, oc, sv["zg"], w["out"],
                                                          w["out_t"], nb, seq)
        dqa, dka, dva = _attn_bwd("A", qa, ka, va, oa, doa, lse_a, None, nb, seq)
        dqb, dkb, dvb, dbt = _attn_bwd("B", qb, kb, vb, ob, dob, lse_b, sv["bias"], nb, seq)
        dqc, dkc, dvc, dfr, dfq = _attn_bwd("C", qc, kc, vc, oc, doc, lse_c, sv["frow"], nb, seq)
        dg = _bias_reduce(jnp.pad(dbt, ((0, 2), (0, 0), (0, 0))))
        grb = jnp.pad(_bias_unline(dg), ((0, 0), (0, 384 - N_REL)))
        dfk = dfr.reshape(nb, 6, seq).transpose(0, 2, 1).reshape(nb * seq, 6)
        dzf, gfb = _forget_bwd(dfq, jnp.pad(dfk, ((0, 0), (0, LANE - 6))), sv["zf"], s["fb"], nb, seq)
        dza, gw_uq, gw_ukv, ggq, ggkv = _a_up_bwd(dqa, dka, dva, sv["za"], s["gq"], s["gkv"], w["uq_t"], w["ukv_t"],
                                                  cos, sina, sinb)
        dz = (dza, dzg, dqb, dkb, dvb, dqc, dkc, dvc, dzf)
        dx, dshift, dscale, gnorm = _inproj_bwd_dx(dz, dx, sv["x"], sv["shift"], sv["scale"], s["norm_g"], w["in"],
                                                   nb, seq)
        gw_in = (_inproj_bwd_dw(sv["h"], dz[0:2], "inproj_bwd_dw0", l, gw_in[0]),
                 _inproj_bwd_dw(sv["h"], dz[2:], "inproj_bwd_dw1", l, gw_in[1]))
        dmod = jnp.concatenate([dshift[:, 0], dscale[:, 0], dgate[:, 0]], axis=1)
        grads.append(dict(w_out=gw_out, w_uq=gw_uq, w_ukv=gw_ukv, dmod=dmod, norm_g=gnorm[0], gq=ggq[0],
                          gkv=ggkv[0], rb8=grb, fb=gfb[0]))
    grads.reverse()
    return loss8[0, 0], dx, grads, gfinal8[0], gw_in


def _layer_weights(w_in, w_out, w_uq, w_ukv):
    wi, wo, wq, wkv = _in_to_padded(w_in), _out_to_padded(w_out), _uq_to_padded(w_uq), _ukv_to_padded(w_ukv)
    return {"in": wi, "out": wo, "out_t": wo.T, "uq": wq, "uq_t": wq.T, "ukv": wkv, "ukv_t": wkv.T}


def _layer_small(norm_g, gq, gkv, rel_bias, forget_b, final_g):
    fb = jnp.pad(forget_b, (0, LANE - 5)).reshape(1, LANE)
    return dict(norm_g=norm_g.reshape(1, -1), gq=gq.reshape(1, -1), gkv=gkv.reshape(1, -1), g8=_bias_line(rel_bias), fb=fb,
                final_g=final_g.reshape(1, -1))


def _small_payload(per_layer, final_g, loss):
    def stack(key):
        return jnp.stack([p[key] for p in per_layer])

    def rows(a, rng):
        return _rows(a, rng[1] - rng[0])

    parts = [rows(stack("dmod"), PAY_DMOD), rows(stack("norm_g"), PAY_NORM), rows(stack("gq"), PAY_GQ),
             rows(stack("gkv"), PAY_GKV), rows(stack("rb8"), PAY_RB), rows(stack("fb"), PAY_FB),
             rows(final_g, PAY_FINAL), rows(loss, PAY_LOSS)]
    return jnp.concatenate(parts, axis=0)


def _payload_split(pay):
    def take(rng, shape):
        n = 1
        for d in shape:
            n *= d
        return pay[rng[0]:rng[1]].reshape(-1)[0:n].reshape(shape)

    norm_g = take(PAY_NORM, (DEPTH, D_MODEL))
    gq = take(PAY_GQ, (DEPTH, A_Q_RANK))
    gkv = take(PAY_GKV, (DEPTH, A_KV_RANK))
    rb = take(PAY_RB, (DEPTH, 8, 384))[:, 0:5, 0:N_REL]
    fb = take(PAY_FB, (DEPTH, LANE))[:, 0:5]
    final_g = take(PAY_FINAL, (D_MODEL,))
    return norm_g, gq, gkv, rb, fb, final_g


def kernel(x, c, positions, w_ada, b_ada, norm_g, w_in, a_q_norm_g, a_w_uq, a_kv_norm_g, a_w_ukv, b_rel_bias, c_forget_b, w_out, final_g, loss_target, m_w_ada, m_b_ada, m_norm_g, m_w_in, m_a_q_norm_g, m_a_w_uq, m_a_kv_norm_g, m_a_w_ukv, m_b_rel_bias, m_c_forget_b, m_w_out, m_final_g, v_w_ada, v_b_ada, v_norm_g, v_w_in, v_a_q_norm_g, v_a_w_uq, v_a_kv_norm_g, v_a_w_ukv, v_b_rel_bias, v_c_forget_b, v_w_out, v_final_g):
    nb, seq, _ = x.shape
    ix, iy, ic = lax.axis_index("x"), lax.axis_index("y"), lax.axis_index("c")
    chip = 2 * ix + iy
    me = 2 * chip + ic

    weight_plan = _gather_plan((256, 256, LR_ROWS))

    def gather_plan(x, y, c_):
        me_ = 4 * x + 2 * y + c_
        return weight_plan(x, y, c_) + [(3, False, 0, 8, 8 * me_, f, ()) for f in range(N_DEV)]

    full_in, full_out, full_lr, c_rows = _transfer(
        "gather_weights", [w_in.astype(BF16), w_out.astype(BF16), _pack_lowrank(a_w_uq, a_w_ukv).astype(BF16),
                           jnp.pad(c, ((0, 8 - nb), (0, 0)))[None]],
        [((DEPTH, D_MODEL, N_IN), BF16), ((DEPTH, D_MODEL, D_MODEL), BF16), ((1, 4 * LR_ROWS, PACK_COLS), BF16),
         ((1, 8 * N_DEV, D_MODEL), F32)], gather_plan)
    lowrank = [_unpack_lowrank(full_lr[0, LR_ROWS * j:LR_ROWS * (j + 1)]) for j in range(4)]
    full_uq = jnp.concatenate([s[0] for s in lowrank], axis=2)
    full_ukv = jnp.concatenate([s[1] for s in lowrank], axis=2)
    weights = [_layer_weights(full_in[l], full_out[l], full_uq[l], full_ukv[l]) for l in range(DEPTH)]
    small = [_layer_small(norm_g[l], a_q_norm_g[l], a_kv_norm_g[l], b_rel_bias[l], c_forget_b[l], final_g)
             for l in range(DEPTH)]

    c_all = c_rows.reshape(N_DEV, 8, D_MODEL)[:, 0:nb].reshape(N_DEV * nb, D_MODEL)
    cols = w_ada.shape[2]
    b_cols = lax.dynamic_slice_in_dim(b_ada, chip * cols, cols, axis=1)[:, None, :]
    mod_cols = _ada_fwd(c_all, w_ada, b_cols)
    mod_g = _gather_small(mod_cols.reshape(DEPTH * N_DEV * nb, cols), "gather_mod")
    mod_all = jnp.concatenate([mod_g[2 * j].reshape(DEPTH, N_DEV * nb, cols) for j in range(4)], axis=2)
    mod = lax.dynamic_slice_in_dim(mod_all, me * nb, nb, axis=1)

    tables = _rope_tables(positions)
    loss_part, dx, grads, gfinal, gw_in = _forward_backward(
        x.reshape(nb * seq, D_MODEL), mod, tables, loss_target.reshape(nb * seq, D_MODEL), weights, small, nb, seq)

    pay = _small_payload(grads, gfinal, loss_part)
    pay_all = _gather_small(pay, "gather_small")
    tot = _sum_blocks(pay_all, "sum_small")
    loss = tot[PAY_LOSS[0], 0]
    dmod_all = pay_all[:, PAY_DMOD[0]:PAY_DMOD[1]].reshape(N_DEV, -1)[:, 0:DEPTH * nb * 3 * D_MODEL]
    dmod_all = dmod_all.reshape(N_DEV, DEPTH, nb, 3 * D_MODEL).transpose(1, 0, 2, 3)
    dmod_all = dmod_all.reshape(DEPTH, N_DEV * nb, 3 * D_MODEL)
    my_cols = lax.dynamic_slice_in_dim(dmod_all, chip * cols, cols, axis=2)
    g_w_ada, g_b_ada = _ada_bwd(c_all, my_cols, dmod_all)
    g_b_ada = g_b_ada[:, 0]

    g_uq = jnp.stack([_uq_from_padded(g["w_uq"]) for g in grads])
    g_ukv = jnp.stack([_ukv_from_padded(g["w_ukv"]) for g in grads])
    g_lr = jnp.concatenate([_pack_lowrank(g_uq[:, :, 144 * j:144 * (j + 1)], g_ukv[:, :, 192 * j:192 * (j + 1)])
                            for j in range(4)], axis=1)
    partials = [gw_in[0], gw_in[1], jnp.stack([_out_from_padded(g["w_out"]) for g in grads]), g_lr]
    shapes = [(p.shape[0], p.shape[2]) for p in partials]
    halves = [r // 2 for r in SHARD_ROWS]
    core_s = jnp.reshape(ic, (1,)).astype(jnp.int32)
    place_s = jnp.stack([chip, ic]).astype(jnp.int32)
    from_pair = _transfer("pair_reduce", partials, [((nl, 4 * h, nc), F32) for (nl, nc), h in zip(shapes, halves)],
                          _pair_reduce_plan(SHARD_ROWS))
    chip_sums = [_sum_pair(p, r, core_s, rc, "sum_pair%d" % i)
                 for i, (p, r, rc) in enumerate(zip(partials, from_pair, SHARD_ROWS))]
    from_chips = _transfer("chip_scatter", chip_sums, [((nl, 3 * h, nc), BF16) for (nl, nc), h in zip(shapes, halves)],
                           _chip_scatter_plan(SHARD_ROWS))
    reduced = [_sum_chips(s, r, place_s, rc, "sum_chips%d" % i)
               for i, (s, r, rc) in enumerate(zip(chip_sums, from_chips, SHARD_ROWS))]
    g_in_a, g_in_b, g_out_sh, g_lr_sh = _transfer("pair_share", reduced, [(r.shape, F32) for r in reduced],
                                                  _pair_share_plan(SHARD_ROWS), in_place=True)
    g_uq_sh, g_ukv_sh = _unpack_lowrank(g_lr_sh[0])

    def cols_first(a):
        return jnp.transpose(a, (2, 0, 1))

    g_in_t = _in_cols_first(g_in_a, g_in_b)
    upd_in = tuple(jnp.transpose(a, (1, 2, 0)) for a in _adamw(cols_first(w_in), g_in_t, cols_first(m_w_in),
                                                               cols_first(v_w_in), "adamw_in"))
    gw = (jnp.transpose(g_in_t, (1, 2, 0)), g_out_sh, g_uq_sh, g_ukv_sh)
    upd = [upd_in, _adamw(w_out, gw[1], m_w_out, v_w_out, "adamw_out"),
           _adamw(a_w_uq, gw[2], m_a_w_uq, v_a_w_uq, "adamw_uq"), _adamw(a_w_ukv, gw[3], m_a_w_ukv, v_a_w_ukv, "adamw_ukv")]
    dw, mw, vw = (tuple(u[i] for u in upd) for i in range(3))
    d_ada, m_ada, v_ada = _adamw(w_ada, g_w_ada, m_w_ada, v_w_ada, "adamw_ada")

    def adam_small(w, g, m, v, name):
        shape3 = (1,) * (3 - w.ndim) + w.shape
        return tuple(a.reshape(w.shape) for a in _adamw(w.reshape(shape3), g.reshape(shape3), m.reshape(shape3),
                                                        v.reshape(shape3), name))

    d_b, m_b, v_b = adam_small(b_ada, g_b_ada, m_b_ada, v_b_ada, "adamw_b_ada")
    gs = _payload_split(tot)
    small_upd = [adam_small(w, g, m, v, "adamw_small%d" % i) for i, (w, g, m, v) in enumerate(zip(
        (norm_g, a_q_norm_g, a_kv_norm_g, b_rel_bias, c_forget_b, final_g), gs,
        (m_norm_g, m_a_q_norm_g, m_a_kv_norm_g, m_b_rel_bias, m_c_forget_b, m_final_g),
        (v_norm_g, v_a_q_norm_g, v_a_kv_norm_g, v_b_rel_bias, v_c_forget_b, v_final_g)))]
    ds, ms, vs = (tuple(u[i] for u in small_upd) for i in range(3))

    def ordered(ada, b, sm, big):
        ng, gq, gkv, rb, fb, fg = sm
        b_in, b_out, b_uq, b_ukv = big
        return (ada, b, ng, b_in, gq, b_uq, gkv, b_ukv, rb, fb, b_out, fg)

    return (loss, dx.reshape(nb, seq, D_MODEL), *ordered(g_w_ada, g_b_ada, gs, gw), *ordered(d_ada, d_b, ds, dw),
            *ordered(m_ada, m_b, ms, mw), *ordered(v_ada, v_b, vs, vw))
```

```python
import functools

import jax
import jax.numpy as jnp
from jax import lax
from jax.experimental import pallas as pl
from jax.experimental.pallas import tpu as pltpu

F32 = jnp.float32
BF16 = jnp.bfloat16

D_MODEL = 1024
DEPTH = 2
EPS = 1e-6
NEG = -1e30
LOG2E = 1.4426950408889634
ROPE_THETA = 10000.0
A_ROPE = 32
A_Q_RANK = 384
A_KV_RANK = 256
REL_CLIP = 128
N_REL = 2 * REL_CLIP + 1
N_IN = 3621

ADAM_LR = 0.001
ADAM_B1 = 0.9
ADAM_B2 = 0.999
ADAM_EPS = 1e-08
ADAM_WD = 0.01
ADAM_STEP = 10

LANE = 128
VMEM_LIMIT = 56 * 1024 * 1024

NP_IN = 4352
Z_A = (0, 768)
Z_G = (768, 1920)
Z_QKV = tuple((1920 + 384 * i, 1920 + 384 * (i + 1)) for i in range(6))
Z_F = (4224, 4352)
IN_PIECES = ((0, 672, 96), (672, 1056, 0), (2016, 2336, 64), (3301, 3621, 64), (1056, 1376, 64), (1376, 1696, 64),
             (1696, 2016, 64), (2336, 2656, 64), (2656, 2976, 64), (2976, 3296, 64), (3296, 3301, 123))
D_CAT = 1152

TM = 512
T_CAUSAL = 256
T_BAND = 128
BAND_TILES = 5
N_DEV = 8

PAY_DMOD = (0, 96)
PAY_NORM = (96, 112)
PAY_GQ = (112, 120)
PAY_GKV = (120, 128)
PAY_RB = (128, 176)
PAY_FB = (176, 184)
PAY_FINAL = (184, 192)
PAY_LOSS = (192, 200)

PACK_COLS = 1024


def _params(sem=None):
    return pltpu.CompilerParams(dimension_semantics=sem, vmem_limit_bytes=VMEM_LIMIT)


def _lane_iota(shape):
    return lax.broadcasted_iota(jnp.int32, shape, len(shape) - 1)


def _silu(u):
    return u * jax.nn.sigmoid(u)


def _dsilu(u):
    s = jax.nn.sigmoid(u)
    return s * (1.0 + u * (1.0 - s))


def _rms(x, g):
    r = lax.rsqrt(jnp.mean(x * x, axis=-1, keepdims=True) + EPS)
    xh = x * r
    return xh * g, xh, r


def _rms_bwd(dy, xh, r, g):
    dxh = dy * g
    return r * (dxh - xh * jnp.mean(dxh * xh, axis=-1, keepdims=True))


def _rope(x, cos, sina, sinb):
    return x * cos + pltpu.roll(x, 16, 1) * sinb + pltpu.roll(x, LANE - 16, 1) * sina


def _rope_t(dy, cos, sina, sinb):
    return dy * cos + pltpu.roll(dy * sinb, LANE - 16, 1) + pltpu.roll(dy * sina, 16, 1)


def _split3(x):
    hi = x.astype(BF16)
    r1 = x - hi.astype(F32)
    mid = r1.astype(BF16)
    lo = (r1 - mid.astype(F32)).astype(BF16)
    return hi, mid, lo


def _dot(a, b):
    return jnp.dot(a, b, preferred_element_type=F32)


def _dot_nt(a, b):
    return lax.dot_general(a, b, (((1,), (1,)), ((), ())), preferred_element_type=F32)


def _dot_tn(a, b):
    return lax.dot_general(a, b, (((0,), (0,)), ((), ())), preferred_element_type=F32)


def _row_spec(cols):
    return pl.BlockSpec((TM, cols), lambda i: (i, 0))


def _full_spec(shape):
    return pl.BlockSpec(shape, lambda i: (0,) * len(shape))


def _ex_spec(tiles_per_ex):
    return pl.BlockSpec((1, 1, D_MODEL), lambda i: (i // tiles_per_ex, 0, 0))


def _ln_inproj(x, shift, scale, g, w_in_p, seq):
    t = x.shape[0]

    def body(x_ref, sh_ref, sc_ref, g_ref, w_ref, h_ref, za_ref, zg_ref, q0, q1, q2, q3, q4, q5, zf_ref):
        n, _, _ = _rms(x_ref[...], g_ref[...])
        h = (n * (1.0 + sc_ref[0]) + sh_ref[0]).astype(BF16)
        h_ref[...] = h
        za_ref[...] = _dot(h, w_ref[:, Z_A[0]:Z_A[1]])
        zg_ref[...] = _dot(h, w_ref[:, Z_G[0]:Z_G[1]])
        for ref, (c0, c1) in zip((q0, q1, q2, q3, q4, q5), Z_QKV):
            ref[...] = _dot(h, w_ref[:, c0:c1]).astype(BF16)
        zf_ref[...] = _dot(h, w_ref[:, Z_F[0]:Z_F[1]])

    tpe = seq // TM
    shapes = [jax.ShapeDtypeStruct((t, D_MODEL), BF16), jax.ShapeDtypeStruct((t, 768), F32),
              jax.ShapeDtypeStruct((t, D_CAT), F32)]
    shapes += [jax.ShapeDtypeStruct((t, 384), BF16)] * 6 + [jax.ShapeDtypeStruct((t, LANE), F32)]
    return pl.pallas_call(
        body, name="ln_inproj", grid=(t // TM,),
        in_specs=[_row_spec(D_MODEL), _ex_spec(tpe), _ex_spec(tpe), _full_spec((1, D_MODEL)),
                  _full_spec((D_MODEL, NP_IN))],
        out_specs=[_row_spec(D_MODEL), _row_spec(768), _row_spec(D_CAT)] + [_row_spec(384)] * 6 + [_row_spec(LANE)],
        out_shape=shapes, compiler_params=_params(("parallel",)),
    )(x, shift, scale, g, w_in_p)


def _a_up(za, gq, gkv, w_uq_p, w_ukv_p, cos, sina, sinb):
    t = za.shape[0]

    def body(za_ref, gq_ref, gkv_ref, wq_ref, wkv_ref, cos_ref, sa_ref, sb_ref, q_ref, k_ref, v_ref):
        cos_t, sa, sb = cos_ref[...], sa_ref[...], sb_ref[...]
        cqn, _, _ = _rms(za_ref[:, 0:384], gq_ref[...])
        q = _dot(cqn.astype(BF16), wq_ref[...])
        ckvn, _, _ = _rms(za_ref[:, 384:640], gkv_ref[...])
        kv = _dot(ckvn.astype(BF16), wkv_ref[...])
        kpe = za_ref[:, 640:768]
        kpe = _rope(kpe + pltpu.roll(kpe, 32, 1), cos_t, sa, sb).astype(BF16)
        for p in range(3):
            q_ref[:, 256 * p:256 * p + 128] = q[:, 256 * p:256 * p + 128].astype(BF16)
            q_ref[:, 256 * p + 128:256 * p + 256] = _rope(q[:, 256 * p + 128:256 * p + 256], cos_t, sa, sb).astype(BF16)
            k_ref[:, 256 * p:256 * p + 128] = kv[:, 128 * p:128 * p + 128].astype(BF16)
            k_ref[:, 256 * p + 128:256 * p + 256] = kpe
        v_ref[...] = kv[:, 384:768].astype(BF16)

    return pl.pallas_call(
        body, name="a_up", grid=(t // TM,),
        in_specs=[_row_spec(768), _full_spec((1, 384)), _full_spec((1, 256)), _full_spec((384, 768)),
                  _full_spec((256, 768)), _row_spec(LANE), _row_spec(LANE), _row_spec(LANE)],
        out_specs=[_row_spec(768), _row_spec(768), _row_spec(384)],
        out_shape=[jax.ShapeDtypeStruct((t, 768), BF16), jax.ShapeDtypeStruct((t, 768), BF16),
                   jax.ShapeDtypeStruct((t, 384), BF16)],
        compiler_params=_params(("parallel",)),
    )(za, gq, gkv, w_uq_p, w_ukv_p, cos, sina, sinb)


def _tri(n, upper):
    r = lax.broadcasted_iota(jnp.int32, (n, n), 0)
    c = lax.broadcasted_iota(jnp.int32, (n, n), 1)
    return jnp.where((c >= r) if upper else (c <= r), 1.0, 0.0).astype(BF16)


def _forget_fwd(zf, fb, nb, seq):
    blk = 256

    def body(zf_ref, fb_ref, f_ref):
        tri = _tri(blk, False)
        live = _lane_iota((blk, LANE)) < 5
        carry = jnp.zeros((1, LANE), F32)
        for i in range(seq // blk):
            u = zf_ref[i * blk:(i + 1) * blk, :] + fb_ref[...]
            lf = jnp.where(live, jnp.minimum(u, 0.0) - jnp.log(1.0 + jnp.exp(-jnp.abs(u))), 0.0)
            hi, mid, lo = _split3(lf)
            f_ref[i * blk:(i + 1) * blk, :] = (_dot(tri, hi) + _dot(tri, mid) + _dot(tri, lo) + carry) * LOG2E
            carry = carry + jnp.sum(lf, axis=0, keepdims=True)

    return pl.pallas_call(
        body, name="forget_fwd", grid=(nb,),
        in_specs=[pl.BlockSpec((seq, LANE), lambda b: (b, 0)), pl.BlockSpec((1, LANE), lambda b: (0, 0))],
        out_specs=pl.BlockSpec((seq, LANE), lambda b: (b, 0)),
        out_shape=jax.ShapeDtypeStruct((nb * seq, LANE), F32), compiler_params=_params(("parallel",)),
    )(zf, fb)


def _forget_bwd(dfq, dfk, zf, fb, nb, seq):
    blk = 256

    def body(dfq_ref, dfk_ref, zf_ref, fb_ref, dz_ref, gb_ref):
        @pl.when(pl.program_id(0) == 0)
        def _():
            gb_ref[...] = jnp.zeros_like(gb_ref)

        tri = _tri(blk, True)
        lane = _lane_iota((blk, LANE))
        wide = _lane_iota((blk, 384))
        live = lane < 5
        carry = jnp.zeros((1, LANE), F32)
        gsum = jnp.zeros((1, LANE), F32)
        for i in reversed(range(seq // blk)):
            d = dfk_ref[i * blk:(i + 1) * blk, :]
            dq = dfq_ref[i * blk:(i + 1) * blk, :]
            for hd in range(5):
                col = jnp.sum(jnp.where(wide == 64 * hd, dq, 0.0), axis=-1, keepdims=True)
                d = d + jnp.where(lane == hd, col, 0.0)
            hi, mid, lo = _split3(d)
            dlf = _dot(tri, hi) + _dot(tri, mid) + _dot(tri, lo) + carry
            carry = carry + jnp.sum(d, axis=0, keepdims=True)
            u = zf_ref[i * blk:(i + 1) * blk, :] + fb_ref[...]
            du = jnp.where(live, dlf * jax.nn.sigmoid(-u), 0.0)
            dz_ref[i * blk:(i + 1) * blk, :] = du.astype(BF16)
            gsum = gsum + jnp.sum(du, axis=0, keepdims=True)
        gb_ref[...] += jnp.broadcast_to(gsum, gb_ref.shape)

    return pl.pallas_call(
        body, name="forget_bwd", grid=(nb,),
        in_specs=[pl.BlockSpec((seq, 384), lambda b: (b, 0)), pl.BlockSpec((seq, LANE), lambda b: (b, 0)),
                  pl.BlockSpec((seq, LANE), lambda b: (b, 0)), pl.BlockSpec((1, LANE), lambda b: (0, 0))],
        out_specs=[pl.BlockSpec((seq, LANE), lambda b: (b, 0)), pl.BlockSpec((8, LANE), lambda b: (0, 0))],
        out_shape=[jax.ShapeDtypeStruct((nb * seq, LANE), BF16), jax.ShapeDtypeStruct((8, LANE), F32)],
        compiler_params=_params(("arbitrary",)),
    )(dfq, dfk, zf, fb)


def _gate_outproj(x, gate, oa, ob, oc, zg, w_out_p, seq):
    t = x.shape[0]

    def body(x_ref, gate_ref, oa_ref, ob_ref, oc_ref, zg_ref, w_ref, y_ref, xn_ref):
        y = jnp.zeros((TM, D_MODEL), F32)
        for i, o_ref in enumerate((oa_ref, ob_ref, oc_ref)):
            cat = (o_ref[...] * _silu(zg_ref[:, 384 * i:384 * (i + 1)])).astype(BF16)
            y = y + _dot(cat, w_ref[384 * i:384 * (i + 1), :])
        y_ref[...] = y
        xn_ref[...] = x_ref[...] + gate_ref[0] * y

    return pl.pallas_call(
        body, name="gate_outproj", grid=(t // TM,),
        in_specs=[_row_spec(D_MODEL), _ex_spec(seq // TM), _row_spec(384), _row_spec(384), _row_spec(384),
                  _row_spec(D_CAT), _full_spec((D_CAT, D_MODEL))],
        out_specs=[_row_spec(D_MODEL), _row_spec(D_MODEL)],
        out_shape=[jax.ShapeDtypeStruct((t, D_MODEL), F32)] * 2, compiler_params=_params(("parallel",)),
    )(x, gate, oa, ob, oc, zg, w_out_p)


def _final_loss(x, target, g):
    t = x.shape[0]

    def body(x_ref, t_ref, g_ref, dx_ref, loss_ref, gg_ref):
        @pl.when(pl.program_id(0) == 0)
        def _():
            loss_ref[...] = jnp.zeros_like(loss_ref)
            gg_ref[...] = jnp.zeros_like(gg_ref)

        gv = g_ref[...]
        out, xh, r = _rms(x_ref[...], gv)
        err = out - t_ref[...]
        loss_ref[...] += 0.5 * jnp.sum(jnp.mean(err * err, axis=-1, keepdims=True), axis=0, keepdims=True)
        dout = err / D_MODEL
        gg_ref[...] += jnp.broadcast_to(jnp.sum(dout * xh, axis=0, keepdims=True), gg_ref.shape)
        dx_ref[...] = _rms_bwd(dout, xh, r, gv)

    return pl.pallas_call(
        body, name="final_loss", grid=(t // TM,),
        in_specs=[_row_spec(D_MODEL), _row_spec(D_MODEL), _full_spec((1, D_MODEL))],
        out_specs=[_row_spec(D_MODEL), _full_spec((8, LANE)), _full_spec((8, D_MODEL))],
        out_shape=[jax.ShapeDtypeStruct((t, D_MODEL), F32), jax.ShapeDtypeStruct((8, LANE), F32),
                   jax.ShapeDtypeStruct((8, D_MODEL), F32)],
        compiler_params=_params(("arbitrary",)),
    )(x, target, g)


def _head_masks(kind, rows, dq, h):
    lq = _lane_iota((rows, dq))
    lv = _lane_iota((rows, LANE))
    mq = (lq >= 64 * h) & (lq < 64 * h + 64)
    if kind == "A":
        mq = mq | ((lq >= 128 + 32 * h) & (lq < 160 + 32 * h))
    return mq, (lv >= 64 * h) & (lv < 64 * h + 64)


def _tile_mask(kind, tile):
    row = lax.broadcasted_iota(jnp.int32, (tile, tile), 0)
    col = lax.broadcasted_iota(jnp.int32, (tile, tile), 1)
    return (col >> 6) <= (row >> 6) if kind == "A" else col <= row


def _attn_scale(kind):
    return 96.0 ** -0.5 if kind == "A" else 0.125


BAND_W = BAND_TILES * T_BAND


def _segments(kind, qi, tile):
    r0 = qi * tile
    if kind == "B":
        lo = max(qi - (BAND_TILES - 1), 0) * tile
        return [(lo, r0 + tile, False, lo - (qi - (BAND_TILES - 1)) * tile)]
    return ([(0, r0, False, 0)] if qi else []) + [(r0, r0 + tile, True, 0)]


def _scores(kind, qh, k_ref, aux_ref, h, seg, tile, scale):
    a, b, diag, c0 = seg
    s = _dot_nt(qh, k_ref[a:b, :]) * (scale * LOG2E)
    if kind == "B":
        return s + aux_ref[h, :, c0:BAND_W]
    if kind == "C":
        s = s - aux_ref[0, h, :, a:b]
    if diag:
        s = jnp.where(_tile_mask(kind, tile), s, NEG)
    return s


FWD_AHEAD = 2
BWD_AHEAD = 2


def _run_ahead(units, first, second, depth):
    queue = [first(*u) for u in units[:depth]]
    for i, (_, h) in enumerate(units):
        if i + depth < len(units):
            queue.append(first(*units[i + depth]))
        second(h, *queue.pop(0))


def _attn_fwd(kind, q, k, v, aux, nb, seq):
    dq = q.shape[1] // 3
    tile = T_BAND if kind == "B" else T_CAUSAL
    nq = seq // tile
    scale = _attn_scale(kind)

    def body(*refs):
        if kind == "A":
            q_ref, k_ref, v_ref, o_ref, lse_ref = refs
            aux_ref = None
        else:
            q_ref, k_ref, v_ref, aux_ref, o_ref, lse_ref = refs
        def logits(qi, h):
            rows = slice(qi * tile, (qi + 1) * tile)
            q2 = q_ref[rows, :]
            mq, _ = _head_masks(kind, tile, dq, h)
            qh = jnp.where(mq, q2, jnp.zeros_like(q2))
            segs = _segments(kind, qi, tile)
            return rows, segs, [_scores(kind, qh, k_ref, aux_ref, h, seg, tile, scale) for seg in segs]

        def finish(h, rows, segs, ss):
            _, mv = _head_masks(kind, tile, dq, h)
            mx = functools.reduce(jnp.maximum, [jnp.max(s, axis=-1, keepdims=True) for s in ss])
            ps = [jnp.exp2(s - mx) for s in ss]
            l = functools.reduce(jnp.add, [jnp.sum(p, axis=-1, keepdims=True) for p in ps])
            acc = functools.reduce(jnp.add, [_dot(p.astype(BF16), v_ref[seg[0]:seg[1], :]) for p, seg in zip(ps, segs)])
            o_h = jnp.where(mv, acc / l, 0.0)
            lse_h = jnp.where(mv, mx + jnp.log(l) * LOG2E, 0.0)
            if h == 0:
                o_ref[rows, :] = o_h
                lse_ref[rows, :] = lse_h
            else:
                o_ref[rows, :] += o_h
                lse_ref[rows, :] += lse_h

        _run_ahead([(qi, h) for qi in range(nq) for h in range(2)], logits, finish, FWD_AHEAD)

    def seq_spec(cols):
        return pl.BlockSpec((seq, cols), lambda b, p: (b, p))

    in_specs = [seq_spec(dq), seq_spec(dq), seq_spec(LANE)]
    args = [q, k, v]
    if kind == "B":
        in_specs.append(pl.BlockSpec((2, tile, BAND_W), lambda b, p: (p, 0, 0)))
        args.append(aux)
    if kind == "C":
        in_specs.append(pl.BlockSpec((1, 2, 1, seq), lambda b, p: (b, p, 0, 0)))
        args.append(aux)
    return pl.pallas_call(
        body, name="attn_fwd_" + kind, grid=(nb, 3), in_specs=in_specs, out_specs=[seq_spec(LANE), seq_spec(LANE)],
        out_shape=[jax.ShapeDtypeStruct((nb * seq, 384), F32)] * 2, compiler_params=_params(("parallel", "parallel")),
    )(*args)


def _attn_bwd(kind, q, k, v, o, do, lse, aux, nb, seq):
    dq = q.shape[1] // 3
    tile = T_BAND if kind == "B" else T_CAUSAL
    nq = seq // tile
    scale = _attn_scale(kind)
    dqk_dtype = F32 if kind == "A" else BF16

    def body(*refs):
        dfr_ref = dfq_ref = dbt_ref = aux_ref = None
        if kind == "A":
            q_ref, k_ref, v_ref, o_ref, do_ref, lse_ref, dq_ref, dk_ref, dv_ref, dkt_acc, dvt_acc = refs
        elif kind == "B":
            q_ref, k_ref, v_ref, o_ref, do_ref, lse_ref, aux_ref, dq_ref, dk_ref, dv_ref, dbt_ref, dkt_acc, dvt_acc = refs
        else:
            (q_ref, k_ref, v_ref, o_ref, do_ref, lse_ref, aux_ref, dq_ref, dk_ref, dv_ref, dfr_ref, dfq_ref,
             dkt_acc, dvt_acc) = refs
        dkt_acc[...] = jnp.zeros_like(dkt_acc)
        dvt_acc[...] = jnp.zeros_like(dvt_acc)
        if kind == "C":
            dfr_ref[...] = jnp.zeros_like(dfr_ref)
        if kind == "B":
            @pl.when(pl.program_id(1) == 0)
            def _():
                dbt_ref[...] = jnp.zeros_like(dbt_ref)

        def products(qi, h):
            rows = slice(qi * tile, (qi + 1) * tile)
            q2 = q_ref[rows, :]
            mq, mv = _head_masks(kind, tile, dq, h)
            qh = jnp.where(mq, q2, jnp.zeros_like(q2))
            doh = jnp.where(mv, do_ref[rows, :], 0.0)
            dob = doh.astype(BF16)
            segs = _segments(kind, qi, tile)
            ts = [_scores(kind, qh, k_ref, aux_ref, h, seg, tile, scale) for seg in segs]
            dps = [_dot_nt(dob, v_ref[seg[0]:seg[1], :]) for seg in segs]
            return rows, segs, qh, doh, ts, dps

        def finish(h, rows, segs, qh, doh, ts, dps):
            mq, mv = _head_masks(kind, tile, dq, h)
            qht = qh.astype(F32).T.astype(BF16)
            dobt = doh.T.astype(BF16)
            head_rows = [(64 * h, 64)] + ([(128 + 32 * h, 32)] if kind == "A" else [])
            delta = jnp.sum(doh * o_ref[rows, :], axis=-1, keepdims=True)
            lseh = jnp.max(jnp.where(mv, lse_ref[rows, :], NEG), axis=-1, keepdims=True)
            rs = jnp.zeros((tile, 1), F32)
            dq_h = jnp.zeros((tile, dq), F32)
            for (a, b, _, c0), t, dp in zip(segs, ts, dps):
                p = jnp.exp2(t - lseh)
                ds = p * (dp - delta)
                if kind == "B":
                    dbt_ref[h, :, c0:BAND_W] += ds
                if kind == "C":
                    dfr_ref[0, h, :, a:b] -= jnp.sum(ds, axis=0, keepdims=True)
                    rs = rs + jnp.sum(ds, axis=-1, keepdims=True)
                dss = (ds * scale).astype(BF16)
                dvt_acc[64 * h:64 * h + 64, a:b] += _dot(dobt[64 * h:64 * h + 64, :], p.astype(BF16))
                for r0, n in head_rows:
                    dkt_acc[r0:r0 + n, a:b] += _dot(qht[r0:r0 + n, :], dss)
                dq_h = dq_h + _dot(dss, k_ref[a:b, :])
            dq_h = jnp.where(mq, dq_h, 0.0).astype(dqk_dtype)
            if h == 0:
                dq_ref[rows, :] = dq_h
            else:
                dq_ref[rows, :] += dq_h
            if kind == "C":
                if h == 0:
                    dfq_ref[rows, :] = jnp.where(mv, rs, 0.0)
                else:
                    dfq_ref[rows, :] += jnp.where(mv, rs, 0.0)

        _run_ahead([(qi, h) for qi in range(nq) for h in range(2)], products, finish, BWD_AHEAD)
        for j in range(seq // 256):
            cols = slice(256 * j, 256 * (j + 1))
            dk_ref[cols, :] = dkt_acc[:, cols].T.astype(dqk_dtype)
            dv_ref[cols, :] = dvt_acc[:, cols].T.astype(BF16)

    def seq_spec(cols):
        return pl.BlockSpec((seq, cols), lambda p, b: (b, p))

    in_specs = [seq_spec(dq), seq_spec(dq), seq_spec(LANE), seq_spec(LANE), seq_spec(LANE), seq_spec(LANE)]
    args = [q, k, v, o, do, lse]
    out_specs = [seq_spec(dq), seq_spec(dq), seq_spec(LANE)]
    out_shape = [jax.ShapeDtypeStruct((nb * seq, 3 * dq), dqk_dtype)] * 2 + [jax.ShapeDtypeStruct((nb * seq, 384), BF16)]
    if kind == "B":
        spec = pl.BlockSpec((2, tile, BAND_W), lambda p, b: (p, 0, 0))
        in_specs.append(spec)
        args.append(aux)
        out_specs.append(spec)
        out_shape.append(jax.ShapeDtypeStruct((6, tile, BAND_W), F32))
    if kind == "C":
        spec = pl.BlockSpec((1, 2, 1, seq), lambda p, b: (b, p, 0, 0))
        in_specs.append(spec)
        args.append(aux)
        out_specs += [spec, seq_spec(LANE)]
        out_shape += [jax.ShapeDtypeStruct((nb, 6, 1, seq), F32), jax.ShapeDtypeStruct((nb * seq, 384), F32)]
    return pl.pallas_call(
        body, name="attn_bwd_" + kind, grid=(3, nb), in_specs=in_specs, out_specs=out_specs, out_shape=out_shape,
        scratch_shapes=[pltpu.VMEM((dq, seq), F32), pltpu.VMEM((LANE, seq), F32)],
        compiler_params=_params(("arbitrary", "arbitrary")),
    )(*args)


BIAS_G = 768
BIAS_EDGE = BIAS_G - N_REL


def _bias_line(rel_bias):
    g = jnp.concatenate([jnp.broadcast_to(rel_bias[:, N_REL - 1:], (rel_bias.shape[0], BIAS_EDGE)),
                         jnp.flip(rel_bias, axis=1)], axis=1)
    return jnp.pad(g, ((0, 8 - g.shape[0]), (0, 0)))


def _bias_unline(dg):
    return jnp.flip(dg[:, BIAS_EDGE:], axis=1)


def _bias_expand(g8):
    def body(g_ref, out_ref):
        line = jnp.broadcast_to(g_ref[0] * LOG2E, (T_BAND, BIAS_G))
        slab = pltpu.roll(line, 1, 1, stride=1, stride_axis=0)[:, LANE:BIAS_G]
        row = lax.broadcasted_iota(jnp.int32, (T_BAND, BAND_W), 0)
        col = lax.broadcasted_iota(jnp.int32, (T_BAND, BAND_W), 1)
        hidden = ((row >= 64) & (col < 64)) | ((row < 64) & (col >= BAND_W - 64))
        out_ref[0] = jnp.where(hidden, NEG, slab)

    return pl.pallas_call(
        body, name="bias_expand", grid=(8,), in_specs=[pl.BlockSpec((1, 1, BIAS_G), lambda h: (h, 0, 0))],
        out_specs=pl.BlockSpec((1, T_BAND, BAND_W), lambda h: (h, 0, 0)),
        out_shape=jax.ShapeDtypeStruct((8, T_BAND, BAND_W), F32), compiler_params=_params(("parallel",)),
    )(g8.reshape(8, 1, BIAS_G))


def _bias_reduce(d_slab):
    def body(d_ref, out_ref):
        r = lax.broadcasted_iota(jnp.int32, (T_BAND, T_BAND), 0)
        k = lax.broadcasted_iota(jnp.int32, (T_BAND, T_BAND), 1)
        flip = jnp.where(r + k == T_BAND - 1, 1.0, 0.0).astype(BF16)
        hi, mid, lo = _split3(d_ref[0])
        d_rev = _dot(flip, hi) + _dot(flip, mid) + _dot(flip, lo)
        wide = jnp.concatenate([jnp.zeros((T_BAND, LANE), F32), d_rev, jnp.zeros((T_BAND, 2 * LANE), F32)], axis=1)
        skew = pltpu.roll(wide, 0, 1, stride=1, stride_axis=0)
        dg = jnp.sum(skew, axis=0, keepdims=True)[:, LANE:LANE + BIAS_G]
        lane = _lane_iota((1, BIAS_G))
        clipped = jnp.sum(jnp.where(lane <= BIAS_EDGE, dg, 0.0), axis=1, keepdims=True)
        out_ref[0] = jnp.where(lane == BIAS_EDGE, clipped, dg)

    return pl.pallas_call(
        body, name="bias_reduce", grid=(8,), in_specs=[pl.BlockSpec((1, T_BAND, BAND_W), lambda h: (h, 0, 0))],
        out_specs=pl.BlockSpec((1, 1, BIAS_G), lambda h: (h, 0, 0)),
        out_shape=jax.ShapeDtypeStruct((8, 1, BIAS_G), F32), compiler_params=_params(("parallel",)),
    )(d_slab).reshape(8, BIAS_G)


def _outproj_bwd(dxn, y, gate, oa, ob, oc, zg, w_out_p, w_out_pt, nb, seq):
    t = dxn.shape[0]
    tpe = seq // TM

    def body(dxn_ref, y_ref, gate_ref, oa_ref, ob_ref, oc_ref, zg_ref, w_ref, wt_ref,
             doa_ref, dob_ref, doc_ref, dzg_ref, gw_ref, dgate_ref):
        i = pl.program_id(0)

        @pl.when(i == 0)
        def _():
            gw_ref[...] = jnp.zeros_like(gw_ref)

        @pl.when(i % tpe == 0)
        def _():
            dgate_ref[...] = jnp.zeros_like(dgate_ref)

        dxn_t = dxn_ref[...]
        dgate_ref[0] += jnp.sum(dxn_t * y_ref[...], axis=0, keepdims=True)
        dy = (dxn_t * gate_ref[0]).astype(BF16)
        for gi, (o_ref, do_ref) in enumerate(((oa_ref, doa_ref), (ob_ref, dob_ref), (oc_ref, doc_ref))):
            cols = slice(384 * gi, 384 * (gi + 1))
            u = zg_ref[:, cols]
            o_t = o_ref[...]
            su = _silu(u)
            dcat = _dot(dy, wt_ref[:, cols])
            do_ref[...] = dcat * su
            dzg_ref[:, cols] = (dcat * o_t * _dsilu(u)).astype(BF16)
            gw_ref[cols, :] += _dot_tn((o_t * su).astype(BF16), dy)

    return pl.pallas_call(
        body, name="outproj_bwd", grid=(t // TM,),
        in_specs=[_row_spec(D_MODEL), _row_spec(D_MODEL), _ex_spec(tpe), _row_spec(384), _row_spec(384), _row_spec(384),
                  _row_spec(D_CAT), _full_spec((D_CAT, D_MODEL)), _full_spec((D_MODEL, D_CAT))],
        out_specs=[_row_spec(384), _row_spec(384), _row_spec(384), _row_spec(D_CAT), _full_spec((D_CAT, D_MODEL)),
                   _ex_spec(tpe)],
        out_shape=[jax.ShapeDtypeStruct((t, 384), F32)] * 3 + [jax.ShapeDtypeStruct((t, D_CAT), BF16),
                                                                jax.ShapeDtypeStruct((D_CAT, D_MODEL), F32),
                                                                jax.ShapeDtypeStruct((nb, 1, D_MODEL), F32)],
        compiler_params=_params(("arbitrary",)),
    )(dxn, y, gate, oa, ob, oc, zg, w_out_p, w_out_pt)


def _a_up_bwd(dqa, dka, dva, za, gq, gkv, w_uq_pt, w_ukv_pt, cos, sina, sinb):
    t = za.shape[0]

    def body(dq_ref, dk_ref, dv_ref, za_ref, gq_ref, gkv_ref, wqt_ref, wkvt_ref, cos_ref, sa_ref, sb_ref,
             dza_ref, gwq_ref, gwkv_ref, ggq_ref, ggkv_ref, dqb, dkvb):
        @pl.when(pl.program_id(0) == 0)
        def _():
            gwq_ref[...] = jnp.zeros_like(gwq_ref)
            gwkv_ref[...] = jnp.zeros_like(gwkv_ref)
            ggq_ref[...] = jnp.zeros_like(ggq_ref)
            ggkv_ref[...] = jnp.zeros_like(ggkv_ref)

        cos_t, sa, sb = cos_ref[...], sa_ref[...], sb_ref[...]
        dkpe = jnp.zeros((TM, LANE), F32)
        for p in range(3):
            dqb[:, 256 * p:256 * p + 128] = dq_ref[:, 256 * p:256 * p + 128].astype(BF16)
            dqb[:, 256 * p + 128:256 * p + 256] = _rope_t(dq_ref[:, 256 * p + 128:256 * p + 256], cos_t, sa, sb).astype(BF16)
            dkvb[:, 128 * p:128 * p + 128] = dk_ref[:, 256 * p:256 * p + 128].astype(BF16)
            dkpe = dkpe + dk_ref[:, 256 * p + 128:256 * p + 256]
        dkvb[:, 384:768] = dv_ref[...]
        dkpe = _rope_t(dkpe, cos_t, sa, sb)
        dkpe = jnp.where(_lane_iota((TM, LANE)) < A_ROPE, dkpe + pltpu.roll(dkpe, LANE - 32, 1), 0.0)

        gqv = gq_ref[...]
        cqn, cqh, rq = _rms(za_ref[:, 0:384], gqv)
        dq_t = dqb[...]
        gwq_ref[...] += _dot_tn(cqn.astype(BF16), dq_t)
        dcqn = _dot(dq_t, wqt_ref[...])
        ggq_ref[...] += jnp.broadcast_to(jnp.sum(dcqn * cqh, axis=0, keepdims=True), ggq_ref.shape)
        dza_ref[:, 0:384] = _rms_bwd(dcqn, cqh, rq, gqv).astype(BF16)

        gkvv = gkv_ref[...]
        ckvn, ckvh, rkv = _rms(za_ref[:, 384:640], gkvv)
        dkv_t = dkvb[...]
        gwkv_ref[...] += _dot_tn(ckvn.astype(BF16), dkv_t)
        dckvn = _dot(dkv_t, wkvt_ref[...])
        ggkv_ref[...] += jnp.broadcast_to(jnp.sum(dckvn * ckvh, axis=0, keepdims=True), ggkv_ref.shape)
        dza_ref[:, 384:640] = _rms_bwd(dckvn, ckvh, rkv, gkvv).astype(BF16)
        dza_ref[:, 640:768] = dkpe.astype(BF16)

    return pl.pallas_call(
        body, name="a_up_bwd", grid=(t // TM,),
        in_specs=[_row_spec(768), _row_spec(768), _row_spec(384), _row_spec(768), _full_spec((1, 384)),
                  _full_spec((1, 256)), _full_spec((768, 384)), _full_spec((768, 256)), _row_spec(LANE), _row_spec(LANE),
                  _row_spec(LANE)],
        out_specs=[_row_spec(768), _full_spec((384, 768)), _full_spec((256, 768)), _full_spec((8, 384)),
                   _full_spec((8, 256))],
        out_shape=[jax.ShapeDtypeStruct((t, 768), BF16), jax.ShapeDtypeStruct((384, 768), F32),
                   jax.ShapeDtypeStruct((256, 768), F32), jax.ShapeDtypeStruct((8, 384), F32),
                   jax.ShapeDtypeStruct((8, 256), F32)],
        scratch_shapes=[pltpu.VMEM((TM, 768), BF16), pltpu.VMEM((TM, 768), BF16)],
        compiler_params=_params(("arbitrary",)),
    )(dqa, dka, dva, za, gq, gkv, w_uq_pt, w_ukv_pt, cos, sina, sinb)


def _dz_cols():
    return (Z_A, Z_G) + Z_QKV + (Z_F,)


def _inproj_bwd_dx(dz, dxn, x, shift, scale, g, w_in_pt, nb, seq):
    t = x.shape[0]
    tpe = seq // TM
    cols = _dz_cols()

    def body(*refs):
        dz_refs = refs[:len(cols)]
        dxn_ref, x_ref, sh_ref, sc_ref, g_ref, wt_ref, dx_ref, dsh_ref, dsc_ref, dg_ref = refs[len(cols):]
        i = pl.program_id(0)

        @pl.when(i == 0)
        def _():
            dg_ref[...] = jnp.zeros_like(dg_ref)

        @pl.when(i % tpe == 0)
        def _():
            dsh_ref[...] = jnp.zeros_like(dsh_ref)
            dsc_ref[...] = jnp.zeros_like(dsc_ref)

        dh = jnp.zeros((TM, D_MODEL), F32)
        for ref, (c0, c1) in zip(dz_refs, cols):
            dh = dh + _dot_nt(ref[...], wt_ref[:, c0:c1])
        gv = g_ref[...]
        n, xh, r = _rms(x_ref[...], gv)
        dsh_ref[0] += jnp.sum(dh, axis=0, keepdims=True)
        dsc_ref[0] += jnp.sum(dh * n, axis=0, keepdims=True)
        dn = dh * (1.0 + sc_ref[0])
        dg_ref[...] += jnp.broadcast_to(jnp.sum(dn * xh, axis=0, keepdims=True), dg_ref.shape)
        dx_ref[...] = dxn_ref[...] + _rms_bwd(dn, xh, r, gv)

    in_specs = [_row_spec(c1 - c0) for c0, c1 in cols]
    in_specs += [_row_spec(D_MODEL), _row_spec(D_MODEL), _ex_spec(tpe), _ex_spec(tpe), _full_spec((1, D_MODEL)),
                 _full_spec((D_MODEL, NP_IN))]
    return pl.pallas_call(
        body, name="inproj_bwd_dx", grid=(t // TM,), in_specs=in_specs,
        out_specs=[_row_spec(D_MODEL), _ex_spec(tpe), _ex_spec(tpe), _full_spec((8, D_MODEL))],
        out_shape=[jax.ShapeDtypeStruct((t, D_MODEL), F32), jax.ShapeDtypeStruct((nb, 1, D_MODEL), F32),
                   jax.ShapeDtypeStruct((nb, 1, D_MODEL), F32), jax.ShapeDtypeStruct((8, D_MODEL), F32)],
        compiler_params=_params(("arbitrary",)),
    )(*dz, dxn, x, shift, scale, g, w_in_pt)


def _inproj_bwd_dw(h, dz, name, layer, both=None):
    t = h.shape[0]
    widths = [d.shape[1] for d in dz]
    total = sum(widths)

    def body(*refs):
        h_ref = refs[0]
        dz_refs = refs[1:1 + len(dz)]
        gw_ref = refs[-1]

        @pl.when(pl.program_id(0) == 0)
        def _():
            gw_ref[...] = jnp.zeros_like(gw_ref)

        h_t = h_ref[...]
        c0 = 0
        for ref, w in zip(dz_refs, widths):
            gw_ref[0, :, c0:c0 + w] += _dot_tn(h_t, ref[...])
            c0 += w

    in_specs = [_row_spec(D_MODEL)] + [_row_spec(w) for w in widths]
    args = [h, *dz]
    aliases = {}
    if both is not None:
        in_specs.append(pl.BlockSpec(memory_space=pl.ANY))
        aliases = {len(args): 0}
        args.append(both)
    return pl.pallas_call(
        body, name=name, grid=(t // TM,), in_specs=in_specs,
        out_specs=pl.BlockSpec((1, D_MODEL, total), lambda i: (layer, 0, 0)),
        out_shape=jax.ShapeDtypeStruct((DEPTH, D_MODEL, total), F32), input_output_aliases=aliases,
        compiler_params=_params(("arbitrary",)),
    )(*args)


def _ada_fwd(c_all, w_ada, b_cols):
    n = c_all.shape[0]
    cols = w_ada.shape[2]

    def body(c_ref, w_ref, b_ref, out_ref):
        act = _silu(c_ref[...]).astype(BF16)
        out_ref[0] = _dot(act, w_ref[0].astype(BF16)) + b_ref[0]

    return pl.pallas_call(
        body, name="ada_fwd", grid=(DEPTH,),
        in_specs=[pl.BlockSpec((n, D_MODEL), lambda l: (0, 0)), pl.BlockSpec((1, D_MODEL, cols), lambda l: (l, 0, 0)),
                  pl.BlockSpec((1, 1, cols), lambda l: (l, 0, 0))],
        out_specs=pl.BlockSpec((1, n, cols), lambda l: (l, 0, 0)),
        out_shape=jax.ShapeDtypeStruct((DEPTH, n, cols), F32), compiler_params=_params(("parallel",)),
    )(c_all, w_ada, b_cols)


def _ada_bwd(c_all, dmod_cols, dmod_all):
    n = c_all.shape[0]
    cols = dmod_cols.shape[2]

    def body(c_ref, dc_ref, da_ref, gw_ref, gb_ref):
        act = _silu(c_ref[...]).astype(BF16)
        gw_ref[0] = _dot_tn(act, dc_ref[0].astype(BF16))
        gb_ref[0] = jnp.sum(da_ref[0], axis=0, keepdims=True)

    return pl.pallas_call(
        body, name="ada_bwd", grid=(DEPTH,),
        in_specs=[pl.BlockSpec((n, D_MODEL), lambda l: (0, 0)), pl.BlockSpec((1, n, cols), lambda l: (l, 0, 0)),
                  pl.BlockSpec((1, n, 3 * D_MODEL), lambda l: (l, 0, 0))],
        out_specs=[pl.BlockSpec((1, D_MODEL, cols), lambda l: (l, 0, 0)),
                   pl.BlockSpec((1, 1, 3 * D_MODEL), lambda l: (l, 0, 0))],
        out_shape=[jax.ShapeDtypeStruct((DEPTH, D_MODEL, cols), F32), jax.ShapeDtypeStruct((DEPTH, 1, 3 * D_MODEL), F32)],
        compiler_params=_params(("parallel",)),
    )(c_all, dmod_cols, dmod_all)


def _sum_blocks(parts, name):
    n, rows, cols = parts.shape
    tr = rows if rows <= 256 else 8 * next(d for d in range(32, 0, -1) if (rows // 8) % d == 0)

    def body(p_ref, out_ref):
        acc = p_ref[0].astype(F32)
        for k in range(1, n):
            acc = acc + p_ref[k].astype(F32)
        out_ref[...] = acc

    return pl.pallas_call(
        body, name=name, grid=(rows // tr,), in_specs=[pl.BlockSpec((n, tr, cols), lambda i: (0, i, 0))],
        out_specs=pl.BlockSpec((tr, cols), lambda i: (i, 0)), out_shape=jax.ShapeDtypeStruct((rows, cols), F32),
        compiler_params=_params(("parallel",)),
    )(parts)


def _adamw(w, g, m, v, name):
    nl, rows, cols = w.shape
    if rows * cols <= 64 * 1024:
        tr = rows
        tl = next(t for t in range(nl, 0, -1) if nl % t == 0 and t * max(rows, 8) * cols <= 512 * 1024)
    else:
        tl = 1
        tr = next(t for t in (rows, 256, 128, 64, 32, 16, 8) if rows % t == 0 and t * cols <= 256 * 1024)

    def body(w_ref, g_ref, m_ref, v_ref, d_ref, mo_ref, vo_ref):
        gv = g_ref[...]
        mn = ADAM_B1 * m_ref[...] + (1.0 - ADAM_B1) * gv
        vn = ADAM_B2 * v_ref[...] + (1.0 - ADAM_B2) * jnp.square(gv)
        m_hat = mn / (1.0 - ADAM_B1 ** ADAM_STEP)
        v_hat = vn / (1.0 - ADAM_B2 ** ADAM_STEP)
        d_ref[...] = -ADAM_LR * (m_hat / (jnp.sqrt(v_hat) + ADAM_EPS) + ADAM_WD * w_ref[...])
        mo_ref[...] = mn
        vo_ref[...] = vn

    spec = pl.BlockSpec((tl, tr, cols), lambda l, i: (l, i, 0))
    return pl.pallas_call(
        body, name=name, grid=(nl // tl, rows // tr), in_specs=[spec] * 4, out_specs=[spec] * 3,
        out_shape=[jax.ShapeDtypeStruct((nl, rows, cols), F32)] * 3, compiler_params=_params(("parallel", "parallel")),
    )(w, g, m, v)


ALL_FLIPS = tuple(range(1, N_DEV))


def _gather_small(src, name):
    rows, cols = src.shape

    def body(src_ref, dst_ref, send_sems, recv_sems, local_sem):
        x, y, c = lax.axis_index("x"), lax.axis_index("y"), lax.axis_index("c")
        me = 4 * x + 2 * y + c

        def copy(i, f, slot):
            to = (1 - x if f & 4 else x, 1 - y if f & 2 else y, 1 - c if f & 1 else c)
            return pltpu.make_async_remote_copy(
                src_ref=src_ref, dst_ref=dst_ref.at[slot], send_sem=send_sems.at[i], recv_sem=recv_sems.at[i],
                device_id=to, device_id_type=pl.DeviceIdType.MESH)

        own = pltpu.make_async_copy(src_ref, dst_ref.at[me], local_sem)
        own.start()
        sends = [copy(i, f, me) for i, f in enumerate(ALL_FLIPS)]
        for cp in sends:
            cp.start()
        for i, f in enumerate(ALL_FLIPS):
            copy(i, f, me ^ f).wait_recv()
        for cp in sends:
            cp.wait_send()
        own.wait()

    nf = len(ALL_FLIPS)
    return pl.pallas_call(
        body, name=name, out_shape=jax.ShapeDtypeStruct((N_DEV, rows, cols), src.dtype),
        in_specs=[pl.BlockSpec(memory_space=pl.ANY)], out_specs=pl.BlockSpec(memory_space=pl.ANY),
        scratch_shapes=[pltpu.SemaphoreType.DMA((nf,)), pltpu.SemaphoreType.DMA((nf,)), pltpu.SemaphoreType.DMA],
    )(src)


def _transfer(name, srcs, dst_shapes, plan, in_place=False):
    n_arr = len(srcs)
    probe = plan(0, 0, 0)
    n_steps = len(probe)

    def body(*refs):
        src_refs, dst_refs = refs[:n_arr], refs[n_arr:2 * n_arr]
        send_sems, recv_sems, local_sems = refs[2 * n_arr:]
        x, y, c = lax.axis_index("x"), lax.axis_index("y"), lax.axis_index("c")
        steps = plan(x, y, c)

        def rows(ref, r0, n):
            return ref.at[:, pl.ds(r0, n), :]

        def arrival(t):
            a, _, _, n, _, f, _ = steps[t]
            return pltpu.make_async_remote_copy(
                src_ref=rows(dst_refs[a], 0, n), dst_ref=rows(dst_refs[a], 0, n), send_sem=send_sems.at[t],
                recv_sem=recv_sems.at[t], device_id=(x, y, c), device_id_type=pl.DeviceIdType.MESH)

        arrived, started = set(), []
        for t, (a, from_dst, sr, n, dr, f, after) in enumerate(steps):
            for u in after:
                if u not in arrived:
                    arrival(u).wait_recv()
                    arrived.add(u)
            src = rows(dst_refs[a] if from_dst else src_refs[a], sr, n)
            dst = rows(dst_refs[a], dr, n)
            if f == 0:
                cp = pltpu.make_async_copy(src, dst, local_sems.at[t])
            else:
                to = (1 - x if f & 4 else x, 1 - y if f & 2 else y, 1 - c if f & 1 else c)
                cp = pltpu.make_async_remote_copy(src_ref=src, dst_ref=dst, send_sem=send_sems.at[t],
                                                  recv_sem=recv_sems.at[t], device_id=to,
                                                  device_id_type=pl.DeviceIdType.MESH)
            cp.start()
            started.append(cp)
        for t, step in enumerate(steps):
            if step[5] != 0 and t not in arrived:
                arrival(t).wait_recv()
        for cp, step in zip(started, steps):
            if step[5] == 0:
                cp.wait()
            else:
                cp.wait_send()

    any_spec = pl.BlockSpec(memory_space=pl.ANY)
    return pl.pallas_call(
        body, name=name, out_shape=[jax.ShapeDtypeStruct(s, d) for s, d in dst_shapes],
        in_specs=[any_spec] * n_arr, out_specs=[any_spec] * n_arr,
        input_output_aliases={a: a for a in range(n_arr)} if in_place else {},
        scratch_shapes=[pltpu.SemaphoreType.DMA((n_steps,)), pltpu.SemaphoreType.DMA((n_steps,)),
                        pltpu.SemaphoreType.DMA((n_steps,))],
    )(*srcs)


CHIP_FLIPS = (2, 4, 6)


def _gather_plan(chip_rows):
    def plan(x, y, c):
        steps = []
        chip = 2 * x + y
        for a, rc in enumerate(chip_rows):
            h = rc // 2
            first = (h // 32) * 16
            mine = rc * chip + h * c
            from_x, from_y, diag = rc * (chip ^ 2) + h * c, rc * (chip ^ 1) + h * c, rc * (chip ^ 3) + h * c
            steps.append((a, False, h * c, h, mine, 0, ()))
            to_x = len(steps)
            steps.append((a, False, h * c, h, mine, 4, ()))
            to_y = len(steps)
            steps.append((a, False, h * c, h, mine, 2, ()))
            fwd_y = len(steps)
            steps.append((a, True, from_x, first, from_x, 2, (to_x,)))
            fwd_x = len(steps)
            steps.append((a, True, from_y + first, h - first, from_y + first, 4, (to_y,)))
            steps.append((a, False, h * c, h, mine, 1, ()))
            steps.append((a, True, from_x, h, from_x, 1, (to_x,)))
            steps.append((a, True, from_y, h, from_y, 1, (to_y,)))
            steps.append((a, True, diag, first, diag, 1, (fwd_y,)))
            steps.append((a, True, diag + first, h - first, diag + first, 1, (fwd_x,)))
        return steps
    return plan


def _pair_reduce_plan(chip_rows):
    def plan(x, y, c):
        steps = []
        for a, rc in enumerate(chip_rows):
            h = rc // 2
            for j in range(4):
                steps.append((a, False, rc * j + h * (1 - c), h, h * j, 1, ()))
        return steps
    return plan


def _chip_scatter_plan(chip_rows):
    def plan(x, y, c):
        steps = []
        for a, rc in enumerate(chip_rows):
            h = rc // 2
            for k, f in enumerate(CHIP_FLIPS):
                steps.append((a, False, h * ((2 * x + y) ^ (f >> 1)), h, h * k, f, ()))
        return steps
    return plan


def _pair_share_plan(chip_rows):
    def plan(x, y, c):
        return [(a, True, (rc // 2) * c, rc // 2, (rc // 2) * c, 1, ()) for a, rc in enumerate(chip_rows)]
    return plan


def _tile_rows(h):
    return h


def _sum_pair(partial, recv, core, rc, name):
    nl, _, cols = partial.shape
    h = rc // 2
    tr = _tile_rows(h)

    def body(c_ref, p_ref, r_ref, out_ref):
        out_ref[...] = (p_ref[...] + r_ref[...]).astype(BF16)

    spec = pl.BlockSpec((1, tr, cols), lambda l, j, i, c_ref: (l, (h // tr) * j + i, 0))
    return pl.pallas_call(
        body, name=name, out_shape=jax.ShapeDtypeStruct((nl, 4 * h, cols), BF16),
        grid_spec=pltpu.PrefetchScalarGridSpec(
            num_scalar_prefetch=1, grid=(nl, 4, h // tr),
            in_specs=[pl.BlockSpec((1, tr, cols), lambda l, j, i, c_ref: (l, (rc // tr) * j + (h // tr) * c_ref[0] + i, 0)),
                      spec],
            out_specs=spec),
        compiler_params=_params(("parallel", "parallel", "parallel")),
    )(core, partial, recv)


def _sum_chips(chip_sum, recv, place, rc, name):
    nl, _, cols = chip_sum.shape
    h = rc // 2
    tr = _tile_rows(h)

    def body(s_ref, own_ref, r_ref, out_ref):
        acc = own_ref[0].astype(F32)
        for k in range(3):
            acc = acc + r_ref[0, k].astype(F32)
        out_ref[0] = acc

    return pl.pallas_call(
        body, name=name, out_shape=jax.ShapeDtypeStruct((nl, rc, cols), F32),
        grid_spec=pltpu.PrefetchScalarGridSpec(
            num_scalar_prefetch=1, grid=(nl, h // tr),
            in_specs=[pl.BlockSpec((1, tr, cols), lambda l, i, s_ref: (l, (h // tr) * s_ref[0] + i, 0)),
                      pl.BlockSpec((1, 3, tr, cols), lambda l, i, s_ref: (l, 0, i, 0))],
            out_specs=pl.BlockSpec((1, tr, cols), lambda l, i, s_ref: (l, (h // tr) * s_ref[1] + i, 0))),
        compiler_params=_params(("parallel", "parallel")),
    )(place, chip_sum, recv.reshape(nl, 3, h, cols))


def _pad_cols(a, n):
    return a if n == 0 else jnp.pad(a, ((0, 0), (0, n)))


def _in_to_padded(w):
    return jnp.concatenate([_pad_cols(w[:, a:b], z) for a, b, z in IN_PIECES], axis=1)


def _in_cols_first(ga, gb):
    split = Z_G[1]
    pos, out = 0, {}
    for a, b, z in IN_PIECES:
        src, off = (ga, pos) if pos < split else (gb, pos - split)
        out[a] = jnp.transpose(src[:, :, off:off + (b - a)], (2, 0, 1))
        pos += (b - a) + z
    return jnp.concatenate([out[a] for a in sorted(out)], axis=0)


def _out_to_padded(w):
    z = jnp.zeros((64, w.shape[1]), w.dtype)
    return jnp.concatenate([w[0:384], w[384:704], z, w[704:1024], z], axis=0)


def _out_from_padded(gp):
    return jnp.concatenate([gp[0:384], gp[384:704], gp[768:1088]], axis=0)


def _uq_to_padded(w):
    parts = []
    for p in range(3):
        h0, h1 = 2 * p, 2 * p + 1
        parts += [w[:, 96 * h0:96 * h0 + 64], w[:, 96 * h1:96 * h1 + 64], w[:, 96 * h0 + 64:96 * h0 + 96],
                  w[:, 96 * h1 + 64:96 * h1 + 96], jnp.zeros((w.shape[0], 64), w.dtype)]
    return jnp.concatenate(parts, axis=1)


def _uq_from_padded(gp):
    parts = []
    for h in range(6):
        p, s = h // 2, h % 2
        parts += [gp[:, 256 * p + 64 * s:256 * p + 64 * s + 64], gp[:, 256 * p + 128 + 32 * s:256 * p + 160 + 32 * s]]
    return jnp.concatenate(parts, axis=1)


def _ukv_to_padded(w):
    return jnp.concatenate([w[:, 128 * h:128 * h + 64] for h in range(6)]
                           + [w[:, 128 * h + 64:128 * h + 128] for h in range(6)], axis=1)


def _ukv_from_padded(gp):
    parts = []
    for h in range(6):
        parts += [gp[:, 64 * h:64 * h + 64], gp[:, 384 + 64 * h:384 + 64 * h + 64]]
    return jnp.concatenate(parts, axis=1)


LR_ROWS = 224
SHARD_ROWS = (256, 256, 256, LR_ROWS)


def _pack_lowrank(w_uq, w_ukv):
    flat = jnp.concatenate([w_uq.reshape(-1), w_ukv.reshape(-1)])
    return jnp.pad(flat, (0, LR_ROWS * PACK_COLS - flat.shape[0])).reshape(1, LR_ROWS, PACK_COLS)


def _unpack_lowrank(packed):
    flat = packed.reshape(-1)
    n_uq = DEPTH * A_Q_RANK * 144
    n_ukv = DEPTH * A_KV_RANK * 192
    return flat[0:n_uq].reshape(DEPTH, A_Q_RANK, 144), flat[n_uq:n_uq + n_ukv].reshape(DEPTH, A_KV_RANK, 192)


def _rope_tables(positions):
    t = positions.size
    inv = ROPE_THETA ** (-jnp.arange(0, A_ROPE, 2, dtype=F32) / A_ROPE)
    inv_row = jnp.pad(jnp.tile(inv, 4), (0, 64)).reshape(1, LANE)

    def body(p_ref, i_ref, c_ref, sa_ref, sb_ref):
        ang = p_ref[...].astype(F32) * i_ref[...]
        lane = _lane_iota((TM, LANE))
        live = lane < 64
        second = (lane & 31) >= 16
        s = jnp.sin(ang)
        c_ref[...] = jnp.where(live, jnp.cos(ang), 0.0)
        sa_ref[...] = jnp.where(live & jnp.logical_not(second), -s, 0.0)
        sb_ref[...] = jnp.where(live & second, s, 0.0)

    return pl.pallas_call(
        body, name="rope_tables", grid=(t // TM,), in_specs=[_row_spec(1), _full_spec((1, LANE))],
        out_specs=[_row_spec(LANE)] * 3, out_shape=[jax.ShapeDtypeStruct((t, LANE), F32)] * 3,
        compiler_params=_params(("parallel",)),
    )(positions.reshape(t, 1), inv_row)


def _rows(a, n):
    flat = a.reshape(-1)
    return jnp.pad(flat, (0, n * LANE - flat.shape[0])).reshape(n, LANE)


def _forward_backward(x, mod, tables, target, weights, small, nb, seq):
    cos, sina, sinb = tables
    saved = []
    for l in range(DEPTH):
        w, s = weights[l], small[l]
        shift = mod[l][:, None, 0:D_MODEL]
        scale = mod[l][:, None, D_MODEL:2 * D_MODEL]
        gate = mod[l][:, None, 2 * D_MODEL:]
        h, za, zg, qb, kb, vb, qc, kc, vc, zf = _ln_inproj(x, shift, scale, s["norm_g"], w["in"], seq)
        qa, ka, va = _a_up(za, s["gq"], s["gkv"], w["uq"], w["ukv"], cos, sina, sinb)
        bias = _bias_expand(s["g8"])[0:6]
        f = _forget_fwd(zf, s["fb"], nb, seq)
        frow = jnp.pad(f[:, 0:5].reshape(nb, seq, 5).transpose(0, 2, 1), ((0, 0), (0, 1), (0, 0)))
        frow = frow.reshape(nb, 6, 1, seq)
        oa, lse_a = _attn_fwd("A", qa, ka, va, None, nb, seq)
        ob, lse_b = _attn_fwd("B", qb, kb, vb, bias, nb, seq)
        oc, lse_c = _attn_fwd("C", qc, kc, vc, frow, nb, seq)
        y, xn = _gate_outproj(x, gate, oa, ob, oc, zg, w["out"], seq)
        saved.append(dict(x=x, h=h, za=za, zg=zg, zf=zf, y=y, shift=shift, scale=scale, gate=gate, bias=bias, frow=frow,
                          a=(qa, ka, va, oa, lse_a), b=(qb, kb, vb, ob, lse_b), c=(qc, kc, vc, oc, lse_c)))
        x = xn
    dx, loss8, gfinal8 = _final_loss(x, target, small[0]["final_g"])
    grads = []
    gw_in = (None, None)
    for l in reversed(range(DEPTH)):
        w, s, sv = weights[l], small[l], saved[l]
        qa, ka, va, oa, lse_a = sv["a"]
        qb, kb, vb, ob, lse_b = sv["b"]
        qc, kc, vc, oc, lse_c = sv["c"]
        doa, dob, doc, dzg, gw_out, dgate = _outproj_bwd(dx, sv["y"], sv["gate"], oa, ob, oc, sv["zg"], w["out"],
                                                          w["out_t"], nb, seq)
        dqa, dka, dva = _attn_bwd("A", qa, ka, va, oa, doa, lse_a, None, nb, seq)
        dqb, dkb, dvb, dbt = _attn_bwd("B", qb, kb, vb, ob, dob, lse_b, sv["bias"], nb, seq)
        dqc, dkc, dvc, dfr, dfq = _attn_bwd("C", qc, kc, vc, oc, doc, lse_c, sv["frow"], nb, seq)
        dg = _bias_reduce(jnp.pad(dbt, ((0, 2), (0, 0), (0, 0))))
        grb = jnp.pad(_bias_unline(dg), ((0, 0), (0, 384 - N_REL)))
        dfk = dfr.reshape(nb, 6, seq).transpose(0, 2, 1).reshape(nb * seq, 6)
        dzf, gfb = _forget_bwd(dfq, jnp.pad(dfk, ((0, 0), (0, LANE - 6))), sv["zf"], s["fb"], nb, seq)
        dza, gw_uq, gw_ukv, ggq, ggkv = _a_up_bwd(dqa, dka, dva, sv["za"], s["gq"], s["gkv"], w["uq_t"], w["ukv_t"],
                                                  cos, sina, sinb)
        dz = (dza, dzg, dqb, dkb, dvb, dqc, dkc, dvc, dzf)
        dx, dshift, dscale, gnorm = _inproj_bwd_dx(dz, dx, sv["x"], sv["shift"], sv["scale"], s["norm_g"], w["in"],
                                                   nb, seq)
        gw_in = (_inproj_bwd_dw(sv["h"], dz[0:2], "inproj_bwd_dw0", l, gw_in[0]),
                 _inproj_bwd_dw(sv["h"], dz[2:], "inproj_bwd_dw1", l, gw_in[1]))
        dmod = jnp.concatenate([dshift[:, 0], dscale[:, 0], dgate[:, 0]], axis=1)
        grads.append(dict(w_out=gw_out, w_uq=gw_uq, w_ukv=gw_ukv, dmod=dmod, norm_g=gnorm[0], gq=ggq[0],
                          gkv=ggkv[0], rb8=grb, fb=gfb[0]))
    grads.reverse()
    return loss8[0, 0], dx, grads, gfinal8[0], gw_in


def _layer_weights(w_in, w_out, w_uq, w_ukv):
    wi, wo, wq, wkv = _in_to_padded(w_in), _out_to_padded(w_out), _uq_to_padded(w_uq), _ukv_to_padded(w_ukv)
    return {"in": wi, "out": wo, "out_t": wo.T, "uq": wq, "uq_t": wq.T, "ukv": wkv, "ukv_t": wkv.T}


def _layer_small(norm_g, gq, gkv, rel_bias, forget_b, final_g):
    fb = jnp.pad(forget_b, (0, LANE - 5)).reshape(1, LANE)
    return dict(norm_g=norm_g.reshape(1, -1), gq=gq.reshape(1, -1), gkv=gkv.reshape(1, -1), g8=_bias_line(rel_bias), fb=fb,
                final_g=final_g.reshape(1, -1))


def _small_payload(per_layer, final_g, loss):
    def stack(key):
        return jnp.stack([p[key] for p in per_layer])

    def rows(a, rng):
        return _rows(a, rng[1] - rng[0])

    parts = [rows(stack("dmod"), PAY_DMOD), rows(stack("norm_g"), PAY_NORM), rows(stack("gq"), PAY_GQ),
             rows(stack("gkv"), PAY_GKV), rows(stack("rb8"), PAY_RB), rows(stack("fb"), PAY_FB),
             rows(final_g, PAY_FINAL), rows(loss, PAY_LOSS)]
    return jnp.concatenate(parts, axis=0)


def _payload_split(pay):
    def take(rng, shape):
        n = 1
        for d in shape:
            n *= d
        return pay[rng[0]:rng[1]].reshape(-1)[0:n].reshape(shape)

    norm_g = take(PAY_NORM, (DEPTH, D_MODEL))
    gq = take(PAY_GQ, (DEPTH, A_Q_RANK))
    gkv = take(PAY_GKV, (DEPTH, A_KV_RANK))
    rb = take(PAY_RB, (DEPTH, 8, 384))[:, 0:5, 0:N_REL]
    fb = take(PAY_FB, (DEPTH, LANE))[:, 0:5]
    final_g = take(PAY_FINAL, (D_MODEL,))
    return norm_g, gq, gkv, rb, fb, final_g


def kernel(x, c, positions, w_ada, b_ada, norm_g, w_in, a_q_norm_g, a_w_uq, a_kv_norm_g, a_w_ukv, b_rel_bias, c_forget_b, w_out, final_g, loss_target, m_w_ada, m_b_ada, m_norm_g, m_w_in, m_a_q_norm_g, m_a_w_uq, m_a_kv_norm_g, m_a_w_ukv, m_b_rel_bias, m_c_forget_b, m_w_out, m_final_g, v_w_ada, v_b_ada, v_norm_g, v_w_in, v_a_q_norm_g, v_a_w_uq, v_a_kv_norm_g, v_a_w_ukv, v_b_rel_bias, v_c_forget_b, v_w_out, v_final_g):
    nb, seq, _ = x.shape
    ix, iy, ic = lax.axis_index("x"), lax.axis_index("y"), lax.axis_index("c")
    chip = 2 * ix + iy
    me = 2 * chip + ic

    weight_plan = _gather_plan((256, 256, LR_ROWS))

    def gather_plan(x, y, c_):
        me_ = 4 * x + 2 * y + c_
        return weight_plan(x, y, c_) + [(3, False, 0, 8, 8 * me_, f, ()) for f in range(N_DEV)]

    full_in, full_out, full_lr, c_rows = _transfer(
        "gather_weights", [w_in.astype(BF16), w_out.astype(BF16), _pack_lowrank(a_w_uq, a_w_ukv).astype(BF16),
                           jnp.pad(c, ((0, 8 - nb), (0, 0)))[None]],
        [((DEPTH, D_MODEL, N_IN), BF16), ((DEPTH, D_MODEL, D_MODEL), BF16), ((1, 4 * LR_ROWS, PACK_COLS), BF16),
         ((1, 8 * N_DEV, D_MODEL), F32)], gather_plan)
    lowrank = [_unpack_lowrank(full_lr[0, LR_ROWS * j:LR_ROWS * (j + 1)]) for j in range(4)]
    full_uq = jnp.concatenate([s[0] for s in lowrank], axis=2)
    full_ukv = jnp.concatenate([s[1] for s in lowrank], axis=2)
    weights = [_layer_weights(full_in[l], full_out[l], full_uq[l], full_ukv[l]) for l in range(DEPTH)]
    small = [_layer_small(norm_g[l], a_q_norm_g[l], a_kv_norm_g[l], b_rel_bias[l], c_forget_b[l], final_g)
             for l in range(DEPTH)]

    c_all = c_rows.reshape(N_DEV, 8, D_MODEL)[:, 0:nb].reshape(N_DEV * nb, D_MODEL)
    cols = w_ada.shape[2]
    b_cols = lax.dynamic_slice_in_dim(b_ada, chip * cols, cols, axis=1)[:, None, :]
    mod_cols = _ada_fwd(c_all, w_ada, b_cols)
    mod_g = _gather_small(mod_cols.reshape(DEPTH * N_DEV * nb, cols), "gather_mod")
    mod_all = jnp.concatenate([mod_g[2 * j].reshape(DEPTH, N_DEV * nb, cols) for j in range(4)], axis=2)
    mod = lax.dynamic_slice_in_dim(mod_all, me * nb, nb, axis=1)

    tables = _rope_tables(positions)
    loss_part, dx, grads, gfinal, gw_in = _forward_backward(
        x.reshape(nb * seq, D_MODEL), mod, tables, loss_target.reshape(nb * seq, D_MODEL), weights, small, nb, seq)

    g_uq = jnp.stack([_uq_from_padded(g["w_uq"]) for g in grads])
    g_ukv = jnp.stack([_ukv_from_padded(g["w_ukv"]) for g in grads])
    g_lr = jnp.concatenate([_pack_lowrank(g_uq[:, :, 144 * j:144 * (j + 1)], g_ukv[:, :, 192 * j:192 * (j + 1)])
                            for j in range(4)], axis=1)
    partials = [gw_in[0], gw_in[1], jnp.stack([_out_from_padded(g["w_out"]) for g in grads]), g_lr]
    shapes = [(p.shape[0], p.shape[2]) for p in partials]
    halves = [r // 2 for r in SHARD_ROWS]
    core_s = jnp.reshape(ic, (1,)).astype(jnp.int32)
    place_s = jnp.stack([chip, ic]).astype(jnp.int32)
    pay = _small_payload(grads, gfinal, loss_part)
    pay_rows = pay.shape[0]
    reduce_plan = _pair_reduce_plan(SHARD_ROWS)

    def reduce_plan_with_payload(x, y, c_):
        me_ = 4 * x + 2 * y + c_
        return reduce_plan(x, y, c_) + [(len(partials), False, 0, pay_rows, pay_rows * me_, f, ()) for f in range(N_DEV)]

    *from_pair, pay_all = _transfer(
        "pair_reduce", partials + [pay[None]],
        [((nl, 4 * h, nc), F32) for (nl, nc), h in zip(shapes, halves)] + [((1, N_DEV * pay_rows, LANE), F32)],
        reduce_plan_with_payload)
    pay_all = pay_all.reshape(N_DEV, pay_rows, LANE)

    tot = _sum_blocks(pay_all, "sum_small")
    loss = tot[PAY_LOSS[0], 0]
    dmod_all = pay_all[:, PAY_DMOD[0]:PAY_DMOD[1]].reshape(N_DEV, -1)[:, 0:DEPTH * nb * 3 * D_MODEL]
    dmod_all = dmod_all.reshape(N_DEV, DEPTH, nb, 3 * D_MODEL).transpose(1, 0, 2, 3)
    dmod_all = dmod_all.reshape(DEPTH, N_DEV * nb, 3 * D_MODEL)
    my_cols = lax.dynamic_slice_in_dim(dmod_all, chip * cols, cols, axis=2)
    g_w_ada, g_b_ada = _ada_bwd(c_all, my_cols, dmod_all)
    g_b_ada = g_b_ada[:, 0]
    chip_sums = [_sum_pair(p, r, core_s, rc, "sum_pair%d" % i)
                 for i, (p, r, rc) in enumerate(zip(partials, from_pair, SHARD_ROWS))]
    from_chips = _transfer("chip_scatter", chip_sums, [((nl, 3 * h, nc), BF16) for (nl, nc), h in zip(shapes, halves)],
                           _chip_scatter_plan(SHARD_ROWS))
    reduced = [_sum_chips(s, r, place_s, rc, "sum_chips%d" % i)
               for i, (s, r, rc) in enumerate(zip(chip_sums, from_chips, SHARD_ROWS))]
    g_in_a, g_in_b, g_out_sh, g_lr_sh = _transfer("pair_share", reduced, [(r.shape, F32) for r in reduced],
                                                  _pair_share_plan(SHARD_ROWS), in_place=True)
    g_uq_sh, g_ukv_sh = _unpack_lowrank(g_lr_sh[0])

    def cols_first(a):
        return jnp.transpose(a, (2, 0, 1))

    g_in_t = _in_cols_first(g_in_a, g_in_b)
    upd_in = tuple(jnp.transpose(a, (1, 2, 0)) for a in _adamw(cols_first(w_in), g_in_t, cols_first(m_w_in),
                                                               cols_first(v_w_in), "adamw_in"))
    gw = (jnp.transpose(g_in_t, (1, 2, 0)), g_out_sh, g_uq_sh, g_ukv_sh)
    upd = [upd_in, _adamw(w_out, gw[1], m_w_out, v_w_out, "adamw_out"),
           _adamw(a_w_uq, gw[2], m_a_w_uq, v_a_w_uq, "adamw_uq"), _adamw(a_w_ukv, gw[3], m_a_w_ukv, v_a_w_ukv, "adamw_ukv")]
    dw, mw, vw = (tuple(u[i] for u in upd) for i in range(3))
    d_ada, m_ada, v_ada = _adamw(w_ada, g_w_ada, m_w_ada, v_w_ada, "adamw_ada")

    def adam_small(w, g, m, v, name):
        shape3 = (1,) * (3 - w.ndim) + w.shape
        return tuple(a.reshape(w.shape) for a in _adamw(w.reshape(shape3), g.reshape(shape3), m.reshape(shape3),
                                                        v.reshape(shape3), name))

    d_b, m_b, v_b = adam_small(b_ada, g_b_ada, m_b_ada, v_b_ada, "adamw_b_ada")
    gs = _payload_split(tot)
    small_upd = [adam_small(w, g, m, v, "adamw_small%d" % i) for i, (w, g, m, v) in enumerate(zip(
        (norm_g, a_q_norm_g, a_kv_norm_g, b_rel_bias, c_forget_b, final_g), gs,
        (m_norm_g, m_a_q_norm_g, m_a_kv_norm_g, m_b_rel_bias, m_c_forget_b, m_final_g),
        (v_norm_g, v_a_q_norm_g, v_a_kv_norm_g, v_b_rel_bias, v_c_forget_b, v_final_g)))]
    ds, ms, vs = (tuple(u[i] for u in small_upd) for i in range(3))

    def ordered(ada, b, sm, big):
        ng, gq, gkv, rb, fb, fg = sm
        b_in, b_out, b_uq, b_ukv = big
        return (ada, b, ng, b_in, gq, b_uq, gkv, b_ukv, rb, fb, b_out, fg)

    return (loss, dx.reshape(nb, seq, D_MODEL), *ordered(g_w_ada, g_b_ada, gs, gw), *ordered(d_ada, d_b, ds, dw),
            *ordered(m_ada, m_b, ms, mw), *ordered(v_ada, v_b, vs, vw))
```

```python
import functools

import jax
import jax.numpy as jnp
from jax import lax
from jax.experimental import pallas as pl
from jax.experimental.pallas import tpu as pltpu

F32 = jnp.float32
BF16 = jnp.bfloat16

D_MODEL = 1024
DEPTH = 2
EPS = 1e-6
NEG = -1e30
LOG2E = 1.4426950408889634
ROPE_THETA = 10000.0
A_ROPE = 32
A_Q_RANK = 384
A_KV_RANK = 256
REL_CLIP = 128
N_REL = 2 * REL_CLIP + 1
N_IN = 3621

ADAM_LR = 0.001
ADAM_B1 = 0.9
ADAM_B2 = 0.999
ADAM_EPS = 1e-08
ADAM_WD = 0.01
ADAM_STEP = 10

LANE = 128
VMEM_LIMIT = 56 * 1024 * 1024

NP_IN = 4352
Z_A = (0, 768)
Z_G = (768, 1920)
Z_QKV = tuple((1920 + 384 * i, 1920 + 384 * (i + 1)) for i in range(6))
Z_F = (4224, 4352)
IN_PIECES = ((0, 672, 96), (672, 1056, 0), (2016, 2336, 64), (3301, 3621, 64), (1056, 1376, 64), (1376, 1696, 64),
             (1696, 2016, 64), (2336, 2656, 64), (2656, 2976, 64), (2976, 3296, 64), (3296, 3301, 123))
D_CAT = 1152

TM = 512
T_CAUSAL = 256
T_BAND = 128
BAND_TILES = 5
N_DEV = 8

PAY_DMOD = (0, 96)
PAY_NORM = (96, 112)
PAY_GQ = (112, 120)
PAY_GKV = (120, 128)
PAY_RB = (128, 176)
PAY_FB = (176, 184)
PAY_FINAL = (184, 192)
PAY_LOSS = (192, 200)

PACK_COLS = 1024


def _params(sem=None):
    return pltpu.CompilerParams(dimension_semantics=sem, vmem_limit_bytes=VMEM_LIMIT)


def _lane_iota(shape):
    return lax.broadcasted_iota(jnp.int32, shape, len(shape) - 1)


def _silu(u):
    return u * jax.nn.sigmoid(u)


def _dsilu(u):
    s = jax.nn.sigmoid(u)
    return s * (1.0 + u * (1.0 - s))


def _rms(x, g):
    r = lax.rsqrt(jnp.mean(x * x, axis=-1, keepdims=True) + EPS)
    xh = x * r
    return xh * g, xh, r


def _rms_bwd(dy, xh, r, g):
    dxh = dy * g
    return r * (dxh - xh * jnp.mean(dxh * xh, axis=-1, keepdims=True))


def _rope(x, cos, sina, sinb):
    return x * cos + pltpu.roll(x, 16, 1) * sinb + pltpu.roll(x, LANE - 16, 1) * sina


def _rope_t(dy, cos, sina, sinb):
    return dy * cos + pltpu.roll(dy * sinb, LANE - 16, 1) + pltpu.roll(dy * sina, 16, 1)


def _split3(x):
    hi = x.astype(BF16)
    r1 = x - hi.astype(F32)
    mid = r1.astype(BF16)
    lo = (r1 - mid.astype(F32)).astype(BF16)
    return hi, mid, lo


def _dot(a, b):
    return jnp.dot(a, b, preferred_element_type=F32)


def _dot_nt(a, b):
    return lax.dot_general(a, b, (((1,), (1,)), ((), ())), preferred_element_type=F32)


def _dot_tn(a, b):
    return lax.dot_general(a, b, (((0,), (0,)), ((), ())), preferred_element_type=F32)


def _row_spec(cols):
    return pl.BlockSpec((TM, cols), lambda i: (i, 0))


def _full_spec(shape):
    return pl.BlockSpec(shape, lambda i: (0,) * len(shape))


def _ex_spec(tiles_per_ex):
    return pl.BlockSpec((1, 1, D_MODEL), lambda i: (i // tiles_per_ex, 0, 0))


def _ln_inproj(x, shift, scale, g, w_in_p, seq):
    t = x.shape[0]

    def body(x_ref, sh_ref, sc_ref, g_ref, w_ref, h_ref, za_ref, zg_ref, q0, q1, q2, q3, q4, q5, zf_ref):
        n, _, _ = _rms(x_ref[...], g_ref[...])
        h = (n * (1.0 + sc_ref[0]) + sh_ref[0]).astype(BF16)
        h_ref[...] = h
        za_ref[...] = _dot(h, w_ref[:, Z_A[0]:Z_A[1]])
        zg_ref[...] = _dot(h, w_ref[:, Z_G[0]:Z_G[1]])
        for ref, (c0, c1) in zip((q0, q1, q2, q3, q4, q5), Z_QKV):
            ref[...] = _dot(h, w_ref[:, c0:c1]).astype(BF16)
        zf_ref[...] = _dot(h, w_ref[:, Z_F[0]:Z_F[1]])

    tpe = seq // TM
    shapes = [jax.ShapeDtypeStruct((t, D_MODEL), BF16), jax.ShapeDtypeStruct((t, 768), F32),
              jax.ShapeDtypeStruct((t, D_CAT), F32)]
    shapes += [jax.ShapeDtypeStruct((t, 384), BF16)] * 6 + [jax.ShapeDtypeStruct((t, LANE), F32)]
    return pl.pallas_call(
        body, name="ln_inproj", grid=(t // TM,),
        in_specs=[_row_spec(D_MODEL), _ex_spec(tpe), _ex_spec(tpe), _full_spec((1, D_MODEL)),
                  _full_spec((D_MODEL, NP_IN))],
        out_specs=[_row_spec(D_MODEL), _row_spec(768), _row_spec(D_CAT)] + [_row_spec(384)] * 6 + [_row_spec(LANE)],
        out_shape=shapes, compiler_params=_params(("parallel",)),
    )(x, shift, scale, g, w_in_p)


def _a_up(za, gq, gkv, w_uq_p, w_ukv_p, cos, sina, sinb):
    t = za.shape[0]

    def body(za_ref, gq_ref, gkv_ref, wq_ref, wkv_ref, cos_ref, sa_ref, sb_ref, q_ref, k_ref, v_ref):
        cos_t, sa, sb = cos_ref[...], sa_ref[...], sb_ref[...]
        cqn, _, _ = _rms(za_ref[:, 0:384], gq_ref[...])
        q = _dot(cqn.astype(BF16), wq_ref[...])
        ckvn, _, _ = _rms(za_ref[:, 384:640], gkv_ref[...])
        kv = _dot(ckvn.astype(BF16), wkv_ref[...])
        kpe = za_ref[:, 640:768]
        kpe = _rope(kpe + pltpu.roll(kpe, 32, 1), cos_t, sa, sb).astype(BF16)
        for p in range(3):
            q_ref[:, 256 * p:256 * p + 128] = q[:, 256 * p:256 * p + 128].astype(BF16)
            q_ref[:, 256 * p + 128:256 * p + 256] = _rope(q[:, 256 * p + 128:256 * p + 256], cos_t, sa, sb).astype(BF16)
            k_ref[:, 256 * p:256 * p + 128] = kv[:, 128 * p:128 * p + 128].astype(BF16)
            k_ref[:, 256 * p + 128:256 * p + 256] = kpe
        v_ref[...] = kv[:, 384:768].astype(BF16)

    return pl.pallas_call(
        body, name="a_up", grid=(t // TM,),
        in_specs=[_row_spec(768), _full_spec((1, 384)), _full_spec((1, 256)), _full_spec((384, 768)),
                  _full_spec((256, 768)), _row_spec(LANE), _row_spec(LANE), _row_spec(LANE)],
        out_specs=[_row_spec(768), _row_spec(768), _row_spec(384)],
        out_shape=[jax.ShapeDtypeStruct((t, 768), BF16), jax.ShapeDtypeStruct((t, 768), BF16),
                   jax.ShapeDtypeStruct((t, 384), BF16)],
        compiler_params=_params(("parallel",)),
    )(za, gq, gkv, w_uq_p, w_ukv_p, cos, sina, sinb)


def _tri(n, upper):
    r = lax.broadcasted_iota(jnp.int32, (n, n), 0)
    c = lax.broadcasted_iota(jnp.int32, (n, n), 1)
    return jnp.where((c >= r) if upper else (c <= r), 1.0, 0.0).astype(BF16)


def _forget_fwd(zf, fb, nb, seq):
    blk = 256

    def body(zf_ref, fb_ref, f_ref):
        tri = _tri(blk, False)
        live = _lane_iota((blk, LANE)) < 5
        carry = jnp.zeros((1, LANE), F32)
        for i in range(seq // blk):
            u = zf_ref[i * blk:(i + 1) * blk, :] + fb_ref[...]
            lf = jnp.where(live, jnp.minimum(u, 0.0) - jnp.log(1.0 + jnp.exp(-jnp.abs(u))), 0.0)
            hi, mid, lo = _split3(lf)
            f_ref[i * blk:(i + 1) * blk, :] = (_dot(tri, hi) + _dot(tri, mid) + _dot(tri, lo) + carry) * LOG2E
            carry = carry + jnp.sum(lf, axis=0, keepdims=True)

    return pl.pallas_call(
        body, name="forget_fwd", grid=(nb,),
        in_specs=[pl.BlockSpec((seq, LANE), lambda b: (b, 0)), pl.BlockSpec((1, LANE), lambda b: (0, 0))],
        out_specs=pl.BlockSpec((seq, LANE), lambda b: (b, 0)),
        out_shape=jax.ShapeDtypeStruct((nb * seq, LANE), F32), compiler_params=_params(("parallel",)),
    )(zf, fb)


def _forget_bwd(dfq, dfk, zf, fb, nb, seq):
    blk = 256

    def body(dfq_ref, dfk_ref, zf_ref, fb_ref, dz_ref, gb_ref):
        @pl.when(pl.program_id(0) == 0)
        def _():
            gb_ref[...] = jnp.zeros_like(gb_ref)

        tri = _tri(blk, True)
        lane = _lane_iota((blk, LANE))
        wide = _lane_iota((blk, 384))
        live = lane < 5
        carry = jnp.zeros((1, LANE), F32)
        gsum = jnp.zeros((1, LANE), F32)
        for i in reversed(range(seq // blk)):
            d = dfk_ref[i * blk:(i + 1) * blk, :]
            dq = dfq_ref[i * blk:(i + 1) * blk, :]
            for hd in range(5):
                col = jnp.sum(jnp.where(wide == 64 * hd, dq, 0.0), axis=-1, keepdims=True)
                d = d + jnp.where(lane == hd, col, 0.0)
            hi, mid, lo = _split3(d)
            dlf = _dot(tri, hi) + _dot(tri, mid) + _dot(tri, lo) + carry
            carry = carry + jnp.sum(d, axis=0, keepdims=True)
            u = zf_ref[i * blk:(i + 1) * blk, :] + fb_ref[...]
            du = jnp.where(live, dlf * jax.nn.sigmoid(-u), 0.0)
            dz_ref[i * blk:(i + 1) * blk, :] = du.astype(BF16)
            gsum = gsum + jnp.sum(du, axis=0, keepdims=True)
        gb_ref[...] += jnp.broadcast_to(gsum, gb_ref.shape)

    return pl.pallas_call(
        body, name="forget_bwd", grid=(nb,),
        in_specs=[pl.BlockSpec((seq, 384), lambda b: (b, 0)), pl.BlockSpec((seq, LANE), lambda b: (b, 0)),
                  pl.BlockSpec((seq, LANE), lambda b: (b, 0)), pl.BlockSpec((1, LANE), lambda b: (0, 0))],
        out_specs=[pl.BlockSpec((seq, LANE), lambda b: (b, 0)), pl.BlockSpec((8, LANE), lambda b: (0, 0))],
        out_shape=[jax.ShapeDtypeStruct((nb * seq, LANE), BF16), jax.ShapeDtypeStruct((8, LANE), F32)],
        compiler_params=_params(("arbitrary",)),
    )(dfq, dfk, zf, fb)


def _gate_outproj(x, gate, oa, ob, oc, zg, w_out_p, seq):
    t = x.shape[0]

    def body(x_ref, gate_ref, oa_ref, ob_ref, oc_ref, zg_ref, w_ref, y_ref, xn_ref):
        y = jnp.zeros((TM, D_MODEL), F32)
        for i, o_ref in enumerate((oa_ref, ob_ref, oc_ref)):
            cat = (o_ref[...] * _silu(zg_ref[:, 384 * i:384 * (i + 1)])).astype(BF16)
            y = y + _dot(cat, w_ref[384 * i:384 * (i + 1), :])
        y_ref[...] = y
        xn_ref[...] = x_ref[...] + gate_ref[0] * y

    return pl.pallas_call(
        body, name="gate_outproj", grid=(t // TM,),
        in_specs=[_row_spec(D_MODEL), _ex_spec(seq // TM), _row_spec(384), _row_spec(384), _row_spec(384),
                  _row_spec(D_CAT), _full_spec((D_CAT, D_MODEL))],
        out_specs=[_row_spec(D_MODEL), _row_spec(D_MODEL)],
        out_shape=[jax.ShapeDtypeStruct((t, D_MODEL), F32)] * 2, compiler_params=_params(("parallel",)),
    )(x, gate, oa, ob, oc, zg, w_out_p)


def _final_loss(x, target, g):
    t = x.shape[0]

    def body(x_ref, t_ref, g_ref, dx_ref, loss_ref, gg_ref):
        @pl.when(pl.program_id(0) == 0)
        def _():
            loss_ref[...] = jnp.zeros_like(loss_ref)
            gg_ref[...] = jnp.zeros_like(gg_ref)

        gv = g_ref[...]
        out, xh, r = _rms(x_ref[...], gv)
        err = out - t_ref[...]
        loss_ref[...] += 0.5 * jnp.sum(jnp.mean(err * err, axis=-1, keepdims=True), axis=0, keepdims=True)
        dout = err / D_MODEL
        gg_ref[...] += jnp.broadcast_to(jnp.sum(dout * xh, axis=0, keepdims=True), gg_ref.shape)
        dx_ref[...] = _rms_bwd(dout, xh, r, gv)

    return pl.pallas_call(
        body, name="final_loss", grid=(t // TM,),
        in_specs=[_row_spec(D_MODEL), _row_spec(D_MODEL), _full_spec((1, D_MODEL))],
        out_specs=[_row_spec(D_MODEL), _full_spec((8, LANE)), _full_spec((8, D_MODEL))],
        out_shape=[jax.ShapeDtypeStruct((t, D_MODEL), F32), jax.ShapeDtypeStruct((8, LANE), F32),
                   jax.ShapeDtypeStruct((8, D_MODEL), F32)],
        compiler_params=_params(("arbitrary",)),
    )(x, target, g)


def _head_masks(kind, rows, dq, h):
    lq = _lane_iota((rows, dq))
    lv = _lane_iota((rows, LANE))
    mq = (lq >= 64 * h) & (lq < 64 * h + 64)
    if kind == "A":
        mq = mq | ((lq >= 128 + 32 * h) & (lq < 160 + 32 * h))
    return mq, (lv >= 64 * h) & (lv < 64 * h + 64)


def _tile_mask(kind, tile):
    row = lax.broadcasted_iota(jnp.int32, (tile, tile), 0)
    col = lax.broadcasted_iota(jnp.int32, (tile, tile), 1)
    return (col >> 6) <= (row >> 6) if kind == "A" else col <= row


def _attn_scale(kind):
    return 96.0 ** -0.5 if kind == "A" else 0.125


BAND_W = BAND_TILES * T_BAND


def _segments(kind, qi, tile):
    r0 = qi * tile
    if kind == "B":
        lo = max(qi - (BAND_TILES - 1), 0) * tile
        return [(lo, r0 + tile, False, lo - (qi - (BAND_TILES - 1)) * tile)]
    return ([(0, r0, False, 0)] if qi else []) + [(r0, r0 + tile, True, 0)]


def _scores(kind, qh, k_ref, aux_ref, h, seg, tile, scale):
    a, b, diag, c0 = seg
    s = _dot_nt(qh, k_ref[a:b, :]) * (scale * LOG2E)
    if kind == "B":
        return s + aux_ref[h, :, c0:BAND_W]
    if kind == "C":
        s = s - aux_ref[0, h, :, a:b]
    if diag:
        s = jnp.where(_tile_mask(kind, tile), s, NEG)
    return s


FWD_AHEAD = 2
BWD_AHEAD = 2


def _run_ahead(units, first, second, depth):
    queue = [first(*u) for u in units[:depth]]
    for i, (_, h) in enumerate(units):
        if i + depth < len(units):
            queue.append(first(*units[i + depth]))
        second(h, *queue.pop(0))


def _attn_fwd(kind, q, k, v, aux, nb, seq):
    dq = q.shape[1] // 3
    tile = T_BAND if kind == "B" else T_CAUSAL
    nq = seq // tile
    scale = _attn_scale(kind)

    def body(*refs):
        if kind == "A":
            q_ref, k_ref, v_ref, o_ref, lse_ref = refs
            aux_ref = None
        else:
            q_ref, k_ref, v_ref, aux_ref, o_ref, lse_ref = refs
        def logits(qi, h):
            rows = slice(qi * tile, (qi + 1) * tile)
            q2 = q_ref[rows, :]
            mq, _ = _head_masks(kind, tile, dq, h)
            qh = jnp.where(mq, q2, jnp.zeros_like(q2))
            segs = _segments(kind, qi, tile)
            return rows, segs, [_scores(kind, qh, k_ref, aux_ref, h, seg, tile, scale) for seg in segs]

        def finish(h, rows, segs, ss):
            _, mv = _head_masks(kind, tile, dq, h)
            mx = functools.reduce(jnp.maximum, [jnp.max(s, axis=-1, keepdims=True) for s in ss])
            ps = [jnp.exp2(s - mx) for s in ss]
            l = functools.reduce(jnp.add, [jnp.sum(p, axis=-1, keepdims=True) for p in ps])
            acc = functools.reduce(jnp.add, [_dot(p.astype(BF16), v_ref[seg[0]:seg[1], :]) for p, seg in zip(ps, segs)])
            o_h = jnp.where(mv, acc / l, 0.0)
            lse_h = jnp.where(mv, mx + jnp.log(l) * LOG2E, 0.0)
            if h == 0:
                o_ref[rows, :] = o_h
                lse_ref[rows, :] = lse_h
            else:
                o_ref[rows, :] += o_h
                lse_ref[rows, :] += lse_h

        _run_ahead([(qi, h) for qi in range(nq) for h in range(2)], logits, finish, FWD_AHEAD)

    def seq_spec(cols):
        return pl.BlockSpec((seq, cols), lambda b, p: (b, p))

    in_specs = [seq_spec(dq), seq_spec(dq), seq_spec(LANE)]
    args = [q, k, v]
    if kind == "B":
        in_specs.append(pl.BlockSpec((2, tile, BAND_W), lambda b, p: (p, 0, 0)))
        args.append(aux)
    if kind == "C":
        in_specs.append(pl.BlockSpec((1, 2, 1, seq), lambda b, p: (b, p, 0, 0)))
        args.append(aux)
    return pl.pallas_call(
        body, name="attn_fwd_" + kind, grid=(nb, 3), in_specs=in_specs, out_specs=[seq_spec(LANE), seq_spec(LANE)],
        out_shape=[jax.ShapeDtypeStruct((nb * seq, 384), F32)] * 2, compiler_params=_params(("parallel", "parallel")),
    )(*args)


def _attn_bwd(kind, q, k, v, o, do, lse, aux, nb, seq):
    dq = q.shape[1] // 3
    tile = T_BAND if kind == "B" else T_CAUSAL
    nq = seq // tile
    scale = _attn_scale(kind)
    dqk_dtype = F32 if kind == "A" else BF16

    def body(*refs):
        dfr_ref = dfq_ref = dbt_ref = aux_ref = None
        if kind == "A":
            q_ref, k_ref, v_ref, o_ref, do_ref, lse_ref, dq_ref, dk_ref, dv_ref, dkt_acc, dvt_acc = refs
        elif kind == "B":
            q_ref, k_ref, v_ref, o_ref, do_ref, lse_ref, aux_ref, dq_ref, dk_ref, dv_ref, dbt_ref, dkt_acc, dvt_acc = refs
        else:
            (q_ref, k_ref, v_ref, o_ref, do_ref, lse_ref, aux_ref, dq_ref, dk_ref, dv_ref, dfr_ref, dfq_ref,
             dkt_acc, dvt_acc) = refs
        dkt_acc[...] = jnp.zeros_like(dkt_acc)
        dvt_acc[...] = jnp.zeros_like(dvt_acc)
        if kind == "C":
            dfr_ref[...] = jnp.zeros_like(dfr_ref)
        if kind == "B":
            @pl.when(pl.program_id(1) == 0)
            def _():
                dbt_ref[...] = jnp.zeros_like(dbt_ref)

        def products(qi, h):
            rows = slice(qi * tile, (qi + 1) * tile)
            q2 = q_ref[rows, :]
            mq, mv = _head_masks(kind, tile, dq, h)
            qh = jnp.where(mq, q2, jnp.zeros_like(q2))
            doh = jnp.where(mv, do_ref[rows, :], 0.0)
            dob = doh.astype(BF16)
            segs = _segments(kind, qi, tile)
            ts = [_scores(kind, qh, k_ref, aux_ref, h, seg, tile, scale) for seg in segs]
            dps = [_dot_nt(dob, v_ref[seg[0]:seg[1], :]) for seg in segs]
            return rows, segs, qh, doh, ts, dps

        def finish(h, rows, segs, qh, doh, ts, dps):
            mq, mv = _head_masks(kind, tile, dq, h)
            qht = qh.astype(F32).T.astype(BF16)
            dobt = doh.T.astype(BF16)
            head_rows = [(64 * h, 64)] + ([(128 + 32 * h, 32)] if kind == "A" else [])
            delta = jnp.sum(doh * o_ref[rows, :], axis=-1, keepdims=True)
            lseh = jnp.max(jnp.where(mv, lse_ref[rows, :], NEG), axis=-1, keepdims=True)
            rs = jnp.zeros((tile, 1), F32)
            dq_h = jnp.zeros((tile, dq), F32)
            for (a, b, _, c0), t, dp in zip(segs, ts, dps):
                p = jnp.exp2(t - lseh)
                ds = p * (dp - delta)
                if kind == "B":
                    dbt_ref[h, :, c0:BAND_W] += ds
                if kind == "C":
                    dfr_ref[0, h, :, a:b] -= jnp.sum(ds, axis=0, keepdims=True)
                    rs = rs + jnp.sum(ds, axis=-1, keepdims=True)
                dss = (ds * scale).astype(BF16)
                dvt_acc[64 * h:64 * h + 64, a:b] += _dot(dobt[64 * h:64 * h + 64, :], p.astype(BF16))
                for r0, n in head_rows:
                    dkt_acc[r0:r0 + n, a:b] += _dot(qht[r0:r0 + n, :], dss)
                dq_h = dq_h + _dot(dss, k_ref[a:b, :])
            dq_h = jnp.where(mq, dq_h, 0.0).astype(dqk_dtype)
            if h == 0:
                dq_ref[rows, :] = dq_h
            else:
                dq_ref[rows, :] += dq_h
            if kind == "C":
                if h == 0:
                    dfq_ref[rows, :] = jnp.where(mv, rs, 0.0)
                else:
                    dfq_ref[rows, :] += jnp.where(mv, rs, 0.0)

        _run_ahead([(qi, h) for qi in range(nq) for h in range(2)], products, finish, BWD_AHEAD)
        for j in range(seq // 256):
            cols = slice(256 * j, 256 * (j + 1))
            dk_ref[cols, :] = dkt_acc[:, cols].T.astype(dqk_dtype)
            dv_ref[cols, :] = dvt_acc[:, cols].T.astype(BF16)

    def seq_spec(cols):
        return pl.BlockSpec((seq, cols), lambda p, b: (b, p))

    in_specs = [seq_spec(dq), seq_spec(dq), seq_spec(LANE), seq_spec(LANE), seq_spec(LANE), seq_spec(LANE)]
    args = [q, k, v, o, do, lse]
    out_specs = [seq_spec(dq), seq_spec(dq), seq_spec(LANE)]
    out_shape = [jax.ShapeDtypeStruct((nb * seq, 3 * dq), dqk_dtype)] * 2 + [jax.ShapeDtypeStruct((nb * seq, 384), BF16)]
    if kind == "B":
        spec = pl.BlockSpec((2, tile, BAND_W), lambda p, b: (p, 0, 0))
        in_specs.append(spec)
        args.append(aux)
        out_specs.append(spec)
        out_shape.append(jax.ShapeDtypeStruct((6, tile, BAND_W), F32))
    if kind == "C":
        spec = pl.BlockSpec((1, 2, 1, seq), lambda p, b: (b, p, 0, 0))
        in_specs.append(spec)
        args.append(aux)
        out_specs += [spec, seq_spec(LANE)]
        out_shape += [jax.ShapeDtypeStruct((nb, 6, 1, seq), F32), jax.ShapeDtypeStruct((nb * seq, 384), F32)]
    return pl.pallas_call(
        body, name="attn_bwd_" + kind, grid=(3, nb), in_specs=in_specs, out_specs=out_specs, out_shape=out_shape,
        scratch_shapes=[pltpu.VMEM((dq, seq), F32), pltpu.VMEM((LANE, seq), F32)],
        compiler_params=_params(("arbitrary", "arbitrary")),
    )(*args)


BIAS_G = 768
BIAS_EDGE = BIAS_G - N_REL


def _bias_line(rel_bias):
    g = jnp.concatenate([jnp.broadcast_to(rel_bias[:, N_REL - 1:], (rel_bias.shape[0], BIAS_EDGE)),
                         jnp.flip(rel_bias, axis=1)], axis=1)
    return jnp.pad(g, ((0, 8 - g.shape[0]), (0, 0)))


def _bias_unline(dg):
    return jnp.flip(dg[:, BIAS_EDGE:], axis=1)


def _bias_expand(g8):
    def body(g_ref, out_ref):
        line = jnp.broadcast_to(g_ref[0] * LOG2E, (T_BAND, BIAS_G))
        slab = pltpu.roll(line, 1, 1, stride=1, stride_axis=0)[:, LANE:BIAS_G]
        row = lax.broadcasted_iota(jnp.int32, (T_BAND, BAND_W), 0)
        col = lax.broadcasted_iota(jnp.int32, (T_BAND, BAND_W), 1)
        hidden = ((row >= 64) & (col < 64)) | ((row < 64) & (col >= BAND_W - 64))
        out_ref[0] = jnp.where(hidden, NEG, slab)

    return pl.pallas_call(
        body, name="bias_expand", grid=(8,), in_specs=[pl.BlockSpec((1, 1, BIAS_G), lambda h: (h, 0, 0))],
        out_specs=pl.BlockSpec((1, T_BAND, BAND_W), lambda h: (h, 0, 0)),
        out_shape=jax.ShapeDtypeStruct((8, T_BAND, BAND_W), F32), compiler_params=_params(("parallel",)),
    )(g8.reshape(8, 1, BIAS_G))


def _bias_reduce(d_slab):
    def body(d_ref, out_ref):
        r = lax.broadcasted_iota(jnp.int32, (T_BAND, T_BAND), 0)
        k = lax.broadcasted_iota(jnp.int32, (T_BAND, T_BAND), 1)
        flip = jnp.where(r + k == T_BAND - 1, 1.0, 0.0).astype(BF16)
        hi, mid, lo = _split3(d_ref[0])
        d_rev = _dot(flip, hi) + _dot(flip, mid) + _dot(flip, lo)
        wide = jnp.concatenate([jnp.zeros((T_BAND, LANE), F32), d_rev, jnp.zeros((T_BAND, 2 * LANE), F32)], axis=1)
        skew = pltpu.roll(wide, 0, 1, stride=1, stride_axis=0)
        dg = jnp.sum(skew, axis=0, keepdims=True)[:, LANE:LANE + BIAS_G]
        lane = _lane_iota((1, BIAS_G))
        clipped = jnp.sum(jnp.where(lane <= BIAS_EDGE, dg, 0.0), axis=1, keepdims=True)
        out_ref[0] = jnp.where(lane == BIAS_EDGE, clipped, dg)

    return pl.pallas_call(
        body, name="bias_reduce", grid=(8,), in_specs=[pl.BlockSpec((1, T_BAND, BAND_W), lambda h: (h, 0, 0))],
        out_specs=pl.BlockSpec((1, 1, BIAS_G), lambda h: (h, 0, 0)),
        out_shape=jax.ShapeDtypeStruct((8, 1, BIAS_G), F32), compiler_params=_params(("parallel",)),
    )(d_slab).reshape(8, BIAS_G)


def _outproj_bwd(dxn, y, gate, oa, ob, oc, zg, w_out_p, w_out_pt, nb, seq):
    t = dxn.shape[0]
    tpe = seq // TM

    def body(dxn_ref, y_ref, gate_ref, oa_ref, ob_ref, oc_ref, zg_ref, w_ref, wt_ref,
             doa_ref, dob_ref, doc_ref, dzg_ref, gw_ref, dgate_ref):
        i = pl.program_id(0)

        @pl.when(i == 0)
        def _():
            gw_ref[...] = jnp.zeros_like(gw_ref)

        @pl.when(i % tpe == 0)
        def _():
            dgate_ref[...] = jnp.zeros_like(dgate_ref)

        dxn_t = dxn_ref[...]
        dgate_ref[0] += jnp.sum(dxn_t * y_ref[...], axis=0, keepdims=True)
        dy = (dxn_t * gate_ref[0]).astype(BF16)
        for gi, (o_ref, do_ref) in enumerate(((oa_ref, doa_ref), (ob_ref, dob_ref), (oc_ref, doc_ref))):
            cols = slice(384 * gi, 384 * (gi + 1))
            u = zg_ref[:, cols]
            o_t = o_ref[...]
            su = _silu(u)
            dcat = _dot(dy, wt_ref[:, cols])
            do_ref[...] = dcat * su
            dzg_ref[:, cols] = (dcat * o_t * _dsilu(u)).astype(BF16)
            gw_ref[cols, :] += _dot_tn((o_t * su).astype(BF16), dy)

    return pl.pallas_call(
        body, name="outproj_bwd", grid=(t // TM,),
        in_specs=[_row_spec(D_MODEL), _row_spec(D_MODEL), _ex_spec(tpe), _row_spec(384), _row_spec(384), _row_spec(384),
                  _row_spec(D_CAT), _full_spec((D_CAT, D_MODEL)), _full_spec((D_MODEL, D_CAT))],
        out_specs=[_row_spec(384), _row_spec(384), _row_spec(384), _row_spec(D_CAT), _full_spec((D_CAT, D_MODEL)),
                   _ex_spec(tpe)],
        out_shape=[jax.ShapeDtypeStruct((t, 384), F32)] * 3 + [jax.ShapeDtypeStruct((t, D_CAT), BF16),
                                                                jax.ShapeDtypeStruct((D_CAT, D_MODEL), F32),
                                                                jax.ShapeDtypeStruct((nb, 1, D_MODEL), F32)],
        compiler_params=_params(("arbitrary",)),
    )(dxn, y, gate, oa, ob, oc, zg, w_out_p, w_out_pt)


def _a_up_bwd(dqa, dka, dva, za, gq, gkv, w_uq_pt, w_ukv_pt, cos, sina, sinb):
    t = za.shape[0]

    def body(dq_ref, dk_ref, dv_ref, za_ref, gq_ref, gkv_ref, wqt_ref, wkvt_ref, cos_ref, sa_ref, sb_ref,
             dza_ref, gwq_ref, gwkv_ref, ggq_ref, ggkv_ref, dqb, dkvb):
        @pl.when(pl.program_id(0) == 0)
        def _():
            gwq_ref[...] = jnp.zeros_like(gwq_ref)
            gwkv_ref[...] = jnp.zeros_like(gwkv_ref)
            ggq_ref[...] = jnp.zeros_like(ggq_ref)
            ggkv_ref[...] = jnp.zeros_like(ggkv_ref)

        cos_t, sa, sb = cos_ref[...], sa_ref[...], sb_ref[...]
        dkpe = jnp.zeros((TM, LANE), F32)
        for p in range(3):
            dqb[:, 256 * p:256 * p + 128] = dq_ref[:, 256 * p:256 * p + 128].astype(BF16)
            dqb[:, 256 * p + 128:256 * p + 256] = _rope_t(dq_ref[:, 256 * p + 128:256 * p + 256], cos_t, sa, sb).astype(BF16)
            dkvb[:, 128 * p:128 * p + 128] = dk_ref[:, 256 * p:256 * p + 128].astype(BF16)
            dkpe = dkpe + dk_ref[:, 256 * p + 128:256 * p + 256]
        dkvb[:, 384:768] = dv_ref[...]
        dkpe = _rope_t(dkpe, cos_t, sa, sb)
        dkpe = jnp.where(_lane_iota((TM, LANE)) < A_ROPE, dkpe + pltpu.roll(dkpe, LANE - 32, 1), 0.0)

        gqv = gq_ref[...]
        cqn, cqh, rq = _rms(za_ref[:, 0:384], gqv)
        dq_t = dqb[...]
        gwq_ref[...] += _dot_tn(cqn.astype(BF16), dq_t)
        dcqn = _dot(dq_t, wqt_ref[...])
        ggq_ref[...] += jnp.broadcast_to(jnp.sum(dcqn * cqh, axis=0, keepdims=True), ggq_ref.shape)
        dza_ref[:, 0:384] = _rms_bwd(dcqn, cqh, rq, gqv).astype(BF16)

        gkvv = gkv_ref[...]
        ckvn, ckvh, rkv = _rms(za_ref[:, 384:640], gkvv)
        dkv_t = dkvb[...]
        gwkv_ref[...] += _dot_tn(ckvn.astype(BF16), dkv_t)
        dckvn = _dot(dkv_t, wkvt_ref[...])
        ggkv_ref[...] += jnp.broadcast_to(jnp.sum(dckvn * ckvh, axis=0, keepdims=True), ggkv_ref.shape)
        dza_ref[:, 384:640] = _rms_bwd(dckvn, ckvh, rkv, gkvv).astype(BF16)
        dza_ref[:, 640:768] = dkpe.astype(BF16)

    return pl.pallas_call(
        body, name="a_up_bwd", grid=(t // TM,),
        in_specs=[_row_spec(768), _row_spec(768), _row_spec(384), _row_spec(768), _full_spec((1, 384)),
                  _full_spec((1, 256)), _full_spec((768, 384)), _full_spec((768, 256)), _row_spec(LANE), _row_spec(LANE),
                  _row_spec(LANE)],
        out_specs=[_row_spec(768), _full_spec((384, 768)), _full_spec((256, 768)), _full_spec((8, 384)),
                   _full_spec((8, 256))],
        out_shape=[jax.ShapeDtypeStruct((t, 768), BF16), jax.ShapeDtypeStruct((384, 768), F32),
                   jax.ShapeDtypeStruct((256, 768), F32), jax.ShapeDtypeStruct((8, 384), F32),
                   jax.ShapeDtypeStruct((8, 256), F32)],
        scratch_shapes=[pltpu.VMEM((TM, 768), BF16), pltpu.VMEM((TM, 768), BF16)],
        compiler_params=_params(("arbitrary",)),
    )(dqa, dka, dva, za, gq, gkv, w_uq_pt, w_ukv_pt, cos, sina, sinb)


def _dz_cols():
    return (Z_A, Z_G) + Z_QKV + (Z_F,)


def _inproj_bwd_dx(dz, dxn, x, shift, scale, g, w_in_pt, nb, seq):
    t = x.shape[0]
    tpe = seq // TM
    cols = _dz_cols()

    def body(*refs):
        dz_refs = refs[:len(cols)]
        dxn_ref, x_ref, sh_ref, sc_ref, g_ref, wt_ref, dx_ref, dsh_ref, dsc_ref, dg_ref = refs[len(cols):]
        i = pl.program_id(0)

        @pl.when(i == 0)
        def _():
            dg_ref[...] = jnp.zeros_like(dg_ref)

        @pl.when(i % tpe == 0)
        def _():
            dsh_ref[...] = jnp.zeros_like(dsh_ref)
            dsc_ref[...] = jnp.zeros_like(dsc_ref)

        dh = jnp.zeros((TM, D_MODEL), F32)
        for ref, (c0, c1) in zip(dz_refs, cols):
            dh = dh + _dot_nt(ref[...], wt_ref[:, c0:c1])
        gv = g_ref[...]
        n, xh, r = _rms(x_ref[...], gv)
        dsh_ref[0] += jnp.sum(dh, axis=0, keepdims=True)
        dsc_ref[0] += jnp.sum(dh * n, axis=0, keepdims=True)
        dn = dh * (1.0 + sc_ref[0])
        dg_ref[...] += jnp.broadcast_to(jnp.sum(dn * xh, axis=0, keepdims=True), dg_ref.shape)
        dx_ref[...] = dxn_ref[...] + _rms_bwd(dn, xh, r, gv)

    in_specs = [_row_spec(c1 - c0) for c0, c1 in cols]
    in_specs += [_row_spec(D_MODEL), _row_spec(D_MODEL), _ex_spec(tpe), _ex_spec(tpe), _full_spec((1, D_MODEL)),
                 _full_spec((D_MODEL, NP_IN))]
    return pl.pallas_call(
        body, name="inproj_bwd_dx", grid=(t // TM,), in_specs=in_specs,
        out_specs=[_row_spec(D_MODEL), _ex_spec(tpe), _ex_spec(tpe), _full_spec((8, D_MODEL))],
        out_shape=[jax.ShapeDtypeStruct((t, D_MODEL), F32), jax.ShapeDtypeStruct((nb, 1, D_MODEL), F32),
                   jax.ShapeDtypeStruct((nb, 1, D_MODEL), F32), jax.ShapeDtypeStruct((8, D_MODEL), F32)],
        compiler_params=_params(("arbitrary",)),
    )(*dz, dxn, x, shift, scale, g, w_in_pt)


def _inproj_bwd_dw(h, dz, name, layer, both=None):
    t = h.shape[0]
    widths = [d.shape[1] for d in dz]
    total = sum(widths)

    def body(*refs):
        h_ref = refs[0]
        dz_refs = refs[1:1 + len(dz)]
        gw_ref = refs[-1]

        @pl.when(pl.program_id(0) == 0)
        def _():
            gw_ref[...] = jnp.zeros_like(gw_ref)

        h_t = h_ref[...]
        c0 = 0
        for ref, w in zip(dz_refs, widths):
            gw_ref[0, :, c0:c0 + w] += _dot_tn(h_t, ref[...])
            c0 += w

    in_specs = [_row_spec(D_MODEL)] + [_row_spec(w) for w in widths]
    args = [h, *dz]
    aliases = {}
    if both is not None:
        in_specs.append(pl.BlockSpec(memory_space=pl.ANY))
        aliases = {len(args): 0}
        args.append(both)
    return pl.pallas_call(
        body, name=name, grid=(t // TM,), in_specs=in_specs,
        out_specs=pl.BlockSpec((1, D_MODEL, total), lambda i: (layer, 0, 0)),
        out_shape=jax.ShapeDtypeStruct((DEPTH, D_MODEL, total), F32), input_output_aliases=aliases,
        compiler_params=_params(("arbitrary",)),
    )(*args)


def _ada_fwd(c_all, w_ada, b_cols):
    n = c_all.shape[0]
    cols = w_ada.shape[2]

    def body(c_ref, w_ref, b_ref, out_ref):
        act = _silu(c_ref[...]).astype(BF16)
        out_ref[0] = _dot(act, w_ref[0].astype(BF16)) + b_ref[0]

    return pl.pallas_call(
        body, name="ada_fwd", grid=(DEPTH,),
        in_specs=[pl.BlockSpec((n, D_MODEL), lambda l: (0, 0)), pl.BlockSpec((1, D_MODEL, cols), lambda l: (l, 0, 0)),
                  pl.BlockSpec((1, 1, cols), lambda l: (l, 0, 0))],
        out_specs=pl.BlockSpec((1, n, cols), lambda l: (l, 0, 0)),
        out_shape=jax.ShapeDtypeStruct((DEPTH, n, cols), F32), compiler_params=_params(("parallel",)),
    )(c_all, w_ada, b_cols)


def _ada_bwd(c_all, dmod_cols, dmod_all):
    n = c_all.shape[0]
    cols = dmod_cols.shape[2]

    def body(c_ref, dc_ref, da_ref, gw_ref, gb_ref):
        act = _silu(c_ref[...]).astype(BF16)
        gw_ref[0] = _dot_tn(act, dc_ref[0].astype(BF16))
        gb_ref[0] = jnp.sum(da_ref[0], axis=0, keepdims=True)

    return pl.pallas_call(
        body, name="ada_bwd", grid=(DEPTH,),
        in_specs=[pl.BlockSpec((n, D_MODEL), lambda l: (0, 0)), pl.BlockSpec((1, n, cols), lambda l: (l, 0, 0)),
                  pl.BlockSpec((1, n, 3 * D_MODEL), lambda l: (l, 0, 0))],
        out_specs=[pl.BlockSpec((1, D_MODEL, cols), lambda l: (l, 0, 0)),
                   pl.BlockSpec((1, 1, 3 * D_MODEL), lambda l: (l, 0, 0))],
        out_shape=[jax.ShapeDtypeStruct((DEPTH, D_MODEL, cols), F32), jax.ShapeDtypeStruct((DEPTH, 1, 3 * D_MODEL), F32)],
        compiler_params=_params(("parallel",)),
    )(c_all, dmod_cols, dmod_all)


def _sum_blocks(parts, name):
    n, rows, cols = parts.shape
    tr = rows if rows <= 256 else 8 * next(d for d in range(32, 0, -1) if (rows // 8) % d == 0)

    def body(p_ref, out_ref):
        acc = p_ref[0].astype(F32)
        for k in range(1, n):
            acc = acc + p_ref[k].astype(F32)
        out_ref[...] = acc

    return pl.pallas_call(
        body, name=name, grid=(rows // tr,), in_specs=[pl.BlockSpec((n, tr, cols), lambda i: (0, i, 0))],
        out_specs=pl.BlockSpec((tr, cols), lambda i: (i, 0)), out_shape=jax.ShapeDtypeStruct((rows, cols), F32),
        compiler_params=_params(("parallel",)),
    )(parts)


def _adamw(w, g, m, v, name):
    nl, rows, cols = w.shape
    if rows * cols <= 64 * 1024:
        tr = rows
        tl = next(t for t in range(nl, 0, -1) if nl % t == 0 and t * max(rows, 8) * cols <= 512 * 1024)
    else:
        tl = 1
        tr = next(t for t in (rows, 256, 128, 64, 32, 16, 8) if rows % t == 0 and t * cols <= 256 * 1024)

    def body(w_ref, g_ref, m_ref, v_ref, d_ref, mo_ref, vo_ref):
        gv = g_ref[...]
        mn = ADAM_B1 * m_ref[...] + (1.0 - ADAM_B1) * gv
        vn = ADAM_B2 * v_ref[...] + (1.0 - ADAM_B2) * jnp.square(gv)
        m_hat = mn / (1.0 - ADAM_B1 ** ADAM_STEP)
        v_hat = vn / (1.0 - ADAM_B2 ** ADAM_STEP)
        d_ref[...] = -ADAM_LR * (m_hat / (jnp.sqrt(v_hat) + ADAM_EPS) + ADAM_WD * w_ref[...])
        mo_ref[...] = mn
        vo_ref[...] = vn

    spec = pl.BlockSpec((tl, tr, cols), lambda l, i: (l, i, 0))
    return pl.pallas_call(
        body, name=name, grid=(nl // tl, rows // tr), in_specs=[spec] * 4, out_specs=[spec] * 3,
        out_shape=[jax.ShapeDtypeStruct((nl, rows, cols), F32)] * 3, compiler_params=_params(("parallel", "parallel")),
    )(w, g, m, v)


ALL_FLIPS = tuple(range(1, N_DEV))


def _gather_small(src, name):
    rows, cols = src.shape

    def body(src_ref, dst_ref, send_sems, recv_sems, local_sem):
        x, y, c = lax.axis_index("x"), lax.axis_index("y"), lax.axis_index("c")
        me = 4 * x + 2 * y + c

        def copy(i, f, slot):
            to = (1 - x if f & 4 else x, 1 - y if f & 2 else y, 1 - c if f & 1 else c)
            return pltpu.make_async_remote_copy(
                src_ref=src_ref, dst_ref=dst_ref.at[slot], send_sem=send_sems.at[i], recv_sem=recv_sems.at[i],
                device_id=to, device_id_type=pl.DeviceIdType.MESH)

        own = pltpu.make_async_copy(src_ref, dst_ref.at[me], local_sem)
        own.start()
        sends = [copy(i, f, me) for i, f in enumerate(ALL_FLIPS)]
        for cp in sends:
            cp.start()
        for i, f in enumerate(ALL_FLIPS):
            copy(i, f, me ^ f).wait_recv()
        for cp in sends:
            cp.wait_send()
        own.wait()

    nf = len(ALL_FLIPS)
    return pl.pallas_call(
        body, name=name, out_shape=jax.ShapeDtypeStruct((N_DEV, rows, cols), src.dtype),
        in_specs=[pl.BlockSpec(memory_space=pl.ANY)], out_specs=pl.BlockSpec(memory_space=pl.ANY),
        scratch_shapes=[pltpu.SemaphoreType.DMA((nf,)), pltpu.SemaphoreType.DMA((nf,)), pltpu.SemaphoreType.DMA],
    )(src)


def _transfer(name, srcs, dst_shapes, plan):
    n_arr = len(srcs)
    probe = plan(0, 0, 0)
    n_steps = len(probe)

    def body(*refs):
        src_refs, dst_refs = refs[:n_arr], refs[n_arr:2 * n_arr]
        send_sems, recv_sems, local_sems = refs[2 * n_arr:]
        x, y, c = lax.axis_index("x"), lax.axis_index("y"), lax.axis_index("c")
        steps = plan(x, y, c)

        def rows(ref, r0, n):
            return ref.at[:, pl.ds(r0, n), :]

        def arrival(t):
            a, _, _, n, _, f, _ = steps[t]
            return pltpu.make_async_remote_copy(
                src_ref=rows(dst_refs[a], 0, n), dst_ref=rows(dst_refs[a], 0, n), send_sem=send_sems.at[t],
                recv_sem=recv_sems.at[t], device_id=(x, y, c), device_id_type=pl.DeviceIdType.MESH)

        arrived, started = set(), []
        for t, (a, from_dst, sr, n, dr, f, after) in enumerate(steps):
            for u in after:
                if u not in arrived:
                    arrival(u).wait_recv()
                    arrived.add(u)
            src = rows(dst_refs[a] if from_dst else src_refs[a], sr, n)
            dst = rows(dst_refs[a], dr, n)
            if f == 0:
                cp = pltpu.make_async_copy(src, dst, local_sems.at[t])
            else:
                to = (1 - x if f & 4 else x, 1 - y if f & 2 else y, 1 - c if f & 1 else c)
                cp = pltpu.make_async_remote_copy(src_ref=src, dst_ref=dst, send_sem=send_sems.at[t],
                                                  recv_sem=recv_sems.at[t], device_id=to,
                                                  device_id_type=pl.DeviceIdType.MESH)
            cp.start()
            started.append(cp)
        for t, step in enumerate(steps):
            if step[5] != 0 and t not in arrived:
                arrival(t).wait_recv()
        for cp, step in zip(started, steps):
            if step[5] == 0:
                cp.wait()
            else:
                cp.wait_send()

    any_spec = pl.BlockSpec(memory_space=pl.ANY)
    return pl.pallas_call(
        body, name=name, out_shape=[jax.ShapeDtypeStruct(s, d) for s, d in dst_shapes],
        in_specs=[any_spec] * n_arr, out_specs=[any_spec] * n_arr,
        scratch_shapes=[pltpu.SemaphoreType.DMA((n_steps,)), pltpu.SemaphoreType.DMA((n_steps,)),
                        pltpu.SemaphoreType.DMA((n_steps,))],
    )(*srcs)


CHIP_FLIPS = (2, 4, 6)


def _gather_plan(chip_rows):
    def plan(x, y, c):
        steps = []
        chip = 2 * x + y
        for a, rc in enumerate(chip_rows):
            h = rc // 2
            first = (h // 32) * 16
            mine = rc * chip + h * c
            from_x, from_y, diag = rc * (chip ^ 2) + h * c, rc * (chip ^ 1) + h * c, rc * (chip ^ 3) + h * c
            steps.append((a, False, h * c, h, mine, 0, ()))
            to_x = len(steps)
            steps.append((a, False, h * c, h, mine, 4, ()))
            to_y = len(steps)
            steps.append((a, False, h * c, h, mine, 2, ()))
            fwd_y = len(steps)
            steps.append((a, True, from_x, first, from_x, 2, (to_x,)))
            fwd_x = len(steps)
            steps.append((a, True, from_y + first, h - first, from_y + first, 4, (to_y,)))
            steps.append((a, False, h * c, h, mine, 1, ()))
            steps.append((a, True, from_x, h, from_x, 1, (to_x,)))
            steps.append((a, True, from_y, h, from_y, 1, (to_y,)))
            steps.append((a, True, diag, first, diag, 1, (fwd_y,)))
            steps.append((a, True, diag + first, h - first, diag + first, 1, (fwd_x,)))
        return steps
    return plan


def _pair_reduce_plan(chip_rows):
    def plan(x, y, c):
        steps = []
        for a, rc in enumerate(chip_rows):
            h = rc // 2
            for j in range(4):
                steps.append((a, False, rc * j + h * (1 - c), h, h * j, 1, ()))
        return steps
    return plan


def _chip_scatter_plan(chip_rows):
    def plan(x, y, c):
        steps = []
        for a, rc in enumerate(chip_rows):
            h = rc // 2
            mine = h * (2 * x + y)
            arrivals = []
            for k, f in enumerate(CHIP_FLIPS):
                arrivals.append(len(steps))
                steps.append((a, False, h * ((2 * x + y) ^ (f >> 1)), h, h * k, f, ()))
            steps.append((a, False, mine, h, 3 * h, 0, ()))
            steps.append((a, False, mine, h, 7 * h, 1, ()))
            for k, t in enumerate(arrivals):
                steps.append((a, True, h * k, h, (4 + k) * h, 1, (t,)))
        return steps
    return plan


def _tile_rows(h):
    return h


def _sum_pair(partial, recv, core, rc, name):
    nl, _, cols = partial.shape
    h = rc // 2
    tr = _tile_rows(h)

    def body(c_ref, p_ref, r_ref, out_ref):
        out_ref[...] = (p_ref[...] + r_ref[...]).astype(BF16)

    spec = pl.BlockSpec((1, tr, cols), lambda l, j, i, c_ref: (l, (h // tr) * j + i, 0))
    return pl.pallas_call(
        body, name=name, out_shape=jax.ShapeDtypeStruct((nl, 4 * h, cols), BF16),
        grid_spec=pltpu.PrefetchScalarGridSpec(
            num_scalar_prefetch=1, grid=(nl, 4, h // tr),
            in_specs=[pl.BlockSpec((1, tr, cols), lambda l, j, i, c_ref: (l, (rc // tr) * j + (h // tr) * c_ref[0] + i, 0)),
                      spec],
            out_specs=spec),
        compiler_params=_params(("parallel", "parallel", "parallel")),
    )(core, partial, recv)


def _sum_chips(slots, core, rc, name):
    nl, _, cols = slots.shape
    h = rc // 2

    def body(c_ref, s_ref, out_ref):
        acc = s_ref[0, 3].astype(F32)
        for k in range(3):
            acc = acc + s_ref[0, k].astype(F32)
        out_ref[0] = acc

    return pl.pallas_call(
        body, name=name, out_shape=jax.ShapeDtypeStruct((nl, rc, cols), F32),
        grid_spec=pltpu.PrefetchScalarGridSpec(
            num_scalar_prefetch=1, grid=(nl, 2),
            in_specs=[pl.BlockSpec((1, 4, h, cols), lambda l, q, c_ref: (l, q, 0, 0))],
            out_specs=pl.BlockSpec((1, h, cols), lambda l, q, c_ref: (l, q + c_ref[0] - 2 * q * c_ref[0], 0))),
        compiler_params=_params(("parallel", "parallel")),
    )(core, slots.reshape(nl, 8, h, cols))


def _pad_cols(a, n):
    return a if n == 0 else jnp.pad(a, ((0, 0), (0, n)))


def _in_to_padded(w):
    return jnp.concatenate([_pad_cols(w[:, a:b], z) for a, b, z in IN_PIECES], axis=1)


def _in_cols_first(ga, gb):
    split = Z_G[1]
    pos, out = 0, {}
    for a, b, z in IN_PIECES:
        src, off = (ga, pos) if pos < split else (gb, pos - split)
        out[a] = jnp.transpose(src[:, :, off:off + (b - a)], (2, 0, 1))
        pos += (b - a) + z
    return jnp.concatenate([out[a] for a in sorted(out)], axis=0)


def _out_to_padded(w):
    z = jnp.zeros((64, w.shape[1]), w.dtype)
    return jnp.concatenate([w[0:384], w[384:704], z, w[704:1024], z], axis=0)


def _out_from_padded(gp):
    return jnp.concatenate([gp[0:384], gp[384:704], gp[768:1088]], axis=0)


def _uq_to_padded(w):
    parts = []
    for p in range(3):
        h0, h1 = 2 * p, 2 * p + 1
        parts += [w[:, 96 * h0:96 * h0 + 64], w[:, 96 * h1:96 * h1 + 64], w[:, 96 * h0 + 64:96 * h0 + 96],
                  w[:, 96 * h1 + 64:96 * h1 + 96], jnp.zeros((w.shape[0], 64), w.dtype)]
    return jnp.concatenate(parts, axis=1)


def _uq_from_padded(gp):
    parts = []
    for h in range(6):
        p, s = h // 2, h % 2
        parts += [gp[:, 256 * p + 64 * s:256 * p + 64 * s + 64], gp[:, 256 * p + 128 + 32 * s:256 * p + 160 + 32 * s]]
    return jnp.concatenate(parts, axis=1)


def _ukv_to_padded(w):
    return jnp.concatenate([w[:, 128 * h:128 * h + 64] for h in range(6)]
                           + [w[:, 128 * h + 64:128 * h + 128] for h in range(6)], axis=1)


def _ukv_from_padded(gp):
    parts = []
    for h in range(6):
        parts += [gp[:, 64 * h:64 * h + 64], gp[:, 384 + 64 * h:384 + 64 * h + 64]]
    return jnp.concatenate(parts, axis=1)


LR_ROWS = 224
SHARD_ROWS = (256, 256, 256, LR_ROWS)


def _pack_lowrank(w_uq, w_ukv):
    flat = jnp.concatenate([w_uq.reshape(-1), w_ukv.reshape(-1)])
    return jnp.pad(flat, (0, LR_ROWS * PACK_COLS - flat.shape[0])).reshape(1, LR_ROWS, PACK_COLS)


def _unpack_lowrank(packed):
    flat = packed.reshape(-1)
    n_uq = DEPTH * A_Q_RANK * 144
    n_ukv = DEPTH * A_KV_RANK * 192
    return flat[0:n_uq].reshape(DEPTH, A_Q_RANK, 144), flat[n_uq:n_uq + n_ukv].reshape(DEPTH, A_KV_RANK, 192)


def _rope_tables(positions):
    t = positions.size
    inv = ROPE_THETA ** (-jnp.arange(0, A_ROPE, 2, dtype=F32) / A_ROPE)
    inv_row = jnp.pad(jnp.tile(inv, 4), (0, 64)).reshape(1, LANE)

    def body(p_ref, i_ref, c_ref, sa_ref, sb_ref):
        ang = p_ref[...].astype(F32) * i_ref[...]
        lane = _lane_iota((TM, LANE))
        live = lane < 64
        second = (lane & 31) >= 16
        s = jnp.sin(ang)
        c_ref[...] = jnp.where(live, jnp.cos(ang), 0.0)
        sa_ref[...] = jnp.where(live & jnp.logical_not(second), -s, 0.0)
        sb_ref[...] = jnp.where(live & second, s, 0.0)

    return pl.pallas_call(
        body, name="rope_tables", grid=(t // TM,), in_specs=[_row_spec(1), _full_spec((1, LANE))],
        out_specs=[_row_spec(LANE)] * 3, out_shape=[jax.ShapeDtypeStruct((t, LANE), F32)] * 3,
        compiler_params=_params(("parallel",)),
    )(positions.reshape(t, 1), inv_row)


def _rows(a, n):
    flat = a.reshape(-1)
    return jnp.pad(flat, (0, n * LANE - flat.shape[0])).reshape(n, LANE)


def _forward_backward(x, mod, tables, target, weights, small, nb, seq):
    cos, sina, sinb = tables
    saved = []
    for l in range(DEPTH):
        w, s = weights[l], small[l]
        shift = mod[l][:, None, 0:D_MODEL]
        scale = mod[l][:, None, D_MODEL:2 * D_MODEL]
        gate = mod[l][:, None, 2 * D_MODEL:]
        h, za, zg, qb, kb, vb, qc, kc, vc, zf = _ln_inproj(x, shift, scale, s["norm_g"], w["in"], seq)
        qa, ka, va = _a_up(za, s["gq"], s["gkv"], w["uq"], w["ukv"], cos, sina, sinb)
        bias = _bias_expand(s["g8"])[0:6]
        f = _forget_fwd(zf, s["fb"], nb, seq)
        frow = jnp.pad(f[:, 0:5].reshape(nb, seq, 5).transpose(0, 2, 1), ((0, 0), (0, 1), (0, 0)))
        frow = frow.reshape(nb, 6, 1, seq)
        oa, lse_a = _attn_fwd("A", qa, ka, va, None, nb, seq)
        ob, lse_b = _attn_fwd("B", qb, kb, vb, bias, nb, seq)
        oc, lse_c = _attn_fwd("C", qc, kc, vc, frow, nb, seq)
        y, xn = _gate_outproj(x, gate, oa, ob, oc, zg, w["out"], seq)
        saved.append(dict(x=x, h=h, za=za, zg=zg, zf=zf, y=y, shift=shift, scale=scale, gate=gate, bias=bias, frow=frow,
                          a=(qa, ka, va, oa, lse_a), b=(qb, kb, vb, ob, lse_b), c=(qc, kc, vc, oc, lse_c)))
        x = xn
    dx, loss8, gfinal8 = _final_loss(x, target, small[0]["final_g"])
    grads = []
    gw_in = (None, None)
    for l in reversed(range(DEPTH)):
        w, s, sv = weights[l], small[l], saved[l]
        qa, ka, va, oa, lse_a = sv["a"]
        qb, kb, vb, ob, lse_b = sv["b"]
        qc, kc, vc, oc, lse_c = sv["c"]
        doa, dob, doc, dzg, gw_out, dgate = _outproj_bwd(dx, sv["y"], sv["gate"], oa, ob, oc, sv["zg"], w["out"],
                                                          w["out_t"], nb, seq)
        dqa, dka, dva = _attn_bwd("A", qa, ka, va, oa, doa, lse_a, None, nb, seq)
        dqb, dkb, dvb, dbt = _attn_bwd("B", qb, kb, vb, ob, dob, lse_b, sv["bias"], nb, seq)
        dqc, dkc, dvc, dfr, dfq = _attn_bwd("C", qc, kc, vc, oc, doc, lse_c, sv["frow"], nb, seq)
        dg = _bias_reduce(jnp.pad(dbt, ((0, 2), (0, 0), (0, 0))))
        grb = jnp.pad(_bias_unline(dg), ((0, 0), (0, 384 - N_REL)))
        dfk = dfr.reshape(nb, 6, seq).transpose(0, 2, 1).reshape(nb * seq, 6)
        dzf, gfb = _forget_bwd(dfq, jnp.pad(dfk, ((0, 0), (0, LANE - 6))), sv["zf"], s["fb"], nb, seq)
        dza, gw_uq, gw_ukv, ggq, ggkv = _a_up_bwd(dqa, dka, dva, sv["za"], s["gq"], s["gkv"], w["uq_t"], w["ukv_t"],
                                                  cos, sina, sinb)
        dz = (dza, dzg, dqb, dkb, dvb, dqc, dkc, dvc, dzf)
        dx, dshift, dscale, gnorm = _inproj_bwd_dx(dz, dx, sv["x"], sv["shift"], sv["scale"], s["norm_g"], w["in"],
                                                   nb, seq)
        gw_in = (_inproj_bwd_dw(sv["h"], dz[0:2], "inproj_bwd_dw0", l, gw_in[0]),
                 _inproj_bwd_dw(sv["h"], dz[2:], "inproj_bwd_dw1", l, gw_in[1]))
        dmod = jnp.concatenate([dshift[:, 0], dscale[:, 0], dgate[:, 0]], axis=1)
        grads.append(dict(w_out=gw_out, w_uq=gw_uq, w_ukv=gw_ukv, dmod=dmod, norm_g=gnorm[0], gq=ggq[0],
                          gkv=ggkv[0], rb8=grb, fb=gfb[0]))
    grads.reverse()
    return loss8[0, 0], dx, grads, gfinal8[0], gw_in


def _layer_weights(w_in, w_out, w_uq, w_ukv):
    wi, wo, wq, wkv = _in_to_padded(w_in), _out_to_padded(w_out), _uq_to_padded(w_uq), _ukv_to_padded(w_ukv)
    return {"in": wi, "out": wo, "out_t": wo.T, "uq": wq, "uq_t": wq.T, "ukv": wkv, "ukv_t": wkv.T}


def _layer_small(norm_g, gq, gkv, rel_bias, forget_b, final_g):
    fb = jnp.pad(forget_b, (0, LANE - 5)).reshape(1, LANE)
    return dict(norm_g=norm_g.reshape(1, -1), gq=gq.reshape(1, -1), gkv=gkv.reshape(1, -1), g8=_bias_line(rel_bias), fb=fb,
                final_g=final_g.reshape(1, -1))


def _small_payload(per_layer, final_g, loss):
    def stack(key):
        return jnp.stack([p[key] for p in per_layer])

    def rows(a, rng):
        return _rows(a, rng[1] - rng[0])

    parts = [rows(stack("dmod"), PAY_DMOD), rows(stack("norm_g"), PAY_NORM), rows(stack("gq"), PAY_GQ),
             rows(stack("gkv"), PAY_GKV), rows(stack("rb8"), PAY_RB), rows(stack("fb"), PAY_FB),
             rows(final_g, PAY_FINAL), rows(loss, PAY_LOSS)]
    return jnp.concatenate(parts, axis=0)


def _payload_split(pay):
    def take(rng, shape):
        n = 1
        for d in shape:
            n *= d
        return pay[rng[0]:rng[1]].reshape(-1)[0:n].reshape(shape)

    norm_g = take(PAY_NORM, (DEPTH, D_MODEL))
    gq = take(PAY_GQ, (DEPTH, A_Q_RANK))
    gkv = take(PAY_GKV, (DEPTH, A_KV_RANK))
    rb = take(PAY_RB, (DEPTH, 8, 384))[:, 0:5, 0:N_REL]
    fb = take(PAY_FB, (DEPTH, LANE))[:, 0:5]
    final_g = take(PAY_FINAL, (D_MODEL,))
    return norm_g, gq, gkv, rb, fb, final_g


def kernel(x, c, positions, w_ada, b_ada, norm_g, w_in, a_q_norm_g, a_w_uq, a_kv_norm_g, a_w_ukv, b_rel_bias, c_forget_b, w_out, final_g, loss_target, m_w_ada, m_b_ada, m_norm_g, m_w_in, m_a_q_norm_g, m_a_w_uq, m_a_kv_norm_g, m_a_w_ukv, m_b_rel_bias, m_c_forget_b, m_w_out, m_final_g, v_w_ada, v_b_ada, v_norm_g, v_w_in, v_a_q_norm_g, v_a_w_uq, v_a_kv_norm_g, v_a_w_ukv, v_b_rel_bias, v_c_forget_b, v_w_out, v_final_g):
    nb, seq, _ = x.shape
    ix, iy, ic = lax.axis_index("x"), lax.axis_index("y"), lax.axis_index("c")
    chip = 2 * ix + iy
    me = 2 * chip + ic

    weight_plan = _gather_plan((256, 256, LR_ROWS))

    def gather_plan(x, y, c_):
        me_ = 4 * x + 2 * y + c_
        return weight_plan(x, y, c_) + [(3, False, 0, 8, 8 * me_, f, ()) for f in range(N_DEV)]

    full_in, full_out, full_lr, c_rows = _transfer(
        "gather_weights", [w_in.astype(BF16), w_out.astype(BF16), _pack_lowrank(a_w_uq, a_w_ukv).astype(BF16),
                           jnp.pad(c, ((0, 8 - nb), (0, 0)))[None]],
        [((DEPTH, D_MODEL, N_IN), BF16), ((DEPTH, D_MODEL, D_MODEL), BF16), ((1, 4 * LR_ROWS, PACK_COLS), BF16),
         ((1, 8 * N_DEV, D_MODEL), F32)], gather_plan)
    lowrank = [_unpack_lowrank(full_lr[0, LR_ROWS * j:LR_ROWS * (j + 1)]) for j in range(4)]
    full_uq = jnp.concatenate([s[0] for s in lowrank], axis=2)
    full_ukv = jnp.concatenate([s[1] for s in lowrank], axis=2)
    weights = [_layer_weights(full_in[l], full_out[l], full_uq[l], full_ukv[l]) for l in range(DEPTH)]
    small = [_layer_small(norm_g[l], a_q_norm_g[l], a_kv_norm_g[l], b_rel_bias[l], c_forget_b[l], final_g)
             for l in range(DEPTH)]

    c_all = c_rows.reshape(N_DEV, 8, D_MODEL)[:, 0:nb].reshape(N_DEV * nb, D_MODEL)
    cols = w_ada.shape[2]
    b_cols = lax.dynamic_slice_in_dim(b_ada, chip * cols, cols, axis=1)[:, None, :]
    mod_cols = _ada_fwd(c_all, w_ada, b_cols)
    mod_g = _gather_small(mod_cols.reshape(DEPTH * N_DEV * nb, cols), "gather_mod")
    mod_all = jnp.concatenate([mod_g[2 * j].reshape(DEPTH, N_DEV * nb, cols) for j in range(4)], axis=2)
    mod = lax.dynamic_slice_in_dim(mod_all, me * nb, nb, axis=1)

    tables = _rope_tables(positions)
    loss_part, dx, grads, gfinal, gw_in = _forward_backward(
        x.reshape(nb * seq, D_MODEL), mod, tables, loss_target.reshape(nb * seq, D_MODEL), weights, small, nb, seq)

    g_uq = jnp.stack([_uq_from_padded(g["w_uq"]) for g in grads])
    g_ukv = jnp.stack([_ukv_from_padded(g["w_ukv"]) for g in grads])
    g_lr = jnp.concatenate([_pack_lowrank(g_uq[:, :, 144 * j:144 * (j + 1)], g_ukv[:, :, 192 * j:192 * (j + 1)])
                            for j in range(4)], axis=1)
    partials = [gw_in[0], gw_in[1], jnp.stack([_out_from_padded(g["w_out"]) for g in grads]), g_lr]
    shapes = [(p.shape[0], p.shape[2]) for p in partials]
    halves = [r // 2 for r in SHARD_ROWS]
    core_s = jnp.reshape(ic, (1,)).astype(jnp.int32)
    pay = _small_payload(grads, gfinal, loss_part)
    pay_rows = pay.shape[0]
    reduce_plan = _pair_reduce_plan(SHARD_ROWS)

    def reduce_plan_with_payload(x, y, c_):
        me_ = 4 * x + 2 * y + c_
        return reduce_plan(x, y, c_) + [(len(partials), False, 0, pay_rows, pay_rows * me_, f, ()) for f in range(N_DEV)]

    *from_pair, pay_all = _transfer(
        "pair_reduce", partials + [pay[None]],
        [((nl, 4 * h, nc), F32) for (nl, nc), h in zip(shapes, halves)] + [((1, N_DEV * pay_rows, LANE), F32)],
        reduce_plan_with_payload)
    pay_all = pay_all.reshape(N_DEV, pay_rows, LANE)

    tot = _sum_blocks(pay_all, "sum_small")
    loss = tot[PAY_LOSS[0], 0]
    dmod_all = pay_all[:, PAY_DMOD[0]:PAY_DMOD[1]].reshape(N_DEV, -1)[:, 0:DEPTH * nb * 3 * D_MODEL]
    dmod_all = dmod_all.reshape(N_DEV, DEPTH, nb, 3 * D_MODEL).transpose(1, 0, 2, 3)
    dmod_all = dmod_all.reshape(DEPTH, N_DEV * nb, 3 * D_MODEL)
    my_cols = lax.dynamic_slice_in_dim(dmod_all, chip * cols, cols, axis=2)
    g_w_ada, g_b_ada = _ada_bwd(c_all, my_cols, dmod_all)
    g_b_ada = g_b_ada[:, 0]
    chip_sums = [_sum_pair(p, r, core_s, rc, "sum_pair%d" % i)
                 for i, (p, r, rc) in enumerate(zip(partials, from_pair, SHARD_ROWS))]
    slots = _transfer("chip_scatter", chip_sums, [((nl, 8 * h, nc), BF16) for (nl, nc), h in zip(shapes, halves)],
                      _chip_scatter_plan(SHARD_ROWS))
    g_in_a, g_in_b, g_out_sh, g_lr_sh = [_sum_chips(s, core_s, rc, "sum_chips%d" % i)
                                         for i, (s, rc) in enumerate(zip(slots, SHARD_ROWS))]
    g_uq_sh, g_ukv_sh = _unpack_lowrank(g_lr_sh[0])

    def cols_first(a):
        return jnp.transpose(a, (2, 0, 1))

    g_in_t = _in_cols_first(g_in_a, g_in_b)
    upd_in = tuple(jnp.transpose(a, (1, 2, 0)) for a in _adamw(cols_first(w_in), g_in_t, cols_first(m_w_in),
                                                               cols_first(v_w_in), "adamw_in"))
    gw = (jnp.transpose(g_in_t, (1, 2, 0)), g_out_sh, g_uq_sh, g_ukv_sh)
    upd = [upd_in, _adamw(w_out, gw[1], m_w_out, v_w_out, "adamw_out"),
           _adamw(a_w_uq, gw[2], m_a_w_uq, v_a_w_uq, "adamw_uq"), _adamw(a_w_ukv, gw[3], m_a_w_ukv, v_a_w_ukv, "adamw_ukv")]
    dw, mw, vw = (tuple(u[i] for u in upd) for i in range(3))
    d_ada, m_ada, v_ada = _adamw(w_ada, g_w_ada, m_w_ada, v_w_ada, "adamw_ada")

    def adam_small(w, g, m, v, name):
        shape3 = (1,) * (3 - w.ndim) + w.shape
        return tuple(a.reshape(w.shape) for a in _adamw(w.reshape(shape3), g.reshape(shape3), m.reshape(shape3),
                                                        v.reshape(shape3), name))

    d_b, m_b, v_b = adam_small(b_ada, g_b_ada, m_b_ada, v_b_ada, "adamw_b_ada")
    gs = _payload_split(tot)
    small_upd = [adam_small(w, g, m, v, "adamw_small%d" % i) for i, (w, g, m, v) in enumerate(zip(
        (norm_g, a_q_norm_g, a_kv_norm_g, b_rel_bias, c_forget_b, final_g), gs,
        (m_norm_g, m_a_q_norm_g, m_a_kv_norm_g, m_b_rel_bias, m_c_forget_b, m_final_g),
        (v_norm_g, v_a_q_norm_g, v_a_kv_norm_g, v_b_rel_bias, v_c_forget_b, v_final_g)))]
    ds, ms, vs = (tuple(u[i] for u in small_upd) for i in range(3))

    def ordered(ada, b, sm, big):
        ng, gq, gkv, rb, fb, fg = sm
        b_in, b_out, b_uq, b_ukv = big
        return (ada, b, ng, b_in, gq, b_uq, gkv, b_ukv, rb, fb, b_out, fg)

    return (loss, dx.reshape(nb, seq, D_MODEL), *ordered(g_w_ada, g_b_ada, gs, gw), *ordered(d_ada, d_b, ds, dw),
            *ordered(m_ada, m_b, ms, mw), *ordered(v_ada, v_b, vs, vw))
```

```python
import functools

import jax
import jax.numpy as jnp
from jax import lax
from jax.experimental import pallas as pl
from jax.experimental.pallas import tpu as pltpu

F32 = jnp.float32
BF16 = jnp.bfloat16

D_MODEL = 1024
DEPTH = 2
EPS = 1e-6
NEG = -1e30
LOG2E = 1.4426950408889634
ROPE_THETA = 10000.0
A_ROPE = 32
A_Q_RANK = 384
A_KV_RANK = 256
REL_CLIP = 128
N_REL = 2 * REL_CLIP + 1
N_IN = 3621

ADAM_LR = 0.001
ADAM_B1 = 0.9
ADAM_B2 = 0.999
ADAM_EPS = 1e-08
ADAM_WD = 0.01
ADAM_STEP = 10

LANE = 128
VMEM_LIMIT = 56 * 1024 * 1024

NP_IN = 4352
Z_A = (0, 768)
Z_G = (768, 1920)
Z_QKV = tuple((1920 + 384 * i, 1920 + 384 * (i + 1)) for i in range(6))
Z_F = (4224, 4352)
IN_PIECES = ((0, 672, 96), (672, 1056, 0), (2016, 2336, 64), (3301, 3621, 64), (1056, 1376, 64), (1376, 1696, 64),
             (1696, 2016, 64), (2336, 2656, 64), (2656, 2976, 64), (2976, 3296, 64), (3296, 3301, 123))
D_CAT = 1152

TM = 512
T_CAUSAL = 256
T_BAND = 128
BAND_TILES = 5
N_DEV = 8

PAY_DMOD = (0, 96)
PAY_NORM = (96, 112)
PAY_GQ = (112, 120)
PAY_GKV = (120, 128)
PAY_RB = (128, 176)
PAY_FB = (176, 184)
PAY_FINAL = (184, 192)
PAY_LOSS = (192, 200)

PACK_COLS = 1024


def _params(sem=None):
    return pltpu.CompilerParams(dimension_semantics=sem, vmem_limit_bytes=VMEM_LIMIT)


def _lane_iota(shape):
    return lax.broadcasted_iota(jnp.int32, shape, len(shape) - 1)


def _silu(u):
    return u * jax.nn.sigmoid(u)


def _dsilu(u):
    s = jax.nn.sigmoid(u)
    return s * (1.0 + u * (1.0 - s))


def _rms(x, g):
    r = lax.rsqrt(jnp.mean(x * x, axis=-1, keepdims=True) + EPS)
    xh = x * r
    return xh * g, xh, r


def _rms_bwd(dy, xh, r, g):
    dxh = dy * g
    return r * (dxh - xh * jnp.mean(dxh * xh, axis=-1, keepdims=True))


def _rope(x, cos, sina, sinb):
    return x * cos + pltpu.roll(x, 16, 1) * sinb + pltpu.roll(x, LANE - 16, 1) * sina


def _rope_t(dy, cos, sina, sinb):
    return dy * cos + pltpu.roll(dy * sinb, LANE - 16, 1) + pltpu.roll(dy * sina, 16, 1)


def _split3(x):
    hi = x.astype(BF16)
    r1 = x - hi.astype(F32)
    mid = r1.astype(BF16)
    lo = (r1 - mid.astype(F32)).astype(BF16)
    return hi, mid, lo


def _dot(a, b):
    return jnp.dot(a, b, preferred_element_type=F32)


def _dot_nt(a, b):
    return lax.dot_general(a, b, (((1,), (1,)), ((), ())), preferred_element_type=F32)


def _dot_tn(a, b):
    return lax.dot_general(a, b, (((0,), (0,)), ((), ())), preferred_element_type=F32)


def _row_spec(cols):
    return pl.BlockSpec((TM, cols), lambda i: (i, 0))


def _full_spec(shape):
    return pl.BlockSpec(shape, lambda i: (0,) * len(shape))


def _ex_spec(tiles_per_ex):
    return pl.BlockSpec((1, 1, D_MODEL), lambda i: (i // tiles_per_ex, 0, 0))


def _ln_inproj(x, shift, scale, g, w_in_p, seq):
    t = x.shape[0]

    def body(x_ref, sh_ref, sc_ref, g_ref, w_ref, h_ref, za_ref, zg_ref, q0, q1, q2, q3, q4, q5, zf_ref):
        n, _, _ = _rms(x_ref[...], g_ref[...])
        h = (n * (1.0 + sc_ref[0]) + sh_ref[0]).astype(BF16)
        h_ref[...] = h
        za_ref[...] = _dot(h, w_ref[:, Z_A[0]:Z_A[1]])
        zg_ref[...] = _dot(h, w_ref[:, Z_G[0]:Z_G[1]])
        for ref, (c0, c1) in zip((q0, q1, q2, q3, q4, q5), Z_QKV):
            ref[...] = _dot(h, w_ref[:, c0:c1]).astype(BF16)
        zf_ref[...] = _dot(h, w_ref[:, Z_F[0]:Z_F[1]])

    tpe = seq // TM
    shapes = [jax.ShapeDtypeStruct((t, D_MODEL), BF16), jax.ShapeDtypeStruct((t, 768), F32),
              jax.ShapeDtypeStruct((t, D_CAT), F32)]
    shapes += [jax.ShapeDtypeStruct((t, 384), BF16)] * 6 + [jax.ShapeDtypeStruct((t, LANE), F32)]
    return pl.pallas_call(
        body, name="ln_inproj", grid=(t // TM,),
        in_specs=[_row_spec(D_MODEL), _ex_spec(tpe), _ex_spec(tpe), _full_spec((1, D_MODEL)),
                  _full_spec((D_MODEL, NP_IN))],
        out_specs=[_row_spec(D_MODEL), _row_spec(768), _row_spec(D_CAT)] + [_row_spec(384)] * 6 + [_row_spec(LANE)],
        out_shape=shapes, compiler_params=_params(("parallel",)),
    )(x, shift, scale, g, w_in_p)


def _a_up(za, gq, gkv, w_uq_p, w_ukv_p, cos, sina, sinb):
    t = za.shape[0]

    def body(za_ref, gq_ref, gkv_ref, wq_ref, wkv_ref, cos_ref, sa_ref, sb_ref, q_ref, k_ref, v_ref):
        cos_t, sa, sb = cos_ref[...], sa_ref[...], sb_ref[...]
        cqn, _, _ = _rms(za_ref[:, 0:384], gq_ref[...])
        q = _dot(cqn.astype(BF16), wq_ref[...])
        ckvn, _, _ = _rms(za_ref[:, 384:640], gkv_ref[...])
        kv = _dot(ckvn.astype(BF16), wkv_ref[...])
        kpe = za_ref[:, 640:768]
        kpe = _rope(kpe + pltpu.roll(kpe, 32, 1), cos_t, sa, sb).astype(BF16)
        for p in range(3):
            q_ref[:, 256 * p:256 * p + 128] = q[:, 256 * p:256 * p + 128].astype(BF16)
            q_ref[:, 256 * p + 128:256 * p + 256] = _rope(q[:, 256 * p + 128:256 * p + 256], cos_t, sa, sb).astype(BF16)
            k_ref[:, 256 * p:256 * p + 128] = kv[:, 128 * p:128 * p + 128].astype(BF16)
            k_ref[:, 256 * p + 128:256 * p + 256] = kpe
        v_ref[...] = kv[:, 384:768].astype(BF16)

    return pl.pallas_call(
        body, name="a_up", grid=(t // TM,),
        in_specs=[_row_spec(768), _full_spec((1, 384)), _full_spec((1, 256)), _full_spec((384, 768)),
                  _full_spec((256, 768)), _row_spec(LANE), _row_spec(LANE), _row_spec(LANE)],
        out_specs=[_row_spec(768), _row_spec(768), _row_spec(384)],
        out_shape=[jax.ShapeDtypeStruct((t, 768), BF16), jax.ShapeDtypeStruct((t, 768), BF16),
                   jax.ShapeDtypeStruct((t, 384), BF16)],
        compiler_params=_params(("parallel",)),
    )(za, gq, gkv, w_uq_p, w_ukv_p, cos, sina, sinb)


def _tri(n, upper):
    r = lax.broadcasted_iota(jnp.int32, (n, n), 0)
    c = lax.broadcasted_iota(jnp.int32, (n, n), 1)
    return jnp.where((c >= r) if upper else (c <= r), 1.0, 0.0).astype(BF16)


def _forget_fwd(zf, fb, nb, seq):
    blk = 256

    def body(zf_ref, fb_ref, f_ref):
        tri = _tri(blk, False)
        live = _lane_iota((blk, LANE)) < 5
        carry = jnp.zeros((1, LANE), F32)
        for i in range(seq // blk):
            u = zf_ref[i * blk:(i + 1) * blk, :] + fb_ref[...]
            lf = jnp.where(live, jnp.minimum(u, 0.0) - jnp.log(1.0 + jnp.exp(-jnp.abs(u))), 0.0)
            hi, mid, lo = _split3(lf)
            f_ref[i * blk:(i + 1) * blk, :] = (_dot(tri, hi) + _dot(tri, mid) + _dot(tri, lo) + carry) * LOG2E
            carry = carry + jnp.sum(lf, axis=0, keepdims=True)

    return pl.pallas_call(
        body, name="forget_fwd", grid=(nb,),
        in_specs=[pl.BlockSpec((seq, LANE), lambda b: (b, 0)), pl.BlockSpec((1, LANE), lambda b: (0, 0))],
        out_specs=pl.BlockSpec((seq, LANE), lambda b: (b, 0)),
        out_shape=jax.ShapeDtypeStruct((nb * seq, LANE), F32), compiler_params=_params(("parallel",)),
    )(zf, fb)


def _forget_bwd(dfq, dfk, zf, fb, nb, seq):
    blk = 256

    def body(dfq_ref, dfk_ref, zf_ref, fb_ref, dz_ref, gb_ref):
        @pl.when(pl.program_id(0) == 0)
        def _():
            gb_ref[...] = jnp.zeros_like(gb_ref)

        tri = _tri(blk, True)
        lane = _lane_iota((blk, LANE))
        wide = _lane_iota((blk, 384))
        live = lane < 5
        carry = jnp.zeros((1, LANE), F32)
        gsum = jnp.zeros((1, LANE), F32)
        for i in reversed(range(seq // blk)):
            d = dfk_ref[i * blk:(i + 1) * blk, :]
            dq = dfq_ref[i * blk:(i + 1) * blk, :]
            for hd in range(5):
                col = jnp.sum(jnp.where(wide == 64 * hd, dq, 0.0), axis=-1, keepdims=True)
                d = d + jnp.where(lane == hd, col, 0.0)
            hi, mid, lo = _split3(d)
            dlf = _dot(tri, hi) + _dot(tri, mid) + _dot(tri, lo) + carry
            carry = carry + jnp.sum(d, axis=0, keepdims=True)
            u = zf_ref[i * blk:(i + 1) * blk, :] + fb_ref[...]
            du = jnp.where(live, dlf * jax.nn.sigmoid(-u), 0.0)
            dz_ref[i * blk:(i + 1) * blk, :] = du.astype(BF16)
            gsum = gsum + jnp.sum(du, axis=0, keepdims=True)
        gb_ref[...] += jnp.broadcast_to(gsum, gb_ref.shape)

    return pl.pallas_call(
        body, name="forget_bwd", grid=(nb,),
        in_specs=[pl.BlockSpec((seq, 384), lambda b: (b, 0)), pl.BlockSpec((seq, LANE), lambda b: (b, 0)),
                  pl.BlockSpec((seq, LANE), lambda b: (b, 0)), pl.BlockSpec((1, LANE), lambda b: (0, 0))],
        out_specs=[pl.BlockSpec((seq, LANE), lambda b: (b, 0)), pl.BlockSpec((8, LANE), lambda b: (0, 0))],
        out_shape=[jax.ShapeDtypeStruct((nb * seq, LANE), BF16), jax.ShapeDtypeStruct((8, LANE), F32)],
        compiler_params=_params(("arbitrary",)),
    )(dfq, dfk, zf, fb)


def _gate_outproj(x, gate, oa, ob, oc, zg, w_out_p, seq):
    t = x.shape[0]

    def body(x_ref, gate_ref, oa_ref, ob_ref, oc_ref, zg_ref, w_ref, y_ref, xn_ref):
        y = jnp.zeros((TM, D_MODEL), F32)
        for i, o_ref in enumerate((oa_ref, ob_ref, oc_ref)):
            cat = (o_ref[...] * _silu(zg_ref[:, 384 * i:384 * (i + 1)])).astype(BF16)
            y = y + _dot(cat, w_ref[384 * i:384 * (i + 1), :])
        y_ref[...] = y
        xn_ref[...] = x_ref[...] + gate_ref[0] * y

    return pl.pallas_call(
        body, name="gate_outproj", grid=(t // TM,),
        in_specs=[_row_spec(D_MODEL), _ex_spec(seq // TM), _row_spec(384), _row_spec(384), _row_spec(384),
                  _row_spec(D_CAT), _full_spec((D_CAT, D_MODEL))],
        out_specs=[_row_spec(D_MODEL), _row_spec(D_MODEL)],
        out_shape=[jax.ShapeDtypeStruct((t, D_MODEL), F32)] * 2, compiler_params=_params(("parallel",)),
    )(x, gate, oa, ob, oc, zg, w_out_p)


def _final_loss(x, target, g):
    t = x.shape[0]

    def body(x_ref, t_ref, g_ref, dx_ref, loss_ref, gg_ref):
        @pl.when(pl.program_id(0) == 0)
        def _():
            loss_ref[...] = jnp.zeros_like(loss_ref)
            gg_ref[...] = jnp.zeros_like(gg_ref)

        gv = g_ref[...]
        out, xh, r = _rms(x_ref[...], gv)
        err = out - t_ref[...]
        loss_ref[...] += 0.5 * jnp.sum(jnp.mean(err * err, axis=-1, keepdims=True), axis=0, keepdims=True)
        dout = err / D_MODEL
        gg_ref[...] += jnp.broadcast_to(jnp.sum(dout * xh, axis=0, keepdims=True), gg_ref.shape)
        dx_ref[...] = _rms_bwd(dout, xh, r, gv)

    return pl.pallas_call(
        body, name="final_loss", grid=(t // TM,),
        in_specs=[_row_spec(D_MODEL), _row_spec(D_MODEL), _full_spec((1, D_MODEL))],
        out_specs=[_row_spec(D_MODEL), _full_spec((8, LANE)), _full_spec((8, D_MODEL))],
        out_shape=[jax.ShapeDtypeStruct((t, D_MODEL), F32), jax.ShapeDtypeStruct((8, LANE), F32),
                   jax.ShapeDtypeStruct((8, D_MODEL), F32)],
        compiler_params=_params(("arbitrary",)),
    )(x, target, g)


def _head_masks(kind, rows, dq, h):
    lq = _lane_iota((rows, dq))
    lv = _lane_iota((rows, LANE))
    mq = (lq >= 64 * h) & (lq < 64 * h + 64)
    if kind == "A":
        mq = mq | ((lq >= 128 + 32 * h) & (lq < 160 + 32 * h))
    return mq, (lv >= 64 * h) & (lv < 64 * h + 64)


def _tile_mask(kind, tile):
    row = lax.broadcasted_iota(jnp.int32, (tile, tile), 0)
    col = lax.broadcasted_iota(jnp.int32, (tile, tile), 1)
    return (col >> 6) <= (row >> 6) if kind == "A" else col <= row


def _attn_scale(kind):
    return 96.0 ** -0.5 if kind == "A" else 0.125


BAND_W = BAND_TILES * T_BAND


def _segments(kind, qi, tile):
    r0 = qi * tile
    if kind == "B":
        lo = max(qi - (BAND_TILES - 1), 0) * tile
        return [(lo, r0 + tile, False, lo - (qi - (BAND_TILES - 1)) * tile)]
    return ([(0, r0, False, 0)] if qi else []) + [(r0, r0 + tile, True, 0)]


def _scores(kind, qh, k_ref, aux_ref, h, seg, tile, scale):
    a, b, diag, c0 = seg
    s = _dot_nt(qh, k_ref[a:b, :]) * (scale * LOG2E)
    if kind == "B":
        return s + aux_ref[h, :, c0:BAND_W]
    if kind == "C":
        s = s - aux_ref[0, h, :, a:b]
    if diag:
        s = jnp.where(_tile_mask(kind, tile), s, NEG)
    return s


FWD_AHEAD = 2
BWD_AHEAD = 2


def _run_ahead(units, first, second, depth):
    queue = [first(*u) for u in units[:depth]]
    for i, (_, h) in enumerate(units):
        if i + depth < len(units):
            queue.append(first(*units[i + depth]))
        second(h, *queue.pop(0))


def _attn_fwd(kind, q, k, v, aux, nb, seq):
    dq = q.shape[1] // 3
    tile = T_BAND if kind == "B" else T_CAUSAL
    nq = seq // tile
    scale = _attn_scale(kind)

    def body(*refs):
        if kind == "A":
            q_ref, k_ref, v_ref, o_ref, lse_ref = refs
            aux_ref = None
        else:
            q_ref, k_ref, v_ref, aux_ref, o_ref, lse_ref = refs
        def logits(qi, h):
            rows = slice(qi * tile, (qi + 1) * tile)
            q2 = q_ref[rows, :]
            mq, _ = _head_masks(kind, tile, dq, h)
            qh = jnp.where(mq, q2, jnp.zeros_like(q2))
            segs = _segments(kind, qi, tile)
            return rows, segs, [_scores(kind, qh, k_ref, aux_ref, h, seg, tile, scale) for seg in segs]

        def finish(h, rows, segs, ss):
            _, mv = _head_masks(kind, tile, dq, h)
            mx = functools.reduce(jnp.maximum, [jnp.max(s, axis=-1, keepdims=True) for s in ss])
            ps = [jnp.exp2(s - mx) for s in ss]
            l = functools.reduce(jnp.add, [jnp.sum(p, axis=-1, keepdims=True) for p in ps])
            acc = functools.reduce(jnp.add, [_dot(p.astype(BF16), v_ref[seg[0]:seg[1], :]) for p, seg in zip(ps, segs)])
            o_h = jnp.where(mv, acc / l, 0.0)
            lse_h = jnp.where(mv, mx + jnp.log(l) * LOG2E, 0.0)
            if h == 0:
                o_ref[rows, :] = o_h
                lse_ref[rows, :] = lse_h
            else:
                o_ref[rows, :] += o_h
                lse_ref[rows, :] += lse_h

        _run_ahead([(qi, h) for qi in range(nq) for h in range(2)], logits, finish, FWD_AHEAD)

    def seq_spec(cols):
        return pl.BlockSpec((seq, cols), lambda b, p: (b, p))

    in_specs = [seq_spec(dq), seq_spec(dq), seq_spec(LANE)]
    args = [q, k, v]
    if kind == "B":
        in_specs.append(pl.BlockSpec((2, tile, BAND_W), lambda b, p: (p, 0, 0)))
        args.append(aux)
    if kind == "C":
        in_specs.append(pl.BlockSpec((1, 2, 1, seq), lambda b, p: (b, p, 0, 0)))
        args.append(aux)
    return pl.pallas_call(
        body, name="attn_fwd_" + kind, grid=(nb, 3), in_specs=in_specs, out_specs=[seq_spec(LANE), seq_spec(LANE)],
        out_shape=[jax.ShapeDtypeStruct((nb * seq, 384), F32)] * 2, compiler_params=_params(("parallel", "parallel")),
    )(*args)


def _attn_bwd(kind, q, k, v, o, do, lse, aux, nb, seq):
    dq = q.shape[1] // 3
    tile = T_BAND if kind == "B" else T_CAUSAL
    nq = seq // tile
    scale = _attn_scale(kind)
    dqk_dtype = F32 if kind == "A" else BF16

    def body(*refs):
        dfr_ref = dfq_ref = dbt_ref = aux_ref = None
        if kind == "A":
            q_ref, k_ref, v_ref, o_ref, do_ref, lse_ref, dq_ref, dk_ref, dv_ref, dkt_acc, dvt_acc = refs
        elif kind == "B":
            q_ref, k_ref, v_ref, o_ref, do_ref, lse_ref, aux_ref, dq_ref, dk_ref, dv_ref, dbt_ref, dkt_acc, dvt_acc = refs
        else:
            (q_ref, k_ref, v_ref, o_ref, do_ref, lse_ref, aux_ref, dq_ref, dk_ref, dv_ref, dfr_ref, dfq_ref,
             dkt_acc, dvt_acc) = refs
        dkt_acc[...] = jnp.zeros_like(dkt_acc)
        dvt_acc[...] = jnp.zeros_like(dvt_acc)
        if kind == "C":
            dfr_ref[...] = jnp.zeros_like(dfr_ref)
        if kind == "B":
            @pl.when(pl.program_id(1) == 0)
            def _():
                dbt_ref[...] = jnp.zeros_like(dbt_ref)

        def products(qi, h):
            rows = slice(qi * tile, (qi + 1) * tile)
            q2 = q_ref[rows, :]
            mq, mv = _head_masks(kind, tile, dq, h)
            qh = jnp.where(mq, q2, jnp.zeros_like(q2))
            doh = jnp.where(mv, do_ref[rows, :], 0.0)
            dob = doh.astype(BF16)
            segs = _segments(kind, qi, tile)
            ts = [_scores(kind, qh, k_ref, aux_ref, h, seg, tile, scale) for seg in segs]
            dps = [_dot_nt(dob, v_ref[seg[0]:seg[1], :]) for seg in segs]
            return rows, segs, qh, doh, ts, dps

        def finish(h, rows, segs, qh, doh, ts, dps):
            mq, mv = _head_masks(kind, tile, dq, h)
            qht = qh.astype(F32).T.astype(BF16)
            dobt = doh.T.astype(BF16)
            head_rows = [(64 * h, 64)] + ([(128 + 32 * h, 32)] if kind == "A" else [])
            delta = jnp.sum(doh * o_ref[rows, :], axis=-1, keepdims=True)
            lseh = jnp.max(jnp.where(mv, lse_ref[rows, :], NEG), axis=-1, keepdims=True)
            rs = jnp.zeros((tile, 1), F32)
            dq_h = jnp.zeros((tile, dq), F32)
            for (a, b, _, c0), t, dp in zip(segs, ts, dps):
                p = jnp.exp2(t - lseh)
                ds = p * (dp - delta)
                if kind == "B":
                    dbt_ref[h, :, c0:BAND_W] += ds
                if kind == "C":
                    dfr_ref[0, h, :, a:b] -= jnp.sum(ds, axis=0, keepdims=True)
                    rs = rs + jnp.sum(ds, axis=-1, keepdims=True)
                dss = (ds * scale).astype(BF16)
                dvt_acc[64 * h:64 * h + 64, a:b] += _dot(dobt[64 * h:64 * h + 64, :], p.astype(BF16))
                for r0, n in head_rows:
                    dkt_acc[r0:r0 + n, a:b] += _dot(qht[r0:r0 + n, :], dss)
                dq_h = dq_h + _dot(dss, k_ref[a:b, :])
            dq_h = jnp.where(mq, dq_h, 0.0).astype(dqk_dtype)
            if h == 0:
                dq_ref[rows, :] = dq_h
            else:
                dq_ref[rows, :] += dq_h
            if kind == "C":
                if h == 0:
                    dfq_ref[rows, :] = jnp.where(mv, rs, 0.0)
                else:
                    dfq_ref[rows, :] += jnp.where(mv, rs, 0.0)

        _run_ahead([(qi, h) for qi in range(nq) for h in range(2)], products, finish, BWD_AHEAD)
        for j in range(seq // 256):
            cols = slice(256 * j, 256 * (j + 1))
            dk_ref[cols, :] = dkt_acc[:, cols].T.astype(dqk_dtype)
            dv_ref[cols, :] = dvt_acc[:, cols].T.astype(BF16)

    def seq_spec(cols):
        return pl.BlockSpec((seq, cols), lambda p, b: (b, p))

    in_specs = [seq_spec(dq), seq_spec(dq), seq_spec(LANE), seq_spec(LANE), seq_spec(LANE), seq_spec(LANE)]
    args = [q, k, v, o, do, lse]
    out_specs = [seq_spec(dq), seq_spec(dq), seq_spec(LANE)]
    out_shape = [jax.ShapeDtypeStruct((nb * seq, 3 * dq), dqk_dtype)] * 2 + [jax.ShapeDtypeStruct((nb * seq, 384), BF16)]
    if kind == "B":
        spec = pl.BlockSpec((2, tile, BAND_W), lambda p, b: (p, 0, 0))
        in_specs.append(spec)
        args.append(aux)
        out_specs.append(spec)
        out_shape.append(jax.ShapeDtypeStruct((6, tile, BAND_W), F32))
    if kind == "C":
        spec = pl.BlockSpec((1, 2, 1, seq), lambda p, b: (b, p, 0, 0))
        in_specs.append(spec)
        args.append(aux)
        out_specs += [spec, seq_spec(LANE)]
        out_shape += [jax.ShapeDtypeStruct((nb, 6, 1, seq), F32), jax.ShapeDtypeStruct((nb * seq, 384), F32)]
    return pl.pallas_call(
        body, name="attn_bwd_" + kind, grid=(3, nb), in_specs=in_specs, out_specs=out_specs, out_shape=out_shape,
        scratch_shapes=[pltpu.VMEM((dq, seq), F32), pltpu.VMEM((LANE, seq), F32)],
        compiler_params=_params(("arbitrary", "arbitrary")),
    )(*args)


BIAS_G = 768
BIAS_EDGE = BIAS_G - N_REL


def _bias_line(rel_bias):
    g = jnp.concatenate([jnp.broadcast_to(rel_bias[:, N_REL - 1:], (rel_bias.shape[0], BIAS_EDGE)),
                         jnp.flip(rel_bias, axis=1)], axis=1)
    return jnp.pad(g, ((0, 8 - g.shape[0]), (0, 0)))


def _bias_unline(dg):
    return jnp.flip(dg[:, BIAS_EDGE:], axis=1)


def _bias_expand(g8):
    def body(g_ref, out_ref):
        line = jnp.broadcast_to(g_ref[0] * LOG2E, (T_BAND, BIAS_G))
        slab = pltpu.roll(line, 1, 1, stride=1, stride_axis=0)[:, LANE:BIAS_G]
        row = lax.broadcasted_iota(jnp.int32, (T_BAND, BAND_W), 0)
        col = lax.broadcasted_iota(jnp.int32, (T_BAND, BAND_W), 1)
        hidden = ((row >= 64) & (col < 64)) | ((row < 64) & (col >= BAND_W - 64))
        out_ref[0] = jnp.where(hidden, NEG, slab)

    return pl.pallas_call(
        body, name="bias_expand", grid=(8,), in_specs=[pl.BlockSpec((1, 1, BIAS_G), lambda h: (h, 0, 0))],
        out_specs=pl.BlockSpec((1, T_BAND, BAND_W), lambda h: (h, 0, 0)),
        out_shape=jax.ShapeDtypeStruct((8, T_BAND, BAND_W), F32), compiler_params=_params(("parallel",)),
    )(g8.reshape(8, 1, BIAS_G))


def _bias_reduce(d_slab):
    def body(d_ref, out_ref):
        r = lax.broadcasted_iota(jnp.int32, (T_BAND, T_BAND), 0)
        k = lax.broadcasted_iota(jnp.int32, (T_BAND, T_BAND), 1)
        flip = jnp.where(r + k == T_BAND - 1, 1.0, 0.0).astype(BF16)
        hi, mid, lo = _split3(d_ref[0])
        d_rev = _dot(flip, hi) + _dot(flip, mid) + _dot(flip, lo)
        wide = jnp.concatenate([jnp.zeros((T_BAND, LANE), F32), d_rev, jnp.zeros((T_BAND, 2 * LANE), F32)], axis=1)
        skew = pltpu.roll(wide, 0, 1, stride=1, stride_axis=0)
        dg = jnp.sum(skew, axis=0, keepdims=True)[:, LANE:LANE + BIAS_G]
        lane = _lane_iota((1, BIAS_G))
        clipped = jnp.sum(jnp.where(lane <= BIAS_EDGE, dg, 0.0), axis=1, keepdims=True)
        out_ref[0] = jnp.where(lane == BIAS_EDGE, clipped, dg)

    return pl.pallas_call(
        body, name="bias_reduce", grid=(8,), in_specs=[pl.BlockSpec((1, T_BAND, BAND_W), lambda h: (h, 0, 0))],
        out_specs=pl.BlockSpec((1, 1, BIAS_G), lambda h: (h, 0, 0)),
        out_shape=jax.ShapeDtypeStruct((8, 1, BIAS_G), F32), compiler_params=_params(("parallel",)),
    )(d_slab).reshape(8, BIAS_G)


def _outproj_bwd(dxn, y, gate, oa, ob, oc, zg, w_out_p, w_out_pt, nb, seq):
    t = dxn.shape[0]
    tpe = seq // TM

    def body(dxn_ref, y_ref, gate_ref, oa_ref, ob_ref, oc_ref, zg_ref, w_ref, wt_ref,
             doa_ref, dob_ref, doc_ref, dzg_ref, gw_ref, dgate_ref):
        i = pl.program_id(0)

        @pl.when(i == 0)
        def _():
            gw_ref[...] = jnp.zeros_like(gw_ref)

        @pl.when(i % tpe == 0)
        def _():
            dgate_ref[...] = jnp.zeros_like(dgate_ref)

        dxn_t = dxn_ref[...]
        dgate_ref[0] += jnp.sum(dxn_t * y_ref[...], axis=0, keepdims=True)
        dy = (dxn_t * gate_ref[0]).astype(BF16)
        for gi, (o_ref, do_ref) in enumerate(((oa_ref, doa_ref), (ob_ref, dob_ref), (oc_ref, doc_ref))):
            cols = slice(384 * gi, 384 * (gi + 1))
            u = zg_ref[:, cols]
            o_t = o_ref[...]
            su = _silu(u)
            dcat = _dot(dy, wt_ref[:, cols])
            do_ref[...] = dcat * su
            dzg_ref[:, cols] = (dcat * o_t * _dsilu(u)).astype(BF16)
            gw_ref[cols, :] += _dot_tn((o_t * su).astype(BF16), dy)

    return pl.pallas_call(
        body, name="outproj_bwd", grid=(t // TM,),
        in_specs=[_row_spec(D_MODEL), _row_spec(D_MODEL), _ex_spec(tpe), _row_spec(384), _row_spec(384), _row_spec(384),
                  _row_spec(D_CAT), _full_spec((D_CAT, D_MODEL)), _full_spec((D_MODEL, D_CAT))],
        out_specs=[_row_spec(384), _row_spec(384), _row_spec(384), _row_spec(D_CAT), _full_spec((D_CAT, D_MODEL)),
                   _ex_spec(tpe)],
        out_shape=[jax.ShapeDtypeStruct((t, 384), F32)] * 3 + [jax.ShapeDtypeStruct((t, D_CAT), BF16),
                                                                jax.ShapeDtypeStruct((D_CAT, D_MODEL), F32),
                                                                jax.ShapeDtypeStruct((nb, 1, D_MODEL), F32)],
        compiler_params=_params(("arbitrary",)),
    )(dxn, y, gate, oa, ob, oc, zg, w_out_p, w_out_pt)


def _a_up_bwd(dqa, dka, dva, za, gq, gkv, w_uq_pt, w_ukv_pt, cos, sina, sinb):
    t = za.shape[0]

    def body(dq_ref, dk_ref, dv_ref, za_ref, gq_ref, gkv_ref, wqt_ref, wkvt_ref, cos_ref, sa_ref, sb_ref,
             dza_ref, gwq_ref, gwkv_ref, ggq_ref, ggkv_ref, dqb, dkvb):
        @pl.when(pl.program_id(0) == 0)
        def _():
            gwq_ref[...] = jnp.zeros_like(gwq_ref)
            gwkv_ref[...] = jnp.zeros_like(gwkv_ref)
            ggq_ref[...] = jnp.zeros_like(ggq_ref)
            ggkv_ref[...] = jnp.zeros_like(ggkv_ref)

        cos_t, sa, sb = cos_ref[...], sa_ref[...], sb_ref[...]
        dkpe = jnp.zeros((TM, LANE), F32)
        for p in range(3):
            dqb[:, 256 * p:256 * p + 128] = dq_ref[:, 256 * p:256 * p + 128].astype(BF16)
            dqb[:, 256 * p + 128:256 * p + 256] = _rope_t(dq_ref[:, 256 * p + 128:256 * p + 256], cos_t, sa, sb).astype(BF16)
            dkvb[:, 128 * p:128 * p + 128] = dk_ref[:, 256 * p:256 * p + 128].astype(BF16)
            dkpe = dkpe + dk_ref[:, 256 * p + 128:256 * p + 256]
        dkvb[:, 384:768] = dv_ref[...]
        dkpe = _rope_t(dkpe, cos_t, sa, sb)
        dkpe = jnp.where(_lane_iota((TM, LANE)) < A_ROPE, dkpe + pltpu.roll(dkpe, LANE - 32, 1), 0.0)

        gqv = gq_ref[...]
        cqn, cqh, rq = _rms(za_ref[:, 0:384], gqv)
        dq_t = dqb[...]
        gwq_ref[...] += _dot_tn(cqn.astype(BF16), dq_t)
        dcqn = _dot(dq_t, wqt_ref[...])
        ggq_ref[...] += jnp.broadcast_to(jnp.sum(dcqn * cqh, axis=0, keepdims=True), ggq_ref.shape)
        dza_ref[:, 0:384] = _rms_bwd(dcqn, cqh, rq, gqv).astype(BF16)

        gkvv = gkv_ref[...]
        ckvn, ckvh, rkv = _rms(za_ref[:, 384:640], gkvv)
        dkv_t = dkvb[...]
        gwkv_ref[...] += _dot_tn(ckvn.astype(BF16), dkv_t)
        dckvn = _dot(dkv_t, wkvt_ref[...])
        ggkv_ref[...] += jnp.broadcast_to(jnp.sum(dckvn * ckvh, axis=0, keepdims=True), ggkv_ref.shape)
        dza_ref[:, 384:640] = _rms_bwd(dckvn, ckvh, rkv, gkvv).astype(BF16)
        dza_ref[:, 640:768] = dkpe.astype(BF16)

    return pl.pallas_call(
        body, name="a_up_bwd", grid=(t // TM,),
        in_specs=[_row_spec(768), _row_spec(768), _row_spec(384), _row_spec(768), _full_spec((1, 384)),
                  _full_spec((1, 256)), _full_spec((768, 384)), _full_spec((768, 256)), _row_spec(LANE), _row_spec(LANE),
                  _row_spec(LANE)],
        out_specs=[_row_spec(768), _full_spec((384, 768)), _full_spec((256, 768)), _full_spec((8, 384)),
                   _full_spec((8, 256))],
        out_shape=[jax.ShapeDtypeStruct((t, 768), BF16), jax.ShapeDtypeStruct((384, 768), F32),
                   jax.ShapeDtypeStruct((256, 768), F32), jax.ShapeDtypeStruct((8, 384), F32),
                   jax.ShapeDtypeStruct((8, 256), F32)],
        scratch_shapes=[pltpu.VMEM((TM, 768), BF16), pltpu.VMEM((TM, 768), BF16)],
        compiler_params=_params(("arbitrary",)),
    )(dqa, dka, dva, za, gq, gkv, w_uq_pt, w_ukv_pt, cos, sina, sinb)


def _dz_cols():
    return (Z_A, Z_G) + Z_QKV + (Z_F,)


def _inproj_bwd_dx(dz, dxn, x, shift, scale, g, w_in_pt, nb, seq):
    t = x.shape[0]
    tpe = seq // TM
    cols = _dz_cols()

    def body(*refs):
        dz_refs = refs[:len(cols)]
        dxn_ref, x_ref, sh_ref, sc_ref, g_ref, wt_ref, dx_ref, dsh_ref, dsc_ref, dg_ref = refs[len(cols):]
        i = pl.program_id(0)

        @pl.when(i == 0)
        def _():
            dg_ref[...] = jnp.zeros_like(dg_ref)

        @pl.when(i % tpe == 0)
        def _():
            dsh_ref[...] = jnp.zeros_like(dsh_ref)
            dsc_ref[...] = jnp.zeros_like(dsc_ref)

        dh = jnp.zeros((TM, D_MODEL), F32)
        for ref, (c0, c1) in zip(dz_refs, cols):
            dh = dh + _dot_nt(ref[...], wt_ref[:, c0:c1])
        gv = g_ref[...]
        n, xh, r = _rms(x_ref[...], gv)
        dsh_ref[0] += jnp.sum(dh, axis=0, keepdims=True)
        dsc_ref[0] += jnp.sum(dh * n, axis=0, keepdims=True)
        dn = dh * (1.0 + sc_ref[0])
        dg_ref[...] += jnp.broadcast_to(jnp.sum(dn * xh, axis=0, keepdims=True), dg_ref.shape)
        dx_ref[...] = dxn_ref[...] + _rms_bwd(dn, xh, r, gv)

    in_specs = [_row_spec(c1 - c0) for c0, c1 in cols]
    in_specs += [_row_spec(D_MODEL), _row_spec(D_MODEL), _ex_spec(tpe), _ex_spec(tpe), _full_spec((1, D_MODEL)),
                 _full_spec((D_MODEL, NP_IN))]
    return pl.pallas_call(
        body, name="inproj_bwd_dx", grid=(t // TM,), in_specs=in_specs,
        out_specs=[_row_spec(D_MODEL), _ex_spec(tpe), _ex_spec(tpe), _full_spec((8, D_MODEL))],
        out_shape=[jax.ShapeDtypeStruct((t, D_MODEL), F32), jax.ShapeDtypeStruct((nb, 1, D_MODEL), F32),
                   jax.ShapeDtypeStruct((nb, 1, D_MODEL), F32), jax.ShapeDtypeStruct((8, D_MODEL), F32)],
        compiler_params=_params(("arbitrary",)),
    )(*dz, dxn, x, shift, scale, g, w_in_pt)


def _inproj_bwd_dw(h, dz, name, layer, both=None):
    t = h.shape[0]
    widths = [d.shape[1] for d in dz]
    total = sum(widths)

    def body(*refs):
        h_ref = refs[0]
        dz_refs = refs[1:1 + len(dz)]
        gw_ref = refs[-1]

        @pl.when(pl.program_id(0) == 0)
        def _():
            gw_ref[...] = jnp.zeros_like(gw_ref)

        h_t = h_ref[...]
        c0 = 0
        for ref, w in zip(dz_refs, widths):
            gw_ref[0, :, c0:c0 + w] += _dot_tn(h_t, ref[...])
            c0 += w

    in_specs = [_row_spec(D_MODEL)] + [_row_spec(w) for w in widths]
    args = [h, *dz]
    aliases = {}
    if both is not None:
        in_specs.append(pl.BlockSpec(memory_space=pl.ANY))
        aliases = {len(args): 0}
        args.append(both)
    return pl.pallas_call(
        body, name=name, grid=(t // TM,), in_specs=in_specs,
        out_specs=pl.BlockSpec((1, D_MODEL, total), lambda i: (layer, 0, 0)),
        out_shape=jax.ShapeDtypeStruct((DEPTH, D_MODEL, total), F32), input_output_aliases=aliases,
        compiler_params=_params(("arbitrary",)),
    )(*args)


def _ada_bwd(c_all, dmod_cols, dmod_all):
    n = c_all.shape[0]
    cols = dmod_cols.shape[2]

    def body(c_ref, dc_ref, da_ref, gw_ref, gb_ref):
        act = _silu(c_ref[...]).astype(BF16)
        gw_ref[0] = _dot_tn(act, dc_ref[0].astype(BF16))
        gb_ref[0] = jnp.sum(da_ref[0], axis=0, keepdims=True)

    return pl.pallas_call(
        body, name="ada_bwd", grid=(DEPTH,),
        in_specs=[pl.BlockSpec((n, D_MODEL), lambda l: (0, 0)), pl.BlockSpec((1, n, cols), lambda l: (l, 0, 0)),
                  pl.BlockSpec((1, n, 3 * D_MODEL), lambda l: (l, 0, 0))],
        out_specs=[pl.BlockSpec((1, D_MODEL, cols), lambda l: (l, 0, 0)),
                   pl.BlockSpec((1, 1, 3 * D_MODEL), lambda l: (l, 0, 0))],
        out_shape=[jax.ShapeDtypeStruct((DEPTH, D_MODEL, cols), F32), jax.ShapeDtypeStruct((DEPTH, 1, 3 * D_MODEL), F32)],
        compiler_params=_params(("parallel",)),
    )(c_all, dmod_cols, dmod_all)


def _sum_blocks(parts, name):
    n, rows, cols = parts.shape
    tr = rows if rows <= 256 else 8 * next(d for d in range(32, 0, -1) if (rows // 8) % d == 0)

    def body(p_ref, out_ref):
        acc = p_ref[0].astype(F32)
        for k in range(1, n):
            acc = acc + p_ref[k].astype(F32)
        out_ref[...] = acc

    return pl.pallas_call(
        body, name=name, grid=(rows // tr,), in_specs=[pl.BlockSpec((n, tr, cols), lambda i: (0, i, 0))],
        out_specs=pl.BlockSpec((tr, cols), lambda i: (i, 0)), out_shape=jax.ShapeDtypeStruct((rows, cols), F32),
        compiler_params=_params(("parallel",)),
    )(parts)


def _adamw(w, g, m, v, name):
    nl, rows, cols = w.shape
    if rows * cols <= 64 * 1024:
        tr = rows
        tl = next(t for t in range(nl, 0, -1) if nl % t == 0 and t * max(rows, 8) * cols <= 512 * 1024)
    else:
        tl = 1
        tr = next(t for t in (rows, 256, 128, 64, 32, 16, 8) if rows % t == 0 and t * cols <= 256 * 1024)

    def body(w_ref, g_ref, m_ref, v_ref, d_ref, mo_ref, vo_ref):
        gv = g_ref[...]
        mn = ADAM_B1 * m_ref[...] + (1.0 - ADAM_B1) * gv
        vn = ADAM_B2 * v_ref[...] + (1.0 - ADAM_B2) * jnp.square(gv)
        m_hat = mn / (1.0 - ADAM_B1 ** ADAM_STEP)
        v_hat = vn / (1.0 - ADAM_B2 ** ADAM_STEP)
        d_ref[...] = -ADAM_LR * (m_hat / (jnp.sqrt(v_hat) + ADAM_EPS) + ADAM_WD * w_ref[...])
        mo_ref[...] = mn
        vo_ref[...] = vn

    spec = pl.BlockSpec((tl, tr, cols), lambda l, i: (l, i, 0))
    return pl.pallas_call(
        body, name=name, grid=(nl // tl, rows // tr), in_specs=[spec] * 4, out_specs=[spec] * 3,
        out_shape=[jax.ShapeDtypeStruct((nl, rows, cols), F32)] * 3, compiler_params=_params(("parallel", "parallel")),
    )(w, g, m, v)


ALL_FLIPS = tuple(range(1, N_DEV))


def _transfer(name, srcs, dst_shapes, plan, side=None):
    n_arr = len(srcs)
    probe = plan(0, 0, 0)
    n_steps = len(probe)
    side_in = [a for a, _ in side["inputs"]] if side else []
    side_out = side["outputs"] if side else []

    def body(*refs):
        src_refs, refs = refs[:n_arr], refs[n_arr:]
        side_in_refs, refs = refs[:len(side_in)], refs[len(side_in):]
        dst_refs, refs = refs[:n_arr], refs[n_arr:]
        side_out_refs, refs = refs[:len(side_out)], refs[len(side_out):]
        send_sems, recv_sems, local_sems = refs[:3]
        side_scratch = refs[3:]
        x, y, c = lax.axis_index("x"), lax.axis_index("y"), lax.axis_index("c")
        steps = plan(x, y, c)
        side_done = []

        def rows(ref, r0, n):
            return ref.at[:, pl.ds(r0, n), :]

        def arrival(t):
            a, _, _, n, _, f, _ = steps[t]
            return pltpu.make_async_remote_copy(
                src_ref=rows(dst_refs[a], 0, n), dst_ref=rows(dst_refs[a], 0, n), send_sem=send_sems.at[t],
                recv_sem=recv_sems.at[t], device_id=(x, y, c), device_id_type=pl.DeviceIdType.MESH)

        arrived, started = set(), []
        for t, (a, from_dst, sr, n, dr, f, after) in enumerate(steps):
            if side and t == side["at"]:
                after = tuple(after) + tuple(side["needs"])
            for u in after:
                if u not in arrived:
                    if steps[u][5] == 0:
                        started[u].wait()
                    else:
                        arrival(u).wait_recv()
                    arrived.add(u)
            if side and t == side["at"]:
                side_done.append(side["run"](x, y, c, dst_refs, side_in_refs, side_out_refs, side_scratch))
            src = rows(dst_refs[a] if from_dst else src_refs[a], sr, n)
            dst = rows(dst_refs[a], dr, n)
            if f == 0:
                cp = pltpu.make_async_copy(src, dst, local_sems.at[t])
            else:
                to = (1 - x if f & 4 else x, 1 - y if f & 2 else y, 1 - c if f & 1 else c)
                cp = pltpu.make_async_remote_copy(src_ref=src, dst_ref=dst, send_sem=send_sems.at[t],
                                                  recv_sem=recv_sems.at[t], device_id=to,
                                                  device_id_type=pl.DeviceIdType.MESH)
            cp.start()
            started.append(cp)
        for t, step in enumerate(steps):
            if step[5] != 0 and t not in arrived:
                arrival(t).wait_recv()
        for t, (cp, step) in enumerate(zip(started, steps)):
            if step[5] != 0:
                cp.wait_send()
            elif t not in arrived:
                cp.wait()
        for wait in side_done:
            wait()

    any_spec = pl.BlockSpec(memory_space=pl.ANY)
    return pl.pallas_call(
        body, name=name, out_shape=[jax.ShapeDtypeStruct(s, d) for s, d in list(dst_shapes) + list(side_out)],
        in_specs=[any_spec] * n_arr + [spec for _, spec in (side["inputs"] if side else [])],
        out_specs=[any_spec] * (n_arr + len(side_out)),
        scratch_shapes=[pltpu.SemaphoreType.DMA((n_steps,)), pltpu.SemaphoreType.DMA((n_steps,)),
                        pltpu.SemaphoreType.DMA((n_steps,))] + (side["scratch"] if side else []),
    )(*srcs, *side_in)


CHIP_FLIPS = (2, 4, 6)


def _gather_plan(chip_rows):
    def plan(x, y, c):
        steps = []
        chip = 2 * x + y
        for a, rc in enumerate(chip_rows):
            h = rc // 2
            first = (h // 32) * 16
            mine = rc * chip + h * c
            from_x, from_y, diag = rc * (chip ^ 2) + h * c, rc * (chip ^ 1) + h * c, rc * (chip ^ 3) + h * c
            steps.append((a, False, h * c, h, mine, 0, ()))
            to_x = len(steps)
            steps.append((a, False, h * c, h, mine, 4, ()))
            to_y = len(steps)
            steps.append((a, False, h * c, h, mine, 2, ()))
            fwd_y = len(steps)
            steps.append((a, True, from_x, first, from_x, 2, (to_x,)))
            fwd_x = len(steps)
            steps.append((a, True, from_y + first, h - first, from_y + first, 4, (to_y,)))
            steps.append((a, False, h * c, h, mine, 1, ()))
            steps.append((a, True, from_x, h, from_x, 1, (to_x,)))
            steps.append((a, True, from_y, h, from_y, 1, (to_y,)))
            steps.append((a, True, diag, first, diag, 1, (fwd_y,)))
            steps.append((a, True, diag + first, h - first, diag + first, 1, (fwd_x,)))
        return steps
    return plan


def _pair_reduce_plan(chip_rows):
    def plan(x, y, c):
        steps = []
        for a, rc in enumerate(chip_rows):
            h = rc // 2
            for j in range(4):
                steps.append((a, False, rc * j + h * (1 - c), h, h * j, 1, ()))
        return steps
    return plan


def _chip_scatter_plan(chip_rows):
    def plan(x, y, c):
        steps = []
        for a, rc in enumerate(chip_rows):
            h = rc // 2
            mine = h * (2 * x + y)
            arrivals = []
            for k, f in enumerate(CHIP_FLIPS):
                arrivals.append(len(steps))
                steps.append((a, False, h * ((2 * x + y) ^ (f >> 1)), h, h * k, f, ()))
            steps.append((a, False, mine, h, 3 * h, 0, ()))
            steps.append((a, False, mine, h, 7 * h, 1, ()))
            for k, t in enumerate(arrivals):
                steps.append((a, True, h * k, h, (4 + k) * h, 1, (t,)))
        return steps
    return plan


def _tile_rows(h):
    return h


def _sum_pair(partial, recv, core, rc, name):
    nl, _, cols = partial.shape
    h = rc // 2
    tr = _tile_rows(h)

    def body(c_ref, p_ref, r_ref, out_ref):
        out_ref[...] = (p_ref[...] + r_ref[...]).astype(BF16)

    spec = pl.BlockSpec((1, tr, cols), lambda l, j, i, c_ref: (l, (h // tr) * j + i, 0))
    return pl.pallas_call(
        body, name=name, out_shape=jax.ShapeDtypeStruct((nl, 4 * h, cols), BF16),
        grid_spec=pltpu.PrefetchScalarGridSpec(
            num_scalar_prefetch=1, grid=(nl, 4, h // tr),
            in_specs=[pl.BlockSpec((1, tr, cols), lambda l, j, i, c_ref: (l, (rc // tr) * j + (h // tr) * c_ref[0] + i, 0)),
                      spec],
            out_specs=spec),
        compiler_params=_params(("parallel", "parallel", "parallel")),
    )(core, partial, recv)


def _sum_chips(slots, core, rc, name):
    nl, _, cols = slots.shape
    h = rc // 2

    def body(c_ref, s_ref, out_ref):
        acc = s_ref[0, 3].astype(F32)
        for k in range(3):
            acc = acc + s_ref[0, k].astype(F32)
        out_ref[0] = acc

    return pl.pallas_call(
        body, name=name, out_shape=jax.ShapeDtypeStruct((nl, rc, cols), F32),
        grid_spec=pltpu.PrefetchScalarGridSpec(
            num_scalar_prefetch=1, grid=(nl, 2),
            in_specs=[pl.BlockSpec((1, 4, h, cols), lambda l, q, c_ref: (l, q, 0, 0))],
            out_specs=pl.BlockSpec((1, h, cols), lambda l, q, c_ref: (l, q + c_ref[0] - 2 * q * c_ref[0], 0))),
        compiler_params=_params(("parallel", "parallel")),
    )(core, slots.reshape(nl, 8, h, cols))


def _pad_cols(a, n):
    return a if n == 0 else jnp.pad(a, ((0, 0), (0, n)))


def _in_to_padded(w):
    return jnp.concatenate([_pad_cols(w[:, a:b], z) for a, b, z in IN_PIECES], axis=1)


def _in_cols_first(ga, gb):
    split = Z_G[1]
    pos, out = 0, {}
    for a, b, z in IN_PIECES:
        src, off = (ga, pos) if pos < split else (gb, pos - split)
        out[a] = jnp.transpose(src[:, :, off:off + (b - a)], (2, 0, 1))
        pos += (b - a) + z
    return jnp.concatenate([out[a] for a in sorted(out)], axis=0)


def _out_to_padded(w):
    z = jnp.zeros((64, w.shape[1]), w.dtype)
    return jnp.concatenate([w[0:384], w[384:704], z, w[704:1024], z], axis=0)


def _out_from_padded(gp):
    return jnp.concatenate([gp[0:384], gp[384:704], gp[768:1088]], axis=0)


def _uq_to_padded(w):
    parts = []
    for p in range(3):
        h0, h1 = 2 * p, 2 * p + 1
        parts += [w[:, 96 * h0:96 * h0 + 64], w[:, 96 * h1:96 * h1 + 64], w[:, 96 * h0 + 64:96 * h0 + 96],
                  w[:, 96 * h1 + 64:96 * h1 + 96], jnp.zeros((w.shape[0], 64), w.dtype)]
    return jnp.concatenate(parts, axis=1)


def _uq_from_padded(gp):
    parts = []
    for h in range(6):
        p, s = h // 2, h % 2
        parts += [gp[:, 256 * p + 64 * s:256 * p + 64 * s + 64], gp[:, 256 * p + 128 + 32 * s:256 * p + 160 + 32 * s]]
    return jnp.concatenate(parts, axis=1)


def _ukv_to_padded(w):
    return jnp.concatenate([w[:, 128 * h:128 * h + 64] for h in range(6)]
                           + [w[:, 128 * h + 64:128 * h + 128] for h in range(6)], axis=1)


def _ukv_from_padded(gp):
    parts = []
    for h in range(6):
        parts += [gp[:, 64 * h:64 * h + 64], gp[:, 384 + 64 * h:384 + 64 * h + 64]]
    return jnp.concatenate(parts, axis=1)


LR_ROWS = 224
SHARD_ROWS = (256, 256, 256, LR_ROWS)


def _pack_lowrank(w_uq, w_ukv):
    flat = jnp.concatenate([w_uq.reshape(-1), w_ukv.reshape(-1)])
    return jnp.pad(flat, (0, LR_ROWS * PACK_COLS - flat.shape[0])).reshape(1, LR_ROWS, PACK_COLS)


def _unpack_lowrank(packed):
    flat = packed.reshape(-1)
    n_uq = DEPTH * A_Q_RANK * 144
    n_ukv = DEPTH * A_KV_RANK * 192
    return flat[0:n_uq].reshape(DEPTH, A_Q_RANK, 144), flat[n_uq:n_uq + n_ukv].reshape(DEPTH, A_KV_RANK, 192)


def _rope_tables(positions):
    t = positions.size
    inv = ROPE_THETA ** (-jnp.arange(0, A_ROPE, 2, dtype=F32) / A_ROPE)
    inv_row = jnp.pad(jnp.tile(inv, 4), (0, 64)).reshape(1, LANE)

    def body(p_ref, i_ref, c_ref, sa_ref, sb_ref):
        ang = p_ref[...].astype(F32) * i_ref[...]
        lane = _lane_iota((TM, LANE))
        live = lane < 64
        second = (lane & 31) >= 16
        s = jnp.sin(ang)
        c_ref[...] = jnp.where(live, jnp.cos(ang), 0.0)
        sa_ref[...] = jnp.where(live & jnp.logical_not(second), -s, 0.0)
        sb_ref[...] = jnp.where(live & second, s, 0.0)

    return pl.pallas_call(
        body, name="rope_tables", grid=(t // TM,), in_specs=[_row_spec(1), _full_spec((1, LANE))],
        out_specs=[_row_spec(LANE)] * 3, out_shape=[jax.ShapeDtypeStruct((t, LANE), F32)] * 3,
        compiler_params=_params(("parallel",)),
    )(positions.reshape(t, 1), inv_row)


def _rows(a, n):
    flat = a.reshape(-1)
    return jnp.pad(flat, (0, n * LANE - flat.shape[0])).reshape(n, LANE)


def _forward_backward(x, mod, tables, target, weights, small, nb, seq):
    cos, sina, sinb = tables
    saved = []
    for l in range(DEPTH):
        w, s = weights[l], small[l]
        shift = mod[l][:, None, 0:D_MODEL]
        scale = mod[l][:, None, D_MODEL:2 * D_MODEL]
        gate = mod[l][:, None, 2 * D_MODEL:]
        h, za, zg, qb, kb, vb, qc, kc, vc, zf = _ln_inproj(x, shift, scale, s["norm_g"], w["in"], seq)
        qa, ka, va = _a_up(za, s["gq"], s["gkv"], w["uq"], w["ukv"], cos, sina, sinb)
        bias = _bias_expand(s["g8"])[0:6]
        f = _forget_fwd(zf, s["fb"], nb, seq)
        frow = jnp.pad(f[:, 0:5].reshape(nb, seq, 5).transpose(0, 2, 1), ((0, 0), (0, 1), (0, 0)))
        frow = frow.reshape(nb, 6, 1, seq)
        oa, lse_a = _attn_fwd("A", qa, ka, va, None, nb, seq)
        ob, lse_b = _attn_fwd("B", qb, kb, vb, bias, nb, seq)
        oc, lse_c = _attn_fwd("C", qc, kc, vc, frow, nb, seq)
        y, xn = _gate_outproj(x, gate, oa, ob, oc, zg, w["out"], seq)
        saved.append(dict(x=x, h=h, za=za, zg=zg, zf=zf, y=y, shift=shift, scale=scale, gate=gate, bias=bias, frow=frow,
                          a=(qa, ka, va, oa, lse_a), b=(qb, kb, vb, ob, lse_b), c=(qc, kc, vc, oc, lse_c)))
        x = xn
    dx, loss8, gfinal8 = _final_loss(x, target, small[0]["final_g"])
    grads = []
    gw_in = (None, None)
    for l in reversed(range(DEPTH)):
        w, s, sv = weights[l], small[l], saved[l]
        qa, ka, va, oa, lse_a = sv["a"]
        qb, kb, vb, ob, lse_b = sv["b"]
        qc, kc, vc, oc, lse_c = sv["c"]
        doa, dob, doc, dzg, gw_out, dgate = _outproj_bwd(dx, sv["y"], sv["gate"], oa, ob, oc, sv["zg"], w["out"],
                                                          w["out_t"], nb, seq)
        dqa, dka, dva = _attn_bwd("A", qa, ka, va, oa, doa, lse_a, None, nb, seq)
        dqb, dkb, dvb, dbt = _attn_bwd("B", qb, kb, vb, ob, dob, lse_b, sv["bias"], nb, seq)
        dqc, dkc, dvc, dfr, dfq = _attn_bwd("C", qc, kc, vc, oc, doc, lse_c, sv["frow"], nb, seq)
        dg = _bias_reduce(jnp.pad(dbt, ((0, 2), (0, 0), (0, 0))))
        grb = jnp.pad(_bias_unline(dg), ((0, 0), (0, 384 - N_REL)))
        dfk = dfr.reshape(nb, 6, seq).transpose(0, 2, 1).reshape(nb * seq, 6)
        dzf, gfb = _forget_bwd(dfq, jnp.pad(dfk, ((0, 0), (0, LANE - 6))), sv["zf"], s["fb"], nb, seq)
        dza, gw_uq, gw_ukv, ggq, ggkv = _a_up_bwd(dqa, dka, dva, sv["za"], s["gq"], s["gkv"], w["uq_t"], w["ukv_t"],
                                                  cos, sina, sinb)
        dz = (dza, dzg, dqb, dkb, dvb, dqc, dkc, dvc, dzf)
        dx, dshift, dscale, gnorm = _inproj_bwd_dx(dz, dx, sv["x"], sv["shift"], sv["scale"], s["norm_g"], w["in"],
                                                   nb, seq)
        gw_in = (_inproj_bwd_dw(sv["h"], dz[0:2], "inproj_bwd_dw0", l, gw_in[0]),
                 _inproj_bwd_dw(sv["h"], dz[2:], "inproj_bwd_dw1", l, gw_in[1]))
        dmod = jnp.concatenate([dshift[:, 0], dscale[:, 0], dgate[:, 0]], axis=1)
        grads.append(dict(w_out=gw_out, w_uq=gw_uq, w_ukv=gw_ukv, dmod=dmod, norm_g=gnorm[0], gq=ggq[0],
                          gkv=ggkv[0], rb8=grb, fb=gfb[0]))
    grads.reverse()
    return loss8[0, 0], dx, grads, gfinal8[0], gw_in


def _layer_weights(w_in, w_out, w_uq, w_ukv):
    wi, wo, wq, wkv = _in_to_padded(w_in), _out_to_padded(w_out), _uq_to_padded(w_uq), _ukv_to_padded(w_ukv)
    return {"in": wi, "out": wo, "out_t": wo.T, "uq": wq, "uq_t": wq.T, "ukv": wkv, "ukv_t": wkv.T}


def _layer_small(norm_g, gq, gkv, rel_bias, forget_b, final_g):
    fb = jnp.pad(forget_b, (0, LANE - 5)).reshape(1, LANE)
    return dict(norm_g=norm_g.reshape(1, -1), gq=gq.reshape(1, -1), gkv=gkv.reshape(1, -1), g8=_bias_line(rel_bias), fb=fb,
                final_g=final_g.reshape(1, -1))


def _small_payload(per_layer, final_g, loss):
    def stack(key):
        return jnp.stack([p[key] for p in per_layer])

    def rows(a, rng):
        return _rows(a, rng[1] - rng[0])

    parts = [rows(stack("dmod"), PAY_DMOD), rows(stack("norm_g"), PAY_NORM), rows(stack("gq"), PAY_GQ),
             rows(stack("gkv"), PAY_GKV), rows(stack("rb8"), PAY_RB), rows(stack("fb"), PAY_FB),
             rows(final_g, PAY_FINAL), rows(loss, PAY_LOSS)]
    return jnp.concatenate(parts, axis=0)


def _payload_split(pay):
    def take(rng, shape):
        n = 1
        for d in shape:
            n *= d
        return pay[rng[0]:rng[1]].reshape(-1)[0:n].reshape(shape)

    norm_g = take(PAY_NORM, (DEPTH, D_MODEL))
    gq = take(PAY_GQ, (DEPTH, A_Q_RANK))
    gkv = take(PAY_GKV, (DEPTH, A_KV_RANK))
    rb = take(PAY_RB, (DEPTH, 8, 384))[:, 0:5, 0:N_REL]
    fb = take(PAY_FB, (DEPTH, LANE))[:, 0:5]
    final_g = take(PAY_FINAL, (D_MODEL,))
    return norm_g, gq, gkv, rb, fb, final_g


def kernel(x, c, positions, w_ada, b_ada, norm_g, w_in, a_q_norm_g, a_w_uq, a_kv_norm_g, a_w_ukv, b_rel_bias, c_forget_b, w_out, final_g, loss_target, m_w_ada, m_b_ada, m_norm_g, m_w_in, m_a_q_norm_g, m_a_w_uq, m_a_kv_norm_g, m_a_w_ukv, m_b_rel_bias, m_c_forget_b, m_w_out, m_final_g, v_w_ada, v_b_ada, v_norm_g, v_w_in, v_a_q_norm_g, v_a_w_uq, v_a_kv_norm_g, v_a_w_ukv, v_b_rel_bias, v_c_forget_b, v_w_out, v_final_g):
    nb, seq, _ = x.shape
    ix, iy, ic = lax.axis_index("x"), lax.axis_index("y"), lax.axis_index("c")
    chip = 2 * ix + iy

    weight_plan = _gather_plan((256, 256, LR_ROWS))
    cols = w_ada.shape[2]

    def gather_plan(x, y, c_):
        me_ = 4 * x + 2 * y + c_
        first = [(3, False, 0, 8, 8 * me_, f, ()) for f in range(N_DEV)]
        rest = weight_plan(x, y, c_)
        order = [i for i, s in enumerate(rest) if not s[6]] + [i for i, s in enumerate(rest) if s[6]]
        place = {old: len(first) + new for new, old in enumerate(order)}
        return first + [rest[i][:6] + (tuple(place[u] for u in rest[i][6]),) for i in order]

    probe = gather_plan(0, 0, 0)

    def project(x, y, c_, dst_refs, in_refs, out_refs, scratch):
        w_ref, b_ref = in_refs
        (mod_ref,) = out_refs
        c_vm, mod_vm, send_m, recv_m, local_m = scratch
        me_ = 4 * x + 2 * y + c_
        load = pltpu.make_async_copy(dst_refs[3].at[0], c_vm, local_m.at[0])
        load.start()
        load.wait()
        act = _silu(c_vm[...]).astype(BF16)
        for l in range(DEPTH):
            mod_vm[l] = _dot(act, w_ref[l].astype(BF16)) + b_ref[l]

        def rows_of(j):
            return mod_vm.at[:, pl.ds(pl.multiple_of(8 * j, 8), 8), :]

        def send(i, f, slot, j):
            to = (1 - x if f & 4 else x, 1 - y if f & 2 else y, 1 - c_ if f & 1 else c_)
            return pltpu.make_async_remote_copy(src_ref=rows_of(j), dst_ref=mod_ref.at[slot], send_sem=send_m.at[i],
                                                recv_sem=recv_m.at[i], device_id=to, device_id_type=pl.DeviceIdType.MESH)

        own = pltpu.make_async_copy(rows_of(me_), mod_ref.at[me_], local_m.at[1])
        own.start()
        sends = [send(i, f, me_, me_ ^ f) for i, f in enumerate(ALL_FLIPS)]
        for s in sends:
            s.start()

        def finish():
            for i, f in enumerate(ALL_FLIPS):
                send(i, f, me_ ^ f, 0).wait_recv()
            for s in sends:
                s.wait_send()
            own.wait()
        return finish

    b_cols = lax.dynamic_slice_in_dim(b_ada, chip * cols, cols, axis=1)[:, None, :]
    vmem = pl.BlockSpec(memory_space=pltpu.VMEM)
    side = dict(at=next(t for t, s in enumerate(probe) if s[6]), needs=tuple(range(N_DEV)),
                inputs=[(w_ada, vmem), (b_cols, vmem)], outputs=[((N_DEV, DEPTH, 8, cols), F32)],
                scratch=[pltpu.VMEM((8 * N_DEV, D_MODEL), F32), pltpu.VMEM((DEPTH, 8 * N_DEV, cols), F32),
                         pltpu.SemaphoreType.DMA((len(ALL_FLIPS),)), pltpu.SemaphoreType.DMA((len(ALL_FLIPS),)),
                         pltpu.SemaphoreType.DMA((2,))],
                run=project)
    full_in, full_out, full_lr, c_rows, mod_rows = _transfer(
        "gather_weights", [w_in.astype(BF16), w_out.astype(BF16), _pack_lowrank(a_w_uq, a_w_ukv).astype(BF16),
                           jnp.pad(c, ((0, 8 - nb), (0, 0)))[None]],
        [((DEPTH, D_MODEL, N_IN), BF16), ((DEPTH, D_MODEL, D_MODEL), BF16), ((1, 4 * LR_ROWS, PACK_COLS), BF16),
         ((1, 8 * N_DEV, D_MODEL), F32)], gather_plan, side)
    lowrank = [_unpack_lowrank(full_lr[0, LR_ROWS * j:LR_ROWS * (j + 1)]) for j in range(4)]
    full_uq = jnp.concatenate([s[0] for s in lowrank], axis=2)
    full_ukv = jnp.concatenate([s[1] for s in lowrank], axis=2)
    weights = [_layer_weights(full_in[l], full_out[l], full_uq[l], full_ukv[l]) for l in range(DEPTH)]
    small = [_layer_small(norm_g[l], a_q_norm_g[l], a_kv_norm_g[l], b_rel_bias[l], c_forget_b[l], final_g)
             for l in range(DEPTH)]

    c_all = c_rows.reshape(N_DEV, 8, D_MODEL)[:, 0:nb].reshape(N_DEV * nb, D_MODEL)
    mod = jnp.concatenate([mod_rows[2 * j][:, 0:nb] for j in range(4)], axis=2)

    tables = _rope_tables(positions)
    loss_part, dx, grads, gfinal, gw_in = _forward_backward(
        x.reshape(nb * seq, D_MODEL), mod, tables, loss_target.reshape(nb * seq, D_MODEL), weights, small, nb, seq)

    g_uq = jnp.stack([_uq_from_padded(g["w_uq"]) for g in grads])
    g_ukv = jnp.stack([_ukv_from_padded(g["w_ukv"]) for g in grads])
    g_lr = jnp.concatenate([_pack_lowrank(g_uq[:, :, 144 * j:144 * (j + 1)], g_ukv[:, :, 192 * j:192 * (j + 1)])
                            for j in range(4)], axis=1)
    partials = [gw_in[0], gw_in[1], jnp.stack([_out_from_padded(g["w_out"]) for g in grads]), g_lr]
    shapes = [(p.shape[0], p.shape[2]) for p in partials]
    halves = [r // 2 for r in SHARD_ROWS]
    core_s = jnp.reshape(ic, (1,)).astype(jnp.int32)
    pay = _small_payload(grads, gfinal, loss_part)
    pay_rows = pay.shape[0]
    reduce_plan = _pair_reduce_plan(SHARD_ROWS)

    def reduce_plan_with_payload(x, y, c_):
        me_ = 4 * x + 2 * y + c_
        return reduce_plan(x, y, c_) + [(len(partials), False, 0, pay_rows, pay_rows * me_, f, ()) for f in range(N_DEV)]

    *from_pair, pay_all = _transfer(
        "pair_reduce", partials + [pay[None]],
        [((nl, 4 * h, nc), F32) for (nl, nc), h in zip(shapes, halves)] + [((1, N_DEV * pay_rows, LANE), F32)],
        reduce_plan_with_payload)
    pay_all = pay_all.reshape(N_DEV, pay_rows, LANE)

    tot = _sum_blocks(pay_all, "sum_small")
    loss = tot[PAY_LOSS[0], 0]
    dmod_all = pay_all[:, PAY_DMOD[0]:PAY_DMOD[1]].reshape(N_DEV, -1)[:, 0:DEPTH * nb * 3 * D_MODEL]
    dmod_all = dmod_all.reshape(N_DEV, DEPTH, nb, 3 * D_MODEL).transpose(1, 0, 2, 3)
    dmod_all = dmod_all.reshape(DEPTH, N_DEV * nb, 3 * D_MODEL)
    my_cols = lax.dynamic_slice_in_dim(dmod_all, chip * cols, cols, axis=2)
    g_w_ada, g_b_ada = _ada_bwd(c_all, my_cols, dmod_all)
    g_b_ada = g_b_ada[:, 0]
    chip_sums = [_sum_pair(p, r, core_s, rc, "sum_pair%d" % i)
                 for i, (p, r, rc) in enumerate(zip(partials, from_pair, SHARD_ROWS))]
    slots = _transfer("chip_scatter", chip_sums, [((nl, 8 * h, nc), BF16) for (nl, nc), h in zip(shapes, halves)],
                      _chip_scatter_plan(SHARD_ROWS))
    g_in_a, g_in_b, g_out_sh, g_lr_sh = [_sum_chips(s, core_s, rc, "sum_chips%d" % i)
                                         for i, (s, rc) in enumerate(zip(slots, SHARD_ROWS))]
    g_uq_sh, g_ukv_sh = _unpack_lowrank(g_lr_sh[0])

    def cols_first(a):
        return jnp.transpose(a, (2, 0, 1))

    g_in_t = _in_cols_first(g_in_a, g_in_b)
    upd_in = tuple(jnp.transpose(a, (1, 2, 0)) for a in _adamw(cols_first(w_in), g_in_t, cols_first(m_w_in),
                                                               cols_first(v_w_in), "adamw_in"))
    gw = (jnp.transpose(g_in_t, (1, 2, 0)), g_out_sh, g_uq_sh, g_ukv_sh)
    upd = [upd_in, _adamw(w_out, gw[1], m_w_out, v_w_out, "adamw_out"),
           _adamw(a_w_uq, gw[2], m_a_w_uq, v_a_w_uq, "adamw_uq"), _adamw(a_w_ukv, gw[3], m_a_w_ukv, v_a_w_ukv, "adamw_ukv")]
    dw, mw, vw = (tuple(u[i] for u in upd) for i in range(3))
    d_ada, m_ada, v_ada = _adamw(w_ada, g_w_ada, m_w_ada, v_w_ada, "adamw_ada")

    def adam_small(w, g, m, v, name):
        shape3 = (1,) * (3 - w.ndim) + w.shape
        return tuple(a.reshape(w.shape) for a in _adamw(w.reshape(shape3), g.reshape(shape3), m.reshape(shape3),
                                                        v.reshape(shape3), name))

    d_b, m_b, v_b = adam_small(b_ada, g_b_ada, m_b_ada, v_b_ada, "adamw_b_ada")
    gs = _payload_split(tot)
    small_upd = [adam_small(w, g, m, v, "adamw_small%d" % i) for i, (w, g, m, v) in enumerate(zip(
        (norm_g, a_q_norm_g, a_kv_norm_g, b_rel_bias, c_forget_b, final_g), gs,
        (m_norm_g, m_a_q_norm_g, m_a_kv_norm_g, m_b_rel_bias, m_c_forget_b, m_final_g),
        (v_norm_g, v_a_q_norm_g, v_a_kv_norm_g, v_b_rel_bias, v_c_forget_b, v_final_g)))]
    ds, ms, vs = (tuple(u[i] for u in small_upd) for i in range(3))

    def ordered(ada, b, sm, big):
        ng, gq, gkv, rb, fb, fg = sm
        b_in, b_out, b_uq, b_ukv = big
        return (ada, b, ng, b_in, gq, b_uq, gkv, b_ukv, rb, fb, b_out, fg)

    return (loss, dx.reshape(nb, seq, D_MODEL), *ordered(g_w_ada, g_b_ada, gs, gw), *ordered(d_ada, d_b, ds, dw),
            *ordered(m_ada, m_b, ms, mw), *ordered(v_ada, v_b, vs, vw))
```

```python
import functools

import jax
import jax.numpy as jnp
from jax import lax
from jax.experimental import pallas as pl
from jax.experimental.pallas import tpu as pltpu

F32 = jnp.float32
BF16 = jnp.bfloat16

D_MODEL = 1024
DEPTH = 2
EPS = 1e-6
NEG = -1e30
LOG2E = 1.4426950408889634
ROPE_THETA = 10000.0
A_ROPE = 32
A_Q_RANK = 384
A_KV_RANK = 256
REL_CLIP = 128
N_REL = 2 * REL_CLIP + 1
N_IN = 3621

ADAM_LR = 0.001
ADAM_B1 = 0.9
ADAM_B2 = 0.999
ADAM_EPS = 1e-08
ADAM_WD = 0.01
ADAM_STEP = 10

LANE = 128
VMEM_LIMIT = 56 * 1024 * 1024

NP_IN = 4352
Z_A = (0, 768)
Z_G = (768, 1920)
Z_QKV = tuple((1920 + 384 * i, 1920 + 384 * (i + 1)) for i in range(6))
Z_F = (4224, 4352)
IN_PIECES = ((0, 672, 96), (672, 1056, 0), (2016, 2336, 64), (3301, 3621, 64), (1056, 1376, 64), (1376, 1696, 64),
             (1696, 2016, 64), (2336, 2656, 64), (2656, 2976, 64), (2976, 3296, 64), (3296, 3301, 123))
D_CAT = 1152

TM = 512
T_CAUSAL = 256
T_BAND = 128
BAND_TILES = 5
N_DEV = 8

PAY_DMOD = (0, 96)
PAY_NORM = (96, 112)
PAY_GQ = (112, 120)
PAY_GKV = (120, 128)
PAY_RB = (128, 176)
PAY_FB = (176, 184)
PAY_FINAL = (184, 192)
PAY_LOSS = (192, 200)

PACK_COLS = 1024


def _params(sem=None):
    return pltpu.CompilerParams(dimension_semantics=sem, vmem_limit_bytes=VMEM_LIMIT)


def _lane_iota(shape):
    return lax.broadcasted_iota(jnp.int32, shape, len(shape) - 1)


def _silu(u):
    return u * jax.nn.sigmoid(u)


def _dsilu(u):
    s = jax.nn.sigmoid(u)
    return s * (1.0 + u * (1.0 - s))


def _rms(x, g):
    r = lax.rsqrt(jnp.mean(x * x, axis=-1, keepdims=True) + EPS)
    xh = x * r
    return xh * g, xh, r


def _rms_bwd(dy, xh, r, g):
    dxh = dy * g
    return r * (dxh - xh * jnp.mean(dxh * xh, axis=-1, keepdims=True))


def _rope(x, cos, sina, sinb):
    return x * cos + pltpu.roll(x, 16, 1) * sinb + pltpu.roll(x, LANE - 16, 1) * sina


def _rope_t(dy, cos, sina, sinb):
    return dy * cos + pltpu.roll(dy * sinb, LANE - 16, 1) + pltpu.roll(dy * sina, 16, 1)


def _split3(x):
    hi = x.astype(BF16)
    r1 = x - hi.astype(F32)
    mid = r1.astype(BF16)
    lo = (r1 - mid.astype(F32)).astype(BF16)
    return hi, mid, lo


def _dot(a, b):
    return jnp.dot(a, b, preferred_element_type=F32)


def _dot_nt(a, b):
    return lax.dot_general(a, b, (((1,), (1,)), ((), ())), preferred_element_type=F32)


def _dot_tn(a, b):
    return lax.dot_general(a, b, (((0,), (0,)), ((), ())), preferred_element_type=F32)


def _row_spec(cols):
    return pl.BlockSpec((TM, cols), lambda i: (i, 0))


def _full_spec(shape):
    return pl.BlockSpec(shape, lambda i: (0,) * len(shape))


def _ex_spec(tiles_per_ex):
    return pl.BlockSpec((1, 1, D_MODEL), lambda i: (i // tiles_per_ex, 0, 0))


def _ln_inproj(x, shift, scale, g, w_in_p, seq):
    t = x.shape[0]

    def body(x_ref, sh_ref, sc_ref, g_ref, w_ref, h_ref, za_ref, zg_ref, q0, q1, q2, q3, q4, q5, zf_ref):
        n, _, _ = _rms(x_ref[...], g_ref[...])
        h = (n * (1.0 + sc_ref[0]) + sh_ref[0]).astype(BF16)
        h_ref[...] = h
        za_ref[...] = _dot(h, w_ref[:, Z_A[0]:Z_A[1]])
        zg_ref[...] = _dot(h, w_ref[:, Z_G[0]:Z_G[1]])
        for ref, (c0, c1) in zip((q0, q1, q2, q3, q4, q5), Z_QKV):
            ref[...] = _dot(h, w_ref[:, c0:c1]).astype(BF16)
        zf_ref[...] = _dot(h, w_ref[:, Z_F[0]:Z_F[1]])

    tpe = seq // TM
    shapes = [jax.ShapeDtypeStruct((t, D_MODEL), BF16), jax.ShapeDtypeStruct((t, 768), F32),
              jax.ShapeDtypeStruct((t, D_CAT), F32)]
    shapes += [jax.ShapeDtypeStruct((t, 384), BF16)] * 6 + [jax.ShapeDtypeStruct((t, LANE), F32)]
    return pl.pallas_call(
        body, name="ln_inproj", grid=(t // TM,),
        in_specs=[_row_spec(D_MODEL), _ex_spec(tpe), _ex_spec(tpe), _full_spec((1, D_MODEL)),
                  _full_spec((D_MODEL, NP_IN))],
        out_specs=[_row_spec(D_MODEL), _row_spec(768), _row_spec(D_CAT)] + [_row_spec(384)] * 6 + [_row_spec(LANE)],
        out_shape=shapes, compiler_params=_params(("parallel",)),
    )(x, shift, scale, g, w_in_p)


def _a_up(za, gq, gkv, w_uq_p, w_ukv_p, cos, sina, sinb):
    t = za.shape[0]

    def body(za_ref, gq_ref, gkv_ref, wq_ref, wkv_ref, cos_ref, sa_ref, sb_ref, q_ref, k_ref, v_ref):
        cos_t, sa, sb = cos_ref[...], sa_ref[...], sb_ref[...]
        cqn, _, _ = _rms(za_ref[:, 0:384], gq_ref[...])
        q = _dot(cqn.astype(BF16), wq_ref[...])
        ckvn, _, _ = _rms(za_ref[:, 384:640], gkv_ref[...])
        kv = _dot(ckvn.astype(BF16), wkv_ref[...])
        kpe = za_ref[:, 640:768]
        kpe = _rope(kpe + pltpu.roll(kpe, 32, 1), cos_t, sa, sb).astype(BF16)
        for p in range(3):
            q_ref[:, 256 * p:256 * p + 128] = q[:, 256 * p:256 * p + 128].astype(BF16)
            q_ref[:, 256 * p + 128:256 * p + 256] = _rope(q[:, 256 * p + 128:256 * p + 256], cos_t, sa, sb).astype(BF16)
            k_ref[:, 256 * p:256 * p + 128] = kv[:, 128 * p:128 * p + 128].astype(BF16)
            k_ref[:, 256 * p + 128:256 * p + 256] = kpe
        v_ref[...] = kv[:, 384:768].astype(BF16)

    return pl.pallas_call(
        body, name="a_up", grid=(t // TM,),
        in_specs=[_row_spec(768), _full_spec((1, 384)), _full_spec((1, 256)), _full_spec((384, 768)),
                  _full_spec((256, 768)), _row_spec(LANE), _row_spec(LANE), _row_spec(LANE)],
        out_specs=[_row_spec(768), _row_spec(768), _row_spec(384)],
        out_shape=[jax.ShapeDtypeStruct((t, 768), BF16), jax.ShapeDtypeStruct((t, 768), BF16),
                   jax.ShapeDtypeStruct((t, 384), BF16)],
        compiler_params=_params(("parallel",)),
    )(za, gq, gkv, w_uq_p, w_ukv_p, cos, sina, sinb)


def _tri(n, upper):
    r = lax.broadcasted_iota(jnp.int32, (n, n), 0)
    c = lax.broadcasted_iota(jnp.int32, (n, n), 1)
    return jnp.where((c >= r) if upper else (c <= r), 1.0, 0.0).astype(BF16)


def _forget_fwd(zf, fb, nb, seq):
    blk = 256

    def body(zf_ref, fb_ref, f_ref):
        tri = _tri(blk, False)
        live = _lane_iota((blk, LANE)) < 5
        carry = jnp.zeros((1, LANE), F32)
        for i in range(seq // blk):
            u = zf_ref[i * blk:(i + 1) * blk, :] + fb_ref[...]
            lf = jnp.where(live, jnp.minimum(u, 0.0) - jnp.log(1.0 + jnp.exp(-jnp.abs(u))), 0.0)
            hi, mid, lo = _split3(lf)
            f_ref[i * blk:(i + 1) * blk, :] = (_dot(tri, hi) + _dot(tri, mid) + _dot(tri, lo) + carry) * LOG2E
            carry = carry + jnp.sum(lf, axis=0, keepdims=True)

    return pl.pallas_call(
        body, name="forget_fwd", grid=(nb,),
        in_specs=[pl.BlockSpec((seq, LANE), lambda b: (b, 0)), pl.BlockSpec((1, LANE), lambda b: (0, 0))],
        out_specs=pl.BlockSpec((seq, LANE), lambda b: (b, 0)),
        out_shape=jax.ShapeDtypeStruct((nb * seq, LANE), F32), compiler_params=_params(("parallel",)),
    )(zf, fb)


def _forget_bwd(dfq, dfk, zf, fb, nb, seq):
    blk = 256

    def body(dfq_ref, dfk_ref, zf_ref, fb_ref, dz_ref, gb_ref):
        @pl.when(pl.program_id(0) == 0)
        def _():
            gb_ref[...] = jnp.zeros_like(gb_ref)

        tri = _tri(blk, True)
        lane = _lane_iota((blk, LANE))
        wide = _lane_iota((blk, 384))
        live = lane < 5
        carry = jnp.zeros((1, LANE), F32)
        gsum = jnp.zeros((1, LANE), F32)
        for i in reversed(range(seq // blk)):
            d = dfk_ref[i * blk:(i + 1) * blk, :]
            dq = dfq_ref[i * blk:(i + 1) * blk, :]
            for hd in range(5):
                col = jnp.sum(jnp.where(wide == 64 * hd, dq, 0.0), axis=-1, keepdims=True)
                d = d + jnp.where(lane == hd, col, 0.0)
            hi, mid, lo = _split3(d)
            dlf = _dot(tri, hi) + _dot(tri, mid) + _dot(tri, lo) + carry
            carry = carry + jnp.sum(d, axis=0, keepdims=True)
            u = zf_ref[i * blk:(i + 1) * blk, :] + fb_ref[...]
            du = jnp.where(live, dlf * jax.nn.sigmoid(-u), 0.0)
            dz_ref[i * blk:(i + 1) * blk, :] = du.astype(BF16)
            gsum = gsum + jnp.sum(du, axis=0, keepdims=True)
        gb_ref[...] += jnp.broadcast_to(gsum, gb_ref.shape)

    return pl.pallas_call(
        body, name="forget_bwd", grid=(nb,),
        in_specs=[pl.BlockSpec((seq, 384), lambda b: (b, 0)), pl.BlockSpec((seq, LANE), lambda b: (b, 0)),
                  pl.BlockSpec((seq, LANE), lambda b: (b, 0)), pl.BlockSpec((1, LANE), lambda b: (0, 0))],
        out_specs=[pl.BlockSpec((seq, LANE), lambda b: (b, 0)), pl.BlockSpec((8, LANE), lambda b: (0, 0))],
        out_shape=[jax.ShapeDtypeStruct((nb * seq, LANE), BF16), jax.ShapeDtypeStruct((8, LANE), F32)],
        compiler_params=_params(("arbitrary",)),
    )(dfq, dfk, zf, fb)


def _gate_outproj(x, gate, oa, ob, oc, zg, w_out_p, seq):
    t = x.shape[0]

    def body(x_ref, gate_ref, oa_ref, ob_ref, oc_ref, zg_ref, w_ref, y_ref, xn_ref):
        y = jnp.zeros((TM, D_MODEL), F32)
        for i, o_ref in enumerate((oa_ref, ob_ref, oc_ref)):
            cat = (o_ref[...] * _silu(zg_ref[:, 384 * i:384 * (i + 1)])).astype(BF16)
            y = y + _dot(cat, w_ref[384 * i:384 * (i + 1), :])
        y_ref[...] = y
        xn_ref[...] = x_ref[...] + gate_ref[0] * y

    return pl.pallas_call(
        body, name="gate_outproj", grid=(t // TM,),
        in_specs=[_row_spec(D_MODEL), _ex_spec(seq // TM), _row_spec(384), _row_spec(384), _row_spec(384),
                  _row_spec(D_CAT), _full_spec((D_CAT, D_MODEL))],
        out_specs=[_row_spec(D_MODEL), _row_spec(D_MODEL)],
        out_shape=[jax.ShapeDtypeStruct((t, D_MODEL), F32)] * 2, compiler_params=_params(("parallel",)),
    )(x, gate, oa, ob, oc, zg, w_out_p)


def _final_loss(x, target, g):
    t = x.shape[0]

    def body(x_ref, t_ref, g_ref, dx_ref, loss_ref, gg_ref):
        @pl.when(pl.program_id(0) == 0)
        def _():
            loss_ref[...] = jnp.zeros_like(loss_ref)
            gg_ref[...] = jnp.zeros_like(gg_ref)

        gv = g_ref[...]
        out, xh, r = _rms(x_ref[...], gv)
        err = out - t_ref[...]
        loss_ref[...] += 0.5 * jnp.sum(jnp.mean(err * err, axis=-1, keepdims=True), axis=0, keepdims=True)
        dout = err / D_MODEL
        gg_ref[...] += jnp.broadcast_to(jnp.sum(dout * xh, axis=0, keepdims=True), gg_ref.shape)
        dx_ref[...] = _rms_bwd(dout, xh, r, gv)

    return pl.pallas_call(
        body, name="final_loss", grid=(t // TM,),
        in_specs=[_row_spec(D_MODEL), _row_spec(D_MODEL), _full_spec((1, D_MODEL))],
        out_specs=[_row_spec(D_MODEL), _full_spec((8, LANE)), _full_spec((8, D_MODEL))],
        out_shape=[jax.ShapeDtypeStruct((t, D_MODEL), F32), jax.ShapeDtypeStruct((8, LANE), F32),
                   jax.ShapeDtypeStruct((8, D_MODEL), F32)],
        compiler_params=_params(("arbitrary",)),
    )(x, target, g)


def _head_masks(kind, rows, dq, h):
    lq = _lane_iota((rows, dq))
    lv = _lane_iota((rows, LANE))
    mq = (lq >= 64 * h) & (lq < 64 * h + 64)
    if kind == "A":
        mq = mq | ((lq >= 128 + 32 * h) & (lq < 160 + 32 * h))
    return mq, (lv >= 64 * h) & (lv < 64 * h + 64)


def _tile_mask(kind, tile):
    row = lax.broadcasted_iota(jnp.int32, (tile, tile), 0)
    col = lax.broadcasted_iota(jnp.int32, (tile, tile), 1)
    return (col >> 6) <= (row >> 6) if kind == "A" else col <= row


def _attn_scale(kind):
    return 96.0 ** -0.5 if kind == "A" else 0.125


BAND_W = BAND_TILES * T_BAND


def _segments(kind, qi, tile):
    r0 = qi * tile
    if kind == "B":
        lo = max(qi - (BAND_TILES - 1), 0) * tile
        return [(lo, r0 + tile, False, lo - (qi - (BAND_TILES - 1)) * tile)]
    return ([(0, r0, False, 0)] if qi else []) + [(r0, r0 + tile, True, 0)]


def _scores(kind, qh, k_ref, aux_ref, h, seg, tile, scale):
    a, b, diag, c0 = seg
    s = _dot_nt(qh, k_ref[a:b, :]) * (scale * LOG2E)
    if kind == "B":
        return s + aux_ref[h, :, c0:BAND_W]
    if kind == "C":
        s = s - aux_ref[0, h, :, a:b]
    if diag:
        s = jnp.where(_tile_mask(kind, tile), s, NEG)
    return s


FWD_AHEAD = 2
BWD_AHEAD = 2


def _run_ahead(units, first, second, depth):
    queue = [first(*u) for u in units[:depth]]
    for i, (_, h) in enumerate(units):
        if i + depth < len(units):
            queue.append(first(*units[i + depth]))
        second(h, *queue.pop(0))


def _attn_fwd(kind, q, k, v, aux, nb, seq):
    dq = q.shape[1] // 3
    tile = T_BAND if kind == "B" else T_CAUSAL
    nq = seq // tile
    scale = _attn_scale(kind)

    def body(*refs):
        if kind == "A":
            q_ref, k_ref, v_ref, o_ref, lse_ref = refs
            aux_ref = None
        else:
            q_ref, k_ref, v_ref, aux_ref, o_ref, lse_ref = refs
        def logits(qi, h):
            rows = slice(qi * tile, (qi + 1) * tile)
            q2 = q_ref[rows, :]
            mq, _ = _head_masks(kind, tile, dq, h)
            qh = jnp.where(mq, q2, jnp.zeros_like(q2))
            segs = _segments(kind, qi, tile)
            return rows, segs, [_scores(kind, qh, k_ref, aux_ref, h, seg, tile, scale) for seg in segs]

        def finish(h, rows, segs, ss):
            _, mv = _head_masks(kind, tile, dq, h)
            mx = functools.reduce(jnp.maximum, [jnp.max(s, axis=-1, keepdims=True) for s in ss])
            ps = [jnp.exp2(s - mx) for s in ss]
            l = functools.reduce(jnp.add, [jnp.sum(p, axis=-1, keepdims=True) for p in ps])
            acc = functools.reduce(jnp.add, [_dot(p.astype(BF16), v_ref[seg[0]:seg[1], :]) for p, seg in zip(ps, segs)])
            o_h = jnp.where(mv, acc / l, 0.0)
            lse_h = jnp.where(mv, mx + jnp.log(l) * LOG2E, 0.0)
            if h == 0:
                o_ref[rows, :] = o_h
                lse_ref[rows, :] = lse_h
            else:
                o_ref[rows, :] += o_h
                lse_ref[rows, :] += lse_h

        _run_ahead([(qi, h) for qi in range(nq) for h in range(2)], logits, finish, FWD_AHEAD)

    def seq_spec(cols):
        return pl.BlockSpec((seq, cols), lambda b, p: (b, p))

    in_specs = [seq_spec(dq), seq_spec(dq), seq_spec(LANE)]
    args = [q, k, v]
    if kind == "B":
        in_specs.append(pl.BlockSpec((2, tile, BAND_W), lambda b, p: (p, 0, 0)))
        args.append(aux)
    if kind == "C":
        in_specs.append(pl.BlockSpec((1, 2, 1, seq), lambda b, p: (b, p, 0, 0)))
        args.append(aux)
    return pl.pallas_call(
        body, name="attn_fwd_" + kind, grid=(nb, 3), in_specs=in_specs, out_specs=[seq_spec(LANE), seq_spec(LANE)],
        out_shape=[jax.ShapeDtypeStruct((nb * seq, 384), F32)] * 2, compiler_params=_params(("parallel", "parallel")),
    )(*args)


def _attn_bwd(kind, q, k, v, o, do, lse, aux, nb, seq):
    dq = q.shape[1] // 3
    tile = T_BAND if kind == "B" else T_CAUSAL
    nq = seq // tile
    scale = _attn_scale(kind)
    dqk_dtype = F32 if kind == "A" else BF16

    def body(*refs):
        dfr_ref = dfq_ref = dbt_ref = aux_ref = None
        if kind == "A":
            q_ref, k_ref, v_ref, o_ref, do_ref, lse_ref, dq_ref, dk_ref, dv_ref, dkt_acc, dvt_acc = refs
        elif kind == "B":
            q_ref, k_ref, v_ref, o_ref, do_ref, lse_ref, aux_ref, dq_ref, dk_ref, dv_ref, dbt_ref, dkt_acc, dvt_acc = refs
        else:
            (q_ref, k_ref, v_ref, o_ref, do_ref, lse_ref, aux_ref, dq_ref, dk_ref, dv_ref, dfr_ref, dfq_ref,
             dkt_acc, dvt_acc) = refs
        dkt_acc[...] = jnp.zeros_like(dkt_acc)
        dvt_acc[...] = jnp.zeros_like(dvt_acc)
        if kind == "C":
            dfr_ref[...] = jnp.zeros_like(dfr_ref)
        if kind == "B":
            @pl.when(pl.program_id(1) == 0)
            def _():
                dbt_ref[...] = jnp.zeros_like(dbt_ref)

        def products(qi, h):
            rows = slice(qi * tile, (qi + 1) * tile)
            q2 = q_ref[rows, :]
            mq, mv = _head_masks(kind, tile, dq, h)
            qh = jnp.where(mq, q2, jnp.zeros_like(q2))
            doh = jnp.where(mv, do_ref[rows, :], 0.0)
            dob = doh.astype(BF16)
            segs = _segments(kind, qi, tile)
            ts = [_scores(kind, qh, k_ref, aux_ref, h, seg, tile, scale) for seg in segs]
            dps = [_dot_nt(dob, v_ref[seg[0]:seg[1], :]) for seg in segs]
            return rows, segs, qh, doh, ts, dps

        def finish(h, rows, segs, qh, doh, ts, dps):
            mq, mv = _head_masks(kind, tile, dq, h)
            qht = qh.astype(F32).T.astype(BF16)
            dobt = doh.T.astype(BF16)
            head_rows = [(64 * h, 64)] + ([(128 + 32 * h, 32)] if kind == "A" else [])
            delta = jnp.sum(doh * o_ref[rows, :], axis=-1, keepdims=True)
            lseh = jnp.max(jnp.where(mv, lse_ref[rows, :], NEG), axis=-1, keepdims=True)
            rs = jnp.zeros((tile, 1), F32)
            dq_h = jnp.zeros((tile, dq), F32)
            for (a, b, _, c0), t, dp in zip(segs, ts, dps):
                p = jnp.exp2(t - lseh)
                ds = p * (dp - delta)
                if kind == "B":
                    dbt_ref[h, :, c0:BAND_W] += ds
                if kind == "C":
                    dfr_ref[0, h, :, a:b] -= jnp.sum(ds, axis=0, keepdims=True)
                    rs = rs + jnp.sum(ds, axis=-1, keepdims=True)
                dss = (ds * scale).astype(BF16)
                dvt_acc[64 * h:64 * h + 64, a:b] += _dot(dobt[64 * h:64 * h + 64, :], p.astype(BF16))
                for r0, n in head_rows:
                    dkt_acc[r0:r0 + n, a:b] += _dot(qht[r0:r0 + n, :], dss)
                dq_h = dq_h + _dot(dss, k_ref[a:b, :])
            dq_h = jnp.where(mq, dq_h, 0.0).astype(dqk_dtype)
            if h == 0:
                dq_ref[rows, :] = dq_h
            else:
                dq_ref[rows, :] += dq_h
            if kind == "C":
                if h == 0:
                    dfq_ref[rows, :] = jnp.where(mv, rs, 0.0)
                else:
                    dfq_ref[rows, :] += jnp.where(mv, rs, 0.0)

        _run_ahead([(qi, h) for qi in range(nq) for h in range(2)], products, finish, BWD_AHEAD)
        for j in range(seq // 256):
            cols = slice(256 * j, 256 * (j + 1))
            dk_ref[cols, :] = dkt_acc[:, cols].T.astype(dqk_dtype)
            dv_ref[cols, :] = dvt_acc[:, cols].T.astype(BF16)

    def seq_spec(cols):
        return pl.BlockSpec((seq, cols), lambda p, b: (b, p))

    in_specs = [seq_spec(dq), seq_spec(dq), seq_spec(LANE), seq_spec(LANE), seq_spec(LANE), seq_spec(LANE)]
    args = [q, k, v, o, do, lse]
    out_specs = [seq_spec(dq), seq_spec(dq), seq_spec(LANE)]
    out_shape = [jax.ShapeDtypeStruct((nb * seq, 3 * dq), dqk_dtype)] * 2 + [jax.ShapeDtypeStruct((nb * seq, 384), BF16)]
    if kind == "B":
        spec = pl.BlockSpec((2, tile, BAND_W), lambda p, b: (p, 0, 0))
        in_specs.append(spec)
        args.append(aux)
        out_specs.append(spec)
        out_shape.append(jax.ShapeDtypeStruct((6, tile, BAND_W), F32))
    if kind == "C":
        spec = pl.BlockSpec((1, 2, 1, seq), lambda p, b: (b, p, 0, 0))
        in_specs.append(spec)
        args.append(aux)
        out_specs += [spec, seq_spec(LANE)]
        out_shape += [jax.ShapeDtypeStruct((nb, 6, 1, seq), F32), jax.ShapeDtypeStruct((nb * seq, 384), F32)]
    return pl.pallas_call(
        body, name="attn_bwd_" + kind, grid=(3, nb), in_specs=in_specs, out_specs=out_specs, out_shape=out_shape,
        scratch_shapes=[pltpu.VMEM((dq, seq), F32), pltpu.VMEM((LANE, seq), F32)],
        compiler_params=_params(("arbitrary", "arbitrary")),
    )(*args)


BIAS_G = 768
BIAS_EDGE = BIAS_G - N_REL


def _bias_line(rel_bias):
    g = jnp.concatenate([jnp.broadcast_to(rel_bias[:, N_REL - 1:], (rel_bias.shape[0], BIAS_EDGE)),
                         jnp.flip(rel_bias, axis=1)], axis=1)
    return jnp.pad(g, ((0, 8 - g.shape[0]), (0, 0)))


def _bias_unline(dg):
    return jnp.flip(dg[:, BIAS_EDGE:], axis=1)


def _bias_expand(g8):
    def body(g_ref, out_ref):
        line = jnp.broadcast_to(g_ref[0] * LOG2E, (T_BAND, BIAS_G))
        slab = pltpu.roll(line, 1, 1, stride=1, stride_axis=0)[:, LANE:BIAS_G]
        row = lax.broadcasted_iota(jnp.int32, (T_BAND, BAND_W), 0)
        col = lax.broadcasted_iota(jnp.int32, (T_BAND, BAND_W), 1)
        hidden = ((row >= 64) & (col < 64)) | ((row < 64) & (col >= BAND_W - 64))
        out_ref[0] = jnp.where(hidden, NEG, slab)

    return pl.pallas_call(
        body, name="bias_expand", grid=(8,), in_specs=[pl.BlockSpec((1, 1, BIAS_G), lambda h: (h, 0, 0))],
        out_specs=pl.BlockSpec((1, T_BAND, BAND_W), lambda h: (h, 0, 0)),
        out_shape=jax.ShapeDtypeStruct((8, T_BAND, BAND_W), F32), compiler_params=_params(("parallel",)),
    )(g8.reshape(8, 1, BIAS_G))


def _bias_reduce(d_slab):
    def body(d_ref, out_ref):
        r = lax.broadcasted_iota(jnp.int32, (T_BAND, T_BAND), 0)
        k = lax.broadcasted_iota(jnp.int32, (T_BAND, T_BAND), 1)
        flip = jnp.where(r + k == T_BAND - 1, 1.0, 0.0).astype(BF16)
        hi, mid, lo = _split3(d_ref[0])
        d_rev = _dot(flip, hi) + _dot(flip, mid) + _dot(flip, lo)
        wide = jnp.concatenate([jnp.zeros((T_BAND, LANE), F32), d_rev, jnp.zeros((T_BAND, 2 * LANE), F32)], axis=1)
        skew = pltpu.roll(wide, 0, 1, stride=1, stride_axis=0)
        dg = jnp.sum(skew, axis=0, keepdims=True)[:, LANE:LANE + BIAS_G]
        lane = _lane_iota((1, BIAS_G))
        clipped = jnp.sum(jnp.where(lane <= BIAS_EDGE, dg, 0.0), axis=1, keepdims=True)
        out_ref[0] = jnp.where(lane == BIAS_EDGE, clipped, dg)

    return pl.pallas_call(
        body, name="bias_reduce", grid=(8,), in_specs=[pl.BlockSpec((1, T_BAND, BAND_W), lambda h: (h, 0, 0))],
        out_specs=pl.BlockSpec((1, 1, BIAS_G), lambda h: (h, 0, 0)),
        out_shape=jax.ShapeDtypeStruct((8, 1, BIAS_G), F32), compiler_params=_params(("parallel",)),
    )(d_slab).reshape(8, BIAS_G)


def _outproj_bwd(dxn, y, gate, oa, ob, oc, zg, w_out_p, w_out_pt, nb, seq):
    t = dxn.shape[0]
    tpe = seq // TM

    def body(dxn_ref, y_ref, gate_ref, oa_ref, ob_ref, oc_ref, zg_ref, w_ref, wt_ref,
             doa_ref, dob_ref, doc_ref, dzg_ref, gw_ref, dgate_ref):
        i = pl.program_id(0)

        @pl.when(i == 0)
        def _():
            gw_ref[...] = jnp.zeros_like(gw_ref)

        @pl.when(i % tpe == 0)
        def _():
            dgate_ref[...] = jnp.zeros_like(dgate_ref)

        dxn_t = dxn_ref[...]
        dgate_ref[0] += jnp.sum(dxn_t * y_ref[...], axis=0, keepdims=True)
        dy = (dxn_t * gate_ref[0]).astype(BF16)
        for gi, (o_ref, do_ref) in enumerate(((oa_ref, doa_ref), (ob_ref, dob_ref), (oc_ref, doc_ref))):
            cols = slice(384 * gi, 384 * (gi + 1))
            u = zg_ref[:, cols]
            o_t = o_ref[...]
            su = _silu(u)
            dcat = _dot(dy, wt_ref[:, cols])
            do_ref[...] = dcat * su
            dzg_ref[:, cols] = (dcat * o_t * _dsilu(u)).astype(BF16)
            gw_ref[cols, :] += _dot_tn((o_t * su).astype(BF16), dy)

    return pl.pallas_call(
        body, name="outproj_bwd", grid=(t // TM,),
        in_specs=[_row_spec(D_MODEL), _row_spec(D_MODEL), _ex_spec(tpe), _row_spec(384), _row_spec(384), _row_spec(384),
                  _row_spec(D_CAT), _full_spec((D_CAT, D_MODEL)), _full_spec((D_MODEL, D_CAT))],
        out_specs=[_row_spec(384), _row_spec(384), _row_spec(384), _row_spec(D_CAT), _full_spec((D_CAT, D_MODEL)),
                   _ex_spec(tpe)],
        out_shape=[jax.ShapeDtypeStruct((t, 384), F32)] * 3 + [jax.ShapeDtypeStruct((t, D_CAT), BF16),
                                                                jax.ShapeDtypeStruct((D_CAT, D_MODEL), F32),
                                                                jax.ShapeDtypeStruct((nb, 1, D_MODEL), F32)],
        compiler_params=_params(("arbitrary",)),
    )(dxn, y, gate, oa, ob, oc, zg, w_out_p, w_out_pt)


def _a_up_bwd(dqa, dka, dva, za, gq, gkv, w_uq_pt, w_ukv_pt, cos, sina, sinb):
    t = za.shape[0]

    def body(dq_ref, dk_ref, dv_ref, za_ref, gq_ref, gkv_ref, wqt_ref, wkvt_ref, cos_ref, sa_ref, sb_ref,
             dza_ref, gwq_ref, gwkv_ref, ggq_ref, ggkv_ref, dqb, dkvb):
        @pl.when(pl.program_id(0) == 0)
        def _():
            gwq_ref[...] = jnp.zeros_like(gwq_ref)
            gwkv_ref[...] = jnp.zeros_like(gwkv_ref)
            ggq_ref[...] = jnp.zeros_like(ggq_ref)
            ggkv_ref[...] = jnp.zeros_like(ggkv_ref)

        cos_t, sa, sb = cos_ref[...], sa_ref[...], sb_ref[...]
        dkpe = jnp.zeros((TM, LANE), F32)
        for p in range(3):
            dqb[:, 256 * p:256 * p + 128] = dq_ref[:, 256 * p:256 * p + 128].astype(BF16)
            dqb[:, 256 * p + 128:256 * p + 256] = _rope_t(dq_ref[:, 256 * p + 128:256 * p + 256], cos_t, sa, sb).astype(BF16)
            dkvb[:, 128 * p:128 * p + 128] = dk_ref[:, 256 * p:256 * p + 128].astype(BF16)
            dkpe = dkpe + dk_ref[:, 256 * p + 128:256 * p + 256]
        dkvb[:, 384:768] = dv_ref[...]
        dkpe = _rope_t(dkpe, cos_t, sa, sb)
        dkpe = jnp.where(_lane_iota((TM, LANE)) < A_ROPE, dkpe + pltpu.roll(dkpe, LANE - 32, 1), 0.0)

        gqv = gq_ref[...]
        cqn, cqh, rq = _rms(za_ref[:, 0:384], gqv)
        dq_t = dqb[...]
        gwq_ref[...] += _dot_tn(cqn.astype(BF16), dq_t)
        dcqn = _dot(dq_t, wqt_ref[...])
        ggq_ref[...] += jnp.broadcast_to(jnp.sum(dcqn * cqh, axis=0, keepdims=True), ggq_ref.shape)
        dza_ref[:, 0:384] = _rms_bwd(dcqn, cqh, rq, gqv).astype(BF16)

        gkvv = gkv_ref[...]
        ckvn, ckvh, rkv = _rms(za_ref[:, 384:640], gkvv)
        dkv_t = dkvb[...]
        gwkv_ref[...] += _dot_tn(ckvn.astype(BF16), dkv_t)
        dckvn = _dot(dkv_t, wkvt_ref[...])
        ggkv_ref[...] += jnp.broadcast_to(jnp.sum(dckvn * ckvh, axis=0, keepdims=True), ggkv_ref.shape)
        dza_ref[:, 384:640] = _rms_bwd(dckvn, ckvh, rkv, gkvv).astype(BF16)
        dza_ref[:, 640:768] = dkpe.astype(BF16)

    return pl.pallas_call(
        body, name="a_up_bwd", grid=(t // TM,),
        in_specs=[_row_spec(768), _row_spec(768), _row_spec(384), _row_spec(768), _full_spec((1, 384)),
                  _full_spec((1, 256)), _full_spec((768, 384)), _full_spec((768, 256)), _row_spec(LANE), _row_spec(LANE),
                  _row_spec(LANE)],
        out_specs=[_row_spec(768), _full_spec((384, 768)), _full_spec((256, 768)), _full_spec((8, 384)),
                   _full_spec((8, 256))],
        out_shape=[jax.ShapeDtypeStruct((t, 768), BF16), jax.ShapeDtypeStruct((384, 768), F32),
                   jax.ShapeDtypeStruct((256, 768), F32), jax.ShapeDtypeStruct((8, 384), F32),
                   jax.ShapeDtypeStruct((8, 256), F32)],
        scratch_shapes=[pltpu.VMEM((TM, 768), BF16), pltpu.VMEM((TM, 768), BF16)],
        compiler_params=_params(("arbitrary",)),
    )(dqa, dka, dva, za, gq, gkv, w_uq_pt, w_ukv_pt, cos, sina, sinb)


def _dz_cols():
    return (Z_A, Z_G) + Z_QKV + (Z_F,)


def _inproj_bwd_dx(dz, dxn, x, shift, scale, g, w_in_pt, nb, seq):
    t = x.shape[0]
    tpe = seq // TM
    cols = _dz_cols()

    def body(*refs):
        dz_refs = refs[:len(cols)]
        dxn_ref, x_ref, sh_ref, sc_ref, g_ref, wt_ref, dx_ref, dsh_ref, dsc_ref, dg_ref = refs[len(cols):]
        i = pl.program_id(0)

        @pl.when(i == 0)
        def _():
            dg_ref[...] = jnp.zeros_like(dg_ref)

        @pl.when(i % tpe == 0)
        def _():
            dsh_ref[...] = jnp.zeros_like(dsh_ref)
            dsc_ref[...] = jnp.zeros_like(dsc_ref)

        dh = jnp.zeros((TM, D_MODEL), F32)
        for ref, (c0, c1) in zip(dz_refs, cols):
            dh = dh + _dot_nt(ref[...], wt_ref[:, c0:c1])
        gv = g_ref[...]
        n, xh, r = _rms(x_ref[...], gv)
        dsh_ref[0] += jnp.sum(dh, axis=0, keepdims=True)
        dsc_ref[0] += jnp.sum(dh * n, axis=0, keepdims=True)
        dn = dh * (1.0 + sc_ref[0])
        dg_ref[...] += jnp.broadcast_to(jnp.sum(dn * xh, axis=0, keepdims=True), dg_ref.shape)
        dx_ref[...] = dxn_ref[...] + _rms_bwd(dn, xh, r, gv)

    in_specs = [_row_spec(c1 - c0) for c0, c1 in cols]
    in_specs += [_row_spec(D_MODEL), _row_spec(D_MODEL), _ex_spec(tpe), _ex_spec(tpe), _full_spec((1, D_MODEL)),
                 _full_spec((D_MODEL, NP_IN))]
    return pl.pallas_call(
        body, name="inproj_bwd_dx", grid=(t // TM,), in_specs=in_specs,
        out_specs=[_row_spec(D_MODEL), _ex_spec(tpe), _ex_spec(tpe), _full_spec((8, D_MODEL))],
        out_shape=[jax.ShapeDtypeStruct((t, D_MODEL), F32), jax.ShapeDtypeStruct((nb, 1, D_MODEL), F32),
                   jax.ShapeDtypeStruct((nb, 1, D_MODEL), F32), jax.ShapeDtypeStruct((8, D_MODEL), F32)],
        compiler_params=_params(("arbitrary",)),
    )(*dz, dxn, x, shift, scale, g, w_in_pt)


def _inproj_bwd_dw(h, dz, name, layer, both=None):
    t = h.shape[0]
    widths = [d.shape[1] for d in dz]
    total = sum(widths)

    def body(*refs):
        h_ref = refs[0]
        dz_refs = refs[1:1 + len(dz)]
        gw_ref = refs[-1]

        @pl.when(pl.program_id(0) == 0)
        def _():
            gw_ref[...] = jnp.zeros_like(gw_ref)

        h_t = h_ref[...]
        c0 = 0
        for ref, w in zip(dz_refs, widths):
            gw_ref[0, :, c0:c0 + w] += _dot_tn(h_t, ref[...])
            c0 += w

    in_specs = [_row_spec(D_MODEL)] + [_row_spec(w) for w in widths]
    args = [h, *dz]
    aliases = {}
    if both is not None:
        in_specs.append(pl.BlockSpec(memory_space=pl.ANY))
        aliases = {len(args): 0}
        args.append(both)
    return pl.pallas_call(
        body, name=name, grid=(t // TM,), in_specs=in_specs,
        out_specs=pl.BlockSpec((1, D_MODEL, total), lambda i: (layer, 0, 0)),
        out_shape=jax.ShapeDtypeStruct((DEPTH, D_MODEL, total), F32), input_output_aliases=aliases,
        compiler_params=_params(("arbitrary",)),
    )(*args)


def _ada_bwd(c_all, dmod_cols, dmod_all):
    n = c_all.shape[0]
    cols = dmod_cols.shape[2]

    def body(c_ref, dc_ref, da_ref, gw_ref, gb_ref):
        act = _silu(c_ref[...]).astype(BF16)
        gw_ref[0] = _dot_tn(act, dc_ref[0].astype(BF16))
        gb_ref[0] = jnp.sum(da_ref[0], axis=0, keepdims=True)

    return pl.pallas_call(
        body, name="ada_bwd", grid=(DEPTH,),
        in_specs=[pl.BlockSpec((n, D_MODEL), lambda l: (0, 0)), pl.BlockSpec((1, n, cols), lambda l: (l, 0, 0)),
                  pl.BlockSpec((1, n, 3 * D_MODEL), lambda l: (l, 0, 0))],
        out_specs=[pl.BlockSpec((1, D_MODEL, cols), lambda l: (l, 0, 0)),
                   pl.BlockSpec((1, 1, 3 * D_MODEL), lambda l: (l, 0, 0))],
        out_shape=[jax.ShapeDtypeStruct((DEPTH, D_MODEL, cols), F32), jax.ShapeDtypeStruct((DEPTH, 1, 3 * D_MODEL), F32)],
        compiler_params=_params(("parallel",)),
    )(c_all, dmod_cols, dmod_all)


def _sum_blocks(parts, name):
    n, rows, cols = parts.shape
    tr = rows if rows <= 256 else 8 * next(d for d in range(32, 0, -1) if (rows // 8) % d == 0)

    def body(p_ref, out_ref):
        acc = p_ref[0].astype(F32)
        for k in range(1, n):
            acc = acc + p_ref[k].astype(F32)
        out_ref[...] = acc

    return pl.pallas_call(
        body, name=name, grid=(rows // tr,), in_specs=[pl.BlockSpec((n, tr, cols), lambda i: (0, i, 0))],
        out_specs=pl.BlockSpec((tr, cols), lambda i: (i, 0)), out_shape=jax.ShapeDtypeStruct((rows, cols), F32),
        compiler_params=_params(("parallel",)),
    )(parts)


def _adamw(w, g, m, v, name):
    nl, rows, cols = w.shape
    if rows * cols <= 64 * 1024:
        tr = rows
        tl = next(t for t in range(nl, 0, -1) if nl % t == 0 and t * max(rows, 8) * cols <= 512 * 1024)
    else:
        tl = 1
        tr = next(t for t in (rows, 256, 128, 64, 32, 16, 8) if rows % t == 0 and t * cols <= 256 * 1024)

    def body(w_ref, g_ref, m_ref, v_ref, d_ref, mo_ref, vo_ref):
        gv = g_ref[...]
        mn = ADAM_B1 * m_ref[...] + (1.0 - ADAM_B1) * gv
        vn = ADAM_B2 * v_ref[...] + (1.0 - ADAM_B2) * jnp.square(gv)
        m_hat = mn / (1.0 - ADAM_B1 ** ADAM_STEP)
        v_hat = vn / (1.0 - ADAM_B2 ** ADAM_STEP)
        d_ref[...] = -ADAM_LR * (m_hat / (jnp.sqrt(v_hat) + ADAM_EPS) + ADAM_WD * w_ref[...])
        mo_ref[...] = mn
        vo_ref[...] = vn

    spec = pl.BlockSpec((tl, tr, cols), lambda l, i: (l, i, 0))
    return pl.pallas_call(
        body, name=name, grid=(nl // tl, rows // tr), in_specs=[spec] * 4, out_specs=[spec] * 3,
        out_shape=[jax.ShapeDtypeStruct((nl, rows, cols), F32)] * 3, compiler_params=_params(("parallel", "parallel")),
    )(w, g, m, v)


ALL_FLIPS = tuple(range(1, N_DEV))


def _transfer(name, srcs, dst_shapes, plan, side=None):
    n_arr = len(srcs)
    probe = plan(0, 0, 0)
    n_steps = len(probe)
    side_in = [a for a, _ in side["inputs"]] if side else []
    side_out = side["outputs"] if side else []

    def body(*refs):
        src_refs, refs = refs[:n_arr], refs[n_arr:]
        side_in_refs, refs = refs[:len(side_in)], refs[len(side_in):]
        dst_refs, refs = refs[:n_arr], refs[n_arr:]
        side_out_refs, refs = refs[:len(side_out)], refs[len(side_out):]
        send_sems, recv_sems, local_sems = refs[:3]
        side_scratch = refs[3:]
        x, y, c = lax.axis_index("x"), lax.axis_index("y"), lax.axis_index("c")
        steps = plan(x, y, c)
        side_done = []

        def rows(ref, r0, n):
            return ref.at[:, pl.ds(r0, n), :]

        def arrival(t):
            a, _, _, n, _, f, _ = steps[t]
            return pltpu.make_async_remote_copy(
                src_ref=rows(dst_refs[a], 0, n), dst_ref=rows(dst_refs[a], 0, n), send_sem=send_sems.at[t],
                recv_sem=recv_sems.at[t], device_id=(x, y, c), device_id_type=pl.DeviceIdType.MESH)

        arrived, started = set(), []
        for t, (a, from_dst, sr, n, dr, f, after) in enumerate(steps):
            if side and t == side["at"]:
                after = tuple(after) + tuple(side["needs"])
            for u in after:
                if u not in arrived:
                    if steps[u][5] == 0:
                        started[u].wait()
                    else:
                        arrival(u).wait_recv()
                    arrived.add(u)
            if side and t == side["at"]:
                side_done.append(side["run"](x, y, c, dst_refs, side_in_refs, side_out_refs, side_scratch))
            src = rows(dst_refs[a] if from_dst else src_refs[a], sr, n)
            dst = rows(dst_refs[a], dr, n)
            if f == 0:
                cp = pltpu.make_async_copy(src, dst, local_sems.at[t])
            else:
                to = (1 - x if f & 4 else x, 1 - y if f & 2 else y, 1 - c if f & 1 else c)
                cp = pltpu.make_async_remote_copy(src_ref=src, dst_ref=dst, send_sem=send_sems.at[t],
                                                  recv_sem=recv_sems.at[t], device_id=to,
                                                  device_id_type=pl.DeviceIdType.MESH)
            cp.start()
            started.append(cp)
        for t, step in enumerate(steps):
            if step[5] != 0 and t not in arrived:
                arrival(t).wait_recv()
        for t, (cp, step) in enumerate(zip(started, steps)):
            if step[5] != 0:
                cp.wait_send()
            elif t not in arrived:
                cp.wait()
        for wait in side_done:
            wait()

    any_spec = pl.BlockSpec(memory_space=pl.ANY)
    return pl.pallas_call(
        body, name=name, out_shape=[jax.ShapeDtypeStruct(s, d) for s, d in list(dst_shapes) + list(side_out)],
        in_specs=[any_spec] * n_arr + [spec for _, spec in (side["inputs"] if side else [])],
        out_specs=[any_spec] * (n_arr + len(side_out)),
        scratch_shapes=[pltpu.SemaphoreType.DMA((n_steps,)), pltpu.SemaphoreType.DMA((n_steps,)),
                        pltpu.SemaphoreType.DMA((n_steps,))] + (side["scratch"] if side else []),
    )(*srcs, *side_in)


CHIP_FLIPS = (2, 4, 6)


def _gather_plan(chip_rows):
    def plan(x, y, c):
        steps = []
        chip = 2 * x + y
        for a, rc in enumerate(chip_rows):
            h = rc // 2
            first = (h // 32) * 16
            mine = rc * chip + h * c
            from_x, from_y, diag = rc * (chip ^ 2) + h * c, rc * (chip ^ 1) + h * c, rc * (chip ^ 3) + h * c
            steps.append((a, False, h * c, h, mine, 0, ()))
            to_x = len(steps)
            steps.append((a, False, h * c, h, mine, 4, ()))
            to_y = len(steps)
            steps.append((a, False, h * c, h, mine, 2, ()))
            fwd_y = len(steps)
            steps.append((a, True, from_x, first, from_x, 2, (to_x,)))
            fwd_x = len(steps)
            steps.append((a, True, from_y + first, h - first, from_y + first, 4, (to_y,)))
            steps.append((a, False, h * c, h, mine, 1, ()))
            steps.append((a, True, from_x, h, from_x, 1, (to_x,)))
            steps.append((a, True, from_y, h, from_y, 1, (to_y,)))
            steps.append((a, True, diag, first, diag, 1, (fwd_y,)))
            steps.append((a, True, diag + first, h - first, diag + first, 1, (fwd_x,)))
        return steps
    return plan


def _pair_reduce_plan(chip_rows):
    def plan(x, y, c):
        steps = []
        for a, rc in enumerate(chip_rows):
            h = rc // 2
            for j in range(4):
                steps.append((a, False, rc * j + h * (1 - c), h, h * j, 1, ()))
        return steps
    return plan


def _chip_scatter_plan(chip_rows):
    def plan(x, y, c):
        steps = []
        for a, rc in enumerate(chip_rows):
            h = rc // 2
            arrivals = []
            for k, f in enumerate(CHIP_FLIPS):
                arrivals.append(len(steps))
                steps.append((a, False, h * ((2 * x + y) ^ (f >> 1)), h, h * k, f, ()))
            steps.append((a, False, h * (2 * x + y), h, 6 * h, 1, ()))
            for k, t in enumerate(arrivals):
                steps.append((a, True, h * k, h, (3 + k) * h, 1, (t,)))
        return steps
    return plan


def _tile_rows(h):
    return h


def _sum_pair(partial, recv, core, rc, name):
    nl, _, cols = partial.shape
    h = rc // 2
    tr = _tile_rows(h)

    def body(c_ref, p_ref, r_ref, out_ref):
        out_ref[...] = (p_ref[...] + r_ref[...]).astype(BF16)

    spec = pl.BlockSpec((1, tr, cols), lambda l, j, i, c_ref: (l, (h // tr) * j + i, 0))
    return pl.pallas_call(
        body, name=name, out_shape=jax.ShapeDtypeStruct((nl, 4 * h, cols), BF16),
        grid_spec=pltpu.PrefetchScalarGridSpec(
            num_scalar_prefetch=1, grid=(nl, 4, h // tr),
            in_specs=[pl.BlockSpec((1, tr, cols), lambda l, j, i, c_ref: (l, (rc // tr) * j + (h // tr) * c_ref[0] + i, 0)),
                      spec],
            out_specs=spec),
        compiler_params=_params(("parallel", "parallel", "parallel")),
    )(core, partial, recv)


def _sum_chips(chip_sum, slots, place, rc, name):
    nl, _, cols = slots.shape
    h = rc // 2

    def body(p_ref, own_ref, other_ref, s_ref, out_ref):
        first = jnp.where(pl.program_id(1) == 0, own_ref[0], other_ref[0, 0])
        acc = first.astype(F32)
        for k in range(3):
            acc = acc + s_ref[0, k].astype(F32)
        out_ref[0] = acc

    view = slots.reshape(nl, 7, h, cols)
    return pl.pallas_call(
        body, name=name, out_shape=jax.ShapeDtypeStruct((nl, rc, cols), F32),
        grid_spec=pltpu.PrefetchScalarGridSpec(
            num_scalar_prefetch=1, grid=(nl, 2),
            in_specs=[pl.BlockSpec((1, h, cols), lambda l, q, p_ref: (l, p_ref[0], 0)),
                      pl.BlockSpec((1, 1, h, cols), lambda l, q, p_ref: (l, 6, 0, 0)),
                      pl.BlockSpec((1, 3, h, cols), lambda l, q, p_ref: (l, q, 0, 0))],
            out_specs=pl.BlockSpec((1, h, cols), lambda l, q, p_ref: (l, q + p_ref[1] - 2 * q * p_ref[1], 0))),
        compiler_params=_params(("parallel", "parallel")),
    )(place, chip_sum, view, view)


def _pad_cols(a, n):
    return a if n == 0 else jnp.pad(a, ((0, 0), (0, n)))


def _in_to_padded(w):
    return jnp.concatenate([_pad_cols(w[:, a:b], z) for a, b, z in IN_PIECES], axis=1)


def _in_cols_first(ga, gb):
    split = Z_G[1]
    pos, out = 0, {}
    for a, b, z in IN_PIECES:
        src, off = (ga, pos) if pos < split else (gb, pos - split)
        out[a] = jnp.transpose(src[:, :, off:off + (b - a)], (2, 0, 1))
        pos += (b - a) + z
    return jnp.concatenate([out[a] for a in sorted(out)], axis=0)


def _out_to_padded(w):
    z = jnp.zeros((64, w.shape[1]), w.dtype)
    return jnp.concatenate([w[0:384], w[384:704], z, w[704:1024], z], axis=0)


def _out_from_padded(gp):
    return jnp.concatenate([gp[0:384], gp[384:704], gp[768:1088]], axis=0)


def _uq_to_padded(w):
    parts = []
    for p in range(3):
        h0, h1 = 2 * p, 2 * p + 1
        parts += [w[:, 96 * h0:96 * h0 + 64], w[:, 96 * h1:96 * h1 + 64], w[:, 96 * h0 + 64:96 * h0 + 96],
                  w[:, 96 * h1 + 64:96 * h1 + 96], jnp.zeros((w.shape[0], 64), w.dtype)]
    return jnp.concatenate(parts, axis=1)


def _uq_from_padded(gp):
    parts = []
    for h in range(6):
        p, s = h // 2, h % 2
        parts += [gp[:, 256 * p + 64 * s:256 * p + 64 * s + 64], gp[:, 256 * p + 128 + 32 * s:256 * p + 160 + 32 * s]]
    return jnp.concatenate(parts, axis=1)


def _ukv_to_padded(w):
    return jnp.concatenate([w[:, 128 * h:128 * h + 64] for h in range(6)]
                           + [w[:, 128 * h + 64:128 * h + 128] for h in range(6)], axis=1)


def _ukv_from_padded(gp):
    parts = []
    for h in range(6):
        parts += [gp[:, 64 * h:64 * h + 64], gp[:, 384 + 64 * h:384 + 64 * h + 64]]
    return jnp.concatenate(parts, axis=1)


LR_ROWS = 224
SHARD_ROWS = (256, 256, 256, LR_ROWS)


def _pack_lowrank(w_uq, w_ukv):
    flat = jnp.concatenate([w_uq.reshape(-1), w_ukv.reshape(-1)])
    return jnp.pad(flat, (0, LR_ROWS * PACK_COLS - flat.shape[0])).reshape(1, LR_ROWS, PACK_COLS)


def _unpack_lowrank(packed):
    flat = packed.reshape(-1)
    n_uq = DEPTH * A_Q_RANK * 144
    n_ukv = DEPTH * A_KV_RANK * 192
    return flat[0:n_uq].reshape(DEPTH, A_Q_RANK, 144), flat[n_uq:n_uq + n_ukv].reshape(DEPTH, A_KV_RANK, 192)


def _rope_tables(positions):
    t = positions.size
    inv = ROPE_THETA ** (-jnp.arange(0, A_ROPE, 2, dtype=F32) / A_ROPE)
    inv_row = jnp.pad(jnp.tile(inv, 4), (0, 64)).reshape(1, LANE)

    def body(p_ref, i_ref, c_ref, sa_ref, sb_ref):
        ang = p_ref[...].astype(F32) * i_ref[...]
        lane = _lane_iota((TM, LANE))
        live = lane < 64
        second = (lane & 31) >= 16
        s = jnp.sin(ang)
        c_ref[...] = jnp.where(live, jnp.cos(ang), 0.0)
        sa_ref[...] = jnp.where(live & jnp.logical_not(second), -s, 0.0)
        sb_ref[...] = jnp.where(live & second, s, 0.0)

    return pl.pallas_call(
        body, name="rope_tables", grid=(t // TM,), in_specs=[_row_spec(1), _full_spec((1, LANE))],
        out_specs=[_row_spec(LANE)] * 3, out_shape=[jax.ShapeDtypeStruct((t, LANE), F32)] * 3,
        compiler_params=_params(("parallel",)),
    )(positions.reshape(t, 1), inv_row)


def _rows(a, n):
    flat = a.reshape(-1)
    return jnp.pad(flat, (0, n * LANE - flat.shape[0])).reshape(n, LANE)


def _forward_backward(x, mod, tables, target, weights, small, nb, seq):
    cos, sina, sinb = tables
    saved = []
    for l in range(DEPTH):
        w, s = weights[l], small[l]
        shift = mod[l][:, None, 0:D_MODEL]
        scale = mod[l][:, None, D_MODEL:2 * D_MODEL]
        gate = mod[l][:, None, 2 * D_MODEL:]
        h, za, zg, qb, kb, vb, qc, kc, vc, zf = _ln_inproj(x, shift, scale, s["norm_g"], w["in"], seq)
        qa, ka, va = _a_up(za, s["gq"], s["gkv"], w["uq"], w["ukv"], cos, sina, sinb)
        bias = _bias_expand(s["g8"])[0:6]
        f = _forget_fwd(zf, s["fb"], nb, seq)
        frow = jnp.pad(f[:, 0:5].reshape(nb, seq, 5).transpose(0, 2, 1), ((0, 0), (0, 1), (0, 0)))
        frow = frow.reshape(nb, 6, 1, seq)
        oa, lse_a = _attn_fwd("A", qa, ka, va, None, nb, seq)
        ob, lse_b = _attn_fwd("B", qb, kb, vb, bias, nb, seq)
        oc, lse_c = _attn_fwd("C", qc, kc, vc, frow, nb, seq)
        y, xn = _gate_outproj(x, gate, oa, ob, oc, zg, w["out"], seq)
        saved.append(dict(x=x, h=h, za=za, zg=zg, zf=zf, y=y, shift=shift, scale=scale, gate=gate, bias=bias, frow=frow,
                          a=(qa, ka, va, oa, lse_a), b=(qb, kb, vb, ob, lse_b), c=(qc, kc, vc, oc, lse_c)))
        x = xn
    dx, loss8, gfinal8 = _final_loss(x, target, small[0]["final_g"])
    grads = []
    gw_in = (None, None)
    for l in reversed(range(DEPTH)):
        w, s, sv = weights[l], small[l], saved[l]
        qa, ka, va, oa, lse_a = sv["a"]
        qb, kb, vb, ob, lse_b = sv["b"]
        qc, kc, vc, oc, lse_c = sv["c"]
        doa, dob, doc, dzg, gw_out, dgate = _outproj_bwd(dx, sv["y"], sv["gate"], oa, ob, oc, sv["zg"], w["out"],
                                                          w["out_t"], nb, seq)
        dqa, dka, dva = _attn_bwd("A", qa, ka, va, oa, doa, lse_a, None, nb, seq)
        dqb, dkb, dvb, dbt = _attn_bwd("B", qb, kb, vb, ob, dob, lse_b, sv["bias"], nb, seq)
        dqc, dkc, dvc, dfr, dfq = _attn_bwd("C", qc, kc, vc, oc, doc, lse_c, sv["frow"], nb, seq)
        dg = _bias_reduce(jnp.pad(dbt, ((0, 2), (0, 0), (0, 0))))
        grb = jnp.pad(_bias_unline(dg), ((0, 0), (0, 384 - N_REL)))
        dfk = dfr.reshape(nb, 6, seq).transpose(0, 2, 1).reshape(nb * seq, 6)
        dzf, gfb = _forget_bwd(dfq, jnp.pad(dfk, ((0, 0), (0, LANE - 6))), sv["zf"], s["fb"], nb, seq)
        dza, gw_uq, gw_ukv, ggq, ggkv = _a_up_bwd(dqa, dka, dva, sv["za"], s["gq"], s["gkv"], w["uq_t"], w["ukv_t"],
                                                  cos, sina, sinb)
        dz = (dza, dzg, dqb, dkb, dvb, dqc, dkc, dvc, dzf)
        dx, dshift, dscale, gnorm = _inproj_bwd_dx(dz, dx, sv["x"], sv["shift"], sv["scale"], s["norm_g"], w["in"],
                                                   nb, seq)
        gw_in = (_inproj_bwd_dw(sv["h"], dz[0:2], "inproj_bwd_dw0", l, gw_in[0]),
                 _inproj_bwd_dw(sv["h"], dz[2:], "inproj_bwd_dw1", l, gw_in[1]))
        dmod = jnp.concatenate([dshift[:, 0], dscale[:, 0], dgate[:, 0]], axis=1)
        grads.append(dict(w_out=gw_out, w_uq=gw_uq, w_ukv=gw_ukv, dmod=dmod, norm_g=gnorm[0], gq=ggq[0],
                          gkv=ggkv[0], rb8=grb, fb=gfb[0]))
    grads.reverse()
    return loss8[0, 0], dx, grads, gfinal8[0], gw_in


def _layer_weights(w_in, w_out, w_uq, w_ukv):
    wi, wo, wq, wkv = _in_to_padded(w_in), _out_to_padded(w_out), _uq_to_padded(w_uq), _ukv_to_padded(w_ukv)
    return {"in": wi, "out": wo, "out_t": wo.T, "uq": wq, "uq_t": wq.T, "ukv": wkv, "ukv_t": wkv.T}


def _layer_small(norm_g, gq, gkv, rel_bias, forget_b, final_g):
    fb = jnp.pad(forget_b, (0, LANE - 5)).reshape(1, LANE)
    return dict(norm_g=norm_g.reshape(1, -1), gq=gq.reshape(1, -1), gkv=gkv.reshape(1, -1), g8=_bias_line(rel_bias), fb=fb,
                final_g=final_g.reshape(1, -1))


def _small_payload(per_layer, final_g, loss):
    def stack(key):
        return jnp.stack([p[key] for p in per_layer])

    def rows(a, rng):
        return _rows(a, rng[1] - rng[0])

    parts = [rows(stack("dmod"), PAY_DMOD), rows(stack("norm_g"), PAY_NORM), rows(stack("gq"), PAY_GQ),
             rows(stack("gkv"), PAY_GKV), rows(stack("rb8"), PAY_RB), rows(stack("fb"), PAY_FB),
             rows(final_g, PAY_FINAL), rows(loss, PAY_LOSS)]
    return jnp.concatenate(parts, axis=0)


def _payload_split(pay):
    def take(rng, shape):
        n = 1
        for d in shape:
            n *= d
        return pay[rng[0]:rng[1]].reshape(-1)[0:n].reshape(shape)

    norm_g = take(PAY_NORM, (DEPTH, D_MODEL))
    gq = take(PAY_GQ, (DEPTH, A_Q_RANK))
    gkv = take(PAY_GKV, (DEPTH, A_KV_RANK))
    rb = take(PAY_RB, (DEPTH, 8, 384))[:, 0:5, 0:N_REL]
    fb = take(PAY_FB, (DEPTH, LANE))[:, 0:5]
    final_g = take(PAY_FINAL, (D_MODEL,))
    return norm_g, gq, gkv, rb, fb, final_g


def kernel(x, c, positions, w_ada, b_ada, norm_g, w_in, a_q_norm_g, a_w_uq, a_kv_norm_g, a_w_ukv, b_rel_bias, c_forget_b, w_out, final_g, loss_target, m_w_ada, m_b_ada, m_norm_g, m_w_in, m_a_q_norm_g, m_a_w_uq, m_a_kv_norm_g, m_a_w_ukv, m_b_rel_bias, m_c_forget_b, m_w_out, m_final_g, v_w_ada, v_b_ada, v_norm_g, v_w_in, v_a_q_norm_g, v_a_w_uq, v_a_kv_norm_g, v_a_w_ukv, v_b_rel_bias, v_c_forget_b, v_w_out, v_final_g):
    nb, seq, _ = x.shape
    ix, iy, ic = lax.axis_index("x"), lax.axis_index("y"), lax.axis_index("c")
    chip = 2 * ix + iy

    weight_plan = _gather_plan((256, 256, LR_ROWS))
    cols = w_ada.shape[2]

    def gather_plan(x, y, c_):
        me_ = 4 * x + 2 * y + c_
        first = [(3, False, 0, 8, 8 * me_, f, ()) for f in range(N_DEV)]
        rest = weight_plan(x, y, c_)
        order = [i for i, s in enumerate(rest) if not s[6]] + [i for i, s in enumerate(rest) if s[6]]
        place = {old: len(first) + new for new, old in enumerate(order)}
        return first + [rest[i][:6] + (tuple(place[u] for u in rest[i][6]),) for i in order]

    probe = gather_plan(0, 0, 0)

    def project(x, y, c_, dst_refs, in_refs, out_refs, scratch):
        w_ref, b_ref = in_refs
        (mod_ref,) = out_refs
        c_vm, mod_vm, send_m, recv_m, local_m = scratch
        me_ = 4 * x + 2 * y + c_
        load = pltpu.make_async_copy(dst_refs[3].at[0], c_vm, local_m.at[0])
        load.start()
        load.wait()
        act = _silu(c_vm[...]).astype(BF16)
        for l in range(DEPTH):
            mod_vm[l] = _dot(act, w_ref[l].astype(BF16)) + b_ref[l]

        def rows_of(j):
            return mod_vm.at[:, pl.ds(pl.multiple_of(8 * j, 8), 8), :]

        def send(i, f, slot, j):
            to = (1 - x if f & 4 else x, 1 - y if f & 2 else y, 1 - c_ if f & 1 else c_)
            return pltpu.make_async_remote_copy(src_ref=rows_of(j), dst_ref=mod_ref.at[slot], send_sem=send_m.at[i],
                                                recv_sem=recv_m.at[i], device_id=to, device_id_type=pl.DeviceIdType.MESH)

        own = pltpu.make_async_copy(rows_of(me_), mod_ref.at[me_], local_m.at[1])
        own.start()
        sends = [send(i, f, me_, me_ ^ f) for i, f in enumerate(ALL_FLIPS)]
        for s in sends:
            s.start()

        def finish():
            for i, f in enumerate(ALL_FLIPS):
                send(i, f, me_ ^ f, 0).wait_recv()
            for s in sends:
                s.wait_send()
            own.wait()
        return finish

    b_cols = lax.dynamic_slice_in_dim(b_ada, chip * cols, cols, axis=1)[:, None, :]
    vmem = pl.BlockSpec(memory_space=pltpu.VMEM)
    side = dict(at=next(t for t, s in enumerate(probe) if s[6]), needs=tuple(range(N_DEV)),
                inputs=[(w_ada, vmem), (b_cols, vmem)], outputs=[((N_DEV, DEPTH, 8, cols), F32)],
                scratch=[pltpu.VMEM((8 * N_DEV, D_MODEL), F32), pltpu.VMEM((DEPTH, 8 * N_DEV, cols), F32),
                         pltpu.SemaphoreType.DMA((len(ALL_FLIPS),)), pltpu.SemaphoreType.DMA((len(ALL_FLIPS),)),
                         pltpu.SemaphoreType.DMA((2,))],
                run=project)
    full_in, full_out, full_lr, c_rows, mod_rows = _transfer(
        "gather_weights", [w_in.astype(BF16), w_out.astype(BF16), _pack_lowrank(a_w_uq, a_w_ukv).astype(BF16),
                           jnp.pad(c, ((0, 8 - nb), (0, 0)))[None]],
        [((DEPTH, D_MODEL, N_IN), BF16), ((DEPTH, D_MODEL, D_MODEL), BF16), ((1, 4 * LR_ROWS, PACK_COLS), BF16),
         ((1, 8 * N_DEV, D_MODEL), F32)], gather_plan, side)
    lowrank = [_unpack_lowrank(full_lr[0, LR_ROWS * j:LR_ROWS * (j + 1)]) for j in range(4)]
    full_uq = jnp.concatenate([s[0] for s in lowrank], axis=2)
    full_ukv = jnp.concatenate([s[1] for s in lowrank], axis=2)
    weights = [_layer_weights(full_in[l], full_out[l], full_uq[l], full_ukv[l]) for l in range(DEPTH)]
    small = [_layer_small(norm_g[l], a_q_norm_g[l], a_kv_norm_g[l], b_rel_bias[l], c_forget_b[l], final_g)
             for l in range(DEPTH)]

    c_all = c_rows.reshape(N_DEV, 8, D_MODEL)[:, 0:nb].reshape(N_DEV * nb, D_MODEL)
    mod = jnp.concatenate([mod_rows[2 * j][:, 0:nb] for j in range(4)], axis=2)

    tables = _rope_tables(positions)
    loss_part, dx, grads, gfinal, gw_in = _forward_backward(
        x.reshape(nb * seq, D_MODEL), mod, tables, loss_target.reshape(nb * seq, D_MODEL), weights, small, nb, seq)

    g_uq = jnp.stack([_uq_from_padded(g["w_uq"]) for g in grads])
    g_ukv = jnp.stack([_ukv_from_padded(g["w_ukv"]) for g in grads])
    g_lr = jnp.concatenate([_pack_lowrank(g_uq[:, :, 144 * j:144 * (j + 1)], g_ukv[:, :, 192 * j:192 * (j + 1)])
                            for j in range(4)], axis=1)
    partials = [gw_in[0], gw_in[1], jnp.stack([_out_from_padded(g["w_out"]) for g in grads]), g_lr]
    shapes = [(p.shape[0], p.shape[2]) for p in partials]
    halves = [r // 2 for r in SHARD_ROWS]
    core_s = jnp.reshape(ic, (1,)).astype(jnp.int32)
    pay = _small_payload(grads, gfinal, loss_part)
    pay_rows = pay.shape[0]
    reduce_plan = _pair_reduce_plan(SHARD_ROWS)

    def reduce_plan_with_payload(x, y, c_):
        me_ = 4 * x + 2 * y + c_
        return reduce_plan(x, y, c_) + [(len(partials), False, 0, pay_rows, pay_rows * me_, f, ()) for f in range(N_DEV)]

    *from_pair, pay_all = _transfer(
        "pair_reduce", partials + [pay[None]],
        [((nl, 4 * h, nc), F32) for (nl, nc), h in zip(shapes, halves)] + [((1, N_DEV * pay_rows, LANE), F32)],
        reduce_plan_with_payload)
    pay_all = pay_all.reshape(N_DEV, pay_rows, LANE)

    tot = _sum_blocks(pay_all, "sum_small")
    loss = tot[PAY_LOSS[0], 0]
    dmod_all = pay_all[:, PAY_DMOD[0]:PAY_DMOD[1]].reshape(N_DEV, -1)[:, 0:DEPTH * nb * 3 * D_MODEL]
    dmod_all = dmod_all.reshape(N_DEV, DEPTH, nb, 3 * D_MODEL).transpose(1, 0, 2, 3)
    dmod_all = dmod_all.reshape(DEPTH, N_DEV * nb, 3 * D_MODEL)
    my_cols = lax.dynamic_slice_in_dim(dmod_all, chip * cols, cols, axis=2)
    g_w_ada, g_b_ada = _ada_bwd(c_all, my_cols, dmod_all)
    g_b_ada = g_b_ada[:, 0]
    chip_sums = [_sum_pair(p, r, core_s, rc, "sum_pair%d" % i)
                 for i, (p, r, rc) in enumerate(zip(partials, from_pair, SHARD_ROWS))]
    slots = _transfer("chip_scatter", chip_sums, [((nl, 7 * h, nc), BF16) for (nl, nc), h in zip(shapes, halves)],
                      _chip_scatter_plan(SHARD_ROWS))
    place_s = jnp.stack([chip, ic]).astype(jnp.int32)
    g_in_a, g_in_b, g_out_sh, g_lr_sh = [_sum_chips(cs, s, place_s, rc, "sum_chips%d" % i)
                                         for i, (cs, s, rc) in enumerate(zip(chip_sums, slots, SHARD_ROWS))]
    g_uq_sh, g_ukv_sh = _unpack_lowrank(g_lr_sh[0])

    def cols_first(a):
        return jnp.transpose(a, (2, 0, 1))

    g_in_t = _in_cols_first(g_in_a, g_in_b)
    upd_in = tuple(jnp.transpose(a, (1, 2, 0)) for a in _adamw(cols_first(w_in), g_in_t, cols_first(m_w_in),
                                                               cols_first(v_w_in), "adamw_in"))
    gw = (jnp.transpose(g_in_t, (1, 2, 0)), g_out_sh, g_uq_sh, g_ukv_sh)
    upd = [upd_in, _adamw(w_out, gw[1], m_w_out, v_w_out, "adamw_out"),
           _adamw(a_w_uq, gw[2], m_a_w_uq, v_a_w_uq, "adamw_uq"), _adamw(a_w_ukv, gw[3], m_a_w_ukv, v_a_w_ukv, "adamw_ukv")]
    dw, mw, vw = (tuple(u[i] for u in upd) for i in range(3))
    d_ada, m_ada, v_ada = _adamw(w_ada, g_w_ada, m_w_ada, v_w_ada, "adamw_ada")

    def adam_small(w, g, m, v, name):
        shape3 = (1,) * (3 - w.ndim) + w.shape
        return tuple(a.reshape(w.shape) for a in _adamw(w.reshape(shape3), g.reshape(shape3), m.reshape(shape3),
                                                        v.reshape(shape3), name))

    d_b, m_b, v_b = adam_small(b_ada, g_b_ada, m_b_ada, v_b_ada, "adamw_b_ada")
    gs = _payload_split(tot)
    small_upd = [adam_small(w, g, m, v, "adamw_small%d" % i) for i, (w, g, m, v) in enumerate(zip(
        (norm_g, a_q_norm_g, a_kv_norm_g, b_rel_bias, c_forget_b, final_g), gs,
        (m_norm_g, m_a_q_norm_g, m_a_kv_norm_g, m_b_rel_bias, m_c_forget_b, m_final_g),
        (v_norm_g, v_a_q_norm_g, v_a_kv_norm_g, v_b_rel_bias, v_c_forget_b, v_final_g)))]
    ds, ms, vs = (tuple(u[i] for u in small_upd) for i in range(3))

    def ordered(ada, b, sm, big):
        ng, gq, gkv, rb, fb, fg = sm
        b_in, b_out, b_uq, b_ukv = big
        return (ada, b, ng, b_in, gq, b_uq, gkv, b_ukv, rb, fb, b_out, fg)

    return (loss, dx.reshape(nb, seq, D_MODEL), *ordered(g_w_ada, g_b_ada, gs, gw), *ordered(d_ada, d_b, ds, dw),
            *ordered(m_ada, m_b, ms, mw), *ordered(v_ada, v_b, vs, vw))
```

```python
import functools

import jax
import jax.numpy as jnp
from jax import lax
from jax.experimental import pallas as pl
from jax.experimental.pallas import tpu as pltpu

F32 = jnp.float32
BF16 = jnp.bfloat16

D_MODEL = 1024
DEPTH = 2
EPS = 1e-6
NEG = -1e30
LOG2E = 1.4426950408889634
ROPE_THETA = 10000.0
A_ROPE = 32
A_Q_RANK = 384
A_KV_RANK = 256
REL_CLIP = 128
N_REL = 2 * REL_CLIP + 1
N_IN = 3621

ADAM_LR = 0.001
ADAM_B1 = 0.9
ADAM_B2 = 0.999
ADAM_EPS = 1e-08
ADAM_WD = 0.01
ADAM_STEP = 10

LANE = 128
VMEM_LIMIT = 56 * 1024 * 1024

NP_IN = 4352
Z_A = (0, 768)
Z_G = (768, 1920)
Z_QKV = tuple((1920 + 384 * i, 1920 + 384 * (i + 1)) for i in range(6))
Z_F = (4224, 4352)
IN_PIECES = ((0, 672, 96), (672, 1056, 0), (2016, 2336, 64), (3301, 3621, 64), (1056, 1376, 64), (1376, 1696, 64),
             (1696, 2016, 64), (2336, 2656, 64), (2656, 2976, 64), (2976, 3296, 64), (3296, 3301, 123))
D_CAT = 1152

TM = 512
T_CAUSAL = 256
T_BAND = 128
BAND_TILES = 5
N_DEV = 8

PAY_DMOD = (0, 96)
PAY_NORM = (96, 112)
PAY_GQ = (112, 120)
PAY_GKV = (120, 128)
PAY_RB = (128, 176)
PAY_FB = (176, 184)
PAY_FINAL = (184, 192)
PAY_LOSS = (192, 200)

PACK_COLS = 1024


def _params(sem=None):
    return pltpu.CompilerParams(dimension_semantics=sem, vmem_limit_bytes=VMEM_LIMIT)


def _lane_iota(shape):
    return lax.broadcasted_iota(jnp.int32, shape, len(shape) - 1)


def _silu(u):
    return u * jax.nn.sigmoid(u)


def _dsilu(u):
    s = jax.nn.sigmoid(u)
    return s * (1.0 + u * (1.0 - s))


def _rms(x, g):
    r = lax.rsqrt(jnp.mean(x * x, axis=-1, keepdims=True) + EPS)
    xh = x * r
    return xh * g, xh, r


def _rms_bwd(dy, xh, r, g):
    dxh = dy * g
    return r * (dxh - xh * jnp.mean(dxh * xh, axis=-1, keepdims=True))


def _rope(x, cos, sina, sinb):
    return x * cos + pltpu.roll(x, 16, 1) * sinb + pltpu.roll(x, LANE - 16, 1) * sina


def _rope_t(dy, cos, sina, sinb):
    return dy * cos + pltpu.roll(dy * sinb, LANE - 16, 1) + pltpu.roll(dy * sina, 16, 1)


def _split3(x):
    hi = x.astype(BF16)
    r1 = x - hi.astype(F32)
    mid = r1.astype(BF16)
    lo = (r1 - mid.astype(F32)).astype(BF16)
    return hi, mid, lo


def _dot(a, b):
    return jnp.dot(a, b, preferred_element_type=F32)


def _dot_nt(a, b):
    return lax.dot_general(a, b, (((1,), (1,)), ((), ())), preferred_element_type=F32)


def _dot_tn(a, b):
    return lax.dot_general(a, b, (((0,), (0,)), ((), ())), preferred_element_type=F32)


def _row_spec(cols):
    return pl.BlockSpec((TM, cols), lambda i: (i, 0))


def _full_spec(shape):
    return pl.BlockSpec(shape, lambda i: (0,) * len(shape))


def _ex_spec(tiles_per_ex):
    return pl.BlockSpec((1, 1, D_MODEL), lambda i: (i // tiles_per_ex, 0, 0))


def _ln_inproj(x, shift, scale, g, w_in_p, gq, gkv, w_uq_p, w_ukv_p, cos, sina, sinb, seq):
    t = x.shape[0]

    def body(x_ref, sh_ref, sc_ref, g_ref, w_ref, gq_ref, gkv_ref, wq_ref, wkv_ref, cos_ref, sa_ref, sb_ref,
             h_ref, za_ref, zg_ref, q0, q1, q2, q3, q4, q5, zf_ref, qa_ref, ka_ref, va_ref):
        n, _, _ = _rms(x_ref[...], g_ref[...])
        h = (n * (1.0 + sc_ref[0]) + sh_ref[0]).astype(BF16)
        h_ref[...] = h
        za = _dot(h, w_ref[:, Z_A[0]:Z_A[1]])
        za_ref[...] = za
        zg_ref[...] = _dot(h, w_ref[:, Z_G[0]:Z_G[1]])
        for ref, (c0, c1) in zip((q0, q1, q2, q3, q4, q5), Z_QKV):
            ref[...] = _dot(h, w_ref[:, c0:c1]).astype(BF16)
        zf_ref[...] = _dot(h, w_ref[:, Z_F[0]:Z_F[1]])

        cos_t, sa, sb = cos_ref[...], sa_ref[...], sb_ref[...]
        cqn, _, _ = _rms(za[:, 0:384], gq_ref[...])
        q = _dot(cqn.astype(BF16), wq_ref[...])
        ckvn, _, _ = _rms(za[:, 384:640], gkv_ref[...])
        kv = _dot(ckvn.astype(BF16), wkv_ref[...])
        kpe = za[:, 640:768]
        kpe = _rope(kpe + pltpu.roll(kpe, 32, 1), cos_t, sa, sb).astype(BF16)
        for p in range(3):
            qa_ref[:, 256 * p:256 * p + 128] = q[:, 256 * p:256 * p + 128].astype(BF16)
            qa_ref[:, 256 * p + 128:256 * p + 256] = _rope(q[:, 256 * p + 128:256 * p + 256], cos_t, sa, sb).astype(BF16)
            ka_ref[:, 256 * p:256 * p + 128] = kv[:, 128 * p:128 * p + 128].astype(BF16)
            ka_ref[:, 256 * p + 128:256 * p + 256] = kpe
        va_ref[...] = kv[:, 384:768].astype(BF16)

    tpe = seq // TM
    shapes = [jax.ShapeDtypeStruct((t, D_MODEL), BF16), jax.ShapeDtypeStruct((t, 768), F32),
              jax.ShapeDtypeStruct((t, D_CAT), F32)]
    shapes += [jax.ShapeDtypeStruct((t, 384), BF16)] * 6 + [jax.ShapeDtypeStruct((t, LANE), F32)]
    shapes += [jax.ShapeDtypeStruct((t, 768), BF16), jax.ShapeDtypeStruct((t, 768), BF16), jax.ShapeDtypeStruct((t, 384), BF16)]
    return pl.pallas_call(
        body, name="ln_inproj", grid=(t // TM,),
        in_specs=[_row_spec(D_MODEL), _ex_spec(tpe), _ex_spec(tpe), _full_spec((1, D_MODEL)),
                  _full_spec((D_MODEL, NP_IN)), _full_spec((1, 384)), _full_spec((1, 256)), _full_spec((384, 768)),
                  _full_spec((256, 768)), _row_spec(LANE), _row_spec(LANE), _row_spec(LANE)],
        out_specs=[_row_spec(D_MODEL), _row_spec(768), _row_spec(D_CAT)] + [_row_spec(384)] * 6
        + [_row_spec(LANE), _row_spec(768), _row_spec(768), _row_spec(384)],
        out_shape=shapes, compiler_params=_params(("parallel",)),
    )(x, shift, scale, g, w_in_p, gq, gkv, w_uq_p, w_ukv_p, cos, sina, sinb)


def _tri(n, upper):
    r = lax.broadcasted_iota(jnp.int32, (n, n), 0)
    c = lax.broadcasted_iota(jnp.int32, (n, n), 1)
    return jnp.where((c >= r) if upper else (c <= r), 1.0, 0.0).astype(BF16)


def _forget_fwd(zf, fb, nb, seq):
    blk = 256

    def body(zf_ref, fb_ref, f_ref):
        tri = _tri(blk, False)
        live = _lane_iota((blk, LANE)) < 5
        carry = jnp.zeros((1, LANE), F32)
        for i in range(seq // blk):
            u = zf_ref[i * blk:(i + 1) * blk, :] + fb_ref[...]
            lf = jnp.where(live, jnp.minimum(u, 0.0) - jnp.log(1.0 + jnp.exp(-jnp.abs(u))), 0.0)
            hi, mid, lo = _split3(lf)
            f_ref[i * blk:(i + 1) * blk, :] = (_dot(tri, hi) + _dot(tri, mid) + _dot(tri, lo) + carry) * LOG2E
            carry = carry + jnp.sum(lf, axis=0, keepdims=True)

    return pl.pallas_call(
        body, name="forget_fwd", grid=(nb,),
        in_specs=[pl.BlockSpec((seq, LANE), lambda b: (b, 0)), pl.BlockSpec((1, LANE), lambda b: (0, 0))],
        out_specs=pl.BlockSpec((seq, LANE), lambda b: (b, 0)),
        out_shape=jax.ShapeDtypeStruct((nb * seq, LANE), F32), compiler_params=_params(("parallel",)),
    )(zf, fb)


def _forget_bwd(dfq, dfk, zf, fb, nb, seq):
    blk = 256

    def body(dfq_ref, dfk_ref, zf_ref, fb_ref, dz_ref, gb_ref):
        @pl.when(pl.program_id(0) == 0)
        def _():
            gb_ref[...] = jnp.zeros_like(gb_ref)

        tri = _tri(blk, True)
        lane = _lane_iota((blk, LANE))
        wide = _lane_iota((blk, 384))
        live = lane < 5
        carry = jnp.zeros((1, LANE), F32)
        gsum = jnp.zeros((1, LANE), F32)
        for i in reversed(range(seq // blk)):
            d = dfk_ref[i * blk:(i + 1) * blk, :]
            dq = dfq_ref[i * blk:(i + 1) * blk, :]
            for hd in range(5):
                col = jnp.sum(jnp.where(wide == 64 * hd, dq, 0.0), axis=-1, keepdims=True)
                d = d + jnp.where(lane == hd, col, 0.0)
            hi, mid, lo = _split3(d)
            dlf = _dot(tri, hi) + _dot(tri, mid) + _dot(tri, lo) + carry
            carry = carry + jnp.sum(d, axis=0, keepdims=True)
            u = zf_ref[i * blk:(i + 1) * blk, :] + fb_ref[...]
            du = jnp.where(live, dlf * jax.nn.sigmoid(-u), 0.0)
            dz_ref[i * blk:(i + 1) * blk, :] = du.astype(BF16)
            gsum = gsum + jnp.sum(du, axis=0, keepdims=True)
        gb_ref[...] += jnp.broadcast_to(gsum, gb_ref.shape)

    return pl.pallas_call(
        body, name="forget_bwd", grid=(nb,),
        in_specs=[pl.BlockSpec((seq, 384), lambda b: (b, 0)), pl.BlockSpec((seq, LANE), lambda b: (b, 0)),
                  pl.BlockSpec((seq, LANE), lambda b: (b, 0)), pl.BlockSpec((1, LANE), lambda b: (0, 0))],
        out_specs=[pl.BlockSpec((seq, LANE), lambda b: (b, 0)), pl.BlockSpec((8, LANE), lambda b: (0, 0))],
        out_shape=[jax.ShapeDtypeStruct((nb * seq, LANE), BF16), jax.ShapeDtypeStruct((8, LANE), F32)],
        compiler_params=_params(("arbitrary",)),
    )(dfq, dfk, zf, fb)


def _gate_outproj(x, gate, oa, ob, oc, zg, w_out_p, seq):
    t = x.shape[0]

    def body(x_ref, gate_ref, oa_ref, ob_ref, oc_ref, zg_ref, w_ref, y_ref, xn_ref):
        y = jnp.zeros((TM, D_MODEL), F32)
        for i, o_ref in enumerate((oa_ref, ob_ref, oc_ref)):
            cat = (o_ref[...] * _silu(zg_ref[:, 384 * i:384 * (i + 1)])).astype(BF16)
            y = y + _dot(cat, w_ref[384 * i:384 * (i + 1), :])
        y_ref[...] = y
        xn_ref[...] = x_ref[...] + gate_ref[0] * y

    return pl.pallas_call(
        body, name="gate_outproj", grid=(t // TM,),
        in_specs=[_row_spec(D_MODEL), _ex_spec(seq // TM), _row_spec(384), _row_spec(384), _row_spec(384),
                  _row_spec(D_CAT), _full_spec((D_CAT, D_MODEL))],
        out_specs=[_row_spec(D_MODEL), _row_spec(D_MODEL)],
        out_shape=[jax.ShapeDtypeStruct((t, D_MODEL), F32)] * 2, compiler_params=_params(("parallel",)),
    )(x, gate, oa, ob, oc, zg, w_out_p)


def _final_loss(x, target, g):
    t = x.shape[0]

    def body(x_ref, t_ref, g_ref, dx_ref, loss_ref, gg_ref):
        @pl.when(pl.program_id(0) == 0)
        def _():
            loss_ref[...] = jnp.zeros_like(loss_ref)
            gg_ref[...] = jnp.zeros_like(gg_ref)

        gv = g_ref[...]
        out, xh, r = _rms(x_ref[...], gv)
        err = out - t_ref[...]
        loss_ref[...] += 0.5 * jnp.sum(jnp.mean(err * err, axis=-1, keepdims=True), axis=0, keepdims=True)
        dout = err / D_MODEL
        gg_ref[...] += jnp.broadcast_to(jnp.sum(dout * xh, axis=0, keepdims=True), gg_ref.shape)
        dx_ref[...] = _rms_bwd(dout, xh, r, gv)

    return pl.pallas_call(
        body, name="final_loss", grid=(t // TM,),
        in_specs=[_row_spec(D_MODEL), _row_spec(D_MODEL), _full_spec((1, D_MODEL))],
        out_specs=[_row_spec(D_MODEL), _full_spec((8, LANE)), _full_spec((8, D_MODEL))],
        out_shape=[jax.ShapeDtypeStruct((t, D_MODEL), F32), jax.ShapeDtypeStruct((8, LANE), F32),
                   jax.ShapeDtypeStruct((8, D_MODEL), F32)],
        compiler_params=_params(("arbitrary",)),
    )(x, target, g)


def _head_masks(kind, rows, dq, h):
    lq = _lane_iota((rows, dq))
    lv = _lane_iota((rows, LANE))
    mq = (lq >= 64 * h) & (lq < 64 * h + 64)
    if kind == "A":
        mq = mq | ((lq >= 128 + 32 * h) & (lq < 160 + 32 * h))
    return mq, (lv >= 64 * h) & (lv < 64 * h + 64)


def _tile_mask(kind, tile):
    row = lax.broadcasted_iota(jnp.int32, (tile, tile), 0)
    col = lax.broadcasted_iota(jnp.int32, (tile, tile), 1)
    return (col >> 6) <= (row >> 6) if kind == "A" else col <= row


def _attn_scale(kind):
    return 96.0 ** -0.5 if kind == "A" else 0.125


BAND_W = BAND_TILES * T_BAND


def _segments(kind, qi, tile):
    r0 = qi * tile
    if kind == "B":
        lo = max(qi - (BAND_TILES - 1), 0) * tile
        return [(lo, r0 + tile, False, lo - (qi - (BAND_TILES - 1)) * tile)]
    return ([(0, r0, False, 0)] if qi else []) + [(r0, r0 + tile, True, 0)]


def _scores(kind, qh, k_ref, aux_ref, h, seg, tile, scale):
    a, b, diag, c0 = seg
    s = _dot_nt(qh, k_ref[a:b, :]) * (scale * LOG2E)
    if kind == "B":
        return s + aux_ref[h, :, c0:BAND_W]
    if kind == "C":
        s = s - aux_ref[0, h, :, a:b]
    if diag:
        s = jnp.where(_tile_mask(kind, tile), s, NEG)
    return s


FWD_AHEAD = 2
BWD_AHEAD = 2


def _run_ahead(units, first, second, depth):
    queue = [first(*u) for u in units[:depth]]
    for i, (_, h) in enumerate(units):
        if i + depth < len(units):
            queue.append(first(*units[i + depth]))
        second(h, *queue.pop(0))


def _attn_fwd(kind, q, k, v, aux, nb, seq):
    dq = q.shape[1] // 3
    tile = T_BAND if kind == "B" else T_CAUSAL
    nq = seq // tile
    scale = _attn_scale(kind)

    def body(*refs):
        if kind == "A":
            q_ref, k_ref, v_ref, o_ref, lse_ref = refs
            aux_ref = None
        else:
            q_ref, k_ref, v_ref, aux_ref, o_ref, lse_ref = refs
        def logits(qi, h):
            rows = slice(qi * tile, (qi + 1) * tile)
            q2 = q_ref[rows, :]
            mq, _ = _head_masks(kind, tile, dq, h)
            qh = jnp.where(mq, q2, jnp.zeros_like(q2))
            segs = _segments(kind, qi, tile)
            return rows, segs, [_scores(kind, qh, k_ref, aux_ref, h, seg, tile, scale) for seg in segs]

        def finish(h, rows, segs, ss):
            _, mv = _head_masks(kind, tile, dq, h)
            mx = functools.reduce(jnp.maximum, [jnp.max(s, axis=-1, keepdims=True) for s in ss])
            ps = [jnp.exp2(s - mx) for s in ss]
            l = functools.reduce(jnp.add, [jnp.sum(p, axis=-1, keepdims=True) for p in ps])
            acc = functools.reduce(jnp.add, [_dot(p.astype(BF16), v_ref[seg[0]:seg[1], :]) for p, seg in zip(ps, segs)])
            o_h = jnp.where(mv, acc / l, 0.0)
            lse_h = jnp.where(mv, mx + jnp.log(l) * LOG2E, 0.0)
            if h == 0:
                o_ref[rows, :] = o_h
                lse_ref[rows, :] = lse_h
            else:
                o_ref[rows, :] += o_h
                lse_ref[rows, :] += lse_h

        _run_ahead([(qi, h) for qi in range(nq) for h in range(2)], logits, finish, FWD_AHEAD)

    def seq_spec(cols):
        return pl.BlockSpec((seq, cols), lambda b, p: (b, p))

    in_specs = [seq_spec(dq), seq_spec(dq), seq_spec(LANE)]
    args = [q, k, v]
    if kind == "B":
        in_specs.append(pl.BlockSpec((2, tile, BAND_W), lambda b, p: (p, 0, 0)))
        args.append(aux)
    if kind == "C":
        in_specs.append(pl.BlockSpec((1, 2, 1, seq), lambda b, p: (b, p, 0, 0)))
        args.append(aux)
    return pl.pallas_call(
        body, name="attn_fwd_" + kind, grid=(nb, 3), in_specs=in_specs, out_specs=[seq_spec(LANE), seq_spec(LANE)],
        out_shape=[jax.ShapeDtypeStruct((nb * seq, 384), F32)] * 2, compiler_params=_params(("parallel", "parallel")),
    )(*args)


def _attn_bwd(kind, q, k, v, o, do, lse, aux, nb, seq):
    dq = q.shape[1] // 3
    tile = T_BAND if kind == "B" else T_CAUSAL
    nq = seq // tile
    scale = _attn_scale(kind)
    dqk_dtype = F32 if kind == "A" else BF16

    def body(*refs):
        dfr_ref = dfq_ref = dbt_ref = aux_ref = None
        if kind == "A":
            q_ref, k_ref, v_ref, o_ref, do_ref, lse_ref, dq_ref, dk_ref, dv_ref, dkt_acc, dvt_acc = refs
        elif kind == "B":
            q_ref, k_ref, v_ref, o_ref, do_ref, lse_ref, aux_ref, dq_ref, dk_ref, dv_ref, dbt_ref, dkt_acc, dvt_acc = refs
        else:
            (q_ref, k_ref, v_ref, o_ref, do_ref, lse_ref, aux_ref, dq_ref, dk_ref, dv_ref, dfr_ref, dfq_ref,
             dkt_acc, dvt_acc) = refs
        dkt_acc[...] = jnp.zeros_like(dkt_acc)
        dvt_acc[...] = jnp.zeros_like(dvt_acc)
        if kind == "C":
            dfr_ref[...] = jnp.zeros_like(dfr_ref)
        if kind == "B":
            @pl.when(pl.program_id(1) == 0)
            def _():
                dbt_ref[...] = jnp.zeros_like(dbt_ref)

        def products(qi, h):
            rows = slice(qi * tile, (qi + 1) * tile)
            q2 = q_ref[rows, :]
            mq, mv = _head_masks(kind, tile, dq, h)
            qh = jnp.where(mq, q2, jnp.zeros_like(q2))
            doh = jnp.where(mv, do_ref[rows, :], 0.0)
            dob = doh.astype(BF16)
            segs = _segments(kind, qi, tile)
            ts = [_scores(kind, qh, k_ref, aux_ref, h, seg, tile, scale) for seg in segs]
            dps = [_dot_nt(dob, v_ref[seg[0]:seg[1], :]) for seg in segs]
            return rows, segs, qh, doh, ts, dps

        def finish(h, rows, segs, qh, doh, ts, dps):
            mq, mv = _head_masks(kind, tile, dq, h)
            qht = qh.astype(F32).T.astype(BF16)
            dobt = doh.T.astype(BF16)
            head_rows = [(64 * h, 64)] + ([(128 + 32 * h, 32)] if kind == "A" else [])
            delta = jnp.sum(doh * o_ref[rows, :], axis=-1, keepdims=True)
            lseh = jnp.max(jnp.where(mv, lse_ref[rows, :], NEG), axis=-1, keepdims=True)
            rs = jnp.zeros((tile, 1), F32)
            dq_h = jnp.zeros((tile, dq), F32)
            for (a, b, _, c0), t, dp in zip(segs, ts, dps):
                p = jnp.exp2(t - lseh)
                ds = p * (dp - delta)
                if kind == "B":
                    dbt_ref[h, :, c0:BAND_W] += ds
                if kind == "C":
                    dfr_ref[0, h, :, a:b] -= jnp.sum(ds, axis=0, keepdims=True)
                    rs = rs + jnp.sum(ds, axis=-1, keepdims=True)
                dss = (ds * scale).astype(BF16)
                dvt_acc[64 * h:64 * h + 64, a:b] += _dot(dobt[64 * h:64 * h + 64, :], p.astype(BF16))
                for r0, n in head_rows:
                    dkt_acc[r0:r0 + n, a:b] += _dot(qht[r0:r0 + n, :], dss)
                dq_h = dq_h + _dot(dss, k_ref[a:b, :])
            dq_h = jnp.where(mq, dq_h, 0.0).astype(dqk_dtype)
            if h == 0:
                dq_ref[rows, :] = dq_h
            else:
                dq_ref[rows, :] += dq_h
            if kind == "C":
                if h == 0:
                    dfq_ref[rows, :] = jnp.where(mv, rs, 0.0)
                else:
                    dfq_ref[rows, :] += jnp.where(mv, rs, 0.0)

        _run_ahead([(qi, h) for qi in range(nq) for h in range(2)], products, finish, BWD_AHEAD)
        for j in range(seq // 256):
            cols = slice(256 * j, 256 * (j + 1))
            dk_ref[cols, :] = dkt_acc[:, cols].T.astype(dqk_dtype)
            dv_ref[cols, :] = dvt_acc[:, cols].T.astype(BF16)

    def seq_spec(cols):
        return pl.BlockSpec((seq, cols), lambda p, b: (b, p))

    in_specs = [seq_spec(dq), seq_spec(dq), seq_spec(LANE), seq_spec(LANE), seq_spec(LANE), seq_spec(LANE)]
    args = [q, k, v, o, do, lse]
    out_specs = [seq_spec(dq), seq_spec(dq), seq_spec(LANE)]
    out_shape = [jax.ShapeDtypeStruct((nb * seq, 3 * dq), dqk_dtype)] * 2 + [jax.ShapeDtypeStruct((nb * seq, 384), BF16)]
    if kind == "B":
        spec = pl.BlockSpec((2, tile, BAND_W), lambda p, b: (p, 0, 0))
        in_specs.append(spec)
        args.append(aux)
        out_specs.append(spec)
        out_shape.append(jax.ShapeDtypeStruct((6, tile, BAND_W), F32))
    if kind == "C":
        spec = pl.BlockSpec((1, 2, 1, seq), lambda p, b: (b, p, 0, 0))
        in_specs.append(spec)
        args.append(aux)
        out_specs += [spec, seq_spec(LANE)]
        out_shape += [jax.ShapeDtypeStruct((nb, 6, 1, seq), F32), jax.ShapeDtypeStruct((nb * seq, 384), F32)]
    return pl.pallas_call(
        body, name="attn_bwd_" + kind, grid=(3, nb), in_specs=in_specs, out_specs=out_specs, out_shape=out_shape,
        scratch_shapes=[pltpu.VMEM((dq, seq), F32), pltpu.VMEM((LANE, seq), F32)],
        compiler_params=_params(("arbitrary", "arbitrary")),
    )(*args)


BIAS_G = 768
BIAS_EDGE = BIAS_G - N_REL


def _bias_line(rel_bias):
    g = jnp.concatenate([jnp.broadcast_to(rel_bias[:, N_REL - 1:], (rel_bias.shape[0], BIAS_EDGE)),
                         jnp.flip(rel_bias, axis=1)], axis=1)
    return jnp.pad(g, ((0, 8 - g.shape[0]), (0, 0)))


def _bias_unline(dg):
    return jnp.flip(dg[:, BIAS_EDGE:], axis=1)


def _bias_expand(g8):
    def body(g_ref, out_ref):
        line = jnp.broadcast_to(g_ref[0] * LOG2E, (T_BAND, BIAS_G))
        slab = pltpu.roll(line, 1, 1, stride=1, stride_axis=0)[:, LANE:BIAS_G]
        row = lax.broadcasted_iota(jnp.int32, (T_BAND, BAND_W), 0)
        col = lax.broadcasted_iota(jnp.int32, (T_BAND, BAND_W), 1)
        hidden = ((row >= 64) & (col < 64)) | ((row < 64) & (col >= BAND_W - 64))
        out_ref[0] = jnp.where(hidden, NEG, slab)

    return pl.pallas_call(
        body, name="bias_expand", grid=(8,), in_specs=[pl.BlockSpec((1, 1, BIAS_G), lambda h: (h, 0, 0))],
        out_specs=pl.BlockSpec((1, T_BAND, BAND_W), lambda h: (h, 0, 0)),
        out_shape=jax.ShapeDtypeStruct((8, T_BAND, BAND_W), F32), compiler_params=_params(("parallel",)),
    )(g8.reshape(8, 1, BIAS_G))


def _bias_reduce(d_slab):
    def body(d_ref, out_ref):
        r = lax.broadcasted_iota(jnp.int32, (T_BAND, T_BAND), 0)
        k = lax.broadcasted_iota(jnp.int32, (T_BAND, T_BAND), 1)
        flip = jnp.where(r + k == T_BAND - 1, 1.0, 0.0).astype(BF16)
        hi, mid, lo = _split3(d_ref[0])
        d_rev = _dot(flip, hi) + _dot(flip, mid) + _dot(flip, lo)
        wide = jnp.concatenate([jnp.zeros((T_BAND, LANE), F32), d_rev, jnp.zeros((T_BAND, 2 * LANE), F32)], axis=1)
        skew = pltpu.roll(wide, 0, 1, stride=1, stride_axis=0)
        dg = jnp.sum(skew, axis=0, keepdims=True)[:, LANE:LANE + BIAS_G]
        lane = _lane_iota((1, BIAS_G))
        clipped = jnp.sum(jnp.where(lane <= BIAS_EDGE, dg, 0.0), axis=1, keepdims=True)
        out_ref[0] = jnp.where(lane == BIAS_EDGE, clipped, dg)

    return pl.pallas_call(
        body, name="bias_reduce", grid=(8,), in_specs=[pl.BlockSpec((1, T_BAND, BAND_W), lambda h: (h, 0, 0))],
        out_specs=pl.BlockSpec((1, 1, BIAS_G), lambda h: (h, 0, 0)),
        out_shape=jax.ShapeDtypeStruct((8, 1, BIAS_G), F32), compiler_params=_params(("parallel",)),
    )(d_slab).reshape(8, BIAS_G)


def _outproj_bwd(dxn, y, gate, oa, ob, oc, zg, w_out_p, w_out_pt, nb, seq):
    t = dxn.shape[0]
    tpe = seq // TM

    def body(dxn_ref, y_ref, gate_ref, oa_ref, ob_ref, oc_ref, zg_ref, w_ref, wt_ref,
             doa_ref, dob_ref, doc_ref, dzg_ref, gw_ref, dgate_ref):
        i = pl.program_id(0)

        @pl.when(i == 0)
        def _():
            gw_ref[...] = jnp.zeros_like(gw_ref)

        @pl.when(i % tpe == 0)
        def _():
            dgate_ref[...] = jnp.zeros_like(dgate_ref)

        dxn_t = dxn_ref[...]
        dgate_ref[0] += jnp.sum(dxn_t * y_ref[...], axis=0, keepdims=True)
        dy = (dxn_t * gate_ref[0]).astype(BF16)
        for gi, (o_ref, do_ref) in enumerate(((oa_ref, doa_ref), (ob_ref, dob_ref), (oc_ref, doc_ref))):
            cols = slice(384 * gi, 384 * (gi + 1))
            u = zg_ref[:, cols]
            o_t = o_ref[...]
            su = _silu(u)
            dcat = _dot(dy, wt_ref[:, cols])
            do_ref[...] = dcat * su
            dzg_ref[:, cols] = (dcat * o_t * _dsilu(u)).astype(BF16)
            gw_ref[cols, :] += _dot_tn((o_t * su).astype(BF16), dy)

    return pl.pallas_call(
        body, name="outproj_bwd", grid=(t // TM,),
        in_specs=[_row_spec(D_MODEL), _row_spec(D_MODEL), _ex_spec(tpe), _row_spec(384), _row_spec(384), _row_spec(384),
                  _row_spec(D_CAT), _full_spec((D_CAT, D_MODEL)), _full_spec((D_MODEL, D_CAT))],
        out_specs=[_row_spec(384), _row_spec(384), _row_spec(384), _row_spec(D_CAT), _full_spec((D_CAT, D_MODEL)),
                   _ex_spec(tpe)],
        out_shape=[jax.ShapeDtypeStruct((t, 384), F32)] * 3 + [jax.ShapeDtypeStruct((t, D_CAT), BF16),
                                                                jax.ShapeDtypeStruct((D_CAT, D_MODEL), F32),
                                                                jax.ShapeDtypeStruct((nb, 1, D_MODEL), F32)],
        compiler_params=_params(("arbitrary",)),
    )(dxn, y, gate, oa, ob, oc, zg, w_out_p, w_out_pt)


def _a_up_bwd(dqa, dka, dva, za, gq, gkv, w_uq_pt, w_ukv_pt, cos, sina, sinb):
    t = za.shape[0]

    def body(dq_ref, dk_ref, dv_ref, za_ref, gq_ref, gkv_ref, wqt_ref, wkvt_ref, cos_ref, sa_ref, sb_ref,
             dza_ref, gwq_ref, gwkv_ref, ggq_ref, ggkv_ref, dqb, dkvb):
        @pl.when(pl.program_id(0) == 0)
        def _():
            gwq_ref[...] = jnp.zeros_like(gwq_ref)
            gwkv_ref[...] = jnp.zeros_like(gwkv_ref)
            ggq_ref[...] = jnp.zeros_like(ggq_ref)
            ggkv_ref[...] = jnp.zeros_like(ggkv_ref)

        cos_t, sa, sb = cos_ref[...], sa_ref[...], sb_ref[...]
        dkpe = jnp.zeros((TM, LANE), F32)
        for p in range(3):
            dqb[:, 256 * p:256 * p + 128] = dq_ref[:, 256 * p:256 * p + 128].astype(BF16)
            dqb[:, 256 * p + 128:256 * p + 256] = _rope_t(dq_ref[:, 256 * p + 128:256 * p + 256], cos_t, sa, sb).astype(BF16)
            dkvb[:, 128 * p:128 * p + 128] = dk_ref[:, 256 * p:256 * p + 128].astype(BF16)
            dkpe = dkpe + dk_ref[:, 256 * p + 128:256 * p + 256]
        dkvb[:, 384:768] = dv_ref[...]
        dkpe = _rope_t(dkpe, cos_t, sa, sb)
        dkpe = jnp.where(_lane_iota((TM, LANE)) < A_ROPE, dkpe + pltpu.roll(dkpe, LANE - 32, 1), 0.0)

        gqv = gq_ref[...]
        cqn, cqh, rq = _rms(za_ref[:, 0:384], gqv)
        dq_t = dqb[...]
        gwq_ref[...] += _dot_tn(cqn.astype(BF16), dq_t)
        dcqn = _dot(dq_t, wqt_ref[...])
        ggq_ref[...] += jnp.broadcast_to(jnp.sum(dcqn * cqh, axis=0, keepdims=True), ggq_ref.shape)
        dza_ref[:, 0:384] = _rms_bwd(dcqn, cqh, rq, gqv).astype(BF16)

        gkvv = gkv_ref[...]
        ckvn, ckvh, rkv = _rms(za_ref[:, 384:640], gkvv)
        dkv_t = dkvb[...]
        gwkv_ref[...] += _dot_tn(ckvn.astype(BF16), dkv_t)
        dckvn = _dot(dkv_t, wkvt_ref[...])
        ggkv_ref[...] += jnp.broadcast_to(jnp.sum(dckvn * ckvh, axis=0, keepdims=True), ggkv_ref.shape)
        dza_ref[:, 384:640] = _rms_bwd(dckvn, ckvh, rkv, gkvv).astype(BF16)
        dza_ref[:, 640:768] = dkpe.astype(BF16)

    return pl.pallas_call(
        body, name="a_up_bwd", grid=(t // TM,),
        in_specs=[_row_spec(768), _row_spec(768), _row_spec(384), _row_spec(768), _full_spec((1, 384)),
                  _full_spec((1, 256)), _full_spec((768, 384)), _full_spec((768, 256)), _row_spec(LANE), _row_spec(LANE),
                  _row_spec(LANE)],
        out_specs=[_row_spec(768), _full_spec((384, 768)), _full_spec((256, 768)), _full_spec((8, 384)),
                   _full_spec((8, 256))],
        out_shape=[jax.ShapeDtypeStruct((t, 768), BF16), jax.ShapeDtypeStruct((384, 768), F32),
                   jax.ShapeDtypeStruct((256, 768), F32), jax.ShapeDtypeStruct((8, 384), F32),
                   jax.ShapeDtypeStruct((8, 256), F32)],
        scratch_shapes=[pltpu.VMEM((TM, 768), BF16), pltpu.VMEM((TM, 768), BF16)],
        compiler_params=_params(("arbitrary",)),
    )(dqa, dka, dva, za, gq, gkv, w_uq_pt, w_ukv_pt, cos, sina, sinb)


def _dz_cols():
    return (Z_A, Z_G) + Z_QKV + (Z_F,)


def _inproj_bwd_dx(dz, dxn, x, shift, scale, g, w_in_pt, nb, seq):
    t = x.shape[0]
    tpe = seq // TM
    cols = _dz_cols()

    def body(*refs):
        dz_refs = refs[:len(cols)]
        dxn_ref, x_ref, sh_ref, sc_ref, g_ref, wt_ref, dx_ref, dsh_ref, dsc_ref, dg_ref = refs[len(cols):]
        i = pl.program_id(0)

        @pl.when(i == 0)
        def _():
            dg_ref[...] = jnp.zeros_like(dg_ref)

        @pl.when(i % tpe == 0)
        def _():
            dsh_ref[...] = jnp.zeros_like(dsh_ref)
            dsc_ref[...] = jnp.zeros_like(dsc_ref)

        dh = jnp.zeros((TM, D_MODEL), F32)
        for ref, (c0, c1) in zip(dz_refs, cols):
            dh = dh + _dot_nt(ref[...], wt_ref[:, c0:c1])
        gv = g_ref[...]
        n, xh, r = _rms(x_ref[...], gv)
        dsh_ref[0] += jnp.sum(dh, axis=0, keepdims=True)
        dsc_ref[0] += jnp.sum(dh * n, axis=0, keepdims=True)
        dn = dh * (1.0 + sc_ref[0])
        dg_ref[...] += jnp.broadcast_to(jnp.sum(dn * xh, axis=0, keepdims=True), dg_ref.shape)
        dx_ref[...] = dxn_ref[...] + _rms_bwd(dn, xh, r, gv)

    in_specs = [_row_spec(c1 - c0) for c0, c1 in cols]
    in_specs += [_row_spec(D_MODEL), _row_spec(D_MODEL), _ex_spec(tpe), _ex_spec(tpe), _full_spec((1, D_MODEL)),
                 _full_spec((D_MODEL, NP_IN))]
    return pl.pallas_call(
        body, name="inproj_bwd_dx", grid=(t // TM,), in_specs=in_specs,
        out_specs=[_row_spec(D_MODEL), _ex_spec(tpe), _ex_spec(tpe), _full_spec((8, D_MODEL))],
        out_shape=[jax.ShapeDtypeStruct((t, D_MODEL), F32), jax.ShapeDtypeStruct((nb, 1, D_MODEL), F32),
                   jax.ShapeDtypeStruct((nb, 1, D_MODEL), F32), jax.ShapeDtypeStruct((8, D_MODEL), F32)],
        compiler_params=_params(("arbitrary",)),
    )(*dz, dxn, x, shift, scale, g, w_in_pt)


def _inproj_bwd_dw(h, dz, name, layer, both=None):
    t = h.shape[0]
    widths = [d.shape[1] for d in dz]
    total = sum(widths)

    def body(*refs):
        h_ref = refs[0]
        dz_refs = refs[1:1 + len(dz)]
        gw_ref = refs[-1]

        @pl.when(pl.program_id(0) == 0)
        def _():
            gw_ref[...] = jnp.zeros_like(gw_ref)

        h_t = h_ref[...]
        c0 = 0
        for ref, w in zip(dz_refs, widths):
            gw_ref[0, :, c0:c0 + w] += _dot_tn(h_t, ref[...])
            c0 += w

    in_specs = [_row_spec(D_MODEL)] + [_row_spec(w) for w in widths]
    args = [h, *dz]
    aliases = {}
    if both is not None:
        in_specs.append(pl.BlockSpec(memory_space=pl.ANY))
        aliases = {len(args): 0}
        args.append(both)
    return pl.pallas_call(
        body, name=name, grid=(t // TM,), in_specs=in_specs,
        out_specs=pl.BlockSpec((1, D_MODEL, total), lambda i: (layer, 0, 0)),
        out_shape=jax.ShapeDtypeStruct((DEPTH, D_MODEL, total), F32), input_output_aliases=aliases,
        compiler_params=_params(("arbitrary",)),
    )(*args)


def _ada_bwd(c_all, dmod_cols, dmod_all):
    n = c_all.shape[0]
    cols = dmod_cols.shape[2]

    def body(c_ref, dc_ref, da_ref, gw_ref, gb_ref):
        act = _silu(c_ref[...]).astype(BF16)
        gw_ref[0] = _dot_tn(act, dc_ref[0].astype(BF16))
        gb_ref[0] = jnp.sum(da_ref[0], axis=0, keepdims=True)

    return pl.pallas_call(
        body, name="ada_bwd", grid=(DEPTH,),
        in_specs=[pl.BlockSpec((n, D_MODEL), lambda l: (0, 0)), pl.BlockSpec((1, n, cols), lambda l: (l, 0, 0)),
                  pl.BlockSpec((1, n, 3 * D_MODEL), lambda l: (l, 0, 0))],
        out_specs=[pl.BlockSpec((1, D_MODEL, cols), lambda l: (l, 0, 0)),
                   pl.BlockSpec((1, 1, 3 * D_MODEL), lambda l: (l, 0, 0))],
        out_shape=[jax.ShapeDtypeStruct((DEPTH, D_MODEL, cols), F32), jax.ShapeDtypeStruct((DEPTH, 1, 3 * D_MODEL), F32)],
        compiler_params=_params(("parallel",)),
    )(c_all, dmod_cols, dmod_all)


def _sum_blocks(parts, name):
    n, rows, cols = parts.shape
    tr = rows if rows <= 256 else 8 * next(d for d in range(32, 0, -1) if (rows // 8) % d == 0)

    def body(p_ref, out_ref):
        acc = p_ref[0].astype(F32)
        for k in range(1, n):
            acc = acc + p_ref[k].astype(F32)
        out_ref[...] = acc

    return pl.pallas_call(
        body, name=name, grid=(rows // tr,), in_specs=[pl.BlockSpec((n, tr, cols), lambda i: (0, i, 0))],
        out_specs=pl.BlockSpec((tr, cols), lambda i: (i, 0)), out_shape=jax.ShapeDtypeStruct((rows, cols), F32),
        compiler_params=_params(("parallel",)),
    )(parts)


def _adamw(w, g, m, v, name):
    nl, rows, cols = w.shape
    if rows * cols <= 64 * 1024:
        tr = rows
        tl = next(t for t in range(nl, 0, -1) if nl % t == 0 and t * max(rows, 8) * cols <= 512 * 1024)
    else:
        tl = 1
        tr = next(t for t in (rows, 256, 128, 64, 32, 16, 8) if rows % t == 0 and t * cols <= 256 * 1024)

    def body(w_ref, g_ref, m_ref, v_ref, d_ref, mo_ref, vo_ref):
        gv = g_ref[...]
        mn = ADAM_B1 * m_ref[...] + (1.0 - ADAM_B1) * gv
        vn = ADAM_B2 * v_ref[...] + (1.0 - ADAM_B2) * jnp.square(gv)
        m_hat = mn / (1.0 - ADAM_B1 ** ADAM_STEP)
        v_hat = vn / (1.0 - ADAM_B2 ** ADAM_STEP)
        d_ref[...] = -ADAM_LR * (m_hat / (jnp.sqrt(v_hat) + ADAM_EPS) + ADAM_WD * w_ref[...])
        mo_ref[...] = mn
        vo_ref[...] = vn

    spec = pl.BlockSpec((tl, tr, cols), lambda l, i: (l, i, 0))
    return pl.pallas_call(
        body, name=name, grid=(nl // tl, rows // tr), in_specs=[spec] * 4, out_specs=[spec] * 3,
        out_shape=[jax.ShapeDtypeStruct((nl, rows, cols), F32)] * 3, compiler_params=_params(("parallel", "parallel")),
    )(w, g, m, v)


ALL_FLIPS = tuple(range(1, N_DEV))


def _transfer(name, srcs, dst_shapes, plan, side=None):
    n_arr = len(srcs)
    probe = plan(0, 0, 0)
    n_steps = len(probe)
    side_in = [a for a, _ in side["inputs"]] if side else []
    side_out = side["outputs"] if side else []

    def body(*refs):
        src_refs, refs = refs[:n_arr], refs[n_arr:]
        side_in_refs, refs = refs[:len(side_in)], refs[len(side_in):]
        dst_refs, refs = refs[:n_arr], refs[n_arr:]
        side_out_refs, refs = refs[:len(side_out)], refs[len(side_out):]
        send_sems, recv_sems, local_sems = refs[:3]
        side_scratch = refs[3:]
        x, y, c = lax.axis_index("x"), lax.axis_index("y"), lax.axis_index("c")
        steps = plan(x, y, c)
        side_done = []

        def rows(ref, r0, n):
            return ref.at[:, pl.ds(r0, n), :]

        def arrival(t):
            a, _, _, n, _, f, _ = steps[t]
            return pltpu.make_async_remote_copy(
                src_ref=rows(dst_refs[a], 0, n), dst_ref=rows(dst_refs[a], 0, n), send_sem=send_sems.at[t],
                recv_sem=recv_sems.at[t], device_id=(x, y, c), device_id_type=pl.DeviceIdType.MESH)

        arrived, started = set(), []
        for t, (a, from_dst, sr, n, dr, f, after) in enumerate(steps):
            if side and t == side["at"]:
                after = tuple(after) + tuple(side["needs"])
            for u in after:
                if u not in arrived:
                    if steps[u][5] == 0:
                        started[u].wait()
                    else:
                        arrival(u).wait_recv()
                    arrived.add(u)
            if side and t == side["at"]:
                side_done.append(side["run"](x, y, c, dst_refs, side_in_refs, side_out_refs, side_scratch))
            src = rows(dst_refs[a] if from_dst else src_refs[a], sr, n)
            dst = rows(dst_refs[a], dr, n)
            if f == 0:
                cp = pltpu.make_async_copy(src, dst, local_sems.at[t])
            else:
                to = (1 - x if f & 4 else x, 1 - y if f & 2 else y, 1 - c if f & 1 else c)
                cp = pltpu.make_async_remote_copy(src_ref=src, dst_ref=dst, send_sem=send_sems.at[t],
                                                  recv_sem=recv_sems.at[t], device_id=to,
                                                  device_id_type=pl.DeviceIdType.MESH)
            cp.start()
            started.append(cp)
        for t, step in enumerate(steps):
            if step[5] != 0 and t not in arrived:
                arrival(t).wait_recv()
        for t, (cp, step) in enumerate(zip(started, steps)):
            if step[5] != 0:
                cp.wait_send()
            elif t not in arrived:
                cp.wait()
        for wait in side_done:
            wait()

    any_spec = pl.BlockSpec(memory_space=pl.ANY)
    return pl.pallas_call(
        body, name=name, out_shape=[jax.ShapeDtypeStruct(s, d) for s, d in list(dst_shapes) + list(side_out)],
        in_specs=[any_spec] * n_arr + [spec for _, spec in (side["inputs"] if side else [])],
        out_specs=[any_spec] * (n_arr + len(side_out)),
        scratch_shapes=[pltpu.SemaphoreType.DMA((n_steps,)), pltpu.SemaphoreType.DMA((n_steps,)),
                        pltpu.SemaphoreType.DMA((n_steps,))] + (side["scratch"] if side else []),
    )(*srcs, *side_in)


CHIP_FLIPS = (2, 4, 6)


def _gather_plan(chip_rows):
    def plan(x, y, c):
        steps = []
        chip = 2 * x + y
        for a, rc in enumerate(chip_rows):
            h = rc // 2
            first = (h // 32) * 16
            mine = rc * chip + h * c
            from_x, from_y, diag = rc * (chip ^ 2) + h * c, rc * (chip ^ 1) + h * c, rc * (chip ^ 3) + h * c
            steps.append((a, False, h * c, h, mine, 0, ()))
            to_x = len(steps)
            steps.append((a, False, h * c, h, mine, 4, ()))
            to_y = len(steps)
            steps.append((a, False, h * c, h, mine, 2, ()))
            fwd_y = len(steps)
            steps.append((a, True, from_x, first, from_x, 2, (to_x,)))
            fwd_x = len(steps)
            steps.append((a, True, from_y + first, h - first, from_y + first, 4, (to_y,)))
            steps.append((a, False, h * c, h, mine, 1, ()))
            steps.append((a, True, from_x, h, from_x, 1, (to_x,)))
            steps.append((a, True, from_y, h, from_y, 1, (to_y,)))
            steps.append((a, True, diag, first, diag, 1, (fwd_y,)))
            steps.append((a, True, diag + first, h - first, diag + first, 1, (fwd_x,)))
        return steps
    return plan


def _pair_reduce_plan(chip_rows):
    def plan(x, y, c):
        steps = []
        for a, rc in enumerate(chip_rows):
            h = rc // 2
            for j in range(4):
                steps.append((a, False, rc * j + h * (1 - c), h, h * j, 1, ()))
        return steps
    return plan


def _chip_scatter_plan(chip_rows):
    def plan(x, y, c):
        steps = []
        for a, rc in enumerate(chip_rows):
            h = rc // 2
            mine = h * (2 * x + y)
            arrivals = []
            for k, f in enumerate(CHIP_FLIPS):
                arrivals.append(len(steps))
                steps.append((a, False, h * ((2 * x + y) ^ (f >> 1)), h, h * k, f, ()))
            steps.append((a, False, mine, h, 3 * h, 0, ()))
            steps.append((a, False, mine, h, 7 * h, 1, ()))
            for k, t in enumerate(arrivals):
                steps.append((a, True, h * k, h, (4 + k) * h, 1, (t,)))
        return steps
    return plan


def _tile_rows(h):
    return h


def _sum_pair(partial, recv, core, rc, name):
    nl, _, cols = partial.shape
    h = rc // 2
    tr = _tile_rows(h)

    def body(c_ref, p_ref, r_ref, out_ref):
        out_ref[...] = (p_ref[...] + r_ref[...]).astype(BF16)

    spec = pl.BlockSpec((1, tr, cols), lambda l, j, i, c_ref: (l, (h // tr) * j + i, 0))
    return pl.pallas_call(
        body, name=name, out_shape=jax.ShapeDtypeStruct((nl, 4 * h, cols), BF16),
        grid_spec=pltpu.PrefetchScalarGridSpec(
            num_scalar_prefetch=1, grid=(nl, 4, h // tr),
            in_specs=[pl.BlockSpec((1, tr, cols), lambda l, j, i, c_ref: (l, (rc // tr) * j + (h // tr) * c_ref[0] + i, 0)),
                      spec],
            out_specs=spec),
        compiler_params=_params(("parallel", "parallel", "parallel")),
    )(core, partial, recv)


def _sum_chips(slots, core, rc, name):
    nl, _, cols = slots.shape
    h = rc // 2

    def body(c_ref, s_ref, out_ref):
        acc = s_ref[0, 3].astype(F32)
        for k in range(3):
            acc = acc + s_ref[0, k].astype(F32)
        out_ref[0] = acc

    return pl.pallas_call(
        body, name=name, out_shape=jax.ShapeDtypeStruct((nl, rc, cols), F32),
        grid_spec=pltpu.PrefetchScalarGridSpec(
            num_scalar_prefetch=1, grid=(nl, 2),
            in_specs=[pl.BlockSpec((1, 4, h, cols), lambda l, q, c_ref: (l, q, 0, 0))],
            out_specs=pl.BlockSpec((1, h, cols), lambda l, q, c_ref: (l, q + c_ref[0] - 2 * q * c_ref[0], 0))),
        compiler_params=_params(("parallel", "parallel")),
    )(core, slots.reshape(nl, 8, h, cols))


def _pad_cols(a, n):
    return a if n == 0 else jnp.pad(a, ((0, 0), (0, n)))


def _in_to_padded(w):
    return jnp.concatenate([_pad_cols(w[:, a:b], z) for a, b, z in IN_PIECES], axis=1)


def _in_cols_first(ga, gb):
    split = Z_G[1]
    pos, out = 0, {}
    for a, b, z in IN_PIECES:
        src, off = (ga, pos) if pos < split else (gb, pos - split)
        out[a] = jnp.transpose(src[:, :, off:off + (b - a)], (2, 0, 1))
        pos += (b - a) + z
    return jnp.concatenate([out[a] for a in sorted(out)], axis=0)


def _out_to_padded(w):
    z = jnp.zeros((64, w.shape[1]), w.dtype)
    return jnp.concatenate([w[0:384], w[384:704], z, w[704:1024], z], axis=0)


def _out_from_padded(gp):
    return jnp.concatenate([gp[0:384], gp[384:704], gp[768:1088]], axis=0)


def _uq_to_padded(w):
    parts = []
    for p in range(3):
        h0, h1 = 2 * p, 2 * p + 1
        parts += [w[:, 96 * h0:96 * h0 + 64], w[:, 96 * h1:96 * h1 + 64], w[:, 96 * h0 + 64:96 * h0 + 96],
                  w[:, 96 * h1 + 64:96 * h1 + 96], jnp.zeros((w.shape[0], 64), w.dtype)]
    return jnp.concatenate(parts, axis=1)


def _uq_from_padded(gp):
    parts = []
    for h in range(6):
        p, s = h // 2, h % 2
        parts += [gp[:, 256 * p + 64 * s:256 * p + 64 * s + 64], gp[:, 256 * p + 128 + 32 * s:256 * p + 160 + 32 * s]]
    return jnp.concatenate(parts, axis=1)


def _ukv_to_padded(w):
    return jnp.concatenate([w[:, 128 * h:128 * h + 64] for h in range(6)]
                           + [w[:, 128 * h + 64:128 * h + 128] for h in range(6)], axis=1)


def _ukv_from_padded(gp):
    parts = []
    for h in range(6):
        parts += [gp[:, 64 * h:64 * h + 64], gp[:, 384 + 64 * h:384 + 64 * h + 64]]
    return jnp.concatenate(parts, axis=1)


LR_ROWS = 224
SHARD_ROWS = (256, 256, 256, LR_ROWS)


def _pack_lowrank(w_uq, w_ukv):
    flat = jnp.concatenate([w_uq.reshape(-1), w_ukv.reshape(-1)])
    return jnp.pad(flat, (0, LR_ROWS * PACK_COLS - flat.shape[0])).reshape(1, LR_ROWS, PACK_COLS)


def _unpack_lowrank(packed):
    flat = packed.reshape(-1)
    n_uq = DEPTH * A_Q_RANK * 144
    n_ukv = DEPTH * A_KV_RANK * 192
    return flat[0:n_uq].reshape(DEPTH, A_Q_RANK, 144), flat[n_uq:n_uq + n_ukv].reshape(DEPTH, A_KV_RANK, 192)


def _rope_tables(positions):
    t = positions.size
    inv = ROPE_THETA ** (-jnp.arange(0, A_ROPE, 2, dtype=F32) / A_ROPE)
    inv_row = jnp.pad(jnp.tile(inv, 4), (0, 64)).reshape(1, LANE)

    def body(p_ref, i_ref, c_ref, sa_ref, sb_ref):
        ang = p_ref[...].astype(F32) * i_ref[...]
        lane = _lane_iota((TM, LANE))
        live = lane < 64
        second = (lane & 31) >= 16
        s = jnp.sin(ang)
        c_ref[...] = jnp.where(live, jnp.cos(ang), 0.0)
        sa_ref[...] = jnp.where(live & jnp.logical_not(second), -s, 0.0)
        sb_ref[...] = jnp.where(live & second, s, 0.0)

    return pl.pallas_call(
        body, name="rope_tables", grid=(t // TM,), in_specs=[_row_spec(1), _full_spec((1, LANE))],
        out_specs=[_row_spec(LANE)] * 3, out_shape=[jax.ShapeDtypeStruct((t, LANE), F32)] * 3,
        compiler_params=_params(("parallel",)),
    )(positions.reshape(t, 1), inv_row)


def _rows(a, n):
    flat = a.reshape(-1)
    return jnp.pad(flat, (0, n * LANE - flat.shape[0])).reshape(n, LANE)


def _forward_backward(x, mod, tables, target, weights, small, nb, seq):
    cos, sina, sinb = tables
    saved = []
    for l in range(DEPTH):
        w, s = weights[l], small[l]
        shift = mod[l][:, None, 0:D_MODEL]
        scale = mod[l][:, None, D_MODEL:2 * D_MODEL]
        gate = mod[l][:, None, 2 * D_MODEL:]
        h, za, zg, qb, kb, vb, qc, kc, vc, zf, qa, ka, va = _ln_inproj(
            x, shift, scale, s["norm_g"], w["in"], s["gq"], s["gkv"], w["uq"], w["ukv"], cos, sina, sinb, seq)
        bias = _bias_expand(s["g8"])[0:6]
        f = _forget_fwd(zf, s["fb"], nb, seq)
        frow = jnp.pad(f[:, 0:5].reshape(nb, seq, 5).transpose(0, 2, 1), ((0, 0), (0, 1), (0, 0)))
        frow = frow.reshape(nb, 6, 1, seq)
        oa, lse_a = _attn_fwd("A", qa, ka, va, None, nb, seq)
        ob, lse_b = _attn_fwd("B", qb, kb, vb, bias, nb, seq)
        oc, lse_c = _attn_fwd("C", qc, kc, vc, frow, nb, seq)
        y, xn = _gate_outproj(x, gate, oa, ob, oc, zg, w["out"], seq)
        saved.append(dict(x=x, h=h, za=za, zg=zg, zf=zf, y=y, shift=shift, scale=scale, gate=gate, bias=bias, frow=frow,
                          a=(qa, ka, va, oa, lse_a), b=(qb, kb, vb, ob, lse_b), c=(qc, kc, vc, oc, lse_c)))
        x = xn
    dx, loss8, gfinal8 = _final_loss(x, target, small[0]["final_g"])
    grads = []
    gw_in = (None, None)
    for l in reversed(range(DEPTH)):
        w, s, sv = weights[l], small[l], saved[l]
        qa, ka, va, oa, lse_a = sv["a"]
        qb, kb, vb, ob, lse_b = sv["b"]
        qc, kc, vc, oc, lse_c = sv["c"]
        doa, dob, doc, dzg, gw_out, dgate = _outproj_bwd(dx, sv["y"], sv["gate"], oa, ob, oc, sv["zg"], w["out"],
                                                          w["out_t"], nb, seq)
        dqa, dka, dva = _attn_bwd("A", qa, ka, va, oa, doa, lse_a, None, nb, seq)
        dqb, dkb, dvb, dbt = _attn_bwd("B", qb, kb, vb, ob, dob, lse_b, sv["bias"], nb, seq)
        dqc, dkc, dvc, dfr, dfq = _attn_bwd("C", qc, kc, vc, oc, doc, lse_c, sv["frow"], nb, seq)
        dg = _bias_reduce(jnp.pad(dbt, ((0, 2), (0, 0), (0, 0))))
        grb = jnp.pad(_bias_unline(dg), ((0, 0), (0, 384 - N_REL)))
        dfk = dfr.reshape(nb, 6, seq).transpose(0, 2, 1).reshape(nb * seq, 6)
        dzf, gfb = _forget_bwd(dfq, jnp.pad(dfk, ((0, 0), (0, LANE - 6))), sv["zf"], s["fb"], nb, seq)
        dza, gw_uq, gw_ukv, ggq, ggkv = _a_up_bwd(dqa, dka, dva, sv["za"], s["gq"], s["gkv"], w["uq_t"], w["ukv_t"],
                                                  cos, sina, sinb)
        dz = (dza, dzg, dqb, dkb, dvb, dqc, dkc, dvc, dzf)
        dx, dshift, dscale, gnorm = _inproj_bwd_dx(dz, dx, sv["x"], sv["shift"], sv["scale"], s["norm_g"], w["in"],
                                                   nb, seq)
        gw_in = (_inproj_bwd_dw(sv["h"], dz[0:2], "inproj_bwd_dw0", l, gw_in[0]),
                 _inproj_bwd_dw(sv["h"], dz[2:], "inproj_bwd_dw1", l, gw_in[1]))
        dmod = jnp.concatenate([dshift[:, 0], dscale[:, 0], dgate[:, 0]], axis=1)
        grads.append(dict(w_out=gw_out, w_uq=gw_uq, w_ukv=gw_ukv, dmod=dmod, norm_g=gnorm[0], gq=ggq[0],
                          gkv=ggkv[0], rb8=grb, fb=gfb[0]))
    grads.reverse()
    return loss8[0, 0], dx, grads, gfinal8[0], gw_in


def _layer_weights(w_in, w_out, w_uq, w_ukv):
    wi, wo, wq, wkv = _in_to_padded(w_in), _out_to_padded(w_out), _uq_to_padded(w_uq), _ukv_to_padded(w_ukv)
    return {"in": wi, "out": wo, "out_t": wo.T, "uq": wq, "uq_t": wq.T, "ukv": wkv, "ukv_t": wkv.T}


def _layer_small(norm_g, gq, gkv, rel_bias, forget_b, final_g):
    fb = jnp.pad(forget_b, (0, LANE - 5)).reshape(1, LANE)
    return dict(norm_g=norm_g.reshape(1, -1), gq=gq.reshape(1, -1), gkv=gkv.reshape(1, -1), g8=_bias_line(rel_bias), fb=fb,
                final_g=final_g.reshape(1, -1))


def _small_payload(per_layer, final_g, loss):
    def stack(key):
        return jnp.stack([p[key] for p in per_layer])

    def rows(a, rng):
        return _rows(a, rng[1] - rng[0])

    parts = [rows(stack("dmod"), PAY_DMOD), rows(stack("norm_g"), PAY_NORM), rows(stack("gq"), PAY_GQ),
             rows(stack("gkv"), PAY_GKV), rows(stack("rb8"), PAY_RB), rows(stack("fb"), PAY_FB),
             rows(final_g, PAY_FINAL), rows(loss, PAY_LOSS)]
    return jnp.concatenate(parts, axis=0)


def _payload_split(pay):
    def take(rng, shape):
        n = 1
        for d in shape:
            n *= d
        return pay[rng[0]:rng[1]].reshape(-1)[0:n].reshape(shape)

    norm_g = take(PAY_NORM, (DEPTH, D_MODEL))
    gq = take(PAY_GQ, (DEPTH, A_Q_RANK))
    gkv = take(PAY_GKV, (DEPTH, A_KV_RANK))
    rb = take(PAY_RB, (DEPTH, 8, 384))[:, 0:5, 0:N_REL]
    fb = take(PAY_FB, (DEPTH, LANE))[:, 0:5]
    final_g = take(PAY_FINAL, (D_MODEL,))
    return norm_g, gq, gkv, rb, fb, final_g


def kernel(x, c, positions, w_ada, b_ada, norm_g, w_in, a_q_norm_g, a_w_uq, a_kv_norm_g, a_w_ukv, b_rel_bias, c_forget_b, w_out, final_g, loss_target, m_w_ada, m_b_ada, m_norm_g, m_w_in, m_a_q_norm_g, m_a_w_uq, m_a_kv_norm_g, m_a_w_ukv, m_b_rel_bias, m_c_forget_b, m_w_out, m_final_g, v_w_ada, v_b_ada, v_norm_g, v_w_in, v_a_q_norm_g, v_a_w_uq, v_a_kv_norm_g, v_a_w_ukv, v_b_rel_bias, v_c_forget_b, v_w_out, v_final_g):
    nb, seq, _ = x.shape
    ix, iy, ic = lax.axis_index("x"), lax.axis_index("y"), lax.axis_index("c")
    chip = 2 * ix + iy

    weight_plan = _gather_plan((256, 256, LR_ROWS))
    cols = w_ada.shape[2]

    def gather_plan(x, y, c_):
        me_ = 4 * x + 2 * y + c_
        first = [(3, False, 0, 8, 8 * me_, f, ()) for f in range(N_DEV)]
        rest = weight_plan(x, y, c_)
        order = [i for i, s in enumerate(rest) if not s[6]] + [i for i, s in enumerate(rest) if s[6]]
        place = {old: len(first) + new for new, old in enumerate(order)}
        return first + [rest[i][:6] + (tuple(place[u] for u in rest[i][6]),) for i in order]

    probe = gather_plan(0, 0, 0)

    def project(x, y, c_, dst_refs, in_refs, out_refs, scratch):
        w_ref, b_ref = in_refs
        (mod_ref,) = out_refs
        c_vm, mod_vm, send_m, recv_m, local_m = scratch
        me_ = 4 * x + 2 * y + c_
        load = pltpu.make_async_copy(dst_refs[3].at[0], c_vm, local_m.at[0])
        load.start()
        load.wait()
        act = _silu(c_vm[...]).astype(BF16)
        for l in range(DEPTH):
            mod_vm[l] = _dot(act, w_ref[l].astype(BF16)) + b_ref[l]

        def rows_of(j):
            return mod_vm.at[:, pl.ds(pl.multiple_of(8 * j, 8), 8), :]

        def send(i, f, slot, j):
            to = (1 - x if f & 4 else x, 1 - y if f & 2 else y, 1 - c_ if f & 1 else c_)
            return pltpu.make_async_remote_copy(src_ref=rows_of(j), dst_ref=mod_ref.at[slot], send_sem=send_m.at[i],
                                                recv_sem=recv_m.at[i], device_id=to, device_id_type=pl.DeviceIdType.MESH)

        own = pltpu.make_async_copy(rows_of(me_), mod_ref.at[me_], local_m.at[1])
        own.start()
        sends = [send(i, f, me_, me_ ^ f) for i, f in enumerate(ALL_FLIPS)]
        for s in sends:
            s.start()

        def finish():
            for i, f in enumerate(ALL_FLIPS):
                send(i, f, me_ ^ f, 0).wait_recv()
            for s in sends:
                s.wait_send()
            own.wait()
        return finish

    b_cols = lax.dynamic_slice_in_dim(b_ada, chip * cols, cols, axis=1)[:, None, :]
    vmem = pl.BlockSpec(memory_space=pltpu.VMEM)
    side = dict(at=next(t for t, s in enumerate(probe) if s[6]), needs=tuple(range(N_DEV)),
                inputs=[(w_ada, vmem), (b_cols, vmem)], outputs=[((N_DEV, DEPTH, 8, cols), F32)],
                scratch=[pltpu.VMEM((8 * N_DEV, D_MODEL), F32), pltpu.VMEM((DEPTH, 8 * N_DEV, cols), F32),
                         pltpu.SemaphoreType.DMA((len(ALL_FLIPS),)), pltpu.SemaphoreType.DMA((len(ALL_FLIPS),)),
                         pltpu.SemaphoreType.DMA((2,))],
                run=project)
    full_in, full_out, full_lr, c_rows, mod_rows = _transfer(
        "gather_weights", [w_in.astype(BF16), w_out.astype(BF16), _pack_lowrank(a_w_uq, a_w_ukv).astype(BF16),
                           jnp.pad(c, ((0, 8 - nb), (0, 0)))[None]],
        [((DEPTH, D_MODEL, N_IN), BF16), ((DEPTH, D_MODEL, D_MODEL), BF16), ((1, 4 * LR_ROWS, PACK_COLS), BF16),
         ((1, 8 * N_DEV, D_MODEL), F32)], gather_plan, side)
    lowrank = [_unpack_lowrank(full_lr[0, LR_ROWS * j:LR_ROWS * (j + 1)]) for j in range(4)]
    full_uq = jnp.concatenate([s[0] for s in lowrank], axis=2)
    full_ukv = jnp.concatenate([s[1] for s in lowrank], axis=2)
    weights = [_layer_weights(full_in[l], full_out[l], full_uq[l], full_ukv[l]) for l in range(DEPTH)]
    small = [_layer_small(norm_g[l], a_q_norm_g[l], a_kv_norm_g[l], b_rel_bias[l], c_forget_b[l], final_g)
             for l in range(DEPTH)]

    c_all = c_rows.reshape(N_DEV, 8, D_MODEL)[:, 0:nb].reshape(N_DEV * nb, D_MODEL)
    mod = jnp.concatenate([mod_rows[2 * j][:, 0:nb] for j in range(4)], axis=2)

    tables = _rope_tables(positions)
    loss_part, dx, grads, gfinal, gw_in = _forward_backward(
        x.reshape(nb * seq, D_MODEL), mod, tables, loss_target.reshape(nb * seq, D_MODEL), weights, small, nb, seq)

    g_uq = jnp.stack([_uq_from_padded(g["w_uq"]) for g in grads])
    g_ukv = jnp.stack([_ukv_from_padded(g["w_ukv"]) for g in grads])
    g_lr = jnp.concatenate([_pack_lowrank(g_uq[:, :, 144 * j:144 * (j + 1)], g_ukv[:, :, 192 * j:192 * (j + 1)])
                            for j in range(4)], axis=1)
    partials = [gw_in[0], gw_in[1], jnp.stack([_out_from_padded(g["w_out"]) for g in grads]), g_lr]
    shapes = [(p.shape[0], p.shape[2]) for p in partials]
    halves = [r // 2 for r in SHARD_ROWS]
    core_s = jnp.reshape(ic, (1,)).astype(jnp.int32)
    pay = _small_payload(grads, gfinal, loss_part)
    pay_rows = pay.shape[0]
    reduce_plan = _pair_reduce_plan(SHARD_ROWS)

    def reduce_plan_with_payload(x, y, c_):
        me_ = 4 * x + 2 * y + c_
        return reduce_plan(x, y, c_) + [(len(partials), False, 0, pay_rows, pay_rows * me_, f, ()) for f in range(N_DEV)]

    *from_pair, pay_all = _transfer(
        "pair_reduce", partials + [pay[None]],
        [((nl, 4 * h, nc), F32) for (nl, nc), h in zip(shapes, halves)] + [((1, N_DEV * pay_rows, LANE), F32)],
        reduce_plan_with_payload)
    pay_all = pay_all.reshape(N_DEV, pay_rows, LANE)

    tot = _sum_blocks(pay_all, "sum_small")
    loss = tot[PAY_LOSS[0], 0]
    dmod_all = pay_all[:, PAY_DMOD[0]:PAY_DMOD[1]].reshape(N_DEV, -1)[:, 0:DEPTH * nb * 3 * D_MODEL]
    dmod_all = dmod_all.reshape(N_DEV, DEPTH, nb, 3 * D_MODEL).transpose(1, 0, 2, 3)
    dmod_all = dmod_all.reshape(DEPTH, N_DEV * nb, 3 * D_MODEL)
    my_cols = lax.dynamic_slice_in_dim(dmod_all, chip * cols, cols, axis=2)
    g_w_ada, g_b_ada = _ada_bwd(c_all, my_cols, dmod_all)
    g_b_ada = g_b_ada[:, 0]
    chip_sums = [_sum_pair(p, r, core_s, rc, "sum_pair%d" % i)
                 for i, (p, r, rc) in enumerate(zip(partials, from_pair, SHARD_ROWS))]
    slots = _transfer("chip_scatter", chip_sums, [((nl, 8 * h, nc), BF16) for (nl, nc), h in zip(shapes, halves)],
                      _chip_scatter_plan(SHARD_ROWS))
    g_in_a, g_in_b, g_out_sh, g_lr_sh = [_sum_chips(s, core_s, rc, "sum_chips%d" % i)
                                         for i, (s, rc) in enumerate(zip(slots, SHARD_ROWS))]
    g_uq_sh, g_ukv_sh = _unpack_lowrank(g_lr_sh[0])

    def cols_first(a):
        return jnp.transpose(a, (2, 0, 1))

    g_in_t = _in_cols_first(g_in_a, g_in_b)
    upd_in = tuple(jnp.transpose(a, (1, 2, 0)) for a in _adamw(cols_first(w_in), g_in_t, cols_first(m_w_in),
                                                               cols_first(v_w_in), "adamw_in"))
    gw = (jnp.transpose(g_in_t, (1, 2, 0)), g_out_sh, g_uq_sh, g_ukv_sh)
    upd = [upd_in, _adamw(w_out, gw[1], m_w_out, v_w_out, "adamw_out"),
           _adamw(a_w_uq, gw[2], m_a_w_uq, v_a_w_uq, "adamw_uq"), _adamw(a_w_ukv, gw[3], m_a_w_ukv, v_a_w_ukv, "adamw_ukv")]
    dw, mw, vw = (tuple(u[i] for u in upd) for i in range(3))
    d_ada, m_ada, v_ada = _adamw(w_ada, g_w_ada, m_w_ada, v_w_ada, "adamw_ada")

    def adam_small(w, g, m, v, name):
        shape3 = (1,) * (3 - w.ndim) + w.shape
        return tuple(a.reshape(w.shape) for a in _adamw(w.reshape(shape3), g.reshape(shape3), m.reshape(shape3),
                                                        v.reshape(shape3), name))

    d_b, m_b, v_b = adam_small(b_ada, g_b_ada, m_b_ada, v_b_ada, "adamw_b_ada")
    gs = _payload_split(tot)
    small_upd = [adam_small(w, g, m, v, "adamw_small%d" % i) for i, (w, g, m, v) in enumerate(zip(
        (norm_g, a_q_norm_g, a_kv_norm_g, b_rel_bias, c_forget_b, final_g), gs,
        (m_norm_g, m_a_q_norm_g, m_a_kv_norm_g, m_b_rel_bias, m_c_forget_b, m_final_g),
        (v_norm_g, v_a_q_norm_g, v_a_kv_norm_g, v_b_rel_bias, v_c_forget_b, v_final_g)))]
    ds, ms, vs = (tuple(u[i] for u in small_upd) for i in range(3))

    def ordered(ada, b, sm, big):
        ng, gq, gkv, rb, fb, fg = sm
        b_in, b_out, b_uq, b_ukv = big
        return (ada, b, ng, b_in, gq, b_uq, gkv, b_ukv, rb, fb, b_out, fg)

    return (loss, dx.reshape(nb, seq, D_MODEL), *ordered(g_w_ada, g_b_ada, gs, gw), *ordered(d_ada, d_b, ds, dw),
            *ordered(m_ada, m_b, ms, mw), *ordered(v_ada, v_b, vs, vw))
```

```python
import functools

import jax
import jax.numpy as jnp
from jax import lax
from jax.experimental import pallas as pl
from jax.experimental.pallas import tpu as pltpu

F32 = jnp.float32
BF16 = jnp.bfloat16

D_MODEL = 1024
DEPTH = 2
EPS = 1e-6
NEG = -1e30
LOG2E = 1.4426950408889634
ROPE_THETA = 10000.0
A_ROPE = 32
A_Q_RANK = 384
A_KV_RANK = 256
REL_CLIP = 128
N_REL = 2 * REL_CLIP + 1
N_IN = 3621

ADAM_LR = 0.001
ADAM_B1 = 0.9
ADAM_B2 = 0.999
ADAM_EPS = 1e-08
ADAM_WD = 0.01
ADAM_STEP = 10

LANE = 128
VMEM_LIMIT = 56 * 1024 * 1024

NP_IN = 4352
Z_A = (0, 768)
Z_G = (768, 1920)
Z_QKV = tuple((1920 + 384 * i, 1920 + 384 * (i + 1)) for i in range(6))
Z_F = (4224, 4352)
IN_PIECES = ((0, 672, 96), (672, 1056, 0), (2016, 2336, 64), (3301, 3621, 64), (1056, 1376, 64), (1376, 1696, 64),
             (1696, 2016, 64), (2336, 2656, 64), (2656, 2976, 64), (2976, 3296, 64), (3296, 3301, 123))
D_CAT = 1152

TM = 512
T_CAUSAL = 256
T_BAND = 128
BAND_TILES = 5
N_DEV = 8

PAY_DMOD = (0, 96)
PAY_NORM = (96, 112)
PAY_GQ = (112, 120)
PAY_GKV = (120, 128)
PAY_RB = (128, 176)
PAY_FB = (176, 184)
PAY_FINAL = (184, 192)
PAY_LOSS = (192, 200)

PACK_COLS = 1024


def _params(sem=None):
    return pltpu.CompilerParams(dimension_semantics=sem, vmem_limit_bytes=VMEM_LIMIT)


def _lane_iota(shape):
    return lax.broadcasted_iota(jnp.int32, shape, len(shape) - 1)


def _silu(u):
    return u * jax.nn.sigmoid(u)


def _dsilu(u):
    s = jax.nn.sigmoid(u)
    return s * (1.0 + u * (1.0 - s))


def _rms(x, g):
    r = lax.rsqrt(jnp.mean(x * x, axis=-1, keepdims=True) + EPS)
    xh = x * r
    return xh * g, xh, r


def _rms_bwd(dy, xh, r, g):
    dxh = dy * g
    return r * (dxh - xh * jnp.mean(dxh * xh, axis=-1, keepdims=True))


def _rope(x, cos, sina, sinb):
    return x * cos + pltpu.roll(x, 16, 1) * sinb + pltpu.roll(x, LANE - 16, 1) * sina


def _rope_t(dy, cos, sina, sinb):
    return dy * cos + pltpu.roll(dy * sinb, LANE - 16, 1) + pltpu.roll(dy * sina, 16, 1)


def _split3(x):
    hi = x.astype(BF16)
    r1 = x - hi.astype(F32)
    mid = r1.astype(BF16)
    lo = (r1 - mid.astype(F32)).astype(BF16)
    return hi, mid, lo


def _dot(a, b):
    return jnp.dot(a, b, preferred_element_type=F32)


def _dot_nt(a, b):
    return lax.dot_general(a, b, (((1,), (1,)), ((), ())), preferred_element_type=F32)


def _dot_tn(a, b):
    return lax.dot_general(a, b, (((0,), (0,)), ((), ())), preferred_element_type=F32)


def _row_spec(cols):
    return pl.BlockSpec((TM, cols), lambda i: (i, 0))


def _full_spec(shape):
    return pl.BlockSpec(shape, lambda i: (0,) * len(shape))


def _ex_spec(tiles_per_ex):
    return pl.BlockSpec((1, 1, D_MODEL), lambda i: (i // tiles_per_ex, 0, 0))


def _ln_inproj(x, shift, scale, g, w_in_p, seq):
    t = x.shape[0]

    def body(x_ref, sh_ref, sc_ref, g_ref, w_ref, h_ref, za_ref, zg_ref, q0, q1, q2, q3, q4, q5, zf_ref):
        n, _, _ = _rms(x_ref[...], g_ref[...])
        h = (n * (1.0 + sc_ref[0]) + sh_ref[0]).astype(BF16)
        h_ref[...] = h
        za_ref[...] = _dot(h, w_ref[:, Z_A[0]:Z_A[1]])
        zg_ref[...] = _dot(h, w_ref[:, Z_G[0]:Z_G[1]])
        for ref, (c0, c1) in zip((q0, q1, q2, q3, q4, q5), Z_QKV):
            ref[...] = _dot(h, w_ref[:, c0:c1]).astype(BF16)
        zf_ref[...] = _dot(h, w_ref[:, Z_F[0]:Z_F[1]])

    tpe = seq // TM
    shapes = [jax.ShapeDtypeStruct((t, D_MODEL), BF16), jax.ShapeDtypeStruct((t, 768), F32),
              jax.ShapeDtypeStruct((t, D_CAT), F32)]
    shapes += [jax.ShapeDtypeStruct((t, 384), BF16)] * 6 + [jax.ShapeDtypeStruct((t, LANE), F32)]
    return pl.pallas_call(
        body, name="ln_inproj", grid=(t // TM,),
        in_specs=[_row_spec(D_MODEL), _ex_spec(tpe), _ex_spec(tpe), _full_spec((1, D_MODEL)),
                  _full_spec((D_MODEL, NP_IN))],
        out_specs=[_row_spec(D_MODEL), _row_spec(768), _row_spec(D_CAT)] + [_row_spec(384)] * 6 + [_row_spec(LANE)],
        out_shape=shapes, compiler_params=_params(("parallel",)),
    )(x, shift, scale, g, w_in_p)


def _a_up(za, gq, gkv, w_uq_p, w_ukv_p, cos, sina, sinb):
    t = za.shape[0]

    def body(za_ref, gq_ref, gkv_ref, wq_ref, wkv_ref, cos_ref, sa_ref, sb_ref, q_ref, k_ref, v_ref):
        cos_t, sa, sb = cos_ref[...], sa_ref[...], sb_ref[...]
        cqn, _, _ = _rms(za_ref[:, 0:384], gq_ref[...])
        q = _dot(cqn.astype(BF16), wq_ref[...])
        ckvn, _, _ = _rms(za_ref[:, 384:640], gkv_ref[...])
        kv = _dot(ckvn.astype(BF16), wkv_ref[...])
        kpe = za_ref[:, 640:768]
        kpe = _rope(kpe + pltpu.roll(kpe, 32, 1), cos_t, sa, sb).astype(BF16)
        for p in range(3):
            q_ref[:, 256 * p:256 * p + 128] = q[:, 256 * p:256 * p + 128].astype(BF16)
            q_ref[:, 256 * p + 128:256 * p + 256] = _rope(q[:, 256 * p + 128:256 * p + 256], cos_t, sa, sb).astype(BF16)
            k_ref[:, 256 * p:256 * p + 128] = kv[:, 128 * p:128 * p + 128].astype(BF16)
            k_ref[:, 256 * p + 128:256 * p + 256] = kpe
        v_ref[...] = kv[:, 384:768].astype(BF16)

    return pl.pallas_call(
        body, name="a_up", grid=(t // TM,),
        in_specs=[_row_spec(768), _full_spec((1, 384)), _full_spec((1, 256)), _full_spec((384, 768)),
                  _full_spec((256, 768)), _row_spec(LANE), _row_spec(LANE), _row_spec(LANE)],
        out_specs=[_row_spec(768), _row_spec(768), _row_spec(384)],
        out_shape=[jax.ShapeDtypeStruct((t, 768), BF16), jax.ShapeDtypeStruct((t, 768), BF16),
                   jax.ShapeDtypeStruct((t, 384), BF16)],
        compiler_params=_params(("parallel",)),
    )(za, gq, gkv, w_uq_p, w_ukv_p, cos, sina, sinb)


def _tri(n, upper):
    r = lax.broadcasted_iota(jnp.int32, (n, n), 0)
    c = lax.broadcasted_iota(jnp.int32, (n, n), 1)
    return jnp.where((c >= r) if upper else (c <= r), 1.0, 0.0).astype(BF16)


def _forget_fwd(zf, fb, nb, seq):
    blk = 256

    def body(zf_ref, fb_ref, f_ref):
        tri = _tri(blk, False)
        live = _lane_iota((blk, LANE)) < 5
        carry = jnp.zeros((1, LANE), F32)
        for i in range(seq // blk):
            u = zf_ref[i * blk:(i + 1) * blk, :] + fb_ref[...]
            lf = jnp.where(live, jnp.minimum(u, 0.0) - jnp.log(1.0 + jnp.exp(-jnp.abs(u))), 0.0)
            hi, mid, lo = _split3(lf)
            f_ref[i * blk:(i + 1) * blk, :] = (_dot(tri, hi) + _dot(tri, mid) + _dot(tri, lo) + carry) * LOG2E
            carry = carry + jnp.sum(lf, axis=0, keepdims=True)

    return pl.pallas_call(
        body, name="forget_fwd", grid=(nb,),
        in_specs=[pl.BlockSpec((seq, LANE), lambda b: (b, 0)), pl.BlockSpec((1, LANE), lambda b: (0, 0))],
        out_specs=pl.BlockSpec((seq, LANE), lambda b: (b, 0)),
        out_shape=jax.ShapeDtypeStruct((nb * seq, LANE), F32), compiler_params=_params(("parallel",)),
    )(zf, fb)


def _forget_bwd(dfq, dfk, zf, fb, nb, seq):
    blk = 256

    def body(dfq_ref, dfk_ref, zf_ref, fb_ref, dz_ref, gb_ref):
        @pl.when(pl.program_id(0) == 0)
        def _():
            gb_ref[...] = jnp.zeros_like(gb_ref)

        tri = _tri(blk, True)
        lane = _lane_iota((blk, LANE))
        wide = _lane_iota((blk, 384))
        live = lane < 5
        carry = jnp.zeros((1, LANE), F32)
        gsum = jnp.zeros((1, LANE), F32)
        for i in reversed(range(seq // blk)):
            d = dfk_ref[i * blk:(i + 1) * blk, :]
            dq = dfq_ref[i * blk:(i + 1) * blk, :]
            for hd in range(5):
                col = jnp.sum(jnp.where(wide == 64 * hd, dq, 0.0), axis=-1, keepdims=True)
                d = d + jnp.where(lane == hd, col, 0.0)
            hi, mid, lo = _split3(d)
            dlf = _dot(tri, hi) + _dot(tri, mid) + _dot(tri, lo) + carry
            carry = carry + jnp.sum(d, axis=0, keepdims=True)
            u = zf_ref[i * blk:(i + 1) * blk, :] + fb_ref[...]
            du = jnp.where(live, dlf * jax.nn.sigmoid(-u), 0.0)
            dz_ref[i * blk:(i + 1) * blk, :] = du.astype(BF16)
            gsum = gsum + jnp.sum(du, axis=0, keepdims=True)
        gb_ref[...] += jnp.broadcast_to(gsum, gb_ref.shape)

    return pl.pallas_call(
        body, name="forget_bwd", grid=(nb,),
        in_specs=[pl.BlockSpec((seq, 384), lambda b: (b, 0)), pl.BlockSpec((seq, LANE), lambda b: (b, 0)),
                  pl.BlockSpec((seq, LANE), lambda b: (b, 0)), pl.BlockSpec((1, LANE), lambda b: (0, 0))],
        out_specs=[pl.BlockSpec((seq, LANE), lambda b: (b, 0)), pl.BlockSpec((8, LANE), lambda b: (0, 0))],
        out_shape=[jax.ShapeDtypeStruct((nb * seq, LANE), BF16), jax.ShapeDtypeStruct((8, LANE), F32)],
        compiler_params=_params(("arbitrary",)),
    )(dfq, dfk, zf, fb)


def _gate_outproj(x, gate, oa, ob, oc, zg, w_out_p, seq):
    t = x.shape[0]

    def body(x_ref, gate_ref, oa_ref, ob_ref, oc_ref, zg_ref, w_ref, y_ref, xn_ref):
        y = jnp.zeros((TM, D_MODEL), F32)
        for i, o_ref in enumerate((oa_ref, ob_ref, oc_ref)):
            cat = (o_ref[...] * _silu(zg_ref[:, 384 * i:384 * (i + 1)])).astype(BF16)
            y = y + _dot(cat, w_ref[384 * i:384 * (i + 1), :])
        y_ref[...] = y
        xn_ref[...] = x_ref[...] + gate_ref[0] * y

    return pl.pallas_call(
        body, name="gate_outproj", grid=(t // TM,),
        in_specs=[_row_spec(D_MODEL), _ex_spec(seq // TM), _row_spec(384), _row_spec(384), _row_spec(384),
                  _row_spec(D_CAT), _full_spec((D_CAT, D_MODEL))],
        out_specs=[_row_spec(D_MODEL), _row_spec(D_MODEL)],
        out_shape=[jax.ShapeDtypeStruct((t, D_MODEL), F32)] * 2, compiler_params=_params(("parallel",)),
    )(x, gate, oa, ob, oc, zg, w_out_p)


def _gate_outproj_loss(x, gate, oa, ob, oc, zg, w_out_p, target, g, seq):
    t = x.shape[0]

    def body(x_ref, gate_ref, oa_ref, ob_ref, oc_ref, zg_ref, w_ref, t_ref, g_ref, y_ref, dx_ref, loss_ref, gg_ref):
        @pl.when(pl.program_id(0) == 0)
        def _():
            loss_ref[...] = jnp.zeros_like(loss_ref)
            gg_ref[...] = jnp.zeros_like(gg_ref)

        y = jnp.zeros((TM, D_MODEL), F32)
        for i, o_ref in enumerate((oa_ref, ob_ref, oc_ref)):
            cat = (o_ref[...] * _silu(zg_ref[:, 384 * i:384 * (i + 1)])).astype(BF16)
            y = y + _dot(cat, w_ref[384 * i:384 * (i + 1), :])
        y_ref[...] = y
        gv = g_ref[...]
        out, xh, r = _rms(x_ref[...] + gate_ref[0] * y, gv)
        err = out - t_ref[...]
        loss_ref[...] += 0.5 * jnp.sum(jnp.mean(err * err, axis=-1, keepdims=True), axis=0, keepdims=True)
        dout = err / D_MODEL
        gg_ref[...] += jnp.broadcast_to(jnp.sum(dout * xh, axis=0, keepdims=True), gg_ref.shape)
        dx_ref[...] = _rms_bwd(dout, xh, r, gv)

    return pl.pallas_call(
        body, name="gate_outproj_loss", grid=(t // TM,),
        in_specs=[_row_spec(D_MODEL), _ex_spec(seq // TM), _row_spec(384), _row_spec(384), _row_spec(384),
                  _row_spec(D_CAT), _full_spec((D_CAT, D_MODEL)), _row_spec(D_MODEL), _full_spec((1, D_MODEL))],
        out_specs=[_row_spec(D_MODEL), _row_spec(D_MODEL), _full_spec((8, LANE)), _full_spec((8, D_MODEL))],
        out_shape=[jax.ShapeDtypeStruct((t, D_MODEL), F32), jax.ShapeDtypeStruct((t, D_MODEL), F32),
                   jax.ShapeDtypeStruct((8, LANE), F32), jax.ShapeDtypeStruct((8, D_MODEL), F32)],
        compiler_params=_params(("arbitrary",)),
    )(x, gate, oa, ob, oc, zg, w_out_p, target, g)


def _head_masks(kind, rows, dq, h):
    lq = _lane_iota((rows, dq))
    lv = _lane_iota((rows, LANE))
    mq = (lq >= 64 * h) & (lq < 64 * h + 64)
    if kind == "A":
        mq = mq | ((lq >= 128 + 32 * h) & (lq < 160 + 32 * h))
    return mq, (lv >= 64 * h) & (lv < 64 * h + 64)


def _tile_mask(kind, tile):
    row = lax.broadcasted_iota(jnp.int32, (tile, tile), 0)
    col = lax.broadcasted_iota(jnp.int32, (tile, tile), 1)
    return (col >> 6) <= (row >> 6) if kind == "A" else col <= row


def _attn_scale(kind):
    return 96.0 ** -0.5 if kind == "A" else 0.125


BAND_W = BAND_TILES * T_BAND


def _segments(kind, qi, tile):
    r0 = qi * tile
    if kind == "B":
        lo = max(qi - (BAND_TILES - 1), 0) * tile
        return [(lo, r0 + tile, False, lo - (qi - (BAND_TILES - 1)) * tile)]
    return ([(0, r0, False, 0)] if qi else []) + [(r0, r0 + tile, True, 0)]


def _scores(kind, qh, k_ref, aux_ref, h, seg, tile, scale):
    a, b, diag, c0 = seg
    s = _dot_nt(qh, k_ref[a:b, :]) * (scale * LOG2E)
    if kind == "B":
        return s + aux_ref[h, :, c0:BAND_W]
    if kind == "C":
        s = s - aux_ref[0, h, :, a:b]
    if diag:
        s = jnp.where(_tile_mask(kind, tile), s, NEG)
    return s


FWD_AHEAD = 2
BWD_AHEAD = 2


def _run_ahead(units, first, second, depth):
    queue = [first(*u) for u in units[:depth]]
    for i, (_, h) in enumerate(units):
        if i + depth < len(units):
            queue.append(first(*units[i + depth]))
        second(h, *queue.pop(0))


def _attn_fwd(kind, q, k, v, aux, nb, seq):
    dq = q.shape[1] // 3
    tile = T_BAND if kind == "B" else T_CAUSAL
    nq = seq // tile
    scale = _attn_scale(kind)

    def body(*refs):
        if kind == "A":
            q_ref, k_ref, v_ref, o_ref, lse_ref = refs
            aux_ref = None
        else:
            q_ref, k_ref, v_ref, aux_ref, o_ref, lse_ref = refs
        def logits(qi, h):
            rows = slice(qi * tile, (qi + 1) * tile)
            q2 = q_ref[rows, :]
            mq, _ = _head_masks(kind, tile, dq, h)
            qh = jnp.where(mq, q2, jnp.zeros_like(q2))
            segs = _segments(kind, qi, tile)
            return rows, segs, [_scores(kind, qh, k_ref, aux_ref, h, seg, tile, scale) for seg in segs]

        def finish(h, rows, segs, ss):
            _, mv = _head_masks(kind, tile, dq, h)
            mx = functools.reduce(jnp.maximum, [jnp.max(s, axis=-1, keepdims=True) for s in ss])
            ps = [jnp.exp2(s - mx) for s in ss]
            l = functools.reduce(jnp.add, [jnp.sum(p, axis=-1, keepdims=True) for p in ps])
            acc = functools.reduce(jnp.add, [_dot(p.astype(BF16), v_ref[seg[0]:seg[1], :]) for p, seg in zip(ps, segs)])
            o_h = jnp.where(mv, acc / l, 0.0)
            lse_h = jnp.where(mv, mx + jnp.log(l) * LOG2E, 0.0)
            if h == 0:
                o_ref[rows, :] = o_h
                lse_ref[rows, :] = lse_h
            else:
                o_ref[rows, :] += o_h
                lse_ref[rows, :] += lse_h

        _run_ahead([(qi, h) for qi in range(nq) for h in range(2)], logits, finish, FWD_AHEAD)

    def seq_spec(cols):
        return pl.BlockSpec((seq, cols), lambda b, p: (b, p))

    in_specs = [seq_spec(dq), seq_spec(dq), seq_spec(LANE)]
    args = [q, k, v]
    if kind == "B":
        in_specs.append(pl.BlockSpec((2, tile, BAND_W), lambda b, p: (p, 0, 0)))
        args.append(aux)
    if kind == "C":
        in_specs.append(pl.BlockSpec((1, 2, 1, seq), lambda b, p: (b, p, 0, 0)))
        args.append(aux)
    return pl.pallas_call(
        body, name="attn_fwd_" + kind, grid=(nb, 3), in_specs=in_specs, out_specs=[seq_spec(LANE), seq_spec(LANE)],
        out_shape=[jax.ShapeDtypeStruct((nb * seq, 384), F32)] * 2, compiler_params=_params(("parallel", "parallel")),
    )(*args)


def _attn_bwd(kind, q, k, v, o, do, lse, aux, nb, seq):
    dq = q.shape[1] // 3
    tile = T_BAND if kind == "B" else T_CAUSAL
    nq = seq // tile
    scale = _attn_scale(kind)
    dqk_dtype = F32 if kind == "A" else BF16

    def body(*refs):
        dfr_ref = dfq_ref = dbt_ref = aux_ref = None
        if kind == "A":
            q_ref, k_ref, v_ref, o_ref, do_ref, lse_ref, dq_ref, dk_ref, dv_ref, dkt_acc, dvt_acc = refs
        elif kind == "B":
            q_ref, k_ref, v_ref, o_ref, do_ref, lse_ref, aux_ref, dq_ref, dk_ref, dv_ref, dbt_ref, dkt_acc, dvt_acc = refs
        else:
            (q_ref, k_ref, v_ref, o_ref, do_ref, lse_ref, aux_ref, dq_ref, dk_ref, dv_ref, dfr_ref, dfq_ref,
             dkt_acc, dvt_acc) = refs
        dkt_acc[...] = jnp.zeros_like(dkt_acc)
        dvt_acc[...] = jnp.zeros_like(dvt_acc)
        if kind == "C":
            dfr_ref[...] = jnp.zeros_like(dfr_ref)
        if kind == "B":
            @pl.when(pl.program_id(1) == 0)
            def _():
                dbt_ref[...] = jnp.zeros_like(dbt_ref)

        def products(qi, h):
            rows = slice(qi * tile, (qi + 1) * tile)
            q2 = q_ref[rows, :]
            mq, mv = _head_masks(kind, tile, dq, h)
            qh = jnp.where(mq, q2, jnp.zeros_like(q2))
            doh = jnp.where(mv, do_ref[rows, :], 0.0)
            dob = doh.astype(BF16)
            segs = _segments(kind, qi, tile)
            ts = [_scores(kind, qh, k_ref, aux_ref, h, seg, tile, scale) for seg in segs]
            dps = [_dot_nt(dob, v_ref[seg[0]:seg[1], :]) for seg in segs]
            return rows, segs, qh, doh, ts, dps

        def finish(h, rows, segs, qh, doh, ts, dps):
            mq, mv = _head_masks(kind, tile, dq, h)
            qht = qh.astype(F32).T.astype(BF16)
            dobt = doh.T.astype(BF16)
            head_rows = [(64 * h, 64)] + ([(128 + 32 * h, 32)] if kind == "A" else [])
            delta = jnp.sum(doh * o_ref[rows, :], axis=-1, keepdims=True)
            lseh = jnp.max(jnp.where(mv, lse_ref[rows, :], NEG), axis=-1, keepdims=True)
            rs = jnp.zeros((tile, 1), F32)
            dq_h = jnp.zeros((tile, dq), F32)
            for (a, b, _, c0), t, dp in zip(segs, ts, dps):
                p = jnp.exp2(t - lseh)
                ds = p * (dp - delta)
                if kind == "B":
                    dbt_ref[h, :, c0:BAND_W] += ds
                if kind == "C":
                    dfr_ref[0, h, :, a:b] -= jnp.sum(ds, axis=0, keepdims=True)
                    rs = rs + jnp.sum(ds, axis=-1, keepdims=True)
                dss = (ds * scale).astype(BF16)
                dvt_acc[64 * h:64 * h + 64, a:b] += _dot(dobt[64 * h:64 * h + 64, :], p.astype(BF16))
                for r0, n in head_rows:
                    dkt_acc[r0:r0 + n, a:b] += _dot(qht[r0:r0 + n, :], dss)
                dq_h = dq_h + _dot(dss, k_ref[a:b, :])
            dq_h = jnp.where(mq, dq_h, 0.0).astype(dqk_dtype)
            if h == 0:
                dq_ref[rows, :] = dq_h
            else:
                dq_ref[rows, :] += dq_h
            if kind == "C":
                if h == 0:
                    dfq_ref[rows, :] = jnp.where(mv, rs, 0.0)
                else:
                    dfq_ref[rows, :] += jnp.where(mv, rs, 0.0)

        _run_ahead([(qi, h) for qi in range(nq) for h in range(2)], products, finish, BWD_AHEAD)
        for j in range(seq // 256):
            cols = slice(256 * j, 256 * (j + 1))
            dk_ref[cols, :] = dkt_acc[:, cols].T.astype(dqk_dtype)
            dv_ref[cols, :] = dvt_acc[:, cols].T.astype(BF16)

    def seq_spec(cols):
        return pl.BlockSpec((seq, cols), lambda p, b: (b, p))

    in_specs = [seq_spec(dq), seq_spec(dq), seq_spec(LANE), seq_spec(LANE), seq_spec(LANE), seq_spec(LANE)]
    args = [q, k, v, o, do, lse]
    out_specs = [seq_spec(dq), seq_spec(dq), seq_spec(LANE)]
    out_shape = [jax.ShapeDtypeStruct((nb * seq, 3 * dq), dqk_dtype)] * 2 + [jax.ShapeDtypeStruct((nb * seq, 384), BF16)]
    if kind == "B":
        spec = pl.BlockSpec((2, tile, BAND_W), lambda p, b: (p, 0, 0))
        in_specs.append(spec)
        args.append(aux)
        out_specs.append(spec)
        out_shape.append(jax.ShapeDtypeStruct((6, tile, BAND_W), F32))
    if kind == "C":
        spec = pl.BlockSpec((1, 2, 1, seq), lambda p, b: (b, p, 0, 0))
        in_specs.append(spec)
        args.append(aux)
        out_specs += [spec, seq_spec(LANE)]
        out_shape += [jax.ShapeDtypeStruct((nb, 6, 1, seq), F32), jax.ShapeDtypeStruct((nb * seq, 384), F32)]
    return pl.pallas_call(
        body, name="attn_bwd_" + kind, grid=(3, nb), in_specs=in_specs, out_specs=out_specs, out_shape=out_shape,
        scratch_shapes=[pltpu.VMEM((dq, seq), F32), pltpu.VMEM((LANE, seq), F32)],
        compiler_params=_params(("arbitrary", "arbitrary")),
    )(*args)


BIAS_G = 768
BIAS_EDGE = BIAS_G - N_REL


def _bias_line(rel_bias):
    g = jnp.concatenate([jnp.broadcast_to(rel_bias[:, N_REL - 1:], (rel_bias.shape[0], BIAS_EDGE)),
                         jnp.flip(rel_bias, axis=1)], axis=1)
    return jnp.pad(g, ((0, 8 - g.shape[0]), (0, 0)))


def _bias_unline(dg):
    return jnp.flip(dg[:, BIAS_EDGE:], axis=1)


def _bias_expand(g8):
    def body(g_ref, out_ref):
        line = jnp.broadcast_to(g_ref[0] * LOG2E, (T_BAND, BIAS_G))
        slab = pltpu.roll(line, 1, 1, stride=1, stride_axis=0)[:, LANE:BIAS_G]
        row = lax.broadcasted_iota(jnp.int32, (T_BAND, BAND_W), 0)
        col = lax.broadcasted_iota(jnp.int32, (T_BAND, BAND_W), 1)
        hidden = ((row >= 64) & (col < 64)) | ((row < 64) & (col >= BAND_W - 64))
        out_ref[0] = jnp.where(hidden, NEG, slab)

    return pl.pallas_call(
        body, name="bias_expand", grid=(8,), in_specs=[pl.BlockSpec((1, 1, BIAS_G), lambda h: (h, 0, 0))],
        out_specs=pl.BlockSpec((1, T_BAND, BAND_W), lambda h: (h, 0, 0)),
        out_shape=jax.ShapeDtypeStruct((8, T_BAND, BAND_W), F32), compiler_params=_params(("parallel",)),
    )(g8.reshape(8, 1, BIAS_G))


def _bias_reduce(d_slab):
    def body(d_ref, out_ref):
        r = lax.broadcasted_iota(jnp.int32, (T_BAND, T_BAND), 0)
        k = lax.broadcasted_iota(jnp.int32, (T_BAND, T_BAND), 1)
        flip = jnp.where(r + k == T_BAND - 1, 1.0, 0.0).astype(BF16)
        hi, mid, lo = _split3(d_ref[0])
        d_rev = _dot(flip, hi) + _dot(flip, mid) + _dot(flip, lo)
        wide = jnp.concatenate([jnp.zeros((T_BAND, LANE), F32), d_rev, jnp.zeros((T_BAND, 2 * LANE), F32)], axis=1)
        skew = pltpu.roll(wide, 0, 1, stride=1, stride_axis=0)
        dg = jnp.sum(skew, axis=0, keepdims=True)[:, LANE:LANE + BIAS_G]
        lane = _lane_iota((1, BIAS_G))
        clipped = jnp.sum(jnp.where(lane <= BIAS_EDGE, dg, 0.0), axis=1, keepdims=True)
        out_ref[0] = jnp.where(lane == BIAS_EDGE, clipped, dg)

    return pl.pallas_call(
        body, name="bias_reduce", grid=(8,), in_specs=[pl.BlockSpec((1, T_BAND, BAND_W), lambda h: (h, 0, 0))],
        out_specs=pl.BlockSpec((1, 1, BIAS_G), lambda h: (h, 0, 0)),
        out_shape=jax.ShapeDtypeStruct((8, 1, BIAS_G), F32), compiler_params=_params(("parallel",)),
    )(d_slab).reshape(8, BIAS_G)


def _outproj_bwd(dxn, y, gate, oa, ob, oc, zg, w_out_p, w_out_pt, nb, seq):
    t = dxn.shape[0]
    tpe = seq // TM

    def body(dxn_ref, y_ref, gate_ref, oa_ref, ob_ref, oc_ref, zg_ref, w_ref, wt_ref,
             doa_ref, dob_ref, doc_ref, dzg_ref, gw_ref, dgate_ref):
        i = pl.program_id(0)

        @pl.when(i == 0)
        def _():
            gw_ref[...] = jnp.zeros_like(gw_ref)

        @pl.when(i % tpe == 0)
        def _():
            dgate_ref[...] = jnp.zeros_like(dgate_ref)

        dxn_t = dxn_ref[...]
        dgate_ref[0] += jnp.sum(dxn_t * y_ref[...], axis=0, keepdims=True)
        dy = (dxn_t * gate_ref[0]).astype(BF16)
        for gi, (o_ref, do_ref) in enumerate(((oa_ref, doa_ref), (ob_ref, dob_ref), (oc_ref, doc_ref))):
            cols = slice(384 * gi, 384 * (gi + 1))
            u = zg_ref[:, cols]
            o_t = o_ref[...]
            su = _silu(u)
            dcat = _dot(dy, wt_ref[:, cols])
            do_ref[...] = dcat * su
            dzg_ref[:, cols] = (dcat * o_t * _dsilu(u)).astype(BF16)
            gw_ref[cols, :] += _dot_tn((o_t * su).astype(BF16), dy)

    return pl.pallas_call(
        body, name="outproj_bwd", grid=(t // TM,),
        in_specs=[_row_spec(D_MODEL), _row_spec(D_MODEL), _ex_spec(tpe), _row_spec(384), _row_spec(384), _row_spec(384),
                  _row_spec(D_CAT), _full_spec((D_CAT, D_MODEL)), _full_spec((D_MODEL, D_CAT))],
        out_specs=[_row_spec(384), _row_spec(384), _row_spec(384), _row_spec(D_CAT), _full_spec((D_CAT, D_MODEL)),
                   _ex_spec(tpe)],
        out_shape=[jax.ShapeDtypeStruct((t, 384), F32)] * 3 + [jax.ShapeDtypeStruct((t, D_CAT), BF16),
                                                                jax.ShapeDtypeStruct((D_CAT, D_MODEL), F32),
                                                                jax.ShapeDtypeStruct((nb, 1, D_MODEL), F32)],
        compiler_params=_params(("arbitrary",)),
    )(dxn, y, gate, oa, ob, oc, zg, w_out_p, w_out_pt)


def _a_up_bwd(dqa, dka, dva, za, gq, gkv, w_uq_pt, w_ukv_pt, cos, sina, sinb):
    t = za.shape[0]

    def body(dq_ref, dk_ref, dv_ref, za_ref, gq_ref, gkv_ref, wqt_ref, wkvt_ref, cos_ref, sa_ref, sb_ref,
             dza_ref, gwq_ref, gwkv_ref, ggq_ref, ggkv_ref, dqb, dkvb):
        @pl.when(pl.program_id(0) == 0)
        def _():
            gwq_ref[...] = jnp.zeros_like(gwq_ref)
            gwkv_ref[...] = jnp.zeros_like(gwkv_ref)
            ggq_ref[...] = jnp.zeros_like(ggq_ref)
            ggkv_ref[...] = jnp.zeros_like(ggkv_ref)

        cos_t, sa, sb = cos_ref[...], sa_ref[...], sb_ref[...]
        dkpe = jnp.zeros((TM, LANE), F32)
        for p in range(3):
            dqb[:, 256 * p:256 * p + 128] = dq_ref[:, 256 * p:256 * p + 128].astype(BF16)
            dqb[:, 256 * p + 128:256 * p + 256] = _rope_t(dq_ref[:, 256 * p + 128:256 * p + 256], cos_t, sa, sb).astype(BF16)
            dkvb[:, 128 * p:128 * p + 128] = dk_ref[:, 256 * p:256 * p + 128].astype(BF16)
            dkpe = dkpe + dk_ref[:, 256 * p + 128:256 * p + 256]
        dkvb[:, 384:768] = dv_ref[...]
        dkpe = _rope_t(dkpe, cos_t, sa, sb)
        dkpe = jnp.where(_lane_iota((TM, LANE)) < A_ROPE, dkpe + pltpu.roll(dkpe, LANE - 32, 1), 0.0)

        gqv = gq_ref[...]
        cqn, cqh, rq = _rms(za_ref[:, 0:384], gqv)
        dq_t = dqb[...]
        gwq_ref[...] += _dot_tn(cqn.astype(BF16), dq_t)
        dcqn = _dot(dq_t, wqt_ref[...])
        ggq_ref[...] += jnp.broadcast_to(jnp.sum(dcqn * cqh, axis=0, keepdims=True), ggq_ref.shape)
        dza_ref[:, 0:384] = _rms_bwd(dcqn, cqh, rq, gqv).astype(BF16)

        gkvv = gkv_ref[...]
        ckvn, ckvh, rkv = _rms(za_ref[:, 384:640], gkvv)
        dkv_t = dkvb[...]
        gwkv_ref[...] += _dot_tn(ckvn.astype(BF16), dkv_t)
        dckvn = _dot(dkv_t, wkvt_ref[...])
        ggkv_ref[...] += jnp.broadcast_to(jnp.sum(dckvn * ckvh, axis=0, keepdims=True), ggkv_ref.shape)
        dza_ref[:, 384:640] = _rms_bwd(dckvn, ckvh, rkv, gkvv).astype(BF16)
        dza_ref[:, 640:768] = dkpe.astype(BF16)

    return pl.pallas_call(
        body, name="a_up_bwd", grid=(t // TM,),
        in_specs=[_row_spec(768), _row_spec(768), _row_spec(384), _row_spec(768), _full_spec((1, 384)),
                  _full_spec((1, 256)), _full_spec((768, 384)), _full_spec((768, 256)), _row_spec(LANE), _row_spec(LANE),
                  _row_spec(LANE)],
        out_specs=[_row_spec(768), _full_spec((384, 768)), _full_spec((256, 768)), _full_spec((8, 384)),
                   _full_spec((8, 256))],
        out_shape=[jax.ShapeDtypeStruct((t, 768), BF16), jax.ShapeDtypeStruct((384, 768), F32),
                   jax.ShapeDtypeStruct((256, 768), F32), jax.ShapeDtypeStruct((8, 384), F32),
                   jax.ShapeDtypeStruct((8, 256), F32)],
        scratch_shapes=[pltpu.VMEM((TM, 768), BF16), pltpu.VMEM((TM, 768), BF16)],
        compiler_params=_params(("arbitrary",)),
    )(dqa, dka, dva, za, gq, gkv, w_uq_pt, w_ukv_pt, cos, sina, sinb)


def _dz_cols():
    return (Z_A, Z_G) + Z_QKV + (Z_F,)


def _inproj_bwd_dx(dz, dxn, x, shift, scale, g, w_in_pt, nb, seq):
    t = x.shape[0]
    tpe = seq // TM
    cols = _dz_cols()

    def body(*refs):
        dz_refs = refs[:len(cols)]
        dxn_ref, x_ref, sh_ref, sc_ref, g_ref, wt_ref, dx_ref, dsh_ref, dsc_ref, dg_ref = refs[len(cols):]
        i = pl.program_id(0)

        @pl.when(i == 0)
        def _():
            dg_ref[...] = jnp.zeros_like(dg_ref)

        @pl.when(i % tpe == 0)
        def _():
            dsh_ref[...] = jnp.zeros_like(dsh_ref)
            dsc_ref[...] = jnp.zeros_like(dsc_ref)

        dh = jnp.zeros((TM, D_MODEL), F32)
        for ref, (c0, c1) in zip(dz_refs, cols):
            dh = dh + _dot_nt(ref[...], wt_ref[:, c0:c1])
        gv = g_ref[...]
        n, xh, r = _rms(x_ref[...], gv)
        dsh_ref[0] += jnp.sum(dh, axis=0, keepdims=True)
        dsc_ref[0] += jnp.sum(dh * n, axis=0, keepdims=True)
        dn = dh * (1.0 + sc_ref[0])
        dg_ref[...] += jnp.broadcast_to(jnp.sum(dn * xh, axis=0, keepdims=True), dg_ref.shape)
        dx_ref[...] = dxn_ref[...] + _rms_bwd(dn, xh, r, gv)

    in_specs = [_row_spec(c1 - c0) for c0, c1 in cols]
    in_specs += [_row_spec(D_MODEL), _row_spec(D_MODEL), _ex_spec(tpe), _ex_spec(tpe), _full_spec((1, D_MODEL)),
                 _full_spec((D_MODEL, NP_IN))]
    return pl.pallas_call(
        body, name="inproj_bwd_dx", grid=(t // TM,), in_specs=in_specs,
        out_specs=[_row_spec(D_MODEL), _ex_spec(tpe), _ex_spec(tpe), _full_spec((8, D_MODEL))],
        out_shape=[jax.ShapeDtypeStruct((t, D_MODEL), F32), jax.ShapeDtypeStruct((nb, 1, D_MODEL), F32),
                   jax.ShapeDtypeStruct((nb, 1, D_MODEL), F32), jax.ShapeDtypeStruct((8, D_MODEL), F32)],
        compiler_params=_params(("arbitrary",)),
    )(*dz, dxn, x, shift, scale, g, w_in_pt)


def _inproj_bwd_dw(h, dz, name, layer, both=None):
    t = h.shape[0]
    widths = [d.shape[1] for d in dz]
    total = sum(widths)

    def body(*refs):
        h_ref = refs[0]
        dz_refs = refs[1:1 + len(dz)]
        gw_ref = refs[-1]

        @pl.when(pl.program_id(0) == 0)
        def _():
            gw_ref[...] = jnp.zeros_like(gw_ref)

        h_t = h_ref[...]
        c0 = 0
        for ref, w in zip(dz_refs, widths):
            gw_ref[0, :, c0:c0 + w] += _dot_tn(h_t, ref[...])
            c0 += w

    in_specs = [_row_spec(D_MODEL)] + [_row_spec(w) for w in widths]
    args = [h, *dz]
    aliases = {}
    if both is not None:
        in_specs.append(pl.BlockSpec(memory_space=pl.ANY))
        aliases = {len(args): 0}
        args.append(both)
    return pl.pallas_call(
        body, name=name, grid=(t // TM,), in_specs=in_specs,
        out_specs=pl.BlockSpec((1, D_MODEL, total), lambda i: (layer, 0, 0)),
        out_shape=jax.ShapeDtypeStruct((DEPTH, D_MODEL, total), F32), input_output_aliases=aliases,
        compiler_params=_params(("arbitrary",)),
    )(*args)


def _ada_bwd(c_all, dmod_cols, dmod_all):
    n = c_all.shape[0]
    cols = dmod_cols.shape[2]

    def body(c_ref, dc_ref, da_ref, gw_ref, gb_ref):
        act = _silu(c_ref[...]).astype(BF16)
        gw_ref[0] = _dot_tn(act, dc_ref[0].astype(BF16))
        gb_ref[0] = jnp.sum(da_ref[0], axis=0, keepdims=True)

    return pl.pallas_call(
        body, name="ada_bwd", grid=(DEPTH,),
        in_specs=[pl.BlockSpec((n, D_MODEL), lambda l: (0, 0)), pl.BlockSpec((1, n, cols), lambda l: (l, 0, 0)),
                  pl.BlockSpec((1, n, 3 * D_MODEL), lambda l: (l, 0, 0))],
        out_specs=[pl.BlockSpec((1, D_MODEL, cols), lambda l: (l, 0, 0)),
                   pl.BlockSpec((1, 1, 3 * D_MODEL), lambda l: (l, 0, 0))],
        out_shape=[jax.ShapeDtypeStruct((DEPTH, D_MODEL, cols), F32), jax.ShapeDtypeStruct((DEPTH, 1, 3 * D_MODEL), F32)],
        compiler_params=_params(("parallel",)),
    )(c_all, dmod_cols, dmod_all)


def _sum_blocks(parts, name):
    n, rows, cols = parts.shape
    tr = rows if rows <= 256 else 8 * next(d for d in range(32, 0, -1) if (rows // 8) % d == 0)

    def body(p_ref, out_ref):
        acc = p_ref[0].astype(F32)
        for k in range(1, n):
            acc = acc + p_ref[k].astype(F32)
        out_ref[...] = acc

    return pl.pallas_call(
        body, name=name, grid=(rows // tr,), in_specs=[pl.BlockSpec((n, tr, cols), lambda i: (0, i, 0))],
        out_specs=pl.BlockSpec((tr, cols), lambda i: (i, 0)), out_shape=jax.ShapeDtypeStruct((rows, cols), F32),
        compiler_params=_params(("parallel",)),
    )(parts)


def _adamw(w, g, m, v, name):
    nl, rows, cols = w.shape
    if rows * cols <= 64 * 1024:
        tr = rows
        tl = next(t for t in range(nl, 0, -1) if nl % t == 0 and t * max(rows, 8) * cols <= 512 * 1024)
    else:
        tl = 1
        tr = next(t for t in (rows, 256, 128, 64, 32, 16, 8) if rows % t == 0 and t * cols <= 256 * 1024)

    def body(w_ref, g_ref, m_ref, v_ref, d_ref, mo_ref, vo_ref):
        gv = g_ref[...]
        mn = ADAM_B1 * m_ref[...] + (1.0 - ADAM_B1) * gv
        vn = ADAM_B2 * v_ref[...] + (1.0 - ADAM_B2) * jnp.square(gv)
        m_hat = mn / (1.0 - ADAM_B1 ** ADAM_STEP)
        v_hat = vn / (1.0 - ADAM_B2 ** ADAM_STEP)
        d_ref[...] = -ADAM_LR * (m_hat / (jnp.sqrt(v_hat) + ADAM_EPS) + ADAM_WD * w_ref[...])
        mo_ref[...] = mn
        vo_ref[...] = vn

    spec = pl.BlockSpec((tl, tr, cols), lambda l, i: (l, i, 0))
    return pl.pallas_call(
        body, name=name, grid=(nl // tl, rows // tr), in_specs=[spec] * 4, out_specs=[spec] * 3,
        out_shape=[jax.ShapeDtypeStruct((nl, rows, cols), F32)] * 3, compiler_params=_params(("parallel", "parallel")),
    )(w, g, m, v)


ALL_FLIPS = tuple(range(1, N_DEV))


def _transfer(name, srcs, dst_shapes, plan, side=None):
    n_arr = len(srcs)
    probe = plan(0, 0, 0)
    n_steps = len(probe)
    side_in = [a for a, _ in side["inputs"]] if side else []
    side_out = side["outputs"] if side else []

    def body(*refs):
        src_refs, refs = refs[:n_arr], refs[n_arr:]
        side_in_refs, refs = refs[:len(side_in)], refs[len(side_in):]
        dst_refs, refs = refs[:n_arr], refs[n_arr:]
        side_out_refs, refs = refs[:len(side_out)], refs[len(side_out):]
        send_sems, recv_sems, local_sems = refs[:3]
        side_scratch = refs[3:]
        x, y, c = lax.axis_index("x"), lax.axis_index("y"), lax.axis_index("c")
        steps = plan(x, y, c)
        side_done = []

        def rows(ref, r0, n):
            return ref.at[:, pl.ds(r0, n), :]

        def arrival(t):
            a, _, _, n, _, f, _ = steps[t]
            return pltpu.make_async_remote_copy(
                src_ref=rows(dst_refs[a], 0, n), dst_ref=rows(dst_refs[a], 0, n), send_sem=send_sems.at[t],
                recv_sem=recv_sems.at[t], device_id=(x, y, c), device_id_type=pl.DeviceIdType.MESH)

        arrived, started = set(), []
        for t, (a, from_dst, sr, n, dr, f, after) in enumerate(steps):
            if side and t == side["at"]:
                after = tuple(after) + tuple(side["needs"])
            for u in after:
                if u not in arrived:
                    if steps[u][5] == 0:
                        started[u].wait()
                    else:
                        arrival(u).wait_recv()
                    arrived.add(u)
            if side and t == side["at"]:
                side_done.append(side["run"](x, y, c, dst_refs, side_in_refs, side_out_refs, side_scratch))
            src = rows(dst_refs[a] if from_dst else src_refs[a], sr, n)
            dst = rows(dst_refs[a], dr, n)
            if f == 0:
                cp = pltpu.make_async_copy(src, dst, local_sems.at[t])
            else:
                to = (1 - x if f & 4 else x, 1 - y if f & 2 else y, 1 - c if f & 1 else c)
                cp = pltpu.make_async_remote_copy(src_ref=src, dst_ref=dst, send_sem=send_sems.at[t],
                                                  recv_sem=recv_sems.at[t], device_id=to,
                                                  device_id_type=pl.DeviceIdType.MESH)
            cp.start()
            started.append(cp)
        for t, step in enumerate(steps):
            if step[5] != 0 and t not in arrived:
                arrival(t).wait_recv()
        for t, (cp, step) in enumerate(zip(started, steps)):
            if step[5] != 0:
                cp.wait_send()
            elif t not in arrived:
                cp.wait()
        for wait in side_done:
            wait()

    any_spec = pl.BlockSpec(memory_space=pl.ANY)
    return pl.pallas_call(
        body, name=name, out_shape=[jax.ShapeDtypeStruct(s, d) for s, d in list(dst_shapes) + list(side_out)],
        in_specs=[any_spec] * n_arr + [spec for _, spec in (side["inputs"] if side else [])],
        out_specs=[any_spec] * (n_arr + len(side_out)),
        scratch_shapes=[pltpu.SemaphoreType.DMA((n_steps,)), pltpu.SemaphoreType.DMA((n_steps,)),
                        pltpu.SemaphoreType.DMA((n_steps,))] + (side["scratch"] if side else []),
    )(*srcs, *side_in)


CHIP_FLIPS = (2, 4, 6)


def _gather_plan(chip_rows):
    def plan(x, y, c):
        steps = []
        chip = 2 * x + y
        for a, rc in enumerate(chip_rows):
            h = rc // 2
            first = (h // 32) * 16
            mine = rc * chip + h * c
            from_x, from_y, diag = rc * (chip ^ 2) + h * c, rc * (chip ^ 1) + h * c, rc * (chip ^ 3) + h * c
            steps.append((a, False, h * c, h, mine, 0, ()))
            to_x = len(steps)
            steps.append((a, False, h * c, h, mine, 4, ()))
            to_y = len(steps)
            steps.append((a, False, h * c, h, mine, 2, ()))
            fwd_y = len(steps)
            steps.append((a, True, from_x, first, from_x, 2, (to_x,)))
            fwd_x = len(steps)
            steps.append((a, True, from_y + first, h - first, from_y + first, 4, (to_y,)))
            steps.append((a, False, h * c, h, mine, 1, ()))
            steps.append((a, True, from_x, h, from_x, 1, (to_x,)))
            steps.append((a, True, from_y, h, from_y, 1, (to_y,)))
            steps.append((a, True, diag, first, diag, 1, (fwd_y,)))
            steps.append((a, True, diag + first, h - first, diag + first, 1, (fwd_x,)))
        return steps
    return plan


def _pair_reduce_plan(chip_rows):
    def plan(x, y, c):
        steps = []
        for a, rc in enumerate(chip_rows):
            h = rc // 2
            for j in range(4):
                steps.append((a, False, rc * j + h * (1 - c), h, h * j, 1, ()))
        return steps
    return plan


def _chip_scatter_plan(chip_rows):
    def plan(x, y, c):
        steps = []
        for a, rc in enumerate(chip_rows):
            h = rc // 2
            mine = h * (2 * x + y)
            arrivals = []
            for k, f in enumerate(CHIP_FLIPS):
                arrivals.append(len(steps))
                steps.append((a, False, h * ((2 * x + y) ^ (f >> 1)), h, h * k, f, ()))
            steps.append((a, False, mine, h, 3 * h, 0, ()))
            steps.append((a, False, mine, h, 7 * h, 1, ()))
            for k, t in enumerate(arrivals):
                steps.append((a, True, h * k, h, (4 + k) * h, 1, (t,)))
        return steps
    return plan


def _tile_rows(h):
    return h


def _sum_pair(partial, recv, core, rc, name):
    nl, _, cols = partial.shape
    h = rc // 2
    tr = _tile_rows(h)

    def body(c_ref, p_ref, r_ref, out_ref):
        out_ref[...] = (p_ref[...] + r_ref[...]).astype(BF16)

    spec = pl.BlockSpec((1, tr, cols), lambda l, j, i, c_ref: (l, (h // tr) * j + i, 0))
    return pl.pallas_call(
        body, name=name, out_shape=jax.ShapeDtypeStruct((nl, 4 * h, cols), BF16),
        grid_spec=pltpu.PrefetchScalarGridSpec(
            num_scalar_prefetch=1, grid=(nl, 4, h // tr),
            in_specs=[pl.BlockSpec((1, tr, cols), lambda l, j, i, c_ref: (l, (rc // tr) * j + (h // tr) * c_ref[0] + i, 0)),
                      spec],
            out_specs=spec),
        compiler_params=_params(("parallel", "parallel", "parallel")),
    )(core, partial, recv)


def _sum_chips(slots, core, rc, name):
    nl, _, cols = slots.shape
    h = rc // 2

    def body(c_ref, s_ref, out_ref):
        acc = s_ref[0, 3].astype(F32)
        for k in range(3):
            acc = acc + s_ref[0, k].astype(F32)
        out_ref[0] = acc

    return pl.pallas_call(
        body, name=name, out_shape=jax.ShapeDtypeStruct((nl, rc, cols), F32),
        grid_spec=pltpu.PrefetchScalarGridSpec(
            num_scalar_prefetch=1, grid=(nl, 2),
            in_specs=[pl.BlockSpec((1, 4, h, cols), lambda l, q, c_ref: (l, q, 0, 0))],
            out_specs=pl.BlockSpec((1, h, cols), lambda l, q, c_ref: (l, q + c_ref[0] - 2 * q * c_ref[0], 0))),
        compiler_params=_params(("parallel", "parallel")),
    )(core, slots.reshape(nl, 8, h, cols))


def _pad_cols(a, n):
    return a if n == 0 else jnp.pad(a, ((0, 0), (0, n)))


def _in_to_padded(w):
    return jnp.concatenate([_pad_cols(w[:, a:b], z) for a, b, z in IN_PIECES], axis=1)


def _in_cols_first(ga, gb):
    split = Z_G[1]
    pos, out = 0, {}
    for a, b, z in IN_PIECES:
        src, off = (ga, pos) if pos < split else (gb, pos - split)
        out[a] = jnp.transpose(src[:, :, off:off + (b - a)], (2, 0, 1))
        pos += (b - a) + z
    return jnp.concatenate([out[a] for a in sorted(out)], axis=0)


def _out_to_padded(w):
    z = jnp.zeros((64, w.shape[1]), w.dtype)
    return jnp.concatenate([w[0:384], w[384:704], z, w[704:1024], z], axis=0)


def _out_from_padded(gp):
    return jnp.concatenate([gp[0:384], gp[384:704], gp[768:1088]], axis=0)


def _uq_to_padded(w):
    parts = []
    for p in range(3):
        h0, h1 = 2 * p, 2 * p + 1
        parts += [w[:, 96 * h0:96 * h0 + 64], w[:, 96 * h1:96 * h1 + 64], w[:, 96 * h0 + 64:96 * h0 + 96],
                  w[:, 96 * h1 + 64:96 * h1 + 96], jnp.zeros((w.shape[0], 64), w.dtype)]
    return jnp.concatenate(parts, axis=1)


def _uq_from_padded(gp):
    parts = []
    for h in range(6):
        p, s = h // 2, h % 2
        parts += [gp[:, 256 * p + 64 * s:256 * p + 64 * s + 64], gp[:, 256 * p + 128 + 32 * s:256 * p + 160 + 32 * s]]
    return jnp.concatenate(parts, axis=1)


def _ukv_to_padded(w):
    return jnp.concatenate([w[:, 128 * h:128 * h + 64] for h in range(6)]
                           + [w[:, 128 * h + 64:128 * h + 128] for h in range(6)], axis=1)


def _ukv_from_padded(gp):
    parts = []
    for h in range(6):
        parts += [gp[:, 64 * h:64 * h + 64], gp[:, 384 + 64 * h:384 + 64 * h + 64]]
    return jnp.concatenate(parts, axis=1)


LR_ROWS = 224
SHARD_ROWS = (256, 256, 256, LR_ROWS)


def _pack_lowrank(w_uq, w_ukv):
    flat = jnp.concatenate([w_uq.reshape(-1), w_ukv.reshape(-1)])
    return jnp.pad(flat, (0, LR_ROWS * PACK_COLS - flat.shape[0])).reshape(1, LR_ROWS, PACK_COLS)


def _unpack_lowrank(packed):
    flat = packed.reshape(-1)
    n_uq = DEPTH * A_Q_RANK * 144
    n_ukv = DEPTH * A_KV_RANK * 192
    return flat[0:n_uq].reshape(DEPTH, A_Q_RANK, 144), flat[n_uq:n_uq + n_ukv].reshape(DEPTH, A_KV_RANK, 192)


def _rope_tables(positions):
    t = positions.size
    inv = ROPE_THETA ** (-jnp.arange(0, A_ROPE, 2, dtype=F32) / A_ROPE)
    inv_row = jnp.pad(jnp.tile(inv, 4), (0, 64)).reshape(1, LANE)

    def body(p_ref, i_ref, c_ref, sa_ref, sb_ref):
        ang = p_ref[...].astype(F32) * i_ref[...]
        lane = _lane_iota((TM, LANE))
        live = lane < 64
        second = (lane & 31) >= 16
        s = jnp.sin(ang)
        c_ref[...] = jnp.where(live, jnp.cos(ang), 0.0)
        sa_ref[...] = jnp.where(live & jnp.logical_not(second), -s, 0.0)
        sb_ref[...] = jnp.where(live & second, s, 0.0)

    return pl.pallas_call(
        body, name="rope_tables", grid=(t // TM,), in_specs=[_row_spec(1), _full_spec((1, LANE))],
        out_specs=[_row_spec(LANE)] * 3, out_shape=[jax.ShapeDtypeStruct((t, LANE), F32)] * 3,
        compiler_params=_params(("parallel",)),
    )(positions.reshape(t, 1), inv_row)


def _rows(a, n):
    flat = a.reshape(-1)
    return jnp.pad(flat, (0, n * LANE - flat.shape[0])).reshape(n, LANE)


def _forward_backward(x, mod, tables, target, weights, small, nb, seq):
    cos, sina, sinb = tables
    saved = []
    for l in range(DEPTH):
        w, s = weights[l], small[l]
        shift = mod[l][:, None, 0:D_MODEL]
        scale = mod[l][:, None, D_MODEL:2 * D_MODEL]
        gate = mod[l][:, None, 2 * D_MODEL:]
        h, za, zg, qb, kb, vb, qc, kc, vc, zf = _ln_inproj(x, shift, scale, s["norm_g"], w["in"], seq)
        qa, ka, va = _a_up(za, s["gq"], s["gkv"], w["uq"], w["ukv"], cos, sina, sinb)
        bias = _bias_expand(s["g8"])[0:6]
        f = _forget_fwd(zf, s["fb"], nb, seq)
        frow = jnp.pad(f[:, 0:5].reshape(nb, seq, 5).transpose(0, 2, 1), ((0, 0), (0, 1), (0, 0)))
        frow = frow.reshape(nb, 6, 1, seq)
        oa, lse_a = _attn_fwd("A", qa, ka, va, None, nb, seq)
        ob, lse_b = _attn_fwd("B", qb, kb, vb, bias, nb, seq)
        oc, lse_c = _attn_fwd("C", qc, kc, vc, frow, nb, seq)
        if l + 1 < DEPTH:
            y, x_next = _gate_outproj(x, gate, oa, ob, oc, zg, w["out"], seq)
        else:
            y, dx, loss8, gfinal8 = _gate_outproj_loss(x, gate, oa, ob, oc, zg, w["out"], target, s["final_g"], seq)
            x_next = None
        saved.append(dict(x=x, h=h, za=za, zg=zg, zf=zf, y=y, shift=shift, scale=scale, gate=gate, bias=bias, frow=frow,
                          a=(qa, ka, va, oa, lse_a), b=(qb, kb, vb, ob, lse_b), c=(qc, kc, vc, oc, lse_c)))
        x = x_next
    grads = []
    gw_in = (None, None)
    for l in reversed(range(DEPTH)):
        w, s, sv = weights[l], small[l], saved[l]
        qa, ka, va, oa, lse_a = sv["a"]
        qb, kb, vb, ob, lse_b = sv["b"]
        qc, kc, vc, oc, lse_c = sv["c"]
        doa, dob, doc, dzg, gw_out, dgate = _outproj_bwd(dx, sv["y"], sv["gate"], oa, ob, oc, sv["zg"], w["out"],
                                                          w["out_t"], nb, seq)
        dqa, dka, dva = _attn_bwd("A", qa, ka, va, oa, doa, lse_a, None, nb, seq)
        dqb, dkb, dvb, dbt = _attn_bwd("B", qb, kb, vb, ob, dob, lse_b, sv["bias"], nb, seq)
        dqc, dkc, dvc, dfr, dfq = _attn_bwd("C", qc, kc, vc, oc, doc, lse_c, sv["frow"], nb, seq)
        dg = _bias_reduce(jnp.pad(dbt, ((0, 2), (0, 0), (0, 0))))
        grb = jnp.pad(_bias_unline(dg), ((0, 0), (0, 384 - N_REL)))
        dfk = dfr.reshape(nb, 6, seq).transpose(0, 2, 1).reshape(nb * seq, 6)
        dzf, gfb = _forget_bwd(dfq, jnp.pad(dfk, ((0, 0), (0, LANE - 6))), sv["zf"], s["fb"], nb, seq)
        dza, gw_uq, gw_ukv, ggq, ggkv = _a_up_bwd(dqa, dka, dva, sv["za"], s["gq"], s["gkv"], w["uq_t"], w["ukv_t"],
                                                  cos, sina, sinb)
        dz = (dza, dzg, dqb, dkb, dvb, dqc, dkc, dvc, dzf)
        dx, dshift, dscale, gnorm = _inproj_bwd_dx(dz, dx, sv["x"], sv["shift"], sv["scale"], s["norm_g"], w["in"],
                                                   nb, seq)
        gw_in = (_inproj_bwd_dw(sv["h"], dz[0:2], "inproj_bwd_dw0", l, gw_in[0]),
                 _inproj_bwd_dw(sv["h"], dz[2:], "inproj_bwd_dw1", l, gw_in[1]))
        dmod = jnp.concatenate([dshift[:, 0], dscale[:, 0], dgate[:, 0]], axis=1)
        grads.append(dict(w_out=gw_out, w_uq=gw_uq, w_ukv=gw_ukv, dmod=dmod, norm_g=gnorm[0], gq=ggq[0],
                          gkv=ggkv[0], rb8=grb, fb=gfb[0]))
    grads.reverse()
    return loss8[0, 0], dx, grads, gfinal8[0], gw_in


def _layer_weights(w_in, w_out, w_uq, w_ukv):
    wi, wo, wq, wkv = _in_to_padded(w_in), _out_to_padded(w_out), _uq_to_padded(w_uq), _ukv_to_padded(w_ukv)
    return {"in": wi, "out": wo, "out_t": wo.T, "uq": wq, "uq_t": wq.T, "ukv": wkv, "ukv_t": wkv.T}


def _layer_small(norm_g, gq, gkv, rel_bias, forget_b, final_g):
    fb = jnp.pad(forget_b, (0, LANE - 5)).reshape(1, LANE)
    return dict(norm_g=norm_g.reshape(1, -1), gq=gq.reshape(1, -1), gkv=gkv.reshape(1, -1), g8=_bias_line(rel_bias), fb=fb,
                final_g=final_g.reshape(1, -1))


def _small_payload(per_layer, final_g, loss):
    def stack(key):
        return jnp.stack([p[key] for p in per_layer])

    def rows(a, rng):
        return _rows(a, rng[1] - rng[0])

    parts = [rows(stack("dmod"), PAY_DMOD), rows(stack("norm_g"), PAY_NORM), rows(stack("gq"), PAY_GQ),
             rows(stack("gkv"), PAY_GKV), rows(stack("rb8"), PAY_RB), rows(stack("fb"), PAY_FB),
             rows(final_g, PAY_FINAL), rows(loss, PAY_LOSS)]
    return jnp.concatenate(parts, axis=0)


def _payload_split(pay):
    def take(rng, shape):
        n = 1
        for d in shape:
            n *= d
        return pay[rng[0]:rng[1]].reshape(-1)[0:n].reshape(shape)

    norm_g = take(PAY_NORM, (DEPTH, D_MODEL))
    gq = take(PAY_GQ, (DEPTH, A_Q_RANK))
    gkv = take(PAY_GKV, (DEPTH, A_KV_RANK))
    rb = take(PAY_RB, (DEPTH, 8, 384))[:, 0:5, 0:N_REL]
    fb = take(PAY_FB, (DEPTH, LANE))[:, 0:5]
    final_g = take(PAY_FINAL, (D_MODEL,))
    return norm_g, gq, gkv, rb, fb, final_g


def kernel(x, c, positions, w_ada, b_ada, norm_g, w_in, a_q_norm_g, a_w_uq, a_kv_norm_g, a_w_ukv, b_rel_bias, c_forget_b, w_out, final_g, loss_target, m_w_ada, m_b_ada, m_norm_g, m_w_in, m_a_q_norm_g, m_a_w_uq, m_a_kv_norm_g, m_a_w_ukv, m_b_rel_bias, m_c_forget_b, m_w_out, m_final_g, v_w_ada, v_b_ada, v_norm_g, v_w_in, v_a_q_norm_g, v_a_w_uq, v_a_kv_norm_g, v_a_w_ukv, v_b_rel_bias, v_c_forget_b, v_w_out, v_final_g):
    nb, seq, _ = x.shape
    ix, iy, ic = lax.axis_index("x"), lax.axis_index("y"), lax.axis_index("c")
    chip = 2 * ix + iy

    weight_plan = _gather_plan((256, 256, LR_ROWS))
    cols = w_ada.shape[2]

    def gather_plan(x, y, c_):
        me_ = 4 * x + 2 * y + c_
        first = [(3, False, 0, 8, 8 * me_, f, ()) for f in range(N_DEV)]
        rest = weight_plan(x, y, c_)
        order = [i for i, s in enumerate(rest) if not s[6]] + [i for i, s in enumerate(rest) if s[6]]
        place = {old: len(first) + new for new, old in enumerate(order)}
        return first + [rest[i][:6] + (tuple(place[u] for u in rest[i][6]),) for i in order]

    probe = gather_plan(0, 0, 0)

    def project(x, y, c_, dst_refs, in_refs, out_refs, scratch):
        w_ref, b_ref = in_refs
        (mod_ref,) = out_refs
        c_vm, mod_vm, send_m, recv_m, local_m = scratch
        me_ = 4 * x + 2 * y + c_
        load = pltpu.make_async_copy(dst_refs[3].at[0], c_vm, local_m.at[0])
        load.start()
        load.wait()
        act = _silu(c_vm[...]).astype(BF16)
        for l in range(DEPTH):
            mod_vm[l] = _dot(act, w_ref[l].astype(BF16)) + b_ref[l]

        def rows_of(j):
            return mod_vm.at[:, pl.ds(pl.multiple_of(8 * j, 8), 8), :]

        def send(i, f, slot, j):
            to = (1 - x if f & 4 else x, 1 - y if f & 2 else y, 1 - c_ if f & 1 else c_)
            return pltpu.make_async_remote_copy(src_ref=rows_of(j), dst_ref=mod_ref.at[slot], send_sem=send_m.at[i],
                                                recv_sem=recv_m.at[i], device_id=to, device_id_type=pl.DeviceIdType.MESH)

        own = pltpu.make_async_copy(rows_of(me_), mod_ref.at[me_], local_m.at[1])
        own.start()
        sends = [send(i, f, me_, me_ ^ f) for i, f in enumerate(ALL_FLIPS)]
        for s in sends:
            s.start()

        def finish():
            for i, f in enumerate(ALL_FLIPS):
                send(i, f, me_ ^ f, 0).wait_recv()
            for s in sends:
                s.wait_send()
            own.wait()
        return finish

    b_cols = lax.dynamic_slice_in_dim(b_ada, chip * cols, cols, axis=1)[:, None, :]
    vmem = pl.BlockSpec(memory_space=pltpu.VMEM)
    side = dict(at=next(t for t, s in enumerate(probe) if s[6]), needs=tuple(range(N_DEV)),
                inputs=[(w_ada, vmem), (b_cols, vmem)], outputs=[((N_DEV, DEPTH, 8, cols), F32)],
                scratch=[pltpu.VMEM((8 * N_DEV, D_MODEL), F32), pltpu.VMEM((DEPTH, 8 * N_DEV, cols), F32),
                         pltpu.SemaphoreType.DMA((len(ALL_FLIPS),)), pltpu.SemaphoreType.DMA((len(ALL_FLIPS),)),
                         pltpu.SemaphoreType.DMA((2,))],
                run=project)
    full_in, full_out, full_lr, c_rows, mod_rows = _transfer(
        "gather_weights", [w_in.astype(BF16), w_out.astype(BF16), _pack_lowrank(a_w_uq, a_w_ukv).astype(BF16),
                           jnp.pad(c, ((0, 8 - nb), (0, 0)))[None]],
        [((DEPTH, D_MODEL, N_IN), BF16), ((DEPTH, D_MODEL, D_MODEL), BF16), ((1, 4 * LR_ROWS, PACK_COLS), BF16),
         ((1, 8 * N_DEV, D_MODEL), F32)], gather_plan, side)
    lowrank = [_unpack_lowrank(full_lr[0, LR_ROWS * j:LR_ROWS * (j + 1)]) for j in range(4)]
    full_uq = jnp.concatenate([s[0] for s in lowrank], axis=2)
    full_ukv = jnp.concatenate([s[1] for s in lowrank], axis=2)
    weights = [_layer_weights(full_in[l], full_out[l], full_uq[l], full_ukv[l]) for l in range(DEPTH)]
    small = [_layer_small(norm_g[l], a_q_norm_g[l], a_kv_norm_g[l], b_rel_bias[l], c_forget_b[l], final_g)
             for l in range(DEPTH)]

    c_all = c_rows.reshape(N_DEV, 8, D_MODEL)[:, 0:nb].reshape(N_DEV * nb, D_MODEL)
    mod = jnp.concatenate([mod_rows[2 * j][:, 0:nb] for j in range(4)], axis=2)

    tables = _rope_tables(positions)
    loss_part, dx, grads, gfinal, gw_in = _forward_backward(
        x.reshape(nb * seq, D_MODEL), mod, tables, loss_target.reshape(nb * seq, D_MODEL), weights, small, nb, seq)

    g_uq = jnp.stack([_uq_from_padded(g["w_uq"]) for g in grads])
    g_ukv = jnp.stack([_ukv_from_padded(g["w_ukv"]) for g in grads])
    g_lr = jnp.concatenate([_pack_lowrank(g_uq[:, :, 144 * j:144 * (j + 1)], g_ukv[:, :, 192 * j:192 * (j + 1)])
                            for j in range(4)], axis=1)
    partials = [gw_in[0], gw_in[1], jnp.stack([_out_from_padded(g["w_out"]) for g in grads]), g_lr]
    shapes = [(p.shape[0], p.shape[2]) for p in partials]
    halves = [r // 2 for r in SHARD_ROWS]
    core_s = jnp.reshape(ic, (1,)).astype(jnp.int32)
    pay = _small_payload(grads, gfinal, loss_part)
    pay_rows = pay.shape[0]
    reduce_plan = _pair_reduce_plan(SHARD_ROWS)

    def reduce_plan_with_payload(x, y, c_):
        me_ = 4 * x + 2 * y + c_
        return reduce_plan(x, y, c_) + [(len(partials), False, 0, pay_rows, pay_rows * me_, f, ()) for f in range(N_DEV)]

    *from_pair, pay_all = _transfer(
        "pair_reduce", partials + [pay[None]],
        [((nl, 4 * h, nc), F32) for (nl, nc), h in zip(shapes, halves)] + [((1, N_DEV * pay_rows, LANE), F32)],
        reduce_plan_with_payload)
    pay_all = pay_all.reshape(N_DEV, pay_rows, LANE)

    tot = _sum_blocks(pay_all, "sum_small")
    loss = tot[PAY_LOSS[0], 0]
    dmod_all = pay_all[:, PAY_DMOD[0]:PAY_DMOD[1]].reshape(N_DEV, -1)[:, 0:DEPTH * nb * 3 * D_MODEL]
    dmod_all = dmod_all.reshape(N_DEV, DEPTH, nb, 3 * D_MODEL).transpose(1, 0, 2, 3)
    dmod_all = dmod_all.reshape(DEPTH, N_DEV * nb, 3 * D_MODEL)
    my_cols = lax.dynamic_slice_in_dim(dmod_all, chip * cols, cols, axis=2)
    g_w_ada, g_b_ada = _ada_bwd(c_all, my_cols, dmod_all)
    g_b_ada = g_b_ada[:, 0]
    chip_sums = [_sum_pair(p, r, core_s, rc, "sum_pair%d" % i)
                 for i, (p, r, rc) in enumerate(zip(partials, from_pair, SHARD_ROWS))]
    slots = _transfer("chip_scatter", chip_sums, [((nl, 8 * h, nc), BF16) for (nl, nc), h in zip(shapes, halves)],
                      _chip_scatter_plan(SHARD_ROWS))
    g_in_a, g_in_b, g_out_sh, g_lr_sh = [_sum_chips(s, core_s, rc, "sum_chips%d" % i)
                                         for i, (s, rc) in enumerate(zip(slots, SHARD_ROWS))]
    g_uq_sh, g_ukv_sh = _unpack_lowrank(g_lr_sh[0])

    def cols_first(a):
        return jnp.transpose(a, (2, 0, 1))

    g_in_t = _in_cols_first(g_in_a, g_in_b)
    upd_in = tuple(jnp.transpose(a, (1, 2, 0)) for a in _adamw(cols_first(w_in), g_in_t, cols_first(m_w_in),
                                                               cols_first(v_w_in), "adamw_in"))
    gw = (jnp.transpose(g_in_t, (1, 2, 0)), g_out_sh, g_uq_sh, g_ukv_sh)
    upd = [upd_in, _adamw(w_out, gw[1], m_w_out, v_w_out, "adamw_out"),
           _adamw(a_w_uq, gw[2], m_a_w_uq, v_a_w_uq, "adamw_uq"), _adamw(a_w_ukv, gw[3], m_a_w_ukv, v_a_w_ukv, "adamw_ukv")]
    dw, mw, vw = (tuple(u[i] for u in upd) for i in range(3))
    d_ada, m_ada, v_ada = _adamw(w_ada, g_w_ada, m_w_ada, v_w_ada, "adamw_ada")

    def adam_small(w, g, m, v, name):
        shape3 = (1,) * (3 - w.ndim) + w.shape
        return tuple(a.reshape(w.shape) for a in _adamw(w.reshape(shape3), g.reshape(shape3), m.reshape(shape3),
                                                        v.reshape(shape3), name))

    d_b, m_b, v_b = adam_small(b_ada, g_b_ada, m_b_ada, v_b_ada, "adamw_b_ada")
    gs = _payload_split(tot)
    small_upd = [adam_small(w, g, m, v, "adamw_small%d" % i) for i, (w, g, m, v) in enumerate(zip(
        (norm_g, a_q_norm_g, a_kv_norm_g, b_rel_bias, c_forget_b, final_g), gs,
        (m_norm_g, m_a_q_norm_g, m_a_kv_norm_g, m_b_rel_bias, m_c_forget_b, m_final_g),
        (v_norm_g, v_a_q_norm_g, v_a_kv_norm_g, v_b_rel_bias, v_c_forget_b, v_final_g)))]
    ds, ms, vs = (tuple(u[i] for u in small_upd) for i in range(3))

    def ordered(ada, b, sm, big):
        ng, gq, gkv, rb, fb, fg = sm
        b_in, b_out, b_uq, b_ukv = big
        return (ada, b, ng, b_in, gq, b_uq, gkv, b_ukv, rb, fb, b_out, fg)

    return (loss, dx.reshape(nb, seq, D_MODEL), *ordered(g_w_ada, g_b_ada, gs, gw), *ordered(d_ada, d_b, ds, dw),
            *ordered(m_ada, m_b, ms, mw), *ordered(v_ada, v_b, vs, vw))
```

```python
import functools

import jax
import jax.numpy as jnp
from jax import lax
from jax.experimental import pallas as pl
from jax.experimental.pallas import tpu as pltpu

F32 = jnp.float32
BF16 = jnp.bfloat16

D_MODEL = 1024
DEPTH = 2
EPS = 1e-6
NEG = -1e30
LOG2E = 1.4426950408889634
ROPE_THETA = 10000.0
A_ROPE = 32
A_Q_RANK = 384
A_KV_RANK = 256
REL_CLIP = 128
N_REL = 2 * REL_CLIP + 1
N_IN = 3621

ADAM_LR = 0.001
ADAM_B1 = 0.9
ADAM_B2 = 0.999
ADAM_EPS = 1e-08
ADAM_WD = 0.01
ADAM_STEP = 10

LANE = 128
VMEM_LIMIT = 56 * 1024 * 1024

NP_IN = 4352
Z_A = (0, 768)
Z_G = (768, 1920)
Z_QKV = tuple((1920 + 384 * i, 1920 + 384 * (i + 1)) for i in range(6))
Z_F = (4224, 4352)
IN_PIECES = ((0, 672, 96), (672, 1056, 0), (2016, 2336, 64), (3301, 3621, 64), (1056, 1376, 64), (1376, 1696, 64),
             (1696, 2016, 64), (2336, 2656, 64), (2656, 2976, 64), (2976, 3296, 64), (3296, 3301, 123))
D_CAT = 1152

TM = 512
T_CAUSAL = 256
T_BAND = 128
BAND_TILES = 5
N_DEV = 8

PAY_DMOD = (0, 96)
PAY_NORM = (96, 112)
PAY_GQ = (112, 120)
PAY_GKV = (120, 128)
PAY_RB = (128, 176)
PAY_FB = (176, 184)
PAY_FINAL = (184, 192)
PAY_LOSS = (192, 200)

PACK_COLS = 1024


def _params(sem=None):
    return pltpu.CompilerParams(dimension_semantics=sem, vmem_limit_bytes=VMEM_LIMIT)


def _lane_iota(shape):
    return lax.broadcasted_iota(jnp.int32, shape, len(shape) - 1)


def _silu(u):
    return u * jax.nn.sigmoid(u)


def _dsilu(u):
    s = jax.nn.sigmoid(u)
    return s * (1.0 + u * (1.0 - s))


def _rms(x, g):
    r = lax.rsqrt(jnp.mean(x * x, axis=-1, keepdims=True) + EPS)
    xh = x * r
    return xh * g, xh, r


def _rms_bwd(dy, xh, r, g):
    dxh = dy * g
    return r * (dxh - xh * jnp.mean(dxh * xh, axis=-1, keepdims=True))


def _rope(x, cos, sina, sinb):
    return x * cos + pltpu.roll(x, 16, 1) * sinb + pltpu.roll(x, LANE - 16, 1) * sina


def _rope_t(dy, cos, sina, sinb):
    return dy * cos + pltpu.roll(dy * sinb, LANE - 16, 1) + pltpu.roll(dy * sina, 16, 1)


def _split3(x):
    hi = x.astype(BF16)
    r1 = x - hi.astype(F32)
    mid = r1.astype(BF16)
    lo = (r1 - mid.astype(F32)).astype(BF16)
    return hi, mid, lo


def _dot(a, b):
    return jnp.dot(a, b, preferred_element_type=F32)


def _dot_nt(a, b):
    return lax.dot_general(a, b, (((1,), (1,)), ((), ())), preferred_element_type=F32)


def _dot_tn(a, b):
    return lax.dot_general(a, b, (((0,), (0,)), ((), ())), preferred_element_type=F32)


def _row_spec(cols):
    return pl.BlockSpec((TM, cols), lambda i: (i, 0))


def _full_spec(shape):
    return pl.BlockSpec(shape, lambda i: (0,) * len(shape))


def _ex_spec(tiles_per_ex):
    return pl.BlockSpec((1, 1, D_MODEL), lambda i: (i // tiles_per_ex, 0, 0))


def _ln_inproj(x, shift, scale, g, w_in_p, seq):
    t = x.shape[0]

    def body(x_ref, sh_ref, sc_ref, g_ref, w_ref, h_ref, za_ref, zg_ref, q0, q1, q2, q3, q4, q5, zf_ref):
        n, _, _ = _rms(x_ref[...], g_ref[...])
        h = (n * (1.0 + sc_ref[0]) + sh_ref[0]).astype(BF16)
        h_ref[...] = h
        za_ref[...] = _dot(h, w_ref[:, Z_A[0]:Z_A[1]])
        zg_ref[...] = _dot(h, w_ref[:, Z_G[0]:Z_G[1]])
        for ref, (c0, c1) in zip((q0, q1, q2, q3, q4, q5), Z_QKV):
            ref[...] = _dot(h, w_ref[:, c0:c1]).astype(BF16)
        zf_ref[...] = _dot(h, w_ref[:, Z_F[0]:Z_F[1]])

    tpe = seq // TM
    shapes = [jax.ShapeDtypeStruct((t, D_MODEL), BF16), jax.ShapeDtypeStruct((t, 768), F32),
              jax.ShapeDtypeStruct((t, D_CAT), F32)]
    shapes += [jax.ShapeDtypeStruct((t, 384), BF16)] * 6 + [jax.ShapeDtypeStruct((t, LANE), F32)]
    return pl.pallas_call(
        body, name="ln_inproj", grid=(t // TM,),
        in_specs=[_row_spec(D_MODEL), _ex_spec(tpe), _ex_spec(tpe), _full_spec((1, D_MODEL)),
                  _full_spec((D_MODEL, NP_IN))],
        out_specs=[_row_spec(D_MODEL), _row_spec(768), _row_spec(D_CAT)] + [_row_spec(384)] * 6 + [_row_spec(LANE)],
        out_shape=shapes, compiler_params=_params(("parallel",)),
    )(x, shift, scale, g, w_in_p)


def _a_up(za, gq, gkv, w_uq_p, w_ukv_p, cos, sina, sinb):
    t = za.shape[0]

    def body(za_ref, gq_ref, gkv_ref, wq_ref, wkv_ref, cos_ref, sa_ref, sb_ref, q_ref, k_ref, v_ref):
        cos_t, sa, sb = cos_ref[...], sa_ref[...], sb_ref[...]
        cqn, _, _ = _rms(za_ref[:, 0:384], gq_ref[...])
        q = _dot(cqn.astype(BF16), wq_ref[...])
        ckvn, _, _ = _rms(za_ref[:, 384:640], gkv_ref[...])
        kv = _dot(ckvn.astype(BF16), wkv_ref[...])
        kpe = za_ref[:, 640:768]
        kpe = _rope(kpe + pltpu.roll(kpe, 32, 1), cos_t, sa, sb).astype(BF16)
        for p in range(3):
            q_ref[:, 256 * p:256 * p + 128] = q[:, 256 * p:256 * p + 128].astype(BF16)
            q_ref[:, 256 * p + 128:256 * p + 256] = _rope(q[:, 256 * p + 128:256 * p + 256], cos_t, sa, sb).astype(BF16)
            k_ref[:, 256 * p:256 * p + 128] = kv[:, 128 * p:128 * p + 128].astype(BF16)
            k_ref[:, 256 * p + 128:256 * p + 256] = kpe
        v_ref[...] = kv[:, 384:768].astype(BF16)

    return pl.pallas_call(
        body, name="a_up", grid=(t // TM,),
        in_specs=[_row_spec(768), _full_spec((1, 384)), _full_spec((1, 256)), _full_spec((384, 768)),
                  _full_spec((256, 768)), _row_spec(LANE), _row_spec(LANE), _row_spec(LANE)],
        out_specs=[_row_spec(768), _row_spec(768), _row_spec(384)],
        out_shape=[jax.ShapeDtypeStruct((t, 768), BF16), jax.ShapeDtypeStruct((t, 768), BF16),
                   jax.ShapeDtypeStruct((t, 384), BF16)],
        compiler_params=_params(("parallel",)),
    )(za, gq, gkv, w_uq_p, w_ukv_p, cos, sina, sinb)


def _tri(n, upper):
    r = lax.broadcasted_iota(jnp.int32, (n, n), 0)
    c = lax.broadcasted_iota(jnp.int32, (n, n), 1)
    return jnp.where((c >= r) if upper else (c <= r), 1.0, 0.0).astype(BF16)


def _forget_fwd(zf, fb, nb, seq):
    blk = 256

    def body(zf_ref, fb_ref, f_ref):
        tri = _tri(blk, False)
        live = _lane_iota((blk, LANE)) < 5
        carry = jnp.zeros((1, LANE), F32)
        for i in range(seq // blk):
            u = zf_ref[i * blk:(i + 1) * blk, :] + fb_ref[...]
            lf = jnp.where(live, jnp.minimum(u, 0.0) - jnp.log(1.0 + jnp.exp(-jnp.abs(u))), 0.0)
            hi, mid, lo = _split3(lf)
            f_ref[i * blk:(i + 1) * blk, :] = (_dot(tri, hi) + _dot(tri, mid) + _dot(tri, lo) + carry) * LOG2E
            carry = carry + jnp.sum(lf, axis=0, keepdims=True)

    return pl.pallas_call(
        body, name="forget_fwd", grid=(nb,),
        in_specs=[pl.BlockSpec((seq, LANE), lambda b: (b, 0)), pl.BlockSpec((1, LANE), lambda b: (0, 0))],
        out_specs=pl.BlockSpec((seq, LANE), lambda b: (b, 0)),
        out_shape=jax.ShapeDtypeStruct((nb * seq, LANE), F32), compiler_params=_params(("parallel",)),
    )(zf, fb)


def _forget_bwd(dfq, dfk, zf, fb, nb, seq):
    blk = 256

    def body(dfq_ref, dfk_ref, zf_ref, fb_ref, dz_ref, gb_ref):
        @pl.when(pl.program_id(0) == 0)
        def _():
            gb_ref[...] = jnp.zeros_like(gb_ref)

        tri = _tri(blk, True)
        lane = _lane_iota((blk, LANE))
        wide = _lane_iota((blk, 384))
        live = lane < 5
        carry = jnp.zeros((1, LANE), F32)
        gsum = jnp.zeros((1, LANE), F32)
        for i in reversed(range(seq // blk)):
            d = dfk_ref[i * blk:(i + 1) * blk, :]
            dq = dfq_ref[i * blk:(i + 1) * blk, :]
            for hd in range(5):
                col = jnp.sum(jnp.where(wide == 64 * hd, dq, 0.0), axis=-1, keepdims=True)
                d = d + jnp.where(lane == hd, col, 0.0)
            hi, mid, lo = _split3(d)
            dlf = _dot(tri, hi) + _dot(tri, mid) + _dot(tri, lo) + carry
            carry = carry + jnp.sum(d, axis=0, keepdims=True)
            u = zf_ref[i * blk:(i + 1) * blk, :] + fb_ref[...]
            du = jnp.where(live, dlf * jax.nn.sigmoid(-u), 0.0)
            dz_ref[i * blk:(i + 1) * blk, :] = du.astype(BF16)
            gsum = gsum + jnp.sum(du, axis=0, keepdims=True)
        gb_ref[...] += jnp.broadcast_to(gsum, gb_ref.shape)

    return pl.pallas_call(
        body, name="forget_bwd", grid=(nb,),
        in_specs=[pl.BlockSpec((seq, 384), lambda b: (b, 0)), pl.BlockSpec((seq, LANE), lambda b: (b, 0)),
                  pl.BlockSpec((seq, LANE), lambda b: (b, 0)), pl.BlockSpec((1, LANE), lambda b: (0, 0))],
        out_specs=[pl.BlockSpec((seq, LANE), lambda b: (b, 0)), pl.BlockSpec((8, LANE), lambda b: (0, 0))],
        out_shape=[jax.ShapeDtypeStruct((nb * seq, LANE), BF16), jax.ShapeDtypeStruct((8, LANE), F32)],
        compiler_params=_params(("arbitrary",)),
    )(dfq, dfk, zf, fb)


def _gate_outproj(x, gate, oa, ob, oc, zg, w_out_p, seq):
    t = x.shape[0]

    def body(x_ref, gate_ref, oa_ref, ob_ref, oc_ref, zg_ref, w_ref, y_ref, xn_ref):
        y = jnp.zeros((TM, D_MODEL), F32)
        for i, o_ref in enumerate((oa_ref, ob_ref, oc_ref)):
            cat = (o_ref[...] * _silu(zg_ref[:, 384 * i:384 * (i + 1)])).astype(BF16)
            y = y + _dot(cat, w_ref[384 * i:384 * (i + 1), :])
        y_ref[...] = y
        xn_ref[...] = x_ref[...] + gate_ref[0] * y

    return pl.pallas_call(
        body, name="gate_outproj", grid=(t // TM,),
        in_specs=[_row_spec(D_MODEL), _ex_spec(seq // TM), _row_spec(384), _row_spec(384), _row_spec(384),
                  _row_spec(D_CAT), _full_spec((D_CAT, D_MODEL))],
        out_specs=[_row_spec(D_MODEL), _row_spec(D_MODEL)],
        out_shape=[jax.ShapeDtypeStruct((t, D_MODEL), F32)] * 2, compiler_params=_params(("parallel",)),
    )(x, gate, oa, ob, oc, zg, w_out_p)


TH = 256


def _head(x, gate, oa, ob, oc, zg, w_out_p, w_out_pt, target, g, nb, seq):
    t = x.shape[0]
    tpe = seq // TH

    def rows(cols):
        return pl.BlockSpec((TH, cols), lambda i: (i, 0))

    ex = pl.BlockSpec((1, 1, D_MODEL), lambda i: (i // tpe, 0, 0))

    def body(x_ref, gate_ref, oa_ref, ob_ref, oc_ref, zg_ref, w_ref, wt_ref, t_ref, g_ref,
             dx_ref, doa_ref, dob_ref, doc_ref, dzg_ref, gw_ref, dgate_ref, loss_ref, gg_ref):
        i = pl.program_id(0)

        @pl.when(i == 0)
        def _():
            gw_ref[...] = jnp.zeros_like(gw_ref)
            loss_ref[...] = jnp.zeros_like(loss_ref)
            gg_ref[...] = jnp.zeros_like(gg_ref)

        @pl.when(i % tpe == 0)
        def _():
            dgate_ref[...] = jnp.zeros_like(dgate_ref)

        groups = []
        y = jnp.zeros((TH, D_MODEL), F32)
        for gi, o_ref in enumerate((oa_ref, ob_ref, oc_ref)):
            cols = slice(384 * gi, 384 * (gi + 1))
            u = zg_ref[:, cols]
            o_t = o_ref[...]
            su = _silu(u)
            cat = (o_t * su).astype(BF16)
            y = y + _dot(cat, w_ref[cols, :])
            groups.append((cols, u, o_t, su, cat))
        gate_t = gate_ref[0]
        gv = g_ref[...]
        out, xh, r = _rms(x_ref[...] + gate_t * y, gv)
        err = out - t_ref[...]
        loss_ref[...] += 0.5 * jnp.sum(jnp.mean(err * err, axis=-1, keepdims=True), axis=0, keepdims=True)
        dout = err / D_MODEL
        gg_ref[...] += jnp.broadcast_to(jnp.sum(dout * xh, axis=0, keepdims=True), gg_ref.shape)
        dxn = _rms_bwd(dout, xh, r, gv)
        dx_ref[...] = dxn
        dgate_ref[0] += jnp.sum(dxn * y, axis=0, keepdims=True)
        dy = (dxn * gate_t).astype(BF16)
        for (cols, u, o_t, su, cat), do_ref in zip(groups, (doa_ref, dob_ref, doc_ref)):
            dcat = _dot(dy, wt_ref[:, cols])
            do_ref[...] = dcat * su
            dzg_ref[:, cols] = (dcat * o_t * _dsilu(u)).astype(BF16)
            gw_ref[cols, :] += _dot_tn(cat, dy)

    return pl.pallas_call(
        body, name="head", grid=(t // TH,),
        in_specs=[rows(D_MODEL), ex, rows(384), rows(384), rows(384), rows(D_CAT), _full_spec((D_CAT, D_MODEL)),
                  _full_spec((D_MODEL, D_CAT)), rows(D_MODEL), _full_spec((1, D_MODEL))],
        out_specs=[rows(D_MODEL), rows(384), rows(384), rows(384), rows(D_CAT), _full_spec((D_CAT, D_MODEL)), ex,
                   _full_spec((8, LANE)), _full_spec((8, D_MODEL))],
        out_shape=[jax.ShapeDtypeStruct((t, D_MODEL), F32)] + [jax.ShapeDtypeStruct((t, 384), F32)] * 3
        + [jax.ShapeDtypeStruct((t, D_CAT), BF16), jax.ShapeDtypeStruct((D_CAT, D_MODEL), F32),
           jax.ShapeDtypeStruct((nb, 1, D_MODEL), F32), jax.ShapeDtypeStruct((8, LANE), F32),
           jax.ShapeDtypeStruct((8, D_MODEL), F32)],
        compiler_params=_params(("arbitrary",)),
    )(x, gate, oa, ob, oc, zg, w_out_p, w_out_pt, target, g)


def _head_masks(kind, rows, dq, h):
    lq = _lane_iota((rows, dq))
    lv = _lane_iota((rows, LANE))
    mq = (lq >= 64 * h) & (lq < 64 * h + 64)
    if kind == "A":
        mq = mq | ((lq >= 128 + 32 * h) & (lq < 160 + 32 * h))
    return mq, (lv >= 64 * h) & (lv < 64 * h + 64)


def _tile_mask(kind, tile):
    row = lax.broadcasted_iota(jnp.int32, (tile, tile), 0)
    col = lax.broadcasted_iota(jnp.int32, (tile, tile), 1)
    return (col >> 6) <= (row >> 6) if kind == "A" else col <= row


def _attn_scale(kind):
    return 96.0 ** -0.5 if kind == "A" else 0.125


BAND_W = BAND_TILES * T_BAND


def _segments(kind, qi, tile):
    r0 = qi * tile
    if kind == "B":
        lo = max(qi - (BAND_TILES - 1), 0) * tile
        return [(lo, r0 + tile, False, lo - (qi - (BAND_TILES - 1)) * tile)]
    return ([(0, r0, False, 0)] if qi else []) + [(r0, r0 + tile, True, 0)]


def _scores(kind, qh, k_ref, aux_ref, h, seg, tile, scale):
    a, b, diag, c0 = seg
    s = _dot_nt(qh, k_ref[a:b, :]) * (scale * LOG2E)
    if kind == "B":
        return s + aux_ref[h, :, c0:BAND_W]
    if kind == "C":
        s = s - aux_ref[0, h, :, a:b]
    if diag:
        s = jnp.where(_tile_mask(kind, tile), s, NEG)
    return s


FWD_AHEAD = 2
BWD_AHEAD = 2


def _run_ahead(units, first, second, depth):
    queue = [first(*u) for u in units[:depth]]
    for i, (_, h) in enumerate(units):
        if i + depth < len(units):
            queue.append(first(*units[i + depth]))
        second(h, *queue.pop(0))


def _attn_fwd(kind, q, k, v, aux, nb, seq):
    dq = q.shape[1] // 3
    tile = T_BAND if kind == "B" else T_CAUSAL
    nq = seq // tile
    scale = _attn_scale(kind)

    def body(*refs):
        if kind == "A":
            q_ref, k_ref, v_ref, o_ref, lse_ref = refs
            aux_ref = None
        else:
            q_ref, k_ref, v_ref, aux_ref, o_ref, lse_ref = refs
        def logits(qi, h):
            rows = slice(qi * tile, (qi + 1) * tile)
            q2 = q_ref[rows, :]
            mq, _ = _head_masks(kind, tile, dq, h)
            qh = jnp.where(mq, q2, jnp.zeros_like(q2))
            segs = _segments(kind, qi, tile)
            return rows, segs, [_scores(kind, qh, k_ref, aux_ref, h, seg, tile, scale) for seg in segs]

        def finish(h, rows, segs, ss):
            _, mv = _head_masks(kind, tile, dq, h)
            mx = functools.reduce(jnp.maximum, [jnp.max(s, axis=-1, keepdims=True) for s in ss])
            ps = [jnp.exp2(s - mx) for s in ss]
            l = functools.reduce(jnp.add, [jnp.sum(p, axis=-1, keepdims=True) for p in ps])
            acc = functools.reduce(jnp.add, [_dot(p.astype(BF16), v_ref[seg[0]:seg[1], :]) for p, seg in zip(ps, segs)])
            o_h = jnp.where(mv, acc / l, 0.0)
            lse_h = jnp.where(mv, mx + jnp.log(l) * LOG2E, 0.0)
            if h == 0:
                o_ref[rows, :] = o_h
                lse_ref[rows, :] = lse_h
            else:
                o_ref[rows, :] += o_h
                lse_ref[rows, :] += lse_h

        _run_ahead([(qi, h) for qi in range(nq) for h in range(2)], logits, finish, FWD_AHEAD)

    def seq_spec(cols):
        return pl.BlockSpec((seq, cols), lambda b, p: (b, p))

    in_specs = [seq_spec(dq), seq_spec(dq), seq_spec(LANE)]
    args = [q, k, v]
    if kind == "B":
        in_specs.append(pl.BlockSpec((2, tile, BAND_W), lambda b, p: (p, 0, 0)))
        args.append(aux)
    if kind == "C":
        in_specs.append(pl.BlockSpec((1, 2, 1, seq), lambda b, p: (b, p, 0, 0)))
        args.append(aux)
    return pl.pallas_call(
        body, name="attn_fwd_" + kind, grid=(nb, 3), in_specs=in_specs, out_specs=[seq_spec(LANE), seq_spec(LANE)],
        out_shape=[jax.ShapeDtypeStruct((nb * seq, 384), F32)] * 2, compiler_params=_params(("parallel", "parallel")),
    )(*args)


def _attn_bwd(kind, q, k, v, o, do, lse, aux, nb, seq):
    dq = q.shape[1] // 3
    tile = T_BAND if kind == "B" else T_CAUSAL
    nq = seq // tile
    scale = _attn_scale(kind)
    dqk_dtype = F32 if kind == "A" else BF16

    def body(*refs):
        dfr_ref = dfq_ref = dbt_ref = aux_ref = None
        if kind == "A":
            q_ref, k_ref, v_ref, o_ref, do_ref, lse_ref, dq_ref, dk_ref, dv_ref, dkt_acc, dvt_acc = refs
        elif kind == "B":
            q_ref, k_ref, v_ref, o_ref, do_ref, lse_ref, aux_ref, dq_ref, dk_ref, dv_ref, dbt_ref, dkt_acc, dvt_acc = refs
        else:
            (q_ref, k_ref, v_ref, o_ref, do_ref, lse_ref, aux_ref, dq_ref, dk_ref, dv_ref, dfr_ref, dfq_ref,
             dkt_acc, dvt_acc) = refs
        dkt_acc[...] = jnp.zeros_like(dkt_acc)
        dvt_acc[...] = jnp.zeros_like(dvt_acc)
        if kind == "C":
            dfr_ref[...] = jnp.zeros_like(dfr_ref)
        if kind == "B":
            @pl.when(pl.program_id(1) == 0)
            def _():
                dbt_ref[...] = jnp.zeros_like(dbt_ref)

        def products(qi, h):
            rows = slice(qi * tile, (qi + 1) * tile)
            q2 = q_ref[rows, :]
            mq, mv = _head_masks(kind, tile, dq, h)
            qh = jnp.where(mq, q2, jnp.zeros_like(q2))
            doh = jnp.where(mv, do_ref[rows, :], 0.0)
            dob = doh.astype(BF16)
            segs = _segments(kind, qi, tile)
            ts = [_scores(kind, qh, k_ref, aux_ref, h, seg, tile, scale) for seg in segs]
            dps = [_dot_nt(dob, v_ref[seg[0]:seg[1], :]) for seg in segs]
            return rows, segs, qh, doh, ts, dps

        def finish(h, rows, segs, qh, doh, ts, dps):
            mq, mv = _head_masks(kind, tile, dq, h)
            qht = qh.astype(F32).T.astype(BF16)
            dobt = doh.T.astype(BF16)
            head_rows = [(64 * h, 64)] + ([(128 + 32 * h, 32)] if kind == "A" else [])
            delta = jnp.sum(doh * o_ref[rows, :], axis=-1, keepdims=True)
            lseh = jnp.max(jnp.where(mv, lse_ref[rows, :], NEG), axis=-1, keepdims=True)
            rs = jnp.zeros((tile, 1), F32)
            dq_h = jnp.zeros((tile, dq), F32)
            for (a, b, _, c0), t, dp in zip(segs, ts, dps):
                p = jnp.exp2(t - lseh)
                ds = p * (dp - delta)
                if kind == "B":
                    dbt_ref[h, :, c0:BAND_W] += ds
                if kind == "C":
                    dfr_ref[0, h, :, a:b] -= jnp.sum(ds, axis=0, keepdims=True)
                    rs = rs + jnp.sum(ds, axis=-1, keepdims=True)
                dss = (ds * scale).astype(BF16)
                dvt_acc[64 * h:64 * h + 64, a:b] += _dot(dobt[64 * h:64 * h + 64, :], p.astype(BF16))
                for r0, n in head_rows:
                    dkt_acc[r0:r0 + n, a:b] += _dot(qht[r0:r0 + n, :], dss)
                dq_h = dq_h + _dot(dss, k_ref[a:b, :])
            dq_h = jnp.where(mq, dq_h, 0.0).astype(dqk_dtype)
            if h == 0:
                dq_ref[rows, :] = dq_h
            else:
                dq_ref[rows, :] += dq_h
            if kind == "C":
                if h == 0:
                    dfq_ref[rows, :] = jnp.where(mv, rs, 0.0)
                else:
                    dfq_ref[rows, :] += jnp.where(mv, rs, 0.0)

        _run_ahead([(qi, h) for qi in range(nq) for h in range(2)], products, finish, BWD_AHEAD)
        for j in range(seq // 256):
            cols = slice(256 * j, 256 * (j + 1))
            dk_ref[cols, :] = dkt_acc[:, cols].T.astype(dqk_dtype)
            dv_ref[cols, :] = dvt_acc[:, cols].T.astype(BF16)

    def seq_spec(cols):
        return pl.BlockSpec((seq, cols), lambda p, b: (b, p))

    in_specs = [seq_spec(dq), seq_spec(dq), seq_spec(LANE), seq_spec(LANE), seq_spec(LANE), seq_spec(LANE)]
    args = [q, k, v, o, do, lse]
    out_specs = [seq_spec(dq), seq_spec(dq), seq_spec(LANE)]
    out_shape = [jax.ShapeDtypeStruct((nb * seq, 3 * dq), dqk_dtype)] * 2 + [jax.ShapeDtypeStruct((nb * seq, 384), BF16)]
    if kind == "B":
        spec = pl.BlockSpec((2, tile, BAND_W), lambda p, b: (p, 0, 0))
        in_specs.append(spec)
        args.append(aux)
        out_specs.append(spec)
        out_shape.append(jax.ShapeDtypeStruct((6, tile, BAND_W), F32))
    if kind == "C":
        spec = pl.BlockSpec((1, 2, 1, seq), lambda p, b: (b, p, 0, 0))
        in_specs.append(spec)
        args.append(aux)
        out_specs += [spec, seq_spec(LANE)]
        out_shape += [jax.ShapeDtypeStruct((nb, 6, 1, seq), F32), jax.ShapeDtypeStruct((nb * seq, 384), F32)]
    return pl.pallas_call(
        body, name="attn_bwd_" + kind, grid=(3, nb), in_specs=in_specs, out_specs=out_specs, out_shape=out_shape,
        scratch_shapes=[pltpu.VMEM((dq, seq), F32), pltpu.VMEM((LANE, seq), F32)],
        compiler_params=_params(("arbitrary", "arbitrary")),
    )(*args)


BIAS_G = 768
BIAS_EDGE = BIAS_G - N_REL


def _bias_line(rel_bias):
    g = jnp.concatenate([jnp.broadcast_to(rel_bias[:, N_REL - 1:], (rel_bias.shape[0], BIAS_EDGE)),
                         jnp.flip(rel_bias, axis=1)], axis=1)
    return jnp.pad(g, ((0, 8 - g.shape[0]), (0, 0)))


def _bias_unline(dg):
    return jnp.flip(dg[:, BIAS_EDGE:], axis=1)


def _bias_expand(g8):
    def body(g_ref, out_ref):
        line = jnp.broadcast_to(g_ref[0] * LOG2E, (T_BAND, BIAS_G))
        slab = pltpu.roll(line, 1, 1, stride=1, stride_axis=0)[:, LANE:BIAS_G]
        row = lax.broadcasted_iota(jnp.int32, (T_BAND, BAND_W), 0)
        col = lax.broadcasted_iota(jnp.int32, (T_BAND, BAND_W), 1)
        hidden = ((row >= 64) & (col < 64)) | ((row < 64) & (col >= BAND_W - 64))
        out_ref[0] = jnp.where(hidden, NEG, slab)

    return pl.pallas_call(
        body, name="bias_expand", grid=(8,), in_specs=[pl.BlockSpec((1, 1, BIAS_G), lambda h: (h, 0, 0))],
        out_specs=pl.BlockSpec((1, T_BAND, BAND_W), lambda h: (h, 0, 0)),
        out_shape=jax.ShapeDtypeStruct((8, T_BAND, BAND_W), F32), compiler_params=_params(("parallel",)),
    )(g8.reshape(8, 1, BIAS_G))


def _bias_reduce(d_slab):
    def body(d_ref, out_ref):
        r = lax.broadcasted_iota(jnp.int32, (T_BAND, T_BAND), 0)
        k = lax.broadcasted_iota(jnp.int32, (T_BAND, T_BAND), 1)
        flip = jnp.where(r + k == T_BAND - 1, 1.0, 0.0).astype(BF16)
        hi, mid, lo = _split3(d_ref[0])
        d_rev = _dot(flip, hi) + _dot(flip, mid) + _dot(flip, lo)
        wide = jnp.concatenate([jnp.zeros((T_BAND, LANE), F32), d_rev, jnp.zeros((T_BAND, 2 * LANE), F32)], axis=1)
        skew = pltpu.roll(wide, 0, 1, stride=1, stride_axis=0)
        dg = jnp.sum(skew, axis=0, keepdims=True)[:, LANE:LANE + BIAS_G]
        lane = _lane_iota((1, BIAS_G))
        clipped = jnp.sum(jnp.where(lane <= BIAS_EDGE, dg, 0.0), axis=1, keepdims=True)
        out_ref[0] = jnp.where(lane == BIAS_EDGE, clipped, dg)

    return pl.pallas_call(
        body, name="bias_reduce", grid=(8,), in_specs=[pl.BlockSpec((1, T_BAND, BAND_W), lambda h: (h, 0, 0))],
        out_specs=pl.BlockSpec((1, 1, BIAS_G), lambda h: (h, 0, 0)),
        out_shape=jax.ShapeDtypeStruct((8, 1, BIAS_G), F32), compiler_params=_params(("parallel",)),
    )(d_slab).reshape(8, BIAS_G)


def _outproj_bwd(dxn, y, gate, oa, ob, oc, zg, w_out_p, w_out_pt, nb, seq):
    t = dxn.shape[0]
    tpe = seq // TM

    def body(dxn_ref, y_ref, gate_ref, oa_ref, ob_ref, oc_ref, zg_ref, w_ref, wt_ref,
             doa_ref, dob_ref, doc_ref, dzg_ref, gw_ref, dgate_ref):
        i = pl.program_id(0)

        @pl.when(i == 0)
        def _():
            gw_ref[...] = jnp.zeros_like(gw_ref)

        @pl.when(i % tpe == 0)
        def _():
            dgate_ref[...] = jnp.zeros_like(dgate_ref)

        dxn_t = dxn_ref[...]
        dgate_ref[0] += jnp.sum(dxn_t * y_ref[...], axis=0, keepdims=True)
        dy = (dxn_t * gate_ref[0]).astype(BF16)
        for gi, (o_ref, do_ref) in enumerate(((oa_ref, doa_ref), (ob_ref, dob_ref), (oc_ref, doc_ref))):
            cols = slice(384 * gi, 384 * (gi + 1))
            u = zg_ref[:, cols]
            o_t = o_ref[...]
            su = _silu(u)
            dcat = _dot(dy, wt_ref[:, cols])
            do_ref[...] = dcat * su
            dzg_ref[:, cols] = (dcat * o_t * _dsilu(u)).astype(BF16)
            gw_ref[cols, :] += _dot_tn((o_t * su).astype(BF16), dy)

    return pl.pallas_call(
        body, name="outproj_bwd", grid=(t // TM,),
        in_specs=[_row_spec(D_MODEL), _row_spec(D_MODEL), _ex_spec(tpe), _row_spec(384), _row_spec(384), _row_spec(384),
                  _row_spec(D_CAT), _full_spec((D_CAT, D_MODEL)), _full_spec((D_MODEL, D_CAT))],
        out_specs=[_row_spec(384), _row_spec(384), _row_spec(384), _row_spec(D_CAT), _full_spec((D_CAT, D_MODEL)),
                   _ex_spec(tpe)],
        out_shape=[jax.ShapeDtypeStruct((t, 384), F32)] * 3 + [jax.ShapeDtypeStruct((t, D_CAT), BF16),
                                                                jax.ShapeDtypeStruct((D_CAT, D_MODEL), F32),
                                                                jax.ShapeDtypeStruct((nb, 1, D_MODEL), F32)],
        compiler_params=_params(("arbitrary",)),
    )(dxn, y, gate, oa, ob, oc, zg, w_out_p, w_out_pt)


def _a_up_bwd(dqa, dka, dva, za, gq, gkv, w_uq_pt, w_ukv_pt, cos, sina, sinb):
    t = za.shape[0]

    def body(dq_ref, dk_ref, dv_ref, za_ref, gq_ref, gkv_ref, wqt_ref, wkvt_ref, cos_ref, sa_ref, sb_ref,
             dza_ref, gwq_ref, gwkv_ref, ggq_ref, ggkv_ref, dqb, dkvb):
        @pl.when(pl.program_id(0) == 0)
        def _():
            gwq_ref[...] = jnp.zeros_like(gwq_ref)
            gwkv_ref[...] = jnp.zeros_like(gwkv_ref)
            ggq_ref[...] = jnp.zeros_like(ggq_ref)
            ggkv_ref[...] = jnp.zeros_like(ggkv_ref)

        cos_t, sa, sb = cos_ref[...], sa_ref[...], sb_ref[...]
        dkpe = jnp.zeros((TM, LANE), F32)
        for p in range(3):
            dqb[:, 256 * p:256 * p + 128] = dq_ref[:, 256 * p:256 * p + 128].astype(BF16)
            dqb[:, 256 * p + 128:256 * p + 256] = _rope_t(dq_ref[:, 256 * p + 128:256 * p + 256], cos_t, sa, sb).astype(BF16)
            dkvb[:, 128 * p:128 * p + 128] = dk_ref[:, 256 * p:256 * p + 128].astype(BF16)
            dkpe = dkpe + dk_ref[:, 256 * p + 128:256 * p + 256]
        dkvb[:, 384:768] = dv_ref[...]
        dkpe = _rope_t(dkpe, cos_t, sa, sb)
        dkpe = jnp.where(_lane_iota((TM, LANE)) < A_ROPE, dkpe + pltpu.roll(dkpe, LANE - 32, 1), 0.0)

        gqv = gq_ref[...]
        cqn, cqh, rq = _rms(za_ref[:, 0:384], gqv)
        dq_t = dqb[...]
        gwq_ref[...] += _dot_tn(cqn.astype(BF16), dq_t)
        dcqn = _dot(dq_t, wqt_ref[...])
        ggq_ref[...] += jnp.broadcast_to(jnp.sum(dcqn * cqh, axis=0, keepdims=True), ggq_ref.shape)
        dza_ref[:, 0:384] = _rms_bwd(dcqn, cqh, rq, gqv).astype(BF16)

        gkvv = gkv_ref[...]
        ckvn, ckvh, rkv = _rms(za_ref[:, 384:640], gkvv)
        dkv_t = dkvb[...]
        gwkv_ref[...] += _dot_tn(ckvn.astype(BF16), dkv_t)
        dckvn = _dot(dkv_t, wkvt_ref[...])
        ggkv_ref[...] += jnp.broadcast_to(jnp.sum(dckvn * ckvh, axis=0, keepdims=True), ggkv_ref.shape)
        dza_ref[:, 384:640] = _rms_bwd(dckvn, ckvh, rkv, gkvv).astype(BF16)
        dza_ref[:, 640:768] = dkpe.astype(BF16)

    return pl.pallas_call(
        body, name="a_up_bwd", grid=(t // TM,),
        in_specs=[_row_spec(768), _row_spec(768), _row_spec(384), _row_spec(768), _full_spec((1, 384)),
                  _full_spec((1, 256)), _full_spec((768, 384)), _full_spec((768, 256)), _row_spec(LANE), _row_spec(LANE),
                  _row_spec(LANE)],
        out_specs=[_row_spec(768), _full_spec((384, 768)), _full_spec((256, 768)), _full_spec((8, 384)),
                   _full_spec((8, 256))],
        out_shape=[jax.ShapeDtypeStruct((t, 768), BF16), jax.ShapeDtypeStruct((384, 768), F32),
                   jax.ShapeDtypeStruct((256, 768), F32), jax.ShapeDtypeStruct((8, 384), F32),
                   jax.ShapeDtypeStruct((8, 256), F32)],
        scratch_shapes=[pltpu.VMEM((TM, 768), BF16), pltpu.VMEM((TM, 768), BF16)],
        compiler_params=_params(("arbitrary",)),
    )(dqa, dka, dva, za, gq, gkv, w_uq_pt, w_ukv_pt, cos, sina, sinb)


def _dz_cols():
    return (Z_A, Z_G) + Z_QKV + (Z_F,)


def _inproj_bwd_dx(dz, dxn, x, shift, scale, g, w_in_pt, nb, seq):
    t = x.shape[0]
    tpe = seq // TM
    cols = _dz_cols()

    def body(*refs):
        dz_refs = refs[:len(cols)]
        dxn_ref, x_ref, sh_ref, sc_ref, g_ref, wt_ref, dx_ref, dsh_ref, dsc_ref, dg_ref = refs[len(cols):]
        i = pl.program_id(0)

        @pl.when(i == 0)
        def _():
            dg_ref[...] = jnp.zeros_like(dg_ref)

        @pl.when(i % tpe == 0)
        def _():
            dsh_ref[...] = jnp.zeros_like(dsh_ref)
            dsc_ref[...] = jnp.zeros_like(dsc_ref)

        dh = jnp.zeros((TM, D_MODEL), F32)
        for ref, (c0, c1) in zip(dz_refs, cols):
            dh = dh + _dot_nt(ref[...], wt_ref[:, c0:c1])
        gv = g_ref[...]
        n, xh, r = _rms(x_ref[...], gv)
        dsh_ref[0] += jnp.sum(dh, axis=0, keepdims=True)
        dsc_ref[0] += jnp.sum(dh * n, axis=0, keepdims=True)
        dn = dh * (1.0 + sc_ref[0])
        dg_ref[...] += jnp.broadcast_to(jnp.sum(dn * xh, axis=0, keepdims=True), dg_ref.shape)
        dx_ref[...] = dxn_ref[...] + _rms_bwd(dn, xh, r, gv)

    in_specs = [_row_spec(c1 - c0) for c0, c1 in cols]
    in_specs += [_row_spec(D_MODEL), _row_spec(D_MODEL), _ex_spec(tpe), _ex_spec(tpe), _full_spec((1, D_MODEL)),
                 _full_spec((D_MODEL, NP_IN))]
    return pl.pallas_call(
        body, name="inproj_bwd_dx", grid=(t // TM,), in_specs=in_specs,
        out_specs=[_row_spec(D_MODEL), _ex_spec(tpe), _ex_spec(tpe), _full_spec((8, D_MODEL))],
        out_shape=[jax.ShapeDtypeStruct((t, D_MODEL), F32), jax.ShapeDtypeStruct((nb, 1, D_MODEL), F32),
                   jax.ShapeDtypeStruct((nb, 1, D_MODEL), F32), jax.ShapeDtypeStruct((8, D_MODEL), F32)],
        compiler_params=_params(("arbitrary",)),
    )(*dz, dxn, x, shift, scale, g, w_in_pt)


def _inproj_bwd_dw(h, dz, name, layer, both=None):
    t = h.shape[0]
    widths = [d.shape[1] for d in dz]
    total = sum(widths)

    def body(*refs):
        h_ref = refs[0]
        dz_refs = refs[1:1 + len(dz)]
        gw_ref = refs[-1]

        @pl.when(pl.program_id(0) == 0)
        def _():
            gw_ref[...] = jnp.zeros_like(gw_ref)

        h_t = h_ref[...]
        c0 = 0
        for ref, w in zip(dz_refs, widths):
            gw_ref[0, :, c0:c0 + w] += _dot_tn(h_t, ref[...])
            c0 += w

    in_specs = [_row_spec(D_MODEL)] + [_row_spec(w) for w in widths]
    args = [h, *dz]
    aliases = {}
    if both is not None:
        in_specs.append(pl.BlockSpec(memory_space=pl.ANY))
        aliases = {len(args): 0}
        args.append(both)
    return pl.pallas_call(
        body, name=name, grid=(t // TM,), in_specs=in_specs,
        out_specs=pl.BlockSpec((1, D_MODEL, total), lambda i: (layer, 0, 0)),
        out_shape=jax.ShapeDtypeStruct((DEPTH, D_MODEL, total), F32), input_output_aliases=aliases,
        compiler_params=_params(("arbitrary",)),
    )(*args)


def _ada_bwd(c_all, dmod_cols, dmod_all):
    n = c_all.shape[0]
    cols = dmod_cols.shape[2]

    def body(c_ref, dc_ref, da_ref, gw_ref, gb_ref):
        act = _silu(c_ref[...]).astype(BF16)
        gw_ref[0] = _dot_tn(act, dc_ref[0].astype(BF16))
        gb_ref[0] = jnp.sum(da_ref[0], axis=0, keepdims=True)

    return pl.pallas_call(
        body, name="ada_bwd", grid=(DEPTH,),
        in_specs=[pl.BlockSpec((n, D_MODEL), lambda l: (0, 0)), pl.BlockSpec((1, n, cols), lambda l: (l, 0, 0)),
                  pl.BlockSpec((1, n, 3 * D_MODEL), lambda l: (l, 0, 0))],
        out_specs=[pl.BlockSpec((1, D_MODEL, cols), lambda l: (l, 0, 0)),
                   pl.BlockSpec((1, 1, 3 * D_MODEL), lambda l: (l, 0, 0))],
        out_shape=[jax.ShapeDtypeStruct((DEPTH, D_MODEL, cols), F32), jax.ShapeDtypeStruct((DEPTH, 1, 3 * D_MODEL), F32)],
        compiler_params=_params(("parallel",)),
    )(c_all, dmod_cols, dmod_all)


def _sum_blocks(parts, name):
    n, rows, cols = parts.shape
    tr = rows if rows <= 256 else 8 * next(d for d in range(32, 0, -1) if (rows // 8) % d == 0)

    def body(p_ref, out_ref):
        acc = p_ref[0].astype(F32)
        for k in range(1, n):
            acc = acc + p_ref[k].astype(F32)
        out_ref[...] = acc

    return pl.pallas_call(
        body, name=name, grid=(rows // tr,), in_specs=[pl.BlockSpec((n, tr, cols), lambda i: (0, i, 0))],
        out_specs=pl.BlockSpec((tr, cols), lambda i: (i, 0)), out_shape=jax.ShapeDtypeStruct((rows, cols), F32),
        compiler_params=_params(("parallel",)),
    )(parts)


def _adamw(w, g, m, v, name):
    nl, rows, cols = w.shape
    if rows * cols <= 64 * 1024:
        tr = rows
        tl = next(t for t in range(nl, 0, -1) if nl % t == 0 and t * max(rows, 8) * cols <= 512 * 1024)
    else:
        tl = 1
        tr = next(t for t in (rows, 256, 128, 64, 32, 16, 8) if rows % t == 0 and t * cols <= 256 * 1024)

    def body(w_ref, g_ref, m_ref, v_ref, d_ref, mo_ref, vo_ref):
        gv = g_ref[...]
        mn = ADAM_B1 * m_ref[...] + (1.0 - ADAM_B1) * gv
        vn = ADAM_B2 * v_ref[...] + (1.0 - ADAM_B2) * jnp.square(gv)
        m_hat = mn / (1.0 - ADAM_B1 ** ADAM_STEP)
        v_hat = vn / (1.0 - ADAM_B2 ** ADAM_STEP)
        d_ref[...] = -ADAM_LR * (m_hat / (jnp.sqrt(v_hat) + ADAM_EPS) + ADAM_WD * w_ref[...])
        mo_ref[...] = mn
        vo_ref[...] = vn

    spec = pl.BlockSpec((tl, tr, cols), lambda l, i: (l, i, 0))
    return pl.pallas_call(
        body, name=name, grid=(nl // tl, rows // tr), in_specs=[spec] * 4, out_specs=[spec] * 3,
        out_shape=[jax.ShapeDtypeStruct((nl, rows, cols), F32)] * 3, compiler_params=_params(("parallel", "parallel")),
    )(w, g, m, v)


ALL_FLIPS = tuple(range(1, N_DEV))


def _transfer(name, srcs, dst_shapes, plan, side=None):
    n_arr = len(srcs)
    probe = plan(0, 0, 0)
    n_steps = len(probe)
    side_in = [a for a, _ in side["inputs"]] if side else []
    side_out = side["outputs"] if side else []

    def body(*refs):
        src_refs, refs = refs[:n_arr], refs[n_arr:]
        side_in_refs, refs = refs[:len(side_in)], refs[len(side_in):]
        dst_refs, refs = refs[:n_arr], refs[n_arr:]
        side_out_refs, refs = refs[:len(side_out)], refs[len(side_out):]
        send_sems, recv_sems, local_sems = refs[:3]
        side_scratch = refs[3:]
        x, y, c = lax.axis_index("x"), lax.axis_index("y"), lax.axis_index("c")
        steps = plan(x, y, c)
        side_done = []

        def rows(ref, r0, n):
            return ref.at[:, pl.ds(r0, n), :]

        def arrival(t):
            a, _, _, n, _, f, _ = steps[t]
            return pltpu.make_async_remote_copy(
                src_ref=rows(dst_refs[a], 0, n), dst_ref=rows(dst_refs[a], 0, n), send_sem=send_sems.at[t],
                recv_sem=recv_sems.at[t], device_id=(x, y, c), device_id_type=pl.DeviceIdType.MESH)

        arrived, started = set(), []
        for t, (a, from_dst, sr, n, dr, f, after) in enumerate(steps):
            if side and t == side["at"]:
                after = tuple(after) + tuple(side["needs"])
            for u in after:
                if u not in arrived:
                    if steps[u][5] == 0:
                        started[u].wait()
                    else:
                        arrival(u).wait_recv()
                    arrived.add(u)
            if side and t == side["at"]:
                side_done.append(side["run"](x, y, c, dst_refs, side_in_refs, side_out_refs, side_scratch))
            src = rows(dst_refs[a] if from_dst else src_refs[a], sr, n)
            dst = rows(dst_refs[a], dr, n)
            if f == 0:
                cp = pltpu.make_async_copy(src, dst, local_sems.at[t])
            else:
                to = (1 - x if f & 4 else x, 1 - y if f & 2 else y, 1 - c if f & 1 else c)
                cp = pltpu.make_async_remote_copy(src_ref=src, dst_ref=dst, send_sem=send_sems.at[t],
                                                  recv_sem=recv_sems.at[t], device_id=to,
                                                  device_id_type=pl.DeviceIdType.MESH)
            cp.start()
            started.append(cp)
        for t, step in enumerate(steps):
            if step[5] != 0 and t not in arrived:
                arrival(t).wait_recv()
        for t, (cp, step) in enumerate(zip(started, steps)):
            if step[5] != 0:
                cp.wait_send()
            elif t not in arrived:
                cp.wait()
        for wait in side_done:
            wait()

    any_spec = pl.BlockSpec(memory_space=pl.ANY)
    return pl.pallas_call(
        body, name=name, out_shape=[jax.ShapeDtypeStruct(s, d) for s, d in list(dst_shapes) + list(side_out)],
        in_specs=[any_spec] * n_arr + [spec for _, spec in (side["inputs"] if side else [])],
        out_specs=[any_spec] * (n_arr + len(side_out)),
        scratch_shapes=[pltpu.SemaphoreType.DMA((n_steps,)), pltpu.SemaphoreType.DMA((n_steps,)),
                        pltpu.SemaphoreType.DMA((n_steps,))] + (side["scratch"] if side else []),
    )(*srcs, *side_in)


CHIP_FLIPS = (2, 4, 6)


def _gather_plan(chip_rows):
    def plan(x, y, c):
        steps = []
        chip = 2 * x + y
        for a, rc in enumerate(chip_rows):
            h = rc // 2
            first = (h // 32) * 16
            mine = rc * chip + h * c
            from_x, from_y, diag = rc * (chip ^ 2) + h * c, rc * (chip ^ 1) + h * c, rc * (chip ^ 3) + h * c
            steps.append((a, False, h * c, h, mine, 0, ()))
            to_x = len(steps)
            steps.append((a, False, h * c, h, mine, 4, ()))
            to_y = len(steps)
            steps.append((a, False, h * c, h, mine, 2, ()))
            fwd_y = len(steps)
            steps.append((a, True, from_x, first, from_x, 2, (to_x,)))
            fwd_x = len(steps)
            steps.append((a, True, from_y + first, h - first, from_y + first, 4, (to_y,)))
            steps.append((a, False, h * c, h, mine, 1, ()))
            steps.append((a, True, from_x, h, from_x, 1, (to_x,)))
            steps.append((a, True, from_y, h, from_y, 1, (to_y,)))
            steps.append((a, True, diag, first, diag, 1, (fwd_y,)))
            steps.append((a, True, diag + first, h - first, diag + first, 1, (fwd_x,)))
        return steps
    return plan


def _pair_reduce_plan(chip_rows):
    def plan(x, y, c):
        steps = []
        for a, rc in enumerate(chip_rows):
            h = rc // 2
            for j in range(4):
                steps.append((a, False, rc * j + h * (1 - c), h, h * j, 1, ()))
        return steps
    return plan


def _chip_scatter_plan(chip_rows):
    def plan(x, y, c):
        steps = []
        for a, rc in enumerate(chip_rows):
            h = rc // 2
            mine = h * (2 * x + y)
            arrivals = []
            for k, f in enumerate(CHIP_FLIPS):
                arrivals.append(len(steps))
                steps.append((a, False, h * ((2 * x + y) ^ (f >> 1)), h, h * k, f, ()))
            steps.append((a, False, mine, h, 3 * h, 0, ()))
            steps.append((a, False, mine, h, 7 * h, 1, ()))
            for k, t in enumerate(arrivals):
                steps.append((a, True, h * k, h, (4 + k) * h, 1, (t,)))
        return steps
    return plan


def _tile_rows(h):
    return h


def _sum_pair(partial, recv, core, rc, name):
    nl, _, cols = partial.shape
    h = rc // 2
    tr = _tile_rows(h)

    def body(c_ref, p_ref, r_ref, out_ref):
        out_ref[...] = (p_ref[...] + r_ref[...]).astype(BF16)

    spec = pl.BlockSpec((1, tr, cols), lambda l, j, i, c_ref: (l, (h // tr) * j + i, 0))
    return pl.pallas_call(
        body, name=name, out_shape=jax.ShapeDtypeStruct((nl, 4 * h, cols), BF16),
        grid_spec=pltpu.PrefetchScalarGridSpec(
            num_scalar_prefetch=1, grid=(nl, 4, h // tr),
            in_specs=[pl.BlockSpec((1, tr, cols), lambda l, j, i, c_ref: (l, (rc // tr) * j + (h // tr) * c_ref[0] + i, 0)),
                      spec],
            out_specs=spec),
        compiler_params=_params(("parallel", "parallel", "parallel")),
    )(core, partial, recv)


def _sum_chips(slots, core, rc, name):
    nl, _, cols = slots.shape
    h = rc // 2

    def body(c_ref, s_ref, out_ref):
        acc = s_ref[0, 3].astype(F32)
        for k in range(3):
            acc = acc + s_ref[0, k].astype(F32)
        out_ref[0] = acc

    return pl.pallas_call(
        body, name=name, out_shape=jax.ShapeDtypeStruct((nl, rc, cols), F32),
        grid_spec=pltpu.PrefetchScalarGridSpec(
            num_scalar_prefetch=1, grid=(nl, 2),
            in_specs=[pl.BlockSpec((1, 4, h, cols), lambda l, q, c_ref: (l, q, 0, 0))],
            out_specs=pl.BlockSpec((1, h, cols), lambda l, q, c_ref: (l, q + c_ref[0] - 2 * q * c_ref[0], 0))),
        compiler_params=_params(("parallel", "parallel")),
    )(core, slots.reshape(nl, 8, h, cols))


def _pad_cols(a, n):
    return a if n == 0 else jnp.pad(a, ((0, 0), (0, n)))


def _in_to_padded(w):
    return jnp.concatenate([_pad_cols(w[:, a:b], z) for a, b, z in IN_PIECES], axis=1)


def _in_cols_first(ga, gb):
    split = Z_G[1]
    pos, out = 0, {}
    for a, b, z in IN_PIECES:
        src, off = (ga, pos) if pos < split else (gb, pos - split)
        out[a] = jnp.transpose(src[:, :, off:off + (b - a)], (2, 0, 1))
        pos += (b - a) + z
    return jnp.concatenate([out[a] for a in sorted(out)], axis=0)


def _out_to_padded(w):
    z = jnp.zeros((64, w.shape[1]), w.dtype)
    return jnp.concatenate([w[0:384], w[384:704], z, w[704:1024], z], axis=0)


def _out_from_padded(gp):
    return jnp.concatenate([gp[0:384], gp[384:704], gp[768:1088]], axis=0)


def _uq_to_padded(w):
    parts = []
    for p in range(3):
        h0, h1 = 2 * p, 2 * p + 1
        parts += [w[:, 96 * h0:96 * h0 + 64], w[:, 96 * h1:96 * h1 + 64], w[:, 96 * h0 + 64:96 * h0 + 96],
                  w[:, 96 * h1 + 64:96 * h1 + 96], jnp.zeros((w.shape[0], 64), w.dtype)]
    return jnp.concatenate(parts, axis=1)


def _uq_from_padded(gp):
    parts = []
    for h in range(6):
        p, s = h // 2, h % 2
        parts += [gp[:, 256 * p + 64 * s:256 * p + 64 * s + 64], gp[:, 256 * p + 128 + 32 * s:256 * p + 160 + 32 * s]]
    return jnp.concatenate(parts, axis=1)


def _ukv_to_padded(w):
    return jnp.concatenate([w[:, 128 * h:128 * h + 64] for h in range(6)]
                           + [w[:, 128 * h + 64:128 * h + 128] for h in range(6)], axis=1)


def _ukv_from_padded(gp):
    parts = []
    for h in range(6):
        parts += [gp[:, 64 * h:64 * h + 64], gp[:, 384 + 64 * h:384 + 64 * h + 64]]
    return jnp.concatenate(parts, axis=1)


LR_ROWS = 224
SHARD_ROWS = (256, 256, 256, LR_ROWS)


def _pack_lowrank(w_uq, w_ukv):
    flat = jnp.concatenate([w_uq.reshape(-1), w_ukv.reshape(-1)])
    return jnp.pad(flat, (0, LR_ROWS * PACK_COLS - flat.shape[0])).reshape(1, LR_ROWS, PACK_COLS)


def _unpack_lowrank(packed):
    flat = packed.reshape(-1)
    n_uq = DEPTH * A_Q_RANK * 144
    n_ukv = DEPTH * A_KV_RANK * 192
    return flat[0:n_uq].reshape(DEPTH, A_Q_RANK, 144), flat[n_uq:n_uq + n_ukv].reshape(DEPTH, A_KV_RANK, 192)


def _rope_tables(positions):
    t = positions.size
    inv = ROPE_THETA ** (-jnp.arange(0, A_ROPE, 2, dtype=F32) / A_ROPE)
    inv_row = jnp.pad(jnp.tile(inv, 4), (0, 64)).reshape(1, LANE)

    def body(p_ref, i_ref, c_ref, sa_ref, sb_ref):
        ang = p_ref[...].astype(F32) * i_ref[...]
        lane = _lane_iota((TM, LANE))
        live = lane < 64
        second = (lane & 31) >= 16
        s = jnp.sin(ang)
        c_ref[...] = jnp.where(live, jnp.cos(ang), 0.0)
        sa_ref[...] = jnp.where(live & jnp.logical_not(second), -s, 0.0)
        sb_ref[...] = jnp.where(live & second, s, 0.0)

    return pl.pallas_call(
        body, name="rope_tables", grid=(t // TM,), in_specs=[_row_spec(1), _full_spec((1, LANE))],
        out_specs=[_row_spec(LANE)] * 3, out_shape=[jax.ShapeDtypeStruct((t, LANE), F32)] * 3,
        compiler_params=_params(("parallel",)),
    )(positions.reshape(t, 1), inv_row)


def _rows(a, n):
    flat = a.reshape(-1)
    return jnp.pad(flat, (0, n * LANE - flat.shape[0])).reshape(n, LANE)


def _forward_backward(x, mod, tables, target, weights, small, nb, seq):
    cos, sina, sinb = tables
    saved = []
    for l in range(DEPTH):
        w, s = weights[l], small[l]
        shift = mod[l][:, None, 0:D_MODEL]
        scale = mod[l][:, None, D_MODEL:2 * D_MODEL]
        gate = mod[l][:, None, 2 * D_MODEL:]
        h, za, zg, qb, kb, vb, qc, kc, vc, zf = _ln_inproj(x, shift, scale, s["norm_g"], w["in"], seq)
        qa, ka, va = _a_up(za, s["gq"], s["gkv"], w["uq"], w["ukv"], cos, sina, sinb)
        bias = _bias_expand(s["g8"])[0:6]
        f = _forget_fwd(zf, s["fb"], nb, seq)
        frow = jnp.pad(f[:, 0:5].reshape(nb, seq, 5).transpose(0, 2, 1), ((0, 0), (0, 1), (0, 0)))
        frow = frow.reshape(nb, 6, 1, seq)
        oa, lse_a = _attn_fwd("A", qa, ka, va, None, nb, seq)
        ob, lse_b = _attn_fwd("B", qb, kb, vb, bias, nb, seq)
        oc, lse_c = _attn_fwd("C", qc, kc, vc, frow, nb, seq)
        if l + 1 < DEPTH:
            y, x_next = _gate_outproj(x, gate, oa, ob, oc, zg, w["out"], seq)
        else:
            dx, *head, loss8, gfinal8 = _head(x, gate, oa, ob, oc, zg, w["out"], w["out_t"], target, s["final_g"],
                                              nb, seq)
            y = x_next = None
        saved.append(dict(x=x, h=h, za=za, zg=zg, zf=zf, y=y, shift=shift, scale=scale, gate=gate, bias=bias, frow=frow,
                          a=(qa, ka, va, oa, lse_a), b=(qb, kb, vb, ob, lse_b), c=(qc, kc, vc, oc, lse_c)))
        x = x_next
    grads = []
    gw_in = (None, None)
    for l in reversed(range(DEPTH)):
        w, s, sv = weights[l], small[l], saved[l]
        qa, ka, va, oa, lse_a = sv["a"]
        qb, kb, vb, ob, lse_b = sv["b"]
        qc, kc, vc, oc, lse_c = sv["c"]
        if l + 1 < DEPTH:
            doa, dob, doc, dzg, gw_out, dgate = _outproj_bwd(dx, sv["y"], sv["gate"], oa, ob, oc, sv["zg"], w["out"],
                                                              w["out_t"], nb, seq)
        else:
            doa, dob, doc, dzg, gw_out, dgate = head
        dqa, dka, dva = _attn_bwd("A", qa, ka, va, oa, doa, lse_a, None, nb, seq)
        dqb, dkb, dvb, dbt = _attn_bwd("B", qb, kb, vb, ob, dob, lse_b, sv["bias"], nb, seq)
        dqc, dkc, dvc, dfr, dfq = _attn_bwd("C", qc, kc, vc, oc, doc, lse_c, sv["frow"], nb, seq)
        dg = _bias_reduce(jnp.pad(dbt, ((0, 2), (0, 0), (0, 0))))
        grb = jnp.pad(_bias_unline(dg), ((0, 0), (0, 384 - N_REL)))
        dfk = dfr.reshape(nb, 6, seq).transpose(0, 2, 1).reshape(nb * seq, 6)
        dzf, gfb = _forget_bwd(dfq, jnp.pad(dfk, ((0, 0), (0, LANE - 6))), sv["zf"], s["fb"], nb, seq)
        dza, gw_uq, gw_ukv, ggq, ggkv = _a_up_bwd(dqa, dka, dva, sv["za"], s["gq"], s["gkv"], w["uq_t"], w["ukv_t"],
                                                  cos, sina, sinb)
        dz = (dza, dzg, dqb, dkb, dvb, dqc, dkc, dvc, dzf)
        dx, dshift, dscale, gnorm = _inproj_bwd_dx(dz, dx, sv["x"], sv["shift"], sv["scale"], s["norm_g"], w["in"],
                                                   nb, seq)
        gw_in = (_inproj_bwd_dw(sv["h"], dz[0:2], "inproj_bwd_dw0", l, gw_in[0]),
                 _inproj_bwd_dw(sv["h"], dz[2:], "inproj_bwd_dw1", l, gw_in[1]))
        dmod = jnp.concatenate([dshift[:, 0], dscale[:, 0], dgate[:, 0]], axis=1)
        grads.append(dict(w_out=gw_out, w_uq=gw_uq, w_ukv=gw_ukv, dmod=dmod, norm_g=gnorm[0], gq=ggq[0],
                          gkv=ggkv[0], rb8=grb, fb=gfb[0]))
    grads.reverse()
    return loss8[0, 0], dx, grads, gfinal8[0], gw_in


def _layer_weights(w_in, w_out, w_uq, w_ukv):
    wi, wo, wq, wkv = _in_to_padded(w_in), _out_to_padded(w_out), _uq_to_padded(w_uq), _ukv_to_padded(w_ukv)
    return {"in": wi, "out": wo, "out_t": wo.T, "uq": wq, "uq_t": wq.T, "ukv": wkv, "ukv_t": wkv.T}


def _layer_small(norm_g, gq, gkv, rel_bias, forget_b, final_g):
    fb = jnp.pad(forget_b, (0, LANE - 5)).reshape(1, LANE)
    return dict(norm_g=norm_g.reshape(1, -1), gq=gq.reshape(1, -1), gkv=gkv.reshape(1, -1), g8=_bias_line(rel_bias), fb=fb,
                final_g=final_g.reshape(1, -1))


def _small_payload(per_layer, final_g, loss):
    def stack(key):
        return jnp.stack([p[key] for p in per_layer])

    def rows(a, rng):
        return _rows(a, rng[1] - rng[0])

    parts = [rows(stack("dmod"), PAY_DMOD), rows(stack("norm_g"), PAY_NORM), rows(stack("gq"), PAY_GQ),
             rows(stack("gkv"), PAY_GKV), rows(stack("rb8"), PAY_RB), rows(stack("fb"), PAY_FB),
             rows(final_g, PAY_FINAL), rows(loss, PAY_LOSS)]
    return jnp.concatenate(parts, axis=0)


def _payload_split(pay):
    def take(rng, shape):
        n = 1
        for d in shape:
            n *= d
        return pay[rng[0]:rng[1]].reshape(-1)[0:n].reshape(shape)

    norm_g = take(PAY_NORM, (DEPTH, D_MODEL))
    gq = take(PAY_GQ, (DEPTH, A_Q_RANK))
    gkv = take(PAY_GKV, (DEPTH, A_KV_RANK))
    rb = take(PAY_RB, (DEPTH, 8, 384))[:, 0:5, 0:N_REL]
    fb = take(PAY_FB, (DEPTH, LANE))[:, 0:5]
    final_g = take(PAY_FINAL, (D_MODEL,))
    return norm_g, gq, gkv, rb, fb, final_g


def kernel(x, c, positions, w_ada, b_ada, norm_g, w_in, a_q_norm_g, a_w_uq, a_kv_norm_g, a_w_ukv, b_rel_bias, c_forget_b, w_out, final_g, loss_target, m_w_ada, m_b_ada, m_norm_g, m_w_in, m_a_q_norm_g, m_a_w_uq, m_a_kv_norm_g, m_a_w_ukv, m_b_rel_bias, m_c_forget_b, m_w_out, m_final_g, v_w_ada, v_b_ada, v_norm_g, v_w_in, v_a_q_norm_g, v_a_w_uq, v_a_kv_norm_g, v_a_w_ukv, v_b_rel_bias, v_c_forget_b, v_w_out, v_final_g):
    nb, seq, _ = x.shape
    ix, iy, ic = lax.axis_index("x"), lax.axis_index("y"), lax.axis_index("c")
    chip = 2 * ix + iy

    weight_plan = _gather_plan((256, 256, LR_ROWS))
    cols = w_ada.shape[2]

    def gather_plan(x, y, c_):
        me_ = 4 * x + 2 * y + c_
        first = [(3, False, 0, 8, 8 * me_, f, ()) for f in range(N_DEV)]
        rest = weight_plan(x, y, c_)
        order = [i for i, s in enumerate(rest) if not s[6]] + [i for i, s in enumerate(rest) if s[6]]
        place = {old: len(first) + new for new, old in enumerate(order)}
        return first + [rest[i][:6] + (tuple(place[u] for u in rest[i][6]),) for i in order]

    probe = gather_plan(0, 0, 0)

    def project(x, y, c_, dst_refs, in_refs, out_refs, scratch):
        w_ref, b_ref = in_refs
        (mod_ref,) = out_refs
        c_vm, mod_vm, send_m, recv_m, local_m = scratch
        me_ = 4 * x + 2 * y + c_
        load = pltpu.make_async_copy(dst_refs[3].at[0], c_vm, local_m.at[0])
        load.start()
        load.wait()
        act = _silu(c_vm[...]).astype(BF16)
        for l in range(DEPTH):
            mod_vm[l] = _dot(act, w_ref[l].astype(BF16)) + b_ref[l]

        def rows_of(j):
            return mod_vm.at[:, pl.ds(pl.multiple_of(8 * j, 8), 8), :]

        def send(i, f, slot, j):
            to = (1 - x if f & 4 else x, 1 - y if f & 2 else y, 1 - c_ if f & 1 else c_)
            return pltpu.make_async_remote_copy(src_ref=rows_of(j), dst_ref=mod_ref.at[slot], send_sem=send_m.at[i],
                                                recv_sem=recv_m.at[i], device_id=to, device_id_type=pl.DeviceIdType.MESH)

        own = pltpu.make_async_copy(rows_of(me_), mod_ref.at[me_], local_m.at[1])
        own.start()
        sends = [send(i, f, me_, me_ ^ f) for i, f in enumerate(ALL_FLIPS)]
        for s in sends:
            s.start()

        def finish():
            for i, f in enumerate(ALL_FLIPS):
                send(i, f, me_ ^ f, 0).wait_recv()
            for s in sends:
                s.wait_send()
            own.wait()
        return finish

    b_cols = lax.dynamic_slice_in_dim(b_ada, chip * cols, cols, axis=1)[:, None, :]
    vmem = pl.BlockSpec(memory_space=pltpu.VMEM)
    side = dict(at=next(t for t, s in enumerate(probe) if s[6]), needs=tuple(range(N_DEV)),
                inputs=[(w_ada, vmem), (b_cols, vmem)], outputs=[((N_DEV, DEPTH, 8, cols), F32)],
                scratch=[pltpu.VMEM((8 * N_DEV, D_MODEL), F32), pltpu.VMEM((DEPTH, 8 * N_DEV, cols), F32),
                         pltpu.SemaphoreType.DMA((len(ALL_FLIPS),)), pltpu.SemaphoreType.DMA((len(ALL_FLIPS),)),
                         pltpu.SemaphoreType.DMA((2,))],
                run=project)
    full_in, full_out, full_lr, c_rows, mod_rows = _transfer(
        "gather_weights", [w_in.astype(BF16), w_out.astype(BF16), _pack_lowrank(a_w_uq, a_w_ukv).astype(BF16),
                           jnp.pad(c, ((0, 8 - nb), (0, 0)))[None]],
        [((DEPTH, D_MODEL, N_IN), BF16), ((DEPTH, D_MODEL, D_MODEL), BF16), ((1, 4 * LR_ROWS, PACK_COLS), BF16),
         ((1, 8 * N_DEV, D_MODEL), F32)], gather_plan, side)
    lowrank = [_unpack_lowrank(full_lr[0, LR_ROWS * j:LR_ROWS * (j + 1)]) for j in range(4)]
    full_uq = jnp.concatenate([s[0] for s in lowrank], axis=2)
    full_ukv = jnp.concatenate([s[1] for s in lowrank], axis=2)
    weights = [_layer_weights(full_in[l], full_out[l], full_uq[l], full_ukv[l]) for l in range(DEPTH)]
    small = [_layer_small(norm_g[l], a_q_norm_g[l], a_kv_norm_g[l], b_rel_bias[l], c_forget_b[l], final_g)
             for l in range(DEPTH)]

    c_all = c_rows.reshape(N_DEV, 8, D_MODEL)[:, 0:nb].reshape(N_DEV * nb, D_MODEL)
    mod = jnp.concatenate([mod_rows[2 * j][:, 0:nb] for j in range(4)], axis=2)

    tables = _rope_tables(positions)
    loss_part, dx, grads, gfinal, gw_in = _forward_backward(
        x.reshape(nb * seq, D_MODEL), mod, tables, loss_target.reshape(nb * seq, D_MODEL), weights, small, nb, seq)

    g_uq = jnp.stack([_uq_from_padded(g["w_uq"]) for g in grads])
    g_ukv = jnp.stack([_ukv_from_padded(g["w_ukv"]) for g in grads])
    g_lr = jnp.concatenate([_pack_lowrank(g_uq[:, :, 144 * j:144 * (j + 1)], g_ukv[:, :, 192 * j:192 * (j + 1)])
                            for j in range(4)], axis=1)
    partials = [gw_in[0], gw_in[1], jnp.stack([_out_from_padded(g["w_out"]) for g in grads]), g_lr]
    shapes = [(p.shape[0], p.shape[2]) for p in partials]
    halves = [r // 2 for r in SHARD_ROWS]
    core_s = jnp.reshape(ic, (1,)).astype(jnp.int32)
    pay = _small_payload(grads, gfinal, loss_part)
    pay_rows = pay.shape[0]
    reduce_plan = _pair_reduce_plan(SHARD_ROWS)

    def reduce_plan_with_payload(x, y, c_):
        me_ = 4 * x + 2 * y + c_
        return reduce_plan(x, y, c_) + [(len(partials), False, 0, pay_rows, pay_rows * me_, f, ()) for f in range(N_DEV)]

    *from_pair, pay_all = _transfer(
        "pair_reduce", partials + [pay[None]],
        [((nl, 4 * h, nc), F32) for (nl, nc), h in zip(shapes, halves)] + [((1, N_DEV * pay_rows, LANE), F32)],
        reduce_plan_with_payload)
    pay_all = pay_all.reshape(N_DEV, pay_rows, LANE)

    tot = _sum_blocks(pay_all, "sum_small")
    loss = tot[PAY_LOSS[0], 0]
    dmod_all = pay_all[:, PAY_DMOD[0]:PAY_DMOD[1]].reshape(N_DEV, -1)[:, 0:DEPTH * nb * 3 * D_MODEL]
    dmod_all = dmod_all.reshape(N_DEV, DEPTH, nb, 3 * D_MODEL).transpose(1, 0, 2, 3)
    dmod_all = dmod_all.reshape(DEPTH, N_DEV * nb, 3 * D_MODEL)
    my_cols = lax.dynamic_slice_in_dim(dmod_all, chip * cols, cols, axis=2)
    g_w_ada, g_b_ada = _ada_bwd(c_all, my_cols, dmod_all)
    g_b_ada = g_b_ada[:, 0]
    chip_sums = [_sum_pair(p, r, core_s, rc, "sum_pair%d" % i)
                 for i, (p, r, rc) in enumerate(zip(partials, from_pair, SHARD_ROWS))]
    slots = _transfer("chip_scatter", chip_sums, [((nl, 8 * h, nc), BF16) for (nl, nc), h in zip(shapes, halves)],
                      _chip_scatter_plan(SHARD_ROWS))
    g_in_a, g_in_b, g_out_sh, g_lr_sh = [_sum_chips(s, core_s, rc, "sum_chips%d" % i)
                                         for i, (s, rc) in enumerate(zip(slots, SHARD_ROWS))]
    g_uq_sh, g_ukv_sh = _unpack_lowrank(g_lr_sh[0])

    def cols_first(a):
        return jnp.transpose(a, (2, 0, 1))

    g_in_t = _in_cols_first(g_in_a, g_in_b)
    upd_in = tuple(jnp.transpose(a, (1, 2, 0)) for a in _adamw(cols_first(w_in), g_in_t, cols_first(m_w_in),
                                                               cols_first(v_w_in), "adamw_in"))
    gw = (jnp.transpose(g_in_t, (1, 2, 0)), g_out_sh, g_uq_sh, g_ukv_sh)
    upd = [upd_in, _adamw(w_out, gw[1], m_w_out, v_w_out, "adamw_out"),
           _adamw(a_w_uq, gw[2], m_a_w_uq, v_a_w_uq, "adamw_uq"), _adamw(a_w_ukv, gw[3], m_a_w_ukv, v_a_w_ukv, "adamw_ukv")]
    dw, mw, vw = (tuple(u[i] for u in upd) for i in range(3))
    d_ada, m_ada, v_ada = _adamw(w_ada, g_w_ada, m_w_ada, v_w_ada, "adamw_ada")

    def adam_small(w, g, m, v, name):
        shape3 = (1,) * (3 - w.ndim) + w.shape
        return tuple(a.reshape(w.shape) for a in _adamw(w.reshape(shape3), g.reshape(shape3), m.reshape(shape3),
                                                        v.reshape(shape3), name))

    d_b, m_b, v_b = adam_small(b_ada, g_b_ada, m_b_ada, v_b_ada, "adamw_b_ada")
    gs = _payload_split(tot)
    small_upd = [adam_small(w, g, m, v, "adamw_small%d" % i) for i, (w, g, m, v) in enumerate(zip(
        (norm_g, a_q_norm_g, a_kv_norm_g, b_rel_bias, c_forget_b, final_g), gs,
        (m_norm_g, m_a_q_norm_g, m_a_kv_norm_g, m_b_rel_bias, m_c_forget_b, m_final_g),
        (v_norm_g, v_a_q_norm_g, v_a_kv_norm_g, v_b_rel_bias, v_c_forget_b, v_final_g)))]
    ds, ms, vs = (tuple(u[i] for u in small_upd) for i in range(3))

    def ordered(ada, b, sm, big):
        ng, gq, gkv, rb, fb, fg = sm
        b_in, b_out, b_uq, b_ukv = big
        return (ada, b, ng, b_in, gq, b_uq, gkv, b_ukv, rb, fb, b_out, fg)

    return (loss, dx.reshape(nb, seq, D_MODEL), *ordered(g_w_ada, g_b_ada, gs, gw), *ordered(d_ada, d_b, ds, dw),
            *ordered(m_ada, m_b, ms, mw), *ordered(v_ada, v_b, vs, vw))
```

```python
import functools

import jax
import jax.numpy as jnp
from jax import lax
from jax.experimental import pallas as pl
from jax.experimental.pallas import tpu as pltpu

F32 = jnp.float32
BF16 = jnp.bfloat16

D_MODEL = 1024
DEPTH = 2
EPS = 1e-6
NEG = -1e30
LOG2E = 1.4426950408889634
ROPE_THETA = 10000.0
A_ROPE = 32
A_Q_RANK = 384
A_KV_RANK = 256
REL_CLIP = 128
N_REL = 2 * REL_CLIP + 1
N_IN = 3621

ADAM_LR = 0.001
ADAM_B1 = 0.9
ADAM_B2 = 0.999
ADAM_EPS = 1e-08
ADAM_WD = 0.01
ADAM_STEP = 10

LANE = 128
VMEM_LIMIT = 56 * 1024 * 1024

NP_IN = 4352
Z_A = (0, 768)
Z_G = (768, 1920)
Z_QKV = tuple((1920 + 384 * i, 1920 + 384 * (i + 1)) for i in range(6))
Z_F = (4224, 4352)
IN_PIECES = ((0, 672, 96), (672, 1056, 0), (2016, 2336, 64), (3301, 3621, 64), (1056, 1376, 64), (1376, 1696, 64),
             (1696, 2016, 64), (2336, 2656, 64), (2656, 2976, 64), (2976, 3296, 64), (3296, 3301, 123))
D_CAT = 1152

TM = 512
T_CAUSAL = 256
T_BAND = 128
BAND_TILES = 5
N_DEV = 8

PAY_DMOD = (0, 96)
PAY_NORM = (96, 112)
PAY_GQ = (112, 120)
PAY_GKV = (120, 128)
PAY_RB = (128, 176)
PAY_FB = (176, 184)
PAY_FINAL = (184, 192)
PAY_LOSS = (192, 200)

PACK_COLS = 1024


def _params(sem=None):
    return pltpu.CompilerParams(dimension_semantics=sem, vmem_limit_bytes=VMEM_LIMIT)


def _lane_iota(shape):
    return lax.broadcasted_iota(jnp.int32, shape, len(shape) - 1)


def _silu(u):
    return u * jax.nn.sigmoid(u)


def _dsilu(u):
    s = jax.nn.sigmoid(u)
    return s * (1.0 + u * (1.0 - s))


def _rms(x, g):
    r = lax.rsqrt(jnp.mean(x * x, axis=-1, keepdims=True) + EPS)
    xh = x * r
    return xh * g, xh, r


def _rms_bwd(dy, xh, r, g):
    dxh = dy * g
    return r * (dxh - xh * jnp.mean(dxh * xh, axis=-1, keepdims=True))


def _rope(x, cos, sina, sinb):
    return x * cos + pltpu.roll(x, 16, 1) * sinb + pltpu.roll(x, LANE - 16, 1) * sina


def _rope_t(dy, cos, sina, sinb):
    return dy * cos + pltpu.roll(dy * sinb, LANE - 16, 1) + pltpu.roll(dy * sina, 16, 1)


def _split3(x):
    hi = x.astype(BF16)
    r1 = x - hi.astype(F32)
    mid = r1.astype(BF16)
    lo = (r1 - mid.astype(F32)).astype(BF16)
    return hi, mid, lo


def _dot(a, b):
    return jnp.dot(a, b, preferred_element_type=F32)


def _dot_nt(a, b):
    return lax.dot_general(a, b, (((1,), (1,)), ((), ())), preferred_element_type=F32)


def _dot_tn(a, b):
    return lax.dot_general(a, b, (((0,), (0,)), ((), ())), preferred_element_type=F32)


def _row_spec(cols):
    return pl.BlockSpec((TM, cols), lambda i: (i, 0))


def _full_spec(shape):
    return pl.BlockSpec(shape, lambda i: (0,) * len(shape))


def _ex_spec(tiles_per_ex):
    return pl.BlockSpec((1, 1, D_MODEL), lambda i: (i // tiles_per_ex, 0, 0))


def _ln_inproj(x, shift, scale, g, w_in_p, seq):
    t = x.shape[0]

    def body(x_ref, sh_ref, sc_ref, g_ref, w_ref, h_ref, za_ref, zg_ref, q0, q1, q2, q3, q4, q5, zf_ref):
        n, _, _ = _rms(x_ref[...], g_ref[...])
        h = (n * (1.0 + sc_ref[0]) + sh_ref[0]).astype(BF16)
        h_ref[...] = h
        za_ref[...] = _dot(h, w_ref[:, Z_A[0]:Z_A[1]])
        zg_ref[...] = _dot(h, w_ref[:, Z_G[0]:Z_G[1]])
        for ref, (c0, c1) in zip((q0, q1, q2, q3, q4, q5), Z_QKV):
            ref[...] = _dot(h, w_ref[:, c0:c1]).astype(BF16)
        zf_ref[...] = _dot(h, w_ref[:, Z_F[0]:Z_F[1]])

    tpe = seq // TM
    shapes = [jax.ShapeDtypeStruct((t, D_MODEL), BF16), jax.ShapeDtypeStruct((t, 768), F32),
              jax.ShapeDtypeStruct((t, D_CAT), F32)]
    shapes += [jax.ShapeDtypeStruct((t, 384), BF16)] * 6 + [jax.ShapeDtypeStruct((t, LANE), F32)]
    return pl.pallas_call(
        body, name="ln_inproj", grid=(t // TM,),
        in_specs=[_row_spec(D_MODEL), _ex_spec(tpe), _ex_spec(tpe), _full_spec((1, D_MODEL)),
                  _full_spec((D_MODEL, NP_IN))],
        out_specs=[_row_spec(D_MODEL), _row_spec(768), _row_spec(D_CAT)] + [_row_spec(384)] * 6 + [_row_spec(LANE)],
        out_shape=shapes, compiler_params=_params(("parallel",)),
    )(x, shift, scale, g, w_in_p)


def _a_up(za, gq, gkv, w_uq_p, w_ukv_p, cos, sina, sinb):
    t = za.shape[0]

    def body(za_ref, gq_ref, gkv_ref, wq_ref, wkv_ref, cos_ref, sa_ref, sb_ref, q_ref, k_ref, v_ref):
        cos_t, sa, sb = cos_ref[...], sa_ref[...], sb_ref[...]
        cqn, _, _ = _rms(za_ref[:, 0:384], gq_ref[...])
        q = _dot(cqn.astype(BF16), wq_ref[...])
        ckvn, _, _ = _rms(za_ref[:, 384:640], gkv_ref[...])
        kv = _dot(ckvn.astype(BF16), wkv_ref[...])
        kpe = za_ref[:, 640:768]
        kpe = _rope(kpe + pltpu.roll(kpe, 32, 1), cos_t, sa, sb).astype(BF16)
        for p in range(3):
            q_ref[:, 256 * p:256 * p + 128] = q[:, 256 * p:256 * p + 128].astype(BF16)
            q_ref[:, 256 * p + 128:256 * p + 256] = _rope(q[:, 256 * p + 128:256 * p + 256], cos_t, sa, sb).astype(BF16)
            k_ref[:, 256 * p:256 * p + 128] = kv[:, 128 * p:128 * p + 128].astype(BF16)
            k_ref[:, 256 * p + 128:256 * p + 256] = kpe
        v_ref[...] = kv[:, 384:768].astype(BF16)

    return pl.pallas_call(
        body, name="a_up", grid=(t // TM,),
        in_specs=[_row_spec(768), _full_spec((1, 384)), _full_spec((1, 256)), _full_spec((384, 768)),
                  _full_spec((256, 768)), _row_spec(LANE), _row_spec(LANE), _row_spec(LANE)],
        out_specs=[_row_spec(768), _row_spec(768), _row_spec(384)],
        out_shape=[jax.ShapeDtypeStruct((t, 768), BF16), jax.ShapeDtypeStruct((t, 768), BF16),
                   jax.ShapeDtypeStruct((t, 384), BF16)],
        compiler_params=_params(("parallel",)),
    )(za, gq, gkv, w_uq_p, w_ukv_p, cos, sina, sinb)


def _tri(n, upper):
    r = lax.broadcasted_iota(jnp.int32, (n, n), 0)
    c = lax.broadcasted_iota(jnp.int32, (n, n), 1)
    return jnp.where((c >= r) if upper else (c <= r), 1.0, 0.0).astype(BF16)


def _forget_fwd(zf, fb, nb, seq):
    blk = 256

    def body(zf_ref, fb_ref, f_ref):
        tri = _tri(blk, False)
        live = _lane_iota((blk, LANE)) < 5
        carry = jnp.zeros((1, LANE), F32)
        for i in range(seq // blk):
            u = zf_ref[i * blk:(i + 1) * blk, :] + fb_ref[...]
            lf = jnp.where(live, jnp.minimum(u, 0.0) - jnp.log(1.0 + jnp.exp(-jnp.abs(u))), 0.0)
            hi, mid, lo = _split3(lf)
            f_ref[i * blk:(i + 1) * blk, :] = (_dot(tri, hi) + _dot(tri, mid) + _dot(tri, lo) + carry) * LOG2E
            carry = carry + jnp.sum(lf, axis=0, keepdims=True)

    return pl.pallas_call(
        body, name="forget_fwd", grid=(nb,),
        in_specs=[pl.BlockSpec((seq, LANE), lambda b: (b, 0)), pl.BlockSpec((1, LANE), lambda b: (0, 0))],
        out_specs=pl.BlockSpec((seq, LANE), lambda b: (b, 0)),
        out_shape=jax.ShapeDtypeStruct((nb * seq, LANE), F32), compiler_params=_params(("parallel",)),
    )(zf, fb)


def _forget_bwd(dfq, dfk, zf, fb, nb, seq):
    blk = 256

    def body(dfq_ref, dfk_ref, zf_ref, fb_ref, dz_ref, gb_ref):
        @pl.when(pl.program_id(0) == 0)
        def _():
            gb_ref[...] = jnp.zeros_like(gb_ref)

        tri = _tri(blk, True)
        lane = _lane_iota((blk, LANE))
        wide = _lane_iota((blk, 384))
        live = lane < 5
        carry = jnp.zeros((1, LANE), F32)
        gsum = jnp.zeros((1, LANE), F32)
        for i in reversed(range(seq // blk)):
            d = dfk_ref[i * blk:(i + 1) * blk, :]
            dq = dfq_ref[i * blk:(i + 1) * blk, :]
            for hd in range(5):
                col = jnp.sum(jnp.where(wide == 64 * hd, dq, 0.0), axis=-1, keepdims=True)
                d = d + jnp.where(lane == hd, col, 0.0)
            hi, mid, lo = _split3(d)
            dlf = _dot(tri, hi) + _dot(tri, mid) + _dot(tri, lo) + carry
            carry = carry + jnp.sum(d, axis=0, keepdims=True)
            u = zf_ref[i * blk:(i + 1) * blk, :] + fb_ref[...]
            du = jnp.where(live, dlf * jax.nn.sigmoid(-u), 0.0)
            dz_ref[i * blk:(i + 1) * blk, :] = du.astype(BF16)
            gsum = gsum + jnp.sum(du, axis=0, keepdims=True)
        gb_ref[...] += jnp.broadcast_to(gsum, gb_ref.shape)

    return pl.pallas_call(
        body, name="forget_bwd", grid=(nb,),
        in_specs=[pl.BlockSpec((seq, 384), lambda b: (b, 0)), pl.BlockSpec((seq, LANE), lambda b: (b, 0)),
                  pl.BlockSpec((seq, LANE), lambda b: (b, 0)), pl.BlockSpec((1, LANE), lambda b: (0, 0))],
        out_specs=[pl.BlockSpec((seq, LANE), lambda b: (b, 0)), pl.BlockSpec((8, LANE), lambda b: (0, 0))],
        out_shape=[jax.ShapeDtypeStruct((nb * seq, LANE), BF16), jax.ShapeDtypeStruct((8, LANE), F32)],
        compiler_params=_params(("arbitrary",)),
    )(dfq, dfk, zf, fb)


def _gate_outproj(x, gate, oa, ob, oc, zg, w_out_p, seq):
    t = x.shape[0]

    def body(x_ref, gate_ref, oa_ref, ob_ref, oc_ref, zg_ref, w_ref, y_ref, xn_ref):
        y = jnp.zeros((TM, D_MODEL), F32)
        for i, o_ref in enumerate((oa_ref, ob_ref, oc_ref)):
            cat = (o_ref[...] * _silu(zg_ref[:, 384 * i:384 * (i + 1)])).astype(BF16)
            y = y + _dot(cat, w_ref[384 * i:384 * (i + 1), :])
        y_ref[...] = y
        xn_ref[...] = x_ref[...] + gate_ref[0] * y

    return pl.pallas_call(
        body, name="gate_outproj", grid=(t // TM,),
        in_specs=[_row_spec(D_MODEL), _ex_spec(seq // TM), _row_spec(384), _row_spec(384), _row_spec(384),
                  _row_spec(D_CAT), _full_spec((D_CAT, D_MODEL))],
        out_specs=[_row_spec(D_MODEL), _row_spec(D_MODEL)],
        out_shape=[jax.ShapeDtypeStruct((t, D_MODEL), F32)] * 2, compiler_params=_params(("parallel",)),
    )(x, gate, oa, ob, oc, zg, w_out_p)


def _gate_outproj_loss(x, gate, oa, ob, oc, zg, w_out_p, target, g, seq):
    t = x.shape[0]

    def body(x_ref, gate_ref, oa_ref, ob_ref, oc_ref, zg_ref, w_ref, t_ref, g_ref, y_ref, dx_ref, loss_ref, gg_ref):
        @pl.when(pl.program_id(0) == 0)
        def _():
            loss_ref[...] = jnp.zeros_like(loss_ref)
            gg_ref[...] = jnp.zeros_like(gg_ref)

        y = jnp.zeros((TM, D_MODEL), F32)
        for i, o_ref in enumerate((oa_ref, ob_ref, oc_ref)):
            cat = (o_ref[...] * _silu(zg_ref[:, 384 * i:384 * (i + 1)])).astype(BF16)
            y = y + _dot(cat, w_ref[384 * i:384 * (i + 1), :])
        y_ref[...] = y
        gv = g_ref[...]
        out, xh, r = _rms(x_ref[...] + gate_ref[0] * y, gv)
        err = out - t_ref[...]
        loss_ref[...] += 0.5 * jnp.sum(jnp.mean(err * err, axis=-1, keepdims=True), axis=0, keepdims=True)
        dout = err / D_MODEL
        gg_ref[...] += jnp.broadcast_to(jnp.sum(dout * xh, axis=0, keepdims=True), gg_ref.shape)
        dx_ref[...] = _rms_bwd(dout, xh, r, gv)

    return pl.pallas_call(
        body, name="gate_outproj_loss", grid=(t // TM,),
        in_specs=[_row_spec(D_MODEL), _ex_spec(seq // TM), _row_spec(384), _row_spec(384), _row_spec(384),
                  _row_spec(D_CAT), _full_spec((D_CAT, D_MODEL)), _row_spec(D_MODEL), _full_spec((1, D_MODEL))],
        out_specs=[_row_spec(D_MODEL), _row_spec(D_MODEL), _full_spec((8, LANE)), _full_spec((8, D_MODEL))],
        out_shape=[jax.ShapeDtypeStruct((t, D_MODEL), F32), jax.ShapeDtypeStruct((t, D_MODEL), F32),
                   jax.ShapeDtypeStruct((8, LANE), F32), jax.ShapeDtypeStruct((8, D_MODEL), F32)],
        compiler_params=_params(("arbitrary",)),
    )(x, gate, oa, ob, oc, zg, w_out_p, target, g)


def _head_masks(kind, rows, dq, h):
    lq = _lane_iota((rows, dq))
    lv = _lane_iota((rows, LANE))
    mq = (lq >= 64 * h) & (lq < 64 * h + 64)
    if kind == "A":
        mq = mq | ((lq >= 128 + 32 * h) & (lq < 160 + 32 * h))
    return mq, (lv >= 64 * h) & (lv < 64 * h + 64)


def _tile_mask(kind, tile):
    row = lax.broadcasted_iota(jnp.int32, (tile, tile), 0)
    col = lax.broadcasted_iota(jnp.int32, (tile, tile), 1)
    return (col >> 6) <= (row >> 6) if kind == "A" else col <= row


def _attn_scale(kind):
    return 96.0 ** -0.5 if kind == "A" else 0.125


BAND_W = BAND_TILES * T_BAND


def _segments(kind, qi, tile):
    r0 = qi * tile
    if kind == "B":
        lo = max(qi - (BAND_TILES - 1), 0) * tile
        return [(lo, r0 + tile, False, lo - (qi - (BAND_TILES - 1)) * tile)]
    return ([(0, r0, False, 0)] if qi else []) + [(r0, r0 + tile, True, 0)]


def _scores(kind, qh, k_ref, aux_ref, h, seg, tile, scale):
    a, b, diag, c0 = seg
    s = _dot_nt(qh, k_ref[a:b, :]) * (scale * LOG2E)
    if kind == "B":
        return s + aux_ref[h, :, c0:BAND_W]
    if kind == "C":
        s = s - aux_ref[0, h, :, a:b]
    if diag:
        s = jnp.where(_tile_mask(kind, tile), s, NEG)
    return s


FWD_AHEAD = 2
BWD_AHEAD = 2


def _run_ahead(units, first, second, depth):
    queue = [first(*u) for u in units[:depth]]
    for i, (_, h) in enumerate(units):
        if i + depth < len(units):
            queue.append(first(*units[i + depth]))
        second(h, *queue.pop(0))


def _attn_fwd(kind, q, k, v, aux, nb, seq):
    dq = q.shape[1] // 3
    tile = T_BAND if kind == "B" else T_CAUSAL
    nq = seq // tile
    scale = _attn_scale(kind)

    def body(*refs):
        if kind == "A":
            q_ref, k_ref, v_ref, o_ref, lse_ref = refs
            aux_ref = None
        else:
            q_ref, k_ref, v_ref, aux_ref, o_ref, lse_ref = refs
        def logits(qi, h):
            rows = slice(qi * tile, (qi + 1) * tile)
            q2 = q_ref[rows, :]
            mq, _ = _head_masks(kind, tile, dq, h)
            qh = jnp.where(mq, q2, jnp.zeros_like(q2))
            segs = _segments(kind, qi, tile)
            return rows, segs, [_scores(kind, qh, k_ref, aux_ref, h, seg, tile, scale) for seg in segs]

        def finish(h, rows, segs, ss):
            _, mv = _head_masks(kind, tile, dq, h)
            mx = functools.reduce(jnp.maximum, [jnp.max(s, axis=-1, keepdims=True) for s in ss])
            ps = [jnp.exp2(s - mx) for s in ss]
            l = functools.reduce(jnp.add, [jnp.sum(p, axis=-1, keepdims=True) for p in ps])
            acc = functools.reduce(jnp.add, [_dot(p.astype(BF16), v_ref[seg[0]:seg[1], :]) for p, seg in zip(ps, segs)])
            o_h = jnp.where(mv, acc / l, 0.0)
            lse_h = jnp.where(mv, mx + jnp.log(l) * LOG2E, 0.0)
            if h == 0:
                o_ref[rows, :] = o_h
                lse_ref[rows, :] = lse_h
            else:
                o_ref[rows, :] += o_h
                lse_ref[rows, :] += lse_h

        _run_ahead([(qi, h) for qi in range(nq) for h in range(2)], logits, finish, FWD_AHEAD)

    def seq_spec(cols):
        return pl.BlockSpec((seq, cols), lambda b, p: (b, p))

    in_specs = [seq_spec(dq), seq_spec(dq), seq_spec(LANE)]
    args = [q, k, v]
    if kind == "B":
        in_specs.append(pl.BlockSpec((2, tile, BAND_W), lambda b, p: (p, 0, 0)))
        args.append(aux)
    if kind == "C":
        in_specs.append(pl.BlockSpec((1, 2, 1, seq), lambda b, p: (b, p, 0, 0)))
        args.append(aux)
    return pl.pallas_call(
        body, name="attn_fwd_" + kind, grid=(nb, 3), in_specs=in_specs, out_specs=[seq_spec(LANE), seq_spec(LANE)],
        out_shape=[jax.ShapeDtypeStruct((nb * seq, 384), F32)] * 2, compiler_params=_params(("parallel", "parallel")),
    )(*args)


def _attn_bwd(kind, q, k, v, o, do, lse, aux, nb, seq):
    dq = q.shape[1] // 3
    tile = T_BAND if kind == "B" else T_CAUSAL
    nq = seq // tile
    scale = _attn_scale(kind)
    dqk_dtype = F32 if kind == "A" else BF16

    def body(*refs):
        dfr_ref = dfq_ref = dbt_ref = aux_ref = None
        if kind == "A":
            q_ref, k_ref, v_ref, o_ref, do_ref, lse_ref, dq_ref, dk_ref, dv_ref, dkt_acc, dvt_acc = refs
        elif kind == "B":
            q_ref, k_ref, v_ref, o_ref, do_ref, lse_ref, aux_ref, dq_ref, dk_ref, dv_ref, dbt_ref, dkt_acc, dvt_acc = refs
        else:
            (q_ref, k_ref, v_ref, o_ref, do_ref, lse_ref, aux_ref, dq_ref, dk_ref, dv_ref, dfr_ref, dfq_ref,
             dkt_acc, dvt_acc) = refs
        dkt_acc[...] = jnp.zeros_like(dkt_acc)
        dvt_acc[...] = jnp.zeros_like(dvt_acc)
        if kind == "C":
            dfr_ref[...] = jnp.zeros_like(dfr_ref)
        if kind == "B":
            @pl.when(pl.program_id(1) == 0)
            def _():
                dbt_ref[...] = jnp.zeros_like(dbt_ref)

        def products(qi, h):
            rows = slice(qi * tile, (qi + 1) * tile)
            q2 = q_ref[rows, :]
            mq, mv = _head_masks(kind, tile, dq, h)
            qh = jnp.where(mq, q2, jnp.zeros_like(q2))
            doh = jnp.where(mv, do_ref[rows, :], 0.0)
            dob = doh.astype(BF16)
            segs = _segments(kind, qi, tile)
            ts = [_scores(kind, qh, k_ref, aux_ref, h, seg, tile, scale) for seg in segs]
            dps = [_dot_nt(dob, v_ref[seg[0]:seg[1], :]) for seg in segs]
            return rows, segs, qh, doh, ts, dps

        def finish(h, rows, segs, qh, doh, ts, dps):
            mq, mv = _head_masks(kind, tile, dq, h)
            qht = qh.astype(F32).T.astype(BF16)
            dobt = doh.T.astype(BF16)
            head_rows = [(64 * h, 64)] + ([(128 + 32 * h, 32)] if kind == "A" else [])
            delta = jnp.sum(doh * o_ref[rows, :], axis=-1, keepdims=True)
            lseh = jnp.max(jnp.where(mv, lse_ref[rows, :], NEG), axis=-1, keepdims=True)
            rs = jnp.zeros((tile, 1), F32)
            dq_h = jnp.zeros((tile, dq), F32)
            for (a, b, _, c0), t, dp in zip(segs, ts, dps):
                p = jnp.exp2(t - lseh)
                ds = p * (dp - delta)
                if kind == "B":
                    dbt_ref[h, :, c0:BAND_W] += ds
                if kind == "C":
                    dfr_ref[0, h, :, a:b] -= jnp.sum(ds, axis=0, keepdims=True)
                    rs = rs + jnp.sum(ds, axis=-1, keepdims=True)
                dss = (ds * scale).astype(BF16)
                dvt_acc[64 * h:64 * h + 64, a:b] += _dot(dobt[64 * h:64 * h + 64, :], p.astype(BF16))
                for r0, n in head_rows:
                    dkt_acc[r0:r0 + n, a:b] += _dot(qht[r0:r0 + n, :], dss)
                dq_h = dq_h + _dot(dss, k_ref[a:b, :])
            dq_h = jnp.where(mq, dq_h, 0.0).astype(dqk_dtype)
            if h == 0:
                dq_ref[rows, :] = dq_h
            else:
                dq_ref[rows, :] += dq_h
            if kind == "C":
                if h == 0:
                    dfq_ref[rows, :] = jnp.where(mv, rs, 0.0)
                else:
                    dfq_ref[rows, :] += jnp.where(mv, rs, 0.0)

        _run_ahead([(qi, h) for qi in range(nq) for h in range(2)], products, finish, BWD_AHEAD)
        for j in range(seq // 256):
            cols = slice(256 * j, 256 * (j + 1))
            dk_ref[cols, :] = dkt_acc[:, cols].T.astype(dqk_dtype)
            dv_ref[cols, :] = dvt_acc[:, cols].T.astype(BF16)

    def seq_spec(cols):
        return pl.BlockSpec((seq, cols), lambda p, b: (b, p))

    in_specs = [seq_spec(dq), seq_spec(dq), seq_spec(LANE), seq_spec(LANE), seq_spec(LANE), seq_spec(LANE)]
    args = [q, k, v, o, do, lse]
    out_specs = [seq_spec(dq), seq_spec(dq), seq_spec(LANE)]
    out_shape = [jax.ShapeDtypeStruct((nb * seq, 3 * dq), dqk_dtype)] * 2 + [jax.ShapeDtypeStruct((nb * seq, 384), BF16)]
    if kind == "B":
        spec = pl.BlockSpec((2, tile, BAND_W), lambda p, b: (p, 0, 0))
        in_specs.append(spec)
        args.append(aux)
        out_specs.append(spec)
        out_shape.append(jax.ShapeDtypeStruct((6, tile, BAND_W), F32))
    if kind == "C":
        spec = pl.BlockSpec((1, 2, 1, seq), lambda p, b: (b, p, 0, 0))
        in_specs.append(spec)
        args.append(aux)
        out_specs += [spec, seq_spec(LANE)]
        out_shape += [jax.ShapeDtypeStruct((nb, 6, 1, seq), F32), jax.ShapeDtypeStruct((nb * seq, 384), F32)]
    return pl.pallas_call(
        body, name="attn_bwd_" + kind, grid=(3, nb), in_specs=in_specs, out_specs=out_specs, out_shape=out_shape,
        scratch_shapes=[pltpu.VMEM((dq, seq), F32), pltpu.VMEM((LANE, seq), F32)],
        compiler_params=_params(("arbitrary", "arbitrary")),
    )(*args)


BIAS_G = 768
BIAS_EDGE = BIAS_G - N_REL


def _bias_line(rel_bias):
    g = jnp.concatenate([jnp.broadcast_to(rel_bias[:, N_REL - 1:], (rel_bias.shape[0], BIAS_EDGE)),
                         jnp.flip(rel_bias, axis=1)], axis=1)
    return jnp.pad(g, ((0, 8 - g.shape[0]), (0, 0)))


def _bias_unline(dg):
    return jnp.flip(dg[:, BIAS_EDGE:], axis=1)


def _bias_expand(g8):
    def body(g_ref, out_ref):
        line = jnp.broadcast_to(g_ref[0] * LOG2E, (T_BAND, BIAS_G))
        slab = pltpu.roll(line, 1, 1, stride=1, stride_axis=0)[:, LANE:BIAS_G]
        row = lax.broadcasted_iota(jnp.int32, (T_BAND, BAND_W), 0)
        col = lax.broadcasted_iota(jnp.int32, (T_BAND, BAND_W), 1)
        hidden = ((row >= 64) & (col < 64)) | ((row < 64) & (col >= BAND_W - 64))
        out_ref[0] = jnp.where(hidden, NEG, slab)

    return pl.pallas_call(
        body, name="bias_expand", grid=(8,), in_specs=[pl.BlockSpec((1, 1, BIAS_G), lambda h: (h, 0, 0))],
        out_specs=pl.BlockSpec((1, T_BAND, BAND_W), lambda h: (h, 0, 0)),
        out_shape=jax.ShapeDtypeStruct((8, T_BAND, BAND_W), F32), compiler_params=_params(("parallel",)),
    )(g8.reshape(8, 1, BIAS_G))


def _bias_reduce(d_slab):
    def body(d_ref, out_ref):
        r = lax.broadcasted_iota(jnp.int32, (T_BAND, T_BAND), 0)
        k = lax.broadcasted_iota(jnp.int32, (T_BAND, T_BAND), 1)
        flip = jnp.where(r + k == T_BAND - 1, 1.0, 0.0).astype(BF16)
        hi, mid, lo = _split3(d_ref[0])
        d_rev = _dot(flip, hi) + _dot(flip, mid) + _dot(flip, lo)
        wide = jnp.concatenate([jnp.zeros((T_BAND, LANE), F32), d_rev, jnp.zeros((T_BAND, 2 * LANE), F32)], axis=1)
        skew = pltpu.roll(wide, 0, 1, stride=1, stride_axis=0)
        dg = jnp.sum(skew, axis=0, keepdims=True)[:, LANE:LANE + BIAS_G]
        lane = _lane_iota((1, BIAS_G))
        clipped = jnp.sum(jnp.where(lane <= BIAS_EDGE, dg, 0.0), axis=1, keepdims=True)
        out_ref[0] = jnp.where(lane == BIAS_EDGE, clipped, dg)

    return pl.pallas_call(
        body, name="bias_reduce", grid=(8,), in_specs=[pl.BlockSpec((1, T_BAND, BAND_W), lambda h: (h, 0, 0))],
        out_specs=pl.BlockSpec((1, 1, BIAS_G), lambda h: (h, 0, 0)),
        out_shape=jax.ShapeDtypeStruct((8, 1, BIAS_G), F32), compiler_params=_params(("parallel",)),
    )(d_slab).reshape(8, BIAS_G)


def _outproj_bwd(dxn, y, gate, oa, ob, oc, zg, w_out_p, w_out_pt, nb, seq):
    t = dxn.shape[0]
    tpe = seq // TM

    def body(dxn_ref, y_ref, gate_ref, oa_ref, ob_ref, oc_ref, zg_ref, w_ref, wt_ref,
             doa_ref, dob_ref, doc_ref, dzg_ref, gw_ref, dgate_ref):
        i = pl.program_id(0)

        @pl.when(i == 0)
        def _():
            gw_ref[...] = jnp.zeros_like(gw_ref)

        @pl.when(i % tpe == 0)
        def _():
            dgate_ref[...] = jnp.zeros_like(dgate_ref)

        dxn_t = dxn_ref[...]
        dgate_ref[0] += jnp.sum(dxn_t * y_ref[...], axis=0, keepdims=True)
        dy = (dxn_t * gate_ref[0]).astype(BF16)
        for gi, (o_ref, do_ref) in enumerate(((oa_ref, doa_ref), (ob_ref, dob_ref), (oc_ref, doc_ref))):
            cols = slice(384 * gi, 384 * (gi + 1))
            u = zg_ref[:, cols]
            o_t = o_ref[...]
            su = _silu(u)
            dcat = _dot(dy, wt_ref[:, cols])
            do_ref[...] = dcat * su
            dzg_ref[:, cols] = (dcat * o_t * _dsilu(u)).astype(BF16)
            gw_ref[cols, :] += _dot_tn((o_t * su).astype(BF16), dy)

    return pl.pallas_call(
        body, name="outproj_bwd", grid=(t // TM,),
        in_specs=[_row_spec(D_MODEL), _row_spec(D_MODEL), _ex_spec(tpe), _row_spec(384), _row_spec(384), _row_spec(384),
                  _row_spec(D_CAT), _full_spec((D_CAT, D_MODEL)), _full_spec((D_MODEL, D_CAT))],
        out_specs=[_row_spec(384), _row_spec(384), _row_spec(384), _row_spec(D_CAT), _full_spec((D_CAT, D_MODEL)),
                   _ex_spec(tpe)],
        out_shape=[jax.ShapeDtypeStruct((t, 384), F32)] * 3 + [jax.ShapeDtypeStruct((t, D_CAT), BF16),
                                                                jax.ShapeDtypeStruct((D_CAT, D_MODEL), F32),
                                                                jax.ShapeDtypeStruct((nb, 1, D_MODEL), F32)],
        compiler_params=_params(("arbitrary",)),
    )(dxn, y, gate, oa, ob, oc, zg, w_out_p, w_out_pt)


def _a_up_bwd(dqa, dka, dva, za, gq, gkv, w_uq_pt, w_ukv_pt, cos, sina, sinb):
    t = za.shape[0]

    def body(dq_ref, dk_ref, dv_ref, za_ref, gq_ref, gkv_ref, wqt_ref, wkvt_ref, cos_ref, sa_ref, sb_ref,
             dza_ref, gwq_ref, gwkv_ref, ggq_ref, ggkv_ref, dqb, dkvb):
        @pl.when(pl.program_id(0) == 0)
        def _():
            gwq_ref[...] = jnp.zeros_like(gwq_ref)
            gwkv_ref[...] = jnp.zeros_like(gwkv_ref)
            ggq_ref[...] = jnp.zeros_like(ggq_ref)
            ggkv_ref[...] = jnp.zeros_like(ggkv_ref)

        cos_t, sa, sb = cos_ref[...], sa_ref[...], sb_ref[...]
        dkpe = jnp.zeros((TM, LANE), F32)
        for p in range(3):
            dqb[:, 256 * p:256 * p + 128] = dq_ref[:, 256 * p:256 * p + 128].astype(BF16)
            dqb[:, 256 * p + 128:256 * p + 256] = _rope_t(dq_ref[:, 256 * p + 128:256 * p + 256], cos_t, sa, sb).astype(BF16)
            dkvb[:, 128 * p:128 * p + 128] = dk_ref[:, 256 * p:256 * p + 128].astype(BF16)
            dkpe = dkpe + dk_ref[:, 256 * p + 128:256 * p + 256]
        dkvb[:, 384:768] = dv_ref[...]
        dkpe = _rope_t(dkpe, cos_t, sa, sb)
        dkpe = jnp.where(_lane_iota((TM, LANE)) < A_ROPE, dkpe + pltpu.roll(dkpe, LANE - 32, 1), 0.0)

        gqv = gq_ref[...]
        cqn, cqh, rq = _rms(za_ref[:, 0:384], gqv)
        dq_t = dqb[...]
        gwq_ref[...] += _dot_tn(cqn.astype(BF16), dq_t)
        dcqn = _dot(dq_t, wqt_ref[...])
        ggq_ref[...] += jnp.broadcast_to(jnp.sum(dcqn * cqh, axis=0, keepdims=True), ggq_ref.shape)
        dza_ref[:, 0:384] = _rms_bwd(dcqn, cqh, rq, gqv).astype(BF16)

        gkvv = gkv_ref[...]
        ckvn, ckvh, rkv = _rms(za_ref[:, 384:640], gkvv)
        dkv_t = dkvb[...]
        gwkv_ref[...] += _dot_tn(ckvn.astype(BF16), dkv_t)
        dckvn = _dot(dkv_t, wkvt_ref[...])
        ggkv_ref[...] += jnp.broadcast_to(jnp.sum(dckvn * ckvh, axis=0, keepdims=True), ggkv_ref.shape)
        dza_ref[:, 384:640] = _rms_bwd(dckvn, ckvh, rkv, gkvv).astype(BF16)
        dza_ref[:, 640:768] = dkpe.astype(BF16)

    return pl.pallas_call(
        body, name="a_up_bwd", grid=(t // TM,),
        in_specs=[_row_spec(768), _row_spec(768), _row_spec(384), _row_spec(768), _full_spec((1, 384)),
                  _full_spec((1, 256)), _full_spec((768, 384)), _full_spec((768, 256)), _row_spec(LANE), _row_spec(LANE),
                  _row_spec(LANE)],
        out_specs=[_row_spec(768), _full_spec((384, 768)), _full_spec((256, 768)), _full_spec((8, 384)),
                   _full_spec((8, 256))],
        out_shape=[jax.ShapeDtypeStruct((t, 768), BF16), jax.ShapeDtypeStruct((384, 768), F32),
                   jax.ShapeDtypeStruct((256, 768), F32), jax.ShapeDtypeStruct((8, 384), F32),
                   jax.ShapeDtypeStruct((8, 256), F32)],
        scratch_shapes=[pltpu.VMEM((TM, 768), BF16), pltpu.VMEM((TM, 768), BF16)],
        compiler_params=_params(("arbitrary",)),
    )(dqa, dka, dva, za, gq, gkv, w_uq_pt, w_ukv_pt, cos, sina, sinb)


def _dz_cols():
    return (Z_A, Z_G) + Z_QKV + (Z_F,)


def _inproj_bwd_dx(dz, dxn, x, shift, scale, g, w_in_pt, nb, seq):
    t = x.shape[0]
    tpe = seq // TM
    cols = _dz_cols()

    def body(*refs):
        dz_refs = refs[:len(cols)]
        dxn_ref, x_ref, sh_ref, sc_ref, g_ref, wt_ref, dx_ref, dsh_ref, dsc_ref, dg_ref = refs[len(cols):]
        i = pl.program_id(0)

        @pl.when(i == 0)
        def _():
            dg_ref[...] = jnp.zeros_like(dg_ref)

        @pl.when(i % tpe == 0)
        def _():
            dsh_ref[...] = jnp.zeros_like(dsh_ref)
            dsc_ref[...] = jnp.zeros_like(dsc_ref)

        dh = jnp.zeros((TM, D_MODEL), F32)
        for ref, (c0, c1) in zip(dz_refs, cols):
            dh = dh + _dot_nt(ref[...], wt_ref[:, c0:c1])
        gv = g_ref[...]
        n, xh, r = _rms(x_ref[...], gv)
        dsh_ref[0] += jnp.sum(dh, axis=0, keepdims=True)
        dsc_ref[0] += jnp.sum(dh * n, axis=0, keepdims=True)
        dn = dh * (1.0 + sc_ref[0])
        dg_ref[...] += jnp.broadcast_to(jnp.sum(dn * xh, axis=0, keepdims=True), dg_ref.shape)
        dx_ref[...] = dxn_ref[...] + _rms_bwd(dn, xh, r, gv)

    in_specs = [_row_spec(c1 - c0) for c0, c1 in cols]
    in_specs += [_row_spec(D_MODEL), _row_spec(D_MODEL), _ex_spec(tpe), _ex_spec(tpe), _full_spec((1, D_MODEL)),
                 _full_spec((D_MODEL, NP_IN))]
    return pl.pallas_call(
        body, name="inproj_bwd_dx", grid=(t // TM,), in_specs=in_specs,
        out_specs=[_row_spec(D_MODEL), _ex_spec(tpe), _ex_spec(tpe), _full_spec((8, D_MODEL))],
        out_shape=[jax.ShapeDtypeStruct((t, D_MODEL), F32), jax.ShapeDtypeStruct((nb, 1, D_MODEL), F32),
                   jax.ShapeDtypeStruct((nb, 1, D_MODEL), F32), jax.ShapeDtypeStruct((8, D_MODEL), F32)],
        compiler_params=_params(("arbitrary",)),
    )(*dz, dxn, x, shift, scale, g, w_in_pt)


def _inproj_bwd_dw(h, dz, name, layer, both=None):
    t = h.shape[0]
    widths = [d.shape[1] for d in dz]
    total = sum(widths)

    def body(*refs):
        h_ref = refs[0]
        dz_refs = refs[1:1 + len(dz)]
        gw_ref = refs[-1]

        @pl.when(pl.program_id(0) == 0)
        def _():
            gw_ref[...] = jnp.zeros_like(gw_ref)

        h_t = h_ref[...]
        c0 = 0
        for ref, w in zip(dz_refs, widths):
            gw_ref[0, :, c0:c0 + w] += _dot_tn(h_t, ref[...])
            c0 += w

    in_specs = [_row_spec(D_MODEL)] + [_row_spec(w) for w in widths]
    args = [h, *dz]
    aliases = {}
    if both is not None:
        in_specs.append(pl.BlockSpec(memory_space=pl.ANY))
        aliases = {len(args): 0}
        args.append(both)
    return pl.pallas_call(
        body, name=name, grid=(t // TM,), in_specs=in_specs,
        out_specs=pl.BlockSpec((1, D_MODEL, total), lambda i: (layer, 0, 0)),
        out_shape=jax.ShapeDtypeStruct((DEPTH, D_MODEL, total), F32), input_output_aliases=aliases,
        compiler_params=_params(("arbitrary",)),
    )(*args)


def _ada_bwd(c_all, dmod_cols, dmod_all):
    n = c_all.shape[0]
    cols = dmod_cols.shape[2]

    def body(c_ref, dc_ref, da_ref, gw_ref, gb_ref):
        act = _silu(c_ref[...]).astype(BF16)
        gw_ref[0] = _dot_tn(act, dc_ref[0].astype(BF16))
        gb_ref[0] = jnp.sum(da_ref[0], axis=0, keepdims=True)

    return pl.pallas_call(
        body, name="ada_bwd", grid=(DEPTH,),
        in_specs=[pl.BlockSpec((n, D_MODEL), lambda l: (0, 0)), pl.BlockSpec((1, n, cols), lambda l: (l, 0, 0)),
                  pl.BlockSpec((1, n, 3 * D_MODEL), lambda l: (l, 0, 0))],
        out_specs=[pl.BlockSpec((1, D_MODEL, cols), lambda l: (l, 0, 0)),
                   pl.BlockSpec((1, 1, 3 * D_MODEL), lambda l: (l, 0, 0))],
        out_shape=[jax.ShapeDtypeStruct((DEPTH, D_MODEL, cols), F32), jax.ShapeDtypeStruct((DEPTH, 1, 3 * D_MODEL), F32)],
        compiler_params=_params(("parallel",)),
    )(c_all, dmod_cols, dmod_all)


def _sum_blocks(parts, name):
    n, rows, cols = parts.shape
    tr = rows if rows <= 256 else 8 * next(d for d in range(32, 0, -1) if (rows // 8) % d == 0)

    def body(p_ref, out_ref):
        acc = p_ref[0].astype(F32)
        for k in range(1, n):
            acc = acc + p_ref[k].astype(F32)
        out_ref[...] = acc

    return pl.pallas_call(
        body, name=name, grid=(rows // tr,), in_specs=[pl.BlockSpec((n, tr, cols), lambda i: (0, i, 0))],
        out_specs=pl.BlockSpec((tr, cols), lambda i: (i, 0)), out_shape=jax.ShapeDtypeStruct((rows, cols), F32),
        compiler_params=_params(("parallel",)),
    )(parts)


def _adamw(w, g, m, v, name):
    nl, rows, cols = w.shape
    if rows * cols <= 64 * 1024:
        tr = rows
        tl = next(t for t in range(nl, 0, -1) if nl % t == 0 and t * max(rows, 8) * cols <= 512 * 1024)
    else:
        tl = 1
        tr = next(t for t in (rows, 256, 128, 64, 32, 16, 8) if rows % t == 0 and t * cols <= 256 * 1024)

    def body(w_ref, g_ref, m_ref, v_ref, d_ref, mo_ref, vo_ref):
        d_ref[...], mo_ref[...], vo_ref[...] = _adam_update(w_ref[...], g_ref[...], m_ref[...], v_ref[...])

    spec = pl.BlockSpec((tl, tr, cols), lambda l, i: (l, i, 0))
    return pl.pallas_call(
        body, name=name, grid=(nl // tl, rows // tr), in_specs=[spec] * 4, out_specs=[spec] * 3,
        out_shape=[jax.ShapeDtypeStruct((nl, rows, cols), F32)] * 3, compiler_params=_params(("parallel", "parallel")),
    )(w, g, m, v)


def _adam_update(w, gv, m, v):
    mn = ADAM_B1 * m + (1.0 - ADAM_B1) * gv
    vn = ADAM_B2 * v + (1.0 - ADAM_B2) * jnp.square(gv)
    m_hat = mn / (1.0 - ADAM_B1 ** ADAM_STEP)
    v_hat = vn / (1.0 - ADAM_B2 ** ADAM_STEP)
    return -ADAM_LR * (m_hat / (jnp.sqrt(v_hat) + ADAM_EPS) + ADAM_WD * w), mn, vn


def _adamw_small(ws, gs, ms, vs):
    n = len(ws)

    def body(*refs):
        ins, outs = refs[:4 * n], refs[4 * n:]
        for i in range(n):
            w_ref, g_ref, m_ref, v_ref = (ins[k * n + i] for k in range(4))
            outs[i][...], outs[n + i][...], outs[2 * n + i][...] = _adam_update(w_ref[...], g_ref[...], m_ref[...],
                                                                                v_ref[...])

    vmem = pl.BlockSpec(memory_space=pltpu.VMEM)
    res = pl.pallas_call(
        body, name="adamw_small", in_specs=[vmem] * (4 * n), out_specs=[vmem] * (3 * n),
        out_shape=[jax.ShapeDtypeStruct(w.shape, F32) for w in ws] * 3,
    )(*ws, *gs, *ms, *vs)
    return res[:n], res[n:2 * n], res[2 * n:]


ALL_FLIPS = tuple(range(1, N_DEV))


def _transfer(name, srcs, dst_shapes, plan, side=None):
    n_arr = len(srcs)
    probe = plan(0, 0, 0)
    n_steps = len(probe)
    side_in = [a for a, _ in side["inputs"]] if side else []
    side_out = side["outputs"] if side else []

    def body(*refs):
        src_refs, refs = refs[:n_arr], refs[n_arr:]
        side_in_refs, refs = refs[:len(side_in)], refs[len(side_in):]
        dst_refs, refs = refs[:n_arr], refs[n_arr:]
        side_out_refs, refs = refs[:len(side_out)], refs[len(side_out):]
        send_sems, recv_sems, local_sems = refs[:3]
        side_scratch = refs[3:]
        x, y, c = lax.axis_index("x"), lax.axis_index("y"), lax.axis_index("c")
        steps = plan(x, y, c)
        side_done = []

        def rows(ref, r0, n):
            return ref.at[:, pl.ds(r0, n), :]

        def arrival(t):
            a, _, _, n, _, f, _ = steps[t]
            return pltpu.make_async_remote_copy(
                src_ref=rows(dst_refs[a], 0, n), dst_ref=rows(dst_refs[a], 0, n), send_sem=send_sems.at[t],
                recv_sem=recv_sems.at[t], device_id=(x, y, c), device_id_type=pl.DeviceIdType.MESH)

        arrived, started = set(), []
        for t, (a, from_dst, sr, n, dr, f, after) in enumerate(steps):
            if side and t == side["at"]:
                after = tuple(after) + tuple(side["needs"])
            for u in after:
                if u not in arrived:
                    if steps[u][5] == 0:
                        started[u].wait()
                    else:
                        arrival(u).wait_recv()
                    arrived.add(u)
            if side and t == side["at"]:
                side_done.append(side["run"](x, y, c, dst_refs, side_in_refs, side_out_refs, side_scratch))
            src = rows(dst_refs[a] if from_dst else src_refs[a], sr, n)
            dst = rows(dst_refs[a], dr, n)
            if f == 0:
                cp = pltpu.make_async_copy(src, dst, local_sems.at[t])
            else:
                to = (1 - x if f & 4 else x, 1 - y if f & 2 else y, 1 - c if f & 1 else c)
                cp = pltpu.make_async_remote_copy(src_ref=src, dst_ref=dst, send_sem=send_sems.at[t],
                                                  recv_sem=recv_sems.at[t], device_id=to,
                                                  device_id_type=pl.DeviceIdType.MESH)
            cp.start()
            started.append(cp)
        for t, step in enumerate(steps):
            if step[5] != 0 and t not in arrived:
                arrival(t).wait_recv()
        for t, (cp, step) in enumerate(zip(started, steps)):
            if step[5] != 0:
                cp.wait_send()
            elif t not in arrived:
                cp.wait()
        for wait in side_done:
            wait()

    any_spec = pl.BlockSpec(memory_space=pl.ANY)
    return pl.pallas_call(
        body, name=name, out_shape=[jax.ShapeDtypeStruct(s, d) for s, d in list(dst_shapes) + list(side_out)],
        in_specs=[any_spec] * n_arr + [spec for _, spec in (side["inputs"] if side else [])],
        out_specs=[any_spec] * (n_arr + len(side_out)),
        scratch_shapes=[pltpu.SemaphoreType.DMA((n_steps,)), pltpu.SemaphoreType.DMA((n_steps,)),
                        pltpu.SemaphoreType.DMA((n_steps,))] + (side["scratch"] if side else []),
    )(*srcs, *side_in)


CHIP_FLIPS = (2, 4, 6)


def _gather_plan(chip_rows):
    def plan(x, y, c):
        steps = []
        chip = 2 * x + y
        for a, rc in enumerate(chip_rows):
            h = rc // 2
            first = (h // 32) * 16
            mine = rc * chip + h * c
            from_x, from_y, diag = rc * (chip ^ 2) + h * c, rc * (chip ^ 1) + h * c, rc * (chip ^ 3) + h * c
            steps.append((a, False, h * c, h, mine, 0, ()))
            to_x = len(steps)
            steps.append((a, False, h * c, h, mine, 4, ()))
            to_y = len(steps)
            steps.append((a, False, h * c, h, mine, 2, ()))
            fwd_y = len(steps)
            steps.append((a, True, from_x, first, from_x, 2, (to_x,)))
            fwd_x = len(steps)
            steps.append((a, True, from_y + first, h - first, from_y + first, 4, (to_y,)))
            steps.append((a, False, h * c, h, mine, 1, ()))
            steps.append((a, True, from_x, h, from_x, 1, (to_x,)))
            steps.append((a, True, from_y, h, from_y, 1, (to_y,)))
            steps.append((a, True, diag, first, diag, 1, (fwd_y,)))
            steps.append((a, True, diag + first, h - first, diag + first, 1, (fwd_x,)))
        return steps
    return plan


def _pair_reduce_plan(chip_rows):
    def plan(x, y, c):
        steps = []
        for a, rc in enumerate(chip_rows):
            h = rc // 2
            for j in range(4):
                steps.append((a, False, rc * j + h * (1 - c), h, h * j, 1, ()))
        return steps
    return plan


def _chip_scatter_plan(chip_rows):
    def plan(x, y, c):
        steps = []
        for a, rc in enumerate(chip_rows):
            h = rc // 2
            mine = h * (2 * x + y)
            arrivals = []
            for k, f in enumerate(CHIP_FLIPS):
                arrivals.append(len(steps))
                steps.append((a, False, h * ((2 * x + y) ^ (f >> 1)), h, h * k, f, ()))
            steps.append((a, False, mine, h, 3 * h, 0, ()))
            steps.append((a, False, mine, h, 7 * h, 1, ()))
            for k, t in enumerate(arrivals):
                steps.append((a, True, h * k, h, (4 + k) * h, 1, (t,)))
        return steps
    return plan


def _tile_rows(h):
    return h


def _sum_pair(partial, recv, core, rc, name):
    nl, _, cols = partial.shape
    h = rc // 2
    tr = _tile_rows(h)

    def body(c_ref, p_ref, r_ref, out_ref):
        out_ref[...] = (p_ref[...] + r_ref[...]).astype(BF16)

    spec = pl.BlockSpec((1, tr, cols), lambda l, j, i, c_ref: (l, (h // tr) * j + i, 0))
    return pl.pallas_call(
        body, name=name, out_shape=jax.ShapeDtypeStruct((nl, 4 * h, cols), BF16),
        grid_spec=pltpu.PrefetchScalarGridSpec(
            num_scalar_prefetch=1, grid=(nl, 4, h // tr),
            in_specs=[pl.BlockSpec((1, tr, cols), lambda l, j, i, c_ref: (l, (rc // tr) * j + (h // tr) * c_ref[0] + i, 0)),
                      spec],
            out_specs=spec),
        compiler_params=_params(("parallel", "parallel", "parallel")),
    )(core, partial, recv)


def _sum_chips(slots, core, rc, name):
    nl, _, cols = slots.shape
    h = rc // 2

    def body(c_ref, s_ref, out_ref):
        acc = s_ref[0, 3].astype(F32)
        for k in range(3):
            acc = acc + s_ref[0, k].astype(F32)
        out_ref[0] = acc

    return pl.pallas_call(
        body, name=name, out_shape=jax.ShapeDtypeStruct((nl, rc, cols), F32),
        grid_spec=pltpu.PrefetchScalarGridSpec(
            num_scalar_prefetch=1, grid=(nl, 2),
            in_specs=[pl.BlockSpec((1, 4, h, cols), lambda l, q, c_ref: (l, q, 0, 0))],
            out_specs=pl.BlockSpec((1, h, cols), lambda l, q, c_ref: (l, q + c_ref[0] - 2 * q * c_ref[0], 0))),
        compiler_params=_params(("parallel", "parallel")),
    )(core, slots.reshape(nl, 8, h, cols))


def _pad_cols(a, n):
    return a if n == 0 else jnp.pad(a, ((0, 0), (0, n)))


def _in_to_padded(w):
    return jnp.concatenate([_pad_cols(w[:, a:b], z) for a, b, z in IN_PIECES], axis=1)


def _in_cols_first(ga, gb):
    split = Z_G[1]
    pos, out = 0, {}
    for a, b, z in IN_PIECES:
        src, off = (ga, pos) if pos < split else (gb, pos - split)
        out[a] = jnp.transpose(src[:, :, off:off + (b - a)], (2, 0, 1))
        pos += (b - a) + z
    return jnp.concatenate([out[a] for a in sorted(out)], axis=0)


def _out_to_padded(w):
    z = jnp.zeros((64, w.shape[1]), w.dtype)
    return jnp.concatenate([w[0:384], w[384:704], z, w[704:1024], z], axis=0)


def _out_from_padded(gp):
    return jnp.concatenate([gp[0:384], gp[384:704], gp[768:1088]], axis=0)


def _uq_to_padded(w):
    parts = []
    for p in range(3):
        h0, h1 = 2 * p, 2 * p + 1
        parts += [w[:, 96 * h0:96 * h0 + 64], w[:, 96 * h1:96 * h1 + 64], w[:, 96 * h0 + 64:96 * h0 + 96],
                  w[:, 96 * h1 + 64:96 * h1 + 96], jnp.zeros((w.shape[0], 64), w.dtype)]
    return jnp.concatenate(parts, axis=1)


def _uq_from_padded(gp):
    parts = []
    for h in range(6):
        p, s = h // 2, h % 2
        parts += [gp[:, 256 * p + 64 * s:256 * p + 64 * s + 64], gp[:, 256 * p + 128 + 32 * s:256 * p + 160 + 32 * s]]
    return jnp.concatenate(parts, axis=1)


def _ukv_to_padded(w):
    return jnp.concatenate([w[:, 128 * h:128 * h + 64] for h in range(6)]
                           + [w[:, 128 * h + 64:128 * h + 128] for h in range(6)], axis=1)


def _ukv_from_padded(gp):
    parts = []
    for h in range(6):
        parts += [gp[:, 64 * h:64 * h + 64], gp[:, 384 + 64 * h:384 + 64 * h + 64]]
    return jnp.concatenate(parts, axis=1)


LR_ROWS = 224
SHARD_ROWS = (256, 256, 256, LR_ROWS)


def _pack_lowrank(w_uq, w_ukv):
    flat = jnp.concatenate([w_uq.reshape(-1), w_ukv.reshape(-1)])
    return jnp.pad(flat, (0, LR_ROWS * PACK_COLS - flat.shape[0])).reshape(1, LR_ROWS, PACK_COLS)


def _unpack_lowrank(packed):
    flat = packed.reshape(-1)
    n_uq = DEPTH * A_Q_RANK * 144
    n_ukv = DEPTH * A_KV_RANK * 192
    return flat[0:n_uq].reshape(DEPTH, A_Q_RANK, 144), flat[n_uq:n_uq + n_ukv].reshape(DEPTH, A_KV_RANK, 192)


def _rope_tables(positions):
    t = positions.size
    inv = ROPE_THETA ** (-jnp.arange(0, A_ROPE, 2, dtype=F32) / A_ROPE)
    inv_row = jnp.pad(jnp.tile(inv, 4), (0, 64)).reshape(1, LANE)

    def body(p_ref, i_ref, c_ref, sa_ref, sb_ref):
        ang = p_ref[...].astype(F32) * i_ref[...]
        lane = _lane_iota((TM, LANE))
        live = lane < 64
        second = (lane & 31) >= 16
        s = jnp.sin(ang)
        c_ref[...] = jnp.where(live, jnp.cos(ang), 0.0)
        sa_ref[...] = jnp.where(live & jnp.logical_not(second), -s, 0.0)
        sb_ref[...] = jnp.where(live & second, s, 0.0)

    return pl.pallas_call(
        body, name="rope_tables", grid=(t // TM,), in_specs=[_row_spec(1), _full_spec((1, LANE))],
        out_specs=[_row_spec(LANE)] * 3, out_shape=[jax.ShapeDtypeStruct((t, LANE), F32)] * 3,
        compiler_params=_params(("parallel",)),
    )(positions.reshape(t, 1), inv_row)


def _rows(a, n):
    flat = a.reshape(-1)
    return jnp.pad(flat, (0, n * LANE - flat.shape[0])).reshape(n, LANE)


def _forward_backward(x, mod, tables, target, weights, small, nb, seq):
    cos, sina, sinb = tables
    saved = []
    for l in range(DEPTH):
        w, s = weights[l], small[l]
        shift = mod[l][:, None, 0:D_MODEL]
        scale = mod[l][:, None, D_MODEL:2 * D_MODEL]
        gate = mod[l][:, None, 2 * D_MODEL:]
        h, za, zg, qb, kb, vb, qc, kc, vc, zf = _ln_inproj(x, shift, scale, s["norm_g"], w["in"], seq)
        qa, ka, va = _a_up(za, s["gq"], s["gkv"], w["uq"], w["ukv"], cos, sina, sinb)
        bias = _bias_expand(s["g8"])[0:6]
        f = _forget_fwd(zf, s["fb"], nb, seq)
        frow = jnp.pad(f[:, 0:5].reshape(nb, seq, 5).transpose(0, 2, 1), ((0, 0), (0, 1), (0, 0)))
        frow = frow.reshape(nb, 6, 1, seq)
        oa, lse_a = _attn_fwd("A", qa, ka, va, None, nb, seq)
        ob, lse_b = _attn_fwd("B", qb, kb, vb, bias, nb, seq)
        oc, lse_c = _attn_fwd("C", qc, kc, vc, frow, nb, seq)
        if l + 1 < DEPTH:
            y, x_next = _gate_outproj(x, gate, oa, ob, oc, zg, w["out"], seq)
        else:
            y, dx, loss8, gfinal8 = _gate_outproj_loss(x, gate, oa, ob, oc, zg, w["out"], target, s["final_g"], seq)
            x_next = None
        saved.append(dict(x=x, h=h, za=za, zg=zg, zf=zf, y=y, shift=shift, scale=scale, gate=gate, bias=bias, frow=frow,
                          a=(qa, ka, va, oa, lse_a), b=(qb, kb, vb, ob, lse_b), c=(qc, kc, vc, oc, lse_c)))
        x = x_next
    grads = []
    gw_in = (None, None)
    for l in reversed(range(DEPTH)):
        w, s, sv = weights[l], small[l], saved[l]
        qa, ka, va, oa, lse_a = sv["a"]
        qb, kb, vb, ob, lse_b = sv["b"]
        qc, kc, vc, oc, lse_c = sv["c"]
        doa, dob, doc, dzg, gw_out, dgate = _outproj_bwd(dx, sv["y"], sv["gate"], oa, ob, oc, sv["zg"], w["out"],
                                                          w["out_t"], nb, seq)
        dqa, dka, dva = _attn_bwd("A", qa, ka, va, oa, doa, lse_a, None, nb, seq)
        dqb, dkb, dvb, dbt = _attn_bwd("B", qb, kb, vb, ob, dob, lse_b, sv["bias"], nb, seq)
        dqc, dkc, dvc, dfr, dfq = _attn_bwd("C", qc, kc, vc, oc, doc, lse_c, sv["frow"], nb, seq)
        dg = _bias_reduce(jnp.pad(dbt, ((0, 2), (0, 0), (0, 0))))
        grb = jnp.pad(_bias_unline(dg), ((0, 0), (0, 384 - N_REL)))
        dfk = dfr.reshape(nb, 6, seq).transpose(0, 2, 1).reshape(nb * seq, 6)
        dzf, gfb = _forget_bwd(dfq, jnp.pad(dfk, ((0, 0), (0, LANE - 6))), sv["zf"], s["fb"], nb, seq)
        dza, gw_uq, gw_ukv, ggq, ggkv = _a_up_bwd(dqa, dka, dva, sv["za"], s["gq"], s["gkv"], w["uq_t"], w["ukv_t"],
                                                  cos, sina, sinb)
        dz = (dza, dzg, dqb, dkb, dvb, dqc, dkc, dvc, dzf)
        dx, dshift, dscale, gnorm = _inproj_bwd_dx(dz, dx, sv["x"], sv["shift"], sv["scale"], s["norm_g"], w["in"],
                                                   nb, seq)
        gw_in = (_inproj_bwd_dw(sv["h"], dz[0:2], "inproj_bwd_dw0", l, gw_in[0]),
                 _inproj_bwd_dw(sv["h"], dz[2:], "inproj_bwd_dw1", l, gw_in[1]))
        dmod = jnp.concatenate([dshift[:, 0], dscale[:, 0], dgate[:, 0]], axis=1)
        grads.append(dict(w_out=gw_out, w_uq=gw_uq, w_ukv=gw_ukv, dmod=dmod, norm_g=gnorm[0], gq=ggq[0],
                          gkv=ggkv[0], rb8=grb, fb=gfb[0]))
    grads.reverse()
    return loss8[0, 0], dx, grads, gfinal8[0], gw_in


def _layer_weights(w_in, w_out, w_uq, w_ukv):
    wi, wo, wq, wkv = _in_to_padded(w_in), _out_to_padded(w_out), _uq_to_padded(w_uq), _ukv_to_padded(w_ukv)
    return {"in": wi, "out": wo, "out_t": wo.T, "uq": wq, "uq_t": wq.T, "ukv": wkv, "ukv_t": wkv.T}


def _layer_small(norm_g, gq, gkv, rel_bias, forget_b, final_g):
    fb = jnp.pad(forget_b, (0, LANE - 5)).reshape(1, LANE)
    return dict(norm_g=norm_g.reshape(1, -1), gq=gq.reshape(1, -1), gkv=gkv.reshape(1, -1), g8=_bias_line(rel_bias), fb=fb,
                final_g=final_g.reshape(1, -1))


def _small_payload(per_layer, final_g, loss):
    def stack(key):
        return jnp.stack([p[key] for p in per_layer])

    def rows(a, rng):
        return _rows(a, rng[1] - rng[0])

    parts = [rows(stack("dmod"), PAY_DMOD), rows(stack("norm_g"), PAY_NORM), rows(stack("gq"), PAY_GQ),
             rows(stack("gkv"), PAY_GKV), rows(stack("rb8"), PAY_RB), rows(stack("fb"), PAY_FB),
             rows(final_g, PAY_FINAL), rows(loss, PAY_LOSS)]
    return jnp.concatenate(parts, axis=0)


def _payload_split(pay):
    def take(rng, shape):
        n = 1
        for d in shape:
            n *= d
        return pay[rng[0]:rng[1]].reshape(-1)[0:n].reshape(shape)

    norm_g = take(PAY_NORM, (DEPTH, D_MODEL))
    gq = take(PAY_GQ, (DEPTH, A_Q_RANK))
    gkv = take(PAY_GKV, (DEPTH, A_KV_RANK))
    rb = take(PAY_RB, (DEPTH, 8, 384))[:, 0:5, 0:N_REL]
    fb = take(PAY_FB, (DEPTH, LANE))[:, 0:5]
    final_g = take(PAY_FINAL, (D_MODEL,))
    return norm_g, gq, gkv, rb, fb, final_g


def kernel(x, c, positions, w_ada, b_ada, norm_g, w_in, a_q_norm_g, a_w_uq, a_kv_norm_g, a_w_ukv, b_rel_bias, c_forget_b, w_out, final_g, loss_target, m_w_ada, m_b_ada, m_norm_g, m_w_in, m_a_q_norm_g, m_a_w_uq, m_a_kv_norm_g, m_a_w_ukv, m_b_rel_bias, m_c_forget_b, m_w_out, m_final_g, v_w_ada, v_b_ada, v_norm_g, v_w_in, v_a_q_norm_g, v_a_w_uq, v_a_kv_norm_g, v_a_w_ukv, v_b_rel_bias, v_c_forget_b, v_w_out, v_final_g):
    nb, seq, _ = x.shape
    ix, iy, ic = lax.axis_index("x"), lax.axis_index("y"), lax.axis_index("c")
    chip = 2 * ix + iy

    weight_plan = _gather_plan((256, 256, LR_ROWS))
    cols = w_ada.shape[2]

    def gather_plan(x, y, c_):
        me_ = 4 * x + 2 * y + c_
        first = [(3, False, 0, 8, 8 * me_, f, ()) for f in range(N_DEV)]
        rest = weight_plan(x, y, c_)
        order = [i for i, s in enumerate(rest) if not s[6]] + [i for i, s in enumerate(rest) if s[6]]
        place = {old: len(first) + new for new, old in enumerate(order)}
        return first + [rest[i][:6] + (tuple(place[u] for u in rest[i][6]),) for i in order]

    probe = gather_plan(0, 0, 0)

    def project(x, y, c_, dst_refs, in_refs, out_refs, scratch):
        w_ref, b_ref = in_refs
        (mod_ref,) = out_refs
        c_vm, mod_vm, send_m, recv_m, local_m = scratch
        me_ = 4 * x + 2 * y + c_
        load = pltpu.make_async_copy(dst_refs[3].at[0], c_vm, local_m.at[0])
        load.start()
        load.wait()
        act = _silu(c_vm[...]).astype(BF16)
        for l in range(DEPTH):
            mod_vm[l] = _dot(act, w_ref[l].astype(BF16)) + b_ref[l]

        def rows_of(j):
            return mod_vm.at[:, pl.ds(pl.multiple_of(8 * j, 8), 8), :]

        def send(i, f, slot, j):
            to = (1 - x if f & 4 else x, 1 - y if f & 2 else y, 1 - c_ if f & 1 else c_)
            return pltpu.make_async_remote_copy(src_ref=rows_of(j), dst_ref=mod_ref.at[slot], send_sem=send_m.at[i],
                                                recv_sem=recv_m.at[i], device_id=to, device_id_type=pl.DeviceIdType.MESH)

        own = pltpu.make_async_copy(rows_of(me_), mod_ref.at[me_], local_m.at[1])
        own.start()
        sends = [send(i, f, me_, me_ ^ f) for i, f in enumerate(ALL_FLIPS)]
        for s in sends:
            s.start()

        def finish():
            for i, f in enumerate(ALL_FLIPS):
                send(i, f, me_ ^ f, 0).wait_recv()
            for s in sends:
                s.wait_send()
            own.wait()
        return finish

    b_cols = lax.dynamic_slice_in_dim(b_ada, chip * cols, cols, axis=1)[:, None, :]
    vmem = pl.BlockSpec(memory_space=pltpu.VMEM)
    side = dict(at=next(t for t, s in enumerate(probe) if s[6]), needs=tuple(range(N_DEV)),
                inputs=[(w_ada, vmem), (b_cols, vmem)], outputs=[((N_DEV, DEPTH, 8, cols), F32)],
                scratch=[pltpu.VMEM((8 * N_DEV, D_MODEL), F32), pltpu.VMEM((DEPTH, 8 * N_DEV, cols), F32),
                         pltpu.SemaphoreType.DMA((len(ALL_FLIPS),)), pltpu.SemaphoreType.DMA((len(ALL_FLIPS),)),
                         pltpu.SemaphoreType.DMA((2,))],
                run=project)
    full_in, full_out, full_lr, c_rows, mod_rows = _transfer(
        "gather_weights", [w_in.astype(BF16), w_out.astype(BF16), _pack_lowrank(a_w_uq, a_w_ukv).astype(BF16),
                           jnp.pad(c, ((0, 8 - nb), (0, 0)))[None]],
        [((DEPTH, D_MODEL, N_IN), BF16), ((DEPTH, D_MODEL, D_MODEL), BF16), ((1, 4 * LR_ROWS, PACK_COLS), BF16),
         ((1, 8 * N_DEV, D_MODEL), F32)], gather_plan, side)
    lowrank = [_unpack_lowrank(full_lr[0, LR_ROWS * j:LR_ROWS * (j + 1)]) for j in range(4)]
    full_uq = jnp.concatenate([s[0] for s in lowrank], axis=2)
    full_ukv = jnp.concatenate([s[1] for s in lowrank], axis=2)
    weights = [_layer_weights(full_in[l], full_out[l], full_uq[l], full_ukv[l]) for l in range(DEPTH)]
    small = [_layer_small(norm_g[l], a_q_norm_g[l], a_kv_norm_g[l], b_rel_bias[l], c_forget_b[l], final_g)
             for l in range(DEPTH)]

    c_all = c_rows.reshape(N_DEV, 8, D_MODEL)[:, 0:nb].reshape(N_DEV * nb, D_MODEL)
    mod = jnp.concatenate([mod_rows[2 * j][:, 0:nb] for j in range(4)], axis=2)

    tables = _rope_tables(positions)
    loss_part, dx, grads, gfinal, gw_in = _forward_backward(
        x.reshape(nb * seq, D_MODEL), mod, tables, loss_target.reshape(nb * seq, D_MODEL), weights, small, nb, seq)

    g_uq = jnp.stack([_uq_from_padded(g["w_uq"]) for g in grads])
    g_ukv = jnp.stack([_ukv_from_padded(g["w_ukv"]) for g in grads])
    g_lr = jnp.concatenate([_pack_lowrank(g_uq[:, :, 144 * j:144 * (j + 1)], g_ukv[:, :, 192 * j:192 * (j + 1)])
                            for j in range(4)], axis=1)
    partials = [gw_in[0], gw_in[1], jnp.stack([_out_from_padded(g["w_out"]) for g in grads]), g_lr]
    shapes = [(p.shape[0], p.shape[2]) for p in partials]
    halves = [r // 2 for r in SHARD_ROWS]
    core_s = jnp.reshape(ic, (1,)).astype(jnp.int32)
    pay = _small_payload(grads, gfinal, loss_part)
    pay_rows = pay.shape[0]
    reduce_plan = _pair_reduce_plan(SHARD_ROWS)

    def reduce_plan_with_payload(x, y, c_):
        me_ = 4 * x + 2 * y + c_
        return reduce_plan(x, y, c_) + [(len(partials), False, 0, pay_rows, pay_rows * me_, f, ()) for f in range(N_DEV)]

    *from_pair, pay_all = _transfer(
        "pair_reduce", partials + [pay[None]],
        [((nl, 4 * h, nc), F32) for (nl, nc), h in zip(shapes, halves)] + [((1, N_DEV * pay_rows, LANE), F32)],
        reduce_plan_with_payload)
    pay_all = pay_all.reshape(N_DEV, pay_rows, LANE)

    tot = _sum_blocks(pay_all, "sum_small")
    loss = tot[PAY_LOSS[0], 0]
    dmod_all = pay_all[:, PAY_DMOD[0]:PAY_DMOD[1]].reshape(N_DEV, -1)[:, 0:DEPTH * nb * 3 * D_MODEL]
    dmod_all = dmod_all.reshape(N_DEV, DEPTH, nb, 3 * D_MODEL).transpose(1, 0, 2, 3)
    dmod_all = dmod_all.reshape(DEPTH, N_DEV * nb, 3 * D_MODEL)
    my_cols = lax.dynamic_slice_in_dim(dmod_all, chip * cols, cols, axis=2)
    g_w_ada, g_b_ada = _ada_bwd(c_all, my_cols, dmod_all)
    g_b_ada = g_b_ada[:, 0]
    chip_sums = [_sum_pair(p, r, core_s, rc, "sum_pair%d" % i)
                 for i, (p, r, rc) in enumerate(zip(partials, from_pair, SHARD_ROWS))]
    slots = _transfer("chip_scatter", chip_sums, [((nl, 8 * h, nc), BF16) for (nl, nc), h in zip(shapes, halves)],
                      _chip_scatter_plan(SHARD_ROWS))
    g_in_a, g_in_b, g_out_sh, g_lr_sh = [_sum_chips(s, core_s, rc, "sum_chips%d" % i)
                                         for i, (s, rc) in enumerate(zip(slots, SHARD_ROWS))]
    g_uq_sh, g_ukv_sh = _unpack_lowrank(g_lr_sh[0])

    def cols_first(a):
        return jnp.transpose(a, (2, 0, 1))

    g_in_t = _in_cols_first(g_in_a, g_in_b)
    upd_in = tuple(jnp.transpose(a, (1, 2, 0)) for a in _adamw(cols_first(w_in), g_in_t, cols_first(m_w_in),
                                                               cols_first(v_w_in), "adamw_in"))
    gw = (jnp.transpose(g_in_t, (1, 2, 0)), g_out_sh, g_uq_sh, g_ukv_sh)
    upd = [upd_in, _adamw(w_out, gw[1], m_w_out, v_w_out, "adamw_out"),
           _adamw(a_w_uq, gw[2], m_a_w_uq, v_a_w_uq, "adamw_uq"), _adamw(a_w_ukv, gw[3], m_a_w_ukv, v_a_w_ukv, "adamw_ukv")]
    dw, mw, vw = (tuple(u[i] for u in upd) for i in range(3))
    d_ada, m_ada, v_ada = _adamw(w_ada, g_w_ada, m_w_ada, v_w_ada, "adamw_ada")

    gs = _payload_split(tot)
    small_w = (norm_g, a_q_norm_g, a_kv_norm_g, b_rel_bias, c_forget_b, final_g, b_ada)

    def view3(arrays):
        return [a.reshape((1,) * (3 - w.ndim) + w.shape) for a, w in zip(arrays, small_w)]

    small_upd = _adamw_small(
        view3(small_w), view3(gs + (g_b_ada,)),
        view3((m_norm_g, m_a_q_norm_g, m_a_kv_norm_g, m_b_rel_bias, m_c_forget_b, m_final_g, m_b_ada)),
        view3((v_norm_g, v_a_q_norm_g, v_a_kv_norm_g, v_b_rel_bias, v_c_forget_b, v_final_g, v_b_ada)))
    (ds, d_b), (ms, m_b), (vs, v_b) = ((tuple(a.reshape(w.shape) for a, w in zip(u[:-1], small_w)),
                                         u[-1].reshape(b_ada.shape)) for u in small_upd)

    def ordered(ada, b, sm, big):
        ng, gq, gkv, rb, fb, fg = sm
        b_in, b_out, b_uq, b_ukv = big
        return (ada, b, ng, b_in, gq, b_uq, gkv, b_ukv, rb, fb, b_out, fg)

    return (loss, dx.reshape(nb, seq, D_MODEL), *ordered(g_w_ada, g_b_ada, gs, gw), *ordered(d_ada, d_b, ds, dw),
            *ordered(m_ada, m_b, ms, mw), *ordered(v_ada, v_b, vs, vw))
```

```python
import functools

import jax
import jax.numpy as jnp
from jax import lax
from jax.experimental import pallas as pl
from jax.experimental.pallas import tpu as pltpu

F32 = jnp.float32
BF16 = jnp.bfloat16

D_MODEL = 1024
DEPTH = 2
EPS = 1e-6
NEG = -1e30
LOG2E = 1.4426950408889634
ROPE_THETA = 10000.0
A_ROPE = 32
A_Q_RANK = 384
A_KV_RANK = 256
REL_CLIP = 128
N_REL = 2 * REL_CLIP + 1
N_IN = 3621

ADAM_LR = 0.001
ADAM_B1 = 0.9
ADAM_B2 = 0.999
ADAM_EPS = 1e-08
ADAM_WD = 0.01
ADAM_STEP = 10

LANE = 128
VMEM_LIMIT = 56 * 1024 * 1024

NP_IN = 4352
Z_A = (0, 768)
Z_G = (768, 1920)
Z_QKV = tuple((1920 + 384 * i, 1920 + 384 * (i + 1)) for i in range(6))
Z_F = (4224, 4352)
IN_PIECES = ((0, 672, 96), (672, 1056, 0), (2016, 2336, 64), (3301, 3621, 64), (1056, 1376, 64), (1376, 1696, 64),
             (1696, 2016, 64), (2336, 2656, 64), (2656, 2976, 64), (2976, 3296, 64), (3296, 3301, 123))
D_CAT = 1152

TM = 512
T_CAUSAL = 256
T_BAND = 128
BAND_TILES = 5
N_DEV = 8

PAY_DMOD = (0, 96)
PAY_NORM = (96, 112)
PAY_GQ = (112, 120)
PAY_GKV = (120, 128)
PAY_RB = (128, 176)
PAY_FB = (176, 184)
PAY_FINAL = (184, 192)
PAY_LOSS = (192, 200)

PACK_COLS = 1024


def _params(sem=None):
    return pltpu.CompilerParams(dimension_semantics=sem, vmem_limit_bytes=VMEM_LIMIT)


def _lane_iota(shape):
    return lax.broadcasted_iota(jnp.int32, shape, len(shape) - 1)


def _silu(u):
    return u * jax.nn.sigmoid(u)


def _dsilu(u):
    s = jax.nn.sigmoid(u)
    return s * (1.0 + u * (1.0 - s))


def _rms(x, g):
    r = lax.rsqrt(jnp.mean(x * x, axis=-1, keepdims=True) + EPS)
    xh = x * r
    return xh * g, xh, r


def _rms_bwd(dy, xh, r, g):
    dxh = dy * g
    return r * (dxh - xh * jnp.mean(dxh * xh, axis=-1, keepdims=True))


def _rope(x, cos, sina, sinb):
    return x * cos + pltpu.roll(x, 16, 1) * sinb + pltpu.roll(x, LANE - 16, 1) * sina


def _rope_t(dy, cos, sina, sinb):
    return dy * cos + pltpu.roll(dy * sinb, LANE - 16, 1) + pltpu.roll(dy * sina, 16, 1)


def _split3(x):
    hi = x.astype(BF16)
    r1 = x - hi.astype(F32)
    mid = r1.astype(BF16)
    lo = (r1 - mid.astype(F32)).astype(BF16)
    return hi, mid, lo


def _dot(a, b):
    return jnp.dot(a, b, preferred_element_type=F32)


def _dot_nt(a, b):
    return lax.dot_general(a, b, (((1,), (1,)), ((), ())), preferred_element_type=F32)


def _dot_tn(a, b):
    return lax.dot_general(a, b, (((0,), (0,)), ((), ())), preferred_element_type=F32)


def _row_spec(cols):
    return pl.BlockSpec((TM, cols), lambda i: (i, 0))


def _full_spec(shape):
    return pl.BlockSpec(shape, lambda i: (0,) * len(shape))


def _ex_spec(tiles_per_ex):
    return pl.BlockSpec((1, 1, D_MODEL), lambda i: (i // tiles_per_ex, 0, 0))


def _ln_inproj(x, shift, scale, g, w_in_p, seq):
    t = x.shape[0]

    def body(x_ref, sh_ref, sc_ref, g_ref, w_ref, h_ref, za_ref, zg_ref, q0, q1, q2, q3, q4, q5, zf_ref):
        n, _, _ = _rms(x_ref[...], g_ref[...])
        h = (n * (1.0 + sc_ref[0]) + sh_ref[0]).astype(BF16)
        h_ref[...] = h
        za_ref[...] = _dot(h, w_ref[:, Z_A[0]:Z_A[1]])
        zg_ref[...] = _dot(h, w_ref[:, Z_G[0]:Z_G[1]])
        for ref, (c0, c1) in zip((q0, q1, q2, q3, q4, q5), Z_QKV):
            ref[...] = _dot(h, w_ref[:, c0:c1]).astype(BF16)
        zf_ref[...] = _dot(h, w_ref[:, Z_F[0]:Z_F[1]])

    tpe = seq // TM
    shapes = [jax.ShapeDtypeStruct((t, D_MODEL), BF16), jax.ShapeDtypeStruct((t, 768), F32),
              jax.ShapeDtypeStruct((t, D_CAT), F32)]
    shapes += [jax.ShapeDtypeStruct((t, 384), BF16)] * 6 + [jax.ShapeDtypeStruct((t, LANE), F32)]
    return pl.pallas_call(
        body, name="ln_inproj", grid=(t // TM,),
        in_specs=[_row_spec(D_MODEL), _ex_spec(tpe), _ex_spec(tpe), _full_spec((1, D_MODEL)),
                  _full_spec((D_MODEL, NP_IN))],
        out_specs=[_row_spec(D_MODEL), _row_spec(768), _row_spec(D_CAT)] + [_row_spec(384)] * 6 + [_row_spec(LANE)],
        out_shape=shapes, compiler_params=_params(("parallel",)),
    )(x, shift, scale, g, w_in_p)


def _a_up(za, gq, gkv, w_uq_p, w_ukv_p, cos, sina, sinb):
    t = za.shape[0]

    def body(za_ref, gq_ref, gkv_ref, wq_ref, wkv_ref, cos_ref, sa_ref, sb_ref, q_ref, k_ref, v_ref):
        cos_t, sa, sb = cos_ref[...], sa_ref[...], sb_ref[...]
        cqn, _, _ = _rms(za_ref[:, 0:384], gq_ref[...])
        q = _dot(cqn.astype(BF16), wq_ref[...])
        ckvn, _, _ = _rms(za_ref[:, 384:640], gkv_ref[...])
        kv = _dot(ckvn.astype(BF16), wkv_ref[...])
        kpe = za_ref[:, 640:768]
        kpe = _rope(kpe + pltpu.roll(kpe, 32, 1), cos_t, sa, sb).astype(BF16)
        for p in range(3):
            q_ref[:, 256 * p:256 * p + 128] = q[:, 256 * p:256 * p + 128].astype(BF16)
            q_ref[:, 256 * p + 128:256 * p + 256] = _rope(q[:, 256 * p + 128:256 * p + 256], cos_t, sa, sb).astype(BF16)
            k_ref[:, 256 * p:256 * p + 128] = kv[:, 128 * p:128 * p + 128].astype(BF16)
            k_ref[:, 256 * p + 128:256 * p + 256] = kpe
        v_ref[...] = kv[:, 384:768].astype(BF16)

    return pl.pallas_call(
        body, name="a_up", grid=(t // TM,),
        in_specs=[_row_spec(768), _full_spec((1, 384)), _full_spec((1, 256)), _full_spec((384, 768)),
                  _full_spec((256, 768)), _row_spec(LANE), _row_spec(LANE), _row_spec(LANE)],
        out_specs=[_row_spec(768), _row_spec(768), _row_spec(384)],
        out_shape=[jax.ShapeDtypeStruct((t, 768), BF16), jax.ShapeDtypeStruct((t, 768), BF16),
                   jax.ShapeDtypeStruct((t, 384), BF16)],
        compiler_params=_params(("parallel",)),
    )(za, gq, gkv, w_uq_p, w_ukv_p, cos, sina, sinb)


def _tri(n, upper):
    r = lax.broadcasted_iota(jnp.int32, (n, n), 0)
    c = lax.broadcasted_iota(jnp.int32, (n, n), 1)
    return jnp.where((c >= r) if upper else (c <= r), 1.0, 0.0).astype(BF16)


def _forget_fwd(zf, fb, nb, seq):
    blk = 256

    def body(zf_ref, fb_ref, f_ref):
        tri = _tri(blk, False)
        live = _lane_iota((blk, LANE)) < 5
        carry = jnp.zeros((1, LANE), F32)
        for i in range(seq // blk):
            u = zf_ref[i * blk:(i + 1) * blk, :] + fb_ref[...]
            lf = jnp.where(live, jnp.minimum(u, 0.0) - jnp.log(1.0 + jnp.exp(-jnp.abs(u))), 0.0)
            hi, mid, lo = _split3(lf)
            f_ref[i * blk:(i + 1) * blk, :] = (_dot(tri, hi) + _dot(tri, mid) + _dot(tri, lo) + carry) * LOG2E
            carry = carry + jnp.sum(lf, axis=0, keepdims=True)

    return pl.pallas_call(
        body, name="forget_fwd", grid=(nb,),
        in_specs=[pl.BlockSpec((seq, LANE), lambda b: (b, 0)), pl.BlockSpec((1, LANE), lambda b: (0, 0))],
        out_specs=pl.BlockSpec((seq, LANE), lambda b: (b, 0)),
        out_shape=jax.ShapeDtypeStruct((nb * seq, LANE), F32), compiler_params=_params(("parallel",)),
    )(zf, fb)


def _forget_bwd(dfq, dfk, zf, fb, nb, seq):
    blk = 256

    def body(dfq_ref, dfk_ref, zf_ref, fb_ref, dz_ref, gb_ref):
        @pl.when(pl.program_id(0) == 0)
        def _():
            gb_ref[...] = jnp.zeros_like(gb_ref)

        tri = _tri(blk, True)
        lane = _lane_iota((blk, LANE))
        wide = _lane_iota((blk, 384))
        live = lane < 5
        carry = jnp.zeros((1, LANE), F32)
        gsum = jnp.zeros((1, LANE), F32)
        for i in reversed(range(seq // blk)):
            d = dfk_ref[i * blk:(i + 1) * blk, :]
            dq = dfq_ref[i * blk:(i + 1) * blk, :]
            for hd in range(5):
                col = jnp.sum(jnp.where(wide == 64 * hd, dq, 0.0), axis=-1, keepdims=True)
                d = d + jnp.where(lane == hd, col, 0.0)
            hi, mid, lo = _split3(d)
            dlf = _dot(tri, hi) + _dot(tri, mid) + _dot(tri, lo) + carry
            carry = carry + jnp.sum(d, axis=0, keepdims=True)
            u = zf_ref[i * blk:(i + 1) * blk, :] + fb_ref[...]
            du = jnp.where(live, dlf * jax.nn.sigmoid(-u), 0.0)
            dz_ref[i * blk:(i + 1) * blk, :] = du.astype(BF16)
            gsum = gsum + jnp.sum(du, axis=0, keepdims=True)
        gb_ref[...] += jnp.broadcast_to(gsum, gb_ref.shape)

    return pl.pallas_call(
        body, name="forget_bwd", grid=(nb,),
        in_specs=[pl.BlockSpec((seq, 384), lambda b: (b, 0)), pl.BlockSpec((seq, LANE), lambda b: (b, 0)),
                  pl.BlockSpec((seq, LANE), lambda b: (b, 0)), pl.BlockSpec((1, LANE), lambda b: (0, 0))],
        out_specs=[pl.BlockSpec((seq, LANE), lambda b: (b, 0)), pl.BlockSpec((8, LANE), lambda b: (0, 0))],
        out_shape=[jax.ShapeDtypeStruct((nb * seq, LANE), BF16), jax.ShapeDtypeStruct((8, LANE), F32)],
        compiler_params=_params(("arbitrary",)),
    )(dfq, dfk, zf, fb)


def _gate_outproj(x, gate, oa, ob, oc, zg, w_out_p, seq):
    t = x.shape[0]

    def body(x_ref, gate_ref, oa_ref, ob_ref, oc_ref, zg_ref, w_ref, y_ref, xn_ref):
        y = jnp.zeros((TM, D_MODEL), F32)
        for i, o_ref in enumerate((oa_ref, ob_ref, oc_ref)):
            cat = (o_ref[...] * _silu(zg_ref[:, 384 * i:384 * (i + 1)])).astype(BF16)
            y = y + _dot(cat, w_ref[384 * i:384 * (i + 1), :])
        y_ref[...] = y
        xn_ref[...] = x_ref[...] + gate_ref[0] * y

    return pl.pallas_call(
        body, name="gate_outproj", grid=(t // TM,),
        in_specs=[_row_spec(D_MODEL), _ex_spec(seq // TM), _row_spec(384), _row_spec(384), _row_spec(384),
                  _row_spec(D_CAT), _full_spec((D_CAT, D_MODEL))],
        out_specs=[_row_spec(D_MODEL), _row_spec(D_MODEL)],
        out_shape=[jax.ShapeDtypeStruct((t, D_MODEL), F32)] * 2, compiler_params=_params(("parallel",)),
    )(x, gate, oa, ob, oc, zg, w_out_p)


def _gate_outproj_loss(x, gate, oa, ob, oc, zg, w_out_p, target, g, seq):
    t = x.shape[0]

    def body(x_ref, gate_ref, oa_ref, ob_ref, oc_ref, zg_ref, w_ref, t_ref, g_ref, y_ref, dx_ref, loss_ref, gg_ref):
        @pl.when(pl.program_id(0) == 0)
        def _():
            loss_ref[...] = jnp.zeros_like(loss_ref)
            gg_ref[...] = jnp.zeros_like(gg_ref)

        y = jnp.zeros((TM, D_MODEL), F32)
        for i, o_ref in enumerate((oa_ref, ob_ref, oc_ref)):
            cat = (o_ref[...] * _silu(zg_ref[:, 384 * i:384 * (i + 1)])).astype(BF16)
            y = y + _dot(cat, w_ref[384 * i:384 * (i + 1), :])
        y_ref[...] = y
        gv = g_ref[...]
        out, xh, r = _rms(x_ref[...] + gate_ref[0] * y, gv)
        err = out - t_ref[...]
        loss_ref[...] += 0.5 * jnp.sum(jnp.mean(err * err, axis=-1, keepdims=True), axis=0, keepdims=True)
        dout = err / D_MODEL
        gg_ref[...] += jnp.broadcast_to(jnp.sum(dout * xh, axis=0, keepdims=True), gg_ref.shape)
        dx_ref[...] = _rms_bwd(dout, xh, r, gv)

    return pl.pallas_call(
        body, name="gate_outproj_loss", grid=(t // TM,),
        in_specs=[_row_spec(D_MODEL), _ex_spec(seq // TM), _row_spec(384), _row_spec(384), _row_spec(384),
                  _row_spec(D_CAT), _full_spec((D_CAT, D_MODEL)), _row_spec(D_MODEL), _full_spec((1, D_MODEL))],
        out_specs=[_row_spec(D_MODEL), _row_spec(D_MODEL), _full_spec((8, LANE)), _full_spec((8, D_MODEL))],
        out_shape=[jax.ShapeDtypeStruct((t, D_MODEL), F32), jax.ShapeDtypeStruct((t, D_MODEL), F32),
                   jax.ShapeDtypeStruct((8, LANE), F32), jax.ShapeDtypeStruct((8, D_MODEL), F32)],
        compiler_params=_params(("arbitrary",)),
    )(x, gate, oa, ob, oc, zg, w_out_p, target, g)


def _head_masks(kind, rows, dq, h):
    lq = _lane_iota((rows, dq))
    lv = _lane_iota((rows, LANE))
    mq = (lq >= 64 * h) & (lq < 64 * h + 64)
    if kind == "A":
        mq = mq | ((lq >= 128 + 32 * h) & (lq < 160 + 32 * h))
    return mq, (lv >= 64 * h) & (lv < 64 * h + 64)


def _tile_mask(kind, tile):
    row = lax.broadcasted_iota(jnp.int32, (tile, tile), 0)
    col = lax.broadcasted_iota(jnp.int32, (tile, tile), 1)
    return (col >> 6) <= (row >> 6) if kind == "A" else col <= row


def _attn_scale(kind):
    return 96.0 ** -0.5 if kind == "A" else 0.125


BAND_W = BAND_TILES * T_BAND


def _segments(kind, qi, tile):
    r0 = qi * tile
    if kind == "B":
        lo = max(qi - (BAND_TILES - 1), 0) * tile
        return [(lo, r0 + tile, False, lo - (qi - (BAND_TILES - 1)) * tile)]
    return ([(0, r0, False, 0)] if qi else []) + [(r0, r0 + tile, True, 0)]


def _scores(kind, qh, k_ref, aux_ref, h, seg, tile, scale):
    a, b, diag, c0 = seg
    s = _dot_nt(qh, k_ref[a:b, :]) * (scale * LOG2E)
    if kind == "B":
        return s + aux_ref[h, :, c0:BAND_W]
    if kind == "C":
        s = s - aux_ref[0, h, :, a:b]
    if diag:
        s = jnp.where(_tile_mask(kind, tile), s, NEG)
    return s


FWD_AHEAD = 2
BWD_AHEAD = 2
KEY_CHUNK = {"A": 256, "B": 512, "C": 256}


def _run_ahead(units, first, second, depth):
    queue = [first(*u) for u in units[:depth]]
    for i, (_, h) in enumerate(units):
        if i + depth < len(units):
            queue.append(first(*units[i + depth]))
        second(h, *queue.pop(0))


def _attn_fwd(kind, q, k, v, aux, nb, seq):
    dq = q.shape[1] // 3
    tile = T_BAND if kind == "B" else T_CAUSAL
    nq = seq // tile
    scale = _attn_scale(kind)

    def body(*refs):
        if kind == "A":
            q_ref, k_ref, v_ref, o_ref, lse_ref = refs
            aux_ref = None
        else:
            q_ref, k_ref, v_ref, aux_ref, o_ref, lse_ref = refs
        def logits(qi, h):
            rows = slice(qi * tile, (qi + 1) * tile)
            q2 = q_ref[rows, :]
            mq, _ = _head_masks(kind, tile, dq, h)
            qh = jnp.where(mq, q2, jnp.zeros_like(q2))
            segs = _segments(kind, qi, tile)
            return rows, segs, [_scores(kind, qh, k_ref, aux_ref, h, seg, tile, scale) for seg in segs]

        def finish(h, rows, segs, ss):
            _, mv = _head_masks(kind, tile, dq, h)
            mx = functools.reduce(jnp.maximum, [jnp.max(s, axis=-1, keepdims=True) for s in ss])
            ps = [jnp.exp2(s - mx) for s in ss]
            l = functools.reduce(jnp.add, [jnp.sum(p, axis=-1, keepdims=True) for p in ps])
            acc = functools.reduce(jnp.add, [_dot(p.astype(BF16), v_ref[seg[0]:seg[1], :]) for p, seg in zip(ps, segs)])
            o_h = jnp.where(mv, acc / l, 0.0)
            lse_h = jnp.where(mv, mx + jnp.log(l) * LOG2E, 0.0)
            if h == 0:
                o_ref[rows, :] = o_h
                lse_ref[rows, :] = lse_h
            else:
                o_ref[rows, :] += o_h
                lse_ref[rows, :] += lse_h

        _run_ahead([(qi, h) for qi in range(nq) for h in range(2)], logits, finish, FWD_AHEAD)

    def seq_spec(cols):
        return pl.BlockSpec((seq, cols), lambda b, p: (b, p))

    in_specs = [seq_spec(dq), seq_spec(dq), seq_spec(LANE)]
    args = [q, k, v]
    if kind == "B":
        in_specs.append(pl.BlockSpec((2, tile, BAND_W), lambda b, p: (p, 0, 0)))
        args.append(aux)
    if kind == "C":
        in_specs.append(pl.BlockSpec((1, 2, 1, seq), lambda b, p: (b, p, 0, 0)))
        args.append(aux)
    return pl.pallas_call(
        body, name="attn_fwd_" + kind, grid=(nb, 3), in_specs=in_specs, out_specs=[seq_spec(LANE), seq_spec(LANE)],
        out_shape=[jax.ShapeDtypeStruct((nb * seq, 384), F32)] * 2, compiler_params=_params(("parallel", "parallel")),
    )(*args)


def _attn_bwd(kind, q, k, v, o, do, lse, aux, nb, seq):
    dq = q.shape[1] // 3
    tile = T_BAND if kind == "B" else T_CAUSAL
    nq = seq // tile
    scale = _attn_scale(kind)
    dqk_dtype = F32 if kind == "A" else BF16

    def body(*refs):
        dfr_ref = dfq_ref = dbt_ref = aux_ref = None
        if kind == "A":
            q_ref, k_ref, v_ref, o_ref, do_ref, lse_ref, dq_ref, dk_ref, dv_ref, dkt_acc, dvt_acc = refs
        elif kind == "B":
            q_ref, k_ref, v_ref, o_ref, do_ref, lse_ref, aux_ref, dq_ref, dk_ref, dv_ref, dbt_ref, dkt_acc, dvt_acc = refs
        else:
            (q_ref, k_ref, v_ref, o_ref, do_ref, lse_ref, aux_ref, dq_ref, dk_ref, dv_ref, dfr_ref, dfq_ref,
             dkt_acc, dvt_acc) = refs
        dkt_acc[...] = jnp.zeros_like(dkt_acc)
        dvt_acc[...] = jnp.zeros_like(dvt_acc)
        if kind == "C":
            dfr_ref[...] = jnp.zeros_like(dfr_ref)
        if kind == "B":
            @pl.when(pl.program_id(1) == 0)
            def _():
                dbt_ref[...] = jnp.zeros_like(dbt_ref)

        def products(qi, h):
            rows = slice(qi * tile, (qi + 1) * tile)
            q2 = q_ref[rows, :]
            mq, mv = _head_masks(kind, tile, dq, h)
            qh = jnp.where(mq, q2, jnp.zeros_like(q2))
            doh = jnp.where(mv, do_ref[rows, :], 0.0)
            dob = doh.astype(BF16)
            segs = _segments(kind, qi, tile)
            ts = [_scores(kind, qh, k_ref, aux_ref, h, seg, tile, scale) for seg in segs]
            dps = [_dot_nt(dob, v_ref[seg[0]:seg[1], :]) for seg in segs]
            return rows, segs, qh, doh, ts, dps

        def finish(h, rows, segs, qh, doh, ts, dps):
            mq, mv = _head_masks(kind, tile, dq, h)
            qht = qh.astype(F32).T.astype(BF16)
            dobt = doh.T.astype(BF16)
            head_rows = [(64 * h, 64)] + ([(128 + 32 * h, 32)] if kind == "A" else [])
            delta = jnp.sum(doh * o_ref[rows, :], axis=-1, keepdims=True)
            lseh = jnp.max(jnp.where(mv, lse_ref[rows, :], NEG), axis=-1, keepdims=True)
            rs = jnp.zeros((tile, 1), F32)
            dq_h = jnp.zeros((tile, dq), F32)
            kc = KEY_CHUNK[kind]
            chunks = [(a + k0, min(a + k0 + kc, b), c0 + k0, t[:, k0:k0 + kc], dp[:, k0:k0 + kc])
                      for (a, b, _, c0), t, dp in zip(segs, ts, dps) for k0 in range(0, b - a, kc)]
            for a, b, c0, t, dp in chunks:
                p = jnp.exp2(t - lseh)
                ds = p * (dp - delta)
                if kind == "B":
                    dbt_ref[h, :, c0:c0 + (b - a)] += ds
                if kind == "C":
                    dfr_ref[0, h, :, a:b] -= jnp.sum(ds, axis=0, keepdims=True)
                    rs = rs + jnp.sum(ds, axis=-1, keepdims=True)
                dss = (ds * scale).astype(BF16)
                dvt_acc[64 * h:64 * h + 64, a:b] += _dot(dobt[64 * h:64 * h + 64, :], p.astype(BF16))
                for r0, n in head_rows:
                    dkt_acc[r0:r0 + n, a:b] += _dot(qht[r0:r0 + n, :], dss)
                dq_h = dq_h + _dot(dss, k_ref[a:b, :])
            dq_h = jnp.where(mq, dq_h, 0.0).astype(dqk_dtype)
            if h == 0:
                dq_ref[rows, :] = dq_h
            else:
                dq_ref[rows, :] += dq_h
            if kind == "C":
                if h == 0:
                    dfq_ref[rows, :] = jnp.where(mv, rs, 0.0)
                else:
                    dfq_ref[rows, :] += jnp.where(mv, rs, 0.0)

        _run_ahead([(qi, h) for qi in range(nq) for h in range(2)], products, finish, BWD_AHEAD)
        for j in range(seq // 256):
            cols = slice(256 * j, 256 * (j + 1))
            dk_ref[cols, :] = dkt_acc[:, cols].T.astype(dqk_dtype)
            dv_ref[cols, :] = dvt_acc[:, cols].T.astype(BF16)

    def seq_spec(cols):
        return pl.BlockSpec((seq, cols), lambda p, b: (b, p))

    in_specs = [seq_spec(dq), seq_spec(dq), seq_spec(LANE), seq_spec(LANE), seq_spec(LANE), seq_spec(LANE)]
    args = [q, k, v, o, do, lse]
    out_specs = [seq_spec(dq), seq_spec(dq), seq_spec(LANE)]
    out_shape = [jax.ShapeDtypeStruct((nb * seq, 3 * dq), dqk_dtype)] * 2 + [jax.ShapeDtypeStruct((nb * seq, 384), BF16)]
    if kind == "B":
        spec = pl.BlockSpec((2, tile, BAND_W), lambda p, b: (p, 0, 0))
        in_specs.append(spec)
        args.append(aux)
        out_specs.append(spec)
        out_shape.append(jax.ShapeDtypeStruct((6, tile, BAND_W), F32))
    if kind == "C":
        spec = pl.BlockSpec((1, 2, 1, seq), lambda p, b: (b, p, 0, 0))
        in_specs.append(spec)
        args.append(aux)
        out_specs += [spec, seq_spec(LANE)]
        out_shape += [jax.ShapeDtypeStruct((nb, 6, 1, seq), F32), jax.ShapeDtypeStruct((nb * seq, 384), F32)]
    return pl.pallas_call(
        body, name="attn_bwd_" + kind, grid=(3, nb), in_specs=in_specs, out_specs=out_specs, out_shape=out_shape,
        scratch_shapes=[pltpu.VMEM((dq, seq), F32), pltpu.VMEM((LANE, seq), F32)],
        compiler_params=_params(("arbitrary", "arbitrary")),
    )(*args)


BIAS_G = 768
BIAS_EDGE = BIAS_G - N_REL


def _bias_line(rel_bias):
    g = jnp.concatenate([jnp.broadcast_to(rel_bias[:, N_REL - 1:], (rel_bias.shape[0], BIAS_EDGE)),
                         jnp.flip(rel_bias, axis=1)], axis=1)
    return jnp.pad(g, ((0, 8 - g.shape[0]), (0, 0)))


def _bias_unline(dg):
    return jnp.flip(dg[:, BIAS_EDGE:], axis=1)


def _bias_expand(g8):
    def body(g_ref, out_ref):
        line = jnp.broadcast_to(g_ref[0] * LOG2E, (T_BAND, BIAS_G))
        slab = pltpu.roll(line, 1, 1, stride=1, stride_axis=0)[:, LANE:BIAS_G]
        row = lax.broadcasted_iota(jnp.int32, (T_BAND, BAND_W), 0)
        col = lax.broadcasted_iota(jnp.int32, (T_BAND, BAND_W), 1)
        hidden = ((row >= 64) & (col < 64)) | ((row < 64) & (col >= BAND_W - 64))
        out_ref[0] = jnp.where(hidden, NEG, slab)

    return pl.pallas_call(
        body, name="bias_expand", grid=(8,), in_specs=[pl.BlockSpec((1, 1, BIAS_G), lambda h: (h, 0, 0))],
        out_specs=pl.BlockSpec((1, T_BAND, BAND_W), lambda h: (h, 0, 0)),
        out_shape=jax.ShapeDtypeStruct((8, T_BAND, BAND_W), F32), compiler_params=_params(("parallel",)),
    )(g8.reshape(8, 1, BIAS_G))


def _bias_reduce(d_slab):
    def body(d_ref, out_ref):
        r = lax.broadcasted_iota(jnp.int32, (T_BAND, T_BAND), 0)
        k = lax.broadcasted_iota(jnp.int32, (T_BAND, T_BAND), 1)
        flip = jnp.where(r + k == T_BAND - 1, 1.0, 0.0).astype(BF16)
        hi, mid, lo = _split3(d_ref[0])
        d_rev = _dot(flip, hi) + _dot(flip, mid) + _dot(flip, lo)
        wide = jnp.concatenate([jnp.zeros((T_BAND, LANE), F32), d_rev, jnp.zeros((T_BAND, 2 * LANE), F32)], axis=1)
        skew = pltpu.roll(wide, 0, 1, stride=1, stride_axis=0)
        dg = jnp.sum(skew, axis=0, keepdims=True)[:, LANE:LANE + BIAS_G]
        lane = _lane_iota((1, BIAS_G))
        clipped = jnp.sum(jnp.where(lane <= BIAS_EDGE, dg, 0.0), axis=1, keepdims=True)
        out_ref[0] = jnp.where(lane == BIAS_EDGE, clipped, dg)

    return pl.pallas_call(
        body, name="bias_reduce", grid=(8,), in_specs=[pl.BlockSpec((1, T_BAND, BAND_W), lambda h: (h, 0, 0))],
        out_specs=pl.BlockSpec((1, 1, BIAS_G), lambda h: (h, 0, 0)),
        out_shape=jax.ShapeDtypeStruct((8, 1, BIAS_G), F32), compiler_params=_params(("parallel",)),
    )(d_slab).reshape(8, BIAS_G)


def _outproj_bwd(dxn, y, gate, oa, ob, oc, zg, w_out_p, w_out_pt, nb, seq):
    t = dxn.shape[0]
    tpe = seq // TM

    def body(dxn_ref, y_ref, gate_ref, oa_ref, ob_ref, oc_ref, zg_ref, w_ref, wt_ref,
             doa_ref, dob_ref, doc_ref, dzg_ref, gw_ref, dgate_ref):
        i = pl.program_id(0)

        @pl.when(i == 0)
        def _():
            gw_ref[...] = jnp.zeros_like(gw_ref)

        @pl.when(i % tpe == 0)
        def _():
            dgate_ref[...] = jnp.zeros_like(dgate_ref)

        dxn_t = dxn_ref[...]
        dgate_ref[0] += jnp.sum(dxn_t * y_ref[...], axis=0, keepdims=True)
        dy = (dxn_t * gate_ref[0]).astype(BF16)
        for gi, (o_ref, do_ref) in enumerate(((oa_ref, doa_ref), (ob_ref, dob_ref), (oc_ref, doc_ref))):
            cols = slice(384 * gi, 384 * (gi + 1))
            u = zg_ref[:, cols]
            o_t = o_ref[...]
            su = _silu(u)
            dcat = _dot(dy, wt_ref[:, cols])
            do_ref[...] = dcat * su
            dzg_ref[:, cols] = (dcat * o_t * _dsilu(u)).astype(BF16)
            gw_ref[cols, :] += _dot_tn((o_t * su).astype(BF16), dy)

    return pl.pallas_call(
        body, name="outproj_bwd", grid=(t // TM,),
        in_specs=[_row_spec(D_MODEL), _row_spec(D_MODEL), _ex_spec(tpe), _row_spec(384), _row_spec(384), _row_spec(384),
                  _row_spec(D_CAT), _full_spec((D_CAT, D_MODEL)), _full_spec((D_MODEL, D_CAT))],
        out_specs=[_row_spec(384), _row_spec(384), _row_spec(384), _row_spec(D_CAT), _full_spec((D_CAT, D_MODEL)),
                   _ex_spec(tpe)],
        out_shape=[jax.ShapeDtypeStruct((t, 384), F32)] * 3 + [jax.ShapeDtypeStruct((t, D_CAT), BF16),
                                                                jax.ShapeDtypeStruct((D_CAT, D_MODEL), F32),
                                                                jax.ShapeDtypeStruct((nb, 1, D_MODEL), F32)],
        compiler_params=_params(("arbitrary",)),
    )(dxn, y, gate, oa, ob, oc, zg, w_out_p, w_out_pt)


def _a_up_bwd(dqa, dka, dva, za, gq, gkv, w_uq_pt, w_ukv_pt, cos, sina, sinb):
    t = za.shape[0]

    def body(dq_ref, dk_ref, dv_ref, za_ref, gq_ref, gkv_ref, wqt_ref, wkvt_ref, cos_ref, sa_ref, sb_ref,
             dza_ref, gwq_ref, gwkv_ref, ggq_ref, ggkv_ref, dqb, dkvb):
        @pl.when(pl.program_id(0) == 0)
        def _():
            gwq_ref[...] = jnp.zeros_like(gwq_ref)
            gwkv_ref[...] = jnp.zeros_like(gwkv_ref)
            ggq_ref[...] = jnp.zeros_like(ggq_ref)
            ggkv_ref[...] = jnp.zeros_like(ggkv_ref)

        cos_t, sa, sb = cos_ref[...], sa_ref[...], sb_ref[...]
        dkpe = jnp.zeros((TM, LANE), F32)
        for p in range(3):
            dqb[:, 256 * p:256 * p + 128] = dq_ref[:, 256 * p:256 * p + 128].astype(BF16)
            dqb[:, 256 * p + 128:256 * p + 256] = _rope_t(dq_ref[:, 256 * p + 128:256 * p + 256], cos_t, sa, sb).astype(BF16)
            dkvb[:, 128 * p:128 * p + 128] = dk_ref[:, 256 * p:256 * p + 128].astype(BF16)
            dkpe = dkpe + dk_ref[:, 256 * p + 128:256 * p + 256]
        dkvb[:, 384:768] = dv_ref[...]
        dkpe = _rope_t(dkpe, cos_t, sa, sb)
        dkpe = jnp.where(_lane_iota((TM, LANE)) < A_ROPE, dkpe + pltpu.roll(dkpe, LANE - 32, 1), 0.0)

        gqv = gq_ref[...]
        cqn, cqh, rq = _rms(za_ref[:, 0:384], gqv)
        dq_t = dqb[...]
        gwq_ref[...] += _dot_tn(cqn.astype(BF16), dq_t)
        dcqn = _dot(dq_t, wqt_ref[...])
        ggq_ref[...] += jnp.broadcast_to(jnp.sum(dcqn * cqh, axis=0, keepdims=True), ggq_ref.shape)
        dza_ref[:, 0:384] = _rms_bwd(dcqn, cqh, rq, gqv).astype(BF16)

        gkvv = gkv_ref[...]
        ckvn, ckvh, rkv = _rms(za_ref[:, 384:640], gkvv)
        dkv_t = dkvb[...]
        gwkv_ref[...] += _dot_tn(ckvn.astype(BF16), dkv_t)
        dckvn = _dot(dkv_t, wkvt_ref[...])
        ggkv_ref[...] += jnp.broadcast_to(jnp.sum(dckvn * ckvh, axis=0, keepdims=True), ggkv_ref.shape)
        dza_ref[:, 384:640] = _rms_bwd(dckvn, ckvh, rkv, gkvv).astype(BF16)
        dza_ref[:, 640:768] = dkpe.astype(BF16)

    return pl.pallas_call(
        body, name="a_up_bwd", grid=(t // TM,),
        in_specs=[_row_spec(768), _row_spec(768), _row_spec(384), _row_spec(768), _full_spec((1, 384)),
                  _full_spec((1, 256)), _full_spec((768, 384)), _full_spec((768, 256)), _row_spec(LANE), _row_spec(LANE),
                  _row_spec(LANE)],
        out_specs=[_row_spec(768), _full_spec((384, 768)), _full_spec((256, 768)), _full_spec((8, 384)),
                   _full_spec((8, 256))],
        out_shape=[jax.ShapeDtypeStruct((t, 768), BF16), jax.ShapeDtypeStruct((384, 768), F32),
                   jax.ShapeDtypeStruct((256, 768), F32), jax.ShapeDtypeStruct((8, 384), F32),
                   jax.ShapeDtypeStruct((8, 256), F32)],
        scratch_shapes=[pltpu.VMEM((TM, 768), BF16), pltpu.VMEM((TM, 768), BF16)],
        compiler_params=_params(("arbitrary",)),
    )(dqa, dka, dva, za, gq, gkv, w_uq_pt, w_ukv_pt, cos, sina, sinb)


def _dz_cols():
    return (Z_A, Z_G) + Z_QKV + (Z_F,)


def _inproj_bwd_dx(dz, dxn, x, shift, scale, g, w_in_pt, nb, seq):
    t = x.shape[0]
    tpe = seq // TM
    cols = _dz_cols()

    def body(*refs):
        dz_refs = refs[:len(cols)]
        dxn_ref, x_ref, sh_ref, sc_ref, g_ref, wt_ref, dx_ref, dsh_ref, dsc_ref, dg_ref = refs[len(cols):]
        i = pl.program_id(0)

        @pl.when(i == 0)
        def _():
            dg_ref[...] = jnp.zeros_like(dg_ref)

        @pl.when(i % tpe == 0)
        def _():
            dsh_ref[...] = jnp.zeros_like(dsh_ref)
            dsc_ref[...] = jnp.zeros_like(dsc_ref)

        dh = jnp.zeros((TM, D_MODEL), F32)
        for ref, (c0, c1) in zip(dz_refs, cols):
            dh = dh + _dot_nt(ref[...], wt_ref[:, c0:c1])
        gv = g_ref[...]
        n, xh, r = _rms(x_ref[...], gv)
        dsh_ref[0] += jnp.sum(dh, axis=0, keepdims=True)
        dsc_ref[0] += jnp.sum(dh * n, axis=0, keepdims=True)
        dn = dh * (1.0 + sc_ref[0])
        dg_ref[...] += jnp.broadcast_to(jnp.sum(dn * xh, axis=0, keepdims=True), dg_ref.shape)
        dx_ref[...] = dxn_ref[...] + _rms_bwd(dn, xh, r, gv)

    in_specs = [_row_spec(c1 - c0) for c0, c1 in cols]
    in_specs += [_row_spec(D_MODEL), _row_spec(D_MODEL), _ex_spec(tpe), _ex_spec(tpe), _full_spec((1, D_MODEL)),
                 _full_spec((D_MODEL, NP_IN))]
    return pl.pallas_call(
        body, name="inproj_bwd_dx", grid=(t // TM,), in_specs=in_specs,
        out_specs=[_row_spec(D_MODEL), _ex_spec(tpe), _ex_spec(tpe), _full_spec((8, D_MODEL))],
        out_shape=[jax.ShapeDtypeStruct((t, D_MODEL), F32), jax.ShapeDtypeStruct((nb, 1, D_MODEL), F32),
                   jax.ShapeDtypeStruct((nb, 1, D_MODEL), F32), jax.ShapeDtypeStruct((8, D_MODEL), F32)],
        compiler_params=_params(("arbitrary",)),
    )(*dz, dxn, x, shift, scale, g, w_in_pt)


def _inproj_bwd_dw(h, dz, name, layer, both=None):
    t = h.shape[0]
    widths = [d.shape[1] for d in dz]
    total = sum(widths)

    def body(*refs):
        h_ref = refs[0]
        dz_refs = refs[1:1 + len(dz)]
        gw_ref = refs[-1]

        @pl.when(pl.program_id(0) == 0)
        def _():
            gw_ref[...] = jnp.zeros_like(gw_ref)

        h_t = h_ref[...]
        c0 = 0
        for ref, w in zip(dz_refs, widths):
            gw_ref[0, :, c0:c0 + w] += _dot_tn(h_t, ref[...])
            c0 += w

    in_specs = [_row_spec(D_MODEL)] + [_row_spec(w) for w in widths]
    args = [h, *dz]
    aliases = {}
    if both is not None:
        in_specs.append(pl.BlockSpec(memory_space=pl.ANY))
        aliases = {len(args): 0}
        args.append(both)
    return pl.pallas_call(
        body, name=name, grid=(t // TM,), in_specs=in_specs,
        out_specs=pl.BlockSpec((1, D_MODEL, total), lambda i: (layer, 0, 0)),
        out_shape=jax.ShapeDtypeStruct((DEPTH, D_MODEL, total), F32), input_output_aliases=aliases,
        compiler_params=_params(("arbitrary",)),
    )(*args)


def _ada_bwd(c_all, dmod_cols, dmod_all):
    n = c_all.shape[0]
    cols = dmod_cols.shape[2]

    def body(c_ref, dc_ref, da_ref, gw_ref, gb_ref):
        act = _silu(c_ref[...]).astype(BF16)
        gw_ref[0] = _dot_tn(act, dc_ref[0].astype(BF16))
        gb_ref[0] = jnp.sum(da_ref[0], axis=0, keepdims=True)

    return pl.pallas_call(
        body, name="ada_bwd", grid=(DEPTH,),
        in_specs=[pl.BlockSpec((n, D_MODEL), lambda l: (0, 0)), pl.BlockSpec((1, n, cols), lambda l: (l, 0, 0)),
                  pl.BlockSpec((1, n, 3 * D_MODEL), lambda l: (l, 0, 0))],
        out_specs=[pl.BlockSpec((1, D_MODEL, cols), lambda l: (l, 0, 0)),
                   pl.BlockSpec((1, 1, 3 * D_MODEL), lambda l: (l, 0, 0))],
        out_shape=[jax.ShapeDtypeStruct((DEPTH, D_MODEL, cols), F32), jax.ShapeDtypeStruct((DEPTH, 1, 3 * D_MODEL), F32)],
        compiler_params=_params(("parallel",)),
    )(c_all, dmod_cols, dmod_all)


def _sum_blocks(parts, name):
    n, rows, cols = parts.shape
    tr = rows if rows <= 256 else 8 * next(d for d in range(32, 0, -1) if (rows // 8) % d == 0)

    def body(p_ref, out_ref):
        acc = p_ref[0].astype(F32)
        for k in range(1, n):
            acc = acc + p_ref[k].astype(F32)
        out_ref[...] = acc

    return pl.pallas_call(
        body, name=name, grid=(rows // tr,), in_specs=[pl.BlockSpec((n, tr, cols), lambda i: (0, i, 0))],
        out_specs=pl.BlockSpec((tr, cols), lambda i: (i, 0)), out_shape=jax.ShapeDtypeStruct((rows, cols), F32),
        compiler_params=_params(("parallel",)),
    )(parts)


def _adamw(w, g, m, v, name):
    nl, rows, cols = w.shape
    if rows * cols <= 64 * 1024:
        tr = rows
        tl = next(t for t in range(nl, 0, -1) if nl % t == 0 and t * max(rows, 8) * cols <= 512 * 1024)
    else:
        tl = 1
        tr = next(t for t in (rows, 256, 128, 64, 32, 16, 8) if rows % t == 0 and t * cols <= 256 * 1024)

    def body(w_ref, g_ref, m_ref, v_ref, d_ref, mo_ref, vo_ref):
        d_ref[...], mo_ref[...], vo_ref[...] = _adam_update(w_ref[...], g_ref[...], m_ref[...], v_ref[...])

    spec = pl.BlockSpec((tl, tr, cols), lambda l, i: (l, i, 0))
    return pl.pallas_call(
        body, name=name, grid=(nl // tl, rows // tr), in_specs=[spec] * 4, out_specs=[spec] * 3,
        out_shape=[jax.ShapeDtypeStruct((nl, rows, cols), F32)] * 3, compiler_params=_params(("parallel", "parallel")),
    )(w, g, m, v)


def _adam_update(w, gv, m, v):
    mn = ADAM_B1 * m + (1.0 - ADAM_B1) * gv
    vn = ADAM_B2 * v + (1.0 - ADAM_B2) * jnp.square(gv)
    m_hat = mn / (1.0 - ADAM_B1 ** ADAM_STEP)
    v_hat = vn / (1.0 - ADAM_B2 ** ADAM_STEP)
    return -ADAM_LR * (m_hat / (jnp.sqrt(v_hat) + ADAM_EPS) + ADAM_WD * w), mn, vn


def _adamw_small(ws, gs, ms, vs):
    n = len(ws)

    def body(*refs):
        ins, outs = refs[:4 * n], refs[4 * n:]
        for i in range(n):
            w_ref, g_ref, m_ref, v_ref = (ins[k * n + i] for k in range(4))
            outs[i][...], outs[n + i][...], outs[2 * n + i][...] = _adam_update(w_ref[...], g_ref[...], m_ref[...],
                                                                                v_ref[...])

    vmem = pl.BlockSpec(memory_space=pltpu.VMEM)
    res = pl.pallas_call(
        body, name="adamw_small", in_specs=[vmem] * (4 * n), out_specs=[vmem] * (3 * n),
        out_shape=[jax.ShapeDtypeStruct(w.shape, F32) for w in ws] * 3,
    )(*ws, *gs, *ms, *vs)
    return res[:n], res[n:2 * n], res[2 * n:]


ALL_FLIPS = tuple(range(1, N_DEV))


def _transfer(name, srcs, dst_shapes, plan, side=None):
    n_arr = len(srcs)
    probe = plan(0, 0, 0)
    n_steps = len(probe)
    side_in = [a for a, _ in side["inputs"]] if side else []
    side_out = side["outputs"] if side else []

    def body(*refs):
        src_refs, refs = refs[:n_arr], refs[n_arr:]
        side_in_refs, refs = refs[:len(side_in)], refs[len(side_in):]
        dst_refs, refs = refs[:n_arr], refs[n_arr:]
        side_out_refs, refs = refs[:len(side_out)], refs[len(side_out):]
        send_sems, recv_sems, local_sems = refs[:3]
        side_scratch = refs[3:]
        x, y, c = lax.axis_index("x"), lax.axis_index("y"), lax.axis_index("c")
        steps = plan(x, y, c)
        side_done = []

        def rows(ref, r0, n):
            return ref.at[:, pl.ds(r0, n), :]

        def arrival(t):
            a, _, _, n, _, f, _ = steps[t]
            return pltpu.make_async_remote_copy(
                src_ref=rows(dst_refs[a], 0, n), dst_ref=rows(dst_refs[a], 0, n), send_sem=send_sems.at[t],
                recv_sem=recv_sems.at[t], device_id=(x, y, c), device_id_type=pl.DeviceIdType.MESH)

        arrived, started = set(), []
        for t, (a, from_dst, sr, n, dr, f, after) in enumerate(steps):
            if side and t == side["at"]:
                after = tuple(after) + tuple(side["needs"])
            for u in after:
                if u not in arrived:
                    if steps[u][5] == 0:
                        started[u].wait()
                    else:
                        arrival(u).wait_recv()
                    arrived.add(u)
            if side and t == side["at"]:
                side_done.append(side["run"](x, y, c, dst_refs, side_in_refs, side_out_refs, side_scratch))
            src = rows(dst_refs[a] if from_dst else src_refs[a], sr, n)
            dst = rows(dst_refs[a], dr, n)
            if f == 0:
                cp = pltpu.make_async_copy(src, dst, local_sems.at[t])
            else:
                to = (1 - x if f & 4 else x, 1 - y if f & 2 else y, 1 - c if f & 1 else c)
                cp = pltpu.make_async_remote_copy(src_ref=src, dst_ref=dst, send_sem=send_sems.at[t],
                                                  recv_sem=recv_sems.at[t], device_id=to,
                                                  device_id_type=pl.DeviceIdType.MESH)
            cp.start()
            started.append(cp)
        for t, step in enumerate(steps):
            if step[5] != 0 and t not in arrived:
                arrival(t).wait_recv()
        for t, (cp, step) in enumerate(zip(started, steps)):
            if step[5] != 0:
                cp.wait_send()
            elif t not in arrived:
                cp.wait()
        for wait in side_done:
            wait()

    any_spec = pl.BlockSpec(memory_space=pl.ANY)
    return pl.pallas_call(
        body, name=name, out_shape=[jax.ShapeDtypeStruct(s, d) for s, d in list(dst_shapes) + list(side_out)],
        in_specs=[any_spec] * n_arr + [spec for _, spec in (side["inputs"] if side else [])],
        out_specs=[any_spec] * (n_arr + len(side_out)),
        scratch_shapes=[pltpu.SemaphoreType.DMA((n_steps,)), pltpu.SemaphoreType.DMA((n_steps,)),
                        pltpu.SemaphoreType.DMA((n_steps,))] + (side["scratch"] if side else []),
    )(*srcs, *side_in)


CHIP_FLIPS = (2, 4, 6)


def _gather_plan(chip_rows):
    def plan(x, y, c):
        steps = []
        chip = 2 * x + y
        for a, rc in enumerate(chip_rows):
            h = rc // 2
            first = (h // 32) * 16
            mine = rc * chip + h * c
            from_x, from_y, diag = rc * (chip ^ 2) + h * c, rc * (chip ^ 1) + h * c, rc * (chip ^ 3) + h * c
            steps.append((a, False, h * c, h, mine, 0, ()))
            to_x = len(steps)
            steps.append((a, False, h * c, h, mine, 4, ()))
            to_y = len(steps)
            steps.append((a, False, h * c, h, mine, 2, ()))
            fwd_y = len(steps)
            steps.append((a, True, from_x, first, from_x, 2, (to_x,)))
            fwd_x = len(steps)
            steps.append((a, True, from_y + first, h - first, from_y + first, 4, (to_y,)))
            steps.append((a, False, h * c, h, mine, 1, ()))
            steps.append((a, True, from_x, h, from_x, 1, (to_x,)))
            steps.append((a, True, from_y, h, from_y, 1, (to_y,)))
            steps.append((a, True, diag, first, diag, 1, (fwd_y,)))
            steps.append((a, True, diag + first, h - first, diag + first, 1, (fwd_x,)))
        return steps
    return plan


def _pair_reduce_plan(chip_rows):
    def plan(x, y, c):
        steps = []
        for a, rc in enumerate(chip_rows):
            h = rc // 2
            for j in range(4):
                steps.append((a, False, rc * j + h * (1 - c), h, h * j, 1, ()))
        return steps
    return plan


def _chip_scatter_plan(chip_rows):
    def plan(x, y, c):
        steps = []
        for a, rc in enumerate(chip_rows):
            h = rc // 2
            mine = h * (2 * x + y)
            arrivals = []
            for k, f in enumerate(CHIP_FLIPS):
                arrivals.append(len(steps))
                steps.append((a, False, h * ((2 * x + y) ^ (f >> 1)), h, h * k, f, ()))
            steps.append((a, False, mine, h, 3 * h, 0, ()))
            steps.append((a, False, mine, h, 7 * h, 1, ()))
            for k, t in enumerate(arrivals):
                steps.append((a, True, h * k, h, (4 + k) * h, 1, (t,)))
        return steps
    return plan


def _tile_rows(h):
    return h


def _sum_pair(partial, recv, core, rc, name):
    nl, _, cols = partial.shape
    h = rc // 2
    tr = _tile_rows(h)

    def body(c_ref, p_ref, r_ref, out_ref):
        out_ref[...] = (p_ref[...] + r_ref[...]).astype(BF16)

    spec = pl.BlockSpec((1, tr, cols), lambda l, j, i, c_ref: (l, (h // tr) * j + i, 0))
    return pl.pallas_call(
        body, name=name, out_shape=jax.ShapeDtypeStruct((nl, 4 * h, cols), BF16),
        grid_spec=pltpu.PrefetchScalarGridSpec(
            num_scalar_prefetch=1, grid=(nl, 4, h // tr),
            in_specs=[pl.BlockSpec((1, tr, cols), lambda l, j, i, c_ref: (l, (rc // tr) * j + (h // tr) * c_ref[0] + i, 0)),
                      spec],
            out_specs=spec),
        compiler_params=_params(("parallel", "parallel", "parallel")),
    )(core, partial, recv)


def _sum_chips(slots, core, rc, name):
    nl, _, cols = slots.shape
    h = rc // 2

    def body(c_ref, s_ref, out_ref):
        acc = s_ref[0, 3].astype(F32)
        for k in range(3):
            acc = acc + s_ref[0, k].astype(F32)
        out_ref[0] = acc

    return pl.pallas_call(
        body, name=name, out_shape=jax.ShapeDtypeStruct((nl, rc, cols), F32),
        grid_spec=pltpu.PrefetchScalarGridSpec(
            num_scalar_prefetch=1, grid=(nl, 2),
            in_specs=[pl.BlockSpec((1, 4, h, cols), lambda l, q, c_ref: (l, q, 0, 0))],
            out_specs=pl.BlockSpec((1, h, cols), lambda l, q, c_ref: (l, q + c_ref[0] - 2 * q * c_ref[0], 0))),
        compiler_params=_params(("parallel", "parallel")),
    )(core, slots.reshape(nl, 8, h, cols))


def _pad_cols(a, n):
    return a if n == 0 else jnp.pad(a, ((0, 0), (0, n)))


def _in_to_padded(w):
    return jnp.concatenate([_pad_cols(w[:, a:b], z) for a, b, z in IN_PIECES], axis=1)


def _in_cols_first(ga, gb):
    split = Z_G[1]
    pos, out = 0, {}
    for a, b, z in IN_PIECES:
        src, off = (ga, pos) if pos < split else (gb, pos - split)
        out[a] = jnp.transpose(src[:, :, off:off + (b - a)], (2, 0, 1))
        pos += (b - a) + z
    return jnp.concatenate([out[a] for a in sorted(out)], axis=0)


def _out_to_padded(w):
    z = jnp.zeros((64, w.shape[1]), w.dtype)
    return jnp.concatenate([w[0:384], w[384:704], z, w[704:1024], z], axis=0)


def _out_from_padded(gp):
    return jnp.concatenate([gp[0:384], gp[384:704], gp[768:1088]], axis=0)


def _uq_to_padded(w):
    parts = []
    for p in range(3):
        h0, h1 = 2 * p, 2 * p + 1
        parts += [w[:, 96 * h0:96 * h0 + 64], w[:, 96 * h1:96 * h1 + 64], w[:, 96 * h0 + 64:96 * h0 + 96],
                  w[:, 96 * h1 + 64:96 * h1 + 96], jnp.zeros((w.shape[0], 64), w.dtype)]
    return jnp.concatenate(parts, axis=1)


def _uq_from_padded(gp):
    parts = []
    for h in range(6):
        p, s = h // 2, h % 2
        parts += [gp[:, 256 * p + 64 * s:256 * p + 64 * s + 64], gp[:, 256 * p + 128 + 32 * s:256 * p + 160 + 32 * s]]
    return jnp.concatenate(parts, axis=1)


def _ukv_to_padded(w):
    return jnp.concatenate([w[:, 128 * h:128 * h + 64] for h in range(6)]
                           + [w[:, 128 * h + 64:128 * h + 128] for h in range(6)], axis=1)


def _ukv_from_padded(gp):
    parts = []
    for h in range(6):
        parts += [gp[:, 64 * h:64 * h + 64], gp[:, 384 + 64 * h:384 + 64 * h + 64]]
    return jnp.concatenate(parts, axis=1)


LR_ROWS = 224
SHARD_ROWS = (256, 256, 256, LR_ROWS)


def _pack_lowrank(w_uq, w_ukv):
    flat = jnp.concatenate([w_uq.reshape(-1), w_ukv.reshape(-1)])
    return jnp.pad(flat, (0, LR_ROWS * PACK_COLS - flat.shape[0])).reshape(1, LR_ROWS, PACK_COLS)


def _unpack_lowrank(packed):
    flat = packed.reshape(-1)
    n_uq = DEPTH * A_Q_RANK * 144
    n_ukv = DEPTH * A_KV_RANK * 192
    return flat[0:n_uq].reshape(DEPTH, A_Q_RANK, 144), flat[n_uq:n_uq + n_ukv].reshape(DEPTH, A_KV_RANK, 192)


def _rope_tables(positions):
    t = positions.size
    inv = ROPE_THETA ** (-jnp.arange(0, A_ROPE, 2, dtype=F32) / A_ROPE)
    inv_row = jnp.pad(jnp.tile(inv, 4), (0, 64)).reshape(1, LANE)

    def body(p_ref, i_ref, c_ref, sa_ref, sb_ref):
        ang = p_ref[...].astype(F32) * i_ref[...]
        lane = _lane_iota((TM, LANE))
        live = lane < 64
        second = (lane & 31) >= 16
        s = jnp.sin(ang)
        c_ref[...] = jnp.where(live, jnp.cos(ang), 0.0)
        sa_ref[...] = jnp.where(live & jnp.logical_not(second), -s, 0.0)
        sb_ref[...] = jnp.where(live & second, s, 0.0)

    return pl.pallas_call(
        body, name="rope_tables", grid=(t // TM,), in_specs=[_row_spec(1), _full_spec((1, LANE))],
        out_specs=[_row_spec(LANE)] * 3, out_shape=[jax.ShapeDtypeStruct((t, LANE), F32)] * 3,
        compiler_params=_params(("parallel",)),
    )(positions.reshape(t, 1), inv_row)


def _rows(a, n):
    flat = a.reshape(-1)
    return jnp.pad(flat, (0, n * LANE - flat.shape[0])).reshape(n, LANE)


def _forward_backward(x, mod, tables, target, weights, small, nb, seq):
    cos, sina, sinb = tables
    saved = []
    for l in range(DEPTH):
        w, s = weights[l], small[l]
        shift = mod[l][:, None, 0:D_MODEL]
        scale = mod[l][:, None, D_MODEL:2 * D_MODEL]
        gate = mod[l][:, None, 2 * D_MODEL:]
        h, za, zg, qb, kb, vb, qc, kc, vc, zf = _ln_inproj(x, shift, scale, s["norm_g"], w["in"], seq)
        qa, ka, va = _a_up(za, s["gq"], s["gkv"], w["uq"], w["ukv"], cos, sina, sinb)
        bias = _bias_expand(s["g8"])[0:6]
        f = _forget_fwd(zf, s["fb"], nb, seq)
        frow = jnp.pad(f[:, 0:5].reshape(nb, seq, 5).transpose(0, 2, 1), ((0, 0), (0, 1), (0, 0)))
        frow = frow.reshape(nb, 6, 1, seq)
        oa, lse_a = _attn_fwd("A", qa, ka, va, None, nb, seq)
        ob, lse_b = _attn_fwd("B", qb, kb, vb, bias, nb, seq)
        oc, lse_c = _attn_fwd("C", qc, kc, vc, frow, nb, seq)
        if l + 1 < DEPTH:
            y, x_next = _gate_outproj(x, gate, oa, ob, oc, zg, w["out"], seq)
        else:
            y, dx, loss8, gfinal8 = _gate_outproj_loss(x, gate, oa, ob, oc, zg, w["out"], target, s["final_g"], seq)
            x_next = None
        saved.append(dict(x=x, h=h, za=za, zg=zg, zf=zf, y=y, shift=shift, scale=scale, gate=gate, bias=bias, frow=frow,
                          a=(qa, ka, va, oa, lse_a), b=(qb, kb, vb, ob, lse_b), c=(qc, kc, vc, oc, lse_c)))
        x = x_next
    grads = []
    gw_in = (None, None)
    for l in reversed(range(DEPTH)):
        w, s, sv = weights[l], small[l], saved[l]
        qa, ka, va, oa, lse_a = sv["a"]
        qb, kb, vb, ob, lse_b = sv["b"]
        qc, kc, vc, oc, lse_c = sv["c"]
        doa, dob, doc, dzg, gw_out, dgate = _outproj_bwd(dx, sv["y"], sv["gate"], oa, ob, oc, sv["zg"], w["out"],
                                                          w["out_t"], nb, seq)
        dqa, dka, dva = _attn_bwd("A", qa, ka, va, oa, doa, lse_a, None, nb, seq)
        dqb, dkb, dvb, dbt = _attn_bwd("B", qb, kb, vb, ob, dob, lse_b, sv["bias"], nb, seq)
        dqc, dkc, dvc, dfr, dfq = _attn_bwd("C", qc, kc, vc, oc, doc, lse_c, sv["frow"], nb, seq)
        dg = _bias_reduce(jnp.pad(dbt, ((0, 2), (0, 0), (0, 0))))
        grb = jnp.pad(_bias_unline(dg), ((0, 0), (0, 384 - N_REL)))
        dfk = dfr.reshape(nb, 6, seq).transpose(0, 2, 1).reshape(nb * seq, 6)
        dzf, gfb = _forget_bwd(dfq, jnp.pad(dfk, ((0, 0), (0, LANE - 6))), sv["zf"], s["fb"], nb, seq)
        dza, gw_uq, gw_ukv, ggq, ggkv = _a_up_bwd(dqa, dka, dva, sv["za"], s["gq"], s["gkv"], w["uq_t"], w["ukv_t"],
                                                  cos, sina, sinb)
        dz = (dza, dzg, dqb, dkb, dvb, dqc, dkc, dvc, dzf)
        dx, dshift, dscale, gnorm = _inproj_bwd_dx(dz, dx, sv["x"], sv["shift"], sv["scale"], s["norm_g"], w["in"],
                                                   nb, seq)
        gw_in = (_inproj_bwd_dw(sv["h"], dz[0:2], "inproj_bwd_dw0", l, gw_in[0]),
                 _inproj_bwd_dw(sv["h"], dz[2:], "inproj_bwd_dw1", l, gw_in[1]))
        dmod = jnp.concatenate([dshift[:, 0], dscale[:, 0], dgate[:, 0]], axis=1)
        grads.append(dict(w_out=gw_out, w_uq=gw_uq, w_ukv=gw_ukv, dmod=dmod, norm_g=gnorm[0], gq=ggq[0],
                          gkv=ggkv[0], rb8=grb, fb=gfb[0]))
    grads.reverse()
    return loss8[0, 0], dx, grads, gfinal8[0], gw_in


def _layer_weights(w_in, w_out, w_uq, w_ukv):
    wi, wo, wq, wkv = _in_to_padded(w_in), _out_to_padded(w_out), _uq_to_padded(w_uq), _ukv_to_padded(w_ukv)
    return {"in": wi, "out": wo, "out_t": wo.T, "uq": wq, "uq_t": wq.T, "ukv": wkv, "ukv_t": wkv.T}


def _layer_small(norm_g, gq, gkv, rel_bias, forget_b, final_g):
    fb = jnp.pad(forget_b, (0, LANE - 5)).reshape(1, LANE)
    return dict(norm_g=norm_g.reshape(1, -1), gq=gq.reshape(1, -1), gkv=gkv.reshape(1, -1), g8=_bias_line(rel_bias), fb=fb,
                final_g=final_g.reshape(1, -1))


def _small_payload(per_layer, final_g, loss):
    def stack(key):
        return jnp.stack([p[key] for p in per_layer])

    def rows(a, rng):
        return _rows(a, rng[1] - rng[0])

    parts = [rows(stack("dmod"), PAY_DMOD), rows(stack("norm_g"), PAY_NORM), rows(stack("gq"), PAY_GQ),
             rows(stack("gkv"), PAY_GKV), rows(stack("rb8"), PAY_RB), rows(stack("fb"), PAY_FB),
             rows(final_g, PAY_FINAL), rows(loss, PAY_LOSS)]
    return jnp.concatenate(parts, axis=0)


def _payload_split(pay):
    def take(rng, shape):
        n = 1
        for d in shape:
            n *= d
        return pay[rng[0]:rng[1]].reshape(-1)[0:n].reshape(shape)

    norm_g = take(PAY_NORM, (DEPTH, D_MODEL))
    gq = take(PAY_GQ, (DEPTH, A_Q_RANK))
    gkv = take(PAY_GKV, (DEPTH, A_KV_RANK))
    rb = take(PAY_RB, (DEPTH, 8, 384))[:, 0:5, 0:N_REL]
    fb = take(PAY_FB, (DEPTH, LANE))[:, 0:5]
    final_g = take(PAY_FINAL, (D_MODEL,))
    return norm_g, gq, gkv, rb, fb, final_g


def kernel(x, c, positions, w_ada, b_ada, norm_g, w_in, a_q_norm_g, a_w_uq, a_kv_norm_g, a_w_ukv, b_rel_bias, c_forget_b, w_out, final_g, loss_target, m_w_ada, m_b_ada, m_norm_g, m_w_in, m_a_q_norm_g, m_a_w_uq, m_a_kv_norm_g, m_a_w_ukv, m_b_rel_bias, m_c_forget_b, m_w_out, m_final_g, v_w_ada, v_b_ada, v_norm_g, v_w_in, v_a_q_norm_g, v_a_w_uq, v_a_kv_norm_g, v_a_w_ukv, v_b_rel_bias, v_c_forget_b, v_w_out, v_final_g):
    nb, seq, _ = x.shape
    ix, iy, ic = lax.axis_index("x"), lax.axis_index("y"), lax.axis_index("c")
    chip = 2 * ix + iy

    weight_plan = _gather_plan((256, 256, LR_ROWS))
    cols = w_ada.shape[2]

    def gather_plan(x, y, c_):
        me_ = 4 * x + 2 * y + c_
        first = [(3, False, 0, 8, 8 * me_, f, ()) for f in range(N_DEV)]
        rest = weight_plan(x, y, c_)
        order = [i for i, s in enumerate(rest) if not s[6]] + [i for i, s in enumerate(rest) if s[6]]
        place = {old: len(first) + new for new, old in enumerate(order)}
        return first + [rest[i][:6] + (tuple(place[u] for u in rest[i][6]),) for i in order]

    probe = gather_plan(0, 0, 0)

    def project(x, y, c_, dst_refs, in_refs, out_refs, scratch):
        w_ref, b_ref = in_refs
        (mod_ref,) = out_refs
        c_vm, mod_vm, send_m, recv_m, local_m = scratch
        me_ = 4 * x + 2 * y + c_
        load = pltpu.make_async_copy(dst_refs[3].at[0], c_vm, local_m.at[0])
        load.start()
        load.wait()
        act = _silu(c_vm[...]).astype(BF16)
        for l in range(DEPTH):
            mod_vm[l] = _dot(act, w_ref[l].astype(BF16)) + b_ref[l]

        def rows_of(j):
            return mod_vm.at[:, pl.ds(pl.multiple_of(8 * j, 8), 8), :]

        def send(i, f, slot, j):
            to = (1 - x if f & 4 else x, 1 - y if f & 2 else y, 1 - c_ if f & 1 else c_)
            return pltpu.make_async_remote_copy(src_ref=rows_of(j), dst_ref=mod_ref.at[slot], send_sem=send_m.at[i],
                                                recv_sem=recv_m.at[i], device_id=to, device_id_type=pl.DeviceIdType.MESH)

        own = pltpu.make_async_copy(rows_of(me_), mod_ref.at[me_], local_m.at[1])
        own.start()
        sends = [send(i, f, me_, me_ ^ f) for i, f in enumerate(ALL_FLIPS)]
        for s in sends:
            s.start()

        def finish():
            for i, f in enumerate(ALL_FLIPS):
                send(i, f, me_ ^ f, 0).wait_recv()
            for s in sends:
                s.wait_send()
            own.wait()
        return finish

    b_cols = lax.dynamic_slice_in_dim(b_ada, chip * cols, cols, axis=1)[:, None, :]
    vmem = pl.BlockSpec(memory_space=pltpu.VMEM)
    side = dict(at=next(t for t, s in enumerate(probe) if s[6]), needs=tuple(range(N_DEV)),
                inputs=[(w_ada, vmem), (b_cols, vmem)], outputs=[((N_DEV, DEPTH, 8, cols), F32)],
                scratch=[pltpu.VMEM((8 * N_DEV, D_MODEL), F32), pltpu.VMEM((DEPTH, 8 * N_DEV, cols), F32),
                         pltpu.SemaphoreType.DMA((len(ALL_FLIPS),)), pltpu.SemaphoreType.DMA((len(ALL_FLIPS),)),
                         pltpu.SemaphoreType.DMA((2,))],
                run=project)
    full_in, full_out, full_lr, c_rows, mod_rows = _transfer(
        "gather_weights", [w_in.astype(BF16), w_out.astype(BF16), _pack_lowrank(a_w_uq, a_w_ukv).astype(BF16),
                           jnp.pad(c, ((0, 8 - nb), (0, 0)))[None]],
        [((DEPTH, D_MODEL, N_IN), BF16), ((DEPTH, D_MODEL, D_MODEL), BF16), ((1, 4 * LR_ROWS, PACK_COLS), BF16),
         ((1, 8 * N_DEV, D_MODEL), F32)], gather_plan, side)
    lowrank = [_unpack_lowrank(full_lr[0, LR_ROWS * j:LR_ROWS * (j + 1)]) for j in range(4)]
    full_uq = jnp.concatenate([s[0] for s in lowrank], axis=2)
    full_ukv = jnp.concatenate([s[1] for s in lowrank], axis=2)
    weights = [_layer_weights(full_in[l], full_out[l], full_uq[l], full_ukv[l]) for l in range(DEPTH)]
    small = [_layer_small(norm_g[l], a_q_norm_g[l], a_kv_norm_g[l], b_rel_bias[l], c_forget_b[l], final_g)
             for l in range(DEPTH)]

    c_all = c_rows.reshape(N_DEV, 8, D_MODEL)[:, 0:nb].reshape(N_DEV * nb, D_MODEL)
    mod = jnp.concatenate([mod_rows[2 * j][:, 0:nb] for j in range(4)], axis=2)

    tables = _rope_tables(positions)
    loss_part, dx, grads, gfinal, gw_in = _forward_backward(
        x.reshape(nb * seq, D_MODEL), mod, tables, loss_target.reshape(nb * seq, D_MODEL), weights, small, nb, seq)

    g_uq = jnp.stack([_uq_from_padded(g["w_uq"]) for g in grads])
    g_ukv = jnp.stack([_ukv_from_padded(g["w_ukv"]) for g in grads])
    g_lr = jnp.concatenate([_pack_lowrank(g_uq[:, :, 144 * j:144 * (j + 1)], g_ukv[:, :, 192 * j:192 * (j + 1)])
                            for j in range(4)], axis=1)
    partials = [gw_in[0], gw_in[1], jnp.stack([_out_from_padded(g["w_out"]) for g in grads]), g_lr]
    shapes = [(p.shape[0], p.shape[2]) for p in partials]
    halves = [r // 2 for r in SHARD_ROWS]
    core_s = jnp.reshape(ic, (1,)).astype(jnp.int32)
    pay = _small_payload(grads, gfinal, loss_part)
    pay_rows = pay.shape[0]
    reduce_plan = _pair_reduce_plan(SHARD_ROWS)

    def reduce_plan_with_payload(x, y, c_):
        me_ = 4 * x + 2 * y + c_
        return reduce_plan(x, y, c_) + [(len(partials), False, 0, pay_rows, pay_rows * me_, f, ()) for f in range(N_DEV)]

    *from_pair, pay_all = _transfer(
        "pair_reduce", partials + [pay[None]],
        [((nl, 4 * h, nc), F32) for (nl, nc), h in zip(shapes, halves)] + [((1, N_DEV * pay_rows, LANE), F32)],
        reduce_plan_with_payload)
    pay_all = pay_all.reshape(N_DEV, pay_rows, LANE)

    tot = _sum_blocks(pay_all, "sum_small")
    loss = tot[PAY_LOSS[0], 0]
    dmod_all = pay_all[:, PAY_DMOD[0]:PAY_DMOD[1]].reshape(N_DEV, -1)[:, 0:DEPTH * nb * 3 * D_MODEL]
    dmod_all = dmod_all.reshape(N_DEV, DEPTH, nb, 3 * D_MODEL).transpose(1, 0, 2, 3)
    dmod_all = dmod_all.reshape(DEPTH, N_DEV * nb, 3 * D_MODEL)
    my_cols = lax.dynamic_slice_in_dim(dmod_all, chip * cols, cols, axis=2)
    g_w_ada, g_b_ada = _ada_bwd(c_all, my_cols, dmod_all)
    g_b_ada = g_b_ada[:, 0]
    chip_sums = [_sum_pair(p, r, core_s, rc, "sum_pair%d" % i)
                 for i, (p, r, rc) in enumerate(zip(partials, from_pair, SHARD_ROWS))]
    slots = _transfer("chip_scatter", chip_sums, [((nl, 8 * h, nc), BF16) for (nl, nc), h in zip(shapes, halves)],
                      _chip_scatter_plan(SHARD_ROWS))
    g_in_a, g_in_b, g_out_sh, g_lr_sh = [_sum_chips(s, core_s, rc, "sum_chips%d" % i)
                                         for i, (s, rc) in enumerate(zip(slots, SHARD_ROWS))]
    g_uq_sh, g_ukv_sh = _unpack_lowrank(g_lr_sh[0])

    def cols_first(a):
        return jnp.transpose(a, (2, 0, 1))

    g_in_t = _in_cols_first(g_in_a, g_in_b)
    upd_in = tuple(jnp.transpose(a, (1, 2, 0)) for a in _adamw(cols_first(w_in), g_in_t, cols_first(m_w_in),
                                                               cols_first(v_w_in), "adamw_in"))
    gw = (jnp.transpose(g_in_t, (1, 2, 0)), g_out_sh, g_uq_sh, g_ukv_sh)
    upd = [upd_in, _adamw(w_out, gw[1], m_w_out, v_w_out, "adamw_out"),
           _adamw(a_w_uq, gw[2], m_a_w_uq, v_a_w_uq, "adamw_uq"), _adamw(a_w_ukv, gw[3], m_a_w_ukv, v_a_w_ukv, "adamw_ukv")]
    dw, mw, vw = (tuple(u[i] for u in upd) for i in range(3))
    d_ada, m_ada, v_ada = _adamw(w_ada, g_w_ada, m_w_ada, v_w_ada, "adamw_ada")

    gs = _payload_split(tot)
    small_w = (norm_g, a_q_norm_g, a_kv_norm_g, b_rel_bias, c_forget_b, final_g, b_ada)

    def view3(arrays):
        return [a.reshape((1,) * (3 - w.ndim) + w.shape) for a, w in zip(arrays, small_w)]

    small_upd = _adamw_small(
        view3(small_w), view3(gs + (g_b_ada,)),
        view3((m_norm_g, m_a_q_norm_g, m_a_kv_norm_g, m_b_rel_bias, m_c_forget_b, m_final_g, m_b_ada)),
        view3((v_norm_g, v_a_q_norm_g, v_a_kv_norm_g, v_b_rel_bias, v_c_forget_b, v_final_g, v_b_ada)))
    (ds, d_b), (ms, m_b), (vs, v_b) = ((tuple(a.reshape(w.shape) for a, w in zip(u[:-1], small_w)),
                                         u[-1].reshape(b_ada.shape)) for u in small_upd)

    def ordered(ada, b, sm, big):
        ng, gq, gkv, rb, fb, fg = sm
        b_in, b_out, b_uq, b_ukv = big
        return (ada, b, ng, b_in, gq, b_uq, gkv, b_ukv, rb, fb, b_out, fg)

    return (loss, dx.reshape(nb, seq, D_MODEL), *ordered(g_w_ada, g_b_ada, gs, gw), *ordered(d_ada, d_b, ds, dw),
            *ordered(m_ada, m_b, ms, mw), *ordered(v_ada, v_b, vs, vw))
```
